```python
import math
import jax, jax.numpy as jnp
from jax import lax
import numpy as np

D_MODEL = 1024
BATCH = 8
SEQ = 4096
DEPTH = 2

A_HEADS = 8
A_HEAD_DIM = 64
A_WIDTH = A_HEADS * A_HEAD_DIM
DILATED_CONFIGS = ((128, 1), (512, 4), (2048, 16))
A_BLOCK = 128
A_SPAN = A_BLOCK * max(d for _, d in DILATED_CONFIGS)
B_WIDTH = 512
B_GROUP = 16
B_GROUPS = B_WIDTH // B_GROUP
B_STATE = 64
AB_WIDTH = A_WIDTH + B_WIDTH
AB_IN = 3 * A_WIDTH + B_WIDTH + AB_WIDTH
C_HEADS = 8
C_DK = 128
C_DV = 128
C_CONV = 4
C_CHUNK = 64
C_WIDTH = C_HEADS * C_DV
QKV_WIDTH = 2 * C_HEADS * C_DK + C_WIDTH
C_IN = QKV_WIDTH + C_WIDTH + 2 * C_HEADS
REL_BUCKETS = 32
REL_MAX_DIST = 2048
N_EVEN = (DEPTH + 1) // 2
N_ODD = DEPTH // 2
EPS = 1e-6

kernel_name = "hybrid_dilated_s5_gdn_block"


def rms_norm(x, g):
    xf = x.astype(jnp.float32)
    y = xf * lax.rsqrt(jnp.mean(xf * xf, axis=-1, keepdims=True) + EPS)
    return (y * g.astype(jnp.float32)).astype(x.dtype)


def t5_bucket_np(dist):
    dist = np.maximum(dist, 0)
    max_exact = REL_BUCKETS // 2
    large = max_exact + (np.log(np.maximum(dist, 1) / max_exact)
                         / math.log(REL_MAX_DIST / max_exact) * (REL_BUCKETS - max_exact)).astype(np.int32)
    large = np.minimum(large, REL_BUCKETS - 1)
    return np.where(dist < max_exact, dist, large).astype(np.int32)


def dilated_attention(q, k, v, rel_bias):
    bsz, s_len = q.shape[:2]
    sp = -(-s_len // A_SPAN) * A_SPAN
    pad = ((0, 0), (0, sp - s_len), (0, 0), (0, 0))
    q, k, v = jnp.pad(q, pad), jnp.pad(k, pad), jnp.pad(v, pad)
    scale = A_HEAD_DIM ** -0.5
    qi = np.arange(A_BLOCK)[:, None]
    kj = np.arange(2 * A_BLOCK)[None, :]
    rel = qi + A_BLOCK - kj
    neg = jnp.finfo(jnp.float32).min
    outs, lses = [], []
    for window, dil in DILATED_CONFIGS:
        n_keys = window // dil
        nb = sp // dil // A_BLOCK
        qb, kb, vb = (t.reshape(bsz, nb, A_BLOCK, dil, A_HEADS, A_HEAD_DIM) for t in (q, k, v))
        prev = lambda t: jnp.pad(t[:, :-1], ((0, 0), (1, 0), (0, 0), (0, 0), (0, 0), (0, 0)))
        kw = jnp.concatenate([prev(kb), kb], axis=2)
        vw = jnp.concatenate([prev(vb), vb], axis=2)
        bias = jnp.transpose(rel_bias[t5_bucket_np(rel * dil)], (2, 0, 1)).astype(jnp.float32)
        band = (rel >= 0) & (rel <= n_keys)
        first = band & (kj >= A_BLOCK)
        blk_mask = np.concatenate([first[None], np.broadcast_to(band, (nb - 1,) + band.shape)], 0)
        s = jnp.einsum('bnqrhd,bnkrhd->bnrhqk', qb, kw, preferred_element_type=jnp.float32) * scale
        s = jnp.where(blk_mask[None, :, None, None], s + bias[None, None, None], neg)
        m = jnp.max(s, axis=-1, keepdims=True)
        p = jnp.exp(s - m)
        den = jnp.sum(p, axis=-1)
        o = jnp.einsum('bnrhqk,bnkrhd->bnqrhd', p, vw.astype(jnp.float32))
        den_t = jnp.transpose(den, (0, 1, 4, 2, 3))
        lse = jnp.transpose(m[..., 0], (0, 1, 4, 2, 3)) + jnp.log(den_t)
        outs.append((o / den_t[..., None]).reshape(bsz, sp, A_HEADS, A_HEAD_DIM))
        lses.append(lse.reshape(bsz, sp, A_HEADS))
    w = jax.nn.softmax(jnp.stack(lses, 0), axis=0)
    out = jnp.einsum('cbsh,cbshd->bshd', w, jnp.stack(outs, 0))
    return out[:, :s_len].reshape(bsz, s_len, A_WIDTH)


def _complex_affine_combine(e1, e2):
    a1r, a1i, b1r, b1i = e1
    a2r, a2i, b2r, b2i = e2
    return (a2r * a1r - a2i * a1i, a2r * a1i + a2i * a1r,
            a2r * b1r - a2i * b1i + b2r, a2r * b1i + a2i * b1r + b2i)


def s5_layer(u, a_re, a_im, log_dt, b_re, b_im, c_re, c_im, d_skip, glu_w, glu_b):
    bsz, s_len = u.shape[:2]
    uf = u.astype(jnp.float32)
    ug = uf.reshape(bsz, s_len, B_GROUPS, B_GROUP)
    dt = jnp.exp(log_dt.astype(jnp.float32))[:, None]
    ar, ai = a_re.astype(jnp.float32), a_im.astype(jnp.float32)
    mag = jnp.exp(dt * ar)
    abar_r, abar_i = mag * jnp.cos(dt * ai), mag * jnp.sin(dt * ai)
    den = ar * ar + ai * ai
    fr = ((abar_r - 1.0) * ar + abar_i * ai) / den
    fi = (abar_i * ar - (abar_r - 1.0) * ai) / den
    br, bi = b_re.astype(jnp.float32), b_im.astype(jnp.float32)
    bbar_r = fr[..., None] * br - fi[..., None] * bi
    bbar_i = fr[..., None] * bi + fi[..., None] * br
    bu_r = jnp.einsum('bsgm,gpm->bsgp', ug, bbar_r)
    bu_i = jnp.einsum('bsgm,gpm->bsgp', ug, bbar_i)
    a_r = jnp.broadcast_to(abar_r, bu_r.shape)
    a_i = jnp.broadcast_to(abar_i, bu_i.shape)
    _, _, x_r, x_i = lax.associative_scan(_complex_affine_combine, (a_r, a_i, bu_r, bu_i), axis=1)
    y = (jnp.einsum('gmp,bsgp->bsgm', c_re.astype(jnp.float32), x_r)
         - jnp.einsum('gmp,bsgp->bsgm', c_im.astype(jnp.float32), x_i))
    y = jax.nn.gelu(y.reshape(bsz, s_len, B_WIDTH) + d_skip.astype(jnp.float32) * uf)
    return (y * jax.nn.sigmoid(y @ glu_w.astype(jnp.float32) + glu_b.astype(jnp.float32))).astype(u.dtype)


def ab_mixer(h, w_in, w_out, rel_bias, a_re, a_im, log_dt, b_re, b_im, c_re, c_im, d_skip, glu_w, glu_b):
    bsz, s_len, _ = h.shape
    z = h @ w_in
    q, k, v, u, gate = jnp.split(z, [A_WIDTH, 2 * A_WIDTH, 3 * A_WIDTH, 3 * A_WIDTH + B_WIDTH], axis=-1)
    hs = (bsz, s_len, A_HEADS, A_HEAD_DIM)
    o_a = dilated_attention(q.reshape(hs), k.reshape(hs), v.reshape(hs), rel_bias).astype(h.dtype)
    o_b = s5_layer(u, a_re, a_im, log_dt, b_re, b_im, c_re, c_im, d_skip, glu_w, glu_b)
    o = jnp.concatenate([o_a, o_b], axis=-1) * jax.nn.silu(gate)
    return o @ w_out


def l2_normalize(t):
    return t * lax.rsqrt(jnp.sum(t * t, axis=-1, keepdims=True) + EPS)


def gdn_mixer(h, w_in, conv_w, a_log, dt_bias, norm_g, w_out):
    bsz, s_len, _ = h.shape
    z = h @ w_in
    qkv, gate, beta_raw, a_raw = jnp.split(z, [QKV_WIDTH, QKV_WIDTH + C_WIDTH, QKV_WIDTH + C_WIDTH + C_HEADS], axis=-1)
    qkv = lax.conv_general_dilated(qkv, conv_w[:, None, :].astype(qkv.dtype), (1,), [(C_CONV - 1, 0)],
                                   dimension_numbers=('NWC', 'WIO', 'NWC'), feature_group_count=QKV_WIDTH)
    qkv = jax.nn.silu(qkv).astype(jnp.float32)
    q, k, v = jnp.split(qkv, [C_HEADS * C_DK, 2 * C_HEADS * C_DK], axis=-1)
    q = l2_normalize(q.reshape(bsz, s_len, C_HEADS, C_DK)) * (C_DK ** -0.5)
    k = l2_normalize(k.reshape(bsz, s_len, C_HEADS, C_DK))
    v = v.reshape(bsz, s_len, C_HEADS, C_DV)
    beta = jax.nn.sigmoid(beta_raw.astype(jnp.float32))
    g = -jnp.exp(a_log.astype(jnp.float32)) * jax.nn.softplus(a_raw.astype(jnp.float32) + dt_bias.astype(jnp.float32))
    n_chunks = s_len // C_CHUNK

    def chunks(t):
        return jnp.moveaxis(t.reshape(bsz, n_chunks, C_CHUNK, C_HEADS, *t.shape[3:]), 3, 1)

    qc, kc, vc, bc = chunks(q), chunks(k), chunks(v), chunks(beta)
    gc = jnp.cumsum(chunks(g), axis=-1)
    idx = np.arange(C_CHUNK)
    tril = idx[:, None] >= idx[None, :]
    strict = idx[:, None] > idx[None, :]
    decay = jnp.exp(jnp.where(tril, gc[..., :, None] - gc[..., None, :], -jnp.inf))
    kb = kc * bc[..., None]
    lower = jnp.where(strict, jnp.einsum('bhnid,bhnjd->bhnij', kb, kc) * decay, 0.0)
    tri = lower + jnp.eye(C_CHUNK, dtype=lower.dtype)
    u_c = lax.linalg.triangular_solve(tri, vc * bc[..., None], left_side=True, lower=True, unit_diagonal=True)
    w_c = lax.linalg.triangular_solve(tri, kb * jnp.exp(gc)[..., None], left_side=True, lower=True, unit_diagonal=True)
    aqk = jnp.einsum('bhnid,bhnjd->bhnij', qc, kc) * decay
    qg = qc * jnp.exp(gc)[..., None]
    g_last = gc[..., -1]
    kd = kc * jnp.exp(g_last[..., None] - gc)[..., None]
    xs = tuple(jnp.moveaxis(t, 2, 0) for t in (w_c, u_c, qg, kd, aqk, jnp.exp(g_last)))

    def step(state, xs_c):
        w_i, u_i, qg_i, kd_i, aqk_i, dec_i = xs_c
        v_new = u_i - jnp.einsum('bhik,bhkv->bhiv', w_i, state)
        o_i = jnp.einsum('bhik,bhkv->bhiv', qg_i, state) + jnp.einsum('bhij,bhjv->bhiv', aqk_i, v_new)
        state = state * dec_i[..., None, None] + jnp.einsum('bhik,bhiv->bhkv', kd_i, v_new)
        return state, o_i

    s0 = jnp.zeros((bsz, C_HEADS, C_DK, C_DV), jnp.float32)
    _, o = lax.scan(step, s0, xs)
    o = jnp.transpose(o, (1, 0, 3, 2, 4)).reshape(bsz, s_len, C_HEADS, C_DV)
    o = rms_norm(o, norm_g).reshape(bsz, s_len, C_WIDTH).astype(h.dtype) * jax.nn.silu(gate)
    return o @ w_out


def _fwd_setup_inputs(seed: int = 0) -> dict:
    key = jax.random.key(seed)
    ks = iter(jax.random.split(key, 32))
    nrm = lambda shape, s: jax.random.normal(next(ks), shape, jnp.float32) * s
    uni = lambda shape, lo, hi: jax.random.uniform(next(ks), shape, jnp.float32, lo, hi)
    n_idx = jnp.arange(B_STATE, dtype=jnp.float32)
    dt_c = uni((N_ODD, C_HEADS), 0.001, 0.1)
    return {
        "x": nrm((BATCH, SEQ, D_MODEL), 1.0),
        "c": nrm((BATCH, D_MODEL), 1.0),
        "ada_w": nrm((DEPTH, D_MODEL, 3 * D_MODEL), 0.3 * D_MODEL ** -0.5),
        "ada_b": nrm((DEPTH, 3 * D_MODEL), 0.02),
        "pre_g": 1.0 + nrm((DEPTH, D_MODEL), 0.02),
        "post_g": 1.0 + nrm((DEPTH, D_MODEL), 0.02),
        "rel_bias": nrm((REL_BUCKETS, A_HEADS), 0.5),
        "ab_w_in": nrm((N_EVEN, D_MODEL, AB_IN), D_MODEL ** -0.5),
        "ab_w_out": nrm((N_EVEN, AB_WIDTH, D_MODEL), AB_WIDTH ** -0.5),
        "s5_a_re": -0.5 + nrm((N_EVEN, B_GROUPS, B_STATE), 0.01),
        "s5_a_im": math.pi * n_idx + nrm((N_EVEN, B_GROUPS, B_STATE), 0.01),
        "s5_log_dt": uni((N_EVEN, B_GROUPS), math.log(0.001), math.log(0.1)),
        "s5_b_re": nrm((N_EVEN, B_GROUPS, B_STATE, B_GROUP), (2 * B_GROUP) ** -0.5),
        "s5_b_im": nrm((N_EVEN, B_GROUPS, B_STATE, B_GROUP), (2 * B_GROUP) ** -0.5),
        "s5_c_re": nrm((N_EVEN, B_GROUPS, B_GROUP, B_STATE), (2 * B_STATE) ** -0.5),
        "s5_c_im": nrm((N_EVEN, B_GROUPS, B_GROUP, B_STATE), (2 * B_STATE) ** -0.5),
        "s5_d": nrm((N_EVEN, B_WIDTH), 1.0),
        "s5_glu_w": nrm((N_EVEN, B_WIDTH, B_WIDTH), B_WIDTH ** -0.5),
        "s5_glu_b": nrm((N_EVEN, B_WIDTH), 0.02),
        "gdn_w_in": nrm((N_ODD, D_MODEL, C_IN), D_MODEL ** -0.5),
        "gdn_conv": nrm((N_ODD, C_CONV, QKV_WIDTH), C_CONV ** -0.5),
        "gdn_a_log": jnp.log(uni((N_ODD, C_HEADS), 1.0, 16.0)),
        "gdn_dt_bias": dt_c + jnp.log(-jnp.expm1(-dt_c)),
        "gdn_norm_g": 1.0 + nrm((N_ODD, C_DV), 0.02),
        "gdn_w_out": nrm((N_ODD, C_WIDTH, D_MODEL), C_WIDTH ** -0.5),
    }


def _fwd_reference(x, c, ada_w, ada_b, pre_g, post_g, rel_bias, ab_w_in, ab_w_out,
              s5_a_re, s5_a_im, s5_log_dt, s5_b_re, s5_b_im, s5_c_re, s5_c_im, s5_d, s5_glu_w, s5_glu_b,
              gdn_w_in, gdn_conv, gdn_a_log, gdn_dt_bias, gdn_norm_g, gdn_w_out):
    c_act = jax.nn.silu(c)
    for layer in range(DEPTH):
        mod = c_act @ ada_w[layer] + ada_b[layer]
        shift, scale, gate = jnp.split(mod, 3, axis=-1)
        h = rms_norm(x, pre_g[layer]) * (1.0 + scale[:, None]) + shift[:, None]
        j = layer // 2
        if layer % 2 == 0:
            y = ab_mixer(h, ab_w_in[j], ab_w_out[j], rel_bias, s5_a_re[j], s5_a_im[j], s5_log_dt[j],
                         s5_b_re[j], s5_b_im[j], s5_c_re[j], s5_c_im[j], s5_d[j], s5_glu_w[j], s5_glu_b[j])
        else:
            y = gdn_mixer(h, gdn_w_in[j], gdn_conv[j], gdn_a_log[j], gdn_dt_bias[j], gdn_norm_g[j], gdn_w_out[j])
        x = x + gate[:, None] * rms_norm(y, post_g[layer])
    return x


import jax as _jax
import jax.numpy as _jnp

TWIN_FORMAT = 'train_step'
FWD_PARAMS = ['x', 'c', 'ada_w', 'ada_b', 'pre_g', 'post_g', 'rel_bias', 'ab_w_in', 'ab_w_out', 's5_a_re', 's5_a_im', 's5_log_dt', 's5_b_re', 's5_b_im', 's5_c_re', 's5_c_im', 's5_d', 's5_glu_w', 's5_glu_b', 'gdn_w_in', 'gdn_conv', 'gdn_a_log', 'gdn_dt_bias', 'gdn_norm_g', 'gdn_w_out']
TWIN_WEIGHTS = ['ada_w', 'ada_b', 'pre_g', 'post_g', 'rel_bias', 'ab_w_in', 'ab_w_out', 's5_a_re', 's5_a_im', 's5_log_dt', 's5_b_re', 's5_b_im', 's5_c_re', 's5_c_im', 's5_d', 's5_glu_w', 's5_glu_b', 'gdn_w_in', 'gdn_conv', 'gdn_a_log', 'gdn_dt_bias', 'gdn_norm_g', 'gdn_w_out']
TWIN_DIFF_INPUT = 'x'
TWIN_INPUTS = ['x', 'c', 'ada_w', 'ada_b', 'pre_g', 'post_g', 'rel_bias', 'ab_w_in', 'ab_w_out', 's5_a_re', 's5_a_im', 's5_log_dt', 's5_b_re', 's5_b_im', 's5_c_re', 's5_c_im', 's5_d', 's5_glu_w', 's5_glu_b', 'gdn_w_in', 'gdn_conv', 'gdn_a_log', 'gdn_dt_bias', 'gdn_norm_g', 'gdn_w_out', 'loss_target', 'm_ada_w', 'm_ada_b', 'm_pre_g', 'm_post_g', 'm_rel_bias', 'm_ab_w_in', 'm_ab_w_out', 'm_s5_a_re', 'm_s5_a_im', 'm_s5_log_dt', 'm_s5_b_re', 'm_s5_b_im', 'm_s5_c_re', 'm_s5_c_im', 'm_s5_d', 'm_s5_glu_w', 'm_s5_glu_b', 'm_gdn_w_in', 'm_gdn_conv', 'm_gdn_a_log', 'm_gdn_dt_bias', 'm_gdn_norm_g', 'm_gdn_w_out', 'v_ada_w', 'v_ada_b', 'v_pre_g', 'v_post_g', 'v_rel_bias', 'v_ab_w_in', 'v_ab_w_out', 'v_s5_a_re', 'v_s5_a_im', 'v_s5_log_dt', 'v_s5_b_re', 'v_s5_b_im', 'v_s5_c_re', 'v_s5_c_im', 'v_s5_d', 'v_s5_glu_w', 'v_s5_glu_b', 'v_gdn_w_in', 'v_gdn_conv', 'v_gdn_a_log', 'v_gdn_dt_bias', 'v_gdn_norm_g', 'v_gdn_w_out']
TWIN_OUTPUTS = ['loss', 'grad_x', 'grad_ada_w', 'grad_ada_b', 'grad_pre_g', 'grad_post_g', 'grad_rel_bias', 'grad_ab_w_in', 'grad_ab_w_out', 'grad_s5_a_re', 'grad_s5_a_im', 'grad_s5_log_dt', 'grad_s5_b_re', 'grad_s5_b_im', 'grad_s5_c_re', 'grad_s5_c_im', 'grad_s5_d', 'grad_s5_glu_w', 'grad_s5_glu_b', 'grad_gdn_w_in', 'grad_gdn_conv', 'grad_gdn_a_log', 'grad_gdn_dt_bias', 'grad_gdn_norm_g', 'grad_gdn_w_out', 'delta_ada_w', 'delta_ada_b', 'delta_pre_g', 'delta_post_g', 'delta_rel_bias', 'delta_ab_w_in', 'delta_ab_w_out', 'delta_s5_a_re', 'delta_s5_a_im', 'delta_s5_log_dt', 'delta_s5_b_re', 'delta_s5_b_im', 'delta_s5_c_re', 'delta_s5_c_im', 'delta_s5_d', 'delta_s5_glu_w', 'delta_s5_glu_b', 'delta_gdn_w_in', 'delta_gdn_conv', 'delta_gdn_a_log', 'delta_gdn_dt_bias', 'delta_gdn_norm_g', 'delta_gdn_w_out', 'new_m_ada_w', 'new_m_ada_b', 'new_m_pre_g', 'new_m_post_g', 'new_m_rel_bias', 'new_m_ab_w_in', 'new_m_ab_w_out', 'new_m_s5_a_re', 'new_m_s5_a_im', 'new_m_s5_log_dt', 'new_m_s5_b_re', 'new_m_s5_b_im', 'new_m_s5_c_re', 'new_m_s5_c_im', 'new_m_s5_d', 'new_m_s5_glu_w', 'new_m_s5_glu_b', 'new_m_gdn_w_in', 'new_m_gdn_conv', 'new_m_gdn_a_log', 'new_m_gdn_dt_bias', 'new_m_gdn_norm_g', 'new_m_gdn_w_out', 'new_v_ada_w', 'new_v_ada_b', 'new_v_pre_g', 'new_v_post_g', 'new_v_rel_bias', 'new_v_ab_w_in', 'new_v_ab_w_out', 'new_v_s5_a_re', 'new_v_s5_a_im', 'new_v_s5_log_dt', 'new_v_s5_b_re', 'new_v_s5_b_im', 'new_v_s5_c_re', 'new_v_s5_c_im', 'new_v_s5_d', 'new_v_s5_glu_w', 'new_v_s5_glu_b', 'new_v_gdn_w_in', 'new_v_gdn_conv', 'new_v_gdn_a_log', 'new_v_gdn_dt_bias', 'new_v_gdn_norm_g', 'new_v_gdn_w_out']
TWIN_LEAF_KINDS = {'loss': 'loss', 'grad_x': 'grad_x', 'grad_ada_w': 'grad_w', 'grad_ada_b': 'grad_w', 'grad_pre_g': 'grad_w', 'grad_post_g': 'grad_w', 'grad_rel_bias': 'grad_w', 'grad_ab_w_in': 'grad_w', 'grad_ab_w_out': 'grad_w', 'grad_s5_a_re': 'grad_w', 'grad_s5_a_im': 'grad_w', 'grad_s5_log_dt': 'grad_w', 'grad_s5_b_re': 'grad_w', 'grad_s5_b_im': 'grad_w', 'grad_s5_c_re': 'grad_w', 'grad_s5_c_im': 'grad_w', 'grad_s5_d': 'grad_w', 'grad_s5_glu_w': 'grad_w', 'grad_s5_glu_b': 'grad_w', 'grad_gdn_w_in': 'grad_w', 'grad_gdn_conv': 'grad_w', 'grad_gdn_a_log': 'grad_w', 'grad_gdn_dt_bias': 'grad_w', 'grad_gdn_norm_g': 'grad_w', 'grad_gdn_w_out': 'grad_w', 'delta_ada_w': 'delta_w', 'delta_ada_b': 'delta_w', 'delta_pre_g': 'delta_w', 'delta_post_g': 'delta_w', 'delta_rel_bias': 'delta_w', 'delta_ab_w_in': 'delta_w', 'delta_ab_w_out': 'delta_w', 'delta_s5_a_re': 'delta_w', 'delta_s5_a_im': 'delta_w', 'delta_s5_log_dt': 'delta_w', 'delta_s5_b_re': 'delta_w', 'delta_s5_b_im': 'delta_w', 'delta_s5_c_re': 'delta_w', 'delta_s5_c_im': 'delta_w', 'delta_s5_d': 'delta_w', 'delta_s5_glu_w': 'delta_w', 'delta_s5_glu_b': 'delta_w', 'delta_gdn_w_in': 'delta_w', 'delta_gdn_conv': 'delta_w', 'delta_gdn_a_log': 'delta_w', 'delta_gdn_dt_bias': 'delta_w', 'delta_gdn_norm_g': 'delta_w', 'delta_gdn_w_out': 'delta_w', 'new_m_ada_w': 'new_m', 'new_m_ada_b': 'new_m', 'new_m_pre_g': 'new_m', 'new_m_post_g': 'new_m', 'new_m_rel_bias': 'new_m', 'new_m_ab_w_in': 'new_m', 'new_m_ab_w_out': 'new_m', 'new_m_s5_a_re': 'new_m', 'new_m_s5_a_im': 'new_m', 'new_m_s5_log_dt': 'new_m', 'new_m_s5_b_re': 'new_m', 'new_m_s5_b_im': 'new_m', 'new_m_s5_c_re': 'new_m', 'new_m_s5_c_im': 'new_m', 'new_m_s5_d': 'new_m', 'new_m_s5_glu_w': 'new_m', 'new_m_s5_glu_b': 'new_m', 'new_m_gdn_w_in': 'new_m', 'new_m_gdn_conv': 'new_m', 'new_m_gdn_a_log': 'new_m', 'new_m_gdn_dt_bias': 'new_m', 'new_m_gdn_norm_g': 'new_m', 'new_m_gdn_w_out': 'new_m', 'new_v_ada_w': 'new_v', 'new_v_ada_b': 'new_v', 'new_v_pre_g': 'new_v', 'new_v_post_g': 'new_v', 'new_v_rel_bias': 'new_v', 'new_v_ab_w_in': 'new_v', 'new_v_ab_w_out': 'new_v', 'new_v_s5_a_re': 'new_v', 'new_v_s5_a_im': 'new_v', 'new_v_s5_log_dt': 'new_v', 'new_v_s5_b_re': 'new_v', 'new_v_s5_b_im': 'new_v', 'new_v_s5_c_re': 'new_v', 'new_v_s5_c_im': 'new_v', 'new_v_s5_d': 'new_v', 'new_v_s5_glu_w': 'new_v', 'new_v_s5_glu_b': 'new_v', 'new_v_gdn_w_in': 'new_v', 'new_v_gdn_conv': 'new_v', 'new_v_gdn_a_log': 'new_v', 'new_v_gdn_dt_bias': 'new_v', 'new_v_gdn_norm_g': 'new_v', 'new_v_gdn_w_out': 'new_v'}


def _forward(args):
    return _fwd_reference(*[args[k] for k in FWD_PARAMS])


def _output_shape():
    def fwd():
        inp = _fwd_setup_inputs(0)
        return _fwd_reference(*[inp[k] for k in FWD_PARAMS])
    out = _jax.eval_shape(fwd)
    return out.shape, out.dtype

N_MICROBATCH = 1
ADAM_LR = 0.001
ADAM_B1 = 0.9
ADAM_B2 = 0.999
ADAM_EPS = 1e-08
ADAM_WD = 0.01
ADAM_STEP = 10
PER_EXAMPLE_BATCH_AXIS = {'x': 0, 'c': 0, 'loss_target': 0}
SHARED_INPUTS = []
_WEIGHT_DTYPES = {'ada_w': _jnp.float32, 'ada_b': _jnp.float32, 'pre_g': _jnp.float32, 'post_g': _jnp.float32, 'rel_bias': _jnp.float32, 'ab_w_in': _jnp.float32, 'ab_w_out': _jnp.float32, 's5_a_re': _jnp.float32, 's5_a_im': _jnp.float32, 's5_log_dt': _jnp.float32, 's5_b_re': _jnp.float32, 's5_b_im': _jnp.float32, 's5_c_re': _jnp.float32, 's5_c_im': _jnp.float32, 's5_d': _jnp.float32, 's5_glu_w': _jnp.float32, 's5_glu_b': _jnp.float32, 'gdn_w_in': _jnp.float32, 'gdn_conv': _jnp.float32, 'gdn_a_log': _jnp.float32, 'gdn_dt_bias': _jnp.float32, 'gdn_norm_g': _jnp.float32, 'gdn_w_out': _jnp.float32}
MOMENT_SCALE = {'ada_w': 6.968057e-01, 'ada_b': 1.556281e+00, 'pre_g': 7.717244e-02, 'post_g': 1.133799e+00, 'rel_bias': 3.595761e-02, 'ab_w_in': 4.814679e-02, 'ab_w_out': 6.345567e-02, 's5_a_re': 3.501865e-03, 's5_a_im': 4.690446e-03, 's5_log_dt': 1.573301e+00, 's5_b_re': 2.466422e-03, 's5_b_im': 2.357497e-03, 's5_c_re': 4.784458e-03, 's5_c_im': 4.937524e-03, 's5_d': 7.900170e-02, 's5_glu_w': 1.753798e-02, 's5_glu_b': 3.255065e-02, 'gdn_w_in': 4.137735e-02, 'gdn_conv': 3.818471e-02, 'gdn_a_log': 4.085861e-01, 'gdn_dt_bias': 3.673886e-01, 'gdn_norm_g': 1.516084e-01, 'gdn_w_out': 6.225734e-02}


def _to_microbatches(a, axis):
    t = _jnp.moveaxis(a, axis, 0)
    t = t.reshape((N_MICROBATCH, t.shape[0] // N_MICROBATCH) + t.shape[1:])
    return _jnp.moveaxis(t, 1, axis + 1)


def setup_inputs(seed: int = 0) -> dict:
    inp = _fwd_setup_inputs(seed)
    key = _jax.random.fold_in(_jax.random.key(seed), 7919)
    shape, _ = _output_shape()
    out = dict(inp)
    out["loss_target"] = _jax.random.normal(_jax.random.fold_in(key, 0), shape, _jnp.float32)
    for i, name in enumerate(TWIN_WEIGHTS):
        w = inp[name].astype(_jnp.float32)
        if MOMENT_SCALE is None:
            s = _jnp.sqrt(_jnp.mean(_jnp.square(w)) + 1e-30)
        else:
            s = MOMENT_SCALE[name]
        km, kv = _jax.random.split(_jax.random.fold_in(key, i + 1))
        out[name] = w
        out["m_" + name] = s * _jax.random.normal(km, w.shape, _jnp.float32)
        out["v_" + name] = (s * s) * _jax.random.uniform(kv, w.shape, _jnp.float32, 0.5, 1.5)
    if N_MICROBATCH > 1:
        for name, axis in PER_EXAMPLE_BATCH_AXIS.items():
            out[name] = _to_microbatches(out[name], axis)
    return {'x': out['x'], 'c': out['c'], 'ada_w': out['ada_w'], 'ada_b': out['ada_b'], 'pre_g': out['pre_g'], 'post_g': out['post_g'], 'rel_bias': out['rel_bias'], 'ab_w_in': out['ab_w_in'], 'ab_w_out': out['ab_w_out'], 's5_a_re': out['s5_a_re'], 's5_a_im': out['s5_a_im'], 's5_log_dt': out['s5_log_dt'], 's5_b_re': out['s5_b_re'], 's5_b_im': out['s5_b_im'], 's5_c_re': out['s5_c_re'], 's5_c_im': out['s5_c_im'], 's5_d': out['s5_d'], 's5_glu_w': out['s5_glu_w'], 's5_glu_b': out['s5_glu_b'], 'gdn_w_in': out['gdn_w_in'], 'gdn_conv': out['gdn_conv'], 'gdn_a_log': out['gdn_a_log'], 'gdn_dt_bias': out['gdn_dt_bias'], 'gdn_norm_g': out['gdn_norm_g'], 'gdn_w_out': out['gdn_w_out'], 'loss_target': out['loss_target'], 'm_ada_w': out['m_ada_w'], 'm_ada_b': out['m_ada_b'], 'm_pre_g': out['m_pre_g'], 'm_post_g': out['m_post_g'], 'm_rel_bias': out['m_rel_bias'], 'm_ab_w_in': out['m_ab_w_in'], 'm_ab_w_out': out['m_ab_w_out'], 'm_s5_a_re': out['m_s5_a_re'], 'm_s5_a_im': out['m_s5_a_im'], 'm_s5_log_dt': out['m_s5_log_dt'], 'm_s5_b_re': out['m_s5_b_re'], 'm_s5_b_im': out['m_s5_b_im'], 'm_s5_c_re': out['m_s5_c_re'], 'm_s5_c_im': out['m_s5_c_im'], 'm_s5_d': out['m_s5_d'], 'm_s5_glu_w': out['m_s5_glu_w'], 'm_s5_glu_b': out['m_s5_glu_b'], 'm_gdn_w_in': out['m_gdn_w_in'], 'm_gdn_conv': out['m_gdn_conv'], 'm_gdn_a_log': out['m_gdn_a_log'], 'm_gdn_dt_bias': out['m_gdn_dt_bias'], 'm_gdn_norm_g': out['m_gdn_norm_g'], 'm_gdn_w_out': out['m_gdn_w_out'], 'v_ada_w': out['v_ada_w'], 'v_ada_b': out['v_ada_b'], 'v_pre_g': out['v_pre_g'], 'v_post_g': out['v_post_g'], 'v_rel_bias': out['v_rel_bias'], 'v_ab_w_in': out['v_ab_w_in'], 'v_ab_w_out': out['v_ab_w_out'], 'v_s5_a_re': out['v_s5_a_re'], 'v_s5_a_im': out['v_s5_a_im'], 'v_s5_log_dt': out['v_s5_log_dt'], 'v_s5_b_re': out['v_s5_b_re'], 'v_s5_b_im': out['v_s5_b_im'], 'v_s5_c_re': out['v_s5_c_re'], 'v_s5_c_im': out['v_s5_c_im'], 'v_s5_d': out['v_s5_d'], 'v_s5_glu_w': out['v_s5_glu_w'], 'v_s5_glu_b': out['v_s5_glu_b'], 'v_gdn_w_in': out['v_gdn_w_in'], 'v_gdn_conv': out['v_gdn_conv'], 'v_gdn_a_log': out['v_gdn_a_log'], 'v_gdn_dt_bias': out['v_gdn_dt_bias'], 'v_gdn_norm_g': out['v_gdn_norm_g'], 'v_gdn_w_out': out['v_gdn_w_out']}


def _loss(weights, diff, rest, loss_target):
    with _jax.named_scope("forward"):
        args = {**rest, TWIN_DIFF_INPUT: diff, **{k: w.astype(_WEIGHT_DTYPES[k]) for k, w in weights.items()}}
        y = _forward(args)
    with _jax.named_scope("loss_head"):
        err = _jnp.square(y.astype(_jnp.float32) - loss_target)
        return 0.5 * _jnp.sum(_jnp.mean(err, axis=-1)) if err.ndim else 0.5 * err


def _adamw(w, g, m, v):
    m = ADAM_B1 * m + (1.0 - ADAM_B1) * g
    v = ADAM_B2 * v + (1.0 - ADAM_B2) * _jnp.square(g)
    m_hat = m / (1.0 - ADAM_B1 ** ADAM_STEP)
    v_hat = v / (1.0 - ADAM_B2 ** ADAM_STEP)
    delta = -ADAM_LR * (m_hat / (_jnp.sqrt(v_hat) + ADAM_EPS) + ADAM_WD * w)
    return delta, m, v


def reference(x, c, ada_w, ada_b, pre_g, post_g, rel_bias, ab_w_in, ab_w_out, s5_a_re, s5_a_im, s5_log_dt, s5_b_re, s5_b_im, s5_c_re, s5_c_im, s5_d, s5_glu_w, s5_glu_b, gdn_w_in, gdn_conv, gdn_a_log, gdn_dt_bias, gdn_norm_g, gdn_w_out, loss_target, m_ada_w, m_ada_b, m_pre_g, m_post_g, m_rel_bias, m_ab_w_in, m_ab_w_out, m_s5_a_re, m_s5_a_im, m_s5_log_dt, m_s5_b_re, m_s5_b_im, m_s5_c_re, m_s5_c_im, m_s5_d, m_s5_glu_w, m_s5_glu_b, m_gdn_w_in, m_gdn_conv, m_gdn_a_log, m_gdn_dt_bias, m_gdn_norm_g, m_gdn_w_out, v_ada_w, v_ada_b, v_pre_g, v_post_g, v_rel_bias, v_ab_w_in, v_ab_w_out, v_s5_a_re, v_s5_a_im, v_s5_log_dt, v_s5_b_re, v_s5_b_im, v_s5_c_re, v_s5_c_im, v_s5_d, v_s5_glu_w, v_s5_glu_b, v_gdn_w_in, v_gdn_conv, v_gdn_a_log, v_gdn_dt_bias, v_gdn_norm_g, v_gdn_w_out):
    given = dict(x=x, c=c, ada_w=ada_w, ada_b=ada_b, pre_g=pre_g, post_g=post_g, rel_bias=rel_bias, ab_w_in=ab_w_in, ab_w_out=ab_w_out, s5_a_re=s5_a_re, s5_a_im=s5_a_im, s5_log_dt=s5_log_dt, s5_b_re=s5_b_re, s5_b_im=s5_b_im, s5_c_re=s5_c_re, s5_c_im=s5_c_im, s5_d=s5_d, s5_glu_w=s5_glu_w, s5_glu_b=s5_glu_b, gdn_w_in=gdn_w_in, gdn_conv=gdn_conv, gdn_a_log=gdn_a_log, gdn_dt_bias=gdn_dt_bias, gdn_norm_g=gdn_norm_g, gdn_w_out=gdn_w_out, loss_target=loss_target, m_ada_w=m_ada_w, m_ada_b=m_ada_b, m_pre_g=m_pre_g, m_post_g=m_post_g, m_rel_bias=m_rel_bias, m_ab_w_in=m_ab_w_in, m_ab_w_out=m_ab_w_out, m_s5_a_re=m_s5_a_re, m_s5_a_im=m_s5_a_im, m_s5_log_dt=m_s5_log_dt, m_s5_b_re=m_s5_b_re, m_s5_b_im=m_s5_b_im, m_s5_c_re=m_s5_c_re, m_s5_c_im=m_s5_c_im, m_s5_d=m_s5_d, m_s5_glu_w=m_s5_glu_w, m_s5_glu_b=m_s5_glu_b, m_gdn_w_in=m_gdn_w_in, m_gdn_conv=m_gdn_conv, m_gdn_a_log=m_gdn_a_log, m_gdn_dt_bias=m_gdn_dt_bias, m_gdn_norm_g=m_gdn_norm_g, m_gdn_w_out=m_gdn_w_out, v_ada_w=v_ada_w, v_ada_b=v_ada_b, v_pre_g=v_pre_g, v_post_g=v_post_g, v_rel_bias=v_rel_bias, v_ab_w_in=v_ab_w_in, v_ab_w_out=v_ab_w_out, v_s5_a_re=v_s5_a_re, v_s5_a_im=v_s5_a_im, v_s5_log_dt=v_s5_log_dt, v_s5_b_re=v_s5_b_re, v_s5_b_im=v_s5_b_im, v_s5_c_re=v_s5_c_re, v_s5_c_im=v_s5_c_im, v_s5_d=v_s5_d, v_s5_glu_w=v_s5_glu_w, v_s5_glu_b=v_s5_glu_b, v_gdn_w_in=v_gdn_w_in, v_gdn_conv=v_gdn_conv, v_gdn_a_log=v_gdn_a_log, v_gdn_dt_bias=v_gdn_dt_bias, v_gdn_norm_g=v_gdn_norm_g, v_gdn_w_out=v_gdn_w_out)
    weights = {n: given[n] for n in TWIN_WEIGHTS}
    shared = {n: given[n] for n in SHARED_INPUTS}
    per_example = {n: given[n] for n in ['x', 'c']}
    grad_fn = _jax.value_and_grad(_loss, argnums=(0, 1))

    def one_microbatch(ex, loss_target):
        ex = dict(ex)
        diff = ex.pop(TWIN_DIFF_INPUT)
        return grad_fn(weights, diff, {**shared, **ex}, loss_target)

    if N_MICROBATCH == 1:
        loss, (grad_w, grad_x) = one_microbatch(per_example, given["loss_target"])
    else:
        def body(carry, xs):
            loss_sum, grad_sum = carry
            l_k, (gw_k, gx_k) = one_microbatch(xs[0], xs[1])
            with _jax.named_scope("update"):
                return (loss_sum + l_k, _jax.tree.map(_jnp.add, grad_sum, gw_k)), gx_k

        init = (_jnp.zeros((), _jnp.float32), _jax.tree.map(_jnp.zeros_like, weights))
        (loss, grad_w), grad_x = _jax.lax.scan(body, init, (per_example, given["loss_target"]))
    with _jax.named_scope("update"):
        delta_w, new_m, new_v = {}, {}, {}
        for n in TWIN_WEIGHTS:
            delta_w[n], new_m[n], new_v[n] = _adamw(weights[n], grad_w[n], given["m_" + n], given["v_" + n])
    return (loss, grad_x, *[grad_w[n] for n in TWIN_WEIGHTS], *[delta_w[n] for n in TWIN_WEIGHTS],
            *[new_m[n] for n in TWIN_WEIGHTS], *[new_v[n] for n in TWIN_WEIGHTS])
```

```python
import functools
import math

import numpy as np
import jax
import jax.numpy as jnp
from jax import lax
from jax.experimental import pallas as pl
from jax.experimental.pallas import tpu as pltpu

F32 = jnp.float32
BF = jnp.bfloat16
HI = lax.Precision.HIGHEST
MESH = pl.DeviceIdType.MESH

D_MODEL = 1024
EPS = 1e-6
A_HEADS, A_HD, A_WIDTH, A_BLOCK = 8, 64, 512, 128
DILATIONS = (1, 4, 16)
N_KEYS = 128
REL_BUCKETS, REL_MAX_DIST = 32, 2048
B_WIDTH, B_GROUP, B_GROUPS, B_STATE = 512, 16, 32, 64
S5_LANES = 512
S5_TILES = 4
S5_T = 256
C_HEADS, C_DK, C_CHUNK, C_CONV = 8, 128, 64, 4
QKV = 3072
C_IN_PAD = 4224
TM = 256
VMEM_LIMIT_BYTES = 56 * 1024 * 1024
ADAM_LR, ADAM_B1, ADAM_B2, ADAM_EPS, ADAM_WD, ADAM_STEP = 0.001, 0.9, 0.999, 1e-08, 0.01, 10
NEG = float(np.finfo(np.float32).min)


def _cp(*sem):
    return pltpu.CompilerParams(dimension_semantics=sem, vmem_limit_bytes=VMEM_LIMIT_BYTES)


def _bdot(a, b):
    return jnp.dot(a.astype(BF), b.astype(BF), preferred_element_type=F32)


def _bdot_nt(a, b):
    return lax.dot_general(a.astype(BF), b.astype(BF), (((1,), (1,)), ((), ())), preferred_element_type=F32)


def _bdot_tn(a, b):
    return lax.dot_general(a.astype(BF), b.astype(BF), (((0,), (0,)), ((), ())), preferred_element_type=F32)


def _hdot(a, b):
    return jnp.dot(a, b, precision=HI, preferred_element_type=F32)


def _bein(eq, a, b):
    return jnp.einsum(eq, a.astype(BF), b.astype(BF), preferred_element_type=F32)


def _hein(eq, a, b):
    return jnp.einsum(eq, a, b, precision=HI, preferred_element_type=F32)


def _row(tm, n):
    return pl.BlockSpec((tm, n), lambda i: (i, 0))


def _fix(shape):
    return pl.BlockSpec(shape, lambda i: (0,) * len(shape))


def _sds(*shape, dtype=F32):
    return jax.ShapeDtypeStruct(shape, dtype)


def _acc(ref, val):
    ref[...] += val


def _zero_at_first(refs, axis=0):
    @pl.when(pl.program_id(axis) == 0)
    def _():
        for r in refs:
            r[...] = jnp.zeros_like(r)


def _rms(x):
    return x * lax.rsqrt(jnp.mean(x * x, axis=-1, keepdims=True) + EPS)


def _pre_mod(x, g, scale, shift):
    return (_rms(x) * g) * (1.0 + scale) + shift


def _post_res(y, x, post_g, gate):
    return x + gate * (_rms(y) * post_g)


def _merge_gate(o1, o2, o3, l1, l2, l3, ga):
    m = jnp.maximum(jnp.maximum(l1, l2), l3)
    e1, e2, e3 = jnp.exp(l1 - m), jnp.exp(l2 - m), jnp.exp(l3 - m)
    inv = 1.0 / (e1 + e2 + e3)
    return ((e1 * inv) * o1 + (e2 * inv) * o2 + (e3 * inv) * o3) * jax.nn.silu(ga)


def _s5_gelu(ypre, u, d_skip):
    return jax.nn.gelu(ypre + d_skip * u)


def _s5_glu(yb, gl, gb):
    return yb * jax.nn.sigmoid(gl) * jax.nn.silu(gb)


def _l0_front(x, pre_g, scale, shift, w_in):
    s_len = x.shape[0]

    def body(x_ref, g_ref, sc_ref, sh_ref, w_ref, q_ref, k_ref, v_ref, u_ref, ga_ref, gb_ref, h_ref):
        hb = _pre_mod(x_ref[...], g_ref[...], sc_ref[...], sh_ref[...]).astype(BF)
        h_ref[...] = hb
        z = jnp.dot(hb, w_ref[...], preferred_element_type=F32)
        q_ref[...] = z[:, 0:512]
        k_ref[...] = z[:, 512:1024]
        v_ref[...] = z[:, 1024:1536]
        u_ref[...] = z[:, 1536:2048]
        ga_ref[...] = z[:, 2048:2560]
        gb_ref[...] = z[:, 2560:3072]

    vec = _fix((1, D_MODEL))
    return pl.pallas_call(
        body, name="l0_front", grid=(s_len // TM,),
        in_specs=[_row(TM, D_MODEL), vec, vec, vec, _fix((D_MODEL, 3072))],
        out_specs=[_row(TM, 512)] * 6 + [_row(TM, D_MODEL)],
        out_shape=[_sds(s_len, 512)] * 6 + [_sds(s_len, D_MODEL, dtype=BF)],
        compiler_params=_cp("arbitrary"),
    )(x, pre_g, scale, shift, w_in)


def _front_bwd(name, x, pre_g, scale, shift, w_in, dres, parts, widths):
    s_len = x.shape[0]
    n_in = sum(len(p) for p in parts)
    n_cols = sum(widths)

    def body(*refs):
        x_ref, g_ref, sc_ref, sh_ref, w_ref, dres_ref = refs[:6]
        part_refs = refs[6:6 + n_in]
        dz_ref, dx_ref, dg_ref, dsc_ref, dsh_ref = refs[6 + n_in:]
        _zero_at_first([dg_ref, dsc_ref, dsh_ref])
        _, vjp = jax.vjp(_pre_mod, x_ref[...], g_ref[...], sc_ref[...], sh_ref[...])
        dh = jnp.zeros((TM, D_MODEL), F32)
        col, at = 0, 0
        for grp, width in zip(parts, widths):
            dz = part_refs[at][...]
            for r in part_refs[at + 1:at + len(grp)]:
                dz = dz + r[...]
            at += len(grp)
            dzb = dz.astype(BF)
            dz_ref[:, col:col + width] = dzb
            dh = dh + lax.dot_general(dzb, w_ref[:, col:col + width], (((1,), (1,)), ((), ())), preferred_element_type=F32)
            col += width
        dx, dg, dsc, dsh = vjp(dh)
        dx_ref[...] = dx + dres_ref[...]
        _acc(dg_ref, dg)
        _acc(dsc_ref, dsc)
        _acc(dsh_ref, dsh)

    vec = _fix((1, D_MODEL))
    flat = [a for p in parts for a in p]
    return pl.pallas_call(
        body, name=name, grid=(s_len // TM,),
        in_specs=[_row(TM, D_MODEL), vec, vec, vec, _fix((D_MODEL, n_cols)), _row(TM, D_MODEL)]
        + [_row(TM, a.shape[1]) for a in flat],
        out_specs=[_row(TM, n_cols), _row(TM, D_MODEL), vec, vec, vec],
        out_shape=[_sds(s_len, n_cols, dtype=BF), _sds(s_len, D_MODEL), _sds(1, D_MODEL), _sds(1, D_MODEL), _sds(1, D_MODEL)],
        compiler_params=_cp("arbitrary"),
    )(x, pre_g, scale, shift, w_in, dres, *flat)


def _matmul_tn(name, a, b, tn):
    s_len, k_dim = a.shape
    n_dim = b.shape[1]
    ts = 512

    def body(a_ref, b_ref, o_ref):
        _zero_at_first([o_ref], axis=1)
        o_ref[...] += lax.dot_general(a_ref[...], b_ref[...], (((0,), (0,)), ((), ())), preferred_element_type=F32)

    return pl.pallas_call(
        body, name=name, grid=(n_dim // tn, s_len // ts),
        in_specs=[pl.BlockSpec((ts, k_dim), lambda j, i: (i, 0)), pl.BlockSpec((ts, tn), lambda j, i: (i, j))],
        out_specs=pl.BlockSpec((k_dim, tn), lambda j, i: (0, j)),
        out_shape=_sds(k_dim, n_dim),
        compiler_params=_cp("arbitrary", "arbitrary"),
    )(a, b)


def _t5_bucket_np(dist):
    dist = np.maximum(dist, 0)
    max_exact = REL_BUCKETS // 2
    large = max_exact + (np.log(np.maximum(dist, 1) / max_exact)
                         / math.log(REL_MAX_DIST / max_exact) * (REL_BUCKETS - max_exact)).astype(np.int32)
    large = np.minimum(large, REL_BUCKETS - 1)
    return np.where(dist < max_exact, dist, large).astype(np.int32)


def _bucket_table():
    qi = np.arange(A_BLOCK)[:, None]
    kj = np.arange(2 * A_BLOCK)[None, :]
    rel = qi + A_BLOCK - kj
    return np.stack([_t5_bucket_np(rel * d) for d in DILATIONS], 0)


def _attn_mask(first):
    qi = lax.broadcasted_iota(jnp.int32, (A_BLOCK, 2 * A_BLOCK), 0)
    kj = lax.broadcasted_iota(jnp.int32, (A_BLOCK, 2 * A_BLOCK), 1)
    rel = qi + A_BLOCK - kj
    band = (rel >= 0) & (rel <= N_KEYS)
    return band & (jnp.logical_not(first) | (kj >= A_BLOCK))


def _attn_first(s_len):
    c, g = pl.program_id(0), pl.program_id(1)
    nbs = [s_len // A_BLOCK // d for d in DILATIONS]
    nb = jnp.where(c == 0, nbs[0], jnp.where(c == 1, nbs[1], nbs[2]))
    return (g % nb) == 0


def _attn_specs():
    cur = pl.BlockSpec((1, A_BLOCK, A_WIDTH), lambda c, g: (c, g, 0))
    prev = pl.BlockSpec((1, A_BLOCK, A_WIDTH), lambda c, g: (c, jnp.maximum(g - 1, 0), 0))
    bias = pl.BlockSpec((1, A_HEADS, A_BLOCK, 2 * A_BLOCK), lambda c, g: (c, 0, 0, 0))
    return cur, prev, bias


def _attn_fwd(q3, k3, v3, bias3):
    s_len = q3.shape[1]
    scale = A_HD ** -0.5

    def body(q_ref, kp_ref, kc_ref, vp_ref, vc_ref, b_ref, o_ref, l_ref):
        mask = _attn_mask(_attn_first(s_len))
        lane = lax.broadcasted_iota(jnp.int32, (1, 128), 1)
        for hp in range(A_HEADS // 2):
            sl = slice(hp * 128, (hp + 1) * 128)
            qp = q_ref[0, :, sl]
            kw = jnp.concatenate([kp_ref[0, :, sl], kc_ref[0, :, sl]], axis=0).astype(BF)
            vw = jnp.concatenate([vp_ref[0, :, sl], vc_ref[0, :, sl]], axis=0).astype(BF)
            outs, lses = [], []
            for j in range(2):
                hm = (lane < 64) if j == 0 else (lane >= 64)
                s = _bdot_nt(jnp.where(hm, qp, 0.0), kw) * scale
                s = jnp.where(mask, s + b_ref[0, 2 * hp + j], NEG)
                m = jnp.max(s, axis=-1, keepdims=True)
                p = jnp.exp(s - m)
                den = jnp.sum(p, axis=-1, keepdims=True)
                outs.append(_bdot(p, vw) / den)
                lses.append(m + jnp.log(den))
            hm0 = lane < 64
            o_ref[0, :, sl] = jnp.where(hm0, outs[0], outs[1])
            l_ref[0, :, sl] = jnp.where(hm0, lses[0], lses[1])

    cur, prev, bias = _attn_specs()
    return pl.pallas_call(
        body, name="attn_fwd", grid=(3, s_len // A_BLOCK),
        in_specs=[cur, prev, cur, prev, cur, bias],
        out_specs=[cur, cur],
        out_shape=[_sds(3, s_len, A_WIDTH)] * 2,
        compiler_params=_cp("arbitrary", "arbitrary"),
    )(q3, k3, k3, v3, v3, bias3)


def _attn_bwd(q3, k3, v3, bias3, o3, l3, do3, dl3):
    s_len = q3.shape[1]
    scale = A_HD ** -0.5

    def body(q_ref, kp_ref, kc_ref, vp_ref, vc_ref, b_ref, o_ref, l_ref, do_ref, dl_ref,
             dq_ref, dka_ref, dkb_ref, dva_ref, dvb_ref, db_ref):
        _zero_at_first([db_ref], axis=1)
        mask = _attn_mask(_attn_first(s_len))
        lane = lax.broadcasted_iota(jnp.int32, (1, 128), 1)
        for hp in range(A_HEADS // 2):
            sl = slice(hp * 128, (hp + 1) * 128)
            qp = q_ref[0, :, sl]
            kw = jnp.concatenate([kp_ref[0, :, sl], kc_ref[0, :, sl]], axis=0).astype(BF)
            vw = jnp.concatenate([vp_ref[0, :, sl], vc_ref[0, :, sl]], axis=0).astype(BF)
            op, lp, dop, dlp = o_ref[0, :, sl], l_ref[0, :, sl], do_ref[0, :, sl], dl_ref[0, :, sl]
            dq_acc = jnp.zeros((A_BLOCK, 128), F32)
            dk_acc = jnp.zeros((2 * A_BLOCK, 128), F32)
            dv_acc = jnp.zeros((2 * A_BLOCK, 128), F32)
            for j in range(2):
                hm = (lane < 64) if j == 0 else (lane >= 64)
                qm = jnp.where(hm, qp, 0.0)
                s = _bdot_nt(qm, kw) * scale
                s = jnp.where(mask, s + b_ref[0, 2 * hp + j], NEG)
                lse = jnp.max(jnp.where(hm, lp, NEG), axis=-1, keepdims=True)
                p = jnp.exp(s - lse)
                do_h = jnp.where(hm, dop, 0.0)
                dd = jnp.sum(do_h * op, axis=-1, keepdims=True)
                dl = jnp.sum(jnp.where(hm, dlp, 0.0), axis=-1, keepdims=True)
                ds = p * (_bdot_nt(do_h, vw) - dd + dl)
                dv_acc = dv_acc + _bdot_tn(p, do_h)
                dq_acc = dq_acc + jnp.where(hm, _bdot(ds, kw), 0.0) * scale
                dk_acc = dk_acc + _bdot_tn(ds, qm) * scale
                db_ref[0, 2 * hp + j] += ds
            dq_ref[0, :, sl] = dq_acc
            dkb_ref[0, :, sl] = dk_acc[:A_BLOCK]
            dka_ref[0, :, sl] = dk_acc[A_BLOCK:]
            dvb_ref[0, :, sl] = dv_acc[:A_BLOCK]
            dva_ref[0, :, sl] = dv_acc[A_BLOCK:]

    cur, prev, bias = _attn_specs()
    return pl.pallas_call(
        body, name="attn_bwd", grid=(3, s_len // A_BLOCK),
        in_specs=[cur, prev, cur, prev, cur, bias, cur, cur, cur, cur],
        out_specs=[cur] * 5 + [bias],
        out_shape=[_sds(3, s_len, A_WIDTH)] * 5 + [_sds(3, A_HEADS, A_BLOCK, 2 * A_BLOCK)],
        compiler_params=_cp("arbitrary", "arbitrary"),
    )(q3, k3, k3, v3, v3, bias3, o3, l3, do3, dl3)


def _rel_bias_grad(db3, idx_rows):
    n = A_BLOCK * 2 * A_BLOCK
    dbf = db3.reshape(3, A_HEADS, n)

    def body(db_ref, idx_ref, o_ref):
        bucket = lax.broadcasted_iota(jnp.int32, (REL_BUCKETS, n), 0).astype(F32)
        acc = jnp.zeros((A_HEADS, REL_BUCKETS), F32)
        for c in range(3):
            onehot = (idx_ref[c:c + 1, :] == bucket).astype(F32)
            acc = acc + lax.dot_general(db_ref[c], onehot, (((1,), (1,)), ((), ())), precision=HI, preferred_element_type=F32)
        o_ref[...] = acc

    return pl.pallas_call(body, name="rel_bias_grad", out_shape=_sds(A_HEADS, REL_BUCKETS),
                          compiler_params=pltpu.CompilerParams(vmem_limit_bytes=VMEM_LIMIT_BYTES))(dbf, idx_rows)


def _perm(a, dil):
    s_len, n = a.shape
    return a.reshape(s_len // dil, dil, n).transpose(1, 0, 2).reshape(s_len, n)


def _unperm(a, dil):
    s_len, n = a.shape
    return a.reshape(dil, s_len // dil, n).transpose(1, 0, 2).reshape(s_len, n)


def _perm3(a):
    return jnp.stack([_perm(a, d) for d in DILATIONS], 0)


def _unperm3(a3):
    return jnp.stack([_unperm(a3[i], d) for i, d in enumerate(DILATIONS)], 0)


def _s5_param_fn(a_re, a_im, log_dt, bt_re, bt_im):
    dt = jnp.exp(log_dt)
    mag = jnp.exp(dt * a_re)
    abar_r, abar_i = mag * jnp.cos(dt * a_im), mag * jnp.sin(dt * a_im)
    den = a_re * a_re + a_im * a_im
    fr = ((abar_r - 1.0) * a_re + abar_i * a_im) / den
    fi = (abar_i * a_re - (abar_r - 1.0) * a_im) / den
    row = lax.broadcasted_iota(jnp.int32, (B_WIDTH, B_GROUPS), 0)
    grp = lax.broadcasted_iota(jnp.int32, (B_WIDTH, B_GROUPS), 1)
    expand = ((row // B_GROUP) == grp).astype(F32)
    fr_e, fi_e = _hdot(expand, fr), _hdot(expand, fi)
    return abar_r, abar_i, fr_e * bt_re - fi_e * bt_im, fr_e * bt_im + fi_e * bt_re


def _s5_params(a_re, a_im, log_dt, bt_re, bt_im):
    def body(ar, ai, ld, br, bi, o1, o2, o3, o4):
        o1[...], o2[...], o3[...], o4[...] = _s5_param_fn(ar[...], ai[...], ld[...], br[...], bi[...])

    return pl.pallas_call(body, name="s5_params",
                          out_shape=[_sds(B_GROUPS, B_STATE)] * 2 + [_sds(B_WIDTH, B_STATE)] * 2)(a_re, a_im, log_dt, bt_re, bt_im)


def _s5_params_bwd(a_re, a_im, log_dt, bt_re, bt_im, d1, d2, d3, d4):
    def body(ar, ai, ld, br, bi, c1, c2, c3, c4, o1, o2, o3, o4, o5):
        _, vjp = jax.vjp(_s5_param_fn, ar[...], ai[...], ld[...], br[...], bi[...])
        o1[...], o2[...], o3[...], o4[...], o5[...] = vjp((c1[...], c2[...], c3[...], c4[...]))

    return pl.pallas_call(body, name="s5_params_bwd",
                          out_shape=[_sds(B_GROUPS, B_STATE)] * 2 + [_sds(B_GROUPS, 1)] + [_sds(B_WIDTH, B_STATE)] * 2,
                          )(a_re, a_im, log_dt, bt_re, bt_im, d1, d2, d3, d4)


def _cscan(br, bi, ar, ai, reverse):
    t_len = br.shape[0]
    rows = lax.broadcasted_iota(jnp.int32, br.shape, 0)
    xr, xi, cr, ci = br, bi, ar, ai
    k = 1
    while k < t_len:
        if reverse:
            keep, shift = rows < t_len - k, t_len - k
        else:
            keep, shift = rows >= k, k
        sr = jnp.where(keep, pltpu.roll(xr, shift, 0), 0.0)
        si = jnp.where(keep, pltpu.roll(xi, shift, 0), 0.0)
        xr, xi = xr + cr * sr - ci * si, xi + cr * si + ci * sr
        cr, ci = cr * cr - ci * ci, 2.0 * cr * ci
        k *= 2
    return xr, xi


def _pick_row(x, r):
    rows = lax.broadcasted_iota(jnp.int32, x.shape, 0)
    return jnp.sum(jnp.where(rows == r, x, 0.0), axis=0, keepdims=True)


def _s5_tile_specs(n_t, rev):
    t_of = (lambda i: n_t - 1 - i) if rev else (lambda i: i)
    u_spec = pl.BlockSpec((S5_T, 128), lambda j, i: (t_of(i), j))
    x_spec = pl.BlockSpec((S5_T, S5_LANES), lambda j, i: (t_of(i), j))
    b_spec = pl.BlockSpec((1, 128, S5_LANES), lambda j, i: (j, 0, 0))
    c_spec = pl.BlockSpec((1, S5_LANES, 128), lambda j, i: (j, 0, 0))
    a_spec = pl.BlockSpec((1, S5_LANES), lambda j, i: (0, j))
    return u_spec, x_spec, b_spec, c_spec, a_spec


def _s5_scan_fwd(u, btr, bti, ctr, cti, abr, abi):
    s_len = u.shape[0]
    n_t = s_len // S5_T

    def body(u_ref, btr_ref, bti_ref, ctr_ref, cti_ref, ar_ref, ai_ref, xr_ref, xi_ref, y_ref, car, cai, pwr, pwi):
        ar, ai = ar_ref[...], ai_ref[...]

        @pl.when(pl.program_id(1) == 0)
        def _():
            car[...] = jnp.zeros_like(car)
            cai[...] = jnp.zeros_like(cai)
            rows = lax.broadcasted_iota(jnp.int32, (S5_T, S5_LANES), 0)
            pwr[...], pwi[...] = _cscan(jnp.where(rows == 0, ar, 0.0), jnp.where(rows == 0, ai, 0.0), ar, ai, False)

        ub = u_ref[...]
        xr, xi = _cscan(_bdot(ub, btr_ref[0]), _bdot(ub, bti_ref[0]), ar, ai, False)
        cr, ci = car[...], cai[...]
        pr, pi_ = pwr[...], pwi[...]
        xr, xi = xr + pr * cr - pi_ * ci, xi + pr * ci + pi_ * cr
        xr_ref[...] = xr
        xi_ref[...] = xi
        car[...] = _pick_row(xr, S5_T - 1)
        cai[...] = _pick_row(xi, S5_T - 1)
        y_ref[...] = _bdot(xr, ctr_ref[0]) - _bdot(xi, cti_ref[0])

    u_spec, x_spec, b_spec, c_spec, a_spec = _s5_tile_specs(n_t, False)
    return pl.pallas_call(
        body, name="s5_scan_fwd", grid=(S5_TILES, n_t),
        in_specs=[u_spec, b_spec, b_spec, c_spec, c_spec, a_spec, a_spec],
        out_specs=[x_spec, x_spec, u_spec],
        out_shape=[_sds(s_len, S5_TILES * S5_LANES)] * 2 + [_sds(s_len, B_WIDTH)],
        scratch_shapes=[pltpu.VMEM((1, S5_LANES), F32)] * 2 + [pltpu.VMEM((S5_T, S5_LANES), F32)] * 2,
        compiler_params=_cp("arbitrary", "arbitrary"),
    )(u, btr, bti, ctr, cti, abr, abi)


def _s5_scan_bwd(dy, xr, xi, u, btr, bti, ctr, cti, abr, abi):
    s_len = u.shape[0]
    n_t = s_len // S5_T

    def body(dy_ref, xr_ref, xi_ref, xrp_ref, xip_ref, u_ref, btr_ref, bti_ref, ctr_ref, cti_ref, ar_ref, ai_ref,
             du_ref, dbtr_ref, dbti_ref, dctr_ref, dcti_ref, dar_ref, dai_ref, car, cai, pwr, pwi):
        ar, ai = ar_ref[...], ai_ref[...]
        i = pl.program_id(1)
        rows = lax.broadcasted_iota(jnp.int32, (S5_T, S5_LANES), 0)

        @pl.when(i == 0)
        def _():
            for r in (car, cai, dbtr_ref, dbti_ref, dctr_ref, dcti_ref, dar_ref, dai_ref):
                r[...] = jnp.zeros_like(r)
            last = rows == S5_T - 1
            pwr[...], pwi[...] = _cscan(jnp.where(last, ar, 0.0), jnp.where(last, -ai, 0.0), ar, -ai, True)

        dyb = dy_ref[...]
        xr_b, xi_b, ub = xr_ref[...], xi_ref[...], u_ref[...]
        dctr_ref[0] += _bdot_tn(xr_b, dyb)
        dcti_ref[0] -= _bdot_tn(xi_b, dyb)
        gr, gi = _cscan(_bdot_nt(dyb, ctr_ref[0]), -_bdot_nt(dyb, cti_ref[0]), ar, -ai, True)
        cr, ci = car[...], cai[...]
        pr, pi_ = pwr[...], pwi[...]
        gr, gi = gr + pr * cr - pi_ * ci, gi + pr * ci + pi_ * cr
        car[...] = _pick_row(gr, 0)
        cai[...] = _pick_row(gi, 0)
        du_ref[...] = _bdot_nt(gr, btr_ref[0]) + _bdot_nt(gi, bti_ref[0])
        dbtr_ref[0] += _bdot_tn(ub, gr)
        dbti_ref[0] += _bdot_tn(ub, gi)
        has_prev = (i < n_t - 1).astype(F32)
        hr = _pick_row(xrp_ref[...], 7) * has_prev
        hi = _pick_row(xip_ref[...], 7) * has_prev
        xpr = jnp.where(rows == 0, hr, pltpu.roll(xr_b, 1, 0))
        xpi = jnp.where(rows == 0, hi, pltpu.roll(xi_b, 1, 0))
        dar_ref[...] += jnp.sum(gr * xpr + gi * xpi, axis=0, keepdims=True)
        dai_ref[...] += jnp.sum(gi * xpr - gr * xpi, axis=0, keepdims=True)

    u_spec, x_spec, b_spec, c_spec, a_spec = _s5_tile_specs(n_t, True)
    halo = pl.BlockSpec((8, S5_LANES), lambda j, i: (jnp.maximum((n_t - 1 - i) * (S5_T // 8) - 1, 0), j))
    return pl.pallas_call(
        body, name="s5_scan_bwd", grid=(S5_TILES, n_t),
        in_specs=[u_spec, x_spec, x_spec, halo, halo, u_spec, b_spec, b_spec, c_spec, c_spec, a_spec, a_spec],
        out_specs=[u_spec, b_spec, b_spec, c_spec, c_spec, a_spec, a_spec],
        out_shape=[_sds(s_len, B_WIDTH)] + [_sds(S5_TILES, 128, S5_LANES)] * 2 + [_sds(S5_TILES, S5_LANES, 128)] * 2
        + [_sds(1, S5_TILES * S5_LANES)] * 2,
        scratch_shapes=[pltpu.VMEM((1, S5_LANES), F32)] * 2 + [pltpu.VMEM((S5_T, S5_LANES), F32)] * 2,
        compiler_params=_cp("arbitrary", "arbitrary"),
    )(dy, xr, xi, xr, xi, u, btr, bti, ctr, cti, abr, abi)


def _blockdiag_b(bbar_t):
    blocks = bbar_t.reshape(S5_TILES, 8, B_GROUP, B_STATE)
    return jnp.einsum('jgmp,gh->jgmhp', blocks, jnp.eye(8, dtype=F32)).reshape(S5_TILES, 128, S5_LANES)


def _blockdiag_b_t(d):
    return jnp.einsum('jgmgp->jgmp', d.reshape(S5_TILES, 8, B_GROUP, 8, B_STATE)).reshape(B_WIDTH, B_STATE)


def _blockdiag_c(c):
    blocks = c.reshape(S5_TILES, 8, B_GROUP, B_STATE)
    return jnp.einsum('jgmp,gh->jhpgm', blocks, jnp.eye(8, dtype=F32)).reshape(S5_TILES, S5_LANES, 128)


def _blockdiag_c_t(d):
    return jnp.einsum('jgpgm->jgmp', d.reshape(S5_TILES, 8, B_STATE, 8, B_GROUP)).reshape(B_GROUPS, B_GROUP, B_STATE)


def _l0_out(o3, l3, ga, gb, ypre, u, x, d_skip, glu_w, glu_b, w_out, post_g, gate):
    s_len = x.shape[0]

    def body(o_ref, l_ref, ga_ref, gb_ref, yp_ref, u_ref, x_ref, d_ref, gw_ref, gbias_ref, w_ref, pg_ref, gt_ref, x1_ref, y_ref):
        oa = _merge_gate(o_ref[0], o_ref[1], o_ref[2], l_ref[0], l_ref[1], l_ref[2], ga_ref[...])
        yb = _s5_gelu(yp_ref[...], u_ref[...], d_ref[...])
        ob = _s5_glu(yb, _bdot(yb, gw_ref[...]) + gbias_ref[...], gb_ref[...])
        y = _bdot(oa, w_ref[0:512, :]) + _bdot(ob, w_ref[512:1024, :])
        y_ref[...] = y
        x1_ref[...] = _post_res(y, x_ref[...], pg_ref[...], gt_ref[...])

    t3 = pl.BlockSpec((3, TM, 512), lambda i: (0, i, 0))
    vec, half = _fix((1, D_MODEL)), _fix((1, 512))
    return pl.pallas_call(
        body, name="l0_out", grid=(s_len // TM,),
        in_specs=[t3, t3, _row(TM, 512), _row(TM, 512), _row(TM, 512), _row(TM, 512), _row(TM, D_MODEL),
                  half, _fix((512, 512)), half, _fix((D_MODEL, D_MODEL)), vec, vec],
        out_specs=[_row(TM, D_MODEL)] * 2,
        out_shape=[_sds(s_len, D_MODEL)] * 2,
        compiler_params=_cp("arbitrary"),
    )(o3, l3, ga, gb, ypre, u, x, d_skip, glu_w, glu_b, w_out, post_g, gate)


def _l0_out_bwd(o3, l3, ga, gb, ypre, u, x, y, d_skip, glu_w, glu_b, w_out, post_g, gate, dx1):
    s_len = x.shape[0]

    def body(o_ref, l_ref, ga_ref, gb_ref, yp_ref, u_ref, x_ref, y_ref, d_ref, gw_ref, gbias_ref, w_ref, pg_ref, gt_ref, dx1_ref,
             do_ref, dl_ref, dga_ref, dgb_ref, dyp_ref, du_ref, dd_ref, dgw_ref, dgbias_ref, dw_ref, dpg_ref, dgt_ref):
        _zero_at_first([dd_ref, dgw_ref, dgbias_ref, dw_ref, dpg_ref, dgt_ref])
        _, vjp2 = jax.vjp(_post_res, y_ref[...], x_ref[...], pg_ref[...], gt_ref[...])
        dy, _, dpg, dgt = vjp2(dx1_ref[...])
        _acc(dpg_ref, dpg)
        _acc(dgt_ref, dgt)
        oa, vjp_a = jax.vjp(_merge_gate, o_ref[0], o_ref[1], o_ref[2], l_ref[0], l_ref[1], l_ref[2], ga_ref[...])
        yb, vjp_g = jax.vjp(_s5_gelu, yp_ref[...], u_ref[...], d_ref[...])
        gl = _bdot(yb, gw_ref[...]) + gbias_ref[...]
        ob, vjp_b = jax.vjp(_s5_glu, yb, gl, gb_ref[...])
        dw_ref[0:512, :] += _bdot_tn(oa, dy)
        dw_ref[512:1024, :] += _bdot_tn(ob, dy)
        d1, d2, d3, e1, e2, e3, dga = vjp_a(_bdot_nt(dy, w_ref[0:512, :]))
        do_ref[0], do_ref[1], do_ref[2] = d1, d2, d3
        dl_ref[0], dl_ref[1], dl_ref[2] = e1, e2, e3
        dga_ref[...] = dga
        dyb, dgl, dgb = vjp_b(_bdot_nt(dy, w_ref[512:1024, :]))
        dgb_ref[...] = dgb
        dgw_ref[...] += _bdot_tn(yb, dgl)
        _acc(dgbias_ref, jnp.sum(dgl, axis=0, keepdims=True))
        dyp, du, dd = vjp_g(dyb + _bdot_nt(dgl, gw_ref[...]))
        dyp_ref[...] = dyp
        du_ref[...] = du
        _acc(dd_ref, dd)

    t3 = pl.BlockSpec((3, TM, 512), lambda i: (0, i, 0))
    vec, half = _fix((1, D_MODEL)), _fix((1, 512))
    r5, r10 = _row(TM, 512), _row(TM, D_MODEL)
    return pl.pallas_call(
        body, name="l0_out_bwd", grid=(s_len // TM,),
        in_specs=[t3, t3, r5, r5, r5, r5, r10, r10, half, _fix((512, 512)), half, _fix((D_MODEL, D_MODEL)), vec, vec, r10],
        out_specs=[t3, t3, r5, r5, r5, r5, half, _fix((512, 512)), half, _fix((D_MODEL, D_MODEL)), vec, vec],
        out_shape=[_sds(3, s_len, 512)] * 2 + [_sds(s_len, 512)] * 4
        + [_sds(1, 512), _sds(512, 512), _sds(1, 512), _sds(D_MODEL, D_MODEL), _sds(1, D_MODEL), _sds(1, D_MODEL)],
        compiler_params=_cp("arbitrary"),
    )(o3, l3, ga, gb, ypre, u, x, y, d_skip, glu_w, glu_b, w_out, post_g, gate, dx1)


def _l1_front(x, pre_g, scale, shift, w_in):
    s_len = x.shape[0]

    def body(x_ref, g_ref, sc_ref, sh_ref, w_ref, raw_ref, gate_ref, ba_ref, h_ref):
        hb = _pre_mod(x_ref[...], g_ref[...], sc_ref[...], sh_ref[...]).astype(BF)
        h_ref[...] = hb
        z = jnp.dot(hb, w_ref[...], preferred_element_type=F32)
        raw_ref[...] = z[:, 0:QKV]
        gate_ref[...] = z[:, QKV:QKV + 1024]
        ba_ref[...] = z[:, QKV + 1024:C_IN_PAD]

    vec = _fix((1, D_MODEL))
    return pl.pallas_call(
        body, name="l1_front", grid=(s_len // TM,),
        in_specs=[_row(TM, D_MODEL), vec, vec, vec, _fix((D_MODEL, C_IN_PAD))],
        out_specs=[_row(TM, QKV), _row(TM, 1024), _row(TM, 128), _row(TM, D_MODEL)],
        out_shape=[_sds(s_len, QKV), _sds(s_len, 1024), _sds(s_len, 128), _sds(s_len, D_MODEL, dtype=BF)],
        compiler_params=_cp("arbitrary"),
    )(x, pre_g, scale, shift, w_in)


def _bg_fn(ba, alog_row, dtb_row):
    lane = lax.broadcasted_iota(jnp.int32, (1, 128), 1)
    g = -jnp.exp(alog_row) * jax.nn.softplus(ba + dtb_row)
    return jnp.where(lane < C_HEADS, jax.nn.sigmoid(ba), jnp.where(lane < 2 * C_HEADS, g, 0.0))


def _act_q(c):
    q = jax.nn.silu(c)
    return q * lax.rsqrt(jnp.sum(q * q, axis=-1, keepdims=True) + EPS) * (C_DK ** -0.5)


def _act_k(c):
    k = jax.nn.silu(c)
    return k * lax.rsqrt(jnp.sum(k * k, axis=-1, keepdims=True) + EPS)


def _act_of(s):
    return _act_q if s < 8 else (_act_k if s < 16 else jax.nn.silu)


def _gdn_prep(raw, ba, conv_w, alog_row, dtb_row):
    s_len = raw.shape[0]

    def body(raw_ref, halo_ref, ba_ref, w_ref, al_ref, dt_ref, qkv_ref, bg_ref):
        bg_ref[...] = _bg_fn(ba_ref[...], al_ref[...], dt_ref[...])
        has_prev = (pl.program_id(0) > 0).astype(F32)
        for s in range(24):
            sl = slice(s * 128, (s + 1) * 128)
            cat = jnp.concatenate([halo_ref[:, sl] * has_prev, raw_ref[:, sl]], axis=0)
            conv = w_ref[3:4, sl] * cat[8:]
            for j in range(3):
                conv = conv + w_ref[j:j + 1, sl] * pltpu.roll(cat, 3 - j, 0)[8:]
            qkv_ref[:, sl] = _act_of(s)(conv)

    halo = pl.BlockSpec((8, QKV), lambda i: (jnp.maximum(i * (TM // 8) - 1, 0), 0))
    row128 = _fix((1, 128))
    return pl.pallas_call(
        body, name="gdn_prep", grid=(s_len // TM,),
        in_specs=[_row(TM, QKV), halo, _row(TM, 128), _fix((C_CONV, QKV)), row128, row128],
        out_specs=[_row(TM, QKV), _row(TM, 128)],
        out_shape=[_sds(s_len, QKV), _sds(s_len, 128)],
        compiler_params=_cp("arbitrary"),
    )(raw, raw, ba, conv_w, alog_row, dtb_row)


def _gdn_prep_bwd(raw, ba, conv_w, alog_row, dtb_row, dq, dk, dv, dbg):
    s_len = raw.shape[0]
    n_tiles = s_len // TM
    ext = TM + 8

    def body(raw_ref, prev_ref, next_ref, ba_ref, w_ref, al_ref, dt_ref, dq_ref, dqn_ref, dk_ref, dkn_ref, dv_ref, dvn_ref, dbg_ref,
             draw_ref, dba_ref, dw_ref, dal_ref, ddt_ref):
        _zero_at_first([dw_ref, dal_ref, ddt_ref])
        i = pl.program_id(0)
        _, vjp_bg = jax.vjp(_bg_fn, ba_ref[...], al_ref[...], dt_ref[...])
        dba, dal, ddt = vjp_bg(dbg_ref[...])
        dba_ref[...] = dba
        _acc(dal_ref, dal)
        _acc(ddt_ref, ddt)
        has_prev = (i > 0).astype(F32)
        has_next = (i < n_tiles - 1).astype(F32)
        ct_refs = ((dq_ref, dqn_ref), (dk_ref, dkn_ref), (dv_ref, dvn_ref))
        for s in range(24):
            sl = slice(s * 128, (s + 1) * 128)
            hl = slice((s % 8) * 128, (s % 8 + 1) * 128)
            tile_ref, nxt_ref = ct_refs[s // 8]
            cat = jnp.concatenate([prev_ref[:, sl] * has_prev, raw_ref[:, sl], next_ref[:, sl] * has_next], axis=0)
            shifted = [pltpu.roll(cat, 3 - j, 0)[8:] for j in range(3)] + [cat[8:]]
            conv = w_ref[3:4, sl] * shifted[3]
            for j in range(3):
                conv = conv + w_ref[j:j + 1, sl] * shifted[j]
            ct = jnp.concatenate([tile_ref[:, hl], nxt_ref[:, hl] * has_next], axis=0)
            _, vjp_act = jax.vjp(_act_of(s), conv)
            dconv, = vjp_act(ct)
            draw = w_ref[3:4, sl] * dconv[:TM]
            for j in range(3):
                draw = draw + w_ref[j:j + 1, sl] * pltpu.roll(dconv, ext - (3 - j), 0)[:TM]
            draw_ref[:, sl] = draw
            for j in range(4):
                dw_ref[j:j + 1, sl] += jnp.sum(dconv[:TM] * shifted[j][:TM], axis=0, keepdims=True)

    prev = pl.BlockSpec((8, QKV), lambda i: (jnp.maximum(i * (TM // 8) - 1, 0), 0))
    nxt = lambda n: pl.BlockSpec((8, n), lambda i: (jnp.minimum((i + 1) * (TM // 8), s_len // 8 - 1), 0))
    row128 = _fix((1, 128))
    ct_specs = [_row(TM, 1024), nxt(1024)] * 3
    return pl.pallas_call(
        body, name="gdn_prep_bwd", grid=(n_tiles,),
        in_specs=[_row(TM, QKV), prev, nxt(QKV), _row(TM, 128), _fix((C_CONV, QKV)), row128, row128] + ct_specs + [_row(TM, 128)],
        out_specs=[_row(TM, QKV), _row(TM, 128), _fix((C_CONV, QKV)), row128, row128],
        out_shape=[_sds(s_len, QKV), _sds(s_len, 128), _sds(C_CONV, QKV), _sds(1, 128), _sds(1, 128)],
        compiler_params=_cp("arbitrary"),
    )(raw, raw, raw, ba, conv_w, alog_row, dtb_row, dq, dq, dk, dk, dv, dv, dbg)


def _gdn_chunk(q, k, v, bg, state):
    lane = lax.broadcasted_iota(jnp.int32, (1, 128), 1)
    ri = lax.broadcasted_iota(jnp.int32, (C_CHUNK, C_CHUNK), 0)
    ci = lax.broadcasted_iota(jnp.int32, (C_CHUNK, C_CHUNK), 1)
    gc_t = _hdot((ri >= ci).astype(F32), bg)
    beta = jnp.stack([jnp.sum(jnp.where(lane == h, bg, 0.0), axis=-1, keepdims=True) for h in range(C_HEADS)], axis=0)
    gc = jnp.stack([jnp.sum(jnp.where(lane == C_HEADS + h, gc_t, 0.0), axis=-1, keepdims=True) for h in range(C_HEADS)], axis=0)
    ones = jnp.full((C_HEADS, C_CHUNK, C_DK), 1.0 / C_DK, F32)
    gcj = _hein('hil,hjl->hij', ones, jnp.broadcast_to(gc, (C_HEADS, C_CHUNK, C_DK)))
    tril, strict = (ri >= ci)[None], (ri > ci)[None]
    eye = (ri == ci).astype(F32)[None]
    decay = jnp.exp(jnp.where(tril, gc - gcj, -1e30))
    kb = k * beta
    n_mat = jnp.where(strict, -(_bein('hid,hjd->hij', kb, k) * decay), 0.0)
    inv = eye + n_mat
    p_mat = n_mat
    for _ in range(5):
        p_mat = _hein('hij,hjk->hik', p_mat, p_mat)
        inv = inv + _hein('hij,hjk->hik', inv, p_mat)
    egc = jnp.exp(gc)
    u_c = _hein('hij,hjd->hid', inv, v * beta)
    w_c = _hein('hij,hjd->hid', inv, kb * egc)
    aqk = _bein('hid,hjd->hij', q, k) * decay
    rowi = lax.broadcasted_iota(jnp.int32, (1, C_CHUNK, 1), 1)
    g_last = jnp.sum(jnp.where(rowi == C_CHUNK - 1, gc, 0.0), axis=1, keepdims=True)
    kd = k * jnp.exp(g_last - gc)
    v_new = u_c - _bein('hik,hkv->hiv', w_c, state)
    o = _bein('hik,hkv->hiv', q * egc, state) + _bein('hij,hjv->hiv', aqk, v_new)
    return o, state * jnp.exp(g_last) + _bein('hik,hiv->hkv', kd, v_new)


def _heads(ref):
    return jnp.stack([ref[:, h * C_DK:(h + 1) * C_DK] for h in range(C_HEADS)], axis=0)


def _gdn_fwd(qkv, bg):
    s_len = qkv.shape[0]
    n_c = s_len // C_CHUNK

    def body(q_ref, k_ref, v_ref, bg_ref, o_ref, ss_ref, st_ref):
        _zero_at_first([st_ref])
        s0 = st_ref[...]
        ss_ref[0] = s0
        o, s2 = _gdn_chunk(_heads(q_ref), _heads(k_ref), _heads(v_ref), bg_ref[...], s0)
        st_ref[...] = s2
        for h in range(C_HEADS):
            o_ref[:, h * C_DK:(h + 1) * C_DK] = o[h]

    col = lambda c: pl.BlockSpec((C_CHUNK, 1024), lambda i: (i, c))
    return pl.pallas_call(
        body, name="gdn_fwd", grid=(n_c,),
        in_specs=[col(0), col(1), col(2), _row(C_CHUNK, 128)],
        out_specs=[_row(C_CHUNK, 1024), pl.BlockSpec((1, C_HEADS, C_DK, C_DK), lambda i: (i, 0, 0, 0))],
        out_shape=[_sds(s_len, 1024), _sds(n_c, C_HEADS, C_DK, C_DK)],
        scratch_shapes=[pltpu.VMEM((C_HEADS, C_DK, C_DK), F32)],
        compiler_params=_cp("arbitrary"),
    )(qkv, qkv, qkv, bg)


def _gdn_bwd(qkv, bg, states, do):
    s_len = qkv.shape[0]
    n_c = s_len // C_CHUNK

    def body(q_ref, k_ref, v_ref, bg_ref, ss_ref, do_ref, dq_ref, dk_ref, dv_ref, dbg_ref, ds_ref):
        _zero_at_first([ds_ref])
        _, vjp = jax.vjp(_gdn_chunk, _heads(q_ref), _heads(k_ref), _heads(v_ref), bg_ref[...], ss_ref[0])
        dq, dk, dv, dbg, ds = vjp((_heads(do_ref), ds_ref[...]))
        ds_ref[...] = ds
        dbg_ref[...] = dbg
        for h in range(C_HEADS):
            sl = slice(h * C_DK, (h + 1) * C_DK)
            dq_ref[:, sl], dk_ref[:, sl], dv_ref[:, sl] = dq[h], dk[h], dv[h]

    rev = lambda i: n_c - 1 - i
    col = lambda c: pl.BlockSpec((C_CHUNK, 1024), lambda i: (rev(i), c))
    row128 = pl.BlockSpec((C_CHUNK, 128), lambda i: (rev(i), 0))
    return pl.pallas_call(
        body, name="gdn_bwd", grid=(n_c,),
        in_specs=[col(0), col(1), col(2), row128, pl.BlockSpec((1, C_HEADS, C_DK, C_DK), lambda i: (rev(i), 0, 0, 0)), col(0)],
        out_specs=[col(0), col(0), col(0), row128],
        out_shape=[_sds(s_len, 1024)] * 3 + [_sds(s_len, 128)],
        scratch_shapes=[pltpu.VMEM((C_HEADS, C_DK, C_DK), F32)],
        compiler_params=_cp("arbitrary"),
    )(qkv, qkv, qkv, bg, states, do)


def _head_norm_gate(o, gate, norm_g):
    return (_rms(o) * norm_g) * jax.nn.silu(gate)


def _l1_out_fb(o, gate_c, x1, target, norm_g, w_out, post_g, gate):
    s_len = x1.shape[0]

    def body(o_ref, gc_ref, x1_ref, t_ref, ng_ref, w_ref, pg_ref, gt_ref,
             loss_ref, dres_ref, do_ref, dgc_ref, dw_ref, dng_ref, dpg_ref, dgt_ref):
        _zero_at_first([loss_ref, dw_ref, dng_ref, dpg_ref, dgt_ref])
        ng = ng_ref[...]
        ons, vjps = [], []
        for h in range(C_HEADS):
            sl = slice(h * C_DK, (h + 1) * C_DK)
            on, vjp_h = jax.vjp(_head_norm_gate, o_ref[:, sl], gc_ref[:, sl], ng)
            ons.append(on)
            vjps.append(vjp_h)
        on_all = jnp.concatenate(ons, axis=-1)
        y = _bdot(on_all, w_ref[...])
        x2, vjp2 = jax.vjp(_post_res, y, x1_ref[...], pg_ref[...], gt_ref[...])
        err = x2 - t_ref[...]
        _acc(loss_ref, jnp.full((1, 128), 0.5 * jnp.sum(jnp.mean(err * err, axis=-1)), F32))
        dx2 = err * (1.0 / D_MODEL)
        dy, _, dpg, dgt = vjp2(dx2)
        dres_ref[...] = dx2
        _acc(dpg_ref, dpg)
        _acc(dgt_ref, dgt)
        dw_ref[...] += _bdot_tn(on_all, dy)
        don = _bdot_nt(dy, w_ref[...])
        for h in range(C_HEADS):
            sl = slice(h * C_DK, (h + 1) * C_DK)
            do_h, dgc_h, dng = vjps[h](don[:, sl])
            do_ref[:, sl] = do_h
            dgc_ref[:, sl] = dgc_h
            _acc(dng_ref, dng)

    vec, r10 = _fix((1, D_MODEL)), _row(TM, D_MODEL)
    row128 = _fix((1, 128))
    return pl.pallas_call(
        body, name="l1_out_fb", grid=(s_len // TM,),
        in_specs=[r10, r10, r10, r10, row128, _fix((D_MODEL, D_MODEL)), vec, vec],
        out_specs=[row128, r10, r10, r10, _fix((D_MODEL, D_MODEL)), row128, vec, vec],
        out_shape=[_sds(1, 128), _sds(s_len, D_MODEL), _sds(s_len, D_MODEL), _sds(s_len, D_MODEL),
                   _sds(D_MODEL, D_MODEL), _sds(1, 128), _sds(1, D_MODEL), _sds(1, D_MODEL)],
        compiler_params=_cp("arbitrary"),
    )(o, gate_c, x1, target, norm_g, w_out, post_g, gate)


def _row_of(v, width, at):
    return jnp.zeros((1, width), F32).at[0, at:at + v.shape[-1]].set(v.reshape(-1))


def _local_step(x, target, mod, wd):
    s_len = x.shape[0]
    shift0, scale0, gate0 = (mod[0:1, i * 1024:(i + 1) * 1024] for i in range(3))
    shift1, scale1, gate1 = (mod[1:2, i * 1024:(i + 1) * 1024] for i in range(3))
    pre_g0, pre_g1 = wd["pre_g"][0:1], wd["pre_g"][1:2]
    post_g0, post_g1 = wd["post_g"][0:1], wd["post_g"][1:2]
    w_in0 = wd["ab_w_in"].astype(BF)
    w_out0 = wd["ab_w_out"].astype(BF)
    glu_w = wd["s5_glu_w"].astype(BF)
    w_in1 = jnp.concatenate([wd["gdn_w_in"], jnp.zeros((D_MODEL, C_IN_PAD - wd["gdn_w_in"].shape[1]), F32)], axis=1).astype(BF)
    w_out1 = wd["gdn_w_out"].astype(BF)
    d_skip, glu_b = wd["s5_d"].reshape(1, 512), wd["s5_glu_b"].reshape(1, 512)
    norm_g = wd["gdn_norm_g"].reshape(1, 128)
    alog_row = _row_of(wd["gdn_a_log"], 128, C_HEADS)
    dtb_row = _row_of(wd["gdn_dt_bias"], 128, C_HEADS)
    conv_w = wd["gdn_conv"]

    a_re, a_im = wd["s5_a_re"], wd["s5_a_im"]
    log_dt = wd["s5_log_dt"].reshape(B_GROUPS, 1)
    bt_re = wd["s5_b_re"].transpose(0, 2, 1).reshape(B_WIDTH, B_STATE)
    bt_im = wd["s5_b_im"].transpose(0, 2, 1).reshape(B_WIDTH, B_STATE)
    abar_r, abar_i, bbar_r, bbar_i = _s5_params(a_re, a_im, log_dt, bt_re, bt_im)
    abr, abi = abar_r.reshape(1, -1), abar_i.reshape(1, -1)
    btr, bti = _blockdiag_b(bbar_r).astype(BF), _blockdiag_b(bbar_i).astype(BF)
    ctr, cti = _blockdiag_c(wd["s5_c_re"]).astype(BF), _blockdiag_c(wd["s5_c_im"]).astype(BF)

    table = _bucket_table()
    bias3 = jnp.transpose(wd["rel_bias"][table], (0, 3, 1, 2))
    q, k, v, u, ga, gb, h0 = _l0_front(x, pre_g0, scale0, shift0, w_in0)
    q3, k3, v3 = _perm3(q), _perm3(k), _perm3(v)
    o3p, l3p = _attn_fwd(q3, k3, v3, bias3)
    o3, l3 = _unperm3(o3p), _unperm3(l3p)
    xr, xi, ypre = _s5_scan_fwd(u, btr, bti, ctr, cti, abr, abi)
    x1, y0 = _l0_out(o3, l3, ga, gb, ypre, u, x, d_skip, glu_w, glu_b, w_out0, post_g0, gate0)

    raw, gate_c, ba, h1 = _l1_front(x1, pre_g1, scale1, shift1, w_in1)
    qkv, bg = _gdn_prep(raw, ba, conv_w, alog_row, dtb_row)
    o_gdn, states = _gdn_fwd(qkv, bg)
    loss_row, dres1, do_gdn, dgate_c, dw_out1, dnorm_g, dpost_g1, dgate1 = _l1_out_fb(
        o_gdn, gate_c, x1, target, norm_g, w_out1, post_g1, gate1)

    dq1, dk1, dv1, dbg = _gdn_bwd(qkv, bg, states, do_gdn)
    draw, dba, dconv_w, dalog_row, ddtb_row = _gdn_prep_bwd(raw, ba, conv_w, alog_row, dtb_row, dq1, dk1, dv1, dbg)
    dz1, dx1, dpre_g1, dscale1, dshift1 = _front_bwd(
        "l1_front_bwd", x1, pre_g1, scale1, shift1, w_in1, dres1, [[draw], [dgate_c], [dba]], [QKV, 1024, 128])
    dw_in1 = _matmul_tn("l1_dw_in", h1, dz1, 1408)

    (do3, dl3, dga, dgb, dypre, du_skip, dd_skip, dglu_w, dglu_b, dw_out0, dpost_g0, dgate0) = _l0_out_bwd(
        o3, l3, ga, gb, ypre, u, x, y0, d_skip, glu_w, glu_b, w_out0, post_g0, gate0, dx1)
    du_scan, dbtr, dbti, dctr, dcti, dabr, dabi = _s5_scan_bwd(dypre, xr, xi, u, btr, bti, ctr, cti, abr, abi)
    dq3p, dka, dkb, dva, dvb, dbias3 = _attn_bwd(q3, k3, v3, bias3, o3p, l3p, _perm3_each(do3), _perm3_each(dl3))
    up = lambda a3: jnp.concatenate([a3[:, A_BLOCK:], jnp.zeros((3, A_BLOCK, A_WIDTH), F32)], axis=1)
    dq3, dka3, dkb3, dva3, dvb3 = (_unperm3(a) for a in (dq3p, dka, up(dkb), dva, up(dvb)))
    parts = [[dq3[0], dq3[1], dq3[2]],
             [dka3[0], dka3[1], dka3[2], dkb3[0], dkb3[1], dkb3[2]],
             [dva3[0], dva3[1], dva3[2], dvb3[0], dvb3[1], dvb3[2]],
             [du_skip, du_scan], [dga], [dgb]]
    dz0, grad_x, dpre_g0, dscale0, dshift0 = _front_bwd(
        "l0_front_bwd", x, pre_g0, scale0, shift0, w_in0, dx1, parts, [512] * 6)
    dw_in0 = _matmul_tn("l0_dw_in", h0, dz0, 768)

    idx_rows = jnp.asarray(table.reshape(3, -1), F32)
    drel = _rel_bias_grad(dbias3, idx_rows).T
    da_re, da_im, dlog_dt, dbt_re, dbt_im = _s5_params_bwd(
        a_re, a_im, log_dt, bt_re, bt_im, dabr.reshape(B_GROUPS, B_STATE), dabi.reshape(B_GROUPS, B_STATE),
        _blockdiag_b_t(dbtr), _blockdiag_b_t(dbti))
    unb = lambda d: d.reshape(B_GROUPS, B_GROUP, B_STATE).transpose(0, 2, 1)
    grads = {
        "pre_g": jnp.concatenate([dpre_g0, dpre_g1], 0), "post_g": jnp.concatenate([dpost_g0, dpost_g1], 0),
        "rel_bias": drel, "ab_w_in": dw_in0, "ab_w_out": dw_out0,
        "s5_a_re": da_re, "s5_a_im": da_im, "s5_log_dt": dlog_dt.reshape(B_GROUPS),
        "s5_b_re": unb(dbt_re), "s5_b_im": unb(dbt_im),
        "s5_c_re": _blockdiag_c_t(dctr), "s5_c_im": _blockdiag_c_t(dcti),
        "s5_d": dd_skip.reshape(512), "s5_glu_w": dglu_w, "s5_glu_b": dglu_b.reshape(512),
        "gdn_w_in": dw_in1[:, :wd["gdn_w_in"].shape[1]], "gdn_conv": dconv_w,
        "gdn_a_log": dalog_row[0, C_HEADS:2 * C_HEADS], "gdn_dt_bias": ddtb_row[0, C_HEADS:2 * C_HEADS],
        "gdn_norm_g": dnorm_g.reshape(128), "gdn_w_out": dw_out1,
    }
    dmod = jnp.concatenate([jnp.concatenate([dshift0, dscale0, dgate0], 1), jnp.concatenate([dshift1, dscale1, dgate1], 1)], 0)
    return loss_row[0, 0], grad_x, grads, dmod


def _perm3_each(a3):
    return jnp.stack([_perm(a3[i], d) for i, d in enumerate(DILATIONS)], 0)


def _place():
    return lax.axis_index("x"), lax.axis_index("y"), lax.axis_index("c")


def _flip(v, bit):
    return 1 - v if bit else v


def _hbm_call(name, body, arrs, out_shapes, n_sem, n_local):
    any_spec = pl.BlockSpec(memory_space=pl.ANY)
    return pl.pallas_call(
        body, name=name,
        in_specs=[any_spec] * len(arrs), out_specs=[any_spec] * len(out_shapes), out_shape=out_shapes,
        scratch_shapes=[pltpu.SemaphoreType.DMA((n_sem,)), pltpu.SemaphoreType.DMA((n_sem,)), pltpu.SemaphoreType.DMA((max(n_local, 1),))],
    )(*arrs)


def _all_gather8(name, arr):
    def body(x_ref, out_ref, send_sems, recv_sems, local_sems):
        x, y, c = _place()
        me = 4 * x + 2 * y + c
        local = pltpu.make_async_copy(x_ref, out_ref.at[me], local_sems.at[0])
        local.start()
        sends, recvs = [], []
        for m in range(1, 8):
            peer = (_flip(x, m & 4), _flip(y, m & 2), _flip(c, m & 1))
            sends.append(pltpu.make_async_remote_copy(x_ref, out_ref.at[me], send_sems.at[m - 1], recv_sems.at[m - 1],
                                                      device_id=peer, device_id_type=MESH))
            recvs.append(pltpu.make_async_remote_copy(x_ref, out_ref.at[4 * peer[0] + 2 * peer[1] + peer[2]], send_sems.at[m - 1],
                                                      recv_sems.at[m - 1], device_id=peer, device_id_type=MESH))
        for cp in sends:
            cp.start()
        for cp in recvs:
            cp.wait_recv()
        for cp in sends:
            cp.wait_send()
        local.wait()

    return _hbm_call(name, body, [arr], [jax.ShapeDtypeStruct((8,) + arr.shape, arr.dtype)], 7, 1)[0]


def _chip_exchange(name, arrs, scatter):
    n = len(arrs)

    def body(*refs):
        ins, outs = refs[:n], refs[n:2 * n]
        send_sems, recv_sems, local_sems = refs[2 * n:]
        x, y, c = _place()
        mine = 2 * x + y
        locals_, sends, recvs = [], [], []
        for a in range(n):
            src = ins[a].at[mine] if scatter else ins[a]
            locals_.append(pltpu.make_async_copy(src, outs[a].at[mine], local_sems.at[a]))
            for m in range(1, 4):
                px, py = _flip(x, m & 2), _flip(y, m & 1)
                k = 3 * a + m - 1
                src = ins[a].at[2 * px + py] if scatter else ins[a]
                sends.append(pltpu.make_async_remote_copy(src, outs[a].at[mine], send_sems.at[k], recv_sems.at[k],
                                                          device_id=(px, py, c), device_id_type=MESH))
                recvs.append(pltpu.make_async_remote_copy(src, outs[a].at[2 * px + py], send_sems.at[k], recv_sems.at[k],
                                                          device_id=(px, py, c), device_id_type=MESH))
        for cp in locals_ + sends:
            cp.start()
        for cp in recvs:
            cp.wait_recv()
        for cp in sends:
            cp.wait_send()
        for cp in locals_:
            cp.wait()

    shapes = [jax.ShapeDtypeStruct(a.shape if scatter else (4,) + a.shape, a.dtype) for a in arrs]
    return _hbm_call(name, body, arrs, shapes, 3 * n, n)


def _sibling_exchange(name, arrs, keep):
    n = len(arrs)

    def body(*refs):
        ins, outs = refs[:n], refs[n:2 * n]
        send_sems, recv_sems, local_sems = refs[2 * n:]
        x, y, c = _place()
        sib = (x, y, 1 - c)
        locals_, sends, recvs = [], [], []
        for a in range(n):
            if keep:
                locals_.append(pltpu.make_async_copy(ins[a], outs[a].at[c], local_sems.at[a]))
            sends.append(pltpu.make_async_remote_copy(ins[a], outs[a].at[c] if keep else outs[a], send_sems.at[a], recv_sems.at[a],
                                                      device_id=sib, device_id_type=MESH))
            recvs.append(pltpu.make_async_remote_copy(ins[a], outs[a].at[1 - c] if keep else outs[a], send_sems.at[a], recv_sems.at[a],
                                                      device_id=sib, device_id_type=MESH))
        for cp in locals_ + sends:
            cp.start()
        for cp in recvs:
            cp.wait_recv()
        for cp in sends:
            cp.wait_send()
        for cp in locals_:
            cp.wait()

    shapes = [jax.ShapeDtypeStruct((2,) + a.shape if keep else a.shape, a.dtype) for a in arrs]
    return _hbm_call(name, body, arrs, shapes, n, n if keep else 0)


def _row_tile(rows):
    for t in (256, 128, 64, 32, 16, 8):
        if rows % t == 0:
            return t
    return rows


def _ordered_sum(name, arrs):
    rows, cols = arrs[0].shape
    tr = _row_tile(rows)

    def body(*refs):
        acc = refs[0][...]
        for r in refs[1:-1]:
            acc = acc + r[...]
        refs[-1][...] = acc

    return pl.pallas_call(body, name=name, grid=(rows // tr,), in_specs=[_row(tr, cols)] * len(arrs), out_specs=_row(tr, cols),
                          out_shape=_sds(rows, cols), compiler_params=_cp("arbitrary"))(*arrs)


def _slot_sum(name, arr):
    n, rows, cols = arr.shape
    tr = _row_tile(rows)

    def body(*refs):
        acc = refs[0][0]
        for r in refs[1:-1]:
            acc = acc + r[0]
        refs[-1][...] = acc

    specs = [pl.BlockSpec((1, tr, cols), functools.partial(lambda s, i: (s, i, 0), s)) for s in range(n)]
    return pl.pallas_call(body, name=name, grid=(rows // tr,), in_specs=specs, out_specs=_row(tr, cols),
                          out_shape=_sds(rows, cols), compiler_params=_cp("arbitrary"))(*([arr] * n))


def _adamw(name, w, g, m, v):
    rows, cols = w.shape
    tr = _row_tile(rows)

    def body(w_ref, g_ref, m_ref, v_ref, d_ref, nm_ref, nv_ref):
        g_ = g_ref[...]
        m_ = ADAM_B1 * m_ref[...] + (1.0 - ADAM_B1) * g_
        v_ = ADAM_B2 * v_ref[...] + (1.0 - ADAM_B2) * (g_ * g_)
        m_hat = m_ / (1.0 - ADAM_B1 ** ADAM_STEP)
        v_hat = v_ / (1.0 - ADAM_B2 ** ADAM_STEP)
        d_ref[...] = -ADAM_LR * (m_hat / (jnp.sqrt(v_hat) + ADAM_EPS) + ADAM_WD * w_ref[...])
        nm_ref[...] = m_
        nv_ref[...] = v_

    spec = _row(tr, cols)
    return pl.pallas_call(body, name=name, grid=(rows // tr,), in_specs=[spec] * 4, out_specs=[spec] * 3,
                          out_shape=[_sds(rows, cols)] * 3, compiler_params=_cp("arbitrary"))(w, g, m, v)


def _mod_local(c_all, ada_w):
    def body(c_ref, w_ref, o_ref):
        c_act = jax.nn.silu(c_ref[...])
        for l in range(2):
            o_ref[l] = _hdot(c_act, w_ref[l])

    return pl.pallas_call(body, name="mod_local", out_shape=_sds(2, 8, ada_w.shape[2]),
                          compiler_params=pltpu.CompilerParams(vmem_limit_bytes=VMEM_LIMIT_BYTES))(c_all, ada_w)


def _ada_w_grad(c_all, dmod_cols):
    def body(c_ref, d_ref, o_ref):
        c_act = jax.nn.silu(c_ref[...])
        for l in range(2):
            o_ref[l] = lax.dot_general(c_act, d_ref[l], (((0,), (0,)), ((), ())), precision=HI, preferred_element_type=F32)

    return pl.pallas_call(body, name="ada_w_grad", out_shape=_sds(2, D_MODEL, dmod_cols.shape[2]),
                          compiler_params=pltpu.CompilerParams(vmem_limit_bytes=VMEM_LIMIT_BYTES))(c_all, dmod_cols)


_SMALL = ("ada_b", "pre_g", "post_g", "rel_bias", "s5_a_re", "s5_a_im", "s5_log_dt", "s5_b_re", "s5_b_im", "s5_c_re", "s5_c_im",
          "s5_d", "s5_glu_b", "gdn_a_log", "gdn_dt_bias", "gdn_norm_g")
_SHARDED = ("ab_w_in", "ab_w_out", "s5_glu_w", "gdn_w_in", "gdn_w_out")
_COL_SHARDED = ("ab_w_in", "gdn_w_in")
_WEIGHTS = ("ada_w", "ada_b", "pre_g", "post_g", "rel_bias", "ab_w_in", "ab_w_out", "s5_a_re", "s5_a_im", "s5_log_dt", "s5_b_re",
            "s5_b_im", "s5_c_re", "s5_c_im", "s5_d", "s5_glu_w", "s5_glu_b", "gdn_w_in", "gdn_conv", "gdn_a_log", "gdn_dt_bias",
            "gdn_norm_g", "gdn_w_out")


def _rows128(n):
    return -(-n // 128)


def _pack(arrs, total_rows):
    pieces = []
    for a in arrs:
        flat = a.reshape(-1)
        pieces.append(jnp.pad(flat, (0, _rows128(flat.shape[0]) * 128 - flat.shape[0])).reshape(-1, 128))
    used = sum(p.shape[0] for p in pieces)
    pieces.append(jnp.zeros((total_rows - used, 128), F32))
    return jnp.concatenate(pieces, axis=0)


def _unpack(buf, shapes):
    out, at = [], 0
    for shp in shapes:
        n = int(np.prod(shp))
        out.append(buf[at:at + _rows128(n)].reshape(-1)[:n].reshape(shp))
        at += _rows128(n)
    return out


def _full_from_halves(name, g):
    if name in _COL_SHARDED:
        return g.transpose(0, 2, 1, 3).reshape(2 * g.shape[2], 4 * g.shape[3])
    return g.transpose(1, 0, 2, 3).reshape(8 * g.shape[2], g.shape[3])


def _shard_major(name, g):
    if name in _COL_SHARDED:
        return g.reshape(g.shape[0], 4, g.shape[1] // 4).transpose(1, 0, 2)
    return g.reshape(4, g.shape[0] // 4, g.shape[1])


def kernel(x, c, ada_w, ada_b, pre_g, post_g, rel_bias, ab_w_in, ab_w_out, s5_a_re, s5_a_im, s5_log_dt, s5_b_re, s5_b_im, s5_c_re, s5_c_im, s5_d, s5_glu_w, s5_glu_b, gdn_w_in, gdn_conv, gdn_a_log, gdn_dt_bias, gdn_norm_g, gdn_w_out, loss_target, m_ada_w, m_ada_b, m_pre_g, m_post_g, m_rel_bias, m_ab_w_in, m_ab_w_out, m_s5_a_re, m_s5_a_im, m_s5_log_dt, m_s5_b_re, m_s5_b_im, m_s5_c_re, m_s5_c_im, m_s5_d, m_s5_glu_w, m_s5_glu_b, m_gdn_w_in, m_gdn_conv, m_gdn_a_log, m_gdn_dt_bias, m_gdn_norm_g, m_gdn_w_out, v_ada_w, v_ada_b, v_pre_g, v_post_g, v_rel_bias, v_ab_w_in, v_ab_w_out, v_s5_a_re, v_s5_a_im, v_s5_log_dt, v_s5_b_re, v_s5_b_im, v_s5_c_re, v_s5_c_im, v_s5_d, v_s5_glu_w, v_s5_glu_b, v_gdn_w_in, v_gdn_conv, v_gdn_a_log, v_gdn_dt_bias, v_gdn_norm_g, v_gdn_w_out):
    w = dict(ada_w=ada_w, ada_b=ada_b, pre_g=pre_g, post_g=post_g, rel_bias=rel_bias, ab_w_in=ab_w_in, ab_w_out=ab_w_out,
             s5_a_re=s5_a_re, s5_a_im=s5_a_im, s5_log_dt=s5_log_dt, s5_b_re=s5_b_re, s5_b_im=s5_b_im, s5_c_re=s5_c_re, s5_c_im=s5_c_im,
             s5_d=s5_d, s5_glu_w=s5_glu_w, s5_glu_b=s5_glu_b, gdn_w_in=gdn_w_in, gdn_conv=gdn_conv, gdn_a_log=gdn_a_log,
             gdn_dt_bias=gdn_dt_bias, gdn_norm_g=gdn_norm_g, gdn_w_out=gdn_w_out)
    m = dict(ada_w=m_ada_w, ada_b=m_ada_b, pre_g=m_pre_g, post_g=m_post_g, rel_bias=m_rel_bias, ab_w_in=m_ab_w_in, ab_w_out=m_ab_w_out,
             s5_a_re=m_s5_a_re, s5_a_im=m_s5_a_im, s5_log_dt=m_s5_log_dt, s5_b_re=m_s5_b_re, s5_b_im=m_s5_b_im, s5_c_re=m_s5_c_re,
             s5_c_im=m_s5_c_im, s5_d=m_s5_d, s5_glu_w=m_s5_glu_w, s5_glu_b=m_s5_glu_b, gdn_w_in=m_gdn_w_in, gdn_conv=m_gdn_conv,
             gdn_a_log=m_gdn_a_log, gdn_dt_bias=m_gdn_dt_bias, gdn_norm_g=m_gdn_norm_g, gdn_w_out=m_gdn_w_out)
    v = dict(ada_w=v_ada_w, ada_b=v_ada_b, pre_g=v_pre_g, post_g=v_post_g, rel_bias=v_rel_bias, ab_w_in=v_ab_w_in, ab_w_out=v_ab_w_out,
             s5_a_re=v_s5_a_re, s5_a_im=v_s5_a_im, s5_log_dt=v_s5_log_dt, s5_b_re=v_s5_b_re, s5_b_im=v_s5_b_im, s5_c_re=v_s5_c_re,
             s5_c_im=v_s5_c_im, s5_d=v_s5_d, s5_glu_w=v_s5_glu_w, s5_glu_b=v_s5_glu_b, gdn_w_in=v_gdn_w_in, gdn_conv=v_gdn_conv,
             gdn_a_log=v_gdn_a_log, gdn_dt_bias=v_gdn_dt_bias, gdn_norm_g=v_gdn_norm_g, gdn_w_out=v_gdn_w_out)
    ix, iy, ic = _place()
    me = 4 * ix + 2 * iy + ic
    chip = 2 * ix + iy
    n_cols = ada_w.shape[2]

    first = _all_gather8("gather_c_conv", _pack([c, gdn_conv], 32))
    c_all = first[:, 0:8].reshape(8, D_MODEL)
    conv_full = first[0::2, 8:32].reshape(4, C_CONV, n_cols).transpose(1, 0, 2).reshape(C_CONV, 4 * n_cols)
    modl = _all_gather8("gather_mod", _mod_local(c_all, ada_w))
    mod = lax.dynamic_index_in_dim(modl[0::2], me, axis=2, keepdims=False)
    mod = mod.transpose(1, 0, 2).reshape(2, 4 * n_cols) + ada_b

    halves = []
    for name in _SHARDED:
        shard = w[name][0].astype(BF)
        h = shard.shape[0] // 2
        halves.append(lax.dynamic_slice_in_dim(shard, ic * h, h, axis=0))
    gathered = _sibling_exchange("gather_w_sibling", _chip_exchange("gather_w_chips", halves, False), True)
    wd = {name: w[name] for name in _SMALL if name != "ada_b"}
    wd = {k: (a if k in ("pre_g", "post_g", "rel_bias") else a[0]) for k, a in wd.items()}
    wd["gdn_conv"] = conv_full
    for name, g in zip(_SHARDED, gathered):
        wd[name] = _full_from_halves(name, g)

    loss_local, grad_x, grads, dmod = _local_step(x[0], loss_target[0], mod, wd)
    loss = lax.psum(loss_local, ("x", "y", "c"))

    small_shapes = [w[name].shape for name in _SMALL] + [(C_CONV, 4 * n_cols)]
    small_rows = -(-sum(_rows128(int(np.prod(s))) for s in small_shapes) // 8) * 8
    partial = _pack([dmod] + [grads[name] for name in _SMALL[1:]] + [grads["gdn_conv"]], small_rows)
    every = _all_gather8("gather_small_grads", partial)
    g_small = _slot_sum("sum_small_grads", every)
    g_list = _unpack(g_small, small_shapes)
    d_small, m_small, v_small = _adamw("adamw_small", _pack([w[n] for n in _SMALL], small_rows), g_small,
                                       _pack([m[n] for n in _SMALL], small_rows), _pack([v[n] for n in _SMALL], small_rows))
    out_g = dict(zip(_SMALL, g_list[:-1]))
    out_d = dict(zip(_SMALL, _unpack(d_small, small_shapes[:-1])))
    out_m = dict(zip(_SMALL, _unpack(m_small, small_shapes[:-1])))
    out_v = dict(zip(_SMALL, _unpack(v_small, small_shapes[:-1])))

    def update(name, g2d):
        shp = w[name].shape
        two_d = lambda a: a.reshape(-1, shp[-1])
        d_, m_, v_ = _adamw("adamw_" + name, two_d(w[name]), g2d, two_d(m[name]), two_d(v[name]))
        out_g[name], out_d[name], out_m[name], out_v[name] = (a.reshape(shp) for a in (g2d, d_, m_, v_))

    update("gdn_conv", lax.dynamic_slice_in_dim(g_list[-1], chip * n_cols, n_cols, axis=1))

    dmod_all = every[:, 0:_rows128(2 * 3 * D_MODEL)].reshape(8, 2, 4, n_cols)
    dmod_cols = lax.dynamic_index_in_dim(dmod_all, chip, axis=2, keepdims=False).transpose(1, 0, 2)
    update("ada_w", _ada_w_grad(c_all, dmod_cols).reshape(-1, n_cols))

    mine, other = [], []
    for name in _SHARDED:
        sm = _shard_major(name, grads[name])
        h = sm.shape[1] // 2
        mine.append(lax.dynamic_slice_in_dim(sm, ic * h, h, axis=1))
        other.append(lax.dynamic_slice_in_dim(sm, (1 - ic) * h, h, axis=1))
    from_sibling = _sibling_exchange("reduce_sibling", other, False)
    chip_partials = []
    for name, a, b in zip(_SHARDED, mine, from_sibling):
        flat = lambda t: t.reshape(-1, t.shape[-1])
        chip_partials.append(_ordered_sum("sum_sibling_" + name, [flat(a), flat(b)]).reshape(a.shape))
    from_chips = _chip_exchange("reduce_chips", chip_partials, True)
    reduced = [_slot_sum("sum_chips_" + name, t) for name, t in zip(_SHARDED, from_chips)]
    for name, both in zip(_SHARDED, _sibling_exchange("reduce_share", reduced, True)):
        update(name, both.reshape(-1, both.shape[-1]))

    return (loss, grad_x[None], *[out_g[n] for n in _WEIGHTS], *[out_d[n] for n in _WEIGHTS],
            *[out_m[n] for n in _WEIGHTS], *[out_v[n] for n in _WEIGHTS])
```

```python
import functools
import math

import numpy as np
import jax
import jax.numpy as jnp
from jax import lax
from jax.experimental import pallas as pl
from jax.experimental.pallas import tpu as pltpu

F32 = jnp.float32
BF = jnp.bfloat16
HI = lax.Precision.HIGHEST
MESH = pl.DeviceIdType.MESH

D_MODEL = 1024
EPS = 1e-6
A_HEADS, A_HD, A_WIDTH, A_BLOCK = 8, 64, 512, 128
DILATIONS = (1, 4, 16)
N_KEYS = 128
REL_BUCKETS, REL_MAX_DIST = 32, 2048
B_WIDTH, B_GROUP, B_GROUPS, B_STATE = 512, 16, 32, 64
S5_LANES = 512
S5_TILES = 4
S5_T = 256
C_HEADS, C_DK, C_CHUNK, C_CONV = 8, 128, 64, 4
QKV = 3072
C_IN_PAD = 4224
TM = 256
VMEM_LIMIT_BYTES = 56 * 1024 * 1024
ADAM_LR, ADAM_B1, ADAM_B2, ADAM_EPS, ADAM_WD, ADAM_STEP = 0.001, 0.9, 0.999, 1e-08, 0.01, 10
NEG = float(np.finfo(np.float32).min)


def _cp(*sem):
    return pltpu.CompilerParams(dimension_semantics=sem, vmem_limit_bytes=VMEM_LIMIT_BYTES)


def _bdot(a, b):
    return jnp.dot(a.astype(BF), b.astype(BF), preferred_element_type=F32)


def _bdot_nt(a, b):
    return lax.dot_general(a.astype(BF), b.astype(BF), (((1,), (1,)), ((), ())), preferred_element_type=F32)


def _bdot_tn(a, b):
    return lax.dot_general(a.astype(BF), b.astype(BF), (((0,), (0,)), ((), ())), preferred_element_type=F32)


def _hdot(a, b):
    return jnp.dot(a, b, precision=HI, preferred_element_type=F32)


def _bein(eq, a, b):
    return jnp.einsum(eq, a.astype(BF), b.astype(BF), preferred_element_type=F32)


def _hein(eq, a, b):
    return jnp.einsum(eq, a, b, precision=HI, preferred_element_type=F32)


def _row(tm, n):
    return pl.BlockSpec((tm, n), lambda i: (i, 0))


def _fix(shape):
    return pl.BlockSpec(shape, lambda i: (0,) * len(shape))


def _sds(*shape, dtype=F32):
    return jax.ShapeDtypeStruct(shape, dtype)


def _acc(ref, val):
    ref[...] += val


def _zero_at_first(refs, axis=0):
    @pl.when(pl.program_id(axis) == 0)
    def _():
        for r in refs:
            r[...] = jnp.zeros_like(r)


def _rms(x):
    return x * lax.rsqrt(jnp.mean(x * x, axis=-1, keepdims=True) + EPS)


def _pre_mod(x, g, scale, shift):
    return (_rms(x) * g) * (1.0 + scale) + shift


def _post_res(y, x, post_g, gate):
    return x + gate * (_rms(y) * post_g)


def _merge_gate(o1, o2, o3, l1, l2, l3, ga):
    m = jnp.maximum(jnp.maximum(l1, l2), l3)
    e1, e2, e3 = jnp.exp(l1 - m), jnp.exp(l2 - m), jnp.exp(l3 - m)
    inv = 1.0 / (e1 + e2 + e3)
    return ((e1 * inv) * o1 + (e2 * inv) * o2 + (e3 * inv) * o3) * jax.nn.silu(ga)


def _s5_gelu(ypre, u, d_skip):
    return jax.nn.gelu(ypre + d_skip * u)


def _s5_glu(yb, gl, gb):
    return yb * jax.nn.sigmoid(gl) * jax.nn.silu(gb)


def _l0_front(x, pre_g, scale, shift, w_in):
    s_len = x.shape[0]

    def body(x_ref, g_ref, sc_ref, sh_ref, w_ref, q_ref, k_ref, v_ref, u_ref, ga_ref, gb_ref, h_ref):
        hb = _pre_mod(x_ref[...], g_ref[...], sc_ref[...], sh_ref[...]).astype(BF)
        h_ref[...] = hb
        z = jnp.dot(hb, w_ref[...], preferred_element_type=F32)
        q_ref[...] = z[:, 0:512]
        k_ref[...] = z[:, 512:1024]
        v_ref[...] = z[:, 1024:1536]
        u_ref[...] = z[:, 1536:2048]
        ga_ref[...] = z[:, 2048:2560]
        gb_ref[...] = z[:, 2560:3072]

    vec = _fix((1, D_MODEL))
    return pl.pallas_call(
        body, name="l0_front", grid=(s_len // TM,),
        in_specs=[_row(TM, D_MODEL), vec, vec, vec, _fix((D_MODEL, 3072))],
        out_specs=[_row(TM, 512)] * 6 + [_row(TM, D_MODEL)],
        out_shape=[_sds(s_len, 512)] * 6 + [_sds(s_len, D_MODEL, dtype=BF)],
        compiler_params=_cp("arbitrary"),
    )(x, pre_g, scale, shift, w_in)


def _front_bwd(name, x, pre_g, scale, shift, w_in, dres, parts, widths):
    s_len = x.shape[0]
    n_in = sum(len(p) for p in parts)
    n_cols = sum(widths)

    def body(*refs):
        x_ref, g_ref, sc_ref, sh_ref, w_ref, dres_ref = refs[:6]
        part_refs = refs[6:6 + n_in]
        dz_ref, dx_ref, dg_ref, dsc_ref, dsh_ref = refs[6 + n_in:]
        _zero_at_first([dg_ref, dsc_ref, dsh_ref])
        _, vjp = jax.vjp(_pre_mod, x_ref[...], g_ref[...], sc_ref[...], sh_ref[...])
        dh = jnp.zeros((TM, D_MODEL), F32)
        col, at = 0, 0
        for grp, width in zip(parts, widths):
            dz = part_refs[at][...]
            for r in part_refs[at + 1:at + len(grp)]:
                dz = dz + r[...]
            at += len(grp)
            dzb = dz.astype(BF)
            dz_ref[:, col:col + width] = dzb
            dh = dh + lax.dot_general(dzb, w_ref[:, col:col + width], (((1,), (1,)), ((), ())), preferred_element_type=F32)
            col += width
        dx, dg, dsc, dsh = vjp(dh)
        dx_ref[...] = dx + dres_ref[...]
        _acc(dg_ref, dg)
        _acc(dsc_ref, dsc)
        _acc(dsh_ref, dsh)

    vec = _fix((1, D_MODEL))
    flat = [a for p in parts for a in p]
    return pl.pallas_call(
        body, name=name, grid=(s_len // TM,),
        in_specs=[_row(TM, D_MODEL), vec, vec, vec, _fix((D_MODEL, n_cols)), _row(TM, D_MODEL)]
        + [_row(TM, a.shape[1]) for a in flat],
        out_specs=[_row(TM, n_cols), _row(TM, D_MODEL), vec, vec, vec],
        out_shape=[_sds(s_len, n_cols, dtype=BF), _sds(s_len, D_MODEL), _sds(1, D_MODEL), _sds(1, D_MODEL), _sds(1, D_MODEL)],
        compiler_params=_cp("arbitrary"),
    )(x, pre_g, scale, shift, w_in, dres, *flat)


def _matmul_tn(name, a, b, tn):
    s_len, k_dim = a.shape
    n_dim = b.shape[1]
    ts = 512

    def body(a_ref, b_ref, o_ref):
        _zero_at_first([o_ref], axis=1)
        o_ref[...] += lax.dot_general(a_ref[...], b_ref[...], (((0,), (0,)), ((), ())), preferred_element_type=F32)

    return pl.pallas_call(
        body, name=name, grid=(n_dim // tn, s_len // ts),
        in_specs=[pl.BlockSpec((ts, k_dim), lambda j, i: (i, 0)), pl.BlockSpec((ts, tn), lambda j, i: (i, j))],
        out_specs=pl.BlockSpec((k_dim, tn), lambda j, i: (0, j)),
        out_shape=_sds(k_dim, n_dim),
        compiler_params=_cp("arbitrary", "arbitrary"),
    )(a, b)


def _t5_bucket_np(dist):
    dist = np.maximum(dist, 0)
    max_exact = REL_BUCKETS // 2
    large = max_exact + (np.log(np.maximum(dist, 1) / max_exact)
                         / math.log(REL_MAX_DIST / max_exact) * (REL_BUCKETS - max_exact)).astype(np.int32)
    large = np.minimum(large, REL_BUCKETS - 1)
    return np.where(dist < max_exact, dist, large).astype(np.int32)


def _bucket_table():
    qi = np.arange(A_BLOCK)[:, None]
    kj = np.arange(2 * A_BLOCK)[None, :]
    rel = qi + A_BLOCK - kj
    return np.stack([_t5_bucket_np(rel * d) for d in DILATIONS], 0)


def _attn_mask(first):
    qi = lax.broadcasted_iota(jnp.int32, (A_BLOCK, 2 * A_BLOCK), 0)
    kj = lax.broadcasted_iota(jnp.int32, (A_BLOCK, 2 * A_BLOCK), 1)
    rel = qi + A_BLOCK - kj
    band = (rel >= 0) & (rel <= N_KEYS)
    return band & (jnp.logical_not(first) | (kj >= A_BLOCK))


def _attn_first(s_len):
    c, g = pl.program_id(0), pl.program_id(1)
    nbs = [s_len // A_BLOCK // d for d in DILATIONS]
    nb = jnp.where(c == 0, nbs[0], jnp.where(c == 1, nbs[1], nbs[2]))
    return (g % nb) == 0


def _attn_specs():
    cur = pl.BlockSpec((1, A_BLOCK, A_WIDTH), lambda c, g: (c, g, 0))
    prev = pl.BlockSpec((1, A_BLOCK, A_WIDTH), lambda c, g: (c, jnp.maximum(g - 1, 0), 0))
    bias = pl.BlockSpec((1, A_HEADS, A_BLOCK, 2 * A_BLOCK), lambda c, g: (c, 0, 0, 0))
    return cur, prev, bias


def _attn_fwd(q3, k3, v3, bias3):
    s_len = q3.shape[1]
    scale = A_HD ** -0.5

    def body(q_ref, kp_ref, kc_ref, vp_ref, vc_ref, b_ref, o_ref, l_ref):
        mask = _attn_mask(_attn_first(s_len))
        lane = lax.broadcasted_iota(jnp.int32, (1, 128), 1)
        for hp in range(A_HEADS // 2):
            sl = slice(hp * 128, (hp + 1) * 128)
            qp = q_ref[0, :, sl]
            kw = jnp.concatenate([kp_ref[0, :, sl], kc_ref[0, :, sl]], axis=0).astype(BF)
            vw = jnp.concatenate([vp_ref[0, :, sl], vc_ref[0, :, sl]], axis=0).astype(BF)
            outs, lses = [], []
            for j in range(2):
                hm = (lane < 64) if j == 0 else (lane >= 64)
                s = _bdot_nt(jnp.where(hm, qp, 0.0), kw) * scale
                s = jnp.where(mask, s + b_ref[0, 2 * hp + j], NEG)
                m = jnp.max(s, axis=-1, keepdims=True)
                p = jnp.exp(s - m)
                den = jnp.sum(p, axis=-1, keepdims=True)
                outs.append(_bdot(p, vw) / den)
                lses.append(m + jnp.log(den))
            hm0 = lane < 64
            o_ref[0, :, sl] = jnp.where(hm0, outs[0], outs[1])
            l_ref[0, :, sl] = jnp.where(hm0, lses[0], lses[1])

    cur, prev, bias = _attn_specs()
    return pl.pallas_call(
        body, name="attn_fwd", grid=(3, s_len // A_BLOCK),
        in_specs=[cur, prev, cur, prev, cur, bias],
        out_specs=[cur, cur],
        out_shape=[_sds(3, s_len, A_WIDTH)] * 2,
        compiler_params=_cp("arbitrary", "arbitrary"),
    )(q3, k3, k3, v3, v3, bias3)


def _attn_bwd(q3, k3, v3, bias3, o3, l3, do3, dl3):
    s_len = q3.shape[1]
    scale = A_HD ** -0.5

    def body(q_ref, kp_ref, kc_ref, vp_ref, vc_ref, b_ref, o_ref, l_ref, do_ref, dl_ref,
             dq_ref, dka_ref, dkb_ref, dva_ref, dvb_ref, db_ref):
        _zero_at_first([db_ref], axis=1)
        mask = _attn_mask(_attn_first(s_len))
        lane = lax.broadcasted_iota(jnp.int32, (1, 128), 1)
        for hp in range(A_HEADS // 2):
            sl = slice(hp * 128, (hp + 1) * 128)
            qp = q_ref[0, :, sl]
            kw = jnp.concatenate([kp_ref[0, :, sl], kc_ref[0, :, sl]], axis=0).astype(BF)
            vw = jnp.concatenate([vp_ref[0, :, sl], vc_ref[0, :, sl]], axis=0).astype(BF)
            op, lp, dop, dlp = o_ref[0, :, sl], l_ref[0, :, sl], do_ref[0, :, sl], dl_ref[0, :, sl]
            dq_acc = jnp.zeros((A_BLOCK, 128), F32)
            dk_acc = jnp.zeros((2 * A_BLOCK, 128), F32)
            dv_acc = jnp.zeros((2 * A_BLOCK, 128), F32)
            for j in range(2):
                hm = (lane < 64) if j == 0 else (lane >= 64)
                qm = jnp.where(hm, qp, 0.0)
                s = _bdot_nt(qm, kw) * scale
                s = jnp.where(mask, s + b_ref[0, 2 * hp + j], NEG)
                lse = jnp.max(jnp.where(hm, lp, NEG), axis=-1, keepdims=True)
                p = jnp.exp(s - lse)
                do_h = jnp.where(hm, dop, 0.0)
                dd = jnp.sum(do_h * op, axis=-1, keepdims=True)
                dl = jnp.sum(jnp.where(hm, dlp, 0.0), axis=-1, keepdims=True)
                ds = p * (_bdot_nt(do_h, vw) - dd + dl)
                dv_acc = dv_acc + _bdot_tn(p, do_h)
                dq_acc = dq_acc + jnp.where(hm, _bdot(ds, kw), 0.0) * scale
                dk_acc = dk_acc + _bdot_tn(ds, qm) * scale
                db_ref[0, 2 * hp + j] += ds
            dq_ref[0, :, sl] = dq_acc
            dkb_ref[0, :, sl] = dk_acc[:A_BLOCK]
            dka_ref[0, :, sl] = dk_acc[A_BLOCK:]
            dvb_ref[0, :, sl] = dv_acc[:A_BLOCK]
            dva_ref[0, :, sl] = dv_acc[A_BLOCK:]

    cur, prev, bias = _attn_specs()
    return pl.pallas_call(
        body, name="attn_bwd", grid=(3, s_len // A_BLOCK),
        in_specs=[cur, prev, cur, prev, cur, bias, cur, cur, cur, cur],
        out_specs=[cur] * 5 + [bias],
        out_shape=[_sds(3, s_len, A_WIDTH)] * 5 + [_sds(3, A_HEADS, A_BLOCK, 2 * A_BLOCK)],
        compiler_params=_cp("arbitrary", "arbitrary"),
    )(q3, k3, k3, v3, v3, bias3, o3, l3, do3, dl3)


def _attn_bias(rel_bias, table):
    def body(rb_ref, t_ref, o_ref):
        for c in range(3):
            t = t_ref[c]
            acc = [jnp.zeros((A_BLOCK, 2 * A_BLOCK), F32) for _ in range(A_HEADS)]
            for b in range(REL_BUCKETS):
                hit = t == b
                acc = [jnp.where(hit, rb_ref[b, h], acc[h]) for h in range(A_HEADS)]
            for h in range(A_HEADS):
                o_ref[c, h] = acc[h]

    return pl.pallas_call(body, name="attn_bias", out_shape=_sds(3, A_HEADS, A_BLOCK, 2 * A_BLOCK),
                          in_specs=[pl.BlockSpec(memory_space=pltpu.SMEM), pl.BlockSpec(memory_space=pltpu.VMEM)],
                          compiler_params=pltpu.CompilerParams(vmem_limit_bytes=VMEM_LIMIT_BYTES))(rel_bias, table)


def _rel_bias_grad(db3, idx_rows):
    n = A_BLOCK * 2 * A_BLOCK
    dbf = db3.reshape(3, A_HEADS, n)

    def body(db_ref, idx_ref, o_ref):
        bucket = lax.broadcasted_iota(jnp.int32, (REL_BUCKETS, n), 0).astype(F32)
        acc = jnp.zeros((A_HEADS, REL_BUCKETS), F32)
        for c in range(3):
            onehot = (idx_ref[c:c + 1, :] == bucket).astype(F32)
            acc = acc + lax.dot_general(db_ref[c], onehot, (((1,), (1,)), ((), ())), precision=HI, preferred_element_type=F32)
        o_ref[...] = acc

    return pl.pallas_call(body, name="rel_bias_grad", out_shape=_sds(A_HEADS, REL_BUCKETS),
                          compiler_params=pltpu.CompilerParams(vmem_limit_bytes=VMEM_LIMIT_BYTES))(dbf, idx_rows)


def _perm(a, dil):
    s_len, n = a.shape
    return a.reshape(s_len // dil, dil, n).transpose(1, 0, 2).reshape(s_len, n)


def _unperm(a, dil):
    s_len, n = a.shape
    return a.reshape(dil, s_len // dil, n).transpose(1, 0, 2).reshape(s_len, n)


def _perm3(a):
    return jnp.stack([_perm(a, d) for d in DILATIONS], 0)


def _unperm3(a3):
    return jnp.stack([_unperm(a3[i], d) for i, d in enumerate(DILATIONS)], 0)


def _s5_param_fn(a_re, a_im, log_dt, bt_re, bt_im):
    dt = jnp.exp(log_dt)
    mag = jnp.exp(dt * a_re)
    abar_r, abar_i = mag * jnp.cos(dt * a_im), mag * jnp.sin(dt * a_im)
    den = a_re * a_re + a_im * a_im
    fr = ((abar_r - 1.0) * a_re + abar_i * a_im) / den
    fi = (abar_i * a_re - (abar_r - 1.0) * a_im) / den
    row = lax.broadcasted_iota(jnp.int32, (B_WIDTH, B_GROUPS), 0)
    grp = lax.broadcasted_iota(jnp.int32, (B_WIDTH, B_GROUPS), 1)
    expand = ((row // B_GROUP) == grp).astype(F32)
    fr_e, fi_e = _hdot(expand, fr), _hdot(expand, fi)
    return abar_r, abar_i, fr_e * bt_re - fi_e * bt_im, fr_e * bt_im + fi_e * bt_re


def _s5_params(a_re, a_im, log_dt, bt_re, bt_im):
    def body(ar, ai, ld, br, bi, o1, o2, o3, o4):
        o1[...], o2[...], o3[...], o4[...] = _s5_param_fn(ar[...], ai[...], ld[...], br[...], bi[...])

    return pl.pallas_call(body, name="s5_params",
                          out_shape=[_sds(B_GROUPS, B_STATE)] * 2 + [_sds(B_WIDTH, B_STATE)] * 2)(a_re, a_im, log_dt, bt_re, bt_im)


def _s5_params_bwd(a_re, a_im, log_dt, bt_re, bt_im, d1, d2, d3, d4):
    def body(ar, ai, ld, br, bi, c1, c2, c3, c4, o1, o2, o3, o4, o5):
        _, vjp = jax.vjp(_s5_param_fn, ar[...], ai[...], ld[...], br[...], bi[...])
        o1[...], o2[...], o3[...], o4[...], o5[...] = vjp((c1[...], c2[...], c3[...], c4[...]))

    return pl.pallas_call(body, name="s5_params_bwd",
                          out_shape=[_sds(B_GROUPS, B_STATE)] * 2 + [_sds(B_GROUPS, 1)] + [_sds(B_WIDTH, B_STATE)] * 2,
                          )(a_re, a_im, log_dt, bt_re, bt_im, d1, d2, d3, d4)


def _cscan(br, bi, ar, ai, reverse):
    t_len = br.shape[0]
    rows = lax.broadcasted_iota(jnp.int32, br.shape, 0)
    xr, xi, cr, ci = br, bi, ar, ai
    k = 1
    while k < t_len:
        if reverse:
            keep, shift = rows < t_len - k, t_len - k
        else:
            keep, shift = rows >= k, k
        sr = jnp.where(keep, pltpu.roll(xr, shift, 0), 0.0)
        si = jnp.where(keep, pltpu.roll(xi, shift, 0), 0.0)
        xr, xi = xr + cr * sr - ci * si, xi + cr * si + ci * sr
        cr, ci = cr * cr - ci * ci, 2.0 * cr * ci
        k *= 2
    return xr, xi


def _pick_row(x, r):
    rows = lax.broadcasted_iota(jnp.int32, x.shape, 0)
    return jnp.sum(jnp.where(rows == r, x, 0.0), axis=0, keepdims=True)


def _s5_tile_specs(n_t, rev):
    t_of = (lambda i: n_t - 1 - i) if rev else (lambda i: i)
    u_spec = pl.BlockSpec((S5_T, 128), lambda j, i: (t_of(i), j))
    x_spec = pl.BlockSpec((S5_T, S5_LANES), lambda j, i: (t_of(i), j))
    b_spec = pl.BlockSpec((1, 128, S5_LANES), lambda j, i: (j, 0, 0))
    c_spec = pl.BlockSpec((1, S5_LANES, 128), lambda j, i: (j, 0, 0))
    a_spec = pl.BlockSpec((1, S5_LANES), lambda j, i: (0, j))
    return u_spec, x_spec, b_spec, c_spec, a_spec


def _s5_scan_fwd(u, btr, bti, ctr, cti, abr, abi):
    s_len = u.shape[0]
    n_t = s_len // S5_T

    def body(u_ref, btr_ref, bti_ref, ctr_ref, cti_ref, ar_ref, ai_ref, xr_ref, xi_ref, y_ref, car, cai, pwr, pwi):
        ar, ai = ar_ref[...], ai_ref[...]

        @pl.when(pl.program_id(1) == 0)
        def _():
            car[...] = jnp.zeros_like(car)
            cai[...] = jnp.zeros_like(cai)
            rows = lax.broadcasted_iota(jnp.int32, (S5_T, S5_LANES), 0)
            pwr[...], pwi[...] = _cscan(jnp.where(rows == 0, ar, 0.0), jnp.where(rows == 0, ai, 0.0), ar, ai, False)

        ub = u_ref[...]
        xr, xi = _cscan(_bdot(ub, btr_ref[0]), _bdot(ub, bti_ref[0]), ar, ai, False)
        cr, ci = car[...], cai[...]
        pr, pi_ = pwr[...], pwi[...]
        xr, xi = xr + pr * cr - pi_ * ci, xi + pr * ci + pi_ * cr
        xr_ref[...] = xr
        xi_ref[...] = xi
        car[...] = _pick_row(xr, S5_T - 1)
        cai[...] = _pick_row(xi, S5_T - 1)
        y_ref[...] = _bdot(xr, ctr_ref[0]) - _bdot(xi, cti_ref[0])

    u_spec, x_spec, b_spec, c_spec, a_spec = _s5_tile_specs(n_t, False)
    return pl.pallas_call(
        body, name="s5_scan_fwd", grid=(S5_TILES, n_t),
        in_specs=[u_spec, b_spec, b_spec, c_spec, c_spec, a_spec, a_spec],
        out_specs=[x_spec, x_spec, u_spec],
        out_shape=[_sds(s_len, S5_TILES * S5_LANES)] * 2 + [_sds(s_len, B_WIDTH)],
        scratch_shapes=[pltpu.VMEM((1, S5_LANES), F32)] * 2 + [pltpu.VMEM((S5_T, S5_LANES), F32)] * 2,
        compiler_params=_cp("arbitrary", "arbitrary"),
    )(u, btr, bti, ctr, cti, abr, abi)


def _s5_scan_bwd(dy, xr, xi, u, btr, bti, ctr, cti, abr, abi):
    s_len = u.shape[0]
    n_t = s_len // S5_T

    def body(dy_ref, xr_ref, xi_ref, xrp_ref, xip_ref, u_ref, btr_ref, bti_ref, ctr_ref, cti_ref, ar_ref, ai_ref,
             du_ref, dbtr_ref, dbti_ref, dctr_ref, dcti_ref, dar_ref, dai_ref, car, cai, pwr, pwi):
        ar, ai = ar_ref[...], ai_ref[...]
        i = pl.program_id(1)
        rows = lax.broadcasted_iota(jnp.int32, (S5_T, S5_LANES), 0)

        @pl.when(i == 0)
        def _():
            for r in (car, cai, dbtr_ref, dbti_ref, dctr_ref, dcti_ref, dar_ref, dai_ref):
                r[...] = jnp.zeros_like(r)
            last = rows == S5_T - 1
            pwr[...], pwi[...] = _cscan(jnp.where(last, ar, 0.0), jnp.where(last, -ai, 0.0), ar, -ai, True)

        dyb = dy_ref[...]
        xr_b, xi_b, ub = xr_ref[...], xi_ref[...], u_ref[...]
        dctr_ref[0] += _bdot_tn(xr_b, dyb)
        dcti_ref[0] -= _bdot_tn(xi_b, dyb)
        gr, gi = _cscan(_bdot_nt(dyb, ctr_ref[0]), -_bdot_nt(dyb, cti_ref[0]), ar, -ai, True)
        cr, ci = car[...], cai[...]
        pr, pi_ = pwr[...], pwi[...]
        gr, gi = gr + pr * cr - pi_ * ci, gi + pr * ci + pi_ * cr
        car[...] = _pick_row(gr, 0)
        cai[...] = _pick_row(gi, 0)
        du_ref[...] = _bdot_nt(gr, btr_ref[0]) + _bdot_nt(gi, bti_ref[0])
        dbtr_ref[0] += _bdot_tn(ub, gr)
        dbti_ref[0] += _bdot_tn(ub, gi)
        has_prev = (i < n_t - 1).astype(F32)
        hr = _pick_row(xrp_ref[...], 7) * has_prev
        hi = _pick_row(xip_ref[...], 7) * has_prev
        xpr = jnp.where(rows == 0, hr, pltpu.roll(xr_b, 1, 0))
        xpi = jnp.where(rows == 0, hi, pltpu.roll(xi_b, 1, 0))
        dar_ref[...] += jnp.sum(gr * xpr + gi * xpi, axis=0, keepdims=True)
        dai_ref[...] += jnp.sum(gi * xpr - gr * xpi, axis=0, keepdims=True)

    u_spec, x_spec, b_spec, c_spec, a_spec = _s5_tile_specs(n_t, True)
    halo = pl.BlockSpec((8, S5_LANES), lambda j, i: (jnp.maximum((n_t - 1 - i) * (S5_T // 8) - 1, 0), j))
    return pl.pallas_call(
        body, name="s5_scan_bwd", grid=(S5_TILES, n_t),
        in_specs=[u_spec, x_spec, x_spec, halo, halo, u_spec, b_spec, b_spec, c_spec, c_spec, a_spec, a_spec],
        out_specs=[u_spec, b_spec, b_spec, c_spec, c_spec, a_spec, a_spec],
        out_shape=[_sds(s_len, B_WIDTH)] + [_sds(S5_TILES, 128, S5_LANES)] * 2 + [_sds(S5_TILES, S5_LANES, 128)] * 2
        + [_sds(1, S5_TILES * S5_LANES)] * 2,
        scratch_shapes=[pltpu.VMEM((1, S5_LANES), F32)] * 2 + [pltpu.VMEM((S5_T, S5_LANES), F32)] * 2,
        compiler_params=_cp("arbitrary", "arbitrary"),
    )(dy, xr, xi, xr, xi, u, btr, bti, ctr, cti, abr, abi)


def _blockdiag_b(bbar_t):
    blocks = bbar_t.reshape(S5_TILES, 8, B_GROUP, B_STATE)
    return jnp.einsum('jgmp,gh->jgmhp', blocks, jnp.eye(8, dtype=F32)).reshape(S5_TILES, 128, S5_LANES)


def _blockdiag_b_t(d):
    return jnp.einsum('jgmgp->jgmp', d.reshape(S5_TILES, 8, B_GROUP, 8, B_STATE)).reshape(B_WIDTH, B_STATE)


def _blockdiag_c(c):
    blocks = c.reshape(S5_TILES, 8, B_GROUP, B_STATE)
    return jnp.einsum('jgmp,gh->jhpgm', blocks, jnp.eye(8, dtype=F32)).reshape(S5_TILES, S5_LANES, 128)


def _blockdiag_c_t(d):
    return jnp.einsum('jgpgm->jgmp', d.reshape(S5_TILES, 8, B_STATE, 8, B_GROUP)).reshape(B_GROUPS, B_GROUP, B_STATE)


def _l0_out(o3, l3, ga, gb, ypre, u, x, d_skip, glu_w, glu_b, w_out, post_g, gate):
    s_len = x.shape[0]

    def body(o_ref, l_ref, ga_ref, gb_ref, yp_ref, u_ref, x_ref, d_ref, gw_ref, gbias_ref, w_ref, pg_ref, gt_ref, x1_ref, y_ref):
        oa = _merge_gate(o_ref[0], o_ref[1], o_ref[2], l_ref[0], l_ref[1], l_ref[2], ga_ref[...])
        yb = _s5_gelu(yp_ref[...], u_ref[...], d_ref[...])
        ob = _s5_glu(yb, _bdot(yb, gw_ref[...]) + gbias_ref[...], gb_ref[...])
        y = _bdot(oa, w_ref[0:512, :]) + _bdot(ob, w_ref[512:1024, :])
        y_ref[...] = y
        x1_ref[...] = _post_res(y, x_ref[...], pg_ref[...], gt_ref[...])

    t3 = pl.BlockSpec((3, TM, 512), lambda i: (0, i, 0))
    vec, half = _fix((1, D_MODEL)), _fix((1, 512))
    return pl.pallas_call(
        body, name="l0_out", grid=(s_len // TM,),
        in_specs=[t3, t3, _row(TM, 512), _row(TM, 512), _row(TM, 512), _row(TM, 512), _row(TM, D_MODEL),
                  half, _fix((512, 512)), half, _fix((D_MODEL, D_MODEL)), vec, vec],
        out_specs=[_row(TM, D_MODEL)] * 2,
        out_shape=[_sds(s_len, D_MODEL)] * 2,
        compiler_params=_cp("arbitrary"),
    )(o3, l3, ga, gb, ypre, u, x, d_skip, glu_w, glu_b, w_out, post_g, gate)


def _l0_out_bwd(o3, l3, ga, gb, ypre, u, x, y, d_skip, glu_w, glu_b, w_out, post_g, gate, dx1):
    s_len = x.shape[0]

    def body(o_ref, l_ref, ga_ref, gb_ref, yp_ref, u_ref, x_ref, y_ref, d_ref, gw_ref, gbias_ref, w_ref, pg_ref, gt_ref, dx1_ref,
             do_ref, dl_ref, dga_ref, dgb_ref, dyp_ref, du_ref, dd_ref, dgw_ref, dgbias_ref, dw_ref, dpg_ref, dgt_ref):
        _zero_at_first([dd_ref, dgw_ref, dgbias_ref, dw_ref, dpg_ref, dgt_ref])
        _, vjp2 = jax.vjp(_post_res, y_ref[...], x_ref[...], pg_ref[...], gt_ref[...])
        dy, _, dpg, dgt = vjp2(dx1_ref[...])
        _acc(dpg_ref, dpg)
        _acc(dgt_ref, dgt)
        oa, vjp_a = jax.vjp(_merge_gate, o_ref[0], o_ref[1], o_ref[2], l_ref[0], l_ref[1], l_ref[2], ga_ref[...])
        yb, vjp_g = jax.vjp(_s5_gelu, yp_ref[...], u_ref[...], d_ref[...])
        gl = _bdot(yb, gw_ref[...]) + gbias_ref[...]
        ob, vjp_b = jax.vjp(_s5_glu, yb, gl, gb_ref[...])
        dw_ref[0:512, :] += _bdot_tn(oa, dy)
        dw_ref[512:1024, :] += _bdot_tn(ob, dy)
        d1, d2, d3, e1, e2, e3, dga = vjp_a(_bdot_nt(dy, w_ref[0:512, :]))
        do_ref[0], do_ref[1], do_ref[2] = d1, d2, d3
        dl_ref[0], dl_ref[1], dl_ref[2] = e1, e2, e3
        dga_ref[...] = dga
        dyb, dgl, dgb = vjp_b(_bdot_nt(dy, w_ref[512:1024, :]))
        dgb_ref[...] = dgb
        dgw_ref[...] += _bdot_tn(yb, dgl)
        _acc(dgbias_ref, jnp.sum(dgl, axis=0, keepdims=True))
        dyp, du, dd = vjp_g(dyb + _bdot_nt(dgl, gw_ref[...]))
        dyp_ref[...] = dyp
        du_ref[...] = du
        _acc(dd_ref, dd)

    t3 = pl.BlockSpec((3, TM, 512), lambda i: (0, i, 0))
    vec, half = _fix((1, D_MODEL)), _fix((1, 512))
    r5, r10 = _row(TM, 512), _row(TM, D_MODEL)
    return pl.pallas_call(
        body, name="l0_out_bwd", grid=(s_len // TM,),
        in_specs=[t3, t3, r5, r5, r5, r5, r10, r10, half, _fix((512, 512)), half, _fix((D_MODEL, D_MODEL)), vec, vec, r10],
        out_specs=[t3, t3, r5, r5, r5, r5, half, _fix((512, 512)), half, _fix((D_MODEL, D_MODEL)), vec, vec],
        out_shape=[_sds(3, s_len, 512)] * 2 + [_sds(s_len, 512)] * 4
        + [_sds(1, 512), _sds(512, 512), _sds(1, 512), _sds(D_MODEL, D_MODEL), _sds(1, D_MODEL), _sds(1, D_MODEL)],
        compiler_params=_cp("arbitrary"),
    )(o3, l3, ga, gb, ypre, u, x, y, d_skip, glu_w, glu_b, w_out, post_g, gate, dx1)


def _l1_front(x, pre_g, scale, shift, w_in):
    s_len = x.shape[0]

    def body(x_ref, g_ref, sc_ref, sh_ref, w_ref, raw_ref, gate_ref, ba_ref, h_ref):
        hb = _pre_mod(x_ref[...], g_ref[...], sc_ref[...], sh_ref[...]).astype(BF)
        h_ref[...] = hb
        z = jnp.dot(hb, w_ref[...], preferred_element_type=F32)
        raw_ref[...] = z[:, 0:QKV]
        gate_ref[...] = z[:, QKV:QKV + 1024]
        ba_ref[...] = z[:, QKV + 1024:C_IN_PAD]

    vec = _fix((1, D_MODEL))
    return pl.pallas_call(
        body, name="l1_front", grid=(s_len // TM,),
        in_specs=[_row(TM, D_MODEL), vec, vec, vec, _fix((D_MODEL, C_IN_PAD))],
        out_specs=[_row(TM, QKV), _row(TM, 1024), _row(TM, 128), _row(TM, D_MODEL)],
        out_shape=[_sds(s_len, QKV), _sds(s_len, 1024), _sds(s_len, 128), _sds(s_len, D_MODEL, dtype=BF)],
        compiler_params=_cp("arbitrary"),
    )(x, pre_g, scale, shift, w_in)


def _bg_fn(ba, alog_row, dtb_row):
    lane = lax.broadcasted_iota(jnp.int32, (1, 128), 1)
    g = -jnp.exp(alog_row) * jax.nn.softplus(ba + dtb_row)
    return jnp.where(lane < C_HEADS, jax.nn.sigmoid(ba), jnp.where(lane < 2 * C_HEADS, g, 0.0))


def _act_q(c):
    q = jax.nn.silu(c)
    return q * lax.rsqrt(jnp.sum(q * q, axis=-1, keepdims=True) + EPS) * (C_DK ** -0.5)


def _act_k(c):
    k = jax.nn.silu(c)
    return k * lax.rsqrt(jnp.sum(k * k, axis=-1, keepdims=True) + EPS)


def _act_of(s):
    return _act_q if s < 8 else (_act_k if s < 16 else jax.nn.silu)


def _gdn_prep(raw, ba, conv_w, alog_row, dtb_row):
    s_len = raw.shape[0]

    def body(raw_ref, halo_ref, ba_ref, w_ref, al_ref, dt_ref, qkv_ref, bg_ref):
        bg_ref[...] = _bg_fn(ba_ref[...], al_ref[...], dt_ref[...])
        has_prev = (pl.program_id(0) > 0).astype(F32)
        for s in range(24):
            sl = slice(s * 128, (s + 1) * 128)
            cat = jnp.concatenate([halo_ref[:, sl] * has_prev, raw_ref[:, sl]], axis=0)
            conv = w_ref[3:4, sl] * cat[8:]
            for j in range(3):
                conv = conv + w_ref[j:j + 1, sl] * pltpu.roll(cat, 3 - j, 0)[8:]
            qkv_ref[:, sl] = _act_of(s)(conv)

    halo = pl.BlockSpec((8, QKV), lambda i: (jnp.maximum(i * (TM // 8) - 1, 0), 0))
    row128 = _fix((1, 128))
    return pl.pallas_call(
        body, name="gdn_prep", grid=(s_len // TM,),
        in_specs=[_row(TM, QKV), halo, _row(TM, 128), _fix((C_CONV, QKV)), row128, row128],
        out_specs=[_row(TM, QKV), _row(TM, 128)],
        out_shape=[_sds(s_len, QKV), _sds(s_len, 128)],
        compiler_params=_cp("arbitrary"),
    )(raw, raw, ba, conv_w, alog_row, dtb_row)


def _gdn_prep_bwd(raw, ba, conv_w, alog_row, dtb_row, dq, dk, dv, dbg):
    s_len = raw.shape[0]
    n_tiles = s_len // TM
    ext = TM + 8

    def body(raw_ref, prev_ref, next_ref, ba_ref, w_ref, al_ref, dt_ref, dq_ref, dqn_ref, dk_ref, dkn_ref, dv_ref, dvn_ref, dbg_ref,
             draw_ref, dba_ref, dw_ref, dal_ref, ddt_ref):
        _zero_at_first([dw_ref, dal_ref, ddt_ref])
        i = pl.program_id(0)
        _, vjp_bg = jax.vjp(_bg_fn, ba_ref[...], al_ref[...], dt_ref[...])
        dba, dal, ddt = vjp_bg(dbg_ref[...])
        dba_ref[...] = dba
        _acc(dal_ref, dal)
        _acc(ddt_ref, ddt)
        has_prev = (i > 0).astype(F32)
        has_next = (i < n_tiles - 1).astype(F32)
        ct_refs = ((dq_ref, dqn_ref), (dk_ref, dkn_ref), (dv_ref, dvn_ref))
        for s in range(24):
            sl = slice(s * 128, (s + 1) * 128)
            hl = slice((s % 8) * 128, (s % 8 + 1) * 128)
            tile_ref, nxt_ref = ct_refs[s // 8]
            cat = jnp.concatenate([prev_ref[:, sl] * has_prev, raw_ref[:, sl], next_ref[:, sl] * has_next], axis=0)
            shifted = [pltpu.roll(cat, 3 - j, 0)[8:] for j in range(3)] + [cat[8:]]
            conv = w_ref[3:4, sl] * shifted[3]
            for j in range(3):
                conv = conv + w_ref[j:j + 1, sl] * shifted[j]
            ct = jnp.concatenate([tile_ref[:, hl], nxt_ref[:, hl] * has_next], axis=0)
            _, vjp_act = jax.vjp(_act_of(s), conv)
            dconv, = vjp_act(ct)
            draw = w_ref[3:4, sl] * dconv[:TM]
            for j in range(3):
                draw = draw + w_ref[j:j + 1, sl] * pltpu.roll(dconv, ext - (3 - j), 0)[:TM]
            draw_ref[:, sl] = draw
            for j in range(4):
                dw_ref[j:j + 1, sl] += jnp.sum(dconv[:TM] * shifted[j][:TM], axis=0, keepdims=True)

    prev = pl.BlockSpec((8, QKV), lambda i: (jnp.maximum(i * (TM // 8) - 1, 0), 0))
    nxt = lambda n: pl.BlockSpec((8, n), lambda i: (jnp.minimum((i + 1) * (TM // 8), s_len // 8 - 1), 0))
    row128 = _fix((1, 128))
    ct_specs = [_row(TM, 1024), nxt(1024)] * 3
    return pl.pallas_call(
        body, name="gdn_prep_bwd", grid=(n_tiles,),
        in_specs=[_row(TM, QKV), prev, nxt(QKV), _row(TM, 128), _fix((C_CONV, QKV)), row128, row128] + ct_specs + [_row(TM, 128)],
        out_specs=[_row(TM, QKV), _row(TM, 128), _fix((C_CONV, QKV)), row128, row128],
        out_shape=[_sds(s_len, QKV), _sds(s_len, 128), _sds(C_CONV, QKV), _sds(1, 128), _sds(1, 128)],
        compiler_params=_cp("arbitrary"),
    )(raw, raw, raw, ba, conv_w, alog_row, dtb_row, dq, dq, dk, dk, dv, dv, dbg)


def _tein(eq, a, b):
    return jnp.einsum(eq, a, b, precision=lax.Precision.HIGH, preferred_element_type=F32)


def _unit_lower_inverse(lower):
    ri = lax.broadcasted_iota(jnp.int32, (C_CHUNK, C_CHUNK), 0)
    ci = lax.broadcasted_iota(jnp.int32, (C_CHUNK, C_CHUNK), 1)
    p_mat = -lower
    inv = (ri == ci).astype(F32)[None] + p_mat
    for _ in range(5):
        p_mat = _tein('hij,hjk->hik', p_mat, p_mat)
        inv = inv + _tein('hij,hjk->hik', inv, p_mat)
    return inv


@jax.custom_vjp
def _known_inverse(lower, inv):
    return inv


def _known_inverse_fwd(lower, inv):
    return inv, inv


def _known_inverse_bwd(inv, d_inv):
    d_lower = -_tein('hik,hjk->hij', _tein('hji,hjk->hik', inv, d_inv), inv)
    return d_lower, jnp.zeros_like(inv)


_known_inverse.defvjp(_known_inverse_fwd, _known_inverse_bwd)


def _gdn_chunk(q, k, v, bg, state, inv_known=None):
    lane = lax.broadcasted_iota(jnp.int32, (1, 128), 1)
    ri = lax.broadcasted_iota(jnp.int32, (C_CHUNK, C_CHUNK), 0)
    ci = lax.broadcasted_iota(jnp.int32, (C_CHUNK, C_CHUNK), 1)
    gc_t = _hdot((ri >= ci).astype(F32), bg)
    beta = jnp.stack([jnp.sum(jnp.where(lane == h, bg, 0.0), axis=-1, keepdims=True) for h in range(C_HEADS)], axis=0)
    gc = jnp.stack([jnp.sum(jnp.where(lane == C_HEADS + h, gc_t, 0.0), axis=-1, keepdims=True) for h in range(C_HEADS)], axis=0)
    gc_rows = gc_t.T
    row_id = lax.broadcasted_iota(jnp.int32, (128, C_CHUNK), 0)
    gcj = jnp.stack([jnp.sum(jnp.where(row_id == C_HEADS + h, gc_rows, 0.0), axis=0, keepdims=True) for h in range(C_HEADS)], axis=0)
    tril, strict = (ri >= ci)[None], (ri > ci)[None]
    decay = jnp.exp(jnp.where(tril, gc - gcj, -1e30))
    kb = k * beta
    lower = jnp.where(strict, _bein('hid,hjd->hij', kb, k) * decay, 0.0)
    inv = _unit_lower_inverse(lower) if inv_known is None else _known_inverse(lower, inv_known)
    egc = jnp.exp(gc)
    u_c = _tein('hij,hjd->hid', inv, v * beta)
    w_c = _tein('hij,hjd->hid', inv, kb * egc)
    aqk = _bein('hid,hjd->hij', q, k) * decay
    rowi = lax.broadcasted_iota(jnp.int32, (1, C_CHUNK, 1), 1)
    g_last = jnp.sum(jnp.where(rowi == C_CHUNK - 1, gc, 0.0), axis=1, keepdims=True)
    kd = k * jnp.exp(g_last - gc)
    v_new = u_c - _bein('hik,hkv->hiv', w_c, state)
    o = _bein('hik,hkv->hiv', q * egc, state) + _bein('hij,hjv->hiv', aqk, v_new)
    return o, state * jnp.exp(g_last) + _bein('hik,hiv->hkv', kd, v_new), inv


def _heads(ref):
    return jnp.stack([ref[:, h * C_DK:(h + 1) * C_DK] for h in range(C_HEADS)], axis=0)


def _gdn_fwd(qkv, bg):
    s_len = qkv.shape[0]
    n_c = s_len // C_CHUNK

    def body(q_ref, k_ref, v_ref, bg_ref, o_ref, ss_ref, inv_ref, st_ref):
        _zero_at_first([st_ref])
        s0 = st_ref[...]
        ss_ref[0] = s0
        o, s2, inv = _gdn_chunk(_heads(q_ref), _heads(k_ref), _heads(v_ref), bg_ref[...], s0)
        st_ref[...] = s2
        inv_ref[0] = inv
        for h in range(C_HEADS):
            o_ref[:, h * C_DK:(h + 1) * C_DK] = o[h]

    col = lambda c: pl.BlockSpec((C_CHUNK, 1024), lambda i: (i, c))
    return pl.pallas_call(
        body, name="gdn_fwd", grid=(n_c,),
        in_specs=[col(0), col(1), col(2), _row(C_CHUNK, 128)],
        out_specs=[_row(C_CHUNK, 1024), pl.BlockSpec((1, C_HEADS, C_DK, C_DK), lambda i: (i, 0, 0, 0)),
                   pl.BlockSpec((1, C_HEADS, C_CHUNK, C_CHUNK), lambda i: (i, 0, 0, 0))],
        out_shape=[_sds(s_len, 1024), _sds(n_c, C_HEADS, C_DK, C_DK), _sds(n_c, C_HEADS, C_CHUNK, C_CHUNK)],
        scratch_shapes=[pltpu.VMEM((C_HEADS, C_DK, C_DK), F32)],
        compiler_params=_cp("arbitrary"),
    )(qkv, qkv, qkv, bg)


def _gdn_bwd(qkv, bg, states, invs, do):
    s_len = qkv.shape[0]
    n_c = s_len // C_CHUNK

    def body(q_ref, k_ref, v_ref, bg_ref, ss_ref, inv_ref, do_ref, dq_ref, dk_ref, dv_ref, dbg_ref, ds_ref):
        _zero_at_first([ds_ref])
        inv_known = inv_ref[0]
        chunk = lambda q, k, v, bg_, st: _gdn_chunk(q, k, v, bg_, st, inv_known)[:2]
        _, vjp = jax.vjp(chunk, _heads(q_ref), _heads(k_ref), _heads(v_ref), bg_ref[...], ss_ref[0])
        dq, dk, dv, dbg, ds = vjp((_heads(do_ref), ds_ref[...]))
        ds_ref[...] = ds
        dbg_ref[...] = dbg
        for h in range(C_HEADS):
            sl = slice(h * C_DK, (h + 1) * C_DK)
            dq_ref[:, sl], dk_ref[:, sl], dv_ref[:, sl] = dq[h], dk[h], dv[h]

    rev = lambda i: n_c - 1 - i
    col = lambda c: pl.BlockSpec((C_CHUNK, 1024), lambda i: (rev(i), c))
    row128 = pl.BlockSpec((C_CHUNK, 128), lambda i: (rev(i), 0))
    per_chunk = lambda n: pl.BlockSpec((1, C_HEADS, n, n), lambda i: (rev(i), 0, 0, 0))
    return pl.pallas_call(
        body, name="gdn_bwd", grid=(n_c,),
        in_specs=[col(0), col(1), col(2), row128, per_chunk(C_DK), per_chunk(C_CHUNK), col(0)],
        out_specs=[col(0), col(0), col(0), row128],
        out_shape=[_sds(s_len, 1024)] * 3 + [_sds(s_len, 128)],
        scratch_shapes=[pltpu.VMEM((C_HEADS, C_DK, C_DK), F32)],
        compiler_params=_cp("arbitrary"),
    )(qkv, qkv, qkv, bg, states, invs, do)


def _head_norm_gate(o, gate, norm_g):
    return (_rms(o) * norm_g) * jax.nn.silu(gate)


def _l1_out_fb(o, gate_c, x1, target, norm_g, w_out, post_g, gate):
    s_len = x1.shape[0]

    def body(o_ref, gc_ref, x1_ref, t_ref, ng_ref, w_ref, pg_ref, gt_ref,
             loss_ref, dres_ref, do_ref, dgc_ref, dw_ref, dng_ref, dpg_ref, dgt_ref):
        _zero_at_first([loss_ref, dw_ref, dng_ref, dpg_ref, dgt_ref])
        ng = ng_ref[...]
        ons, vjps = [], []
        for h in range(C_HEADS):
            sl = slice(h * C_DK, (h + 1) * C_DK)
            on, vjp_h = jax.vjp(_head_norm_gate, o_ref[:, sl], gc_ref[:, sl], ng)
            ons.append(on)
            vjps.append(vjp_h)
        on_all = jnp.concatenate(ons, axis=-1)
        y = _bdot(on_all, w_ref[...])
        x2, vjp2 = jax.vjp(_post_res, y, x1_ref[...], pg_ref[...], gt_ref[...])
        err = x2 - t_ref[...]
        _acc(loss_ref, jnp.full((1, 128), 0.5 * jnp.sum(jnp.mean(err * err, axis=-1)), F32))
        dx2 = err * (1.0 / D_MODEL)
        dy, _, dpg, dgt = vjp2(dx2)
        dres_ref[...] = dx2
        _acc(dpg_ref, dpg)
        _acc(dgt_ref, dgt)
        dw_ref[...] += _bdot_tn(on_all, dy)
        don = _bdot_nt(dy, w_ref[...])
        for h in range(C_HEADS):
            sl = slice(h * C_DK, (h + 1) * C_DK)
            do_h, dgc_h, dng = vjps[h](don[:, sl])
            do_ref[:, sl] = do_h
            dgc_ref[:, sl] = dgc_h
            _acc(dng_ref, dng)

    vec, r10 = _fix((1, D_MODEL)), _row(TM, D_MODEL)
    row128 = _fix((1, 128))
    return pl.pallas_call(
        body, name="l1_out_fb", grid=(s_len // TM,),
        in_specs=[r10, r10, r10, r10, row128, _fix((D_MODEL, D_MODEL)), vec, vec],
        out_specs=[row128, r10, r10, r10, _fix((D_MODEL, D_MODEL)), row128, vec, vec],
        out_shape=[_sds(1, 128), _sds(s_len, D_MODEL), _sds(s_len, D_MODEL), _sds(s_len, D_MODEL),
                   _sds(D_MODEL, D_MODEL), _sds(1, 128), _sds(1, D_MODEL), _sds(1, D_MODEL)],
        compiler_params=_cp("arbitrary"),
    )(o, gate_c, x1, target, norm_g, w_out, post_g, gate)


def _row_of(v, width, at):
    return jnp.zeros((1, width), F32).at[0, at:at + v.shape[-1]].set(v.reshape(-1))


def _local_step(x, target, mod, wd):
    s_len = x.shape[0]
    shift0, scale0, gate0 = (mod[0:1, i * 1024:(i + 1) * 1024] for i in range(3))
    shift1, scale1, gate1 = (mod[1:2, i * 1024:(i + 1) * 1024] for i in range(3))
    pre_g0, pre_g1 = wd["pre_g"][0:1], wd["pre_g"][1:2]
    post_g0, post_g1 = wd["post_g"][0:1], wd["post_g"][1:2]
    w_in0 = wd["ab_w_in"].astype(BF)
    w_out0 = wd["ab_w_out"].astype(BF)
    glu_w = wd["s5_glu_w"].astype(BF)
    w_in1 = jnp.concatenate([wd["gdn_w_in"], jnp.zeros((D_MODEL, C_IN_PAD - wd["gdn_w_in"].shape[1]), F32)], axis=1).astype(BF)
    w_out1 = wd["gdn_w_out"].astype(BF)
    d_skip, glu_b = wd["s5_d"].reshape(1, 512), wd["s5_glu_b"].reshape(1, 512)
    norm_g = wd["gdn_norm_g"].reshape(1, 128)
    alog_row = _row_of(wd["gdn_a_log"], 128, C_HEADS)
    dtb_row = _row_of(wd["gdn_dt_bias"], 128, C_HEADS)
    conv_w = wd["gdn_conv"]

    a_re, a_im = wd["s5_a_re"], wd["s5_a_im"]
    log_dt = wd["s5_log_dt"].reshape(B_GROUPS, 1)
    bt_re = wd["s5_b_re"].transpose(0, 2, 1).reshape(B_WIDTH, B_STATE)
    bt_im = wd["s5_b_im"].transpose(0, 2, 1).reshape(B_WIDTH, B_STATE)
    abar_r, abar_i, bbar_r, bbar_i = _s5_params(a_re, a_im, log_dt, bt_re, bt_im)
    abr, abi = abar_r.reshape(1, -1), abar_i.reshape(1, -1)
    btr, bti = _blockdiag_b(bbar_r).astype(BF), _blockdiag_b(bbar_i).astype(BF)
    ctr, cti = _blockdiag_c(wd["s5_c_re"]).astype(BF), _blockdiag_c(wd["s5_c_im"]).astype(BF)

    table = _bucket_table()
    bias3 = _attn_bias(wd["rel_bias"], jnp.asarray(table))
    q, k, v, u, ga, gb, h0 = _l0_front(x, pre_g0, scale0, shift0, w_in0)
    q3, k3, v3 = _perm3(q), _perm3(k), _perm3(v)
    o3p, l3p = _attn_fwd(q3, k3, v3, bias3)
    o3, l3 = _unperm3(o3p), _unperm3(l3p)
    xr, xi, ypre = _s5_scan_fwd(u, btr, bti, ctr, cti, abr, abi)
    x1, y0 = _l0_out(o3, l3, ga, gb, ypre, u, x, d_skip, glu_w, glu_b, w_out0, post_g0, gate0)

    raw, gate_c, ba, h1 = _l1_front(x1, pre_g1, scale1, shift1, w_in1)
    qkv, bg = _gdn_prep(raw, ba, conv_w, alog_row, dtb_row)
    o_gdn, states, invs = _gdn_fwd(qkv, bg)
    loss_row, dres1, do_gdn, dgate_c, dw_out1, dnorm_g, dpost_g1, dgate1 = _l1_out_fb(
        o_gdn, gate_c, x1, target, norm_g, w_out1, post_g1, gate1)

    dq1, dk1, dv1, dbg = _gdn_bwd(qkv, bg, states, invs, do_gdn)
    draw, dba, dconv_w, dalog_row, ddtb_row = _gdn_prep_bwd(raw, ba, conv_w, alog_row, dtb_row, dq1, dk1, dv1, dbg)
    dz1, dx1, dpre_g1, dscale1, dshift1 = _front_bwd(
        "l1_front_bwd", x1, pre_g1, scale1, shift1, w_in1, dres1, [[draw], [dgate_c], [dba]], [QKV, 1024, 128])
    dw_in1 = _matmul_tn("l1_dw_in", h1, dz1, 1408)

    (do3, dl3, dga, dgb, dypre, du_skip, dd_skip, dglu_w, dglu_b, dw_out0, dpost_g0, dgate0) = _l0_out_bwd(
        o3, l3, ga, gb, ypre, u, x, y0, d_skip, glu_w, glu_b, w_out0, post_g0, gate0, dx1)
    du_scan, dbtr, dbti, dctr, dcti, dabr, dabi = _s5_scan_bwd(dypre, xr, xi, u, btr, bti, ctr, cti, abr, abi)
    dq3p, dka, dkb, dva, dvb, dbias3 = _attn_bwd(q3, k3, v3, bias3, o3p, l3p, _perm3_each(do3), _perm3_each(dl3))
    up = lambda a3: jnp.concatenate([a3[:, A_BLOCK:], jnp.zeros((3, A_BLOCK, A_WIDTH), F32)], axis=1)
    dq3, dka3, dkb3, dva3, dvb3 = (_unperm3(a) for a in (dq3p, dka, up(dkb), dva, up(dvb)))
    parts = [[dq3[0], dq3[1], dq3[2]],
             [dka3[0], dka3[1], dka3[2], dkb3[0], dkb3[1], dkb3[2]],
             [dva3[0], dva3[1], dva3[2], dvb3[0], dvb3[1], dvb3[2]],
             [du_skip, du_scan], [dga], [dgb]]
    dz0, grad_x, dpre_g0, dscale0, dshift0 = _front_bwd(
        "l0_front_bwd", x, pre_g0, scale0, shift0, w_in0, dx1, parts, [512] * 6)
    dw_in0 = _matmul_tn("l0_dw_in", h0, dz0, 768)

    idx_rows = jnp.asarray(table.reshape(3, -1), F32)
    drel = _rel_bias_grad(dbias3, idx_rows).T
    da_re, da_im, dlog_dt, dbt_re, dbt_im = _s5_params_bwd(
        a_re, a_im, log_dt, bt_re, bt_im, dabr.reshape(B_GROUPS, B_STATE), dabi.reshape(B_GROUPS, B_STATE),
        _blockdiag_b_t(dbtr), _blockdiag_b_t(dbti))
    unb = lambda d: d.reshape(B_GROUPS, B_GROUP, B_STATE).transpose(0, 2, 1)
    grads = {
        "pre_g": jnp.concatenate([dpre_g0, dpre_g1], 0), "post_g": jnp.concatenate([dpost_g0, dpost_g1], 0),
        "rel_bias": drel, "ab_w_in": dw_in0, "ab_w_out": dw_out0,
        "s5_a_re": da_re, "s5_a_im": da_im, "s5_log_dt": dlog_dt.reshape(B_GROUPS),
        "s5_b_re": unb(dbt_re), "s5_b_im": unb(dbt_im),
        "s5_c_re": _blockdiag_c_t(dctr), "s5_c_im": _blockdiag_c_t(dcti),
        "s5_d": dd_skip.reshape(512), "s5_glu_w": dglu_w, "s5_glu_b": dglu_b.reshape(512),
        "gdn_w_in": dw_in1[:, :wd["gdn_w_in"].shape[1]], "gdn_conv": dconv_w,
        "gdn_a_log": dalog_row[0, C_HEADS:2 * C_HEADS], "gdn_dt_bias": ddtb_row[0, C_HEADS:2 * C_HEADS],
        "gdn_norm_g": dnorm_g.reshape(128), "gdn_w_out": dw_out1,
    }
    dmod = jnp.concatenate([jnp.concatenate([dshift0, dscale0, dgate0], 1), jnp.concatenate([dshift1, dscale1, dgate1], 1)], 0)
    return loss_row[0, 0], grad_x, grads, dmod


def _perm3_each(a3):
    return jnp.stack([_perm(a3[i], d) for i, d in enumerate(DILATIONS)], 0)


def _place():
    return lax.axis_index("x"), lax.axis_index("y"), lax.axis_index("c")


def _flip(v, bit):
    return 1 - v if bit else v


def _hbm_call(name, body, arrs, out_shapes, n_sem):
    any_spec = pl.BlockSpec(memory_space=pl.ANY)
    return pl.pallas_call(
        body, name=name,
        in_specs=[any_spec] * len(arrs), out_specs=[any_spec] * len(out_shapes), out_shape=out_shapes,
        scratch_shapes=[pltpu.SemaphoreType.DMA((n_sem,)), pltpu.SemaphoreType.DMA((n_sem,))],
    )(*arrs)


def _own_slot(gathered, own, slot):
    idx = lax.broadcasted_iota(jnp.int32, (gathered.shape[0],) + (1,) * own.ndim, 0)
    return jnp.where(idx == slot, own[None], gathered)


def _all_gather8(name, arr):
    def body(x_ref, out_ref, send_sems, recv_sems):
        x, y, c = _place()
        me = 4 * x + 2 * y + c
        sends, recvs = [], []
        for m in range(1, 8):
            peer = (_flip(x, m & 4), _flip(y, m & 2), _flip(c, m & 1))
            sends.append(pltpu.make_async_remote_copy(x_ref, out_ref.at[me], send_sems.at[m - 1], recv_sems.at[m - 1],
                                                      device_id=peer, device_id_type=MESH))
            recvs.append(pltpu.make_async_remote_copy(x_ref, out_ref.at[4 * peer[0] + 2 * peer[1] + peer[2]], send_sems.at[m - 1],
                                                      recv_sems.at[m - 1], device_id=peer, device_id_type=MESH))
        for cp in sends:
            cp.start()
        for cp in recvs:
            cp.wait_recv()
        for cp in sends:
            cp.wait_send()

    return _hbm_call(name, body, [arr], [jax.ShapeDtypeStruct((8,) + arr.shape, arr.dtype)], 7)[0]


def _chip_exchange(name, arrs, scatter):
    n = len(arrs)

    def body(*refs):
        ins, outs = refs[:n], refs[n:2 * n]
        send_sems, recv_sems = refs[2 * n:]
        x, y, c = _place()
        mine = 2 * x + y
        sends, recvs = [], []
        for a in range(n):
            for m in range(1, 4):
                px, py = _flip(x, m & 2), _flip(y, m & 1)
                k = 3 * a + m - 1
                src = ins[a].at[2 * px + py] if scatter else ins[a]
                sends.append(pltpu.make_async_remote_copy(src, outs[a].at[mine], send_sems.at[k], recv_sems.at[k],
                                                          device_id=(px, py, c), device_id_type=MESH))
                recvs.append(pltpu.make_async_remote_copy(src, outs[a].at[2 * px + py], send_sems.at[k], recv_sems.at[k],
                                                          device_id=(px, py, c), device_id_type=MESH))
        for cp in sends:
            cp.start()
        for cp in recvs:
            cp.wait_recv()
        for cp in sends:
            cp.wait_send()

    shapes = [jax.ShapeDtypeStruct(a.shape if scatter else (4,) + a.shape, a.dtype) for a in arrs]
    return _hbm_call(name, body, arrs, shapes, 3 * n)


def _sibling_exchange(name, arrs):
    n = len(arrs)

    def body(*refs):
        ins, outs = refs[:n], refs[n:2 * n]
        send_sems, recv_sems = refs[2 * n:]
        x, y, c = _place()
        copies = [pltpu.make_async_remote_copy(ins[a], outs[a], send_sems.at[a], recv_sems.at[a],
                                               device_id=(x, y, 1 - c), device_id_type=MESH) for a in range(n)]
        for cp in copies:
            cp.start()
        for cp in copies:
            cp.wait_recv()
        for cp in copies:
            cp.wait_send()

    return _hbm_call(name, body, arrs, [jax.ShapeDtypeStruct(a.shape, a.dtype) for a in arrs], n)


def _row_tile(rows):
    for t in (256, 128, 64, 32, 16, 8):
        if rows % t == 0:
            return t
    return rows


def _pair_sum(name, a, b, out_dtype):
    rows, cols = a.shape
    tr = _row_tile(rows)

    def body(a_ref, b_ref, o_ref):
        o_ref[...] = (a_ref[...] + b_ref[...]).astype(out_dtype)

    return pl.pallas_call(body, name=name, grid=(rows // tr,), in_specs=[_row(tr, cols)] * 2, out_specs=_row(tr, cols),
                          out_shape=_sds(rows, cols, dtype=out_dtype), compiler_params=_cp("arbitrary"))(a, b)


def _chip_sum(name, recv, partial, mine):
    n, rows, cols = recv.shape
    tr = _row_tile(rows)

    def body(mine_ref, *refs):
        own = refs[n][0].astype(F32)
        acc = None
        for s in range(n):
            term = jnp.where(mine_ref[0] == s, own, refs[s][0].astype(F32))
            acc = term if acc is None else acc + term
        refs[-1][...] = acc

    def slot_spec(s):
        return pl.BlockSpec((1, tr, cols), lambda i, m: (jnp.where(m[0] == s, (s + 1) % n, s), i, 0))

    grid_spec = pltpu.PrefetchScalarGridSpec(
        num_scalar_prefetch=1, grid=(rows // tr,),
        in_specs=[slot_spec(s) for s in range(n)] + [pl.BlockSpec((1, tr, cols), lambda i, m: (m[0], i, 0))],
        out_specs=pl.BlockSpec((tr, cols), lambda i, m: (i, 0)))
    return pl.pallas_call(body, name=name, grid_spec=grid_spec, out_shape=_sds(rows, cols),
                          compiler_params=_cp("arbitrary"))(mine, *([recv] * n), partial)


def _slot_sum(name, arr):
    n, rows, cols = arr.shape
    tr = _row_tile(rows)

    def body(*refs):
        acc = refs[0][0]
        for r in refs[1:-1]:
            acc = acc + r[0]
        refs[-1][...] = acc

    specs = [pl.BlockSpec((1, tr, cols), functools.partial(lambda s, i: (s, i, 0), s)) for s in range(n)]
    return pl.pallas_call(body, name=name, grid=(rows // tr,), in_specs=specs, out_specs=_row(tr, cols),
                          out_shape=_sds(rows, cols), compiler_params=_cp("arbitrary"))(*([arr] * n))


def _adamw(name, w, g, m, v):
    rows, cols = w.shape
    tr = _row_tile(rows)

    def body(w_ref, g_ref, m_ref, v_ref, d_ref, nm_ref, nv_ref):
        g_ = g_ref[...]
        m_ = ADAM_B1 * m_ref[...] + (1.0 - ADAM_B1) * g_
        v_ = ADAM_B2 * v_ref[...] + (1.0 - ADAM_B2) * (g_ * g_)
        m_hat = m_ / (1.0 - ADAM_B1 ** ADAM_STEP)
        v_hat = v_ / (1.0 - ADAM_B2 ** ADAM_STEP)
        d_ref[...] = -ADAM_LR * (m_hat / (jnp.sqrt(v_hat) + ADAM_EPS) + ADAM_WD * w_ref[...])
        nm_ref[...] = m_
        nv_ref[...] = v_

    spec = _row(tr, cols)
    return pl.pallas_call(body, name=name, grid=(rows // tr,), in_specs=[spec] * 4, out_specs=[spec] * 3,
                          out_shape=[_sds(rows, cols)] * 3, compiler_params=_cp("arbitrary"))(w, g, m, v)


def _adamw_halves(name, w, g_mine, g_sibling, m, v, core):
    rows, cols = w.shape
    half = rows // 2
    tr = _row_tile(half)
    per_half = half // tr

    def body(core_ref, w_ref, gm_ref, gs_ref, m_ref, v_ref, g_ref, d_ref, nm_ref, nv_ref):
        g_ = jnp.where(pl.program_id(0) // per_half == core_ref[0], gm_ref[...], gs_ref[...])
        m_ = ADAM_B1 * m_ref[...] + (1.0 - ADAM_B1) * g_
        v_ = ADAM_B2 * v_ref[...] + (1.0 - ADAM_B2) * (g_ * g_)
        m_hat = m_ / (1.0 - ADAM_B1 ** ADAM_STEP)
        v_hat = v_ / (1.0 - ADAM_B2 ** ADAM_STEP)
        g_ref[...] = g_
        d_ref[...] = -ADAM_LR * (m_hat / (jnp.sqrt(v_hat) + ADAM_EPS) + ADAM_WD * w_ref[...])
        nm_ref[...] = m_
        nv_ref[...] = v_

    full = pl.BlockSpec((tr, cols), lambda i, c: (i, 0))
    in_half = pl.BlockSpec((tr, cols), lambda i, c: (i % per_half, 0))
    grid_spec = pltpu.PrefetchScalarGridSpec(num_scalar_prefetch=1, grid=(rows // tr,),
                                             in_specs=[full, in_half, in_half, full, full], out_specs=[full] * 4)
    return pl.pallas_call(body, name=name, grid_spec=grid_spec, out_shape=[_sds(rows, cols)] * 4,
                          compiler_params=_cp("arbitrary"))(core, w, g_mine, g_sibling, m, v)


def _mod_local(c_all, ada_w):
    def body(c_ref, w_ref, o_ref):
        c_act = jax.nn.silu(c_ref[...])
        for l in range(2):
            o_ref[l] = _hdot(c_act, w_ref[l])

    return pl.pallas_call(body, name="mod_local", out_shape=_sds(2, 8, ada_w.shape[2]),
                          compiler_params=pltpu.CompilerParams(vmem_limit_bytes=VMEM_LIMIT_BYTES))(c_all, ada_w)


def _ada_w_grad(c_all, dmod_cols):
    def body(c_ref, d_ref, o_ref):
        c_act = jax.nn.silu(c_ref[...])
        for l in range(2):
            o_ref[l] = lax.dot_general(c_act, d_ref[l], (((0,), (0,)), ((), ())), precision=HI, preferred_element_type=F32)

    return pl.pallas_call(body, name="ada_w_grad", out_shape=_sds(2, D_MODEL, dmod_cols.shape[2]),
                          compiler_params=pltpu.CompilerParams(vmem_limit_bytes=VMEM_LIMIT_BYTES))(c_all, dmod_cols)


_SMALL = ("ada_b", "pre_g", "post_g", "rel_bias", "s5_a_re", "s5_a_im", "s5_log_dt", "s5_b_re", "s5_b_im", "s5_c_re", "s5_c_im",
          "s5_d", "s5_glu_b", "gdn_a_log", "gdn_dt_bias", "gdn_norm_g")
_SHARDED = ("ab_w_in", "ab_w_out", "s5_glu_w", "gdn_w_in", "gdn_w_out")
_COL_SHARDED = ("ab_w_in", "gdn_w_in")
_WEIGHTS = ("ada_w", "ada_b", "pre_g", "post_g", "rel_bias", "ab_w_in", "ab_w_out", "s5_a_re", "s5_a_im", "s5_log_dt", "s5_b_re",
            "s5_b_im", "s5_c_re", "s5_c_im", "s5_d", "s5_glu_w", "s5_glu_b", "gdn_w_in", "gdn_conv", "gdn_a_log", "gdn_dt_bias",
            "gdn_norm_g", "gdn_w_out")


def _rows128(n):
    return -(-n // 128)


def _pack(arrs, total_rows):
    pieces = []
    for a in arrs:
        flat = a.reshape(-1)
        pieces.append(jnp.pad(flat, (0, _rows128(flat.shape[0]) * 128 - flat.shape[0])).reshape(-1, 128))
    used = sum(p.shape[0] for p in pieces)
    pieces.append(jnp.zeros((total_rows - used, 128), F32))
    return jnp.concatenate(pieces, axis=0)


def _unpack(buf, shapes):
    out, at = [], 0
    for shp in shapes:
        n = int(np.prod(shp))
        out.append(buf[at:at + _rows128(n)].reshape(-1)[:n].reshape(shp))
        at += _rows128(n)
    return out


def _full_from_halves(name, g):
    if name in _COL_SHARDED:
        return g.transpose(0, 2, 1, 3).reshape(2 * g.shape[2], 4 * g.shape[3])
    return g.transpose(1, 0, 2, 3).reshape(8 * g.shape[2], g.shape[3])


def _shard_major(name, g):
    if name in _COL_SHARDED:
        return g.reshape(g.shape[0], 4, g.shape[1] // 4).transpose(1, 0, 2)
    return g.reshape(4, g.shape[0] // 4, g.shape[1])


def kernel(x, c, ada_w, ada_b, pre_g, post_g, rel_bias, ab_w_in, ab_w_out, s5_a_re, s5_a_im, s5_log_dt, s5_b_re, s5_b_im, s5_c_re, s5_c_im, s5_d, s5_glu_w, s5_glu_b, gdn_w_in, gdn_conv, gdn_a_log, gdn_dt_bias, gdn_norm_g, gdn_w_out, loss_target, m_ada_w, m_ada_b, m_pre_g, m_post_g, m_rel_bias, m_ab_w_in, m_ab_w_out, m_s5_a_re, m_s5_a_im, m_s5_log_dt, m_s5_b_re, m_s5_b_im, m_s5_c_re, m_s5_c_im, m_s5_d, m_s5_glu_w, m_s5_glu_b, m_gdn_w_in, m_gdn_conv, m_gdn_a_log, m_gdn_dt_bias, m_gdn_norm_g, m_gdn_w_out, v_ada_w, v_ada_b, v_pre_g, v_post_g, v_rel_bias, v_ab_w_in, v_ab_w_out, v_s5_a_re, v_s5_a_im, v_s5_log_dt, v_s5_b_re, v_s5_b_im, v_s5_c_re, v_s5_c_im, v_s5_d, v_s5_glu_w, v_s5_glu_b, v_gdn_w_in, v_gdn_conv, v_gdn_a_log, v_gdn_dt_bias, v_gdn_norm_g, v_gdn_w_out):
    w = dict(ada_w=ada_w, ada_b=ada_b, pre_g=pre_g, post_g=post_g, rel_bias=rel_bias, ab_w_in=ab_w_in, ab_w_out=ab_w_out,
             s5_a_re=s5_a_re, s5_a_im=s5_a_im, s5_log_dt=s5_log_dt, s5_b_re=s5_b_re, s5_b_im=s5_b_im, s5_c_re=s5_c_re, s5_c_im=s5_c_im,
             s5_d=s5_d, s5_glu_w=s5_glu_w, s5_glu_b=s5_glu_b, gdn_w_in=gdn_w_in, gdn_conv=gdn_conv, gdn_a_log=gdn_a_log,
             gdn_dt_bias=gdn_dt_bias, gdn_norm_g=gdn_norm_g, gdn_w_out=gdn_w_out)
    m = dict(ada_w=m_ada_w, ada_b=m_ada_b, pre_g=m_pre_g, post_g=m_post_g, rel_bias=m_rel_bias, ab_w_in=m_ab_w_in, ab_w_out=m_ab_w_out,
             s5_a_re=m_s5_a_re, s5_a_im=m_s5_a_im, s5_log_dt=m_s5_log_dt, s5_b_re=m_s5_b_re, s5_b_im=m_s5_b_im, s5_c_re=m_s5_c_re,
             s5_c_im=m_s5_c_im, s5_d=m_s5_d, s5_glu_w=m_s5_glu_w, s5_glu_b=m_s5_glu_b, gdn_w_in=m_gdn_w_in, gdn_conv=m_gdn_conv,
             gdn_a_log=m_gdn_a_log, gdn_dt_bias=m_gdn_dt_bias, gdn_norm_g=m_gdn_norm_g, gdn_w_out=m_gdn_w_out)
    v = dict(ada_w=v_ada_w, ada_b=v_ada_b, pre_g=v_pre_g, post_g=v_post_g, rel_bias=v_rel_bias, ab_w_in=v_ab_w_in, ab_w_out=v_ab_w_out,
             s5_a_re=v_s5_a_re, s5_a_im=v_s5_a_im, s5_log_dt=v_s5_log_dt, s5_b_re=v_s5_b_re, s5_b_im=v_s5_b_im, s5_c_re=v_s5_c_re,
             s5_c_im=v_s5_c_im, s5_d=v_s5_d, s5_glu_w=v_s5_glu_w, s5_glu_b=v_s5_glu_b, gdn_w_in=v_gdn_w_in, gdn_conv=v_gdn_conv,
             gdn_a_log=v_gdn_a_log, gdn_dt_bias=v_gdn_dt_bias, gdn_norm_g=v_gdn_norm_g, gdn_w_out=v_gdn_w_out)
    ix, iy, ic = _place()
    me = 4 * ix + 2 * iy + ic
    chip = 2 * ix + iy
    n_cols = ada_w.shape[2]

    mine_first = _pack([c, gdn_conv], 32)
    first = _own_slot(_all_gather8("gather_c_conv", mine_first), mine_first, me)
    c_all = first[:, 0:8].reshape(8, D_MODEL)
    conv_full = first[0::2, 8:32].reshape(4, C_CONV, n_cols).transpose(1, 0, 2).reshape(C_CONV, 4 * n_cols)
    mine_mod = _mod_local(c_all, ada_w)
    modl = _own_slot(_all_gather8("gather_mod", mine_mod), mine_mod, me)
    mod = lax.dynamic_index_in_dim(modl[0::2], me, axis=2, keepdims=False)
    mod = mod.transpose(1, 0, 2).reshape(2, 4 * n_cols) + ada_b

    halves = []
    for name in _SHARDED:
        shard = w[name][0].astype(BF)
        h = shard.shape[0] // 2
        halves.append(lax.dynamic_slice_in_dim(shard, ic * h, h, axis=0))
    from_chips = _chip_exchange("gather_w_chips", halves, False)
    my_halves = [_own_slot(g, own, chip) for g, own in zip(from_chips, halves)]
    their_halves = _sibling_exchange("gather_w_sibling", my_halves)
    wd = {name: w[name] for name in _SMALL if name != "ada_b"}
    wd = {k: (a if k in ("pre_g", "post_g", "rel_bias") else a[0]) for k, a in wd.items()}
    wd["gdn_conv"] = conv_full
    for name, a, b in zip(_SHARDED, my_halves, their_halves):
        wd[name] = _full_from_halves(name, jnp.where(ic == 0, jnp.stack([a, b], 0), jnp.stack([b, a], 0)))

    loss_local, grad_x, grads, dmod = _local_step(x[0], loss_target[0], mod, wd)
    loss = lax.psum(loss_local, ("x", "y", "c"))

    small_shapes = [w[name].shape for name in _SMALL] + [(C_CONV, 4 * n_cols)]
    small_rows = -(-sum(_rows128(int(np.prod(s))) for s in small_shapes) // 8) * 8
    partial = _pack([dmod] + [grads[name] for name in _SMALL[1:]] + [grads["gdn_conv"]], small_rows)
    every = _own_slot(_all_gather8("gather_small_grads", partial), partial, me)
    g_small = _slot_sum("sum_small_grads", every)
    g_list = _unpack(g_small, small_shapes)
    d_small, m_small, v_small = _adamw("adamw_small", _pack([w[n] for n in _SMALL], small_rows), g_small,
                                       _pack([m[n] for n in _SMALL], small_rows), _pack([v[n] for n in _SMALL], small_rows))
    out_g = dict(zip(_SMALL, g_list[:-1]))
    out_d = dict(zip(_SMALL, _unpack(d_small, small_shapes[:-1])))
    out_m = dict(zip(_SMALL, _unpack(m_small, small_shapes[:-1])))
    out_v = dict(zip(_SMALL, _unpack(v_small, small_shapes[:-1])))

    def update(name, g2d):
        shp = w[name].shape
        two_d = lambda a: a.reshape(-1, shp[-1])
        d_, m_, v_ = _adamw("adamw_" + name, two_d(w[name]), g2d, two_d(m[name]), two_d(v[name]))
        out_g[name], out_d[name], out_m[name], out_v[name] = (a.reshape(shp) for a in (g2d, d_, m_, v_))

    update("gdn_conv", lax.dynamic_slice_in_dim(g_list[-1], chip * n_cols, n_cols, axis=1))

    dmod_all = every[:, 0:_rows128(2 * 3 * D_MODEL)].reshape(8, 2, 4, n_cols)
    dmod_cols = lax.dynamic_index_in_dim(dmod_all, chip, axis=2, keepdims=False).transpose(1, 0, 2)
    update("ada_w", _ada_w_grad(c_all, dmod_cols).reshape(-1, n_cols))

    mine, other = [], []
    for name in _SHARDED:
        sm = _shard_major(name, grads[name])
        h = sm.shape[1] // 2
        mine.append(lax.dynamic_slice_in_dim(sm, ic * h, h, axis=1))
        other.append(lax.dynamic_slice_in_dim(sm, (1 - ic) * h, h, axis=1))
    from_sibling = _sibling_exchange("reduce_sibling", other)
    chip_partials = []
    for name, a, b in zip(_SHARDED, mine, from_sibling):
        flat = lambda t: t.reshape(-1, t.shape[-1])
        chip_partials.append(_pair_sum("sum_sibling_" + name, flat(a), flat(b), BF).reshape(a.shape))
    from_all = _chip_exchange("reduce_chips", chip_partials, True)
    chip_1 = jnp.reshape(chip, (1,)).astype(jnp.int32)
    core_1 = jnp.reshape(ic, (1,)).astype(jnp.int32)
    reduced = [_chip_sum("sum_chips_" + name, t, p, chip_1) for name, t, p in zip(_SHARDED, from_all, chip_partials)]
    for name, g_mine, g_sib in zip(_SHARDED, reduced, _sibling_exchange("reduce_share", reduced)):
        shp = w[name].shape
        two_d = lambda a: a.reshape(-1, shp[-1])
        outs = _adamw_halves("adamw_" + name, two_d(w[name]), g_mine, g_sib, two_d(m[name]), two_d(v[name]), core_1)
        out_g[name], out_d[name], out_m[name], out_v[name] = (a.reshape(shp) for a in outs)

    return (loss, grad_x[None], *[out_g[n] for n in _WEIGHTS], *[out_d[n] for n in _WEIGHTS],
            *[out_m[n] for n in _WEIGHTS], *[out_v[n] for n in _WEIGHTS])
```

```python
import functools
import math

import numpy as np
import jax
import jax.numpy as jnp
from jax import lax
from jax.experimental import pallas as pl
from jax.experimental.pallas import tpu as pltpu

F32 = jnp.float32
BF = jnp.bfloat16
HI = lax.Precision.HIGHEST
MESH = pl.DeviceIdType.MESH

D_MODEL = 1024
EPS = 1e-6
A_HEADS, A_HD, A_WIDTH, A_BLOCK = 8, 64, 512, 128
DILATIONS = (1, 4, 16)
N_KEYS = 128
REL_BUCKETS, REL_MAX_DIST = 32, 2048
B_WIDTH, B_GROUP, B_GROUPS, B_STATE = 512, 16, 32, 64
S5_LANES = 512
S5_TILES = 4
S5_T = 256
C_HEADS, C_DK, C_CHUNK, C_CONV = 8, 128, 64, 4
QKV = 3072
C_IN_PAD = 4224
TM = 256
VMEM_LIMIT_BYTES = 56 * 1024 * 1024
ADAM_LR, ADAM_B1, ADAM_B2, ADAM_EPS, ADAM_WD, ADAM_STEP = 0.001, 0.9, 0.999, 1e-08, 0.01, 10
NEG = float(np.finfo(np.float32).min)


def _cp(*sem):
    return pltpu.CompilerParams(dimension_semantics=sem, vmem_limit_bytes=VMEM_LIMIT_BYTES)


def _bdot(a, b):
    return jnp.dot(a.astype(BF), b.astype(BF), preferred_element_type=F32)


def _bdot_nt(a, b):
    return lax.dot_general(a.astype(BF), b.astype(BF), (((1,), (1,)), ((), ())), preferred_element_type=F32)


def _bdot_tn(a, b):
    return lax.dot_general(a.astype(BF), b.astype(BF), (((0,), (0,)), ((), ())), preferred_element_type=F32)


def _hdot(a, b):
    return jnp.dot(a, b, precision=HI, preferred_element_type=F32)


def _bein(eq, a, b):
    return jnp.einsum(eq, a.astype(BF), b.astype(BF), preferred_element_type=F32)


def _hein(eq, a, b):
    return jnp.einsum(eq, a, b, precision=HI, preferred_element_type=F32)


def _row(tm, n):
    return pl.BlockSpec((tm, n), lambda i: (i, 0))


def _fix(shape):
    return pl.BlockSpec(shape, lambda i: (0,) * len(shape))


def _sds(*shape, dtype=F32):
    return jax.ShapeDtypeStruct(shape, dtype)


def _acc(ref, val):
    ref[...] += val


def _zero_at_first(refs, axis=0):
    @pl.when(pl.program_id(axis) == 0)
    def _():
        for r in refs:
            r[...] = jnp.zeros_like(r)


def _rms(x):
    return x * lax.rsqrt(jnp.mean(x * x, axis=-1, keepdims=True) + EPS)


def _pre_mod(x, g, scale, shift):
    return (_rms(x) * g) * (1.0 + scale) + shift


def _post_res(y, x, post_g, gate):
    return x + gate * (_rms(y) * post_g)


def _merge_gate(o1, o2, o3, l1, l2, l3, ga):
    m = jnp.maximum(jnp.maximum(l1, l2), l3)
    e1, e2, e3 = jnp.exp(l1 - m), jnp.exp(l2 - m), jnp.exp(l3 - m)
    inv = 1.0 / (e1 + e2 + e3)
    return ((e1 * inv) * o1 + (e2 * inv) * o2 + (e3 * inv) * o3) * jax.nn.silu(ga)


def _s5_gelu(ypre, u, d_skip):
    return jax.nn.gelu(ypre + d_skip * u)


def _s5_glu(yb, gl, gb):
    return yb * jax.nn.sigmoid(gl) * jax.nn.silu(gb)


def _l0_front(x, pre_g, scale, shift, w_in):
    s_len = x.shape[0]

    def body(x_ref, g_ref, sc_ref, sh_ref, w_ref, q_ref, k_ref, v_ref, u_ref, ga_ref, gb_ref, h_ref):
        hb = _pre_mod(x_ref[...], g_ref[...], sc_ref[...], sh_ref[...]).astype(BF)
        h_ref[...] = hb
        z = jnp.dot(hb, w_ref[...], preferred_element_type=F32)
        q_ref[...] = z[:, 0:512]
        k_ref[...] = z[:, 512:1024]
        v_ref[...] = z[:, 1024:1536]
        u_ref[...] = z[:, 1536:2048]
        ga_ref[...] = z[:, 2048:2560]
        gb_ref[...] = z[:, 2560:3072]

    vec = _fix((1, D_MODEL))
    return pl.pallas_call(
        body, name="l0_front", grid=(s_len // TM,),
        in_specs=[_row(TM, D_MODEL), vec, vec, vec, _fix((D_MODEL, 3072))],
        out_specs=[_row(TM, 512)] * 6 + [_row(TM, D_MODEL)],
        out_shape=[_sds(s_len, 512)] * 6 + [_sds(s_len, D_MODEL, dtype=BF)],
        compiler_params=_cp("arbitrary"),
    )(x, pre_g, scale, shift, w_in)


def _front_bwd(name, x, pre_g, scale, shift, w_in, dres, parts, widths):
    s_len = x.shape[0]
    n_in = sum(len(p) for p in parts)
    n_cols = sum(widths)

    def body(*refs):
        x_ref, g_ref, sc_ref, sh_ref, w_ref, dres_ref = refs[:6]
        part_refs = refs[6:6 + n_in]
        dz_ref, dx_ref, dg_ref, dsc_ref, dsh_ref = refs[6 + n_in:]
        _zero_at_first([dg_ref, dsc_ref, dsh_ref])
        _, vjp = jax.vjp(_pre_mod, x_ref[...], g_ref[...], sc_ref[...], sh_ref[...])
        dh = jnp.zeros((TM, D_MODEL), F32)
        col, at = 0, 0
        for grp, width in zip(parts, widths):
            dz = part_refs[at][...]
            for r in part_refs[at + 1:at + len(grp)]:
                dz = dz + r[...]
            at += len(grp)
            dzb = dz.astype(BF)
            dz_ref[:, col:col + width] = dzb
            dh = dh + lax.dot_general(dzb, w_ref[:, col:col + width], (((1,), (1,)), ((), ())), preferred_element_type=F32)
            col += width
        dx, dg, dsc, dsh = vjp(dh)
        dx_ref[...] = dx + dres_ref[...]
        _acc(dg_ref, dg)
        _acc(dsc_ref, dsc)
        _acc(dsh_ref, dsh)

    vec = _fix((1, D_MODEL))
    flat = [a for p in parts for a in p]
    return pl.pallas_call(
        body, name=name, grid=(s_len // TM,),
        in_specs=[_row(TM, D_MODEL), vec, vec, vec, _fix((D_MODEL, n_cols)), _row(TM, D_MODEL)]
        + [_row(TM, a.shape[1]) for a in flat],
        out_specs=[_row(TM, n_cols), _row(TM, D_MODEL), vec, vec, vec],
        out_shape=[_sds(s_len, n_cols, dtype=BF), _sds(s_len, D_MODEL), _sds(1, D_MODEL), _sds(1, D_MODEL), _sds(1, D_MODEL)],
        compiler_params=_cp("arbitrary"),
    )(x, pre_g, scale, shift, w_in, dres, *flat)


def _matmul_tn(name, a, b, tn):
    s_len, k_dim = a.shape
    n_dim = b.shape[1]
    ts = 512

    def body(a_ref, b_ref, o_ref):
        _zero_at_first([o_ref], axis=1)
        o_ref[...] += lax.dot_general(a_ref[...], b_ref[...], (((0,), (0,)), ((), ())), preferred_element_type=F32)

    return pl.pallas_call(
        body, name=name, grid=(n_dim // tn, s_len // ts),
        in_specs=[pl.BlockSpec((ts, k_dim), lambda j, i: (i, 0)), pl.BlockSpec((ts, tn), lambda j, i: (i, j))],
        out_specs=pl.BlockSpec((k_dim, tn), lambda j, i: (0, j)),
        out_shape=_sds(k_dim, n_dim),
        compiler_params=_cp("arbitrary", "arbitrary"),
    )(a, b)


def _t5_bucket_np(dist):
    dist = np.maximum(dist, 0)
    max_exact = REL_BUCKETS // 2
    large = max_exact + (np.log(np.maximum(dist, 1) / max_exact)
                         / math.log(REL_MAX_DIST / max_exact) * (REL_BUCKETS - max_exact)).astype(np.int32)
    large = np.minimum(large, REL_BUCKETS - 1)
    return np.where(dist < max_exact, dist, large).astype(np.int32)


A_VIEW = 16


def _block_order(dil):
    a = np.arange(A_BLOCK)
    return (A_VIEW // dil) * (a % (8 * dil)) + a // (8 * dil)


def _rel_table():
    out = []
    for d in DILATIONS:
        t = _block_order(d)
        out.append(t[:, None] + A_BLOCK - np.concatenate([t, t + A_BLOCK])[None, :])
    return np.stack(out, 0).astype(np.int32)


def _bucket_table():
    return np.stack([_t5_bucket_np(r * d) for r, d in zip(_rel_table(), DILATIONS)], 0)


def _attn_mask(rel, first):
    kj = lax.broadcasted_iota(jnp.int32, (A_BLOCK, 2 * A_BLOCK), 1)
    return (rel >= 0) & (rel <= N_KEYS) & (jnp.logical_not(first) | (kj >= A_BLOCK))


def _attn_width(dil):
    return 128 if dil == 16 else A_WIDTH


def _attn_specs(dil, nb, rev):
    rows, width = 8 * dil, _attn_width(dil)
    n_of = (lambda i: nb - 1 - i) if rev else (lambda i: i)
    cur = pl.BlockSpec((rows, A_VIEW, width), lambda g, i: (n_of(i), 0, g))
    prev = pl.BlockSpec((rows, A_VIEW, width), lambda g, i: (jnp.maximum(n_of(i) - 1, 0), 0, g))
    bias = pl.BlockSpec((width // 64, A_BLOCK, 2 * A_BLOCK), lambda g, i: (g, 0, 0))
    rel = pl.BlockSpec((A_BLOCK, 2 * A_BLOCK), lambda g, i: (0, 0))
    return cur, prev, bias, rel


def _gather_block(ref, dil, res, sl):
    return jnp.concatenate([ref[:, j * dil + res, sl] for j in range(A_VIEW // dil)], axis=0)


def _scatter_block(ref, dil, res, sl, val):
    rows = 8 * dil
    for j in range(A_VIEW // dil):
        ref[:, j * dil + res, sl] = val[j * rows:(j + 1) * rows]


def _attn_fwd(dil, q, k, v, bias, rel):
    n_rows = q.shape[0]
    nb = n_rows // (8 * dil)
    width = _attn_width(dil)
    scale = A_HD ** -0.5

    def body(q_ref, kp_ref, kc_ref, vp_ref, vc_ref, b_ref, rel_ref, o_ref, l_ref):
        mask = _attn_mask(rel_ref[...], pl.program_id(1) == 0)
        lane = lax.broadcasted_iota(jnp.int32, (1, 128), 1)
        for res in range(dil):
            for hp in range(width // 128):
                sl = slice(hp * 128, (hp + 1) * 128)
                qp = _gather_block(q_ref, dil, res, sl)
                kw = jnp.concatenate([_gather_block(kp_ref, dil, res, sl), _gather_block(kc_ref, dil, res, sl)], axis=0).astype(BF)
                vw = jnp.concatenate([_gather_block(vp_ref, dil, res, sl), _gather_block(vc_ref, dil, res, sl)], axis=0).astype(BF)
                outs, lses = [], []
                for j in range(2):
                    hm = (lane < 64) if j == 0 else (lane >= 64)
                    s = _bdot_nt(jnp.where(hm, qp, 0.0), kw) * scale
                    s = jnp.where(mask, s + b_ref[2 * hp + j], NEG)
                    m = jnp.max(s, axis=-1, keepdims=True)
                    p = jnp.exp(s - m)
                    den = jnp.sum(p, axis=-1, keepdims=True)
                    outs.append(_bdot(p, vw) / den)
                    lses.append(m + jnp.log(den))
                hm0 = lane < 64
                _scatter_block(o_ref, dil, res, sl, jnp.where(hm0, outs[0], outs[1]))
                _scatter_block(l_ref, dil, res, sl, jnp.where(hm0, lses[0], lses[1]))

    cur, prev, bias_spec, rel_spec = _attn_specs(dil, nb, False)
    return pl.pallas_call(
        body, name=f"attn_fwd_d{dil}", grid=(A_WIDTH // width, nb),
        in_specs=[cur, prev, cur, prev, cur, bias_spec, rel_spec],
        out_specs=[cur, cur],
        out_shape=[_sds(n_rows, A_VIEW, A_WIDTH)] * 2,
        compiler_params=_cp("arbitrary", "arbitrary"),
    )(q, k, k, v, v, bias, rel)


def _attn_bwd(dil, q, k, v, bias, rel, o, l, do, dl):
    n_rows = q.shape[0]
    nb = n_rows // (8 * dil)
    width = _attn_width(dil)
    scale = A_HD ** -0.5

    def body(q_ref, kp_ref, kc_ref, vp_ref, vc_ref, b_ref, rel_ref, o_ref, l_ref, do_ref, dl_ref,
             dq_ref, dk_ref, dv_ref, db_ref, ck_ref, cv_ref):
        _zero_at_first([db_ref, ck_ref, cv_ref], axis=1)
        mask = _attn_mask(rel_ref[...], pl.program_id(1) == nb - 1)
        lane = lax.broadcasted_iota(jnp.int32, (1, 128), 1)
        for res in range(dil):
            for hp in range(width // 128):
                sl = slice(hp * 128, (hp + 1) * 128)
                qp = _gather_block(q_ref, dil, res, sl)
                kw = jnp.concatenate([_gather_block(kp_ref, dil, res, sl), _gather_block(kc_ref, dil, res, sl)], axis=0).astype(BF)
                vw = jnp.concatenate([_gather_block(vp_ref, dil, res, sl), _gather_block(vc_ref, dil, res, sl)], axis=0).astype(BF)
                op, lp = _gather_block(o_ref, dil, res, sl), _gather_block(l_ref, dil, res, sl)
                dop, dlp = _gather_block(do_ref, dil, res, sl), _gather_block(dl_ref, dil, res, sl)
                dq_acc = jnp.zeros((A_BLOCK, 128), F32)
                dk_acc = jnp.zeros((2 * A_BLOCK, 128), F32)
                dv_acc = jnp.zeros((2 * A_BLOCK, 128), F32)
                for j in range(2):
                    hm = (lane < 64) if j == 0 else (lane >= 64)
                    qm = jnp.where(hm, qp, 0.0)
                    s = _bdot_nt(qm, kw) * scale
                    s = jnp.where(mask, s + b_ref[2 * hp + j], NEG)
                    lse = jnp.max(jnp.where(hm, lp, NEG), axis=-1, keepdims=True)
                    p = jnp.exp(s - lse)
                    do_h = jnp.where(hm, dop, 0.0)
                    dd = jnp.sum(do_h * op, axis=-1, keepdims=True)
                    dlse = jnp.sum(jnp.where(hm, dlp, 0.0), axis=-1, keepdims=True)
                    ds = p * (_bdot_nt(do_h, vw) - dd + dlse)
                    dv_acc = dv_acc + _bdot_tn(p, do_h)
                    dq_acc = dq_acc + jnp.where(hm, _bdot(ds, kw), 0.0) * scale
                    dk_acc = dk_acc + _bdot_tn(ds, qm) * scale
                    db_ref[2 * hp + j] += ds
                _scatter_block(dq_ref, dil, res, sl, dq_acc)
                _scatter_block(dk_ref, dil, res, sl, dk_acc[A_BLOCK:] + ck_ref[res, :, sl])
                _scatter_block(dv_ref, dil, res, sl, dv_acc[A_BLOCK:] + cv_ref[res, :, sl])
                ck_ref[res, :, sl] = dk_acc[:A_BLOCK]
                cv_ref[res, :, sl] = dv_acc[:A_BLOCK]

    cur, prev, bias_spec, rel_spec = _attn_specs(dil, nb, True)
    return pl.pallas_call(
        body, name=f"attn_bwd_d{dil}", grid=(A_WIDTH // width, nb),
        in_specs=[cur, prev, cur, prev, cur, bias_spec, rel_spec, cur, cur, cur, cur],
        out_specs=[cur, cur, cur, bias_spec],
        out_shape=[_sds(n_rows, A_VIEW, A_WIDTH)] * 3 + [_sds(A_HEADS, A_BLOCK, 2 * A_BLOCK)],
        scratch_shapes=[pltpu.VMEM((dil, A_BLOCK, width), F32)] * 2,
        compiler_params=_cp("arbitrary", "arbitrary"),
    )(q, k, k, v, v, bias, rel, o, l, do, dl)


def _attn_bias(rel_bias, table):
    def body(rb_ref, t_ref, *o_refs):
        for c in range(3):
            t = t_ref[c]
            acc = [jnp.zeros((A_BLOCK, 2 * A_BLOCK), F32) for _ in range(A_HEADS)]
            for b in range(REL_BUCKETS):
                hit = t == b
                acc = [jnp.where(hit, rb_ref[b, h], acc[h]) for h in range(A_HEADS)]
            for h in range(A_HEADS):
                o_refs[c][h] = acc[h]

    return pl.pallas_call(body, name="attn_bias", out_shape=[_sds(A_HEADS, A_BLOCK, 2 * A_BLOCK)] * 3,
                          in_specs=[pl.BlockSpec(memory_space=pltpu.SMEM), pl.BlockSpec(memory_space=pltpu.VMEM)],
                          compiler_params=pltpu.CompilerParams(vmem_limit_bytes=VMEM_LIMIT_BYTES))(rel_bias, table)


def _rel_bias_grad(dbs, idx_rows):
    n = A_BLOCK * 2 * A_BLOCK

    def body(d0_ref, d1_ref, d2_ref, idx_ref, o_ref):
        bucket = lax.broadcasted_iota(jnp.int32, (REL_BUCKETS, n), 0).astype(F32)
        acc = jnp.zeros((A_HEADS, REL_BUCKETS), F32)
        for c, db_ref in enumerate((d0_ref, d1_ref, d2_ref)):
            onehot = (idx_ref[c:c + 1, :] == bucket).astype(F32)
            acc = acc + lax.dot_general(db_ref[...], onehot, (((1,), (1,)), ((), ())), precision=HI, preferred_element_type=F32)
        o_ref[...] = acc

    return pl.pallas_call(body, name="rel_bias_grad", out_shape=_sds(A_HEADS, REL_BUCKETS),
                          compiler_params=pltpu.CompilerParams(vmem_limit_bytes=VMEM_LIMIT_BYTES))(
                              *[d.reshape(A_HEADS, n) for d in dbs], idx_rows)


def _s5_param_fn(a_re, a_im, log_dt, bt_re, bt_im):
    dt = jnp.exp(log_dt)
    mag = jnp.exp(dt * a_re)
    abar_r, abar_i = mag * jnp.cos(dt * a_im), mag * jnp.sin(dt * a_im)
    den = a_re * a_re + a_im * a_im
    fr = ((abar_r - 1.0) * a_re + abar_i * a_im) / den
    fi = (abar_i * a_re - (abar_r - 1.0) * a_im) / den
    row = lax.broadcasted_iota(jnp.int32, (B_WIDTH, B_GROUPS), 0)
    grp = lax.broadcasted_iota(jnp.int32, (B_WIDTH, B_GROUPS), 1)
    expand = ((row // B_GROUP) == grp).astype(F32)
    fr_e, fi_e = _hdot(expand, fr), _hdot(expand, fi)
    return abar_r, abar_i, fr_e * bt_re - fi_e * bt_im, fr_e * bt_im + fi_e * bt_re


def _s5_params(a_re, a_im, log_dt, bt_re, bt_im):
    def body(ar, ai, ld, br, bi, o1, o2, o3, o4):
        o1[...], o2[...], o3[...], o4[...] = _s5_param_fn(ar[...], ai[...], ld[...], br[...], bi[...])

    return pl.pallas_call(body, name="s5_params",
                          out_shape=[_sds(B_GROUPS, B_STATE)] * 2 + [_sds(B_WIDTH, B_STATE)] * 2)(a_re, a_im, log_dt, bt_re, bt_im)


def _s5_params_bwd(a_re, a_im, log_dt, bt_re, bt_im, d1, d2, d3, d4):
    def body(ar, ai, ld, br, bi, c1, c2, c3, c4, o1, o2, o3, o4, o5):
        _, vjp = jax.vjp(_s5_param_fn, ar[...], ai[...], ld[...], br[...], bi[...])
        o1[...], o2[...], o3[...], o4[...], o5[...] = vjp((c1[...], c2[...], c3[...], c4[...]))

    return pl.pallas_call(body, name="s5_params_bwd",
                          out_shape=[_sds(B_GROUPS, B_STATE)] * 2 + [_sds(B_GROUPS, 1)] + [_sds(B_WIDTH, B_STATE)] * 2,
                          )(a_re, a_im, log_dt, bt_re, bt_im, d1, d2, d3, d4)


def _cscan(br, bi, ar, ai, reverse):
    t_len = br.shape[0]
    rows = lax.broadcasted_iota(jnp.int32, br.shape, 0)
    xr, xi, cr, ci = br, bi, ar, ai
    k = 1
    while k < t_len:
        if reverse:
            keep, shift = rows < t_len - k, t_len - k
        else:
            keep, shift = rows >= k, k
        sr = jnp.where(keep, pltpu.roll(xr, shift, 0), 0.0)
        si = jnp.where(keep, pltpu.roll(xi, shift, 0), 0.0)
        xr, xi = xr + cr * sr - ci * si, xi + cr * si + ci * sr
        cr, ci = cr * cr - ci * ci, 2.0 * cr * ci
        k *= 2
    return xr, xi


def _pick_row(x, r):
    rows = lax.broadcasted_iota(jnp.int32, x.shape, 0)
    return jnp.sum(jnp.where(rows == r, x, 0.0), axis=0, keepdims=True)


def _s5_tile_specs(n_t, rev):
    t_of = (lambda i: n_t - 1 - i) if rev else (lambda i: i)
    u_spec = pl.BlockSpec((S5_T, 128), lambda j, i: (t_of(i), j))
    x_spec = pl.BlockSpec((S5_T, S5_LANES), lambda j, i: (t_of(i), j))
    b_spec = pl.BlockSpec((1, 128, S5_LANES), lambda j, i: (j, 0, 0))
    c_spec = pl.BlockSpec((1, S5_LANES, 128), lambda j, i: (j, 0, 0))
    a_spec = pl.BlockSpec((1, S5_LANES), lambda j, i: (0, j))
    return u_spec, x_spec, b_spec, c_spec, a_spec


def _s5_scan_fwd(u, btr, bti, ctr, cti, abr, abi):
    s_len = u.shape[0]
    n_t = s_len // S5_T

    def body(u_ref, btr_ref, bti_ref, ctr_ref, cti_ref, ar_ref, ai_ref, xr_ref, xi_ref, y_ref, car, cai, pwr, pwi):
        ar, ai = ar_ref[...], ai_ref[...]

        @pl.when(pl.program_id(1) == 0)
        def _():
            car[...] = jnp.zeros_like(car)
            cai[...] = jnp.zeros_like(cai)
            rows = lax.broadcasted_iota(jnp.int32, (S5_T, S5_LANES), 0)
            pwr[...], pwi[...] = _cscan(jnp.where(rows == 0, ar, 0.0), jnp.where(rows == 0, ai, 0.0), ar, ai, False)

        ub = u_ref[...]
        xr, xi = _cscan(_bdot(ub, btr_ref[0]), _bdot(ub, bti_ref[0]), ar, ai, False)
        cr, ci = car[...], cai[...]
        pr, pi_ = pwr[...], pwi[...]
        xr, xi = xr + pr * cr - pi_ * ci, xi + pr * ci + pi_ * cr
        xr_ref[...] = xr
        xi_ref[...] = xi
        car[...] = _pick_row(xr, S5_T - 1)
        cai[...] = _pick_row(xi, S5_T - 1)
        y_ref[...] = _bdot(xr, ctr_ref[0]) - _bdot(xi, cti_ref[0])

    u_spec, x_spec, b_spec, c_spec, a_spec = _s5_tile_specs(n_t, False)
    return pl.pallas_call(
        body, name="s5_scan_fwd", grid=(S5_TILES, n_t),
        in_specs=[u_spec, b_spec, b_spec, c_spec, c_spec, a_spec, a_spec],
        out_specs=[x_spec, x_spec, u_spec],
        out_shape=[_sds(s_len, S5_TILES * S5_LANES)] * 2 + [_sds(s_len, B_WIDTH)],
        scratch_shapes=[pltpu.VMEM((1, S5_LANES), F32)] * 2 + [pltpu.VMEM((S5_T, S5_LANES), F32)] * 2,
        compiler_params=_cp("arbitrary", "arbitrary"),
    )(u, btr, bti, ctr, cti, abr, abi)


def _s5_scan_bwd(dy, xr, xi, u, btr, bti, ctr, cti, abr, abi):
    s_len = u.shape[0]
    n_t = s_len // S5_T

    def body(dy_ref, xr_ref, xi_ref, xrp_ref, xip_ref, u_ref, btr_ref, bti_ref, ctr_ref, cti_ref, ar_ref, ai_ref,
             du_ref, dbtr_ref, dbti_ref, dctr_ref, dcti_ref, dar_ref, dai_ref, car, cai, pwr, pwi):
        ar, ai = ar_ref[...], ai_ref[...]
        i = pl.program_id(1)
        rows = lax.broadcasted_iota(jnp.int32, (S5_T, S5_LANES), 0)

        @pl.when(i == 0)
        def _():
            for r in (car, cai, dbtr_ref, dbti_ref, dctr_ref, dcti_ref, dar_ref, dai_ref):
                r[...] = jnp.zeros_like(r)
            last = rows == S5_T - 1
            pwr[...], pwi[...] = _cscan(jnp.where(last, ar, 0.0), jnp.where(last, -ai, 0.0), ar, -ai, True)

        dyb = dy_ref[...]
        xr_b, xi_b, ub = xr_ref[...], xi_ref[...], u_ref[...]
        dctr_ref[0] += _bdot_tn(xr_b, dyb)
        dcti_ref[0] -= _bdot_tn(xi_b, dyb)
        gr, gi = _cscan(_bdot_nt(dyb, ctr_ref[0]), -_bdot_nt(dyb, cti_ref[0]), ar, -ai, True)
        cr, ci = car[...], cai[...]
        pr, pi_ = pwr[...], pwi[...]
        gr, gi = gr + pr * cr - pi_ * ci, gi + pr * ci + pi_ * cr
        car[...] = _pick_row(gr, 0)
        cai[...] = _pick_row(gi, 0)
        du_ref[...] = _bdot_nt(gr, btr_ref[0]) + _bdot_nt(gi, bti_ref[0])
        dbtr_ref[0] += _bdot_tn(ub, gr)
        dbti_ref[0] += _bdot_tn(ub, gi)
        has_prev = (i < n_t - 1).astype(F32)
        hr = _pick_row(xrp_ref[...], 7) * has_prev
        hi = _pick_row(xip_ref[...], 7) * has_prev
        xpr = jnp.where(rows == 0, hr, pltpu.roll(xr_b, 1, 0))
        xpi = jnp.where(rows == 0, hi, pltpu.roll(xi_b, 1, 0))
        dar_ref[...] += jnp.sum(gr * xpr + gi * xpi, axis=0, keepdims=True)
        dai_ref[...] += jnp.sum(gi * xpr - gr * xpi, axis=0, keepdims=True)

    u_spec, x_spec, b_spec, c_spec, a_spec = _s5_tile_specs(n_t, True)
    halo = pl.BlockSpec((8, S5_LANES), lambda j, i: (jnp.maximum((n_t - 1 - i) * (S5_T // 8) - 1, 0), j))
    return pl.pallas_call(
        body, name="s5_scan_bwd", grid=(S5_TILES, n_t),
        in_specs=[u_spec, x_spec, x_spec, halo, halo, u_spec, b_spec, b_spec, c_spec, c_spec, a_spec, a_spec],
        out_specs=[u_spec, b_spec, b_spec, c_spec, c_spec, a_spec, a_spec],
        out_shape=[_sds(s_len, B_WIDTH)] + [_sds(S5_TILES, 128, S5_LANES)] * 2 + [_sds(S5_TILES, S5_LANES, 128)] * 2
        + [_sds(1, S5_TILES * S5_LANES)] * 2,
        scratch_shapes=[pltpu.VMEM((1, S5_LANES), F32)] * 2 + [pltpu.VMEM((S5_T, S5_LANES), F32)] * 2,
        compiler_params=_cp("arbitrary", "arbitrary"),
    )(dy, xr, xi, xr, xi, u, btr, bti, ctr, cti, abr, abi)


def _blockdiag_b(bbar_t):
    blocks = bbar_t.reshape(S5_TILES, 8, B_GROUP, B_STATE)
    return jnp.einsum('jgmp,gh->jgmhp', blocks, jnp.eye(8, dtype=F32)).reshape(S5_TILES, 128, S5_LANES)


def _blockdiag_b_t(d):
    return jnp.einsum('jgmgp->jgmp', d.reshape(S5_TILES, 8, B_GROUP, 8, B_STATE)).reshape(B_WIDTH, B_STATE)


def _blockdiag_c(c):
    blocks = c.reshape(S5_TILES, 8, B_GROUP, B_STATE)
    return jnp.einsum('jgmp,gh->jhpgm', blocks, jnp.eye(8, dtype=F32)).reshape(S5_TILES, S5_LANES, 128)


def _blockdiag_c_t(d):
    return jnp.einsum('jgpgm->jgmp', d.reshape(S5_TILES, 8, B_STATE, 8, B_GROUP)).reshape(B_GROUPS, B_GROUP, B_STATE)


def _l0_out(os, ls, ga, gb, ypre, u, x, d_skip, glu_w, glu_b, w_out, post_g, gate):
    s_len = x.shape[0]

    def body(o0, o1, o2, l0, l1, l2, ga_ref, gb_ref, yp_ref, u_ref, x_ref, d_ref, gw_ref, gbias_ref, w_ref, pg_ref, gt_ref, x1_ref, y_ref):
        oa = _merge_gate(o0[...], o1[...], o2[...], l0[...], l1[...], l2[...], ga_ref[...])
        yb = _s5_gelu(yp_ref[...], u_ref[...], d_ref[...])
        ob = _s5_glu(yb, _bdot(yb, gw_ref[...]) + gbias_ref[...], gb_ref[...])
        y = _bdot(oa, w_ref[0:512, :]) + _bdot(ob, w_ref[512:1024, :])
        y_ref[...] = y
        x1_ref[...] = _post_res(y, x_ref[...], pg_ref[...], gt_ref[...])

    vec, half = _fix((1, D_MODEL)), _fix((1, 512))
    return pl.pallas_call(
        body, name="l0_out", grid=(s_len // TM,),
        in_specs=[_row(TM, 512)] * 10 + [_row(TM, D_MODEL), half, _fix((512, 512)), half, _fix((D_MODEL, D_MODEL)), vec, vec],
        out_specs=[_row(TM, D_MODEL)] * 2,
        out_shape=[_sds(s_len, D_MODEL)] * 2,
        compiler_params=_cp("arbitrary"),
    )(*os, *ls, ga, gb, ypre, u, x, d_skip, glu_w, glu_b, w_out, post_g, gate)


def _l0_out_bwd(os, ls, ga, gb, ypre, u, x, y, d_skip, glu_w, glu_b, w_out, post_g, gate, dx1):
    s_len = x.shape[0]

    def body(o0, o1, o2, l0, l1, l2, ga_ref, gb_ref, yp_ref, u_ref, x_ref, y_ref, d_ref, gw_ref, gbias_ref, w_ref, pg_ref, gt_ref, dx1_ref,
             do0, do1, do2, dl0, dl1, dl2, dga_ref, dgb_ref, dyp_ref, du_ref, dd_ref, dgw_ref, dgbias_ref, dw_ref, dpg_ref, dgt_ref):
        _zero_at_first([dd_ref, dgw_ref, dgbias_ref, dw_ref, dpg_ref, dgt_ref])
        _, vjp2 = jax.vjp(_post_res, y_ref[...], x_ref[...], pg_ref[...], gt_ref[...])
        dy, _, dpg, dgt = vjp2(dx1_ref[...])
        _acc(dpg_ref, dpg)
        _acc(dgt_ref, dgt)
        oa, vjp_a = jax.vjp(_merge_gate, o0[...], o1[...], o2[...], l0[...], l1[...], l2[...], ga_ref[...])
        yb, vjp_g = jax.vjp(_s5_gelu, yp_ref[...], u_ref[...], d_ref[...])
        gl = _bdot(yb, gw_ref[...]) + gbias_ref[...]
        ob, vjp_b = jax.vjp(_s5_glu, yb, gl, gb_ref[...])
        dw_ref[0:512, :] += _bdot_tn(oa, dy)
        dw_ref[512:1024, :] += _bdot_tn(ob, dy)
        d1, d2, d3, e1, e2, e3, dga = vjp_a(_bdot_nt(dy, w_ref[0:512, :]))
        do0[...], do1[...], do2[...] = d1, d2, d3
        dl0[...], dl1[...], dl2[...] = e1, e2, e3
        dga_ref[...] = dga
        dyb, dgl, dgb = vjp_b(_bdot_nt(dy, w_ref[512:1024, :]))
        dgb_ref[...] = dgb
        dgw_ref[...] += _bdot_tn(yb, dgl)
        _acc(dgbias_ref, jnp.sum(dgl, axis=0, keepdims=True))
        dyp, du, dd = vjp_g(dyb + _bdot_nt(dgl, gw_ref[...]))
        dyp_ref[...] = dyp
        du_ref[...] = du
        _acc(dd_ref, dd)

    vec, half = _fix((1, D_MODEL)), _fix((1, 512))
    r5, r10 = _row(TM, 512), _row(TM, D_MODEL)
    return pl.pallas_call(
        body, name="l0_out_bwd", grid=(s_len // TM,),
        in_specs=[r5] * 10 + [r10, r10, half, _fix((512, 512)), half, _fix((D_MODEL, D_MODEL)), vec, vec, r10],
        out_specs=[r5] * 10 + [half, _fix((512, 512)), half, _fix((D_MODEL, D_MODEL)), vec, vec],
        out_shape=[_sds(s_len, 512)] * 10
        + [_sds(1, 512), _sds(512, 512), _sds(1, 512), _sds(D_MODEL, D_MODEL), _sds(1, D_MODEL), _sds(1, D_MODEL)],
        compiler_params=_cp("arbitrary"),
    )(*os, *ls, ga, gb, ypre, u, x, y, d_skip, glu_w, glu_b, w_out, post_g, gate, dx1)


def _l1_front(x, pre_g, scale, shift, w_in):
    s_len = x.shape[0]

    def body(x_ref, g_ref, sc_ref, sh_ref, w_ref, raw_ref, gate_ref, ba_ref, h_ref):
        hb = _pre_mod(x_ref[...], g_ref[...], sc_ref[...], sh_ref[...]).astype(BF)
        h_ref[...] = hb
        z = jnp.dot(hb, w_ref[...], preferred_element_type=F32)
        raw_ref[...] = z[:, 0:QKV]
        gate_ref[...] = z[:, QKV:QKV + 1024]
        ba_ref[...] = z[:, QKV + 1024:C_IN_PAD]

    vec = _fix((1, D_MODEL))
    return pl.pallas_call(
        body, name="l1_front", grid=(s_len // TM,),
        in_specs=[_row(TM, D_MODEL), vec, vec, vec, _fix((D_MODEL, C_IN_PAD))],
        out_specs=[_row(TM, QKV), _row(TM, 1024), _row(TM, 128), _row(TM, D_MODEL)],
        out_shape=[_sds(s_len, QKV), _sds(s_len, 1024), _sds(s_len, 128), _sds(s_len, D_MODEL, dtype=BF)],
        compiler_params=_cp("arbitrary"),
    )(x, pre_g, scale, shift, w_in)


def _bg_fn(ba, alog_row, dtb_row):
    lane = lax.broadcasted_iota(jnp.int32, (1, 128), 1)
    g = -jnp.exp(alog_row) * jax.nn.softplus(ba + dtb_row)
    return jnp.where(lane < C_HEADS, jax.nn.sigmoid(ba), jnp.where(lane < 2 * C_HEADS, g, 0.0))


def _act_q(c):
    q = jax.nn.silu(c)
    return q * lax.rsqrt(jnp.sum(q * q, axis=-1, keepdims=True) + EPS) * (C_DK ** -0.5)


def _act_k(c):
    k = jax.nn.silu(c)
    return k * lax.rsqrt(jnp.sum(k * k, axis=-1, keepdims=True) + EPS)


def _act_of(s):
    return _act_q if s < 8 else (_act_k if s < 16 else jax.nn.silu)


def _gdn_prep(raw, ba, conv_w, alog_row, dtb_row):
    s_len = raw.shape[0]

    def body(raw_ref, halo_ref, ba_ref, w_ref, al_ref, dt_ref, qkv_ref, bg_ref):
        bg_ref[...] = _bg_fn(ba_ref[...], al_ref[...], dt_ref[...])
        has_prev = (pl.program_id(0) > 0).astype(F32)
        for s in range(24):
            sl = slice(s * 128, (s + 1) * 128)
            cat = jnp.concatenate([halo_ref[:, sl] * has_prev, raw_ref[:, sl]], axis=0)
            conv = w_ref[3:4, sl] * cat[8:]
            for j in range(3):
                conv = conv + w_ref[j:j + 1, sl] * pltpu.roll(cat, 3 - j, 0)[8:]
            qkv_ref[:, sl] = _act_of(s)(conv)

    halo = pl.BlockSpec((8, QKV), lambda i: (jnp.maximum(i * (TM // 8) - 1, 0), 0))
    row128 = _fix((1, 128))
    return pl.pallas_call(
        body, name="gdn_prep", grid=(s_len // TM,),
        in_specs=[_row(TM, QKV), halo, _row(TM, 128), _fix((C_CONV, QKV)), row128, row128],
        out_specs=[_row(TM, QKV), _row(TM, 128)],
        out_shape=[_sds(s_len, QKV), _sds(s_len, 128)],
        compiler_params=_cp("arbitrary"),
    )(raw, raw, ba, conv_w, alog_row, dtb_row)


def _gdn_prep_bwd(raw, ba, conv_w, alog_row, dtb_row, dq, dk, dv, dbg):
    s_len = raw.shape[0]
    n_tiles = s_len // TM
    ext = TM + 8

    def body(raw_ref, prev_ref, next_ref, ba_ref, w_ref, al_ref, dt_ref, dq_ref, dqn_ref, dk_ref, dkn_ref, dv_ref, dvn_ref, dbg_ref,
             draw_ref, dba_ref, dw_ref, dal_ref, ddt_ref):
        _zero_at_first([dw_ref, dal_ref, ddt_ref])
        i = pl.program_id(0)
        _, vjp_bg = jax.vjp(_bg_fn, ba_ref[...], al_ref[...], dt_ref[...])
        dba, dal, ddt = vjp_bg(dbg_ref[...])
        dba_ref[...] = dba
        _acc(dal_ref, dal)
        _acc(ddt_ref, ddt)
        has_prev = (i > 0).astype(F32)
        has_next = (i < n_tiles - 1).astype(F32)
        ct_refs = ((dq_ref, dqn_ref), (dk_ref, dkn_ref), (dv_ref, dvn_ref))
        for s in range(24):
            sl = slice(s * 128, (s + 1) * 128)
            hl = slice((s % 8) * 128, (s % 8 + 1) * 128)
            tile_ref, nxt_ref = ct_refs[s // 8]
            cat = jnp.concatenate([prev_ref[:, sl] * has_prev, raw_ref[:, sl], next_ref[:, sl] * has_next], axis=0)
            shifted = [pltpu.roll(cat, 3 - j, 0)[8:] for j in range(3)] + [cat[8:]]
            conv = w_ref[3:4, sl] * shifted[3]
            for j in range(3):
                conv = conv + w_ref[j:j + 1, sl] * shifted[j]
            ct = jnp.concatenate([tile_ref[:, hl], nxt_ref[:, hl] * has_next], axis=0)
            _, vjp_act = jax.vjp(_act_of(s), conv)
            dconv, = vjp_act(ct)
            draw = w_ref[3:4, sl] * dconv[:TM]
            for j in range(3):
                draw = draw + w_ref[j:j + 1, sl] * pltpu.roll(dconv, ext - (3 - j), 0)[:TM]
            draw_ref[:, sl] = draw
            for j in range(4):
                dw_ref[j:j + 1, sl] += jnp.sum(dconv[:TM] * shifted[j][:TM], axis=0, keepdims=True)

    prev = pl.BlockSpec((8, QKV), lambda i: (jnp.maximum(i * (TM // 8) - 1, 0), 0))
    nxt = lambda n: pl.BlockSpec((8, n), lambda i: (jnp.minimum((i + 1) * (TM // 8), s_len // 8 - 1), 0))
    row128 = _fix((1, 128))
    ct_specs = [_row(TM, 1024), nxt(1024)] * 3
    return pl.pallas_call(
        body, name="gdn_prep_bwd", grid=(n_tiles,),
        in_specs=[_row(TM, QKV), prev, nxt(QKV), _row(TM, 128), _fix((C_CONV, QKV)), row128, row128] + ct_specs + [_row(TM, 128)],
        out_specs=[_row(TM, QKV), _row(TM, 128), _fix((C_CONV, QKV)), row128, row128],
        out_shape=[_sds(s_len, QKV), _sds(s_len, 128), _sds(C_CONV, QKV), _sds(1, 128), _sds(1, 128)],
        compiler_params=_cp("arbitrary"),
    )(raw, raw, raw, ba, conv_w, alog_row, dtb_row, dq, dq, dk, dk, dv, dv, dbg)


def _tein(eq, a, b):
    return jnp.einsum(eq, a, b, precision=lax.Precision.HIGH, preferred_element_type=F32)


def _unit_lower_inverse(lower):
    ri = lax.broadcasted_iota(jnp.int32, (C_CHUNK, C_CHUNK), 0)
    ci = lax.broadcasted_iota(jnp.int32, (C_CHUNK, C_CHUNK), 1)
    p_mat = -lower
    inv = (ri == ci).astype(F32)[None] + p_mat
    for _ in range(5):
        p_mat = _tein('hij,hjk->hik', p_mat, p_mat)
        inv = inv + _tein('hij,hjk->hik', inv, p_mat)
    return inv


@jax.custom_vjp
def _known_inverse(lower, inv):
    return inv


def _known_inverse_fwd(lower, inv):
    return inv, inv


def _known_inverse_bwd(inv, d_inv):
    d_lower = -_tein('hik,hjk->hij', _tein('hji,hjk->hik', inv, d_inv), inv)
    return d_lower, jnp.zeros_like(inv)


_known_inverse.defvjp(_known_inverse_fwd, _known_inverse_bwd)


def _gdn_chunk(q, k, v, bg, state, inv_known=None):
    lane = lax.broadcasted_iota(jnp.int32, (1, 128), 1)
    ri = lax.broadcasted_iota(jnp.int32, (C_CHUNK, C_CHUNK), 0)
    ci = lax.broadcasted_iota(jnp.int32, (C_CHUNK, C_CHUNK), 1)
    gc_t = _hdot((ri >= ci).astype(F32), bg)
    beta = jnp.stack([jnp.sum(jnp.where(lane == h, bg, 0.0), axis=-1, keepdims=True) for h in range(C_HEADS)], axis=0)
    gc = jnp.stack([jnp.sum(jnp.where(lane == C_HEADS + h, gc_t, 0.0), axis=-1, keepdims=True) for h in range(C_HEADS)], axis=0)
    gc_rows = gc_t.T
    row_id = lax.broadcasted_iota(jnp.int32, (128, C_CHUNK), 0)
    gcj = jnp.stack([jnp.sum(jnp.where(row_id == C_HEADS + h, gc_rows, 0.0), axis=0, keepdims=True) for h in range(C_HEADS)], axis=0)
    tril, strict = (ri >= ci)[None], (ri > ci)[None]
    decay = jnp.exp(jnp.where(tril, gc - gcj, -1e30))
    kb = k * beta
    lower = jnp.where(strict, _bein('hid,hjd->hij', kb, k) * decay, 0.0)
    inv = _unit_lower_inverse(lower) if inv_known is None else _known_inverse(lower, inv_known)
    egc = jnp.exp(gc)
    u_c = _tein('hij,hjd->hid', inv, v * beta)
    w_c = _tein('hij,hjd->hid', inv, kb * egc)
    aqk = _bein('hid,hjd->hij', q, k) * decay
    rowi = lax.broadcasted_iota(jnp.int32, (1, C_CHUNK, 1), 1)
    g_last = jnp.sum(jnp.where(rowi == C_CHUNK - 1, gc, 0.0), axis=1, keepdims=True)
    kd = k * jnp.exp(g_last - gc)
    v_new = u_c - _bein('hik,hkv->hiv', w_c, state)
    o = _bein('hik,hkv->hiv', q * egc, state) + _bein('hij,hjv->hiv', aqk, v_new)
    return o, state * jnp.exp(g_last) + _bein('hik,hiv->hkv', kd, v_new), inv


def _heads(ref):
    return jnp.stack([ref[:, h * C_DK:(h + 1) * C_DK] for h in range(C_HEADS)], axis=0)


def _gdn_fwd(qkv, bg):
    s_len = qkv.shape[0]
    n_c = s_len // C_CHUNK

    def body(q_ref, k_ref, v_ref, bg_ref, o_ref, ss_ref, inv_ref, st_ref):
        _zero_at_first([st_ref])
        s0 = st_ref[...]
        ss_ref[0] = s0
        o, s2, inv = _gdn_chunk(_heads(q_ref), _heads(k_ref), _heads(v_ref), bg_ref[...], s0)
        st_ref[...] = s2
        inv_ref[0] = inv
        for h in range(C_HEADS):
            o_ref[:, h * C_DK:(h + 1) * C_DK] = o[h]

    col = lambda c: pl.BlockSpec((C_CHUNK, 1024), lambda i: (i, c))
    return pl.pallas_call(
        body, name="gdn_fwd", grid=(n_c,),
        in_specs=[col(0), col(1), col(2), _row(C_CHUNK, 128)],
        out_specs=[_row(C_CHUNK, 1024), pl.BlockSpec((1, C_HEADS, C_DK, C_DK), lambda i: (i, 0, 0, 0)),
                   pl.BlockSpec((1, C_HEADS, C_CHUNK, C_CHUNK), lambda i: (i, 0, 0, 0))],
        out_shape=[_sds(s_len, 1024), _sds(n_c, C_HEADS, C_DK, C_DK), _sds(n_c, C_HEADS, C_CHUNK, C_CHUNK)],
        scratch_shapes=[pltpu.VMEM((C_HEADS, C_DK, C_DK), F32)],
        compiler_params=_cp("arbitrary"),
    )(qkv, qkv, qkv, bg)


def _gdn_bwd(qkv, bg, states, invs, do):
    s_len = qkv.shape[0]
    n_c = s_len // C_CHUNK

    def body(q_ref, k_ref, v_ref, bg_ref, ss_ref, inv_ref, do_ref, dq_ref, dk_ref, dv_ref, dbg_ref, ds_ref):
        _zero_at_first([ds_ref])
        inv_known = inv_ref[0]
        chunk = lambda q, k, v, bg_, st: _gdn_chunk(q, k, v, bg_, st, inv_known)[:2]
        _, vjp = jax.vjp(chunk, _heads(q_ref), _heads(k_ref), _heads(v_ref), bg_ref[...], ss_ref[0])
        dq, dk, dv, dbg, ds = vjp((_heads(do_ref), ds_ref[...]))
        ds_ref[...] = ds
        dbg_ref[...] = dbg
        for h in range(C_HEADS):
            sl = slice(h * C_DK, (h + 1) * C_DK)
            dq_ref[:, sl], dk_ref[:, sl], dv_ref[:, sl] = dq[h], dk[h], dv[h]

    rev = lambda i: n_c - 1 - i
    col = lambda c: pl.BlockSpec((C_CHUNK, 1024), lambda i: (rev(i), c))
    row128 = pl.BlockSpec((C_CHUNK, 128), lambda i: (rev(i), 0))
    per_chunk = lambda n: pl.BlockSpec((1, C_HEADS, n, n), lambda i: (rev(i), 0, 0, 0))
    return pl.pallas_call(
        body, name="gdn_bwd", grid=(n_c,),
        in_specs=[col(0), col(1), col(2), row128, per_chunk(C_DK), per_chunk(C_CHUNK), col(0)],
        out_specs=[col(0), col(0), col(0), row128],
        out_shape=[_sds(s_len, 1024)] * 3 + [_sds(s_len, 128)],
        scratch_shapes=[pltpu.VMEM((C_HEADS, C_DK, C_DK), F32)],
        compiler_params=_cp("arbitrary"),
    )(qkv, qkv, qkv, bg, states, invs, do)


def _head_norm_gate(o, gate, norm_g):
    return (_rms(o) * norm_g) * jax.nn.silu(gate)


def _l1_out_fb(o, gate_c, x1, target, norm_g, w_out, post_g, gate):
    s_len = x1.shape[0]

    def body(o_ref, gc_ref, x1_ref, t_ref, ng_ref, w_ref, pg_ref, gt_ref,
             loss_ref, dres_ref, do_ref, dgc_ref, dw_ref, dng_ref, dpg_ref, dgt_ref):
        _zero_at_first([loss_ref, dw_ref, dng_ref, dpg_ref, dgt_ref])
        ng = ng_ref[...]
        ons, vjps = [], []
        for h in range(C_HEADS):
            sl = slice(h * C_DK, (h + 1) * C_DK)
            on, vjp_h = jax.vjp(_head_norm_gate, o_ref[:, sl], gc_ref[:, sl], ng)
            ons.append(on)
            vjps.append(vjp_h)
        on_all = jnp.concatenate(ons, axis=-1)
        y = _bdot(on_all, w_ref[...])
        x2, vjp2 = jax.vjp(_post_res, y, x1_ref[...], pg_ref[...], gt_ref[...])
        err = x2 - t_ref[...]
        _acc(loss_ref, jnp.full((1, 128), 0.5 * jnp.sum(jnp.mean(err * err, axis=-1)), F32))
        dx2 = err * (1.0 / D_MODEL)
        dy, _, dpg, dgt = vjp2(dx2)
        dres_ref[...] = dx2
        _acc(dpg_ref, dpg)
        _acc(dgt_ref, dgt)
        dw_ref[...] += _bdot_tn(on_all, dy)
        don = _bdot_nt(dy, w_ref[...])
        for h in range(C_HEADS):
            sl = slice(h * C_DK, (h + 1) * C_DK)
            do_h, dgc_h, dng = vjps[h](don[:, sl])
            do_ref[:, sl] = do_h
            dgc_ref[:, sl] = dgc_h
            _acc(dng_ref, dng)

    vec, r10 = _fix((1, D_MODEL)), _row(TM, D_MODEL)
    row128 = _fix((1, 128))
    return pl.pallas_call(
        body, name="l1_out_fb", grid=(s_len // TM,),
        in_specs=[r10, r10, r10, r10, row128, _fix((D_MODEL, D_MODEL)), vec, vec],
        out_specs=[row128, r10, r10, r10, _fix((D_MODEL, D_MODEL)), row128, vec, vec],
        out_shape=[_sds(1, 128), _sds(s_len, D_MODEL), _sds(s_len, D_MODEL), _sds(s_len, D_MODEL),
                   _sds(D_MODEL, D_MODEL), _sds(1, 128), _sds(1, D_MODEL), _sds(1, D_MODEL)],
        compiler_params=_cp("arbitrary"),
    )(o, gate_c, x1, target, norm_g, w_out, post_g, gate)


def _row_of(v, width, at):
    return jnp.zeros((1, width), F32).at[0, at:at + v.shape[-1]].set(v.reshape(-1))


def _local_step(x, target, mod, wd):
    s_len = x.shape[0]
    shift0, scale0, gate0 = (mod[0:1, i * 1024:(i + 1) * 1024] for i in range(3))
    shift1, scale1, gate1 = (mod[1:2, i * 1024:(i + 1) * 1024] for i in range(3))
    pre_g0, pre_g1 = wd["pre_g"][0:1], wd["pre_g"][1:2]
    post_g0, post_g1 = wd["post_g"][0:1], wd["post_g"][1:2]
    w_in0 = wd["ab_w_in"].astype(BF)
    w_out0 = wd["ab_w_out"].astype(BF)
    glu_w = wd["s5_glu_w"].astype(BF)
    w_in1 = jnp.concatenate([wd["gdn_w_in"], jnp.zeros((D_MODEL, C_IN_PAD - wd["gdn_w_in"].shape[1]), F32)], axis=1).astype(BF)
    w_out1 = wd["gdn_w_out"].astype(BF)
    d_skip, glu_b = wd["s5_d"].reshape(1, 512), wd["s5_glu_b"].reshape(1, 512)
    norm_g = wd["gdn_norm_g"].reshape(1, 128)
    alog_row = _row_of(wd["gdn_a_log"], 128, C_HEADS)
    dtb_row = _row_of(wd["gdn_dt_bias"], 128, C_HEADS)
    conv_w = wd["gdn_conv"]

    a_re, a_im = wd["s5_a_re"], wd["s5_a_im"]
    log_dt = wd["s5_log_dt"].reshape(B_GROUPS, 1)
    bt_re = wd["s5_b_re"].transpose(0, 2, 1).reshape(B_WIDTH, B_STATE)
    bt_im = wd["s5_b_im"].transpose(0, 2, 1).reshape(B_WIDTH, B_STATE)
    abar_r, abar_i, bbar_r, bbar_i = _s5_params(a_re, a_im, log_dt, bt_re, bt_im)
    abr, abi = abar_r.reshape(1, -1), abar_i.reshape(1, -1)
    btr, bti = _blockdiag_b(bbar_r).astype(BF), _blockdiag_b(bbar_i).astype(BF)
    ctr, cti = _blockdiag_c(wd["s5_c_re"]).astype(BF), _blockdiag_c(wd["s5_c_im"]).astype(BF)

    table = _bucket_table()
    biases = _attn_bias(wd["rel_bias"], jnp.asarray(table))
    rels = [jnp.asarray(r) for r in _rel_table()]
    view = lambda a: a.reshape(s_len // A_VIEW, A_VIEW, a.shape[1])
    flat = lambda a: a.reshape(s_len, a.shape[2])
    q, k, v, u, ga, gb, h0 = _l0_front(x, pre_g0, scale0, shift0, w_in0)
    qv, kv, vv = view(q), view(k), view(v)
    ovs, lvs = zip(*[_attn_fwd(d, qv, kv, vv, biases[i], rels[i]) for i, d in enumerate(DILATIONS)])
    os, ls = [flat(a) for a in ovs], [flat(a) for a in lvs]
    xr, xi, ypre = _s5_scan_fwd(u, btr, bti, ctr, cti, abr, abi)
    x1, y0 = _l0_out(os, ls, ga, gb, ypre, u, x, d_skip, glu_w, glu_b, w_out0, post_g0, gate0)

    raw, gate_c, ba, h1 = _l1_front(x1, pre_g1, scale1, shift1, w_in1)
    qkv, bg = _gdn_prep(raw, ba, conv_w, alog_row, dtb_row)
    o_gdn, states, invs = _gdn_fwd(qkv, bg)
    loss_row, dres1, do_gdn, dgate_c, dw_out1, dnorm_g, dpost_g1, dgate1 = _l1_out_fb(
        o_gdn, gate_c, x1, target, norm_g, w_out1, post_g1, gate1)

    dq1, dk1, dv1, dbg = _gdn_bwd(qkv, bg, states, invs, do_gdn)
    draw, dba, dconv_w, dalog_row, ddtb_row = _gdn_prep_bwd(raw, ba, conv_w, alog_row, dtb_row, dq1, dk1, dv1, dbg)
    dz1, dx1, dpre_g1, dscale1, dshift1 = _front_bwd(
        "l1_front_bwd", x1, pre_g1, scale1, shift1, w_in1, dres1, [[draw], [dgate_c], [dba]], [QKV, 1024, 128])
    dw_in1 = _matmul_tn("l1_dw_in", h1, dz1, 1408)

    l0b = _l0_out_bwd(os, ls, ga, gb, ypre, u, x, y0, d_skip, glu_w, glu_b, w_out0, post_g0, gate0, dx1)
    dos, dls = l0b[0:3], l0b[3:6]
    dga, dgb, dypre, du_skip, dd_skip, dglu_w, dglu_b, dw_out0, dpost_g0, dgate0 = l0b[6:]
    du_scan, dbtr, dbti, dctr, dcti, dabr, dabi = _s5_scan_bwd(dypre, xr, xi, u, btr, bti, ctr, cti, abr, abi)
    dqs, dks, dvs, dbs = [], [], [], []
    for i, d in enumerate(DILATIONS):
        dq_d, dk_d, dv_d, db_d = _attn_bwd(d, qv, kv, vv, biases[i], rels[i], ovs[i], lvs[i], view(dos[i]), view(dls[i]))
        dqs.append(flat(dq_d))
        dks.append(flat(dk_d))
        dvs.append(flat(dv_d))
        dbs.append(db_d)
    parts = [dqs, dks, dvs, [du_skip, du_scan], [dga], [dgb]]
    dz0, grad_x, dpre_g0, dscale0, dshift0 = _front_bwd(
        "l0_front_bwd", x, pre_g0, scale0, shift0, w_in0, dx1, parts, [512] * 6)
    dw_in0 = _matmul_tn("l0_dw_in", h0, dz0, 768)

    idx_rows = jnp.asarray(table.reshape(3, -1), F32)
    drel = _rel_bias_grad(dbs, idx_rows).T
    da_re, da_im, dlog_dt, dbt_re, dbt_im = _s5_params_bwd(
        a_re, a_im, log_dt, bt_re, bt_im, dabr.reshape(B_GROUPS, B_STATE), dabi.reshape(B_GROUPS, B_STATE),
        _blockdiag_b_t(dbtr), _blockdiag_b_t(dbti))
    unb = lambda d: d.reshape(B_GROUPS, B_GROUP, B_STATE).transpose(0, 2, 1)
    grads = {
        "pre_g": jnp.concatenate([dpre_g0, dpre_g1], 0), "post_g": jnp.concatenate([dpost_g0, dpost_g1], 0),
        "rel_bias": drel, "ab_w_in": dw_in0, "ab_w_out": dw_out0,
        "s5_a_re": da_re, "s5_a_im": da_im, "s5_log_dt": dlog_dt.reshape(B_GROUPS),
        "s5_b_re": unb(dbt_re), "s5_b_im": unb(dbt_im),
        "s5_c_re": _blockdiag_c_t(dctr), "s5_c_im": _blockdiag_c_t(dcti),
        "s5_d": dd_skip.reshape(512), "s5_glu_w": dglu_w, "s5_glu_b": dglu_b.reshape(512),
        "gdn_w_in": dw_in1[:, :wd["gdn_w_in"].shape[1]], "gdn_conv": dconv_w,
        "gdn_a_log": dalog_row[0, C_HEADS:2 * C_HEADS], "gdn_dt_bias": ddtb_row[0, C_HEADS:2 * C_HEADS],
        "gdn_norm_g": dnorm_g.reshape(128), "gdn_w_out": dw_out1,
    }
    dmod = jnp.concatenate([jnp.concatenate([dshift0, dscale0, dgate0], 1), jnp.concatenate([dshift1, dscale1, dgate1], 1)], 0)
    return loss_row[0, 0], grad_x, grads, dmod


def _place():
    return lax.axis_index("x"), lax.axis_index("y"), lax.axis_index("c")


def _flip(v, bit):
    return 1 - v if bit else v


def _hbm_call(name, body, arrs, out_shapes, n_sem):
    any_spec = pl.BlockSpec(memory_space=pl.ANY)
    return pl.pallas_call(
        body, name=name,
        in_specs=[any_spec] * len(arrs), out_specs=[any_spec] * len(out_shapes), out_shape=out_shapes,
        scratch_shapes=[pltpu.SemaphoreType.DMA((n_sem,)), pltpu.SemaphoreType.DMA((n_sem,))],
    )(*arrs)


def _own_slot(gathered, own, slot):
    idx = lax.broadcasted_iota(jnp.int32, (gathered.shape[0],) + (1,) * own.ndim, 0)
    return jnp.where(idx == slot, own[None], gathered)


def _all_gather8(name, arr):
    def body(x_ref, out_ref, send_sems, recv_sems):
        x, y, c = _place()
        me = 4 * x + 2 * y + c
        sends, recvs = [], []
        for m in range(1, 8):
            peer = (_flip(x, m & 4), _flip(y, m & 2), _flip(c, m & 1))
            sends.append(pltpu.make_async_remote_copy(x_ref, out_ref.at[me], send_sems.at[m - 1], recv_sems.at[m - 1],
                                                      device_id=peer, device_id_type=MESH))
            recvs.append(pltpu.make_async_remote_copy(x_ref, out_ref.at[4 * peer[0] + 2 * peer[1] + peer[2]], send_sems.at[m - 1],
                                                      recv_sems.at[m - 1], device_id=peer, device_id_type=MESH))
        for cp in sends:
            cp.start()
        for cp in recvs:
            cp.wait_recv()
        for cp in sends:
            cp.wait_send()

    return _hbm_call(name, body, [arr], [jax.ShapeDtypeStruct((8,) + arr.shape, arr.dtype)], 7)[0]


def _chip_exchange(name, arrs, scatter):
    n = len(arrs)

    def body(*refs):
        ins, outs = refs[:n], refs[n:2 * n]
        send_sems, recv_sems = refs[2 * n:]
        x, y, c = _place()
        mine = 2 * x + y
        sends, recvs = [], []
        for a in range(n):
            for m in range(1, 4):
                px, py = _flip(x, m & 2), _flip(y, m & 1)
                k = 3 * a + m - 1
                src = ins[a].at[2 * px + py] if scatter else ins[a]
                sends.append(pltpu.make_async_remote_copy(src, outs[a].at[mine], send_sems.at[k], recv_sems.at[k],
                                                          device_id=(px, py, c), device_id_type=MESH))
                recvs.append(pltpu.make_async_remote_copy(src, outs[a].at[2 * px + py], send_sems.at[k], recv_sems.at[k],
                                                          device_id=(px, py, c), device_id_type=MESH))
        for cp in sends:
            cp.start()
        for cp in recvs:
            cp.wait_recv()
        for cp in sends:
            cp.wait_send()

    shapes = [jax.ShapeDtypeStruct(a.shape if scatter else (4,) + a.shape, a.dtype) for a in arrs]
    return _hbm_call(name, body, arrs, shapes, 3 * n)


def _sibling_exchange(name, arrs):
    n = len(arrs)

    def body(*refs):
        ins, outs = refs[:n], refs[n:2 * n]
        send_sems, recv_sems = refs[2 * n:]
        x, y, c = _place()
        copies = [pltpu.make_async_remote_copy(ins[a], outs[a], send_sems.at[a], recv_sems.at[a],
                                               device_id=(x, y, 1 - c), device_id_type=MESH) for a in range(n)]
        for cp in copies:
            cp.start()
        for cp in copies:
            cp.wait_recv()
        for cp in copies:
            cp.wait_send()

    return _hbm_call(name, body, arrs, [jax.ShapeDtypeStruct(a.shape, a.dtype) for a in arrs], n)


def _row_tile(rows):
    for t in (256, 128, 64, 32, 16, 8):
        if rows % t == 0:
            return t
    return rows


def _pair_sum(name, a, b, out_dtype):
    rows, cols = a.shape
    tr = _row_tile(rows)

    def body(a_ref, b_ref, o_ref):
        o_ref[...] = (a_ref[...] + b_ref[...]).astype(out_dtype)

    return pl.pallas_call(body, name=name, grid=(rows // tr,), in_specs=[_row(tr, cols)] * 2, out_specs=_row(tr, cols),
                          out_shape=_sds(rows, cols, dtype=out_dtype), compiler_params=_cp("arbitrary"))(a, b)


def _chip_sum(name, recv, partial, mine):
    n, rows, cols = recv.shape
    tr = _row_tile(rows)

    def body(mine_ref, *refs):
        own = refs[n][0].astype(F32)
        acc = None
        for s in range(n):
            term = jnp.where(mine_ref[0] == s, own, refs[s][0].astype(F32))
            acc = term if acc is None else acc + term
        refs[-1][...] = acc

    def slot_spec(s):
        return pl.BlockSpec((1, tr, cols), lambda i, m: (jnp.where(m[0] == s, (s + 1) % n, s), i, 0))

    grid_spec = pltpu.PrefetchScalarGridSpec(
        num_scalar_prefetch=1, grid=(rows // tr,),
        in_specs=[slot_spec(s) for s in range(n)] + [pl.BlockSpec((1, tr, cols), lambda i, m: (m[0], i, 0))],
        out_specs=pl.BlockSpec((tr, cols), lambda i, m: (i, 0)))
    return pl.pallas_call(body, name=name, grid_spec=grid_spec, out_shape=_sds(rows, cols),
                          compiler_params=_cp("arbitrary"))(mine, *([recv] * n), partial)


def _slot_sum(name, arr):
    n, rows, cols = arr.shape
    tr = _row_tile(rows)

    def body(*refs):
        acc = refs[0][0]
        for r in refs[1:-1]:
            acc = acc + r[0]
        refs[-1][...] = acc

    specs = [pl.BlockSpec((1, tr, cols), functools.partial(lambda s, i: (s, i, 0), s)) for s in range(n)]
    return pl.pallas_call(body, name=name, grid=(rows // tr,), in_specs=specs, out_specs=_row(tr, cols),
                          out_shape=_sds(rows, cols), compiler_params=_cp("arbitrary"))(*([arr] * n))


def _adamw(name, w, g, m, v):
    rows, cols = w.shape
    tr = _row_tile(rows)

    def body(w_ref, g_ref, m_ref, v_ref, d_ref, nm_ref, nv_ref):
        g_ = g_ref[...]
        m_ = ADAM_B1 * m_ref[...] + (1.0 - ADAM_B1) * g_
        v_ = ADAM_B2 * v_ref[...] + (1.0 - ADAM_B2) * (g_ * g_)
        m_hat = m_ / (1.0 - ADAM_B1 ** ADAM_STEP)
        v_hat = v_ / (1.0 - ADAM_B2 ** ADAM_STEP)
        d_ref[...] = -ADAM_LR * (m_hat / (jnp.sqrt(v_hat) + ADAM_EPS) + ADAM_WD * w_ref[...])
        nm_ref[...] = m_
        nv_ref[...] = v_

    spec = _row(tr, cols)
    return pl.pallas_call(body, name=name, grid=(rows // tr,), in_specs=[spec] * 4, out_specs=[spec] * 3,
                          out_shape=[_sds(rows, cols)] * 3, compiler_params=_cp("arbitrary"))(w, g, m, v)


def _adamw_halves(name, w, g_mine, g_sibling, m, v, core):
    rows, cols = w.shape
    half = rows // 2
    tr = _row_tile(half)
    per_half = half // tr

    def body(core_ref, w_ref, gm_ref, gs_ref, m_ref, v_ref, g_ref, d_ref, nm_ref, nv_ref):
        g_ = jnp.where(pl.program_id(0) // per_half == core_ref[0], gm_ref[...], gs_ref[...])
        m_ = ADAM_B1 * m_ref[...] + (1.0 - ADAM_B1) * g_
        v_ = ADAM_B2 * v_ref[...] + (1.0 - ADAM_B2) * (g_ * g_)
        m_hat = m_ / (1.0 - ADAM_B1 ** ADAM_STEP)
        v_hat = v_ / (1.0 - ADAM_B2 ** ADAM_STEP)
        g_ref[...] = g_
        d_ref[...] = -ADAM_LR * (m_hat / (jnp.sqrt(v_hat) + ADAM_EPS) + ADAM_WD * w_ref[...])
        nm_ref[...] = m_
        nv_ref[...] = v_

    full = pl.BlockSpec((tr, cols), lambda i, c: (i, 0))
    in_half = pl.BlockSpec((tr, cols), lambda i, c: (i % per_half, 0))
    grid_spec = pltpu.PrefetchScalarGridSpec(num_scalar_prefetch=1, grid=(rows // tr,),
                                             in_specs=[full, in_half, in_half, full, full], out_specs=[full] * 4)
    return pl.pallas_call(body, name=name, grid_spec=grid_spec, out_shape=[_sds(rows, cols)] * 4,
                          compiler_params=_cp("arbitrary"))(core, w, g_mine, g_sibling, m, v)


def _mod_local(c_all, ada_w):
    def body(c_ref, w_ref, o_ref):
        c_act = jax.nn.silu(c_ref[...])
        for l in range(2):
            o_ref[l] = _hdot(c_act, w_ref[l])

    return pl.pallas_call(body, name="mod_local", out_shape=_sds(2, 8, ada_w.shape[2]),
                          compiler_params=pltpu.CompilerParams(vmem_limit_bytes=VMEM_LIMIT_BYTES))(c_all, ada_w)


def _ada_w_grad(c_all, dmod_cols):
    def body(c_ref, d_ref, o_ref):
        c_act = jax.nn.silu(c_ref[...])
        for l in range(2):
            o_ref[l] = lax.dot_general(c_act, d_ref[l], (((0,), (0,)), ((), ())), precision=HI, preferred_element_type=F32)

    return pl.pallas_call(body, name="ada_w_grad", out_shape=_sds(2, D_MODEL, dmod_cols.shape[2]),
                          compiler_params=pltpu.CompilerParams(vmem_limit_bytes=VMEM_LIMIT_BYTES))(c_all, dmod_cols)


_SMALL = ("ada_b", "pre_g", "post_g", "rel_bias", "s5_a_re", "s5_a_im", "s5_log_dt", "s5_b_re", "s5_b_im", "s5_c_re", "s5_c_im",
          "s5_d", "s5_glu_b", "gdn_a_log", "gdn_dt_bias", "gdn_norm_g")
_SHARDED = ("ab_w_in", "ab_w_out", "s5_glu_w", "gdn_w_in", "gdn_w_out")
_COL_SHARDED = ("ab_w_in", "gdn_w_in")
_WEIGHTS = ("ada_w", "ada_b", "pre_g", "post_g", "rel_bias", "ab_w_in", "ab_w_out", "s5_a_re", "s5_a_im", "s5_log_dt", "s5_b_re",
            "s5_b_im", "s5_c_re", "s5_c_im", "s5_d", "s5_glu_w", "s5_glu_b", "gdn_w_in", "gdn_conv", "gdn_a_log", "gdn_dt_bias",
            "gdn_norm_g", "gdn_w_out")


def _rows128(n):
    return -(-n // 128)


def _pack(arrs, total_rows):
    pieces = []
    for a in arrs:
        flat = a.reshape(-1)
        pieces.append(jnp.pad(flat, (0, _rows128(flat.shape[0]) * 128 - flat.shape[0])).reshape(-1, 128))
    used = sum(p.shape[0] for p in pieces)
    pieces.append(jnp.zeros((total_rows - used, 128), F32))
    return jnp.concatenate(pieces, axis=0)


def _unpack(buf, shapes):
    out, at = [], 0
    for shp in shapes:
        n = int(np.prod(shp))
        out.append(buf[at:at + _rows128(n)].reshape(-1)[:n].reshape(shp))
        at += _rows128(n)
    return out


def _full_from_halves(name, g):
    if name in _COL_SHARDED:
        return g.transpose(0, 2, 1, 3).reshape(2 * g.shape[2], 4 * g.shape[3])
    return g.transpose(1, 0, 2, 3).reshape(8 * g.shape[2], g.shape[3])


def _shard_major(name, g):
    if name in _COL_SHARDED:
        return g.reshape(g.shape[0], 4, g.shape[1] // 4).transpose(1, 0, 2)
    return g.reshape(4, g.shape[0] // 4, g.shape[1])


def kernel(x, c, ada_w, ada_b, pre_g, post_g, rel_bias, ab_w_in, ab_w_out, s5_a_re, s5_a_im, s5_log_dt, s5_b_re, s5_b_im, s5_c_re, s5_c_im, s5_d, s5_glu_w, s5_glu_b, gdn_w_in, gdn_conv, gdn_a_log, gdn_dt_bias, gdn_norm_g, gdn_w_out, loss_target, m_ada_w, m_ada_b, m_pre_g, m_post_g, m_rel_bias, m_ab_w_in, m_ab_w_out, m_s5_a_re, m_s5_a_im, m_s5_log_dt, m_s5_b_re, m_s5_b_im, m_s5_c_re, m_s5_c_im, m_s5_d, m_s5_glu_w, m_s5_glu_b, m_gdn_w_in, m_gdn_conv, m_gdn_a_log, m_gdn_dt_bias, m_gdn_norm_g, m_gdn_w_out, v_ada_w, v_ada_b, v_pre_g, v_post_g, v_rel_bias, v_ab_w_in, v_ab_w_out, v_s5_a_re, v_s5_a_im, v_s5_log_dt, v_s5_b_re, v_s5_b_im, v_s5_c_re, v_s5_c_im, v_s5_d, v_s5_glu_w, v_s5_glu_b, v_gdn_w_in, v_gdn_conv, v_gdn_a_log, v_gdn_dt_bias, v_gdn_norm_g, v_gdn_w_out):
    w = dict(ada_w=ada_w, ada_b=ada_b, pre_g=pre_g, post_g=post_g, rel_bias=rel_bias, ab_w_in=ab_w_in, ab_w_out=ab_w_out,
             s5_a_re=s5_a_re, s5_a_im=s5_a_im, s5_log_dt=s5_log_dt, s5_b_re=s5_b_re, s5_b_im=s5_b_im, s5_c_re=s5_c_re, s5_c_im=s5_c_im,
             s5_d=s5_d, s5_glu_w=s5_glu_w, s5_glu_b=s5_glu_b, gdn_w_in=gdn_w_in, gdn_conv=gdn_conv, gdn_a_log=gdn_a_log,
             gdn_dt_bias=gdn_dt_bias, gdn_norm_g=gdn_norm_g, gdn_w_out=gdn_w_out)
    m = dict(ada_w=m_ada_w, ada_b=m_ada_b, pre_g=m_pre_g, post_g=m_post_g, rel_bias=m_rel_bias, ab_w_in=m_ab_w_in, ab_w_out=m_ab_w_out,
             s5_a_re=m_s5_a_re, s5_a_im=m_s5_a_im, s5_log_dt=m_s5_log_dt, s5_b_re=m_s5_b_re, s5_b_im=m_s5_b_im, s5_c_re=m_s5_c_re,
             s5_c_im=m_s5_c_im, s5_d=m_s5_d, s5_glu_w=m_s5_glu_w, s5_glu_b=m_s5_glu_b, gdn_w_in=m_gdn_w_in, gdn_conv=m_gdn_conv,
             gdn_a_log=m_gdn_a_log, gdn_dt_bias=m_gdn_dt_bias, gdn_norm_g=m_gdn_norm_g, gdn_w_out=m_gdn_w_out)
    v = dict(ada_w=v_ada_w, ada_b=v_ada_b, pre_g=v_pre_g, post_g=v_post_g, rel_bias=v_rel_bias, ab_w_in=v_ab_w_in, ab_w_out=v_ab_w_out,
             s5_a_re=v_s5_a_re, s5_a_im=v_s5_a_im, s5_log_dt=v_s5_log_dt, s5_b_re=v_s5_b_re, s5_b_im=v_s5_b_im, s5_c_re=v_s5_c_re,
             s5_c_im=v_s5_c_im, s5_d=v_s5_d, s5_glu_w=v_s5_glu_w, s5_glu_b=v_s5_glu_b, gdn_w_in=v_gdn_w_in, gdn_conv=v_gdn_conv,
             gdn_a_log=v_gdn_a_log, gdn_dt_bias=v_gdn_dt_bias, gdn_norm_g=v_gdn_norm_g, gdn_w_out=v_gdn_w_out)
    ix, iy, ic = _place()
    me = 4 * ix + 2 * iy + ic
    chip = 2 * ix + iy
    n_cols = ada_w.shape[2]

    mine_first = _pack([c, gdn_conv], 32)
    first = _own_slot(_all_gather8("gather_c_conv", mine_first), mine_first, me)
    c_all = first[:, 0:8].reshape(8, D_MODEL)
    conv_full = first[0::2, 8:32].reshape(4, C_CONV, n_cols).transpose(1, 0, 2).reshape(C_CONV, 4 * n_cols)
    mine_mod = _mod_local(c_all, ada_w)
    modl = _own_slot(_all_gather8("gather_mod", mine_mod), mine_mod, me)
    mod = lax.dynamic_index_in_dim(modl[0::2], me, axis=2, keepdims=False)
    mod = mod.transpose(1, 0, 2).reshape(2, 4 * n_cols) + ada_b

    halves = []
    for name in _SHARDED:
        shard = w[name][0].astype(BF)
        h = shard.shape[0] // 2
        halves.append(lax.dynamic_slice_in_dim(shard, ic * h, h, axis=0))
    from_chips = _chip_exchange("gather_w_chips", halves, False)
    my_halves = [_own_slot(g, own, chip) for g, own in zip(from_chips, halves)]
    their_halves = _sibling_exchange("gather_w_sibling", my_halves)
    wd = {name: w[name] for name in _SMALL if name != "ada_b"}
    wd = {k: (a if k in ("pre_g", "post_g", "rel_bias") else a[0]) for k, a in wd.items()}
    wd["gdn_conv"] = conv_full
    for name, a, b in zip(_SHARDED, my_halves, their_halves):
        wd[name] = _full_from_halves(name, jnp.where(ic == 0, jnp.stack([a, b], 0), jnp.stack([b, a], 0)))

    loss_local, grad_x, grads, dmod = _local_step(x[0], loss_target[0], mod, wd)
    loss = lax.psum(loss_local, ("x", "y", "c"))

    small_shapes = [w[name].shape for name in _SMALL] + [(C_CONV, 4 * n_cols)]
    small_rows = -(-sum(_rows128(int(np.prod(s))) for s in small_shapes) // 8) * 8
    partial = _pack([dmod] + [grads[name] for name in _SMALL[1:]] + [grads["gdn_conv"]], small_rows)
    every = _own_slot(_all_gather8("gather_small_grads", partial), partial, me)
    g_small = _slot_sum("sum_small_grads", every)
    g_list = _unpack(g_small, small_shapes)
    d_small, m_small, v_small = _adamw("adamw_small", _pack([w[n] for n in _SMALL], small_rows), g_small,
                                       _pack([m[n] for n in _SMALL], small_rows), _pack([v[n] for n in _SMALL], small_rows))
    out_g = dict(zip(_SMALL, g_list[:-1]))
    out_d = dict(zip(_SMALL, _unpack(d_small, small_shapes[:-1])))
    out_m = dict(zip(_SMALL, _unpack(m_small, small_shapes[:-1])))
    out_v = dict(zip(_SMALL, _unpack(v_small, small_shapes[:-1])))

    def update(name, g2d):
        shp = w[name].shape
        two_d = lambda a: a.reshape(-1, shp[-1])
        d_, m_, v_ = _adamw("adamw_" + name, two_d(w[name]), g2d, two_d(m[name]), two_d(v[name]))
        out_g[name], out_d[name], out_m[name], out_v[name] = (a.reshape(shp) for a in (g2d, d_, m_, v_))

    update("gdn_conv", lax.dynamic_slice_in_dim(g_list[-1], chip * n_cols, n_cols, axis=1))

    dmod_all = every[:, 0:_rows128(2 * 3 * D_MODEL)].reshape(8, 2, 4, n_cols)
    dmod_cols = lax.dynamic_index_in_dim(dmod_all, chip, axis=2, keepdims=False).transpose(1, 0, 2)
    update("ada_w", _ada_w_grad(c_all, dmod_cols).reshape(-1, n_cols))

    mine, other = [], []
    for name in _SHARDED:
        sm = _shard_major(name, grads[name])
        h = sm.shape[1] // 2
        mine.append(lax.dynamic_slice_in_dim(sm, ic * h, h, axis=1))
        other.append(lax.dynamic_slice_in_dim(sm, (1 - ic) * h, h, axis=1))
    from_sibling = _sibling_exchange("reduce_sibling", other)
    chip_partials = []
    for name, a, b in zip(_SHARDED, mine, from_sibling):
        flat = lambda t: t.reshape(-1, t.shape[-1])
        chip_partials.append(_pair_sum("sum_sibling_" + name, flat(a), flat(b), BF).reshape(a.shape))
    from_all = _chip_exchange("reduce_chips", chip_partials, True)
    chip_1 = jnp.reshape(chip, (1,)).astype(jnp.int32)
    core_1 = jnp.reshape(ic, (1,)).astype(jnp.int32)
    reduced = [_chip_sum("sum_chips_" + name, t, p, chip_1) for name, t, p in zip(_SHARDED, from_all, chip_partials)]
    for name, g_mine, g_sib in zip(_SHARDED, reduced, _sibling_exchange("reduce_share", reduced)):
        shp = w[name].shape
        two_d = lambda a: a.reshape(-1, shp[-1])
        outs = _adamw_halves("adamw_" + name, two_d(w[name]), g_mine, g_sib, two_d(m[name]), two_d(v[name]), core_1)
        out_g[name], out_d[name], out_m[name], out_v[name] = (a.reshape(shp) for a in outs)

    return (loss, grad_x[None], *[out_g[n] for n in _WEIGHTS], *[out_d[n] for n in _WEIGHTS],
            *[out_m[n] for n in _WEIGHTS], *[out_v[n] for n in _WEIGHTS])
```

```python
import functools
import math

import numpy as np
import jax
import jax.numpy as jnp
from jax import lax
from jax.experimental import pallas as pl
from jax.experimental.pallas import tpu as pltpu

F32 = jnp.float32
BF = jnp.bfloat16
HI = lax.Precision.HIGHEST
MESH = pl.DeviceIdType.MESH

D_MODEL = 1024
EPS = 1e-6
A_HEADS, A_HD, A_WIDTH, A_BLOCK = 8, 64, 512, 128
DILATIONS = (1, 4, 16)
N_KEYS = 128
REL_BUCKETS, REL_MAX_DIST = 32, 2048
B_WIDTH, B_GROUP, B_GROUPS, B_STATE = 512, 16, 32, 64
S5_LANES = 512
S5_TILES = 4
S5_T = 256
C_HEADS, C_DK, C_CHUNK, C_CONV = 8, 128, 64, 4
QKV = 3072
C_IN_PAD = 4224
TM = 256
VMEM_LIMIT_BYTES = 56 * 1024 * 1024
ADAM_LR, ADAM_B1, ADAM_B2, ADAM_EPS, ADAM_WD, ADAM_STEP = 0.001, 0.9, 0.999, 1e-08, 0.01, 10
NEG = float(np.finfo(np.float32).min)


def _cp(*sem):
    return pltpu.CompilerParams(dimension_semantics=sem, vmem_limit_bytes=VMEM_LIMIT_BYTES)


def _bdot(a, b):
    return jnp.dot(a.astype(BF), b.astype(BF), preferred_element_type=F32)


def _bdot_nt(a, b):
    return lax.dot_general(a.astype(BF), b.astype(BF), (((1,), (1,)), ((), ())), preferred_element_type=F32)


def _bdot_tn(a, b):
    return lax.dot_general(a.astype(BF), b.astype(BF), (((0,), (0,)), ((), ())), preferred_element_type=F32)


def _hdot(a, b):
    return jnp.dot(a, b, precision=HI, preferred_element_type=F32)


def _bein(eq, a, b):
    return jnp.einsum(eq, a.astype(BF), b.astype(BF), preferred_element_type=F32)


def _hein(eq, a, b):
    return jnp.einsum(eq, a, b, precision=HI, preferred_element_type=F32)


def _row(tm, n):
    return pl.BlockSpec((tm, n), lambda i: (i, 0))


def _fix(shape):
    return pl.BlockSpec(shape, lambda i: (0,) * len(shape))


def _sds(*shape, dtype=F32):
    return jax.ShapeDtypeStruct(shape, dtype)


def _acc(ref, val):
    ref[...] += val


def _zero_at_first(refs, axis=0):
    @pl.when(pl.program_id(axis) == 0)
    def _():
        for r in refs:
            r[...] = jnp.zeros_like(r)


def _rms(x):
    return x * lax.rsqrt(jnp.mean(x * x, axis=-1, keepdims=True) + EPS)


def _pre_mod(x, g, scale, shift):
    return (_rms(x) * g) * (1.0 + scale) + shift


def _post_res(y, x, post_g, gate):
    return x + gate * (_rms(y) * post_g)


def _merge_gate(o1, o2, o3, l1, l2, l3, ga):
    m = jnp.maximum(jnp.maximum(l1, l2), l3)
    e1, e2, e3 = jnp.exp(l1 - m), jnp.exp(l2 - m), jnp.exp(l3 - m)
    inv = 1.0 / (e1 + e2 + e3)
    return ((e1 * inv) * o1 + (e2 * inv) * o2 + (e3 * inv) * o3) * jax.nn.silu(ga)


def _s5_gelu(ypre, u, d_skip):
    return jax.nn.gelu(ypre + d_skip * u)


def _s5_glu(yb, gl, gb):
    return yb * jax.nn.sigmoid(gl) * jax.nn.silu(gb)


def _l0_front(x, pre_g, scale, shift, w_in):
    s_len = x.shape[0]

    def body(x_ref, g_ref, sc_ref, sh_ref, w_ref, *out_refs):
        qkv_refs, (u_ref, ga_ref, gb_ref, h_ref) = out_refs[:9], out_refs[9:]
        hb = _pre_mod(x_ref[...], g_ref[...], sc_ref[...], sh_ref[...]).astype(BF)
        h_ref[...] = hb
        z = jnp.dot(hb, w_ref[...], preferred_element_type=F32)
        for a in range(3):
            piece = z[:, a * 512:(a + 1) * 512]
            for i, d in enumerate(DILATIONS):
                qkv_refs[3 * a + i][...] = _to_res(piece, d)
        u_ref[...] = z[:, 1536:2048]
        ga_ref[...] = z[:, 2048:2560]
        gb_ref[...] = z[:, 2560:3072]

    vec = _fix((1, D_MODEL))
    return pl.pallas_call(
        body, name="l0_front", grid=(s_len // TM,),
        in_specs=[_row(TM, D_MODEL), vec, vec, vec, _fix((D_MODEL, 3072))],
        out_specs=[_res_spec(d) for d in DILATIONS] * 3 + [_row(TM, 512)] * 3 + [_row(TM, D_MODEL)],
        out_shape=[_sds(*_res_shape(s_len, d)) for d in DILATIONS] * 3 + [_sds(s_len, 512)] * 3 + [_sds(s_len, D_MODEL, dtype=BF)],
        compiler_params=_cp("arbitrary"),
    )(x, pre_g, scale, shift, w_in)


def _front_bwd(name, x, pre_g, scale, shift, w_in, dres, parts, widths):
    s_len = x.shape[0]
    n_in = sum(len(p) for p in parts)
    n_cols = sum(widths)

    def body(*refs):
        x_ref, g_ref, sc_ref, sh_ref, w_ref, dres_ref = refs[:6]
        part_refs = refs[6:6 + n_in]
        dz_ref, dx_ref, dg_ref, dsc_ref, dsh_ref = refs[6 + n_in:]
        _zero_at_first([dg_ref, dsc_ref, dsh_ref])
        _, vjp = jax.vjp(_pre_mod, x_ref[...], g_ref[...], sc_ref[...], sh_ref[...])
        dh = jnp.zeros((TM, D_MODEL), F32)
        col, at = 0, 0
        for grp, width in zip(parts, widths):
            tile = lambda r: _from_res(r[...]) if len(r.shape) == 3 else r[...]
            dz = tile(part_refs[at])
            for r in part_refs[at + 1:at + len(grp)]:
                dz = dz + tile(r)
            at += len(grp)
            dzb = dz.astype(BF)
            dz_ref[:, col:col + width] = dzb
            dh = dh + lax.dot_general(dzb, w_ref[:, col:col + width], (((1,), (1,)), ((), ())), preferred_element_type=F32)
            col += width
        dx, dg, dsc, dsh = vjp(dh)
        dx_ref[...] = dx + dres_ref[...]
        _acc(dg_ref, dg)
        _acc(dsc_ref, dsc)
        _acc(dsh_ref, dsh)

    vec = _fix((1, D_MODEL))
    flat = [a for p in parts for a in p]
    return pl.pallas_call(
        body, name=name, grid=(s_len // TM,),
        in_specs=[_row(TM, D_MODEL), vec, vec, vec, _fix((D_MODEL, n_cols)), _row(TM, D_MODEL)]
        + [_res_spec(a.shape[0], a.shape[2]) if a.ndim == 3 else _row(TM, a.shape[1]) for a in flat],
        out_specs=[_row(TM, n_cols), _row(TM, D_MODEL), vec, vec, vec],
        out_shape=[_sds(s_len, n_cols, dtype=BF), _sds(s_len, D_MODEL), _sds(1, D_MODEL), _sds(1, D_MODEL), _sds(1, D_MODEL)],
        compiler_params=_cp("arbitrary"),
    )(x, pre_g, scale, shift, w_in, dres, *flat)


def _matmul_tn(name, a, b, tn):
    s_len, k_dim = a.shape
    n_dim = b.shape[1]
    ts = 512

    def body(a_ref, b_ref, o_ref):
        _zero_at_first([o_ref], axis=1)
        o_ref[...] += lax.dot_general(a_ref[...], b_ref[...], (((0,), (0,)), ((), ())), preferred_element_type=F32)

    return pl.pallas_call(
        body, name=name, grid=(n_dim // tn, s_len // ts),
        in_specs=[pl.BlockSpec((ts, k_dim), lambda j, i: (i, 0)), pl.BlockSpec((ts, tn), lambda j, i: (i, j))],
        out_specs=pl.BlockSpec((k_dim, tn), lambda j, i: (0, j)),
        out_shape=_sds(k_dim, n_dim),
        compiler_params=_cp("arbitrary", "arbitrary"),
    )(a, b)


def _t5_bucket_np(dist):
    dist = np.maximum(dist, 0)
    max_exact = REL_BUCKETS // 2
    large = max_exact + (np.log(np.maximum(dist, 1) / max_exact)
                         / math.log(REL_MAX_DIST / max_exact) * (REL_BUCKETS - max_exact)).astype(np.int32)
    large = np.minimum(large, REL_BUCKETS - 1)
    return np.where(dist < max_exact, dist, large).astype(np.int32)


def _to_res(z, dil):
    if dil == 1:
        return z[None]
    return jnp.swapaxes(z.reshape(z.shape[0] // dil, dil, z.shape[1]), 0, 1)


def _from_res(z):
    if z.shape[0] == 1:
        return z[0]
    return jnp.swapaxes(z, 0, 1).reshape(z.shape[0] * z.shape[1], z.shape[2])


def _res_shape(s_len, dil, width=A_WIDTH):
    return (dil, s_len // dil, width)


def _res_spec(dil, width=A_WIDTH):
    return pl.BlockSpec((dil, TM // dil, width), lambda i: (0, i, 0))


def _bucket_table():
    qi = np.arange(A_BLOCK)[:, None]
    kj = np.arange(2 * A_BLOCK)[None, :]
    return np.stack([_t5_bucket_np((qi + A_BLOCK - kj) * d) for d in DILATIONS], 0)


def _attn_mask(first):
    qi = lax.broadcasted_iota(jnp.int32, (A_BLOCK, 2 * A_BLOCK), 0)
    kj = lax.broadcasted_iota(jnp.int32, (A_BLOCK, 2 * A_BLOCK), 1)
    rel = qi + A_BLOCK - kj
    return (rel >= 0) & (rel <= N_KEYS) & (jnp.logical_not(first) | (kj >= A_BLOCK))


def _attn_specs(nb, rev):
    n_of = (lambda i: nb - 1 - i) if rev else (lambda i: i)
    cur = pl.BlockSpec((None, A_BLOCK, A_WIDTH), lambda r, i: (r, n_of(i), 0))
    prev = pl.BlockSpec((None, A_BLOCK, A_WIDTH), lambda r, i: (r, jnp.maximum(n_of(i) - 1, 0), 0))
    bias = pl.BlockSpec((A_HEADS, A_BLOCK, 2 * A_BLOCK), lambda r, i: (0, 0, 0))
    return cur, prev, bias


def _attn_fwd(q, k, v, bias):
    dil, t_len, _ = q.shape
    nb = t_len // A_BLOCK
    scale = A_HD ** -0.5

    def body(q_ref, kp_ref, kc_ref, vp_ref, vc_ref, b_ref, o_ref, l_ref):
        mask = _attn_mask(pl.program_id(1) == 0)
        lane = lax.broadcasted_iota(jnp.int32, (1, 128), 1)
        for hp in range(A_HEADS // 2):
            sl = slice(hp * 128, (hp + 1) * 128)
            qp = q_ref[:, sl]
            kw = jnp.concatenate([kp_ref[:, sl], kc_ref[:, sl]], axis=0).astype(BF)
            vw = jnp.concatenate([vp_ref[:, sl], vc_ref[:, sl]], axis=0).astype(BF)
            outs, lses = [], []
            for j in range(2):
                hm = (lane < 64) if j == 0 else (lane >= 64)
                s = _bdot_nt(jnp.where(hm, qp, 0.0), kw) * scale
                s = jnp.where(mask, s + b_ref[2 * hp + j], NEG)
                m = jnp.max(s, axis=-1, keepdims=True)
                p = jnp.exp(s - m)
                den = jnp.sum(p, axis=-1, keepdims=True)
                outs.append(_bdot(p, vw) / den)
                lses.append(m + jnp.log(den))
            hm0 = lane < 64
            o_ref[:, sl] = jnp.where(hm0, outs[0], outs[1])
            l_ref[:, sl] = jnp.where(hm0, lses[0], lses[1])

    cur, prev, bias_spec = _attn_specs(nb, False)
    return pl.pallas_call(
        body, name=f"attn_fwd_d{dil}", grid=(dil, nb),
        in_specs=[cur, prev, cur, prev, cur, bias_spec],
        out_specs=[cur, cur],
        out_shape=[_sds(dil, t_len, A_WIDTH)] * 2,
        compiler_params=_cp("arbitrary", "arbitrary"),
    )(q, k, k, v, v, bias)


def _attn_bwd(q, k, v, bias, o, l, do, dl):
    dil, t_len, _ = q.shape
    nb = t_len // A_BLOCK
    scale = A_HD ** -0.5

    def body(q_ref, kp_ref, kc_ref, vp_ref, vc_ref, b_ref, o_ref, l_ref, do_ref, dl_ref,
             dq_ref, dk_ref, dv_ref, db_ref, ck_ref, cv_ref):
        _zero_at_first([ck_ref, cv_ref], axis=1)

        @pl.when((pl.program_id(0) == 0) & (pl.program_id(1) == 0))
        def _():
            db_ref[...] = jnp.zeros_like(db_ref)

        mask = _attn_mask(pl.program_id(1) == nb - 1)
        lane = lax.broadcasted_iota(jnp.int32, (1, 128), 1)
        for hp in range(A_HEADS // 2):
            sl = slice(hp * 128, (hp + 1) * 128)
            qp = q_ref[:, sl]
            kw = jnp.concatenate([kp_ref[:, sl], kc_ref[:, sl]], axis=0).astype(BF)
            vw = jnp.concatenate([vp_ref[:, sl], vc_ref[:, sl]], axis=0).astype(BF)
            op, lp, dop, dlp = o_ref[:, sl], l_ref[:, sl], do_ref[:, sl], dl_ref[:, sl]
            dq_acc = jnp.zeros((A_BLOCK, 128), F32)
            dk_acc = jnp.zeros((2 * A_BLOCK, 128), F32)
            dv_acc = jnp.zeros((2 * A_BLOCK, 128), F32)
            for j in range(2):
                hm = (lane < 64) if j == 0 else (lane >= 64)
                qm = jnp.where(hm, qp, 0.0)
                s = _bdot_nt(qm, kw) * scale
                s = jnp.where(mask, s + b_ref[2 * hp + j], NEG)
                lse = jnp.max(jnp.where(hm, lp, NEG), axis=-1, keepdims=True)
                p = jnp.exp(s - lse)
                do_h = jnp.where(hm, dop, 0.0)
                dd = jnp.sum(do_h * op, axis=-1, keepdims=True)
                dlse = jnp.sum(jnp.where(hm, dlp, 0.0), axis=-1, keepdims=True)
                ds = p * (_bdot_nt(do_h, vw) - dd + dlse)
                dv_acc = dv_acc + _bdot_tn(p, do_h)
                dq_acc = dq_acc + jnp.where(hm, _bdot(ds, kw), 0.0) * scale
                dk_acc = dk_acc + _bdot_tn(ds, qm) * scale
                db_ref[2 * hp + j] += ds
            dq_ref[:, sl] = dq_acc
            dk_ref[:, sl] = dk_acc[A_BLOCK:] + ck_ref[:, sl]
            dv_ref[:, sl] = dv_acc[A_BLOCK:] + cv_ref[:, sl]
            ck_ref[:, sl] = dk_acc[:A_BLOCK]
            cv_ref[:, sl] = dv_acc[:A_BLOCK]

    cur, prev, bias_spec = _attn_specs(nb, True)
    return pl.pallas_call(
        body, name=f"attn_bwd_d{dil}", grid=(dil, nb),
        in_specs=[cur, prev, cur, prev, cur, bias_spec, cur, cur, cur, cur],
        out_specs=[cur, cur, cur, bias_spec],
        out_shape=[_sds(dil, t_len, A_WIDTH)] * 3 + [_sds(A_HEADS, A_BLOCK, 2 * A_BLOCK)],
        scratch_shapes=[pltpu.VMEM((A_BLOCK, A_WIDTH), F32)] * 2,
        compiler_params=_cp("arbitrary", "arbitrary"),
    )(q, k, k, v, v, bias, o, l, do, dl)


def _attn_bias(rel_bias, table):
    def body(rb_ref, t_ref, *o_refs):
        for c in range(3):
            t = t_ref[c]
            acc = [jnp.zeros((A_BLOCK, 2 * A_BLOCK), F32) for _ in range(A_HEADS)]
            for b in range(REL_BUCKETS):
                hit = t == b
                acc = [jnp.where(hit, rb_ref[b, h], acc[h]) for h in range(A_HEADS)]
            for h in range(A_HEADS):
                o_refs[c][h] = acc[h]

    return pl.pallas_call(body, name="attn_bias", out_shape=[_sds(A_HEADS, A_BLOCK, 2 * A_BLOCK)] * 3,
                          in_specs=[pl.BlockSpec(memory_space=pltpu.SMEM), pl.BlockSpec(memory_space=pltpu.VMEM)],
                          compiler_params=pltpu.CompilerParams(vmem_limit_bytes=VMEM_LIMIT_BYTES))(rel_bias, table)


def _rel_bias_grad(dbs, idx_rows):
    n = A_BLOCK * 2 * A_BLOCK

    def body(d0_ref, d1_ref, d2_ref, idx_ref, o_ref):
        bucket = lax.broadcasted_iota(jnp.int32, (REL_BUCKETS, n), 0).astype(F32)
        acc = jnp.zeros((A_HEADS, REL_BUCKETS), F32)
        for c, db_ref in enumerate((d0_ref, d1_ref, d2_ref)):
            onehot = (idx_ref[c:c + 1, :] == bucket).astype(F32)
            acc = acc + lax.dot_general(db_ref[...], onehot, (((1,), (1,)), ((), ())), precision=HI, preferred_element_type=F32)
        o_ref[...] = acc

    return pl.pallas_call(body, name="rel_bias_grad", out_shape=_sds(A_HEADS, REL_BUCKETS),
                          compiler_params=pltpu.CompilerParams(vmem_limit_bytes=VMEM_LIMIT_BYTES))(
                              *[d.reshape(A_HEADS, n) for d in dbs], idx_rows)


def _s5_param_fn(a_re, a_im, log_dt, bt_re, bt_im):
    dt = jnp.exp(log_dt)
    mag = jnp.exp(dt * a_re)
    abar_r, abar_i = mag * jnp.cos(dt * a_im), mag * jnp.sin(dt * a_im)
    den = a_re * a_re + a_im * a_im
    fr = ((abar_r - 1.0) * a_re + abar_i * a_im) / den
    fi = (abar_i * a_re - (abar_r - 1.0) * a_im) / den
    row = lax.broadcasted_iota(jnp.int32, (B_WIDTH, B_GROUPS), 0)
    grp = lax.broadcasted_iota(jnp.int32, (B_WIDTH, B_GROUPS), 1)
    expand = ((row // B_GROUP) == grp).astype(F32)
    fr_e, fi_e = _hdot(expand, fr), _hdot(expand, fi)
    return abar_r, abar_i, fr_e * bt_re - fi_e * bt_im, fr_e * bt_im + fi_e * bt_re


def _s5_params(a_re, a_im, log_dt, bt_re, bt_im):
    def body(ar, ai, ld, br, bi, o1, o2, o3, o4):
        o1[...], o2[...], o3[...], o4[...] = _s5_param_fn(ar[...], ai[...], ld[...], br[...], bi[...])

    return pl.pallas_call(body, name="s5_params",
                          out_shape=[_sds(B_GROUPS, B_STATE)] * 2 + [_sds(B_WIDTH, B_STATE)] * 2)(a_re, a_im, log_dt, bt_re, bt_im)


def _s5_params_bwd(a_re, a_im, log_dt, bt_re, bt_im, d1, d2, d3, d4):
    def body(ar, ai, ld, br, bi, c1, c2, c3, c4, o1, o2, o3, o4, o5):
        _, vjp = jax.vjp(_s5_param_fn, ar[...], ai[...], ld[...], br[...], bi[...])
        o1[...], o2[...], o3[...], o4[...], o5[...] = vjp((c1[...], c2[...], c3[...], c4[...]))

    return pl.pallas_call(body, name="s5_params_bwd",
                          out_shape=[_sds(B_GROUPS, B_STATE)] * 2 + [_sds(B_GROUPS, 1)] + [_sds(B_WIDTH, B_STATE)] * 2,
                          )(a_re, a_im, log_dt, bt_re, bt_im, d1, d2, d3, d4)


def _cscan(br, bi, ar, ai, reverse):
    t_len = br.shape[0]
    rows = lax.broadcasted_iota(jnp.int32, br.shape, 0)
    xr, xi, cr, ci = br, bi, ar, ai
    k = 1
    while k < t_len:
        if reverse:
            keep, shift = rows < t_len - k, t_len - k
        else:
            keep, shift = rows >= k, k
        sr = jnp.where(keep, pltpu.roll(xr, shift, 0), 0.0)
        si = jnp.where(keep, pltpu.roll(xi, shift, 0), 0.0)
        xr, xi = xr + cr * sr - ci * si, xi + cr * si + ci * sr
        cr, ci = cr * cr - ci * ci, 2.0 * cr * ci
        k *= 2
    return xr, xi


def _pick_row(x, r):
    rows = lax.broadcasted_iota(jnp.int32, x.shape, 0)
    return jnp.sum(jnp.where(rows == r, x, 0.0), axis=0, keepdims=True)


def _s5_tile_specs(n_t, rev):
    t_of = (lambda i: n_t - 1 - i) if rev else (lambda i: i)
    u_spec = pl.BlockSpec((S5_T, 128), lambda j, i: (t_of(i), j))
    x_spec = pl.BlockSpec((S5_T, S5_LANES), lambda j, i: (t_of(i), j))
    b_spec = pl.BlockSpec((1, 128, S5_LANES), lambda j, i: (j, 0, 0))
    c_spec = pl.BlockSpec((1, S5_LANES, 128), lambda j, i: (j, 0, 0))
    a_spec = pl.BlockSpec((1, S5_LANES), lambda j, i: (0, j))
    return u_spec, x_spec, b_spec, c_spec, a_spec


def _s5_scan_fwd(u, btr, bti, ctr, cti, abr, abi):
    s_len = u.shape[0]
    n_t = s_len // S5_T

    def body(u_ref, btr_ref, bti_ref, ctr_ref, cti_ref, ar_ref, ai_ref, xr_ref, xi_ref, y_ref, car, cai, pwr, pwi):
        ar, ai = ar_ref[...], ai_ref[...]

        @pl.when(pl.program_id(1) == 0)
        def _():
            car[...] = jnp.zeros_like(car)
            cai[...] = jnp.zeros_like(cai)
            rows = lax.broadcasted_iota(jnp.int32, (S5_T, S5_LANES), 0)
            pwr[...], pwi[...] = _cscan(jnp.where(rows == 0, ar, 0.0), jnp.where(rows == 0, ai, 0.0), ar, ai, False)

        ub = u_ref[...]
        xr, xi = _cscan(_bdot(ub, btr_ref[0]), _bdot(ub, bti_ref[0]), ar, ai, False)
        cr, ci = car[...], cai[...]
        pr, pi_ = pwr[...], pwi[...]
        xr, xi = xr + pr * cr - pi_ * ci, xi + pr * ci + pi_ * cr
        xr_ref[...] = xr
        xi_ref[...] = xi
        car[...] = _pick_row(xr, S5_T - 1)
        cai[...] = _pick_row(xi, S5_T - 1)
        y_ref[...] = _bdot(xr, ctr_ref[0]) - _bdot(xi, cti_ref[0])

    u_spec, x_spec, b_spec, c_spec, a_spec = _s5_tile_specs(n_t, False)
    return pl.pallas_call(
        body, name="s5_scan_fwd", grid=(S5_TILES, n_t),
        in_specs=[u_spec, b_spec, b_spec, c_spec, c_spec, a_spec, a_spec],
        out_specs=[x_spec, x_spec, u_spec],
        out_shape=[_sds(s_len, S5_TILES * S5_LANES)] * 2 + [_sds(s_len, B_WIDTH)],
        scratch_shapes=[pltpu.VMEM((1, S5_LANES), F32)] * 2 + [pltpu.VMEM((S5_T, S5_LANES), F32)] * 2,
        compiler_params=_cp("arbitrary", "arbitrary"),
    )(u, btr, bti, ctr, cti, abr, abi)


def _s5_scan_bwd(dy, xr, xi, u, btr, bti, ctr, cti, abr, abi):
    s_len = u.shape[0]
    n_t = s_len // S5_T

    def body(dy_ref, xr_ref, xi_ref, xrp_ref, xip_ref, u_ref, btr_ref, bti_ref, ctr_ref, cti_ref, ar_ref, ai_ref,
             du_ref, dbtr_ref, dbti_ref, dctr_ref, dcti_ref, dar_ref, dai_ref, car, cai, pwr, pwi):
        ar, ai = ar_ref[...], ai_ref[...]
        i = pl.program_id(1)
        rows = lax.broadcasted_iota(jnp.int32, (S5_T, S5_LANES), 0)

        @pl.when(i == 0)
        def _():
            for r in (car, cai, dbtr_ref, dbti_ref, dctr_ref, dcti_ref, dar_ref, dai_ref):
                r[...] = jnp.zeros_like(r)
            last = rows == S5_T - 1
            pwr[...], pwi[...] = _cscan(jnp.where(last, ar, 0.0), jnp.where(last, -ai, 0.0), ar, -ai, True)

        dyb = dy_ref[...]
        xr_b, xi_b, ub = xr_ref[...], xi_ref[...], u_ref[...]
        dctr_ref[0] += _bdot_tn(xr_b, dyb)
        dcti_ref[0] -= _bdot_tn(xi_b, dyb)
        gr, gi = _cscan(_bdot_nt(dyb, ctr_ref[0]), -_bdot_nt(dyb, cti_ref[0]), ar, -ai, True)
        cr, ci = car[...], cai[...]
        pr, pi_ = pwr[...], pwi[...]
        gr, gi = gr + pr * cr - pi_ * ci, gi + pr * ci + pi_ * cr
        car[...] = _pick_row(gr, 0)
        cai[...] = _pick_row(gi, 0)
        du_ref[...] = _bdot_nt(gr, btr_ref[0]) + _bdot_nt(gi, bti_ref[0])
        dbtr_ref[0] += _bdot_tn(ub, gr)
        dbti_ref[0] += _bdot_tn(ub, gi)
        has_prev = (i < n_t - 1).astype(F32)
        hr = _pick_row(xrp_ref[...], 7) * has_prev
        hi = _pick_row(xip_ref[...], 7) * has_prev
        xpr = jnp.where(rows == 0, hr, pltpu.roll(xr_b, 1, 0))
        xpi = jnp.where(rows == 0, hi, pltpu.roll(xi_b, 1, 0))
        dar_ref[...] += jnp.sum(gr * xpr + gi * xpi, axis=0, keepdims=True)
        dai_ref[...] += jnp.sum(gi * xpr - gr * xpi, axis=0, keepdims=True)

    u_spec, x_spec, b_spec, c_spec, a_spec = _s5_tile_specs(n_t, True)
    halo = pl.BlockSpec((8, S5_LANES), lambda j, i: (jnp.maximum((n_t - 1 - i) * (S5_T // 8) - 1, 0), j))
    return pl.pallas_call(
        body, name="s5_scan_bwd", grid=(S5_TILES, n_t),
        in_specs=[u_spec, x_spec, x_spec, halo, halo, u_spec, b_spec, b_spec, c_spec, c_spec, a_spec, a_spec],
        out_specs=[u_spec, b_spec, b_spec, c_spec, c_spec, a_spec, a_spec],
        out_shape=[_sds(s_len, B_WIDTH)] + [_sds(S5_TILES, 128, S5_LANES)] * 2 + [_sds(S5_TILES, S5_LANES, 128)] * 2
        + [_sds(1, S5_TILES * S5_LANES)] * 2,
        scratch_shapes=[pltpu.VMEM((1, S5_LANES), F32)] * 2 + [pltpu.VMEM((S5_T, S5_LANES), F32)] * 2,
        compiler_params=_cp("arbitrary", "arbitrary"),
    )(dy, xr, xi, xr, xi, u, btr, bti, ctr, cti, abr, abi)


def _blockdiag_b(bbar_t):
    blocks = bbar_t.reshape(S5_TILES, 8, B_GROUP, B_STATE)
    return jnp.einsum('jgmp,gh->jgmhp', blocks, jnp.eye(8, dtype=F32)).reshape(S5_TILES, 128, S5_LANES)


def _blockdiag_b_t(d):
    return jnp.einsum('jgmgp->jgmp', d.reshape(S5_TILES, 8, B_GROUP, 8, B_STATE)).reshape(B_WIDTH, B_STATE)


def _blockdiag_c(c):
    blocks = c.reshape(S5_TILES, 8, B_GROUP, B_STATE)
    return jnp.einsum('jgmp,gh->jhpgm', blocks, jnp.eye(8, dtype=F32)).reshape(S5_TILES, S5_LANES, 128)


def _blockdiag_c_t(d):
    return jnp.einsum('jgpgm->jgmp', d.reshape(S5_TILES, 8, B_STATE, 8, B_GROUP)).reshape(B_GROUPS, B_GROUP, B_STATE)


def _l0_out(os, ls, ga, gb, ypre, u, x, d_skip, glu_w, glu_b, w_out, post_g, gate):
    s_len = x.shape[0]

    def body(o0, o1, o2, l0, l1, l2, ga_ref, gb_ref, yp_ref, u_ref, x_ref, d_ref, gw_ref, gbias_ref, w_ref, pg_ref, gt_ref, x1_ref, y_ref):
        oa = _merge_gate(*[_from_res(r[...]) for r in (o0, o1, o2, l0, l1, l2)], ga_ref[...])
        yb = _s5_gelu(yp_ref[...], u_ref[...], d_ref[...])
        ob = _s5_glu(yb, _bdot(yb, gw_ref[...]) + gbias_ref[...], gb_ref[...])
        y = _bdot(oa, w_ref[0:512, :]) + _bdot(ob, w_ref[512:1024, :])
        y_ref[...] = y
        x1_ref[...] = _post_res(y, x_ref[...], pg_ref[...], gt_ref[...])

    vec, half = _fix((1, D_MODEL)), _fix((1, 512))
    return pl.pallas_call(
        body, name="l0_out", grid=(s_len // TM,),
        in_specs=[_res_spec(d) for d in DILATIONS] * 2 + [_row(TM, 512)] * 4
        + [_row(TM, D_MODEL), half, _fix((512, 512)), half, _fix((D_MODEL, D_MODEL)), vec, vec],
        out_specs=[_row(TM, D_MODEL)] * 2,
        out_shape=[_sds(s_len, D_MODEL)] * 2,
        compiler_params=_cp("arbitrary"),
    )(*os, *ls, ga, gb, ypre, u, x, d_skip, glu_w, glu_b, w_out, post_g, gate)


def _l0_out_bwd(os, ls, ga, gb, ypre, u, x, y, d_skip, glu_w, glu_b, w_out, post_g, gate, dx1):
    s_len = x.shape[0]

    def body(o0, o1, o2, l0, l1, l2, ga_ref, gb_ref, yp_ref, u_ref, x_ref, y_ref, d_ref, gw_ref, gbias_ref, w_ref, pg_ref, gt_ref, dx1_ref,
             do0, do1, do2, dl0, dl1, dl2, dga_ref, dgb_ref, dyp_ref, du_ref, dd_ref, dgw_ref, dgbias_ref, dw_ref, dpg_ref, dgt_ref):
        _zero_at_first([dd_ref, dgw_ref, dgbias_ref, dw_ref, dpg_ref, dgt_ref])
        _, vjp2 = jax.vjp(_post_res, y_ref[...], x_ref[...], pg_ref[...], gt_ref[...])
        dy, _, dpg, dgt = vjp2(dx1_ref[...])
        _acc(dpg_ref, dpg)
        _acc(dgt_ref, dgt)
        oa, vjp_a = jax.vjp(_merge_gate, *[_from_res(r[...]) for r in (o0, o1, o2, l0, l1, l2)], ga_ref[...])
        yb, vjp_g = jax.vjp(_s5_gelu, yp_ref[...], u_ref[...], d_ref[...])
        gl = _bdot(yb, gw_ref[...]) + gbias_ref[...]
        ob, vjp_b = jax.vjp(_s5_glu, yb, gl, gb_ref[...])
        dw_ref[0:512, :] += _bdot_tn(oa, dy)
        dw_ref[512:1024, :] += _bdot_tn(ob, dy)
        d1, d2, d3, e1, e2, e3, dga = vjp_a(_bdot_nt(dy, w_ref[0:512, :]))
        for ref, val, d in zip((do0, do1, do2, dl0, dl1, dl2), (d1, d2, d3, e1, e2, e3), DILATIONS * 2):
            ref[...] = _to_res(val, d)
        dga_ref[...] = dga
        dyb, dgl, dgb = vjp_b(_bdot_nt(dy, w_ref[512:1024, :]))
        dgb_ref[...] = dgb
        dgw_ref[...] += _bdot_tn(yb, dgl)
        _acc(dgbias_ref, jnp.sum(dgl, axis=0, keepdims=True))
        dyp, du, dd = vjp_g(dyb + _bdot_nt(dgl, gw_ref[...]))
        dyp_ref[...] = dyp
        du_ref[...] = du
        _acc(dd_ref, dd)

    vec, half = _fix((1, D_MODEL)), _fix((1, 512))
    r5, r10 = _row(TM, 512), _row(TM, D_MODEL)
    res6 = [_res_spec(d) for d in DILATIONS] * 2
    return pl.pallas_call(
        body, name="l0_out_bwd", grid=(s_len // TM,),
        in_specs=res6 + [r5] * 4 + [r10, r10, half, _fix((512, 512)), half, _fix((D_MODEL, D_MODEL)), vec, vec, r10],
        out_specs=res6 + [r5] * 4 + [half, _fix((512, 512)), half, _fix((D_MODEL, D_MODEL)), vec, vec],
        out_shape=[_sds(*_res_shape(s_len, d)) for d in DILATIONS] * 2 + [_sds(s_len, 512)] * 4
        + [_sds(1, 512), _sds(512, 512), _sds(1, 512), _sds(D_MODEL, D_MODEL), _sds(1, D_MODEL), _sds(1, D_MODEL)],
        compiler_params=_cp("arbitrary"),
    )(*os, *ls, ga, gb, ypre, u, x, y, d_skip, glu_w, glu_b, w_out, post_g, gate, dx1)


def _l1_front(x, pre_g, scale, shift, w_in):
    s_len = x.shape[0]

    def body(x_ref, g_ref, sc_ref, sh_ref, w_ref, raw_ref, gate_ref, ba_ref, h_ref):
        hb = _pre_mod(x_ref[...], g_ref[...], sc_ref[...], sh_ref[...]).astype(BF)
        h_ref[...] = hb
        z = jnp.dot(hb, w_ref[...], preferred_element_type=F32)
        raw_ref[...] = z[:, 0:QKV]
        gate_ref[...] = z[:, QKV:QKV + 1024]
        ba_ref[...] = z[:, QKV + 1024:C_IN_PAD]

    vec = _fix((1, D_MODEL))
    return pl.pallas_call(
        body, name="l1_front", grid=(s_len // TM,),
        in_specs=[_row(TM, D_MODEL), vec, vec, vec, _fix((D_MODEL, C_IN_PAD))],
        out_specs=[_row(TM, QKV), _row(TM, 1024), _row(TM, 128), _row(TM, D_MODEL)],
        out_shape=[_sds(s_len, QKV), _sds(s_len, 1024), _sds(s_len, 128), _sds(s_len, D_MODEL, dtype=BF)],
        compiler_params=_cp("arbitrary"),
    )(x, pre_g, scale, shift, w_in)


def _bg_fn(ba, alog_row, dtb_row):
    lane = lax.broadcasted_iota(jnp.int32, (1, 128), 1)
    g = -jnp.exp(alog_row) * jax.nn.softplus(ba + dtb_row)
    return jnp.where(lane < C_HEADS, jax.nn.sigmoid(ba), jnp.where(lane < 2 * C_HEADS, g, 0.0))


def _act_q(c):
    q = jax.nn.silu(c)
    return q * lax.rsqrt(jnp.sum(q * q, axis=-1, keepdims=True) + EPS) * (C_DK ** -0.5)


def _act_k(c):
    k = jax.nn.silu(c)
    return k * lax.rsqrt(jnp.sum(k * k, axis=-1, keepdims=True) + EPS)


def _act_of(s):
    return _act_q if s < 8 else (_act_k if s < 16 else jax.nn.silu)


def _gdn_prep(raw, ba, conv_w, alog_row, dtb_row):
    s_len = raw.shape[0]

    def body(raw_ref, halo_ref, ba_ref, w_ref, al_ref, dt_ref, qkv_ref, bg_ref):
        bg_ref[...] = _bg_fn(ba_ref[...], al_ref[...], dt_ref[...])
        has_prev = (pl.program_id(0) > 0).astype(F32)
        for s in range(24):
            sl = slice(s * 128, (s + 1) * 128)
            cat = jnp.concatenate([halo_ref[:, sl] * has_prev, raw_ref[:, sl]], axis=0)
            conv = w_ref[3:4, sl] * cat[8:]
            for j in range(3):
                conv = conv + w_ref[j:j + 1, sl] * pltpu.roll(cat, 3 - j, 0)[8:]
            qkv_ref[:, sl] = _act_of(s)(conv)

    halo = pl.BlockSpec((8, QKV), lambda i: (jnp.maximum(i * (TM // 8) - 1, 0), 0))
    row128 = _fix((1, 128))
    return pl.pallas_call(
        body, name="gdn_prep", grid=(s_len // TM,),
        in_specs=[_row(TM, QKV), halo, _row(TM, 128), _fix((C_CONV, QKV)), row128, row128],
        out_specs=[_row(TM, QKV), _row(TM, 128)],
        out_shape=[_sds(s_len, QKV), _sds(s_len, 128)],
        compiler_params=_cp("arbitrary"),
    )(raw, raw, ba, conv_w, alog_row, dtb_row)


def _gdn_prep_bwd(raw, ba, conv_w, alog_row, dtb_row, dq, dk, dv, dbg):
    s_len = raw.shape[0]
    n_tiles = s_len // TM
    ext = TM + 8

    def body(raw_ref, prev_ref, next_ref, ba_ref, w_ref, al_ref, dt_ref, dq_ref, dqn_ref, dk_ref, dkn_ref, dv_ref, dvn_ref, dbg_ref,
             draw_ref, dba_ref, dw_ref, dal_ref, ddt_ref):
        _zero_at_first([dw_ref, dal_ref, ddt_ref])
        i = pl.program_id(0)
        _, vjp_bg = jax.vjp(_bg_fn, ba_ref[...], al_ref[...], dt_ref[...])
        dba, dal, ddt = vjp_bg(dbg_ref[...])
        dba_ref[...] = dba
        _acc(dal_ref, dal)
        _acc(ddt_ref, ddt)
        has_prev = (i > 0).astype(F32)
        has_next = (i < n_tiles - 1).astype(F32)
        ct_refs = ((dq_ref, dqn_ref), (dk_ref, dkn_ref), (dv_ref, dvn_ref))
        for s in range(24):
            sl = slice(s * 128, (s + 1) * 128)
            hl = slice((s % 8) * 128, (s % 8 + 1) * 128)
            tile_ref, nxt_ref = ct_refs[s // 8]
            cat = jnp.concatenate([prev_ref[:, sl] * has_prev, raw_ref[:, sl], next_ref[:, sl] * has_next], axis=0)
            shifted = [pltpu.roll(cat, 3 - j, 0)[8:] for j in range(3)] + [cat[8:]]
            conv = w_ref[3:4, sl] * shifted[3]
            for j in range(3):
                conv = conv + w_ref[j:j + 1, sl] * shifted[j]
            ct = jnp.concatenate([tile_ref[:, hl], nxt_ref[:, hl] * has_next], axis=0)
            _, vjp_act = jax.vjp(_act_of(s), conv)
            dconv, = vjp_act(ct)
            draw = w_ref[3:4, sl] * dconv[:TM]
            for j in range(3):
                draw = draw + w_ref[j:j + 1, sl] * pltpu.roll(dconv, ext - (3 - j), 0)[:TM]
            draw_ref[:, sl] = draw
            for j in range(4):
                dw_ref[j:j + 1, sl] += jnp.sum(dconv[:TM] * shifted[j][:TM], axis=0, keepdims=True)

    prev = pl.BlockSpec((8, QKV), lambda i: (jnp.maximum(i * (TM // 8) - 1, 0), 0))
    nxt = lambda n: pl.BlockSpec((8, n), lambda i: (jnp.minimum((i + 1) * (TM // 8), s_len // 8 - 1), 0))
    row128 = _fix((1, 128))
    ct_specs = [_row(TM, 1024), nxt(1024)] * 3
    return pl.pallas_call(
        body, name="gdn_prep_bwd", grid=(n_tiles,),
        in_specs=[_row(TM, QKV), prev, nxt(QKV), _row(TM, 128), _fix((C_CONV, QKV)), row128, row128] + ct_specs + [_row(TM, 128)],
        out_specs=[_row(TM, QKV), _row(TM, 128), _fix((C_CONV, QKV)), row128, row128],
        out_shape=[_sds(s_len, QKV), _sds(s_len, 128), _sds(C_CONV, QKV), _sds(1, 128), _sds(1, 128)],
        compiler_params=_cp("arbitrary"),
    )(raw, raw, raw, ba, conv_w, alog_row, dtb_row, dq, dq, dk, dk, dv, dv, dbg)


def _tein(eq, a, b):
    return jnp.einsum(eq, a, b, precision=lax.Precision.HIGH, preferred_element_type=F32)


def _unit_lower_inverse(lower):
    ri = lax.broadcasted_iota(jnp.int32, (C_CHUNK, C_CHUNK), 0)
    ci = lax.broadcasted_iota(jnp.int32, (C_CHUNK, C_CHUNK), 1)
    p_mat = -lower
    inv = (ri == ci).astype(F32)[None] + p_mat
    for _ in range(5):
        p_mat = _tein('hij,hjk->hik', p_mat, p_mat)
        inv = inv + _tein('hij,hjk->hik', inv, p_mat)
    return inv


@jax.custom_vjp
def _known_inverse(lower, inv):
    return inv


def _known_inverse_fwd(lower, inv):
    return inv, inv


def _known_inverse_bwd(inv, d_inv):
    d_lower = -_tein('hik,hjk->hij', _tein('hji,hjk->hik', inv, d_inv), inv)
    return d_lower, jnp.zeros_like(inv)


_known_inverse.defvjp(_known_inverse_fwd, _known_inverse_bwd)


def _gdn_chunk(q, k, v, bg, state, inv_known=None):
    lane = lax.broadcasted_iota(jnp.int32, (1, 128), 1)
    ri = lax.broadcasted_iota(jnp.int32, (C_CHUNK, C_CHUNK), 0)
    ci = lax.broadcasted_iota(jnp.int32, (C_CHUNK, C_CHUNK), 1)
    gc_t = _hdot((ri >= ci).astype(F32), bg)
    beta = jnp.stack([jnp.sum(jnp.where(lane == h, bg, 0.0), axis=-1, keepdims=True) for h in range(C_HEADS)], axis=0)
    gc = jnp.stack([jnp.sum(jnp.where(lane == C_HEADS + h, gc_t, 0.0), axis=-1, keepdims=True) for h in range(C_HEADS)], axis=0)
    gc_rows = gc_t.T
    row_id = lax.broadcasted_iota(jnp.int32, (128, C_CHUNK), 0)
    gcj = jnp.stack([jnp.sum(jnp.where(row_id == C_HEADS + h, gc_rows, 0.0), axis=0, keepdims=True) for h in range(C_HEADS)], axis=0)
    tril, strict = (ri >= ci)[None], (ri > ci)[None]
    decay = jnp.exp(jnp.where(tril, gc - gcj, -1e30))
    kb = k * beta
    lower = jnp.where(strict, _bein('hid,hjd->hij', kb, k) * decay, 0.0)
    inv = _unit_lower_inverse(lower) if inv_known is None else _known_inverse(lower, inv_known)
    egc = jnp.exp(gc)
    u_c = _tein('hij,hjd->hid', inv, v * beta)
    w_c = _tein('hij,hjd->hid', inv, kb * egc)
    aqk = _bein('hid,hjd->hij', q, k) * decay
    rowi = lax.broadcasted_iota(jnp.int32, (1, C_CHUNK, 1), 1)
    g_last = jnp.sum(jnp.where(rowi == C_CHUNK - 1, gc, 0.0), axis=1, keepdims=True)
    kd = k * jnp.exp(g_last - gc)
    v_new = u_c - _bein('hik,hkv->hiv', w_c, state)
    o = _bein('hik,hkv->hiv', q * egc, state) + _bein('hij,hjv->hiv', aqk, v_new)
    return o, state * jnp.exp(g_last) + _bein('hik,hiv->hkv', kd, v_new), inv


def _heads(ref):
    return jnp.stack([ref[:, h * C_DK:(h + 1) * C_DK] for h in range(C_HEADS)], axis=0)


def _gdn_fwd(qkv, bg):
    s_len = qkv.shape[0]
    n_c = s_len // C_CHUNK

    def body(q_ref, k_ref, v_ref, bg_ref, o_ref, ss_ref, inv_ref, st_ref):
        _zero_at_first([st_ref])
        s0 = st_ref[...]
        ss_ref[0] = s0
        o, s2, inv = _gdn_chunk(_heads(q_ref), _heads(k_ref), _heads(v_ref), bg_ref[...], s0)
        st_ref[...] = s2
        inv_ref[0] = inv
        for h in range(C_HEADS):
            o_ref[:, h * C_DK:(h + 1) * C_DK] = o[h]

    col = lambda c: pl.BlockSpec((C_CHUNK, 1024), lambda i: (i, c))
    return pl.pallas_call(
        body, name="gdn_fwd", grid=(n_c,),
        in_specs=[col(0), col(1), col(2), _row(C_CHUNK, 128)],
        out_specs=[_row(C_CHUNK, 1024), pl.BlockSpec((1, C_HEADS, C_DK, C_DK), lambda i: (i, 0, 0, 0)),
                   pl.BlockSpec((1, C_HEADS, C_CHUNK, C_CHUNK), lambda i: (i, 0, 0, 0))],
        out_shape=[_sds(s_len, 1024), _sds(n_c, C_HEADS, C_DK, C_DK), _sds(n_c, C_HEADS, C_CHUNK, C_CHUNK)],
        scratch_shapes=[pltpu.VMEM((C_HEADS, C_DK, C_DK), F32)],
        compiler_params=_cp("arbitrary"),
    )(qkv, qkv, qkv, bg)


def _gdn_bwd(qkv, bg, states, invs, do):
    s_len = qkv.shape[0]
    n_c = s_len // C_CHUNK

    def body(q_ref, k_ref, v_ref, bg_ref, ss_ref, inv_ref, do_ref, dq_ref, dk_ref, dv_ref, dbg_ref, ds_ref):
        _zero_at_first([ds_ref])
        inv_known = inv_ref[0]
        chunk = lambda q, k, v, bg_, st: _gdn_chunk(q, k, v, bg_, st, inv_known)[:2]
        _, vjp = jax.vjp(chunk, _heads(q_ref), _heads(k_ref), _heads(v_ref), bg_ref[...], ss_ref[0])
        dq, dk, dv, dbg, ds = vjp((_heads(do_ref), ds_ref[...]))
        ds_ref[...] = ds
        dbg_ref[...] = dbg
        for h in range(C_HEADS):
            sl = slice(h * C_DK, (h + 1) * C_DK)
            dq_ref[:, sl], dk_ref[:, sl], dv_ref[:, sl] = dq[h], dk[h], dv[h]

    rev = lambda i: n_c - 1 - i
    col = lambda c: pl.BlockSpec((C_CHUNK, 1024), lambda i: (rev(i), c))
    row128 = pl.BlockSpec((C_CHUNK, 128), lambda i: (rev(i), 0))
    per_chunk = lambda n: pl.BlockSpec((1, C_HEADS, n, n), lambda i: (rev(i), 0, 0, 0))
    return pl.pallas_call(
        body, name="gdn_bwd", grid=(n_c,),
        in_specs=[col(0), col(1), col(2), row128, per_chunk(C_DK), per_chunk(C_CHUNK), col(0)],
        out_specs=[col(0), col(0), col(0), row128],
        out_shape=[_sds(s_len, 1024)] * 3 + [_sds(s_len, 128)],
        scratch_shapes=[pltpu.VMEM((C_HEADS, C_DK, C_DK), F32)],
        compiler_params=_cp("arbitrary"),
    )(qkv, qkv, qkv, bg, states, invs, do)


def _head_norm_gate(o, gate, norm_g):
    return (_rms(o) * norm_g) * jax.nn.silu(gate)


def _l1_out_fb(o, gate_c, x1, target, norm_g, w_out, post_g, gate):
    s_len = x1.shape[0]

    def body(o_ref, gc_ref, x1_ref, t_ref, ng_ref, w_ref, pg_ref, gt_ref,
             loss_ref, dres_ref, do_ref, dgc_ref, dw_ref, dng_ref, dpg_ref, dgt_ref):
        _zero_at_first([loss_ref, dw_ref, dng_ref, dpg_ref, dgt_ref])
        ng = ng_ref[...]
        ons, vjps = [], []
        for h in range(C_HEADS):
            sl = slice(h * C_DK, (h + 1) * C_DK)
            on, vjp_h = jax.vjp(_head_norm_gate, o_ref[:, sl], gc_ref[:, sl], ng)
            ons.append(on)
            vjps.append(vjp_h)
        on_all = jnp.concatenate(ons, axis=-1)
        y = _bdot(on_all, w_ref[...])
        x2, vjp2 = jax.vjp(_post_res, y, x1_ref[...], pg_ref[...], gt_ref[...])
        err = x2 - t_ref[...]
        _acc(loss_ref, jnp.full((1, 128), 0.5 * jnp.sum(jnp.mean(err * err, axis=-1)), F32))
        dx2 = err * (1.0 / D_MODEL)
        dy, _, dpg, dgt = vjp2(dx2)
        dres_ref[...] = dx2
        _acc(dpg_ref, dpg)
        _acc(dgt_ref, dgt)
        dw_ref[...] += _bdot_tn(on_all, dy)
        don = _bdot_nt(dy, w_ref[...])
        for h in range(C_HEADS):
            sl = slice(h * C_DK, (h + 1) * C_DK)
            do_h, dgc_h, dng = vjps[h](don[:, sl])
            do_ref[:, sl] = do_h
            dgc_ref[:, sl] = dgc_h
            _acc(dng_ref, dng)

    vec, r10 = _fix((1, D_MODEL)), _row(TM, D_MODEL)
    row128 = _fix((1, 128))
    return pl.pallas_call(
        body, name="l1_out_fb", grid=(s_len // TM,),
        in_specs=[r10, r10, r10, r10, row128, _fix((D_MODEL, D_MODEL)), vec, vec],
        out_specs=[row128, r10, r10, r10, _fix((D_MODEL, D_MODEL)), row128, vec, vec],
        out_shape=[_sds(1, 128), _sds(s_len, D_MODEL), _sds(s_len, D_MODEL), _sds(s_len, D_MODEL),
                   _sds(D_MODEL, D_MODEL), _sds(1, 128), _sds(1, D_MODEL), _sds(1, D_MODEL)],
        compiler_params=_cp("arbitrary"),
    )(o, gate_c, x1, target, norm_g, w_out, post_g, gate)


def _row_of(v, width, at):
    return jnp.zeros((1, width), F32).at[0, at:at + v.shape[-1]].set(v.reshape(-1))


def _local_step(x, target, mod, wd):
    s_len = x.shape[0]
    shift0, scale0, gate0 = (mod[0:1, i * 1024:(i + 1) * 1024] for i in range(3))
    shift1, scale1, gate1 = (mod[1:2, i * 1024:(i + 1) * 1024] for i in range(3))
    pre_g0, pre_g1 = wd["pre_g"][0:1], wd["pre_g"][1:2]
    post_g0, post_g1 = wd["post_g"][0:1], wd["post_g"][1:2]
    w_in0 = wd["ab_w_in"].astype(BF)
    w_out0 = wd["ab_w_out"].astype(BF)
    glu_w = wd["s5_glu_w"].astype(BF)
    w_in1 = jnp.concatenate([wd["gdn_w_in"], jnp.zeros((D_MODEL, C_IN_PAD - wd["gdn_w_in"].shape[1]), F32)], axis=1).astype(BF)
    w_out1 = wd["gdn_w_out"].astype(BF)
    d_skip, glu_b = wd["s5_d"].reshape(1, 512), wd["s5_glu_b"].reshape(1, 512)
    norm_g = wd["gdn_norm_g"].reshape(1, 128)
    alog_row = _row_of(wd["gdn_a_log"], 128, C_HEADS)
    dtb_row = _row_of(wd["gdn_dt_bias"], 128, C_HEADS)
    conv_w = wd["gdn_conv"]

    a_re, a_im = wd["s5_a_re"], wd["s5_a_im"]
    log_dt = wd["s5_log_dt"].reshape(B_GROUPS, 1)
    bt_re = wd["s5_b_re"].transpose(0, 2, 1).reshape(B_WIDTH, B_STATE)
    bt_im = wd["s5_b_im"].transpose(0, 2, 1).reshape(B_WIDTH, B_STATE)
    abar_r, abar_i, bbar_r, bbar_i = _s5_params(a_re, a_im, log_dt, bt_re, bt_im)
    abr, abi = abar_r.reshape(1, -1), abar_i.reshape(1, -1)
    btr, bti = _blockdiag_b(bbar_r).astype(BF), _blockdiag_b(bbar_i).astype(BF)
    ctr, cti = _blockdiag_c(wd["s5_c_re"]).astype(BF), _blockdiag_c(wd["s5_c_im"]).astype(BF)

    table = _bucket_table()
    biases = _attn_bias(wd["rel_bias"], jnp.asarray(table))
    front = _l0_front(x, pre_g0, scale0, shift0, w_in0)
    qs, ks, vs = front[0:3], front[3:6], front[6:9]
    u, ga, gb, h0 = front[9:]
    os, ls = zip(*[_attn_fwd(qs[i], ks[i], vs[i], biases[i]) for i in range(3)])
    xr, xi, ypre = _s5_scan_fwd(u, btr, bti, ctr, cti, abr, abi)
    x1, y0 = _l0_out(os, ls, ga, gb, ypre, u, x, d_skip, glu_w, glu_b, w_out0, post_g0, gate0)

    raw, gate_c, ba, h1 = _l1_front(x1, pre_g1, scale1, shift1, w_in1)
    qkv, bg = _gdn_prep(raw, ba, conv_w, alog_row, dtb_row)
    o_gdn, states, invs = _gdn_fwd(qkv, bg)
    loss_row, dres1, do_gdn, dgate_c, dw_out1, dnorm_g, dpost_g1, dgate1 = _l1_out_fb(
        o_gdn, gate_c, x1, target, norm_g, w_out1, post_g1, gate1)

    dq1, dk1, dv1, dbg = _gdn_bwd(qkv, bg, states, invs, do_gdn)
    draw, dba, dconv_w, dalog_row, ddtb_row = _gdn_prep_bwd(raw, ba, conv_w, alog_row, dtb_row, dq1, dk1, dv1, dbg)
    dz1, dx1, dpre_g1, dscale1, dshift1 = _front_bwd(
        "l1_front_bwd", x1, pre_g1, scale1, shift1, w_in1, dres1, [[draw], [dgate_c], [dba]], [QKV, 1024, 128])
    dw_in1 = _matmul_tn("l1_dw_in", h1, dz1, 1408)

    l0b = _l0_out_bwd(os, ls, ga, gb, ypre, u, x, y0, d_skip, glu_w, glu_b, w_out0, post_g0, gate0, dx1)
    dos, dls = l0b[0:3], l0b[3:6]
    dga, dgb, dypre, du_skip, dd_skip, dglu_w, dglu_b, dw_out0, dpost_g0, dgate0 = l0b[6:]
    du_scan, dbtr, dbti, dctr, dcti, dabr, dabi = _s5_scan_bwd(dypre, xr, xi, u, btr, bti, ctr, cti, abr, abi)
    dqs, dks, dvs, dbs = [], [], [], []
    for i in range(3):
        dq_d, dk_d, dv_d, db_d = _attn_bwd(qs[i], ks[i], vs[i], biases[i], os[i], ls[i], dos[i], dls[i])
        dqs.append(dq_d)
        dks.append(dk_d)
        dvs.append(dv_d)
        dbs.append(db_d)
    parts = [dqs, dks, dvs, [du_skip, du_scan], [dga], [dgb]]
    dz0, grad_x, dpre_g0, dscale0, dshift0 = _front_bwd(
        "l0_front_bwd", x, pre_g0, scale0, shift0, w_in0, dx1, parts, [512] * 6)
    dw_in0 = _matmul_tn("l0_dw_in", h0, dz0, 768)

    idx_rows = jnp.asarray(table.reshape(3, -1), F32)
    drel = _rel_bias_grad(dbs, idx_rows).T
    da_re, da_im, dlog_dt, dbt_re, dbt_im = _s5_params_bwd(
        a_re, a_im, log_dt, bt_re, bt_im, dabr.reshape(B_GROUPS, B_STATE), dabi.reshape(B_GROUPS, B_STATE),
        _blockdiag_b_t(dbtr), _blockdiag_b_t(dbti))
    unb = lambda d: d.reshape(B_GROUPS, B_GROUP, B_STATE).transpose(0, 2, 1)
    grads = {
        "pre_g": jnp.concatenate([dpre_g0, dpre_g1], 0), "post_g": jnp.concatenate([dpost_g0, dpost_g1], 0),
        "rel_bias": drel, "ab_w_in": dw_in0, "ab_w_out": dw_out0,
        "s5_a_re": da_re, "s5_a_im": da_im, "s5_log_dt": dlog_dt.reshape(B_GROUPS),
        "s5_b_re": unb(dbt_re), "s5_b_im": unb(dbt_im),
        "s5_c_re": _blockdiag_c_t(dctr), "s5_c_im": _blockdiag_c_t(dcti),
        "s5_d": dd_skip.reshape(512), "s5_glu_w": dglu_w, "s5_glu_b": dglu_b.reshape(512),
        "gdn_w_in": dw_in1[:, :wd["gdn_w_in"].shape[1]], "gdn_conv": dconv_w,
        "gdn_a_log": dalog_row[0, C_HEADS:2 * C_HEADS], "gdn_dt_bias": ddtb_row[0, C_HEADS:2 * C_HEADS],
        "gdn_norm_g": dnorm_g.reshape(128), "gdn_w_out": dw_out1,
    }
    dmod = jnp.concatenate([jnp.concatenate([dshift0, dscale0, dgate0], 1), jnp.concatenate([dshift1, dscale1, dgate1], 1)], 0)
    return loss_row[0, 0], grad_x, grads, dmod


def _place():
    return lax.axis_index("x"), lax.axis_index("y"), lax.axis_index("c")


def _flip(v, bit):
    return 1 - v if bit else v


def _hbm_call(name, body, arrs, out_shapes, n_sem):
    any_spec = pl.BlockSpec(memory_space=pl.ANY)
    return pl.pallas_call(
        body, name=name,
        in_specs=[any_spec] * len(arrs), out_specs=[any_spec] * len(out_shapes), out_shape=out_shapes,
        scratch_shapes=[pltpu.SemaphoreType.DMA((n_sem,)), pltpu.SemaphoreType.DMA((n_sem,))],
    )(*arrs)


def _own_slot(gathered, own, slot):
    idx = lax.broadcasted_iota(jnp.int32, (gathered.shape[0],) + (1,) * own.ndim, 0)
    return jnp.where(idx == slot, own[None], gathered)


def _all_gather8(name, arr):
    def body(x_ref, out_ref, send_sems, recv_sems):
        x, y, c = _place()
        me = 4 * x + 2 * y + c
        sends, recvs = [], []
        for m in range(1, 8):
            peer = (_flip(x, m & 4), _flip(y, m & 2), _flip(c, m & 1))
            sends.append(pltpu.make_async_remote_copy(x_ref, out_ref.at[me], send_sems.at[m - 1], recv_sems.at[m - 1],
                                                      device_id=peer, device_id_type=MESH))
            recvs.append(pltpu.make_async_remote_copy(x_ref, out_ref.at[4 * peer[0] + 2 * peer[1] + peer[2]], send_sems.at[m - 1],
                                                      recv_sems.at[m - 1], device_id=peer, device_id_type=MESH))
        for cp in sends:
            cp.start()
        for cp in recvs:
            cp.wait_recv()
        for cp in sends:
            cp.wait_send()

    return _hbm_call(name, body, [arr], [jax.ShapeDtypeStruct((8,) + arr.shape, arr.dtype)], 7)[0]


def _chip_exchange(name, arrs, scatter):
    n = len(arrs)

    def body(*refs):
        ins, outs = refs[:n], refs[n:2 * n]
        send_sems, recv_sems = refs[2 * n:]
        x, y, c = _place()
        mine = 2 * x + y
        sends, recvs = [], []
        for a in range(n):
            for m in range(1, 4):
                px, py = _flip(x, m & 2), _flip(y, m & 1)
                k = 3 * a + m - 1
                src = ins[a].at[2 * px + py] if scatter else ins[a]
                sends.append(pltpu.make_async_remote_copy(src, outs[a].at[mine], send_sems.at[k], recv_sems.at[k],
                                                          device_id=(px, py, c), device_id_type=MESH))
                recvs.append(pltpu.make_async_remote_copy(src, outs[a].at[2 * px + py], send_sems.at[k], recv_sems.at[k],
                                                          device_id=(px, py, c), device_id_type=MESH))
        for cp in sends:
            cp.start()
        for cp in recvs:
            cp.wait_recv()
        for cp in sends:
            cp.wait_send()

    shapes = [jax.ShapeDtypeStruct(a.shape if scatter else (4,) + a.shape, a.dtype) for a in arrs]
    return _hbm_call(name, body, arrs, shapes, 3 * n)


def _sibling_exchange(name, arrs):
    n = len(arrs)

    def body(*refs):
        ins, outs = refs[:n], refs[n:2 * n]
        send_sems, recv_sems = refs[2 * n:]
        x, y, c = _place()
        copies = [pltpu.make_async_remote_copy(ins[a], outs[a], send_sems.at[a], recv_sems.at[a],
                                               device_id=(x, y, 1 - c), device_id_type=MESH) for a in range(n)]
        for cp in copies:
            cp.start()
        for cp in copies:
            cp.wait_recv()
        for cp in copies:
            cp.wait_send()

    return _hbm_call(name, body, arrs, [jax.ShapeDtypeStruct(a.shape, a.dtype) for a in arrs], n)


def _row_tile(rows):
    for t in (256, 128, 64, 32, 16, 8):
        if rows % t == 0:
            return t
    return rows


def _pair_sum(name, a, b, out_dtype):
    rows, cols = a.shape
    tr = _row_tile(rows)

    def body(a_ref, b_ref, o_ref):
        o_ref[...] = (a_ref[...] + b_ref[...]).astype(out_dtype)

    return pl.pallas_call(body, name=name, grid=(rows // tr,), in_specs=[_row(tr, cols)] * 2, out_specs=_row(tr, cols),
                          out_shape=_sds(rows, cols, dtype=out_dtype), compiler_params=_cp("arbitrary"))(a, b)


def _chip_sum(name, recv, partial, mine):
    n, rows, cols = recv.shape
    tr = _row_tile(rows)

    def body(mine_ref, *refs):
        own = refs[n][0].astype(F32)
        acc = None
        for s in range(n):
            term = jnp.where(mine_ref[0] == s, own, refs[s][0].astype(F32))
            acc = term if acc is None else acc + term
        refs[-1][...] = acc

    def slot_spec(s):
        return pl.BlockSpec((1, tr, cols), lambda i, m: (jnp.where(m[0] == s, (s + 1) % n, s), i, 0))

    grid_spec = pltpu.PrefetchScalarGridSpec(
        num_scalar_prefetch=1, grid=(rows // tr,),
        in_specs=[slot_spec(s) for s in range(n)] + [pl.BlockSpec((1, tr, cols), lambda i, m: (m[0], i, 0))],
        out_specs=pl.BlockSpec((tr, cols), lambda i, m: (i, 0)))
    return pl.pallas_call(body, name=name, grid_spec=grid_spec, out_shape=_sds(rows, cols),
                          compiler_params=_cp("arbitrary"))(mine, *([recv] * n), partial)


def _slot_sum(name, arr):
    n, rows, cols = arr.shape
    tr = _row_tile(rows)

    def body(*refs):
        acc = refs[0][0]
        for r in refs[1:-1]:
            acc = acc + r[0]
        refs[-1][...] = acc

    specs = [pl.BlockSpec((1, tr, cols), functools.partial(lambda s, i: (s, i, 0), s)) for s in range(n)]
    return pl.pallas_call(body, name=name, grid=(rows // tr,), in_specs=specs, out_specs=_row(tr, cols),
                          out_shape=_sds(rows, cols), compiler_params=_cp("arbitrary"))(*([arr] * n))


def _adamw(name, w, g, m, v):
    rows, cols = w.shape
    tr = _row_tile(rows)

    def body(w_ref, g_ref, m_ref, v_ref, d_ref, nm_ref, nv_ref):
        g_ = g_ref[...]
        m_ = ADAM_B1 * m_ref[...] + (1.0 - ADAM_B1) * g_
        v_ = ADAM_B2 * v_ref[...] + (1.0 - ADAM_B2) * (g_ * g_)
        m_hat = m_ / (1.0 - ADAM_B1 ** ADAM_STEP)
        v_hat = v_ / (1.0 - ADAM_B2 ** ADAM_STEP)
        d_ref[...] = -ADAM_LR * (m_hat / (jnp.sqrt(v_hat) + ADAM_EPS) + ADAM_WD * w_ref[...])
        nm_ref[...] = m_
        nv_ref[...] = v_

    spec = _row(tr, cols)
    return pl.pallas_call(body, name=name, grid=(rows // tr,), in_specs=[spec] * 4, out_specs=[spec] * 3,
                          out_shape=[_sds(rows, cols)] * 3, compiler_params=_cp("arbitrary"))(w, g, m, v)


def _adamw_halves(name, w, g_mine, g_sibling, m, v, core):
    rows, cols = w.shape
    half = rows // 2
    tr = _row_tile(half)
    per_half = half // tr

    def body(core_ref, w_ref, gm_ref, gs_ref, m_ref, v_ref, g_ref, d_ref, nm_ref, nv_ref):
        g_ = jnp.where(pl.program_id(0) // per_half == core_ref[0], gm_ref[...], gs_ref[...])
        m_ = ADAM_B1 * m_ref[...] + (1.0 - ADAM_B1) * g_
        v_ = ADAM_B2 * v_ref[...] + (1.0 - ADAM_B2) * (g_ * g_)
        m_hat = m_ / (1.0 - ADAM_B1 ** ADAM_STEP)
        v_hat = v_ / (1.0 - ADAM_B2 ** ADAM_STEP)
        g_ref[...] = g_
        d_ref[...] = -ADAM_LR * (m_hat / (jnp.sqrt(v_hat) + ADAM_EPS) + ADAM_WD * w_ref[...])
        nm_ref[...] = m_
        nv_ref[...] = v_

    full = pl.BlockSpec((tr, cols), lambda i, c: (i, 0))
    in_half = pl.BlockSpec((tr, cols), lambda i, c: (i % per_half, 0))
    grid_spec = pltpu.PrefetchScalarGridSpec(num_scalar_prefetch=1, grid=(rows // tr,),
                                             in_specs=[full, in_half, in_half, full, full], out_specs=[full] * 4)
    return pl.pallas_call(body, name=name, grid_spec=grid_spec, out_shape=[_sds(rows, cols)] * 4,
                          compiler_params=_cp("arbitrary"))(core, w, g_mine, g_sibling, m, v)


def _mod_local(c_all, ada_w):
    def body(c_ref, w_ref, o_ref):
        c_act = jax.nn.silu(c_ref[...])
        for l in range(2):
            o_ref[l] = _hdot(c_act, w_ref[l])

    return pl.pallas_call(body, name="mod_local", out_shape=_sds(2, 8, ada_w.shape[2]),
                          compiler_params=pltpu.CompilerParams(vmem_limit_bytes=VMEM_LIMIT_BYTES))(c_all, ada_w)


def _ada_w_grad(c_all, dmod_cols):
    def body(c_ref, d_ref, o_ref):
        c_act = jax.nn.silu(c_ref[...])
        for l in range(2):
            o_ref[l] = lax.dot_general(c_act, d_ref[l], (((0,), (0,)), ((), ())), precision=HI, preferred_element_type=F32)

    return pl.pallas_call(body, name="ada_w_grad", out_shape=_sds(2, D_MODEL, dmod_cols.shape[2]),
                          compiler_params=pltpu.CompilerParams(vmem_limit_bytes=VMEM_LIMIT_BYTES))(c_all, dmod_cols)


_SMALL = ("ada_b", "pre_g", "post_g", "rel_bias", "s5_a_re", "s5_a_im", "s5_log_dt", "s5_b_re", "s5_b_im", "s5_c_re", "s5_c_im",
          "s5_d", "s5_glu_b", "gdn_a_log", "gdn_dt_bias", "gdn_norm_g")
_SHARDED = ("ab_w_in", "ab_w_out", "s5_glu_w", "gdn_w_in", "gdn_w_out")
_COL_SHARDED = ("ab_w_in", "gdn_w_in")
_WEIGHTS = ("ada_w", "ada_b", "pre_g", "post_g", "rel_bias", "ab_w_in", "ab_w_out", "s5_a_re", "s5_a_im", "s5_log_dt", "s5_b_re",
            "s5_b_im", "s5_c_re", "s5_c_im", "s5_d", "s5_glu_w", "s5_glu_b", "gdn_w_in", "gdn_conv", "gdn_a_log", "gdn_dt_bias",
            "gdn_norm_g", "gdn_w_out")


def _rows128(n):
    return -(-n // 128)


def _pack(arrs, total_rows):
    pieces = []
    for a in arrs:
        flat = a.reshape(-1)
        pieces.append(jnp.pad(flat, (0, _rows128(flat.shape[0]) * 128 - flat.shape[0])).reshape(-1, 128))
    used = sum(p.shape[0] for p in pieces)
    pieces.append(jnp.zeros((total_rows - used, 128), F32))
    return jnp.concatenate(pieces, axis=0)


def _unpack(buf, shapes):
    out, at = [], 0
    for shp in shapes:
        n = int(np.prod(shp))
        out.append(buf[at:at + _rows128(n)].reshape(-1)[:n].reshape(shp))
        at += _rows128(n)
    return out


def _full_from_halves(name, g):
    if name in _COL_SHARDED:
        return g.transpose(0, 2, 1, 3).reshape(2 * g.shape[2], 4 * g.shape[3])
    return g.transpose(1, 0, 2, 3).reshape(8 * g.shape[2], g.shape[3])


def _shard_major(name, g):
    if name in _COL_SHARDED:
        return g.reshape(g.shape[0], 4, g.shape[1] // 4).transpose(1, 0, 2)
    return g.reshape(4, g.shape[0] // 4, g.shape[1])


def kernel(x, c, ada_w, ada_b, pre_g, post_g, rel_bias, ab_w_in, ab_w_out, s5_a_re, s5_a_im, s5_log_dt, s5_b_re, s5_b_im, s5_c_re, s5_c_im, s5_d, s5_glu_w, s5_glu_b, gdn_w_in, gdn_conv, gdn_a_log, gdn_dt_bias, gdn_norm_g, gdn_w_out, loss_target, m_ada_w, m_ada_b, m_pre_g, m_post_g, m_rel_bias, m_ab_w_in, m_ab_w_out, m_s5_a_re, m_s5_a_im, m_s5_log_dt, m_s5_b_re, m_s5_b_im, m_s5_c_re, m_s5_c_im, m_s5_d, m_s5_glu_w, m_s5_glu_b, m_gdn_w_in, m_gdn_conv, m_gdn_a_log, m_gdn_dt_bias, m_gdn_norm_g, m_gdn_w_out, v_ada_w, v_ada_b, v_pre_g, v_post_g, v_rel_bias, v_ab_w_in, v_ab_w_out, v_s5_a_re, v_s5_a_im, v_s5_log_dt, v_s5_b_re, v_s5_b_im, v_s5_c_re, v_s5_c_im, v_s5_d, v_s5_glu_w, v_s5_glu_b, v_gdn_w_in, v_gdn_conv, v_gdn_a_log, v_gdn_dt_bias, v_gdn_norm_g, v_gdn_w_out):
    w = dict(ada_w=ada_w, ada_b=ada_b, pre_g=pre_g, post_g=post_g, rel_bias=rel_bias, ab_w_in=ab_w_in, ab_w_out=ab_w_out,
             s5_a_re=s5_a_re, s5_a_im=s5_a_im, s5_log_dt=s5_log_dt, s5_b_re=s5_b_re, s5_b_im=s5_b_im, s5_c_re=s5_c_re, s5_c_im=s5_c_im,
             s5_d=s5_d, s5_glu_w=s5_glu_w, s5_glu_b=s5_glu_b, gdn_w_in=gdn_w_in, gdn_conv=gdn_conv, gdn_a_log=gdn_a_log,
             gdn_dt_bias=gdn_dt_bias, gdn_norm_g=gdn_norm_g, gdn_w_out=gdn_w_out)
    m = dict(ada_w=m_ada_w, ada_b=m_ada_b, pre_g=m_pre_g, post_g=m_post_g, rel_bias=m_rel_bias, ab_w_in=m_ab_w_in, ab_w_out=m_ab_w_out,
             s5_a_re=m_s5_a_re, s5_a_im=m_s5_a_im, s5_log_dt=m_s5_log_dt, s5_b_re=m_s5_b_re, s5_b_im=m_s5_b_im, s5_c_re=m_s5_c_re,
             s5_c_im=m_s5_c_im, s5_d=m_s5_d, s5_glu_w=m_s5_glu_w, s5_glu_b=m_s5_glu_b, gdn_w_in=m_gdn_w_in, gdn_conv=m_gdn_conv,
             gdn_a_log=m_gdn_a_log, gdn_dt_bias=m_gdn_dt_bias, gdn_norm_g=m_gdn_norm_g, gdn_w_out=m_gdn_w_out)
    v = dict(ada_w=v_ada_w, ada_b=v_ada_b, pre_g=v_pre_g, post_g=v_post_g, rel_bias=v_rel_bias, ab_w_in=v_ab_w_in, ab_w_out=v_ab_w_out,
             s5_a_re=v_s5_a_re, s5_a_im=v_s5_a_im, s5_log_dt=v_s5_log_dt, s5_b_re=v_s5_b_re, s5_b_im=v_s5_b_im, s5_c_re=v_s5_c_re,
             s5_c_im=v_s5_c_im, s5_d=v_s5_d, s5_glu_w=v_s5_glu_w, s5_glu_b=v_s5_glu_b, gdn_w_in=v_gdn_w_in, gdn_conv=v_gdn_conv,
             gdn_a_log=v_gdn_a_log, gdn_dt_bias=v_gdn_dt_bias, gdn_norm_g=v_gdn_norm_g, gdn_w_out=v_gdn_w_out)
    ix, iy, ic = _place()
    me = 4 * ix + 2 * iy + ic
    chip = 2 * ix + iy
    n_cols = ada_w.shape[2]

    mine_first = _pack([c, gdn_conv], 32)
    first = _own_slot(_all_gather8("gather_c_conv", mine_first), mine_first, me)
    c_all = first[:, 0:8].reshape(8, D_MODEL)
    conv_full = first[0::2, 8:32].reshape(4, C_CONV, n_cols).transpose(1, 0, 2).reshape(C_CONV, 4 * n_cols)
    mine_mod = _mod_local(c_all, ada_w)
    modl = _own_slot(_all_gather8("gather_mod", mine_mod), mine_mod, me)
    mod = lax.dynamic_index_in_dim(modl[0::2], me, axis=2, keepdims=False)
    mod = mod.transpose(1, 0, 2).reshape(2, 4 * n_cols) + ada_b

    halves = []
    for name in _SHARDED:
        shard = w[name][0].astype(BF)
        h = shard.shape[0] // 2
        halves.append(lax.dynamic_slice_in_dim(shard, ic * h, h, axis=0))
    from_chips = _chip_exchange("gather_w_chips", halves, False)
    my_halves = [_own_slot(g, own, chip) for g, own in zip(from_chips, halves)]
    their_halves = _sibling_exchange("gather_w_sibling", my_halves)
    wd = {name: w[name] for name in _SMALL if name != "ada_b"}
    wd = {k: (a if k in ("pre_g", "post_g", "rel_bias") else a[0]) for k, a in wd.items()}
    wd["gdn_conv"] = conv_full
    for name, a, b in zip(_SHARDED, my_halves, their_halves):
        wd[name] = _full_from_halves(name, jnp.where(ic == 0, jnp.stack([a, b], 0), jnp.stack([b, a], 0)))

    loss_local, grad_x, grads, dmod = _local_step(x[0], loss_target[0], mod, wd)
    loss = lax.psum(loss_local, ("x", "y", "c"))

    small_shapes = [w[name].shape for name in _SMALL] + [(C_CONV, 4 * n_cols)]
    small_rows = -(-sum(_rows128(int(np.prod(s))) for s in small_shapes) // 8) * 8
    partial = _pack([dmod] + [grads[name] for name in _SMALL[1:]] + [grads["gdn_conv"]], small_rows)
    every = _own_slot(_all_gather8("gather_small_grads", partial), partial, me)
    g_small = _slot_sum("sum_small_grads", every)
    g_list = _unpack(g_small, small_shapes)
    d_small, m_small, v_small = _adamw("adamw_small", _pack([w[n] for n in _SMALL], small_rows), g_small,
                                       _pack([m[n] for n in _SMALL], small_rows), _pack([v[n] for n in _SMALL], small_rows))
    out_g = dict(zip(_SMALL, g_list[:-1]))
    out_d = dict(zip(_SMALL, _unpack(d_small, small_shapes[:-1])))
    out_m = dict(zip(_SMALL, _unpack(m_small, small_shapes[:-1])))
    out_v = dict(zip(_SMALL, _unpack(v_small, small_shapes[:-1])))

    def update(name, g2d):
        shp = w[name].shape
        two_d = lambda a: a.reshape(-1, shp[-1])
        d_, m_, v_ = _adamw("adamw_" + name, two_d(w[name]), g2d, two_d(m[name]), two_d(v[name]))
        out_g[name], out_d[name], out_m[name], out_v[name] = (a.reshape(shp) for a in (g2d, d_, m_, v_))

    update("gdn_conv", lax.dynamic_slice_in_dim(g_list[-1], chip * n_cols, n_cols, axis=1))

    dmod_all = every[:, 0:_rows128(2 * 3 * D_MODEL)].reshape(8, 2, 4, n_cols)
    dmod_cols = lax.dynamic_index_in_dim(dmod_all, chip, axis=2, keepdims=False).transpose(1, 0, 2)
    update("ada_w", _ada_w_grad(c_all, dmod_cols).reshape(-1, n_cols))

    mine, other = [], []
    for name in _SHARDED:
        sm = _shard_major(name, grads[name])
        h = sm.shape[1] // 2
        mine.append(lax.dynamic_slice_in_dim(sm, ic * h, h, axis=1))
        other.append(lax.dynamic_slice_in_dim(sm, (1 - ic) * h, h, axis=1))
    from_sibling = _sibling_exchange("reduce_sibling", other)
    chip_partials = []
    for name, a, b in zip(_SHARDED, mine, from_sibling):
        flat = lambda t: t.reshape(-1, t.shape[-1])
        chip_partials.append(_pair_sum("sum_sibling_" + name, flat(a), flat(b), BF).reshape(a.shape))
    from_all = _chip_exchange("reduce_chips", chip_partials, True)
    chip_1 = jnp.reshape(chip, (1,)).astype(jnp.int32)
    core_1 = jnp.reshape(ic, (1,)).astype(jnp.int32)
    reduced = [_chip_sum("sum_chips_" + name, t, p, chip_1) for name, t, p in zip(_SHARDED, from_all, chip_partials)]
    for name, g_mine, g_sib in zip(_SHARDED, reduced, _sibling_exchange("reduce_share", reduced)):
        shp = w[name].shape
        two_d = lambda a: a.reshape(-1, shp[-1])
        outs = _adamw_halves("adamw_" + name, two_d(w[name]), g_mine, g_sib, two_d(m[name]), two_d(v[name]), core_1)
        out_g[name], out_d[name], out_m[name], out_v[name] = (a.reshape(shp) for a in outs)

    return (loss, grad_x[None], *[out_g[n] for n in _WEIGHTS], *[out_d[n] for n in _WEIGHTS],
            *[out_m[n] for n in _WEIGHTS], *[out_v[n] for n in _WEIGHTS])
```

```python
import functools
import math

import numpy as np
import jax
import jax.numpy as jnp
from jax import lax
from jax.experimental import pallas as pl
from jax.experimental.pallas import tpu as pltpu

F32 = jnp.float32
BF = jnp.bfloat16
HI = lax.Precision.HIGHEST
MESH = pl.DeviceIdType.MESH

D_MODEL = 1024
EPS = 1e-6
A_HEADS, A_HD, A_WIDTH, A_BLOCK = 8, 64, 512, 128
DILATIONS = (1, 4, 16)
N_KEYS = 128
REL_BUCKETS, REL_MAX_DIST = 32, 2048
B_WIDTH, B_GROUP, B_GROUPS, B_STATE = 512, 16, 32, 64
S5_LANES = 512
S5_TILES = 4
S5_T = 256
C_HEADS, C_DK, C_CHUNK, C_CONV = 8, 128, 64, 4
QKV = 3072
C_IN_PAD = 4224
TM = 256
VMEM_LIMIT_BYTES = 56 * 1024 * 1024
ADAM_LR, ADAM_B1, ADAM_B2, ADAM_EPS, ADAM_WD, ADAM_STEP = 0.001, 0.9, 0.999, 1e-08, 0.01, 10
NEG = float(np.finfo(np.float32).min)


def _cp(*sem):
    return pltpu.CompilerParams(dimension_semantics=sem, vmem_limit_bytes=VMEM_LIMIT_BYTES)


def _bdot(a, b):
    return jnp.dot(a.astype(BF), b.astype(BF), preferred_element_type=F32)


def _bdot_nt(a, b):
    return lax.dot_general(a.astype(BF), b.astype(BF), (((1,), (1,)), ((), ())), preferred_element_type=F32)


def _bdot_tn(a, b):
    return lax.dot_general(a.astype(BF), b.astype(BF), (((0,), (0,)), ((), ())), preferred_element_type=F32)


def _hdot(a, b):
    return jnp.dot(a, b, precision=HI, preferred_element_type=F32)


def _bein(eq, a, b):
    return jnp.einsum(eq, a.astype(BF), b.astype(BF), preferred_element_type=F32)


def _hein(eq, a, b):
    return jnp.einsum(eq, a, b, precision=HI, preferred_element_type=F32)


def _row(tm, n):
    return pl.BlockSpec((tm, n), lambda i: (i, 0))


def _fix(shape):
    return pl.BlockSpec(shape, lambda i: (0,) * len(shape))


def _sds(*shape, dtype=F32):
    return jax.ShapeDtypeStruct(shape, dtype)


def _acc(ref, val):
    ref[...] += val


def _zero_at_first(refs, axis=0):
    @pl.when(pl.program_id(axis) == 0)
    def _():
        for r in refs:
            r[...] = jnp.zeros_like(r)


def _rms(x):
    return x * lax.rsqrt(jnp.mean(x * x, axis=-1, keepdims=True) + EPS)


def _pre_mod(x, g, scale, shift):
    return (_rms(x) * g) * (1.0 + scale) + shift


def _post_res(y, x, post_g, gate):
    return x + gate * (_rms(y) * post_g)


def _merge_gate(o1, o2, o3, l1, l2, l3, ga):
    m = jnp.maximum(jnp.maximum(l1, l2), l3)
    e1, e2, e3 = jnp.exp(l1 - m), jnp.exp(l2 - m), jnp.exp(l3 - m)
    inv = 1.0 / (e1 + e2 + e3)
    return ((e1 * inv) * o1 + (e2 * inv) * o2 + (e3 * inv) * o3) * jax.nn.silu(ga)


def _s5_gelu(ypre, u, d_skip):
    return jax.nn.gelu(ypre + d_skip * u)


def _s5_glu(yb, gl, gb):
    return yb * jax.nn.sigmoid(gl) * jax.nn.silu(gb)


def _l0_front(x, pre_g, scale, shift, w_in):
    s_len = x.shape[0]

    def body(x_ref, g_ref, sc_ref, sh_ref, w_ref, *out_refs):
        qkv_refs, (u_ref, ga_ref, gb_ref, h_ref) = out_refs[:9], out_refs[9:]
        hb = _pre_mod(x_ref[...], g_ref[...], sc_ref[...], sh_ref[...]).astype(BF)
        h_ref[...] = hb
        z = jnp.dot(hb, w_ref[...], preferred_element_type=F32)
        for a in range(3):
            piece = z[:, a * 512:(a + 1) * 512]
            for i, d in enumerate(DILATIONS):
                qkv_refs[3 * a + i][...] = _to_res(piece, d)
        u_ref[...] = z[:, 1536:2048]
        ga_ref[...] = z[:, 2048:2560]
        gb_ref[...] = z[:, 2560:3072]

    vec = _fix((1, D_MODEL))
    return pl.pallas_call(
        body, name="l0_front", grid=(s_len // TM,),
        in_specs=[_row(TM, D_MODEL), vec, vec, vec, _fix((D_MODEL, 3072))],
        out_specs=[_res_spec(d) for d in DILATIONS] * 3 + [_row(TM, 512)] * 3 + [_row(TM, D_MODEL)],
        out_shape=[_sds(*_res_shape(s_len, d)) for d in DILATIONS] * 3 + [_sds(s_len, 512)] * 3 + [_sds(s_len, D_MODEL, dtype=BF)],
        compiler_params=_cp("arbitrary"),
    )(x, pre_g, scale, shift, w_in)


def _front_bwd(name, x, pre_g, scale, shift, w_in, dres, parts, widths):
    s_len = x.shape[0]
    n_in = sum(len(p) for p in parts)
    n_cols = sum(widths)

    def body(*refs):
        x_ref, g_ref, sc_ref, sh_ref, w_ref, dres_ref = refs[:6]
        part_refs = refs[6:6 + n_in]
        dz_ref, dx_ref, dg_ref, dsc_ref, dsh_ref = refs[6 + n_in:]
        _zero_at_first([dg_ref, dsc_ref, dsh_ref])
        _, vjp = jax.vjp(_pre_mod, x_ref[...], g_ref[...], sc_ref[...], sh_ref[...])
        dh = jnp.zeros((TM, D_MODEL), F32)
        col, at = 0, 0
        for grp, width in zip(parts, widths):
            tile = lambda r: _from_res(r[...]) if len(r.shape) == 3 else r[...]
            dz = tile(part_refs[at])
            for r in part_refs[at + 1:at + len(grp)]:
                dz = dz + tile(r)
            at += len(grp)
            dzb = dz.astype(BF)
            dz_ref[:, col:col + width] = dzb
            dh = dh + lax.dot_general(dzb, w_ref[:, col:col + width], (((1,), (1,)), ((), ())), preferred_element_type=F32)
            col += width
        dx, dg, dsc, dsh = vjp(dh)
        dx_ref[...] = dx + dres_ref[...]
        _acc(dg_ref, dg)
        _acc(dsc_ref, dsc)
        _acc(dsh_ref, dsh)

    vec = _fix((1, D_MODEL))
    flat = [a for p in parts for a in p]
    return pl.pallas_call(
        body, name=name, grid=(s_len // TM,),
        in_specs=[_row(TM, D_MODEL), vec, vec, vec, _fix((D_MODEL, n_cols)), _row(TM, D_MODEL)]
        + [_res_spec(a.shape[0], a.shape[2]) if a.ndim == 3 else _row(TM, a.shape[1]) for a in flat],
        out_specs=[_row(TM, n_cols), _row(TM, D_MODEL), vec, vec, vec],
        out_shape=[_sds(s_len, n_cols, dtype=BF), _sds(s_len, D_MODEL), _sds(1, D_MODEL), _sds(1, D_MODEL), _sds(1, D_MODEL)],
        compiler_params=_cp("arbitrary"),
    )(x, pre_g, scale, shift, w_in, dres, *flat)


def _matmul_tn(name, a, b, tn):
    s_len, k_dim = a.shape
    n_dim = b.shape[1]
    ts = 512

    def body(a_ref, b_ref, o_ref):
        _zero_at_first([o_ref], axis=1)
        o_ref[...] += lax.dot_general(a_ref[...], b_ref[...], (((0,), (0,)), ((), ())), preferred_element_type=F32)

    return pl.pallas_call(
        body, name=name, grid=(n_dim // tn, s_len // ts),
        in_specs=[pl.BlockSpec((ts, k_dim), lambda j, i: (i, 0)), pl.BlockSpec((ts, tn), lambda j, i: (i, j))],
        out_specs=pl.BlockSpec((k_dim, tn), lambda j, i: (0, j)),
        out_shape=_sds(k_dim, n_dim),
        compiler_params=_cp("arbitrary", "arbitrary"),
    )(a, b)


def _t5_bucket_np(dist):
    dist = np.maximum(dist, 0)
    max_exact = REL_BUCKETS // 2
    large = max_exact + (np.log(np.maximum(dist, 1) / max_exact)
                         / math.log(REL_MAX_DIST / max_exact) * (REL_BUCKETS - max_exact)).astype(np.int32)
    large = np.minimum(large, REL_BUCKETS - 1)
    return np.where(dist < max_exact, dist, large).astype(np.int32)


def _to_res(z, dil):
    if dil == 1:
        return z[None]
    return jnp.swapaxes(z.reshape(z.shape[0] // dil, dil, z.shape[1]), 0, 1)


def _from_res(z):
    if z.shape[0] == 1:
        return z[0]
    return jnp.swapaxes(z, 0, 1).reshape(z.shape[0] * z.shape[1], z.shape[2])


def _res_shape(s_len, dil, width=A_WIDTH):
    return (dil, s_len // dil, width)


def _res_spec(dil, width=A_WIDTH):
    return pl.BlockSpec((dil, TM // dil, width), lambda i: (0, i, 0))


def _bucket_table():
    qi = np.arange(A_BLOCK)[:, None]
    kj = np.arange(2 * A_BLOCK)[None, :]
    return np.stack([_t5_bucket_np((qi + A_BLOCK - kj) * d) for d in DILATIONS], 0)


def _attn_mask(first):
    qi = lax.broadcasted_iota(jnp.int32, (A_BLOCK, 2 * A_BLOCK), 0)
    kj = lax.broadcasted_iota(jnp.int32, (A_BLOCK, 2 * A_BLOCK), 1)
    rel = qi + A_BLOCK - kj
    return (rel >= 0) & (rel <= N_KEYS) & (jnp.logical_not(first) | (kj >= A_BLOCK))


def _attn_specs(nb, rev):
    n_of = (lambda i: nb - 1 - i) if rev else (lambda i: i)
    cur = pl.BlockSpec((None, A_BLOCK, A_WIDTH), lambda r, i: (r, n_of(i), 0))
    prev = pl.BlockSpec((None, A_BLOCK, A_WIDTH), lambda r, i: (r, jnp.maximum(n_of(i) - 1, 0), 0))
    bias = pl.BlockSpec((A_HEADS, A_BLOCK, 2 * A_BLOCK), lambda r, i: (0, 0, 0))
    return cur, prev, bias


def _attn_fwd(q, k, v, bias):
    dil, t_len, _ = q.shape
    nb = t_len // A_BLOCK
    scale = A_HD ** -0.5

    def body(q_ref, kp_ref, kc_ref, vp_ref, vc_ref, b_ref, o_ref, l_ref):
        mask = _attn_mask(pl.program_id(1) == 0)
        lane = lax.broadcasted_iota(jnp.int32, (1, 128), 1)
        for hp in range(A_HEADS // 2):
            sl = slice(hp * 128, (hp + 1) * 128)
            qp = q_ref[:, sl]
            kw = jnp.concatenate([kp_ref[:, sl], kc_ref[:, sl]], axis=0).astype(BF)
            vw = jnp.concatenate([vp_ref[:, sl], vc_ref[:, sl]], axis=0).astype(BF)
            outs, lses = [], []
            for j in range(2):
                hm = (lane < 64) if j == 0 else (lane >= 64)
                s = _bdot_nt(jnp.where(hm, qp, 0.0), kw) * scale
                s = jnp.where(mask, s + b_ref[2 * hp + j], NEG)
                m = jnp.max(s, axis=-1, keepdims=True)
                p = jnp.exp(s - m)
                den = jnp.sum(p, axis=-1, keepdims=True)
                outs.append(_bdot(p, vw) / den)
                lses.append(m + jnp.log(den))
            hm0 = lane < 64
            o_ref[:, sl] = jnp.where(hm0, outs[0], outs[1])
            l_ref[:, sl] = jnp.where(hm0, lses[0], lses[1])

    cur, prev, bias_spec = _attn_specs(nb, False)
    return pl.pallas_call(
        body, name=f"attn_fwd_d{dil}", grid=(dil, nb),
        in_specs=[cur, prev, cur, prev, cur, bias_spec],
        out_specs=[cur, cur],
        out_shape=[_sds(dil, t_len, A_WIDTH)] * 2,
        compiler_params=_cp("arbitrary", "arbitrary"),
    )(q, k, k, v, v, bias)


def _attn_bwd(q, k, v, bias, o, l, do, dl):
    dil, t_len, _ = q.shape
    nb = t_len // A_BLOCK
    scale = A_HD ** -0.5

    def body(q_ref, kp_ref, kc_ref, vp_ref, vc_ref, b_ref, o_ref, l_ref, do_ref, dl_ref,
             dq_ref, dk_ref, dv_ref, db_ref, ck_ref, cv_ref):
        _zero_at_first([ck_ref, cv_ref], axis=1)

        @pl.when((pl.program_id(0) == 0) & (pl.program_id(1) == 0))
        def _():
            db_ref[...] = jnp.zeros_like(db_ref)

        mask = _attn_mask(pl.program_id(1) == nb - 1)
        lane = lax.broadcasted_iota(jnp.int32, (1, 128), 1)
        for hp in range(A_HEADS // 2):
            sl = slice(hp * 128, (hp + 1) * 128)
            qp = q_ref[:, sl]
            kw = jnp.concatenate([kp_ref[:, sl], kc_ref[:, sl]], axis=0).astype(BF)
            vw = jnp.concatenate([vp_ref[:, sl], vc_ref[:, sl]], axis=0).astype(BF)
            op, lp, dop, dlp = o_ref[:, sl], l_ref[:, sl], do_ref[:, sl], dl_ref[:, sl]
            dq_acc = jnp.zeros((A_BLOCK, 128), F32)
            dk_acc = jnp.zeros((2 * A_BLOCK, 128), F32)
            dv_acc = jnp.zeros((2 * A_BLOCK, 128), F32)
            for j in range(2):
                hm = (lane < 64) if j == 0 else (lane >= 64)
                qm = jnp.where(hm, qp, 0.0)
                s = _bdot_nt(qm, kw) * scale
                s = jnp.where(mask, s + b_ref[2 * hp + j], NEG)
                lse = jnp.max(jnp.where(hm, lp, NEG), axis=-1, keepdims=True)
                p = jnp.exp(s - lse)
                do_h = jnp.where(hm, dop, 0.0)
                dd = jnp.sum(do_h * op, axis=-1, keepdims=True)
                dlse = jnp.sum(jnp.where(hm, dlp, 0.0), axis=-1, keepdims=True)
                ds = p * (_bdot_nt(do_h, vw) - dd + dlse)
                dv_acc = dv_acc + _bdot_tn(p, do_h)
                dq_acc = dq_acc + jnp.where(hm, _bdot(ds, kw), 0.0) * scale
                dk_acc = dk_acc + _bdot_tn(ds, qm) * scale
                db_ref[2 * hp + j] += ds
            dq_ref[:, sl] = dq_acc
            dk_ref[:, sl] = dk_acc[A_BLOCK:] + ck_ref[:, sl]
            dv_ref[:, sl] = dv_acc[A_BLOCK:] + cv_ref[:, sl]
            ck_ref[:, sl] = dk_acc[:A_BLOCK]
            cv_ref[:, sl] = dv_acc[:A_BLOCK]

    cur, prev, bias_spec = _attn_specs(nb, True)
    return pl.pallas_call(
        body, name=f"attn_bwd_d{dil}", grid=(dil, nb),
        in_specs=[cur, prev, cur, prev, cur, bias_spec, cur, cur, cur, cur],
        out_specs=[cur, cur, cur, bias_spec],
        out_shape=[_sds(dil, t_len, A_WIDTH)] * 3 + [_sds(A_HEADS, A_BLOCK, 2 * A_BLOCK)],
        scratch_shapes=[pltpu.VMEM((A_BLOCK, A_WIDTH), F32)] * 2,
        compiler_params=_cp("arbitrary", "arbitrary"),
    )(q, k, k, v, v, bias, o, l, do, dl)


def _attn_bias(rel_bias, table):
    def body(rb_ref, t_ref, *o_refs):
        for c in range(3):
            t = t_ref[c]
            acc = [jnp.zeros((A_BLOCK, 2 * A_BLOCK), F32) for _ in range(A_HEADS)]
            for b in range(REL_BUCKETS):
                hit = t == b
                acc = [jnp.where(hit, rb_ref[b, h], acc[h]) for h in range(A_HEADS)]
            for h in range(A_HEADS):
                o_refs[c][h] = acc[h]

    return pl.pallas_call(body, name="attn_bias", out_shape=[_sds(A_HEADS, A_BLOCK, 2 * A_BLOCK)] * 3,
                          in_specs=[pl.BlockSpec(memory_space=pltpu.SMEM), pl.BlockSpec(memory_space=pltpu.VMEM)],
                          compiler_params=pltpu.CompilerParams(vmem_limit_bytes=VMEM_LIMIT_BYTES))(rel_bias, table)


def _rel_bias_grad(dbs, idx_rows):
    n = A_BLOCK * 2 * A_BLOCK

    def body(d0_ref, d1_ref, d2_ref, idx_ref, o_ref):
        bucket = lax.broadcasted_iota(jnp.int32, (REL_BUCKETS, n), 0).astype(F32)
        acc = jnp.zeros((A_HEADS, REL_BUCKETS), F32)
        for c, db_ref in enumerate((d0_ref, d1_ref, d2_ref)):
            onehot = (idx_ref[c:c + 1, :] == bucket).astype(F32)
            acc = acc + lax.dot_general(db_ref[...], onehot, (((1,), (1,)), ((), ())), precision=HI, preferred_element_type=F32)
        o_ref[...] = acc

    return pl.pallas_call(body, name="rel_bias_grad", out_shape=_sds(A_HEADS, REL_BUCKETS),
                          compiler_params=pltpu.CompilerParams(vmem_limit_bytes=VMEM_LIMIT_BYTES))(
                              *[d.reshape(A_HEADS, n) for d in dbs], idx_rows)


def _s5_param_fn(a_re, a_im, log_dt, bt_re, bt_im):
    dt = jnp.exp(log_dt)
    mag = jnp.exp(dt * a_re)
    abar_r, abar_i = mag * jnp.cos(dt * a_im), mag * jnp.sin(dt * a_im)
    den = a_re * a_re + a_im * a_im
    fr = ((abar_r - 1.0) * a_re + abar_i * a_im) / den
    fi = (abar_i * a_re - (abar_r - 1.0) * a_im) / den
    row = lax.broadcasted_iota(jnp.int32, (B_WIDTH, B_GROUPS), 0)
    grp = lax.broadcasted_iota(jnp.int32, (B_WIDTH, B_GROUPS), 1)
    expand = ((row // B_GROUP) == grp).astype(F32)
    fr_e, fi_e = _hdot(expand, fr), _hdot(expand, fi)
    return abar_r, abar_i, fr_e * bt_re - fi_e * bt_im, fr_e * bt_im + fi_e * bt_re


def _s5_params(a_re, a_im, log_dt, bt_re, bt_im):
    def body(ar, ai, ld, br, bi, o1, o2, o3, o4):
        o1[...], o2[...], o3[...], o4[...] = _s5_param_fn(ar[...], ai[...], ld[...], br[...], bi[...])

    return pl.pallas_call(body, name="s5_params",
                          out_shape=[_sds(B_GROUPS, B_STATE)] * 2 + [_sds(B_WIDTH, B_STATE)] * 2)(a_re, a_im, log_dt, bt_re, bt_im)


def _s5_params_bwd(a_re, a_im, log_dt, bt_re, bt_im, d1, d2, d3, d4):
    def body(ar, ai, ld, br, bi, c1, c2, c3, c4, o1, o2, o3, o4, o5):
        _, vjp = jax.vjp(_s5_param_fn, ar[...], ai[...], ld[...], br[...], bi[...])
        o1[...], o2[...], o3[...], o4[...], o5[...] = vjp((c1[...], c2[...], c3[...], c4[...]))

    return pl.pallas_call(body, name="s5_params_bwd",
                          out_shape=[_sds(B_GROUPS, B_STATE)] * 2 + [_sds(B_GROUPS, 1)] + [_sds(B_WIDTH, B_STATE)] * 2,
                          )(a_re, a_im, log_dt, bt_re, bt_im, d1, d2, d3, d4)


S5_SUB = 8
S5_GROUPS = S5_T // S5_SUB


def _dscan(xr, xi, cr, ci, period, reverse):
    n = xr.shape[0]
    rows = lax.broadcasted_iota(jnp.int32, xr.shape, 0)
    pos = rows % period
    k = 1
    while k < period:
        if reverse:
            keep, shift = pos < period - k, n - k
        else:
            keep, shift = pos >= k, k
        sr = jnp.where(keep, pltpu.roll(xr, shift, 0), 0.0)
        si = jnp.where(keep, pltpu.roll(xi, shift, 0), 0.0)
        xr, xi = xr + cr * sr - ci * si, xi + cr * si + ci * sr
        cr, ci = cr * cr - ci * ci, 2.0 * cr * ci
        k *= 2
    return xr, xi, cr, ci


def _pick_row(x, r):
    rows = lax.broadcasted_iota(jnp.int32, x.shape, 0)
    return jnp.sum(jnp.where(rows == r, x, 0.0), axis=0, keepdims=True)


def _scan_tables(ar, ai, reverse):
    rows = lax.broadcasted_iota(jnp.int32, (S5_SUB, S5_LANES), 0)
    at = rows == (S5_SUB - 1 if reverse else 0)
    p8r, p8i, a8r, a8i = _dscan(jnp.where(at, ar, 0.0), jnp.where(at, ai, 0.0), ar, ai, S5_SUB, reverse)
    grp = lax.broadcasted_iota(jnp.int32, (S5_GROUPS, S5_LANES), 0)
    at = grp == (S5_GROUPS - 1 if reverse else 0)
    pgr, pgi, _, _ = _dscan(jnp.where(at, a8r, 0.0), jnp.where(at, a8i, 0.0), a8r, a8i, S5_GROUPS, reverse)
    return p8r, p8i, pgr, pgi


def _block_scan(br, bi, ar, ai, carry, tables, reverse, work):
    p8r, p8i, pgr, pgi = tables
    cin_r, cin_i = carry
    xw_r, xw_i, ew_r, ew_i = work
    xr, xi, a8r, a8i = _dscan(br, bi, ar, ai, S5_SUB, reverse)
    xw_r[...] = xr
    xw_i[...] = xi
    end = 0 if reverse else S5_SUB - 1
    for g in range(S5_GROUPS):
        ew_r[g:g + 1, :] = xw_r[S5_SUB * g + end:S5_SUB * g + end + 1, :]
        ew_i[g:g + 1, :] = xw_i[S5_SUB * g + end:S5_SUB * g + end + 1, :]
    er, ei, _, _ = _dscan(ew_r[...], ew_i[...], a8r, a8i, S5_GROUPS, reverse)
    er, ei = er + pgr * cin_r - pgi * cin_i, ei + pgr * cin_i + pgi * cin_r
    grp = lax.broadcasted_iota(jnp.int32, (S5_GROUPS, S5_LANES), 0)
    if reverse:
        ew_r[...] = jnp.where(grp == S5_GROUPS - 1, cin_r, pltpu.roll(er, S5_GROUPS - 1, 0))
        ew_i[...] = jnp.where(grp == S5_GROUPS - 1, cin_i, pltpu.roll(ei, S5_GROUPS - 1, 0))
    else:
        ew_r[...] = jnp.where(grp == 0, cin_r, pltpu.roll(er, 1, 0))
        ew_i[...] = jnp.where(grp == 0, cin_i, pltpu.roll(ei, 1, 0))
    for g in range(S5_GROUPS):
        rows = slice(S5_SUB * g, S5_SUB * (g + 1))
        nr, ni = ew_r[g:g + 1, :], ew_i[g:g + 1, :]
        xw_r[rows, :] += p8r * nr - p8i * ni
        xw_i[rows, :] += p8r * ni + p8i * nr
    last = 0 if reverse else S5_GROUPS - 1
    return xw_r[...], xw_i[...], (_pick_row(er, last), _pick_row(ei, last))


def _s5_tile_specs(n_t, rev):
    t_of = (lambda i: n_t - 1 - i) if rev else (lambda i: i)
    u_spec = pl.BlockSpec((S5_T, 128), lambda j, i: (t_of(i), j))
    x_spec = pl.BlockSpec((S5_T, S5_LANES), lambda j, i: (t_of(i), j))
    b_spec = pl.BlockSpec((1, 128, S5_LANES), lambda j, i: (j, 0, 0))
    c_spec = pl.BlockSpec((1, S5_LANES, 128), lambda j, i: (j, 0, 0))
    a_spec = pl.BlockSpec((1, S5_LANES), lambda j, i: (0, j))
    return u_spec, x_spec, b_spec, c_spec, a_spec


def _s5_scratch(with_rows):
    return ([pltpu.VMEM((1, S5_LANES), F32)] * 2 + [pltpu.VMEM((S5_SUB, S5_LANES), F32)] * 2
            + [pltpu.VMEM((S5_GROUPS, S5_LANES), F32)] * 4 + ([pltpu.VMEM((S5_T, S5_LANES), F32)] * 2 if with_rows else []))


def _s5_scan_fwd(u, btr, bti, ctr, cti, abr, abi):
    s_len = u.shape[0]
    n_t = s_len // S5_T

    def body(u_ref, btr_ref, bti_ref, ctr_ref, cti_ref, ar_ref, ai_ref, xr_ref, xi_ref, y_ref, car, cai, p8r, p8i, pgr, pgi, ew_r, ew_i):
        ar, ai = ar_ref[...], ai_ref[...]

        @pl.when(pl.program_id(1) == 0)
        def _():
            car[...] = jnp.zeros_like(car)
            cai[...] = jnp.zeros_like(cai)
            p8r[...], p8i[...], pgr[...], pgi[...] = _scan_tables(ar, ai, False)

        ub = u_ref[...]
        xr, xi, (ncr, nci) = _block_scan(_bdot(ub, btr_ref[0]), _bdot(ub, bti_ref[0]), ar, ai, (car[...], cai[...]),
                                         (p8r[...], p8i[...], pgr[...], pgi[...]), False, (xr_ref, xi_ref, ew_r, ew_i))
        car[...] = ncr
        cai[...] = nci
        y_ref[...] = _bdot(xr, ctr_ref[0]) - _bdot(xi, cti_ref[0])

    u_spec, x_spec, b_spec, c_spec, a_spec = _s5_tile_specs(n_t, False)
    return pl.pallas_call(
        body, name="s5_scan_fwd", grid=(S5_TILES, n_t),
        in_specs=[u_spec, b_spec, b_spec, c_spec, c_spec, a_spec, a_spec],
        out_specs=[x_spec, x_spec, u_spec],
        out_shape=[_sds(s_len, S5_TILES * S5_LANES)] * 2 + [_sds(s_len, B_WIDTH)],
        scratch_shapes=_s5_scratch(False),
        compiler_params=_cp("arbitrary", "arbitrary"),
    )(u, btr, bti, ctr, cti, abr, abi)


def _s5_scan_bwd(dy, xr, xi, u, btr, bti, ctr, cti, abr, abi):
    s_len = u.shape[0]
    n_t = s_len // S5_T

    def body(dy_ref, xr_ref, xi_ref, xrp_ref, xip_ref, u_ref, btr_ref, bti_ref, ctr_ref, cti_ref, ar_ref, ai_ref,
             du_ref, dbtr_ref, dbti_ref, dctr_ref, dcti_ref, dar_ref, dai_ref, car, cai, p8r, p8i, pgr, pgi, ew_r, ew_i, xw_r, xw_i):
        ar, ai = ar_ref[...], ai_ref[...]
        i = pl.program_id(1)
        rows = lax.broadcasted_iota(jnp.int32, (S5_T, S5_LANES), 0)

        @pl.when(i == 0)
        def _():
            for r in (car, cai, dbtr_ref, dbti_ref, dctr_ref, dcti_ref, dar_ref, dai_ref):
                r[...] = jnp.zeros_like(r)
            p8r[...], p8i[...], pgr[...], pgi[...] = _scan_tables(ar, -ai, True)

        dyb = dy_ref[...]
        xr_b, xi_b, ub = xr_ref[...], xi_ref[...], u_ref[...]
        dctr_ref[0] += _bdot_tn(xr_b, dyb)
        dcti_ref[0] -= _bdot_tn(xi_b, dyb)
        gr, gi, (ncr, nci) = _block_scan(_bdot_nt(dyb, ctr_ref[0]), -_bdot_nt(dyb, cti_ref[0]), ar, -ai, (car[...], cai[...]),
                                         (p8r[...], p8i[...], pgr[...], pgi[...]), True, (xw_r, xw_i, ew_r, ew_i))
        car[...] = ncr
        cai[...] = nci
        du_ref[...] = _bdot_nt(gr, btr_ref[0]) + _bdot_nt(gi, bti_ref[0])
        dbtr_ref[0] += _bdot_tn(ub, gr)
        dbti_ref[0] += _bdot_tn(ub, gi)
        has_prev = (i < n_t - 1).astype(F32)
        hr = _pick_row(xrp_ref[...], 7) * has_prev
        hi = _pick_row(xip_ref[...], 7) * has_prev
        xpr = jnp.where(rows == 0, hr, pltpu.roll(xr_b, 1, 0))
        xpi = jnp.where(rows == 0, hi, pltpu.roll(xi_b, 1, 0))
        dar_ref[...] += jnp.sum(gr * xpr + gi * xpi, axis=0, keepdims=True)
        dai_ref[...] += jnp.sum(gi * xpr - gr * xpi, axis=0, keepdims=True)

    u_spec, x_spec, b_spec, c_spec, a_spec = _s5_tile_specs(n_t, True)
    halo = pl.BlockSpec((8, S5_LANES), lambda j, i: (jnp.maximum((n_t - 1 - i) * (S5_T // 8) - 1, 0), j))
    return pl.pallas_call(
        body, name="s5_scan_bwd", grid=(S5_TILES, n_t),
        in_specs=[u_spec, x_spec, x_spec, halo, halo, u_spec, b_spec, b_spec, c_spec, c_spec, a_spec, a_spec],
        out_specs=[u_spec, b_spec, b_spec, c_spec, c_spec, a_spec, a_spec],
        out_shape=[_sds(s_len, B_WIDTH)] + [_sds(S5_TILES, 128, S5_LANES)] * 2 + [_sds(S5_TILES, S5_LANES, 128)] * 2
        + [_sds(1, S5_TILES * S5_LANES)] * 2,
        scratch_shapes=_s5_scratch(True),
        compiler_params=_cp("arbitrary", "arbitrary"),
    )(dy, xr, xi, xr, xi, u, btr, bti, ctr, cti, abr, abi)


def _blockdiag_b(bbar_t):
    blocks = bbar_t.reshape(S5_TILES, 8, B_GROUP, B_STATE)
    return jnp.einsum('jgmp,gh->jgmhp', blocks, jnp.eye(8, dtype=F32)).reshape(S5_TILES, 128, S5_LANES)


def _blockdiag_b_t(d):
    return jnp.einsum('jgmgp->jgmp', d.reshape(S5_TILES, 8, B_GROUP, 8, B_STATE)).reshape(B_WIDTH, B_STATE)


def _blockdiag_c(c):
    blocks = c.reshape(S5_TILES, 8, B_GROUP, B_STATE)
    return jnp.einsum('jgmp,gh->jhpgm', blocks, jnp.eye(8, dtype=F32)).reshape(S5_TILES, S5_LANES, 128)


def _blockdiag_c_t(d):
    return jnp.einsum('jgpgm->jgmp', d.reshape(S5_TILES, 8, B_STATE, 8, B_GROUP)).reshape(B_GROUPS, B_GROUP, B_STATE)


def _l0_out(os, ls, ga, gb, ypre, u, x, d_skip, glu_w, glu_b, w_out, post_g, gate):
    s_len = x.shape[0]

    def body(o0, o1, o2, l0, l1, l2, ga_ref, gb_ref, yp_ref, u_ref, x_ref, d_ref, gw_ref, gbias_ref, w_ref, pg_ref, gt_ref, x1_ref, y_ref):
        oa = _merge_gate(*[_from_res(r[...]) for r in (o0, o1, o2, l0, l1, l2)], ga_ref[...])
        yb = _s5_gelu(yp_ref[...], u_ref[...], d_ref[...])
        ob = _s5_glu(yb, _bdot(yb, gw_ref[...]) + gbias_ref[...], gb_ref[...])
        y = _bdot(oa, w_ref[0:512, :]) + _bdot(ob, w_ref[512:1024, :])
        y_ref[...] = y
        x1_ref[...] = _post_res(y, x_ref[...], pg_ref[...], gt_ref[...])

    vec, half = _fix((1, D_MODEL)), _fix((1, 512))
    return pl.pallas_call(
        body, name="l0_out", grid=(s_len // TM,),
        in_specs=[_res_spec(d) for d in DILATIONS] * 2 + [_row(TM, 512)] * 4
        + [_row(TM, D_MODEL), half, _fix((512, 512)), half, _fix((D_MODEL, D_MODEL)), vec, vec],
        out_specs=[_row(TM, D_MODEL)] * 2,
        out_shape=[_sds(s_len, D_MODEL)] * 2,
        compiler_params=_cp("arbitrary"),
    )(*os, *ls, ga, gb, ypre, u, x, d_skip, glu_w, glu_b, w_out, post_g, gate)


def _l0_out_bwd(os, ls, ga, gb, ypre, u, x, y, d_skip, glu_w, glu_b, w_out, post_g, gate, dx1):
    s_len = x.shape[0]

    def body(o0, o1, o2, l0, l1, l2, ga_ref, gb_ref, yp_ref, u_ref, x_ref, y_ref, d_ref, gw_ref, gbias_ref, w_ref, pg_ref, gt_ref, dx1_ref,
             do0, do1, do2, dl0, dl1, dl2, dga_ref, dgb_ref, dyp_ref, du_ref, dd_ref, dgw_ref, dgbias_ref, dw_ref, dpg_ref, dgt_ref):
        _zero_at_first([dd_ref, dgw_ref, dgbias_ref, dw_ref, dpg_ref, dgt_ref])
        _, vjp2 = jax.vjp(_post_res, y_ref[...], x_ref[...], pg_ref[...], gt_ref[...])
        dy, _, dpg, dgt = vjp2(dx1_ref[...])
        _acc(dpg_ref, dpg)
        _acc(dgt_ref, dgt)
        oa, vjp_a = jax.vjp(_merge_gate, *[_from_res(r[...]) for r in (o0, o1, o2, l0, l1, l2)], ga_ref[...])
        yb, vjp_g = jax.vjp(_s5_gelu, yp_ref[...], u_ref[...], d_ref[...])
        gl = _bdot(yb, gw_ref[...]) + gbias_ref[...]
        ob, vjp_b = jax.vjp(_s5_glu, yb, gl, gb_ref[...])
        dw_ref[0:512, :] += _bdot_tn(oa, dy)
        dw_ref[512:1024, :] += _bdot_tn(ob, dy)
        d1, d2, d3, e1, e2, e3, dga = vjp_a(_bdot_nt(dy, w_ref[0:512, :]))
        for ref, val, d in zip((do0, do1, do2, dl0, dl1, dl2), (d1, d2, d3, e1, e2, e3), DILATIONS * 2):
            ref[...] = _to_res(val, d)
        dga_ref[...] = dga
        dyb, dgl, dgb = vjp_b(_bdot_nt(dy, w_ref[512:1024, :]))
        dgb_ref[...] = dgb
        dgw_ref[...] += _bdot_tn(yb, dgl)
        _acc(dgbias_ref, jnp.sum(dgl, axis=0, keepdims=True))
        dyp, du, dd = vjp_g(dyb + _bdot_nt(dgl, gw_ref[...]))
        dyp_ref[...] = dyp
        du_ref[...] = du
        _acc(dd_ref, dd)

    vec, half = _fix((1, D_MODEL)), _fix((1, 512))
    r5, r10 = _row(TM, 512), _row(TM, D_MODEL)
    res6 = [_res_spec(d) for d in DILATIONS] * 2
    return pl.pallas_call(
        body, name="l0_out_bwd", grid=(s_len // TM,),
        in_specs=res6 + [r5] * 4 + [r10, r10, half, _fix((512, 512)), half, _fix((D_MODEL, D_MODEL)), vec, vec, r10],
        out_specs=res6 + [r5] * 4 + [half, _fix((512, 512)), half, _fix((D_MODEL, D_MODEL)), vec, vec],
        out_shape=[_sds(*_res_shape(s_len, d)) for d in DILATIONS] * 2 + [_sds(s_len, 512)] * 4
        + [_sds(1, 512), _sds(512, 512), _sds(1, 512), _sds(D_MODEL, D_MODEL), _sds(1, D_MODEL), _sds(1, D_MODEL)],
        compiler_params=_cp("arbitrary"),
    )(*os, *ls, ga, gb, ypre, u, x, y, d_skip, glu_w, glu_b, w_out, post_g, gate, dx1)


def _l1_front(x, pre_g, scale, shift, w_in):
    s_len = x.shape[0]

    def body(x_ref, g_ref, sc_ref, sh_ref, w_ref, raw_ref, gate_ref, ba_ref, h_ref):
        hb = _pre_mod(x_ref[...], g_ref[...], sc_ref[...], sh_ref[...]).astype(BF)
        h_ref[...] = hb
        z = jnp.dot(hb, w_ref[...], preferred_element_type=F32)
        raw_ref[...] = z[:, 0:QKV]
        gate_ref[...] = z[:, QKV:QKV + 1024]
        ba_ref[...] = z[:, QKV + 1024:C_IN_PAD]

    vec = _fix((1, D_MODEL))
    return pl.pallas_call(
        body, name="l1_front", grid=(s_len // TM,),
        in_specs=[_row(TM, D_MODEL), vec, vec, vec, _fix((D_MODEL, C_IN_PAD))],
        out_specs=[_row(TM, QKV), _row(TM, 1024), _row(TM, 128), _row(TM, D_MODEL)],
        out_shape=[_sds(s_len, QKV), _sds(s_len, 1024), _sds(s_len, 128), _sds(s_len, D_MODEL, dtype=BF)],
        compiler_params=_cp("arbitrary"),
    )(x, pre_g, scale, shift, w_in)


def _bg_fn(ba, alog_row, dtb_row):
    lane = lax.broadcasted_iota(jnp.int32, (1, 128), 1)
    g = -jnp.exp(alog_row) * jax.nn.softplus(ba + dtb_row)
    return jnp.where(lane < C_HEADS, jax.nn.sigmoid(ba), jnp.where(lane < 2 * C_HEADS, g, 0.0))


def _act_q(c):
    q = jax.nn.silu(c)
    return q * lax.rsqrt(jnp.sum(q * q, axis=-1, keepdims=True) + EPS) * (C_DK ** -0.5)


def _act_k(c):
    k = jax.nn.silu(c)
    return k * lax.rsqrt(jnp.sum(k * k, axis=-1, keepdims=True) + EPS)


def _act_of(s):
    return _act_q if s < 8 else (_act_k if s < 16 else jax.nn.silu)


def _gdn_prep(raw, ba, conv_w, alog_row, dtb_row):
    s_len = raw.shape[0]

    def body(raw_ref, halo_ref, ba_ref, w_ref, al_ref, dt_ref, qkv_ref, bg_ref):
        bg_ref[...] = _bg_fn(ba_ref[...], al_ref[...], dt_ref[...])
        has_prev = (pl.program_id(0) > 0).astype(F32)
        for s in range(24):
            sl = slice(s * 128, (s + 1) * 128)
            cat = jnp.concatenate([halo_ref[:, sl] * has_prev, raw_ref[:, sl]], axis=0)
            conv = w_ref[3:4, sl] * cat[8:]
            for j in range(3):
                conv = conv + w_ref[j:j + 1, sl] * pltpu.roll(cat, 3 - j, 0)[8:]
            qkv_ref[:, sl] = _act_of(s)(conv)

    halo = pl.BlockSpec((8, QKV), lambda i: (jnp.maximum(i * (TM // 8) - 1, 0), 0))
    row128 = _fix((1, 128))
    return pl.pallas_call(
        body, name="gdn_prep", grid=(s_len // TM,),
        in_specs=[_row(TM, QKV), halo, _row(TM, 128), _fix((C_CONV, QKV)), row128, row128],
        out_specs=[_row(TM, QKV), _row(TM, 128)],
        out_shape=[_sds(s_len, QKV), _sds(s_len, 128)],
        compiler_params=_cp("arbitrary"),
    )(raw, raw, ba, conv_w, alog_row, dtb_row)


def _gdn_prep_bwd(raw, ba, conv_w, alog_row, dtb_row, dq, dk, dv, dbg):
    s_len = raw.shape[0]
    n_tiles = s_len // TM
    ext = TM + 8

    def body(raw_ref, prev_ref, next_ref, ba_ref, w_ref, al_ref, dt_ref, dq_ref, dqn_ref, dk_ref, dkn_ref, dv_ref, dvn_ref, dbg_ref,
             draw_ref, dba_ref, dw_ref, dal_ref, ddt_ref):
        _zero_at_first([dw_ref, dal_ref, ddt_ref])
        i = pl.program_id(0)
        _, vjp_bg = jax.vjp(_bg_fn, ba_ref[...], al_ref[...], dt_ref[...])
        dba, dal, ddt = vjp_bg(dbg_ref[...])
        dba_ref[...] = dba
        _acc(dal_ref, dal)
        _acc(ddt_ref, ddt)
        has_prev = (i > 0).astype(F32)
        has_next = (i < n_tiles - 1).astype(F32)
        ct_refs = ((dq_ref, dqn_ref), (dk_ref, dkn_ref), (dv_ref, dvn_ref))
        for s in range(24):
            sl = slice(s * 128, (s + 1) * 128)
            hl = slice((s % 8) * 128, (s % 8 + 1) * 128)
            tile_ref, nxt_ref = ct_refs[s // 8]
            cat = jnp.concatenate([prev_ref[:, sl] * has_prev, raw_ref[:, sl], next_ref[:, sl] * has_next], axis=0)
            shifted = [pltpu.roll(cat, 3 - j, 0)[8:] for j in range(3)] + [cat[8:]]
            conv = w_ref[3:4, sl] * shifted[3]
            for j in range(3):
                conv = conv + w_ref[j:j + 1, sl] * shifted[j]
            ct = jnp.concatenate([tile_ref[:, hl], nxt_ref[:, hl] * has_next], axis=0)
            _, vjp_act = jax.vjp(_act_of(s), conv)
            dconv, = vjp_act(ct)
            draw = w_ref[3:4, sl] * dconv[:TM]
            for j in range(3):
                draw = draw + w_ref[j:j + 1, sl] * pltpu.roll(dconv, ext - (3 - j), 0)[:TM]
            draw_ref[:, sl] = draw
            for j in range(4):
                dw_ref[j:j + 1, sl] += jnp.sum(dconv[:TM] * shifted[j][:TM], axis=0, keepdims=True)

    prev = pl.BlockSpec((8, QKV), lambda i: (jnp.maximum(i * (TM // 8) - 1, 0), 0))
    nxt = lambda n: pl.BlockSpec((8, n), lambda i: (jnp.minimum((i + 1) * (TM // 8), s_len // 8 - 1), 0))
    row128 = _fix((1, 128))
    ct_specs = [_row(TM, 1024), nxt(1024)] * 3
    return pl.pallas_call(
        body, name="gdn_prep_bwd", grid=(n_tiles,),
        in_specs=[_row(TM, QKV), prev, nxt(QKV), _row(TM, 128), _fix((C_CONV, QKV)), row128, row128] + ct_specs + [_row(TM, 128)],
        out_specs=[_row(TM, QKV), _row(TM, 128), _fix((C_CONV, QKV)), row128, row128],
        out_shape=[_sds(s_len, QKV), _sds(s_len, 128), _sds(C_CONV, QKV), _sds(1, 128), _sds(1, 128)],
        compiler_params=_cp("arbitrary"),
    )(raw, raw, raw, ba, conv_w, alog_row, dtb_row, dq, dq, dk, dk, dv, dv, dbg)


def _tein(eq, a, b):
    return jnp.einsum(eq, a, b, precision=lax.Precision.HIGH, preferred_element_type=F32)


def _unit_lower_inverse(lower):
    ri = lax.broadcasted_iota(jnp.int32, (C_CHUNK, C_CHUNK), 0)
    ci = lax.broadcasted_iota(jnp.int32, (C_CHUNK, C_CHUNK), 1)
    p_mat = -lower
    inv = (ri == ci).astype(F32)[None] + p_mat
    for _ in range(5):
        p_mat = _tein('hij,hjk->hik', p_mat, p_mat)
        inv = inv + _tein('hij,hjk->hik', inv, p_mat)
    return inv


@jax.custom_vjp
def _known_inverse(lower, inv):
    return inv


def _known_inverse_fwd(lower, inv):
    return inv, inv


def _known_inverse_bwd(inv, d_inv):
    d_lower = -_tein('hik,hjk->hij', _tein('hji,hjk->hik', inv, d_inv), inv)
    return d_lower, jnp.zeros_like(inv)


_known_inverse.defvjp(_known_inverse_fwd, _known_inverse_bwd)


def _gdn_chunk(q, k, v, bg, state, inv_known=None):
    lane = lax.broadcasted_iota(jnp.int32, (1, 128), 1)
    ri = lax.broadcasted_iota(jnp.int32, (C_CHUNK, C_CHUNK), 0)
    ci = lax.broadcasted_iota(jnp.int32, (C_CHUNK, C_CHUNK), 1)
    gc_t = _hdot((ri >= ci).astype(F32), bg)
    beta = jnp.stack([jnp.sum(jnp.where(lane == h, bg, 0.0), axis=-1, keepdims=True) for h in range(C_HEADS)], axis=0)
    gc = jnp.stack([jnp.sum(jnp.where(lane == C_HEADS + h, gc_t, 0.0), axis=-1, keepdims=True) for h in range(C_HEADS)], axis=0)
    gc_rows = gc_t.T
    row_id = lax.broadcasted_iota(jnp.int32, (128, C_CHUNK), 0)
    gcj = jnp.stack([jnp.sum(jnp.where(row_id == C_HEADS + h, gc_rows, 0.0), axis=0, keepdims=True) for h in range(C_HEADS)], axis=0)
    tril, strict = (ri >= ci)[None], (ri > ci)[None]
    decay = jnp.exp(jnp.where(tril, gc - gcj, -1e30))
    kb = k * beta
    lower = jnp.where(strict, _bein('hid,hjd->hij', kb, k) * decay, 0.0)
    inv = _unit_lower_inverse(lower) if inv_known is None else _known_inverse(lower, inv_known)
    egc = jnp.exp(gc)
    u_c = _tein('hij,hjd->hid', inv, v * beta)
    w_c = _tein('hij,hjd->hid', inv, kb * egc)
    aqk = _bein('hid,hjd->hij', q, k) * decay
    rowi = lax.broadcasted_iota(jnp.int32, (1, C_CHUNK, 1), 1)
    g_last = jnp.sum(jnp.where(rowi == C_CHUNK - 1, gc, 0.0), axis=1, keepdims=True)
    kd = k * jnp.exp(g_last - gc)
    v_new = u_c - _bein('hik,hkv->hiv', w_c, state)
    o = _bein('hik,hkv->hiv', q * egc, state) + _bein('hij,hjv->hiv', aqk, v_new)
    return o, state * jnp.exp(g_last) + _bein('hik,hiv->hkv', kd, v_new), inv


def _heads(ref):
    return jnp.stack([ref[:, h * C_DK:(h + 1) * C_DK] for h in range(C_HEADS)], axis=0)


def _gdn_fwd(qkv, bg):
    s_len = qkv.shape[0]
    n_c = s_len // C_CHUNK

    def body(q_ref, k_ref, v_ref, bg_ref, o_ref, ss_ref, inv_ref, st_ref):
        _zero_at_first([st_ref])
        s0 = st_ref[...]
        ss_ref[0] = s0
        o, s2, inv = _gdn_chunk(_heads(q_ref), _heads(k_ref), _heads(v_ref), bg_ref[...], s0)
        st_ref[...] = s2
        inv_ref[0] = inv
        for h in range(C_HEADS):
            o_ref[:, h * C_DK:(h + 1) * C_DK] = o[h]

    col = lambda c: pl.BlockSpec((C_CHUNK, 1024), lambda i: (i, c))
    return pl.pallas_call(
        body, name="gdn_fwd", grid=(n_c,),
        in_specs=[col(0), col(1), col(2), _row(C_CHUNK, 128)],
        out_specs=[_row(C_CHUNK, 1024), pl.BlockSpec((1, C_HEADS, C_DK, C_DK), lambda i: (i, 0, 0, 0)),
                   pl.BlockSpec((1, C_HEADS, C_CHUNK, C_CHUNK), lambda i: (i, 0, 0, 0))],
        out_shape=[_sds(s_len, 1024), _sds(n_c, C_HEADS, C_DK, C_DK), _sds(n_c, C_HEADS, C_CHUNK, C_CHUNK)],
        scratch_shapes=[pltpu.VMEM((C_HEADS, C_DK, C_DK), F32)],
        compiler_params=_cp("arbitrary"),
    )(qkv, qkv, qkv, bg)


def _gdn_bwd(qkv, bg, states, invs, do):
    s_len = qkv.shape[0]
    n_c = s_len // C_CHUNK

    def body(q_ref, k_ref, v_ref, bg_ref, ss_ref, inv_ref, do_ref, dq_ref, dk_ref, dv_ref, dbg_ref, ds_ref):
        _zero_at_first([ds_ref])
        inv_known = inv_ref[0]
        chunk = lambda q, k, v, bg_, st: _gdn_chunk(q, k, v, bg_, st, inv_known)[:2]
        _, vjp = jax.vjp(chunk, _heads(q_ref), _heads(k_ref), _heads(v_ref), bg_ref[...], ss_ref[0])
        dq, dk, dv, dbg, ds = vjp((_heads(do_ref), ds_ref[...]))
        ds_ref[...] = ds
        dbg_ref[...] = dbg
        for h in range(C_HEADS):
            sl = slice(h * C_DK, (h + 1) * C_DK)
            dq_ref[:, sl], dk_ref[:, sl], dv_ref[:, sl] = dq[h], dk[h], dv[h]

    rev = lambda i: n_c - 1 - i
    col = lambda c: pl.BlockSpec((C_CHUNK, 1024), lambda i: (rev(i), c))
    row128 = pl.BlockSpec((C_CHUNK, 128), lambda i: (rev(i), 0))
    per_chunk = lambda n: pl.BlockSpec((1, C_HEADS, n, n), lambda i: (rev(i), 0, 0, 0))
    return pl.pallas_call(
        body, name="gdn_bwd", grid=(n_c,),
        in_specs=[col(0), col(1), col(2), row128, per_chunk(C_DK), per_chunk(C_CHUNK), col(0)],
        out_specs=[col(0), col(0), col(0), row128],
        out_shape=[_sds(s_len, 1024)] * 3 + [_sds(s_len, 128)],
        scratch_shapes=[pltpu.VMEM((C_HEADS, C_DK, C_DK), F32)],
        compiler_params=_cp("arbitrary"),
    )(qkv, qkv, qkv, bg, states, invs, do)


def _head_norm_gate(o, gate, norm_g):
    return (_rms(o) * norm_g) * jax.nn.silu(gate)


def _l1_out_fb(o, gate_c, x1, target, norm_g, w_out, post_g, gate):
    s_len = x1.shape[0]

    def body(o_ref, gc_ref, x1_ref, t_ref, ng_ref, w_ref, pg_ref, gt_ref,
             loss_ref, dres_ref, do_ref, dgc_ref, dw_ref, dng_ref, dpg_ref, dgt_ref):
        _zero_at_first([loss_ref, dw_ref, dng_ref, dpg_ref, dgt_ref])
        ng = ng_ref[...]
        ons, vjps = [], []
        for h in range(C_HEADS):
            sl = slice(h * C_DK, (h + 1) * C_DK)
            on, vjp_h = jax.vjp(_head_norm_gate, o_ref[:, sl], gc_ref[:, sl], ng)
            ons.append(on)
            vjps.append(vjp_h)
        on_all = jnp.concatenate(ons, axis=-1)
        y = _bdot(on_all, w_ref[...])
        x2, vjp2 = jax.vjp(_post_res, y, x1_ref[...], pg_ref[...], gt_ref[...])
        err = x2 - t_ref[...]
        _acc(loss_ref, jnp.full((1, 128), 0.5 * jnp.sum(jnp.mean(err * err, axis=-1)), F32))
        dx2 = err * (1.0 / D_MODEL)
        dy, _, dpg, dgt = vjp2(dx2)
        dres_ref[...] = dx2
        _acc(dpg_ref, dpg)
        _acc(dgt_ref, dgt)
        dw_ref[...] += _bdot_tn(on_all, dy)
        don = _bdot_nt(dy, w_ref[...])
        for h in range(C_HEADS):
            sl = slice(h * C_DK, (h + 1) * C_DK)
            do_h, dgc_h, dng = vjps[h](don[:, sl])
            do_ref[:, sl] = do_h
            dgc_ref[:, sl] = dgc_h
            _acc(dng_ref, dng)

    vec, r10 = _fix((1, D_MODEL)), _row(TM, D_MODEL)
    row128 = _fix((1, 128))
    return pl.pallas_call(
        body, name="l1_out_fb", grid=(s_len // TM,),
        in_specs=[r10, r10, r10, r10, row128, _fix((D_MODEL, D_MODEL)), vec, vec],
        out_specs=[row128, r10, r10, r10, _fix((D_MODEL, D_MODEL)), row128, vec, vec],
        out_shape=[_sds(1, 128), _sds(s_len, D_MODEL), _sds(s_len, D_MODEL), _sds(s_len, D_MODEL),
                   _sds(D_MODEL, D_MODEL), _sds(1, 128), _sds(1, D_MODEL), _sds(1, D_MODEL)],
        compiler_params=_cp("arbitrary"),
    )(o, gate_c, x1, target, norm_g, w_out, post_g, gate)


def _row_of(v, width, at):
    return jnp.zeros((1, width), F32).at[0, at:at + v.shape[-1]].set(v.reshape(-1))


def _local_step(x, target, mod, wd):
    s_len = x.shape[0]
    shift0, scale0, gate0 = (mod[0:1, i * 1024:(i + 1) * 1024] for i in range(3))
    shift1, scale1, gate1 = (mod[1:2, i * 1024:(i + 1) * 1024] for i in range(3))
    pre_g0, pre_g1 = wd["pre_g"][0:1], wd["pre_g"][1:2]
    post_g0, post_g1 = wd["post_g"][0:1], wd["post_g"][1:2]
    w_in0 = wd["ab_w_in"].astype(BF)
    w_out0 = wd["ab_w_out"].astype(BF)
    glu_w = wd["s5_glu_w"].astype(BF)
    w_in1 = jnp.concatenate([wd["gdn_w_in"], jnp.zeros((D_MODEL, C_IN_PAD - wd["gdn_w_in"].shape[1]), F32)], axis=1).astype(BF)
    w_out1 = wd["gdn_w_out"].astype(BF)
    d_skip, glu_b = wd["s5_d"].reshape(1, 512), wd["s5_glu_b"].reshape(1, 512)
    norm_g = wd["gdn_norm_g"].reshape(1, 128)
    alog_row = _row_of(wd["gdn_a_log"], 128, C_HEADS)
    dtb_row = _row_of(wd["gdn_dt_bias"], 128, C_HEADS)
    conv_w = wd["gdn_conv"]

    a_re, a_im = wd["s5_a_re"], wd["s5_a_im"]
    log_dt = wd["s5_log_dt"].reshape(B_GROUPS, 1)
    bt_re = wd["s5_b_re"].transpose(0, 2, 1).reshape(B_WIDTH, B_STATE)
    bt_im = wd["s5_b_im"].transpose(0, 2, 1).reshape(B_WIDTH, B_STATE)
    abar_r, abar_i, bbar_r, bbar_i = _s5_params(a_re, a_im, log_dt, bt_re, bt_im)
    abr, abi = abar_r.reshape(1, -1), abar_i.reshape(1, -1)
    btr, bti = _blockdiag_b(bbar_r).astype(BF), _blockdiag_b(bbar_i).astype(BF)
    ctr, cti = _blockdiag_c(wd["s5_c_re"]).astype(BF), _blockdiag_c(wd["s5_c_im"]).astype(BF)

    table = _bucket_table()
    biases = _attn_bias(wd["rel_bias"], jnp.asarray(table))
    front = _l0_front(x, pre_g0, scale0, shift0, w_in0)
    qs, ks, vs = front[0:3], front[3:6], front[6:9]
    u, ga, gb, h0 = front[9:]
    os, ls = zip(*[_attn_fwd(qs[i], ks[i], vs[i], biases[i]) for i in range(3)])
    xr, xi, ypre = _s5_scan_fwd(u, btr, bti, ctr, cti, abr, abi)
    x1, y0 = _l0_out(os, ls, ga, gb, ypre, u, x, d_skip, glu_w, glu_b, w_out0, post_g0, gate0)

    raw, gate_c, ba, h1 = _l1_front(x1, pre_g1, scale1, shift1, w_in1)
    qkv, bg = _gdn_prep(raw, ba, conv_w, alog_row, dtb_row)
    o_gdn, states, invs = _gdn_fwd(qkv, bg)
    loss_row, dres1, do_gdn, dgate_c, dw_out1, dnorm_g, dpost_g1, dgate1 = _l1_out_fb(
        o_gdn, gate_c, x1, target, norm_g, w_out1, post_g1, gate1)

    dq1, dk1, dv1, dbg = _gdn_bwd(qkv, bg, states, invs, do_gdn)
    draw, dba, dconv_w, dalog_row, ddtb_row = _gdn_prep_bwd(raw, ba, conv_w, alog_row, dtb_row, dq1, dk1, dv1, dbg)
    dz1, dx1, dpre_g1, dscale1, dshift1 = _front_bwd(
        "l1_front_bwd", x1, pre_g1, scale1, shift1, w_in1, dres1, [[draw], [dgate_c], [dba]], [QKV, 1024, 128])
    dw_in1 = _matmul_tn("l1_dw_in", h1, dz1, 1408)

    l0b = _l0_out_bwd(os, ls, ga, gb, ypre, u, x, y0, d_skip, glu_w, glu_b, w_out0, post_g0, gate0, dx1)
    dos, dls = l0b[0:3], l0b[3:6]
    dga, dgb, dypre, du_skip, dd_skip, dglu_w, dglu_b, dw_out0, dpost_g0, dgate0 = l0b[6:]
    du_scan, dbtr, dbti, dctr, dcti, dabr, dabi = _s5_scan_bwd(dypre, xr, xi, u, btr, bti, ctr, cti, abr, abi)
    dqs, dks, dvs, dbs = [], [], [], []
    for i in range(3):
        dq_d, dk_d, dv_d, db_d = _attn_bwd(qs[i], ks[i], vs[i], biases[i], os[i], ls[i], dos[i], dls[i])
        dqs.append(dq_d)
        dks.append(dk_d)
        dvs.append(dv_d)
        dbs.append(db_d)
    parts = [dqs, dks, dvs, [du_skip, du_scan], [dga], [dgb]]
    dz0, grad_x, dpre_g0, dscale0, dshift0 = _front_bwd(
        "l0_front_bwd", x, pre_g0, scale0, shift0, w_in0, dx1, parts, [512] * 6)
    dw_in0 = _matmul_tn("l0_dw_in", h0, dz0, 768)

    idx_rows = jnp.asarray(table.reshape(3, -1), F32)
    drel = _rel_bias_grad(dbs, idx_rows).T
    da_re, da_im, dlog_dt, dbt_re, dbt_im = _s5_params_bwd(
        a_re, a_im, log_dt, bt_re, bt_im, dabr.reshape(B_GROUPS, B_STATE), dabi.reshape(B_GROUPS, B_STATE),
        _blockdiag_b_t(dbtr), _blockdiag_b_t(dbti))
    unb = lambda d: d.reshape(B_GROUPS, B_GROUP, B_STATE).transpose(0, 2, 1)
    grads = {
        "pre_g": jnp.concatenate([dpre_g0, dpre_g1], 0), "post_g": jnp.concatenate([dpost_g0, dpost_g1], 0),
        "rel_bias": drel, "ab_w_in": dw_in0, "ab_w_out": dw_out0,
        "s5_a_re": da_re, "s5_a_im": da_im, "s5_log_dt": dlog_dt.reshape(B_GROUPS),
        "s5_b_re": unb(dbt_re), "s5_b_im": unb(dbt_im),
        "s5_c_re": _blockdiag_c_t(dctr), "s5_c_im": _blockdiag_c_t(dcti),
        "s5_d": dd_skip.reshape(512), "s5_glu_w": dglu_w, "s5_glu_b": dglu_b.reshape(512),
        "gdn_w_in": dw_in1[:, :wd["gdn_w_in"].shape[1]], "gdn_conv": dconv_w,
        "gdn_a_log": dalog_row[0, C_HEADS:2 * C_HEADS], "gdn_dt_bias": ddtb_row[0, C_HEADS:2 * C_HEADS],
        "gdn_norm_g": dnorm_g.reshape(128), "gdn_w_out": dw_out1,
    }
    dmod = jnp.concatenate([jnp.concatenate([dshift0, dscale0, dgate0], 1), jnp.concatenate([dshift1, dscale1, dgate1], 1)], 0)
    return loss_row[0, 0], grad_x, grads, dmod


def _place():
    return lax.axis_index("x"), lax.axis_index("y"), lax.axis_index("c")


def _flip(v, bit):
    return 1 - v if bit else v


def _hbm_call(name, body, arrs, out_shapes, n_sem):
    any_spec = pl.BlockSpec(memory_space=pl.ANY)
    return pl.pallas_call(
        body, name=name,
        in_specs=[any_spec] * len(arrs), out_specs=[any_spec] * len(out_shapes), out_shape=out_shapes,
        scratch_shapes=[pltpu.SemaphoreType.DMA((n_sem,)), pltpu.SemaphoreType.DMA((n_sem,))],
    )(*arrs)


def _own_slot(gathered, own, slot):
    idx = lax.broadcasted_iota(jnp.int32, (gathered.shape[0],) + (1,) * own.ndim, 0)
    return jnp.where(idx == slot, own[None], gathered)


def _all_gather8(name, arr):
    def body(x_ref, out_ref, send_sems, recv_sems):
        x, y, c = _place()
        me = 4 * x + 2 * y + c
        sends, recvs = [], []
        for m in range(1, 8):
            peer = (_flip(x, m & 4), _flip(y, m & 2), _flip(c, m & 1))
            sends.append(pltpu.make_async_remote_copy(x_ref, out_ref.at[me], send_sems.at[m - 1], recv_sems.at[m - 1],
                                                      device_id=peer, device_id_type=MESH))
            recvs.append(pltpu.make_async_remote_copy(x_ref, out_ref.at[4 * peer[0] + 2 * peer[1] + peer[2]], send_sems.at[m - 1],
                                                      recv_sems.at[m - 1], device_id=peer, device_id_type=MESH))
        for cp in sends:
            cp.start()
        for cp in recvs:
            cp.wait_recv()
        for cp in sends:
            cp.wait_send()

    return _hbm_call(name, body, [arr], [jax.ShapeDtypeStruct((8,) + arr.shape, arr.dtype)], 7)[0]


def _chip_exchange(name, arrs, scatter):
    n = len(arrs)

    def body(*refs):
        ins, outs = refs[:n], refs[n:2 * n]
        send_sems, recv_sems = refs[2 * n:]
        x, y, c = _place()
        mine = 2 * x + y
        sends, recvs = [], []
        for a in range(n):
            for m in range(1, 4):
                px, py = _flip(x, m & 2), _flip(y, m & 1)
                k = 3 * a + m - 1
                src = ins[a].at[2 * px + py] if scatter else ins[a]
                sends.append(pltpu.make_async_remote_copy(src, outs[a].at[mine], send_sems.at[k], recv_sems.at[k],
                                                          device_id=(px, py, c), device_id_type=MESH))
                recvs.append(pltpu.make_async_remote_copy(src, outs[a].at[2 * px + py], send_sems.at[k], recv_sems.at[k],
                                                          device_id=(px, py, c), device_id_type=MESH))
        for cp in sends:
            cp.start()
        for cp in recvs:
            cp.wait_recv()
        for cp in sends:
            cp.wait_send()

    shapes = [jax.ShapeDtypeStruct(a.shape if scatter else (4,) + a.shape, a.dtype) for a in arrs]
    return _hbm_call(name, body, arrs, shapes, 3 * n)


def _sibling_exchange(name, arrs):
    n = len(arrs)

    def body(*refs):
        ins, outs = refs[:n], refs[n:2 * n]
        send_sems, recv_sems = refs[2 * n:]
        x, y, c = _place()
        copies = [pltpu.make_async_remote_copy(ins[a], outs[a], send_sems.at[a], recv_sems.at[a],
                                               device_id=(x, y, 1 - c), device_id_type=MESH) for a in range(n)]
        for cp in copies:
            cp.start()
        for cp in copies:
            cp.wait_recv()
        for cp in copies:
            cp.wait_send()

    return _hbm_call(name, body, arrs, [jax.ShapeDtypeStruct(a.shape, a.dtype) for a in arrs], n)


def _row_tile(rows):
    for t in (256, 128, 64, 32, 16, 8):
        if rows % t == 0:
            return t
    return rows


def _pair_sum(name, a, b, out_dtype):
    rows, cols = a.shape
    tr = _row_tile(rows)

    def body(a_ref, b_ref, o_ref):
        o_ref[...] = (a_ref[...] + b_ref[...]).astype(out_dtype)

    return pl.pallas_call(body, name=name, grid=(rows // tr,), in_specs=[_row(tr, cols)] * 2, out_specs=_row(tr, cols),
                          out_shape=_sds(rows, cols, dtype=out_dtype), compiler_params=_cp("arbitrary"))(a, b)


def _chip_sum(name, recv, partial, mine):
    n, rows, cols = recv.shape
    tr = _row_tile(rows)

    def body(mine_ref, *refs):
        own = refs[n][0].astype(F32)
        acc = None
        for s in range(n):
            term = jnp.where(mine_ref[0] == s, own, refs[s][0].astype(F32))
            acc = term if acc is None else acc + term
        refs[-1][...] = acc

    def slot_spec(s):
        return pl.BlockSpec((1, tr, cols), lambda i, m: (jnp.where(m[0] == s, (s + 1) % n, s), i, 0))

    grid_spec = pltpu.PrefetchScalarGridSpec(
        num_scalar_prefetch=1, grid=(rows // tr,),
        in_specs=[slot_spec(s) for s in range(n)] + [pl.BlockSpec((1, tr, cols), lambda i, m: (m[0], i, 0))],
        out_specs=pl.BlockSpec((tr, cols), lambda i, m: (i, 0)))
    return pl.pallas_call(body, name=name, grid_spec=grid_spec, out_shape=_sds(rows, cols),
                          compiler_params=_cp("arbitrary"))(mine, *([recv] * n), partial)


def _slot_sum(name, arr):
    n, rows, cols = arr.shape
    tr = _row_tile(rows)

    def body(*refs):
        acc = refs[0][0]
        for r in refs[1:-1]:
            acc = acc + r[0]
        refs[-1][...] = acc

    specs = [pl.BlockSpec((1, tr, cols), functools.partial(lambda s, i: (s, i, 0), s)) for s in range(n)]
    return pl.pallas_call(body, name=name, grid=(rows // tr,), in_specs=specs, out_specs=_row(tr, cols),
                          out_shape=_sds(rows, cols), compiler_params=_cp("arbitrary"))(*([arr] * n))


def _adamw(name, w, g, m, v):
    rows, cols = w.shape
    tr = _row_tile(rows)

    def body(w_ref, g_ref, m_ref, v_ref, d_ref, nm_ref, nv_ref):
        g_ = g_ref[...]
        m_ = ADAM_B1 * m_ref[...] + (1.0 - ADAM_B1) * g_
        v_ = ADAM_B2 * v_ref[...] + (1.0 - ADAM_B2) * (g_ * g_)
        m_hat = m_ / (1.0 - ADAM_B1 ** ADAM_STEP)
        v_hat = v_ / (1.0 - ADAM_B2 ** ADAM_STEP)
        d_ref[...] = -ADAM_LR * (m_hat / (jnp.sqrt(v_hat) + ADAM_EPS) + ADAM_WD * w_ref[...])
        nm_ref[...] = m_
        nv_ref[...] = v_

    spec = _row(tr, cols)
    return pl.pallas_call(body, name=name, grid=(rows // tr,), in_specs=[spec] * 4, out_specs=[spec] * 3,
                          out_shape=[_sds(rows, cols)] * 3, compiler_params=_cp("arbitrary"))(w, g, m, v)


def _adamw_halves(name, w, g_mine, g_sibling, m, v, core):
    rows, cols = w.shape
    half = rows // 2
    tr = _row_tile(half)
    per_half = half // tr

    def body(core_ref, w_ref, gm_ref, gs_ref, m_ref, v_ref, g_ref, d_ref, nm_ref, nv_ref):
        g_ = jnp.where(pl.program_id(0) // per_half == core_ref[0], gm_ref[...], gs_ref[...])
        m_ = ADAM_B1 * m_ref[...] + (1.0 - ADAM_B1) * g_
        v_ = ADAM_B2 * v_ref[...] + (1.0 - ADAM_B2) * (g_ * g_)
        m_hat = m_ / (1.0 - ADAM_B1 ** ADAM_STEP)
        v_hat = v_ / (1.0 - ADAM_B2 ** ADAM_STEP)
        g_ref[...] = g_
        d_ref[...] = -ADAM_LR * (m_hat / (jnp.sqrt(v_hat) + ADAM_EPS) + ADAM_WD * w_ref[...])
        nm_ref[...] = m_
        nv_ref[...] = v_

    full = pl.BlockSpec((tr, cols), lambda i, c: (i, 0))
    in_half = pl.BlockSpec((tr, cols), lambda i, c: (i % per_half, 0))
    grid_spec = pltpu.PrefetchScalarGridSpec(num_scalar_prefetch=1, grid=(rows // tr,),
                                             in_specs=[full, in_half, in_half, full, full], out_specs=[full] * 4)
    return pl.pallas_call(body, name=name, grid_spec=grid_spec, out_shape=[_sds(rows, cols)] * 4,
                          compiler_params=_cp("arbitrary"))(core, w, g_mine, g_sibling, m, v)


def _mod_local(c_all, ada_w):
    def body(c_ref, w_ref, o_ref):
        c_act = jax.nn.silu(c_ref[...])
        for l in range(2):
            o_ref[l] = _hdot(c_act, w_ref[l])

    return pl.pallas_call(body, name="mod_local", out_shape=_sds(2, 8, ada_w.shape[2]),
                          compiler_params=pltpu.CompilerParams(vmem_limit_bytes=VMEM_LIMIT_BYTES))(c_all, ada_w)


def _ada_w_grad(c_all, dmod_cols):
    def body(c_ref, d_ref, o_ref):
        c_act = jax.nn.silu(c_ref[...])
        for l in range(2):
            o_ref[l] = lax.dot_general(c_act, d_ref[l], (((0,), (0,)), ((), ())), precision=HI, preferred_element_type=F32)

    return pl.pallas_call(body, name="ada_w_grad", out_shape=_sds(2, D_MODEL, dmod_cols.shape[2]),
                          compiler_params=pltpu.CompilerParams(vmem_limit_bytes=VMEM_LIMIT_BYTES))(c_all, dmod_cols)


_SMALL = ("ada_b", "pre_g", "post_g", "rel_bias", "s5_a_re", "s5_a_im", "s5_log_dt", "s5_b_re", "s5_b_im", "s5_c_re", "s5_c_im",
          "s5_d", "s5_glu_b", "gdn_a_log", "gdn_dt_bias", "gdn_norm_g")
_SHARDED = ("ab_w_in", "ab_w_out", "s5_glu_w", "gdn_w_in", "gdn_w_out")
_COL_SHARDED = ("ab_w_in", "gdn_w_in")
_WEIGHTS = ("ada_w", "ada_b", "pre_g", "post_g", "rel_bias", "ab_w_in", "ab_w_out", "s5_a_re", "s5_a_im", "s5_log_dt", "s5_b_re",
            "s5_b_im", "s5_c_re", "s5_c_im", "s5_d", "s5_glu_w", "s5_glu_b", "gdn_w_in", "gdn_conv", "gdn_a_log", "gdn_dt_bias",
            "gdn_norm_g", "gdn_w_out")


def _rows128(n):
    return -(-n // 128)


def _pack(arrs, total_rows):
    pieces = []
    for a in arrs:
        flat = a.reshape(-1)
        pieces.append(jnp.pad(flat, (0, _rows128(flat.shape[0]) * 128 - flat.shape[0])).reshape(-1, 128))
    used = sum(p.shape[0] for p in pieces)
    pieces.append(jnp.zeros((total_rows - used, 128), F32))
    return jnp.concatenate(pieces, axis=0)


def _unpack(buf, shapes):
    out, at = [], 0
    for shp in shapes:
        n = int(np.prod(shp))
        out.append(buf[at:at + _rows128(n)].reshape(-1)[:n].reshape(shp))
        at += _rows128(n)
    return out


def _full_from_halves(name, g):
    if name in _COL_SHARDED:
        return g.transpose(0, 2, 1, 3).reshape(2 * g.shape[2], 4 * g.shape[3])
    return g.transpose(1, 0, 2, 3).reshape(8 * g.shape[2], g.shape[3])


def _shard_major(name, g):
    if name in _COL_SHARDED:
        return g.reshape(g.shape[0], 4, g.shape[1] // 4).transpose(1, 0, 2)
    return g.reshape(4, g.shape[0] // 4, g.shape[1])


def kernel(x, c, ada_w, ada_b, pre_g, post_g, rel_bias, ab_w_in, ab_w_out, s5_a_re, s5_a_im, s5_log_dt, s5_b_re, s5_b_im, s5_c_re, s5_c_im, s5_d, s5_glu_w, s5_glu_b, gdn_w_in, gdn_conv, gdn_a_log, gdn_dt_bias, gdn_norm_g, gdn_w_out, loss_target, m_ada_w, m_ada_b, m_pre_g, m_post_g, m_rel_bias, m_ab_w_in, m_ab_w_out, m_s5_a_re, m_s5_a_im, m_s5_log_dt, m_s5_b_re, m_s5_b_im, m_s5_c_re, m_s5_c_im, m_s5_d, m_s5_glu_w, m_s5_glu_b, m_gdn_w_in, m_gdn_conv, m_gdn_a_log, m_gdn_dt_bias, m_gdn_norm_g, m_gdn_w_out, v_ada_w, v_ada_b, v_pre_g, v_post_g, v_rel_bias, v_ab_w_in, v_ab_w_out, v_s5_a_re, v_s5_a_im, v_s5_log_dt, v_s5_b_re, v_s5_b_im, v_s5_c_re, v_s5_c_im, v_s5_d, v_s5_glu_w, v_s5_glu_b, v_gdn_w_in, v_gdn_conv, v_gdn_a_log, v_gdn_dt_bias, v_gdn_norm_g, v_gdn_w_out):
    w = dict(ada_w=ada_w, ada_b=ada_b, pre_g=pre_g, post_g=post_g, rel_bias=rel_bias, ab_w_in=ab_w_in, ab_w_out=ab_w_out,
             s5_a_re=s5_a_re, s5_a_im=s5_a_im, s5_log_dt=s5_log_dt, s5_b_re=s5_b_re, s5_b_im=s5_b_im, s5_c_re=s5_c_re, s5_c_im=s5_c_im,
             s5_d=s5_d, s5_glu_w=s5_glu_w, s5_glu_b=s5_glu_b, gdn_w_in=gdn_w_in, gdn_conv=gdn_conv, gdn_a_log=gdn_a_log,
             gdn_dt_bias=gdn_dt_bias, gdn_norm_g=gdn_norm_g, gdn_w_out=gdn_w_out)
    m = dict(ada_w=m_ada_w, ada_b=m_ada_b, pre_g=m_pre_g, post_g=m_post_g, rel_bias=m_rel_bias, ab_w_in=m_ab_w_in, ab_w_out=m_ab_w_out,
             s5_a_re=m_s5_a_re, s5_a_im=m_s5_a_im, s5_log_dt=m_s5_log_dt, s5_b_re=m_s5_b_re, s5_b_im=m_s5_b_im, s5_c_re=m_s5_c_re,
             s5_c_im=m_s5_c_im, s5_d=m_s5_d, s5_glu_w=m_s5_glu_w, s5_glu_b=m_s5_glu_b, gdn_w_in=m_gdn_w_in, gdn_conv=m_gdn_conv,
             gdn_a_log=m_gdn_a_log, gdn_dt_bias=m_gdn_dt_bias, gdn_norm_g=m_gdn_norm_g, gdn_w_out=m_gdn_w_out)
    v = dict(ada_w=v_ada_w, ada_b=v_ada_b, pre_g=v_pre_g, post_g=v_post_g, rel_bias=v_rel_bias, ab_w_in=v_ab_w_in, ab_w_out=v_ab_w_out,
             s5_a_re=v_s5_a_re, s5_a_im=v_s5_a_im, s5_log_dt=v_s5_log_dt, s5_b_re=v_s5_b_re, s5_b_im=v_s5_b_im, s5_c_re=v_s5_c_re,
             s5_c_im=v_s5_c_im, s5_d=v_s5_d, s5_glu_w=v_s5_glu_w, s5_glu_b=v_s5_glu_b, gdn_w_in=v_gdn_w_in, gdn_conv=v_gdn_conv,
             gdn_a_log=v_gdn_a_log, gdn_dt_bias=v_gdn_dt_bias, gdn_norm_g=v_gdn_norm_g, gdn_w_out=v_gdn_w_out)
    ix, iy, ic = _place()
    me = 4 * ix + 2 * iy + ic
    chip = 2 * ix + iy
    n_cols = ada_w.shape[2]

    mine_first = _pack([c, gdn_conv], 32)
    first = _own_slot(_all_gather8("gather_c_conv", mine_first), mine_first, me)
    c_all = first[:, 0:8].reshape(8, D_MODEL)
    conv_full = first[0::2, 8:32].reshape(4, C_CONV, n_cols).transpose(1, 0, 2).reshape(C_CONV, 4 * n_cols)
    mine_mod = _mod_local(c_all, ada_w)
    modl = _own_slot(_all_gather8("gather_mod", mine_mod), mine_mod, me)
    mod = lax.dynamic_index_in_dim(modl[0::2], me, axis=2, keepdims=False)
    mod = mod.transpose(1, 0, 2).reshape(2, 4 * n_cols) + ada_b

    halves = []
    for name in _SHARDED:
        shard = w[name][0].astype(BF)
        h = shard.shape[0] // 2
        halves.append(lax.dynamic_slice_in_dim(shard, ic * h, h, axis=0))
    from_chips = _chip_exchange("gather_w_chips", halves, False)
    my_halves = [_own_slot(g, own, chip) for g, own in zip(from_chips, halves)]
    their_halves = _sibling_exchange("gather_w_sibling", my_halves)
    wd = {name: w[name] for name in _SMALL if name != "ada_b"}
    wd = {k: (a if k in ("pre_g", "post_g", "rel_bias") else a[0]) for k, a in wd.items()}
    wd["gdn_conv"] = conv_full
    for name, a, b in zip(_SHARDED, my_halves, their_halves):
        wd[name] = _full_from_halves(name, jnp.where(ic == 0, jnp.stack([a, b], 0), jnp.stack([b, a], 0)))

    loss_local, grad_x, grads, dmod = _local_step(x[0], loss_target[0], mod, wd)
    loss = lax.psum(loss_local, ("x", "y", "c"))

    small_shapes = [w[name].shape for name in _SMALL] + [(C_CONV, 4 * n_cols)]
    small_rows = -(-sum(_rows128(int(np.prod(s))) for s in small_shapes) // 8) * 8
    partial = _pack([dmod] + [grads[name] for name in _SMALL[1:]] + [grads["gdn_conv"]], small_rows)
    every = _own_slot(_all_gather8("gather_small_grads", partial), partial, me)
    g_small = _slot_sum("sum_small_grads", every)
    g_list = _unpack(g_small, small_shapes)
    d_small, m_small, v_small = _adamw("adamw_small", _pack([w[n] for n in _SMALL], small_rows), g_small,
                                       _pack([m[n] for n in _SMALL], small_rows), _pack([v[n] for n in _SMALL], small_rows))
    out_g = dict(zip(_SMALL, g_list[:-1]))
    out_d = dict(zip(_SMALL, _unpack(d_small, small_shapes[:-1])))
    out_m = dict(zip(_SMALL, _unpack(m_small, small_shapes[:-1])))
    out_v = dict(zip(_SMALL, _unpack(v_small, small_shapes[:-1])))

    def update(name, g2d):
        shp = w[name].shape
        two_d = lambda a: a.reshape(-1, shp[-1])
        d_, m_, v_ = _adamw("adamw_" + name, two_d(w[name]), g2d, two_d(m[name]), two_d(v[name]))
        out_g[name], out_d[name], out_m[name], out_v[name] = (a.reshape(shp) for a in (g2d, d_, m_, v_))

    update("gdn_conv", lax.dynamic_slice_in_dim(g_list[-1], chip * n_cols, n_cols, axis=1))

    dmod_all = every[:, 0:_rows128(2 * 3 * D_MODEL)].reshape(8, 2, 4, n_cols)
    dmod_cols = lax.dynamic_index_in_dim(dmod_all, chip, axis=2, keepdims=False).transpose(1, 0, 2)
    update("ada_w", _ada_w_grad(c_all, dmod_cols).reshape(-1, n_cols))

    mine, other = [], []
    for name in _SHARDED:
        sm = _shard_major(name, grads[name])
        h = sm.shape[1] // 2
        mine.append(lax.dynamic_slice_in_dim(sm, ic * h, h, axis=1))
        other.append(lax.dynamic_slice_in_dim(sm, (1 - ic) * h, h, axis=1))
    from_sibling = _sibling_exchange("reduce_sibling", other)
    chip_partials = []
    for name, a, b in zip(_SHARDED, mine, from_sibling):
        flat = lambda t: t.reshape(-1, t.shape[-1])
        chip_partials.append(_pair_sum("sum_sibling_" + name, flat(a), flat(b), BF).reshape(a.shape))
    from_all = _chip_exchange("reduce_chips", chip_partials, True)
    chip_1 = jnp.reshape(chip, (1,)).astype(jnp.int32)
    core_1 = jnp.reshape(ic, (1,)).astype(jnp.int32)
    reduced = [_chip_sum("sum_chips_" + name, t, p, chip_1) for name, t, p in zip(_SHARDED, from_all, chip_partials)]
    for name, g_mine, g_sib in zip(_SHARDED, reduced, _sibling_exchange("reduce_share", reduced)):
        shp = w[name].shape
        two_d = lambda a: a.reshape(-1, shp[-1])
        outs = _adamw_halves("adamw_" + name, two_d(w[name]), g_mine, g_sib, two_d(m[name]), two_d(v[name]), core_1)
        out_g[name], out_d[name], out_m[name], out_v[name] = (a.reshape(shp) for a in outs)

    return (loss, grad_x[None], *[out_g[n] for n in _WEIGHTS], *[out_d[n] for n in _WEIGHTS],
            *[out_m[n] for n in _WEIGHTS], *[out_v[n] for n in _WEIGHTS])
```

```python
import functools
import math

import numpy as np
import jax
import jax.numpy as jnp
from jax import lax
from jax.experimental import pallas as pl
from jax.experimental.pallas import tpu as pltpu

F32 = jnp.float32
BF = jnp.bfloat16
HI = lax.Precision.HIGHEST
MESH = pl.DeviceIdType.MESH

D_MODEL = 1024
EPS = 1e-6
A_HEADS, A_HD, A_WIDTH, A_BLOCK = 8, 64, 512, 128
DILATIONS = (1, 4, 16)
N_KEYS = 128
REL_BUCKETS, REL_MAX_DIST = 32, 2048
B_WIDTH, B_GROUP, B_GROUPS, B_STATE = 512, 16, 32, 64
S5_LANES = 512
S5_TILES = 4
S5_T = 256
C_HEADS, C_DK, C_CHUNK, C_CONV = 8, 128, 64, 4
QKV = 3072
C_IN_PAD = 4224
TM = 256
VMEM_LIMIT_BYTES = 56 * 1024 * 1024
ADAM_LR, ADAM_B1, ADAM_B2, ADAM_EPS, ADAM_WD, ADAM_STEP = 0.001, 0.9, 0.999, 1e-08, 0.01, 10
NEG = float(np.finfo(np.float32).min)


def _cp(*sem):
    return pltpu.CompilerParams(dimension_semantics=sem, vmem_limit_bytes=VMEM_LIMIT_BYTES)


def _bdot(a, b):
    return jnp.dot(a.astype(BF), b.astype(BF), preferred_element_type=F32)


def _bdot_nt(a, b):
    return lax.dot_general(a.astype(BF), b.astype(BF), (((1,), (1,)), ((), ())), preferred_element_type=F32)


def _bdot_tn(a, b):
    return lax.dot_general(a.astype(BF), b.astype(BF), (((0,), (0,)), ((), ())), preferred_element_type=F32)


def _hdot(a, b):
    return jnp.dot(a, b, precision=HI, preferred_element_type=F32)


def _bein(eq, a, b):
    return jnp.einsum(eq, a.astype(BF), b.astype(BF), preferred_element_type=F32)


def _hein(eq, a, b):
    return jnp.einsum(eq, a, b, precision=HI, preferred_element_type=F32)


def _row(tm, n):
    return pl.BlockSpec((tm, n), lambda i: (i, 0))


def _fix(shape):
    return pl.BlockSpec(shape, lambda i: (0,) * len(shape))


def _sds(*shape, dtype=F32):
    return jax.ShapeDtypeStruct(shape, dtype)


def _acc(ref, val):
    ref[...] += val


def _zero_at_first(refs, axis=0):
    @pl.when(pl.program_id(axis) == 0)
    def _():
        for r in refs:
            r[...] = jnp.zeros_like(r)


def _rms(x):
    return x * lax.rsqrt(jnp.mean(x * x, axis=-1, keepdims=True) + EPS)


def _pre_mod(x, g, scale, shift):
    return (_rms(x) * g) * (1.0 + scale) + shift


def _post_res(y, x, post_g, gate):
    return x + gate * (_rms(y) * post_g)


def _merge_gate(o1, o2, o3, l1, l2, l3, ga):
    m = jnp.maximum(jnp.maximum(l1, l2), l3)
    e1, e2, e3 = jnp.exp(l1 - m), jnp.exp(l2 - m), jnp.exp(l3 - m)
    inv = 1.0 / (e1 + e2 + e3)
    return ((e1 * inv) * o1 + (e2 * inv) * o2 + (e3 * inv) * o3) * jax.nn.silu(ga)


def _s5_gelu(ypre, u, d_skip):
    return jax.nn.gelu(ypre + d_skip * u)


def _s5_glu(yb, gl, gb):
    return yb * jax.nn.sigmoid(gl) * jax.nn.silu(gb)


def _l0_front(x, pre_g, scale, shift, w_in):
    s_len = x.shape[0]

    def body(x_ref, g_ref, sc_ref, sh_ref, w_ref, *out_refs):
        qkv_refs, (u_ref, ga_ref, gb_ref, h_ref) = out_refs[:9], out_refs[9:]
        hb = _pre_mod(x_ref[...], g_ref[...], sc_ref[...], sh_ref[...]).astype(BF)
        h_ref[...] = hb
        z = jnp.dot(hb, w_ref[...], preferred_element_type=F32)
        for a in range(3):
            piece = z[:, a * 512:(a + 1) * 512]
            for i, d in enumerate(DILATIONS):
                qkv_refs[3 * a + i][...] = _to_res(piece, d)
        u_ref[...] = z[:, 1536:2048]
        ga_ref[...] = z[:, 2048:2560]
        gb_ref[...] = z[:, 2560:3072]

    vec = _fix((1, D_MODEL))
    return pl.pallas_call(
        body, name="l0_front", grid=(s_len // TM,),
        in_specs=[_row(TM, D_MODEL), vec, vec, vec, _fix((D_MODEL, 3072))],
        out_specs=[_res_spec(d) for d in DILATIONS] * 3 + [_row(TM, 512)] * 3 + [_row(TM, D_MODEL)],
        out_shape=[_sds(*_res_shape(s_len, d)) for d in DILATIONS] * 3 + [_sds(s_len, 512)] * 3 + [_sds(s_len, D_MODEL, dtype=BF)],
        compiler_params=_cp("arbitrary"),
    )(x, pre_g, scale, shift, w_in)


def _front_bwd(name, x, pre_g, scale, shift, w_in, dres, parts, widths):
    s_len = x.shape[0]
    n_in = sum(len(p) for p in parts)
    n_cols = sum(widths)

    def body(*refs):
        x_ref, g_ref, sc_ref, sh_ref, w_ref, dres_ref = refs[:6]
        part_refs = refs[6:6 + n_in]
        dz_ref, dx_ref, dg_ref, dsc_ref, dsh_ref = refs[6 + n_in:]
        _zero_at_first([dg_ref, dsc_ref, dsh_ref])
        _, vjp = jax.vjp(_pre_mod, x_ref[...], g_ref[...], sc_ref[...], sh_ref[...])
        dh = jnp.zeros((TM, D_MODEL), F32)
        col, at = 0, 0
        for grp, width in zip(parts, widths):
            tile = lambda r: _from_res(r[...]) if len(r.shape) == 3 else r[...]
            dz = tile(part_refs[at])
            for r in part_refs[at + 1:at + len(grp)]:
                dz = dz + tile(r)
            at += len(grp)
            dzb = dz.astype(BF)
            dz_ref[:, col:col + width] = dzb
            dh = dh + lax.dot_general(dzb, w_ref[:, col:col + width], (((1,), (1,)), ((), ())), preferred_element_type=F32)
            col += width
        dx, dg, dsc, dsh = vjp(dh)
        dx_ref[...] = dx + dres_ref[...]
        _acc(dg_ref, dg)
        _acc(dsc_ref, dsc)
        _acc(dsh_ref, dsh)

    vec = _fix((1, D_MODEL))
    flat = [a for p in parts for a in p]
    return pl.pallas_call(
        body, name=name, grid=(s_len // TM,),
        in_specs=[_row(TM, D_MODEL), vec, vec, vec, _fix((D_MODEL, n_cols)), _row(TM, D_MODEL)]
        + [_res_spec(a.shape[0], a.shape[2]) if a.ndim == 3 else _row(TM, a.shape[1]) for a in flat],
        out_specs=[_row(TM, n_cols), _row(TM, D_MODEL), vec, vec, vec],
        out_shape=[_sds(s_len, n_cols, dtype=BF), _sds(s_len, D_MODEL), _sds(1, D_MODEL), _sds(1, D_MODEL), _sds(1, D_MODEL)],
        compiler_params=_cp("arbitrary"),
    )(x, pre_g, scale, shift, w_in, dres, *flat)


def _matmul_tn(name, a, b, tn):
    s_len, k_dim = a.shape
    n_dim = b.shape[1]
    ts = 512

    def body(a_ref, b_ref, o_ref):
        _zero_at_first([o_ref], axis=1)
        o_ref[...] += lax.dot_general(a_ref[...], b_ref[...], (((0,), (0,)), ((), ())), preferred_element_type=F32)

    return pl.pallas_call(
        body, name=name, grid=(n_dim // tn, s_len // ts),
        in_specs=[pl.BlockSpec((ts, k_dim), lambda j, i: (i, 0)), pl.BlockSpec((ts, tn), lambda j, i: (i, j))],
        out_specs=pl.BlockSpec((k_dim, tn), lambda j, i: (0, j)),
        out_shape=_sds(k_dim, n_dim),
        compiler_params=_cp("arbitrary", "arbitrary"),
    )(a, b)


def _t5_bucket_np(dist):
    dist = np.maximum(dist, 0)
    max_exact = REL_BUCKETS // 2
    large = max_exact + (np.log(np.maximum(dist, 1) / max_exact)
                         / math.log(REL_MAX_DIST / max_exact) * (REL_BUCKETS - max_exact)).astype(np.int32)
    large = np.minimum(large, REL_BUCKETS - 1)
    return np.where(dist < max_exact, dist, large).astype(np.int32)


def _to_res(z, dil):
    if dil == 1:
        return z[None]
    return jnp.swapaxes(z.reshape(z.shape[0] // dil, dil, z.shape[1]), 0, 1)


def _from_res(z):
    if z.shape[0] == 1:
        return z[0]
    return jnp.swapaxes(z, 0, 1).reshape(z.shape[0] * z.shape[1], z.shape[2])


def _res_shape(s_len, dil, width=A_WIDTH):
    return (dil, s_len // dil, width)


def _res_spec(dil, width=A_WIDTH):
    return pl.BlockSpec((dil, TM // dil, width), lambda i: (0, i, 0))


def _bucket_table():
    qi = np.arange(A_BLOCK)[:, None]
    kj = np.arange(2 * A_BLOCK)[None, :]
    return np.stack([_t5_bucket_np((qi + A_BLOCK - kj) * d) for d in DILATIONS], 0)


def _attn_mask(first):
    qi = lax.broadcasted_iota(jnp.int32, (A_BLOCK, 2 * A_BLOCK), 0)
    kj = lax.broadcasted_iota(jnp.int32, (A_BLOCK, 2 * A_BLOCK), 1)
    rel = qi + A_BLOCK - kj
    return (rel >= 0) & (rel <= N_KEYS) & (jnp.logical_not(first) | (kj >= A_BLOCK))


def _attn_specs(nb, rev):
    n_of = (lambda i: nb - 1 - i) if rev else (lambda i: i)
    cur = pl.BlockSpec((None, A_BLOCK, A_WIDTH), lambda r, i: (r, n_of(i), 0))
    prev = pl.BlockSpec((None, A_BLOCK, A_WIDTH), lambda r, i: (r, jnp.maximum(n_of(i) - 1, 0), 0))
    bias = pl.BlockSpec((A_HEADS, A_BLOCK, 2 * A_BLOCK), lambda r, i: (0, 0, 0))
    return cur, prev, bias


def _attn_fwd(q, k, v, bias):
    dil, t_len, _ = q.shape
    nb = t_len // A_BLOCK
    scale = A_HD ** -0.5

    def body(q_ref, kp_ref, kc_ref, vp_ref, vc_ref, b_ref, o_ref, l_ref):
        mask = _attn_mask(pl.program_id(1) == 0)
        lane = lax.broadcasted_iota(jnp.int32, (1, 128), 1)
        for hp in range(A_HEADS // 2):
            sl = slice(hp * 128, (hp + 1) * 128)
            qp = q_ref[:, sl]
            kw = jnp.concatenate([kp_ref[:, sl], kc_ref[:, sl]], axis=0).astype(BF)
            vw = jnp.concatenate([vp_ref[:, sl], vc_ref[:, sl]], axis=0).astype(BF)
            outs, lses = [], []
            for j in range(2):
                hm = (lane < 64) if j == 0 else (lane >= 64)
                s = _bdot_nt(jnp.where(hm, qp, 0.0), kw) * scale
                s = jnp.where(mask, s + b_ref[2 * hp + j], NEG)
                m = jnp.max(s, axis=-1, keepdims=True)
                p = jnp.exp(s - m)
                den = jnp.sum(p, axis=-1, keepdims=True)
                outs.append(_bdot(p, vw) / den)
                lses.append(m + jnp.log(den))
            hm0 = lane < 64
            o_ref[:, sl] = jnp.where(hm0, outs[0], outs[1])
            l_ref[:, sl] = jnp.where(hm0, lses[0], lses[1])

    cur, prev, bias_spec = _attn_specs(nb, False)
    return pl.pallas_call(
        body, name=f"attn_fwd_d{dil}", grid=(dil, nb),
        in_specs=[cur, prev, cur, prev, cur, bias_spec],
        out_specs=[cur, cur],
        out_shape=[_sds(dil, t_len, A_WIDTH)] * 2,
        compiler_params=_cp("arbitrary", "arbitrary"),
    )(q, k, k, v, v, bias)


def _attn_bwd(q, k, v, bias, o, l, do, dl):
    dil, t_len, _ = q.shape
    nb = t_len // A_BLOCK
    scale = A_HD ** -0.5

    def body(q_ref, kp_ref, kc_ref, vp_ref, vc_ref, b_ref, o_ref, l_ref, do_ref, dl_ref,
             dq_ref, dk_ref, dv_ref, db_ref, ck_ref, cv_ref):
        _zero_at_first([ck_ref, cv_ref], axis=1)

        @pl.when((pl.program_id(0) == 0) & (pl.program_id(1) == 0))
        def _():
            db_ref[...] = jnp.zeros_like(db_ref)

        mask = _attn_mask(pl.program_id(1) == nb - 1)
        lane = lax.broadcasted_iota(jnp.int32, (1, 128), 1)
        for hp in range(A_HEADS // 2):
            sl = slice(hp * 128, (hp + 1) * 128)
            qp = q_ref[:, sl]
            kw = jnp.concatenate([kp_ref[:, sl], kc_ref[:, sl]], axis=0).astype(BF)
            vw = jnp.concatenate([vp_ref[:, sl], vc_ref[:, sl]], axis=0).astype(BF)
            op, lp, dop, dlp = o_ref[:, sl], l_ref[:, sl], do_ref[:, sl], dl_ref[:, sl]
            dq_acc = jnp.zeros((A_BLOCK, 128), F32)
            dk_acc = jnp.zeros((2 * A_BLOCK, 128), F32)
            dv_acc = jnp.zeros((2 * A_BLOCK, 128), F32)
            for j in range(2):
                hm = (lane < 64) if j == 0 else (lane >= 64)
                qm = jnp.where(hm, qp, 0.0)
                s = _bdot_nt(qm, kw) * scale
                s = jnp.where(mask, s + b_ref[2 * hp + j], NEG)
                lse = jnp.max(jnp.where(hm, lp, NEG), axis=-1, keepdims=True)
                p = jnp.exp(s - lse)
                do_h = jnp.where(hm, dop, 0.0)
                dd = jnp.sum(do_h * op, axis=-1, keepdims=True)
                dlse = jnp.sum(jnp.where(hm, dlp, 0.0), axis=-1, keepdims=True)
                ds = p * (_bdot_nt(do_h, vw) - dd + dlse)
                dv_acc = dv_acc + _bdot_tn(p, do_h)
                dq_acc = dq_acc + jnp.where(hm, _bdot(ds, kw), 0.0) * scale
                dk_acc = dk_acc + _bdot_tn(ds, qm) * scale
                db_ref[2 * hp + j] += ds
            dq_ref[:, sl] = dq_acc
            dk_ref[:, sl] = dk_acc[A_BLOCK:] + ck_ref[:, sl]
            dv_ref[:, sl] = dv_acc[A_BLOCK:] + cv_ref[:, sl]
            ck_ref[:, sl] = dk_acc[:A_BLOCK]
            cv_ref[:, sl] = dv_acc[:A_BLOCK]

    cur, prev, bias_spec = _attn_specs(nb, True)
    return pl.pallas_call(
        body, name=f"attn_bwd_d{dil}", grid=(dil, nb),
        in_specs=[cur, prev, cur, prev, cur, bias_spec, cur, cur, cur, cur],
        out_specs=[cur, cur, cur, bias_spec],
        out_shape=[_sds(dil, t_len, A_WIDTH)] * 3 + [_sds(A_HEADS, A_BLOCK, 2 * A_BLOCK)],
        scratch_shapes=[pltpu.VMEM((A_BLOCK, A_WIDTH), F32)] * 2,
        compiler_params=_cp("arbitrary", "arbitrary"),
    )(q, k, k, v, v, bias, o, l, do, dl)


def _attn_bias(rel_bias, table):
    def body(rb_ref, t_ref, *o_refs):
        for c in range(3):
            t = t_ref[c]
            acc = [jnp.zeros((A_BLOCK, 2 * A_BLOCK), F32) for _ in range(A_HEADS)]
            for b in range(REL_BUCKETS):
                hit = t == b
                acc = [jnp.where(hit, rb_ref[b, h], acc[h]) for h in range(A_HEADS)]
            for h in range(A_HEADS):
                o_refs[c][h] = acc[h]

    return pl.pallas_call(body, name="attn_bias", out_shape=[_sds(A_HEADS, A_BLOCK, 2 * A_BLOCK)] * 3,
                          in_specs=[pl.BlockSpec(memory_space=pltpu.SMEM), pl.BlockSpec(memory_space=pltpu.VMEM)],
                          compiler_params=pltpu.CompilerParams(vmem_limit_bytes=VMEM_LIMIT_BYTES))(rel_bias, table)


def _rel_bias_grad(dbs, idx_rows):
    n = A_BLOCK * 2 * A_BLOCK

    def body(d0_ref, d1_ref, d2_ref, idx_ref, o_ref):
        bucket = lax.broadcasted_iota(jnp.int32, (REL_BUCKETS, n), 0).astype(F32)
        acc = jnp.zeros((A_HEADS, REL_BUCKETS), F32)
        for c, db_ref in enumerate((d0_ref, d1_ref, d2_ref)):
            onehot = (idx_ref[c:c + 1, :] == bucket).astype(F32)
            acc = acc + lax.dot_general(db_ref[...], onehot, (((1,), (1,)), ((), ())), precision=HI, preferred_element_type=F32)
        o_ref[...] = acc

    return pl.pallas_call(body, name="rel_bias_grad", out_shape=_sds(A_HEADS, REL_BUCKETS),
                          compiler_params=pltpu.CompilerParams(vmem_limit_bytes=VMEM_LIMIT_BYTES))(
                              *[d.reshape(A_HEADS, n) for d in dbs], idx_rows)


def _s5_param_fn(a_re, a_im, log_dt, bt_re, bt_im):
    dt = jnp.exp(log_dt)
    mag = jnp.exp(dt * a_re)
    abar_r, abar_i = mag * jnp.cos(dt * a_im), mag * jnp.sin(dt * a_im)
    den = a_re * a_re + a_im * a_im
    fr = ((abar_r - 1.0) * a_re + abar_i * a_im) / den
    fi = (abar_i * a_re - (abar_r - 1.0) * a_im) / den
    row = lax.broadcasted_iota(jnp.int32, (B_WIDTH, B_GROUPS), 0)
    grp = lax.broadcasted_iota(jnp.int32, (B_WIDTH, B_GROUPS), 1)
    expand = ((row // B_GROUP) == grp).astype(F32)
    fr_e, fi_e = _hdot(expand, fr), _hdot(expand, fi)
    return abar_r, abar_i, fr_e * bt_re - fi_e * bt_im, fr_e * bt_im + fi_e * bt_re


def _s5_params(a_re, a_im, log_dt, bt_re, bt_im):
    def body(ar, ai, ld, br, bi, o1, o2, o3, o4):
        o1[...], o2[...], o3[...], o4[...] = _s5_param_fn(ar[...], ai[...], ld[...], br[...], bi[...])

    return pl.pallas_call(body, name="s5_params",
                          out_shape=[_sds(B_GROUPS, B_STATE)] * 2 + [_sds(B_WIDTH, B_STATE)] * 2)(a_re, a_im, log_dt, bt_re, bt_im)


def _s5_params_bwd(a_re, a_im, log_dt, bt_re, bt_im, d1, d2, d3, d4):
    def body(ar, ai, ld, br, bi, c1, c2, c3, c4, o1, o2, o3, o4, o5):
        _, vjp = jax.vjp(_s5_param_fn, ar[...], ai[...], ld[...], br[...], bi[...])
        o1[...], o2[...], o3[...], o4[...], o5[...] = vjp((c1[...], c2[...], c3[...], c4[...]))

    return pl.pallas_call(body, name="s5_params_bwd",
                          out_shape=[_sds(B_GROUPS, B_STATE)] * 2 + [_sds(B_GROUPS, 1)] + [_sds(B_WIDTH, B_STATE)] * 2,
                          )(a_re, a_im, log_dt, bt_re, bt_im, d1, d2, d3, d4)


S5_SUB = 8
S5_GROUPS = S5_T // S5_SUB


def _dscan(xr, xi, cr, ci, period, reverse):
    n = xr.shape[0]
    rows = lax.broadcasted_iota(jnp.int32, xr.shape, 0)
    pos = rows % period
    k = 1
    while k < period:
        if reverse:
            keep, shift = pos < period - k, n - k
        else:
            keep, shift = pos >= k, k
        sr = jnp.where(keep, pltpu.roll(xr, shift, 0), 0.0)
        si = jnp.where(keep, pltpu.roll(xi, shift, 0), 0.0)
        xr, xi = xr + cr * sr - ci * si, xi + cr * si + ci * sr
        cr, ci = cr * cr - ci * ci, 2.0 * cr * ci
        k *= 2
    return xr, xi, cr, ci


def _pick_row(x, r):
    rows = lax.broadcasted_iota(jnp.int32, x.shape, 0)
    return jnp.sum(jnp.where(rows == r, x, 0.0), axis=0, keepdims=True)


def _scan_tables(ar, ai, reverse):
    rows = lax.broadcasted_iota(jnp.int32, (S5_SUB, S5_LANES), 0)
    at = rows == (S5_SUB - 1 if reverse else 0)
    p8r, p8i, a8r, a8i = _dscan(jnp.where(at, ar, 0.0), jnp.where(at, ai, 0.0), ar, ai, S5_SUB, reverse)
    grp = lax.broadcasted_iota(jnp.int32, (S5_GROUPS, S5_LANES), 0)
    at = grp == (S5_GROUPS - 1 if reverse else 0)
    pgr, pgi, _, _ = _dscan(jnp.where(at, a8r, 0.0), jnp.where(at, a8i, 0.0), a8r, a8i, S5_GROUPS, reverse)
    return p8r, p8i, pgr, pgi


def _block_scan(br, bi, ar, ai, carry, tables, reverse, work):
    p8r, p8i, pgr, pgi = tables
    cin_r, cin_i = carry
    xw_r, xw_i, ew_r, ew_i = work
    xr, xi, a8r, a8i = _dscan(br, bi, ar, ai, S5_SUB, reverse)
    xw_r[...] = xr
    xw_i[...] = xi
    end = 0 if reverse else S5_SUB - 1
    for g in range(S5_GROUPS):
        ew_r[g:g + 1, :] = xw_r[S5_SUB * g + end:S5_SUB * g + end + 1, :]
        ew_i[g:g + 1, :] = xw_i[S5_SUB * g + end:S5_SUB * g + end + 1, :]
    er, ei, _, _ = _dscan(ew_r[...], ew_i[...], a8r, a8i, S5_GROUPS, reverse)
    er, ei = er + pgr * cin_r - pgi * cin_i, ei + pgr * cin_i + pgi * cin_r
    grp = lax.broadcasted_iota(jnp.int32, (S5_GROUPS, S5_LANES), 0)
    if reverse:
        ew_r[...] = jnp.where(grp == S5_GROUPS - 1, cin_r, pltpu.roll(er, S5_GROUPS - 1, 0))
        ew_i[...] = jnp.where(grp == S5_GROUPS - 1, cin_i, pltpu.roll(ei, S5_GROUPS - 1, 0))
    else:
        ew_r[...] = jnp.where(grp == 0, cin_r, pltpu.roll(er, 1, 0))
        ew_i[...] = jnp.where(grp == 0, cin_i, pltpu.roll(ei, 1, 0))
    for g in range(S5_GROUPS):
        rows = slice(S5_SUB * g, S5_SUB * (g + 1))
        nr, ni = ew_r[g:g + 1, :], ew_i[g:g + 1, :]
        xw_r[rows, :] += p8r * nr - p8i * ni
        xw_i[rows, :] += p8r * ni + p8i * nr
    last = 0 if reverse else S5_GROUPS - 1
    return xw_r[...], xw_i[...], (_pick_row(er, last), _pick_row(ei, last))


def _s5_tile_specs(n_t, rev):
    t_of = (lambda i: n_t - 1 - i) if rev else (lambda i: i)
    u_spec = pl.BlockSpec((S5_T, 128), lambda j, i: (t_of(i), j))
    x_spec = pl.BlockSpec((S5_T, S5_LANES), lambda j, i: (t_of(i), j))
    b_spec = pl.BlockSpec((1, 128, S5_LANES), lambda j, i: (j, 0, 0))
    c_spec = pl.BlockSpec((1, S5_LANES, 128), lambda j, i: (j, 0, 0))
    a_spec = pl.BlockSpec((1, S5_LANES), lambda j, i: (0, j))
    return u_spec, x_spec, b_spec, c_spec, a_spec


def _s5_scratch(with_rows):
    return ([pltpu.VMEM((1, S5_LANES), F32)] * 2 + [pltpu.VMEM((S5_SUB, S5_LANES), F32)] * 2
            + [pltpu.VMEM((S5_GROUPS, S5_LANES), F32)] * 4 + ([pltpu.VMEM((S5_T, S5_LANES), F32)] * 2 if with_rows else []))


def _s5_scan_fwd(u, btr, bti, ctr, cti, abr, abi):
    s_len = u.shape[0]
    n_t = s_len // S5_T

    def body(u_ref, btr_ref, bti_ref, ctr_ref, cti_ref, ar_ref, ai_ref, xr_ref, xi_ref, y_ref, car, cai, p8r, p8i, pgr, pgi, ew_r, ew_i):
        ar, ai = ar_ref[...], ai_ref[...]

        @pl.when(pl.program_id(1) == 0)
        def _():
            car[...] = jnp.zeros_like(car)
            cai[...] = jnp.zeros_like(cai)
            p8r[...], p8i[...], pgr[...], pgi[...] = _scan_tables(ar, ai, False)

        ub = u_ref[...]
        xr, xi, (ncr, nci) = _block_scan(_bdot(ub, btr_ref[0]), _bdot(ub, bti_ref[0]), ar, ai, (car[...], cai[...]),
                                         (p8r[...], p8i[...], pgr[...], pgi[...]), False, (xr_ref, xi_ref, ew_r, ew_i))
        car[...] = ncr
        cai[...] = nci
        y_ref[...] = _bdot(xr, ctr_ref[0]) - _bdot(xi, cti_ref[0])

    u_spec, x_spec, b_spec, c_spec, a_spec = _s5_tile_specs(n_t, False)
    return pl.pallas_call(
        body, name="s5_scan_fwd", grid=(S5_TILES, n_t),
        in_specs=[u_spec, b_spec, b_spec, c_spec, c_spec, a_spec, a_spec],
        out_specs=[x_spec, x_spec, u_spec],
        out_shape=[_sds(s_len, S5_TILES * S5_LANES)] * 2 + [_sds(s_len, B_WIDTH)],
        scratch_shapes=_s5_scratch(False),
        compiler_params=_cp("arbitrary", "arbitrary"),
    )(u, btr, bti, ctr, cti, abr, abi)


def _s5_scan_bwd(dy, xr, xi, u, btr, bti, ctr, cti, abr, abi):
    s_len = u.shape[0]
    n_t = s_len // S5_T

    def body(dy_ref, xr_ref, xi_ref, xrp_ref, xip_ref, u_ref, btr_ref, bti_ref, ctr_ref, cti_ref, ar_ref, ai_ref,
             du_ref, dbtr_ref, dbti_ref, dctr_ref, dcti_ref, dar_ref, dai_ref, car, cai, p8r, p8i, pgr, pgi, ew_r, ew_i, xw_r, xw_i):
        ar, ai = ar_ref[...], ai_ref[...]
        i = pl.program_id(1)
        rows = lax.broadcasted_iota(jnp.int32, (S5_T, S5_LANES), 0)

        @pl.when(i == 0)
        def _():
            for r in (car, cai, dbtr_ref, dbti_ref, dctr_ref, dcti_ref, dar_ref, dai_ref):
                r[...] = jnp.zeros_like(r)
            p8r[...], p8i[...], pgr[...], pgi[...] = _scan_tables(ar, -ai, True)

        dyb = dy_ref[...]
        xr_b, xi_b, ub = xr_ref[...], xi_ref[...], u_ref[...]
        dctr_ref[0] += _bdot_tn(xr_b, dyb)
        dcti_ref[0] -= _bdot_tn(xi_b, dyb)
        gr, gi, (ncr, nci) = _block_scan(_bdot_nt(dyb, ctr_ref[0]), -_bdot_nt(dyb, cti_ref[0]), ar, -ai, (car[...], cai[...]),
                                         (p8r[...], p8i[...], pgr[...], pgi[...]), True, (xw_r, xw_i, ew_r, ew_i))
        car[...] = ncr
        cai[...] = nci
        du_ref[...] = _bdot_nt(gr, btr_ref[0]) + _bdot_nt(gi, bti_ref[0])
        dbtr_ref[0] += _bdot_tn(ub, gr)
        dbti_ref[0] += _bdot_tn(ub, gi)
        has_prev = (i < n_t - 1).astype(F32)
        hr = _pick_row(xrp_ref[...], 7) * has_prev
        hi = _pick_row(xip_ref[...], 7) * has_prev
        xpr = jnp.where(rows == 0, hr, pltpu.roll(xr_b, 1, 0))
        xpi = jnp.where(rows == 0, hi, pltpu.roll(xi_b, 1, 0))
        dar_ref[...] += jnp.sum(gr * xpr + gi * xpi, axis=0, keepdims=True)
        dai_ref[...] += jnp.sum(gi * xpr - gr * xpi, axis=0, keepdims=True)

    u_spec, x_spec, b_spec, c_spec, a_spec = _s5_tile_specs(n_t, True)
    halo = pl.BlockSpec((8, S5_LANES), lambda j, i: (jnp.maximum((n_t - 1 - i) * (S5_T // 8) - 1, 0), j))
    return pl.pallas_call(
        body, name="s5_scan_bwd", grid=(S5_TILES, n_t),
        in_specs=[u_spec, x_spec, x_spec, halo, halo, u_spec, b_spec, b_spec, c_spec, c_spec, a_spec, a_spec],
        out_specs=[u_spec, b_spec, b_spec, c_spec, c_spec, a_spec, a_spec],
        out_shape=[_sds(s_len, B_WIDTH)] + [_sds(S5_TILES, 128, S5_LANES)] * 2 + [_sds(S5_TILES, S5_LANES, 128)] * 2
        + [_sds(1, S5_TILES * S5_LANES)] * 2,
        scratch_shapes=_s5_scratch(True),
        compiler_params=_cp("arbitrary", "arbitrary"),
    )(dy, xr, xi, xr, xi, u, btr, bti, ctr, cti, abr, abi)


def _blockdiag_b(bbar_t):
    blocks = bbar_t.reshape(S5_TILES, 8, B_GROUP, B_STATE)
    return jnp.einsum('jgmp,gh->jgmhp', blocks, jnp.eye(8, dtype=F32)).reshape(S5_TILES, 128, S5_LANES)


def _blockdiag_b_t(d):
    return jnp.einsum('jgmgp->jgmp', d.reshape(S5_TILES, 8, B_GROUP, 8, B_STATE)).reshape(B_WIDTH, B_STATE)


def _blockdiag_c(c):
    blocks = c.reshape(S5_TILES, 8, B_GROUP, B_STATE)
    return jnp.einsum('jgmp,gh->jhpgm', blocks, jnp.eye(8, dtype=F32)).reshape(S5_TILES, S5_LANES, 128)


def _blockdiag_c_t(d):
    return jnp.einsum('jgpgm->jgmp', d.reshape(S5_TILES, 8, B_STATE, 8, B_GROUP)).reshape(B_GROUPS, B_GROUP, B_STATE)


def _l0_out(os, ls, ga, gb, ypre, u, x, d_skip, glu_w, glu_b, w_out, post_g, gate):
    s_len = x.shape[0]

    def body(o0, o1, o2, l0, l1, l2, ga_ref, gb_ref, yp_ref, u_ref, x_ref, d_ref, gw_ref, gbias_ref, w_ref, pg_ref, gt_ref, x1_ref, y_ref):
        oa = _merge_gate(*[_from_res(r[...]) for r in (o0, o1, o2, l0, l1, l2)], ga_ref[...])
        yb = _s5_gelu(yp_ref[...], u_ref[...], d_ref[...])
        ob = _s5_glu(yb, _bdot(yb, gw_ref[...]) + gbias_ref[...], gb_ref[...])
        y = _bdot(oa, w_ref[0:512, :]) + _bdot(ob, w_ref[512:1024, :])
        y_ref[...] = y
        x1_ref[...] = _post_res(y, x_ref[...], pg_ref[...], gt_ref[...])

    vec, half = _fix((1, D_MODEL)), _fix((1, 512))
    return pl.pallas_call(
        body, name="l0_out", grid=(s_len // TM,),
        in_specs=[_res_spec(d) for d in DILATIONS] * 2 + [_row(TM, 512)] * 4
        + [_row(TM, D_MODEL), half, _fix((512, 512)), half, _fix((D_MODEL, D_MODEL)), vec, vec],
        out_specs=[_row(TM, D_MODEL)] * 2,
        out_shape=[_sds(s_len, D_MODEL)] * 2,
        compiler_params=_cp("arbitrary"),
    )(*os, *ls, ga, gb, ypre, u, x, d_skip, glu_w, glu_b, w_out, post_g, gate)


def _l0_out_bwd(os, ls, ga, gb, ypre, u, x, y, d_skip, glu_w, glu_b, w_out, post_g, gate, dx1):
    s_len = x.shape[0]

    def body(o0, o1, o2, l0, l1, l2, ga_ref, gb_ref, yp_ref, u_ref, x_ref, y_ref, d_ref, gw_ref, gbias_ref, w_ref, pg_ref, gt_ref, dx1_ref,
             do0, do1, do2, dl0, dl1, dl2, dga_ref, dgb_ref, dyp_ref, du_ref, dd_ref, dgw_ref, dgbias_ref, dw_ref, dpg_ref, dgt_ref):
        _zero_at_first([dd_ref, dgw_ref, dgbias_ref, dw_ref, dpg_ref, dgt_ref])
        _, vjp2 = jax.vjp(_post_res, y_ref[...], x_ref[...], pg_ref[...], gt_ref[...])
        dy, _, dpg, dgt = vjp2(dx1_ref[...])
        _acc(dpg_ref, dpg)
        _acc(dgt_ref, dgt)
        oa, vjp_a = jax.vjp(_merge_gate, *[_from_res(r[...]) for r in (o0, o1, o2, l0, l1, l2)], ga_ref[...])
        yb, vjp_g = jax.vjp(_s5_gelu, yp_ref[...], u_ref[...], d_ref[...])
        gl = _bdot(yb, gw_ref[...]) + gbias_ref[...]
        ob, vjp_b = jax.vjp(_s5_glu, yb, gl, gb_ref[...])
        dw_ref[0:512, :] += _bdot_tn(oa, dy)
        dw_ref[512:1024, :] += _bdot_tn(ob, dy)
        d1, d2, d3, e1, e2, e3, dga = vjp_a(_bdot_nt(dy, w_ref[0:512, :]))
        for ref, val, d in zip((do0, do1, do2, dl0, dl1, dl2), (d1, d2, d3, e1, e2, e3), DILATIONS * 2):
            ref[...] = _to_res(val, d)
        dga_ref[...] = dga
        dyb, dgl, dgb = vjp_b(_bdot_nt(dy, w_ref[512:1024, :]))
        dgb_ref[...] = dgb
        dgw_ref[...] += _bdot_tn(yb, dgl)
        _acc(dgbias_ref, jnp.sum(dgl, axis=0, keepdims=True))
        dyp, du, dd = vjp_g(dyb + _bdot_nt(dgl, gw_ref[...]))
        dyp_ref[...] = dyp
        du_ref[...] = du
        _acc(dd_ref, dd)

    vec, half = _fix((1, D_MODEL)), _fix((1, 512))
    r5, r10 = _row(TM, 512), _row(TM, D_MODEL)
    res6 = [_res_spec(d) for d in DILATIONS] * 2
    return pl.pallas_call(
        body, name="l0_out_bwd", grid=(s_len // TM,),
        in_specs=res6 + [r5] * 4 + [r10, r10, half, _fix((512, 512)), half, _fix((D_MODEL, D_MODEL)), vec, vec, r10],
        out_specs=res6 + [r5] * 4 + [half, _fix((512, 512)), half, _fix((D_MODEL, D_MODEL)), vec, vec],
        out_shape=[_sds(*_res_shape(s_len, d)) for d in DILATIONS] * 2 + [_sds(s_len, 512)] * 4
        + [_sds(1, 512), _sds(512, 512), _sds(1, 512), _sds(D_MODEL, D_MODEL), _sds(1, D_MODEL), _sds(1, D_MODEL)],
        compiler_params=_cp("arbitrary"),
    )(*os, *ls, ga, gb, ypre, u, x, y, d_skip, glu_w, glu_b, w_out, post_g, gate, dx1)


def _l1_front(x, pre_g, scale, shift, w_in):
    s_len = x.shape[0]

    def body(x_ref, g_ref, sc_ref, sh_ref, w_ref, raw_ref, gate_ref, ba_ref, h_ref):
        hb = _pre_mod(x_ref[...], g_ref[...], sc_ref[...], sh_ref[...]).astype(BF)
        h_ref[...] = hb
        z = jnp.dot(hb, w_ref[...], preferred_element_type=F32)
        raw_ref[...] = z[:, 0:QKV]
        gate_ref[...] = z[:, QKV:QKV + 1024]
        ba_ref[...] = z[:, QKV + 1024:C_IN_PAD]

    vec = _fix((1, D_MODEL))
    return pl.pallas_call(
        body, name="l1_front", grid=(s_len // TM,),
        in_specs=[_row(TM, D_MODEL), vec, vec, vec, _fix((D_MODEL, C_IN_PAD))],
        out_specs=[_row(TM, QKV), _row(TM, 1024), _row(TM, 128), _row(TM, D_MODEL)],
        out_shape=[_sds(s_len, QKV), _sds(s_len, 1024), _sds(s_len, 128), _sds(s_len, D_MODEL, dtype=BF)],
        compiler_params=_cp("arbitrary"),
    )(x, pre_g, scale, shift, w_in)


def _bg_fn(ba, alog_row, dtb_row):
    lane = lax.broadcasted_iota(jnp.int32, (1, 128), 1)
    g = -jnp.exp(alog_row) * jax.nn.softplus(ba + dtb_row)
    return jnp.where(lane < C_HEADS, jax.nn.sigmoid(ba), jnp.where(lane < 2 * C_HEADS, g, 0.0))


def _act_q(c):
    q = jax.nn.silu(c)
    return q * lax.rsqrt(jnp.sum(q * q, axis=-1, keepdims=True) + EPS) * (C_DK ** -0.5)


def _act_k(c):
    k = jax.nn.silu(c)
    return k * lax.rsqrt(jnp.sum(k * k, axis=-1, keepdims=True) + EPS)


def _act_of(s):
    return _act_q if s < 8 else (_act_k if s < 16 else jax.nn.silu)


def _gdn_prep(raw, ba, conv_w, alog_row, dtb_row):
    s_len = raw.shape[0]

    def body(raw_ref, halo_ref, ba_ref, w_ref, al_ref, dt_ref, qkv_ref, bg_ref):
        bg_ref[...] = _bg_fn(ba_ref[...], al_ref[...], dt_ref[...])
        has_prev = (pl.program_id(0) > 0).astype(F32)
        for s in range(24):
            sl = slice(s * 128, (s + 1) * 128)
            cat = jnp.concatenate([halo_ref[:, sl] * has_prev, raw_ref[:, sl]], axis=0)
            conv = w_ref[3:4, sl] * cat[8:]
            for j in range(3):
                conv = conv + w_ref[j:j + 1, sl] * pltpu.roll(cat, 3 - j, 0)[8:]
            qkv_ref[:, sl] = _act_of(s)(conv)

    halo = pl.BlockSpec((8, QKV), lambda i: (jnp.maximum(i * (TM // 8) - 1, 0), 0))
    row128 = _fix((1, 128))
    return pl.pallas_call(
        body, name="gdn_prep", grid=(s_len // TM,),
        in_specs=[_row(TM, QKV), halo, _row(TM, 128), _fix((C_CONV, QKV)), row128, row128],
        out_specs=[_row(TM, QKV), _row(TM, 128)],
        out_shape=[_sds(s_len, QKV), _sds(s_len, 128)],
        compiler_params=_cp("arbitrary"),
    )(raw, raw, ba, conv_w, alog_row, dtb_row)


def _gdn_prep_bwd(raw, ba, conv_w, alog_row, dtb_row, dq, dk, dv, dbg):
    s_len = raw.shape[0]
    n_tiles = s_len // TM
    ext = TM + 8

    def body(raw_ref, prev_ref, next_ref, ba_ref, w_ref, al_ref, dt_ref, dq_ref, dqn_ref, dk_ref, dkn_ref, dv_ref, dvn_ref, dbg_ref,
             draw_ref, dba_ref, dw_ref, dal_ref, ddt_ref):
        _zero_at_first([dw_ref, dal_ref, ddt_ref])
        i = pl.program_id(0)
        _, vjp_bg = jax.vjp(_bg_fn, ba_ref[...], al_ref[...], dt_ref[...])
        dba, dal, ddt = vjp_bg(dbg_ref[...])
        dba_ref[...] = dba
        _acc(dal_ref, dal)
        _acc(ddt_ref, ddt)
        has_prev = (i > 0).astype(F32)
        has_next = (i < n_tiles - 1).astype(F32)
        ct_refs = ((dq_ref, dqn_ref), (dk_ref, dkn_ref), (dv_ref, dvn_ref))
        for s in range(24):
            sl = slice(s * 128, (s + 1) * 128)
            hl = slice((s % 8) * 128, (s % 8 + 1) * 128)
            tile_ref, nxt_ref = ct_refs[s // 8]
            cat = jnp.concatenate([prev_ref[:, sl] * has_prev, raw_ref[:, sl], next_ref[:, sl] * has_next], axis=0)
            shifted = [pltpu.roll(cat, 3 - j, 0)[8:] for j in range(3)] + [cat[8:]]
            conv = w_ref[3:4, sl] * shifted[3]
            for j in range(3):
                conv = conv + w_ref[j:j + 1, sl] * shifted[j]
            ct = jnp.concatenate([tile_ref[:, hl], nxt_ref[:, hl] * has_next], axis=0)
            _, vjp_act = jax.vjp(_act_of(s), conv)
            dconv, = vjp_act(ct)
            draw = w_ref[3:4, sl] * dconv[:TM]
            for j in range(3):
                draw = draw + w_ref[j:j + 1, sl] * pltpu.roll(dconv, ext - (3 - j), 0)[:TM]
            draw_ref[:, sl] = draw
            for j in range(4):
                dw_ref[j:j + 1, sl] += jnp.sum(dconv[:TM] * shifted[j][:TM], axis=0, keepdims=True)

    prev = pl.BlockSpec((8, QKV), lambda i: (jnp.maximum(i * (TM // 8) - 1, 0), 0))
    nxt = lambda n: pl.BlockSpec((8, n), lambda i: (jnp.minimum((i + 1) * (TM // 8), s_len // 8 - 1), 0))
    row128 = _fix((1, 128))
    ct_specs = [_row(TM, 1024), nxt(1024)] * 3
    return pl.pallas_call(
        body, name="gdn_prep_bwd", grid=(n_tiles,),
        in_specs=[_row(TM, QKV), prev, nxt(QKV), _row(TM, 128), _fix((C_CONV, QKV)), row128, row128] + ct_specs + [_row(TM, 128)],
        out_specs=[_row(TM, QKV), _row(TM, 128), _fix((C_CONV, QKV)), row128, row128],
        out_shape=[_sds(s_len, QKV), _sds(s_len, 128), _sds(C_CONV, QKV), _sds(1, 128), _sds(1, 128)],
        compiler_params=_cp("arbitrary"),
    )(raw, raw, raw, ba, conv_w, alog_row, dtb_row, dq, dq, dk, dk, dv, dv, dbg)


def _tein(eq, a, b):
    return jnp.einsum(eq, a, b, precision=lax.Precision.HIGH, preferred_element_type=F32)


def _unit_lower_inverse(lower):
    ri = lax.broadcasted_iota(jnp.int32, (C_CHUNK, C_CHUNK), 0)
    ci = lax.broadcasted_iota(jnp.int32, (C_CHUNK, C_CHUNK), 1)
    p_mat = -lower
    inv = (ri == ci).astype(F32)[None] + p_mat
    for _ in range(5):
        p_mat = _tein('hij,hjk->hik', p_mat, p_mat)
        inv = inv + _tein('hij,hjk->hik', inv, p_mat)
    return inv


@jax.custom_vjp
def _known_inverse(lower, inv):
    return inv


def _known_inverse_fwd(lower, inv):
    return inv, inv


def _known_inverse_bwd(inv, d_inv):
    d_lower = -_tein('hik,hjk->hij', _tein('hji,hjk->hik', inv, d_inv), inv)
    return d_lower, jnp.zeros_like(inv)


_known_inverse.defvjp(_known_inverse_fwd, _known_inverse_bwd)


def _gdn_chunk(q, k, v, bg, state, inv_known=None):
    lane = lax.broadcasted_iota(jnp.int32, (1, 128), 1)
    ri = lax.broadcasted_iota(jnp.int32, (C_CHUNK, C_CHUNK), 0)
    ci = lax.broadcasted_iota(jnp.int32, (C_CHUNK, C_CHUNK), 1)
    gc_t = _hdot((ri >= ci).astype(F32), bg)
    beta = jnp.stack([jnp.sum(jnp.where(lane == h, bg, 0.0), axis=-1, keepdims=True) for h in range(C_HEADS)], axis=0)
    gc = jnp.stack([jnp.sum(jnp.where(lane == C_HEADS + h, gc_t, 0.0), axis=-1, keepdims=True) for h in range(C_HEADS)], axis=0)
    gc_rows = gc_t.T
    row_id = lax.broadcasted_iota(jnp.int32, (128, C_CHUNK), 0)
    gcj = jnp.stack([jnp.sum(jnp.where(row_id == C_HEADS + h, gc_rows, 0.0), axis=0, keepdims=True) for h in range(C_HEADS)], axis=0)
    tril, strict = (ri >= ci)[None], (ri > ci)[None]
    decay = jnp.exp(jnp.where(tril, gc - gcj, -1e30))
    kb = k * beta
    lower = jnp.where(strict, _bein('hid,hjd->hij', kb, k) * decay, 0.0)
    inv = _unit_lower_inverse(lower) if inv_known is None else _known_inverse(lower, inv_known)
    egc = jnp.exp(gc)
    u_c = _tein('hij,hjd->hid', inv, v * beta)
    w_c = _tein('hij,hjd->hid', inv, kb * egc)
    aqk = _bein('hid,hjd->hij', q, k) * decay
    rowi = lax.broadcasted_iota(jnp.int32, (1, C_CHUNK, 1), 1)
    g_last = jnp.sum(jnp.where(rowi == C_CHUNK - 1, gc, 0.0), axis=1, keepdims=True)
    kd = k * jnp.exp(g_last - gc)
    v_new = u_c - _bein('hik,hkv->hiv', w_c, state)
    o = _bein('hik,hkv->hiv', q * egc, state) + _bein('hij,hjv->hiv', aqk, v_new)
    return o, state * jnp.exp(g_last) + _bein('hik,hiv->hkv', kd, v_new), inv


def _heads(ref):
    return jnp.stack([ref[:, h * C_DK:(h + 1) * C_DK] for h in range(C_HEADS)], axis=0)


def _gdn_fwd(qkv, bg):
    s_len = qkv.shape[0]
    n_c = s_len // C_CHUNK

    def body(q_ref, k_ref, v_ref, bg_ref, o_ref, ss_ref, inv_ref, st_ref):
        _zero_at_first([st_ref])
        s0 = st_ref[...]
        ss_ref[0] = s0
        o, s2, inv = _gdn_chunk(_heads(q_ref), _heads(k_ref), _heads(v_ref), bg_ref[...], s0)
        st_ref[...] = s2
        inv_ref[0] = inv
        for h in range(C_HEADS):
            o_ref[:, h * C_DK:(h + 1) * C_DK] = o[h]

    col = lambda c: pl.BlockSpec((C_CHUNK, 1024), lambda i: (i, c))
    return pl.pallas_call(
        body, name="gdn_fwd", grid=(n_c,),
        in_specs=[col(0), col(1), col(2), _row(C_CHUNK, 128)],
        out_specs=[_row(C_CHUNK, 1024), pl.BlockSpec((1, C_HEADS, C_DK, C_DK), lambda i: (i, 0, 0, 0)),
                   pl.BlockSpec((1, C_HEADS, C_CHUNK, C_CHUNK), lambda i: (i, 0, 0, 0))],
        out_shape=[_sds(s_len, 1024), _sds(n_c, C_HEADS, C_DK, C_DK), _sds(n_c, C_HEADS, C_CHUNK, C_CHUNK)],
        scratch_shapes=[pltpu.VMEM((C_HEADS, C_DK, C_DK), F32)],
        compiler_params=_cp("arbitrary"),
    )(qkv, qkv, qkv, bg)


def _gdn_bwd(qkv, bg, states, invs, do):
    s_len = qkv.shape[0]
    n_c = s_len // C_CHUNK

    def body(q_ref, k_ref, v_ref, bg_ref, ss_ref, inv_ref, do_ref, dq_ref, dk_ref, dv_ref, dbg_ref, ds_ref):
        _zero_at_first([ds_ref])
        inv_known = inv_ref[0]
        chunk = lambda q, k, v, bg_, st: _gdn_chunk(q, k, v, bg_, st, inv_known)[:2]
        _, vjp = jax.vjp(chunk, _heads(q_ref), _heads(k_ref), _heads(v_ref), bg_ref[...], ss_ref[0])
        dq, dk, dv, dbg, ds = vjp((_heads(do_ref), ds_ref[...]))
        ds_ref[...] = ds
        dbg_ref[...] = dbg
        for h in range(C_HEADS):
            sl = slice(h * C_DK, (h + 1) * C_DK)
            dq_ref[:, sl], dk_ref[:, sl], dv_ref[:, sl] = dq[h], dk[h], dv[h]

    rev = lambda i: n_c - 1 - i
    col = lambda c: pl.BlockSpec((C_CHUNK, 1024), lambda i: (rev(i), c))
    row128 = pl.BlockSpec((C_CHUNK, 128), lambda i: (rev(i), 0))
    per_chunk = lambda n: pl.BlockSpec((1, C_HEADS, n, n), lambda i: (rev(i), 0, 0, 0))
    return pl.pallas_call(
        body, name="gdn_bwd", grid=(n_c,),
        in_specs=[col(0), col(1), col(2), row128, per_chunk(C_DK), per_chunk(C_CHUNK), col(0)],
        out_specs=[col(0), col(0), col(0), row128],
        out_shape=[_sds(s_len, 1024)] * 3 + [_sds(s_len, 128)],
        scratch_shapes=[pltpu.VMEM((C_HEADS, C_DK, C_DK), F32)],
        compiler_params=_cp("arbitrary"),
    )(qkv, qkv, qkv, bg, states, invs, do)


def _head_norm_gate(o, gate, norm_g):
    return (_rms(o) * norm_g) * jax.nn.silu(gate)


def _l1_out_fb(o, gate_c, x1, target, norm_g, w_out, post_g, gate):
    s_len = x1.shape[0]

    def body(o_ref, gc_ref, x1_ref, t_ref, ng_ref, w_ref, pg_ref, gt_ref,
             loss_ref, dres_ref, do_ref, dgc_ref, dw_ref, dng_ref, dpg_ref, dgt_ref):
        _zero_at_first([loss_ref, dw_ref, dng_ref, dpg_ref, dgt_ref])
        ng = ng_ref[...]
        ons, vjps = [], []
        for h in range(C_HEADS):
            sl = slice(h * C_DK, (h + 1) * C_DK)
            on, vjp_h = jax.vjp(_head_norm_gate, o_ref[:, sl], gc_ref[:, sl], ng)
            ons.append(on)
            vjps.append(vjp_h)
        on_all = jnp.concatenate(ons, axis=-1)
        y = _bdot(on_all, w_ref[...])
        x2, vjp2 = jax.vjp(_post_res, y, x1_ref[...], pg_ref[...], gt_ref[...])
        err = x2 - t_ref[...]
        _acc(loss_ref, jnp.full((1, 128), 0.5 * jnp.sum(jnp.mean(err * err, axis=-1)), F32))
        dx2 = err * (1.0 / D_MODEL)
        dy, _, dpg, dgt = vjp2(dx2)
        dres_ref[...] = dx2
        _acc(dpg_ref, dpg)
        _acc(dgt_ref, dgt)
        dw_ref[...] += _bdot_tn(on_all, dy)
        don = _bdot_nt(dy, w_ref[...])
        for h in range(C_HEADS):
            sl = slice(h * C_DK, (h + 1) * C_DK)
            do_h, dgc_h, dng = vjps[h](don[:, sl])
            do_ref[:, sl] = do_h
            dgc_ref[:, sl] = dgc_h
            _acc(dng_ref, dng)

    vec, r10 = _fix((1, D_MODEL)), _row(TM, D_MODEL)
    row128 = _fix((1, 128))
    return pl.pallas_call(
        body, name="l1_out_fb", grid=(s_len // TM,),
        in_specs=[r10, r10, r10, r10, row128, _fix((D_MODEL, D_MODEL)), vec, vec],
        out_specs=[row128, r10, r10, r10, _fix((D_MODEL, D_MODEL)), row128, vec, vec],
        out_shape=[_sds(1, 128), _sds(s_len, D_MODEL), _sds(s_len, D_MODEL), _sds(s_len, D_MODEL),
                   _sds(D_MODEL, D_MODEL), _sds(1, 128), _sds(1, D_MODEL), _sds(1, D_MODEL)],
        compiler_params=_cp("arbitrary"),
    )(o, gate_c, x1, target, norm_g, w_out, post_g, gate)


def _row_of(v, width, at):
    return jnp.zeros((1, width), F32).at[0, at:at + v.shape[-1]].set(v.reshape(-1))


def _local_step(x, target, mod, wd):
    s_len = x.shape[0]
    shift0, scale0, gate0 = (mod[0:1, i * 1024:(i + 1) * 1024] for i in range(3))
    shift1, scale1, gate1 = (mod[1:2, i * 1024:(i + 1) * 1024] for i in range(3))
    pre_g0, pre_g1 = wd["pre_g"][0:1], wd["pre_g"][1:2]
    post_g0, post_g1 = wd["post_g"][0:1], wd["post_g"][1:2]
    w_in0 = wd["ab_w_in"].astype(BF)
    w_out0 = wd["ab_w_out"].astype(BF)
    glu_w = wd["s5_glu_w"].astype(BF)
    w_in1 = jnp.concatenate([wd["gdn_w_in"], jnp.zeros((D_MODEL, C_IN_PAD - wd["gdn_w_in"].shape[1]), F32)], axis=1).astype(BF)
    w_out1 = wd["gdn_w_out"].astype(BF)
    d_skip, glu_b = wd["s5_d"].reshape(1, 512), wd["s5_glu_b"].reshape(1, 512)
    norm_g = wd["gdn_norm_g"].reshape(1, 128)
    alog_row = _row_of(wd["gdn_a_log"], 128, C_HEADS)
    dtb_row = _row_of(wd["gdn_dt_bias"], 128, C_HEADS)
    conv_w = wd["gdn_conv"]

    a_re, a_im = wd["s5_a_re"], wd["s5_a_im"]
    log_dt = wd["s5_log_dt"].reshape(B_GROUPS, 1)
    bt_re = wd["s5_b_re"].transpose(0, 2, 1).reshape(B_WIDTH, B_STATE)
    bt_im = wd["s5_b_im"].transpose(0, 2, 1).reshape(B_WIDTH, B_STATE)
    abar_r, abar_i, bbar_r, bbar_i = _s5_params(a_re, a_im, log_dt, bt_re, bt_im)
    abr, abi = abar_r.reshape(1, -1), abar_i.reshape(1, -1)
    btr, bti = _blockdiag_b(bbar_r).astype(BF), _blockdiag_b(bbar_i).astype(BF)
    ctr, cti = _blockdiag_c(wd["s5_c_re"]).astype(BF), _blockdiag_c(wd["s5_c_im"]).astype(BF)

    table = _bucket_table()
    biases = _attn_bias(wd["rel_bias"], jnp.asarray(table))
    front = _l0_front(x, pre_g0, scale0, shift0, w_in0)
    qs, ks, vs = front[0:3], front[3:6], front[6:9]
    u, ga, gb, h0 = front[9:]
    os, ls = zip(*[_attn_fwd(qs[i], ks[i], vs[i], biases[i]) for i in range(3)])
    xr, xi, ypre = _s5_scan_fwd(u, btr, bti, ctr, cti, abr, abi)
    x1, y0 = _l0_out(os, ls, ga, gb, ypre, u, x, d_skip, glu_w, glu_b, w_out0, post_g0, gate0)

    raw, gate_c, ba, h1 = _l1_front(x1, pre_g1, scale1, shift1, w_in1)
    qkv, bg = _gdn_prep(raw, ba, conv_w, alog_row, dtb_row)
    o_gdn, states, invs = _gdn_fwd(qkv, bg)
    loss_row, dres1, do_gdn, dgate_c, dw_out1, dnorm_g, dpost_g1, dgate1 = _l1_out_fb(
        o_gdn, gate_c, x1, target, norm_g, w_out1, post_g1, gate1)

    dq1, dk1, dv1, dbg = _gdn_bwd(qkv, bg, states, invs, do_gdn)
    draw, dba, dconv_w, dalog_row, ddtb_row = _gdn_prep_bwd(raw, ba, conv_w, alog_row, dtb_row, dq1, dk1, dv1, dbg)
    dz1, dx1, dpre_g1, dscale1, dshift1 = _front_bwd(
        "l1_front_bwd", x1, pre_g1, scale1, shift1, w_in1, dres1, [[draw], [dgate_c], [dba]], [QKV, 1024, 128])
    dw_in1 = _matmul_tn("l1_dw_in", h1, dz1, 1408)

    l0b = _l0_out_bwd(os, ls, ga, gb, ypre, u, x, y0, d_skip, glu_w, glu_b, w_out0, post_g0, gate0, dx1)
    dos, dls = l0b[0:3], l0b[3:6]
    dga, dgb, dypre, du_skip, dd_skip, dglu_w, dglu_b, dw_out0, dpost_g0, dgate0 = l0b[6:]
    du_scan, dbtr, dbti, dctr, dcti, dabr, dabi = _s5_scan_bwd(dypre, xr, xi, u, btr, bti, ctr, cti, abr, abi)
    dqs, dks, dvs, dbs = [], [], [], []
    for i in range(3):
        dq_d, dk_d, dv_d, db_d = _attn_bwd(qs[i], ks[i], vs[i], biases[i], os[i], ls[i], dos[i], dls[i])
        dqs.append(dq_d)
        dks.append(dk_d)
        dvs.append(dv_d)
        dbs.append(db_d)
    parts = [dqs, dks, dvs, [du_skip, du_scan], [dga], [dgb]]
    dz0, grad_x, dpre_g0, dscale0, dshift0 = _front_bwd(
        "l0_front_bwd", x, pre_g0, scale0, shift0, w_in0, dx1, parts, [512] * 6)
    dw_in0 = _matmul_tn("l0_dw_in", h0, dz0, 768)

    idx_rows = jnp.asarray(table.reshape(3, -1), F32)
    drel = _rel_bias_grad(dbs, idx_rows).T
    da_re, da_im, dlog_dt, dbt_re, dbt_im = _s5_params_bwd(
        a_re, a_im, log_dt, bt_re, bt_im, dabr.reshape(B_GROUPS, B_STATE), dabi.reshape(B_GROUPS, B_STATE),
        _blockdiag_b_t(dbtr), _blockdiag_b_t(dbti))
    unb = lambda d: d.reshape(B_GROUPS, B_GROUP, B_STATE).transpose(0, 2, 1)
    grads = {
        "pre_g": jnp.concatenate([dpre_g0, dpre_g1], 0), "post_g": jnp.concatenate([dpost_g0, dpost_g1], 0),
        "rel_bias": drel, "ab_w_in": dw_in0, "ab_w_out": dw_out0,
        "s5_a_re": da_re, "s5_a_im": da_im, "s5_log_dt": dlog_dt.reshape(B_GROUPS),
        "s5_b_re": unb(dbt_re), "s5_b_im": unb(dbt_im),
        "s5_c_re": _blockdiag_c_t(dctr), "s5_c_im": _blockdiag_c_t(dcti),
        "s5_d": dd_skip.reshape(512), "s5_glu_w": dglu_w, "s5_glu_b": dglu_b.reshape(512),
        "gdn_w_in": dw_in1[:, :wd["gdn_w_in"].shape[1]], "gdn_conv": dconv_w,
        "gdn_a_log": dalog_row[0, C_HEADS:2 * C_HEADS], "gdn_dt_bias": ddtb_row[0, C_HEADS:2 * C_HEADS],
        "gdn_norm_g": dnorm_g.reshape(128), "gdn_w_out": dw_out1,
    }
    dmod = jnp.concatenate([jnp.concatenate([dshift0, dscale0, dgate0], 1), jnp.concatenate([dshift1, dscale1, dgate1], 1)], 0)
    return loss_row[0, 0], grad_x, grads, dmod


def _place():
    return lax.axis_index("x"), lax.axis_index("y"), lax.axis_index("c")


def _flip(v, bit):
    return 1 - v if bit else v


def _hbm_call(name, body, arrs, out_shapes, n_sem):
    any_spec = pl.BlockSpec(memory_space=pl.ANY)
    return pl.pallas_call(
        body, name=name,
        in_specs=[any_spec] * len(arrs), out_specs=[any_spec] * len(out_shapes), out_shape=out_shapes,
        scratch_shapes=[pltpu.SemaphoreType.DMA((n_sem,)), pltpu.SemaphoreType.DMA((n_sem,))],
    )(*arrs)


def _own_slot(gathered, own, slot):
    idx = lax.broadcasted_iota(jnp.int32, (gathered.shape[0],) + (1,) * own.ndim, 0)
    return jnp.where(idx == slot, own[None], gathered)


def _all_gather8(name, arr):
    def body(x_ref, out_ref, send_sems, recv_sems):
        x, y, c = _place()
        me = 4 * x + 2 * y + c
        sends, recvs = [], []
        for m in range(1, 8):
            peer = (_flip(x, m & 4), _flip(y, m & 2), _flip(c, m & 1))
            sends.append(pltpu.make_async_remote_copy(x_ref, out_ref.at[me], send_sems.at[m - 1], recv_sems.at[m - 1],
                                                      device_id=peer, device_id_type=MESH))
            recvs.append(pltpu.make_async_remote_copy(x_ref, out_ref.at[4 * peer[0] + 2 * peer[1] + peer[2]], send_sems.at[m - 1],
                                                      recv_sems.at[m - 1], device_id=peer, device_id_type=MESH))
        for cp in sends:
            cp.start()
        for cp in recvs:
            cp.wait_recv()
        for cp in sends:
            cp.wait_send()

    return _hbm_call(name, body, [arr], [jax.ShapeDtypeStruct((8,) + arr.shape, arr.dtype)], 7)[0]


def _all_to_all8(name, arr):
    def body(x_ref, out_ref, send_sems, recv_sems):
        x, y, c = _place()
        me = 4 * x + 2 * y + c
        sends, recvs = [], []
        for m in range(1, 8):
            peer = (_flip(x, m & 4), _flip(y, m & 2), _flip(c, m & 1))
            peer_id = 4 * peer[0] + 2 * peer[1] + peer[2]
            sends.append(pltpu.make_async_remote_copy(x_ref.at[peer_id], out_ref.at[me], send_sems.at[m - 1], recv_sems.at[m - 1],
                                                      device_id=peer, device_id_type=MESH))
            recvs.append(pltpu.make_async_remote_copy(x_ref.at[peer_id], out_ref.at[peer_id], send_sems.at[m - 1], recv_sems.at[m - 1],
                                                      device_id=peer, device_id_type=MESH))
        for cp in sends:
            cp.start()
        for cp in recvs:
            cp.wait_recv()
        for cp in sends:
            cp.wait_send()

    return _hbm_call(name, body, [arr], [jax.ShapeDtypeStruct(arr.shape, arr.dtype)], 7)[0]


def _chip_exchange(name, arrs, scatter):
    n = len(arrs)

    def body(*refs):
        ins, outs = refs[:n], refs[n:2 * n]
        send_sems, recv_sems = refs[2 * n:]
        x, y, c = _place()
        mine = 2 * x + y
        sends, recvs = [], []
        for a in range(n):
            for m in range(1, 4):
                px, py = _flip(x, m & 2), _flip(y, m & 1)
                k = 3 * a + m - 1
                src = ins[a].at[2 * px + py] if scatter else ins[a]
                sends.append(pltpu.make_async_remote_copy(src, outs[a].at[mine], send_sems.at[k], recv_sems.at[k],
                                                          device_id=(px, py, c), device_id_type=MESH))
                recvs.append(pltpu.make_async_remote_copy(src, outs[a].at[2 * px + py], send_sems.at[k], recv_sems.at[k],
                                                          device_id=(px, py, c), device_id_type=MESH))
        for cp in sends:
            cp.start()
        for cp in recvs:
            cp.wait_recv()
        for cp in sends:
            cp.wait_send()

    shapes = [jax.ShapeDtypeStruct(a.shape if scatter else (4,) + a.shape, a.dtype) for a in arrs]
    return _hbm_call(name, body, arrs, shapes, 3 * n)


def _sibling_exchange(name, arrs):
    n = len(arrs)

    def body(*refs):
        ins, outs = refs[:n], refs[n:2 * n]
        send_sems, recv_sems = refs[2 * n:]
        x, y, c = _place()
        copies = [pltpu.make_async_remote_copy(ins[a], outs[a], send_sems.at[a], recv_sems.at[a],
                                               device_id=(x, y, 1 - c), device_id_type=MESH) for a in range(n)]
        for cp in copies:
            cp.start()
        for cp in copies:
            cp.wait_recv()
        for cp in copies:
            cp.wait_send()

    return _hbm_call(name, body, arrs, [jax.ShapeDtypeStruct(a.shape, a.dtype) for a in arrs], n)


def _row_tile(rows):
    for t in (256, 128, 64, 32, 16, 8):
        if rows % t == 0:
            return t
    return rows


def _pair_sum(name, a, b, out_dtype):
    rows, cols = a.shape
    tr = _row_tile(rows)

    def body(a_ref, b_ref, o_ref):
        o_ref[...] = (a_ref[...] + b_ref[...]).astype(out_dtype)

    return pl.pallas_call(body, name=name, grid=(rows // tr,), in_specs=[_row(tr, cols)] * 2, out_specs=_row(tr, cols),
                          out_shape=_sds(rows, cols, dtype=out_dtype), compiler_params=_cp("arbitrary"))(a, b)


def _chip_sum(name, recv, partial, mine):
    n, rows, cols = recv.shape
    tr = _row_tile(rows)

    def body(mine_ref, *refs):
        own = refs[n][0].astype(F32)
        acc = None
        for s in range(n):
            term = jnp.where(mine_ref[0] == s, own, refs[s][0].astype(F32))
            acc = term if acc is None else acc + term
        refs[-1][...] = acc

    def slot_spec(s):
        return pl.BlockSpec((1, tr, cols), lambda i, m: (jnp.where(m[0] == s, (s + 1) % n, s), i, 0))

    grid_spec = pltpu.PrefetchScalarGridSpec(
        num_scalar_prefetch=1, grid=(rows // tr,),
        in_specs=[slot_spec(s) for s in range(n)] + [pl.BlockSpec((1, tr, cols), lambda i, m: (m[0], i, 0))],
        out_specs=pl.BlockSpec((tr, cols), lambda i, m: (i, 0)))
    return pl.pallas_call(body, name=name, grid_spec=grid_spec, out_shape=_sds(rows, cols),
                          compiler_params=_cp("arbitrary"))(mine, *([recv] * n), partial)


def _slot_sum(name, arr):
    n, rows, cols = arr.shape
    tr = _row_tile(rows)

    def body(*refs):
        acc = refs[0][0]
        for r in refs[1:-1]:
            acc = acc + r[0]
        refs[-1][...] = acc

    specs = [pl.BlockSpec((1, tr, cols), functools.partial(lambda s, i: (s, i, 0), s)) for s in range(n)]
    return pl.pallas_call(body, name=name, grid=(rows // tr,), in_specs=specs, out_specs=_row(tr, cols),
                          out_shape=_sds(rows, cols), compiler_params=_cp("arbitrary"))(*([arr] * n))


def _adamw(name, w, g, m, v):
    rows, cols = w.shape
    tr = _row_tile(rows)

    def body(w_ref, g_ref, m_ref, v_ref, d_ref, nm_ref, nv_ref):
        g_ = g_ref[...]
        m_ = ADAM_B1 * m_ref[...] + (1.0 - ADAM_B1) * g_
        v_ = ADAM_B2 * v_ref[...] + (1.0 - ADAM_B2) * (g_ * g_)
        m_hat = m_ / (1.0 - ADAM_B1 ** ADAM_STEP)
        v_hat = v_ / (1.0 - ADAM_B2 ** ADAM_STEP)
        d_ref[...] = -ADAM_LR * (m_hat / (jnp.sqrt(v_hat) + ADAM_EPS) + ADAM_WD * w_ref[...])
        nm_ref[...] = m_
        nv_ref[...] = v_

    spec = _row(tr, cols)
    return pl.pallas_call(body, name=name, grid=(rows // tr,), in_specs=[spec] * 4, out_specs=[spec] * 3,
                          out_shape=[_sds(rows, cols)] * 3, compiler_params=_cp("arbitrary"))(w, g, m, v)


def _adamw_halves(name, w, g_mine, g_sibling, m, v, core):
    rows, cols = w.shape
    half = rows // 2
    tr = _row_tile(half)
    per_half = half // tr

    def body(core_ref, w_ref, gm_ref, gs_ref, m_ref, v_ref, g_ref, d_ref, nm_ref, nv_ref):
        g_ = jnp.where(pl.program_id(0) // per_half == core_ref[0], gm_ref[...], gs_ref[...])
        m_ = ADAM_B1 * m_ref[...] + (1.0 - ADAM_B1) * g_
        v_ = ADAM_B2 * v_ref[...] + (1.0 - ADAM_B2) * (g_ * g_)
        m_hat = m_ / (1.0 - ADAM_B1 ** ADAM_STEP)
        v_hat = v_ / (1.0 - ADAM_B2 ** ADAM_STEP)
        g_ref[...] = g_
        d_ref[...] = -ADAM_LR * (m_hat / (jnp.sqrt(v_hat) + ADAM_EPS) + ADAM_WD * w_ref[...])
        nm_ref[...] = m_
        nv_ref[...] = v_

    full = pl.BlockSpec((tr, cols), lambda i, c: (i, 0))
    in_half = pl.BlockSpec((tr, cols), lambda i, c: (i % per_half, 0))
    grid_spec = pltpu.PrefetchScalarGridSpec(num_scalar_prefetch=1, grid=(rows // tr,),
                                             in_specs=[full, in_half, in_half, full, full], out_specs=[full] * 4)
    return pl.pallas_call(body, name=name, grid_spec=grid_spec, out_shape=[_sds(rows, cols)] * 4,
                          compiler_params=_cp("arbitrary"))(core, w, g_mine, g_sibling, m, v)


def _mod_local(c_all, ada_w):
    def body(c_ref, w_ref, o_ref):
        c_act = jax.nn.silu(c_ref[...])
        for l in range(2):
            o_ref[l] = _hdot(c_act, w_ref[l])

    return pl.pallas_call(body, name="mod_local", out_shape=_sds(2, 8, ada_w.shape[2]),
                          compiler_params=pltpu.CompilerParams(vmem_limit_bytes=VMEM_LIMIT_BYTES))(c_all, ada_w)


def _ada_w_grad(c_all, dmod_cols):
    def body(c_ref, d_ref, o_ref):
        c_act = jax.nn.silu(c_ref[...])
        for l in range(2):
            o_ref[l] = lax.dot_general(c_act, d_ref[l], (((0,), (0,)), ((), ())), precision=HI, preferred_element_type=F32)

    return pl.pallas_call(body, name="ada_w_grad", out_shape=_sds(2, D_MODEL, dmod_cols.shape[2]),
                          compiler_params=pltpu.CompilerParams(vmem_limit_bytes=VMEM_LIMIT_BYTES))(c_all, dmod_cols)


_SMALL = ("ada_b", "pre_g", "post_g", "rel_bias", "s5_a_re", "s5_a_im", "s5_log_dt", "s5_b_re", "s5_b_im", "s5_c_re", "s5_c_im",
          "s5_d", "s5_glu_b", "gdn_a_log", "gdn_dt_bias", "gdn_norm_g")
_SHARDED = ("ab_w_in", "ab_w_out", "s5_glu_w", "gdn_w_in", "gdn_w_out")
_COL_SHARDED = ("ab_w_in", "gdn_w_in")
_WEIGHTS = ("ada_w", "ada_b", "pre_g", "post_g", "rel_bias", "ab_w_in", "ab_w_out", "s5_a_re", "s5_a_im", "s5_log_dt", "s5_b_re",
            "s5_b_im", "s5_c_re", "s5_c_im", "s5_d", "s5_glu_w", "s5_glu_b", "gdn_w_in", "gdn_conv", "gdn_a_log", "gdn_dt_bias",
            "gdn_norm_g", "gdn_w_out")


def _rows128(n):
    return -(-n // 128)


def _pack(arrs, total_rows):
    pieces = []
    for a in arrs:
        flat = a.reshape(-1)
        pieces.append(jnp.pad(flat, (0, _rows128(flat.shape[0]) * 128 - flat.shape[0])).reshape(-1, 128))
    used = sum(p.shape[0] for p in pieces)
    pieces.append(jnp.zeros((total_rows - used, 128), F32))
    return jnp.concatenate(pieces, axis=0)


def _unpack(buf, shapes):
    out, at = [], 0
    for shp in shapes:
        n = int(np.prod(shp))
        out.append(buf[at:at + _rows128(n)].reshape(-1)[:n].reshape(shp))
        at += _rows128(n)
    return out


def _full_from_halves(name, g):
    if name in _COL_SHARDED:
        return g.transpose(0, 2, 1, 3).reshape(2 * g.shape[2], 4 * g.shape[3])
    return g.transpose(1, 0, 2, 3).reshape(8 * g.shape[2], g.shape[3])


def _shard_major(name, g):
    if name in _COL_SHARDED:
        return g.reshape(g.shape[0], 4, g.shape[1] // 4).transpose(1, 0, 2)
    return g.reshape(4, g.shape[0] // 4, g.shape[1])


def kernel(x, c, ada_w, ada_b, pre_g, post_g, rel_bias, ab_w_in, ab_w_out, s5_a_re, s5_a_im, s5_log_dt, s5_b_re, s5_b_im, s5_c_re, s5_c_im, s5_d, s5_glu_w, s5_glu_b, gdn_w_in, gdn_conv, gdn_a_log, gdn_dt_bias, gdn_norm_g, gdn_w_out, loss_target, m_ada_w, m_ada_b, m_pre_g, m_post_g, m_rel_bias, m_ab_w_in, m_ab_w_out, m_s5_a_re, m_s5_a_im, m_s5_log_dt, m_s5_b_re, m_s5_b_im, m_s5_c_re, m_s5_c_im, m_s5_d, m_s5_glu_w, m_s5_glu_b, m_gdn_w_in, m_gdn_conv, m_gdn_a_log, m_gdn_dt_bias, m_gdn_norm_g, m_gdn_w_out, v_ada_w, v_ada_b, v_pre_g, v_post_g, v_rel_bias, v_ab_w_in, v_ab_w_out, v_s5_a_re, v_s5_a_im, v_s5_log_dt, v_s5_b_re, v_s5_b_im, v_s5_c_re, v_s5_c_im, v_s5_d, v_s5_glu_w, v_s5_glu_b, v_gdn_w_in, v_gdn_conv, v_gdn_a_log, v_gdn_dt_bias, v_gdn_norm_g, v_gdn_w_out):
    w = dict(ada_w=ada_w, ada_b=ada_b, pre_g=pre_g, post_g=post_g, rel_bias=rel_bias, ab_w_in=ab_w_in, ab_w_out=ab_w_out,
             s5_a_re=s5_a_re, s5_a_im=s5_a_im, s5_log_dt=s5_log_dt, s5_b_re=s5_b_re, s5_b_im=s5_b_im, s5_c_re=s5_c_re, s5_c_im=s5_c_im,
             s5_d=s5_d, s5_glu_w=s5_glu_w, s5_glu_b=s5_glu_b, gdn_w_in=gdn_w_in, gdn_conv=gdn_conv, gdn_a_log=gdn_a_log,
             gdn_dt_bias=gdn_dt_bias, gdn_norm_g=gdn_norm_g, gdn_w_out=gdn_w_out)
    m = dict(ada_w=m_ada_w, ada_b=m_ada_b, pre_g=m_pre_g, post_g=m_post_g, rel_bias=m_rel_bias, ab_w_in=m_ab_w_in, ab_w_out=m_ab_w_out,
             s5_a_re=m_s5_a_re, s5_a_im=m_s5_a_im, s5_log_dt=m_s5_log_dt, s5_b_re=m_s5_b_re, s5_b_im=m_s5_b_im, s5_c_re=m_s5_c_re,
             s5_c_im=m_s5_c_im, s5_d=m_s5_d, s5_glu_w=m_s5_glu_w, s5_glu_b=m_s5_glu_b, gdn_w_in=m_gdn_w_in, gdn_conv=m_gdn_conv,
             gdn_a_log=m_gdn_a_log, gdn_dt_bias=m_gdn_dt_bias, gdn_norm_g=m_gdn_norm_g, gdn_w_out=m_gdn_w_out)
    v = dict(ada_w=v_ada_w, ada_b=v_ada_b, pre_g=v_pre_g, post_g=v_post_g, rel_bias=v_rel_bias, ab_w_in=v_ab_w_in, ab_w_out=v_ab_w_out,
             s5_a_re=v_s5_a_re, s5_a_im=v_s5_a_im, s5_log_dt=v_s5_log_dt, s5_b_re=v_s5_b_re, s5_b_im=v_s5_b_im, s5_c_re=v_s5_c_re,
             s5_c_im=v_s5_c_im, s5_d=v_s5_d, s5_glu_w=v_s5_glu_w, s5_glu_b=v_s5_glu_b, gdn_w_in=v_gdn_w_in, gdn_conv=v_gdn_conv,
             gdn_a_log=v_gdn_a_log, gdn_dt_bias=v_gdn_dt_bias, gdn_norm_g=v_gdn_norm_g, gdn_w_out=v_gdn_w_out)
    ix, iy, ic = _place()
    me = 4 * ix + 2 * iy + ic
    chip = 2 * ix + iy
    n_cols = ada_w.shape[2]

    mine_first = _pack([c, gdn_conv], 32)
    first = _own_slot(_all_gather8("gather_c_conv", mine_first), mine_first, me)
    c_all = first[:, 0:8].reshape(8, D_MODEL)
    conv_full = first[0::2, 8:32].reshape(4, C_CONV, n_cols).transpose(1, 0, 2).reshape(C_CONV, 4 * n_cols)
    mine_mod = _mod_local(c_all, ada_w)
    modl = _own_slot(_all_gather8("gather_mod", mine_mod), mine_mod, me)
    mod = lax.dynamic_index_in_dim(modl[0::2], me, axis=2, keepdims=False)
    mod = mod.transpose(1, 0, 2).reshape(2, 4 * n_cols) + ada_b

    halves = []
    for name in _SHARDED:
        shard = w[name][0].astype(BF)
        h = shard.shape[0] // 2
        halves.append(lax.dynamic_slice_in_dim(shard, ic * h, h, axis=0))
    from_chips = _chip_exchange("gather_w_chips", halves, False)
    my_halves = [_own_slot(g, own, chip) for g, own in zip(from_chips, halves)]
    their_halves = _sibling_exchange("gather_w_sibling", my_halves)
    wd = {name: w[name] for name in _SMALL if name != "ada_b"}
    wd = {k: (a if k in ("pre_g", "post_g", "rel_bias") else a[0]) for k, a in wd.items()}
    wd["gdn_conv"] = conv_full
    for name, a, b in zip(_SHARDED, my_halves, their_halves):
        wd[name] = _full_from_halves(name, jnp.where(ic == 0, jnp.stack([a, b], 0), jnp.stack([b, a], 0)))

    loss_local, grad_x, grads, dmod = _local_step(x[0], loss_target[0], mod, wd)
    loss = lax.psum(loss_local, ("x", "y", "c"))

    small_shapes = [w[name].shape for name in _SMALL] + [(C_CONV, 4 * n_cols)]
    small_rows = -(-sum(_rows128(int(np.prod(s))) for s in small_shapes) // 64) * 64
    per_dev, dmod_rows = small_rows // 8, _rows128(2 * 3 * D_MODEL)
    partial = _pack([dmod] + [grads[name] for name in _SMALL[1:]] + [grads["gdn_conv"]], small_rows)
    outbound = jnp.concatenate([partial.reshape(8, per_dev, 128), jnp.broadcast_to(partial[None, :dmod_rows], (8, dmod_rows, 128))], axis=1)
    inbound = _own_slot(_all_to_all8("reduce_small_grads", outbound), lax.dynamic_index_in_dim(outbound, me, 0, keepdims=False), me)
    my_rows = _slot_sum("sum_small_grads", inbound[:, :per_dev])
    g_small = _own_slot(_all_gather8("gather_small_grads", my_rows), my_rows, me).reshape(small_rows, 128)
    g_list = _unpack(g_small, small_shapes)
    out_g, out_d, out_m, out_v = {}, {}, {}, {}

    def update(name, g2d):
        shp = w[name].shape
        two_d = lambda a: a.reshape(-1, shp[-1])
        d_, m_, v_ = _adamw("adamw_" + name, two_d(w[name]), g2d, two_d(m[name]), two_d(v[name]))
        out_g[name], out_d[name], out_m[name], out_v[name] = (a.reshape(shp) for a in (g2d, d_, m_, v_))

    for name, g in zip(_SMALL, g_list[:-1]):
        update(name, g.reshape(-1, g.shape[-1]))
    update("gdn_conv", lax.dynamic_slice_in_dim(g_list[-1], chip * n_cols, n_cols, axis=1))

    dmod_all = inbound[:, per_dev:].reshape(8, 2, 4, n_cols)
    dmod_cols = lax.dynamic_index_in_dim(dmod_all, chip, axis=2, keepdims=False).transpose(1, 0, 2)
    update("ada_w", _ada_w_grad(c_all, dmod_cols).reshape(-1, n_cols))

    mine, other = [], []
    for name in _SHARDED:
        sm = _shard_major(name, grads[name])
        h = sm.shape[1] // 2
        mine.append(lax.dynamic_slice_in_dim(sm, ic * h, h, axis=1))
        other.append(lax.dynamic_slice_in_dim(sm, (1 - ic) * h, h, axis=1))
    from_sibling = _sibling_exchange("reduce_sibling", other)
    chip_partials = []
    for name, a, b in zip(_SHARDED, mine, from_sibling):
        flat = lambda t: t.reshape(-1, t.shape[-1])
        chip_partials.append(_pair_sum("sum_sibling_" + name, flat(a), flat(b), BF).reshape(a.shape))
    from_all = _chip_exchange("reduce_chips", chip_partials, True)
    chip_1 = jnp.reshape(chip, (1,)).astype(jnp.int32)
    core_1 = jnp.reshape(ic, (1,)).astype(jnp.int32)
    reduced = [_chip_sum("sum_chips_" + name, t, p, chip_1) for name, t, p in zip(_SHARDED, from_all, chip_partials)]
    for name, g_mine, g_sib in zip(_SHARDED, reduced, _sibling_exchange("reduce_share", reduced)):
        shp = w[name].shape
        two_d = lambda a: a.reshape(-1, shp[-1])
        outs = _adamw_halves("adamw_" + name, two_d(w[name]), g_mine, g_sib, two_d(m[name]), two_d(v[name]), core_1)
        out_g[name], out_d[name], out_m[name], out_v[name] = (a.reshape(shp) for a in outs)

    return (loss, grad_x[None], *[out_g[n] for n in _WEIGHTS], *[out_d[n] for n in _WEIGHTS],
            *[out_m[n] for n in _WEIGHTS], *[out_v[n] for n in _WEIGHTS])
```

```python
import functools
import math

import numpy as np
import jax
import jax.numpy as jnp
from jax import lax
from jax.experimental import pallas as pl
from jax.experimental.pallas import tpu as pltpu

F32 = jnp.float32
BF = jnp.bfloat16
HI = lax.Precision.HIGHEST
MESH = pl.DeviceIdType.MESH

D_MODEL = 1024
EPS = 1e-6
A_HEADS, A_HD, A_WIDTH, A_BLOCK = 8, 64, 512, 128
DILATIONS = (1, 4, 16)
N_KEYS = 128
REL_BUCKETS, REL_MAX_DIST = 32, 2048
B_WIDTH, B_GROUP, B_GROUPS, B_STATE = 512, 16, 32, 64
S5_LANES = 512
S5_TILES = 4
S5_T = 256
C_HEADS, C_DK, C_CHUNK, C_CONV = 8, 128, 64, 4
QKV = 3072
C_IN_PAD = 4224
TM = 256
VMEM_LIMIT_BYTES = 56 * 1024 * 1024
ADAM_LR, ADAM_B1, ADAM_B2, ADAM_EPS, ADAM_WD, ADAM_STEP = 0.001, 0.9, 0.999, 1e-08, 0.01, 10
NEG = float(np.finfo(np.float32).min)


def _cp(*sem):
    return pltpu.CompilerParams(dimension_semantics=sem, vmem_limit_bytes=VMEM_LIMIT_BYTES)


def _bdot(a, b):
    return jnp.dot(a.astype(BF), b.astype(BF), preferred_element_type=F32)


def _bdot_nt(a, b):
    return lax.dot_general(a.astype(BF), b.astype(BF), (((1,), (1,)), ((), ())), preferred_element_type=F32)


def _bdot_tn(a, b):
    return lax.dot_general(a.astype(BF), b.astype(BF), (((0,), (0,)), ((), ())), preferred_element_type=F32)


def _hdot(a, b):
    return jnp.dot(a, b, precision=HI, preferred_element_type=F32)


def _bein(eq, a, b):
    return jnp.einsum(eq, a.astype(BF), b.astype(BF), preferred_element_type=F32)


def _hein(eq, a, b):
    return jnp.einsum(eq, a, b, precision=HI, preferred_element_type=F32)


def _row(tm, n):
    return pl.BlockSpec((tm, n), lambda i: (i, 0))


def _fix(shape):
    return pl.BlockSpec(shape, lambda i: (0,) * len(shape))


def _sds(*shape, dtype=F32):
    return jax.ShapeDtypeStruct(shape, dtype)


def _acc(ref, val):
    ref[...] += val


def _zero_at_first(refs, axis=0):
    @pl.when(pl.program_id(axis) == 0)
    def _():
        for r in refs:
            r[...] = jnp.zeros_like(r)


def _rms(x):
    return x * lax.rsqrt(jnp.mean(x * x, axis=-1, keepdims=True) + EPS)


def _pre_mod(x, g, scale, shift):
    return (_rms(x) * g) * (1.0 + scale) + shift


def _post_res(y, x, post_g, gate):
    return x + gate * (_rms(y) * post_g)


def _merge_gate(o1, o2, o3, l1, l2, l3, ga):
    m = jnp.maximum(jnp.maximum(l1, l2), l3)
    e1, e2, e3 = jnp.exp(l1 - m), jnp.exp(l2 - m), jnp.exp(l3 - m)
    inv = 1.0 / (e1 + e2 + e3)
    return ((e1 * inv) * o1 + (e2 * inv) * o2 + (e3 * inv) * o3) * jax.nn.silu(ga)


def _s5_gelu(ypre, u, d_skip):
    return jax.nn.gelu(ypre + d_skip * u)


def _s5_glu(yb, gl, gb):
    return yb * jax.nn.sigmoid(gl) * jax.nn.silu(gb)


def _l0_front(x, pre_g, scale, shift, w_in):
    s_len = x.shape[0]

    def body(x_ref, g_ref, sc_ref, sh_ref, w_ref, *out_refs):
        qkv_refs, (u_ref, ga_ref, gb_ref, h_ref) = out_refs[:9], out_refs[9:]
        hb = _pre_mod(x_ref[...], g_ref[...], sc_ref[...], sh_ref[...]).astype(BF)
        h_ref[...] = hb
        z = jnp.dot(hb, w_ref[...], preferred_element_type=F32)
        for a in range(3):
            piece = z[:, a * 512:(a + 1) * 512]
            for i, d in enumerate(DILATIONS):
                qkv_refs[3 * a + i][...] = _to_res(piece, d)
        u_ref[...] = z[:, 1536:2048]
        ga_ref[...] = z[:, 2048:2560]
        gb_ref[...] = z[:, 2560:3072]

    vec = _fix((1, D_MODEL))
    return pl.pallas_call(
        body, name="l0_front", grid=(s_len // TM,),
        in_specs=[_row(TM, D_MODEL), vec, vec, vec, _fix((D_MODEL, 3072))],
        out_specs=[_res_spec(d) for d in DILATIONS] * 3 + [_row(TM, 512)] * 3 + [_row(TM, D_MODEL)],
        out_shape=[_sds(*_res_shape(s_len, d)) for d in DILATIONS] * 3 + [_sds(s_len, 512)] * 3 + [_sds(s_len, D_MODEL, dtype=BF)],
        compiler_params=_cp("arbitrary"),
    )(x, pre_g, scale, shift, w_in)


def _front_bwd(name, x, pre_g, scale, shift, w_in, dres, parts, widths):
    s_len = x.shape[0]
    n_in = sum(len(p) for p in parts)
    n_cols = sum(widths)

    def body(*refs):
        x_ref, g_ref, sc_ref, sh_ref, w_ref, dres_ref = refs[:6]
        part_refs = refs[6:6 + n_in]
        dz_ref, dx_ref, dg_ref, dsc_ref, dsh_ref = refs[6 + n_in:]
        _zero_at_first([dg_ref, dsc_ref, dsh_ref])
        _, vjp = jax.vjp(_pre_mod, x_ref[...], g_ref[...], sc_ref[...], sh_ref[...])
        dh = jnp.zeros((TM, D_MODEL), F32)
        col, at = 0, 0
        for grp, width in zip(parts, widths):
            tile = lambda r: _from_res(r[...]) if len(r.shape) == 3 else r[...]
            dz = tile(part_refs[at])
            for r in part_refs[at + 1:at + len(grp)]:
                dz = dz + tile(r)
            at += len(grp)
            dzb = dz.astype(BF)
            dz_ref[:, col:col + width] = dzb
            dh = dh + lax.dot_general(dzb, w_ref[:, col:col + width], (((1,), (1,)), ((), ())), preferred_element_type=F32)
            col += width
        dx, dg, dsc, dsh = vjp(dh)
        dx_ref[...] = dx + dres_ref[...]
        _acc(dg_ref, dg)
        _acc(dsc_ref, dsc)
        _acc(dsh_ref, dsh)

    vec = _fix((1, D_MODEL))
    flat = [a for p in parts for a in p]
    return pl.pallas_call(
        body, name=name, grid=(s_len // TM,),
        in_specs=[_row(TM, D_MODEL), vec, vec, vec, _fix((D_MODEL, n_cols)), _row(TM, D_MODEL)]
        + [_res_spec(a.shape[0], a.shape[2]) if a.ndim == 3 else _row(TM, a.shape[1]) for a in flat],
        out_specs=[_row(TM, n_cols), _row(TM, D_MODEL), vec, vec, vec],
        out_shape=[_sds(s_len, n_cols, dtype=BF), _sds(s_len, D_MODEL), _sds(1, D_MODEL), _sds(1, D_MODEL), _sds(1, D_MODEL)],
        compiler_params=_cp("arbitrary"),
    )(x, pre_g, scale, shift, w_in, dres, *flat)


def _matmul_tn(name, a, b, tn):
    s_len, k_dim = a.shape
    n_dim = b.shape[1]
    ts = 512

    def body(a_ref, b_ref, o_ref):
        _zero_at_first([o_ref], axis=1)
        o_ref[...] += lax.dot_general(a_ref[...], b_ref[...], (((0,), (0,)), ((), ())), preferred_element_type=F32)

    return pl.pallas_call(
        body, name=name, grid=(n_dim // tn, s_len // ts),
        in_specs=[pl.BlockSpec((ts, k_dim), lambda j, i: (i, 0)), pl.BlockSpec((ts, tn), lambda j, i: (i, j))],
        out_specs=pl.BlockSpec((k_dim, tn), lambda j, i: (0, j)),
        out_shape=_sds(k_dim, n_dim),
        compiler_params=_cp("arbitrary", "arbitrary"),
    )(a, b)


def _t5_bucket_np(dist):
    dist = np.maximum(dist, 0)
    max_exact = REL_BUCKETS // 2
    large = max_exact + (np.log(np.maximum(dist, 1) / max_exact)
                         / math.log(REL_MAX_DIST / max_exact) * (REL_BUCKETS - max_exact)).astype(np.int32)
    large = np.minimum(large, REL_BUCKETS - 1)
    return np.where(dist < max_exact, dist, large).astype(np.int32)


def _to_res(z, dil):
    if dil == 1:
        return z[None]
    return jnp.swapaxes(z.reshape(z.shape[0] // dil, dil, z.shape[1]), 0, 1)


def _from_res(z):
    if z.shape[0] == 1:
        return z[0]
    return jnp.swapaxes(z, 0, 1).reshape(z.shape[0] * z.shape[1], z.shape[2])


def _res_shape(s_len, dil, width=A_WIDTH):
    return (dil, s_len // dil, width)


def _res_spec(dil, width=A_WIDTH):
    return pl.BlockSpec((dil, TM // dil, width), lambda i: (0, i, 0))


def _bucket_table():
    qi = np.arange(A_BLOCK)[:, None]
    kj = np.arange(2 * A_BLOCK)[None, :]
    return np.stack([_t5_bucket_np((qi + A_BLOCK - kj) * d) for d in DILATIONS], 0)


def _attn_mask(first):
    qi = lax.broadcasted_iota(jnp.int32, (A_BLOCK, 2 * A_BLOCK), 0)
    kj = lax.broadcasted_iota(jnp.int32, (A_BLOCK, 2 * A_BLOCK), 1)
    rel = qi + A_BLOCK - kj
    return (rel >= 0) & (rel <= N_KEYS) & (jnp.logical_not(first) | (kj >= A_BLOCK))


def _attn_specs(nb, rev):
    n_of = (lambda i: nb - 1 - i) if rev else (lambda i: i)
    cur = pl.BlockSpec((None, A_BLOCK, A_WIDTH), lambda r, i: (r, n_of(i), 0))
    prev = pl.BlockSpec((None, A_BLOCK, A_WIDTH), lambda r, i: (r, jnp.maximum(n_of(i) - 1, 0), 0))
    bias = pl.BlockSpec((A_HEADS, A_BLOCK, 2 * A_BLOCK), lambda r, i: (0, 0, 0))
    return cur, prev, bias


def _attn_fwd(q, k, v, bias):
    dil, t_len, _ = q.shape
    nb = t_len // A_BLOCK
    scale = A_HD ** -0.5

    def body(q_ref, kp_ref, kc_ref, vp_ref, vc_ref, b_ref, o_ref, l_ref):
        mask = _attn_mask(pl.program_id(1) == 0)
        lane = lax.broadcasted_iota(jnp.int32, (1, 128), 1)
        for hp in range(A_HEADS // 2):
            sl = slice(hp * 128, (hp + 1) * 128)
            qp = q_ref[:, sl]
            kw = jnp.concatenate([kp_ref[:, sl], kc_ref[:, sl]], axis=0).astype(BF)
            vw = jnp.concatenate([vp_ref[:, sl], vc_ref[:, sl]], axis=0).astype(BF)
            outs, lses = [], []
            for j in range(2):
                hm = (lane < 64) if j == 0 else (lane >= 64)
                s = _bdot_nt(jnp.where(hm, qp, 0.0), kw) * scale
                s = jnp.where(mask, s + b_ref[2 * hp + j], NEG)
                m = jnp.max(s, axis=-1, keepdims=True)
                p = jnp.exp(s - m)
                den = jnp.sum(p, axis=-1, keepdims=True)
                outs.append(_bdot(p, vw) / den)
                lses.append(m + jnp.log(den))
            hm0 = lane < 64
            o_ref[:, sl] = jnp.where(hm0, outs[0], outs[1])
            l_ref[:, sl] = jnp.where(hm0, lses[0], lses[1])

    cur, prev, bias_spec = _attn_specs(nb, False)
    return pl.pallas_call(
        body, name=f"attn_fwd_d{dil}", grid=(dil, nb),
        in_specs=[cur, prev, cur, prev, cur, bias_spec],
        out_specs=[cur, cur],
        out_shape=[_sds(dil, t_len, A_WIDTH)] * 2,
        compiler_params=_cp("arbitrary", "arbitrary"),
    )(q, k, k, v, v, bias)


def _attn_bwd(q, k, v, bias, o, l, do, dl):
    dil, t_len, _ = q.shape
    nb = t_len // A_BLOCK
    scale = A_HD ** -0.5

    def body(q_ref, kp_ref, kc_ref, vp_ref, vc_ref, b_ref, o_ref, l_ref, do_ref, dl_ref,
             dq_ref, dk_ref, dv_ref, db_ref, ck_ref, cv_ref):
        _zero_at_first([ck_ref, cv_ref], axis=1)

        @pl.when((pl.program_id(0) == 0) & (pl.program_id(1) == 0))
        def _():
            db_ref[...] = jnp.zeros_like(db_ref)

        mask = _attn_mask(pl.program_id(1) == nb - 1)
        lane = lax.broadcasted_iota(jnp.int32, (1, 128), 1)
        for hp in range(A_HEADS // 2):
            sl = slice(hp * 128, (hp + 1) * 128)
            qp = q_ref[:, sl]
            kw = jnp.concatenate([kp_ref[:, sl], kc_ref[:, sl]], axis=0).astype(BF)
            vw = jnp.concatenate([vp_ref[:, sl], vc_ref[:, sl]], axis=0).astype(BF)
            op, lp, dop, dlp = o_ref[:, sl], l_ref[:, sl], do_ref[:, sl], dl_ref[:, sl]
            dq_acc = jnp.zeros((A_BLOCK, 128), F32)
            dk_acc = jnp.zeros((2 * A_BLOCK, 128), F32)
            dv_acc = jnp.zeros((2 * A_BLOCK, 128), F32)
            for j in range(2):
                hm = (lane < 64) if j == 0 else (lane >= 64)
                qm = jnp.where(hm, qp, 0.0)
                s = _bdot_nt(qm, kw) * scale
                s = jnp.where(mask, s + b_ref[2 * hp + j], NEG)
                lse = jnp.max(jnp.where(hm, lp, NEG), axis=-1, keepdims=True)
                p = jnp.exp(s - lse)
                do_h = jnp.where(hm, dop, 0.0)
                dd = jnp.sum(do_h * op, axis=-1, keepdims=True)
                dlse = jnp.sum(jnp.where(hm, dlp, 0.0), axis=-1, keepdims=True)
                ds = p * (_bdot_nt(do_h, vw) - dd + dlse)
                dv_acc = dv_acc + _bdot_tn(p, do_h)
                dq_acc = dq_acc + jnp.where(hm, _bdot(ds, kw), 0.0) * scale
                dk_acc = dk_acc + _bdot_tn(ds, qm) * scale
                db_ref[2 * hp + j] += ds
            dq_ref[:, sl] = dq_acc
            dk_ref[:, sl] = dk_acc[A_BLOCK:] + ck_ref[:, sl]
            dv_ref[:, sl] = dv_acc[A_BLOCK:] + cv_ref[:, sl]
            ck_ref[:, sl] = dk_acc[:A_BLOCK]
            cv_ref[:, sl] = dv_acc[:A_BLOCK]

    cur, prev, bias_spec = _attn_specs(nb, True)
    return pl.pallas_call(
        body, name=f"attn_bwd_d{dil}", grid=(dil, nb),
        in_specs=[cur, prev, cur, prev, cur, bias_spec, cur, cur, cur, cur],
        out_specs=[cur, cur, cur, bias_spec],
        out_shape=[_sds(dil, t_len, A_WIDTH)] * 3 + [_sds(A_HEADS, A_BLOCK, 2 * A_BLOCK)],
        scratch_shapes=[pltpu.VMEM((A_BLOCK, A_WIDTH), F32)] * 2,
        compiler_params=_cp("arbitrary", "arbitrary"),
    )(q, k, k, v, v, bias, o, l, do, dl)


def _attn_bias(rel_bias, table):
    def body(rb_ref, t_ref, *o_refs):
        for c in range(3):
            t = t_ref[c]
            acc = [jnp.zeros((A_BLOCK, 2 * A_BLOCK), F32) for _ in range(A_HEADS)]
            for b in range(REL_BUCKETS):
                hit = t == b
                acc = [jnp.where(hit, rb_ref[b, h], acc[h]) for h in range(A_HEADS)]
            for h in range(A_HEADS):
                o_refs[c][h] = acc[h]

    return pl.pallas_call(body, name="attn_bias", out_shape=[_sds(A_HEADS, A_BLOCK, 2 * A_BLOCK)] * 3,
                          in_specs=[pl.BlockSpec(memory_space=pltpu.SMEM), pl.BlockSpec(memory_space=pltpu.VMEM)],
                          compiler_params=pltpu.CompilerParams(vmem_limit_bytes=VMEM_LIMIT_BYTES))(rel_bias, table)


def _rel_bias_grad(dbs, idx_rows):
    n = A_BLOCK * 2 * A_BLOCK

    def body(d0_ref, d1_ref, d2_ref, idx_ref, o_ref):
        bucket = lax.broadcasted_iota(jnp.int32, (REL_BUCKETS, n), 0).astype(F32)
        acc = jnp.zeros((A_HEADS, REL_BUCKETS), F32)
        for c, db_ref in enumerate((d0_ref, d1_ref, d2_ref)):
            onehot = (idx_ref[c:c + 1, :] == bucket).astype(F32)
            acc = acc + lax.dot_general(db_ref[...], onehot, (((1,), (1,)), ((), ())), precision=HI, preferred_element_type=F32)
        o_ref[...] = acc

    return pl.pallas_call(body, name="rel_bias_grad", out_shape=_sds(A_HEADS, REL_BUCKETS),
                          compiler_params=pltpu.CompilerParams(vmem_limit_bytes=VMEM_LIMIT_BYTES))(
                              *[d.reshape(A_HEADS, n) for d in dbs], idx_rows)


def _s5_param_fn(a_re, a_im, log_dt, bt_re, bt_im):
    dt = jnp.exp(log_dt)
    mag = jnp.exp(dt * a_re)
    abar_r, abar_i = mag * jnp.cos(dt * a_im), mag * jnp.sin(dt * a_im)
    den = a_re * a_re + a_im * a_im
    fr = ((abar_r - 1.0) * a_re + abar_i * a_im) / den
    fi = (abar_i * a_re - (abar_r - 1.0) * a_im) / den
    row = lax.broadcasted_iota(jnp.int32, (B_WIDTH, B_GROUPS), 0)
    grp = lax.broadcasted_iota(jnp.int32, (B_WIDTH, B_GROUPS), 1)
    expand = ((row // B_GROUP) == grp).astype(F32)
    fr_e, fi_e = _hdot(expand, fr), _hdot(expand, fi)
    return abar_r, abar_i, fr_e * bt_re - fi_e * bt_im, fr_e * bt_im + fi_e * bt_re


def _s5_params(a_re, a_im, log_dt, bt_re, bt_im):
    def body(ar, ai, ld, br, bi, o1, o2, o3, o4):
        o1[...], o2[...], o3[...], o4[...] = _s5_param_fn(ar[...], ai[...], ld[...], br[...], bi[...])

    return pl.pallas_call(body, name="s5_params",
                          out_shape=[_sds(B_GROUPS, B_STATE)] * 2 + [_sds(B_WIDTH, B_STATE)] * 2)(a_re, a_im, log_dt, bt_re, bt_im)


def _s5_params_bwd(a_re, a_im, log_dt, bt_re, bt_im, d1, d2, d3, d4):
    def body(ar, ai, ld, br, bi, c1, c2, c3, c4, o1, o2, o3, o4, o5):
        _, vjp = jax.vjp(_s5_param_fn, ar[...], ai[...], ld[...], br[...], bi[...])
        o1[...], o2[...], o3[...], o4[...], o5[...] = vjp((c1[...], c2[...], c3[...], c4[...]))

    return pl.pallas_call(body, name="s5_params_bwd",
                          out_shape=[_sds(B_GROUPS, B_STATE)] * 2 + [_sds(B_GROUPS, 1)] + [_sds(B_WIDTH, B_STATE)] * 2,
                          )(a_re, a_im, log_dt, bt_re, bt_im, d1, d2, d3, d4)


S5_SUB = 8
S5_GROUPS = S5_T // S5_SUB


def _dscan(xr, xi, cr, ci, period, reverse):
    n = xr.shape[0]
    rows = lax.broadcasted_iota(jnp.int32, xr.shape, 0)
    pos = rows % period
    k = 1
    while k < period:
        if reverse:
            keep, shift = pos < period - k, n - k
        else:
            keep, shift = pos >= k, k
        sr = jnp.where(keep, pltpu.roll(xr, shift, 0), 0.0)
        si = jnp.where(keep, pltpu.roll(xi, shift, 0), 0.0)
        xr, xi = xr + cr * sr - ci * si, xi + cr * si + ci * sr
        cr, ci = cr * cr - ci * ci, 2.0 * cr * ci
        k *= 2
    return xr, xi, cr, ci


def _pick_row(x, r):
    rows = lax.broadcasted_iota(jnp.int32, x.shape, 0)
    return jnp.sum(jnp.where(rows == r, x, 0.0), axis=0, keepdims=True)


def _scan_tables(ar, ai, reverse):
    rows = lax.broadcasted_iota(jnp.int32, (S5_SUB, S5_LANES), 0)
    at = rows == (S5_SUB - 1 if reverse else 0)
    p8r, p8i, a8r, a8i = _dscan(jnp.where(at, ar, 0.0), jnp.where(at, ai, 0.0), ar, ai, S5_SUB, reverse)
    grp = lax.broadcasted_iota(jnp.int32, (S5_GROUPS, S5_LANES), 0)
    at = grp == (S5_GROUPS - 1 if reverse else 0)
    pgr, pgi, _, _ = _dscan(jnp.where(at, a8r, 0.0), jnp.where(at, a8i, 0.0), a8r, a8i, S5_GROUPS, reverse)
    return p8r, p8i, pgr, pgi


def _block_scan(br, bi, ar, ai, carry, tables, reverse, work):
    p8r, p8i, pgr, pgi = tables
    cin_r, cin_i = carry
    xw_r, xw_i, ew_r, ew_i = work
    xr, xi, a8r, a8i = _dscan(br, bi, ar, ai, S5_SUB, reverse)
    xw_r[...] = xr
    xw_i[...] = xi
    end = 0 if reverse else S5_SUB - 1
    for g in range(S5_GROUPS):
        ew_r[g:g + 1, :] = xw_r[S5_SUB * g + end:S5_SUB * g + end + 1, :]
        ew_i[g:g + 1, :] = xw_i[S5_SUB * g + end:S5_SUB * g + end + 1, :]
    er, ei, _, _ = _dscan(ew_r[...], ew_i[...], a8r, a8i, S5_GROUPS, reverse)
    er, ei = er + pgr * cin_r - pgi * cin_i, ei + pgr * cin_i + pgi * cin_r
    grp = lax.broadcasted_iota(jnp.int32, (S5_GROUPS, S5_LANES), 0)
    if reverse:
        ew_r[...] = jnp.where(grp == S5_GROUPS - 1, cin_r, pltpu.roll(er, S5_GROUPS - 1, 0))
        ew_i[...] = jnp.where(grp == S5_GROUPS - 1, cin_i, pltpu.roll(ei, S5_GROUPS - 1, 0))
    else:
        ew_r[...] = jnp.where(grp == 0, cin_r, pltpu.roll(er, 1, 0))
        ew_i[...] = jnp.where(grp == 0, cin_i, pltpu.roll(ei, 1, 0))
    for g in range(S5_GROUPS):
        rows = slice(S5_SUB * g, S5_SUB * (g + 1))
        nr, ni = ew_r[g:g + 1, :], ew_i[g:g + 1, :]
        xw_r[rows, :] += p8r * nr - p8i * ni
        xw_i[rows, :] += p8r * ni + p8i * nr
    last = 0 if reverse else S5_GROUPS - 1
    return xw_r[...], xw_i[...], (_pick_row(er, last), _pick_row(ei, last))


def _s5_tile_specs(n_t, rev):
    t_of = (lambda i: n_t - 1 - i) if rev else (lambda i: i)
    u_spec = pl.BlockSpec((S5_T, 128), lambda j, i: (t_of(i), j))
    x_spec = pl.BlockSpec((S5_T, S5_LANES), lambda j, i: (t_of(i), j))
    b_spec = pl.BlockSpec((1, 128, S5_LANES), lambda j, i: (j, 0, 0))
    c_spec = pl.BlockSpec((1, S5_LANES, 128), lambda j, i: (j, 0, 0))
    a_spec = pl.BlockSpec((1, S5_LANES), lambda j, i: (0, j))
    return u_spec, x_spec, b_spec, c_spec, a_spec


def _s5_scratch(with_rows):
    return ([pltpu.VMEM((1, S5_LANES), F32)] * 2 + [pltpu.VMEM((S5_SUB, S5_LANES), F32)] * 2
            + [pltpu.VMEM((S5_GROUPS, S5_LANES), F32)] * 4 + ([pltpu.VMEM((S5_T, S5_LANES), F32)] * 2 if with_rows else []))


def _s5_scan_fwd(u, btr, bti, ctr, cti, abr, abi, exchange=None):
    s_len = u.shape[0]
    n_t = s_len // S5_T

    def body(u_ref, btr_ref, bti_ref, ctr_ref, cti_ref, ar_ref, ai_ref, xr_ref, xi_ref, y_ref, car, cai, p8r, p8i, pgr, pgi, ew_r, ew_i):
        ar, ai = ar_ref[...], ai_ref[...]

        @pl.when(pl.program_id(1) == 0)
        def _():
            car[...] = jnp.zeros_like(car)
            cai[...] = jnp.zeros_like(cai)
            p8r[...], p8i[...], pgr[...], pgi[...] = _scan_tables(ar, ai, False)

        ub = u_ref[...]
        xr, xi, (ncr, nci) = _block_scan(_bdot(ub, btr_ref[0]), _bdot(ub, bti_ref[0]), ar, ai, (car[...], cai[...]),
                                         (p8r[...], p8i[...], pgr[...], pgi[...]), False, (xr_ref, xi_ref, ew_r, ew_i))
        car[...] = ncr
        cai[...] = nci
        y_ref[...] = _bdot(xr, ctr_ref[0]) - _bdot(xi, cti_ref[0])

    u_spec, x_spec, b_spec, c_spec, a_spec = _s5_tile_specs(n_t, False)
    return _call_with_exchange(
        body, "s5_scan_fwd", (S5_TILES, n_t),
        [u_spec, b_spec, b_spec, c_spec, c_spec, a_spec, a_spec], [x_spec, x_spec, u_spec],
        [_sds(s_len, S5_TILES * S5_LANES)] * 2 + [_sds(s_len, B_WIDTH)], _s5_scratch(False),
        (u, btr, bti, ctr, cti, abr, abi), exchange)


def _s5_scan_bwd(dy, xr, xi, u, btr, bti, ctr, cti, abr, abi, exchange=None):
    s_len = u.shape[0]
    n_t = s_len // S5_T

    def body(dy_ref, xr_ref, xi_ref, xrp_ref, xip_ref, u_ref, btr_ref, bti_ref, ctr_ref, cti_ref, ar_ref, ai_ref,
             du_ref, dbtr_ref, dbti_ref, dctr_ref, dcti_ref, dar_ref, dai_ref, car, cai, p8r, p8i, pgr, pgi, ew_r, ew_i, xw_r, xw_i):
        ar, ai = ar_ref[...], ai_ref[...]
        i = pl.program_id(1)
        rows = lax.broadcasted_iota(jnp.int32, (S5_T, S5_LANES), 0)

        @pl.when(i == 0)
        def _():
            for r in (car, cai, dbtr_ref, dbti_ref, dctr_ref, dcti_ref, dar_ref, dai_ref):
                r[...] = jnp.zeros_like(r)
            p8r[...], p8i[...], pgr[...], pgi[...] = _scan_tables(ar, -ai, True)

        dyb = dy_ref[...]
        xr_b, xi_b, ub = xr_ref[...], xi_ref[...], u_ref[...]
        dctr_ref[0] += _bdot_tn(xr_b, dyb)
        dcti_ref[0] -= _bdot_tn(xi_b, dyb)
        gr, gi, (ncr, nci) = _block_scan(_bdot_nt(dyb, ctr_ref[0]), -_bdot_nt(dyb, cti_ref[0]), ar, -ai, (car[...], cai[...]),
                                         (p8r[...], p8i[...], pgr[...], pgi[...]), True, (xw_r, xw_i, ew_r, ew_i))
        car[...] = ncr
        cai[...] = nci
        du_ref[...] = _bdot_nt(gr, btr_ref[0]) + _bdot_nt(gi, bti_ref[0])
        dbtr_ref[0] += _bdot_tn(ub, gr)
        dbti_ref[0] += _bdot_tn(ub, gi)
        has_prev = (i < n_t - 1).astype(F32)
        hr = _pick_row(xrp_ref[...], 7) * has_prev
        hi = _pick_row(xip_ref[...], 7) * has_prev
        xpr = jnp.where(rows == 0, hr, pltpu.roll(xr_b, 1, 0))
        xpi = jnp.where(rows == 0, hi, pltpu.roll(xi_b, 1, 0))
        dar_ref[...] += jnp.sum(gr * xpr + gi * xpi, axis=0, keepdims=True)
        dai_ref[...] += jnp.sum(gi * xpr - gr * xpi, axis=0, keepdims=True)

    u_spec, x_spec, b_spec, c_spec, a_spec = _s5_tile_specs(n_t, True)
    halo = pl.BlockSpec((8, S5_LANES), lambda j, i: (jnp.maximum((n_t - 1 - i) * (S5_T // 8) - 1, 0), j))
    return _call_with_exchange(
        body, "s5_scan_bwd", (S5_TILES, n_t),
        [u_spec, x_spec, x_spec, halo, halo, u_spec, b_spec, b_spec, c_spec, c_spec, a_spec, a_spec],
        [u_spec, b_spec, b_spec, c_spec, c_spec, a_spec, a_spec],
        [_sds(s_len, B_WIDTH)] + [_sds(S5_TILES, 128, S5_LANES)] * 2 + [_sds(S5_TILES, S5_LANES, 128)] * 2
        + [_sds(1, S5_TILES * S5_LANES)] * 2,
        _s5_scratch(True), (dy, xr, xi, xr, xi, u, btr, bti, ctr, cti, abr, abi), exchange)


def _blockdiag_b(bbar_t):
    blocks = bbar_t.reshape(S5_TILES, 8, B_GROUP, B_STATE)
    return jnp.einsum('jgmp,gh->jgmhp', blocks, jnp.eye(8, dtype=F32)).reshape(S5_TILES, 128, S5_LANES)


def _blockdiag_b_t(d):
    return jnp.einsum('jgmgp->jgmp', d.reshape(S5_TILES, 8, B_GROUP, 8, B_STATE)).reshape(B_WIDTH, B_STATE)


def _blockdiag_c(c):
    blocks = c.reshape(S5_TILES, 8, B_GROUP, B_STATE)
    return jnp.einsum('jgmp,gh->jhpgm', blocks, jnp.eye(8, dtype=F32)).reshape(S5_TILES, S5_LANES, 128)


def _blockdiag_c_t(d):
    return jnp.einsum('jgpgm->jgmp', d.reshape(S5_TILES, 8, B_STATE, 8, B_GROUP)).reshape(B_GROUPS, B_GROUP, B_STATE)


def _l0_out(os, ls, ga, gb, ypre, u, x, d_skip, glu_w, glu_b, w_out, post_g, gate):
    s_len = x.shape[0]

    def body(o0, o1, o2, l0, l1, l2, ga_ref, gb_ref, yp_ref, u_ref, x_ref, d_ref, gw_ref, gbias_ref, w_ref, pg_ref, gt_ref, x1_ref, y_ref):
        oa = _merge_gate(*[_from_res(r[...]) for r in (o0, o1, o2, l0, l1, l2)], ga_ref[...])
        yb = _s5_gelu(yp_ref[...], u_ref[...], d_ref[...])
        ob = _s5_glu(yb, _bdot(yb, gw_ref[...]) + gbias_ref[...], gb_ref[...])
        y = _bdot(oa, w_ref[0:512, :]) + _bdot(ob, w_ref[512:1024, :])
        y_ref[...] = y
        x1_ref[...] = _post_res(y, x_ref[...], pg_ref[...], gt_ref[...])

    vec, half = _fix((1, D_MODEL)), _fix((1, 512))
    return pl.pallas_call(
        body, name="l0_out", grid=(s_len // TM,),
        in_specs=[_res_spec(d) for d in DILATIONS] * 2 + [_row(TM, 512)] * 4
        + [_row(TM, D_MODEL), half, _fix((512, 512)), half, _fix((D_MODEL, D_MODEL)), vec, vec],
        out_specs=[_row(TM, D_MODEL)] * 2,
        out_shape=[_sds(s_len, D_MODEL)] * 2,
        compiler_params=_cp("arbitrary"),
    )(*os, *ls, ga, gb, ypre, u, x, d_skip, glu_w, glu_b, w_out, post_g, gate)


def _l0_out_bwd(os, ls, ga, gb, ypre, u, x, y, d_skip, glu_w, glu_b, w_out, post_g, gate, dx1, exchange=None):
    s_len = x.shape[0]

    def body(o0, o1, o2, l0, l1, l2, ga_ref, gb_ref, yp_ref, u_ref, x_ref, y_ref, d_ref, gw_ref, gbias_ref, w_ref, pg_ref, gt_ref, dx1_ref,
             do0, do1, do2, dl0, dl1, dl2, dga_ref, dgb_ref, dyp_ref, du_ref, dd_ref, dgw_ref, dgbias_ref, dw_ref, dpg_ref, dgt_ref):
        _zero_at_first([dd_ref, dgw_ref, dgbias_ref, dw_ref, dpg_ref, dgt_ref])
        _, vjp2 = jax.vjp(_post_res, y_ref[...], x_ref[...], pg_ref[...], gt_ref[...])
        dy, _, dpg, dgt = vjp2(dx1_ref[...])
        _acc(dpg_ref, dpg)
        _acc(dgt_ref, dgt)
        oa, vjp_a = jax.vjp(_merge_gate, *[_from_res(r[...]) for r in (o0, o1, o2, l0, l1, l2)], ga_ref[...])
        yb, vjp_g = jax.vjp(_s5_gelu, yp_ref[...], u_ref[...], d_ref[...])
        gl = _bdot(yb, gw_ref[...]) + gbias_ref[...]
        ob, vjp_b = jax.vjp(_s5_glu, yb, gl, gb_ref[...])
        dw_ref[0:512, :] += _bdot_tn(oa, dy)
        dw_ref[512:1024, :] += _bdot_tn(ob, dy)
        d1, d2, d3, e1, e2, e3, dga = vjp_a(_bdot_nt(dy, w_ref[0:512, :]))
        for ref, val, d in zip((do0, do1, do2, dl0, dl1, dl2), (d1, d2, d3, e1, e2, e3), DILATIONS * 2):
            ref[...] = _to_res(val, d)
        dga_ref[...] = dga
        dyb, dgl, dgb = vjp_b(_bdot_nt(dy, w_ref[512:1024, :]))
        dgb_ref[...] = dgb
        dgw_ref[...] += _bdot_tn(yb, dgl)
        _acc(dgbias_ref, jnp.sum(dgl, axis=0, keepdims=True))
        dyp, du, dd = vjp_g(dyb + _bdot_nt(dgl, gw_ref[...]))
        dyp_ref[...] = dyp
        du_ref[...] = du
        _acc(dd_ref, dd)

    vec, half = _fix((1, D_MODEL)), _fix((1, 512))
    r5, r10 = _row(TM, 512), _row(TM, D_MODEL)
    res6 = [_res_spec(d) for d in DILATIONS] * 2
    return _call_with_exchange(
        body, "l0_out_bwd", (s_len // TM,),
        res6 + [r5] * 4 + [r10, r10, half, _fix((512, 512)), half, _fix((D_MODEL, D_MODEL)), vec, vec, r10],
        res6 + [r5] * 4 + [half, _fix((512, 512)), half, _fix((D_MODEL, D_MODEL)), vec, vec],
        [_sds(*_res_shape(s_len, d)) for d in DILATIONS] * 2 + [_sds(s_len, 512)] * 4
        + [_sds(1, 512), _sds(512, 512), _sds(1, 512), _sds(D_MODEL, D_MODEL), _sds(1, D_MODEL), _sds(1, D_MODEL)],
        [], (*os, *ls, ga, gb, ypre, u, x, y, d_skip, glu_w, glu_b, w_out, post_g, gate, dx1), exchange)


def _l1_front(x, pre_g, scale, shift, w_in):
    s_len = x.shape[0]

    def body(x_ref, g_ref, sc_ref, sh_ref, w_ref, raw_ref, gate_ref, ba_ref, h_ref):
        hb = _pre_mod(x_ref[...], g_ref[...], sc_ref[...], sh_ref[...]).astype(BF)
        h_ref[...] = hb
        z = jnp.dot(hb, w_ref[...], preferred_element_type=F32)
        raw_ref[...] = z[:, 0:QKV]
        gate_ref[...] = z[:, QKV:QKV + 1024]
        ba_ref[...] = z[:, QKV + 1024:C_IN_PAD]

    vec = _fix((1, D_MODEL))
    return pl.pallas_call(
        body, name="l1_front", grid=(s_len // TM,),
        in_specs=[_row(TM, D_MODEL), vec, vec, vec, _fix((D_MODEL, C_IN_PAD))],
        out_specs=[_row(TM, QKV), _row(TM, 1024), _row(TM, 128), _row(TM, D_MODEL)],
        out_shape=[_sds(s_len, QKV), _sds(s_len, 1024), _sds(s_len, 128), _sds(s_len, D_MODEL, dtype=BF)],
        compiler_params=_cp("arbitrary"),
    )(x, pre_g, scale, shift, w_in)


def _bg_fn(ba, alog_row, dtb_row):
    lane = lax.broadcasted_iota(jnp.int32, (1, 128), 1)
    g = -jnp.exp(alog_row) * jax.nn.softplus(ba + dtb_row)
    return jnp.where(lane < C_HEADS, jax.nn.sigmoid(ba), jnp.where(lane < 2 * C_HEADS, g, 0.0))


def _act_q(c):
    q = jax.nn.silu(c)
    return q * lax.rsqrt(jnp.sum(q * q, axis=-1, keepdims=True) + EPS) * (C_DK ** -0.5)


def _act_k(c):
    k = jax.nn.silu(c)
    return k * lax.rsqrt(jnp.sum(k * k, axis=-1, keepdims=True) + EPS)


def _act_of(s):
    return _act_q if s < 8 else (_act_k if s < 16 else jax.nn.silu)


def _gdn_prep(raw, ba, conv_w, alog_row, dtb_row):
    s_len = raw.shape[0]

    def body(raw_ref, halo_ref, ba_ref, w_ref, al_ref, dt_ref, qkv_ref, bg_ref):
        bg_ref[...] = _bg_fn(ba_ref[...], al_ref[...], dt_ref[...])
        has_prev = (pl.program_id(0) > 0).astype(F32)
        for s in range(24):
            sl = slice(s * 128, (s + 1) * 128)
            cat = jnp.concatenate([halo_ref[:, sl] * has_prev, raw_ref[:, sl]], axis=0)
            conv = w_ref[3:4, sl] * cat[8:]
            for j in range(3):
                conv = conv + w_ref[j:j + 1, sl] * pltpu.roll(cat, 3 - j, 0)[8:]
            qkv_ref[:, sl] = _act_of(s)(conv)

    halo = pl.BlockSpec((8, QKV), lambda i: (jnp.maximum(i * (TM // 8) - 1, 0), 0))
    row128 = _fix((1, 128))
    return pl.pallas_call(
        body, name="gdn_prep", grid=(s_len // TM,),
        in_specs=[_row(TM, QKV), halo, _row(TM, 128), _fix((C_CONV, QKV)), row128, row128],
        out_specs=[_row(TM, QKV), _row(TM, 128)],
        out_shape=[_sds(s_len, QKV), _sds(s_len, 128)],
        compiler_params=_cp("arbitrary"),
    )(raw, raw, ba, conv_w, alog_row, dtb_row)


def _gdn_prep_bwd(raw, ba, conv_w, alog_row, dtb_row, dq, dk, dv, dbg):
    s_len = raw.shape[0]
    n_tiles = s_len // TM
    ext = TM + 8

    def body(raw_ref, prev_ref, next_ref, ba_ref, w_ref, al_ref, dt_ref, dq_ref, dqn_ref, dk_ref, dkn_ref, dv_ref, dvn_ref, dbg_ref,
             draw_ref, dba_ref, dw_ref, dal_ref, ddt_ref):
        _zero_at_first([dw_ref, dal_ref, ddt_ref])
        i = pl.program_id(0)
        _, vjp_bg = jax.vjp(_bg_fn, ba_ref[...], al_ref[...], dt_ref[...])
        dba, dal, ddt = vjp_bg(dbg_ref[...])
        dba_ref[...] = dba
        _acc(dal_ref, dal)
        _acc(ddt_ref, ddt)
        has_prev = (i > 0).astype(F32)
        has_next = (i < n_tiles - 1).astype(F32)
        ct_refs = ((dq_ref, dqn_ref), (dk_ref, dkn_ref), (dv_ref, dvn_ref))
        for s in range(24):
            sl = slice(s * 128, (s + 1) * 128)
            hl = slice((s % 8) * 128, (s % 8 + 1) * 128)
            tile_ref, nxt_ref = ct_refs[s // 8]
            cat = jnp.concatenate([prev_ref[:, sl] * has_prev, raw_ref[:, sl], next_ref[:, sl] * has_next], axis=0)
            shifted = [pltpu.roll(cat, 3 - j, 0)[8:] for j in range(3)] + [cat[8:]]
            conv = w_ref[3:4, sl] * shifted[3]
            for j in range(3):
                conv = conv + w_ref[j:j + 1, sl] * shifted[j]
            ct = jnp.concatenate([tile_ref[:, hl], nxt_ref[:, hl] * has_next], axis=0)
            _, vjp_act = jax.vjp(_act_of(s), conv)
            dconv, = vjp_act(ct)
            draw = w_ref[3:4, sl] * dconv[:TM]
            for j in range(3):
                draw = draw + w_ref[j:j + 1, sl] * pltpu.roll(dconv, ext - (3 - j), 0)[:TM]
            draw_ref[:, sl] = draw
            for j in range(4):
                dw_ref[j:j + 1, sl] += jnp.sum(dconv[:TM] * shifted[j][:TM], axis=0, keepdims=True)

    prev = pl.BlockSpec((8, QKV), lambda i: (jnp.maximum(i * (TM // 8) - 1, 0), 0))
    nxt = lambda n: pl.BlockSpec((8, n), lambda i: (jnp.minimum((i + 1) * (TM // 8), s_len // 8 - 1), 0))
    row128 = _fix((1, 128))
    ct_specs = [_row(TM, 1024), nxt(1024)] * 3
    return pl.pallas_call(
        body, name="gdn_prep_bwd", grid=(n_tiles,),
        in_specs=[_row(TM, QKV), prev, nxt(QKV), _row(TM, 128), _fix((C_CONV, QKV)), row128, row128] + ct_specs + [_row(TM, 128)],
        out_specs=[_row(TM, QKV), _row(TM, 128), _fix((C_CONV, QKV)), row128, row128],
        out_shape=[_sds(s_len, QKV), _sds(s_len, 128), _sds(C_CONV, QKV), _sds(1, 128), _sds(1, 128)],
        compiler_params=_cp("arbitrary"),
    )(raw, raw, raw, ba, conv_w, alog_row, dtb_row, dq, dq, dk, dk, dv, dv, dbg)


def _tein(eq, a, b):
    return jnp.einsum(eq, a, b, precision=lax.Precision.HIGH, preferred_element_type=F32)


def _unit_lower_inverse(lower):
    ri = lax.broadcasted_iota(jnp.int32, (C_CHUNK, C_CHUNK), 0)
    ci = lax.broadcasted_iota(jnp.int32, (C_CHUNK, C_CHUNK), 1)
    p_mat = -lower
    inv = (ri == ci).astype(F32)[None] + p_mat
    for _ in range(5):
        p_mat = _tein('hij,hjk->hik', p_mat, p_mat)
        inv = inv + _tein('hij,hjk->hik', inv, p_mat)
    return inv


@jax.custom_vjp
def _known_inverse(lower, inv):
    return inv


def _known_inverse_fwd(lower, inv):
    return inv, inv


def _known_inverse_bwd(inv, d_inv):
    d_lower = -_tein('hik,hjk->hij', _tein('hji,hjk->hik', inv, d_inv), inv)
    return d_lower, jnp.zeros_like(inv)


_known_inverse.defvjp(_known_inverse_fwd, _known_inverse_bwd)


def _gdn_chunk(q, k, v, bg, state, inv_known=None):
    lane = lax.broadcasted_iota(jnp.int32, (1, 128), 1)
    ri = lax.broadcasted_iota(jnp.int32, (C_CHUNK, C_CHUNK), 0)
    ci = lax.broadcasted_iota(jnp.int32, (C_CHUNK, C_CHUNK), 1)
    gc_t = _hdot((ri >= ci).astype(F32), bg)
    beta = jnp.stack([jnp.sum(jnp.where(lane == h, bg, 0.0), axis=-1, keepdims=True) for h in range(C_HEADS)], axis=0)
    gc = jnp.stack([jnp.sum(jnp.where(lane == C_HEADS + h, gc_t, 0.0), axis=-1, keepdims=True) for h in range(C_HEADS)], axis=0)
    gc_rows = gc_t.T
    row_id = lax.broadcasted_iota(jnp.int32, (128, C_CHUNK), 0)
    gcj = jnp.stack([jnp.sum(jnp.where(row_id == C_HEADS + h, gc_rows, 0.0), axis=0, keepdims=True) for h in range(C_HEADS)], axis=0)
    tril, strict = (ri >= ci)[None], (ri > ci)[None]
    decay = jnp.exp(jnp.where(tril, gc - gcj, -1e30))
    kb = k * beta
    lower = jnp.where(strict, _bein('hid,hjd->hij', kb, k) * decay, 0.0)
    inv = _unit_lower_inverse(lower) if inv_known is None else _known_inverse(lower, inv_known)
    egc = jnp.exp(gc)
    u_c = _tein('hij,hjd->hid', inv, v * beta)
    w_c = _tein('hij,hjd->hid', inv, kb * egc)
    aqk = _bein('hid,hjd->hij', q, k) * decay
    rowi = lax.broadcasted_iota(jnp.int32, (1, C_CHUNK, 1), 1)
    g_last = jnp.sum(jnp.where(rowi == C_CHUNK - 1, gc, 0.0), axis=1, keepdims=True)
    kd = k * jnp.exp(g_last - gc)
    v_new = u_c - _bein('hik,hkv->hiv', w_c, state)
    o = _bein('hik,hkv->hiv', q * egc, state) + _bein('hij,hjv->hiv', aqk, v_new)
    return o, state * jnp.exp(g_last) + _bein('hik,hiv->hkv', kd, v_new), inv


def _heads(ref):
    return jnp.stack([ref[:, h * C_DK:(h + 1) * C_DK] for h in range(C_HEADS)], axis=0)


def _gdn_fwd(qkv, bg):
    s_len = qkv.shape[0]
    n_c = s_len // C_CHUNK

    def body(q_ref, k_ref, v_ref, bg_ref, o_ref, ss_ref, inv_ref, st_ref):
        _zero_at_first([st_ref])
        s0 = st_ref[...]
        ss_ref[0] = s0
        o, s2, inv = _gdn_chunk(_heads(q_ref), _heads(k_ref), _heads(v_ref), bg_ref[...], s0)
        st_ref[...] = s2
        inv_ref[0] = inv
        for h in range(C_HEADS):
            o_ref[:, h * C_DK:(h + 1) * C_DK] = o[h]

    col = lambda c: pl.BlockSpec((C_CHUNK, 1024), lambda i: (i, c))
    return pl.pallas_call(
        body, name="gdn_fwd", grid=(n_c,),
        in_specs=[col(0), col(1), col(2), _row(C_CHUNK, 128)],
        out_specs=[_row(C_CHUNK, 1024), pl.BlockSpec((1, C_HEADS, C_DK, C_DK), lambda i: (i, 0, 0, 0)),
                   pl.BlockSpec((1, C_HEADS, C_CHUNK, C_CHUNK), lambda i: (i, 0, 0, 0))],
        out_shape=[_sds(s_len, 1024), _sds(n_c, C_HEADS, C_DK, C_DK), _sds(n_c, C_HEADS, C_CHUNK, C_CHUNK)],
        scratch_shapes=[pltpu.VMEM((C_HEADS, C_DK, C_DK), F32)],
        compiler_params=_cp("arbitrary"),
    )(qkv, qkv, qkv, bg)


def _gdn_bwd(qkv, bg, states, invs, do):
    s_len = qkv.shape[0]
    n_c = s_len // C_CHUNK

    def body(q_ref, k_ref, v_ref, bg_ref, ss_ref, inv_ref, do_ref, dq_ref, dk_ref, dv_ref, dbg_ref, ds_ref):
        _zero_at_first([ds_ref])
        inv_known = inv_ref[0]
        chunk = lambda q, k, v, bg_, st: _gdn_chunk(q, k, v, bg_, st, inv_known)[:2]
        _, vjp = jax.vjp(chunk, _heads(q_ref), _heads(k_ref), _heads(v_ref), bg_ref[...], ss_ref[0])
        dq, dk, dv, dbg, ds = vjp((_heads(do_ref), ds_ref[...]))
        ds_ref[...] = ds
        dbg_ref[...] = dbg
        for h in range(C_HEADS):
            sl = slice(h * C_DK, (h + 1) * C_DK)
            dq_ref[:, sl], dk_ref[:, sl], dv_ref[:, sl] = dq[h], dk[h], dv[h]

    rev = lambda i: n_c - 1 - i
    col = lambda c: pl.BlockSpec((C_CHUNK, 1024), lambda i: (rev(i), c))
    row128 = pl.BlockSpec((C_CHUNK, 128), lambda i: (rev(i), 0))
    per_chunk = lambda n: pl.BlockSpec((1, C_HEADS, n, n), lambda i: (rev(i), 0, 0, 0))
    return pl.pallas_call(
        body, name="gdn_bwd", grid=(n_c,),
        in_specs=[col(0), col(1), col(2), row128, per_chunk(C_DK), per_chunk(C_CHUNK), col(0)],
        out_specs=[col(0), col(0), col(0), row128],
        out_shape=[_sds(s_len, 1024)] * 3 + [_sds(s_len, 128)],
        scratch_shapes=[pltpu.VMEM((C_HEADS, C_DK, C_DK), F32)],
        compiler_params=_cp("arbitrary"),
    )(qkv, qkv, qkv, bg, states, invs, do)


def _head_norm_gate(o, gate, norm_g):
    return (_rms(o) * norm_g) * jax.nn.silu(gate)


def _l1_out_fb(o, gate_c, x1, target, norm_g, w_out, post_g, gate):
    s_len = x1.shape[0]

    def body(o_ref, gc_ref, x1_ref, t_ref, ng_ref, w_ref, pg_ref, gt_ref,
             loss_ref, dres_ref, do_ref, dgc_ref, dw_ref, dng_ref, dpg_ref, dgt_ref):
        _zero_at_first([loss_ref, dw_ref, dng_ref, dpg_ref, dgt_ref])
        ng = ng_ref[...]
        ons, vjps = [], []
        for h in range(C_HEADS):
            sl = slice(h * C_DK, (h + 1) * C_DK)
            on, vjp_h = jax.vjp(_head_norm_gate, o_ref[:, sl], gc_ref[:, sl], ng)
            ons.append(on)
            vjps.append(vjp_h)
        on_all = jnp.concatenate(ons, axis=-1)
        y = _bdot(on_all, w_ref[...])
        x2, vjp2 = jax.vjp(_post_res, y, x1_ref[...], pg_ref[...], gt_ref[...])
        err = x2 - t_ref[...]
        _acc(loss_ref, jnp.full((1, 128), 0.5 * jnp.sum(jnp.mean(err * err, axis=-1)), F32))
        dx2 = err * (1.0 / D_MODEL)
        dy, _, dpg, dgt = vjp2(dx2)
        dres_ref[...] = dx2
        _acc(dpg_ref, dpg)
        _acc(dgt_ref, dgt)
        dw_ref[...] += _bdot_tn(on_all, dy)
        don = _bdot_nt(dy, w_ref[...])
        for h in range(C_HEADS):
            sl = slice(h * C_DK, (h + 1) * C_DK)
            do_h, dgc_h, dng = vjps[h](don[:, sl])
            do_ref[:, sl] = do_h
            dgc_ref[:, sl] = dgc_h
            _acc(dng_ref, dng)

    vec, r10 = _fix((1, D_MODEL)), _row(TM, D_MODEL)
    row128 = _fix((1, 128))
    return pl.pallas_call(
        body, name="l1_out_fb", grid=(s_len // TM,),
        in_specs=[r10, r10, r10, r10, row128, _fix((D_MODEL, D_MODEL)), vec, vec],
        out_specs=[row128, r10, r10, r10, _fix((D_MODEL, D_MODEL)), row128, vec, vec],
        out_shape=[_sds(1, 128), _sds(s_len, D_MODEL), _sds(s_len, D_MODEL), _sds(s_len, D_MODEL),
                   _sds(D_MODEL, D_MODEL), _sds(1, 128), _sds(1, D_MODEL), _sds(1, D_MODEL)],
        compiler_params=_cp("arbitrary"),
    )(o, gate_c, x1, target, norm_g, w_out, post_g, gate)


def _row_of(v, width, at):
    return jnp.zeros((1, width), F32).at[0, at:at + v.shape[-1]].set(v.reshape(-1))


def _local_step(x, target, mod, wd, comm=None):
    s_len = x.shape[0]
    shift0, scale0, gate0 = (mod[0:1, i * 1024:(i + 1) * 1024] for i in range(3))
    shift1, scale1, gate1 = (mod[1:2, i * 1024:(i + 1) * 1024] for i in range(3))
    pre_g0, pre_g1 = wd["pre_g"][0:1], wd["pre_g"][1:2]
    post_g0, post_g1 = wd["post_g"][0:1], wd["post_g"][1:2]
    w_in0 = wd["ab_w_in"].astype(BF)
    d_skip, glu_b = wd["s5_d"].reshape(1, 512), wd["s5_glu_b"].reshape(1, 512)
    norm_g = wd["gdn_norm_g"].reshape(1, 128)
    alog_row = _row_of(wd["gdn_a_log"], 128, C_HEADS)
    dtb_row = _row_of(wd["gdn_dt_bias"], 128, C_HEADS)
    conv_w = wd["gdn_conv"]

    a_re, a_im = wd["s5_a_re"], wd["s5_a_im"]
    log_dt = wd["s5_log_dt"].reshape(B_GROUPS, 1)
    bt_re = wd["s5_b_re"].transpose(0, 2, 1).reshape(B_WIDTH, B_STATE)
    bt_im = wd["s5_b_im"].transpose(0, 2, 1).reshape(B_WIDTH, B_STATE)
    abar_r, abar_i, bbar_r, bbar_i = _s5_params(a_re, a_im, log_dt, bt_re, bt_im)
    abr, abi = abar_r.reshape(1, -1), abar_i.reshape(1, -1)
    btr, bti = _blockdiag_b(bbar_r).astype(BF), _blockdiag_b(bbar_i).astype(BF)
    ctr, cti = _blockdiag_c(wd["s5_c_re"]).astype(BF), _blockdiag_c(wd["s5_c_im"]).astype(BF)

    table = _bucket_table()
    biases = _attn_bias(wd["rel_bias"], jnp.asarray(table))
    front = _l0_front(x, pre_g0, scale0, shift0, w_in0)
    qs, ks, vs = front[0:3], front[3:6], front[6:9]
    u, ga, gb, h0 = front[9:]
    os, ls = zip(*[_attn_fwd(qs[i], ks[i], vs[i], biases[i]) for i in range(3)])
    (xr, xi, ypre), late = _s5_scan_fwd(u, btr, bti, ctr, cti, abr, abi, exchange=None if comm is None else (comm.late_halves(), False))
    if comm is not None:
        wd = {**wd, **comm.late_weights(late)}
    w_out0 = wd["ab_w_out"].astype(BF)
    glu_w = wd["s5_glu_w"].astype(BF)
    w_in1 = jnp.concatenate([wd["gdn_w_in"], jnp.zeros((D_MODEL, C_IN_PAD - wd["gdn_w_in"].shape[1]), wd["gdn_w_in"].dtype)], axis=1).astype(BF)
    w_out1 = wd["gdn_w_out"].astype(BF)
    x1, y0 = _l0_out(os, ls, ga, gb, ypre, u, x, d_skip, glu_w, glu_b, w_out0, post_g0, gate0)

    raw, gate_c, ba, h1 = _l1_front(x1, pre_g1, scale1, shift1, w_in1)
    qkv, bg = _gdn_prep(raw, ba, conv_w, alog_row, dtb_row)
    o_gdn, states, invs = _gdn_fwd(qkv, bg)
    loss_row, dres1, do_gdn, dgate_c, dw_out1, dnorm_g, dpost_g1, dgate1 = _l1_out_fb(
        o_gdn, gate_c, x1, target, norm_g, w_out1, post_g1, gate1)

    dq1, dk1, dv1, dbg = _gdn_bwd(qkv, bg, states, invs, do_gdn)
    draw, dba, dconv_w, dalog_row, ddtb_row = _gdn_prep_bwd(raw, ba, conv_w, alog_row, dtb_row, dq1, dk1, dv1, dbg)
    dz1, dx1, dpre_g1, dscale1, dshift1 = _front_bwd(
        "l1_front_bwd", x1, pre_g1, scale1, shift1, w_in1, dres1, [[draw], [dgate_c], [dba]], [QKV, 1024, 128])
    dw_in1 = _matmul_tn("l1_dw_in", h1, dz1, 1408)

    n_w1 = wd["gdn_w_in"].shape[1]
    ex1 = None if comm is None else (comm.chip_partials("l1", {"gdn_w_in": dw_in1[:, :n_w1], "gdn_w_out": dw_out1}), True)
    l0b, got1 = _l0_out_bwd(os, ls, ga, gb, ypre, u, x, y0, d_skip, glu_w, glu_b, w_out0, post_g0, gate0, dx1, exchange=ex1)
    dos, dls = l0b[0:3], l0b[3:6]
    dga, dgb, dypre, du_skip, dd_skip, dglu_w, dglu_b, dw_out0, dpost_g0, dgate0 = l0b[6:]
    ex2 = None if comm is None else (comm.chip_partials("l0_out", {"ab_w_out": dw_out0, "s5_glu_w": dglu_w}), True)
    (du_scan, dbtr, dbti, dctr, dcti, dabr, dabi), got2 = _s5_scan_bwd(dypre, xr, xi, u, btr, bti, ctr, cti, abr, abi, exchange=ex2)
    if comm is not None:
        comm.received.update(zip(("gdn_w_in", "gdn_w_out", "ab_w_out", "s5_glu_w"), list(got1) + list(got2)))
    dqs, dks, dvs, dbs = [], [], [], []
    for i in range(3):
        dq_d, dk_d, dv_d, db_d = _attn_bwd(qs[i], ks[i], vs[i], biases[i], os[i], ls[i], dos[i], dls[i])
        dqs.append(dq_d)
        dks.append(dk_d)
        dvs.append(dv_d)
        dbs.append(db_d)
    parts = [dqs, dks, dvs, [du_skip, du_scan], [dga], [dgb]]
    dz0, grad_x, dpre_g0, dscale0, dshift0 = _front_bwd(
        "l0_front_bwd", x, pre_g0, scale0, shift0, w_in0, dx1, parts, [512] * 6)
    dw_in0 = _matmul_tn("l0_dw_in", h0, dz0, 768)

    idx_rows = jnp.asarray(table.reshape(3, -1), F32)
    drel = _rel_bias_grad(dbs, idx_rows).T
    da_re, da_im, dlog_dt, dbt_re, dbt_im = _s5_params_bwd(
        a_re, a_im, log_dt, bt_re, bt_im, dabr.reshape(B_GROUPS, B_STATE), dabi.reshape(B_GROUPS, B_STATE),
        _blockdiag_b_t(dbtr), _blockdiag_b_t(dbti))
    unb = lambda d: d.reshape(B_GROUPS, B_GROUP, B_STATE).transpose(0, 2, 1)
    grads = {
        "pre_g": jnp.concatenate([dpre_g0, dpre_g1], 0), "post_g": jnp.concatenate([dpost_g0, dpost_g1], 0),
        "rel_bias": drel, "ab_w_in": dw_in0, "ab_w_out": dw_out0,
        "s5_a_re": da_re, "s5_a_im": da_im, "s5_log_dt": dlog_dt.reshape(B_GROUPS),
        "s5_b_re": unb(dbt_re), "s5_b_im": unb(dbt_im),
        "s5_c_re": _blockdiag_c_t(dctr), "s5_c_im": _blockdiag_c_t(dcti),
        "s5_d": dd_skip.reshape(512), "s5_glu_w": dglu_w, "s5_glu_b": dglu_b.reshape(512),
        "gdn_w_in": dw_in1[:, :wd["gdn_w_in"].shape[1]], "gdn_conv": dconv_w,
        "gdn_a_log": dalog_row[0, C_HEADS:2 * C_HEADS], "gdn_dt_bias": ddtb_row[0, C_HEADS:2 * C_HEADS],
        "gdn_norm_g": dnorm_g.reshape(128), "gdn_w_out": dw_out1,
    }
    dmod = jnp.concatenate([jnp.concatenate([dshift0, dscale0, dgate0], 1), jnp.concatenate([dshift1, dscale1, dgate1], 1)], 0)
    return loss_row[0, 0], grad_x, grads, dmod


def _place():
    return lax.axis_index("x"), lax.axis_index("y"), lax.axis_index("c")


def _flip(v, bit):
    return 1 - v if bit else v


def _hbm_call(name, body, arrs, out_shapes, n_sem):
    any_spec = pl.BlockSpec(memory_space=pl.ANY)
    return pl.pallas_call(
        body, name=name,
        in_specs=[any_spec] * len(arrs), out_specs=[any_spec] * len(out_shapes), out_shape=out_shapes,
        scratch_shapes=[pltpu.SemaphoreType.DMA((n_sem,)), pltpu.SemaphoreType.DMA((n_sem,))],
    )(*arrs)


def _own_slot(gathered, own, slot):
    idx = lax.broadcasted_iota(jnp.int32, (gathered.shape[0],) + (1,) * own.ndim, 0)
    return jnp.where(idx == slot, own[None], gathered)


def _all_gather8(name, arr):
    def body(x_ref, out_ref, send_sems, recv_sems):
        x, y, c = _place()
        me = 4 * x + 2 * y + c
        sends, recvs = [], []
        for m in range(1, 8):
            peer = (_flip(x, m & 4), _flip(y, m & 2), _flip(c, m & 1))
            sends.append(pltpu.make_async_remote_copy(x_ref, out_ref.at[me], send_sems.at[m - 1], recv_sems.at[m - 1],
                                                      device_id=peer, device_id_type=MESH))
            recvs.append(pltpu.make_async_remote_copy(x_ref, out_ref.at[4 * peer[0] + 2 * peer[1] + peer[2]], send_sems.at[m - 1],
                                                      recv_sems.at[m - 1], device_id=peer, device_id_type=MESH))
        for cp in sends:
            cp.start()
        for cp in recvs:
            cp.wait_recv()
        for cp in sends:
            cp.wait_send()

    return _hbm_call(name, body, [arr], [jax.ShapeDtypeStruct((8,) + arr.shape, arr.dtype)], 7)[0]


def _all_to_all8(name, arr):
    def body(x_ref, out_ref, send_sems, recv_sems):
        x, y, c = _place()
        me = 4 * x + 2 * y + c
        sends, recvs = [], []
        for m in range(1, 8):
            peer = (_flip(x, m & 4), _flip(y, m & 2), _flip(c, m & 1))
            peer_id = 4 * peer[0] + 2 * peer[1] + peer[2]
            sends.append(pltpu.make_async_remote_copy(x_ref.at[peer_id], out_ref.at[me], send_sems.at[m - 1], recv_sems.at[m - 1],
                                                      device_id=peer, device_id_type=MESH))
            recvs.append(pltpu.make_async_remote_copy(x_ref.at[peer_id], out_ref.at[peer_id], send_sems.at[m - 1], recv_sems.at[m - 1],
                                                      device_id=peer, device_id_type=MESH))
        for cp in sends:
            cp.start()
        for cp in recvs:
            cp.wait_recv()
        for cp in sends:
            cp.wait_send()

    return _hbm_call(name, body, [arr], [jax.ShapeDtypeStruct(arr.shape, arr.dtype)], 7)[0]


def _chip_copies(ins, outs, send_sems, recv_sems, scatter):
    x, y, c = _place()
    mine = 2 * x + y
    sends, recvs = [], []
    for a in range(len(ins)):
        for m in range(1, 4):
            px, py = _flip(x, m & 2), _flip(y, m & 1)
            k = 3 * a + m - 1
            src = ins[a].at[2 * px + py] if scatter else ins[a]
            sends.append(pltpu.make_async_remote_copy(src, outs[a].at[mine], send_sems.at[k], recv_sems.at[k],
                                                      device_id=(px, py, c), device_id_type=MESH))
            recvs.append(pltpu.make_async_remote_copy(src, outs[a].at[2 * px + py], send_sems.at[k], recv_sems.at[k],
                                                      device_id=(px, py, c), device_id_type=MESH))
    return sends, recvs


def _chip_shapes(arrs, scatter):
    return [jax.ShapeDtypeStruct(a.shape if scatter else (4,) + a.shape, a.dtype) for a in arrs]


def _chip_exchange(name, arrs, scatter):
    n = len(arrs)

    def body(*refs):
        sends, recvs = _chip_copies(refs[:n], refs[n:2 * n], refs[2 * n], refs[2 * n + 1], scatter)
        for cp in sends:
            cp.start()
        for cp in recvs:
            cp.wait_recv()
        for cp in sends:
            cp.wait_send()

    return _hbm_call(name, body, arrs, _chip_shapes(arrs, scatter), 3 * n)


def _call_with_exchange(body, name, grid, in_specs, out_specs, out_shape, scratch_shapes, args, exchange):
    if exchange is None:
        return pl.pallas_call(body, name=name, grid=grid, in_specs=in_specs, out_specs=out_specs, out_shape=out_shape,
                              scratch_shapes=scratch_shapes, compiler_params=_cp(*["arbitrary"] * len(grid)))(*args), []
    arrs, scatter = exchange
    n_in, n_out, n_ex, n_scr = len(in_specs), len(out_specs), len(arrs), len(scratch_shapes)

    def fused(*refs):
        ins, ex_in = refs[:n_in], refs[n_in:n_in + n_ex]
        outs, ex_out = refs[n_in + n_ex:n_in + n_ex + n_out], refs[n_in + n_ex + n_out:n_in + 2 * n_ex + n_out]
        rest = refs[n_in + 2 * n_ex + n_out:]
        sends, recvs = _chip_copies(ex_in, ex_out, rest[n_scr], rest[n_scr + 1], scatter)
        first, last = pl.program_id(0) == 0, pl.program_id(0) == grid[0] - 1
        for k in range(1, len(grid)):
            first, last = first & (pl.program_id(k) == 0), last & (pl.program_id(k) == grid[k] - 1)

        @pl.when(first)
        def _():
            for cp in sends:
                cp.start()

        body(*ins, *outs, *rest[:n_scr])

        @pl.when(last)
        def _():
            for cp in recvs:
                cp.wait_recv()
            for cp in sends:
                cp.wait_send()

    any_spec = pl.BlockSpec(memory_space=pl.ANY)
    res = pl.pallas_call(
        fused, name=name, grid=grid, in_specs=list(in_specs) + [any_spec] * n_ex, out_specs=list(out_specs) + [any_spec] * n_ex,
        out_shape=list(out_shape) + _chip_shapes(arrs, scatter),
        scratch_shapes=list(scratch_shapes) + [pltpu.SemaphoreType.DMA((3 * n_ex,))] * 2,
        compiler_params=_cp(*["arbitrary"] * len(grid)))(*args, *arrs)
    return res[:n_out], res[n_out:]


def _sibling_exchange(name, arrs):
    n = len(arrs)

    def body(*refs):
        ins, outs = refs[:n], refs[n:2 * n]
        send_sems, recv_sems = refs[2 * n:]
        x, y, c = _place()
        copies = [pltpu.make_async_remote_copy(ins[a], outs[a], send_sems.at[a], recv_sems.at[a],
                                               device_id=(x, y, 1 - c), device_id_type=MESH) for a in range(n)]
        for cp in copies:
            cp.start()
        for cp in copies:
            cp.wait_recv()
        for cp in copies:
            cp.wait_send()

    return _hbm_call(name, body, arrs, [jax.ShapeDtypeStruct(a.shape, a.dtype) for a in arrs], n)


def _row_tile(rows):
    for t in (256, 128, 64, 32, 16, 8):
        if rows % t == 0:
            return t
    return rows


def _pair_sum(name, a, b, out_dtype):
    rows, cols = a.shape
    tr = _row_tile(rows)

    def body(a_ref, b_ref, o_ref):
        o_ref[...] = (a_ref[...] + b_ref[...]).astype(out_dtype)

    return pl.pallas_call(body, name=name, grid=(rows // tr,), in_specs=[_row(tr, cols)] * 2, out_specs=_row(tr, cols),
                          out_shape=_sds(rows, cols, dtype=out_dtype), compiler_params=_cp("arbitrary"))(a, b)


def _chip_sum(name, recv, partial, mine):
    n, rows, cols = recv.shape
    tr = _row_tile(rows)

    def body(mine_ref, *refs):
        own = refs[n][0].astype(F32)
        acc = None
        for s in range(n):
            term = jnp.where(mine_ref[0] == s, own, refs[s][0].astype(F32))
            acc = term if acc is None else acc + term
        refs[-1][...] = acc

    def slot_spec(s):
        return pl.BlockSpec((1, tr, cols), lambda i, m: (jnp.where(m[0] == s, (s + 1) % n, s), i, 0))

    grid_spec = pltpu.PrefetchScalarGridSpec(
        num_scalar_prefetch=1, grid=(rows // tr,),
        in_specs=[slot_spec(s) for s in range(n)] + [pl.BlockSpec((1, tr, cols), lambda i, m: (m[0], i, 0))],
        out_specs=pl.BlockSpec((tr, cols), lambda i, m: (i, 0)))
    return pl.pallas_call(body, name=name, grid_spec=grid_spec, out_shape=_sds(rows, cols),
                          compiler_params=_cp("arbitrary"))(mine, *([recv] * n), partial)


def _slot_sum(name, arr):
    n, rows, cols = arr.shape
    tr = _row_tile(rows)

    def body(*refs):
        acc = refs[0][0]
        for r in refs[1:-1]:
            acc = acc + r[0]
        refs[-1][...] = acc

    specs = [pl.BlockSpec((1, tr, cols), functools.partial(lambda s, i: (s, i, 0), s)) for s in range(n)]
    return pl.pallas_call(body, name=name, grid=(rows // tr,), in_specs=specs, out_specs=_row(tr, cols),
                          out_shape=_sds(rows, cols), compiler_params=_cp("arbitrary"))(*([arr] * n))


def _adamw(name, w, g, m, v):
    rows, cols = w.shape
    tr = _row_tile(rows)

    def body(w_ref, g_ref, m_ref, v_ref, d_ref, nm_ref, nv_ref):
        g_ = g_ref[...]
        m_ = ADAM_B1 * m_ref[...] + (1.0 - ADAM_B1) * g_
        v_ = ADAM_B2 * v_ref[...] + (1.0 - ADAM_B2) * (g_ * g_)
        m_hat = m_ / (1.0 - ADAM_B1 ** ADAM_STEP)
        v_hat = v_ / (1.0 - ADAM_B2 ** ADAM_STEP)
        d_ref[...] = -ADAM_LR * (m_hat / (jnp.sqrt(v_hat) + ADAM_EPS) + ADAM_WD * w_ref[...])
        nm_ref[...] = m_
        nv_ref[...] = v_

    spec = _row(tr, cols)
    return pl.pallas_call(body, name=name, grid=(rows // tr,), in_specs=[spec] * 4, out_specs=[spec] * 3,
                          out_shape=[_sds(rows, cols)] * 3, compiler_params=_cp("arbitrary"))(w, g, m, v)


def _adamw_halves(name, w, g_mine, g_sibling, m, v, core):
    rows, cols = w.shape
    half = rows // 2
    tr = _row_tile(half)
    per_half = half // tr

    def body(core_ref, w_ref, gm_ref, gs_ref, m_ref, v_ref, g_ref, d_ref, nm_ref, nv_ref):
        g_ = jnp.where(pl.program_id(0) // per_half == core_ref[0], gm_ref[...], gs_ref[...])
        m_ = ADAM_B1 * m_ref[...] + (1.0 - ADAM_B1) * g_
        v_ = ADAM_B2 * v_ref[...] + (1.0 - ADAM_B2) * (g_ * g_)
        m_hat = m_ / (1.0 - ADAM_B1 ** ADAM_STEP)
        v_hat = v_ / (1.0 - ADAM_B2 ** ADAM_STEP)
        g_ref[...] = g_
        d_ref[...] = -ADAM_LR * (m_hat / (jnp.sqrt(v_hat) + ADAM_EPS) + ADAM_WD * w_ref[...])
        nm_ref[...] = m_
        nv_ref[...] = v_

    full = pl.BlockSpec((tr, cols), lambda i, c: (i, 0))
    in_half = pl.BlockSpec((tr, cols), lambda i, c: (i % per_half, 0))
    grid_spec = pltpu.PrefetchScalarGridSpec(num_scalar_prefetch=1, grid=(rows // tr,),
                                             in_specs=[full, in_half, in_half, full, full], out_specs=[full] * 4)
    return pl.pallas_call(body, name=name, grid_spec=grid_spec, out_shape=[_sds(rows, cols)] * 4,
                          compiler_params=_cp("arbitrary"))(core, w, g_mine, g_sibling, m, v)


def _mod_local(c_all, ada_w):
    def body(c_ref, w_ref, o_ref):
        c_act = jax.nn.silu(c_ref[...])
        for l in range(2):
            o_ref[l] = _hdot(c_act, w_ref[l])

    return pl.pallas_call(body, name="mod_local", out_shape=_sds(2, 8, ada_w.shape[2]),
                          compiler_params=pltpu.CompilerParams(vmem_limit_bytes=VMEM_LIMIT_BYTES))(c_all, ada_w)


def _ada_w_grad(c_all, dmod_cols):
    def body(c_ref, d_ref, o_ref):
        c_act = jax.nn.silu(c_ref[...])
        for l in range(2):
            o_ref[l] = lax.dot_general(c_act, d_ref[l], (((0,), (0,)), ((), ())), precision=HI, preferred_element_type=F32)

    return pl.pallas_call(body, name="ada_w_grad", out_shape=_sds(2, D_MODEL, dmod_cols.shape[2]),
                          compiler_params=pltpu.CompilerParams(vmem_limit_bytes=VMEM_LIMIT_BYTES))(c_all, dmod_cols)


_SMALL = ("ada_b", "pre_g", "post_g", "rel_bias", "s5_a_re", "s5_a_im", "s5_log_dt", "s5_b_re", "s5_b_im", "s5_c_re", "s5_c_im",
          "s5_d", "s5_glu_b", "gdn_a_log", "gdn_dt_bias", "gdn_norm_g")
_SHARDED = ("ab_w_in", "ab_w_out", "s5_glu_w", "gdn_w_in", "gdn_w_out")
_COL_SHARDED = ("ab_w_in", "gdn_w_in")
_WEIGHTS = ("ada_w", "ada_b", "pre_g", "post_g", "rel_bias", "ab_w_in", "ab_w_out", "s5_a_re", "s5_a_im", "s5_log_dt", "s5_b_re",
            "s5_b_im", "s5_c_re", "s5_c_im", "s5_d", "s5_glu_w", "s5_glu_b", "gdn_w_in", "gdn_conv", "gdn_a_log", "gdn_dt_bias",
            "gdn_norm_g", "gdn_w_out")


def _rows128(n):
    return -(-n // 128)


def _pack(arrs, total_rows):
    pieces = []
    for a in arrs:
        flat = a.reshape(-1)
        pieces.append(jnp.pad(flat, (0, _rows128(flat.shape[0]) * 128 - flat.shape[0])).reshape(-1, 128))
    used = sum(p.shape[0] for p in pieces)
    pieces.append(jnp.zeros((total_rows - used, 128), F32))
    return jnp.concatenate(pieces, axis=0)


def _unpack(buf, shapes):
    out, at = [], 0
    for shp in shapes:
        n = int(np.prod(shp))
        out.append(buf[at:at + _rows128(n)].reshape(-1)[:n].reshape(shp))
        at += _rows128(n)
    return out


def _full_from_halves(name, g):
    if name in _COL_SHARDED:
        return g.transpose(0, 2, 1, 3).reshape(2 * g.shape[2], 4 * g.shape[3])
    return g.transpose(1, 0, 2, 3).reshape(8 * g.shape[2], g.shape[3])


def _shard_major(name, g):
    if name in _COL_SHARDED:
        return g.reshape(g.shape[0], 4, g.shape[1] // 4).transpose(1, 0, 2)
    return g.reshape(4, g.shape[0] // 4, g.shape[1])


_LATE = ("ab_w_out", "s5_glu_w", "gdn_w_in", "gdn_w_out")


class _WeightExchanges:
    def __init__(self, shards, core, chip):
        self.core, self.chip = core, chip
        self.half = {}
        for name, shard in shards.items():
            h = shard.shape[0] // 2
            self.half[name] = lax.dynamic_slice_in_dim(shard.astype(BF), core * h, h, axis=0)
        self.partial, self.received = {}, {}

    def _full(self, label, names, from_chips):
        mine = [_own_slot(g, self.half[n], self.chip) for n, g in zip(names, from_chips)]
        theirs = _sibling_exchange("gather_w_sibling_" + label, mine)
        return {n: _full_from_halves(n, jnp.where(self.core == 0, jnp.stack([a, b], 0), jnp.stack([b, a], 0)))
                for n, a, b in zip(names, mine, theirs)}

    def first_weights(self):
        return self._full("first", ["ab_w_in"], _chip_exchange("gather_w_chips", [self.half["ab_w_in"]], False))

    def late_halves(self):
        return [self.half[n] for n in _LATE]

    def late_weights(self, from_chips):
        return self._full("late", list(_LATE), from_chips)

    def chip_partials(self, label, grads):
        mine, other = [], []
        for name, g in grads.items():
            sm = _shard_major(name, g)
            h = sm.shape[1] // 2
            mine.append(lax.dynamic_slice_in_dim(sm, self.core * h, h, axis=1))
            other.append(lax.dynamic_slice_in_dim(sm, (1 - self.core) * h, h, axis=1))
        out = []
        for name, a, b in zip(grads, mine, _sibling_exchange("reduce_sibling_" + label, other)):
            flat = lambda t: t.reshape(-1, t.shape[-1])
            self.partial[name] = _pair_sum("sum_sibling_" + name, flat(a), flat(b), BF).reshape(a.shape)
            out.append(self.partial[name])
        return out


def kernel(x, c, ada_w, ada_b, pre_g, post_g, rel_bias, ab_w_in, ab_w_out, s5_a_re, s5_a_im, s5_log_dt, s5_b_re, s5_b_im, s5_c_re, s5_c_im, s5_d, s5_glu_w, s5_glu_b, gdn_w_in, gdn_conv, gdn_a_log, gdn_dt_bias, gdn_norm_g, gdn_w_out, loss_target, m_ada_w, m_ada_b, m_pre_g, m_post_g, m_rel_bias, m_ab_w_in, m_ab_w_out, m_s5_a_re, m_s5_a_im, m_s5_log_dt, m_s5_b_re, m_s5_b_im, m_s5_c_re, m_s5_c_im, m_s5_d, m_s5_glu_w, m_s5_glu_b, m_gdn_w_in, m_gdn_conv, m_gdn_a_log, m_gdn_dt_bias, m_gdn_norm_g, m_gdn_w_out, v_ada_w, v_ada_b, v_pre_g, v_post_g, v_rel_bias, v_ab_w_in, v_ab_w_out, v_s5_a_re, v_s5_a_im, v_s5_log_dt, v_s5_b_re, v_s5_b_im, v_s5_c_re, v_s5_c_im, v_s5_d, v_s5_glu_w, v_s5_glu_b, v_gdn_w_in, v_gdn_conv, v_gdn_a_log, v_gdn_dt_bias, v_gdn_norm_g, v_gdn_w_out):
    w = dict(ada_w=ada_w, ada_b=ada_b, pre_g=pre_g, post_g=post_g, rel_bias=rel_bias, ab_w_in=ab_w_in, ab_w_out=ab_w_out,
             s5_a_re=s5_a_re, s5_a_im=s5_a_im, s5_log_dt=s5_log_dt, s5_b_re=s5_b_re, s5_b_im=s5_b_im, s5_c_re=s5_c_re, s5_c_im=s5_c_im,
             s5_d=s5_d, s5_glu_w=s5_glu_w, s5_glu_b=s5_glu_b, gdn_w_in=gdn_w_in, gdn_conv=gdn_conv, gdn_a_log=gdn_a_log,
             gdn_dt_bias=gdn_dt_bias, gdn_norm_g=gdn_norm_g, gdn_w_out=gdn_w_out)
    m = dict(ada_w=m_ada_w, ada_b=m_ada_b, pre_g=m_pre_g, post_g=m_post_g, rel_bias=m_rel_bias, ab_w_in=m_ab_w_in, ab_w_out=m_ab_w_out,
             s5_a_re=m_s5_a_re, s5_a_im=m_s5_a_im, s5_log_dt=m_s5_log_dt, s5_b_re=m_s5_b_re, s5_b_im=m_s5_b_im, s5_c_re=m_s5_c_re,
             s5_c_im=m_s5_c_im, s5_d=m_s5_d, s5_glu_w=m_s5_glu_w, s5_glu_b=m_s5_glu_b, gdn_w_in=m_gdn_w_in, gdn_conv=m_gdn_conv,
             gdn_a_log=m_gdn_a_log, gdn_dt_bias=m_gdn_dt_bias, gdn_norm_g=m_gdn_norm_g, gdn_w_out=m_gdn_w_out)
    v = dict(ada_w=v_ada_w, ada_b=v_ada_b, pre_g=v_pre_g, post_g=v_post_g, rel_bias=v_rel_bias, ab_w_in=v_ab_w_in, ab_w_out=v_ab_w_out,
             s5_a_re=v_s5_a_re, s5_a_im=v_s5_a_im, s5_log_dt=v_s5_log_dt, s5_b_re=v_s5_b_re, s5_b_im=v_s5_b_im, s5_c_re=v_s5_c_re,
             s5_c_im=v_s5_c_im, s5_d=v_s5_d, s5_glu_w=v_s5_glu_w, s5_glu_b=v_s5_glu_b, gdn_w_in=v_gdn_w_in, gdn_conv=v_gdn_conv,
             gdn_a_log=v_gdn_a_log, gdn_dt_bias=v_gdn_dt_bias, gdn_norm_g=v_gdn_norm_g, gdn_w_out=v_gdn_w_out)
    ix, iy, ic = _place()
    me = 4 * ix + 2 * iy + ic
    chip = 2 * ix + iy
    n_cols = ada_w.shape[2]

    mine_first = _pack([c, gdn_conv], 32)
    first = _own_slot(_all_gather8("gather_c_conv", mine_first), mine_first, me)
    c_all = first[:, 0:8].reshape(8, D_MODEL)
    conv_full = first[0::2, 8:32].reshape(4, C_CONV, n_cols).transpose(1, 0, 2).reshape(C_CONV, 4 * n_cols)
    mine_mod = _mod_local(c_all, ada_w)
    modl = _own_slot(_all_gather8("gather_mod", mine_mod), mine_mod, me)
    mod = lax.dynamic_index_in_dim(modl[0::2], me, axis=2, keepdims=False)
    mod = mod.transpose(1, 0, 2).reshape(2, 4 * n_cols) + ada_b

    comm = _WeightExchanges({name: w[name][0] for name in _SHARDED}, ic, chip)
    wd = {name: w[name] for name in _SMALL if name != "ada_b"}
    wd = {k: (a if k in ("pre_g", "post_g", "rel_bias") else a[0]) for k, a in wd.items()}
    wd["gdn_conv"] = conv_full
    wd.update(comm.first_weights())

    loss_local, grad_x, grads, dmod = _local_step(x[0], loss_target[0], mod, wd, comm)
    loss = lax.psum(loss_local, ("x", "y", "c"))

    small_shapes = [w[name].shape for name in _SMALL] + [(C_CONV, 4 * n_cols)]
    small_rows = -(-sum(_rows128(int(np.prod(s))) for s in small_shapes) // 64) * 64
    per_dev, dmod_rows = small_rows // 8, _rows128(2 * 3 * D_MODEL)
    partial = _pack([dmod] + [grads[name] for name in _SMALL[1:]] + [grads["gdn_conv"]], small_rows)
    outbound = jnp.concatenate([partial.reshape(8, per_dev, 128), jnp.broadcast_to(partial[None, :dmod_rows], (8, dmod_rows, 128))], axis=1)
    inbound = _own_slot(_all_to_all8("reduce_small_grads", outbound), lax.dynamic_index_in_dim(outbound, me, 0, keepdims=False), me)
    my_rows = _slot_sum("sum_small_grads", inbound[:, :per_dev])
    g_small = _own_slot(_all_gather8("gather_small_grads", my_rows), my_rows, me).reshape(small_rows, 128)
    g_list = _unpack(g_small, small_shapes)
    out_g, out_d, out_m, out_v = {}, {}, {}, {}

    def update(name, g2d):
        shp = w[name].shape
        two_d = lambda a: a.reshape(-1, shp[-1])
        d_, m_, v_ = _adamw("adamw_" + name, two_d(w[name]), g2d, two_d(m[name]), two_d(v[name]))
        out_g[name], out_d[name], out_m[name], out_v[name] = (a.reshape(shp) for a in (g2d, d_, m_, v_))

    for name, g in zip(_SMALL, g_list[:-1]):
        update(name, g.reshape(-1, g.shape[-1]))
    update("gdn_conv", lax.dynamic_slice_in_dim(g_list[-1], chip * n_cols, n_cols, axis=1))

    dmod_all = inbound[:, per_dev:].reshape(8, 2, 4, n_cols)
    dmod_cols = lax.dynamic_index_in_dim(dmod_all, chip, axis=2, keepdims=False).transpose(1, 0, 2)
    update("ada_w", _ada_w_grad(c_all, dmod_cols).reshape(-1, n_cols))

    comm.received["ab_w_in"] = _chip_exchange("reduce_chips", comm.chip_partials("l0_in", {"ab_w_in": grads["ab_w_in"]}), True)[0]
    chip_1 = jnp.reshape(chip, (1,)).astype(jnp.int32)
    core_1 = jnp.reshape(ic, (1,)).astype(jnp.int32)
    reduced = [_chip_sum("sum_chips_" + name, comm.received[name], comm.partial[name], chip_1) for name in _SHARDED]
    for name, g_mine, g_sib in zip(_SHARDED, reduced, _sibling_exchange("reduce_share", reduced)):
        shp = w[name].shape
        two_d = lambda a: a.reshape(-1, shp[-1])
        outs = _adamw_halves("adamw_" + name, two_d(w[name]), g_mine, g_sib, two_d(m[name]), two_d(v[name]), core_1)
        out_g[name], out_d[name], out_m[name], out_v[name] = (a.reshape(shp) for a in outs)

    return (loss, grad_x[None], *[out_g[n] for n in _WEIGHTS], *[out_d[n] for n in _WEIGHTS],
            *[out_m[n] for n in _WEIGHTS], *[out_v[n] for n in _WEIGHTS])
```

```python
import functools
import math

import numpy as np
import jax
import jax.numpy as jnp
from jax import lax
from jax.experimental import pallas as pl
from jax.experimental.pallas import tpu as pltpu

F32 = jnp.float32
BF = jnp.bfloat16
HI = lax.Precision.HIGHEST
MESH = pl.DeviceIdType.MESH

D_MODEL = 1024
EPS = 1e-6
A_HEADS, A_HD, A_WIDTH, A_BLOCK = 8, 64, 512, 128
DILATIONS = (1, 4, 16)
N_KEYS = 128
REL_BUCKETS, REL_MAX_DIST = 32, 2048
B_WIDTH, B_GROUP, B_GROUPS, B_STATE = 512, 16, 32, 64
S5_LANES = 512
S5_TILES = 4
S5_T = 256
C_HEADS, C_DK, C_CHUNK, C_CONV = 8, 128, 64, 4
QKV = 3072
C_IN_PAD = 4224
TM = 256
VMEM_LIMIT_BYTES = 56 * 1024 * 1024
ADAM_LR, ADAM_B1, ADAM_B2, ADAM_EPS, ADAM_WD, ADAM_STEP = 0.001, 0.9, 0.999, 1e-08, 0.01, 10
NEG = float(np.finfo(np.float32).min)


def _cp(*sem):
    return pltpu.CompilerParams(dimension_semantics=sem, vmem_limit_bytes=VMEM_LIMIT_BYTES)


def _bdot(a, b):
    return jnp.dot(a.astype(BF), b.astype(BF), preferred_element_type=F32)


def _bdot_nt(a, b):
    return lax.dot_general(a.astype(BF), b.astype(BF), (((1,), (1,)), ((), ())), preferred_element_type=F32)


def _bdot_tn(a, b):
    return lax.dot_general(a.astype(BF), b.astype(BF), (((0,), (0,)), ((), ())), preferred_element_type=F32)


def _hdot(a, b):
    return jnp.dot(a, b, precision=HI, preferred_element_type=F32)


def _bein(eq, a, b):
    return jnp.einsum(eq, a.astype(BF), b.astype(BF), preferred_element_type=F32)


def _hein(eq, a, b):
    return jnp.einsum(eq, a, b, precision=HI, preferred_element_type=F32)


def _row(tm, n):
    return pl.BlockSpec((tm, n), lambda i: (i, 0))


def _fix(shape):
    return pl.BlockSpec(shape, lambda i: (0,) * len(shape))


def _sds(*shape, dtype=F32):
    return jax.ShapeDtypeStruct(shape, dtype)


def _acc(ref, val):
    ref[...] += val


def _zero_at_first(refs, axis=0):
    @pl.when(pl.program_id(axis) == 0)
    def _():
        for r in refs:
            r[...] = jnp.zeros_like(r)


def _rms(x):
    return x * lax.rsqrt(jnp.mean(x * x, axis=-1, keepdims=True) + EPS)


def _pre_mod(x, g, scale, shift):
    return (_rms(x) * g) * (1.0 + scale) + shift


def _post_res(y, x, post_g, gate):
    return x + gate * (_rms(y) * post_g)


def _merge_gate(o1, o2, o3, l1, l2, l3, ga):
    m = jnp.maximum(jnp.maximum(l1, l2), l3)
    e1, e2, e3 = jnp.exp(l1 - m), jnp.exp(l2 - m), jnp.exp(l3 - m)
    inv = 1.0 / (e1 + e2 + e3)
    return ((e1 * inv) * o1 + (e2 * inv) * o2 + (e3 * inv) * o3) * jax.nn.silu(ga)


def _s5_gelu(ypre, u, d_skip):
    return jax.nn.gelu(ypre + d_skip * u)


def _s5_glu(yb, gl, gb):
    return yb * jax.nn.sigmoid(gl) * jax.nn.silu(gb)


def _l0_front(x, pre_g, scale, shift, w_in):
    s_len = x.shape[0]

    def body(x_ref, g_ref, sc_ref, sh_ref, w_ref, *out_refs):
        qkv_refs, (u_ref, ga_ref, gb_ref, h_ref) = out_refs[:9], out_refs[9:]
        hb = _pre_mod(x_ref[...], g_ref[...], sc_ref[...], sh_ref[...]).astype(BF)
        h_ref[...] = hb
        z = jnp.dot(hb, w_ref[...], preferred_element_type=F32)
        for a in range(3):
            piece = z[:, a * 512:(a + 1) * 512]
            for i, d in enumerate(DILATIONS):
                qkv_refs[3 * a + i][...] = _to_res(piece, d).astype(BF)
        u_ref[...] = z[:, 1536:2048]
        ga_ref[...] = z[:, 2048:2560]
        gb_ref[...] = z[:, 2560:3072]

    vec = _fix((1, D_MODEL))
    return pl.pallas_call(
        body, name="l0_front", grid=(s_len // TM,),
        in_specs=[_row(TM, D_MODEL), vec, vec, vec, _fix((D_MODEL, 3072))],
        out_specs=[_res_spec(d) for d in DILATIONS] * 3 + [_row(TM, 512)] * 3 + [_row(TM, D_MODEL)],
        out_shape=[_sds(*_res_shape(s_len, d), dtype=BF) for d in DILATIONS] * 3 + [_sds(s_len, 512)] * 3 + [_sds(s_len, D_MODEL, dtype=BF)],
        compiler_params=_cp("arbitrary"),
    )(x, pre_g, scale, shift, w_in)


def _front_bwd(name, x, pre_g, scale, shift, w_in, dres, parts, widths):
    s_len = x.shape[0]
    n_in = sum(len(p) for p in parts)
    n_cols = sum(widths)

    def body(*refs):
        x_ref, g_ref, sc_ref, sh_ref, w_ref, dres_ref = refs[:6]
        part_refs = refs[6:6 + n_in]
        dz_ref, dx_ref, dg_ref, dsc_ref, dsh_ref = refs[6 + n_in:]
        _zero_at_first([dg_ref, dsc_ref, dsh_ref])
        _, vjp = jax.vjp(_pre_mod, x_ref[...], g_ref[...], sc_ref[...], sh_ref[...])
        dh = jnp.zeros((TM, D_MODEL), F32)
        col, at = 0, 0
        for grp, width in zip(parts, widths):
            tile = lambda r: _from_res(r[...]) if len(r.shape) == 3 else r[...]
            dz = tile(part_refs[at])
            for r in part_refs[at + 1:at + len(grp)]:
                dz = dz + tile(r)
            at += len(grp)
            dzb = dz.astype(BF)
            dz_ref[:, col:col + width] = dzb
            dh = dh + lax.dot_general(dzb, w_ref[:, col:col + width], (((1,), (1,)), ((), ())), preferred_element_type=F32)
            col += width
        dx, dg, dsc, dsh = vjp(dh)
        dx_ref[...] = dx + dres_ref[...]
        _acc(dg_ref, dg)
        _acc(dsc_ref, dsc)
        _acc(dsh_ref, dsh)

    vec = _fix((1, D_MODEL))
    flat = [a for p in parts for a in p]
    return pl.pallas_call(
        body, name=name, grid=(s_len // TM,),
        in_specs=[_row(TM, D_MODEL), vec, vec, vec, _fix((D_MODEL, n_cols)), _row(TM, D_MODEL)]
        + [_res_spec(a.shape[0], a.shape[2]) if a.ndim == 3 else _row(TM, a.shape[1]) for a in flat],
        out_specs=[_row(TM, n_cols), _row(TM, D_MODEL), vec, vec, vec],
        out_shape=[_sds(s_len, n_cols, dtype=BF), _sds(s_len, D_MODEL), _sds(1, D_MODEL), _sds(1, D_MODEL), _sds(1, D_MODEL)],
        compiler_params=_cp("arbitrary"),
    )(x, pre_g, scale, shift, w_in, dres, *flat)


def _matmul_tn(name, a, b, tn):
    s_len, k_dim = a.shape
    n_dim = b.shape[1]
    ts = 512

    def body(a_ref, b_ref, o_ref):
        _zero_at_first([o_ref], axis=1)
        o_ref[...] += lax.dot_general(a_ref[...], b_ref[...], (((0,), (0,)), ((), ())), preferred_element_type=F32)

    return pl.pallas_call(
        body, name=name, grid=(n_dim // tn, s_len // ts),
        in_specs=[pl.BlockSpec((ts, k_dim), lambda j, i: (i, 0)), pl.BlockSpec((ts, tn), lambda j, i: (i, j))],
        out_specs=pl.BlockSpec((k_dim, tn), lambda j, i: (0, j)),
        out_shape=_sds(k_dim, n_dim),
        compiler_params=_cp("arbitrary", "arbitrary"),
    )(a, b)


def _t5_bucket_np(dist):
    dist = np.maximum(dist, 0)
    max_exact = REL_BUCKETS // 2
    large = max_exact + (np.log(np.maximum(dist, 1) / max_exact)
                         / math.log(REL_MAX_DIST / max_exact) * (REL_BUCKETS - max_exact)).astype(np.int32)
    large = np.minimum(large, REL_BUCKETS - 1)
    return np.where(dist < max_exact, dist, large).astype(np.int32)


def _to_res(z, dil):
    if dil == 1:
        return z[None]
    return jnp.swapaxes(z.reshape(z.shape[0] // dil, dil, z.shape[1]), 0, 1)


def _from_res(z):
    if z.shape[0] == 1:
        return z[0]
    return jnp.swapaxes(z, 0, 1).reshape(z.shape[0] * z.shape[1], z.shape[2])


def _res_shape(s_len, dil, width=A_WIDTH):
    return (dil, s_len // dil, width)


def _res_spec(dil, width=A_WIDTH):
    return pl.BlockSpec((dil, TM // dil, width), lambda i: (0, i, 0))


def _bucket_table():
    qi = np.arange(A_BLOCK)[:, None]
    kj = np.arange(2 * A_BLOCK)[None, :]
    return np.stack([_t5_bucket_np((qi + A_BLOCK - kj) * d) for d in DILATIONS], 0)


def _attn_mask(first):
    qi = lax.broadcasted_iota(jnp.int32, (A_BLOCK, 2 * A_BLOCK), 0)
    kj = lax.broadcasted_iota(jnp.int32, (A_BLOCK, 2 * A_BLOCK), 1)
    rel = qi + A_BLOCK - kj
    return (rel >= 0) & (rel <= N_KEYS) & (jnp.logical_not(first) | (kj >= A_BLOCK))


def _attn_specs(nb, rev):
    n_of = (lambda i: nb - 1 - i) if rev else (lambda i: i)
    cur = pl.BlockSpec((None, A_BLOCK, A_WIDTH), lambda r, i: (r, n_of(i), 0))
    prev = pl.BlockSpec((None, A_BLOCK, A_WIDTH), lambda r, i: (r, jnp.maximum(n_of(i) - 1, 0), 0))
    bias = pl.BlockSpec((A_HEADS, A_BLOCK, 2 * A_BLOCK), lambda r, i: (0, 0, 0))
    return cur, prev, bias


def _attn_fwd(q, k, v, bias):
    dil, t_len, _ = q.shape
    nb = t_len // A_BLOCK
    scale = A_HD ** -0.5

    def body(q_ref, kp_ref, kc_ref, vp_ref, vc_ref, b_ref, o_ref, l_ref):
        mask = _attn_mask(pl.program_id(1) == 0)
        lane = lax.broadcasted_iota(jnp.int32, (1, 128), 1)
        for hp in range(A_HEADS // 2):
            sl = slice(hp * 128, (hp + 1) * 128)
            qp = q_ref[:, sl]
            kw = jnp.concatenate([kp_ref[:, sl], kc_ref[:, sl]], axis=0).astype(BF)
            vw = jnp.concatenate([vp_ref[:, sl], vc_ref[:, sl]], axis=0).astype(BF)
            outs, lses = [], []
            for j in range(2):
                hm = (lane < 64) if j == 0 else (lane >= 64)
                s = _bdot_nt(jnp.where(hm, qp, 0.0), kw) * scale
                s = jnp.where(mask, s + b_ref[2 * hp + j], NEG)
                m = jnp.max(s, axis=-1, keepdims=True)
                p = jnp.exp(s - m)
                den = jnp.sum(p, axis=-1, keepdims=True)
                outs.append(_bdot(p, vw) / den)
                lses.append(m + jnp.log(den))
            hm0 = lane < 64
            o_ref[:, sl] = jnp.where(hm0, outs[0], outs[1])
            l_ref[:, sl] = jnp.where(hm0, lses[0], lses[1])

    cur, prev, bias_spec = _attn_specs(nb, False)
    return pl.pallas_call(
        body, name=f"attn_fwd_d{dil}", grid=(dil, nb),
        in_specs=[cur, prev, cur, prev, cur, bias_spec],
        out_specs=[cur, cur],
        out_shape=[_sds(dil, t_len, A_WIDTH)] * 2,
        compiler_params=_cp("arbitrary", "arbitrary"),
    )(q, k, k, v, v, bias)


def _attn_bwd(q, k, v, bias, o, l, do, dl):
    dil, t_len, _ = q.shape
    nb = t_len // A_BLOCK
    scale = A_HD ** -0.5

    def body(q_ref, kp_ref, kc_ref, vp_ref, vc_ref, b_ref, o_ref, l_ref, do_ref, dl_ref,
             dq_ref, dk_ref, dv_ref, db_ref, ck_ref, cv_ref):
        _zero_at_first([ck_ref, cv_ref], axis=1)

        @pl.when((pl.program_id(0) == 0) & (pl.program_id(1) == 0))
        def _():
            db_ref[...] = jnp.zeros_like(db_ref)

        mask = _attn_mask(pl.program_id(1) == nb - 1)
        lane = lax.broadcasted_iota(jnp.int32, (1, 128), 1)
        for hp in range(A_HEADS // 2):
            sl = slice(hp * 128, (hp + 1) * 128)
            qp = q_ref[:, sl]
            kw = jnp.concatenate([kp_ref[:, sl], kc_ref[:, sl]], axis=0).astype(BF)
            vw = jnp.concatenate([vp_ref[:, sl], vc_ref[:, sl]], axis=0).astype(BF)
            op, lp, dop, dlp = o_ref[:, sl], l_ref[:, sl], do_ref[:, sl], dl_ref[:, sl]
            dq_acc = jnp.zeros((A_BLOCK, 128), F32)
            dk_acc = jnp.zeros((2 * A_BLOCK, 128), F32)
            dv_acc = jnp.zeros((2 * A_BLOCK, 128), F32)
            for j in range(2):
                hm = (lane < 64) if j == 0 else (lane >= 64)
                qm = jnp.where(hm, qp, 0.0)
                s = _bdot_nt(qm, kw) * scale
                s = jnp.where(mask, s + b_ref[2 * hp + j], NEG)
                lse = jnp.max(jnp.where(hm, lp, NEG), axis=-1, keepdims=True)
                p = jnp.exp(s - lse)
                do_h = jnp.where(hm, dop, 0.0)
                dd = jnp.sum(do_h * op, axis=-1, keepdims=True)
                dlse = jnp.sum(jnp.where(hm, dlp, 0.0), axis=-1, keepdims=True)
                ds = p * (_bdot_nt(do_h, vw) - dd + dlse)
                dv_acc = dv_acc + _bdot_tn(p, do_h)
                dq_acc = dq_acc + jnp.where(hm, _bdot(ds, kw), 0.0) * scale
                dk_acc = dk_acc + _bdot_tn(ds, qm) * scale
                db_ref[2 * hp + j] += ds
            dq_ref[:, sl] = dq_acc
            dk_ref[:, sl] = dk_acc[A_BLOCK:] + ck_ref[:, sl]
            dv_ref[:, sl] = dv_acc[A_BLOCK:] + cv_ref[:, sl]
            ck_ref[:, sl] = dk_acc[:A_BLOCK]
            cv_ref[:, sl] = dv_acc[:A_BLOCK]

    cur, prev, bias_spec = _attn_specs(nb, True)
    return pl.pallas_call(
        body, name=f"attn_bwd_d{dil}", grid=(dil, nb),
        in_specs=[cur, prev, cur, prev, cur, bias_spec, cur, cur, cur, cur],
        out_specs=[cur, cur, cur, bias_spec],
        out_shape=[_sds(dil, t_len, A_WIDTH)] * 3 + [_sds(A_HEADS, A_BLOCK, 2 * A_BLOCK)],
        scratch_shapes=[pltpu.VMEM((A_BLOCK, A_WIDTH), F32)] * 2,
        compiler_params=_cp("arbitrary", "arbitrary"),
    )(q, k, k, v, v, bias, o, l, do, dl)


def _attn_bias(rel_bias, table):
    def body(rb_ref, t_ref, *o_refs):
        for c in range(3):
            t = t_ref[c]
            acc = [jnp.zeros((A_BLOCK, 2 * A_BLOCK), F32) for _ in range(A_HEADS)]
            for b in range(REL_BUCKETS):
                hit = t == b
                acc = [jnp.where(hit, rb_ref[b, h], acc[h]) for h in range(A_HEADS)]
            for h in range(A_HEADS):
                o_refs[c][h] = acc[h]

    return pl.pallas_call(body, name="attn_bias", out_shape=[_sds(A_HEADS, A_BLOCK, 2 * A_BLOCK)] * 3,
                          in_specs=[pl.BlockSpec(memory_space=pltpu.SMEM), pl.BlockSpec(memory_space=pltpu.VMEM)],
                          compiler_params=pltpu.CompilerParams(vmem_limit_bytes=VMEM_LIMIT_BYTES))(rel_bias, table)


def _rel_bias_grad(dbs, idx_rows):
    n = A_BLOCK * 2 * A_BLOCK

    def body(d0_ref, d1_ref, d2_ref, idx_ref, o_ref):
        bucket = lax.broadcasted_iota(jnp.int32, (REL_BUCKETS, n), 0).astype(F32)
        acc = jnp.zeros((A_HEADS, REL_BUCKETS), F32)
        for c, db_ref in enumerate((d0_ref, d1_ref, d2_ref)):
            onehot = (idx_ref[c:c + 1, :] == bucket).astype(F32)
            acc = acc + lax.dot_general(db_ref[...], onehot, (((1,), (1,)), ((), ())), precision=HI, preferred_element_type=F32)
        o_ref[...] = acc

    return pl.pallas_call(body, name="rel_bias_grad", out_shape=_sds(A_HEADS, REL_BUCKETS),
                          compiler_params=pltpu.CompilerParams(vmem_limit_bytes=VMEM_LIMIT_BYTES))(
                              *[d.reshape(A_HEADS, n) for d in dbs], idx_rows)


def _s5_param_fn(a_re, a_im, log_dt, bt_re, bt_im):
    dt = jnp.exp(log_dt)
    mag = jnp.exp(dt * a_re)
    abar_r, abar_i = mag * jnp.cos(dt * a_im), mag * jnp.sin(dt * a_im)
    den = a_re * a_re + a_im * a_im
    fr = ((abar_r - 1.0) * a_re + abar_i * a_im) / den
    fi = (abar_i * a_re - (abar_r - 1.0) * a_im) / den
    row = lax.broadcasted_iota(jnp.int32, (B_WIDTH, B_GROUPS), 0)
    grp = lax.broadcasted_iota(jnp.int32, (B_WIDTH, B_GROUPS), 1)
    expand = ((row // B_GROUP) == grp).astype(F32)
    fr_e, fi_e = _hdot(expand, fr), _hdot(expand, fi)
    return abar_r, abar_i, fr_e * bt_re - fi_e * bt_im, fr_e * bt_im + fi_e * bt_re


def _s5_params(a_re, a_im, log_dt, bt_re, bt_im):
    def body(ar, ai, ld, br, bi, o1, o2, o3, o4):
        o1[...], o2[...], o3[...], o4[...] = _s5_param_fn(ar[...], ai[...], ld[...], br[...], bi[...])

    return pl.pallas_call(body, name="s5_params",
                          out_shape=[_sds(B_GROUPS, B_STATE)] * 2 + [_sds(B_WIDTH, B_STATE)] * 2)(a_re, a_im, log_dt, bt_re, bt_im)


def _s5_params_bwd(a_re, a_im, log_dt, bt_re, bt_im, d1, d2, d3, d4):
    def body(ar, ai, ld, br, bi, c1, c2, c3, c4, o1, o2, o3, o4, o5):
        _, vjp = jax.vjp(_s5_param_fn, ar[...], ai[...], ld[...], br[...], bi[...])
        o1[...], o2[...], o3[...], o4[...], o5[...] = vjp((c1[...], c2[...], c3[...], c4[...]))

    return pl.pallas_call(body, name="s5_params_bwd",
                          out_shape=[_sds(B_GROUPS, B_STATE)] * 2 + [_sds(B_GROUPS, 1)] + [_sds(B_WIDTH, B_STATE)] * 2,
                          )(a_re, a_im, log_dt, bt_re, bt_im, d1, d2, d3, d4)


S5_SUB = 8
S5_GROUPS = S5_T // S5_SUB


def _dscan(xr, xi, cr, ci, period, reverse):
    n = xr.shape[0]
    rows = lax.broadcasted_iota(jnp.int32, xr.shape, 0)
    pos = rows % period
    k = 1
    while k < period:
        if reverse:
            keep, shift = pos < period - k, n - k
        else:
            keep, shift = pos >= k, k
        sr = jnp.where(keep, pltpu.roll(xr, shift, 0), 0.0)
        si = jnp.where(keep, pltpu.roll(xi, shift, 0), 0.0)
        xr, xi = xr + cr * sr - ci * si, xi + cr * si + ci * sr
        cr, ci = cr * cr - ci * ci, 2.0 * cr * ci
        k *= 2
    return xr, xi, cr, ci


def _pick_row(x, r):
    rows = lax.broadcasted_iota(jnp.int32, x.shape, 0)
    return jnp.sum(jnp.where(rows == r, x, 0.0), axis=0, keepdims=True)


def _scan_tables(ar, ai, reverse):
    rows = lax.broadcasted_iota(jnp.int32, (S5_SUB, S5_LANES), 0)
    at = rows == (S5_SUB - 1 if reverse else 0)
    p8r, p8i, a8r, a8i = _dscan(jnp.where(at, ar, 0.0), jnp.where(at, ai, 0.0), ar, ai, S5_SUB, reverse)
    grp = lax.broadcasted_iota(jnp.int32, (S5_GROUPS, S5_LANES), 0)
    at = grp == (S5_GROUPS - 1 if reverse else 0)
    pgr, pgi, _, _ = _dscan(jnp.where(at, a8r, 0.0), jnp.where(at, a8i, 0.0), a8r, a8i, S5_GROUPS, reverse)
    return p8r, p8i, pgr, pgi


def _block_scan(br, bi, ar, ai, carry, tables, reverse, work):
    p8r, p8i, pgr, pgi = tables
    cin_r, cin_i = carry
    xw_r, xw_i, ew_r, ew_i = work
    xr, xi, a8r, a8i = _dscan(br, bi, ar, ai, S5_SUB, reverse)
    xw_r[...] = xr
    xw_i[...] = xi
    end = 0 if reverse else S5_SUB - 1
    for g in range(S5_GROUPS):
        ew_r[g:g + 1, :] = xw_r[S5_SUB * g + end:S5_SUB * g + end + 1, :]
        ew_i[g:g + 1, :] = xw_i[S5_SUB * g + end:S5_SUB * g + end + 1, :]
    er, ei, _, _ = _dscan(ew_r[...], ew_i[...], a8r, a8i, S5_GROUPS, reverse)
    er, ei = er + pgr * cin_r - pgi * cin_i, ei + pgr * cin_i + pgi * cin_r
    grp = lax.broadcasted_iota(jnp.int32, (S5_GROUPS, S5_LANES), 0)
    if reverse:
        ew_r[...] = jnp.where(grp == S5_GROUPS - 1, cin_r, pltpu.roll(er, S5_GROUPS - 1, 0))
        ew_i[...] = jnp.where(grp == S5_GROUPS - 1, cin_i, pltpu.roll(ei, S5_GROUPS - 1, 0))
    else:
        ew_r[...] = jnp.where(grp == 0, cin_r, pltpu.roll(er, 1, 0))
        ew_i[...] = jnp.where(grp == 0, cin_i, pltpu.roll(ei, 1, 0))
    for g in range(S5_GROUPS):
        rows = slice(S5_SUB * g, S5_SUB * (g + 1))
        nr, ni = ew_r[g:g + 1, :], ew_i[g:g + 1, :]
        xw_r[rows, :] += p8r * nr - p8i * ni
        xw_i[rows, :] += p8r * ni + p8i * nr
    last = 0 if reverse else S5_GROUPS - 1
    return xw_r[...], xw_i[...], (_pick_row(er, last), _pick_row(ei, last))


def _s5_tile_specs(n_t, rev):
    t_of = (lambda i: n_t - 1 - i) if rev else (lambda i: i)
    u_spec = pl.BlockSpec((S5_T, 128), lambda j, i: (t_of(i), j))
    x_spec = pl.BlockSpec((S5_T, S5_LANES), lambda j, i: (t_of(i), j))
    b_spec = pl.BlockSpec((1, 128, S5_LANES), lambda j, i: (j, 0, 0))
    c_spec = pl.BlockSpec((1, S5_LANES, 128), lambda j, i: (j, 0, 0))
    a_spec = pl.BlockSpec((1, S5_LANES), lambda j, i: (0, j))
    return u_spec, x_spec, b_spec, c_spec, a_spec


def _s5_scratch(with_rows):
    return ([pltpu.VMEM((1, S5_LANES), F32)] * 2 + [pltpu.VMEM((S5_SUB, S5_LANES), F32)] * 2
            + [pltpu.VMEM((S5_GROUPS, S5_LANES), F32)] * 4 + ([pltpu.VMEM((S5_T, S5_LANES), F32)] * 2 if with_rows else []))


def _s5_scan_fwd(u, btr, bti, ctr, cti, abr, abi, exchange=None):
    s_len = u.shape[0]
    n_t = s_len // S5_T

    def body(u_ref, btr_ref, bti_ref, ctr_ref, cti_ref, ar_ref, ai_ref, xr_ref, xi_ref, y_ref, car, cai, p8r, p8i, pgr, pgi, ew_r, ew_i):
        ar, ai = ar_ref[...], ai_ref[...]

        @pl.when(pl.program_id(1) == 0)
        def _():
            car[...] = jnp.zeros_like(car)
            cai[...] = jnp.zeros_like(cai)
            p8r[...], p8i[...], pgr[...], pgi[...] = _scan_tables(ar, ai, False)

        ub = u_ref[...]
        xr, xi, (ncr, nci) = _block_scan(_bdot(ub, btr_ref[0]), _bdot(ub, bti_ref[0]), ar, ai, (car[...], cai[...]),
                                         (p8r[...], p8i[...], pgr[...], pgi[...]), False, (xr_ref, xi_ref, ew_r, ew_i))
        car[...] = ncr
        cai[...] = nci
        y_ref[...] = _bdot(xr, ctr_ref[0]) - _bdot(xi, cti_ref[0])

    u_spec, x_spec, b_spec, c_spec, a_spec = _s5_tile_specs(n_t, False)
    return _call_with_exchange(
        body, "s5_scan_fwd", (S5_TILES, n_t),
        [u_spec, b_spec, b_spec, c_spec, c_spec, a_spec, a_spec], [x_spec, x_spec, u_spec],
        [_sds(s_len, S5_TILES * S5_LANES)] * 2 + [_sds(s_len, B_WIDTH)], _s5_scratch(False),
        (u, btr, bti, ctr, cti, abr, abi), exchange)


def _s5_scan_bwd(dy, xr, xi, u, btr, bti, ctr, cti, abr, abi, exchange=None):
    s_len = u.shape[0]
    n_t = s_len // S5_T

    def body(dy_ref, xr_ref, xi_ref, xrp_ref, xip_ref, u_ref, btr_ref, bti_ref, ctr_ref, cti_ref, ar_ref, ai_ref,
             du_ref, dbtr_ref, dbti_ref, dctr_ref, dcti_ref, dar_ref, dai_ref, car, cai, p8r, p8i, pgr, pgi, ew_r, ew_i, xw_r, xw_i):
        ar, ai = ar_ref[...], ai_ref[...]
        i = pl.program_id(1)
        rows = lax.broadcasted_iota(jnp.int32, (S5_T, S5_LANES), 0)

        @pl.when(i == 0)
        def _():
            for r in (car, cai, dbtr_ref, dbti_ref, dctr_ref, dcti_ref, dar_ref, dai_ref):
                r[...] = jnp.zeros_like(r)
            p8r[...], p8i[...], pgr[...], pgi[...] = _scan_tables(ar, -ai, True)

        dyb = dy_ref[...]
        xr_b, xi_b, ub = xr_ref[...], xi_ref[...], u_ref[...]
        dctr_ref[0] += _bdot_tn(xr_b, dyb)
        dcti_ref[0] -= _bdot_tn(xi_b, dyb)
        gr, gi, (ncr, nci) = _block_scan(_bdot_nt(dyb, ctr_ref[0]), -_bdot_nt(dyb, cti_ref[0]), ar, -ai, (car[...], cai[...]),
                                         (p8r[...], p8i[...], pgr[...], pgi[...]), True, (xw_r, xw_i, ew_r, ew_i))
        car[...] = ncr
        cai[...] = nci
        du_ref[...] = _bdot_nt(gr, btr_ref[0]) + _bdot_nt(gi, bti_ref[0])
        dbtr_ref[0] += _bdot_tn(ub, gr)
        dbti_ref[0] += _bdot_tn(ub, gi)
        has_prev = (i < n_t - 1).astype(F32)
        hr = _pick_row(xrp_ref[...], 7) * has_prev
        hi = _pick_row(xip_ref[...], 7) * has_prev
        xpr = jnp.where(rows == 0, hr, pltpu.roll(xr_b, 1, 0))
        xpi = jnp.where(rows == 0, hi, pltpu.roll(xi_b, 1, 0))
        dar_ref[...] += jnp.sum(gr * xpr + gi * xpi, axis=0, keepdims=True)
        dai_ref[...] += jnp.sum(gi * xpr - gr * xpi, axis=0, keepdims=True)

    u_spec, x_spec, b_spec, c_spec, a_spec = _s5_tile_specs(n_t, True)
    halo = pl.BlockSpec((8, S5_LANES), lambda j, i: (jnp.maximum((n_t - 1 - i) * (S5_T // 8) - 1, 0), j))
    return _call_with_exchange(
        body, "s5_scan_bwd", (S5_TILES, n_t),
        [u_spec, x_spec, x_spec, halo, halo, u_spec, b_spec, b_spec, c_spec, c_spec, a_spec, a_spec],
        [u_spec, b_spec, b_spec, c_spec, c_spec, a_spec, a_spec],
        [_sds(s_len, B_WIDTH)] + [_sds(S5_TILES, 128, S5_LANES)] * 2 + [_sds(S5_TILES, S5_LANES, 128)] * 2
        + [_sds(1, S5_TILES * S5_LANES)] * 2,
        _s5_scratch(True), (dy, xr, xi, xr, xi, u, btr, bti, ctr, cti, abr, abi), exchange)


def _blockdiag_b(bbar_t):
    blocks = bbar_t.reshape(S5_TILES, 8, B_GROUP, B_STATE)
    return jnp.einsum('jgmp,gh->jgmhp', blocks, jnp.eye(8, dtype=F32)).reshape(S5_TILES, 128, S5_LANES)


def _blockdiag_b_t(d):
    return jnp.einsum('jgmgp->jgmp', d.reshape(S5_TILES, 8, B_GROUP, 8, B_STATE)).reshape(B_WIDTH, B_STATE)


def _blockdiag_c(c):
    blocks = c.reshape(S5_TILES, 8, B_GROUP, B_STATE)
    return jnp.einsum('jgmp,gh->jhpgm', blocks, jnp.eye(8, dtype=F32)).reshape(S5_TILES, S5_LANES, 128)


def _blockdiag_c_t(d):
    return jnp.einsum('jgpgm->jgmp', d.reshape(S5_TILES, 8, B_STATE, 8, B_GROUP)).reshape(B_GROUPS, B_GROUP, B_STATE)


def _l0_out(os, ls, ga, gb, ypre, u, x, d_skip, glu_w, glu_b, w_out, post_g, gate):
    s_len = x.shape[0]

    def body(o0, o1, o2, l0, l1, l2, ga_ref, gb_ref, yp_ref, u_ref, x_ref, d_ref, gw_ref, gbias_ref, w_ref, pg_ref, gt_ref, x1_ref, y_ref):
        oa = _merge_gate(*[_from_res(r[...]) for r in (o0, o1, o2, l0, l1, l2)], ga_ref[...])
        yb = _s5_gelu(yp_ref[...], u_ref[...], d_ref[...])
        ob = _s5_glu(yb, _bdot(yb, gw_ref[...]) + gbias_ref[...], gb_ref[...])
        y = _bdot(oa, w_ref[0:512, :]) + _bdot(ob, w_ref[512:1024, :])
        y_ref[...] = y
        x1_ref[...] = _post_res(y, x_ref[...], pg_ref[...], gt_ref[...])

    vec, half = _fix((1, D_MODEL)), _fix((1, 512))
    return pl.pallas_call(
        body, name="l0_out", grid=(s_len // TM,),
        in_specs=[_res_spec(d) for d in DILATIONS] * 2 + [_row(TM, 512)] * 4
        + [_row(TM, D_MODEL), half, _fix((512, 512)), half, _fix((D_MODEL, D_MODEL)), vec, vec],
        out_specs=[_row(TM, D_MODEL)] * 2,
        out_shape=[_sds(s_len, D_MODEL)] * 2,
        compiler_params=_cp("arbitrary"),
    )(*os, *ls, ga, gb, ypre, u, x, d_skip, glu_w, glu_b, w_out, post_g, gate)


def _l0_out_bwd(os, ls, ga, gb, ypre, u, x, y, d_skip, glu_w, glu_b, w_out, post_g, gate, dx1, exchange=None):
    s_len = x.shape[0]

    def body(o0, o1, o2, l0, l1, l2, ga_ref, gb_ref, yp_ref, u_ref, x_ref, y_ref, d_ref, gw_ref, gbias_ref, w_ref, pg_ref, gt_ref, dx1_ref,
             do0, do1, do2, dl0, dl1, dl2, dga_ref, dgb_ref, dyp_ref, du_ref, dd_ref, dgw_ref, dgbias_ref, dw_ref, dpg_ref, dgt_ref):
        _zero_at_first([dd_ref, dgw_ref, dgbias_ref, dw_ref, dpg_ref, dgt_ref])
        _, vjp2 = jax.vjp(_post_res, y_ref[...], x_ref[...], pg_ref[...], gt_ref[...])
        dy, _, dpg, dgt = vjp2(dx1_ref[...])
        _acc(dpg_ref, dpg)
        _acc(dgt_ref, dgt)
        oa, vjp_a = jax.vjp(_merge_gate, *[_from_res(r[...]) for r in (o0, o1, o2, l0, l1, l2)], ga_ref[...])
        yb, vjp_g = jax.vjp(_s5_gelu, yp_ref[...], u_ref[...], d_ref[...])
        gl = _bdot(yb, gw_ref[...]) + gbias_ref[...]
        ob, vjp_b = jax.vjp(_s5_glu, yb, gl, gb_ref[...])
        dw_ref[0:512, :] += _bdot_tn(oa, dy)
        dw_ref[512:1024, :] += _bdot_tn(ob, dy)
        d1, d2, d3, e1, e2, e3, dga = vjp_a(_bdot_nt(dy, w_ref[0:512, :]))
        for ref, val, d in zip((do0, do1, do2, dl0, dl1, dl2), (d1, d2, d3, e1, e2, e3), DILATIONS * 2):
            ref[...] = _to_res(val, d)
        dga_ref[...] = dga
        dyb, dgl, dgb = vjp_b(_bdot_nt(dy, w_ref[512:1024, :]))
        dgb_ref[...] = dgb
        dgw_ref[...] += _bdot_tn(yb, dgl)
        _acc(dgbias_ref, jnp.sum(dgl, axis=0, keepdims=True))
        dyp, du, dd = vjp_g(dyb + _bdot_nt(dgl, gw_ref[...]))
        dyp_ref[...] = dyp
        du_ref[...] = du
        _acc(dd_ref, dd)

    vec, half = _fix((1, D_MODEL)), _fix((1, 512))
    r5, r10 = _row(TM, 512), _row(TM, D_MODEL)
    res6 = [_res_spec(d) for d in DILATIONS] * 2
    return _call_with_exchange(
        body, "l0_out_bwd", (s_len // TM,),
        res6 + [r5] * 4 + [r10, r10, half, _fix((512, 512)), half, _fix((D_MODEL, D_MODEL)), vec, vec, r10],
        res6 + [r5] * 4 + [half, _fix((512, 512)), half, _fix((D_MODEL, D_MODEL)), vec, vec],
        [_sds(*_res_shape(s_len, d)) for d in DILATIONS] * 2 + [_sds(s_len, 512)] * 4
        + [_sds(1, 512), _sds(512, 512), _sds(1, 512), _sds(D_MODEL, D_MODEL), _sds(1, D_MODEL), _sds(1, D_MODEL)],
        [], (*os, *ls, ga, gb, ypre, u, x, y, d_skip, glu_w, glu_b, w_out, post_g, gate, dx1), exchange)


def _l1_front(x, pre_g, scale, shift, w_in):
    s_len = x.shape[0]

    def body(x_ref, g_ref, sc_ref, sh_ref, w_ref, raw_ref, gate_ref, ba_ref, h_ref):
        hb = _pre_mod(x_ref[...], g_ref[...], sc_ref[...], sh_ref[...]).astype(BF)
        h_ref[...] = hb
        z = jnp.dot(hb, w_ref[...], preferred_element_type=F32)
        raw_ref[...] = z[:, 0:QKV]
        gate_ref[...] = z[:, QKV:QKV + 1024]
        ba_ref[...] = z[:, QKV + 1024:C_IN_PAD]

    vec = _fix((1, D_MODEL))
    return pl.pallas_call(
        body, name="l1_front", grid=(s_len // TM,),
        in_specs=[_row(TM, D_MODEL), vec, vec, vec, _fix((D_MODEL, C_IN_PAD))],
        out_specs=[_row(TM, QKV), _row(TM, 1024), _row(TM, 128), _row(TM, D_MODEL)],
        out_shape=[_sds(s_len, QKV), _sds(s_len, 1024), _sds(s_len, 128), _sds(s_len, D_MODEL, dtype=BF)],
        compiler_params=_cp("arbitrary"),
    )(x, pre_g, scale, shift, w_in)


def _bg_fn(ba, alog_row, dtb_row):
    lane = lax.broadcasted_iota(jnp.int32, (1, 128), 1)
    g = -jnp.exp(alog_row) * jax.nn.softplus(ba + dtb_row)
    return jnp.where(lane < C_HEADS, jax.nn.sigmoid(ba), jnp.where(lane < 2 * C_HEADS, g, 0.0))


def _act_q(c):
    q = jax.nn.silu(c)
    return q * lax.rsqrt(jnp.sum(q * q, axis=-1, keepdims=True) + EPS) * (C_DK ** -0.5)


def _act_k(c):
    k = jax.nn.silu(c)
    return k * lax.rsqrt(jnp.sum(k * k, axis=-1, keepdims=True) + EPS)


def _act_of(s):
    return _act_q if s < 8 else (_act_k if s < 16 else jax.nn.silu)


def _gdn_prep(raw, ba, conv_w, alog_row, dtb_row):
    s_len = raw.shape[0]

    def body(raw_ref, halo_ref, ba_ref, w_ref, al_ref, dt_ref, qkv_ref, bg_ref):
        bg_ref[...] = _bg_fn(ba_ref[...], al_ref[...], dt_ref[...])
        has_prev = (pl.program_id(0) > 0).astype(F32)
        for s in range(24):
            sl = slice(s * 128, (s + 1) * 128)
            cat = jnp.concatenate([halo_ref[:, sl] * has_prev, raw_ref[:, sl]], axis=0)
            conv = w_ref[3:4, sl] * cat[8:]
            for j in range(3):
                conv = conv + w_ref[j:j + 1, sl] * pltpu.roll(cat, 3 - j, 0)[8:]
            qkv_ref[:, sl] = _act_of(s)(conv)

    halo = pl.BlockSpec((8, QKV), lambda i: (jnp.maximum(i * (TM // 8) - 1, 0), 0))
    row128 = _fix((1, 128))
    return pl.pallas_call(
        body, name="gdn_prep", grid=(s_len // TM,),
        in_specs=[_row(TM, QKV), halo, _row(TM, 128), _fix((C_CONV, QKV)), row128, row128],
        out_specs=[_row(TM, QKV), _row(TM, 128)],
        out_shape=[_sds(s_len, QKV), _sds(s_len, 128)],
        compiler_params=_cp("arbitrary"),
    )(raw, raw, ba, conv_w, alog_row, dtb_row)


def _gdn_prep_bwd(raw, ba, conv_w, alog_row, dtb_row, dq, dk, dv, dbg):
    s_len = raw.shape[0]
    n_tiles = s_len // TM
    ext = TM + 8

    def body(raw_ref, prev_ref, next_ref, ba_ref, w_ref, al_ref, dt_ref, dq_ref, dqn_ref, dk_ref, dkn_ref, dv_ref, dvn_ref, dbg_ref,
             draw_ref, dba_ref, dw_ref, dal_ref, ddt_ref):
        _zero_at_first([dw_ref, dal_ref, ddt_ref])
        i = pl.program_id(0)
        _, vjp_bg = jax.vjp(_bg_fn, ba_ref[...], al_ref[...], dt_ref[...])
        dba, dal, ddt = vjp_bg(dbg_ref[...])
        dba_ref[...] = dba
        _acc(dal_ref, dal)
        _acc(ddt_ref, ddt)
        has_prev = (i > 0).astype(F32)
        has_next = (i < n_tiles - 1).astype(F32)
        ct_refs = ((dq_ref, dqn_ref), (dk_ref, dkn_ref), (dv_ref, dvn_ref))
        for s in range(24):
            sl = slice(s * 128, (s + 1) * 128)
            hl = slice((s % 8) * 128, (s % 8 + 1) * 128)
            tile_ref, nxt_ref = ct_refs[s // 8]
            cat = jnp.concatenate([prev_ref[:, sl] * has_prev, raw_ref[:, sl], next_ref[:, sl] * has_next], axis=0)
            shifted = [pltpu.roll(cat, 3 - j, 0)[8:] for j in range(3)] + [cat[8:]]
            conv = w_ref[3:4, sl] * shifted[3]
            for j in range(3):
                conv = conv + w_ref[j:j + 1, sl] * shifted[j]
            ct = jnp.concatenate([tile_ref[:, hl], nxt_ref[:, hl] * has_next], axis=0)
            _, vjp_act = jax.vjp(_act_of(s), conv)
            dconv, = vjp_act(ct)
            draw = w_ref[3:4, sl] * dconv[:TM]
            for j in range(3):
                draw = draw + w_ref[j:j + 1, sl] * pltpu.roll(dconv, ext - (3 - j), 0)[:TM]
            draw_ref[:, sl] = draw
            for j in range(4):
                dw_ref[j:j + 1, sl] += jnp.sum(dconv[:TM] * shifted[j][:TM], axis=0, keepdims=True)

    prev = pl.BlockSpec((8, QKV), lambda i: (jnp.maximum(i * (TM // 8) - 1, 0), 0))
    nxt = lambda n: pl.BlockSpec((8, n), lambda i: (jnp.minimum((i + 1) * (TM // 8), s_len // 8 - 1), 0))
    row128 = _fix((1, 128))
    ct_specs = [_row(TM, 1024), nxt(1024)] * 3
    return pl.pallas_call(
        body, name="gdn_prep_bwd", grid=(n_tiles,),
        in_specs=[_row(TM, QKV), prev, nxt(QKV), _row(TM, 128), _fix((C_CONV, QKV)), row128, row128] + ct_specs + [_row(TM, 128)],
        out_specs=[_row(TM, QKV), _row(TM, 128), _fix((C_CONV, QKV)), row128, row128],
        out_shape=[_sds(s_len, QKV), _sds(s_len, 128), _sds(C_CONV, QKV), _sds(1, 128), _sds(1, 128)],
        compiler_params=_cp("arbitrary"),
    )(raw, raw, raw, ba, conv_w, alog_row, dtb_row, dq, dq, dk, dk, dv, dv, dbg)


def _tein(eq, a, b):
    return jnp.einsum(eq, a, b, precision=lax.Precision.HIGH, preferred_element_type=F32)


def _unit_lower_inverse(lower):
    ri = lax.broadcasted_iota(jnp.int32, (C_CHUNK, C_CHUNK), 0)
    ci = lax.broadcasted_iota(jnp.int32, (C_CHUNK, C_CHUNK), 1)
    eye = (ri == ci).astype(F32)[None]
    p_mat = -lower
    inv = eye + p_mat
    for _ in range(5):
        p_mat = _bein('hij,hjk->hik', p_mat, p_mat)
        inv = inv + _bein('hij,hjk->hik', inv, p_mat)
    inv = _tein('hij,hjk->hik', inv, 2.0 * eye - _tein('hij,hjk->hik', eye + lower, inv))
    return jnp.where((ri >= ci)[None], inv, 0.0)


@jax.custom_vjp
def _known_inverse(lower, inv):
    return inv


def _known_inverse_fwd(lower, inv):
    return inv, inv


def _known_inverse_bwd(inv, d_inv):
    d_lower = -_tein('hik,hjk->hij', _tein('hji,hjk->hik', inv, d_inv), inv)
    return d_lower, jnp.zeros_like(inv)


_known_inverse.defvjp(_known_inverse_fwd, _known_inverse_bwd)


def _gdn_local(q, k, v, bg, inv_known=None):
    lane = lax.broadcasted_iota(jnp.int32, (1, 128), 1)
    ri = lax.broadcasted_iota(jnp.int32, (C_CHUNK, C_CHUNK), 0)
    ci = lax.broadcasted_iota(jnp.int32, (C_CHUNK, C_CHUNK), 1)
    gc_t = _hdot((ri >= ci).astype(F32), bg)
    beta = jnp.stack([jnp.sum(jnp.where(lane == h, bg, 0.0), axis=-1, keepdims=True) for h in range(C_HEADS)], axis=0)
    gc = jnp.stack([jnp.sum(jnp.where(lane == C_HEADS + h, gc_t, 0.0), axis=-1, keepdims=True) for h in range(C_HEADS)], axis=0)
    gc_rows = gc_t.T
    row_id = lax.broadcasted_iota(jnp.int32, (128, C_CHUNK), 0)
    gcj = jnp.stack([jnp.sum(jnp.where(row_id == C_HEADS + h, gc_rows, 0.0), axis=0, keepdims=True) for h in range(C_HEADS)], axis=0)
    tril, strict = (ri >= ci)[None], (ri > ci)[None]
    decay = jnp.exp(jnp.where(tril, gc - gcj, -1e30))
    kb = k * beta
    lower = jnp.where(strict, _bein('hid,hjd->hij', kb, k) * decay, 0.0)
    inv = _unit_lower_inverse(lower) if inv_known is None else _known_inverse(lower, inv_known)
    egc = jnp.exp(gc)
    u_c = _tein('hij,hjd->hid', inv, v * beta)
    w_c = _tein('hij,hjd->hid', inv, kb * egc)
    aqk = _bein('hid,hjd->hij', q, k) * decay
    rowi = lax.broadcasted_iota(jnp.int32, (1, C_CHUNK, 1), 1)
    g_last = jnp.sum(jnp.where(rowi == C_CHUNK - 1, gc, 0.0), axis=1, keepdims=True)
    kd = k * jnp.exp(g_last - gc)
    return (u_c, w_c, aqk, q * egc, kd, jnp.exp(g_last)), inv


def _gdn_state(local, state):
    u_c, w_c, aqk, qg, kd, dec = local
    v_new = u_c - _bein('hik,hkv->hiv', w_c, state)
    o = _bein('hik,hkv->hiv', qg, state) + _bein('hij,hjv->hiv', aqk, v_new)
    return o, state * dec + _bein('hik,hiv->hkv', kd, v_new)


C_SUB = 2


def _gdn_group(qs, ks, vs, bgs, state, invs_known=None):
    outs, invs = [], []
    locals_ = [_gdn_local(qs[s], ks[s], vs[s], bgs[s], None if invs_known is None else invs_known[s]) for s in range(C_SUB)]
    for local, inv in locals_:
        o, state = _gdn_state(local, state)
        outs.append(o)
        invs.append(inv)
    return outs, state, invs


def _heads(ref, sub):
    rows = slice(sub * C_CHUNK, (sub + 1) * C_CHUNK)
    return jnp.stack([ref[rows, h * C_DK:(h + 1) * C_DK] for h in range(C_HEADS)], axis=0)


def _all_heads(ref):
    return [_heads(ref, s) for s in range(C_SUB)]


def _put_heads(ref, sub, val):
    rows = slice(sub * C_CHUNK, (sub + 1) * C_CHUNK)
    for h in range(C_HEADS):
        ref[rows, h * C_DK:(h + 1) * C_DK] = val[h]


def _gdn_fwd(qkv, bg):
    s_len = qkv.shape[0]
    rows = C_SUB * C_CHUNK
    n_g = s_len // rows

    def body(q_ref, k_ref, v_ref, bg_ref, o_ref, ss_ref, inv_ref, st_ref):
        _zero_at_first([st_ref])
        s0 = st_ref[...]
        ss_ref[0] = s0
        bgs = [bg_ref[s * C_CHUNK:(s + 1) * C_CHUNK, :] for s in range(C_SUB)]
        outs, s2, invs = _gdn_group(_all_heads(q_ref), _all_heads(k_ref), _all_heads(v_ref), bgs, s0)
        st_ref[...] = s2
        for s in range(C_SUB):
            inv_ref[0, s] = invs[s]
            _put_heads(o_ref, s, outs[s])

    col = lambda c: pl.BlockSpec((rows, 1024), lambda i: (i, c))
    return pl.pallas_call(
        body, name="gdn_fwd", grid=(n_g,),
        in_specs=[col(0), col(1), col(2), _row(rows, 128)],
        out_specs=[_row(rows, 1024), pl.BlockSpec((1, C_HEADS, C_DK, C_DK), lambda i: (i, 0, 0, 0)),
                   pl.BlockSpec((1, C_SUB, C_HEADS, C_CHUNK, C_CHUNK), lambda i: (i, 0, 0, 0, 0))],
        out_shape=[_sds(s_len, 1024), _sds(n_g, C_HEADS, C_DK, C_DK), _sds(n_g, C_SUB, C_HEADS, C_CHUNK, C_CHUNK)],
        scratch_shapes=[pltpu.VMEM((C_HEADS, C_DK, C_DK), F32)],
        compiler_params=_cp("arbitrary"),
    )(qkv, qkv, qkv, bg)


def _gdn_bwd(qkv, bg, states, invs, do):
    s_len = qkv.shape[0]
    rows = C_SUB * C_CHUNK
    n_g = s_len // rows

    def body(q_ref, k_ref, v_ref, bg_ref, ss_ref, inv_ref, do_ref, dq_ref, dk_ref, dv_ref, dbg_ref, ds_ref):
        _zero_at_first([ds_ref])
        invs_known = [inv_ref[0, s] for s in range(C_SUB)]

        def group(qs, ks, vs, bgs, st):
            outs, st2, _ = _gdn_group(qs, ks, vs, bgs, st, invs_known)
            return outs, st2

        bgs = [bg_ref[s * C_CHUNK:(s + 1) * C_CHUNK, :] for s in range(C_SUB)]
        _, vjp = jax.vjp(group, _all_heads(q_ref), _all_heads(k_ref), _all_heads(v_ref), bgs, ss_ref[0])
        dqs, dks, dvs, dbgs, ds = vjp((_all_heads(do_ref), ds_ref[...]))
        ds_ref[...] = ds
        for s in range(C_SUB):
            dbg_ref[s * C_CHUNK:(s + 1) * C_CHUNK, :] = dbgs[s]
            _put_heads(dq_ref, s, dqs[s])
            _put_heads(dk_ref, s, dks[s])
            _put_heads(dv_ref, s, dvs[s])

    rev = lambda i: n_g - 1 - i
    col = lambda c: pl.BlockSpec((rows, 1024), lambda i: (rev(i), c))
    row128 = pl.BlockSpec((rows, 128), lambda i: (rev(i), 0))
    return pl.pallas_call(
        body, name="gdn_bwd", grid=(n_g,),
        in_specs=[col(0), col(1), col(2), row128, pl.BlockSpec((1, C_HEADS, C_DK, C_DK), lambda i: (rev(i), 0, 0, 0)),
                  pl.BlockSpec((1, C_SUB, C_HEADS, C_CHUNK, C_CHUNK), lambda i: (rev(i), 0, 0, 0, 0)), col(0)],
        out_specs=[col(0), col(0), col(0), row128],
        out_shape=[_sds(s_len, 1024)] * 3 + [_sds(s_len, 128)],
        scratch_shapes=[pltpu.VMEM((C_HEADS, C_DK, C_DK), F32)],
        compiler_params=_cp("arbitrary"),
    )(qkv, qkv, qkv, bg, states, invs, do)


def _head_norm_gate(o, gate, norm_g):
    return (_rms(o) * norm_g) * jax.nn.silu(gate)


def _l1_out_fb(o, gate_c, x1, target, norm_g, w_out, post_g, gate):
    s_len = x1.shape[0]

    def body(o_ref, gc_ref, x1_ref, t_ref, ng_ref, w_ref, pg_ref, gt_ref,
             loss_ref, dres_ref, do_ref, dgc_ref, dw_ref, dng_ref, dpg_ref, dgt_ref):
        _zero_at_first([loss_ref, dw_ref, dng_ref, dpg_ref, dgt_ref])
        ng = ng_ref[...]
        ons, vjps = [], []
        for h in range(C_HEADS):
            sl = slice(h * C_DK, (h + 1) * C_DK)
            on, vjp_h = jax.vjp(_head_norm_gate, o_ref[:, sl], gc_ref[:, sl], ng)
            ons.append(on)
            vjps.append(vjp_h)
        on_all = jnp.concatenate(ons, axis=-1)
        y = _bdot(on_all, w_ref[...])
        x2, vjp2 = jax.vjp(_post_res, y, x1_ref[...], pg_ref[...], gt_ref[...])
        err = x2 - t_ref[...]
        _acc(loss_ref, jnp.full((1, 128), 0.5 * jnp.sum(jnp.mean(err * err, axis=-1)), F32))
        dx2 = err * (1.0 / D_MODEL)
        dy, _, dpg, dgt = vjp2(dx2)
        dres_ref[...] = dx2
        _acc(dpg_ref, dpg)
        _acc(dgt_ref, dgt)
        dw_ref[...] += _bdot_tn(on_all, dy)
        don = _bdot_nt(dy, w_ref[...])
        for h in range(C_HEADS):
            sl = slice(h * C_DK, (h + 1) * C_DK)
            do_h, dgc_h, dng = vjps[h](don[:, sl])
            do_ref[:, sl] = do_h
            dgc_ref[:, sl] = dgc_h
            _acc(dng_ref, dng)

    vec, r10 = _fix((1, D_MODEL)), _row(TM, D_MODEL)
    row128 = _fix((1, 128))
    return pl.pallas_call(
        body, name="l1_out_fb", grid=(s_len // TM,),
        in_specs=[r10, r10, r10, r10, row128, _fix((D_MODEL, D_MODEL)), vec, vec],
        out_specs=[row128, r10, r10, r10, _fix((D_MODEL, D_MODEL)), row128, vec, vec],
        out_shape=[_sds(1, 128), _sds(s_len, D_MODEL), _sds(s_len, D_MODEL), _sds(s_len, D_MODEL),
                   _sds(D_MODEL, D_MODEL), _sds(1, 128), _sds(1, D_MODEL), _sds(1, D_MODEL)],
        compiler_params=_cp("arbitrary"),
    )(o, gate_c, x1, target, norm_g, w_out, post_g, gate)


def _row_of(v, width, at):
    return jnp.zeros((1, width), F32).at[0, at:at + v.shape[-1]].set(v.reshape(-1))


def _local_step(x, target, mod, wd, comm=None):
    s_len = x.shape[0]
    shift0, scale0, gate0 = (mod[0:1, i * 1024:(i + 1) * 1024] for i in range(3))
    shift1, scale1, gate1 = (mod[1:2, i * 1024:(i + 1) * 1024] for i in range(3))
    pre_g0, pre_g1 = wd["pre_g"][0:1], wd["pre_g"][1:2]
    post_g0, post_g1 = wd["post_g"][0:1], wd["post_g"][1:2]
    w_in0 = wd["ab_w_in"].astype(BF)
    d_skip, glu_b = wd["s5_d"].reshape(1, 512), wd["s5_glu_b"].reshape(1, 512)
    norm_g = wd["gdn_norm_g"].reshape(1, 128)
    alog_row = _row_of(wd["gdn_a_log"], 128, C_HEADS)
    dtb_row = _row_of(wd["gdn_dt_bias"], 128, C_HEADS)
    conv_w = wd["gdn_conv"]

    a_re, a_im = wd["s5_a_re"], wd["s5_a_im"]
    log_dt = wd["s5_log_dt"].reshape(B_GROUPS, 1)
    bt_re = wd["s5_b_re"].transpose(0, 2, 1).reshape(B_WIDTH, B_STATE)
    bt_im = wd["s5_b_im"].transpose(0, 2, 1).reshape(B_WIDTH, B_STATE)
    abar_r, abar_i, bbar_r, bbar_i = _s5_params(a_re, a_im, log_dt, bt_re, bt_im)
    abr, abi = abar_r.reshape(1, -1), abar_i.reshape(1, -1)
    btr, bti = _blockdiag_b(bbar_r).astype(BF), _blockdiag_b(bbar_i).astype(BF)
    ctr, cti = _blockdiag_c(wd["s5_c_re"]).astype(BF), _blockdiag_c(wd["s5_c_im"]).astype(BF)

    table = _bucket_table()
    biases = _attn_bias(wd["rel_bias"], jnp.asarray(table))
    front = _l0_front(x, pre_g0, scale0, shift0, w_in0)
    qs, ks, vs = front[0:3], front[3:6], front[6:9]
    u, ga, gb, h0 = front[9:]
    os, ls = zip(*[_attn_fwd(qs[i], ks[i], vs[i], biases[i]) for i in range(3)])
    (xr, xi, ypre), late = _s5_scan_fwd(u, btr, bti, ctr, cti, abr, abi, exchange=None if comm is None else (comm.late_halves(), False))
    if comm is not None:
        wd = {**wd, **comm.late_weights(late)}
    w_out0 = wd["ab_w_out"].astype(BF)
    glu_w = wd["s5_glu_w"].astype(BF)
    w_in1 = jnp.concatenate([wd["gdn_w_in"], jnp.zeros((D_MODEL, C_IN_PAD - wd["gdn_w_in"].shape[1]), wd["gdn_w_in"].dtype)], axis=1).astype(BF)
    w_out1 = wd["gdn_w_out"].astype(BF)
    x1, y0 = _l0_out(os, ls, ga, gb, ypre, u, x, d_skip, glu_w, glu_b, w_out0, post_g0, gate0)

    raw, gate_c, ba, h1 = _l1_front(x1, pre_g1, scale1, shift1, w_in1)
    qkv, bg = _gdn_prep(raw, ba, conv_w, alog_row, dtb_row)
    o_gdn, states, invs = _gdn_fwd(qkv, bg)
    loss_row, dres1, do_gdn, dgate_c, dw_out1, dnorm_g, dpost_g1, dgate1 = _l1_out_fb(
        o_gdn, gate_c, x1, target, norm_g, w_out1, post_g1, gate1)

    dq1, dk1, dv1, dbg = _gdn_bwd(qkv, bg, states, invs, do_gdn)
    draw, dba, dconv_w, dalog_row, ddtb_row = _gdn_prep_bwd(raw, ba, conv_w, alog_row, dtb_row, dq1, dk1, dv1, dbg)
    dz1, dx1, dpre_g1, dscale1, dshift1 = _front_bwd(
        "l1_front_bwd", x1, pre_g1, scale1, shift1, w_in1, dres1, [[draw], [dgate_c], [dba]], [QKV, 1024, 128])
    dw_in1 = _matmul_tn("l1_dw_in", h1, dz1, 1408)

    n_w1 = wd["gdn_w_in"].shape[1]
    ex1 = None if comm is None else (comm.chip_partials("l1", {"gdn_w_in": dw_in1[:, :n_w1], "gdn_w_out": dw_out1}), True)
    l0b, got1 = _l0_out_bwd(os, ls, ga, gb, ypre, u, x, y0, d_skip, glu_w, glu_b, w_out0, post_g0, gate0, dx1, exchange=ex1)
    dos, dls = l0b[0:3], l0b[3:6]
    dga, dgb, dypre, du_skip, dd_skip, dglu_w, dglu_b, dw_out0, dpost_g0, dgate0 = l0b[6:]
    ex2 = None if comm is None else (comm.chip_partials("l0_out", {"ab_w_out": dw_out0, "s5_glu_w": dglu_w}), True)
    (du_scan, dbtr, dbti, dctr, dcti, dabr, dabi), got2 = _s5_scan_bwd(dypre, xr, xi, u, btr, bti, ctr, cti, abr, abi, exchange=ex2)
    if comm is not None:
        comm.received.update(zip(("gdn_w_in", "gdn_w_out", "ab_w_out", "s5_glu_w"), list(got1) + list(got2)))
    dqs, dks, dvs, dbs = [], [], [], []
    for i in range(3):
        dq_d, dk_d, dv_d, db_d = _attn_bwd(qs[i], ks[i], vs[i], biases[i], os[i], ls[i], dos[i], dls[i])
        dqs.append(dq_d)
        dks.append(dk_d)
        dvs.append(dv_d)
        dbs.append(db_d)
    parts = [dqs, dks, dvs, [du_skip, du_scan], [dga], [dgb]]
    dz0, grad_x, dpre_g0, dscale0, dshift0 = _front_bwd(
        "l0_front_bwd", x, pre_g0, scale0, shift0, w_in0, dx1, parts, [512] * 6)
    dw_in0 = _matmul_tn("l0_dw_in", h0, dz0, 768)

    idx_rows = jnp.asarray(table.reshape(3, -1), F32)
    drel = _rel_bias_grad(dbs, idx_rows).T
    da_re, da_im, dlog_dt, dbt_re, dbt_im = _s5_params_bwd(
        a_re, a_im, log_dt, bt_re, bt_im, dabr.reshape(B_GROUPS, B_STATE), dabi.reshape(B_GROUPS, B_STATE),
        _blockdiag_b_t(dbtr), _blockdiag_b_t(dbti))
    unb = lambda d: d.reshape(B_GROUPS, B_GROUP, B_STATE).transpose(0, 2, 1)
    grads = {
        "pre_g": jnp.concatenate([dpre_g0, dpre_g1], 0), "post_g": jnp.concatenate([dpost_g0, dpost_g1], 0),
        "rel_bias": drel, "ab_w_in": dw_in0, "ab_w_out": dw_out0,
        "s5_a_re": da_re, "s5_a_im": da_im, "s5_log_dt": dlog_dt.reshape(B_GROUPS),
        "s5_b_re": unb(dbt_re), "s5_b_im": unb(dbt_im),
        "s5_c_re": _blockdiag_c_t(dctr), "s5_c_im": _blockdiag_c_t(dcti),
        "s5_d": dd_skip.reshape(512), "s5_glu_w": dglu_w, "s5_glu_b": dglu_b.reshape(512),
        "gdn_w_in": dw_in1[:, :wd["gdn_w_in"].shape[1]], "gdn_conv": dconv_w,
        "gdn_a_log": dalog_row[0, C_HEADS:2 * C_HEADS], "gdn_dt_bias": ddtb_row[0, C_HEADS:2 * C_HEADS],
        "gdn_norm_g": dnorm_g.reshape(128), "gdn_w_out": dw_out1,
    }
    dmod = jnp.concatenate([jnp.concatenate([dshift0, dscale0, dgate0], 1), jnp.concatenate([dshift1, dscale1, dgate1], 1)], 0)
    return loss_row[0, 0], grad_x, grads, dmod


def _place():
    return lax.axis_index("x"), lax.axis_index("y"), lax.axis_index("c")


def _flip(v, bit):
    return 1 - v if bit else v


def _hbm_call(name, body, arrs, out_shapes, n_sem):
    any_spec = pl.BlockSpec(memory_space=pl.ANY)
    return pl.pallas_call(
        body, name=name,
        in_specs=[any_spec] * len(arrs), out_specs=[any_spec] * len(out_shapes), out_shape=out_shapes,
        scratch_shapes=[pltpu.SemaphoreType.DMA((n_sem,)), pltpu.SemaphoreType.DMA((n_sem,))],
    )(*arrs)


def _own_slot(gathered, own, slot):
    idx = lax.broadcasted_iota(jnp.int32, (gathered.shape[0],) + (1,) * own.ndim, 0)
    return jnp.where(idx == slot, own[None], gathered)


def _all_gather8(name, arr):
    def body(x_ref, out_ref, send_sems, recv_sems):
        x, y, c = _place()
        me = 4 * x + 2 * y + c
        sends, recvs = [], []
        for m in range(1, 8):
            peer = (_flip(x, m & 4), _flip(y, m & 2), _flip(c, m & 1))
            sends.append(pltpu.make_async_remote_copy(x_ref, out_ref.at[me], send_sems.at[m - 1], recv_sems.at[m - 1],
                                                      device_id=peer, device_id_type=MESH))
            recvs.append(pltpu.make_async_remote_copy(x_ref, out_ref.at[4 * peer[0] + 2 * peer[1] + peer[2]], send_sems.at[m - 1],
                                                      recv_sems.at[m - 1], device_id=peer, device_id_type=MESH))
        for cp in sends:
            cp.start()
        for cp in recvs:
            cp.wait_recv()
        for cp in sends:
            cp.wait_send()

    return _hbm_call(name, body, [arr], [jax.ShapeDtypeStruct((8,) + arr.shape, arr.dtype)], 7)[0]


def _all_to_all8(name, arr):
    def body(x_ref, out_ref, send_sems, recv_sems):
        x, y, c = _place()
        me = 4 * x + 2 * y + c
        sends, recvs = [], []
        for m in range(1, 8):
            peer = (_flip(x, m & 4), _flip(y, m & 2), _flip(c, m & 1))
            peer_id = 4 * peer[0] + 2 * peer[1] + peer[2]
            sends.append(pltpu.make_async_remote_copy(x_ref.at[peer_id], out_ref.at[me], send_sems.at[m - 1], recv_sems.at[m - 1],
                                                      device_id=peer, device_id_type=MESH))
            recvs.append(pltpu.make_async_remote_copy(x_ref.at[peer_id], out_ref.at[peer_id], send_sems.at[m - 1], recv_sems.at[m - 1],
                                                      device_id=peer, device_id_type=MESH))
        for cp in sends:
            cp.start()
        for cp in recvs:
            cp.wait_recv()
        for cp in sends:
            cp.wait_send()

    return _hbm_call(name, body, [arr], [jax.ShapeDtypeStruct(arr.shape, arr.dtype)], 7)[0]


def _chip_copies(ins, outs, send_sems, recv_sems, scatter):
    x, y, c = _place()
    mine = 2 * x + y
    sends, recvs = [], []
    for a in range(len(ins)):
        for m in range(1, 4):
            px, py = _flip(x, m & 2), _flip(y, m & 1)
            k = 3 * a + m - 1
            src = ins[a].at[2 * px + py] if scatter else ins[a]
            sends.append(pltpu.make_async_remote_copy(src, outs[a].at[mine], send_sems.at[k], recv_sems.at[k],
                                                      device_id=(px, py, c), device_id_type=MESH))
            recvs.append(pltpu.make_async_remote_copy(src, outs[a].at[2 * px + py], send_sems.at[k], recv_sems.at[k],
                                                      device_id=(px, py, c), device_id_type=MESH))
    return sends, recvs


def _chip_shapes(arrs, scatter):
    return [jax.ShapeDtypeStruct(a.shape if scatter else (4,) + a.shape, a.dtype) for a in arrs]


def _chip_exchange(name, arrs, scatter):
    n = len(arrs)

    def body(*refs):
        sends, recvs = _chip_copies(refs[:n], refs[n:2 * n], refs[2 * n], refs[2 * n + 1], scatter)
        for cp in sends:
            cp.start()
        for cp in recvs:
            cp.wait_recv()
        for cp in sends:
            cp.wait_send()

    return _hbm_call(name, body, arrs, _chip_shapes(arrs, scatter), 3 * n)


def _call_with_exchange(body, name, grid, in_specs, out_specs, out_shape, scratch_shapes, args, exchange):
    if exchange is None:
        return pl.pallas_call(body, name=name, grid=grid, in_specs=in_specs, out_specs=out_specs, out_shape=out_shape,
                              scratch_shapes=scratch_shapes, compiler_params=_cp(*["arbitrary"] * len(grid)))(*args), []
    arrs, scatter = exchange
    n_in, n_out, n_ex, n_scr = len(in_specs), len(out_specs), len(arrs), len(scratch_shapes)

    def fused(*refs):
        ins, ex_in = refs[:n_in], refs[n_in:n_in + n_ex]
        outs, ex_out = refs[n_in + n_ex:n_in + n_ex + n_out], refs[n_in + n_ex + n_out:n_in + 2 * n_ex + n_out]
        rest = refs[n_in + 2 * n_ex + n_out:]
        sends, recvs = _chip_copies(ex_in, ex_out, rest[n_scr], rest[n_scr + 1], scatter)
        first, last = pl.program_id(0) == 0, pl.program_id(0) == grid[0] - 1
        for k in range(1, len(grid)):
            first, last = first & (pl.program_id(k) == 0), last & (pl.program_id(k) == grid[k] - 1)

        @pl.when(first)
        def _():
            for cp in sends:
                cp.start()

        body(*ins, *outs, *rest[:n_scr])

        @pl.when(last)
        def _():
            for cp in recvs:
                cp.wait_recv()
            for cp in sends:
                cp.wait_send()

    any_spec = pl.BlockSpec(memory_space=pl.ANY)
    res = pl.pallas_call(
        fused, name=name, grid=grid, in_specs=list(in_specs) + [any_spec] * n_ex, out_specs=list(out_specs) + [any_spec] * n_ex,
        out_shape=list(out_shape) + _chip_shapes(arrs, scatter),
        scratch_shapes=list(scratch_shapes) + [pltpu.SemaphoreType.DMA((3 * n_ex,))] * 2,
        compiler_params=_cp(*["arbitrary"] * len(grid)))(*args, *arrs)
    return res[:n_out], res[n_out:]


def _sibling_exchange(name, arrs):
    n = len(arrs)

    def body(*refs):
        ins, outs = refs[:n], refs[n:2 * n]
        send_sems, recv_sems = refs[2 * n:]
        x, y, c = _place()
        copies = [pltpu.make_async_remote_copy(ins[a], outs[a], send_sems.at[a], recv_sems.at[a],
                                               device_id=(x, y, 1 - c), device_id_type=MESH) for a in range(n)]
        for cp in copies:
            cp.start()
        for cp in copies:
            cp.wait_recv()
        for cp in copies:
            cp.wait_send()

    return _hbm_call(name, body, arrs, [jax.ShapeDtypeStruct(a.shape, a.dtype) for a in arrs], n)


def _row_tile(rows):
    for t in (256, 128, 64, 32, 16, 8):
        if rows % t == 0:
            return t
    return rows


def _pair_sum(name, a, b, out_dtype):
    rows, cols = a.shape
    tr = _row_tile(rows)

    def body(a_ref, b_ref, o_ref):
        o_ref[...] = (a_ref[...] + b_ref[...]).astype(out_dtype)

    return pl.pallas_call(body, name=name, grid=(rows // tr,), in_specs=[_row(tr, cols)] * 2, out_specs=_row(tr, cols),
                          out_shape=_sds(rows, cols, dtype=out_dtype), compiler_params=_cp("arbitrary"))(a, b)


def _chip_sum(name, recv, partial, mine):
    n, rows, cols = recv.shape
    tr = _row_tile(rows)

    def body(mine_ref, *refs):
        own = refs[n][0].astype(F32)
        acc = None
        for s in range(n):
            term = jnp.where(mine_ref[0] == s, own, refs[s][0].astype(F32))
            acc = term if acc is None else acc + term
        refs[-1][...] = acc

    def slot_spec(s):
        return pl.BlockSpec((1, tr, cols), lambda i, m: (jnp.where(m[0] == s, (s + 1) % n, s), i, 0))

    grid_spec = pltpu.PrefetchScalarGridSpec(
        num_scalar_prefetch=1, grid=(rows // tr,),
        in_specs=[slot_spec(s) for s in range(n)] + [pl.BlockSpec((1, tr, cols), lambda i, m: (m[0], i, 0))],
        out_specs=pl.BlockSpec((tr, cols), lambda i, m: (i, 0)))
    return pl.pallas_call(body, name=name, grid_spec=grid_spec, out_shape=_sds(rows, cols),
                          compiler_params=_cp("arbitrary"))(mine, *([recv] * n), partial)


def _slot_sum(name, arr):
    n, rows, cols = arr.shape
    tr = _row_tile(rows)

    def body(*refs):
        acc = refs[0][0]
        for r in refs[1:-1]:
            acc = acc + r[0]
        refs[-1][...] = acc

    specs = [pl.BlockSpec((1, tr, cols), functools.partial(lambda s, i: (s, i, 0), s)) for s in range(n)]
    return pl.pallas_call(body, name=name, grid=(rows // tr,), in_specs=specs, out_specs=_row(tr, cols),
                          out_shape=_sds(rows, cols), compiler_params=_cp("arbitrary"))(*([arr] * n))


def _adamw(name, w, g, m, v):
    rows, cols = w.shape
    tr = _row_tile(rows)

    def body(w_ref, g_ref, m_ref, v_ref, d_ref, nm_ref, nv_ref):
        g_ = g_ref[...]
        m_ = ADAM_B1 * m_ref[...] + (1.0 - ADAM_B1) * g_
        v_ = ADAM_B2 * v_ref[...] + (1.0 - ADAM_B2) * (g_ * g_)
        m_hat = m_ / (1.0 - ADAM_B1 ** ADAM_STEP)
        v_hat = v_ / (1.0 - ADAM_B2 ** ADAM_STEP)
        d_ref[...] = -ADAM_LR * (m_hat / (jnp.sqrt(v_hat) + ADAM_EPS) + ADAM_WD * w_ref[...])
        nm_ref[...] = m_
        nv_ref[...] = v_

    spec = _row(tr, cols)
    return pl.pallas_call(body, name=name, grid=(rows // tr,), in_specs=[spec] * 4, out_specs=[spec] * 3,
                          out_shape=[_sds(rows, cols)] * 3, compiler_params=_cp("arbitrary"))(w, g, m, v)


def _adamw_halves(name, w, g_mine, g_sibling, m, v, core):
    _, rows, cols = w.shape
    half = rows // 2
    tr = _row_tile(half)
    per_half = half // tr

    def body(core_ref, w_ref, gm_ref, gs_ref, m_ref, v_ref, g_ref, d_ref, nm_ref, nv_ref):
        g_ = jnp.where(pl.program_id(0) // per_half == core_ref[0], gm_ref[...], gs_ref[...])
        m_ = ADAM_B1 * m_ref[...] + (1.0 - ADAM_B1) * g_
        v_ = ADAM_B2 * v_ref[...] + (1.0 - ADAM_B2) * (g_ * g_)
        m_hat = m_ / (1.0 - ADAM_B1 ** ADAM_STEP)
        v_hat = v_ / (1.0 - ADAM_B2 ** ADAM_STEP)
        g_ref[...] = g_
        d_ref[...] = -ADAM_LR * (m_hat / (jnp.sqrt(v_hat) + ADAM_EPS) + ADAM_WD * w_ref[...])
        nm_ref[...] = m_
        nv_ref[...] = v_

    full = pl.BlockSpec((None, tr, cols), lambda i, c: (0, i, 0))
    in_half = pl.BlockSpec((tr, cols), lambda i, c: (i % per_half, 0))
    grid_spec = pltpu.PrefetchScalarGridSpec(num_scalar_prefetch=1, grid=(rows // tr,),
                                             in_specs=[full, in_half, in_half, full, full], out_specs=[full] * 4)
    return pl.pallas_call(body, name=name, grid_spec=grid_spec, out_shape=[_sds(1, rows, cols)] * 4,
                          compiler_params=_cp("arbitrary"))(core, w, g_mine, g_sibling, m, v)


def _mod_local(c_all, ada_w):
    def body(c_ref, w_ref, o_ref):
        c_act = jax.nn.silu(c_ref[...])
        for l in range(2):
            o_ref[l] = _hdot(c_act, w_ref[l])

    return pl.pallas_call(body, name="mod_local", out_shape=_sds(2, 8, ada_w.shape[2]),
                          compiler_params=pltpu.CompilerParams(vmem_limit_bytes=VMEM_LIMIT_BYTES))(c_all, ada_w)


def _ada_w_grad(c_all, dmod_cols):
    def body(c_ref, d_ref, o_ref):
        c_act = jax.nn.silu(c_ref[...])
        for l in range(2):
            o_ref[l] = lax.dot_general(c_act, d_ref[l], (((0,), (0,)), ((), ())), precision=HI, preferred_element_type=F32)

    return pl.pallas_call(body, name="ada_w_grad", out_shape=_sds(2, D_MODEL, dmod_cols.shape[2]),
                          compiler_params=pltpu.CompilerParams(vmem_limit_bytes=VMEM_LIMIT_BYTES))(c_all, dmod_cols)


_SMALL = ("ada_b", "pre_g", "post_g", "rel_bias", "s5_a_re", "s5_a_im", "s5_log_dt", "s5_b_re", "s5_b_im", "s5_c_re", "s5_c_im",
          "s5_d", "s5_glu_b", "gdn_a_log", "gdn_dt_bias", "gdn_norm_g")
_SHARDED = ("ab_w_in", "ab_w_out", "s5_glu_w", "gdn_w_in", "gdn_w_out")
_COL_SHARDED = ("ab_w_in", "gdn_w_in")
_WEIGHTS = ("ada_w", "ada_b", "pre_g", "post_g", "rel_bias", "ab_w_in", "ab_w_out", "s5_a_re", "s5_a_im", "s5_log_dt", "s5_b_re",
            "s5_b_im", "s5_c_re", "s5_c_im", "s5_d", "s5_glu_w", "s5_glu_b", "gdn_w_in", "gdn_conv", "gdn_a_log", "gdn_dt_bias",
            "gdn_norm_g", "gdn_w_out")


def _rows128(n):
    return -(-n // 128)


def _pack(arrs, total_rows):
    pieces = []
    for a in arrs:
        flat = a.reshape(-1)
        pieces.append(jnp.pad(flat, (0, _rows128(flat.shape[0]) * 128 - flat.shape[0])).reshape(-1, 128))
    used = sum(p.shape[0] for p in pieces)
    pieces.append(jnp.zeros((total_rows - used, 128), F32))
    return jnp.concatenate(pieces, axis=0)


def _unpack(buf, shapes):
    out, at = [], 0
    for shp in shapes:
        n = int(np.prod(shp))
        out.append(buf[at:at + _rows128(n)].reshape(-1)[:n].reshape(shp))
        at += _rows128(n)
    return out


def _full_from_halves(name, g):
    if name in _COL_SHARDED:
        return g.transpose(0, 2, 1, 3).reshape(2 * g.shape[2], 4 * g.shape[3])
    return g.transpose(1, 0, 2, 3).reshape(8 * g.shape[2], g.shape[3])


def _shard_major(name, g):
    if name in _COL_SHARDED:
        return g.reshape(g.shape[0], 4, g.shape[1] // 4).transpose(1, 0, 2)
    return g.reshape(4, g.shape[0] // 4, g.shape[1])


_LATE = ("ab_w_out", "s5_glu_w", "gdn_w_in", "gdn_w_out")


class _WeightExchanges:
    def __init__(self, shards, core, chip):
        self.core, self.chip = core, chip
        self.half = {}
        for name, shard in shards.items():
            h = shard.shape[0] // 2
            self.half[name] = lax.dynamic_slice_in_dim(shard.astype(BF), core * h, h, axis=0)
        self.partial, self.received = {}, {}

    def _full(self, label, names, from_chips):
        mine = [_own_slot(g, self.half[n], self.chip) for n, g in zip(names, from_chips)]
        theirs = _sibling_exchange("gather_w_sibling_" + label, mine)
        return {n: _full_from_halves(n, jnp.where(self.core == 0, jnp.stack([a, b], 0), jnp.stack([b, a], 0)))
                for n, a, b in zip(names, mine, theirs)}

    def first_weights(self):
        return self._full("first", ["ab_w_in"], _chip_exchange("gather_w_chips", [self.half["ab_w_in"]], False))

    def late_halves(self):
        return [self.half[n] for n in _LATE]

    def late_weights(self, from_chips):
        return self._full("late", list(_LATE), from_chips)

    def chip_partials(self, label, grads):
        mine, other = [], []
        for name, g in grads.items():
            sm = _shard_major(name, g)
            h = sm.shape[1] // 2
            mine.append(lax.dynamic_slice_in_dim(sm, self.core * h, h, axis=1))
            other.append(lax.dynamic_slice_in_dim(sm, (1 - self.core) * h, h, axis=1))
        out = []
        for name, a, b in zip(grads, mine, _sibling_exchange("reduce_sibling_" + label, other)):
            flat = lambda t: t.reshape(-1, t.shape[-1])
            self.partial[name] = _pair_sum("sum_sibling_" + name, flat(a), flat(b), BF).reshape(a.shape)
            out.append(self.partial[name])
        return out


def kernel(x, c, ada_w, ada_b, pre_g, post_g, rel_bias, ab_w_in, ab_w_out, s5_a_re, s5_a_im, s5_log_dt, s5_b_re, s5_b_im, s5_c_re, s5_c_im, s5_d, s5_glu_w, s5_glu_b, gdn_w_in, gdn_conv, gdn_a_log, gdn_dt_bias, gdn_norm_g, gdn_w_out, loss_target, m_ada_w, m_ada_b, m_pre_g, m_post_g, m_rel_bias, m_ab_w_in, m_ab_w_out, m_s5_a_re, m_s5_a_im, m_s5_log_dt, m_s5_b_re, m_s5_b_im, m_s5_c_re, m_s5_c_im, m_s5_d, m_s5_glu_w, m_s5_glu_b, m_gdn_w_in, m_gdn_conv, m_gdn_a_log, m_gdn_dt_bias, m_gdn_norm_g, m_gdn_w_out, v_ada_w, v_ada_b, v_pre_g, v_post_g, v_rel_bias, v_ab_w_in, v_ab_w_out, v_s5_a_re, v_s5_a_im, v_s5_log_dt, v_s5_b_re, v_s5_b_im, v_s5_c_re, v_s5_c_im, v_s5_d, v_s5_glu_w, v_s5_glu_b, v_gdn_w_in, v_gdn_conv, v_gdn_a_log, v_gdn_dt_bias, v_gdn_norm_g, v_gdn_w_out):
    w = dict(ada_w=ada_w, ada_b=ada_b, pre_g=pre_g, post_g=post_g, rel_bias=rel_bias, ab_w_in=ab_w_in, ab_w_out=ab_w_out,
             s5_a_re=s5_a_re, s5_a_im=s5_a_im, s5_log_dt=s5_log_dt, s5_b_re=s5_b_re, s5_b_im=s5_b_im, s5_c_re=s5_c_re, s5_c_im=s5_c_im,
             s5_d=s5_d, s5_glu_w=s5_glu_w, s5_glu_b=s5_glu_b, gdn_w_in=gdn_w_in, gdn_conv=gdn_conv, gdn_a_log=gdn_a_log,
             gdn_dt_bias=gdn_dt_bias, gdn_norm_g=gdn_norm_g, gdn_w_out=gdn_w_out)
    m = dict(ada_w=m_ada_w, ada_b=m_ada_b, pre_g=m_pre_g, post_g=m_post_g, rel_bias=m_rel_bias, ab_w_in=m_ab_w_in, ab_w_out=m_ab_w_out,
             s5_a_re=m_s5_a_re, s5_a_im=m_s5_a_im, s5_log_dt=m_s5_log_dt, s5_b_re=m_s5_b_re, s5_b_im=m_s5_b_im, s5_c_re=m_s5_c_re,
             s5_c_im=m_s5_c_im, s5_d=m_s5_d, s5_glu_w=m_s5_glu_w, s5_glu_b=m_s5_glu_b, gdn_w_in=m_gdn_w_in, gdn_conv=m_gdn_conv,
             gdn_a_log=m_gdn_a_log, gdn_dt_bias=m_gdn_dt_bias, gdn_norm_g=m_gdn_norm_g, gdn_w_out=m_gdn_w_out)
    v = dict(ada_w=v_ada_w, ada_b=v_ada_b, pre_g=v_pre_g, post_g=v_post_g, rel_bias=v_rel_bias, ab_w_in=v_ab_w_in, ab_w_out=v_ab_w_out,
             s5_a_re=v_s5_a_re, s5_a_im=v_s5_a_im, s5_log_dt=v_s5_log_dt, s5_b_re=v_s5_b_re, s5_b_im=v_s5_b_im, s5_c_re=v_s5_c_re,
             s5_c_im=v_s5_c_im, s5_d=v_s5_d, s5_glu_w=v_s5_glu_w, s5_glu_b=v_s5_glu_b, gdn_w_in=v_gdn_w_in, gdn_conv=v_gdn_conv,
             gdn_a_log=v_gdn_a_log, gdn_dt_bias=v_gdn_dt_bias, gdn_norm_g=v_gdn_norm_g, gdn_w_out=v_gdn_w_out)
    ix, iy, ic = _place()
    me = 4 * ix + 2 * iy + ic
    chip = 2 * ix + iy
    n_cols = ada_w.shape[2]

    mine_first = _pack([c, gdn_conv], 32)
    first = _own_slot(_all_gather8("gather_c_conv", mine_first), mine_first, me)
    c_all = first[:, 0:8].reshape(8, D_MODEL)
    conv_full = first[0::2, 8:32].reshape(4, C_CONV, n_cols).transpose(1, 0, 2).reshape(C_CONV, 4 * n_cols)
    mine_mod = _mod_local(c_all, ada_w)
    modl = _own_slot(_all_gather8("gather_mod", mine_mod), mine_mod, me)
    mod = lax.dynamic_index_in_dim(modl[0::2], me, axis=2, keepdims=False)
    mod = mod.transpose(1, 0, 2).reshape(2, 4 * n_cols) + ada_b

    comm = _WeightExchanges({name: w[name][0] for name in _SHARDED}, ic, chip)
    wd = {name: w[name] for name in _SMALL if name != "ada_b"}
    wd = {k: (a if k in ("pre_g", "post_g", "rel_bias") else a[0]) for k, a in wd.items()}
    wd["gdn_conv"] = conv_full
    wd.update(comm.first_weights())

    loss_local, grad_x, grads, dmod = _local_step(x[0], loss_target[0], mod, wd, comm)
    loss = lax.psum(loss_local, ("x", "y", "c"))

    small_shapes = [w[name].shape for name in _SMALL] + [(C_CONV, 4 * n_cols)]
    small_rows = -(-sum(_rows128(int(np.prod(s))) for s in small_shapes) // 64) * 64
    per_dev, dmod_rows = small_rows // 8, _rows128(2 * 3 * D_MODEL)
    partial = _pack([dmod] + [grads[name] for name in _SMALL[1:]] + [grads["gdn_conv"]], small_rows)
    outbound = jnp.concatenate([partial.reshape(8, per_dev, 128), jnp.broadcast_to(partial[None, :dmod_rows], (8, dmod_rows, 128))], axis=1)
    inbound = _own_slot(_all_to_all8("reduce_small_grads", outbound), lax.dynamic_index_in_dim(outbound, me, 0, keepdims=False), me)
    my_rows = _slot_sum("sum_small_grads", inbound[:, :per_dev])
    g_small = _own_slot(_all_gather8("gather_small_grads", my_rows), my_rows, me).reshape(small_rows, 128)
    g_list = _unpack(g_small, small_shapes)
    out_g, out_d, out_m, out_v = {}, {}, {}, {}

    def update(name, g2d):
        shp = w[name].shape
        two_d = lambda a: a.reshape(-1, shp[-1])
        d_, m_, v_ = _adamw("adamw_" + name, two_d(w[name]), g2d, two_d(m[name]), two_d(v[name]))
        out_g[name], out_d[name], out_m[name], out_v[name] = (a.reshape(shp) for a in (g2d, d_, m_, v_))

    for name, g in zip(_SMALL, g_list[:-1]):
        update(name, g.reshape(-1, g.shape[-1]))
    update("gdn_conv", lax.dynamic_slice_in_dim(g_list[-1], chip * n_cols, n_cols, axis=1))

    dmod_all = inbound[:, per_dev:].reshape(8, 2, 4, n_cols)
    dmod_cols = lax.dynamic_index_in_dim(dmod_all, chip, axis=2, keepdims=False).transpose(1, 0, 2)
    update("ada_w", _ada_w_grad(c_all, dmod_cols).reshape(-1, n_cols))

    comm.received["ab_w_in"] = _chip_exchange("reduce_chips", comm.chip_partials("l0_in", {"ab_w_in": grads["ab_w_in"]}), True)[0]
    chip_1 = jnp.reshape(chip, (1,)).astype(jnp.int32)
    core_1 = jnp.reshape(ic, (1,)).astype(jnp.int32)
    reduced = [_chip_sum("sum_chips_" + name, comm.received[name], comm.partial[name], chip_1) for name in _SHARDED]
    for name, g_mine, g_sib in zip(_SHARDED, reduced, _sibling_exchange("reduce_share", reduced)):
        out_g[name], out_d[name], out_m[name], out_v[name] = _adamw_halves(
            "adamw_" + name, w[name], g_mine, g_sib, m[name], v[name], core_1)

    return (loss, grad_x[None], *[out_g[n] for n in _WEIGHTS], *[out_d[n] for n in _WEIGHTS],
            *[out_m[n] for n in _WEIGHTS], *[out_v[n] for n in _WEIGHTS])
```

```python
import functools
import math

import numpy as np
import jax
import jax.numpy as jnp
from jax import lax
from jax.experimental import pallas as pl
from jax.experimental.pallas import tpu as pltpu

F32 = jnp.float32
BF = jnp.bfloat16
HI = lax.Precision.HIGHEST
MESH = pl.DeviceIdType.MESH

D_MODEL = 1024
EPS = 1e-6
A_HEADS, A_HD, A_WIDTH, A_BLOCK = 8, 64, 512, 128
DILATIONS = (1, 4, 16)
N_KEYS = 128
REL_BUCKETS, REL_MAX_DIST = 32, 2048
B_WIDTH, B_GROUP, B_GROUPS, B_STATE = 512, 16, 32, 64
S5_LANES = 512
S5_TILES = 4
S5_T = 256
C_HEADS, C_DK, C_CHUNK, C_CONV = 8, 128, 64, 4
QKV = 3072
C_IN_PAD = 4224
TM = 256
VMEM_LIMIT_BYTES = 56 * 1024 * 1024
ADAM_LR, ADAM_B1, ADAM_B2, ADAM_EPS, ADAM_WD, ADAM_STEP = 0.001, 0.9, 0.999, 1e-08, 0.01, 10
NEG = float(np.finfo(np.float32).min)


def _cp(*sem):
    return pltpu.CompilerParams(dimension_semantics=sem, vmem_limit_bytes=VMEM_LIMIT_BYTES)


def _bdot(a, b):
    return jnp.dot(a.astype(BF), b.astype(BF), preferred_element_type=F32)


def _bdot_nt(a, b):
    return lax.dot_general(a.astype(BF), b.astype(BF), (((1,), (1,)), ((), ())), preferred_element_type=F32)


def _bdot_tn(a, b):
    return lax.dot_general(a.astype(BF), b.astype(BF), (((0,), (0,)), ((), ())), preferred_element_type=F32)


def _hdot(a, b):
    return jnp.dot(a, b, precision=HI, preferred_element_type=F32)


def _bein(eq, a, b):
    return jnp.einsum(eq, a.astype(BF), b.astype(BF), preferred_element_type=F32)


def _hein(eq, a, b):
    return jnp.einsum(eq, a, b, precision=HI, preferred_element_type=F32)


def _row(tm, n):
    return pl.BlockSpec((tm, n), lambda i: (i, 0))


def _fix(shape):
    return pl.BlockSpec(shape, lambda i: (0,) * len(shape))


def _sds(*shape, dtype=F32):
    return jax.ShapeDtypeStruct(shape, dtype)


def _acc(ref, val):
    ref[...] += val


def _zero_at_first(refs, axis=0):
    @pl.when(pl.program_id(axis) == 0)
    def _():
        for r in refs:
            r[...] = jnp.zeros_like(r)


def _rms(x):
    return x * lax.rsqrt(jnp.mean(x * x, axis=-1, keepdims=True) + EPS)


def _pre_mod(x, g, scale, shift):
    return (_rms(x) * g) * (1.0 + scale) + shift


def _post_res(y, x, post_g, gate):
    return x + gate * (_rms(y) * post_g)


def _merge_gate(o1, o2, o3, l1, l2, l3, ga):
    m = jnp.maximum(jnp.maximum(l1, l2), l3)
    e1, e2, e3 = jnp.exp(l1 - m), jnp.exp(l2 - m), jnp.exp(l3 - m)
    inv = 1.0 / (e1 + e2 + e3)
    return ((e1 * inv) * o1 + (e2 * inv) * o2 + (e3 * inv) * o3) * jax.nn.silu(ga)


def _s5_gelu(ypre, u, d_skip):
    return jax.nn.gelu(ypre + d_skip * u)


def _s5_glu(yb, gl, gb):
    return yb * jax.nn.sigmoid(gl) * jax.nn.silu(gb)


def _l0_front(x, pre_g, scale, shift, w_in):
    s_len = x.shape[0]

    def body(x_ref, g_ref, sc_ref, sh_ref, w_ref, *out_refs):
        qkv_refs, (u_ref, ga_ref, gb_ref, h_ref) = out_refs[:9], out_refs[9:]
        hb = _pre_mod(x_ref[...], g_ref[...], sc_ref[...], sh_ref[...]).astype(BF)
        h_ref[...] = hb
        z = jnp.dot(hb, w_ref[...], preferred_element_type=F32)
        for a in range(3):
            piece = z[:, a * 512:(a + 1) * 512]
            for i, d in enumerate(DILATIONS):
                qkv_refs[3 * a + i][...] = _to_res(piece, d).astype(BF)
        u_ref[...] = z[:, 1536:2048]
        ga_ref[...] = z[:, 2048:2560]
        gb_ref[...] = z[:, 2560:3072]

    vec = _fix((1, D_MODEL))
    return pl.pallas_call(
        body, name="l0_front", grid=(s_len // TM,),
        in_specs=[_row(TM, D_MODEL), vec, vec, vec, _fix((D_MODEL, 3072))],
        out_specs=[_res_spec(d) for d in DILATIONS] * 3 + [_row(TM, 512)] * 3 + [_row(TM, D_MODEL)],
        out_shape=[_sds(*_res_shape(s_len, d), dtype=BF) for d in DILATIONS] * 3 + [_sds(s_len, 512)] * 3 + [_sds(s_len, D_MODEL, dtype=BF)],
        compiler_params=_cp("arbitrary"),
    )(x, pre_g, scale, shift, w_in)


def _front_bwd(name, x, pre_g, scale, shift, w_in, dres, parts, widths):
    s_len = x.shape[0]
    n_in = sum(len(p) for p in parts)
    n_cols = sum(widths)

    def body(*refs):
        x_ref, g_ref, sc_ref, sh_ref, w_ref, dres_ref = refs[:6]
        part_refs = refs[6:6 + n_in]
        dz_ref, dx_ref, dg_ref, dsc_ref, dsh_ref = refs[6 + n_in:]
        _zero_at_first([dg_ref, dsc_ref, dsh_ref])
        _, vjp = jax.vjp(_pre_mod, x_ref[...], g_ref[...], sc_ref[...], sh_ref[...])
        dh = jnp.zeros((TM, D_MODEL), F32)
        col, at = 0, 0
        for grp, width in zip(parts, widths):
            tile = lambda r: _from_res(r[...]) if len(r.shape) == 3 else r[...]
            dz = tile(part_refs[at])
            for r in part_refs[at + 1:at + len(grp)]:
                dz = dz + tile(r)
            at += len(grp)
            dzb = dz.astype(BF)
            dz_ref[:, col:col + width] = dzb
            dh = dh + lax.dot_general(dzb, w_ref[:, col:col + width], (((1,), (1,)), ((), ())), preferred_element_type=F32)
            col += width
        dx, dg, dsc, dsh = vjp(dh)
        dx_ref[...] = dx + dres_ref[...]
        _acc(dg_ref, dg)
        _acc(dsc_ref, dsc)
        _acc(dsh_ref, dsh)

    vec = _fix((1, D_MODEL))
    flat = [a for p in parts for a in p]
    return pl.pallas_call(
        body, name=name, grid=(s_len // TM,),
        in_specs=[_row(TM, D_MODEL), vec, vec, vec, _fix((D_MODEL, n_cols)), _row(TM, D_MODEL)]
        + [_res_spec(a.shape[0], a.shape[2]) if a.ndim == 3 else _row(TM, a.shape[1]) for a in flat],
        out_specs=[_row(TM, n_cols), _row(TM, D_MODEL), vec, vec, vec],
        out_shape=[_sds(s_len, n_cols, dtype=BF), _sds(s_len, D_MODEL), _sds(1, D_MODEL), _sds(1, D_MODEL), _sds(1, D_MODEL)],
        compiler_params=_cp("arbitrary"),
    )(x, pre_g, scale, shift, w_in, dres, *flat)


def _matmul_tn(name, a, b, tn):
    s_len, k_dim = a.shape
    n_dim = b.shape[1]
    ts = 512

    def body(a_ref, b_ref, o_ref):
        _zero_at_first([o_ref], axis=1)
        o_ref[...] += lax.dot_general(a_ref[...], b_ref[...], (((0,), (0,)), ((), ())), preferred_element_type=F32)

    return pl.pallas_call(
        body, name=name, grid=(n_dim // tn, s_len // ts),
        in_specs=[pl.BlockSpec((ts, k_dim), lambda j, i: (i, 0)), pl.BlockSpec((ts, tn), lambda j, i: (i, j))],
        out_specs=pl.BlockSpec((k_dim, tn), lambda j, i: (0, j)),
        out_shape=_sds(k_dim, n_dim),
        compiler_params=_cp("arbitrary", "arbitrary"),
    )(a, b)


def _t5_bucket_np(dist):
    dist = np.maximum(dist, 0)
    max_exact = REL_BUCKETS // 2
    large = max_exact + (np.log(np.maximum(dist, 1) / max_exact)
                         / math.log(REL_MAX_DIST / max_exact) * (REL_BUCKETS - max_exact)).astype(np.int32)
    large = np.minimum(large, REL_BUCKETS - 1)
    return np.where(dist < max_exact, dist, large).astype(np.int32)


def _to_res(z, dil):
    if dil == 1:
        return z[None]
    return jnp.swapaxes(z.reshape(z.shape[0] // dil, dil, z.shape[1]), 0, 1)


def _from_res(z):
    if z.shape[0] == 1:
        return z[0]
    return jnp.swapaxes(z, 0, 1).reshape(z.shape[0] * z.shape[1], z.shape[2])


def _res_shape(s_len, dil, width=A_WIDTH):
    return (dil, s_len // dil, width)


def _res_spec(dil, width=A_WIDTH):
    return pl.BlockSpec((dil, TM // dil, width), lambda i: (0, i, 0))


def _bucket_table():
    qi = np.arange(A_BLOCK)[:, None]
    kj = np.arange(2 * A_BLOCK)[None, :]
    return np.stack([_t5_bucket_np((qi + A_BLOCK - kj) * d) for d in DILATIONS], 0)


def _attn_mask(first):
    qi = lax.broadcasted_iota(jnp.int32, (A_BLOCK, 2 * A_BLOCK), 0)
    kj = lax.broadcasted_iota(jnp.int32, (A_BLOCK, 2 * A_BLOCK), 1)
    rel = qi + A_BLOCK - kj
    return (rel >= 0) & (rel <= N_KEYS) & (jnp.logical_not(first) | (kj >= A_BLOCK))


def _attn_specs(nb, rev):
    n_of = (lambda i: nb - 1 - i) if rev else (lambda i: i)
    cur = pl.BlockSpec((None, A_BLOCK, A_WIDTH), lambda r, i: (r, n_of(i), 0))
    prev = pl.BlockSpec((None, A_BLOCK, A_WIDTH), lambda r, i: (r, jnp.maximum(n_of(i) - 1, 0), 0))
    bias = pl.BlockSpec((A_HEADS, A_BLOCK, 2 * A_BLOCK), lambda r, i: (0, 0, 0))
    return cur, prev, bias


def _attn_fwd(q, k, v, bias):
    dil, t_len, _ = q.shape
    nb = t_len // A_BLOCK
    scale = A_HD ** -0.5

    def body(q_ref, kp_ref, kc_ref, vp_ref, vc_ref, b_ref, o_ref, l_ref):
        mask = _attn_mask(pl.program_id(1) == 0)
        lane = lax.broadcasted_iota(jnp.int32, (1, 128), 1)
        for hp in range(A_HEADS // 2):
            sl = slice(hp * 128, (hp + 1) * 128)
            qp = q_ref[:, sl]
            kw = jnp.concatenate([kp_ref[:, sl], kc_ref[:, sl]], axis=0).astype(BF)
            vw = jnp.concatenate([vp_ref[:, sl], vc_ref[:, sl]], axis=0).astype(BF)
            outs, lses = [], []
            for j in range(2):
                hm = (lane < 64) if j == 0 else (lane >= 64)
                s = _bdot_nt(jnp.where(hm, qp, 0.0), kw) * scale
                s = jnp.where(mask, s + b_ref[2 * hp + j], NEG)
                m = jnp.max(s, axis=-1, keepdims=True)
                p = jnp.exp(s - m)
                den = jnp.sum(p, axis=-1, keepdims=True)
                outs.append(_bdot(p, vw) / den)
                lses.append(m + jnp.log(den))
            hm0 = lane < 64
            o_ref[:, sl] = jnp.where(hm0, outs[0], outs[1])
            l_ref[:, sl] = jnp.where(hm0, lses[0], lses[1])

    cur, prev, bias_spec = _attn_specs(nb, False)
    return pl.pallas_call(
        body, name=f"attn_fwd_d{dil}", grid=(dil, nb),
        in_specs=[cur, prev, cur, prev, cur, bias_spec],
        out_specs=[cur, cur],
        out_shape=[_sds(dil, t_len, A_WIDTH)] * 2,
        compiler_params=_cp("arbitrary", "arbitrary"),
    )(q, k, k, v, v, bias)


def _attn_bwd(q, k, v, bias, o, l, do, dl):
    dil, t_len, _ = q.shape
    nb = t_len // A_BLOCK
    scale = A_HD ** -0.5

    def body(q_ref, kp_ref, kc_ref, vp_ref, vc_ref, b_ref, o_ref, l_ref, do_ref, dl_ref,
             dq_ref, dk_ref, dv_ref, db_ref, ck_ref, cv_ref):
        _zero_at_first([ck_ref, cv_ref], axis=1)

        @pl.when((pl.program_id(0) == 0) & (pl.program_id(1) == 0))
        def _():
            db_ref[...] = jnp.zeros_like(db_ref)

        mask = _attn_mask(pl.program_id(1) == nb - 1)
        lane = lax.broadcasted_iota(jnp.int32, (1, 128), 1)
        for hp in range(A_HEADS // 2):
            sl = slice(hp * 128, (hp + 1) * 128)
            qp = q_ref[:, sl]
            kw = jnp.concatenate([kp_ref[:, sl], kc_ref[:, sl]], axis=0).astype(BF)
            vw = jnp.concatenate([vp_ref[:, sl], vc_ref[:, sl]], axis=0).astype(BF)
            op, lp, dop, dlp = o_ref[:, sl], l_ref[:, sl], do_ref[:, sl], dl_ref[:, sl]
            dq_acc = jnp.zeros((A_BLOCK, 128), F32)
            dk_acc = jnp.zeros((2 * A_BLOCK, 128), F32)
            dv_acc = jnp.zeros((2 * A_BLOCK, 128), F32)
            for j in range(2):
                hm = (lane < 64) if j == 0 else (lane >= 64)
                qm = jnp.where(hm, qp, 0.0)
                s = _bdot_nt(qm, kw) * scale
                s = jnp.where(mask, s + b_ref[2 * hp + j], NEG)
                lse = jnp.max(jnp.where(hm, lp, NEG), axis=-1, keepdims=True)
                p = jnp.exp(s - lse)
                do_h = jnp.where(hm, dop, 0.0)
                dd = jnp.sum(do_h * op, axis=-1, keepdims=True)
                dlse = jnp.sum(jnp.where(hm, dlp, 0.0), axis=-1, keepdims=True)
                ds = p * (_bdot_nt(do_h, vw) - dd + dlse)
                dv_acc = dv_acc + _bdot_tn(p, do_h)
                dq_acc = dq_acc + jnp.where(hm, _bdot(ds, kw), 0.0) * scale
                dk_acc = dk_acc + _bdot_tn(ds, qm) * scale
                db_ref[2 * hp + j] += ds
            dq_ref[:, sl] = dq_acc
            dk_ref[:, sl] = dk_acc[A_BLOCK:] + ck_ref[:, sl]
            dv_ref[:, sl] = dv_acc[A_BLOCK:] + cv_ref[:, sl]
            ck_ref[:, sl] = dk_acc[:A_BLOCK]
            cv_ref[:, sl] = dv_acc[:A_BLOCK]

    cur, prev, bias_spec = _attn_specs(nb, True)
    return pl.pallas_call(
        body, name=f"attn_bwd_d{dil}", grid=(dil, nb),
        in_specs=[cur, prev, cur, prev, cur, bias_spec, cur, cur, cur, cur],
        out_specs=[cur, cur, cur, bias_spec],
        out_shape=[_sds(dil, t_len, A_WIDTH)] * 3 + [_sds(A_HEADS, A_BLOCK, 2 * A_BLOCK)],
        scratch_shapes=[pltpu.VMEM((A_BLOCK, A_WIDTH), F32)] * 2,
        compiler_params=_cp("arbitrary", "arbitrary"),
    )(q, k, k, v, v, bias, o, l, do, dl)


def _attn_bias(rel_bias, table):
    def body(rb_ref, t_ref, *o_refs):
        for c in range(3):
            t = t_ref[c]
            acc = [jnp.zeros((A_BLOCK, 2 * A_BLOCK), F32) for _ in range(A_HEADS)]
            for b in range(REL_BUCKETS):
                hit = t == b
                acc = [jnp.where(hit, rb_ref[b, h], acc[h]) for h in range(A_HEADS)]
            for h in range(A_HEADS):
                o_refs[c][h] = acc[h]

    return pl.pallas_call(body, name="attn_bias", out_shape=[_sds(A_HEADS, A_BLOCK, 2 * A_BLOCK)] * 3,
                          in_specs=[pl.BlockSpec(memory_space=pltpu.SMEM), pl.BlockSpec(memory_space=pltpu.VMEM)],
                          compiler_params=pltpu.CompilerParams(vmem_limit_bytes=VMEM_LIMIT_BYTES))(rel_bias, table)


def _rel_bias_grad(dbs, idx_rows):
    n = A_BLOCK * 2 * A_BLOCK

    def body(d0_ref, d1_ref, d2_ref, idx_ref, o_ref):
        bucket = lax.broadcasted_iota(jnp.int32, (REL_BUCKETS, n), 0).astype(F32)
        acc = jnp.zeros((A_HEADS, REL_BUCKETS), F32)
        for c, db_ref in enumerate((d0_ref, d1_ref, d2_ref)):
            onehot = (idx_ref[c:c + 1, :] == bucket).astype(F32)
            acc = acc + lax.dot_general(db_ref[...], onehot, (((1,), (1,)), ((), ())), precision=HI, preferred_element_type=F32)
        o_ref[...] = acc

    return pl.pallas_call(body, name="rel_bias_grad", out_shape=_sds(A_HEADS, REL_BUCKETS),
                          compiler_params=pltpu.CompilerParams(vmem_limit_bytes=VMEM_LIMIT_BYTES))(
                              *[d.reshape(A_HEADS, n) for d in dbs], idx_rows)


def _s5_param_fn(a_re, a_im, log_dt, bt_re, bt_im):
    dt = jnp.exp(log_dt)
    mag = jnp.exp(dt * a_re)
    abar_r, abar_i = mag * jnp.cos(dt * a_im), mag * jnp.sin(dt * a_im)
    den = a_re * a_re + a_im * a_im
    fr = ((abar_r - 1.0) * a_re + abar_i * a_im) / den
    fi = (abar_i * a_re - (abar_r - 1.0) * a_im) / den
    row = lax.broadcasted_iota(jnp.int32, (B_WIDTH, B_GROUPS), 0)
    grp = lax.broadcasted_iota(jnp.int32, (B_WIDTH, B_GROUPS), 1)
    expand = ((row // B_GROUP) == grp).astype(F32)
    fr_e, fi_e = _hdot(expand, fr), _hdot(expand, fi)
    return abar_r, abar_i, fr_e * bt_re - fi_e * bt_im, fr_e * bt_im + fi_e * bt_re


def _s5_params(a_re, a_im, log_dt, bt_re, bt_im):
    def body(ar, ai, ld, br, bi, o1, o2, o3, o4):
        o1[...], o2[...], o3[...], o4[...] = _s5_param_fn(ar[...], ai[...], ld[...], br[...], bi[...])

    return pl.pallas_call(body, name="s5_params",
                          out_shape=[_sds(B_GROUPS, B_STATE)] * 2 + [_sds(B_WIDTH, B_STATE)] * 2)(a_re, a_im, log_dt, bt_re, bt_im)


def _s5_params_bwd(a_re, a_im, log_dt, bt_re, bt_im, d1, d2, d3, d4):
    def body(ar, ai, ld, br, bi, c1, c2, c3, c4, o1, o2, o3, o4, o5):
        _, vjp = jax.vjp(_s5_param_fn, ar[...], ai[...], ld[...], br[...], bi[...])
        o1[...], o2[...], o3[...], o4[...], o5[...] = vjp((c1[...], c2[...], c3[...], c4[...]))

    return pl.pallas_call(body, name="s5_params_bwd",
                          out_shape=[_sds(B_GROUPS, B_STATE)] * 2 + [_sds(B_GROUPS, 1)] + [_sds(B_WIDTH, B_STATE)] * 2,
                          )(a_re, a_im, log_dt, bt_re, bt_im, d1, d2, d3, d4)


S5_SUB = 8
S5_GROUPS = S5_T // S5_SUB


def _dscan(xr, xi, cr, ci, period, reverse):
    n = xr.shape[0]
    rows = lax.broadcasted_iota(jnp.int32, xr.shape, 0)
    pos = rows % period
    k = 1
    while k < period:
        if reverse:
            keep, shift = pos < period - k, n - k
        else:
            keep, shift = pos >= k, k
        sr = jnp.where(keep, pltpu.roll(xr, shift, 0), 0.0)
        si = jnp.where(keep, pltpu.roll(xi, shift, 0), 0.0)
        xr, xi = xr + cr * sr - ci * si, xi + cr * si + ci * sr
        cr, ci = cr * cr - ci * ci, 2.0 * cr * ci
        k *= 2
    return xr, xi, cr, ci


def _pick_row(x, r):
    rows = lax.broadcasted_iota(jnp.int32, x.shape, 0)
    return jnp.sum(jnp.where(rows == r, x, 0.0), axis=0, keepdims=True)


def _scan_tables(ar, ai, reverse):
    rows = lax.broadcasted_iota(jnp.int32, (S5_SUB, S5_LANES), 0)
    at = rows == (S5_SUB - 1 if reverse else 0)
    p8r, p8i, a8r, a8i = _dscan(jnp.where(at, ar, 0.0), jnp.where(at, ai, 0.0), ar, ai, S5_SUB, reverse)
    grp = lax.broadcasted_iota(jnp.int32, (S5_GROUPS, S5_LANES), 0)
    at = grp == (S5_GROUPS - 1 if reverse else 0)
    pgr, pgi, _, _ = _dscan(jnp.where(at, a8r, 0.0), jnp.where(at, a8i, 0.0), a8r, a8i, S5_GROUPS, reverse)
    return p8r, p8i, pgr, pgi


def _block_scan(br, bi, ar, ai, carry, tables, reverse, work):
    p8r, p8i, pgr, pgi = tables
    cin_r, cin_i = carry
    xw_r, xw_i, ew_r, ew_i = work
    xr, xi, a8r, a8i = _dscan(br, bi, ar, ai, S5_SUB, reverse)
    xw_r[...] = xr
    xw_i[...] = xi
    end = 0 if reverse else S5_SUB - 1
    for g in range(S5_GROUPS):
        ew_r[g:g + 1, :] = xw_r[S5_SUB * g + end:S5_SUB * g + end + 1, :]
        ew_i[g:g + 1, :] = xw_i[S5_SUB * g + end:S5_SUB * g + end + 1, :]
    er, ei, _, _ = _dscan(ew_r[...], ew_i[...], a8r, a8i, S5_GROUPS, reverse)
    er, ei = er + pgr * cin_r - pgi * cin_i, ei + pgr * cin_i + pgi * cin_r
    grp = lax.broadcasted_iota(jnp.int32, (S5_GROUPS, S5_LANES), 0)
    if reverse:
        ew_r[...] = jnp.where(grp == S5_GROUPS - 1, cin_r, pltpu.roll(er, S5_GROUPS - 1, 0))
        ew_i[...] = jnp.where(grp == S5_GROUPS - 1, cin_i, pltpu.roll(ei, S5_GROUPS - 1, 0))
    else:
        ew_r[...] = jnp.where(grp == 0, cin_r, pltpu.roll(er, 1, 0))
        ew_i[...] = jnp.where(grp == 0, cin_i, pltpu.roll(ei, 1, 0))
    for g in range(S5_GROUPS):
        rows = slice(S5_SUB * g, S5_SUB * (g + 1))
        nr, ni = ew_r[g:g + 1, :], ew_i[g:g + 1, :]
        xw_r[rows, :] += p8r * nr - p8i * ni
        xw_i[rows, :] += p8r * ni + p8i * nr
    last = 0 if reverse else S5_GROUPS - 1
    return xw_r[...], xw_i[...], (_pick_row(er, last), _pick_row(ei, last))


def _s5_tile_specs(n_t, rev):
    t_of = (lambda i: n_t - 1 - i) if rev else (lambda i: i)
    u_spec = pl.BlockSpec((S5_T, 128), lambda j, i: (t_of(i), j))
    x_spec = pl.BlockSpec((S5_T, S5_LANES), lambda j, i: (t_of(i), j))
    b_spec = pl.BlockSpec((1, 128, S5_LANES), lambda j, i: (j, 0, 0))
    c_spec = pl.BlockSpec((1, S5_LANES, 128), lambda j, i: (j, 0, 0))
    a_spec = pl.BlockSpec((1, S5_LANES), lambda j, i: (0, j))
    return u_spec, x_spec, b_spec, c_spec, a_spec


def _s5_scratch(with_rows):
    return ([pltpu.VMEM((1, S5_LANES), F32)] * 2 + [pltpu.VMEM((S5_SUB, S5_LANES), F32)] * 2
            + [pltpu.VMEM((S5_GROUPS, S5_LANES), F32)] * 4 + ([pltpu.VMEM((S5_T, S5_LANES), F32)] * 2 if with_rows else []))


def _s5_scan_fwd(u, btr, bti, ctr, cti, abr, abi, exchange=None):
    s_len = u.shape[0]
    n_t = s_len // S5_T

    def body(u_ref, btr_ref, bti_ref, ctr_ref, cti_ref, ar_ref, ai_ref, xr_ref, xi_ref, y_ref, car, cai, p8r, p8i, pgr, pgi, ew_r, ew_i):
        ar, ai = ar_ref[...], ai_ref[...]

        @pl.when(pl.program_id(1) == 0)
        def _():
            car[...] = jnp.zeros_like(car)
            cai[...] = jnp.zeros_like(cai)
            p8r[...], p8i[...], pgr[...], pgi[...] = _scan_tables(ar, ai, False)

        ub = u_ref[...]
        xr, xi, (ncr, nci) = _block_scan(_bdot(ub, btr_ref[0]), _bdot(ub, bti_ref[0]), ar, ai, (car[...], cai[...]),
                                         (p8r[...], p8i[...], pgr[...], pgi[...]), False, (xr_ref, xi_ref, ew_r, ew_i))
        car[...] = ncr
        cai[...] = nci
        y_ref[...] = _bdot(xr, ctr_ref[0]) - _bdot(xi, cti_ref[0])

    u_spec, x_spec, b_spec, c_spec, a_spec = _s5_tile_specs(n_t, False)
    return _call_with_exchange(
        body, "s5_scan_fwd", (S5_TILES, n_t),
        [u_spec, b_spec, b_spec, c_spec, c_spec, a_spec, a_spec], [x_spec, x_spec, u_spec],
        [_sds(s_len, S5_TILES * S5_LANES)] * 2 + [_sds(s_len, B_WIDTH)], _s5_scratch(False),
        (u, btr, bti, ctr, cti, abr, abi), exchange)


def _s5_scan_bwd(dy, xr, xi, u, btr, bti, ctr, cti, abr, abi, exchange=None):
    s_len = u.shape[0]
    n_t = s_len // S5_T

    def body(dy_ref, xr_ref, xi_ref, xrp_ref, xip_ref, u_ref, btr_ref, bti_ref, ctr_ref, cti_ref, ar_ref, ai_ref,
             du_ref, dbtr_ref, dbti_ref, dctr_ref, dcti_ref, dar_ref, dai_ref, car, cai, p8r, p8i, pgr, pgi, ew_r, ew_i, xw_r, xw_i):
        ar, ai = ar_ref[...], ai_ref[...]
        i = pl.program_id(1)
        rows = lax.broadcasted_iota(jnp.int32, (S5_T, S5_LANES), 0)

        @pl.when(i == 0)
        def _():
            for r in (car, cai, dbtr_ref, dbti_ref, dctr_ref, dcti_ref, dar_ref, dai_ref):
                r[...] = jnp.zeros_like(r)
            p8r[...], p8i[...], pgr[...], pgi[...] = _scan_tables(ar, -ai, True)

        dyb = dy_ref[...]
        xr_b, xi_b, ub = xr_ref[...], xi_ref[...], u_ref[...]
        dctr_ref[0] += _bdot_tn(xr_b, dyb)
        dcti_ref[0] -= _bdot_tn(xi_b, dyb)
        gr, gi, (ncr, nci) = _block_scan(_bdot_nt(dyb, ctr_ref[0]), -_bdot_nt(dyb, cti_ref[0]), ar, -ai, (car[...], cai[...]),
                                         (p8r[...], p8i[...], pgr[...], pgi[...]), True, (xw_r, xw_i, ew_r, ew_i))
        car[...] = ncr
        cai[...] = nci
        du_ref[...] = _bdot_nt(gr, btr_ref[0]) + _bdot_nt(gi, bti_ref[0])
        dbtr_ref[0] += _bdot_tn(ub, gr)
        dbti_ref[0] += _bdot_tn(ub, gi)
        has_prev = (i < n_t - 1).astype(F32)
        hr = _pick_row(xrp_ref[...], 7) * has_prev
        hi = _pick_row(xip_ref[...], 7) * has_prev
        xpr = jnp.where(rows == 0, hr, pltpu.roll(xr_b, 1, 0))
        xpi = jnp.where(rows == 0, hi, pltpu.roll(xi_b, 1, 0))
        dar_ref[...] += jnp.sum(gr * xpr + gi * xpi, axis=0, keepdims=True)
        dai_ref[...] += jnp.sum(gi * xpr - gr * xpi, axis=0, keepdims=True)

    u_spec, x_spec, b_spec, c_spec, a_spec = _s5_tile_specs(n_t, True)
    halo = pl.BlockSpec((8, S5_LANES), lambda j, i: (jnp.maximum((n_t - 1 - i) * (S5_T // 8) - 1, 0), j))
    return _call_with_exchange(
        body, "s5_scan_bwd", (S5_TILES, n_t),
        [u_spec, x_spec, x_spec, halo, halo, u_spec, b_spec, b_spec, c_spec, c_spec, a_spec, a_spec],
        [u_spec, b_spec, b_spec, c_spec, c_spec, a_spec, a_spec],
        [_sds(s_len, B_WIDTH)] + [_sds(S5_TILES, 128, S5_LANES)] * 2 + [_sds(S5_TILES, S5_LANES, 128)] * 2
        + [_sds(1, S5_TILES * S5_LANES)] * 2,
        _s5_scratch(True), (dy, xr, xi, xr, xi, u, btr, bti, ctr, cti, abr, abi), exchange)


def _blockdiag_b(bbar_t):
    blocks = bbar_t.reshape(S5_TILES, 8, B_GROUP, B_STATE)
    return jnp.einsum('jgmp,gh->jgmhp', blocks, jnp.eye(8, dtype=F32)).reshape(S5_TILES, 128, S5_LANES)


def _blockdiag_b_t(d):
    return jnp.einsum('jgmgp->jgmp', d.reshape(S5_TILES, 8, B_GROUP, 8, B_STATE)).reshape(B_WIDTH, B_STATE)


def _blockdiag_c(c):
    blocks = c.reshape(S5_TILES, 8, B_GROUP, B_STATE)
    return jnp.einsum('jgmp,gh->jhpgm', blocks, jnp.eye(8, dtype=F32)).reshape(S5_TILES, S5_LANES, 128)


def _blockdiag_c_t(d):
    return jnp.einsum('jgpgm->jgmp', d.reshape(S5_TILES, 8, B_STATE, 8, B_GROUP)).reshape(B_GROUPS, B_GROUP, B_STATE)


def _l0_out(os, ls, ga, gb, ypre, u, x, d_skip, glu_w, glu_b, w_out, post_g, gate):
    s_len = x.shape[0]

    def body(o0, o1, o2, l0, l1, l2, ga_ref, gb_ref, yp_ref, u_ref, x_ref, d_ref, gw_ref, gbias_ref, w_ref, pg_ref, gt_ref, x1_ref, y_ref):
        oa = _merge_gate(*[_from_res(r[...]) for r in (o0, o1, o2, l0, l1, l2)], ga_ref[...])
        yb = _s5_gelu(yp_ref[...], u_ref[...], d_ref[...])
        ob = _s5_glu(yb, _bdot(yb, gw_ref[...]) + gbias_ref[...], gb_ref[...])
        y = _bdot(oa, w_ref[0:512, :]) + _bdot(ob, w_ref[512:1024, :])
        y_ref[...] = y
        x1_ref[...] = _post_res(y, x_ref[...], pg_ref[...], gt_ref[...])

    vec, half = _fix((1, D_MODEL)), _fix((1, 512))
    return pl.pallas_call(
        body, name="l0_out", grid=(s_len // TM,),
        in_specs=[_res_spec(d) for d in DILATIONS] * 2 + [_row(TM, 512)] * 4
        + [_row(TM, D_MODEL), half, _fix((512, 512)), half, _fix((D_MODEL, D_MODEL)), vec, vec],
        out_specs=[_row(TM, D_MODEL)] * 2,
        out_shape=[_sds(s_len, D_MODEL)] * 2,
        compiler_params=_cp("arbitrary"),
    )(*os, *ls, ga, gb, ypre, u, x, d_skip, glu_w, glu_b, w_out, post_g, gate)


def _l0_out_bwd(os, ls, ga, gb, ypre, u, x, y, d_skip, glu_w, glu_b, w_out, post_g, gate, dx1, exchange=None):
    s_len = x.shape[0]

    def body(o0, o1, o2, l0, l1, l2, ga_ref, gb_ref, yp_ref, u_ref, x_ref, y_ref, d_ref, gw_ref, gbias_ref, w_ref, pg_ref, gt_ref, dx1_ref,
             do0, do1, do2, dl0, dl1, dl2, dga_ref, dgb_ref, dyp_ref, du_ref, dd_ref, dgw_ref, dgbias_ref, dw_ref, dpg_ref, dgt_ref):
        _zero_at_first([dd_ref, dgw_ref, dgbias_ref, dw_ref, dpg_ref, dgt_ref])
        _, vjp2 = jax.vjp(_post_res, y_ref[...], x_ref[...], pg_ref[...], gt_ref[...])
        dy, _, dpg, dgt = vjp2(dx1_ref[...])
        _acc(dpg_ref, dpg)
        _acc(dgt_ref, dgt)
        oa, vjp_a = jax.vjp(_merge_gate, *[_from_res(r[...]) for r in (o0, o1, o2, l0, l1, l2)], ga_ref[...])
        yb, vjp_g = jax.vjp(_s5_gelu, yp_ref[...], u_ref[...], d_ref[...])
        gl = _bdot(yb, gw_ref[...]) + gbias_ref[...]
        ob, vjp_b = jax.vjp(_s5_glu, yb, gl, gb_ref[...])
        dw_ref[0:512, :] += _bdot_tn(oa, dy)
        dw_ref[512:1024, :] += _bdot_tn(ob, dy)
        d1, d2, d3, e1, e2, e3, dga = vjp_a(_bdot_nt(dy, w_ref[0:512, :]))
        for ref, val, d in zip((do0, do1, do2, dl0, dl1, dl2), (d1, d2, d3, e1, e2, e3), DILATIONS * 2):
            ref[...] = _to_res(val, d)
        dga_ref[...] = dga
        dyb, dgl, dgb = vjp_b(_bdot_nt(dy, w_ref[512:1024, :]))
        dgb_ref[...] = dgb
        dgw_ref[...] += _bdot_tn(yb, dgl)
        _acc(dgbias_ref, jnp.sum(dgl, axis=0, keepdims=True))
        dyp, du, dd = vjp_g(dyb + _bdot_nt(dgl, gw_ref[...]))
        dyp_ref[...] = dyp
        du_ref[...] = du
        _acc(dd_ref, dd)

    vec, half = _fix((1, D_MODEL)), _fix((1, 512))
    r5, r10 = _row(TM, 512), _row(TM, D_MODEL)
    res6 = [_res_spec(d) for d in DILATIONS] * 2
    return _call_with_exchange(
        body, "l0_out_bwd", (s_len // TM,),
        res6 + [r5] * 4 + [r10, r10, half, _fix((512, 512)), half, _fix((D_MODEL, D_MODEL)), vec, vec, r10],
        res6 + [r5] * 4 + [half, _fix((512, 512)), half, _fix((D_MODEL, D_MODEL)), vec, vec],
        [_sds(*_res_shape(s_len, d)) for d in DILATIONS] * 2 + [_sds(s_len, 512)] * 4
        + [_sds(1, 512), _sds(512, 512), _sds(1, 512), _sds(D_MODEL, D_MODEL), _sds(1, D_MODEL), _sds(1, D_MODEL)],
        [], (*os, *ls, ga, gb, ypre, u, x, y, d_skip, glu_w, glu_b, w_out, post_g, gate, dx1), exchange)


def _l1_front(x, pre_g, scale, shift, w_in):
    s_len = x.shape[0]

    def body(x_ref, g_ref, sc_ref, sh_ref, w_ref, raw_ref, gate_ref, ba_ref, h_ref):
        hb = _pre_mod(x_ref[...], g_ref[...], sc_ref[...], sh_ref[...]).astype(BF)
        h_ref[...] = hb
        z = jnp.dot(hb, w_ref[...], preferred_element_type=F32)
        raw_ref[...] = z[:, 0:QKV]
        gate_ref[...] = z[:, QKV:QKV + 1024]
        ba_ref[...] = z[:, QKV + 1024:C_IN_PAD]

    vec = _fix((1, D_MODEL))
    return pl.pallas_call(
        body, name="l1_front", grid=(s_len // TM,),
        in_specs=[_row(TM, D_MODEL), vec, vec, vec, _fix((D_MODEL, C_IN_PAD))],
        out_specs=[_row(TM, QKV), _row(TM, 1024), _row(TM, 128), _row(TM, D_MODEL)],
        out_shape=[_sds(s_len, QKV), _sds(s_len, 1024), _sds(s_len, 128), _sds(s_len, D_MODEL, dtype=BF)],
        compiler_params=_cp("arbitrary"),
    )(x, pre_g, scale, shift, w_in)


def _bg_fn(ba, alog_row, dtb_row):
    lane = lax.broadcasted_iota(jnp.int32, (1, 128), 1)
    g = -jnp.exp(alog_row) * jax.nn.softplus(ba + dtb_row)
    return jnp.where(lane < C_HEADS, jax.nn.sigmoid(ba), jnp.where(lane < 2 * C_HEADS, g, 0.0))


def _act_q(c):
    q = jax.nn.silu(c)
    return q * lax.rsqrt(jnp.sum(q * q, axis=-1, keepdims=True) + EPS) * (C_DK ** -0.5)


def _act_k(c):
    k = jax.nn.silu(c)
    return k * lax.rsqrt(jnp.sum(k * k, axis=-1, keepdims=True) + EPS)


def _act_of(s):
    return _act_q if s < 8 else (_act_k if s < 16 else jax.nn.silu)


def _gdn_prep(raw, ba, conv_w, alog_row, dtb_row):
    s_len = raw.shape[0]

    def body(raw_ref, halo_ref, ba_ref, w_ref, al_ref, dt_ref, qkv_ref, bg_ref):
        bg_ref[...] = _bg_fn(ba_ref[...], al_ref[...], dt_ref[...])
        has_prev = (pl.program_id(0) > 0).astype(F32)
        for s in range(24):
            sl = slice(s * 128, (s + 1) * 128)
            cat = jnp.concatenate([halo_ref[:, sl] * has_prev, raw_ref[:, sl]], axis=0)
            conv = w_ref[3:4, sl] * cat[8:]
            for j in range(3):
                conv = conv + w_ref[j:j + 1, sl] * pltpu.roll(cat, 3 - j, 0)[8:]
            qkv_ref[:, sl] = _act_of(s)(conv)

    halo = pl.BlockSpec((8, QKV), lambda i: (jnp.maximum(i * (TM // 8) - 1, 0), 0))
    row128 = _fix((1, 128))
    return pl.pallas_call(
        body, name="gdn_prep", grid=(s_len // TM,),
        in_specs=[_row(TM, QKV), halo, _row(TM, 128), _fix((C_CONV, QKV)), row128, row128],
        out_specs=[_row(TM, QKV), _row(TM, 128)],
        out_shape=[_sds(s_len, QKV), _sds(s_len, 128)],
        compiler_params=_cp("arbitrary"),
    )(raw, raw, ba, conv_w, alog_row, dtb_row)


def _gdn_prep_bwd(raw, ba, conv_w, alog_row, dtb_row, dq, dk, dv, dbg):
    s_len = raw.shape[0]
    n_tiles = s_len // TM
    ext = TM + 8

    def body(raw_ref, prev_ref, next_ref, ba_ref, w_ref, al_ref, dt_ref, dq_ref, dqn_ref, dk_ref, dkn_ref, dv_ref, dvn_ref, dbg_ref,
             draw_ref, dba_ref, dw_ref, dal_ref, ddt_ref):
        _zero_at_first([dw_ref, dal_ref, ddt_ref])
        i = pl.program_id(0)
        _, vjp_bg = jax.vjp(_bg_fn, ba_ref[...], al_ref[...], dt_ref[...])
        dba, dal, ddt = vjp_bg(dbg_ref[...])
        dba_ref[...] = dba
        _acc(dal_ref, dal)
        _acc(ddt_ref, ddt)
        has_prev = (i > 0).astype(F32)
        has_next = (i < n_tiles - 1).astype(F32)
        ct_refs = ((dq_ref, dqn_ref), (dk_ref, dkn_ref), (dv_ref, dvn_ref))
        for s in range(24):
            sl = slice(s * 128, (s + 1) * 128)
            hl = slice((s % 8) * 128, (s % 8 + 1) * 128)
            tile_ref, nxt_ref = ct_refs[s // 8]
            cat = jnp.concatenate([prev_ref[:, sl] * has_prev, raw_ref[:, sl], next_ref[:, sl] * has_next], axis=0)
            shifted = [pltpu.roll(cat, 3 - j, 0)[8:] for j in range(3)] + [cat[8:]]
            conv = w_ref[3:4, sl] * shifted[3]
            for j in range(3):
                conv = conv + w_ref[j:j + 1, sl] * shifted[j]
            ct = jnp.concatenate([tile_ref[:, hl], nxt_ref[:, hl] * has_next], axis=0)
            _, vjp_act = jax.vjp(_act_of(s), conv)
            dconv, = vjp_act(ct)
            draw = w_ref[3:4, sl] * dconv[:TM]
            for j in range(3):
                draw = draw + w_ref[j:j + 1, sl] * pltpu.roll(dconv, ext - (3 - j), 0)[:TM]
            draw_ref[:, sl] = draw
            for j in range(4):
                dw_ref[j:j + 1, sl] += jnp.sum(dconv[:TM] * shifted[j][:TM], axis=0, keepdims=True)

    prev = pl.BlockSpec((8, QKV), lambda i: (jnp.maximum(i * (TM // 8) - 1, 0), 0))
    nxt = lambda n: pl.BlockSpec((8, n), lambda i: (jnp.minimum((i + 1) * (TM // 8), s_len // 8 - 1), 0))
    row128 = _fix((1, 128))
    ct_specs = [_row(TM, 1024), nxt(1024)] * 3
    return pl.pallas_call(
        body, name="gdn_prep_bwd", grid=(n_tiles,),
        in_specs=[_row(TM, QKV), prev, nxt(QKV), _row(TM, 128), _fix((C_CONV, QKV)), row128, row128] + ct_specs + [_row(TM, 128)],
        out_specs=[_row(TM, QKV), _row(TM, 128), _fix((C_CONV, QKV)), row128, row128],
        out_shape=[_sds(s_len, QKV), _sds(s_len, 128), _sds(C_CONV, QKV), _sds(1, 128), _sds(1, 128)],
        compiler_params=_cp("arbitrary"),
    )(raw, raw, raw, ba, conv_w, alog_row, dtb_row, dq, dq, dk, dk, dv, dv, dbg)


def _tein(eq, a, b):
    return jnp.einsum(eq, a, b, precision=lax.Precision.HIGH, preferred_element_type=F32)


def _unit_lower_inverse(lower):
    ri = lax.broadcasted_iota(jnp.int32, (C_CHUNK, C_CHUNK), 0)
    ci = lax.broadcasted_iota(jnp.int32, (C_CHUNK, C_CHUNK), 1)
    eye = (ri == ci).astype(F32)[None]
    p_mat = -lower
    inv = eye + p_mat
    for _ in range(5):
        p_mat = _bein('hij,hjk->hik', p_mat, p_mat)
        inv = inv + _bein('hij,hjk->hik', inv, p_mat)
    inv = _tein('hij,hjk->hik', inv, 2.0 * eye - _tein('hij,hjk->hik', eye + lower, inv))
    return jnp.where((ri >= ci)[None], inv, 0.0)


@jax.custom_vjp
def _known_inverse(lower, inv):
    return inv


def _known_inverse_fwd(lower, inv):
    return inv, inv


def _known_inverse_bwd(inv, d_inv):
    d_lower = -_bein('hik,hjk->hij', _bein('hji,hjk->hik', inv, d_inv), inv)
    return d_lower, jnp.zeros_like(inv)


_known_inverse.defvjp(_known_inverse_fwd, _known_inverse_bwd)


def _gdn_local(q, k, v, bgs, inv_known=None):
    lane = lax.broadcasted_iota(jnp.int32, (1, 128), 1)
    ri = lax.broadcasted_iota(jnp.int32, (C_CHUNK, C_CHUNK), 0)
    ci = lax.broadcasted_iota(jnp.int32, (C_CHUNK, C_CHUNK), 1)
    row_id = lax.broadcasted_iota(jnp.int32, (128, C_CHUNK), 0)
    beta, gc, gcj = [], [], []
    for bg in bgs:
        gc_t = _hdot((ri >= ci).astype(F32), bg)
        gc_rows = gc_t.T
        for h in range(C_HEADS):
            beta.append(jnp.sum(jnp.where(lane == h, bg, 0.0), axis=-1, keepdims=True))
            gc.append(jnp.sum(jnp.where(lane == C_HEADS + h, gc_t, 0.0), axis=-1, keepdims=True))
            gcj.append(jnp.sum(jnp.where(row_id == C_HEADS + h, gc_rows, 0.0), axis=0, keepdims=True))
    beta, gc, gcj = jnp.stack(beta, axis=0), jnp.stack(gc, axis=0), jnp.stack(gcj, axis=0)
    tril, strict = (ri >= ci)[None], (ri > ci)[None]
    decay = jnp.exp(jnp.where(tril, gc - gcj, -1e30))
    kb = k * beta
    lower = jnp.where(strict, _bein('hid,hjd->hij', kb, k) * decay, 0.0)
    inv = _unit_lower_inverse(lower) if inv_known is None else _known_inverse(lower, inv_known)
    egc = jnp.exp(gc)
    u_c = _bein('hij,hjd->hid', inv, v * beta)
    w_c = _bein('hij,hjd->hid', inv, kb * egc)
    aqk = _bein('hid,hjd->hij', q, k) * decay
    rowi = lax.broadcasted_iota(jnp.int32, (1, C_CHUNK, 1), 1)
    g_last = jnp.sum(jnp.where(rowi == C_CHUNK - 1, gc, 0.0), axis=1, keepdims=True)
    kd = k * jnp.exp(g_last - gc)
    return (u_c, w_c, aqk, q * egc, kd, jnp.exp(g_last)), inv


def _gdn_state(local, state):
    u_c, w_c, aqk, qg, kd, dec = local
    v_new = u_c - _bein('hik,hkv->hiv', w_c, state)
    o = _bein('hik,hkv->hiv', qg, state) + _bein('hij,hjv->hiv', aqk, v_new)
    return o, state * dec + _bein('hik,hiv->hkv', kd, v_new)


C_SUB = 4


def _gdn_group(q, k, v, bgs, state, inv_known=None):
    local, inv = _gdn_local(q, k, v, bgs, inv_known)
    outs = []
    for s in range(len(bgs)):
        o, state = _gdn_state(tuple(t[s * C_HEADS:(s + 1) * C_HEADS] for t in local), state)
        outs.append(o)
    return outs, state, inv


def _heads(ref):
    return jnp.stack([ref[s * C_CHUNK:(s + 1) * C_CHUNK, h * C_DK:(h + 1) * C_DK] for s in range(C_SUB) for h in range(C_HEADS)], axis=0)


def _put_heads(ref, sub, val):
    rows = slice(sub * C_CHUNK, (sub + 1) * C_CHUNK)
    for h in range(C_HEADS):
        ref[rows, h * C_DK:(h + 1) * C_DK] = val[h]


def _gdn_specs(s_len, rev):
    rows = C_SUB * C_CHUNK
    n_g = s_len // rows
    at = (lambda i: n_g - 1 - i) if rev else (lambda i: i)
    col = lambda c: pl.BlockSpec((rows, 1024), lambda i: (at(i), c))
    row128 = pl.BlockSpec((rows, 128), lambda i: (at(i), 0))
    state = pl.BlockSpec((1, C_HEADS, C_DK, C_DK), lambda i: (at(i), 0, 0, 0))
    inv = pl.BlockSpec((1, C_SUB * C_HEADS, C_CHUNK, C_CHUNK), lambda i: (at(i), 0, 0, 0))
    return n_g, col, row128, state, inv


def _gdn_fwd(qkv, bg):
    s_len = qkv.shape[0]
    n_g, col, row128, state_spec, inv_spec = _gdn_specs(s_len, False)

    def body(q_ref, k_ref, v_ref, bg_ref, o_ref, ss_ref, inv_ref, st_ref):
        _zero_at_first([st_ref])
        s0 = st_ref[...]
        ss_ref[0] = s0
        bgs = [bg_ref[s * C_CHUNK:(s + 1) * C_CHUNK, :] for s in range(C_SUB)]
        outs, s2, inv = _gdn_group(_heads(q_ref), _heads(k_ref), _heads(v_ref), bgs, s0)
        st_ref[...] = s2
        inv_ref[0] = inv
        for s in range(C_SUB):
            _put_heads(o_ref, s, outs[s])

    return pl.pallas_call(
        body, name="gdn_fwd", grid=(n_g,),
        in_specs=[col(0), col(1), col(2), row128],
        out_specs=[col(0), state_spec, inv_spec],
        out_shape=[_sds(s_len, 1024), _sds(n_g, C_HEADS, C_DK, C_DK), _sds(n_g, C_SUB * C_HEADS, C_CHUNK, C_CHUNK)],
        scratch_shapes=[pltpu.VMEM((C_HEADS, C_DK, C_DK), F32)],
        compiler_params=_cp("arbitrary"),
    )(qkv, qkv, qkv, bg)


def _gdn_bwd(qkv, bg, states, invs, do):
    s_len = qkv.shape[0]
    n_g, col, row128, state_spec, inv_spec = _gdn_specs(s_len, True)

    def body(q_ref, k_ref, v_ref, bg_ref, ss_ref, inv_ref, do_ref, dq_ref, dk_ref, dv_ref, dbg_ref, ds_ref):
        _zero_at_first([ds_ref])
        inv_known = inv_ref[0]

        def group(q, k, v, bgs, st):
            outs, st2, _ = _gdn_group(q, k, v, bgs, st, inv_known)
            return outs, st2

        bgs = [bg_ref[s * C_CHUNK:(s + 1) * C_CHUNK, :] for s in range(C_SUB)]
        _, vjp = jax.vjp(group, _heads(q_ref), _heads(k_ref), _heads(v_ref), bgs, ss_ref[0])
        douts = [jnp.stack([do_ref[s * C_CHUNK:(s + 1) * C_CHUNK, h * C_DK:(h + 1) * C_DK] for h in range(C_HEADS)], axis=0)
                 for s in range(C_SUB)]
        dq, dk, dv, dbgs, ds = vjp((douts, ds_ref[...]))
        ds_ref[...] = ds
        for s in range(C_SUB):
            dbg_ref[s * C_CHUNK:(s + 1) * C_CHUNK, :] = dbgs[s]
            for ref, val in ((dq_ref, dq), (dk_ref, dk), (dv_ref, dv)):
                _put_heads(ref, s, val[s * C_HEADS:(s + 1) * C_HEADS])

    return pl.pallas_call(
        body, name="gdn_bwd", grid=(n_g,),
        in_specs=[col(0), col(1), col(2), row128, state_spec, inv_spec, col(0)],
        out_specs=[col(0), col(0), col(0), row128],
        out_shape=[_sds(s_len, 1024)] * 3 + [_sds(s_len, 128)],
        scratch_shapes=[pltpu.VMEM((C_HEADS, C_DK, C_DK), F32)],
        compiler_params=_cp("arbitrary"),
    )(qkv, qkv, qkv, bg, states, invs, do)


def _head_norm_gate(o, gate, norm_g):
    return (_rms(o) * norm_g) * jax.nn.silu(gate)


def _l1_out_fb(o, gate_c, x1, target, norm_g, w_out, post_g, gate):
    s_len = x1.shape[0]

    def body(o_ref, gc_ref, x1_ref, t_ref, ng_ref, w_ref, pg_ref, gt_ref,
             loss_ref, dres_ref, do_ref, dgc_ref, dw_ref, dng_ref, dpg_ref, dgt_ref):
        _zero_at_first([loss_ref, dw_ref, dng_ref, dpg_ref, dgt_ref])
        ng = ng_ref[...]
        ons, vjps = [], []
        for h in range(C_HEADS):
            sl = slice(h * C_DK, (h + 1) * C_DK)
            on, vjp_h = jax.vjp(_head_norm_gate, o_ref[:, sl], gc_ref[:, sl], ng)
            ons.append(on)
            vjps.append(vjp_h)
        on_all = jnp.concatenate(ons, axis=-1)
        y = _bdot(on_all, w_ref[...])
        x2, vjp2 = jax.vjp(_post_res, y, x1_ref[...], pg_ref[...], gt_ref[...])
        err = x2 - t_ref[...]
        _acc(loss_ref, jnp.full((1, 128), 0.5 * jnp.sum(jnp.mean(err * err, axis=-1)), F32))
        dx2 = err * (1.0 / D_MODEL)
        dy, _, dpg, dgt = vjp2(dx2)
        dres_ref[...] = dx2
        _acc(dpg_ref, dpg)
        _acc(dgt_ref, dgt)
        dw_ref[...] += _bdot_tn(on_all, dy)
        don = _bdot_nt(dy, w_ref[...])
        for h in range(C_HEADS):
            sl = slice(h * C_DK, (h + 1) * C_DK)
            do_h, dgc_h, dng = vjps[h](don[:, sl])
            do_ref[:, sl] = do_h
            dgc_ref[:, sl] = dgc_h
            _acc(dng_ref, dng)

    vec, r10 = _fix((1, D_MODEL)), _row(TM, D_MODEL)
    row128 = _fix((1, 128))
    return pl.pallas_call(
        body, name="l1_out_fb", grid=(s_len // TM,),
        in_specs=[r10, r10, r10, r10, row128, _fix((D_MODEL, D_MODEL)), vec, vec],
        out_specs=[row128, r10, r10, r10, _fix((D_MODEL, D_MODEL)), row128, vec, vec],
        out_shape=[_sds(1, 128), _sds(s_len, D_MODEL), _sds(s_len, D_MODEL), _sds(s_len, D_MODEL),
                   _sds(D_MODEL, D_MODEL), _sds(1, 128), _sds(1, D_MODEL), _sds(1, D_MODEL)],
        compiler_params=_cp("arbitrary"),
    )(o, gate_c, x1, target, norm_g, w_out, post_g, gate)


def _row_of(v, width, at):
    return jnp.zeros((1, width), F32).at[0, at:at + v.shape[-1]].set(v.reshape(-1))


def _local_step(x, target, mod, wd, comm=None):
    s_len = x.shape[0]
    shift0, scale0, gate0 = (mod[0:1, i * 1024:(i + 1) * 1024] for i in range(3))
    shift1, scale1, gate1 = (mod[1:2, i * 1024:(i + 1) * 1024] for i in range(3))
    pre_g0, pre_g1 = wd["pre_g"][0:1], wd["pre_g"][1:2]
    post_g0, post_g1 = wd["post_g"][0:1], wd["post_g"][1:2]
    w_in0 = wd["ab_w_in"].astype(BF)
    d_skip, glu_b = wd["s5_d"].reshape(1, 512), wd["s5_glu_b"].reshape(1, 512)
    norm_g = wd["gdn_norm_g"].reshape(1, 128)
    alog_row = _row_of(wd["gdn_a_log"], 128, C_HEADS)
    dtb_row = _row_of(wd["gdn_dt_bias"], 128, C_HEADS)
    conv_w = wd["gdn_conv"]

    a_re, a_im = wd["s5_a_re"], wd["s5_a_im"]
    log_dt = wd["s5_log_dt"].reshape(B_GROUPS, 1)
    bt_re = wd["s5_b_re"].transpose(0, 2, 1).reshape(B_WIDTH, B_STATE)
    bt_im = wd["s5_b_im"].transpose(0, 2, 1).reshape(B_WIDTH, B_STATE)
    abar_r, abar_i, bbar_r, bbar_i = _s5_params(a_re, a_im, log_dt, bt_re, bt_im)
    abr, abi = abar_r.reshape(1, -1), abar_i.reshape(1, -1)
    btr, bti = _blockdiag_b(bbar_r).astype(BF), _blockdiag_b(bbar_i).astype(BF)
    ctr, cti = _blockdiag_c(wd["s5_c_re"]).astype(BF), _blockdiag_c(wd["s5_c_im"]).astype(BF)

    table = _bucket_table()
    biases = _attn_bias(wd["rel_bias"], jnp.asarray(table))
    front = _l0_front(x, pre_g0, scale0, shift0, w_in0)
    qs, ks, vs = front[0:3], front[3:6], front[6:9]
    u, ga, gb, h0 = front[9:]
    os, ls = zip(*[_attn_fwd(qs[i], ks[i], vs[i], biases[i]) for i in range(3)])
    (xr, xi, ypre), late = _s5_scan_fwd(u, btr, bti, ctr, cti, abr, abi, exchange=None if comm is None else (comm.late_halves(), False))
    if comm is not None:
        wd = {**wd, **comm.late_weights(late)}
    w_out0 = wd["ab_w_out"].astype(BF)
    glu_w = wd["s5_glu_w"].astype(BF)
    w_in1 = jnp.concatenate([wd["gdn_w_in"], jnp.zeros((D_MODEL, C_IN_PAD - wd["gdn_w_in"].shape[1]), wd["gdn_w_in"].dtype)], axis=1).astype(BF)
    w_out1 = wd["gdn_w_out"].astype(BF)
    x1, y0 = _l0_out(os, ls, ga, gb, ypre, u, x, d_skip, glu_w, glu_b, w_out0, post_g0, gate0)

    raw, gate_c, ba, h1 = _l1_front(x1, pre_g1, scale1, shift1, w_in1)
    qkv, bg = _gdn_prep(raw, ba, conv_w, alog_row, dtb_row)
    o_gdn, states, invs = _gdn_fwd(qkv, bg)
    loss_row, dres1, do_gdn, dgate_c, dw_out1, dnorm_g, dpost_g1, dgate1 = _l1_out_fb(
        o_gdn, gate_c, x1, target, norm_g, w_out1, post_g1, gate1)

    dq1, dk1, dv1, dbg = _gdn_bwd(qkv, bg, states, invs, do_gdn)
    draw, dba, dconv_w, dalog_row, ddtb_row = _gdn_prep_bwd(raw, ba, conv_w, alog_row, dtb_row, dq1, dk1, dv1, dbg)
    dz1, dx1, dpre_g1, dscale1, dshift1 = _front_bwd(
        "l1_front_bwd", x1, pre_g1, scale1, shift1, w_in1, dres1, [[draw], [dgate_c], [dba]], [QKV, 1024, 128])
    dw_in1 = _matmul_tn("l1_dw_in", h1, dz1, 1408)

    n_w1 = wd["gdn_w_in"].shape[1]
    ex1 = None if comm is None else (comm.chip_partials("l1", {"gdn_w_in": dw_in1[:, :n_w1], "gdn_w_out": dw_out1}), True)
    l0b, got1 = _l0_out_bwd(os, ls, ga, gb, ypre, u, x, y0, d_skip, glu_w, glu_b, w_out0, post_g0, gate0, dx1, exchange=ex1)
    dos, dls = l0b[0:3], l0b[3:6]
    dga, dgb, dypre, du_skip, dd_skip, dglu_w, dglu_b, dw_out0, dpost_g0, dgate0 = l0b[6:]
    ex2 = None if comm is None else (comm.chip_partials("l0_out", {"ab_w_out": dw_out0, "s5_glu_w": dglu_w}), True)
    (du_scan, dbtr, dbti, dctr, dcti, dabr, dabi), got2 = _s5_scan_bwd(dypre, xr, xi, u, btr, bti, ctr, cti, abr, abi, exchange=ex2)
    if comm is not None:
        comm.received.update(zip(("gdn_w_in", "gdn_w_out", "ab_w_out", "s5_glu_w"), list(got1) + list(got2)))
    dqs, dks, dvs, dbs = [], [], [], []
    for i in range(3):
        dq_d, dk_d, dv_d, db_d = _attn_bwd(qs[i], ks[i], vs[i], biases[i], os[i], ls[i], dos[i], dls[i])
        dqs.append(dq_d)
        dks.append(dk_d)
        dvs.append(dv_d)
        dbs.append(db_d)
    parts = [dqs, dks, dvs, [du_skip, du_scan], [dga], [dgb]]
    dz0, grad_x, dpre_g0, dscale0, dshift0 = _front_bwd(
        "l0_front_bwd", x, pre_g0, scale0, shift0, w_in0, dx1, parts, [512] * 6)
    dw_in0 = _matmul_tn("l0_dw_in", h0, dz0, 768)

    idx_rows = jnp.asarray(table.reshape(3, -1), F32)
    drel = _rel_bias_grad(dbs, idx_rows).T
    da_re, da_im, dlog_dt, dbt_re, dbt_im = _s5_params_bwd(
        a_re, a_im, log_dt, bt_re, bt_im, dabr.reshape(B_GROUPS, B_STATE), dabi.reshape(B_GROUPS, B_STATE),
        _blockdiag_b_t(dbtr), _blockdiag_b_t(dbti))
    unb = lambda d: d.reshape(B_GROUPS, B_GROUP, B_STATE).transpose(0, 2, 1)
    grads = {
        "pre_g": jnp.concatenate([dpre_g0, dpre_g1], 0), "post_g": jnp.concatenate([dpost_g0, dpost_g1], 0),
        "rel_bias": drel, "ab_w_in": dw_in0, "ab_w_out": dw_out0,
        "s5_a_re": da_re, "s5_a_im": da_im, "s5_log_dt": dlog_dt.reshape(B_GROUPS),
        "s5_b_re": unb(dbt_re), "s5_b_im": unb(dbt_im),
        "s5_c_re": _blockdiag_c_t(dctr), "s5_c_im": _blockdiag_c_t(dcti),
        "s5_d": dd_skip.reshape(512), "s5_glu_w": dglu_w, "s5_glu_b": dglu_b.reshape(512),
        "gdn_w_in": dw_in1[:, :wd["gdn_w_in"].shape[1]], "gdn_conv": dconv_w,
        "gdn_a_log": dalog_row[0, C_HEADS:2 * C_HEADS], "gdn_dt_bias": ddtb_row[0, C_HEADS:2 * C_HEADS],
        "gdn_norm_g": dnorm_g.reshape(128), "gdn_w_out": dw_out1,
    }
    dmod = jnp.concatenate([jnp.concatenate([dshift0, dscale0, dgate0], 1), jnp.concatenate([dshift1, dscale1, dgate1], 1)], 0)
    return loss_row[0, 0], grad_x, grads, dmod


def _place():
    return lax.axis_index("x"), lax.axis_index("y"), lax.axis_index("c")


def _flip(v, bit):
    return 1 - v if bit else v


def _hbm_call(name, body, arrs, out_shapes, n_sem):
    any_spec = pl.BlockSpec(memory_space=pl.ANY)
    return pl.pallas_call(
        body, name=name,
        in_specs=[any_spec] * len(arrs), out_specs=[any_spec] * len(out_shapes), out_shape=out_shapes,
        scratch_shapes=[pltpu.SemaphoreType.DMA((n_sem,)), pltpu.SemaphoreType.DMA((n_sem,))],
    )(*arrs)


def _own_slot(gathered, own, slot):
    idx = lax.broadcasted_iota(jnp.int32, (gathered.shape[0],) + (1,) * own.ndim, 0)
    return jnp.where(idx == slot, own[None], gathered)


def _all_gather8(name, arr):
    def body(x_ref, out_ref, send_sems, recv_sems):
        x, y, c = _place()
        me = 4 * x + 2 * y + c
        sends, recvs = [], []
        for m in range(1, 8):
            peer = (_flip(x, m & 4), _flip(y, m & 2), _flip(c, m & 1))
            sends.append(pltpu.make_async_remote_copy(x_ref, out_ref.at[me], send_sems.at[m - 1], recv_sems.at[m - 1],
                                                      device_id=peer, device_id_type=MESH))
            recvs.append(pltpu.make_async_remote_copy(x_ref, out_ref.at[4 * peer[0] + 2 * peer[1] + peer[2]], send_sems.at[m - 1],
                                                      recv_sems.at[m - 1], device_id=peer, device_id_type=MESH))
        for cp in sends:
            cp.start()
        for cp in recvs:
            cp.wait_recv()
        for cp in sends:
            cp.wait_send()

    return _hbm_call(name, body, [arr], [jax.ShapeDtypeStruct((8,) + arr.shape, arr.dtype)], 7)[0]


def _all_to_all8(name, arr):
    def body(x_ref, out_ref, send_sems, recv_sems):
        x, y, c = _place()
        me = 4 * x + 2 * y + c
        sends, recvs = [], []
        for m in range(1, 8):
            peer = (_flip(x, m & 4), _flip(y, m & 2), _flip(c, m & 1))
            peer_id = 4 * peer[0] + 2 * peer[1] + peer[2]
            sends.append(pltpu.make_async_remote_copy(x_ref.at[peer_id], out_ref.at[me], send_sems.at[m - 1], recv_sems.at[m - 1],
                                                      device_id=peer, device_id_type=MESH))
            recvs.append(pltpu.make_async_remote_copy(x_ref.at[peer_id], out_ref.at[peer_id], send_sems.at[m - 1], recv_sems.at[m - 1],
                                                      device_id=peer, device_id_type=MESH))
        for cp in sends:
            cp.start()
        for cp in recvs:
            cp.wait_recv()
        for cp in sends:
            cp.wait_send()

    return _hbm_call(name, body, [arr], [jax.ShapeDtypeStruct(arr.shape, arr.dtype)], 7)[0]


def _chip_copies(ins, outs, send_sems, recv_sems, scatter):
    x, y, c = _place()
    mine = 2 * x + y
    sends, recvs = [], []
    for a in range(len(ins)):
        for m in range(1, 4):
            px, py = _flip(x, m & 2), _flip(y, m & 1)
            k = 3 * a + m - 1
            src = ins[a].at[2 * px + py] if scatter else ins[a]
            sends.append(pltpu.make_async_remote_copy(src, outs[a].at[mine], send_sems.at[k], recv_sems.at[k],
                                                      device_id=(px, py, c), device_id_type=MESH))
            recvs.append(pltpu.make_async_remote_copy(src, outs[a].at[2 * px + py], send_sems.at[k], recv_sems.at[k],
                                                      device_id=(px, py, c), device_id_type=MESH))
    return sends, recvs


def _chip_shapes(arrs, scatter):
    return [jax.ShapeDtypeStruct(a.shape if scatter else (4,) + a.shape, a.dtype) for a in arrs]


def _chip_exchange(name, arrs, scatter):
    n = len(arrs)

    def body(*refs):
        sends, recvs = _chip_copies(refs[:n], refs[n:2 * n], refs[2 * n], refs[2 * n + 1], scatter)
        for cp in sends:
            cp.start()
        for cp in recvs:
            cp.wait_recv()
        for cp in sends:
            cp.wait_send()

    return _hbm_call(name, body, arrs, _chip_shapes(arrs, scatter), 3 * n)


def _call_with_exchange(body, name, grid, in_specs, out_specs, out_shape, scratch_shapes, args, exchange):
    if exchange is None:
        return pl.pallas_call(body, name=name, grid=grid, in_specs=in_specs, out_specs=out_specs, out_shape=out_shape,
                              scratch_shapes=scratch_shapes, compiler_params=_cp(*["arbitrary"] * len(grid)))(*args), []
    arrs, scatter = exchange
    n_in, n_out, n_ex, n_scr = len(in_specs), len(out_specs), len(arrs), len(scratch_shapes)

    def fused(*refs):
        ins, ex_in = refs[:n_in], refs[n_in:n_in + n_ex]
        outs, ex_out = refs[n_in + n_ex:n_in + n_ex + n_out], refs[n_in + n_ex + n_out:n_in + 2 * n_ex + n_out]
        rest = refs[n_in + 2 * n_ex + n_out:]
        sends, recvs = _chip_copies(ex_in, ex_out, rest[n_scr], rest[n_scr + 1], scatter)
        first, last = pl.program_id(0) == 0, pl.program_id(0) == grid[0] - 1
        for k in range(1, len(grid)):
            first, last = first & (pl.program_id(k) == 0), last & (pl.program_id(k) == grid[k] - 1)

        @pl.when(first)
        def _():
            for cp in sends:
                cp.start()

        body(*ins, *outs, *rest[:n_scr])

        @pl.when(last)
        def _():
            for cp in recvs:
                cp.wait_recv()
            for cp in sends:
                cp.wait_send()

    any_spec = pl.BlockSpec(memory_space=pl.ANY)
    res = pl.pallas_call(
        fused, name=name, grid=grid, in_specs=list(in_specs) + [any_spec] * n_ex, out_specs=list(out_specs) + [any_spec] * n_ex,
        out_shape=list(out_shape) + _chip_shapes(arrs, scatter),
        scratch_shapes=list(scratch_shapes) + [pltpu.SemaphoreType.DMA((3 * n_ex,))] * 2,
        compiler_params=_cp(*["arbitrary"] * len(grid)))(*args, *arrs)
    return res[:n_out], res[n_out:]


def _sibling_exchange(name, arrs):
    n = len(arrs)

    def body(*refs):
        ins, outs = refs[:n], refs[n:2 * n]
        send_sems, recv_sems = refs[2 * n:]
        x, y, c = _place()
        copies = [pltpu.make_async_remote_copy(ins[a], outs[a], send_sems.at[a], recv_sems.at[a],
                                               device_id=(x, y, 1 - c), device_id_type=MESH) for a in range(n)]
        for cp in copies:
            cp.start()
        for cp in copies:
            cp.wait_recv()
        for cp in copies:
            cp.wait_send()

    return _hbm_call(name, body, arrs, [jax.ShapeDtypeStruct(a.shape, a.dtype) for a in arrs], n)


def _row_tile(rows):
    for t in (256, 128, 64, 32, 16, 8):
        if rows % t == 0:
            return t
    return rows


def _pair_sum(name, a, b, out_dtype):
    rows, cols = a.shape
    tr = _row_tile(rows)

    def body(a_ref, b_ref, o_ref):
        o_ref[...] = (a_ref[...] + b_ref[...]).astype(out_dtype)

    return pl.pallas_call(body, name=name, grid=(rows // tr,), in_specs=[_row(tr, cols)] * 2, out_specs=_row(tr, cols),
                          out_shape=_sds(rows, cols, dtype=out_dtype), compiler_params=_cp("arbitrary"))(a, b)


def _chip_sum(name, recv, partial, mine):
    n, rows, cols = recv.shape
    tr = _row_tile(rows)

    def body(mine_ref, *refs):
        own = refs[n][0].astype(F32)
        acc = None
        for s in range(n):
            term = jnp.where(mine_ref[0] == s, own, refs[s][0].astype(F32))
            acc = term if acc is None else acc + term
        refs[-1][...] = acc

    def slot_spec(s):
        return pl.BlockSpec((1, tr, cols), lambda i, m: (jnp.where(m[0] == s, (s + 1) % n, s), i, 0))

    grid_spec = pltpu.PrefetchScalarGridSpec(
        num_scalar_prefetch=1, grid=(rows // tr,),
        in_specs=[slot_spec(s) for s in range(n)] + [pl.BlockSpec((1, tr, cols), lambda i, m: (m[0], i, 0))],
        out_specs=pl.BlockSpec((tr, cols), lambda i, m: (i, 0)))
    return pl.pallas_call(body, name=name, grid_spec=grid_spec, out_shape=_sds(rows, cols),
                          compiler_params=_cp("arbitrary"))(mine, *([recv] * n), partial)


def _slot_sum(name, arr):
    n, rows, cols = arr.shape
    tr = _row_tile(rows)

    def body(*refs):
        acc = refs[0][0]
        for r in refs[1:-1]:
            acc = acc + r[0]
        refs[-1][...] = acc

    specs = [pl.BlockSpec((1, tr, cols), functools.partial(lambda s, i: (s, i, 0), s)) for s in range(n)]
    return pl.pallas_call(body, name=name, grid=(rows // tr,), in_specs=specs, out_specs=_row(tr, cols),
                          out_shape=_sds(rows, cols), compiler_params=_cp("arbitrary"))(*([arr] * n))


def _adamw(name, w, g, m, v):
    rows, cols = w.shape
    tr = _row_tile(rows)

    def body(w_ref, g_ref, m_ref, v_ref, d_ref, nm_ref, nv_ref):
        g_ = g_ref[...]
        m_ = ADAM_B1 * m_ref[...] + (1.0 - ADAM_B1) * g_
        v_ = ADAM_B2 * v_ref[...] + (1.0 - ADAM_B2) * (g_ * g_)
        m_hat = m_ / (1.0 - ADAM_B1 ** ADAM_STEP)
        v_hat = v_ / (1.0 - ADAM_B2 ** ADAM_STEP)
        d_ref[...] = -ADAM_LR * (m_hat / (jnp.sqrt(v_hat) + ADAM_EPS) + ADAM_WD * w_ref[...])
        nm_ref[...] = m_
        nv_ref[...] = v_

    spec = _row(tr, cols)
    return pl.pallas_call(body, name=name, grid=(rows // tr,), in_specs=[spec] * 4, out_specs=[spec] * 3,
                          out_shape=[_sds(rows, cols)] * 3, compiler_params=_cp("arbitrary"))(w, g, m, v)


def _adamw_halves(name, w, g_mine, g_sibling, m, v, core):
    _, rows, cols = w.shape
    half = rows // 2
    tr = _row_tile(half)
    per_half = half // tr

    def body(core_ref, w_ref, gm_ref, gs_ref, m_ref, v_ref, g_ref, d_ref, nm_ref, nv_ref):
        g_ = jnp.where(pl.program_id(0) // per_half == core_ref[0], gm_ref[...], gs_ref[...])
        m_ = ADAM_B1 * m_ref[...] + (1.0 - ADAM_B1) * g_
        v_ = ADAM_B2 * v_ref[...] + (1.0 - ADAM_B2) * (g_ * g_)
        m_hat = m_ / (1.0 - ADAM_B1 ** ADAM_STEP)
        v_hat = v_ / (1.0 - ADAM_B2 ** ADAM_STEP)
        g_ref[...] = g_
        d_ref[...] = -ADAM_LR * (m_hat / (jnp.sqrt(v_hat) + ADAM_EPS) + ADAM_WD * w_ref[...])
        nm_ref[...] = m_
        nv_ref[...] = v_

    full = pl.BlockSpec((None, tr, cols), lambda i, c: (0, i, 0))
    in_half = pl.BlockSpec((tr, cols), lambda i, c: (i % per_half, 0))
    grid_spec = pltpu.PrefetchScalarGridSpec(num_scalar_prefetch=1, grid=(rows // tr,),
                                             in_specs=[full, in_half, in_half, full, full], out_specs=[full] * 4)
    return pl.pallas_call(body, name=name, grid_spec=grid_spec, out_shape=[_sds(1, rows, cols)] * 4,
                          compiler_params=_cp("arbitrary"))(core, w, g_mine, g_sibling, m, v)


def _mod_local(c_all, ada_w):
    def body(c_ref, w_ref, o_ref):
        c_act = jax.nn.silu(c_ref[...])
        for l in range(2):
            o_ref[l] = _hdot(c_act, w_ref[l])

    return pl.pallas_call(body, name="mod_local", out_shape=_sds(2, 8, ada_w.shape[2]),
                          compiler_params=pltpu.CompilerParams(vmem_limit_bytes=VMEM_LIMIT_BYTES))(c_all, ada_w)


def _ada_w_grad(c_all, dmod_cols):
    def body(c_ref, d_ref, o_ref):
        c_act = jax.nn.silu(c_ref[...])
        for l in range(2):
            o_ref[l] = lax.dot_general(c_act, d_ref[l], (((0,), (0,)), ((), ())), precision=HI, preferred_element_type=F32)

    return pl.pallas_call(body, name="ada_w_grad", out_shape=_sds(2, D_MODEL, dmod_cols.shape[2]),
                          compiler_params=pltpu.CompilerParams(vmem_limit_bytes=VMEM_LIMIT_BYTES))(c_all, dmod_cols)


_SMALL = ("ada_b", "pre_g", "post_g", "rel_bias", "s5_a_re", "s5_a_im", "s5_log_dt", "s5_b_re", "s5_b_im", "s5_c_re", "s5_c_im",
          "s5_d", "s5_glu_b", "gdn_a_log", "gdn_dt_bias", "gdn_norm_g")
_SHARDED = ("ab_w_in", "ab_w_out", "s5_glu_w", "gdn_w_in", "gdn_w_out")
_COL_SHARDED = ("ab_w_in", "gdn_w_in")
_WEIGHTS = ("ada_w", "ada_b", "pre_g", "post_g", "rel_bias", "ab_w_in", "ab_w_out", "s5_a_re", "s5_a_im", "s5_log_dt", "s5_b_re",
            "s5_b_im", "s5_c_re", "s5_c_im", "s5_d", "s5_glu_w", "s5_glu_b", "gdn_w_in", "gdn_conv", "gdn_a_log", "gdn_dt_bias",
            "gdn_norm_g", "gdn_w_out")


def _rows128(n):
    return -(-n // 128)


def _pack(arrs, total_rows):
    pieces = []
    for a in arrs:
        flat = a.reshape(-1)
        pieces.append(jnp.pad(flat, (0, _rows128(flat.shape[0]) * 128 - flat.shape[0])).reshape(-1, 128))
    used = sum(p.shape[0] for p in pieces)
    pieces.append(jnp.zeros((total_rows - used, 128), F32))
    return jnp.concatenate(pieces, axis=0)


def _unpack(buf, shapes):
    out, at = [], 0
    for shp in shapes:
        n = int(np.prod(shp))
        out.append(buf[at:at + _rows128(n)].reshape(-1)[:n].reshape(shp))
        at += _rows128(n)
    return out


def _full_from_halves(name, g):
    if name in _COL_SHARDED:
        return g.transpose(0, 2, 1, 3).reshape(2 * g.shape[2], 4 * g.shape[3])
    return g.transpose(1, 0, 2, 3).reshape(8 * g.shape[2], g.shape[3])


def _shard_major(name, g):
    if name in _COL_SHARDED:
        return g.reshape(g.shape[0], 4, g.shape[1] // 4).transpose(1, 0, 2)
    return g.reshape(4, g.shape[0] // 4, g.shape[1])


_LATE = ("ab_w_out", "s5_glu_w", "gdn_w_in", "gdn_w_out")


class _WeightExchanges:
    def __init__(self, shards, core, chip):
        self.core, self.chip = core, chip
        self.half = {}
        for name, shard in shards.items():
            h = shard.shape[0] // 2
            self.half[name] = lax.dynamic_slice_in_dim(shard.astype(BF), core * h, h, axis=0)
        self.partial, self.received = {}, {}

    def _full(self, label, names, from_chips):
        mine = [_own_slot(g, self.half[n], self.chip) for n, g in zip(names, from_chips)]
        theirs = _sibling_exchange("gather_w_sibling_" + label, mine)
        return {n: _full_from_halves(n, jnp.where(self.core == 0, jnp.stack([a, b], 0), jnp.stack([b, a], 0)))
                for n, a, b in zip(names, mine, theirs)}

    def first_weights(self):
        return self._full("first", ["ab_w_in"], _chip_exchange("gather_w_chips", [self.half["ab_w_in"]], False))

    def late_halves(self):
        return [self.half[n] for n in _LATE]

    def late_weights(self, from_chips):
        return self._full("late", list(_LATE), from_chips)

    def chip_partials(self, label, grads):
        mine, other = [], []
        for name, g in grads.items():
            sm = _shard_major(name, g)
            h = sm.shape[1] // 2
            mine.append(lax.dynamic_slice_in_dim(sm, self.core * h, h, axis=1))
            other.append(lax.dynamic_slice_in_dim(sm, (1 - self.core) * h, h, axis=1))
        out = []
        for name, a, b in zip(grads, mine, _sibling_exchange("reduce_sibling_" + label, other)):
            flat = lambda t: t.reshape(-1, t.shape[-1])
            self.partial[name] = _pair_sum("sum_sibling_" + name, flat(a), flat(b), BF).reshape(a.shape)
            out.append(self.partial[name])
        return out


def kernel(x, c, ada_w, ada_b, pre_g, post_g, rel_bias, ab_w_in, ab_w_out, s5_a_re, s5_a_im, s5_log_dt, s5_b_re, s5_b_im, s5_c_re, s5_c_im, s5_d, s5_glu_w, s5_glu_b, gdn_w_in, gdn_conv, gdn_a_log, gdn_dt_bias, gdn_norm_g, gdn_w_out, loss_target, m_ada_w, m_ada_b, m_pre_g, m_post_g, m_rel_bias, m_ab_w_in, m_ab_w_out, m_s5_a_re, m_s5_a_im, m_s5_log_dt, m_s5_b_re, m_s5_b_im, m_s5_c_re, m_s5_c_im, m_s5_d, m_s5_glu_w, m_s5_glu_b, m_gdn_w_in, m_gdn_conv, m_gdn_a_log, m_gdn_dt_bias, m_gdn_norm_g, m_gdn_w_out, v_ada_w, v_ada_b, v_pre_g, v_post_g, v_rel_bias, v_ab_w_in, v_ab_w_out, v_s5_a_re, v_s5_a_im, v_s5_log_dt, v_s5_b_re, v_s5_b_im, v_s5_c_re, v_s5_c_im, v_s5_d, v_s5_glu_w, v_s5_glu_b, v_gdn_w_in, v_gdn_conv, v_gdn_a_log, v_gdn_dt_bias, v_gdn_norm_g, v_gdn_w_out):
    w = dict(ada_w=ada_w, ada_b=ada_b, pre_g=pre_g, post_g=post_g, rel_bias=rel_bias, ab_w_in=ab_w_in, ab_w_out=ab_w_out,
             s5_a_re=s5_a_re, s5_a_im=s5_a_im, s5_log_dt=s5_log_dt, s5_b_re=s5_b_re, s5_b_im=s5_b_im, s5_c_re=s5_c_re, s5_c_im=s5_c_im,
             s5_d=s5_d, s5_glu_w=s5_glu_w, s5_glu_b=s5_glu_b, gdn_w_in=gdn_w_in, gdn_conv=gdn_conv, gdn_a_log=gdn_a_log,
             gdn_dt_bias=gdn_dt_bias, gdn_norm_g=gdn_norm_g, gdn_w_out=gdn_w_out)
    m = dict(ada_w=m_ada_w, ada_b=m_ada_b, pre_g=m_pre_g, post_g=m_post_g, rel_bias=m_rel_bias, ab_w_in=m_ab_w_in, ab_w_out=m_ab_w_out,
             s5_a_re=m_s5_a_re, s5_a_im=m_s5_a_im, s5_log_dt=m_s5_log_dt, s5_b_re=m_s5_b_re, s5_b_im=m_s5_b_im, s5_c_re=m_s5_c_re,
             s5_c_im=m_s5_c_im, s5_d=m_s5_d, s5_glu_w=m_s5_glu_w, s5_glu_b=m_s5_glu_b, gdn_w_in=m_gdn_w_in, gdn_conv=m_gdn_conv,
             gdn_a_log=m_gdn_a_log, gdn_dt_bias=m_gdn_dt_bias, gdn_norm_g=m_gdn_norm_g, gdn_w_out=m_gdn_w_out)
    v = dict(ada_w=v_ada_w, ada_b=v_ada_b, pre_g=v_pre_g, post_g=v_post_g, rel_bias=v_rel_bias, ab_w_in=v_ab_w_in, ab_w_out=v_ab_w_out,
             s5_a_re=v_s5_a_re, s5_a_im=v_s5_a_im, s5_log_dt=v_s5_log_dt, s5_b_re=v_s5_b_re, s5_b_im=v_s5_b_im, s5_c_re=v_s5_c_re,
             s5_c_im=v_s5_c_im, s5_d=v_s5_d, s5_glu_w=v_s5_glu_w, s5_glu_b=v_s5_glu_b, gdn_w_in=v_gdn_w_in, gdn_conv=v_gdn_conv,
             gdn_a_log=v_gdn_a_log, gdn_dt_bias=v_gdn_dt_bias, gdn_norm_g=v_gdn_norm_g, gdn_w_out=v_gdn_w_out)
    ix, iy, ic = _place()
    me = 4 * ix + 2 * iy + ic
    chip = 2 * ix + iy
    n_cols = ada_w.shape[2]

    mine_first = _pack([c, gdn_conv], 32)
    first = _own_slot(_all_gather8("gather_c_conv", mine_first), mine_first, me)
    c_all = first[:, 0:8].reshape(8, D_MODEL)
    conv_full = first[0::2, 8:32].reshape(4, C_CONV, n_cols).transpose(1, 0, 2).reshape(C_CONV, 4 * n_cols)
    mine_mod = _mod_local(c_all, ada_w)
    modl = _own_slot(_all_gather8("gather_mod", mine_mod), mine_mod, me)
    mod = lax.dynamic_index_in_dim(modl[0::2], me, axis=2, keepdims=False)
    mod = mod.transpose(1, 0, 2).reshape(2, 4 * n_cols) + ada_b

    comm = _WeightExchanges({name: w[name][0] for name in _SHARDED}, ic, chip)
    wd = {name: w[name] for name in _SMALL if name != "ada_b"}
    wd = {k: (a if k in ("pre_g", "post_g", "rel_bias") else a[0]) for k, a in wd.items()}
    wd["gdn_conv"] = conv_full
    wd.update(comm.first_weights())

    loss_local, grad_x, grads, dmod = _local_step(x[0], loss_target[0], mod, wd, comm)
    loss = lax.psum(loss_local, ("x", "y", "c"))

    small_shapes = [w[name].shape for name in _SMALL] + [(C_CONV, 4 * n_cols)]
    small_rows = -(-sum(_rows128(int(np.prod(s))) for s in small_shapes) // 64) * 64
    per_dev, dmod_rows = small_rows // 8, _rows128(2 * 3 * D_MODEL)
    partial = _pack([dmod] + [grads[name] for name in _SMALL[1:]] + [grads["gdn_conv"]], small_rows)
    outbound = jnp.concatenate([partial.reshape(8, per_dev, 128), jnp.broadcast_to(partial[None, :dmod_rows], (8, dmod_rows, 128))], axis=1)
    inbound = _own_slot(_all_to_all8("reduce_small_grads", outbound), lax.dynamic_index_in_dim(outbound, me, 0, keepdims=False), me)
    my_rows = _slot_sum("sum_small_grads", inbound[:, :per_dev])
    g_small = _own_slot(_all_gather8("gather_small_grads", my_rows), my_rows, me).reshape(small_rows, 128)
    g_list = _unpack(g_small, small_shapes)
    out_g, out_d, out_m, out_v = {}, {}, {}, {}

    def update(name, g2d):
        shp = w[name].shape
        two_d = lambda a: a.reshape(-1, shp[-1])
        d_, m_, v_ = _adamw("adamw_" + name, two_d(w[name]), g2d, two_d(m[name]), two_d(v[name]))
        out_g[name], out_d[name], out_m[name], out_v[name] = (a.reshape(shp) for a in (g2d, d_, m_, v_))

    for name, g in zip(_SMALL, g_list[:-1]):
        update(name, g.reshape(-1, g.shape[-1]))
    update("gdn_conv", lax.dynamic_slice_in_dim(g_list[-1], chip * n_cols, n_cols, axis=1))

    dmod_all = inbound[:, per_dev:].reshape(8, 2, 4, n_cols)
    dmod_cols = lax.dynamic_index_in_dim(dmod_all, chip, axis=2, keepdims=False).transpose(1, 0, 2)
    update("ada_w", _ada_w_grad(c_all, dmod_cols).reshape(-1, n_cols))

    comm.received["ab_w_in"] = _chip_exchange("reduce_chips", comm.chip_partials("l0_in", {"ab_w_in": grads["ab_w_in"]}), True)[0]
    chip_1 = jnp.reshape(chip, (1,)).astype(jnp.int32)
    core_1 = jnp.reshape(ic, (1,)).astype(jnp.int32)
    reduced = [_chip_sum("sum_chips_" + name, comm.received[name], comm.partial[name], chip_1) for name in _SHARDED]
    for name, g_mine, g_sib in zip(_SHARDED, reduced, _sibling_exchange("reduce_share", reduced)):
        out_g[name], out_d[name], out_m[name], out_v[name] = _adamw_halves(
            "adamw_" + name, w[name], g_mine, g_sib, m[name], v[name], core_1)

    return (loss, grad_x[None], *[out_g[n] for n in _WEIGHTS], *[out_d[n] for n in _WEIGHTS],
            *[out_m[n] for n in _WEIGHTS], *[out_v[n] for n in _WEIGHTS])
```

```python
import functools
import math

import numpy as np
import jax
import jax.numpy as jnp
from jax import lax
from jax.experimental import pallas as pl
from jax.experimental.pallas import tpu as pltpu

F32 = jnp.float32
BF = jnp.bfloat16
HI = lax.Precision.HIGHEST
MESH = pl.DeviceIdType.MESH

D_MODEL = 1024
EPS = 1e-6
A_HEADS, A_HD, A_WIDTH, A_BLOCK = 8, 64, 512, 128
DILATIONS = (1, 4, 16)
N_KEYS = 128
REL_BUCKETS, REL_MAX_DIST = 32, 2048
B_WIDTH, B_GROUP, B_GROUPS, B_STATE = 512, 16, 32, 64
S5_LANES = 512
S5_TILES = 4
S5_T = 256
C_HEADS, C_DK, C_CHUNK, C_CONV = 8, 128, 64, 4
QKV = 3072
C_IN_PAD = 4224
TM = 256
VMEM_LIMIT_BYTES = 56 * 1024 * 1024
ADAM_LR, ADAM_B1, ADAM_B2, ADAM_EPS, ADAM_WD, ADAM_STEP = 0.001, 0.9, 0.999, 1e-08, 0.01, 10
NEG = float(np.finfo(np.float32).min)


def _cp(*sem):
    return pltpu.CompilerParams(dimension_semantics=sem, vmem_limit_bytes=VMEM_LIMIT_BYTES)


def _bdot(a, b):
    return jnp.dot(a.astype(BF), b.astype(BF), preferred_element_type=F32)


def _bdot_nt(a, b):
    return lax.dot_general(a.astype(BF), b.astype(BF), (((1,), (1,)), ((), ())), preferred_element_type=F32)


def _bdot_tn(a, b):
    return lax.dot_general(a.astype(BF), b.astype(BF), (((0,), (0,)), ((), ())), preferred_element_type=F32)


def _hdot(a, b):
    return jnp.dot(a, b, precision=HI, preferred_element_type=F32)


def _bein(eq, a, b):
    return jnp.einsum(eq, a.astype(BF), b.astype(BF), preferred_element_type=F32)


def _hein(eq, a, b):
    return jnp.einsum(eq, a, b, precision=HI, preferred_element_type=F32)


def _row(tm, n):
    return pl.BlockSpec((tm, n), lambda i: (i, 0))


def _fix(shape):
    return pl.BlockSpec(shape, lambda i: (0,) * len(shape))


def _sds(*shape, dtype=F32):
    return jax.ShapeDtypeStruct(shape, dtype)


def _acc(ref, val):
    ref[...] += val


def _zero_at_first(refs, axis=0):
    @pl.when(pl.program_id(axis) == 0)
    def _():
        for r in refs:
            r[...] = jnp.zeros_like(r)


def _rms(x):
    return x * lax.rsqrt(jnp.mean(x * x, axis=-1, keepdims=True) + EPS)


def _pre_mod(x, g, scale, shift):
    return (_rms(x) * g) * (1.0 + scale) + shift


def _post_res(y, x, post_g, gate):
    return x + gate * (_rms(y) * post_g)


def _merge_gate(o1, o2, o3, l1, l2, l3, ga):
    m = jnp.maximum(jnp.maximum(l1, l2), l3)
    e1, e2, e3 = jnp.exp(l1 - m), jnp.exp(l2 - m), jnp.exp(l3 - m)
    inv = 1.0 / (e1 + e2 + e3)
    return ((e1 * inv) * o1 + (e2 * inv) * o2 + (e3 * inv) * o3) * jax.nn.silu(ga)


def _s5_gelu(ypre, u, d_skip):
    return jax.nn.gelu(ypre + d_skip * u)


def _s5_glu(yb, gl, gb):
    return yb * jax.nn.sigmoid(gl) * jax.nn.silu(gb)


def _l0_front(x, pre_g, scale, shift, w_in):
    s_len = x.shape[0]

    def body(x_ref, g_ref, sc_ref, sh_ref, w_ref, *out_refs):
        qkv_refs, (u_ref, ga_ref, gb_ref, h_ref) = out_refs[:9], out_refs[9:]
        hb = _pre_mod(x_ref[...], g_ref[...], sc_ref[...], sh_ref[...]).astype(BF)
        h_ref[...] = hb
        z = jnp.dot(hb, w_ref[...], preferred_element_type=F32)
        for a in range(3):
            piece = z[:, a * 512:(a + 1) * 512]
            for i, d in enumerate(DILATIONS):
                qkv_refs[3 * a + i][...] = _to_res(piece, d).astype(BF)
        u_ref[...] = z[:, 1536:2048]
        ga_ref[...] = z[:, 2048:2560]
        gb_ref[...] = z[:, 2560:3072]

    vec = _fix((1, D_MODEL))
    return pl.pallas_call(
        body, name="l0_front", grid=(s_len // TM,),
        in_specs=[_row(TM, D_MODEL), vec, vec, vec, _fix((D_MODEL, 3072))],
        out_specs=[_res_spec(d) for d in DILATIONS] * 3 + [_row(TM, 512)] * 3 + [_row(TM, D_MODEL)],
        out_shape=[_sds(*_res_shape(s_len, d), dtype=BF) for d in DILATIONS] * 3 + [_sds(s_len, 512)] * 3 + [_sds(s_len, D_MODEL, dtype=BF)],
        compiler_params=_cp("arbitrary"),
    )(x, pre_g, scale, shift, w_in)


def _front_bwd(name, x, pre_g, scale, shift, w_in, dres, parts, widths):
    s_len = x.shape[0]
    n_in = sum(len(p) for p in parts)
    n_cols = sum(widths)

    def body(*refs):
        x_ref, g_ref, sc_ref, sh_ref, w_ref, dres_ref = refs[:6]
        part_refs = refs[6:6 + n_in]
        dz_ref, dx_ref, dg_ref, dsc_ref, dsh_ref = refs[6 + n_in:]
        _zero_at_first([dg_ref, dsc_ref, dsh_ref])
        _, vjp = jax.vjp(_pre_mod, x_ref[...], g_ref[...], sc_ref[...], sh_ref[...])
        dh = jnp.zeros((TM, D_MODEL), F32)
        col, at = 0, 0
        for grp, width in zip(parts, widths):
            tile = lambda r: _from_res(r[...]) if len(r.shape) == 3 else r[...]
            dz = tile(part_refs[at])
            for r in part_refs[at + 1:at + len(grp)]:
                dz = dz + tile(r)
            at += len(grp)
            dzb = dz.astype(BF)
            dz_ref[:, col:col + width] = dzb
            dh = dh + lax.dot_general(dzb, w_ref[:, col:col + width], (((1,), (1,)), ((), ())), preferred_element_type=F32)
            col += width
        dx, dg, dsc, dsh = vjp(dh)
        dx_ref[...] = dx + dres_ref[...]
        _acc(dg_ref, dg)
        _acc(dsc_ref, dsc)
        _acc(dsh_ref, dsh)

    vec = _fix((1, D_MODEL))
    flat = [a for p in parts for a in p]
    return pl.pallas_call(
        body, name=name, grid=(s_len // TM,),
        in_specs=[_row(TM, D_MODEL), vec, vec, vec, _fix((D_MODEL, n_cols)), _row(TM, D_MODEL)]
        + [_res_spec(a.shape[0], a.shape[2]) if a.ndim == 3 else _row(TM, a.shape[1]) for a in flat],
        out_specs=[_row(TM, n_cols), _row(TM, D_MODEL), vec, vec, vec],
        out_shape=[_sds(s_len, n_cols, dtype=BF), _sds(s_len, D_MODEL), _sds(1, D_MODEL), _sds(1, D_MODEL), _sds(1, D_MODEL)],
        compiler_params=_cp("arbitrary"),
    )(x, pre_g, scale, shift, w_in, dres, *flat)


def _matmul_tn(name, a, b, tn):
    s_len, k_dim = a.shape
    n_dim = b.shape[1]
    ts = 512

    def body(a_ref, b_ref, o_ref):
        _zero_at_first([o_ref], axis=1)
        o_ref[...] += lax.dot_general(a_ref[...], b_ref[...], (((0,), (0,)), ((), ())), preferred_element_type=F32)

    return pl.pallas_call(
        body, name=name, grid=(n_dim // tn, s_len // ts),
        in_specs=[pl.BlockSpec((ts, k_dim), lambda j, i: (i, 0)), pl.BlockSpec((ts, tn), lambda j, i: (i, j))],
        out_specs=pl.BlockSpec((k_dim, tn), lambda j, i: (0, j)),
        out_shape=_sds(k_dim, n_dim),
        compiler_params=_cp("arbitrary", "arbitrary"),
    )(a, b)


def _t5_bucket_np(dist):
    dist = np.maximum(dist, 0)
    max_exact = REL_BUCKETS // 2
    large = max_exact + (np.log(np.maximum(dist, 1) / max_exact)
                         / math.log(REL_MAX_DIST / max_exact) * (REL_BUCKETS - max_exact)).astype(np.int32)
    large = np.minimum(large, REL_BUCKETS - 1)
    return np.where(dist < max_exact, dist, large).astype(np.int32)


def _to_res(z, dil):
    if dil == 1:
        return z[None]
    return jnp.swapaxes(z.reshape(z.shape[0] // dil, dil, z.shape[1]), 0, 1)


def _from_res(z):
    if z.shape[0] == 1:
        return z[0]
    return jnp.swapaxes(z, 0, 1).reshape(z.shape[0] * z.shape[1], z.shape[2])


def _res_shape(s_len, dil, width=A_WIDTH):
    return (dil, s_len // dil, width)


def _res_spec(dil, width=A_WIDTH):
    return pl.BlockSpec((dil, TM // dil, width), lambda i: (0, i, 0))


def _bucket_table():
    qi = np.arange(A_BLOCK)[:, None]
    kj = np.arange(2 * A_BLOCK)[None, :]
    return np.stack([_t5_bucket_np((qi + A_BLOCK - kj) * d) for d in DILATIONS], 0)


def _attn_mask(first):
    qi = lax.broadcasted_iota(jnp.int32, (A_BLOCK, 2 * A_BLOCK), 0)
    kj = lax.broadcasted_iota(jnp.int32, (A_BLOCK, 2 * A_BLOCK), 1)
    rel = qi + A_BLOCK - kj
    return (rel >= 0) & (rel <= N_KEYS) & (jnp.logical_not(first) | (kj >= A_BLOCK))


def _attn_specs(nb, rev):
    n_of = (lambda i: nb - 1 - i) if rev else (lambda i: i)
    cur = pl.BlockSpec((None, A_BLOCK, A_WIDTH), lambda r, i: (r, n_of(i), 0))
    prev = pl.BlockSpec((None, A_BLOCK, A_WIDTH), lambda r, i: (r, jnp.maximum(n_of(i) - 1, 0), 0))
    bias = pl.BlockSpec((A_HEADS, A_BLOCK, 2 * A_BLOCK), lambda r, i: (0, 0, 0))
    return cur, prev, bias


def _attn_fwd(q, k, v, bias):
    dil, t_len, _ = q.shape
    nb = t_len // A_BLOCK
    scale = A_HD ** -0.5

    def body(q_ref, kp_ref, kc_ref, vp_ref, vc_ref, b_ref, o_ref, l_ref):
        mask = _attn_mask(pl.program_id(1) == 0)
        lane = lax.broadcasted_iota(jnp.int32, (1, 128), 1)
        for hp in range(A_HEADS // 2):
            sl = slice(hp * 128, (hp + 1) * 128)
            qp = q_ref[:, sl]
            kw = jnp.concatenate([kp_ref[:, sl], kc_ref[:, sl]], axis=0).astype(BF)
            vw = jnp.concatenate([vp_ref[:, sl], vc_ref[:, sl]], axis=0).astype(BF)
            outs, lses = [], []
            for j in range(2):
                hm = (lane < 64) if j == 0 else (lane >= 64)
                s = _bdot_nt(jnp.where(hm, qp, 0.0), kw) * scale
                s = jnp.where(mask, s + b_ref[2 * hp + j], NEG)
                m = jnp.max(s, axis=-1, keepdims=True)
                p = jnp.exp(s - m)
                den = jnp.sum(p, axis=-1, keepdims=True)
                outs.append(_bdot(p, vw) / den)
                lses.append(m + jnp.log(den))
            hm0 = lane < 64
            o_ref[:, sl] = jnp.where(hm0, outs[0], outs[1])
            l_ref[:, sl] = jnp.where(hm0, lses[0], lses[1])

    cur, prev, bias_spec = _attn_specs(nb, False)
    return pl.pallas_call(
        body, name=f"attn_fwd_d{dil}", grid=(dil, nb),
        in_specs=[cur, prev, cur, prev, cur, bias_spec],
        out_specs=[cur, cur],
        out_shape=[_sds(dil, t_len, A_WIDTH)] * 2,
        compiler_params=_cp("arbitrary", "arbitrary"),
    )(q, k, k, v, v, bias)


def _attn_bwd(q, k, v, bias, o, l, do, dl):
    dil, t_len, _ = q.shape
    nb = t_len // A_BLOCK
    scale = A_HD ** -0.5

    def body(q_ref, kp_ref, kc_ref, vp_ref, vc_ref, b_ref, o_ref, l_ref, do_ref, dl_ref,
             dq_ref, dk_ref, dv_ref, db_ref, ck_ref, cv_ref):
        _zero_at_first([ck_ref, cv_ref], axis=1)

        @pl.when((pl.program_id(0) == 0) & (pl.program_id(1) == 0))
        def _():
            db_ref[...] = jnp.zeros_like(db_ref)

        mask = _attn_mask(pl.program_id(1) == nb - 1)
        lane = lax.broadcasted_iota(jnp.int32, (1, 128), 1)
        for hp in range(A_HEADS // 2):
            sl = slice(hp * 128, (hp + 1) * 128)
            qp = q_ref[:, sl]
            kw = jnp.concatenate([kp_ref[:, sl], kc_ref[:, sl]], axis=0).astype(BF)
            vw = jnp.concatenate([vp_ref[:, sl], vc_ref[:, sl]], axis=0).astype(BF)
            op, lp, dop, dlp = o_ref[:, sl], l_ref[:, sl], do_ref[:, sl], dl_ref[:, sl]
            dq_acc = jnp.zeros((A_BLOCK, 128), F32)
            dk_acc = jnp.zeros((2 * A_BLOCK, 128), F32)
            dv_acc = jnp.zeros((2 * A_BLOCK, 128), F32)
            for j in range(2):
                hm = (lane < 64) if j == 0 else (lane >= 64)
                qm = jnp.where(hm, qp, 0.0)
                s = _bdot_nt(qm, kw) * scale
                s = jnp.where(mask, s + b_ref[2 * hp + j], NEG)
                lse = jnp.max(jnp.where(hm, lp, NEG), axis=-1, keepdims=True)
                p = jnp.exp(s - lse)
                do_h = jnp.where(hm, dop, 0.0)
                dd = jnp.sum(do_h * op, axis=-1, keepdims=True)
                dlse = jnp.sum(jnp.where(hm, dlp, 0.0), axis=-1, keepdims=True)
                ds = p * (_bdot_nt(do_h, vw) - dd + dlse)
                dv_acc = dv_acc + _bdot_tn(p, do_h)
                dq_acc = dq_acc + jnp.where(hm, _bdot(ds, kw), 0.0) * scale
                dk_acc = dk_acc + _bdot_tn(ds, qm) * scale
                db_ref[2 * hp + j] += ds
            dq_ref[:, sl] = dq_acc
            dk_ref[:, sl] = dk_acc[A_BLOCK:] + ck_ref[:, sl]
            dv_ref[:, sl] = dv_acc[A_BLOCK:] + cv_ref[:, sl]
            ck_ref[:, sl] = dk_acc[:A_BLOCK]
            cv_ref[:, sl] = dv_acc[:A_BLOCK]

    cur, prev, bias_spec = _attn_specs(nb, True)
    return pl.pallas_call(
        body, name=f"attn_bwd_d{dil}", grid=(dil, nb),
        in_specs=[cur, prev, cur, prev, cur, bias_spec, cur, cur, cur, cur],
        out_specs=[cur, cur, cur, bias_spec],
        out_shape=[_sds(dil, t_len, A_WIDTH)] * 3 + [_sds(A_HEADS, A_BLOCK, 2 * A_BLOCK)],
        scratch_shapes=[pltpu.VMEM((A_BLOCK, A_WIDTH), F32)] * 2,
        compiler_params=_cp("arbitrary", "arbitrary"),
    )(q, k, k, v, v, bias, o, l, do, dl)


def _attn_bias(rel_bias, table):
    def body(rb_ref, t_ref, *o_refs):
        for c in range(3):
            t = t_ref[c]
            acc = [jnp.zeros((A_BLOCK, 2 * A_BLOCK), F32) for _ in range(A_HEADS)]
            for b in range(REL_BUCKETS):
                hit = t == b
                acc = [jnp.where(hit, rb_ref[b, h], acc[h]) for h in range(A_HEADS)]
            for h in range(A_HEADS):
                o_refs[c][h] = acc[h]

    return pl.pallas_call(body, name="attn_bias", out_shape=[_sds(A_HEADS, A_BLOCK, 2 * A_BLOCK)] * 3,
                          in_specs=[pl.BlockSpec(memory_space=pltpu.SMEM), pl.BlockSpec(memory_space=pltpu.VMEM)],
                          compiler_params=pltpu.CompilerParams(vmem_limit_bytes=VMEM_LIMIT_BYTES))(rel_bias, table)


def _rel_bias_grad(dbs, idx_rows):
    n = A_BLOCK * 2 * A_BLOCK

    def body(d0_ref, d1_ref, d2_ref, idx_ref, o_ref):
        bucket = lax.broadcasted_iota(jnp.int32, (REL_BUCKETS, n), 0).astype(F32)
        acc = jnp.zeros((A_HEADS, REL_BUCKETS), F32)
        for c, db_ref in enumerate((d0_ref, d1_ref, d2_ref)):
            onehot = (idx_ref[c:c + 1, :] == bucket).astype(F32)
            acc = acc + lax.dot_general(db_ref[...], onehot, (((1,), (1,)), ((), ())), precision=HI, preferred_element_type=F32)
        o_ref[...] = acc

    return pl.pallas_call(body, name="rel_bias_grad", out_shape=_sds(A_HEADS, REL_BUCKETS),
                          compiler_params=pltpu.CompilerParams(vmem_limit_bytes=VMEM_LIMIT_BYTES))(
                              *[d.reshape(A_HEADS, n) for d in dbs], idx_rows)


def _s5_param_fn(a_re, a_im, log_dt, bt_re, bt_im):
    dt = jnp.exp(log_dt)
    mag = jnp.exp(dt * a_re)
    abar_r, abar_i = mag * jnp.cos(dt * a_im), mag * jnp.sin(dt * a_im)
    den = a_re * a_re + a_im * a_im
    fr = ((abar_r - 1.0) * a_re + abar_i * a_im) / den
    fi = (abar_i * a_re - (abar_r - 1.0) * a_im) / den
    row = lax.broadcasted_iota(jnp.int32, (B_WIDTH, B_GROUPS), 0)
    grp = lax.broadcasted_iota(jnp.int32, (B_WIDTH, B_GROUPS), 1)
    expand = ((row // B_GROUP) == grp).astype(F32)
    fr_e, fi_e = _hdot(expand, fr), _hdot(expand, fi)
    return abar_r, abar_i, fr_e * bt_re - fi_e * bt_im, fr_e * bt_im + fi_e * bt_re


def _s5_params(a_re, a_im, log_dt, bt_re, bt_im):
    def body(ar, ai, ld, br, bi, o1, o2, o3, o4):
        o1[...], o2[...], o3[...], o4[...] = _s5_param_fn(ar[...], ai[...], ld[...], br[...], bi[...])

    return pl.pallas_call(body, name="s5_params",
                          out_shape=[_sds(B_GROUPS, B_STATE)] * 2 + [_sds(B_WIDTH, B_STATE)] * 2)(a_re, a_im, log_dt, bt_re, bt_im)


def _s5_params_bwd(a_re, a_im, log_dt, bt_re, bt_im, d1, d2, d3, d4):
    def body(ar, ai, ld, br, bi, c1, c2, c3, c4, o1, o2, o3, o4, o5):
        _, vjp = jax.vjp(_s5_param_fn, ar[...], ai[...], ld[...], br[...], bi[...])
        o1[...], o2[...], o3[...], o4[...], o5[...] = vjp((c1[...], c2[...], c3[...], c4[...]))

    return pl.pallas_call(body, name="s5_params_bwd",
                          out_shape=[_sds(B_GROUPS, B_STATE)] * 2 + [_sds(B_GROUPS, 1)] + [_sds(B_WIDTH, B_STATE)] * 2,
                          )(a_re, a_im, log_dt, bt_re, bt_im, d1, d2, d3, d4)


S5_SUB = 8
S5_GROUPS = S5_T // S5_SUB


def _dscan(xr, xi, cr, ci, period, reverse):
    n = xr.shape[0]
    rows = lax.broadcasted_iota(jnp.int32, xr.shape, 0)
    pos = rows % period
    k = 1
    while k < period:
        if reverse:
            keep, shift = pos < period - k, n - k
        else:
            keep, shift = pos >= k, k
        sr = jnp.where(keep, pltpu.roll(xr, shift, 0), 0.0)
        si = jnp.where(keep, pltpu.roll(xi, shift, 0), 0.0)
        xr, xi = xr + cr * sr - ci * si, xi + cr * si + ci * sr
        cr, ci = cr * cr - ci * ci, 2.0 * cr * ci
        k *= 2
    return xr, xi, cr, ci


def _pick_row(x, r):
    rows = lax.broadcasted_iota(jnp.int32, x.shape, 0)
    return jnp.sum(jnp.where(rows == r, x, 0.0), axis=0, keepdims=True)


def _scan_tables(ar, ai, reverse):
    rows = lax.broadcasted_iota(jnp.int32, (S5_SUB, S5_LANES), 0)
    at = rows == (S5_SUB - 1 if reverse else 0)
    p8r, p8i, a8r, a8i = _dscan(jnp.where(at, ar, 0.0), jnp.where(at, ai, 0.0), ar, ai, S5_SUB, reverse)
    grp = lax.broadcasted_iota(jnp.int32, (S5_GROUPS, S5_LANES), 0)
    at = grp == (S5_GROUPS - 1 if reverse else 0)
    pgr, pgi, _, _ = _dscan(jnp.where(at, a8r, 0.0), jnp.where(at, a8i, 0.0), a8r, a8i, S5_GROUPS, reverse)
    return p8r, p8i, pgr, pgi


def _block_scan(br, bi, ar, ai, carry, tables, reverse, work):
    p8r, p8i, pgr, pgi = tables
    cin_r, cin_i = carry
    xw_r, xw_i, ew_r, ew_i = work
    xr, xi, a8r, a8i = _dscan(br, bi, ar, ai, S5_SUB, reverse)
    xw_r[...] = xr
    xw_i[...] = xi
    end = 0 if reverse else S5_SUB - 1
    for g in range(S5_GROUPS):
        ew_r[g:g + 1, :] = xw_r[S5_SUB * g + end:S5_SUB * g + end + 1, :]
        ew_i[g:g + 1, :] = xw_i[S5_SUB * g + end:S5_SUB * g + end + 1, :]
    er, ei, _, _ = _dscan(ew_r[...], ew_i[...], a8r, a8i, S5_GROUPS, reverse)
    er, ei = er + pgr * cin_r - pgi * cin_i, ei + pgr * cin_i + pgi * cin_r
    grp = lax.broadcasted_iota(jnp.int32, (S5_GROUPS, S5_LANES), 0)
    if reverse:
        ew_r[...] = jnp.where(grp == S5_GROUPS - 1, cin_r, pltpu.roll(er, S5_GROUPS - 1, 0))
        ew_i[...] = jnp.where(grp == S5_GROUPS - 1, cin_i, pltpu.roll(ei, S5_GROUPS - 1, 0))
    else:
        ew_r[...] = jnp.where(grp == 0, cin_r, pltpu.roll(er, 1, 0))
        ew_i[...] = jnp.where(grp == 0, cin_i, pltpu.roll(ei, 1, 0))
    for g in range(S5_GROUPS):
        rows = slice(S5_SUB * g, S5_SUB * (g + 1))
        nr, ni = ew_r[g:g + 1, :], ew_i[g:g + 1, :]
        xw_r[rows, :] += p8r * nr - p8i * ni
        xw_i[rows, :] += p8r * ni + p8i * nr
    last = 0 if reverse else S5_GROUPS - 1
    return xw_r[...], xw_i[...], (_pick_row(er, last), _pick_row(ei, last))


def _s5_tile_specs(n_t, rev):
    t_of = (lambda i: n_t - 1 - i) if rev else (lambda i: i)
    u_spec = pl.BlockSpec((S5_T, 128), lambda j, i: (t_of(i), j))
    x_spec = pl.BlockSpec((S5_T, S5_LANES), lambda j, i: (t_of(i), j))
    b_spec = pl.BlockSpec((1, 128, S5_LANES), lambda j, i: (j, 0, 0))
    c_spec = pl.BlockSpec((1, S5_LANES, 128), lambda j, i: (j, 0, 0))
    a_spec = pl.BlockSpec((1, S5_LANES), lambda j, i: (0, j))
    return u_spec, x_spec, b_spec, c_spec, a_spec


def _s5_scratch(with_rows):
    return ([pltpu.VMEM((1, S5_LANES), F32)] * 2 + [pltpu.VMEM((S5_SUB, S5_LANES), F32)] * 2
            + [pltpu.VMEM((S5_GROUPS, S5_LANES), F32)] * 4 + ([pltpu.VMEM((S5_T, S5_LANES), F32)] * 2 if with_rows else []))


def _s5_scan_fwd(u, btr, bti, ctr, cti, abr, abi, exchange=None):
    s_len = u.shape[0]
    n_t = s_len // S5_T

    def body(u_ref, btr_ref, bti_ref, ctr_ref, cti_ref, ar_ref, ai_ref, xr_ref, xi_ref, y_ref, car, cai, p8r, p8i, pgr, pgi, ew_r, ew_i):
        ar, ai = ar_ref[...], ai_ref[...]

        @pl.when(pl.program_id(1) == 0)
        def _():
            car[...] = jnp.zeros_like(car)
            cai[...] = jnp.zeros_like(cai)
            p8r[...], p8i[...], pgr[...], pgi[...] = _scan_tables(ar, ai, False)

        ub = u_ref[...]
        xr, xi, (ncr, nci) = _block_scan(_bdot(ub, btr_ref[0]), _bdot(ub, bti_ref[0]), ar, ai, (car[...], cai[...]),
                                         (p8r[...], p8i[...], pgr[...], pgi[...]), False, (xr_ref, xi_ref, ew_r, ew_i))
        car[...] = ncr
        cai[...] = nci
        y_ref[...] = _bdot(xr, ctr_ref[0]) - _bdot(xi, cti_ref[0])

    u_spec, x_spec, b_spec, c_spec, a_spec = _s5_tile_specs(n_t, False)
    return _call_with_exchange(
        body, "s5_scan_fwd", (S5_TILES, n_t),
        [u_spec, b_spec, b_spec, c_spec, c_spec, a_spec, a_spec], [x_spec, x_spec, u_spec],
        [_sds(s_len, S5_TILES * S5_LANES)] * 2 + [_sds(s_len, B_WIDTH)], _s5_scratch(False),
        (u, btr, bti, ctr, cti, abr, abi), exchange)


def _s5_scan_bwd(dy, xr, xi, u, btr, bti, ctr, cti, abr, abi, exchange=None):
    s_len = u.shape[0]
    n_t = s_len // S5_T

    def body(dy_ref, xr_ref, xi_ref, xrp_ref, xip_ref, u_ref, btr_ref, bti_ref, ctr_ref, cti_ref, ar_ref, ai_ref,
             du_ref, dbtr_ref, dbti_ref, dctr_ref, dcti_ref, dar_ref, dai_ref, car, cai, p8r, p8i, pgr, pgi, ew_r, ew_i, xw_r, xw_i):
        ar, ai = ar_ref[...], ai_ref[...]
        i = pl.program_id(1)
        rows = lax.broadcasted_iota(jnp.int32, (S5_T, S5_LANES), 0)

        @pl.when(i == 0)
        def _():
            for r in (car, cai, dbtr_ref, dbti_ref, dctr_ref, dcti_ref, dar_ref, dai_ref):
                r[...] = jnp.zeros_like(r)
            p8r[...], p8i[...], pgr[...], pgi[...] = _scan_tables(ar, -ai, True)

        dyb = dy_ref[...]
        xr_b, xi_b, ub = xr_ref[...], xi_ref[...], u_ref[...]
        dctr_ref[0] += _bdot_tn(xr_b, dyb)
        dcti_ref[0] -= _bdot_tn(xi_b, dyb)
        gr, gi, (ncr, nci) = _block_scan(_bdot_nt(dyb, ctr_ref[0]), -_bdot_nt(dyb, cti_ref[0]), ar, -ai, (car[...], cai[...]),
                                         (p8r[...], p8i[...], pgr[...], pgi[...]), True, (xw_r, xw_i, ew_r, ew_i))
        car[...] = ncr
        cai[...] = nci
        du_ref[...] = _bdot_nt(gr, btr_ref[0]) + _bdot_nt(gi, bti_ref[0])
        dbtr_ref[0] += _bdot_tn(ub, gr)
        dbti_ref[0] += _bdot_tn(ub, gi)
        has_prev = (i < n_t - 1).astype(F32)
        hr = _pick_row(xrp_ref[...], 7) * has_prev
        hi = _pick_row(xip_ref[...], 7) * has_prev
        xpr = jnp.where(rows == 0, hr, pltpu.roll(xr_b, 1, 0))
        xpi = jnp.where(rows == 0, hi, pltpu.roll(xi_b, 1, 0))
        dar_ref[...] += jnp.sum(gr * xpr + gi * xpi, axis=0, keepdims=True)
        dai_ref[...] += jnp.sum(gi * xpr - gr * xpi, axis=0, keepdims=True)

    u_spec, x_spec, b_spec, c_spec, a_spec = _s5_tile_specs(n_t, True)
    halo = pl.BlockSpec((8, S5_LANES), lambda j, i: (jnp.maximum((n_t - 1 - i) * (S5_T // 8) - 1, 0), j))
    return _call_with_exchange(
        body, "s5_scan_bwd", (S5_TILES, n_t),
        [u_spec, x_spec, x_spec, halo, halo, u_spec, b_spec, b_spec, c_spec, c_spec, a_spec, a_spec],
        [u_spec, b_spec, b_spec, c_spec, c_spec, a_spec, a_spec],
        [_sds(s_len, B_WIDTH)] + [_sds(S5_TILES, 128, S5_LANES)] * 2 + [_sds(S5_TILES, S5_LANES, 128)] * 2
        + [_sds(1, S5_TILES * S5_LANES)] * 2,
        _s5_scratch(True), (dy, xr, xi, xr, xi, u, btr, bti, ctr, cti, abr, abi), exchange)


S5_SEG = 8
S5_STEPS = 32
S5_WIDTH = S5_TILES * S5_LANES


def _seg_rows(block):
    return jnp.swapaxes(block, 0, 1).reshape(block.shape[1] * S5_SEG, block.shape[2])


def _seg_block(rows):
    return jnp.swapaxes(rows.reshape(rows.shape[0] // S5_SEG, S5_SEG, rows.shape[1]), 0, 1)


def _seq_specs(n_i, rev):
    at = (lambda i: n_i - 1 - i) if rev else (lambda i: i)
    seg = pl.BlockSpec((S5_SEG, S5_STEPS, B_WIDTH), lambda i: (0, at(i), 0))
    x_spec = pl.BlockSpec((S5_SEG * S5_STEPS, S5_WIDTH), lambda i: (at(i), 0))
    return seg, x_spec, _fix((S5_TILES, 128, S5_LANES)), _fix((S5_TILES, S5_LANES, 128)), _fix((1, S5_WIDTH)), _fix((S5_SEG, S5_WIDTH))


def _tile_dots(dot, lhs, w_ref, lhs_width):
    return jnp.concatenate([dot(lhs[:, t * lhs_width:(t + 1) * lhs_width], w_ref[t]) for t in range(S5_TILES)], axis=1)


def _s5_entries(name, end_r, end_i, abr, abi, steps, reverse):
    def body(er_ref, ei_ref, ar_ref, ai_ref, or_ref, oi_ref):
        pr, pi_ = ar_ref[...], ai_ref[...]
        for _ in range(int(math.log2(steps))):
            pr, pi_ = pr * pr - pi_ * pi_, 2.0 * pr * pi_
        er, ei = er_ref[...], ei_ref[...]
        rows = lax.broadcasted_iota(jnp.int32, er.shape, 0)
        cr, ci = jnp.zeros_like(pr), jnp.zeros_like(pr)
        out_r, out_i = jnp.zeros_like(er), jnp.zeros_like(er)
        for g in (range(S5_SEG - 2, -1, -1) if reverse else range(1, S5_SEG)):
            src = g + 1 if reverse else g - 1
            cr, ci = _pick_row(er, src) + pr * cr - pi_ * ci, _pick_row(ei, src) + pr * ci + pi_ * cr
            out_r, out_i = jnp.where(rows == g, cr, out_r), jnp.where(rows == g, ci, out_i)
        or_ref[...] = out_r
        oi_ref[...] = out_i

    return pl.pallas_call(body, name=name, out_shape=[_sds(*end_r.shape)] * 2)(end_r, end_i, abr, abi)


def _s5_seq_fwd(u, btr, bti, ctr, cti, abr, abi, entry, store, exchange=None):
    s_len = u.shape[0]
    seg_len = s_len // S5_SEG
    n_i = seg_len // S5_STEPS
    rows = S5_SEG * S5_STEPS

    def body(u_ref, btr_ref, bti_ref, ctr_ref, cti_ref, ar_ref, ai_ref, er_ref, ei_ref, *rest):
        if store:
            xr_ref, xi_ref, y_ref, endr_ref, endi_ref, sr_ref, si_ref = rest
        else:
            endr_ref, endi_ref, sr_ref, si_ref = rest
        i = pl.program_id(0)

        @pl.when(i == 0)
        def _():
            sr_ref[...] = er_ref[...]
            si_ref[...] = ei_ref[...]

        ar = jnp.broadcast_to(ar_ref[...], (S5_SEG, S5_WIDTH))
        ai = jnp.broadcast_to(ai_ref[...], (S5_SEG, S5_WIDTH))
        ub = _seg_rows(u_ref[...])
        br, bi = _tile_dots(_bdot, ub, btr_ref, 128), _tile_dots(_bdot, ub, bti_ref, 128)
        sr, si = sr_ref[...], si_ref[...]
        for s in range(S5_STEPS):
            at = slice(S5_SEG * s, S5_SEG * (s + 1))
            sr, si = ar * sr - ai * si + br[at], ar * si + ai * sr + bi[at]
            if store:
                xr_ref[at, :] = sr
                xi_ref[at, :] = si
        sr_ref[...] = sr
        si_ref[...] = si
        if store:
            y_ref[...] = _seg_block(_tile_dots(_bdot, xr_ref[...], ctr_ref, S5_LANES) - _tile_dots(_bdot, xi_ref[...], cti_ref, S5_LANES))

        @pl.when(i == n_i - 1)
        def _():
            endr_ref[...] = sr
            endi_ref[...] = si

    seg, x_spec, b_spec, c_spec, a_spec, e_spec = _seq_specs(n_i, False)
    ends = [_sds(S5_SEG, S5_WIDTH)] * 2
    full = [_sds(s_len, S5_WIDTH)] * 2 + [_sds(S5_SEG, seg_len, B_WIDTH)] if store else []
    return _call_with_exchange(
        body, "s5_scan_fwd" if store else "s5_ends_fwd", (n_i,),
        [seg, b_spec, b_spec, c_spec, c_spec, a_spec, a_spec, e_spec, e_spec],
        ([x_spec, x_spec, seg] if store else []) + [e_spec, e_spec], full + ends,
        [pltpu.VMEM((S5_SEG, S5_WIDTH), F32)] * 2,
        (u.reshape(S5_SEG, seg_len, B_WIDTH), btr, bti, ctr, cti, abr, abi, *entry), exchange)


def _s5_seq_bwd(dy, xr, xi, u, btr, bti, ctr, cti, abr, abi, g_entry, x_entry, full, exchange=None):
    s_len = dy.shape[0]
    seg_len = s_len // S5_SEG
    n_i = seg_len // S5_STEPS
    rows = S5_SEG * S5_STEPS

    def body(*refs):
        if full:
            (dy_ref, btr_ref, bti_ref, ctr_ref, cti_ref, ar_ref, ai_ref, ger_ref, gei_ref,
             xr_ref, xi_ref, xrp_ref, xip_ref, xer_ref, xei_ref, u_ref,
             du_ref, dbtr_ref, dbti_ref, dctr_ref, dcti_ref, dar_ref, dai_ref, str_ref, sti_ref,
             sr_ref, si_ref, gr_s, gi_s) = refs
        else:
            (dy_ref, btr_ref, bti_ref, ctr_ref, cti_ref, ar_ref, ai_ref, ger_ref, gei_ref, str_ref, sti_ref, sr_ref, si_ref) = refs
        i = pl.program_id(0)

        @pl.when(i == 0)
        def _():
            sr_ref[...] = ger_ref[...]
            si_ref[...] = gei_ref[...]
            if full:
                for r in (dbtr_ref, dbti_ref, dctr_ref, dcti_ref, dar_ref, dai_ref):
                    r[...] = jnp.zeros_like(r)

        ar = jnp.broadcast_to(ar_ref[...], (S5_SEG, S5_WIDTH))
        ai = -jnp.broadcast_to(ai_ref[...], (S5_SEG, S5_WIDTH))
        dyb = _seg_rows(dy_ref[...])
        gr, gi = _tile_dots(_bdot_nt, dyb, ctr_ref, 128), -_tile_dots(_bdot_nt, dyb, cti_ref, 128)
        sr, si = sr_ref[...], si_ref[...]
        for s in range(S5_STEPS - 1, -1, -1):
            at = slice(S5_SEG * s, S5_SEG * (s + 1))
            sr, si = ar * sr - ai * si + gr[at], ar * si + ai * sr + gi[at]
            if full:
                gr_s[at, :] = sr
                gi_s[at, :] = si
        sr_ref[...] = sr
        si_ref[...] = si

        @pl.when(i == n_i - 1)
        def _():
            str_ref[...] = sr
            sti_ref[...] = si

        if full:
            g_r, g_i = gr_s[...], gi_s[...]
            du_ref[...] = _seg_block(_tile_dots(_bdot_nt, g_r, btr_ref, S5_LANES) + _tile_dots(_bdot_nt, g_i, bti_ref, S5_LANES))
            ub = _seg_rows(u_ref[...])
            xr_b, xi_b = xr_ref[...], xi_ref[...]
            for t in range(S5_TILES):
                lanes, cols = slice(t * S5_LANES, (t + 1) * S5_LANES), slice(t * 128, (t + 1) * 128)
                dbtr_ref[t] += _bdot_tn(ub[:, cols], g_r[:, lanes])
                dbti_ref[t] += _bdot_tn(ub[:, cols], g_i[:, lanes])
                dctr_ref[t] += _bdot_tn(xr_b[:, lanes], dyb[:, cols])
                dcti_ref[t] -= _bdot_tn(xi_b[:, lanes], dyb[:, cols])
            first = i == n_i - 1
            xpr = jnp.concatenate([jnp.where(first, xer_ref[...], xrp_ref[...]), xr_b[:rows - S5_SEG]], axis=0)
            xpi = jnp.concatenate([jnp.where(first, xei_ref[...], xip_ref[...]), xi_b[:rows - S5_SEG]], axis=0)
            dar_ref[...] += jnp.sum(g_r * xpr + g_i * xpi, axis=0, keepdims=True)
            dai_ref[...] += jnp.sum(g_i * xpr - g_r * xpi, axis=0, keepdims=True)

    seg, x_spec, b_spec, c_spec, a_spec, e_spec = _seq_specs(n_i, True)
    halo = pl.BlockSpec((S5_SEG, S5_WIDTH), lambda i: (jnp.maximum((n_i - 1 - i) * S5_STEPS - 1, 0), 0))
    starts = [_sds(S5_SEG, S5_WIDTH)] * 2
    in_specs = [seg, b_spec, b_spec, c_spec, c_spec, a_spec, a_spec, e_spec, e_spec]
    args = [dy.reshape(S5_SEG, seg_len, B_WIDTH), btr, bti, ctr, cti, abr, abi, *g_entry]
    state = [pltpu.VMEM((S5_SEG, S5_WIDTH), F32)] * 2
    if not full:
        return _call_with_exchange(body, "s5_starts_bwd", (n_i,), in_specs, [e_spec, e_spec], starts, state, args, None)
    return _call_with_exchange(
        body, "s5_scan_bwd", (n_i,),
        in_specs + [x_spec, x_spec, halo, halo, e_spec, e_spec, seg],
        [seg, b_spec, b_spec, c_spec, c_spec, a_spec, a_spec, e_spec, e_spec],
        [_sds(S5_SEG, seg_len, B_WIDTH)] + [_sds(S5_TILES, 128, S5_LANES)] * 2 + [_sds(S5_TILES, S5_LANES, 128)] * 2
        + [_sds(1, S5_WIDTH)] * 2 + starts,
        state + [pltpu.VMEM((rows, S5_WIDTH), F32)] * 2,
        args + [xr, xi, xr, xi, *x_entry, u.reshape(S5_SEG, seg_len, B_WIDTH)], exchange)


def _blockdiag_b(bbar_t):
    blocks = bbar_t.reshape(S5_TILES, 8, B_GROUP, B_STATE)
    return jnp.einsum('jgmp,gh->jgmhp', blocks, jnp.eye(8, dtype=F32)).reshape(S5_TILES, 128, S5_LANES)


def _blockdiag_b_t(d):
    return jnp.einsum('jgmgp->jgmp', d.reshape(S5_TILES, 8, B_GROUP, 8, B_STATE)).reshape(B_WIDTH, B_STATE)


def _blockdiag_c(c):
    blocks = c.reshape(S5_TILES, 8, B_GROUP, B_STATE)
    return jnp.einsum('jgmp,gh->jhpgm', blocks, jnp.eye(8, dtype=F32)).reshape(S5_TILES, S5_LANES, 128)


def _blockdiag_c_t(d):
    return jnp.einsum('jgpgm->jgmp', d.reshape(S5_TILES, 8, B_STATE, 8, B_GROUP)).reshape(B_GROUPS, B_GROUP, B_STATE)


def _l0_out(os, ls, ga, gb, ypre, u, x, d_skip, glu_w, glu_b, w_out, post_g, gate):
    s_len = x.shape[0]

    def body(o0, o1, o2, l0, l1, l2, ga_ref, gb_ref, yp_ref, u_ref, x_ref, d_ref, gw_ref, gbias_ref, w_ref, pg_ref, gt_ref, x1_ref, y_ref):
        oa = _merge_gate(*[_from_res(r[...]) for r in (o0, o1, o2, l0, l1, l2)], ga_ref[...])
        yb = _s5_gelu(yp_ref[...], u_ref[...], d_ref[...])
        ob = _s5_glu(yb, _bdot(yb, gw_ref[...]) + gbias_ref[...], gb_ref[...])
        y = _bdot(oa, w_ref[0:512, :]) + _bdot(ob, w_ref[512:1024, :])
        y_ref[...] = y
        x1_ref[...] = _post_res(y, x_ref[...], pg_ref[...], gt_ref[...])

    vec, half = _fix((1, D_MODEL)), _fix((1, 512))
    return pl.pallas_call(
        body, name="l0_out", grid=(s_len // TM,),
        in_specs=[_res_spec(d) for d in DILATIONS] * 2 + [_row(TM, 512)] * 4
        + [_row(TM, D_MODEL), half, _fix((512, 512)), half, _fix((D_MODEL, D_MODEL)), vec, vec],
        out_specs=[_row(TM, D_MODEL)] * 2,
        out_shape=[_sds(s_len, D_MODEL)] * 2,
        compiler_params=_cp("arbitrary"),
    )(*os, *ls, ga, gb, ypre, u, x, d_skip, glu_w, glu_b, w_out, post_g, gate)


def _l0_out_bwd(os, ls, ga, gb, ypre, u, x, y, d_skip, glu_w, glu_b, w_out, post_g, gate, dx1, exchange=None):
    s_len = x.shape[0]

    def body(o0, o1, o2, l0, l1, l2, ga_ref, gb_ref, yp_ref, u_ref, x_ref, y_ref, d_ref, gw_ref, gbias_ref, w_ref, pg_ref, gt_ref, dx1_ref,
             do0, do1, do2, dl0, dl1, dl2, dga_ref, dgb_ref, dyp_ref, du_ref, dd_ref, dgw_ref, dgbias_ref, dw_ref, dpg_ref, dgt_ref):
        _zero_at_first([dd_ref, dgw_ref, dgbias_ref, dw_ref, dpg_ref, dgt_ref])
        _, vjp2 = jax.vjp(_post_res, y_ref[...], x_ref[...], pg_ref[...], gt_ref[...])
        dy, _, dpg, dgt = vjp2(dx1_ref[...])
        _acc(dpg_ref, dpg)
        _acc(dgt_ref, dgt)
        oa, vjp_a = jax.vjp(_merge_gate, *[_from_res(r[...]) for r in (o0, o1, o2, l0, l1, l2)], ga_ref[...])
        yb, vjp_g = jax.vjp(_s5_gelu, yp_ref[...], u_ref[...], d_ref[...])
        gl = _bdot(yb, gw_ref[...]) + gbias_ref[...]
        ob, vjp_b = jax.vjp(_s5_glu, yb, gl, gb_ref[...])
        dw_ref[0:512, :] += _bdot_tn(oa, dy)
        dw_ref[512:1024, :] += _bdot_tn(ob, dy)
        d1, d2, d3, e1, e2, e3, dga = vjp_a(_bdot_nt(dy, w_ref[0:512, :]))
        for ref, val, d in zip((do0, do1, do2, dl0, dl1, dl2), (d1, d2, d3, e1, e2, e3), DILATIONS * 2):
            ref[...] = _to_res(val, d)
        dga_ref[...] = dga
        dyb, dgl, dgb = vjp_b(_bdot_nt(dy, w_ref[512:1024, :]))
        dgb_ref[...] = dgb
        dgw_ref[...] += _bdot_tn(yb, dgl)
        _acc(dgbias_ref, jnp.sum(dgl, axis=0, keepdims=True))
        dyp, du, dd = vjp_g(dyb + _bdot_nt(dgl, gw_ref[...]))
        dyp_ref[...] = dyp
        du_ref[...] = du
        _acc(dd_ref, dd)

    vec, half = _fix((1, D_MODEL)), _fix((1, 512))
    r5, r10 = _row(TM, 512), _row(TM, D_MODEL)
    res6 = [_res_spec(d) for d in DILATIONS] * 2
    return _call_with_exchange(
        body, "l0_out_bwd", (s_len // TM,),
        res6 + [r5] * 4 + [r10, r10, half, _fix((512, 512)), half, _fix((D_MODEL, D_MODEL)), vec, vec, r10],
        res6 + [r5] * 4 + [half, _fix((512, 512)), half, _fix((D_MODEL, D_MODEL)), vec, vec],
        [_sds(*_res_shape(s_len, d)) for d in DILATIONS] * 2 + [_sds(s_len, 512)] * 4
        + [_sds(1, 512), _sds(512, 512), _sds(1, 512), _sds(D_MODEL, D_MODEL), _sds(1, D_MODEL), _sds(1, D_MODEL)],
        [], (*os, *ls, ga, gb, ypre, u, x, y, d_skip, glu_w, glu_b, w_out, post_g, gate, dx1), exchange)


def _l1_front(x, pre_g, scale, shift, w_in):
    s_len = x.shape[0]

    def body(x_ref, g_ref, sc_ref, sh_ref, w_ref, raw_ref, gate_ref, ba_ref, h_ref):
        hb = _pre_mod(x_ref[...], g_ref[...], sc_ref[...], sh_ref[...]).astype(BF)
        h_ref[...] = hb
        z = jnp.dot(hb, w_ref[...], preferred_element_type=F32)
        raw_ref[...] = z[:, 0:QKV]
        gate_ref[...] = z[:, QKV:QKV + 1024]
        ba_ref[...] = z[:, QKV + 1024:C_IN_PAD]

    vec = _fix((1, D_MODEL))
    return pl.pallas_call(
        body, name="l1_front", grid=(s_len // TM,),
        in_specs=[_row(TM, D_MODEL), vec, vec, vec, _fix((D_MODEL, C_IN_PAD))],
        out_specs=[_row(TM, QKV), _row(TM, 1024), _row(TM, 128), _row(TM, D_MODEL)],
        out_shape=[_sds(s_len, QKV), _sds(s_len, 1024), _sds(s_len, 128), _sds(s_len, D_MODEL, dtype=BF)],
        compiler_params=_cp("arbitrary"),
    )(x, pre_g, scale, shift, w_in)


def _bg_fn(ba, alog_row, dtb_row):
    lane = lax.broadcasted_iota(jnp.int32, (1, 128), 1)
    g = -jnp.exp(alog_row) * jax.nn.softplus(ba + dtb_row)
    return jnp.where(lane < C_HEADS, jax.nn.sigmoid(ba), jnp.where(lane < 2 * C_HEADS, g, 0.0))


def _act_q(c):
    q = jax.nn.silu(c)
    return q * lax.rsqrt(jnp.sum(q * q, axis=-1, keepdims=True) + EPS) * (C_DK ** -0.5)


def _act_k(c):
    k = jax.nn.silu(c)
    return k * lax.rsqrt(jnp.sum(k * k, axis=-1, keepdims=True) + EPS)


def _act_of(s):
    return _act_q if s < 8 else (_act_k if s < 16 else jax.nn.silu)


def _gdn_prep(raw, ba, conv_w, alog_row, dtb_row):
    s_len = raw.shape[0]

    def body(raw_ref, halo_ref, ba_ref, w_ref, al_ref, dt_ref, qkv_ref, bg_ref):
        bg_ref[...] = _bg_fn(ba_ref[...], al_ref[...], dt_ref[...])
        has_prev = (pl.program_id(0) > 0).astype(F32)
        for s in range(24):
            sl = slice(s * 128, (s + 1) * 128)
            cat = jnp.concatenate([halo_ref[:, sl] * has_prev, raw_ref[:, sl]], axis=0)
            conv = w_ref[3:4, sl] * cat[8:]
            for j in range(3):
                conv = conv + w_ref[j:j + 1, sl] * pltpu.roll(cat, 3 - j, 0)[8:]
            qkv_ref[:, sl] = _act_of(s)(conv)

    halo = pl.BlockSpec((8, QKV), lambda i: (jnp.maximum(i * (TM // 8) - 1, 0), 0))
    row128 = _fix((1, 128))
    return pl.pallas_call(
        body, name="gdn_prep", grid=(s_len // TM,),
        in_specs=[_row(TM, QKV), halo, _row(TM, 128), _fix((C_CONV, QKV)), row128, row128],
        out_specs=[_row(TM, QKV), _row(TM, 128)],
        out_shape=[_sds(s_len, QKV), _sds(s_len, 128)],
        compiler_params=_cp("arbitrary"),
    )(raw, raw, ba, conv_w, alog_row, dtb_row)


def _gdn_prep_bwd(raw, ba, conv_w, alog_row, dtb_row, dq, dk, dv, dbg):
    s_len = raw.shape[0]
    n_tiles = s_len // TM
    ext = TM + 8

    def body(raw_ref, prev_ref, next_ref, ba_ref, w_ref, al_ref, dt_ref, dq_ref, dqn_ref, dk_ref, dkn_ref, dv_ref, dvn_ref, dbg_ref,
             draw_ref, dba_ref, dw_ref, dal_ref, ddt_ref):
        _zero_at_first([dw_ref, dal_ref, ddt_ref])
        i = pl.program_id(0)
        _, vjp_bg = jax.vjp(_bg_fn, ba_ref[...], al_ref[...], dt_ref[...])
        dba, dal, ddt = vjp_bg(dbg_ref[...])
        dba_ref[...] = dba
        _acc(dal_ref, dal)
        _acc(ddt_ref, ddt)
        has_prev = (i > 0).astype(F32)
        has_next = (i < n_tiles - 1).astype(F32)
        ct_refs = ((dq_ref, dqn_ref), (dk_ref, dkn_ref), (dv_ref, dvn_ref))
        for s in range(24):
            sl = slice(s * 128, (s + 1) * 128)
            hl = slice((s % 8) * 128, (s % 8 + 1) * 128)
            tile_ref, nxt_ref = ct_refs[s // 8]
            cat = jnp.concatenate([prev_ref[:, sl] * has_prev, raw_ref[:, sl], next_ref[:, sl] * has_next], axis=0)
            shifted = [pltpu.roll(cat, 3 - j, 0)[8:] for j in range(3)] + [cat[8:]]
            conv = w_ref[3:4, sl] * shifted[3]
            for j in range(3):
                conv = conv + w_ref[j:j + 1, sl] * shifted[j]
            ct = jnp.concatenate([tile_ref[:, hl], nxt_ref[:, hl] * has_next], axis=0)
            _, vjp_act = jax.vjp(_act_of(s), conv)
            dconv, = vjp_act(ct)
            draw = w_ref[3:4, sl] * dconv[:TM]
            for j in range(3):
                draw = draw + w_ref[j:j + 1, sl] * pltpu.roll(dconv, ext - (3 - j), 0)[:TM]
            draw_ref[:, sl] = draw
            for j in range(4):
                dw_ref[j:j + 1, sl] += jnp.sum(dconv[:TM] * shifted[j][:TM], axis=0, keepdims=True)

    prev = pl.BlockSpec((8, QKV), lambda i: (jnp.maximum(i * (TM // 8) - 1, 0), 0))
    nxt = lambda n: pl.BlockSpec((8, n), lambda i: (jnp.minimum((i + 1) * (TM // 8), s_len // 8 - 1), 0))
    row128 = _fix((1, 128))
    ct_specs = [_row(TM, 1024), nxt(1024)] * 3
    return pl.pallas_call(
        body, name="gdn_prep_bwd", grid=(n_tiles,),
        in_specs=[_row(TM, QKV), prev, nxt(QKV), _row(TM, 128), _fix((C_CONV, QKV)), row128, row128] + ct_specs + [_row(TM, 128)],
        out_specs=[_row(TM, QKV), _row(TM, 128), _fix((C_CONV, QKV)), row128, row128],
        out_shape=[_sds(s_len, QKV), _sds(s_len, 128), _sds(C_CONV, QKV), _sds(1, 128), _sds(1, 128)],
        compiler_params=_cp("arbitrary"),
    )(raw, raw, raw, ba, conv_w, alog_row, dtb_row, dq, dq, dk, dk, dv, dv, dbg)


def _tein(eq, a, b):
    return jnp.einsum(eq, a, b, precision=lax.Precision.HIGH, preferred_element_type=F32)


def _unit_lower_inverse(lower):
    ri = lax.broadcasted_iota(jnp.int32, (C_CHUNK, C_CHUNK), 0)
    ci = lax.broadcasted_iota(jnp.int32, (C_CHUNK, C_CHUNK), 1)
    eye = (ri == ci).astype(F32)[None]
    p_mat = -lower
    inv = eye + p_mat
    for _ in range(5):
        p_mat = _bein('hij,hjk->hik', p_mat, p_mat)
        inv = inv + _bein('hij,hjk->hik', inv, p_mat)
    inv = _tein('hij,hjk->hik', inv, 2.0 * eye - _tein('hij,hjk->hik', eye + lower, inv))
    return jnp.where((ri >= ci)[None], inv, 0.0)


@jax.custom_vjp
def _known_inverse(lower, inv):
    return inv


def _known_inverse_fwd(lower, inv):
    return inv, inv


def _known_inverse_bwd(inv, d_inv):
    d_lower = -_bein('hik,hjk->hij', _bein('hji,hjk->hik', inv, d_inv), inv)
    return d_lower, jnp.zeros_like(inv)


_known_inverse.defvjp(_known_inverse_fwd, _known_inverse_bwd)


def _gdn_local(q, k, v, bgs, inv_known=None):
    lane = lax.broadcasted_iota(jnp.int32, (1, 128), 1)
    ri = lax.broadcasted_iota(jnp.int32, (C_CHUNK, C_CHUNK), 0)
    ci = lax.broadcasted_iota(jnp.int32, (C_CHUNK, C_CHUNK), 1)
    row_id = lax.broadcasted_iota(jnp.int32, (128, C_CHUNK), 0)
    beta, gc, gcj = [], [], []
    for bg in bgs:
        gc_t = _hdot((ri >= ci).astype(F32), bg)
        gc_rows = gc_t.T
        for h in range(C_HEADS):
            beta.append(jnp.sum(jnp.where(lane == h, bg, 0.0), axis=-1, keepdims=True))
            gc.append(jnp.sum(jnp.where(lane == C_HEADS + h, gc_t, 0.0), axis=-1, keepdims=True))
            gcj.append(jnp.sum(jnp.where(row_id == C_HEADS + h, gc_rows, 0.0), axis=0, keepdims=True))
    beta, gc, gcj = jnp.stack(beta, axis=0), jnp.stack(gc, axis=0), jnp.stack(gcj, axis=0)
    tril, strict = (ri >= ci)[None], (ri > ci)[None]
    decay = jnp.exp(jnp.where(tril, gc - gcj, -1e30))
    kb = k * beta
    lower = jnp.where(strict, _bein('hid,hjd->hij', kb, k) * decay, 0.0)
    inv = _unit_lower_inverse(lower) if inv_known is None else _known_inverse(lower, inv_known)
    egc = jnp.exp(gc)
    u_c = _bein('hij,hjd->hid', inv, v * beta)
    w_c = _bein('hij,hjd->hid', inv, kb * egc)
    aqk = _bein('hid,hjd->hij', q, k) * decay
    rowi = lax.broadcasted_iota(jnp.int32, (1, C_CHUNK, 1), 1)
    g_last = jnp.sum(jnp.where(rowi == C_CHUNK - 1, gc, 0.0), axis=1, keepdims=True)
    kd = k * jnp.exp(g_last - gc)
    return (u_c, w_c, aqk, q * egc, kd, jnp.exp(g_last)), inv


def _gdn_state(local, state):
    u_c, w_c, aqk, qg, kd, dec = local
    v_new = u_c - _bein('hik,hkv->hiv', w_c, state)
    o = _bein('hik,hkv->hiv', qg, state) + _bein('hij,hjv->hiv', aqk, v_new)
    return o, state * dec + _bein('hik,hiv->hkv', kd, v_new)


C_SUB = 4


def _gdn_group(q, k, v, bgs, state, inv_known=None):
    local, inv = _gdn_local(q, k, v, bgs, inv_known)
    outs = []
    for s in range(len(bgs)):
        o, state = _gdn_state(tuple(t[s * C_HEADS:(s + 1) * C_HEADS] for t in local), state)
        outs.append(o)
    return outs, state, inv


def _heads(ref):
    return jnp.stack([ref[s * C_CHUNK:(s + 1) * C_CHUNK, h * C_DK:(h + 1) * C_DK] for s in range(C_SUB) for h in range(C_HEADS)], axis=0)


def _put_heads(ref, sub, val):
    rows = slice(sub * C_CHUNK, (sub + 1) * C_CHUNK)
    for h in range(C_HEADS):
        ref[rows, h * C_DK:(h + 1) * C_DK] = val[h]


def _gdn_specs(s_len, rev):
    rows = C_SUB * C_CHUNK
    n_g = s_len // rows
    at = (lambda i: n_g - 1 - i) if rev else (lambda i: i)
    col = lambda c: pl.BlockSpec((rows, 1024), lambda i: (at(i), c))
    row128 = pl.BlockSpec((rows, 128), lambda i: (at(i), 0))
    state = pl.BlockSpec((1, C_HEADS, C_DK, C_DK), lambda i: (at(i), 0, 0, 0))
    inv = pl.BlockSpec((1, C_SUB * C_HEADS, C_CHUNK, C_CHUNK), lambda i: (at(i), 0, 0, 0))
    return n_g, col, row128, state, inv


def _gdn_fwd(qkv, bg):
    s_len = qkv.shape[0]
    n_g, col, row128, state_spec, inv_spec = _gdn_specs(s_len, False)

    def body(q_ref, k_ref, v_ref, bg_ref, o_ref, ss_ref, inv_ref, st_ref):
        _zero_at_first([st_ref])
        s0 = st_ref[...]
        ss_ref[0] = s0
        bgs = [bg_ref[s * C_CHUNK:(s + 1) * C_CHUNK, :] for s in range(C_SUB)]
        outs, s2, inv = _gdn_group(_heads(q_ref), _heads(k_ref), _heads(v_ref), bgs, s0)
        st_ref[...] = s2
        inv_ref[0] = inv
        for s in range(C_SUB):
            _put_heads(o_ref, s, outs[s])

    return pl.pallas_call(
        body, name="gdn_fwd", grid=(n_g,),
        in_specs=[col(0), col(1), col(2), row128],
        out_specs=[col(0), state_spec, inv_spec],
        out_shape=[_sds(s_len, 1024), _sds(n_g, C_HEADS, C_DK, C_DK), _sds(n_g, C_SUB * C_HEADS, C_CHUNK, C_CHUNK)],
        scratch_shapes=[pltpu.VMEM((C_HEADS, C_DK, C_DK), F32)],
        compiler_params=_cp("arbitrary"),
    )(qkv, qkv, qkv, bg)


def _gdn_bwd(qkv, bg, states, invs, do):
    s_len = qkv.shape[0]
    n_g, col, row128, state_spec, inv_spec = _gdn_specs(s_len, True)

    def body(q_ref, k_ref, v_ref, bg_ref, ss_ref, inv_ref, do_ref, dq_ref, dk_ref, dv_ref, dbg_ref, ds_ref):
        _zero_at_first([ds_ref])
        inv_known = inv_ref[0]

        def group(q, k, v, bgs, st):
            outs, st2, _ = _gdn_group(q, k, v, bgs, st, inv_known)
            return outs, st2

        bgs = [bg_ref[s * C_CHUNK:(s + 1) * C_CHUNK, :] for s in range(C_SUB)]
        _, vjp = jax.vjp(group, _heads(q_ref), _heads(k_ref), _heads(v_ref), bgs, ss_ref[0])
        douts = [jnp.stack([do_ref[s * C_CHUNK:(s + 1) * C_CHUNK, h * C_DK:(h + 1) * C_DK] for h in range(C_HEADS)], axis=0)
                 for s in range(C_SUB)]
        dq, dk, dv, dbgs, ds = vjp((douts, ds_ref[...]))
        ds_ref[...] = ds
        for s in range(C_SUB):
            dbg_ref[s * C_CHUNK:(s + 1) * C_CHUNK, :] = dbgs[s]
            for ref, val in ((dq_ref, dq), (dk_ref, dk), (dv_ref, dv)):
                _put_heads(ref, s, val[s * C_HEADS:(s + 1) * C_HEADS])

    return pl.pallas_call(
        body, name="gdn_bwd", grid=(n_g,),
        in_specs=[col(0), col(1), col(2), row128, state_spec, inv_spec, col(0)],
        out_specs=[col(0), col(0), col(0), row128],
        out_shape=[_sds(s_len, 1024)] * 3 + [_sds(s_len, 128)],
        scratch_shapes=[pltpu.VMEM((C_HEADS, C_DK, C_DK), F32)],
        compiler_params=_cp("arbitrary"),
    )(qkv, qkv, qkv, bg, states, invs, do)


def _head_norm_gate(o, gate, norm_g):
    return (_rms(o) * norm_g) * jax.nn.silu(gate)


def _l1_out_fb(o, gate_c, x1, target, norm_g, w_out, post_g, gate):
    s_len = x1.shape[0]

    def body(o_ref, gc_ref, x1_ref, t_ref, ng_ref, w_ref, pg_ref, gt_ref,
             loss_ref, dres_ref, do_ref, dgc_ref, dw_ref, dng_ref, dpg_ref, dgt_ref):
        _zero_at_first([loss_ref, dw_ref, dng_ref, dpg_ref, dgt_ref])
        ng = ng_ref[...]
        ons, vjps = [], []
        for h in range(C_HEADS):
            sl = slice(h * C_DK, (h + 1) * C_DK)
            on, vjp_h = jax.vjp(_head_norm_gate, o_ref[:, sl], gc_ref[:, sl], ng)
            ons.append(on)
            vjps.append(vjp_h)
        on_all = jnp.concatenate(ons, axis=-1)
        y = _bdot(on_all, w_ref[...])
        x2, vjp2 = jax.vjp(_post_res, y, x1_ref[...], pg_ref[...], gt_ref[...])
        err = x2 - t_ref[...]
        _acc(loss_ref, jnp.full((1, 128), 0.5 * jnp.sum(jnp.mean(err * err, axis=-1)), F32))
        dx2 = err * (1.0 / D_MODEL)
        dy, _, dpg, dgt = vjp2(dx2)
        dres_ref[...] = dx2
        _acc(dpg_ref, dpg)
        _acc(dgt_ref, dgt)
        dw_ref[...] += _bdot_tn(on_all, dy)
        don = _bdot_nt(dy, w_ref[...])
        for h in range(C_HEADS):
            sl = slice(h * C_DK, (h + 1) * C_DK)
            do_h, dgc_h, dng = vjps[h](don[:, sl])
            do_ref[:, sl] = do_h
            dgc_ref[:, sl] = dgc_h
            _acc(dng_ref, dng)

    vec, r10 = _fix((1, D_MODEL)), _row(TM, D_MODEL)
    row128 = _fix((1, 128))
    return pl.pallas_call(
        body, name="l1_out_fb", grid=(s_len // TM,),
        in_specs=[r10, r10, r10, r10, row128, _fix((D_MODEL, D_MODEL)), vec, vec],
        out_specs=[row128, r10, r10, r10, _fix((D_MODEL, D_MODEL)), row128, vec, vec],
        out_shape=[_sds(1, 128), _sds(s_len, D_MODEL), _sds(s_len, D_MODEL), _sds(s_len, D_MODEL),
                   _sds(D_MODEL, D_MODEL), _sds(1, 128), _sds(1, D_MODEL), _sds(1, D_MODEL)],
        compiler_params=_cp("arbitrary"),
    )(o, gate_c, x1, target, norm_g, w_out, post_g, gate)


def _row_of(v, width, at):
    return jnp.zeros((1, width), F32).at[0, at:at + v.shape[-1]].set(v.reshape(-1))


def _local_step(x, target, mod, wd, comm=None):
    s_len = x.shape[0]
    shift0, scale0, gate0 = (mod[0:1, i * 1024:(i + 1) * 1024] for i in range(3))
    shift1, scale1, gate1 = (mod[1:2, i * 1024:(i + 1) * 1024] for i in range(3))
    pre_g0, pre_g1 = wd["pre_g"][0:1], wd["pre_g"][1:2]
    post_g0, post_g1 = wd["post_g"][0:1], wd["post_g"][1:2]
    w_in0 = wd["ab_w_in"].astype(BF)
    d_skip, glu_b = wd["s5_d"].reshape(1, 512), wd["s5_glu_b"].reshape(1, 512)
    norm_g = wd["gdn_norm_g"].reshape(1, 128)
    alog_row = _row_of(wd["gdn_a_log"], 128, C_HEADS)
    dtb_row = _row_of(wd["gdn_dt_bias"], 128, C_HEADS)
    conv_w = wd["gdn_conv"]

    a_re, a_im = wd["s5_a_re"], wd["s5_a_im"]
    log_dt = wd["s5_log_dt"].reshape(B_GROUPS, 1)
    bt_re = wd["s5_b_re"].transpose(0, 2, 1).reshape(B_WIDTH, B_STATE)
    bt_im = wd["s5_b_im"].transpose(0, 2, 1).reshape(B_WIDTH, B_STATE)
    abar_r, abar_i, bbar_r, bbar_i = _s5_params(a_re, a_im, log_dt, bt_re, bt_im)
    abr, abi = abar_r.reshape(1, -1), abar_i.reshape(1, -1)
    btr, bti = _blockdiag_b(bbar_r).astype(BF), _blockdiag_b(bbar_i).astype(BF)
    ctr, cti = _blockdiag_c(wd["s5_c_re"]).astype(BF), _blockdiag_c(wd["s5_c_im"]).astype(BF)

    table = _bucket_table()
    biases = _attn_bias(wd["rel_bias"], jnp.asarray(table))
    front = _l0_front(x, pre_g0, scale0, shift0, w_in0)
    qs, ks, vs = front[0:3], front[3:6], front[6:9]
    u, ga, gb, h0 = front[9:]
    os, ls = zip(*[_attn_fwd(qs[i], ks[i], vs[i], biases[i]) for i in range(3)])
    seg_len = s_len // S5_SEG
    zero_state = (jnp.zeros((S5_SEG, S5_TILES * S5_LANES), F32),) * 2
    ends, _ = _s5_seq_fwd(u, btr, bti, ctr, cti, abr, abi, zero_state, False)
    x_entry = _s5_entries("s5_entries_fwd", *ends, abr, abi, seg_len, False)
    (xr, xi, ypre3, _, _), late = _s5_seq_fwd(u, btr, bti, ctr, cti, abr, abi, x_entry, True,
                                              exchange=None if comm is None else (comm.late_halves(), False))
    ypre = ypre3.reshape(s_len, B_WIDTH)
    if comm is not None:
        wd = {**wd, **comm.late_weights(late)}
    w_out0 = wd["ab_w_out"].astype(BF)
    glu_w = wd["s5_glu_w"].astype(BF)
    w_in1 = jnp.concatenate([wd["gdn_w_in"], jnp.zeros((D_MODEL, C_IN_PAD - wd["gdn_w_in"].shape[1]), wd["gdn_w_in"].dtype)], axis=1).astype(BF)
    w_out1 = wd["gdn_w_out"].astype(BF)
    x1, y0 = _l0_out(os, ls, ga, gb, ypre, u, x, d_skip, glu_w, glu_b, w_out0, post_g0, gate0)

    raw, gate_c, ba, h1 = _l1_front(x1, pre_g1, scale1, shift1, w_in1)
    qkv, bg = _gdn_prep(raw, ba, conv_w, alog_row, dtb_row)
    o_gdn, states, invs = _gdn_fwd(qkv, bg)
    loss_row, dres1, do_gdn, dgate_c, dw_out1, dnorm_g, dpost_g1, dgate1 = _l1_out_fb(
        o_gdn, gate_c, x1, target, norm_g, w_out1, post_g1, gate1)

    dq1, dk1, dv1, dbg = _gdn_bwd(qkv, bg, states, invs, do_gdn)
    draw, dba, dconv_w, dalog_row, ddtb_row = _gdn_prep_bwd(raw, ba, conv_w, alog_row, dtb_row, dq1, dk1, dv1, dbg)
    dz1, dx1, dpre_g1, dscale1, dshift1 = _front_bwd(
        "l1_front_bwd", x1, pre_g1, scale1, shift1, w_in1, dres1, [[draw], [dgate_c], [dba]], [QKV, 1024, 128])
    dw_in1 = _matmul_tn("l1_dw_in", h1, dz1, 1408)

    n_w1 = wd["gdn_w_in"].shape[1]
    ex1 = None if comm is None else (comm.chip_partials("l1", {"gdn_w_in": dw_in1[:, :n_w1], "gdn_w_out": dw_out1}), True)
    l0b, got1 = _l0_out_bwd(os, ls, ga, gb, ypre, u, x, y0, d_skip, glu_w, glu_b, w_out0, post_g0, gate0, dx1, exchange=ex1)
    dos, dls = l0b[0:3], l0b[3:6]
    dga, dgb, dypre, du_skip, dd_skip, dglu_w, dglu_b, dw_out0, dpost_g0, dgate0 = l0b[6:]
    ex2 = None if comm is None else (comm.chip_partials("l0_out", {"ab_w_out": dw_out0, "s5_glu_w": dglu_w}), True)
    starts, _ = _s5_seq_bwd(dypre, None, None, None, btr, bti, ctr, cti, abr, abi, zero_state, None, False)
    g_entry = _s5_entries("s5_entries_bwd", *starts, abr, -abi, seg_len, True)
    (du3, dbtr, dbti, dctr, dcti, dabr, dabi, _, _), got2 = _s5_seq_bwd(
        dypre, xr, xi, u, btr, bti, ctr, cti, abr, abi, g_entry, x_entry, True, exchange=ex2)
    du_scan = du3.reshape(s_len, B_WIDTH)
    if comm is not None:
        comm.received.update(zip(("gdn_w_in", "gdn_w_out", "ab_w_out", "s5_glu_w"), list(got1) + list(got2)))
    dqs, dks, dvs, dbs = [], [], [], []
    for i in range(3):
        dq_d, dk_d, dv_d, db_d = _attn_bwd(qs[i], ks[i], vs[i], biases[i], os[i], ls[i], dos[i], dls[i])
        dqs.append(dq_d)
        dks.append(dk_d)
        dvs.append(dv_d)
        dbs.append(db_d)
    parts = [dqs, dks, dvs, [du_skip, du_scan], [dga], [dgb]]
    dz0, grad_x, dpre_g0, dscale0, dshift0 = _front_bwd(
        "l0_front_bwd", x, pre_g0, scale0, shift0, w_in0, dx1, parts, [512] * 6)
    dw_in0 = _matmul_tn("l0_dw_in", h0, dz0, 768)

    idx_rows = jnp.asarray(table.reshape(3, -1), F32)
    drel = _rel_bias_grad(dbs, idx_rows).T
    da_re, da_im, dlog_dt, dbt_re, dbt_im = _s5_params_bwd(
        a_re, a_im, log_dt, bt_re, bt_im, dabr.reshape(B_GROUPS, B_STATE), dabi.reshape(B_GROUPS, B_STATE),
        _blockdiag_b_t(dbtr), _blockdiag_b_t(dbti))
    unb = lambda d: d.reshape(B_GROUPS, B_GROUP, B_STATE).transpose(0, 2, 1)
    grads = {
        "pre_g": jnp.concatenate([dpre_g0, dpre_g1], 0), "post_g": jnp.concatenate([dpost_g0, dpost_g1], 0),
        "rel_bias": drel, "ab_w_in": dw_in0, "ab_w_out": dw_out0,
        "s5_a_re": da_re, "s5_a_im": da_im, "s5_log_dt": dlog_dt.reshape(B_GROUPS),
        "s5_b_re": unb(dbt_re), "s5_b_im": unb(dbt_im),
        "s5_c_re": _blockdiag_c_t(dctr), "s5_c_im": _blockdiag_c_t(dcti),
        "s5_d": dd_skip.reshape(512), "s5_glu_w": dglu_w, "s5_glu_b": dglu_b.reshape(512),
        "gdn_w_in": dw_in1[:, :wd["gdn_w_in"].shape[1]], "gdn_conv": dconv_w,
        "gdn_a_log": dalog_row[0, C_HEADS:2 * C_HEADS], "gdn_dt_bias": ddtb_row[0, C_HEADS:2 * C_HEADS],
        "gdn_norm_g": dnorm_g.reshape(128), "gdn_w_out": dw_out1,
    }
    dmod = jnp.concatenate([jnp.concatenate([dshift0, dscale0, dgate0], 1), jnp.concatenate([dshift1, dscale1, dgate1], 1)], 0)
    return loss_row[0, 0], grad_x, grads, dmod


def _place():
    return lax.axis_index("x"), lax.axis_index("y"), lax.axis_index("c")


def _flip(v, bit):
    return 1 - v if bit else v


def _hbm_call(name, body, arrs, out_shapes, n_sem):
    any_spec = pl.BlockSpec(memory_space=pl.ANY)
    return pl.pallas_call(
        body, name=name,
        in_specs=[any_spec] * len(arrs), out_specs=[any_spec] * len(out_shapes), out_shape=out_shapes,
        scratch_shapes=[pltpu.SemaphoreType.DMA((n_sem,)), pltpu.SemaphoreType.DMA((n_sem,))],
    )(*arrs)


def _own_slot(gathered, own, slot):
    idx = lax.broadcasted_iota(jnp.int32, (gathered.shape[0],) + (1,) * own.ndim, 0)
    return jnp.where(idx == slot, own[None], gathered)


def _all_gather8(name, arr):
    def body(x_ref, out_ref, send_sems, recv_sems):
        x, y, c = _place()
        me = 4 * x + 2 * y + c
        sends, recvs = [], []
        for m in range(1, 8):
            peer = (_flip(x, m & 4), _flip(y, m & 2), _flip(c, m & 1))
            sends.append(pltpu.make_async_remote_copy(x_ref, out_ref.at[me], send_sems.at[m - 1], recv_sems.at[m - 1],
                                                      device_id=peer, device_id_type=MESH))
            recvs.append(pltpu.make_async_remote_copy(x_ref, out_ref.at[4 * peer[0] + 2 * peer[1] + peer[2]], send_sems.at[m - 1],
                                                      recv_sems.at[m - 1], device_id=peer, device_id_type=MESH))
        for cp in sends:
            cp.start()
        for cp in recvs:
            cp.wait_recv()
        for cp in sends:
            cp.wait_send()

    return _hbm_call(name, body, [arr], [jax.ShapeDtypeStruct((8,) + arr.shape, arr.dtype)], 7)[0]


def _all_to_all8(name, arr):
    def body(x_ref, out_ref, send_sems, recv_sems):
        x, y, c = _place()
        me = 4 * x + 2 * y + c
        sends, recvs = [], []
        for m in range(1, 8):
            peer = (_flip(x, m & 4), _flip(y, m & 2), _flip(c, m & 1))
            peer_id = 4 * peer[0] + 2 * peer[1] + peer[2]
            sends.append(pltpu.make_async_remote_copy(x_ref.at[peer_id], out_ref.at[me], send_sems.at[m - 1], recv_sems.at[m - 1],
                                                      device_id=peer, device_id_type=MESH))
            recvs.append(pltpu.make_async_remote_copy(x_ref.at[peer_id], out_ref.at[peer_id], send_sems.at[m - 1], recv_sems.at[m - 1],
                                                      device_id=peer, device_id_type=MESH))
        for cp in sends:
            cp.start()
        for cp in recvs:
            cp.wait_recv()
        for cp in sends:
            cp.wait_send()

    return _hbm_call(name, body, [arr], [jax.ShapeDtypeStruct(arr.shape, arr.dtype)], 7)[0]


def _chip_copies(ins, outs, send_sems, recv_sems, scatter):
    x, y, c = _place()
    mine = 2 * x + y
    sends, recvs = [], []
    for a in range(len(ins)):
        for m in range(1, 4):
            px, py = _flip(x, m & 2), _flip(y, m & 1)
            k = 3 * a + m - 1
            src = ins[a].at[2 * px + py] if scatter else ins[a]
            sends.append(pltpu.make_async_remote_copy(src, outs[a].at[mine], send_sems.at[k], recv_sems.at[k],
                                                      device_id=(px, py, c), device_id_type=MESH))
            recvs.append(pltpu.make_async_remote_copy(src, outs[a].at[2 * px + py], send_sems.at[k], recv_sems.at[k],
                                                      device_id=(px, py, c), device_id_type=MESH))
    return sends, recvs


def _chip_shapes(arrs, scatter):
    return [jax.ShapeDtypeStruct(a.shape if scatter else (4,) + a.shape, a.dtype) for a in arrs]


def _chip_exchange(name, arrs, scatter):
    n = len(arrs)

    def body(*refs):
        sends, recvs = _chip_copies(refs[:n], refs[n:2 * n], refs[2 * n], refs[2 * n + 1], scatter)
        for cp in sends:
            cp.start()
        for cp in recvs:
            cp.wait_recv()
        for cp in sends:
            cp.wait_send()

    return _hbm_call(name, body, arrs, _chip_shapes(arrs, scatter), 3 * n)


def _call_with_exchange(body, name, grid, in_specs, out_specs, out_shape, scratch_shapes, args, exchange):
    if exchange is None:
        return pl.pallas_call(body, name=name, grid=grid, in_specs=in_specs, out_specs=out_specs, out_shape=out_shape,
                              scratch_shapes=scratch_shapes, compiler_params=_cp(*["arbitrary"] * len(grid)))(*args), []
    arrs, scatter = exchange
    n_in, n_out, n_ex, n_scr = len(in_specs), len(out_specs), len(arrs), len(scratch_shapes)

    def fused(*refs):
        ins, ex_in = refs[:n_in], refs[n_in:n_in + n_ex]
        outs, ex_out = refs[n_in + n_ex:n_in + n_ex + n_out], refs[n_in + n_ex + n_out:n_in + 2 * n_ex + n_out]
        rest = refs[n_in + 2 * n_ex + n_out:]
        sends, recvs = _chip_copies(ex_in, ex_out, rest[n_scr], rest[n_scr + 1], scatter)
        first, last = pl.program_id(0) == 0, pl.program_id(0) == grid[0] - 1
        for k in range(1, len(grid)):
            first, last = first & (pl.program_id(k) == 0), last & (pl.program_id(k) == grid[k] - 1)

        @pl.when(first)
        def _():
            for cp in sends:
                cp.start()

        body(*ins, *outs, *rest[:n_scr])

        @pl.when(last)
        def _():
            for cp in recvs:
                cp.wait_recv()
            for cp in sends:
                cp.wait_send()

    any_spec = pl.BlockSpec(memory_space=pl.ANY)
    res = pl.pallas_call(
        fused, name=name, grid=grid, in_specs=list(in_specs) + [any_spec] * n_ex, out_specs=list(out_specs) + [any_spec] * n_ex,
        out_shape=list(out_shape) + _chip_shapes(arrs, scatter),
        scratch_shapes=list(scratch_shapes) + [pltpu.SemaphoreType.DMA((3 * n_ex,))] * 2,
        compiler_params=_cp(*["arbitrary"] * len(grid)))(*args, *arrs)
    return res[:n_out], res[n_out:]


def _sibling_exchange(name, arrs):
    n = len(arrs)

    def body(*refs):
        ins, outs = refs[:n], refs[n:2 * n]
        send_sems, recv_sems = refs[2 * n:]
        x, y, c = _place()
        copies = [pltpu.make_async_remote_copy(ins[a], outs[a], send_sems.at[a], recv_sems.at[a],
                                               device_id=(x, y, 1 - c), device_id_type=MESH) for a in range(n)]
        for cp in copies:
            cp.start()
        for cp in copies:
            cp.wait_recv()
        for cp in copies:
            cp.wait_send()

    return _hbm_call(name, body, arrs, [jax.ShapeDtypeStruct(a.shape, a.dtype) for a in arrs], n)


def _row_tile(rows):
    for t in (256, 128, 64, 32, 16, 8):
        if rows % t == 0:
            return t
    return rows


def _pair_sum(name, a, b, out_dtype):
    rows, cols = a.shape
    tr = _row_tile(rows)

    def body(a_ref, b_ref, o_ref):
        o_ref[...] = (a_ref[...] + b_ref[...]).astype(out_dtype)

    return pl.pallas_call(body, name=name, grid=(rows // tr,), in_specs=[_row(tr, cols)] * 2, out_specs=_row(tr, cols),
                          out_shape=_sds(rows, cols, dtype=out_dtype), compiler_params=_cp("arbitrary"))(a, b)


def _chip_sum(name, recv, partial, mine):
    n, rows, cols = recv.shape
    tr = _row_tile(rows)

    def body(mine_ref, *refs):
        own = refs[n][0].astype(F32)
        acc = None
        for s in range(n):
            term = jnp.where(mine_ref[0] == s, own, refs[s][0].astype(F32))
            acc = term if acc is None else acc + term
        refs[-1][...] = acc

    def slot_spec(s):
        return pl.BlockSpec((1, tr, cols), lambda i, m: (jnp.where(m[0] == s, (s + 1) % n, s), i, 0))

    grid_spec = pltpu.PrefetchScalarGridSpec(
        num_scalar_prefetch=1, grid=(rows // tr,),
        in_specs=[slot_spec(s) for s in range(n)] + [pl.BlockSpec((1, tr, cols), lambda i, m: (m[0], i, 0))],
        out_specs=pl.BlockSpec((tr, cols), lambda i, m: (i, 0)))
    return pl.pallas_call(body, name=name, grid_spec=grid_spec, out_shape=_sds(rows, cols),
                          compiler_params=_cp("arbitrary"))(mine, *([recv] * n), partial)


def _slot_sum(name, arr):
    n, rows, cols = arr.shape
    tr = _row_tile(rows)

    def body(*refs):
        acc = refs[0][0]
        for r in refs[1:-1]:
            acc = acc + r[0]
        refs[-1][...] = acc

    specs = [pl.BlockSpec((1, tr, cols), functools.partial(lambda s, i: (s, i, 0), s)) for s in range(n)]
    return pl.pallas_call(body, name=name, grid=(rows // tr,), in_specs=specs, out_specs=_row(tr, cols),
                          out_shape=_sds(rows, cols), compiler_params=_cp("arbitrary"))(*([arr] * n))


def _adamw(name, w, g, m, v):
    rows, cols = w.shape
    tr = _row_tile(rows)

    def body(w_ref, g_ref, m_ref, v_ref, d_ref, nm_ref, nv_ref):
        g_ = g_ref[...]
        m_ = ADAM_B1 * m_ref[...] + (1.0 - ADAM_B1) * g_
        v_ = ADAM_B2 * v_ref[...] + (1.0 - ADAM_B2) * (g_ * g_)
        m_hat = m_ / (1.0 - ADAM_B1 ** ADAM_STEP)
        v_hat = v_ / (1.0 - ADAM_B2 ** ADAM_STEP)
        d_ref[...] = -ADAM_LR * (m_hat / (jnp.sqrt(v_hat) + ADAM_EPS) + ADAM_WD * w_ref[...])
        nm_ref[...] = m_
        nv_ref[...] = v_

    spec = _row(tr, cols)
    return pl.pallas_call(body, name=name, grid=(rows // tr,), in_specs=[spec] * 4, out_specs=[spec] * 3,
                          out_shape=[_sds(rows, cols)] * 3, compiler_params=_cp("arbitrary"))(w, g, m, v)


def _adamw_halves(name, w, g_mine, g_sibling, m, v, core):
    _, rows, cols = w.shape
    half = rows // 2
    tr = _row_tile(half)
    per_half = half // tr

    def body(core_ref, w_ref, gm_ref, gs_ref, m_ref, v_ref, g_ref, d_ref, nm_ref, nv_ref):
        g_ = jnp.where(pl.program_id(0) // per_half == core_ref[0], gm_ref[...], gs_ref[...])
        m_ = ADAM_B1 * m_ref[...] + (1.0 - ADAM_B1) * g_
        v_ = ADAM_B2 * v_ref[...] + (1.0 - ADAM_B2) * (g_ * g_)
        m_hat = m_ / (1.0 - ADAM_B1 ** ADAM_STEP)
        v_hat = v_ / (1.0 - ADAM_B2 ** ADAM_STEP)
        g_ref[...] = g_
        d_ref[...] = -ADAM_LR * (m_hat / (jnp.sqrt(v_hat) + ADAM_EPS) + ADAM_WD * w_ref[...])
        nm_ref[...] = m_
        nv_ref[...] = v_

    full = pl.BlockSpec((None, tr, cols), lambda i, c: (0, i, 0))
    in_half = pl.BlockSpec((tr, cols), lambda i, c: (i % per_half, 0))
    grid_spec = pltpu.PrefetchScalarGridSpec(num_scalar_prefetch=1, grid=(rows // tr,),
                                             in_specs=[full, in_half, in_half, full, full], out_specs=[full] * 4)
    return pl.pallas_call(body, name=name, grid_spec=grid_spec, out_shape=[_sds(1, rows, cols)] * 4,
                          compiler_params=_cp("arbitrary"))(core, w, g_mine, g_sibling, m, v)


def _mod_local(c_all, ada_w):
    def body(c_ref, w_ref, o_ref):
        c_act = jax.nn.silu(c_ref[...])
        for l in range(2):
            o_ref[l] = _hdot(c_act, w_ref[l])

    return pl.pallas_call(body, name="mod_local", out_shape=_sds(2, 8, ada_w.shape[2]),
                          compiler_params=pltpu.CompilerParams(vmem_limit_bytes=VMEM_LIMIT_BYTES))(c_all, ada_w)


def _ada_w_grad(c_all, dmod_cols):
    def body(c_ref, d_ref, o_ref):
        c_act = jax.nn.silu(c_ref[...])
        for l in range(2):
            o_ref[l] = lax.dot_general(c_act, d_ref[l], (((0,), (0,)), ((), ())), precision=HI, preferred_element_type=F32)

    return pl.pallas_call(body, name="ada_w_grad", out_shape=_sds(2, D_MODEL, dmod_cols.shape[2]),
                          compiler_params=pltpu.CompilerParams(vmem_limit_bytes=VMEM_LIMIT_BYTES))(c_all, dmod_cols)


_SMALL = ("ada_b", "pre_g", "post_g", "rel_bias", "s5_a_re", "s5_a_im", "s5_log_dt", "s5_b_re", "s5_b_im", "s5_c_re", "s5_c_im",
          "s5_d", "s5_glu_b", "gdn_a_log", "gdn_dt_bias", "gdn_norm_g")
_SHARDED = ("ab_w_in", "ab_w_out", "s5_glu_w", "gdn_w_in", "gdn_w_out")
_COL_SHARDED = ("ab_w_in", "gdn_w_in")
_WEIGHTS = ("ada_w", "ada_b", "pre_g", "post_g", "rel_bias", "ab_w_in", "ab_w_out", "s5_a_re", "s5_a_im", "s5_log_dt", "s5_b_re",
            "s5_b_im", "s5_c_re", "s5_c_im", "s5_d", "s5_glu_w", "s5_glu_b", "gdn_w_in", "gdn_conv", "gdn_a_log", "gdn_dt_bias",
            "gdn_norm_g", "gdn_w_out")


def _rows128(n):
    return -(-n // 128)


def _pack(arrs, total_rows):
    pieces = []
    for a in arrs:
        flat = a.reshape(-1)
        pieces.append(jnp.pad(flat, (0, _rows128(flat.shape[0]) * 128 - flat.shape[0])).reshape(-1, 128))
    used = sum(p.shape[0] for p in pieces)
    pieces.append(jnp.zeros((total_rows - used, 128), F32))
    return jnp.concatenate(pieces, axis=0)


def _unpack(buf, shapes):
    out, at = [], 0
    for shp in shapes:
        n = int(np.prod(shp))
        out.append(buf[at:at + _rows128(n)].reshape(-1)[:n].reshape(shp))
        at += _rows128(n)
    return out


def _full_from_halves(name, g):
    if name in _COL_SHARDED:
        return g.transpose(0, 2, 1, 3).reshape(2 * g.shape[2], 4 * g.shape[3])
    return g.transpose(1, 0, 2, 3).reshape(8 * g.shape[2], g.shape[3])


def _shard_major(name, g):
    if name in _COL_SHARDED:
        return g.reshape(g.shape[0], 4, g.shape[1] // 4).transpose(1, 0, 2)
    return g.reshape(4, g.shape[0] // 4, g.shape[1])


_LATE = ("ab_w_out", "s5_glu_w", "gdn_w_in", "gdn_w_out")


class _WeightExchanges:
    def __init__(self, shards, core, chip):
        self.core, self.chip = core, chip
        self.half = {}
        for name, shard in shards.items():
            h = shard.shape[0] // 2
            self.half[name] = lax.dynamic_slice_in_dim(shard.astype(BF), core * h, h, axis=0)
        self.partial, self.received = {}, {}

    def _full(self, label, names, from_chips):
        mine = [_own_slot(g, self.half[n], self.chip) for n, g in zip(names, from_chips)]
        theirs = _sibling_exchange("gather_w_sibling_" + label, mine)
        return {n: _full_from_halves(n, jnp.where(self.core == 0, jnp.stack([a, b], 0), jnp.stack([b, a], 0)))
                for n, a, b in zip(names, mine, theirs)}

    def first_weights(self):
        return self._full("first", ["ab_w_in"], _chip_exchange("gather_w_chips", [self.half["ab_w_in"]], False))

    def late_halves(self):
        return [self.half[n] for n in _LATE]

    def late_weights(self, from_chips):
        return self._full("late", list(_LATE), from_chips)

    def chip_partials(self, label, grads):
        mine, other = [], []
        for name, g in grads.items():
            sm = _shard_major(name, g)
            h = sm.shape[1] // 2
            mine.append(lax.dynamic_slice_in_dim(sm, self.core * h, h, axis=1))
            other.append(lax.dynamic_slice_in_dim(sm, (1 - self.core) * h, h, axis=1))
        out = []
        for name, a, b in zip(grads, mine, _sibling_exchange("reduce_sibling_" + label, other)):
            flat = lambda t: t.reshape(-1, t.shape[-1])
            self.partial[name] = _pair_sum("sum_sibling_" + name, flat(a), flat(b), BF).reshape(a.shape)
            out.append(self.partial[name])
        return out


def kernel(x, c, ada_w, ada_b, pre_g, post_g, rel_bias, ab_w_in, ab_w_out, s5_a_re, s5_a_im, s5_log_dt, s5_b_re, s5_b_im, s5_c_re, s5_c_im, s5_d, s5_glu_w, s5_glu_b, gdn_w_in, gdn_conv, gdn_a_log, gdn_dt_bias, gdn_norm_g, gdn_w_out, loss_target, m_ada_w, m_ada_b, m_pre_g, m_post_g, m_rel_bias, m_ab_w_in, m_ab_w_out, m_s5_a_re, m_s5_a_im, m_s5_log_dt, m_s5_b_re, m_s5_b_im, m_s5_c_re, m_s5_c_im, m_s5_d, m_s5_glu_w, m_s5_glu_b, m_gdn_w_in, m_gdn_conv, m_gdn_a_log, m_gdn_dt_bias, m_gdn_norm_g, m_gdn_w_out, v_ada_w, v_ada_b, v_pre_g, v_post_g, v_rel_bias, v_ab_w_in, v_ab_w_out, v_s5_a_re, v_s5_a_im, v_s5_log_dt, v_s5_b_re, v_s5_b_im, v_s5_c_re, v_s5_c_im, v_s5_d, v_s5_glu_w, v_s5_glu_b, v_gdn_w_in, v_gdn_conv, v_gdn_a_log, v_gdn_dt_bias, v_gdn_norm_g, v_gdn_w_out):
    w = dict(ada_w=ada_w, ada_b=ada_b, pre_g=pre_g, post_g=post_g, rel_bias=rel_bias, ab_w_in=ab_w_in, ab_w_out=ab_w_out,
             s5_a_re=s5_a_re, s5_a_im=s5_a_im, s5_log_dt=s5_log_dt, s5_b_re=s5_b_re, s5_b_im=s5_b_im, s5_c_re=s5_c_re, s5_c_im=s5_c_im,
             s5_d=s5_d, s5_glu_w=s5_glu_w, s5_glu_b=s5_glu_b, gdn_w_in=gdn_w_in, gdn_conv=gdn_conv, gdn_a_log=gdn_a_log,
             gdn_dt_bias=gdn_dt_bias, gdn_norm_g=gdn_norm_g, gdn_w_out=gdn_w_out)
    m = dict(ada_w=m_ada_w, ada_b=m_ada_b, pre_g=m_pre_g, post_g=m_post_g, rel_bias=m_rel_bias, ab_w_in=m_ab_w_in, ab_w_out=m_ab_w_out,
             s5_a_re=m_s5_a_re, s5_a_im=m_s5_a_im, s5_log_dt=m_s5_log_dt, s5_b_re=m_s5_b_re, s5_b_im=m_s5_b_im, s5_c_re=m_s5_c_re,
             s5_c_im=m_s5_c_im, s5_d=m_s5_d, s5_glu_w=m_s5_glu_w, s5_glu_b=m_s5_glu_b, gdn_w_in=m_gdn_w_in, gdn_conv=m_gdn_conv,
             gdn_a_log=m_gdn_a_log, gdn_dt_bias=m_gdn_dt_bias, gdn_norm_g=m_gdn_norm_g, gdn_w_out=m_gdn_w_out)
    v = dict(ada_w=v_ada_w, ada_b=v_ada_b, pre_g=v_pre_g, post_g=v_post_g, rel_bias=v_rel_bias, ab_w_in=v_ab_w_in, ab_w_out=v_ab_w_out,
             s5_a_re=v_s5_a_re, s5_a_im=v_s5_a_im, s5_log_dt=v_s5_log_dt, s5_b_re=v_s5_b_re, s5_b_im=v_s5_b_im, s5_c_re=v_s5_c_re,
             s5_c_im=v_s5_c_im, s5_d=v_s5_d, s5_glu_w=v_s5_glu_w, s5_glu_b=v_s5_glu_b, gdn_w_in=v_gdn_w_in, gdn_conv=v_gdn_conv,
             gdn_a_log=v_gdn_a_log, gdn_dt_bias=v_gdn_dt_bias, gdn_norm_g=v_gdn_norm_g, gdn_w_out=v_gdn_w_out)
    ix, iy, ic = _place()
    me = 4 * ix + 2 * iy + ic
    chip = 2 * ix + iy
    n_cols = ada_w.shape[2]

    mine_first = _pack([c, gdn_conv], 32)
    first = _own_slot(_all_gather8("gather_c_conv", mine_first), mine_first, me)
    c_all = first[:, 0:8].reshape(8, D_MODEL)
    conv_full = first[0::2, 8:32].reshape(4, C_CONV, n_cols).transpose(1, 0, 2).reshape(C_CONV, 4 * n_cols)
    mine_mod = _mod_local(c_all, ada_w)
    modl = _own_slot(_all_gather8("gather_mod", mine_mod), mine_mod, me)
    mod = lax.dynamic_index_in_dim(modl[0::2], me, axis=2, keepdims=False)
    mod = mod.transpose(1, 0, 2).reshape(2, 4 * n_cols) + ada_b

    comm = _WeightExchanges({name: w[name][0] for name in _SHARDED}, ic, chip)
    wd = {name: w[name] for name in _SMALL if name != "ada_b"}
    wd = {k: (a if k in ("pre_g", "post_g", "rel_bias") else a[0]) for k, a in wd.items()}
    wd["gdn_conv"] = conv_full
    wd.update(comm.first_weights())

    loss_local, grad_x, grads, dmod = _local_step(x[0], loss_target[0], mod, wd, comm)
    loss = lax.psum(loss_local, ("x", "y", "c"))

    small_shapes = [w[name].shape for name in _SMALL] + [(C_CONV, 4 * n_cols)]
    small_rows = -(-sum(_rows128(int(np.prod(s))) for s in small_shapes) // 64) * 64
    per_dev, dmod_rows = small_rows // 8, _rows128(2 * 3 * D_MODEL)
    partial = _pack([dmod] + [grads[name] for name in _SMALL[1:]] + [grads["gdn_conv"]], small_rows)
    outbound = jnp.concatenate([partial.reshape(8, per_dev, 128), jnp.broadcast_to(partial[None, :dmod_rows], (8, dmod_rows, 128))], axis=1)
    inbound = _own_slot(_all_to_all8("reduce_small_grads", outbound), lax.dynamic_index_in_dim(outbound, me, 0, keepdims=False), me)
    my_rows = _slot_sum("sum_small_grads", inbound[:, :per_dev])
    g_small = _own_slot(_all_gather8("gather_small_grads", my_rows), my_rows, me).reshape(small_rows, 128)
    g_list = _unpack(g_small, small_shapes)
    out_g, out_d, out_m, out_v = {}, {}, {}, {}

    def update(name, g2d):
        shp = w[name].shape
        two_d = lambda a: a.reshape(-1, shp[-1])
        d_, m_, v_ = _adamw("adamw_" + name, two_d(w[name]), g2d, two_d(m[name]), two_d(v[name]))
        out_g[name], out_d[name], out_m[name], out_v[name] = (a.reshape(shp) for a in (g2d, d_, m_, v_))

    for name, g in zip(_SMALL, g_list[:-1]):
        update(name, g.reshape(-1, g.shape[-1]))
    update("gdn_conv", lax.dynamic_slice_in_dim(g_list[-1], chip * n_cols, n_cols, axis=1))

    dmod_all = inbound[:, per_dev:].reshape(8, 2, 4, n_cols)
    dmod_cols = lax.dynamic_index_in_dim(dmod_all, chip, axis=2, keepdims=False).transpose(1, 0, 2)
    update("ada_w", _ada_w_grad(c_all, dmod_cols).reshape(-1, n_cols))

    comm.received["ab_w_in"] = _chip_exchange("reduce_chips", comm.chip_partials("l0_in", {"ab_w_in": grads["ab_w_in"]}), True)[0]
    chip_1 = jnp.reshape(chip, (1,)).astype(jnp.int32)
    core_1 = jnp.reshape(ic, (1,)).astype(jnp.int32)
    reduced = [_chip_sum("sum_chips_" + name, comm.received[name], comm.partial[name], chip_1) for name in _SHARDED]
    for name, g_mine, g_sib in zip(_SHARDED, reduced, _sibling_exchange("reduce_share", reduced)):
        out_g[name], out_d[name], out_m[name], out_v[name] = _adamw_halves(
            "adamw_" + name, w[name], g_mine, g_sib, m[name], v[name], core_1)

    return (loss, grad_x[None], *[out_g[n] for n in _WEIGHTS], *[out_d[n] for n in _WEIGHTS],
            *[out_m[n] for n in _WEIGHTS], *[out_v[n] for n in _WEIGHTS])
```

```python
import functools
import math

import numpy as np
import jax
import jax.numpy as jnp
from jax import lax
from jax.experimental import pallas as pl
from jax.experimental.pallas import tpu as pltpu

F32 = jnp.float32
BF = jnp.bfloat16
HI = lax.Precision.HIGHEST
MESH = pl.DeviceIdType.MESH

D_MODEL = 1024
EPS = 1e-6
A_HEADS, A_HD, A_WIDTH, A_BLOCK = 8, 64, 512, 128
DILATIONS = (1, 4, 16)
N_KEYS = 128
REL_BUCKETS, REL_MAX_DIST = 32, 2048
B_WIDTH, B_GROUP, B_GROUPS, B_STATE = 512, 16, 32, 64
S5_LANES = 512
S5_TILES = 4
S5_T = 256
C_HEADS, C_DK, C_CHUNK, C_CONV = 8, 128, 64, 4
QKV = 3072
C_IN_PAD = 4224
TM = 256
VMEM_LIMIT_BYTES = 56 * 1024 * 1024
ADAM_LR, ADAM_B1, ADAM_B2, ADAM_EPS, ADAM_WD, ADAM_STEP = 0.001, 0.9, 0.999, 1e-08, 0.01, 10
NEG = float(np.finfo(np.float32).min)


def _cp(*sem):
    return pltpu.CompilerParams(dimension_semantics=sem, vmem_limit_bytes=VMEM_LIMIT_BYTES)


def _bdot(a, b):
    return jnp.dot(a.astype(BF), b.astype(BF), preferred_element_type=F32)


def _bdot_nt(a, b):
    return lax.dot_general(a.astype(BF), b.astype(BF), (((1,), (1,)), ((), ())), preferred_element_type=F32)


def _bdot_tn(a, b):
    return lax.dot_general(a.astype(BF), b.astype(BF), (((0,), (0,)), ((), ())), preferred_element_type=F32)


def _hdot(a, b):
    return jnp.dot(a, b, precision=HI, preferred_element_type=F32)


def _bein(eq, a, b):
    return jnp.einsum(eq, a.astype(BF), b.astype(BF), preferred_element_type=F32)


def _hein(eq, a, b):
    return jnp.einsum(eq, a, b, precision=HI, preferred_element_type=F32)


def _row(tm, n):
    return pl.BlockSpec((tm, n), lambda i: (i, 0))


def _fix(shape):
    return pl.BlockSpec(shape, lambda i: (0,) * len(shape))


def _sds(*shape, dtype=F32):
    return jax.ShapeDtypeStruct(shape, dtype)


def _acc(ref, val):
    ref[...] += val


def _zero_at_first(refs, axis=0):
    @pl.when(pl.program_id(axis) == 0)
    def _():
        for r in refs:
            r[...] = jnp.zeros_like(r)


def _rms(x):
    return x * lax.rsqrt(jnp.mean(x * x, axis=-1, keepdims=True) + EPS)


def _pre_mod(x, g, scale, shift):
    return (_rms(x) * g) * (1.0 + scale) + shift


def _post_res(y, x, post_g, gate):
    return x + gate * (_rms(y) * post_g)


def _merge_gate(o1, o2, o3, l1, l2, l3, ga):
    m = jnp.maximum(jnp.maximum(l1, l2), l3)
    e1, e2, e3 = jnp.exp(l1 - m), jnp.exp(l2 - m), jnp.exp(l3 - m)
    inv = 1.0 / (e1 + e2 + e3)
    return ((e1 * inv) * o1 + (e2 * inv) * o2 + (e3 * inv) * o3) * jax.nn.silu(ga)


def _s5_gelu(ypre, u, d_skip):
    return jax.nn.gelu(ypre + d_skip * u)


def _s5_glu(yb, gl, gb):
    return yb * jax.nn.sigmoid(gl) * jax.nn.silu(gb)


def _l0_front(x, pre_g, scale, shift, w_in):
    s_len = x.shape[0]

    def body(x_ref, g_ref, sc_ref, sh_ref, w_ref, *out_refs):
        qkv_refs, (u_ref, ga_ref, gb_ref, h_ref) = out_refs[:9], out_refs[9:]
        hb = _pre_mod(x_ref[...], g_ref[...], sc_ref[...], sh_ref[...]).astype(BF)
        h_ref[...] = hb
        z = jnp.dot(hb, w_ref[...], preferred_element_type=F32)
        for a in range(3):
            piece = z[:, a * 512:(a + 1) * 512]
            for i, d in enumerate(DILATIONS):
                qkv_refs[3 * a + i][...] = _to_res(piece, d).astype(BF)
        u_ref[...] = z[:, 1536:2048]
        ga_ref[...] = z[:, 2048:2560]
        gb_ref[...] = z[:, 2560:3072]

    vec = _fix((1, D_MODEL))
    return pl.pallas_call(
        body, name="l0_front", grid=(s_len // TM,),
        in_specs=[_row(TM, D_MODEL), vec, vec, vec, _fix((D_MODEL, 3072))],
        out_specs=[_res_spec(d) for d in DILATIONS] * 3 + [_row(TM, 512)] * 3 + [_row(TM, D_MODEL)],
        out_shape=[_sds(*_res_shape(s_len, d), dtype=BF) for d in DILATIONS] * 3 + [_sds(s_len, 512)] * 3 + [_sds(s_len, D_MODEL, dtype=BF)],
        compiler_params=_cp("arbitrary"),
    )(x, pre_g, scale, shift, w_in)


def _front_bwd(name, x, pre_g, scale, shift, w_in, dres, parts, widths):
    s_len = x.shape[0]
    n_in = sum(len(p) for p in parts)
    n_cols = sum(widths)

    def body(*refs):
        x_ref, g_ref, sc_ref, sh_ref, w_ref, dres_ref = refs[:6]
        part_refs = refs[6:6 + n_in]
        dz_ref, dx_ref, dg_ref, dsc_ref, dsh_ref = refs[6 + n_in:]
        _zero_at_first([dg_ref, dsc_ref, dsh_ref])
        _, vjp = jax.vjp(_pre_mod, x_ref[...], g_ref[...], sc_ref[...], sh_ref[...])
        dh = jnp.zeros((TM, D_MODEL), F32)
        col, at = 0, 0
        for grp, width in zip(parts, widths):
            tile = lambda r: _from_res(r[...]) if len(r.shape) == 3 else r[...]
            dz = tile(part_refs[at])
            for r in part_refs[at + 1:at + len(grp)]:
                dz = dz + tile(r)
            at += len(grp)
            dzb = dz.astype(BF)
            dz_ref[:, col:col + width] = dzb
            dh = dh + lax.dot_general(dzb, w_ref[:, col:col + width], (((1,), (1,)), ((), ())), preferred_element_type=F32)
            col += width
        dx, dg, dsc, dsh = vjp(dh)
        dx_ref[...] = dx + dres_ref[...]
        _acc(dg_ref, dg)
        _acc(dsc_ref, dsc)
        _acc(dsh_ref, dsh)

    vec = _fix((1, D_MODEL))
    flat = [a for p in parts for a in p]
    return pl.pallas_call(
        body, name=name, grid=(s_len // TM,),
        in_specs=[_row(TM, D_MODEL), vec, vec, vec, _fix((D_MODEL, n_cols)), _row(TM, D_MODEL)]
        + [_res_spec(a.shape[0], a.shape[2]) if a.ndim == 3 else _row(TM, a.shape[1]) for a in flat],
        out_specs=[_row(TM, n_cols), _row(TM, D_MODEL), vec, vec, vec],
        out_shape=[_sds(s_len, n_cols, dtype=BF), _sds(s_len, D_MODEL), _sds(1, D_MODEL), _sds(1, D_MODEL), _sds(1, D_MODEL)],
        compiler_params=_cp("arbitrary"),
    )(x, pre_g, scale, shift, w_in, dres, *flat)


def _matmul_tn(name, a, b, tn):
    s_len, k_dim = a.shape
    n_dim = b.shape[1]
    ts = 512

    def body(a_ref, b_ref, o_ref):
        _zero_at_first([o_ref], axis=1)
        o_ref[...] += lax.dot_general(a_ref[...], b_ref[...], (((0,), (0,)), ((), ())), preferred_element_type=F32)

    return pl.pallas_call(
        body, name=name, grid=(n_dim // tn, s_len // ts),
        in_specs=[pl.BlockSpec((ts, k_dim), lambda j, i: (i, 0)), pl.BlockSpec((ts, tn), lambda j, i: (i, j))],
        out_specs=pl.BlockSpec((k_dim, tn), lambda j, i: (0, j)),
        out_shape=_sds(k_dim, n_dim),
        compiler_params=_cp("arbitrary", "arbitrary"),
    )(a, b)


def _t5_bucket_np(dist):
    dist = np.maximum(dist, 0)
    max_exact = REL_BUCKETS // 2
    large = max_exact + (np.log(np.maximum(dist, 1) / max_exact)
                         / math.log(REL_MAX_DIST / max_exact) * (REL_BUCKETS - max_exact)).astype(np.int32)
    large = np.minimum(large, REL_BUCKETS - 1)
    return np.where(dist < max_exact, dist, large).astype(np.int32)


def _to_res(z, dil):
    if dil == 1:
        return z[None]
    return jnp.swapaxes(z.reshape(z.shape[0] // dil, dil, z.shape[1]), 0, 1)


def _from_res(z):
    if z.shape[0] == 1:
        return z[0]
    return jnp.swapaxes(z, 0, 1).reshape(z.shape[0] * z.shape[1], z.shape[2])


def _res_shape(s_len, dil, width=A_WIDTH):
    return (dil, s_len // dil, width)


def _res_spec(dil, width=A_WIDTH):
    return pl.BlockSpec((dil, TM // dil, width), lambda i: (0, i, 0))


def _bucket_table():
    qi = np.arange(A_BLOCK)[:, None]
    kj = np.arange(2 * A_BLOCK)[None, :]
    return np.stack([_t5_bucket_np((qi + A_BLOCK - kj) * d) for d in DILATIONS], 0)


def _attn_mask(first):
    qi = lax.broadcasted_iota(jnp.int32, (A_BLOCK, 2 * A_BLOCK), 0)
    kj = lax.broadcasted_iota(jnp.int32, (A_BLOCK, 2 * A_BLOCK), 1)
    rel = qi + A_BLOCK - kj
    return (rel >= 0) & (rel <= N_KEYS) & (jnp.logical_not(first) | (kj >= A_BLOCK))


def _attn_specs(nb, rev):
    n_of = (lambda i: nb - 1 - i) if rev else (lambda i: i)
    cur = pl.BlockSpec((None, A_BLOCK, A_WIDTH), lambda r, i: (r, n_of(i), 0))
    prev = pl.BlockSpec((None, A_BLOCK, A_WIDTH), lambda r, i: (r, jnp.maximum(n_of(i) - 1, 0), 0))
    bias = pl.BlockSpec((A_HEADS, A_BLOCK, 2 * A_BLOCK), lambda r, i: (0, 0, 0))
    return cur, prev, bias


def _attn_fwd(q, k, v, bias, exchange=None):
    dil, t_len, _ = q.shape
    nb = t_len // A_BLOCK
    scale = A_HD ** -0.5

    def body(q_ref, kp_ref, kc_ref, vp_ref, vc_ref, b_ref, o_ref, l_ref):
        mask = _attn_mask(pl.program_id(1) == 0)
        lane = lax.broadcasted_iota(jnp.int32, (1, 128), 1)
        for hp in range(A_HEADS // 2):
            sl = slice(hp * 128, (hp + 1) * 128)
            qp = q_ref[:, sl]
            kw = jnp.concatenate([kp_ref[:, sl], kc_ref[:, sl]], axis=0).astype(BF)
            vw = jnp.concatenate([vp_ref[:, sl], vc_ref[:, sl]], axis=0).astype(BF)
            outs, lses = [], []
            for j in range(2):
                hm = (lane < 64) if j == 0 else (lane >= 64)
                s = _bdot_nt(jnp.where(hm, qp, 0.0), kw) * scale
                s = jnp.where(mask, s + b_ref[2 * hp + j], NEG)
                m = jnp.max(s, axis=-1, keepdims=True)
                p = jnp.exp(s - m)
                den = jnp.sum(p, axis=-1, keepdims=True)
                outs.append(_bdot(p, vw) / den)
                lses.append(m + jnp.log(den))
            hm0 = lane < 64
            o_ref[:, sl] = jnp.where(hm0, outs[0], outs[1])
            l_ref[:, sl] = jnp.where(hm0, lses[0], lses[1])

    cur, prev, bias_spec = _attn_specs(nb, False)
    return _call_with_exchange(body, f"attn_fwd_d{dil}", (dil, nb), [cur, prev, cur, prev, cur, bias_spec], [cur, cur],
                               [_sds(dil, t_len, A_WIDTH)] * 2, [], (q, k, k, v, v, bias), exchange)


def _attn_bwd(q, k, v, bias, o, l, do, dl):
    dil, t_len, _ = q.shape
    nb = t_len // A_BLOCK
    scale = A_HD ** -0.5

    def body(q_ref, kp_ref, kc_ref, vp_ref, vc_ref, b_ref, o_ref, l_ref, do_ref, dl_ref,
             dq_ref, dk_ref, dv_ref, db_ref, ck_ref, cv_ref):
        _zero_at_first([ck_ref, cv_ref], axis=1)

        @pl.when((pl.program_id(0) == 0) & (pl.program_id(1) == 0))
        def _():
            db_ref[...] = jnp.zeros_like(db_ref)

        mask = _attn_mask(pl.program_id(1) == nb - 1)
        lane = lax.broadcasted_iota(jnp.int32, (1, 128), 1)
        for hp in range(A_HEADS // 2):
            sl = slice(hp * 128, (hp + 1) * 128)
            qp = q_ref[:, sl]
            kw = jnp.concatenate([kp_ref[:, sl], kc_ref[:, sl]], axis=0).astype(BF)
            vw = jnp.concatenate([vp_ref[:, sl], vc_ref[:, sl]], axis=0).astype(BF)
            op, lp, dop, dlp = o_ref[:, sl], l_ref[:, sl], do_ref[:, sl], dl_ref[:, sl]
            dq_acc = jnp.zeros((A_BLOCK, 128), F32)
            dk_acc = jnp.zeros((2 * A_BLOCK, 128), F32)
            dv_acc = jnp.zeros((2 * A_BLOCK, 128), F32)
            for j in range(2):
                hm = (lane < 64) if j == 0 else (lane >= 64)
                qm = jnp.where(hm, qp, 0.0)
                s = _bdot_nt(qm, kw) * scale
                s = jnp.where(mask, s + b_ref[2 * hp + j], NEG)
                lse = jnp.max(jnp.where(hm, lp, NEG), axis=-1, keepdims=True)
                p = jnp.exp(s - lse)
                do_h = jnp.where(hm, dop, 0.0)
                dd = jnp.sum(do_h * op, axis=-1, keepdims=True)
                dlse = jnp.sum(jnp.where(hm, dlp, 0.0), axis=-1, keepdims=True)
                ds = p * (_bdot_nt(do_h, vw) - dd + dlse)
                dv_acc = dv_acc + _bdot_tn(p, do_h)
                dq_acc = dq_acc + jnp.where(hm, _bdot(ds, kw), 0.0) * scale
                dk_acc = dk_acc + _bdot_tn(ds, qm) * scale
                db_ref[2 * hp + j] += ds
            dq_ref[:, sl] = dq_acc
            dk_ref[:, sl] = dk_acc[A_BLOCK:] + ck_ref[:, sl]
            dv_ref[:, sl] = dv_acc[A_BLOCK:] + cv_ref[:, sl]
            ck_ref[:, sl] = dk_acc[:A_BLOCK]
            cv_ref[:, sl] = dv_acc[:A_BLOCK]

    cur, prev, bias_spec = _attn_specs(nb, True)
    return pl.pallas_call(
        body, name=f"attn_bwd_d{dil}", grid=(dil, nb),
        in_specs=[cur, prev, cur, prev, cur, bias_spec, cur, cur, cur, cur],
        out_specs=[cur, cur, cur, bias_spec],
        out_shape=[_sds(dil, t_len, A_WIDTH)] * 3 + [_sds(A_HEADS, A_BLOCK, 2 * A_BLOCK)],
        scratch_shapes=[pltpu.VMEM((A_BLOCK, A_WIDTH), F32)] * 2,
        compiler_params=_cp("arbitrary", "arbitrary"),
    )(q, k, k, v, v, bias, o, l, do, dl)


def _attn_bias(rel_bias, table):
    def body(rb_ref, t_ref, *o_refs):
        for c in range(3):
            t = t_ref[c]
            acc = [jnp.zeros((A_BLOCK, 2 * A_BLOCK), F32) for _ in range(A_HEADS)]
            for b in range(REL_BUCKETS):
                hit = t == b
                acc = [jnp.where(hit, rb_ref[b, h], acc[h]) for h in range(A_HEADS)]
            for h in range(A_HEADS):
                o_refs[c][h] = acc[h]

    return pl.pallas_call(body, name="attn_bias", out_shape=[_sds(A_HEADS, A_BLOCK, 2 * A_BLOCK)] * 3,
                          in_specs=[pl.BlockSpec(memory_space=pltpu.SMEM), pl.BlockSpec(memory_space=pltpu.VMEM)],
                          compiler_params=pltpu.CompilerParams(vmem_limit_bytes=VMEM_LIMIT_BYTES))(rel_bias, table)


def _rel_bias_grad(dbs, idx_rows):
    n = A_BLOCK * 2 * A_BLOCK

    def body(d0_ref, d1_ref, d2_ref, idx_ref, o_ref):
        bucket = lax.broadcasted_iota(jnp.int32, (REL_BUCKETS, n), 0).astype(F32)
        acc = jnp.zeros((A_HEADS, REL_BUCKETS), F32)
        for c, db_ref in enumerate((d0_ref, d1_ref, d2_ref)):
            onehot = (idx_ref[c:c + 1, :] == bucket).astype(F32)
            acc = acc + lax.dot_general(db_ref[...], onehot, (((1,), (1,)), ((), ())), precision=HI, preferred_element_type=F32)
        o_ref[...] = acc

    return pl.pallas_call(body, name="rel_bias_grad", out_shape=_sds(A_HEADS, REL_BUCKETS),
                          compiler_params=pltpu.CompilerParams(vmem_limit_bytes=VMEM_LIMIT_BYTES))(
                              *[d.reshape(A_HEADS, n) for d in dbs], idx_rows)


def _s5_param_fn(a_re, a_im, log_dt, bt_re, bt_im):
    dt = jnp.exp(log_dt)
    mag = jnp.exp(dt * a_re)
    abar_r, abar_i = mag * jnp.cos(dt * a_im), mag * jnp.sin(dt * a_im)
    den = a_re * a_re + a_im * a_im
    fr = ((abar_r - 1.0) * a_re + abar_i * a_im) / den
    fi = (abar_i * a_re - (abar_r - 1.0) * a_im) / den
    row = lax.broadcasted_iota(jnp.int32, (B_WIDTH, B_GROUPS), 0)
    grp = lax.broadcasted_iota(jnp.int32, (B_WIDTH, B_GROUPS), 1)
    expand = ((row // B_GROUP) == grp).astype(F32)
    fr_e, fi_e = _hdot(expand, fr), _hdot(expand, fi)
    return abar_r, abar_i, fr_e * bt_re - fi_e * bt_im, fr_e * bt_im + fi_e * bt_re


def _s5_params(a_re, a_im, log_dt, bt_re, bt_im):
    def body(ar, ai, ld, br, bi, o1, o2, o3, o4):
        o1[...], o2[...], o3[...], o4[...] = _s5_param_fn(ar[...], ai[...], ld[...], br[...], bi[...])

    return pl.pallas_call(body, name="s5_params",
                          out_shape=[_sds(B_GROUPS, B_STATE)] * 2 + [_sds(B_WIDTH, B_STATE)] * 2)(a_re, a_im, log_dt, bt_re, bt_im)


def _s5_params_bwd(a_re, a_im, log_dt, bt_re, bt_im, d1, d2, d3, d4):
    def body(ar, ai, ld, br, bi, c1, c2, c3, c4, o1, o2, o3, o4, o5):
        _, vjp = jax.vjp(_s5_param_fn, ar[...], ai[...], ld[...], br[...], bi[...])
        o1[...], o2[...], o3[...], o4[...], o5[...] = vjp((c1[...], c2[...], c3[...], c4[...]))

    return pl.pallas_call(body, name="s5_params_bwd",
                          out_shape=[_sds(B_GROUPS, B_STATE)] * 2 + [_sds(B_GROUPS, 1)] + [_sds(B_WIDTH, B_STATE)] * 2,
                          )(a_re, a_im, log_dt, bt_re, bt_im, d1, d2, d3, d4)


S5_SUB = 8
S5_GROUPS = S5_T // S5_SUB


def _dscan(xr, xi, cr, ci, period, reverse):
    n = xr.shape[0]
    rows = lax.broadcasted_iota(jnp.int32, xr.shape, 0)
    pos = rows % period
    k = 1
    while k < period:
        if reverse:
            keep, shift = pos < period - k, n - k
        else:
            keep, shift = pos >= k, k
        sr = jnp.where(keep, pltpu.roll(xr, shift, 0), 0.0)
        si = jnp.where(keep, pltpu.roll(xi, shift, 0), 0.0)
        xr, xi = xr + cr * sr - ci * si, xi + cr * si + ci * sr
        cr, ci = cr * cr - ci * ci, 2.0 * cr * ci
        k *= 2
    return xr, xi, cr, ci


def _pick_row(x, r):
    rows = lax.broadcasted_iota(jnp.int32, x.shape, 0)
    return jnp.sum(jnp.where(rows == r, x, 0.0), axis=0, keepdims=True)


def _scan_tables(ar, ai, reverse):
    rows = lax.broadcasted_iota(jnp.int32, (S5_SUB, S5_LANES), 0)
    at = rows == (S5_SUB - 1 if reverse else 0)
    p8r, p8i, a8r, a8i = _dscan(jnp.where(at, ar, 0.0), jnp.where(at, ai, 0.0), ar, ai, S5_SUB, reverse)
    grp = lax.broadcasted_iota(jnp.int32, (S5_GROUPS, S5_LANES), 0)
    at = grp == (S5_GROUPS - 1 if reverse else 0)
    pgr, pgi, _, _ = _dscan(jnp.where(at, a8r, 0.0), jnp.where(at, a8i, 0.0), a8r, a8i, S5_GROUPS, reverse)
    return p8r, p8i, pgr, pgi


def _block_scan(br, bi, ar, ai, carry, tables, reverse, work):
    p8r, p8i, pgr, pgi = tables
    cin_r, cin_i = carry
    xw_r, xw_i, ew_r, ew_i = work
    xr, xi, a8r, a8i = _dscan(br, bi, ar, ai, S5_SUB, reverse)
    xw_r[...] = xr
    xw_i[...] = xi
    end = 0 if reverse else S5_SUB - 1
    for g in range(S5_GROUPS):
        ew_r[g:g + 1, :] = xw_r[S5_SUB * g + end:S5_SUB * g + end + 1, :]
        ew_i[g:g + 1, :] = xw_i[S5_SUB * g + end:S5_SUB * g + end + 1, :]
    er, ei, _, _ = _dscan(ew_r[...], ew_i[...], a8r, a8i, S5_GROUPS, reverse)
    er, ei = er + pgr * cin_r - pgi * cin_i, ei + pgr * cin_i + pgi * cin_r
    grp = lax.broadcasted_iota(jnp.int32, (S5_GROUPS, S5_LANES), 0)
    if reverse:
        ew_r[...] = jnp.where(grp == S5_GROUPS - 1, cin_r, pltpu.roll(er, S5_GROUPS - 1, 0))
        ew_i[...] = jnp.where(grp == S5_GROUPS - 1, cin_i, pltpu.roll(ei, S5_GROUPS - 1, 0))
    else:
        ew_r[...] = jnp.where(grp == 0, cin_r, pltpu.roll(er, 1, 0))
        ew_i[...] = jnp.where(grp == 0, cin_i, pltpu.roll(ei, 1, 0))
    for g in range(S5_GROUPS):
        rows = slice(S5_SUB * g, S5_SUB * (g + 1))
        nr, ni = ew_r[g:g + 1, :], ew_i[g:g + 1, :]
        xw_r[rows, :] += p8r * nr - p8i * ni
        xw_i[rows, :] += p8r * ni + p8i * nr
    last = 0 if reverse else S5_GROUPS - 1
    return xw_r[...], xw_i[...], (_pick_row(er, last), _pick_row(ei, last))


def _s5_tile_specs(n_t, rev):
    t_of = (lambda i: n_t - 1 - i) if rev else (lambda i: i)
    u_spec = pl.BlockSpec((S5_T, 128), lambda j, i: (t_of(i), j))
    x_spec = pl.BlockSpec((S5_T, S5_LANES), lambda j, i: (t_of(i), j))
    b_spec = pl.BlockSpec((1, 128, S5_LANES), lambda j, i: (j, 0, 0))
    c_spec = pl.BlockSpec((1, S5_LANES, 128), lambda j, i: (j, 0, 0))
    a_spec = pl.BlockSpec((1, S5_LANES), lambda j, i: (0, j))
    return u_spec, x_spec, b_spec, c_spec, a_spec


def _s5_scratch(with_rows):
    return ([pltpu.VMEM((1, S5_LANES), F32)] * 2 + [pltpu.VMEM((S5_SUB, S5_LANES), F32)] * 2
            + [pltpu.VMEM((S5_GROUPS, S5_LANES), F32)] * 4 + ([pltpu.VMEM((S5_T, S5_LANES), F32)] * 2 if with_rows else []))


def _s5_scan_fwd(u, btr, bti, ctr, cti, abr, abi, exchange=None):
    s_len = u.shape[0]
    n_t = s_len // S5_T

    def body(u_ref, btr_ref, bti_ref, ctr_ref, cti_ref, ar_ref, ai_ref, xr_ref, xi_ref, y_ref, car, cai, p8r, p8i, pgr, pgi, ew_r, ew_i):
        ar, ai = ar_ref[...], ai_ref[...]

        @pl.when(pl.program_id(1) == 0)
        def _():
            car[...] = jnp.zeros_like(car)
            cai[...] = jnp.zeros_like(cai)
            p8r[...], p8i[...], pgr[...], pgi[...] = _scan_tables(ar, ai, False)

        ub = u_ref[...]
        xr, xi, (ncr, nci) = _block_scan(_bdot(ub, btr_ref[0]), _bdot(ub, bti_ref[0]), ar, ai, (car[...], cai[...]),
                                         (p8r[...], p8i[...], pgr[...], pgi[...]), False, (xr_ref, xi_ref, ew_r, ew_i))
        car[...] = ncr
        cai[...] = nci
        y_ref[...] = _bdot(xr, ctr_ref[0]) - _bdot(xi, cti_ref[0])

    u_spec, x_spec, b_spec, c_spec, a_spec = _s5_tile_specs(n_t, False)
    return _call_with_exchange(
        body, "s5_scan_fwd", (S5_TILES, n_t),
        [u_spec, b_spec, b_spec, c_spec, c_spec, a_spec, a_spec], [x_spec, x_spec, u_spec],
        [_sds(s_len, S5_TILES * S5_LANES)] * 2 + [_sds(s_len, B_WIDTH)], _s5_scratch(False),
        (u, btr, bti, ctr, cti, abr, abi), exchange)


def _s5_scan_bwd(dy, xr, xi, u, btr, bti, ctr, cti, abr, abi, exchange=None):
    s_len = u.shape[0]
    n_t = s_len // S5_T

    def body(dy_ref, xr_ref, xi_ref, xrp_ref, xip_ref, u_ref, btr_ref, bti_ref, ctr_ref, cti_ref, ar_ref, ai_ref,
             du_ref, dbtr_ref, dbti_ref, dctr_ref, dcti_ref, dar_ref, dai_ref, car, cai, p8r, p8i, pgr, pgi, ew_r, ew_i, xw_r, xw_i):
        ar, ai = ar_ref[...], ai_ref[...]
        i = pl.program_id(1)
        rows = lax.broadcasted_iota(jnp.int32, (S5_T, S5_LANES), 0)

        @pl.when(i == 0)
        def _():
            for r in (car, cai, dbtr_ref, dbti_ref, dctr_ref, dcti_ref, dar_ref, dai_ref):
                r[...] = jnp.zeros_like(r)
            p8r[...], p8i[...], pgr[...], pgi[...] = _scan_tables(ar, -ai, True)

        dyb = dy_ref[...]
        xr_b, xi_b, ub = xr_ref[...], xi_ref[...], u_ref[...]
        dctr_ref[0] += _bdot_tn(xr_b, dyb)
        dcti_ref[0] -= _bdot_tn(xi_b, dyb)
        gr, gi, (ncr, nci) = _block_scan(_bdot_nt(dyb, ctr_ref[0]), -_bdot_nt(dyb, cti_ref[0]), ar, -ai, (car[...], cai[...]),
                                         (p8r[...], p8i[...], pgr[...], pgi[...]), True, (xw_r, xw_i, ew_r, ew_i))
        car[...] = ncr
        cai[...] = nci
        du_ref[...] = _bdot_nt(gr, btr_ref[0]) + _bdot_nt(gi, bti_ref[0])
        dbtr_ref[0] += _bdot_tn(ub, gr)
        dbti_ref[0] += _bdot_tn(ub, gi)
        has_prev = (i < n_t - 1).astype(F32)
        hr = _pick_row(xrp_ref[...], 7) * has_prev
        hi = _pick_row(xip_ref[...], 7) * has_prev
        xpr = jnp.where(rows == 0, hr, pltpu.roll(xr_b, 1, 0))
        xpi = jnp.where(rows == 0, hi, pltpu.roll(xi_b, 1, 0))
        dar_ref[...] += jnp.sum(gr * xpr + gi * xpi, axis=0, keepdims=True)
        dai_ref[...] += jnp.sum(gi * xpr - gr * xpi, axis=0, keepdims=True)

    u_spec, x_spec, b_spec, c_spec, a_spec = _s5_tile_specs(n_t, True)
    halo = pl.BlockSpec((8, S5_LANES), lambda j, i: (jnp.maximum((n_t - 1 - i) * (S5_T // 8) - 1, 0), j))
    return _call_with_exchange(
        body, "s5_scan_bwd", (S5_TILES, n_t),
        [u_spec, x_spec, x_spec, halo, halo, u_spec, b_spec, b_spec, c_spec, c_spec, a_spec, a_spec],
        [u_spec, b_spec, b_spec, c_spec, c_spec, a_spec, a_spec],
        [_sds(s_len, B_WIDTH)] + [_sds(S5_TILES, 128, S5_LANES)] * 2 + [_sds(S5_TILES, S5_LANES, 128)] * 2
        + [_sds(1, S5_TILES * S5_LANES)] * 2,
        _s5_scratch(True), (dy, xr, xi, xr, xi, u, btr, bti, ctr, cti, abr, abi), exchange)


S5_SEG = 8
S5_STEPS = 32
S5_WIDTH = S5_TILES * S5_LANES


def _seg_rows(block):
    return jnp.swapaxes(block, 0, 1).reshape(block.shape[1] * S5_SEG, block.shape[2])


def _seg_block(rows):
    return jnp.swapaxes(rows.reshape(rows.shape[0] // S5_SEG, S5_SEG, rows.shape[1]), 0, 1)


def _seq_specs(n_i, rev):
    at = (lambda i: n_i - 1 - i) if rev else (lambda i: i)
    seg = pl.BlockSpec((S5_SEG, S5_STEPS, B_WIDTH), lambda i: (0, at(i), 0))
    x_spec = pl.BlockSpec((S5_SEG * S5_STEPS, S5_WIDTH), lambda i: (at(i), 0))
    return seg, x_spec, _fix((S5_TILES, 128, S5_LANES)), _fix((S5_TILES, S5_LANES, 128)), _fix((1, S5_WIDTH)), _fix((S5_SEG, S5_WIDTH))


def _tile_dots(dot, lhs, w_ref, lhs_width):
    return jnp.concatenate([dot(lhs[:, t * lhs_width:(t + 1) * lhs_width], w_ref[t]) for t in range(S5_TILES)], axis=1)


def _s5_entries(name, end_r, end_i, abr, abi, steps, reverse):
    def body(er_ref, ei_ref, ar_ref, ai_ref, or_ref, oi_ref):
        pr, pi_ = ar_ref[...], ai_ref[...]
        for _ in range(int(math.log2(steps))):
            pr, pi_ = pr * pr - pi_ * pi_, 2.0 * pr * pi_
        er, ei = er_ref[...], ei_ref[...]
        rows = lax.broadcasted_iota(jnp.int32, er.shape, 0)
        cr, ci = jnp.zeros_like(pr), jnp.zeros_like(pr)
        out_r, out_i = jnp.zeros_like(er), jnp.zeros_like(er)
        for g in (range(S5_SEG - 2, -1, -1) if reverse else range(1, S5_SEG)):
            src = g + 1 if reverse else g - 1
            cr, ci = _pick_row(er, src) + pr * cr - pi_ * ci, _pick_row(ei, src) + pr * ci + pi_ * cr
            out_r, out_i = jnp.where(rows == g, cr, out_r), jnp.where(rows == g, ci, out_i)
        or_ref[...] = out_r
        oi_ref[...] = out_i

    return pl.pallas_call(body, name=name, out_shape=[_sds(*end_r.shape)] * 2)(end_r, end_i, abr, abi)


def _s5_seq_fwd(u, btr, bti, ctr, cti, abr, abi, entry, store, exchange=None):
    s_len = u.shape[0]
    seg_len = s_len // S5_SEG
    n_i = seg_len // S5_STEPS
    rows = S5_SEG * S5_STEPS

    def body(u_ref, btr_ref, bti_ref, ctr_ref, cti_ref, ar_ref, ai_ref, er_ref, ei_ref, *rest):
        if store:
            xr_ref, xi_ref, y_ref, endr_ref, endi_ref, sr_ref, si_ref = rest
        else:
            endr_ref, endi_ref, sr_ref, si_ref = rest
        i = pl.program_id(0)

        @pl.when(i == 0)
        def _():
            sr_ref[...] = er_ref[...]
            si_ref[...] = ei_ref[...]

        ar = jnp.broadcast_to(ar_ref[...], (S5_SEG, S5_WIDTH))
        ai = jnp.broadcast_to(ai_ref[...], (S5_SEG, S5_WIDTH))
        ub = _seg_rows(u_ref[...])
        br, bi = _tile_dots(_bdot, ub, btr_ref, 128), _tile_dots(_bdot, ub, bti_ref, 128)
        sr, si = sr_ref[...], si_ref[...]
        for s in range(S5_STEPS):
            at = slice(S5_SEG * s, S5_SEG * (s + 1))
            sr, si = ar * sr - ai * si + br[at], ar * si + ai * sr + bi[at]
            if store:
                xr_ref[at, :] = sr
                xi_ref[at, :] = si
        sr_ref[...] = sr
        si_ref[...] = si
        if store:
            y_ref[...] = _seg_block(_tile_dots(_bdot, xr_ref[...], ctr_ref, S5_LANES) - _tile_dots(_bdot, xi_ref[...], cti_ref, S5_LANES))

        @pl.when(i == n_i - 1)
        def _():
            endr_ref[...] = sr
            endi_ref[...] = si

    seg, x_spec, b_spec, c_spec, a_spec, e_spec = _seq_specs(n_i, False)
    ends = [_sds(S5_SEG, S5_WIDTH)] * 2
    full = [_sds(s_len, S5_WIDTH)] * 2 + [_sds(S5_SEG, seg_len, B_WIDTH)] if store else []
    return _call_with_exchange(
        body, "s5_scan_fwd" if store else "s5_ends_fwd", (n_i,),
        [seg, b_spec, b_spec, c_spec, c_spec, a_spec, a_spec, e_spec, e_spec],
        ([x_spec, x_spec, seg] if store else []) + [e_spec, e_spec], full + ends,
        [pltpu.VMEM((S5_SEG, S5_WIDTH), F32)] * 2,
        (u.reshape(S5_SEG, seg_len, B_WIDTH), btr, bti, ctr, cti, abr, abi, *entry), exchange)


def _s5_seq_bwd(dy, xr, xi, u, btr, bti, ctr, cti, abr, abi, g_entry, x_entry, full, exchange=None):
    s_len = dy.shape[0]
    seg_len = s_len // S5_SEG
    n_i = seg_len // S5_STEPS
    rows = S5_SEG * S5_STEPS

    def body(*refs):
        if full:
            (dy_ref, btr_ref, bti_ref, ctr_ref, cti_ref, ar_ref, ai_ref, ger_ref, gei_ref,
             xr_ref, xi_ref, xrp_ref, xip_ref, xer_ref, xei_ref, u_ref,
             du_ref, dbtr_ref, dbti_ref, dctr_ref, dcti_ref, dar_ref, dai_ref, str_ref, sti_ref,
             sr_ref, si_ref, gr_s, gi_s) = refs
        else:
            (dy_ref, btr_ref, bti_ref, ctr_ref, cti_ref, ar_ref, ai_ref, ger_ref, gei_ref, str_ref, sti_ref, sr_ref, si_ref) = refs
        i = pl.program_id(0)

        @pl.when(i == 0)
        def _():
            sr_ref[...] = ger_ref[...]
            si_ref[...] = gei_ref[...]
            if full:
                for r in (dbtr_ref, dbti_ref, dctr_ref, dcti_ref, dar_ref, dai_ref):
                    r[...] = jnp.zeros_like(r)

        ar = jnp.broadcast_to(ar_ref[...], (S5_SEG, S5_WIDTH))
        ai = -jnp.broadcast_to(ai_ref[...], (S5_SEG, S5_WIDTH))
        dyb = _seg_rows(dy_ref[...])
        gr, gi = _tile_dots(_bdot_nt, dyb, ctr_ref, 128), -_tile_dots(_bdot_nt, dyb, cti_ref, 128)
        sr, si = sr_ref[...], si_ref[...]
        for s in range(S5_STEPS - 1, -1, -1):
            at = slice(S5_SEG * s, S5_SEG * (s + 1))
            sr, si = ar * sr - ai * si + gr[at], ar * si + ai * sr + gi[at]
            if full:
                gr_s[at, :] = sr
                gi_s[at, :] = si
        sr_ref[...] = sr
        si_ref[...] = si

        @pl.when(i == n_i - 1)
        def _():
            str_ref[...] = sr
            sti_ref[...] = si

        if full:
            g_r, g_i = gr_s[...], gi_s[...]
            du_ref[...] = _seg_block(_tile_dots(_bdot_nt, g_r, btr_ref, S5_LANES) + _tile_dots(_bdot_nt, g_i, bti_ref, S5_LANES))
            ub = _seg_rows(u_ref[...])
            xr_b, xi_b = xr_ref[...], xi_ref[...]
            for t in range(S5_TILES):
                lanes, cols = slice(t * S5_LANES, (t + 1) * S5_LANES), slice(t * 128, (t + 1) * 128)
                dbtr_ref[t] += _bdot_tn(ub[:, cols], g_r[:, lanes])
                dbti_ref[t] += _bdot_tn(ub[:, cols], g_i[:, lanes])
                dctr_ref[t] += _bdot_tn(xr_b[:, lanes], dyb[:, cols])
                dcti_ref[t] -= _bdot_tn(xi_b[:, lanes], dyb[:, cols])
            first = i == n_i - 1
            xpr = jnp.concatenate([jnp.where(first, xer_ref[...], xrp_ref[...]), xr_b[:rows - S5_SEG]], axis=0)
            xpi = jnp.concatenate([jnp.where(first, xei_ref[...], xip_ref[...]), xi_b[:rows - S5_SEG]], axis=0)
            dar_ref[...] += jnp.sum(g_r * xpr + g_i * xpi, axis=0, keepdims=True)
            dai_ref[...] += jnp.sum(g_i * xpr - g_r * xpi, axis=0, keepdims=True)

    seg, x_spec, b_spec, c_spec, a_spec, e_spec = _seq_specs(n_i, True)
    halo = pl.BlockSpec((S5_SEG, S5_WIDTH), lambda i: (jnp.maximum((n_i - 1 - i) * S5_STEPS - 1, 0), 0))
    starts = [_sds(S5_SEG, S5_WIDTH)] * 2
    in_specs = [seg, b_spec, b_spec, c_spec, c_spec, a_spec, a_spec, e_spec, e_spec]
    args = [dy.reshape(S5_SEG, seg_len, B_WIDTH), btr, bti, ctr, cti, abr, abi, *g_entry]
    state = [pltpu.VMEM((S5_SEG, S5_WIDTH), F32)] * 2
    if not full:
        return _call_with_exchange(body, "s5_starts_bwd", (n_i,), in_specs, [e_spec, e_spec], starts, state, args, None)
    return _call_with_exchange(
        body, "s5_scan_bwd", (n_i,),
        in_specs + [x_spec, x_spec, halo, halo, e_spec, e_spec, seg],
        [seg, b_spec, b_spec, c_spec, c_spec, a_spec, a_spec, e_spec, e_spec],
        [_sds(S5_SEG, seg_len, B_WIDTH)] + [_sds(S5_TILES, 128, S5_LANES)] * 2 + [_sds(S5_TILES, S5_LANES, 128)] * 2
        + [_sds(1, S5_WIDTH)] * 2 + starts,
        state + [pltpu.VMEM((rows, S5_WIDTH), F32)] * 2,
        args + [xr, xi, xr, xi, *x_entry, u.reshape(S5_SEG, seg_len, B_WIDTH)], exchange)


def _blockdiag_b(bbar_t):
    blocks = bbar_t.reshape(S5_TILES, 8, B_GROUP, B_STATE)
    return jnp.einsum('jgmp,gh->jgmhp', blocks, jnp.eye(8, dtype=F32)).reshape(S5_TILES, 128, S5_LANES)


def _blockdiag_b_t(d):
    return jnp.einsum('jgmgp->jgmp', d.reshape(S5_TILES, 8, B_GROUP, 8, B_STATE)).reshape(B_WIDTH, B_STATE)


def _blockdiag_c(c):
    blocks = c.reshape(S5_TILES, 8, B_GROUP, B_STATE)
    return jnp.einsum('jgmp,gh->jhpgm', blocks, jnp.eye(8, dtype=F32)).reshape(S5_TILES, S5_LANES, 128)


def _blockdiag_c_t(d):
    return jnp.einsum('jgpgm->jgmp', d.reshape(S5_TILES, 8, B_STATE, 8, B_GROUP)).reshape(B_GROUPS, B_GROUP, B_STATE)


def _l0_out(os, ls, ga, gb, ypre, u, x, d_skip, glu_w, glu_b, w_out, post_g, gate):
    s_len = x.shape[0]

    def body(o0, o1, o2, l0, l1, l2, ga_ref, gb_ref, yp_ref, u_ref, x_ref, d_ref, gw_ref, gbias_ref, w_ref, pg_ref, gt_ref, x1_ref, y_ref):
        oa = _merge_gate(*[_from_res(r[...]) for r in (o0, o1, o2, l0, l1, l2)], ga_ref[...])
        yb = _s5_gelu(yp_ref[...], u_ref[...], d_ref[...])
        ob = _s5_glu(yb, _bdot(yb, gw_ref[...]) + gbias_ref[...], gb_ref[...])
        y = _bdot(oa, w_ref[0:512, :]) + _bdot(ob, w_ref[512:1024, :])
        y_ref[...] = y
        x1_ref[...] = _post_res(y, x_ref[...], pg_ref[...], gt_ref[...])

    vec, half = _fix((1, D_MODEL)), _fix((1, 512))
    return pl.pallas_call(
        body, name="l0_out", grid=(s_len // TM,),
        in_specs=[_res_spec(d) for d in DILATIONS] * 2 + [_row(TM, 512)] * 4
        + [_row(TM, D_MODEL), half, _fix((512, 512)), half, _fix((D_MODEL, D_MODEL)), vec, vec],
        out_specs=[_row(TM, D_MODEL)] * 2,
        out_shape=[_sds(s_len, D_MODEL)] * 2,
        compiler_params=_cp("arbitrary"),
    )(*os, *ls, ga, gb, ypre, u, x, d_skip, glu_w, glu_b, w_out, post_g, gate)


def _l0_out_bwd(os, ls, ga, gb, ypre, u, x, y, d_skip, glu_w, glu_b, w_out, post_g, gate, dx1, exchange=None):
    s_len = x.shape[0]

    def body(o0, o1, o2, l0, l1, l2, ga_ref, gb_ref, yp_ref, u_ref, x_ref, y_ref, d_ref, gw_ref, gbias_ref, w_ref, pg_ref, gt_ref, dx1_ref,
             do0, do1, do2, dl0, dl1, dl2, dga_ref, dgb_ref, dyp_ref, du_ref, dd_ref, dgw_ref, dgbias_ref, dw_ref, dpg_ref, dgt_ref):
        _zero_at_first([dd_ref, dgw_ref, dgbias_ref, dw_ref, dpg_ref, dgt_ref])
        _, vjp2 = jax.vjp(_post_res, y_ref[...], x_ref[...], pg_ref[...], gt_ref[...])
        dy, _, dpg, dgt = vjp2(dx1_ref[...])
        _acc(dpg_ref, dpg)
        _acc(dgt_ref, dgt)
        oa, vjp_a = jax.vjp(_merge_gate, *[_from_res(r[...]) for r in (o0, o1, o2, l0, l1, l2)], ga_ref[...])
        yb, vjp_g = jax.vjp(_s5_gelu, yp_ref[...], u_ref[...], d_ref[...])
        gl = _bdot(yb, gw_ref[...]) + gbias_ref[...]
        ob, vjp_b = jax.vjp(_s5_glu, yb, gl, gb_ref[...])
        dw_ref[0:512, :] += _bdot_tn(oa, dy)
        dw_ref[512:1024, :] += _bdot_tn(ob, dy)
        d1, d2, d3, e1, e2, e3, dga = vjp_a(_bdot_nt(dy, w_ref[0:512, :]))
        for ref, val, d in zip((do0, do1, do2, dl0, dl1, dl2), (d1, d2, d3, e1, e2, e3), DILATIONS * 2):
            ref[...] = _to_res(val, d)
        dga_ref[...] = dga
        dyb, dgl, dgb = vjp_b(_bdot_nt(dy, w_ref[512:1024, :]))
        dgb_ref[...] = dgb
        dgw_ref[...] += _bdot_tn(yb, dgl)
        _acc(dgbias_ref, jnp.sum(dgl, axis=0, keepdims=True))
        dyp, du, dd = vjp_g(dyb + _bdot_nt(dgl, gw_ref[...]))
        dyp_ref[...] = dyp
        du_ref[...] = du
        _acc(dd_ref, dd)

    vec, half = _fix((1, D_MODEL)), _fix((1, 512))
    r5, r10 = _row(TM, 512), _row(TM, D_MODEL)
    res6 = [_res_spec(d) for d in DILATIONS] * 2
    return _call_with_exchange(
        body, "l0_out_bwd", (s_len // TM,),
        res6 + [r5] * 4 + [r10, r10, half, _fix((512, 512)), half, _fix((D_MODEL, D_MODEL)), vec, vec, r10],
        res6 + [r5] * 4 + [half, _fix((512, 512)), half, _fix((D_MODEL, D_MODEL)), vec, vec],
        [_sds(*_res_shape(s_len, d)) for d in DILATIONS] * 2 + [_sds(s_len, 512)] * 4
        + [_sds(1, 512), _sds(512, 512), _sds(1, 512), _sds(D_MODEL, D_MODEL), _sds(1, D_MODEL), _sds(1, D_MODEL)],
        [], (*os, *ls, ga, gb, ypre, u, x, y, d_skip, glu_w, glu_b, w_out, post_g, gate, dx1), exchange)


def _l1_front(x, pre_g, scale, shift, w_in):
    s_len = x.shape[0]

    def body(x_ref, g_ref, sc_ref, sh_ref, w_ref, raw_ref, gate_ref, ba_ref, h_ref):
        hb = _pre_mod(x_ref[...], g_ref[...], sc_ref[...], sh_ref[...]).astype(BF)
        h_ref[...] = hb
        z = jnp.dot(hb, w_ref[...], preferred_element_type=F32)
        raw_ref[...] = z[:, 0:QKV]
        gate_ref[...] = z[:, QKV:QKV + 1024]
        ba_ref[...] = z[:, QKV + 1024:C_IN_PAD]

    vec = _fix((1, D_MODEL))
    return pl.pallas_call(
        body, name="l1_front", grid=(s_len // TM,),
        in_specs=[_row(TM, D_MODEL), vec, vec, vec, _fix((D_MODEL, C_IN_PAD))],
        out_specs=[_row(TM, QKV), _row(TM, 1024), _row(TM, 128), _row(TM, D_MODEL)],
        out_shape=[_sds(s_len, QKV), _sds(s_len, 1024), _sds(s_len, 128), _sds(s_len, D_MODEL, dtype=BF)],
        compiler_params=_cp("arbitrary"),
    )(x, pre_g, scale, shift, w_in)


def _bg_fn(ba, alog_row, dtb_row):
    lane = lax.broadcasted_iota(jnp.int32, (1, 128), 1)
    g = -jnp.exp(alog_row) * jax.nn.softplus(ba + dtb_row)
    return jnp.where(lane < C_HEADS, jax.nn.sigmoid(ba), jnp.where(lane < 2 * C_HEADS, g, 0.0))


def _act_q(c):
    q = jax.nn.silu(c)
    return q * lax.rsqrt(jnp.sum(q * q, axis=-1, keepdims=True) + EPS) * (C_DK ** -0.5)


def _act_k(c):
    k = jax.nn.silu(c)
    return k * lax.rsqrt(jnp.sum(k * k, axis=-1, keepdims=True) + EPS)


def _act_of(s):
    return _act_q if s < 8 else (_act_k if s < 16 else jax.nn.silu)


def _gdn_prep(raw, ba, conv_w, alog_row, dtb_row):
    s_len = raw.shape[0]

    def body(raw_ref, halo_ref, ba_ref, w_ref, al_ref, dt_ref, qkv_ref, bg_ref):
        bg_ref[...] = _bg_fn(ba_ref[...], al_ref[...], dt_ref[...])
        has_prev = (pl.program_id(0) > 0).astype(F32)
        for s in range(24):
            sl = slice(s * 128, (s + 1) * 128)
            cat = jnp.concatenate([halo_ref[:, sl] * has_prev, raw_ref[:, sl]], axis=0)
            conv = w_ref[3:4, sl] * cat[8:]
            for j in range(3):
                conv = conv + w_ref[j:j + 1, sl] * pltpu.roll(cat, 3 - j, 0)[8:]
            qkv_ref[:, sl] = _act_of(s)(conv)

    halo = pl.BlockSpec((8, QKV), lambda i: (jnp.maximum(i * (TM // 8) - 1, 0), 0))
    row128 = _fix((1, 128))
    return pl.pallas_call(
        body, name="gdn_prep", grid=(s_len // TM,),
        in_specs=[_row(TM, QKV), halo, _row(TM, 128), _fix((C_CONV, QKV)), row128, row128],
        out_specs=[_row(TM, QKV), _row(TM, 128)],
        out_shape=[_sds(s_len, QKV), _sds(s_len, 128)],
        compiler_params=_cp("arbitrary"),
    )(raw, raw, ba, conv_w, alog_row, dtb_row)


def _gdn_prep_bwd(raw, ba, conv_w, alog_row, dtb_row, dq, dk, dv, dbg):
    s_len = raw.shape[0]
    n_tiles = s_len // TM
    ext = TM + 8

    def body(raw_ref, prev_ref, next_ref, ba_ref, w_ref, al_ref, dt_ref, dq_ref, dqn_ref, dk_ref, dkn_ref, dv_ref, dvn_ref, dbg_ref,
             draw_ref, dba_ref, dw_ref, dal_ref, ddt_ref):
        _zero_at_first([dw_ref, dal_ref, ddt_ref])
        i = pl.program_id(0)
        _, vjp_bg = jax.vjp(_bg_fn, ba_ref[...], al_ref[...], dt_ref[...])
        dba, dal, ddt = vjp_bg(dbg_ref[...])
        dba_ref[...] = dba
        _acc(dal_ref, dal)
        _acc(ddt_ref, ddt)
        has_prev = (i > 0).astype(F32)
        has_next = (i < n_tiles - 1).astype(F32)
        ct_refs = ((dq_ref, dqn_ref), (dk_ref, dkn_ref), (dv_ref, dvn_ref))
        for s in range(24):
            sl = slice(s * 128, (s + 1) * 128)
            hl = slice((s % 8) * 128, (s % 8 + 1) * 128)
            tile_ref, nxt_ref = ct_refs[s // 8]
            cat = jnp.concatenate([prev_ref[:, sl] * has_prev, raw_ref[:, sl], next_ref[:, sl] * has_next], axis=0)
            shifted = [pltpu.roll(cat, 3 - j, 0)[8:] for j in range(3)] + [cat[8:]]
            conv = w_ref[3:4, sl] * shifted[3]
            for j in range(3):
                conv = conv + w_ref[j:j + 1, sl] * shifted[j]
            ct = jnp.concatenate([tile_ref[:, hl], nxt_ref[:, hl] * has_next], axis=0)
            _, vjp_act = jax.vjp(_act_of(s), conv)
            dconv, = vjp_act(ct)
            draw = w_ref[3:4, sl] * dconv[:TM]
            for j in range(3):
                draw = draw + w_ref[j:j + 1, sl] * pltpu.roll(dconv, ext - (3 - j), 0)[:TM]
            draw_ref[:, sl] = draw
            for j in range(4):
                dw_ref[j:j + 1, sl] += jnp.sum(dconv[:TM] * shifted[j][:TM], axis=0, keepdims=True)

    prev = pl.BlockSpec((8, QKV), lambda i: (jnp.maximum(i * (TM // 8) - 1, 0), 0))
    nxt = lambda n: pl.BlockSpec((8, n), lambda i: (jnp.minimum((i + 1) * (TM // 8), s_len // 8 - 1), 0))
    row128 = _fix((1, 128))
    ct_specs = [_row(TM, 1024), nxt(1024)] * 3
    return pl.pallas_call(
        body, name="gdn_prep_bwd", grid=(n_tiles,),
        in_specs=[_row(TM, QKV), prev, nxt(QKV), _row(TM, 128), _fix((C_CONV, QKV)), row128, row128] + ct_specs + [_row(TM, 128)],
        out_specs=[_row(TM, QKV), _row(TM, 128), _fix((C_CONV, QKV)), row128, row128],
        out_shape=[_sds(s_len, QKV), _sds(s_len, 128), _sds(C_CONV, QKV), _sds(1, 128), _sds(1, 128)],
        compiler_params=_cp("arbitrary"),
    )(raw, raw, raw, ba, conv_w, alog_row, dtb_row, dq, dq, dk, dk, dv, dv, dbg)


def _tein(eq, a, b):
    return jnp.einsum(eq, a, b, precision=lax.Precision.HIGH, preferred_element_type=F32)


def _unit_lower_inverse(lower):
    ri = lax.broadcasted_iota(jnp.int32, (C_CHUNK, C_CHUNK), 0)
    ci = lax.broadcasted_iota(jnp.int32, (C_CHUNK, C_CHUNK), 1)
    eye = (ri == ci).astype(F32)[None]
    p_mat = -lower
    inv = eye + p_mat
    for _ in range(5):
        p_mat = _bein('hij,hjk->hik', p_mat, p_mat)
        inv = inv + _bein('hij,hjk->hik', inv, p_mat)
    inv = _tein('hij,hjk->hik', inv, 2.0 * eye - _tein('hij,hjk->hik', eye + lower, inv))
    return jnp.where((ri >= ci)[None], inv, 0.0)


@jax.custom_vjp
def _known_inverse(lower, inv):
    return inv


def _known_inverse_fwd(lower, inv):
    return inv, inv


def _known_inverse_bwd(inv, d_inv):
    d_lower = -_bein('hik,hjk->hij', _bein('hji,hjk->hik', inv, d_inv), inv)
    return d_lower, jnp.zeros_like(inv)


_known_inverse.defvjp(_known_inverse_fwd, _known_inverse_bwd)


def _gdn_local(q, k, v, bgs, inv_known=None):
    lane = lax.broadcasted_iota(jnp.int32, (1, 128), 1)
    ri = lax.broadcasted_iota(jnp.int32, (C_CHUNK, C_CHUNK), 0)
    ci = lax.broadcasted_iota(jnp.int32, (C_CHUNK, C_CHUNK), 1)
    row_id = lax.broadcasted_iota(jnp.int32, (128, C_CHUNK), 0)
    beta, gc, gcj = [], [], []
    for bg in bgs:
        gc_t = _hdot((ri >= ci).astype(F32), bg)
        gc_rows = gc_t.T
        for h in range(C_HEADS):
            beta.append(jnp.sum(jnp.where(lane == h, bg, 0.0), axis=-1, keepdims=True))
            gc.append(jnp.sum(jnp.where(lane == C_HEADS + h, gc_t, 0.0), axis=-1, keepdims=True))
            gcj.append(jnp.sum(jnp.where(row_id == C_HEADS + h, gc_rows, 0.0), axis=0, keepdims=True))
    beta, gc, gcj = jnp.stack(beta, axis=0), jnp.stack(gc, axis=0), jnp.stack(gcj, axis=0)
    tril, strict = (ri >= ci)[None], (ri > ci)[None]
    decay = jnp.exp(jnp.where(tril, gc - gcj, -1e30))
    kb = k * beta
    lower = jnp.where(strict, _bein('hid,hjd->hij', kb, k) * decay, 0.0)
    inv = _unit_lower_inverse(lower) if inv_known is None else _known_inverse(lower, inv_known)
    egc = jnp.exp(gc)
    u_c = _bein('hij,hjd->hid', inv, v * beta)
    w_c = _bein('hij,hjd->hid', inv, kb * egc)
    aqk = _bein('hid,hjd->hij', q, k) * decay
    rowi = lax.broadcasted_iota(jnp.int32, (1, C_CHUNK, 1), 1)
    g_last = jnp.sum(jnp.where(rowi == C_CHUNK - 1, gc, 0.0), axis=1, keepdims=True)
    kd = k * jnp.exp(g_last - gc)
    return (u_c, w_c, aqk, q * egc, kd, jnp.exp(g_last)), inv


def _gdn_state(local, state):
    u_c, w_c, aqk, qg, kd, dec = local
    v_new = u_c - _bein('hik,hkv->hiv', w_c, state)
    o = _bein('hik,hkv->hiv', qg, state) + _bein('hij,hjv->hiv', aqk, v_new)
    return o, state * dec + _bein('hik,hiv->hkv', kd, v_new)


C_SUB = 4


def _gdn_group(q, k, v, bgs, state, inv_known=None):
    local, inv = _gdn_local(q, k, v, bgs, inv_known)
    outs = []
    for s in range(len(bgs)):
        o, state = _gdn_state(tuple(t[s * C_HEADS:(s + 1) * C_HEADS] for t in local), state)
        outs.append(o)
    return outs, state, inv


def _heads(ref):
    return jnp.stack([ref[s * C_CHUNK:(s + 1) * C_CHUNK, h * C_DK:(h + 1) * C_DK] for s in range(C_SUB) for h in range(C_HEADS)], axis=0)


def _put_heads(ref, sub, val):
    rows = slice(sub * C_CHUNK, (sub + 1) * C_CHUNK)
    for h in range(C_HEADS):
        ref[rows, h * C_DK:(h + 1) * C_DK] = val[h]


def _gdn_specs(s_len, rev):
    rows = C_SUB * C_CHUNK
    n_g = s_len // rows
    at = (lambda i: n_g - 1 - i) if rev else (lambda i: i)
    col = lambda c: pl.BlockSpec((rows, 1024), lambda i: (at(i), c))
    row128 = pl.BlockSpec((rows, 128), lambda i: (at(i), 0))
    state = pl.BlockSpec((1, C_HEADS, C_DK, C_DK), lambda i: (at(i), 0, 0, 0))
    inv = pl.BlockSpec((1, C_SUB * C_HEADS, C_CHUNK, C_CHUNK), lambda i: (at(i), 0, 0, 0))
    return n_g, col, row128, state, inv


def _gdn_fwd(qkv, bg):
    s_len = qkv.shape[0]
    n_g, col, row128, state_spec, inv_spec = _gdn_specs(s_len, False)

    def body(q_ref, k_ref, v_ref, bg_ref, o_ref, ss_ref, inv_ref, st_ref):
        _zero_at_first([st_ref])
        s0 = st_ref[...]
        ss_ref[0] = s0
        bgs = [bg_ref[s * C_CHUNK:(s + 1) * C_CHUNK, :] for s in range(C_SUB)]
        outs, s2, inv = _gdn_group(_heads(q_ref), _heads(k_ref), _heads(v_ref), bgs, s0)
        st_ref[...] = s2
        inv_ref[0] = inv
        for s in range(C_SUB):
            _put_heads(o_ref, s, outs[s])

    return pl.pallas_call(
        body, name="gdn_fwd", grid=(n_g,),
        in_specs=[col(0), col(1), col(2), row128],
        out_specs=[col(0), state_spec, inv_spec],
        out_shape=[_sds(s_len, 1024), _sds(n_g, C_HEADS, C_DK, C_DK), _sds(n_g, C_SUB * C_HEADS, C_CHUNK, C_CHUNK)],
        scratch_shapes=[pltpu.VMEM((C_HEADS, C_DK, C_DK), F32)],
        compiler_params=_cp("arbitrary"),
    )(qkv, qkv, qkv, bg)


def _gdn_bwd(qkv, bg, states, invs, do):
    s_len = qkv.shape[0]
    n_g, col, row128, state_spec, inv_spec = _gdn_specs(s_len, True)

    def body(q_ref, k_ref, v_ref, bg_ref, ss_ref, inv_ref, do_ref, dq_ref, dk_ref, dv_ref, dbg_ref, ds_ref):
        _zero_at_first([ds_ref])
        inv_known = inv_ref[0]

        def group(q, k, v, bgs, st):
            outs, st2, _ = _gdn_group(q, k, v, bgs, st, inv_known)
            return outs, st2

        bgs = [bg_ref[s * C_CHUNK:(s + 1) * C_CHUNK, :] for s in range(C_SUB)]
        _, vjp = jax.vjp(group, _heads(q_ref), _heads(k_ref), _heads(v_ref), bgs, ss_ref[0])
        douts = [jnp.stack([do_ref[s * C_CHUNK:(s + 1) * C_CHUNK, h * C_DK:(h + 1) * C_DK] for h in range(C_HEADS)], axis=0)
                 for s in range(C_SUB)]
        dq, dk, dv, dbgs, ds = vjp((douts, ds_ref[...]))
        ds_ref[...] = ds
        for s in range(C_SUB):
            dbg_ref[s * C_CHUNK:(s + 1) * C_CHUNK, :] = dbgs[s]
            for ref, val in ((dq_ref, dq), (dk_ref, dk), (dv_ref, dv)):
                _put_heads(ref, s, val[s * C_HEADS:(s + 1) * C_HEADS])

    return pl.pallas_call(
        body, name="gdn_bwd", grid=(n_g,),
        in_specs=[col(0), col(1), col(2), row128, state_spec, inv_spec, col(0)],
        out_specs=[col(0), col(0), col(0), row128],
        out_shape=[_sds(s_len, 1024)] * 3 + [_sds(s_len, 128)],
        scratch_shapes=[pltpu.VMEM((C_HEADS, C_DK, C_DK), F32)],
        compiler_params=_cp("arbitrary"),
    )(qkv, qkv, qkv, bg, states, invs, do)


def _head_norm_gate(o, gate, norm_g):
    return (_rms(o) * norm_g) * jax.nn.silu(gate)


def _l1_out_fb(o, gate_c, x1, target, norm_g, w_out, post_g, gate):
    s_len = x1.shape[0]

    def body(o_ref, gc_ref, x1_ref, t_ref, ng_ref, w_ref, pg_ref, gt_ref,
             loss_ref, dres_ref, do_ref, dgc_ref, dw_ref, dng_ref, dpg_ref, dgt_ref):
        _zero_at_first([loss_ref, dw_ref, dng_ref, dpg_ref, dgt_ref])
        ng = ng_ref[...]
        ons, vjps = [], []
        for h in range(C_HEADS):
            sl = slice(h * C_DK, (h + 1) * C_DK)
            on, vjp_h = jax.vjp(_head_norm_gate, o_ref[:, sl], gc_ref[:, sl], ng)
            ons.append(on)
            vjps.append(vjp_h)
        on_all = jnp.concatenate(ons, axis=-1)
        y = _bdot(on_all, w_ref[...])
        x2, vjp2 = jax.vjp(_post_res, y, x1_ref[...], pg_ref[...], gt_ref[...])
        err = x2 - t_ref[...]
        _acc(loss_ref, jnp.full((1, 128), 0.5 * jnp.sum(jnp.mean(err * err, axis=-1)), F32))
        dx2 = err * (1.0 / D_MODEL)
        dy, _, dpg, dgt = vjp2(dx2)
        dres_ref[...] = dx2
        _acc(dpg_ref, dpg)
        _acc(dgt_ref, dgt)
        dw_ref[...] += _bdot_tn(on_all, dy)
        don = _bdot_nt(dy, w_ref[...])
        for h in range(C_HEADS):
            sl = slice(h * C_DK, (h + 1) * C_DK)
            do_h, dgc_h, dng = vjps[h](don[:, sl])
            do_ref[:, sl] = do_h
            dgc_ref[:, sl] = dgc_h
            _acc(dng_ref, dng)

    vec, r10 = _fix((1, D_MODEL)), _row(TM, D_MODEL)
    row128 = _fix((1, 128))
    return pl.pallas_call(
        body, name="l1_out_fb", grid=(s_len // TM,),
        in_specs=[r10, r10, r10, r10, row128, _fix((D_MODEL, D_MODEL)), vec, vec],
        out_specs=[row128, r10, r10, r10, _fix((D_MODEL, D_MODEL)), row128, vec, vec],
        out_shape=[_sds(1, 128), _sds(s_len, D_MODEL), _sds(s_len, D_MODEL), _sds(s_len, D_MODEL),
                   _sds(D_MODEL, D_MODEL), _sds(1, 128), _sds(1, D_MODEL), _sds(1, D_MODEL)],
        compiler_params=_cp("arbitrary"),
    )(o, gate_c, x1, target, norm_g, w_out, post_g, gate)


def _row_of(v, width, at):
    return jnp.zeros((1, width), F32).at[0, at:at + v.shape[-1]].set(v.reshape(-1))


def _local_step(x, target, mod, wd, comm=None):
    s_len = x.shape[0]
    shift0, scale0, gate0 = (mod[0:1, i * 1024:(i + 1) * 1024] for i in range(3))
    shift1, scale1, gate1 = (mod[1:2, i * 1024:(i + 1) * 1024] for i in range(3))
    pre_g0, pre_g1 = wd["pre_g"][0:1], wd["pre_g"][1:2]
    post_g0, post_g1 = wd["post_g"][0:1], wd["post_g"][1:2]
    w_in0 = wd["ab_w_in"].astype(BF)
    d_skip, glu_b = wd["s5_d"].reshape(1, 512), wd["s5_glu_b"].reshape(1, 512)
    norm_g = wd["gdn_norm_g"].reshape(1, 128)
    alog_row = _row_of(wd["gdn_a_log"], 128, C_HEADS)
    dtb_row = _row_of(wd["gdn_dt_bias"], 128, C_HEADS)
    conv_w = wd["gdn_conv"]

    a_re, a_im = wd["s5_a_re"], wd["s5_a_im"]
    log_dt = wd["s5_log_dt"].reshape(B_GROUPS, 1)
    bt_re = wd["s5_b_re"].transpose(0, 2, 1).reshape(B_WIDTH, B_STATE)
    bt_im = wd["s5_b_im"].transpose(0, 2, 1).reshape(B_WIDTH, B_STATE)
    abar_r, abar_i, bbar_r, bbar_i = _s5_params(a_re, a_im, log_dt, bt_re, bt_im)
    abr, abi = abar_r.reshape(1, -1), abar_i.reshape(1, -1)
    btr, bti = _blockdiag_b(bbar_r).astype(BF), _blockdiag_b(bbar_i).astype(BF)
    ctr, cti = _blockdiag_c(wd["s5_c_re"]).astype(BF), _blockdiag_c(wd["s5_c_im"]).astype(BF)

    table = _bucket_table()
    biases = _attn_bias(wd["rel_bias"], jnp.asarray(table))
    front = _l0_front(x, pre_g0, scale0, shift0, w_in0)
    qs, ks, vs = front[0:3], front[3:6], front[6:9]
    u, ga, gb, h0 = front[9:]
    riders = [None] * 4 if comm is None else comm.late_exchanges()
    os, ls, got = [], [], []
    for i in range(3):
        (o_d, l_d), g = _attn_fwd(qs[i], ks[i], vs[i], biases[i], exchange=riders[i])
        os.append(o_d)
        ls.append(l_d)
        got.append(g)
    seg_len = s_len // S5_SEG
    zero_state = (jnp.zeros((S5_SEG, S5_WIDTH), F32),) * 2
    ends, _ = _s5_seq_fwd(u, btr, bti, ctr, cti, abr, abi, zero_state, False)
    x_entry = _s5_entries("s5_entries_fwd", *ends, abr, abi, seg_len, False)
    (xr, xi, ypre3, _, _), g = _s5_seq_fwd(u, btr, bti, ctr, cti, abr, abi, x_entry, True, exchange=riders[3])
    got.append(g)
    ypre = ypre3.reshape(s_len, B_WIDTH)
    if comm is not None:
        wd = {**wd, **comm.late_weights(got)}
    w_out0 = wd["ab_w_out"].astype(BF)
    glu_w = wd["s5_glu_w"].astype(BF)
    w_in1 = jnp.concatenate([wd["gdn_w_in"], jnp.zeros((D_MODEL, C_IN_PAD - wd["gdn_w_in"].shape[1]), wd["gdn_w_in"].dtype)], axis=1).astype(BF)
    w_out1 = wd["gdn_w_out"].astype(BF)
    x1, y0 = _l0_out(os, ls, ga, gb, ypre, u, x, d_skip, glu_w, glu_b, w_out0, post_g0, gate0)

    raw, gate_c, ba, h1 = _l1_front(x1, pre_g1, scale1, shift1, w_in1)
    qkv, bg = _gdn_prep(raw, ba, conv_w, alog_row, dtb_row)
    o_gdn, states, invs = _gdn_fwd(qkv, bg)
    loss_row, dres1, do_gdn, dgate_c, dw_out1, dnorm_g, dpost_g1, dgate1 = _l1_out_fb(
        o_gdn, gate_c, x1, target, norm_g, w_out1, post_g1, gate1)

    dq1, dk1, dv1, dbg = _gdn_bwd(qkv, bg, states, invs, do_gdn)
    draw, dba, dconv_w, dalog_row, ddtb_row = _gdn_prep_bwd(raw, ba, conv_w, alog_row, dtb_row, dq1, dk1, dv1, dbg)
    dz1, dx1, dpre_g1, dscale1, dshift1 = _front_bwd(
        "l1_front_bwd", x1, pre_g1, scale1, shift1, w_in1, dres1, [[draw], [dgate_c], [dba]], [QKV, 1024, 128])
    dw_in1 = _matmul_tn("l1_dw_in", h1, dz1, 1408)

    n_w1 = wd["gdn_w_in"].shape[1]
    ex1 = None if comm is None else (comm.chip_partials("l1", {"gdn_w_in": dw_in1[:, :n_w1], "gdn_w_out": dw_out1}), True)
    l0b, got1 = _l0_out_bwd(os, ls, ga, gb, ypre, u, x, y0, d_skip, glu_w, glu_b, w_out0, post_g0, gate0, dx1, exchange=ex1)
    dos, dls = l0b[0:3], l0b[3:6]
    dga, dgb, dypre, du_skip, dd_skip, dglu_w, dglu_b, dw_out0, dpost_g0, dgate0 = l0b[6:]
    ex2 = None if comm is None else (comm.chip_partials("l0_out", {"ab_w_out": dw_out0, "s5_glu_w": dglu_w}), True)
    starts, _ = _s5_seq_bwd(dypre, None, None, None, btr, bti, ctr, cti, abr, abi, zero_state, None, False)
    g_entry = _s5_entries("s5_entries_bwd", *starts, abr, -abi, seg_len, True)
    (du3, dbtr, dbti, dctr, dcti, dabr, dabi, _, _), got2 = _s5_seq_bwd(
        dypre, xr, xi, u, btr, bti, ctr, cti, abr, abi, g_entry, x_entry, True, exchange=ex2)
    du_scan = du3.reshape(s_len, B_WIDTH)
    if comm is not None:
        comm.received.update(zip(("gdn_w_in", "gdn_w_out", "ab_w_out", "s5_glu_w"), list(got1) + list(got2)))
    dqs, dks, dvs, dbs = [], [], [], []
    for i in range(3):
        dq_d, dk_d, dv_d, db_d = _attn_bwd(qs[i], ks[i], vs[i], biases[i], os[i], ls[i], dos[i], dls[i])
        dqs.append(dq_d)
        dks.append(dk_d)
        dvs.append(dv_d)
        dbs.append(db_d)
    parts = [dqs, dks, dvs, [du_skip, du_scan], [dga], [dgb]]
    dz0, grad_x, dpre_g0, dscale0, dshift0 = _front_bwd(
        "l0_front_bwd", x, pre_g0, scale0, shift0, w_in0, dx1, parts, [512] * 6)
    dw_in0 = _matmul_tn("l0_dw_in", h0, dz0, 768)

    idx_rows = jnp.asarray(table.reshape(3, -1), F32)
    drel = _rel_bias_grad(dbs, idx_rows).T
    da_re, da_im, dlog_dt, dbt_re, dbt_im = _s5_params_bwd(
        a_re, a_im, log_dt, bt_re, bt_im, dabr.reshape(B_GROUPS, B_STATE), dabi.reshape(B_GROUPS, B_STATE),
        _blockdiag_b_t(dbtr), _blockdiag_b_t(dbti))
    unb = lambda d: d.reshape(B_GROUPS, B_GROUP, B_STATE).transpose(0, 2, 1)
    grads = {
        "pre_g": jnp.concatenate([dpre_g0, dpre_g1], 0), "post_g": jnp.concatenate([dpost_g0, dpost_g1], 0),
        "rel_bias": drel, "ab_w_in": dw_in0, "ab_w_out": dw_out0,
        "s5_a_re": da_re, "s5_a_im": da_im, "s5_log_dt": dlog_dt.reshape(B_GROUPS),
        "s5_b_re": unb(dbt_re), "s5_b_im": unb(dbt_im),
        "s5_c_re": _blockdiag_c_t(dctr), "s5_c_im": _blockdiag_c_t(dcti),
        "s5_d": dd_skip.reshape(512), "s5_glu_w": dglu_w, "s5_glu_b": dglu_b.reshape(512),
        "gdn_w_in": dw_in1[:, :wd["gdn_w_in"].shape[1]], "gdn_conv": dconv_w,
        "gdn_a_log": dalog_row[0, C_HEADS:2 * C_HEADS], "gdn_dt_bias": ddtb_row[0, C_HEADS:2 * C_HEADS],
        "gdn_norm_g": dnorm_g.reshape(128), "gdn_w_out": dw_out1,
    }
    dmod = jnp.concatenate([jnp.concatenate([dshift0, dscale0, dgate0], 1), jnp.concatenate([dshift1, dscale1, dgate1], 1)], 0)
    return loss_row[0, 0], grad_x, grads, dmod


def _place():
    return lax.axis_index("x"), lax.axis_index("y"), lax.axis_index("c")


def _flip(v, bit):
    return 1 - v if bit else v


def _hbm_call(name, body, arrs, out_shapes, n_sem):
    any_spec = pl.BlockSpec(memory_space=pl.ANY)
    return pl.pallas_call(
        body, name=name,
        in_specs=[any_spec] * len(arrs), out_specs=[any_spec] * len(out_shapes), out_shape=out_shapes,
        scratch_shapes=[pltpu.SemaphoreType.DMA((n_sem,)), pltpu.SemaphoreType.DMA((n_sem,))],
    )(*arrs)


def _own_slot(gathered, own, slot):
    idx = lax.broadcasted_iota(jnp.int32, (gathered.shape[0],) + (1,) * own.ndim, 0)
    return jnp.where(idx == slot, own[None], gathered)


def _all_gather8(name, arr):
    def body(x_ref, out_ref, send_sems, recv_sems):
        x, y, c = _place()
        me = 4 * x + 2 * y + c
        sends, recvs = [], []
        for m in range(1, 8):
            peer = (_flip(x, m & 4), _flip(y, m & 2), _flip(c, m & 1))
            sends.append(pltpu.make_async_remote_copy(x_ref, out_ref.at[me], send_sems.at[m - 1], recv_sems.at[m - 1],
                                                      device_id=peer, device_id_type=MESH))
            recvs.append(pltpu.make_async_remote_copy(x_ref, out_ref.at[4 * peer[0] + 2 * peer[1] + peer[2]], send_sems.at[m - 1],
                                                      recv_sems.at[m - 1], device_id=peer, device_id_type=MESH))
        for cp in sends:
            cp.start()
        for cp in recvs:
            cp.wait_recv()
        for cp in sends:
            cp.wait_send()

    return _hbm_call(name, body, [arr], [jax.ShapeDtypeStruct((8,) + arr.shape, arr.dtype)], 7)[0]


def _all_to_all8(name, arr):
    def body(x_ref, out_ref, send_sems, recv_sems):
        x, y, c = _place()
        me = 4 * x + 2 * y + c
        sends, recvs = [], []
        for m in range(1, 8):
            peer = (_flip(x, m & 4), _flip(y, m & 2), _flip(c, m & 1))
            peer_id = 4 * peer[0] + 2 * peer[1] + peer[2]
            sends.append(pltpu.make_async_remote_copy(x_ref.at[peer_id], out_ref.at[me], send_sems.at[m - 1], recv_sems.at[m - 1],
                                                      device_id=peer, device_id_type=MESH))
            recvs.append(pltpu.make_async_remote_copy(x_ref.at[peer_id], out_ref.at[peer_id], send_sems.at[m - 1], recv_sems.at[m - 1],
                                                      device_id=peer, device_id_type=MESH))
        for cp in sends:
            cp.start()
        for cp in recvs:
            cp.wait_recv()
        for cp in sends:
            cp.wait_send()

    return _hbm_call(name, body, [arr], [jax.ShapeDtypeStruct(arr.shape, arr.dtype)], 7)[0]


def _chip_copies(ins, outs, send_sems, recv_sems, scatter):
    x, y, c = _place()
    mine = 2 * x + y
    sends, recvs = [], []
    for a in range(len(ins)):
        for m in range(1, 4):
            px, py = _flip(x, m & 2), _flip(y, m & 1)
            k = 3 * a + m - 1
            src = ins[a].at[2 * px + py] if scatter else ins[a]
            sends.append(pltpu.make_async_remote_copy(src, outs[a].at[mine], send_sems.at[k], recv_sems.at[k],
                                                      device_id=(px, py, c), device_id_type=MESH))
            recvs.append(pltpu.make_async_remote_copy(src, outs[a].at[2 * px + py], send_sems.at[k], recv_sems.at[k],
                                                      device_id=(px, py, c), device_id_type=MESH))
    return sends, recvs


def _chip_shapes(arrs, scatter):
    return [jax.ShapeDtypeStruct(a.shape if scatter else (4,) + a.shape, a.dtype) for a in arrs]


def _chip_exchange(name, arrs, scatter):
    n = len(arrs)

    def body(*refs):
        sends, recvs = _chip_copies(refs[:n], refs[n:2 * n], refs[2 * n], refs[2 * n + 1], scatter)
        for cp in sends:
            cp.start()
        for cp in recvs:
            cp.wait_recv()
        for cp in sends:
            cp.wait_send()

    return _hbm_call(name, body, arrs, _chip_shapes(arrs, scatter), 3 * n)


def _call_with_exchange(body, name, grid, in_specs, out_specs, out_shape, scratch_shapes, args, exchange):
    if exchange is None:
        return pl.pallas_call(body, name=name, grid=grid, in_specs=in_specs, out_specs=out_specs, out_shape=out_shape,
                              scratch_shapes=scratch_shapes, compiler_params=_cp(*["arbitrary"] * len(grid)))(*args), []
    arrs, scatter = exchange
    n_in, n_out, n_ex, n_scr = len(in_specs), len(out_specs), len(arrs), len(scratch_shapes)

    def fused(*refs):
        ins, ex_in = refs[:n_in], refs[n_in:n_in + n_ex]
        outs, ex_out = refs[n_in + n_ex:n_in + n_ex + n_out], refs[n_in + n_ex + n_out:n_in + 2 * n_ex + n_out]
        rest = refs[n_in + 2 * n_ex + n_out:]
        sends, recvs = _chip_copies(ex_in, ex_out, rest[n_scr], rest[n_scr + 1], scatter)
        first, last = pl.program_id(0) == 0, pl.program_id(0) == grid[0] - 1
        for k in range(1, len(grid)):
            first, last = first & (pl.program_id(k) == 0), last & (pl.program_id(k) == grid[k] - 1)

        @pl.when(first)
        def _():
            for cp in sends:
                cp.start()

        body(*ins, *outs, *rest[:n_scr])

        @pl.when(last)
        def _():
            for cp in recvs:
                cp.wait_recv()
            for cp in sends:
                cp.wait_send()

    any_spec = pl.BlockSpec(memory_space=pl.ANY)
    res = pl.pallas_call(
        fused, name=name, grid=grid, in_specs=list(in_specs) + [any_spec] * n_ex, out_specs=list(out_specs) + [any_spec] * n_ex,
        out_shape=list(out_shape) + _chip_shapes(arrs, scatter),
        scratch_shapes=list(scratch_shapes) + [pltpu.SemaphoreType.DMA((3 * n_ex,))] * 2,
        compiler_params=_cp(*["arbitrary"] * len(grid)))(*args, *arrs)
    return res[:n_out], res[n_out:]


def _sibling_exchange(name, arrs):
    n = len(arrs)

    def body(*refs):
        ins, outs = refs[:n], refs[n:2 * n]
        send_sems, recv_sems = refs[2 * n:]
        x, y, c = _place()
        copies = [pltpu.make_async_remote_copy(ins[a], outs[a], send_sems.at[a], recv_sems.at[a],
                                               device_id=(x, y, 1 - c), device_id_type=MESH) for a in range(n)]
        for cp in copies:
            cp.start()
        for cp in copies:
            cp.wait_recv()
        for cp in copies:
            cp.wait_send()

    return _hbm_call(name, body, arrs, [jax.ShapeDtypeStruct(a.shape, a.dtype) for a in arrs], n)


def _row_tile(rows):
    for t in (256, 128, 64, 32, 16, 8):
        if rows % t == 0:
            return t
    return rows


def _pair_sum(name, a, b, out_dtype):
    rows, cols = a.shape
    tr = _row_tile(rows)

    def body(a_ref, b_ref, o_ref):
        o_ref[...] = (a_ref[...] + b_ref[...]).astype(out_dtype)

    return pl.pallas_call(body, name=name, grid=(rows // tr,), in_specs=[_row(tr, cols)] * 2, out_specs=_row(tr, cols),
                          out_shape=_sds(rows, cols, dtype=out_dtype), compiler_params=_cp("arbitrary"))(a, b)


def _chip_sum(name, recv, partial, mine):
    n, rows, cols = recv.shape
    tr = _row_tile(rows)

    def body(mine_ref, *refs):
        own = refs[n][0].astype(F32)
        acc = None
        for s in range(n):
            term = jnp.where(mine_ref[0] == s, own, refs[s][0].astype(F32))
            acc = term if acc is None else acc + term
        refs[-1][...] = acc

    def slot_spec(s):
        return pl.BlockSpec((1, tr, cols), lambda i, m: (jnp.where(m[0] == s, (s + 1) % n, s), i, 0))

    grid_spec = pltpu.PrefetchScalarGridSpec(
        num_scalar_prefetch=1, grid=(rows // tr,),
        in_specs=[slot_spec(s) for s in range(n)] + [pl.BlockSpec((1, tr, cols), lambda i, m: (m[0], i, 0))],
        out_specs=pl.BlockSpec((tr, cols), lambda i, m: (i, 0)))
    return pl.pallas_call(body, name=name, grid_spec=grid_spec, out_shape=_sds(rows, cols),
                          compiler_params=_cp("arbitrary"))(mine, *([recv] * n), partial)


def _slot_sum(name, arr):
    n, rows, cols = arr.shape
    tr = _row_tile(rows)

    def body(*refs):
        acc = refs[0][0]
        for r in refs[1:-1]:
            acc = acc + r[0]
        refs[-1][...] = acc

    specs = [pl.BlockSpec((1, tr, cols), functools.partial(lambda s, i: (s, i, 0), s)) for s in range(n)]
    return pl.pallas_call(body, name=name, grid=(rows // tr,), in_specs=specs, out_specs=_row(tr, cols),
                          out_shape=_sds(rows, cols), compiler_params=_cp("arbitrary"))(*([arr] * n))


def _adamw(name, w, g, m, v):
    rows, cols = w.shape
    tr = _row_tile(rows)

    def body(w_ref, g_ref, m_ref, v_ref, d_ref, nm_ref, nv_ref):
        g_ = g_ref[...]
        m_ = ADAM_B1 * m_ref[...] + (1.0 - ADAM_B1) * g_
        v_ = ADAM_B2 * v_ref[...] + (1.0 - ADAM_B2) * (g_ * g_)
        m_hat = m_ / (1.0 - ADAM_B1 ** ADAM_STEP)
        v_hat = v_ / (1.0 - ADAM_B2 ** ADAM_STEP)
        d_ref[...] = -ADAM_LR * (m_hat / (jnp.sqrt(v_hat) + ADAM_EPS) + ADAM_WD * w_ref[...])
        nm_ref[...] = m_
        nv_ref[...] = v_

    spec = _row(tr, cols)
    return pl.pallas_call(body, name=name, grid=(rows // tr,), in_specs=[spec] * 4, out_specs=[spec] * 3,
                          out_shape=[_sds(rows, cols)] * 3, compiler_params=_cp("arbitrary"))(w, g, m, v)


def _adamw_halves(name, w, g_mine, g_sibling, m, v, core):
    _, rows, cols = w.shape
    half = rows // 2
    tr = _row_tile(half)
    per_half = half // tr

    def body(core_ref, w_ref, gm_ref, gs_ref, m_ref, v_ref, g_ref, d_ref, nm_ref, nv_ref):
        g_ = jnp.where(pl.program_id(0) // per_half == core_ref[0], gm_ref[...], gs_ref[...])
        m_ = ADAM_B1 * m_ref[...] + (1.0 - ADAM_B1) * g_
        v_ = ADAM_B2 * v_ref[...] + (1.0 - ADAM_B2) * (g_ * g_)
        m_hat = m_ / (1.0 - ADAM_B1 ** ADAM_STEP)
        v_hat = v_ / (1.0 - ADAM_B2 ** ADAM_STEP)
        g_ref[...] = g_
        d_ref[...] = -ADAM_LR * (m_hat / (jnp.sqrt(v_hat) + ADAM_EPS) + ADAM_WD * w_ref[...])
        nm_ref[...] = m_
        nv_ref[...] = v_

    full = pl.BlockSpec((None, tr, cols), lambda i, c: (0, i, 0))
    in_half = pl.BlockSpec((tr, cols), lambda i, c: (i % per_half, 0))
    grid_spec = pltpu.PrefetchScalarGridSpec(num_scalar_prefetch=1, grid=(rows // tr,),
                                             in_specs=[full, in_half, in_half, full, full], out_specs=[full] * 4)
    return pl.pallas_call(body, name=name, grid_spec=grid_spec, out_shape=[_sds(1, rows, cols)] * 4,
                          compiler_params=_cp("arbitrary"))(core, w, g_mine, g_sibling, m, v)


def _mod_local(c_all, ada_w):
    def body(c_ref, w_ref, o_ref):
        c_act = jax.nn.silu(c_ref[...])
        for l in range(2):
            o_ref[l] = _hdot(c_act, w_ref[l])

    return pl.pallas_call(body, name="mod_local", out_shape=_sds(2, 8, ada_w.shape[2]),
                          compiler_params=pltpu.CompilerParams(vmem_limit_bytes=VMEM_LIMIT_BYTES))(c_all, ada_w)


def _ada_w_grad(c_all, dmod_cols):
    def body(c_ref, d_ref, o_ref):
        c_act = jax.nn.silu(c_ref[...])
        for l in range(2):
            o_ref[l] = lax.dot_general(c_act, d_ref[l], (((0,), (0,)), ((), ())), precision=HI, preferred_element_type=F32)

    return pl.pallas_call(body, name="ada_w_grad", out_shape=_sds(2, D_MODEL, dmod_cols.shape[2]),
                          compiler_params=pltpu.CompilerParams(vmem_limit_bytes=VMEM_LIMIT_BYTES))(c_all, dmod_cols)


_SMALL = ("ada_b", "pre_g", "post_g", "rel_bias", "s5_a_re", "s5_a_im", "s5_log_dt", "s5_b_re", "s5_b_im", "s5_c_re", "s5_c_im",
          "s5_d", "s5_glu_b", "gdn_a_log", "gdn_dt_bias", "gdn_norm_g")
_SHARDED = ("ab_w_in", "ab_w_out", "s5_glu_w", "gdn_w_in", "gdn_w_out")
_COL_SHARDED = ("ab_w_in", "gdn_w_in")
_WEIGHTS = ("ada_w", "ada_b", "pre_g", "post_g", "rel_bias", "ab_w_in", "ab_w_out", "s5_a_re", "s5_a_im", "s5_log_dt", "s5_b_re",
            "s5_b_im", "s5_c_re", "s5_c_im", "s5_d", "s5_glu_w", "s5_glu_b", "gdn_w_in", "gdn_conv", "gdn_a_log", "gdn_dt_bias",
            "gdn_norm_g", "gdn_w_out")


def _rows128(n):
    return -(-n // 128)


def _pack(arrs, total_rows):
    pieces = []
    for a in arrs:
        flat = a.reshape(-1)
        pieces.append(jnp.pad(flat, (0, _rows128(flat.shape[0]) * 128 - flat.shape[0])).reshape(-1, 128))
    used = sum(p.shape[0] for p in pieces)
    pieces.append(jnp.zeros((total_rows - used, 128), F32))
    return jnp.concatenate(pieces, axis=0)


def _unpack(buf, shapes):
    out, at = [], 0
    for shp in shapes:
        n = int(np.prod(shp))
        out.append(buf[at:at + _rows128(n)].reshape(-1)[:n].reshape(shp))
        at += _rows128(n)
    return out


def _full_from_halves(name, g):
    if name in _COL_SHARDED:
        return g.transpose(0, 2, 1, 3).reshape(2 * g.shape[2], 4 * g.shape[3])
    return g.transpose(1, 0, 2, 3).reshape(8 * g.shape[2], g.shape[3])


def _shard_major(name, g):
    if name in _COL_SHARDED:
        return g.reshape(g.shape[0], 4, g.shape[1] // 4).transpose(1, 0, 2)
    return g.reshape(4, g.shape[0] // 4, g.shape[1])


_LATE = ("ab_w_out", "s5_glu_w", "gdn_w_in", "gdn_w_out")


class _WeightExchanges:
    def __init__(self, shards, core, chip):
        self.core, self.chip = core, chip
        self.half = {}
        for name, shard in shards.items():
            h = shard.shape[0] // 2
            self.half[name] = lax.dynamic_slice_in_dim(shard.astype(BF), core * h, h, axis=0)
        self.partial, self.received = {}, {}

    def _full(self, label, names, from_chips):
        mine = [_own_slot(g, self.half[n], self.chip) for n, g in zip(names, from_chips)]
        theirs = _sibling_exchange("gather_w_sibling_" + label, mine)
        return {n: _full_from_halves(n, jnp.where(self.core == 0, jnp.stack([a, b], 0), jnp.stack([b, a], 0)))
                for n, a, b in zip(names, mine, theirs)}

    def first_weights(self):
        return self._full("first", ["ab_w_in"], _chip_exchange("gather_w_chips", [self.half["ab_w_in"]], False))

    def late_exchanges(self):
        rows = self.half["gdn_w_in"].shape[0] // 2
        pieces = [self.half["gdn_w_in"][:rows], self.half["gdn_w_in"][rows:]]
        return [([self.half["ab_w_out"], self.half["s5_glu_w"]], False), ([self.half["gdn_w_out"]], False),
                ([pieces[0]], False), ([pieces[1]], False)]

    def late_weights(self, got):
        from_chips = {"ab_w_out": got[0][0], "s5_glu_w": got[0][1], "gdn_w_out": got[1][0],
                      "gdn_w_in": jnp.concatenate([got[2][0], got[3][0]], axis=1)}
        return self._full("late", list(_LATE), [from_chips[n] for n in _LATE])

    def chip_partials(self, label, grads):
        mine, other = [], []
        for name, g in grads.items():
            sm = _shard_major(name, g)
            h = sm.shape[1] // 2
            mine.append(lax.dynamic_slice_in_dim(sm, self.core * h, h, axis=1))
            other.append(lax.dynamic_slice_in_dim(sm, (1 - self.core) * h, h, axis=1))
        out = []
        for name, a, b in zip(grads, mine, _sibling_exchange("reduce_sibling_" + label, other)):
            flat = lambda t: t.reshape(-1, t.shape[-1])
            self.partial[name] = _pair_sum("sum_sibling_" + name, flat(a), flat(b), BF).reshape(a.shape)
            out.append(self.partial[name])
        return out


def kernel(x, c, ada_w, ada_b, pre_g, post_g, rel_bias, ab_w_in, ab_w_out, s5_a_re, s5_a_im, s5_log_dt, s5_b_re, s5_b_im, s5_c_re, s5_c_im, s5_d, s5_glu_w, s5_glu_b, gdn_w_in, gdn_conv, gdn_a_log, gdn_dt_bias, gdn_norm_g, gdn_w_out, loss_target, m_ada_w, m_ada_b, m_pre_g, m_post_g, m_rel_bias, m_ab_w_in, m_ab_w_out, m_s5_a_re, m_s5_a_im, m_s5_log_dt, m_s5_b_re, m_s5_b_im, m_s5_c_re, m_s5_c_im, m_s5_d, m_s5_glu_w, m_s5_glu_b, m_gdn_w_in, m_gdn_conv, m_gdn_a_log, m_gdn_dt_bias, m_gdn_norm_g, m_gdn_w_out, v_ada_w, v_ada_b, v_pre_g, v_post_g, v_rel_bias, v_ab_w_in, v_ab_w_out, v_s5_a_re, v_s5_a_im, v_s5_log_dt, v_s5_b_re, v_s5_b_im, v_s5_c_re, v_s5_c_im, v_s5_d, v_s5_glu_w, v_s5_glu_b, v_gdn_w_in, v_gdn_conv, v_gdn_a_log, v_gdn_dt_bias, v_gdn_norm_g, v_gdn_w_out):
    w = dict(ada_w=ada_w, ada_b=ada_b, pre_g=pre_g, post_g=post_g, rel_bias=rel_bias, ab_w_in=ab_w_in, ab_w_out=ab_w_out,
             s5_a_re=s5_a_re, s5_a_im=s5_a_im, s5_log_dt=s5_log_dt, s5_b_re=s5_b_re, s5_b_im=s5_b_im, s5_c_re=s5_c_re, s5_c_im=s5_c_im,
             s5_d=s5_d, s5_glu_w=s5_glu_w, s5_glu_b=s5_glu_b, gdn_w_in=gdn_w_in, gdn_conv=gdn_conv, gdn_a_log=gdn_a_log,
             gdn_dt_bias=gdn_dt_bias, gdn_norm_g=gdn_norm_g, gdn_w_out=gdn_w_out)
    m = dict(ada_w=m_ada_w, ada_b=m_ada_b, pre_g=m_pre_g, post_g=m_post_g, rel_bias=m_rel_bias, ab_w_in=m_ab_w_in, ab_w_out=m_ab_w_out,
             s5_a_re=m_s5_a_re, s5_a_im=m_s5_a_im, s5_log_dt=m_s5_log_dt, s5_b_re=m_s5_b_re, s5_b_im=m_s5_b_im, s5_c_re=m_s5_c_re,
             s5_c_im=m_s5_c_im, s5_d=m_s5_d, s5_glu_w=m_s5_glu_w, s5_glu_b=m_s5_glu_b, gdn_w_in=m_gdn_w_in, gdn_conv=m_gdn_conv,
             gdn_a_log=m_gdn_a_log, gdn_dt_bias=m_gdn_dt_bias, gdn_norm_g=m_gdn_norm_g, gdn_w_out=m_gdn_w_out)
    v = dict(ada_w=v_ada_w, ada_b=v_ada_b, pre_g=v_pre_g, post_g=v_post_g, rel_bias=v_rel_bias, ab_w_in=v_ab_w_in, ab_w_out=v_ab_w_out,
             s5_a_re=v_s5_a_re, s5_a_im=v_s5_a_im, s5_log_dt=v_s5_log_dt, s5_b_re=v_s5_b_re, s5_b_im=v_s5_b_im, s5_c_re=v_s5_c_re,
             s5_c_im=v_s5_c_im, s5_d=v_s5_d, s5_glu_w=v_s5_glu_w, s5_glu_b=v_s5_glu_b, gdn_w_in=v_gdn_w_in, gdn_conv=v_gdn_conv,
             gdn_a_log=v_gdn_a_log, gdn_dt_bias=v_gdn_dt_bias, gdn_norm_g=v_gdn_norm_g, gdn_w_out=v_gdn_w_out)
    ix, iy, ic = _place()
    me = 4 * ix + 2 * iy + ic
    chip = 2 * ix + iy
    n_cols = ada_w.shape[2]

    mine_first = _pack([c, gdn_conv], 32)
    first = _own_slot(_all_gather8("gather_c_conv", mine_first), mine_first, me)
    c_all = first[:, 0:8].reshape(8, D_MODEL)
    conv_full = first[0::2, 8:32].reshape(4, C_CONV, n_cols).transpose(1, 0, 2).reshape(C_CONV, 4 * n_cols)
    mine_mod = _mod_local(c_all, ada_w)
    modl = _own_slot(_all_gather8("gather_mod", mine_mod), mine_mod, me)
    mod = lax.dynamic_index_in_dim(modl[0::2], me, axis=2, keepdims=False)
    mod = mod.transpose(1, 0, 2).reshape(2, 4 * n_cols) + ada_b

    comm = _WeightExchanges({name: w[name][0] for name in _SHARDED}, ic, chip)
    wd = {name: w[name] for name in _SMALL if name != "ada_b"}
    wd = {k: (a if k in ("pre_g", "post_g", "rel_bias") else a[0]) for k, a in wd.items()}
    wd["gdn_conv"] = conv_full
    wd.update(comm.first_weights())

    loss_local, grad_x, grads, dmod = _local_step(x[0], loss_target[0], mod, wd, comm)
    loss = lax.psum(loss_local, ("x", "y", "c"))

    small_shapes = [w[name].shape for name in _SMALL] + [(C_CONV, 4 * n_cols)]
    small_rows = -(-sum(_rows128(int(np.prod(s))) for s in small_shapes) // 64) * 64
    per_dev, dmod_rows = small_rows // 8, _rows128(2 * 3 * D_MODEL)
    partial = _pack([dmod] + [grads[name] for name in _SMALL[1:]] + [grads["gdn_conv"]], small_rows)
    outbound = jnp.concatenate([partial.reshape(8, per_dev, 128), jnp.broadcast_to(partial[None, :dmod_rows], (8, dmod_rows, 128))], axis=1)
    inbound = _own_slot(_all_to_all8("reduce_small_grads", outbound), lax.dynamic_index_in_dim(outbound, me, 0, keepdims=False), me)
    my_rows = _slot_sum("sum_small_grads", inbound[:, :per_dev])
    g_small = _own_slot(_all_gather8("gather_small_grads", my_rows), my_rows, me).reshape(small_rows, 128)
    g_list = _unpack(g_small, small_shapes)
    out_g, out_d, out_m, out_v = {}, {}, {}, {}

    def update(name, g2d):
        shp = w[name].shape
        two_d = lambda a: a.reshape(-1, shp[-1])
        d_, m_, v_ = _adamw("adamw_" + name, two_d(w[name]), g2d, two_d(m[name]), two_d(v[name]))
        out_g[name], out_d[name], out_m[name], out_v[name] = (a.reshape(shp) for a in (g2d, d_, m_, v_))

    for name, g in zip(_SMALL, g_list[:-1]):
        update(name, g.reshape(-1, g.shape[-1]))
    update("gdn_conv", lax.dynamic_slice_in_dim(g_list[-1], chip * n_cols, n_cols, axis=1))

    dmod_all = inbound[:, per_dev:].reshape(8, 2, 4, n_cols)
    dmod_cols = lax.dynamic_index_in_dim(dmod_all, chip, axis=2, keepdims=False).transpose(1, 0, 2)
    update("ada_w", _ada_w_grad(c_all, dmod_cols).reshape(-1, n_cols))

    comm.received["ab_w_in"] = _chip_exchange("reduce_chips", comm.chip_partials("l0_in", {"ab_w_in": grads["ab_w_in"]}), True)[0]
    chip_1 = jnp.reshape(chip, (1,)).astype(jnp.int32)
    core_1 = jnp.reshape(ic, (1,)).astype(jnp.int32)
    reduced = [_chip_sum("sum_chips_" + name, comm.received[name], comm.partial[name], chip_1) for name in _SHARDED]
    for name, g_mine, g_sib in zip(_SHARDED, reduced, _sibling_exchange("reduce_share", reduced)):
        out_g[name], out_d[name], out_m[name], out_v[name] = _adamw_halves(
            "adamw_" + name, w[name], g_mine, g_sib, m[name], v[name], core_1)

    return (loss, grad_x[None], *[out_g[n] for n in _WEIGHTS], *[out_d[n] for n in _WEIGHTS],
            *[out_m[n] for n in _WEIGHTS], *[out_v[n] for n in _WEIGHTS])
```

```python
import functools
import math

import numpy as np
import jax
import jax.numpy as jnp
from jax import lax
from jax.experimental import pallas as pl
from jax.experimental.pallas import tpu as pltpu

F32 = jnp.float32
BF = jnp.bfloat16
HI = lax.Precision.HIGHEST
MESH = pl.DeviceIdType.MESH

D_MODEL = 1024
EPS = 1e-6
A_HEADS, A_HD, A_WIDTH, A_BLOCK = 8, 64, 512, 128
DILATIONS = (1, 4, 16)
N_KEYS = 128
REL_BUCKETS, REL_MAX_DIST = 32, 2048
B_WIDTH, B_GROUP, B_GROUPS, B_STATE = 512, 16, 32, 64
S5_LANES = 512
S5_TILES = 4
S5_T = 256
C_HEADS, C_DK, C_CHUNK, C_CONV = 8, 128, 64, 4
QKV = 3072
C_IN_PAD = 4224
TM = 256
VMEM_LIMIT_BYTES = 56 * 1024 * 1024
ADAM_LR, ADAM_B1, ADAM_B2, ADAM_EPS, ADAM_WD, ADAM_STEP = 0.001, 0.9, 0.999, 1e-08, 0.01, 10
NEG = float(np.finfo(np.float32).min)


def _cp(*sem):
    return pltpu.CompilerParams(dimension_semantics=sem, vmem_limit_bytes=VMEM_LIMIT_BYTES)


def _bdot(a, b):
    return jnp.dot(a.astype(BF), b.astype(BF), preferred_element_type=F32)


def _bdot_nt(a, b):
    return lax.dot_general(a.astype(BF), b.astype(BF), (((1,), (1,)), ((), ())), preferred_element_type=F32)


def _bdot_tn(a, b):
    return lax.dot_general(a.astype(BF), b.astype(BF), (((0,), (0,)), ((), ())), preferred_element_type=F32)


def _hdot(a, b):
    return jnp.dot(a, b, precision=HI, preferred_element_type=F32)


def _bein(eq, a, b):
    return jnp.einsum(eq, a.astype(BF), b.astype(BF), preferred_element_type=F32)


def _hein(eq, a, b):
    return jnp.einsum(eq, a, b, precision=HI, preferred_element_type=F32)


def _row(tm, n):
    return pl.BlockSpec((tm, n), lambda i: (i, 0))


def _fix(shape):
    return pl.BlockSpec(shape, lambda i: (0,) * len(shape))


def _sds(*shape, dtype=F32):
    return jax.ShapeDtypeStruct(shape, dtype)


def _acc(ref, val):
    ref[...] += val


def _zero_at_first(refs, axis=0):
    @pl.when(pl.program_id(axis) == 0)
    def _():
        for r in refs:
            r[...] = jnp.zeros_like(r)


def _rms(x):
    return x * lax.rsqrt(jnp.mean(x * x, axis=-1, keepdims=True) + EPS)


def _pre_mod(x, g, scale, shift):
    return (_rms(x) * g) * (1.0 + scale) + shift


def _post_res(y, x, post_g, gate):
    return x + gate * (_rms(y) * post_g)


def _merge_gate(o1, o2, o3, l1, l2, l3, ga):
    m = jnp.maximum(jnp.maximum(l1, l2), l3)
    e1, e2, e3 = jnp.exp(l1 - m), jnp.exp(l2 - m), jnp.exp(l3 - m)
    inv = 1.0 / (e1 + e2 + e3)
    return ((e1 * inv) * o1 + (e2 * inv) * o2 + (e3 * inv) * o3) * jax.nn.silu(ga)


def _s5_gelu(ypre, u, d_skip):
    return jax.nn.gelu(ypre + d_skip * u)


def _s5_glu(yb, gl, gb):
    return yb * jax.nn.sigmoid(gl) * jax.nn.silu(gb)


def _l0_front(x, pre_g, scale, shift, w_in):
    s_len = x.shape[0]

    def body(x_ref, g_ref, sc_ref, sh_ref, w_ref, *out_refs):
        qkv_refs, (u_ref, ga_ref, gb_ref, h_ref) = out_refs[:9], out_refs[9:]
        hb = _pre_mod(x_ref[...], g_ref[...], sc_ref[...], sh_ref[...]).astype(BF)
        h_ref[...] = hb
        z = jnp.dot(hb, w_ref[...], preferred_element_type=F32)
        for a in range(3):
            piece = z[:, a * 512:(a + 1) * 512]
            for i, d in enumerate(DILATIONS):
                qkv_refs[3 * a + i][...] = _to_res(piece, d).astype(BF)
        u_ref[...] = z[:, 1536:2048]
        ga_ref[...] = z[:, 2048:2560]
        gb_ref[...] = z[:, 2560:3072]

    vec = _fix((1, D_MODEL))
    return pl.pallas_call(
        body, name="l0_front", grid=(s_len // TM,),
        in_specs=[_row(TM, D_MODEL), vec, vec, vec, _fix((D_MODEL, 3072))],
        out_specs=[_res_spec(d) for d in DILATIONS] * 3 + [_row(TM, 512)] * 3 + [_row(TM, D_MODEL)],
        out_shape=[_sds(*_res_shape(s_len, d), dtype=BF) for d in DILATIONS] * 3 + [_sds(s_len, 512)] * 3 + [_sds(s_len, D_MODEL, dtype=BF)],
        compiler_params=_cp("arbitrary"),
    )(x, pre_g, scale, shift, w_in)


def _front_bwd(name, x, pre_g, scale, shift, w_in, dres, parts, widths):
    s_len = x.shape[0]
    n_in = sum(len(p) for p in parts)
    n_cols = sum(widths)

    def body(*refs):
        x_ref, g_ref, sc_ref, sh_ref, w_ref, dres_ref = refs[:6]
        part_refs = refs[6:6 + n_in]
        dz_ref, dx_ref, dg_ref, dsc_ref, dsh_ref = refs[6 + n_in:]
        _zero_at_first([dg_ref, dsc_ref, dsh_ref])
        _, vjp = jax.vjp(_pre_mod, x_ref[...], g_ref[...], sc_ref[...], sh_ref[...])
        dh = jnp.zeros((TM, D_MODEL), F32)
        col, at = 0, 0
        for grp, width in zip(parts, widths):
            tile = lambda r: _from_res(r[...]) if len(r.shape) == 3 else r[...]
            dz = tile(part_refs[at])
            for r in part_refs[at + 1:at + len(grp)]:
                dz = dz + tile(r)
            at += len(grp)
            dzb = dz.astype(BF)
            dz_ref[:, col:col + width] = dzb
            dh = dh + lax.dot_general(dzb, w_ref[:, col:col + width], (((1,), (1,)), ((), ())), preferred_element_type=F32)
            col += width
        dx, dg, dsc, dsh = vjp(dh)
        dx_ref[...] = dx + dres_ref[...]
        _acc(dg_ref, dg)
        _acc(dsc_ref, dsc)
        _acc(dsh_ref, dsh)

    vec = _fix((1, D_MODEL))
    flat = [a for p in parts for a in p]
    return pl.pallas_call(
        body, name=name, grid=(s_len // TM,),
        in_specs=[_row(TM, D_MODEL), vec, vec, vec, _fix((D_MODEL, n_cols)), _row(TM, D_MODEL)]
        + [_res_spec(a.shape[0], a.shape[2]) if a.ndim == 3 else _row(TM, a.shape[1]) for a in flat],
        out_specs=[_row(TM, n_cols), _row(TM, D_MODEL), vec, vec, vec],
        out_shape=[_sds(s_len, n_cols, dtype=BF), _sds(s_len, D_MODEL), _sds(1, D_MODEL), _sds(1, D_MODEL), _sds(1, D_MODEL)],
        compiler_params=_cp("arbitrary"),
    )(x, pre_g, scale, shift, w_in, dres, *flat)


def _matmul_tn(name, a, b, tn):
    s_len, k_dim = a.shape
    n_dim = b.shape[1]
    ts = 512

    def body(a_ref, b_ref, o_ref):
        _zero_at_first([o_ref], axis=1)
        o_ref[...] += lax.dot_general(a_ref[...], b_ref[...], (((0,), (0,)), ((), ())), preferred_element_type=F32)

    return pl.pallas_call(
        body, name=name, grid=(n_dim // tn, s_len // ts),
        in_specs=[pl.BlockSpec((ts, k_dim), lambda j, i: (i, 0)), pl.BlockSpec((ts, tn), lambda j, i: (i, j))],
        out_specs=pl.BlockSpec((k_dim, tn), lambda j, i: (0, j)),
        out_shape=_sds(k_dim, n_dim),
        compiler_params=_cp("arbitrary", "arbitrary"),
    )(a, b)


def _t5_bucket_np(dist):
    dist = np.maximum(dist, 0)
    max_exact = REL_BUCKETS // 2
    large = max_exact + (np.log(np.maximum(dist, 1) / max_exact)
                         / math.log(REL_MAX_DIST / max_exact) * (REL_BUCKETS - max_exact)).astype(np.int32)
    large = np.minimum(large, REL_BUCKETS - 1)
    return np.where(dist < max_exact, dist, large).astype(np.int32)


def _to_res(z, dil):
    if dil == 1:
        return z[None]
    return jnp.swapaxes(z.reshape(z.shape[0] // dil, dil, z.shape[1]), 0, 1)


def _from_res(z):
    if z.shape[0] == 1:
        return z[0]
    return jnp.swapaxes(z, 0, 1).reshape(z.shape[0] * z.shape[1], z.shape[2])


def _res_shape(s_len, dil, width=A_WIDTH):
    return (dil, s_len // dil, width)


def _res_spec(dil, width=A_WIDTH):
    return pl.BlockSpec((dil, TM // dil, width), lambda i: (0, i, 0))


def _bucket_table():
    qi = np.arange(A_BLOCK)[:, None]
    kj = np.arange(2 * A_BLOCK)[None, :]
    return np.stack([_t5_bucket_np((qi + A_BLOCK - kj) * d) for d in DILATIONS], 0)


def _attn_mask(first):
    qi = lax.broadcasted_iota(jnp.int32, (A_BLOCK, 2 * A_BLOCK), 0)
    kj = lax.broadcasted_iota(jnp.int32, (A_BLOCK, 2 * A_BLOCK), 1)
    rel = qi + A_BLOCK - kj
    return (rel >= 0) & (rel <= N_KEYS) & (jnp.logical_not(first) | (kj >= A_BLOCK))


def _attn_specs(nb, rev):
    n_of = (lambda i: nb - 1 - i) if rev else (lambda i: i)
    cur = pl.BlockSpec((None, A_BLOCK, A_WIDTH), lambda r, i: (r, n_of(i), 0))
    prev = pl.BlockSpec((None, A_BLOCK, A_WIDTH), lambda r, i: (r, jnp.maximum(n_of(i) - 1, 0), 0))
    bias = pl.BlockSpec((A_HEADS, A_BLOCK, 2 * A_BLOCK), lambda r, i: (0, 0, 0))
    return cur, prev, bias


def _attn_fwd(q, k, v, bias, exchange=None):
    dil, t_len, _ = q.shape
    nb = t_len // A_BLOCK
    scale = A_HD ** -0.5

    def body(q_ref, kp_ref, kc_ref, vp_ref, vc_ref, b_ref, o_ref, l_ref):
        mask = _attn_mask(pl.program_id(1) == 0)
        lane = lax.broadcasted_iota(jnp.int32, (1, 128), 1)
        for hp in range(A_HEADS // 2):
            sl = slice(hp * 128, (hp + 1) * 128)
            qp = q_ref[:, sl]
            kw = jnp.concatenate([kp_ref[:, sl], kc_ref[:, sl]], axis=0).astype(BF)
            vw = jnp.concatenate([vp_ref[:, sl], vc_ref[:, sl]], axis=0).astype(BF)
            outs, lses = [], []
            for j in range(2):
                hm = (lane < 64) if j == 0 else (lane >= 64)
                s = _bdot_nt(jnp.where(hm, qp, 0.0), kw) * scale
                s = jnp.where(mask, s + b_ref[2 * hp + j], NEG)
                m = jnp.max(s, axis=-1, keepdims=True)
                p = jnp.exp(s - m)
                den = jnp.sum(p, axis=-1, keepdims=True)
                outs.append(_bdot(p, vw) / den)
                lses.append(m + jnp.log(den))
            hm0 = lane < 64
            o_ref[:, sl] = jnp.where(hm0, outs[0], outs[1])
            l_ref[:, sl] = jnp.where(hm0, lses[0], lses[1])

    cur, prev, bias_spec = _attn_specs(nb, False)
    return _call_with_exchange(body, f"attn_fwd_d{dil}", (dil, nb), [cur, prev, cur, prev, cur, bias_spec], [cur, cur],
                               [_sds(dil, t_len, A_WIDTH)] * 2, [], (q, k, k, v, v, bias), exchange)


def _attn_bwd(q, k, v, bias, o, l, do, dl):
    dil, t_len, _ = q.shape
    nb = t_len // A_BLOCK
    scale = A_HD ** -0.5

    def body(q_ref, kp_ref, kc_ref, vp_ref, vc_ref, b_ref, o_ref, l_ref, do_ref, dl_ref,
             dq_ref, dk_ref, dv_ref, db_ref, ck_ref, cv_ref):
        _zero_at_first([ck_ref, cv_ref], axis=1)

        @pl.when((pl.program_id(0) == 0) & (pl.program_id(1) == 0))
        def _():
            db_ref[...] = jnp.zeros_like(db_ref)

        mask = _attn_mask(pl.program_id(1) == nb - 1)
        lane = lax.broadcasted_iota(jnp.int32, (1, 128), 1)
        for hp in range(A_HEADS // 2):
            sl = slice(hp * 128, (hp + 1) * 128)
            qp = q_ref[:, sl]
            kw = jnp.concatenate([kp_ref[:, sl], kc_ref[:, sl]], axis=0).astype(BF)
            vw = jnp.concatenate([vp_ref[:, sl], vc_ref[:, sl]], axis=0).astype(BF)
            op, lp, dop, dlp = o_ref[:, sl], l_ref[:, sl], do_ref[:, sl], dl_ref[:, sl]
            dq_acc = jnp.zeros((A_BLOCK, 128), F32)
            dk_acc = jnp.zeros((2 * A_BLOCK, 128), F32)
            dv_acc = jnp.zeros((2 * A_BLOCK, 128), F32)
            for j in range(2):
                hm = (lane < 64) if j == 0 else (lane >= 64)
                qm = jnp.where(hm, qp, 0.0)
                s = _bdot_nt(qm, kw) * scale
                s = jnp.where(mask, s + b_ref[2 * hp + j], NEG)
                lse = jnp.max(jnp.where(hm, lp, NEG), axis=-1, keepdims=True)
                p = jnp.exp(s - lse)
                do_h = jnp.where(hm, dop, 0.0)
                dd = jnp.sum(do_h * op, axis=-1, keepdims=True)
                dlse = jnp.sum(jnp.where(hm, dlp, 0.0), axis=-1, keepdims=True)
                ds = p * (_bdot_nt(do_h, vw) - dd + dlse)
                dv_acc = dv_acc + _bdot_tn(p, do_h)
                dq_acc = dq_acc + jnp.where(hm, _bdot(ds, kw), 0.0) * scale
                dk_acc = dk_acc + _bdot_tn(ds, qm) * scale
                db_ref[2 * hp + j] += ds
            dq_ref[:, sl] = dq_acc
            dk_ref[:, sl] = dk_acc[A_BLOCK:] + ck_ref[:, sl]
            dv_ref[:, sl] = dv_acc[A_BLOCK:] + cv_ref[:, sl]
            ck_ref[:, sl] = dk_acc[:A_BLOCK]
            cv_ref[:, sl] = dv_acc[:A_BLOCK]

    cur, prev, bias_spec = _attn_specs(nb, True)
    return pl.pallas_call(
        body, name=f"attn_bwd_d{dil}", grid=(dil, nb),
        in_specs=[cur, prev, cur, prev, cur, bias_spec, cur, cur, cur, cur],
        out_specs=[cur, cur, cur, bias_spec],
        out_shape=[_sds(dil, t_len, A_WIDTH)] * 3 + [_sds(A_HEADS, A_BLOCK, 2 * A_BLOCK)],
        scratch_shapes=[pltpu.VMEM((A_BLOCK, A_WIDTH), F32)] * 2,
        compiler_params=_cp("arbitrary", "arbitrary"),
    )(q, k, k, v, v, bias, o, l, do, dl)


def _attn_bias(rel_bias, table):
    def body(rb_ref, t_ref, *o_refs):
        for c in range(3):
            t = t_ref[c]
            acc = [jnp.zeros((A_BLOCK, 2 * A_BLOCK), F32) for _ in range(A_HEADS)]
            for b in range(REL_BUCKETS):
                hit = t == b
                acc = [jnp.where(hit, rb_ref[b, h], acc[h]) for h in range(A_HEADS)]
            for h in range(A_HEADS):
                o_refs[c][h] = acc[h]

    return pl.pallas_call(body, name="attn_bias", out_shape=[_sds(A_HEADS, A_BLOCK, 2 * A_BLOCK)] * 3,
                          in_specs=[pl.BlockSpec(memory_space=pltpu.SMEM), pl.BlockSpec(memory_space=pltpu.VMEM)],
                          compiler_params=pltpu.CompilerParams(vmem_limit_bytes=VMEM_LIMIT_BYTES))(rel_bias, table)


def _rel_bias_grad(dbs, idx_rows):
    n = A_BLOCK * 2 * A_BLOCK

    def body(d0_ref, d1_ref, d2_ref, idx_ref, o_ref):
        bucket = lax.broadcasted_iota(jnp.int32, (REL_BUCKETS, n), 0).astype(F32)
        acc = jnp.zeros((A_HEADS, REL_BUCKETS), F32)
        for c, db_ref in enumerate((d0_ref, d1_ref, d2_ref)):
            onehot = (idx_ref[c:c + 1, :] == bucket).astype(F32)
            acc = acc + lax.dot_general(db_ref[...], onehot, (((1,), (1,)), ((), ())), precision=HI, preferred_element_type=F32)
        o_ref[...] = acc

    return pl.pallas_call(body, name="rel_bias_grad", out_shape=_sds(A_HEADS, REL_BUCKETS),
                          compiler_params=pltpu.CompilerParams(vmem_limit_bytes=VMEM_LIMIT_BYTES))(
                              *[d.reshape(A_HEADS, n) for d in dbs], idx_rows)


def _s5_param_fn(a_re, a_im, log_dt, bt_re, bt_im):
    dt = jnp.exp(log_dt)
    mag = jnp.exp(dt * a_re)
    abar_r, abar_i = mag * jnp.cos(dt * a_im), mag * jnp.sin(dt * a_im)
    den = a_re * a_re + a_im * a_im
    fr = ((abar_r - 1.0) * a_re + abar_i * a_im) / den
    fi = (abar_i * a_re - (abar_r - 1.0) * a_im) / den
    row = lax.broadcasted_iota(jnp.int32, (B_WIDTH, B_GROUPS), 0)
    grp = lax.broadcasted_iota(jnp.int32, (B_WIDTH, B_GROUPS), 1)
    expand = ((row // B_GROUP) == grp).astype(F32)
    fr_e, fi_e = _hdot(expand, fr), _hdot(expand, fi)
    return abar_r, abar_i, fr_e * bt_re - fi_e * bt_im, fr_e * bt_im + fi_e * bt_re


def _s5_params(a_re, a_im, log_dt, bt_re, bt_im):
    def body(ar, ai, ld, br, bi, o1, o2, o3, o4):
        o1[...], o2[...], o3[...], o4[...] = _s5_param_fn(ar[...], ai[...], ld[...], br[...], bi[...])

    return pl.pallas_call(body, name="s5_params",
                          out_shape=[_sds(B_GROUPS, B_STATE)] * 2 + [_sds(B_WIDTH, B_STATE)] * 2)(a_re, a_im, log_dt, bt_re, bt_im)


def _s5_params_bwd(a_re, a_im, log_dt, bt_re, bt_im, d1, d2, d3, d4):
    def body(ar, ai, ld, br, bi, c1, c2, c3, c4, o1, o2, o3, o4, o5):
        _, vjp = jax.vjp(_s5_param_fn, ar[...], ai[...], ld[...], br[...], bi[...])
        o1[...], o2[...], o3[...], o4[...], o5[...] = vjp((c1[...], c2[...], c3[...], c4[...]))

    return pl.pallas_call(body, name="s5_params_bwd",
                          out_shape=[_sds(B_GROUPS, B_STATE)] * 2 + [_sds(B_GROUPS, 1)] + [_sds(B_WIDTH, B_STATE)] * 2,
                          )(a_re, a_im, log_dt, bt_re, bt_im, d1, d2, d3, d4)


S5_SUB = 8
S5_GROUPS = S5_T // S5_SUB


def _dscan(xr, xi, cr, ci, period, reverse):
    n = xr.shape[0]
    rows = lax.broadcasted_iota(jnp.int32, xr.shape, 0)
    pos = rows % period
    k = 1
    while k < period:
        if reverse:
            keep, shift = pos < period - k, n - k
        else:
            keep, shift = pos >= k, k
        sr = jnp.where(keep, pltpu.roll(xr, shift, 0), 0.0)
        si = jnp.where(keep, pltpu.roll(xi, shift, 0), 0.0)
        xr, xi = xr + cr * sr - ci * si, xi + cr * si + ci * sr
        cr, ci = cr * cr - ci * ci, 2.0 * cr * ci
        k *= 2
    return xr, xi, cr, ci


def _pick_row(x, r):
    rows = lax.broadcasted_iota(jnp.int32, x.shape, 0)
    return jnp.sum(jnp.where(rows == r, x, 0.0), axis=0, keepdims=True)


def _scan_tables(ar, ai, reverse):
    rows = lax.broadcasted_iota(jnp.int32, (S5_SUB, S5_LANES), 0)
    at = rows == (S5_SUB - 1 if reverse else 0)
    p8r, p8i, a8r, a8i = _dscan(jnp.where(at, ar, 0.0), jnp.where(at, ai, 0.0), ar, ai, S5_SUB, reverse)
    grp = lax.broadcasted_iota(jnp.int32, (S5_GROUPS, S5_LANES), 0)
    at = grp == (S5_GROUPS - 1 if reverse else 0)
    pgr, pgi, _, _ = _dscan(jnp.where(at, a8r, 0.0), jnp.where(at, a8i, 0.0), a8r, a8i, S5_GROUPS, reverse)
    return p8r, p8i, pgr, pgi


def _block_scan(br, bi, ar, ai, carry, tables, reverse, work):
    p8r, p8i, pgr, pgi = tables
    cin_r, cin_i = carry
    xw_r, xw_i, ew_r, ew_i = work
    xr, xi, a8r, a8i = _dscan(br, bi, ar, ai, S5_SUB, reverse)
    xw_r[...] = xr
    xw_i[...] = xi
    end = 0 if reverse else S5_SUB - 1
    for g in range(S5_GROUPS):
        ew_r[g:g + 1, :] = xw_r[S5_SUB * g + end:S5_SUB * g + end + 1, :]
        ew_i[g:g + 1, :] = xw_i[S5_SUB * g + end:S5_SUB * g + end + 1, :]
    er, ei, _, _ = _dscan(ew_r[...], ew_i[...], a8r, a8i, S5_GROUPS, reverse)
    er, ei = er + pgr * cin_r - pgi * cin_i, ei + pgr * cin_i + pgi * cin_r
    grp = lax.broadcasted_iota(jnp.int32, (S5_GROUPS, S5_LANES), 0)
    if reverse:
        ew_r[...] = jnp.where(grp == S5_GROUPS - 1, cin_r, pltpu.roll(er, S5_GROUPS - 1, 0))
        ew_i[...] = jnp.where(grp == S5_GROUPS - 1, cin_i, pltpu.roll(ei, S5_GROUPS - 1, 0))
    else:
        ew_r[...] = jnp.where(grp == 0, cin_r, pltpu.roll(er, 1, 0))
        ew_i[...] = jnp.where(grp == 0, cin_i, pltpu.roll(ei, 1, 0))
    for g in range(S5_GROUPS):
        rows = slice(S5_SUB * g, S5_SUB * (g + 1))
        nr, ni = ew_r[g:g + 1, :], ew_i[g:g + 1, :]
        xw_r[rows, :] += p8r * nr - p8i * ni
        xw_i[rows, :] += p8r * ni + p8i * nr
    last = 0 if reverse else S5_GROUPS - 1
    return xw_r[...], xw_i[...], (_pick_row(er, last), _pick_row(ei, last))


def _s5_tile_specs(n_t, rev):
    t_of = (lambda i: n_t - 1 - i) if rev else (lambda i: i)
    u_spec = pl.BlockSpec((S5_T, 128), lambda j, i: (t_of(i), j))
    x_spec = pl.BlockSpec((S5_T, S5_LANES), lambda j, i: (t_of(i), j))
    b_spec = pl.BlockSpec((1, 128, S5_LANES), lambda j, i: (j, 0, 0))
    c_spec = pl.BlockSpec((1, S5_LANES, 128), lambda j, i: (j, 0, 0))
    a_spec = pl.BlockSpec((1, S5_LANES), lambda j, i: (0, j))
    return u_spec, x_spec, b_spec, c_spec, a_spec


def _s5_scratch(with_rows):
    return ([pltpu.VMEM((1, S5_LANES), F32)] * 2 + [pltpu.VMEM((S5_SUB, S5_LANES), F32)] * 2
            + [pltpu.VMEM((S5_GROUPS, S5_LANES), F32)] * 4 + ([pltpu.VMEM((S5_T, S5_LANES), F32)] * 2 if with_rows else []))


def _s5_scan_fwd(u, btr, bti, ctr, cti, abr, abi, exchange=None):
    s_len = u.shape[0]
    n_t = s_len // S5_T

    def body(u_ref, btr_ref, bti_ref, ctr_ref, cti_ref, ar_ref, ai_ref, xr_ref, xi_ref, y_ref, car, cai, p8r, p8i, pgr, pgi, ew_r, ew_i):
        ar, ai = ar_ref[...], ai_ref[...]

        @pl.when(pl.program_id(1) == 0)
        def _():
            car[...] = jnp.zeros_like(car)
            cai[...] = jnp.zeros_like(cai)
            p8r[...], p8i[...], pgr[...], pgi[...] = _scan_tables(ar, ai, False)

        ub = u_ref[...]
        xr, xi, (ncr, nci) = _block_scan(_bdot(ub, btr_ref[0]), _bdot(ub, bti_ref[0]), ar, ai, (car[...], cai[...]),
                                         (p8r[...], p8i[...], pgr[...], pgi[...]), False, (xr_ref, xi_ref, ew_r, ew_i))
        car[...] = ncr
        cai[...] = nci
        y_ref[...] = _bdot(xr, ctr_ref[0]) - _bdot(xi, cti_ref[0])

    u_spec, x_spec, b_spec, c_spec, a_spec = _s5_tile_specs(n_t, False)
    return _call_with_exchange(
        body, "s5_scan_fwd", (S5_TILES, n_t),
        [u_spec, b_spec, b_spec, c_spec, c_spec, a_spec, a_spec], [x_spec, x_spec, u_spec],
        [_sds(s_len, S5_TILES * S5_LANES)] * 2 + [_sds(s_len, B_WIDTH)], _s5_scratch(False),
        (u, btr, bti, ctr, cti, abr, abi), exchange)


def _s5_scan_bwd(dy, xr, xi, u, btr, bti, ctr, cti, abr, abi, exchange=None):
    s_len = u.shape[0]
    n_t = s_len // S5_T

    def body(dy_ref, xr_ref, xi_ref, xrp_ref, xip_ref, u_ref, btr_ref, bti_ref, ctr_ref, cti_ref, ar_ref, ai_ref,
             du_ref, dbtr_ref, dbti_ref, dctr_ref, dcti_ref, dar_ref, dai_ref, car, cai, p8r, p8i, pgr, pgi, ew_r, ew_i, xw_r, xw_i):
        ar, ai = ar_ref[...], ai_ref[...]
        i = pl.program_id(1)
        rows = lax.broadcasted_iota(jnp.int32, (S5_T, S5_LANES), 0)

        @pl.when(i == 0)
        def _():
            for r in (car, cai, dbtr_ref, dbti_ref, dctr_ref, dcti_ref, dar_ref, dai_ref):
                r[...] = jnp.zeros_like(r)
            p8r[...], p8i[...], pgr[...], pgi[...] = _scan_tables(ar, -ai, True)

        dyb = dy_ref[...]
        xr_b, xi_b, ub = xr_ref[...], xi_ref[...], u_ref[...]
        dctr_ref[0] += _bdot_tn(xr_b, dyb)
        dcti_ref[0] -= _bdot_tn(xi_b, dyb)
        gr, gi, (ncr, nci) = _block_scan(_bdot_nt(dyb, ctr_ref[0]), -_bdot_nt(dyb, cti_ref[0]), ar, -ai, (car[...], cai[...]),
                                         (p8r[...], p8i[...], pgr[...], pgi[...]), True, (xw_r, xw_i, ew_r, ew_i))
        car[...] = ncr
        cai[...] = nci
        du_ref[...] = _bdot_nt(gr, btr_ref[0]) + _bdot_nt(gi, bti_ref[0])
        dbtr_ref[0] += _bdot_tn(ub, gr)
        dbti_ref[0] += _bdot_tn(ub, gi)
        has_prev = (i < n_t - 1).astype(F32)
        hr = _pick_row(xrp_ref[...], 7) * has_prev
        hi = _pick_row(xip_ref[...], 7) * has_prev
        xpr = jnp.where(rows == 0, hr, pltpu.roll(xr_b, 1, 0))
        xpi = jnp.where(rows == 0, hi, pltpu.roll(xi_b, 1, 0))
        dar_ref[...] += jnp.sum(gr * xpr + gi * xpi, axis=0, keepdims=True)
        dai_ref[...] += jnp.sum(gi * xpr - gr * xpi, axis=0, keepdims=True)

    u_spec, x_spec, b_spec, c_spec, a_spec = _s5_tile_specs(n_t, True)
    halo = pl.BlockSpec((8, S5_LANES), lambda j, i: (jnp.maximum((n_t - 1 - i) * (S5_T // 8) - 1, 0), j))
    return _call_with_exchange(
        body, "s5_scan_bwd", (S5_TILES, n_t),
        [u_spec, x_spec, x_spec, halo, halo, u_spec, b_spec, b_spec, c_spec, c_spec, a_spec, a_spec],
        [u_spec, b_spec, b_spec, c_spec, c_spec, a_spec, a_spec],
        [_sds(s_len, B_WIDTH)] + [_sds(S5_TILES, 128, S5_LANES)] * 2 + [_sds(S5_TILES, S5_LANES, 128)] * 2
        + [_sds(1, S5_TILES * S5_LANES)] * 2,
        _s5_scratch(True), (dy, xr, xi, xr, xi, u, btr, bti, ctr, cti, abr, abi), exchange)


S5_SEG = 8
S5_STEPS = 32
S5_WIDTH = S5_TILES * S5_LANES


def _seg_rows(block):
    return jnp.swapaxes(block, 0, 1).reshape(block.shape[1] * S5_SEG, block.shape[2])


def _seg_block(rows):
    return jnp.swapaxes(rows.reshape(rows.shape[0] // S5_SEG, S5_SEG, rows.shape[1]), 0, 1)


def _seq_specs(n_i, rev):
    at = (lambda i: n_i - 1 - i) if rev else (lambda i: i)
    seg = pl.BlockSpec((S5_SEG, S5_STEPS, B_WIDTH), lambda i: (0, at(i), 0))
    x_spec = pl.BlockSpec((S5_SEG * S5_STEPS, S5_WIDTH), lambda i: (at(i), 0))
    return seg, x_spec, _fix((S5_TILES, 128, S5_LANES)), _fix((S5_TILES, S5_LANES, 128)), _fix((1, S5_WIDTH)), _fix((S5_SEG, S5_WIDTH))


def _tile_dots(dot, lhs, w_ref, lhs_width):
    return jnp.concatenate([dot(lhs[:, t * lhs_width:(t + 1) * lhs_width], w_ref[t]) for t in range(S5_TILES)], axis=1)


def _s5_entries(name, end_r, end_i, abr, abi, steps, reverse):
    def body(er_ref, ei_ref, ar_ref, ai_ref, or_ref, oi_ref):
        pr, pi_ = ar_ref[...], ai_ref[...]
        for _ in range(int(math.log2(steps))):
            pr, pi_ = pr * pr - pi_ * pi_, 2.0 * pr * pi_
        er, ei = er_ref[...], ei_ref[...]
        rows = lax.broadcasted_iota(jnp.int32, er.shape, 0)
        cr, ci = jnp.zeros_like(pr), jnp.zeros_like(pr)
        out_r, out_i = jnp.zeros_like(er), jnp.zeros_like(er)
        for g in (range(S5_SEG - 2, -1, -1) if reverse else range(1, S5_SEG)):
            src = g + 1 if reverse else g - 1
            cr, ci = _pick_row(er, src) + pr * cr - pi_ * ci, _pick_row(ei, src) + pr * ci + pi_ * cr
            out_r, out_i = jnp.where(rows == g, cr, out_r), jnp.where(rows == g, ci, out_i)
        or_ref[...] = out_r
        oi_ref[...] = out_i

    return pl.pallas_call(body, name=name, out_shape=[_sds(*end_r.shape)] * 2)(end_r, end_i, abr, abi)


def _s5_seq_fwd(u, btr, bti, ctr, cti, abr, abi, entry, store, exchange=None):
    s_len = u.shape[0]
    seg_len = s_len // S5_SEG
    n_i = seg_len // S5_STEPS
    rows = S5_SEG * S5_STEPS

    def body(u_ref, btr_ref, bti_ref, ctr_ref, cti_ref, ar_ref, ai_ref, er_ref, ei_ref, *rest):
        if store:
            xr_ref, xi_ref, y_ref, endr_ref, endi_ref, sr_ref, si_ref = rest
        else:
            endr_ref, endi_ref, sr_ref, si_ref = rest
        i = pl.program_id(0)

        @pl.when(i == 0)
        def _():
            sr_ref[...] = er_ref[...]
            si_ref[...] = ei_ref[...]

        ar = jnp.broadcast_to(ar_ref[...], (S5_SEG, S5_WIDTH))
        ai = jnp.broadcast_to(ai_ref[...], (S5_SEG, S5_WIDTH))
        ub = _seg_rows(u_ref[...])
        br, bi = _tile_dots(_bdot, ub, btr_ref, 128), _tile_dots(_bdot, ub, bti_ref, 128)
        sr, si = sr_ref[...], si_ref[...]
        for s in range(S5_STEPS):
            at = slice(S5_SEG * s, S5_SEG * (s + 1))
            sr, si = ar * sr - ai * si + br[at], ar * si + ai * sr + bi[at]
            if store:
                xr_ref[at, :] = sr
                xi_ref[at, :] = si
        sr_ref[...] = sr
        si_ref[...] = si
        if store:
            y_ref[...] = _seg_block(_tile_dots(_bdot, xr_ref[...], ctr_ref, S5_LANES) - _tile_dots(_bdot, xi_ref[...], cti_ref, S5_LANES))

        @pl.when(i == n_i - 1)
        def _():
            endr_ref[...] = sr
            endi_ref[...] = si

    seg, x_spec, b_spec, c_spec, a_spec, e_spec = _seq_specs(n_i, False)
    ends = [_sds(S5_SEG, S5_WIDTH)] * 2
    full = [_sds(s_len, S5_WIDTH)] * 2 + [_sds(S5_SEG, seg_len, B_WIDTH)] if store else []
    return _call_with_exchange(
        body, "s5_scan_fwd" if store else "s5_ends_fwd", (n_i,),
        [seg, b_spec, b_spec, c_spec, c_spec, a_spec, a_spec, e_spec, e_spec],
        ([x_spec, x_spec, seg] if store else []) + [e_spec, e_spec], full + ends,
        [pltpu.VMEM((S5_SEG, S5_WIDTH), F32)] * 2,
        (u.reshape(S5_SEG, seg_len, B_WIDTH), btr, bti, ctr, cti, abr, abi, *entry), exchange)


def _s5_seq_bwd(dy, xr, xi, u, btr, bti, ctr, cti, abr, abi, g_entry, x_entry, full, exchange=None):
    s_len = dy.shape[0]
    seg_len = s_len // S5_SEG
    n_i = seg_len // S5_STEPS
    rows = S5_SEG * S5_STEPS

    def body(*refs):
        if full:
            (dy_ref, btr_ref, bti_ref, ctr_ref, cti_ref, ar_ref, ai_ref, ger_ref, gei_ref,
             xr_ref, xi_ref, xrp_ref, xip_ref, xer_ref, xei_ref, u_ref,
             du_ref, dbtr_ref, dbti_ref, dctr_ref, dcti_ref, dar_ref, dai_ref, str_ref, sti_ref,
             sr_ref, si_ref, gr_s, gi_s) = refs
        else:
            (dy_ref, btr_ref, bti_ref, ctr_ref, cti_ref, ar_ref, ai_ref, ger_ref, gei_ref, str_ref, sti_ref, sr_ref, si_ref) = refs
        i = pl.program_id(0)

        @pl.when(i == 0)
        def _():
            sr_ref[...] = ger_ref[...]
            si_ref[...] = gei_ref[...]
            if full:
                for r in (dbtr_ref, dbti_ref, dctr_ref, dcti_ref, dar_ref, dai_ref):
                    r[...] = jnp.zeros_like(r)

        ar = jnp.broadcast_to(ar_ref[...], (S5_SEG, S5_WIDTH))
        ai = -jnp.broadcast_to(ai_ref[...], (S5_SEG, S5_WIDTH))
        dyb = _seg_rows(dy_ref[...])
        gr, gi = _tile_dots(_bdot_nt, dyb, ctr_ref, 128), -_tile_dots(_bdot_nt, dyb, cti_ref, 128)
        sr, si = sr_ref[...], si_ref[...]
        for s in range(S5_STEPS - 1, -1, -1):
            at = slice(S5_SEG * s, S5_SEG * (s + 1))
            sr, si = ar * sr - ai * si + gr[at], ar * si + ai * sr + gi[at]
            if full:
                gr_s[at, :] = sr
                gi_s[at, :] = si
        sr_ref[...] = sr
        si_ref[...] = si

        @pl.when(i == n_i - 1)
        def _():
            str_ref[...] = sr
            sti_ref[...] = si

        if full:
            g_r, g_i = gr_s[...], gi_s[...]
            du_ref[...] = _seg_block(_tile_dots(_bdot_nt, g_r, btr_ref, S5_LANES) + _tile_dots(_bdot_nt, g_i, bti_ref, S5_LANES))
            ub = _seg_rows(u_ref[...])
            xr_b, xi_b = xr_ref[...], xi_ref[...]
            for t in range(S5_TILES):
                lanes, cols = slice(t * S5_LANES, (t + 1) * S5_LANES), slice(t * 128, (t + 1) * 128)
                dbtr_ref[t] += _bdot_tn(ub[:, cols], g_r[:, lanes])
                dbti_ref[t] += _bdot_tn(ub[:, cols], g_i[:, lanes])
                dctr_ref[t] += _bdot_tn(xr_b[:, lanes], dyb[:, cols])
                dcti_ref[t] -= _bdot_tn(xi_b[:, lanes], dyb[:, cols])
            first = i == n_i - 1
            xpr = jnp.concatenate([jnp.where(first, xer_ref[...], xrp_ref[...]), xr_b[:rows - S5_SEG]], axis=0)
            xpi = jnp.concatenate([jnp.where(first, xei_ref[...], xip_ref[...]), xi_b[:rows - S5_SEG]], axis=0)
            dar_ref[...] += jnp.sum(g_r * xpr + g_i * xpi, axis=0, keepdims=True)
            dai_ref[...] += jnp.sum(g_i * xpr - g_r * xpi, axis=0, keepdims=True)

    seg, x_spec, b_spec, c_spec, a_spec, e_spec = _seq_specs(n_i, True)
    halo = pl.BlockSpec((S5_SEG, S5_WIDTH), lambda i: (jnp.maximum((n_i - 1 - i) * S5_STEPS - 1, 0), 0))
    starts = [_sds(S5_SEG, S5_WIDTH)] * 2
    in_specs = [seg, b_spec, b_spec, c_spec, c_spec, a_spec, a_spec, e_spec, e_spec]
    args = [dy.reshape(S5_SEG, seg_len, B_WIDTH), btr, bti, ctr, cti, abr, abi, *g_entry]
    state = [pltpu.VMEM((S5_SEG, S5_WIDTH), F32)] * 2
    if not full:
        return _call_with_exchange(body, "s5_starts_bwd", (n_i,), in_specs, [e_spec, e_spec], starts, state, args, None)
    return _call_with_exchange(
        body, "s5_scan_bwd", (n_i,),
        in_specs + [x_spec, x_spec, halo, halo, e_spec, e_spec, seg],
        [seg, b_spec, b_spec, c_spec, c_spec, a_spec, a_spec, e_spec, e_spec],
        [_sds(S5_SEG, seg_len, B_WIDTH)] + [_sds(S5_TILES, 128, S5_LANES)] * 2 + [_sds(S5_TILES, S5_LANES, 128)] * 2
        + [_sds(1, S5_WIDTH)] * 2 + starts,
        state + [pltpu.VMEM((rows, S5_WIDTH), F32)] * 2,
        args + [xr, xi, xr, xi, *x_entry, u.reshape(S5_SEG, seg_len, B_WIDTH)], exchange)


def _blockdiag_b(bbar_t):
    blocks = bbar_t.reshape(S5_TILES, 8, B_GROUP, B_STATE)
    return jnp.einsum('jgmp,gh->jgmhp', blocks, jnp.eye(8, dtype=F32)).reshape(S5_TILES, 128, S5_LANES)


def _blockdiag_b_t(d):
    return jnp.einsum('jgmgp->jgmp', d.reshape(S5_TILES, 8, B_GROUP, 8, B_STATE)).reshape(B_WIDTH, B_STATE)


def _blockdiag_c(c):
    blocks = c.reshape(S5_TILES, 8, B_GROUP, B_STATE)
    return jnp.einsum('jgmp,gh->jhpgm', blocks, jnp.eye(8, dtype=F32)).reshape(S5_TILES, S5_LANES, 128)


def _blockdiag_c_t(d):
    return jnp.einsum('jgpgm->jgmp', d.reshape(S5_TILES, 8, B_STATE, 8, B_GROUP)).reshape(B_GROUPS, B_GROUP, B_STATE)


def _l0_out(os, ls, ga, gb, ypre, u, x, d_skip, glu_w, glu_b, w_out, post_g, gate):
    s_len = x.shape[0]

    def body(o0, o1, o2, l0, l1, l2, ga_ref, gb_ref, yp_ref, u_ref, x_ref, d_ref, gw_ref, gbias_ref, w_ref, pg_ref, gt_ref, x1_ref, y_ref):
        oa = _merge_gate(*[_from_res(r[...]) for r in (o0, o1, o2, l0, l1, l2)], ga_ref[...])
        yb = _s5_gelu(yp_ref[...], u_ref[...], d_ref[...])
        ob = _s5_glu(yb, _bdot(yb, gw_ref[...]) + gbias_ref[...], gb_ref[...])
        y = _bdot(oa, w_ref[0:512, :]) + _bdot(ob, w_ref[512:1024, :])
        y_ref[...] = y
        x1_ref[...] = _post_res(y, x_ref[...], pg_ref[...], gt_ref[...])

    vec, half = _fix((1, D_MODEL)), _fix((1, 512))
    return pl.pallas_call(
        body, name="l0_out", grid=(s_len // TM,),
        in_specs=[_res_spec(d) for d in DILATIONS] * 2 + [_row(TM, 512)] * 4
        + [_row(TM, D_MODEL), half, _fix((512, 512)), half, _fix((D_MODEL, D_MODEL)), vec, vec],
        out_specs=[_row(TM, D_MODEL)] * 2,
        out_shape=[_sds(s_len, D_MODEL)] * 2,
        compiler_params=_cp("arbitrary"),
    )(*os, *ls, ga, gb, ypre, u, x, d_skip, glu_w, glu_b, w_out, post_g, gate)


def _l0_out_bwd(os, ls, ga, gb, ypre, u, x, y, d_skip, glu_w, glu_b, w_out, post_g, gate, dx1, exchange=None):
    s_len = x.shape[0]

    def body(o0, o1, o2, l0, l1, l2, ga_ref, gb_ref, yp_ref, u_ref, x_ref, y_ref, d_ref, gw_ref, gbias_ref, w_ref, pg_ref, gt_ref, dx1_ref,
             do0, do1, do2, dl0, dl1, dl2, dga_ref, dgb_ref, dyp_ref, du_ref, dd_ref, dgw_ref, dgbias_ref, dw_ref, dpg_ref, dgt_ref):
        _zero_at_first([dd_ref, dgw_ref, dgbias_ref, dw_ref, dpg_ref, dgt_ref])
        _, vjp2 = jax.vjp(_post_res, y_ref[...], x_ref[...], pg_ref[...], gt_ref[...])
        dy, _, dpg, dgt = vjp2(dx1_ref[...])
        _acc(dpg_ref, dpg)
        _acc(dgt_ref, dgt)
        oa, vjp_a = jax.vjp(_merge_gate, *[_from_res(r[...]) for r in (o0, o1, o2, l0, l1, l2)], ga_ref[...])
        yb, vjp_g = jax.vjp(_s5_gelu, yp_ref[...], u_ref[...], d_ref[...])
        gl = _bdot(yb, gw_ref[...]) + gbias_ref[...]
        ob, vjp_b = jax.vjp(_s5_glu, yb, gl, gb_ref[...])
        dw_ref[0:512, :] += _bdot_tn(oa, dy)
        dw_ref[512:1024, :] += _bdot_tn(ob, dy)
        d1, d2, d3, e1, e2, e3, dga = vjp_a(_bdot_nt(dy, w_ref[0:512, :]))
        for ref, val, d in zip((do0, do1, do2, dl0, dl1, dl2), (d1, d2, d3, e1, e2, e3), DILATIONS * 2):
            ref[...] = _to_res(val, d)
        dga_ref[...] = dga
        dyb, dgl, dgb = vjp_b(_bdot_nt(dy, w_ref[512:1024, :]))
        dgb_ref[...] = dgb
        dgw_ref[...] += _bdot_tn(yb, dgl)
        _acc(dgbias_ref, jnp.sum(dgl, axis=0, keepdims=True))
        dyp, du, dd = vjp_g(dyb + _bdot_nt(dgl, gw_ref[...]))
        dyp_ref[...] = dyp
        du_ref[...] = du
        _acc(dd_ref, dd)

    vec, half = _fix((1, D_MODEL)), _fix((1, 512))
    r5, r10 = _row(TM, 512), _row(TM, D_MODEL)
    res6 = [_res_spec(d) for d in DILATIONS] * 2
    return _call_with_exchange(
        body, "l0_out_bwd", (s_len // TM,),
        res6 + [r5] * 4 + [r10, r10, half, _fix((512, 512)), half, _fix((D_MODEL, D_MODEL)), vec, vec, r10],
        res6 + [r5] * 4 + [half, _fix((512, 512)), half, _fix((D_MODEL, D_MODEL)), vec, vec],
        [_sds(*_res_shape(s_len, d)) for d in DILATIONS] * 2 + [_sds(s_len, 512)] * 4
        + [_sds(1, 512), _sds(512, 512), _sds(1, 512), _sds(D_MODEL, D_MODEL), _sds(1, D_MODEL), _sds(1, D_MODEL)],
        [], (*os, *ls, ga, gb, ypre, u, x, y, d_skip, glu_w, glu_b, w_out, post_g, gate, dx1), exchange)


def _l1_front(x, pre_g, scale, shift, w_in):
    s_len = x.shape[0]

    def body(x_ref, g_ref, sc_ref, sh_ref, w_ref, raw_ref, gate_ref, ba_ref, h_ref):
        hb = _pre_mod(x_ref[...], g_ref[...], sc_ref[...], sh_ref[...]).astype(BF)
        h_ref[...] = hb
        z = jnp.dot(hb, w_ref[...], preferred_element_type=F32)
        raw_ref[...] = z[:, 0:QKV]
        gate_ref[...] = z[:, QKV:QKV + 1024]
        ba_ref[...] = z[:, QKV + 1024:C_IN_PAD]

    vec = _fix((1, D_MODEL))
    return pl.pallas_call(
        body, name="l1_front", grid=(s_len // TM,),
        in_specs=[_row(TM, D_MODEL), vec, vec, vec, _fix((D_MODEL, C_IN_PAD))],
        out_specs=[_row(TM, QKV), _row(TM, 1024), _row(TM, 128), _row(TM, D_MODEL)],
        out_shape=[_sds(s_len, QKV), _sds(s_len, 1024), _sds(s_len, 128), _sds(s_len, D_MODEL, dtype=BF)],
        compiler_params=_cp("arbitrary"),
    )(x, pre_g, scale, shift, w_in)


def _bg_fn(ba, alog_row, dtb_row):
    lane = lax.broadcasted_iota(jnp.int32, (1, 128), 1)
    g = -jnp.exp(alog_row) * jax.nn.softplus(ba + dtb_row)
    return jnp.where(lane < C_HEADS, jax.nn.sigmoid(ba), jnp.where(lane < 2 * C_HEADS, g, 0.0))


def _act_q(c):
    q = jax.nn.silu(c)
    return q * lax.rsqrt(jnp.sum(q * q, axis=-1, keepdims=True) + EPS) * (C_DK ** -0.5)


def _act_k(c):
    k = jax.nn.silu(c)
    return k * lax.rsqrt(jnp.sum(k * k, axis=-1, keepdims=True) + EPS)


def _act_of(s):
    return _act_q if s < 8 else (_act_k if s < 16 else jax.nn.silu)


def _conv_taps(prev8, tile_ref, sl, next8=None):
    rows = tile_ref.shape[0]
    head = jnp.concatenate([prev8, tile_ref[0:8, sl]], axis=0)
    tail = None if next8 is None else jnp.concatenate([tile_ref[rows - 8:rows, sl], next8], axis=0)
    taps = []
    for j in range(C_CONV):
        shift = C_CONV - 1 - j
        pieces = [head[8:] if shift == 0 else pltpu.roll(head, shift, 0)[8:], tile_ref[pl.ds(8 - shift, rows - 8), sl]]
        if tail is not None:
            pieces.append(tail[8:] if shift == 0 else pltpu.roll(tail, shift, 0)[8:])
        taps.append(jnp.concatenate(pieces, axis=0))
    return taps


def _gdn_prep(raw, ba, conv_w, alog_row, dtb_row):
    s_len = raw.shape[0]

    def body(raw_ref, halo_ref, ba_ref, w_ref, al_ref, dt_ref, qkv_ref, bg_ref):
        bg_ref[...] = _bg_fn(ba_ref[...], al_ref[...], dt_ref[...])
        has_prev = (pl.program_id(0) > 0).astype(F32)
        for s in range(24):
            sl = slice(s * 128, (s + 1) * 128)
            taps = _conv_taps(halo_ref[:, sl] * has_prev, raw_ref, sl)
            conv = w_ref[3:4, sl] * taps[3]
            for j in range(3):
                conv = conv + w_ref[j:j + 1, sl] * taps[j]
            qkv_ref[:, sl] = _act_of(s)(conv)

    halo = pl.BlockSpec((8, QKV), lambda i: (jnp.maximum(i * (TM // 8) - 1, 0), 0))
    row128 = _fix((1, 128))
    return pl.pallas_call(
        body, name="gdn_prep", grid=(s_len // TM,),
        in_specs=[_row(TM, QKV), halo, _row(TM, 128), _fix((C_CONV, QKV)), row128, row128],
        out_specs=[_row(TM, QKV), _row(TM, 128)],
        out_shape=[_sds(s_len, QKV), _sds(s_len, 128)],
        compiler_params=_cp("arbitrary"),
    )(raw, raw, ba, conv_w, alog_row, dtb_row)


def _gdn_prep_bwd(raw, ba, conv_w, alog_row, dtb_row, dq, dk, dv, dbg):
    s_len = raw.shape[0]
    n_tiles = s_len // TM
    ext = TM + 8

    def body(raw_ref, prev_ref, next_ref, ba_ref, w_ref, al_ref, dt_ref, dq_ref, dqn_ref, dk_ref, dkn_ref, dv_ref, dvn_ref, dbg_ref,
             draw_ref, dba_ref, dw_ref, dal_ref, ddt_ref, dconv_ref):
        _zero_at_first([dw_ref, dal_ref, ddt_ref])
        i = pl.program_id(0)
        _, vjp_bg = jax.vjp(_bg_fn, ba_ref[...], al_ref[...], dt_ref[...])
        dba, dal, ddt = vjp_bg(dbg_ref[...])
        dba_ref[...] = dba
        _acc(dal_ref, dal)
        _acc(ddt_ref, ddt)
        has_prev = (i > 0).astype(F32)
        has_next = (i < n_tiles - 1).astype(F32)
        ct_refs = ((dq_ref, dqn_ref), (dk_ref, dkn_ref), (dv_ref, dvn_ref))
        for s in range(24):
            sl = slice(s * 128, (s + 1) * 128)
            hl = slice((s % 8) * 128, (s % 8 + 1) * 128)
            tile_ref, nxt_ref = ct_refs[s // 8]
            taps = _conv_taps(prev_ref[:, sl] * has_prev, raw_ref, sl, next_ref[:, sl] * has_next)
            conv = w_ref[3:4, sl] * taps[3]
            for j in range(3):
                conv = conv + w_ref[j:j + 1, sl] * taps[j]
            ct = jnp.concatenate([tile_ref[:, hl], nxt_ref[:, hl] * has_next], axis=0)
            _, vjp_act = jax.vjp(_act_of(s), conv)
            dconv, = vjp_act(ct)
            dconv_ref[...] = dconv
            draw = w_ref[3:4, sl] * dconv[:TM]
            for j in range(3):
                draw = draw + w_ref[j:j + 1, sl] * dconv_ref[pl.ds(3 - j, TM), :]
            draw_ref[:, sl] = draw
            for j in range(4):
                dw_ref[j:j + 1, sl] += jnp.sum(dconv[:TM] * taps[j][:TM], axis=0, keepdims=True)

    prev = pl.BlockSpec((8, QKV), lambda i: (jnp.maximum(i * (TM // 8) - 1, 0), 0))
    nxt = lambda n: pl.BlockSpec((8, n), lambda i: (jnp.minimum((i + 1) * (TM // 8), s_len // 8 - 1), 0))
    row128 = _fix((1, 128))
    ct_specs = [_row(TM, 1024), nxt(1024)] * 3
    return pl.pallas_call(
        body, name="gdn_prep_bwd", grid=(n_tiles,),
        in_specs=[_row(TM, QKV), prev, nxt(QKV), _row(TM, 128), _fix((C_CONV, QKV)), row128, row128] + ct_specs + [_row(TM, 128)],
        out_specs=[_row(TM, QKV), _row(TM, 128), _fix((C_CONV, QKV)), row128, row128],
        out_shape=[_sds(s_len, QKV), _sds(s_len, 128), _sds(C_CONV, QKV), _sds(1, 128), _sds(1, 128)],
        scratch_shapes=[pltpu.VMEM((TM + 8, 128), F32)],
        compiler_params=_cp("arbitrary"),
    )(raw, raw, raw, ba, conv_w, alog_row, dtb_row, dq, dq, dk, dk, dv, dv, dbg)


def _tein(eq, a, b):
    return jnp.einsum(eq, a, b, precision=lax.Precision.HIGH, preferred_element_type=F32)


def _unit_lower_inverse(lower):
    ri = lax.broadcasted_iota(jnp.int32, (C_CHUNK, C_CHUNK), 0)
    ci = lax.broadcasted_iota(jnp.int32, (C_CHUNK, C_CHUNK), 1)
    eye = (ri == ci).astype(F32)[None]
    p_mat = -lower
    inv = eye + p_mat
    for _ in range(5):
        p_mat = _bein('hij,hjk->hik', p_mat, p_mat)
        inv = inv + _bein('hij,hjk->hik', inv, p_mat)
    inv = _tein('hij,hjk->hik', inv, 2.0 * eye - _tein('hij,hjk->hik', eye + lower, inv))
    return jnp.where((ri >= ci)[None], inv, 0.0)


@jax.custom_vjp
def _known_inverse(lower, inv):
    return inv


def _known_inverse_fwd(lower, inv):
    return inv, inv


def _known_inverse_bwd(inv, d_inv):
    d_lower = -_bein('hik,hjk->hij', _bein('hji,hjk->hik', inv, d_inv), inv)
    return d_lower, jnp.zeros_like(inv)


_known_inverse.defvjp(_known_inverse_fwd, _known_inverse_bwd)


def _gdn_local(q, k, v, bgs, inv_known=None):
    lane = lax.broadcasted_iota(jnp.int32, (1, 128), 1)
    ri = lax.broadcasted_iota(jnp.int32, (C_CHUNK, C_CHUNK), 0)
    ci = lax.broadcasted_iota(jnp.int32, (C_CHUNK, C_CHUNK), 1)
    row_id = lax.broadcasted_iota(jnp.int32, (128, C_CHUNK), 0)
    beta, gc, gcj = [], [], []
    for bg in bgs:
        gc_t = _hdot((ri >= ci).astype(F32), bg)
        gc_rows = gc_t.T
        for h in range(C_HEADS):
            beta.append(jnp.sum(jnp.where(lane == h, bg, 0.0), axis=-1, keepdims=True))
            gc.append(jnp.sum(jnp.where(lane == C_HEADS + h, gc_t, 0.0), axis=-1, keepdims=True))
            gcj.append(jnp.sum(jnp.where(row_id == C_HEADS + h, gc_rows, 0.0), axis=0, keepdims=True))
    beta, gc, gcj = jnp.stack(beta, axis=0), jnp.stack(gc, axis=0), jnp.stack(gcj, axis=0)
    tril, strict = (ri >= ci)[None], (ri > ci)[None]
    decay = jnp.exp(jnp.where(tril, gc - gcj, -1e30))
    kb = k * beta
    lower = jnp.where(strict, _bein('hid,hjd->hij', kb, k) * decay, 0.0)
    inv = _unit_lower_inverse(lower) if inv_known is None else _known_inverse(lower, inv_known)
    egc = jnp.exp(gc)
    u_c = _bein('hij,hjd->hid', inv, v * beta)
    w_c = _bein('hij,hjd->hid', inv, kb * egc)
    aqk = _bein('hid,hjd->hij', q, k) * decay
    rowi = lax.broadcasted_iota(jnp.int32, (1, C_CHUNK, 1), 1)
    g_last = jnp.sum(jnp.where(rowi == C_CHUNK - 1, gc, 0.0), axis=1, keepdims=True)
    kd = k * jnp.exp(g_last - gc)
    return (u_c, w_c, aqk, q * egc, kd, jnp.exp(g_last)), inv


def _gdn_state(local, state):
    u_c, w_c, aqk, qg, kd, dec = local
    v_new = u_c - _bein('hik,hkv->hiv', w_c, state)
    o = _bein('hik,hkv->hiv', qg, state) + _bein('hij,hjv->hiv', aqk, v_new)
    return o, state * dec + _bein('hik,hiv->hkv', kd, v_new)


C_SUB = 4


def _gdn_group(q, k, v, bgs, state, inv_known=None):
    local, inv = _gdn_local(q, k, v, bgs, inv_known)
    outs = []
    for s in range(len(bgs)):
        o, state = _gdn_state(tuple(t[s * C_HEADS:(s + 1) * C_HEADS] for t in local), state)
        outs.append(o)
    return outs, state, inv


def _heads(ref):
    return jnp.stack([ref[s * C_CHUNK:(s + 1) * C_CHUNK, h * C_DK:(h + 1) * C_DK] for s in range(C_SUB) for h in range(C_HEADS)], axis=0)


def _put_heads(ref, sub, val):
    rows = slice(sub * C_CHUNK, (sub + 1) * C_CHUNK)
    for h in range(C_HEADS):
        ref[rows, h * C_DK:(h + 1) * C_DK] = val[h]


def _gdn_specs(s_len, rev):
    rows = C_SUB * C_CHUNK
    n_g = s_len // rows
    at = (lambda i: n_g - 1 - i) if rev else (lambda i: i)
    col = lambda c: pl.BlockSpec((rows, 1024), lambda i: (at(i), c))
    row128 = pl.BlockSpec((rows, 128), lambda i: (at(i), 0))
    state = pl.BlockSpec((1, C_HEADS, C_DK, C_DK), lambda i: (at(i), 0, 0, 0))
    inv = pl.BlockSpec((1, C_SUB * C_HEADS, C_CHUNK, C_CHUNK), lambda i: (at(i), 0, 0, 0))
    return n_g, col, row128, state, inv


def _gdn_fwd(qkv, bg):
    s_len = qkv.shape[0]
    n_g, col, row128, state_spec, inv_spec = _gdn_specs(s_len, False)

    def body(q_ref, k_ref, v_ref, bg_ref, o_ref, ss_ref, inv_ref, st_ref):
        _zero_at_first([st_ref])
        s0 = st_ref[...]
        ss_ref[0] = s0
        bgs = [bg_ref[s * C_CHUNK:(s + 1) * C_CHUNK, :] for s in range(C_SUB)]
        outs, s2, inv = _gdn_group(_heads(q_ref), _heads(k_ref), _heads(v_ref), bgs, s0)
        st_ref[...] = s2
        inv_ref[0] = inv
        for s in range(C_SUB):
            _put_heads(o_ref, s, outs[s])

    return pl.pallas_call(
        body, name="gdn_fwd", grid=(n_g,),
        in_specs=[col(0), col(1), col(2), row128],
        out_specs=[col(0), state_spec, inv_spec],
        out_shape=[_sds(s_len, 1024), _sds(n_g, C_HEADS, C_DK, C_DK), _sds(n_g, C_SUB * C_HEADS, C_CHUNK, C_CHUNK)],
        scratch_shapes=[pltpu.VMEM((C_HEADS, C_DK, C_DK), F32)],
        compiler_params=_cp("arbitrary"),
    )(qkv, qkv, qkv, bg)


def _gdn_bwd(qkv, bg, states, invs, do):
    s_len = qkv.shape[0]
    n_g, col, row128, state_spec, inv_spec = _gdn_specs(s_len, True)

    def body(q_ref, k_ref, v_ref, bg_ref, ss_ref, inv_ref, do_ref, dq_ref, dk_ref, dv_ref, dbg_ref, ds_ref):
        _zero_at_first([ds_ref])
        inv_known = inv_ref[0]

        def group(q, k, v, bgs, st):
            outs, st2, _ = _gdn_group(q, k, v, bgs, st, inv_known)
            return outs, st2

        bgs = [bg_ref[s * C_CHUNK:(s + 1) * C_CHUNK, :] for s in range(C_SUB)]
        _, vjp = jax.vjp(group, _heads(q_ref), _heads(k_ref), _heads(v_ref), bgs, ss_ref[0])
        douts = [jnp.stack([do_ref[s * C_CHUNK:(s + 1) * C_CHUNK, h * C_DK:(h + 1) * C_DK] for h in range(C_HEADS)], axis=0)
                 for s in range(C_SUB)]
        dq, dk, dv, dbgs, ds = vjp((douts, ds_ref[...]))
        ds_ref[...] = ds
        for s in range(C_SUB):
            dbg_ref[s * C_CHUNK:(s + 1) * C_CHUNK, :] = dbgs[s]
            for ref, val in ((dq_ref, dq), (dk_ref, dk), (dv_ref, dv)):
                _put_heads(ref, s, val[s * C_HEADS:(s + 1) * C_HEADS])

    return pl.pallas_call(
        body, name="gdn_bwd", grid=(n_g,),
        in_specs=[col(0), col(1), col(2), row128, state_spec, inv_spec, col(0)],
        out_specs=[col(0), col(0), col(0), row128],
        out_shape=[_sds(s_len, 1024)] * 3 + [_sds(s_len, 128)],
        scratch_shapes=[pltpu.VMEM((C_HEADS, C_DK, C_DK), F32)],
        compiler_params=_cp("arbitrary"),
    )(qkv, qkv, qkv, bg, states, invs, do)


def _head_norm_gate(o, gate, norm_g):
    return (_rms(o) * norm_g) * jax.nn.silu(gate)


def _l1_out_fb(o, gate_c, x1, target, norm_g, w_out, post_g, gate):
    s_len = x1.shape[0]

    def body(o_ref, gc_ref, x1_ref, t_ref, ng_ref, w_ref, pg_ref, gt_ref,
             loss_ref, dres_ref, do_ref, dgc_ref, dw_ref, dng_ref, dpg_ref, dgt_ref):
        _zero_at_first([loss_ref, dw_ref, dng_ref, dpg_ref, dgt_ref])
        ng = ng_ref[...]
        ons, vjps = [], []
        for h in range(C_HEADS):
            sl = slice(h * C_DK, (h + 1) * C_DK)
            on, vjp_h = jax.vjp(_head_norm_gate, o_ref[:, sl], gc_ref[:, sl], ng)
            ons.append(on)
            vjps.append(vjp_h)
        on_all = jnp.concatenate(ons, axis=-1)
        y = _bdot(on_all, w_ref[...])
        x2, vjp2 = jax.vjp(_post_res, y, x1_ref[...], pg_ref[...], gt_ref[...])
        err = x2 - t_ref[...]
        _acc(loss_ref, jnp.full((1, 128), 0.5 * jnp.sum(jnp.mean(err * err, axis=-1)), F32))
        dx2 = err * (1.0 / D_MODEL)
        dy, _, dpg, dgt = vjp2(dx2)
        dres_ref[...] = dx2
        _acc(dpg_ref, dpg)
        _acc(dgt_ref, dgt)
        dw_ref[...] += _bdot_tn(on_all, dy)
        don = _bdot_nt(dy, w_ref[...])
        for h in range(C_HEADS):
            sl = slice(h * C_DK, (h + 1) * C_DK)
            do_h, dgc_h, dng = vjps[h](don[:, sl])
            do_ref[:, sl] = do_h
            dgc_ref[:, sl] = dgc_h
            _acc(dng_ref, dng)

    vec, r10 = _fix((1, D_MODEL)), _row(TM, D_MODEL)
    row128 = _fix((1, 128))
    return pl.pallas_call(
        body, name="l1_out_fb", grid=(s_len // TM,),
        in_specs=[r10, r10, r10, r10, row128, _fix((D_MODEL, D_MODEL)), vec, vec],
        out_specs=[row128, r10, r10, r10, _fix((D_MODEL, D_MODEL)), row128, vec, vec],
        out_shape=[_sds(1, 128), _sds(s_len, D_MODEL), _sds(s_len, D_MODEL), _sds(s_len, D_MODEL),
                   _sds(D_MODEL, D_MODEL), _sds(1, 128), _sds(1, D_MODEL), _sds(1, D_MODEL)],
        compiler_params=_cp("arbitrary"),
    )(o, gate_c, x1, target, norm_g, w_out, post_g, gate)


def _row_of(v, width, at):
    return jnp.zeros((1, width), F32).at[0, at:at + v.shape[-1]].set(v.reshape(-1))


def _local_step(x, target, mod, wd, comm=None):
    s_len = x.shape[0]
    shift0, scale0, gate0 = (mod[0:1, i * 1024:(i + 1) * 1024] for i in range(3))
    shift1, scale1, gate1 = (mod[1:2, i * 1024:(i + 1) * 1024] for i in range(3))
    pre_g0, pre_g1 = wd["pre_g"][0:1], wd["pre_g"][1:2]
    post_g0, post_g1 = wd["post_g"][0:1], wd["post_g"][1:2]
    w_in0 = wd["ab_w_in"].astype(BF)
    d_skip, glu_b = wd["s5_d"].reshape(1, 512), wd["s5_glu_b"].reshape(1, 512)
    norm_g = wd["gdn_norm_g"].reshape(1, 128)
    alog_row = _row_of(wd["gdn_a_log"], 128, C_HEADS)
    dtb_row = _row_of(wd["gdn_dt_bias"], 128, C_HEADS)
    conv_w = wd["gdn_conv"]

    a_re, a_im = wd["s5_a_re"], wd["s5_a_im"]
    log_dt = wd["s5_log_dt"].reshape(B_GROUPS, 1)
    bt_re = wd["s5_b_re"].transpose(0, 2, 1).reshape(B_WIDTH, B_STATE)
    bt_im = wd["s5_b_im"].transpose(0, 2, 1).reshape(B_WIDTH, B_STATE)
    abar_r, abar_i, bbar_r, bbar_i = _s5_params(a_re, a_im, log_dt, bt_re, bt_im)
    abr, abi = abar_r.reshape(1, -1), abar_i.reshape(1, -1)
    btr, bti = _blockdiag_b(bbar_r).astype(BF), _blockdiag_b(bbar_i).astype(BF)
    ctr, cti = _blockdiag_c(wd["s5_c_re"]).astype(BF), _blockdiag_c(wd["s5_c_im"]).astype(BF)

    table = _bucket_table()
    biases = _attn_bias(wd["rel_bias"], jnp.asarray(table))
    front = _l0_front(x, pre_g0, scale0, shift0, w_in0)
    qs, ks, vs = front[0:3], front[3:6], front[6:9]
    u, ga, gb, h0 = front[9:]
    riders = [None] * 4 if comm is None else comm.late_exchanges()
    os, ls, got = [], [], []
    for i in range(3):
        (o_d, l_d), g = _attn_fwd(qs[i], ks[i], vs[i], biases[i], exchange=riders[i])
        os.append(o_d)
        ls.append(l_d)
        got.append(g)
    seg_len = s_len // S5_SEG
    zero_state = (jnp.zeros((S5_SEG, S5_WIDTH), F32),) * 2
    ends, _ = _s5_seq_fwd(u, btr, bti, ctr, cti, abr, abi, zero_state, False)
    x_entry = _s5_entries("s5_entries_fwd", *ends, abr, abi, seg_len, False)
    (xr, xi, ypre3, _, _), g = _s5_seq_fwd(u, btr, bti, ctr, cti, abr, abi, x_entry, True, exchange=riders[3])
    got.append(g)
    ypre = ypre3.reshape(s_len, B_WIDTH)
    if comm is not None:
        wd = {**wd, **comm.late_weights(got)}
    w_out0 = wd["ab_w_out"].astype(BF)
    glu_w = wd["s5_glu_w"].astype(BF)
    w_in1 = jnp.concatenate([wd["gdn_w_in"], jnp.zeros((D_MODEL, C_IN_PAD - wd["gdn_w_in"].shape[1]), wd["gdn_w_in"].dtype)], axis=1).astype(BF)
    w_out1 = wd["gdn_w_out"].astype(BF)
    x1, y0 = _l0_out(os, ls, ga, gb, ypre, u, x, d_skip, glu_w, glu_b, w_out0, post_g0, gate0)

    raw, gate_c, ba, h1 = _l1_front(x1, pre_g1, scale1, shift1, w_in1)
    qkv, bg = _gdn_prep(raw, ba, conv_w, alog_row, dtb_row)
    o_gdn, states, invs = _gdn_fwd(qkv, bg)
    loss_row, dres1, do_gdn, dgate_c, dw_out1, dnorm_g, dpost_g1, dgate1 = _l1_out_fb(
        o_gdn, gate_c, x1, target, norm_g, w_out1, post_g1, gate1)

    dq1, dk1, dv1, dbg = _gdn_bwd(qkv, bg, states, invs, do_gdn)
    draw, dba, dconv_w, dalog_row, ddtb_row = _gdn_prep_bwd(raw, ba, conv_w, alog_row, dtb_row, dq1, dk1, dv1, dbg)
    dz1, dx1, dpre_g1, dscale1, dshift1 = _front_bwd(
        "l1_front_bwd", x1, pre_g1, scale1, shift1, w_in1, dres1, [[draw], [dgate_c], [dba]], [QKV, 1024, 128])
    dw_in1 = _matmul_tn("l1_dw_in", h1, dz1, 1408)

    n_w1 = wd["gdn_w_in"].shape[1]
    ex1 = None if comm is None else (comm.chip_partials("l1", {"gdn_w_in": dw_in1[:, :n_w1], "gdn_w_out": dw_out1}), True)
    l0b, got1 = _l0_out_bwd(os, ls, ga, gb, ypre, u, x, y0, d_skip, glu_w, glu_b, w_out0, post_g0, gate0, dx1, exchange=ex1)
    dos, dls = l0b[0:3], l0b[3:6]
    dga, dgb, dypre, du_skip, dd_skip, dglu_w, dglu_b, dw_out0, dpost_g0, dgate0 = l0b[6:]
    ex2 = None if comm is None else (comm.chip_partials("l0_out", {"ab_w_out": dw_out0, "s5_glu_w": dglu_w}), True)
    starts, _ = _s5_seq_bwd(dypre, None, None, None, btr, bti, ctr, cti, abr, abi, zero_state, None, False)
    g_entry = _s5_entries("s5_entries_bwd", *starts, abr, -abi, seg_len, True)
    (du3, dbtr, dbti, dctr, dcti, dabr, dabi, _, _), got2 = _s5_seq_bwd(
        dypre, xr, xi, u, btr, bti, ctr, cti, abr, abi, g_entry, x_entry, True, exchange=ex2)
    du_scan = du3.reshape(s_len, B_WIDTH)
    if comm is not None:
        comm.received.update(zip(("gdn_w_in", "gdn_w_out", "ab_w_out", "s5_glu_w"), list(got1) + list(got2)))
    dqs, dks, dvs, dbs = [], [], [], []
    for i in range(3):
        dq_d, dk_d, dv_d, db_d = _attn_bwd(qs[i], ks[i], vs[i], biases[i], os[i], ls[i], dos[i], dls[i])
        dqs.append(dq_d)
        dks.append(dk_d)
        dvs.append(dv_d)
        dbs.append(db_d)
    parts = [dqs, dks, dvs, [du_skip, du_scan], [dga], [dgb]]
    dz0, grad_x, dpre_g0, dscale0, dshift0 = _front_bwd(
        "l0_front_bwd", x, pre_g0, scale0, shift0, w_in0, dx1, parts, [512] * 6)
    dw_in0 = _matmul_tn("l0_dw_in", h0, dz0, 768)

    idx_rows = jnp.asarray(table.reshape(3, -1), F32)
    drel = _rel_bias_grad(dbs, idx_rows).T
    da_re, da_im, dlog_dt, dbt_re, dbt_im = _s5_params_bwd(
        a_re, a_im, log_dt, bt_re, bt_im, dabr.reshape(B_GROUPS, B_STATE), dabi.reshape(B_GROUPS, B_STATE),
        _blockdiag_b_t(dbtr), _blockdiag_b_t(dbti))
    unb = lambda d: d.reshape(B_GROUPS, B_GROUP, B_STATE).transpose(0, 2, 1)
    grads = {
        "pre_g": jnp.concatenate([dpre_g0, dpre_g1], 0), "post_g": jnp.concatenate([dpost_g0, dpost_g1], 0),
        "rel_bias": drel, "ab_w_in": dw_in0, "ab_w_out": dw_out0,
        "s5_a_re": da_re, "s5_a_im": da_im, "s5_log_dt": dlog_dt.reshape(B_GROUPS),
        "s5_b_re": unb(dbt_re), "s5_b_im": unb(dbt_im),
        "s5_c_re": _blockdiag_c_t(dctr), "s5_c_im": _blockdiag_c_t(dcti),
        "s5_d": dd_skip.reshape(512), "s5_glu_w": dglu_w, "s5_glu_b": dglu_b.reshape(512),
        "gdn_w_in": dw_in1[:, :wd["gdn_w_in"].shape[1]], "gdn_conv": dconv_w,
        "gdn_a_log": dalog_row[0, C_HEADS:2 * C_HEADS], "gdn_dt_bias": ddtb_row[0, C_HEADS:2 * C_HEADS],
        "gdn_norm_g": dnorm_g.reshape(128), "gdn_w_out": dw_out1,
    }
    dmod = jnp.concatenate([jnp.concatenate([dshift0, dscale0, dgate0], 1), jnp.concatenate([dshift1, dscale1, dgate1], 1)], 0)
    return loss_row[0, 0], grad_x, grads, dmod


def _place():
    return lax.axis_index("x"), lax.axis_index("y"), lax.axis_index("c")


def _flip(v, bit):
    return 1 - v if bit else v


def _hbm_call(name, body, arrs, out_shapes, n_sem):
    any_spec = pl.BlockSpec(memory_space=pl.ANY)
    return pl.pallas_call(
        body, name=name,
        in_specs=[any_spec] * len(arrs), out_specs=[any_spec] * len(out_shapes), out_shape=out_shapes,
        scratch_shapes=[pltpu.SemaphoreType.DMA((n_sem,)), pltpu.SemaphoreType.DMA((n_sem,))],
    )(*arrs)


def _own_slot(gathered, own, slot):
    idx = lax.broadcasted_iota(jnp.int32, (gathered.shape[0],) + (1,) * own.ndim, 0)
    return jnp.where(idx == slot, own[None], gathered)


def _all_gather8(name, arr):
    def body(x_ref, out_ref, send_sems, recv_sems):
        x, y, c = _place()
        me = 4 * x + 2 * y + c
        sends, recvs = [], []
        for m in range(1, 8):
            peer = (_flip(x, m & 4), _flip(y, m & 2), _flip(c, m & 1))
            sends.append(pltpu.make_async_remote_copy(x_ref, out_ref.at[me], send_sems.at[m - 1], recv_sems.at[m - 1],
                                                      device_id=peer, device_id_type=MESH))
            recvs.append(pltpu.make_async_remote_copy(x_ref, out_ref.at[4 * peer[0] + 2 * peer[1] + peer[2]], send_sems.at[m - 1],
                                                      recv_sems.at[m - 1], device_id=peer, device_id_type=MESH))
        for cp in sends:
            cp.start()
        for cp in recvs:
            cp.wait_recv()
        for cp in sends:
            cp.wait_send()

    return _hbm_call(name, body, [arr], [jax.ShapeDtypeStruct((8,) + arr.shape, arr.dtype)], 7)[0]


def _all_to_all8(name, arr):
    def body(x_ref, out_ref, send_sems, recv_sems):
        x, y, c = _place()
        me = 4 * x + 2 * y + c
        sends, recvs = [], []
        for m in range(1, 8):
            peer = (_flip(x, m & 4), _flip(y, m & 2), _flip(c, m & 1))
            peer_id = 4 * peer[0] + 2 * peer[1] + peer[2]
            sends.append(pltpu.make_async_remote_copy(x_ref.at[peer_id], out_ref.at[me], send_sems.at[m - 1], recv_sems.at[m - 1],
                                                      device_id=peer, device_id_type=MESH))
            recvs.append(pltpu.make_async_remote_copy(x_ref.at[peer_id], out_ref.at[peer_id], send_sems.at[m - 1], recv_sems.at[m - 1],
                                                      device_id=peer, device_id_type=MESH))
        for cp in sends:
            cp.start()
        for cp in recvs:
            cp.wait_recv()
        for cp in sends:
            cp.wait_send()

    return _hbm_call(name, body, [arr], [jax.ShapeDtypeStruct(arr.shape, arr.dtype)], 7)[0]


def _chip_copies(ins, outs, send_sems, recv_sems, scatter):
    x, y, c = _place()
    mine = 2 * x + y
    sends, recvs = [], []
    for a in range(len(ins)):
        for m in range(1, 4):
            px, py = _flip(x, m & 2), _flip(y, m & 1)
            k = 3 * a + m - 1
            src = ins[a].at[2 * px + py] if scatter else ins[a]
            sends.append(pltpu.make_async_remote_copy(src, outs[a].at[mine], send_sems.at[k], recv_sems.at[k],
                                                      device_id=(px, py, c), device_id_type=MESH))
            recvs.append(pltpu.make_async_remote_copy(src, outs[a].at[2 * px + py], send_sems.at[k], recv_sems.at[k],
                                                      device_id=(px, py, c), device_id_type=MESH))
    return sends, recvs


def _chip_shapes(arrs, scatter):
    return [jax.ShapeDtypeStruct(a.shape if scatter else (4,) + a.shape, a.dtype) for a in arrs]


def _chip_exchange(name, arrs, scatter):
    n = len(arrs)

    def body(*refs):
        sends, recvs = _chip_copies(refs[:n], refs[n:2 * n], refs[2 * n], refs[2 * n + 1], scatter)
        for cp in sends:
            cp.start()
        for cp in recvs:
            cp.wait_recv()
        for cp in sends:
            cp.wait_send()

    return _hbm_call(name, body, arrs, _chip_shapes(arrs, scatter), 3 * n)


def _call_with_exchange(body, name, grid, in_specs, out_specs, out_shape, scratch_shapes, args, exchange):
    if exchange is None:
        return pl.pallas_call(body, name=name, grid=grid, in_specs=in_specs, out_specs=out_specs, out_shape=out_shape,
                              scratch_shapes=scratch_shapes, compiler_params=_cp(*["arbitrary"] * len(grid)))(*args), []
    arrs, scatter = exchange
    n_in, n_out, n_ex, n_scr = len(in_specs), len(out_specs), len(arrs), len(scratch_shapes)

    def fused(*refs):
        ins, ex_in = refs[:n_in], refs[n_in:n_in + n_ex]
        outs, ex_out = refs[n_in + n_ex:n_in + n_ex + n_out], refs[n_in + n_ex + n_out:n_in + 2 * n_ex + n_out]
        rest = refs[n_in + 2 * n_ex + n_out:]
        sends, recvs = _chip_copies(ex_in, ex_out, rest[n_scr], rest[n_scr + 1], scatter)
        first, last = pl.program_id(0) == 0, pl.program_id(0) == grid[0] - 1
        for k in range(1, len(grid)):
            first, last = first & (pl.program_id(k) == 0), last & (pl.program_id(k) == grid[k] - 1)

        @pl.when(first)
        def _():
            for cp in sends:
                cp.start()

        body(*ins, *outs, *rest[:n_scr])

        @pl.when(last)
        def _():
            for cp in recvs:
                cp.wait_recv()
            for cp in sends:
                cp.wait_send()

    any_spec = pl.BlockSpec(memory_space=pl.ANY)
    res = pl.pallas_call(
        fused, name=name, grid=grid, in_specs=list(in_specs) + [any_spec] * n_ex, out_specs=list(out_specs) + [any_spec] * n_ex,
        out_shape=list(out_shape) + _chip_shapes(arrs, scatter),
        scratch_shapes=list(scratch_shapes) + [pltpu.SemaphoreType.DMA((3 * n_ex,))] * 2,
        compiler_params=_cp(*["arbitrary"] * len(grid)))(*args, *arrs)
    return res[:n_out], res[n_out:]


def _sibling_exchange(name, arrs):
    n = len(arrs)

    def body(*refs):
        ins, outs = refs[:n], refs[n:2 * n]
        send_sems, recv_sems = refs[2 * n:]
        x, y, c = _place()
        copies = [pltpu.make_async_remote_copy(ins[a], outs[a], send_sems.at[a], recv_sems.at[a],
                                               device_id=(x, y, 1 - c), device_id_type=MESH) for a in range(n)]
        for cp in copies:
            cp.start()
        for cp in copies:
            cp.wait_recv()
        for cp in copies:
            cp.wait_send()

    return _hbm_call(name, body, arrs, [jax.ShapeDtypeStruct(a.shape, a.dtype) for a in arrs], n)


def _row_tile(rows):
    for t in (256, 128, 64, 32, 16, 8):
        if rows % t == 0:
            return t
    return rows


def _pair_sum(name, a, b, out_dtype):
    rows, cols = a.shape
    tr = _row_tile(rows)

    def body(a_ref, b_ref, o_ref):
        o_ref[...] = (a_ref[...] + b_ref[...]).astype(out_dtype)

    return pl.pallas_call(body, name=name, grid=(rows // tr,), in_specs=[_row(tr, cols)] * 2, out_specs=_row(tr, cols),
                          out_shape=_sds(rows, cols, dtype=out_dtype), compiler_params=_cp("arbitrary"))(a, b)


def _chip_sum(name, recv, partial, mine):
    n, rows, cols = recv.shape
    tr = _row_tile(rows)

    def body(mine_ref, *refs):
        own = refs[n][0].astype(F32)
        acc = None
        for s in range(n):
            term = jnp.where(mine_ref[0] == s, own, refs[s][0].astype(F32))
            acc = term if acc is None else acc + term
        refs[-1][...] = acc

    def slot_spec(s):
        return pl.BlockSpec((1, tr, cols), lambda i, m: (jnp.where(m[0] == s, (s + 1) % n, s), i, 0))

    grid_spec = pltpu.PrefetchScalarGridSpec(
        num_scalar_prefetch=1, grid=(rows // tr,),
        in_specs=[slot_spec(s) for s in range(n)] + [pl.BlockSpec((1, tr, cols), lambda i, m: (m[0], i, 0))],
        out_specs=pl.BlockSpec((tr, cols), lambda i, m: (i, 0)))
    return pl.pallas_call(body, name=name, grid_spec=grid_spec, out_shape=_sds(rows, cols),
                          compiler_params=_cp("arbitrary"))(mine, *([recv] * n), partial)


def _slot_sum(name, arr):
    n, rows, cols = arr.shape
    tr = _row_tile(rows)

    def body(*refs):
        acc = refs[0][0]
        for r in refs[1:-1]:
            acc = acc + r[0]
        refs[-1][...] = acc

    specs = [pl.BlockSpec((1, tr, cols), functools.partial(lambda s, i: (s, i, 0), s)) for s in range(n)]
    return pl.pallas_call(body, name=name, grid=(rows // tr,), in_specs=specs, out_specs=_row(tr, cols),
                          out_shape=_sds(rows, cols), compiler_params=_cp("arbitrary"))(*([arr] * n))


def _adamw(name, w, g, m, v):
    rows, cols = w.shape
    tr = _row_tile(rows)

    def body(w_ref, g_ref, m_ref, v_ref, d_ref, nm_ref, nv_ref):
        g_ = g_ref[...]
        m_ = ADAM_B1 * m_ref[...] + (1.0 - ADAM_B1) * g_
        v_ = ADAM_B2 * v_ref[...] + (1.0 - ADAM_B2) * (g_ * g_)
        m_hat = m_ / (1.0 - ADAM_B1 ** ADAM_STEP)
        v_hat = v_ / (1.0 - ADAM_B2 ** ADAM_STEP)
        d_ref[...] = -ADAM_LR * (m_hat / (jnp.sqrt(v_hat) + ADAM_EPS) + ADAM_WD * w_ref[...])
        nm_ref[...] = m_
        nv_ref[...] = v_

    spec = _row(tr, cols)
    return pl.pallas_call(body, name=name, grid=(rows // tr,), in_specs=[spec] * 4, out_specs=[spec] * 3,
                          out_shape=[_sds(rows, cols)] * 3, compiler_params=_cp("arbitrary"))(w, g, m, v)


def _adamw_halves(name, w, g_mine, g_sibling, m, v, core):
    _, rows, cols = w.shape
    half = rows // 2
    tr = _row_tile(half)
    per_half = half // tr

    def body(core_ref, w_ref, gm_ref, gs_ref, m_ref, v_ref, g_ref, d_ref, nm_ref, nv_ref):
        g_ = jnp.where(pl.program_id(0) // per_half == core_ref[0], gm_ref[...], gs_ref[...])
        m_ = ADAM_B1 * m_ref[...] + (1.0 - ADAM_B1) * g_
        v_ = ADAM_B2 * v_ref[...] + (1.0 - ADAM_B2) * (g_ * g_)
        m_hat = m_ / (1.0 - ADAM_B1 ** ADAM_STEP)
        v_hat = v_ / (1.0 - ADAM_B2 ** ADAM_STEP)
        g_ref[...] = g_
        d_ref[...] = -ADAM_LR * (m_hat / (jnp.sqrt(v_hat) + ADAM_EPS) + ADAM_WD * w_ref[...])
        nm_ref[...] = m_
        nv_ref[...] = v_

    full = pl.BlockSpec((None, tr, cols), lambda i, c: (0, i, 0))
    in_half = pl.BlockSpec((tr, cols), lambda i, c: (i % per_half, 0))
    grid_spec = pltpu.PrefetchScalarGridSpec(num_scalar_prefetch=1, grid=(rows // tr,),
                                             in_specs=[full, in_half, in_half, full, full], out_specs=[full] * 4)
    return pl.pallas_call(body, name=name, grid_spec=grid_spec, out_shape=[_sds(1, rows, cols)] * 4,
                          compiler_params=_cp("arbitrary"))(core, w, g_mine, g_sibling, m, v)


def _mod_local(c_all, ada_w):
    def body(c_ref, w_ref, o_ref):
        c_act = jax.nn.silu(c_ref[...])
        for l in range(2):
            o_ref[l] = _hdot(c_act, w_ref[l])

    return pl.pallas_call(body, name="mod_local", out_shape=_sds(2, 8, ada_w.shape[2]),
                          compiler_params=pltpu.CompilerParams(vmem_limit_bytes=VMEM_LIMIT_BYTES))(c_all, ada_w)


def _ada_w_grad(c_all, dmod_cols):
    def body(c_ref, d_ref, o_ref):
        c_act = jax.nn.silu(c_ref[...])
        for l in range(2):
            o_ref[l] = lax.dot_general(c_act, d_ref[l], (((0,), (0,)), ((), ())), precision=HI, preferred_element_type=F32)

    return pl.pallas_call(body, name="ada_w_grad", out_shape=_sds(2, D_MODEL, dmod_cols.shape[2]),
                          compiler_params=pltpu.CompilerParams(vmem_limit_bytes=VMEM_LIMIT_BYTES))(c_all, dmod_cols)


_SMALL = ("ada_b", "pre_g", "post_g", "rel_bias", "s5_a_re", "s5_a_im", "s5_log_dt", "s5_b_re", "s5_b_im", "s5_c_re", "s5_c_im",
          "s5_d", "s5_glu_b", "gdn_a_log", "gdn_dt_bias", "gdn_norm_g")
_SHARDED = ("ab_w_in", "ab_w_out", "s5_glu_w", "gdn_w_in", "gdn_w_out")
_COL_SHARDED = ("ab_w_in", "gdn_w_in")
_WEIGHTS = ("ada_w", "ada_b", "pre_g", "post_g", "rel_bias", "ab_w_in", "ab_w_out", "s5_a_re", "s5_a_im", "s5_log_dt", "s5_b_re",
            "s5_b_im", "s5_c_re", "s5_c_im", "s5_d", "s5_glu_w", "s5_glu_b", "gdn_w_in", "gdn_conv", "gdn_a_log", "gdn_dt_bias",
            "gdn_norm_g", "gdn_w_out")


def _rows128(n):
    return -(-n // 128)


def _pack(arrs, total_rows):
    pieces = []
    for a in arrs:
        flat = a.reshape(-1)
        pieces.append(jnp.pad(flat, (0, _rows128(flat.shape[0]) * 128 - flat.shape[0])).reshape(-1, 128))
    used = sum(p.shape[0] for p in pieces)
    pieces.append(jnp.zeros((total_rows - used, 128), F32))
    return jnp.concatenate(pieces, axis=0)


def _unpack(buf, shapes):
    out, at = [], 0
    for shp in shapes:
        n = int(np.prod(shp))
        out.append(buf[at:at + _rows128(n)].reshape(-1)[:n].reshape(shp))
        at += _rows128(n)
    return out


def _full_from_halves(name, g):
    if name in _COL_SHARDED:
        return g.transpose(0, 2, 1, 3).reshape(2 * g.shape[2], 4 * g.shape[3])
    return g.transpose(1, 0, 2, 3).reshape(8 * g.shape[2], g.shape[3])


def _shard_major(name, g):
    if name in _COL_SHARDED:
        return g.reshape(g.shape[0], 4, g.shape[1] // 4).transpose(1, 0, 2)
    return g.reshape(4, g.shape[0] // 4, g.shape[1])


_LATE = ("ab_w_out", "s5_glu_w", "gdn_w_in", "gdn_w_out")


class _WeightExchanges:
    def __init__(self, shards, core, chip):
        self.core, self.chip = core, chip
        self.half = {}
        for name, shard in shards.items():
            h = shard.shape[0] // 2
            self.half[name] = lax.dynamic_slice_in_dim(shard.astype(BF), core * h, h, axis=0)
        self.partial, self.received = {}, {}

    def _full(self, label, names, from_chips):
        mine = [_own_slot(g, self.half[n], self.chip) for n, g in zip(names, from_chips)]
        theirs = _sibling_exchange("gather_w_sibling_" + label, mine)
        return {n: _full_from_halves(n, jnp.where(self.core == 0, jnp.stack([a, b], 0), jnp.stack([b, a], 0)))
                for n, a, b in zip(names, mine, theirs)}

    def first_weights(self):
        return self._full("first", ["ab_w_in"], _chip_exchange("gather_w_chips", [self.half["ab_w_in"]], False))

    def late_exchanges(self):
        rows = self.half["gdn_w_in"].shape[0] // 2
        pieces = [self.half["gdn_w_in"][:rows], self.half["gdn_w_in"][rows:]]
        return [([self.half["ab_w_out"], self.half["s5_glu_w"]], False), ([self.half["gdn_w_out"]], False),
                ([pieces[0]], False), ([pieces[1]], False)]

    def late_weights(self, got):
        from_chips = {"ab_w_out": got[0][0], "s5_glu_w": got[0][1], "gdn_w_out": got[1][0],
                      "gdn_w_in": jnp.concatenate([got[2][0], got[3][0]], axis=1)}
        return self._full("late", list(_LATE), [from_chips[n] for n in _LATE])

    def chip_partials(self, label, grads):
        mine, other = [], []
        for name, g in grads.items():
            sm = _shard_major(name, g)
            h = sm.shape[1] // 2
            mine.append(lax.dynamic_slice_in_dim(sm, self.core * h, h, axis=1))
            other.append(lax.dynamic_slice_in_dim(sm, (1 - self.core) * h, h, axis=1))
        out = []
        for name, a, b in zip(grads, mine, _sibling_exchange("reduce_sibling_" + label, other)):
            flat = lambda t: t.reshape(-1, t.shape[-1])
            self.partial[name] = _pair_sum("sum_sibling_" + name, flat(a), flat(b), BF).reshape(a.shape)
            out.append(self.partial[name])
        return out


def kernel(x, c, ada_w, ada_b, pre_g, post_g, rel_bias, ab_w_in, ab_w_out, s5_a_re, s5_a_im, s5_log_dt, s5_b_re, s5_b_im, s5_c_re, s5_c_im, s5_d, s5_glu_w, s5_glu_b, gdn_w_in, gdn_conv, gdn_a_log, gdn_dt_bias, gdn_norm_g, gdn_w_out, loss_target, m_ada_w, m_ada_b, m_pre_g, m_post_g, m_rel_bias, m_ab_w_in, m_ab_w_out, m_s5_a_re, m_s5_a_im, m_s5_log_dt, m_s5_b_re, m_s5_b_im, m_s5_c_re, m_s5_c_im, m_s5_d, m_s5_glu_w, m_s5_glu_b, m_gdn_w_in, m_gdn_conv, m_gdn_a_log, m_gdn_dt_bias, m_gdn_norm_g, m_gdn_w_out, v_ada_w, v_ada_b, v_pre_g, v_post_g, v_rel_bias, v_ab_w_in, v_ab_w_out, v_s5_a_re, v_s5_a_im, v_s5_log_dt, v_s5_b_re, v_s5_b_im, v_s5_c_re, v_s5_c_im, v_s5_d, v_s5_glu_w, v_s5_glu_b, v_gdn_w_in, v_gdn_conv, v_gdn_a_log, v_gdn_dt_bias, v_gdn_norm_g, v_gdn_w_out):
    w = dict(ada_w=ada_w, ada_b=ada_b, pre_g=pre_g, post_g=post_g, rel_bias=rel_bias, ab_w_in=ab_w_in, ab_w_out=ab_w_out,
             s5_a_re=s5_a_re, s5_a_im=s5_a_im, s5_log_dt=s5_log_dt, s5_b_re=s5_b_re, s5_b_im=s5_b_im, s5_c_re=s5_c_re, s5_c_im=s5_c_im,
             s5_d=s5_d, s5_glu_w=s5_glu_w, s5_glu_b=s5_glu_b, gdn_w_in=gdn_w_in, gdn_conv=gdn_conv, gdn_a_log=gdn_a_log,
             gdn_dt_bias=gdn_dt_bias, gdn_norm_g=gdn_norm_g, gdn_w_out=gdn_w_out)
    m = dict(ada_w=m_ada_w, ada_b=m_ada_b, pre_g=m_pre_g, post_g=m_post_g, rel_bias=m_rel_bias, ab_w_in=m_ab_w_in, ab_w_out=m_ab_w_out,
             s5_a_re=m_s5_a_re, s5_a_im=m_s5_a_im, s5_log_dt=m_s5_log_dt, s5_b_re=m_s5_b_re, s5_b_im=m_s5_b_im, s5_c_re=m_s5_c_re,
             s5_c_im=m_s5_c_im, s5_d=m_s5_d, s5_glu_w=m_s5_glu_w, s5_glu_b=m_s5_glu_b, gdn_w_in=m_gdn_w_in, gdn_conv=m_gdn_conv,
             gdn_a_log=m_gdn_a_log, gdn_dt_bias=m_gdn_dt_bias, gdn_norm_g=m_gdn_norm_g, gdn_w_out=m_gdn_w_out)
    v = dict(ada_w=v_ada_w, ada_b=v_ada_b, pre_g=v_pre_g, post_g=v_post_g, rel_bias=v_rel_bias, ab_w_in=v_ab_w_in, ab_w_out=v_ab_w_out,
             s5_a_re=v_s5_a_re, s5_a_im=v_s5_a_im, s5_log_dt=v_s5_log_dt, s5_b_re=v_s5_b_re, s5_b_im=v_s5_b_im, s5_c_re=v_s5_c_re,
             s5_c_im=v_s5_c_im, s5_d=v_s5_d, s5_glu_w=v_s5_glu_w, s5_glu_b=v_s5_glu_b, gdn_w_in=v_gdn_w_in, gdn_conv=v_gdn_conv,
             gdn_a_log=v_gdn_a_log, gdn_dt_bias=v_gdn_dt_bias, gdn_norm_g=v_gdn_norm_g, gdn_w_out=v_gdn_w_out)
    ix, iy, ic = _place()
    me = 4 * ix + 2 * iy + ic
    chip = 2 * ix + iy
    n_cols = ada_w.shape[2]

    mine_first = _pack([c, gdn_conv], 32)
    first = _own_slot(_all_gather8("gather_c_conv", mine_first), mine_first, me)
    c_all = first[:, 0:8].reshape(8, D_MODEL)
    conv_full = first[0::2, 8:32].reshape(4, C_CONV, n_cols).transpose(1, 0, 2).reshape(C_CONV, 4 * n_cols)
    mine_mod = _mod_local(c_all, ada_w)
    modl = _own_slot(_all_gather8("gather_mod", mine_mod), mine_mod, me)
    mod = lax.dynamic_index_in_dim(modl[0::2], me, axis=2, keepdims=False)
    mod = mod.transpose(1, 0, 2).reshape(2, 4 * n_cols) + ada_b

    comm = _WeightExchanges({name: w[name][0] for name in _SHARDED}, ic, chip)
    wd = {name: w[name] for name in _SMALL if name != "ada_b"}
    wd = {k: (a if k in ("pre_g", "post_g", "rel_bias") else a[0]) for k, a in wd.items()}
    wd["gdn_conv"] = conv_full
    wd.update(comm.first_weights())

    loss_local, grad_x, grads, dmod = _local_step(x[0], loss_target[0], mod, wd, comm)
    loss = lax.psum(loss_local, ("x", "y", "c"))

    small_shapes = [w[name].shape for name in _SMALL] + [(C_CONV, 4 * n_cols)]
    small_rows = -(-sum(_rows128(int(np.prod(s))) for s in small_shapes) // 64) * 64
    per_dev, dmod_rows = small_rows // 8, _rows128(2 * 3 * D_MODEL)
    partial = _pack([dmod] + [grads[name] for name in _SMALL[1:]] + [grads["gdn_conv"]], small_rows)
    outbound = jnp.concatenate([partial.reshape(8, per_dev, 128), jnp.broadcast_to(partial[None, :dmod_rows], (8, dmod_rows, 128))], axis=1)
    inbound = _own_slot(_all_to_all8("reduce_small_grads", outbound), lax.dynamic_index_in_dim(outbound, me, 0, keepdims=False), me)
    my_rows = _slot_sum("sum_small_grads", inbound[:, :per_dev])
    g_small = _own_slot(_all_gather8("gather_small_grads", my_rows), my_rows, me).reshape(small_rows, 128)
    g_list = _unpack(g_small, small_shapes)
    out_g, out_d, out_m, out_v = {}, {}, {}, {}

    def update(name, g2d):
        shp = w[name].shape
        two_d = lambda a: a.reshape(-1, shp[-1])
        d_, m_, v_ = _adamw("adamw_" + name, two_d(w[name]), g2d, two_d(m[name]), two_d(v[name]))
        out_g[name], out_d[name], out_m[name], out_v[name] = (a.reshape(shp) for a in (g2d, d_, m_, v_))

    for name, g in zip(_SMALL, g_list[:-1]):
        update(name, g.reshape(-1, g.shape[-1]))
    update("gdn_conv", lax.dynamic_slice_in_dim(g_list[-1], chip * n_cols, n_cols, axis=1))

    dmod_all = inbound[:, per_dev:].reshape(8, 2, 4, n_cols)
    dmod_cols = lax.dynamic_index_in_dim(dmod_all, chip, axis=2, keepdims=False).transpose(1, 0, 2)
    update("ada_w", _ada_w_grad(c_all, dmod_cols).reshape(-1, n_cols))

    comm.received["ab_w_in"] = _chip_exchange("reduce_chips", comm.chip_partials("l0_in", {"ab_w_in": grads["ab_w_in"]}), True)[0]
    chip_1 = jnp.reshape(chip, (1,)).astype(jnp.int32)
    core_1 = jnp.reshape(ic, (1,)).astype(jnp.int32)
    reduced = [_chip_sum("sum_chips_" + name, comm.received[name], comm.partial[name], chip_1) for name in _SHARDED]
    for name, g_mine, g_sib in zip(_SHARDED, reduced, _sibling_exchange("reduce_share", reduced)):
        out_g[name], out_d[name], out_m[name], out_v[name] = _adamw_halves(
            "adamw_" + name, w[name], g_mine, g_sib, m[name], v[name], core_1)

    return (loss, grad_x[None], *[out_g[n] for n in _WEIGHTS], *[out_d[n] for n in _WEIGHTS],
            *[out_m[n] for n in _WEIGHTS], *[out_v[n] for n in _WEIGHTS])
```

```python
import functools
import math

import numpy as np
import jax
import jax.numpy as jnp
from jax import lax
from jax.experimental import pallas as pl
from jax.experimental.pallas import tpu as pltpu

F32 = jnp.float32
BF = jnp.bfloat16
HI = lax.Precision.HIGHEST
MESH = pl.DeviceIdType.MESH

D_MODEL = 1024
EPS = 1e-6
A_HEADS, A_HD, A_WIDTH, A_BLOCK = 8, 64, 512, 128
DILATIONS = (1, 4, 16)
N_KEYS = 128
REL_BUCKETS, REL_MAX_DIST = 32, 2048
B_WIDTH, B_GROUP, B_GROUPS, B_STATE = 512, 16, 32, 64
S5_LANES = 512
S5_TILES = 4
C_HEADS, C_DK, C_CHUNK, C_CONV = 8, 128, 64, 4
QKV = 3072
C_IN_PAD = 4224
TM = 256
VMEM_LIMIT_BYTES = 56 * 1024 * 1024
ADAM_LR, ADAM_B1, ADAM_B2, ADAM_EPS, ADAM_WD, ADAM_STEP = 0.001, 0.9, 0.999, 1e-08, 0.01, 10
NEG = float(np.finfo(np.float32).min)


def _cp(*sem):
    return pltpu.CompilerParams(dimension_semantics=sem, vmem_limit_bytes=VMEM_LIMIT_BYTES)


def _bdot(a, b):
    return jnp.dot(a.astype(BF), b.astype(BF), preferred_element_type=F32)


def _bdot_nt(a, b):
    return lax.dot_general(a.astype(BF), b.astype(BF), (((1,), (1,)), ((), ())), preferred_element_type=F32)


def _bdot_tn(a, b):
    return lax.dot_general(a.astype(BF), b.astype(BF), (((0,), (0,)), ((), ())), preferred_element_type=F32)


def _hdot(a, b):
    return jnp.dot(a, b, precision=HI, preferred_element_type=F32)


def _bein(eq, a, b):
    return jnp.einsum(eq, a.astype(BF), b.astype(BF), preferred_element_type=F32)


def _row(tm, n):
    return pl.BlockSpec((tm, n), lambda i: (i, 0))


def _fix(shape):
    return pl.BlockSpec(shape, lambda i: (0,) * len(shape))


def _sds(*shape, dtype=F32):
    return jax.ShapeDtypeStruct(shape, dtype)


def _acc(ref, val):
    ref[...] += val


def _zero_at_first(refs, axis=0):
    @pl.when(pl.program_id(axis) == 0)
    def _():
        for r in refs:
            r[...] = jnp.zeros_like(r)


def _rms(x):
    return x * lax.rsqrt(jnp.mean(x * x, axis=-1, keepdims=True) + EPS)


def _pre_mod(x, g, scale, shift):
    return (_rms(x) * g) * (1.0 + scale) + shift


def _post_res(y, x, post_g, gate):
    return x + gate * (_rms(y) * post_g)


def _merge_gate(o1, o2, o3, l1, l2, l3, ga):
    m = jnp.maximum(jnp.maximum(l1, l2), l3)
    e1, e2, e3 = jnp.exp(l1 - m), jnp.exp(l2 - m), jnp.exp(l3 - m)
    inv = 1.0 / (e1 + e2 + e3)
    return ((e1 * inv) * o1 + (e2 * inv) * o2 + (e3 * inv) * o3) * jax.nn.silu(ga)


def _s5_gelu(ypre, u, d_skip):
    return jax.nn.gelu(ypre + d_skip * u)


def _s5_glu(yb, gl, gb):
    return yb * jax.nn.sigmoid(gl) * jax.nn.silu(gb)


def _l0_front(x, pre_g, scale, shift, w_in):
    s_len = x.shape[0]

    def body(x_ref, g_ref, sc_ref, sh_ref, w_ref, *out_refs):
        qkv_refs, (u_ref, ga_ref, gb_ref, h_ref) = out_refs[:9], out_refs[9:]
        hb = _pre_mod(x_ref[...], g_ref[...], sc_ref[...], sh_ref[...]).astype(BF)
        h_ref[...] = hb
        z = jnp.dot(hb, w_ref[...], preferred_element_type=F32)
        for a in range(3):
            piece = z[:, a * 512:(a + 1) * 512]
            for i, d in enumerate(DILATIONS):
                qkv_refs[3 * a + i][...] = _to_res(piece, d).astype(BF)
        u_ref[...] = z[:, 1536:2048]
        ga_ref[...] = z[:, 2048:2560]
        gb_ref[...] = z[:, 2560:3072]

    vec = _fix((1, D_MODEL))
    return pl.pallas_call(
        body, name="l0_front", grid=(s_len // TM,),
        in_specs=[_row(TM, D_MODEL), vec, vec, vec, _fix((D_MODEL, 3072))],
        out_specs=[_res_spec(d) for d in DILATIONS] * 3 + [_row(TM, 512)] * 3 + [_row(TM, D_MODEL)],
        out_shape=[_sds(*_res_shape(s_len, d), dtype=BF) for d in DILATIONS] * 3 + [_sds(s_len, 512)] * 3 + [_sds(s_len, D_MODEL, dtype=BF)],
        compiler_params=_cp("arbitrary"),
    )(x, pre_g, scale, shift, w_in)


def _front_bwd(name, x, pre_g, scale, shift, w_in, dres, parts, widths):
    s_len = x.shape[0]
    n_in = sum(len(p) for p in parts)
    n_cols = sum(widths)

    def body(*refs):
        x_ref, g_ref, sc_ref, sh_ref, w_ref, dres_ref = refs[:6]
        part_refs = refs[6:6 + n_in]
        dz_ref, dx_ref, dg_ref, dsc_ref, dsh_ref = refs[6 + n_in:]
        _zero_at_first([dg_ref, dsc_ref, dsh_ref])
        _, vjp = jax.vjp(_pre_mod, x_ref[...], g_ref[...], sc_ref[...], sh_ref[...])
        dh = jnp.zeros((TM, D_MODEL), F32)
        col, at = 0, 0
        for grp, width in zip(parts, widths):
            tile = lambda r: _from_res(r[...]) if len(r.shape) == 3 else r[...]
            dz = tile(part_refs[at])
            for r in part_refs[at + 1:at + len(grp)]:
                dz = dz + tile(r)
            at += len(grp)
            dzb = dz.astype(BF)
            dz_ref[:, col:col + width] = dzb
            dh = dh + lax.dot_general(dzb, w_ref[:, col:col + width], (((1,), (1,)), ((), ())), preferred_element_type=F32)
            col += width
        dx, dg, dsc, dsh = vjp(dh)
        dx_ref[...] = dx + dres_ref[...]
        _acc(dg_ref, dg)
        _acc(dsc_ref, dsc)
        _acc(dsh_ref, dsh)

    vec = _fix((1, D_MODEL))
    flat = [a for p in parts for a in p]
    return pl.pallas_call(
        body, name=name, grid=(s_len // TM,),
        in_specs=[_row(TM, D_MODEL), vec, vec, vec, _fix((D_MODEL, n_cols)), _row(TM, D_MODEL)]
        + [_res_spec(a.shape[0], a.shape[2]) if a.ndim == 3 else _row(TM, a.shape[1]) for a in flat],
        out_specs=[_row(TM, n_cols), _row(TM, D_MODEL), vec, vec, vec],
        out_shape=[_sds(s_len, n_cols, dtype=BF), _sds(s_len, D_MODEL), _sds(1, D_MODEL), _sds(1, D_MODEL), _sds(1, D_MODEL)],
        compiler_params=_cp("arbitrary"),
    )(x, pre_g, scale, shift, w_in, dres, *flat)


def _matmul_tn(name, a, b, tn):
    s_len, k_dim = a.shape
    n_dim = b.shape[1]
    ts = 512

    def body(a_ref, b_ref, o_ref):
        _zero_at_first([o_ref], axis=1)
        o_ref[...] += lax.dot_general(a_ref[...], b_ref[...], (((0,), (0,)), ((), ())), preferred_element_type=F32)

    return pl.pallas_call(
        body, name=name, grid=(n_dim // tn, s_len // ts),
        in_specs=[pl.BlockSpec((ts, k_dim), lambda j, i: (i, 0)), pl.BlockSpec((ts, tn), lambda j, i: (i, j))],
        out_specs=pl.BlockSpec((k_dim, tn), lambda j, i: (0, j)),
        out_shape=_sds(k_dim, n_dim),
        compiler_params=_cp("arbitrary", "arbitrary"),
    )(a, b)


def _t5_bucket_np(dist):
    dist = np.maximum(dist, 0)
    max_exact = REL_BUCKETS // 2
    large = max_exact + (np.log(np.maximum(dist, 1) / max_exact)
                         / math.log(REL_MAX_DIST / max_exact) * (REL_BUCKETS - max_exact)).astype(np.int32)
    large = np.minimum(large, REL_BUCKETS - 1)
    return np.where(dist < max_exact, dist, large).astype(np.int32)


def _to_res(z, dil):
    if dil == 1:
        return z[None]
    return jnp.swapaxes(z.reshape(z.shape[0] // dil, dil, z.shape[1]), 0, 1)


def _from_res(z):
    if z.shape[0] == 1:
        return z[0]
    return jnp.swapaxes(z, 0, 1).reshape(z.shape[0] * z.shape[1], z.shape[2])


def _res_shape(s_len, dil, width=A_WIDTH):
    return (dil, s_len // dil, width)


def _res_spec(dil, width=A_WIDTH):
    return pl.BlockSpec((dil, TM // dil, width), lambda i: (0, i, 0))


def _bucket_table():
    qi = np.arange(A_BLOCK)[:, None]
    kj = np.arange(2 * A_BLOCK)[None, :]
    return np.stack([_t5_bucket_np((qi + A_BLOCK - kj) * d) for d in DILATIONS], 0)


def _attn_mask(first):
    qi = lax.broadcasted_iota(jnp.int32, (A_BLOCK, 2 * A_BLOCK), 0)
    kj = lax.broadcasted_iota(jnp.int32, (A_BLOCK, 2 * A_BLOCK), 1)
    rel = qi + A_BLOCK - kj
    return (rel >= 0) & (rel <= N_KEYS) & (jnp.logical_not(first) | (kj >= A_BLOCK))


def _attn_specs(nb, rev):
    per = 2 if nb % 2 == 0 else 1
    steps = nb // per
    n_of = (lambda i: steps - 1 - i) if rev else (lambda i: i)
    cur = pl.BlockSpec((None, per * A_BLOCK, A_WIDTH), lambda r, i: (r, n_of(i), 0))
    prev = pl.BlockSpec((None, A_BLOCK, A_WIDTH), lambda r, i: (r, jnp.maximum(per * n_of(i) - 1, 0), 0))
    bias = pl.BlockSpec((A_HEADS, A_BLOCK, 2 * A_BLOCK), lambda r, i: (0, 0, 0))
    return per, steps, cur, prev, bias


def _attn_fwd(q, k, v, bias, exchange=None):
    dil, t_len, _ = q.shape
    per, steps, cur, prev, bias_spec = _attn_specs(t_len // A_BLOCK, False)
    scale = A_HD ** -0.5

    def body(q_ref, kp_ref, kc_ref, vp_ref, vc_ref, b_ref, o_ref, l_ref):
        lane = lax.broadcasted_iota(jnp.int32, (1, 128), 1)
        for sub in range(per):
            rows = slice(sub * A_BLOCK, (sub + 1) * A_BLOCK)
            before = slice((sub - 1) * A_BLOCK, sub * A_BLOCK)
            mask = _attn_mask((pl.program_id(1) == 0) if sub == 0 else False)
            for hp in range(A_HEADS // 2):
                sl = slice(hp * 128, (hp + 1) * 128)
                qp = q_ref[rows, sl]
                kw = jnp.concatenate([kp_ref[:, sl] if sub == 0 else kc_ref[before, sl], kc_ref[rows, sl]], axis=0).astype(BF)
                vw = jnp.concatenate([vp_ref[:, sl] if sub == 0 else vc_ref[before, sl], vc_ref[rows, sl]], axis=0).astype(BF)
                outs, lses = [], []
                for j in range(2):
                    hm = (lane < 64) if j == 0 else (lane >= 64)
                    s = _bdot_nt(jnp.where(hm, qp, 0.0), kw) * scale
                    s = jnp.where(mask, s + b_ref[2 * hp + j], NEG)
                    m = jnp.max(s, axis=-1, keepdims=True)
                    p = jnp.exp(s - m)
                    den = jnp.sum(p, axis=-1, keepdims=True)
                    outs.append(_bdot(p, vw) / den)
                    lses.append(m + jnp.log(den))
                hm0 = lane < 64
                o_ref[rows, sl] = jnp.where(hm0, outs[0], outs[1])
                l_ref[rows, sl] = jnp.where(hm0, lses[0], lses[1])

    return _call_with_exchange(body, f"attn_fwd_d{dil}", (dil, steps), [cur, prev, cur, prev, cur, bias_spec], [cur, cur],
                               [_sds(dil, t_len, A_WIDTH)] * 2, [], (q, k, k, v, v, bias), exchange)


def _attn_bwd(q, k, v, bias, o, l, do, dl):
    dil, t_len, _ = q.shape
    per, steps, cur, prev, bias_spec = _attn_specs(t_len // A_BLOCK, True)
    scale = A_HD ** -0.5

    def body(q_ref, kp_ref, kc_ref, vp_ref, vc_ref, b_ref, o_ref, l_ref, do_ref, dl_ref,
             dq_ref, dk_ref, dv_ref, db_ref, ck_ref, cv_ref):
        _zero_at_first([ck_ref, cv_ref], axis=1)

        @pl.when((pl.program_id(0) == 0) & (pl.program_id(1) == 0))
        def _():
            db_ref[...] = jnp.zeros_like(db_ref)

        lane = lax.broadcasted_iota(jnp.int32, (1, 128), 1)
        for hp in range(A_HEADS // 2):
            sl = slice(hp * 128, (hp + 1) * 128)
            to_prev_k, to_prev_v = ck_ref[:, sl], cv_ref[:, sl]
            for sub in range(per - 1, -1, -1):
                rows = slice(sub * A_BLOCK, (sub + 1) * A_BLOCK)
                before = slice((sub - 1) * A_BLOCK, sub * A_BLOCK)
                mask = _attn_mask((pl.program_id(1) == steps - 1) if sub == 0 else False)
                qp = q_ref[rows, sl]
                kw = jnp.concatenate([kp_ref[:, sl] if sub == 0 else kc_ref[before, sl], kc_ref[rows, sl]], axis=0).astype(BF)
                vw = jnp.concatenate([vp_ref[:, sl] if sub == 0 else vc_ref[before, sl], vc_ref[rows, sl]], axis=0).astype(BF)
                op, lp, dop, dlp = o_ref[rows, sl], l_ref[rows, sl], do_ref[rows, sl], dl_ref[rows, sl]
                dq_acc = jnp.zeros((A_BLOCK, 128), F32)
                dk_acc = jnp.zeros((2 * A_BLOCK, 128), F32)
                dv_acc = jnp.zeros((2 * A_BLOCK, 128), F32)
                for j in range(2):
                    hm = (lane < 64) if j == 0 else (lane >= 64)
                    qm = jnp.where(hm, qp, 0.0)
                    s = _bdot_nt(qm, kw) * scale
                    s = jnp.where(mask, s + b_ref[2 * hp + j], NEG)
                    lse = jnp.max(jnp.where(hm, lp, NEG), axis=-1, keepdims=True)
                    p = jnp.exp(s - lse)
                    do_h = jnp.where(hm, dop, 0.0)
                    dd = jnp.sum(do_h * op, axis=-1, keepdims=True)
                    dlse = jnp.sum(jnp.where(hm, dlp, 0.0), axis=-1, keepdims=True)
                    ds = p * (_bdot_nt(do_h, vw) - dd + dlse)
                    dv_acc = dv_acc + _bdot_tn(p, do_h)
                    dq_acc = dq_acc + jnp.where(hm, _bdot(ds, kw), 0.0) * scale
                    dk_acc = dk_acc + _bdot_tn(ds, qm) * scale
                    db_ref[2 * hp + j] += ds
                dq_ref[rows, sl] = dq_acc
                dk_ref[rows, sl] = dk_acc[A_BLOCK:] + to_prev_k
                dv_ref[rows, sl] = dv_acc[A_BLOCK:] + to_prev_v
                to_prev_k, to_prev_v = dk_acc[:A_BLOCK], dv_acc[:A_BLOCK]
            ck_ref[:, sl] = to_prev_k
            cv_ref[:, sl] = to_prev_v

    return pl.pallas_call(
        body, name=f"attn_bwd_d{dil}", grid=(dil, steps),
        in_specs=[cur, prev, cur, prev, cur, bias_spec, cur, cur, cur, cur],
        out_specs=[cur, cur, cur, bias_spec],
        out_shape=[_sds(dil, t_len, A_WIDTH)] * 3 + [_sds(A_HEADS, A_BLOCK, 2 * A_BLOCK)],
        scratch_shapes=[pltpu.VMEM((A_BLOCK, A_WIDTH), F32)] * 2,
        compiler_params=_cp("arbitrary", "arbitrary"),
    )(q, k, k, v, v, bias, o, l, do, dl)


def _attn_bias(rel_bias, table):
    def body(rb_ref, t_ref, *o_refs):
        for c in range(3):
            t = t_ref[c]
            acc = [jnp.zeros((A_BLOCK, 2 * A_BLOCK), F32) for _ in range(A_HEADS)]
            for b in range(REL_BUCKETS):
                hit = t == b
                acc = [jnp.where(hit, rb_ref[b, h], acc[h]) for h in range(A_HEADS)]
            for h in range(A_HEADS):
                o_refs[c][h] = acc[h]

    return pl.pallas_call(body, name="attn_bias", out_shape=[_sds(A_HEADS, A_BLOCK, 2 * A_BLOCK)] * 3,
                          in_specs=[pl.BlockSpec(memory_space=pltpu.SMEM), pl.BlockSpec(memory_space=pltpu.VMEM)],
                          compiler_params=pltpu.CompilerParams(vmem_limit_bytes=VMEM_LIMIT_BYTES))(rel_bias, table)


def _rel_bias_grad(dbs, idx_rows):
    n = A_BLOCK * 2 * A_BLOCK

    def body(d0_ref, d1_ref, d2_ref, idx_ref, o_ref):
        bucket = lax.broadcasted_iota(jnp.int32, (REL_BUCKETS, n), 0).astype(F32)
        acc = jnp.zeros((A_HEADS, REL_BUCKETS), F32)
        for c, db_ref in enumerate((d0_ref, d1_ref, d2_ref)):
            onehot = (idx_ref[c:c + 1, :] == bucket).astype(F32)
            acc = acc + lax.dot_general(db_ref[...], onehot, (((1,), (1,)), ((), ())), precision=HI, preferred_element_type=F32)
        o_ref[...] = acc

    return pl.pallas_call(body, name="rel_bias_grad", out_shape=_sds(A_HEADS, REL_BUCKETS),
                          compiler_params=pltpu.CompilerParams(vmem_limit_bytes=VMEM_LIMIT_BYTES))(
                              *[d.reshape(A_HEADS, n) for d in dbs], idx_rows)


def _s5_param_fn(a_re, a_im, log_dt, bt_re, bt_im):
    dt = jnp.exp(log_dt)
    mag = jnp.exp(dt * a_re)
    abar_r, abar_i = mag * jnp.cos(dt * a_im), mag * jnp.sin(dt * a_im)
    den = a_re * a_re + a_im * a_im
    fr = ((abar_r - 1.0) * a_re + abar_i * a_im) / den
    fi = (abar_i * a_re - (abar_r - 1.0) * a_im) / den
    row = lax.broadcasted_iota(jnp.int32, (B_WIDTH, B_GROUPS), 0)
    grp = lax.broadcasted_iota(jnp.int32, (B_WIDTH, B_GROUPS), 1)
    expand = ((row // B_GROUP) == grp).astype(F32)
    fr_e, fi_e = _hdot(expand, fr), _hdot(expand, fi)
    return abar_r, abar_i, fr_e * bt_re - fi_e * bt_im, fr_e * bt_im + fi_e * bt_re


def _s5_params(a_re, a_im, log_dt, bt_re, bt_im):
    def body(ar, ai, ld, br, bi, o1, o2, o3, o4):
        o1[...], o2[...], o3[...], o4[...] = _s5_param_fn(ar[...], ai[...], ld[...], br[...], bi[...])

    return pl.pallas_call(body, name="s5_params",
                          out_shape=[_sds(B_GROUPS, B_STATE)] * 2 + [_sds(B_WIDTH, B_STATE)] * 2)(a_re, a_im, log_dt, bt_re, bt_im)


def _s5_params_bwd(a_re, a_im, log_dt, bt_re, bt_im, d1, d2, d3, d4):
    def body(ar, ai, ld, br, bi, c1, c2, c3, c4, o1, o2, o3, o4, o5):
        _, vjp = jax.vjp(_s5_param_fn, ar[...], ai[...], ld[...], br[...], bi[...])
        o1[...], o2[...], o3[...], o4[...], o5[...] = vjp((c1[...], c2[...], c3[...], c4[...]))

    return pl.pallas_call(body, name="s5_params_bwd",
                          out_shape=[_sds(B_GROUPS, B_STATE)] * 2 + [_sds(B_GROUPS, 1)] + [_sds(B_WIDTH, B_STATE)] * 2,
                          )(a_re, a_im, log_dt, bt_re, bt_im, d1, d2, d3, d4)


def _pick_row(x, r):
    rows = lax.broadcasted_iota(jnp.int32, x.shape, 0)
    return jnp.sum(jnp.where(rows == r, x, 0.0), axis=0, keepdims=True)


S5_SEG = 8
S5_STEPS = 32
S5_WIDTH = S5_TILES * S5_LANES


def _seg_rows(block):
    return jnp.swapaxes(block, 0, 1).reshape(block.shape[1] * S5_SEG, block.shape[2])


def _seg_block(rows):
    return jnp.swapaxes(rows.reshape(rows.shape[0] // S5_SEG, S5_SEG, rows.shape[1]), 0, 1)


def _seq_specs(n_i, rev):
    at = (lambda i: n_i - 1 - i) if rev else (lambda i: i)
    seg = pl.BlockSpec((S5_SEG, S5_STEPS, B_WIDTH), lambda i: (0, at(i), 0))
    x_spec = pl.BlockSpec((S5_SEG * S5_STEPS, S5_WIDTH), lambda i: (at(i), 0))
    return seg, x_spec, _fix((S5_TILES, 128, S5_LANES)), _fix((S5_TILES, S5_LANES, 128)), _fix((1, S5_WIDTH)), _fix((S5_SEG, S5_WIDTH))


def _tile_dots(dot, lhs, w_ref, lhs_width):
    return jnp.concatenate([dot(lhs[:, t * lhs_width:(t + 1) * lhs_width], w_ref[t]) for t in range(S5_TILES)], axis=1)


def _s5_entries(name, end_r, end_i, abr, abi, steps, reverse):
    def body(er_ref, ei_ref, ar_ref, ai_ref, or_ref, oi_ref):
        pr, pi_ = ar_ref[...], ai_ref[...]
        for _ in range(int(math.log2(steps))):
            pr, pi_ = pr * pr - pi_ * pi_, 2.0 * pr * pi_
        er, ei = er_ref[...], ei_ref[...]
        rows = lax.broadcasted_iota(jnp.int32, er.shape, 0)
        cr, ci = jnp.zeros_like(pr), jnp.zeros_like(pr)
        out_r, out_i = jnp.zeros_like(er), jnp.zeros_like(er)
        for g in (range(S5_SEG - 2, -1, -1) if reverse else range(1, S5_SEG)):
            src = g + 1 if reverse else g - 1
            cr, ci = _pick_row(er, src) + pr * cr - pi_ * ci, _pick_row(ei, src) + pr * ci + pi_ * cr
            out_r, out_i = jnp.where(rows == g, cr, out_r), jnp.where(rows == g, ci, out_i)
        or_ref[...] = out_r
        oi_ref[...] = out_i

    return pl.pallas_call(body, name=name, out_shape=[_sds(*end_r.shape)] * 2)(end_r, end_i, abr, abi)


def _s5_seq_fwd(u, btr, bti, ctr, cti, abr, abi, entry, store, exchange=None):
    s_len = u.shape[0]
    seg_len = s_len // S5_SEG
    n_i = seg_len // S5_STEPS
    rows = S5_SEG * S5_STEPS

    def body(u_ref, btr_ref, bti_ref, ctr_ref, cti_ref, ar_ref, ai_ref, er_ref, ei_ref, *rest):
        if store:
            xr_ref, xi_ref, y_ref, endr_ref, endi_ref, sr_ref, si_ref = rest
        else:
            endr_ref, endi_ref, sr_ref, si_ref = rest
        i = pl.program_id(0)

        @pl.when(i == 0)
        def _():
            sr_ref[...] = er_ref[...]
            si_ref[...] = ei_ref[...]

        ar = jnp.broadcast_to(ar_ref[...], (S5_SEG, S5_WIDTH))
        ai = jnp.broadcast_to(ai_ref[...], (S5_SEG, S5_WIDTH))
        ub = _seg_rows(u_ref[...])
        br, bi = _tile_dots(_bdot, ub, btr_ref, 128), _tile_dots(_bdot, ub, bti_ref, 128)
        sr, si = sr_ref[...], si_ref[...]
        for s in range(S5_STEPS):
            at = slice(S5_SEG * s, S5_SEG * (s + 1))
            sr, si = ar * sr - ai * si + br[at], ar * si + ai * sr + bi[at]
            if store:
                xr_ref[at, :] = sr
                xi_ref[at, :] = si
        sr_ref[...] = sr
        si_ref[...] = si
        if store:
            y_ref[...] = _seg_block(_tile_dots(_bdot, xr_ref[...], ctr_ref, S5_LANES) - _tile_dots(_bdot, xi_ref[...], cti_ref, S5_LANES))

        @pl.when(i == n_i - 1)
        def _():
            endr_ref[...] = sr
            endi_ref[...] = si

    seg, x_spec, b_spec, c_spec, a_spec, e_spec = _seq_specs(n_i, False)
    ends = [_sds(S5_SEG, S5_WIDTH)] * 2
    full = [_sds(s_len, S5_WIDTH)] * 2 + [_sds(S5_SEG, seg_len, B_WIDTH)] if store else []
    return _call_with_exchange(
        body, "s5_scan_fwd" if store else "s5_ends_fwd", (n_i,),
        [seg, b_spec, b_spec, c_spec, c_spec, a_spec, a_spec, e_spec, e_spec],
        ([x_spec, x_spec, seg] if store else []) + [e_spec, e_spec], full + ends,
        [pltpu.VMEM((S5_SEG, S5_WIDTH), F32)] * 2,
        (u.reshape(S5_SEG, seg_len, B_WIDTH), btr, bti, ctr, cti, abr, abi, *entry), exchange)


def _s5_seq_bwd(dy, xr, xi, u, btr, bti, ctr, cti, abr, abi, g_entry, x_entry, full, exchange=None):
    s_len = dy.shape[0]
    seg_len = s_len // S5_SEG
    n_i = seg_len // S5_STEPS
    rows = S5_SEG * S5_STEPS

    def body(*refs):
        if full:
            (dy_ref, btr_ref, bti_ref, ctr_ref, cti_ref, ar_ref, ai_ref, ger_ref, gei_ref,
             xr_ref, xi_ref, xrp_ref, xip_ref, xer_ref, xei_ref, u_ref,
             du_ref, dbtr_ref, dbti_ref, dctr_ref, dcti_ref, dar_ref, dai_ref, str_ref, sti_ref,
             sr_ref, si_ref, gr_s, gi_s) = refs
        else:
            (dy_ref, btr_ref, bti_ref, ctr_ref, cti_ref, ar_ref, ai_ref, ger_ref, gei_ref, str_ref, sti_ref, sr_ref, si_ref) = refs
        i = pl.program_id(0)

        @pl.when(i == 0)
        def _():
            sr_ref[...] = ger_ref[...]
            si_ref[...] = gei_ref[...]
            if full:
                for r in (dbtr_ref, dbti_ref, dctr_ref, dcti_ref, dar_ref, dai_ref):
                    r[...] = jnp.zeros_like(r)

        ar = jnp.broadcast_to(ar_ref[...], (S5_SEG, S5_WIDTH))
        ai = -jnp.broadcast_to(ai_ref[...], (S5_SEG, S5_WIDTH))
        dyb = _seg_rows(dy_ref[...])
        gr, gi = _tile_dots(_bdot_nt, dyb, ctr_ref, 128), -_tile_dots(_bdot_nt, dyb, cti_ref, 128)
        sr, si = sr_ref[...], si_ref[...]
        for s in range(S5_STEPS - 1, -1, -1):
            at = slice(S5_SEG * s, S5_SEG * (s + 1))
            sr, si = ar * sr - ai * si + gr[at], ar * si + ai * sr + gi[at]
            if full:
                gr_s[at, :] = sr
                gi_s[at, :] = si
        sr_ref[...] = sr
        si_ref[...] = si

        @pl.when(i == n_i - 1)
        def _():
            str_ref[...] = sr
            sti_ref[...] = si

        if full:
            g_r, g_i = gr_s[...], gi_s[...]
            du_ref[...] = _seg_block(_tile_dots(_bdot_nt, g_r, btr_ref, S5_LANES) + _tile_dots(_bdot_nt, g_i, bti_ref, S5_LANES))
            ub = _seg_rows(u_ref[...])
            xr_b, xi_b = xr_ref[...], xi_ref[...]
            for t in range(S5_TILES):
                lanes, cols = slice(t * S5_LANES, (t + 1) * S5_LANES), slice(t * 128, (t + 1) * 128)
                dbtr_ref[t] += _bdot_tn(ub[:, cols], g_r[:, lanes])
                dbti_ref[t] += _bdot_tn(ub[:, cols], g_i[:, lanes])
                dctr_ref[t] += _bdot_tn(xr_b[:, lanes], dyb[:, cols])
                dcti_ref[t] -= _bdot_tn(xi_b[:, lanes], dyb[:, cols])
            first = i == n_i - 1
            xpr = jnp.concatenate([jnp.where(first, xer_ref[...], xrp_ref[...]), xr_b[:rows - S5_SEG]], axis=0)
            xpi = jnp.concatenate([jnp.where(first, xei_ref[...], xip_ref[...]), xi_b[:rows - S5_SEG]], axis=0)
            dar_ref[...] += jnp.sum(g_r * xpr + g_i * xpi, axis=0, keepdims=True)
            dai_ref[...] += jnp.sum(g_i * xpr - g_r * xpi, axis=0, keepdims=True)

    seg, x_spec, b_spec, c_spec, a_spec, e_spec = _seq_specs(n_i, True)
    halo = pl.BlockSpec((S5_SEG, S5_WIDTH), lambda i: (jnp.maximum((n_i - 1 - i) * S5_STEPS - 1, 0), 0))
    starts = [_sds(S5_SEG, S5_WIDTH)] * 2
    in_specs = [seg, b_spec, b_spec, c_spec, c_spec, a_spec, a_spec, e_spec, e_spec]
    args = [dy.reshape(S5_SEG, seg_len, B_WIDTH), btr, bti, ctr, cti, abr, abi, *g_entry]
    state = [pltpu.VMEM((S5_SEG, S5_WIDTH), F32)] * 2
    if not full:
        return _call_with_exchange(body, "s5_starts_bwd", (n_i,), in_specs, [e_spec, e_spec], starts, state, args, None)
    return _call_with_exchange(
        body, "s5_scan_bwd", (n_i,),
        in_specs + [x_spec, x_spec, halo, halo, e_spec, e_spec, seg],
        [seg, b_spec, b_spec, c_spec, c_spec, a_spec, a_spec, e_spec, e_spec],
        [_sds(S5_SEG, seg_len, B_WIDTH)] + [_sds(S5_TILES, 128, S5_LANES)] * 2 + [_sds(S5_TILES, S5_LANES, 128)] * 2
        + [_sds(1, S5_WIDTH)] * 2 + starts,
        state + [pltpu.VMEM((rows, S5_WIDTH), F32)] * 2,
        args + [xr, xi, xr, xi, *x_entry, u.reshape(S5_SEG, seg_len, B_WIDTH)], exchange)


def _blockdiag_b(bbar_t):
    blocks = bbar_t.reshape(S5_TILES, 8, B_GROUP, B_STATE)
    return jnp.einsum('jgmp,gh->jgmhp', blocks, jnp.eye(8, dtype=F32)).reshape(S5_TILES, 128, S5_LANES)


def _blockdiag_b_t(d):
    return jnp.einsum('jgmgp->jgmp', d.reshape(S5_TILES, 8, B_GROUP, 8, B_STATE)).reshape(B_WIDTH, B_STATE)


def _blockdiag_c(c):
    blocks = c.reshape(S5_TILES, 8, B_GROUP, B_STATE)
    return jnp.einsum('jgmp,gh->jhpgm', blocks, jnp.eye(8, dtype=F32)).reshape(S5_TILES, S5_LANES, 128)


def _blockdiag_c_t(d):
    return jnp.einsum('jgpgm->jgmp', d.reshape(S5_TILES, 8, B_STATE, 8, B_GROUP)).reshape(B_GROUPS, B_GROUP, B_STATE)


def _l0_out(os, ls, ga, gb, ypre, u, x, d_skip, glu_w, glu_b, w_out, post_g, gate):
    s_len = x.shape[0]

    def body(o0, o1, o2, l0, l1, l2, ga_ref, gb_ref, yp_ref, u_ref, x_ref, d_ref, gw_ref, gbias_ref, w_ref, pg_ref, gt_ref, x1_ref, y_ref):
        oa = _merge_gate(*[_from_res(r[...]) for r in (o0, o1, o2, l0, l1, l2)], ga_ref[...])
        yb = _s5_gelu(yp_ref[...], u_ref[...], d_ref[...])
        ob = _s5_glu(yb, _bdot(yb, gw_ref[...]) + gbias_ref[...], gb_ref[...])
        y = _bdot(oa, w_ref[0:512, :]) + _bdot(ob, w_ref[512:1024, :])
        y_ref[...] = y
        x1_ref[...] = _post_res(y, x_ref[...], pg_ref[...], gt_ref[...])

    vec, half = _fix((1, D_MODEL)), _fix((1, 512))
    return pl.pallas_call(
        body, name="l0_out", grid=(s_len // TM,),
        in_specs=[_res_spec(d) for d in DILATIONS] * 2 + [_row(TM, 512)] * 4
        + [_row(TM, D_MODEL), half, _fix((512, 512)), half, _fix((D_MODEL, D_MODEL)), vec, vec],
        out_specs=[_row(TM, D_MODEL)] * 2,
        out_shape=[_sds(s_len, D_MODEL)] * 2,
        compiler_params=_cp("arbitrary"),
    )(*os, *ls, ga, gb, ypre, u, x, d_skip, glu_w, glu_b, w_out, post_g, gate)


def _l0_out_bwd(os, ls, ga, gb, ypre, u, x, y, d_skip, glu_w, glu_b, w_out, post_g, gate, dx1, exchange=None):
    s_len = x.shape[0]

    def body(o0, o1, o2, l0, l1, l2, ga_ref, gb_ref, yp_ref, u_ref, x_ref, y_ref, d_ref, gw_ref, gbias_ref, w_ref, pg_ref, gt_ref, dx1_ref,
             do0, do1, do2, dl0, dl1, dl2, dga_ref, dgb_ref, dyp_ref, du_ref, dd_ref, dgw_ref, dgbias_ref, dw_ref, dpg_ref, dgt_ref):
        _zero_at_first([dd_ref, dgw_ref, dgbias_ref, dw_ref, dpg_ref, dgt_ref])
        _, vjp2 = jax.vjp(_post_res, y_ref[...], x_ref[...], pg_ref[...], gt_ref[...])
        dy, _, dpg, dgt = vjp2(dx1_ref[...])
        _acc(dpg_ref, dpg)
        _acc(dgt_ref, dgt)
        oa, vjp_a = jax.vjp(_merge_gate, *[_from_res(r[...]) for r in (o0, o1, o2, l0, l1, l2)], ga_ref[...])
        yb, vjp_g = jax.vjp(_s5_gelu, yp_ref[...], u_ref[...], d_ref[...])
        gl = _bdot(yb, gw_ref[...]) + gbias_ref[...]
        ob, vjp_b = jax.vjp(_s5_glu, yb, gl, gb_ref[...])
        dw_ref[0:512, :] += _bdot_tn(oa, dy)
        dw_ref[512:1024, :] += _bdot_tn(ob, dy)
        d1, d2, d3, e1, e2, e3, dga = vjp_a(_bdot_nt(dy, w_ref[0:512, :]))
        for ref, val, d in zip((do0, do1, do2, dl0, dl1, dl2), (d1, d2, d3, e1, e2, e3), DILATIONS * 2):
            ref[...] = _to_res(val, d)
        dga_ref[...] = dga
        dyb, dgl, dgb = vjp_b(_bdot_nt(dy, w_ref[512:1024, :]))
        dgb_ref[...] = dgb
        dgw_ref[...] += _bdot_tn(yb, dgl)
        _acc(dgbias_ref, jnp.sum(dgl, axis=0, keepdims=True))
        dyp, du, dd = vjp_g(dyb + _bdot_nt(dgl, gw_ref[...]))
        dyp_ref[...] = dyp
        du_ref[...] = du
        _acc(dd_ref, dd)

    vec, half = _fix((1, D_MODEL)), _fix((1, 512))
    r5, r10 = _row(TM, 512), _row(TM, D_MODEL)
    res6 = [_res_spec(d) for d in DILATIONS] * 2
    return _call_with_exchange(
        body, "l0_out_bwd", (s_len // TM,),
        res6 + [r5] * 4 + [r10, r10, half, _fix((512, 512)), half, _fix((D_MODEL, D_MODEL)), vec, vec, r10],
        res6 + [r5] * 4 + [half, _fix((512, 512)), half, _fix((D_MODEL, D_MODEL)), vec, vec],
        [_sds(*_res_shape(s_len, d)) for d in DILATIONS] * 2 + [_sds(s_len, 512)] * 4
        + [_sds(1, 512), _sds(512, 512), _sds(1, 512), _sds(D_MODEL, D_MODEL), _sds(1, D_MODEL), _sds(1, D_MODEL)],
        [], (*os, *ls, ga, gb, ypre, u, x, y, d_skip, glu_w, glu_b, w_out, post_g, gate, dx1), exchange)


def _l1_front(x, pre_g, scale, shift, w_in):
    s_len = x.shape[0]

    def body(x_ref, g_ref, sc_ref, sh_ref, w_ref, raw_ref, gate_ref, ba_ref, h_ref):
        hb = _pre_mod(x_ref[...], g_ref[...], sc_ref[...], sh_ref[...]).astype(BF)
        h_ref[...] = hb
        z = jnp.dot(hb, w_ref[...], preferred_element_type=F32)
        raw_ref[...] = z[:, 0:QKV]
        gate_ref[...] = z[:, QKV:QKV + 1024]
        ba_ref[...] = z[:, QKV + 1024:C_IN_PAD]

    vec = _fix((1, D_MODEL))
    return pl.pallas_call(
        body, name="l1_front", grid=(s_len // TM,),
        in_specs=[_row(TM, D_MODEL), vec, vec, vec, _fix((D_MODEL, C_IN_PAD))],
        out_specs=[_row(TM, QKV), _row(TM, 1024), _row(TM, 128), _row(TM, D_MODEL)],
        out_shape=[_sds(s_len, QKV), _sds(s_len, 1024), _sds(s_len, 128), _sds(s_len, D_MODEL, dtype=BF)],
        compiler_params=_cp("arbitrary"),
    )(x, pre_g, scale, shift, w_in)


def _bg_fn(ba, alog_row, dtb_row):
    lane = lax.broadcasted_iota(jnp.int32, (1, 128), 1)
    g = -jnp.exp(alog_row) * jax.nn.softplus(ba + dtb_row)
    return jnp.where(lane < C_HEADS, jax.nn.sigmoid(ba), jnp.where(lane < 2 * C_HEADS, g, 0.0))


def _act_q(c):
    q = jax.nn.silu(c)
    return q * lax.rsqrt(jnp.sum(q * q, axis=-1, keepdims=True) + EPS) * (C_DK ** -0.5)


def _act_k(c):
    k = jax.nn.silu(c)
    return k * lax.rsqrt(jnp.sum(k * k, axis=-1, keepdims=True) + EPS)


def _act_of(s):
    return _act_q if s < 8 else (_act_k if s < 16 else jax.nn.silu)


def _conv_taps(prev8, tile_ref, sl, next8=None):
    rows = tile_ref.shape[0]
    head = jnp.concatenate([prev8, tile_ref[0:8, sl]], axis=0)
    tail = None if next8 is None else jnp.concatenate([tile_ref[rows - 8:rows, sl], next8], axis=0)
    taps = []
    for j in range(C_CONV):
        shift = C_CONV - 1 - j
        pieces = [head[8:] if shift == 0 else pltpu.roll(head, shift, 0)[8:], tile_ref[pl.ds(8 - shift, rows - 8), sl]]
        if tail is not None:
            pieces.append(tail[8:] if shift == 0 else pltpu.roll(tail, shift, 0)[8:])
        taps.append(jnp.concatenate(pieces, axis=0))
    return taps


def _gdn_prep(raw, ba, conv_w, alog_row, dtb_row):
    s_len = raw.shape[0]

    def body(raw_ref, halo_ref, ba_ref, w_ref, al_ref, dt_ref, qkv_ref, bg_ref):
        bg_ref[...] = _bg_fn(ba_ref[...], al_ref[...], dt_ref[...])
        has_prev = (pl.program_id(0) > 0).astype(F32)
        for s in range(24):
            sl = slice(s * 128, (s + 1) * 128)
            taps = _conv_taps(halo_ref[:, sl] * has_prev, raw_ref, sl)
            conv = w_ref[3:4, sl] * taps[3]
            for j in range(3):
                conv = conv + w_ref[j:j + 1, sl] * taps[j]
            qkv_ref[:, sl] = _act_of(s)(conv)

    halo = pl.BlockSpec((8, QKV), lambda i: (jnp.maximum(i * (TM // 8) - 1, 0), 0))
    row128 = _fix((1, 128))
    return pl.pallas_call(
        body, name="gdn_prep", grid=(s_len // TM,),
        in_specs=[_row(TM, QKV), halo, _row(TM, 128), _fix((C_CONV, QKV)), row128, row128],
        out_specs=[_row(TM, QKV), _row(TM, 128)],
        out_shape=[_sds(s_len, QKV), _sds(s_len, 128)],
        compiler_params=_cp("arbitrary"),
    )(raw, raw, ba, conv_w, alog_row, dtb_row)


def _gdn_prep_bwd(raw, ba, conv_w, alog_row, dtb_row, dq, dk, dv, dbg):
    s_len = raw.shape[0]
    n_tiles = s_len // TM

    def body(raw_ref, prev_ref, next_ref, ba_ref, w_ref, al_ref, dt_ref, dq_ref, dqn_ref, dk_ref, dkn_ref, dv_ref, dvn_ref, dbg_ref,
             draw_ref, dba_ref, dw_ref, dal_ref, ddt_ref, dconv_ref):
        _zero_at_first([dw_ref, dal_ref, ddt_ref])
        i = pl.program_id(0)
        _, vjp_bg = jax.vjp(_bg_fn, ba_ref[...], al_ref[...], dt_ref[...])
        dba, dal, ddt = vjp_bg(dbg_ref[...])
        dba_ref[...] = dba
        _acc(dal_ref, dal)
        _acc(ddt_ref, ddt)
        has_prev = (i > 0).astype(F32)
        has_next = (i < n_tiles - 1).astype(F32)
        ct_refs = ((dq_ref, dqn_ref), (dk_ref, dkn_ref), (dv_ref, dvn_ref))
        for s in range(24):
            sl = slice(s * 128, (s + 1) * 128)
            hl = slice((s % 8) * 128, (s % 8 + 1) * 128)
            tile_ref, nxt_ref = ct_refs[s // 8]
            taps = _conv_taps(prev_ref[:, sl] * has_prev, raw_ref, sl, next_ref[:, sl] * has_next)
            conv = w_ref[3:4, sl] * taps[3]
            for j in range(3):
                conv = conv + w_ref[j:j + 1, sl] * taps[j]
            ct = jnp.concatenate([tile_ref[:, hl], nxt_ref[:, hl] * has_next], axis=0)
            _, vjp_act = jax.vjp(_act_of(s), conv)
            dconv, = vjp_act(ct)
            dconv_ref[...] = dconv
            draw = w_ref[3:4, sl] * dconv[:TM]
            for j in range(3):
                draw = draw + w_ref[j:j + 1, sl] * dconv_ref[pl.ds(3 - j, TM), :]
            draw_ref[:, sl] = draw
            for j in range(4):
                dw_ref[j:j + 1, sl] += jnp.sum(dconv[:TM] * taps[j][:TM], axis=0, keepdims=True)

    prev = pl.BlockSpec((8, QKV), lambda i: (jnp.maximum(i * (TM // 8) - 1, 0), 0))
    nxt = lambda n: pl.BlockSpec((8, n), lambda i: (jnp.minimum((i + 1) * (TM // 8), s_len // 8 - 1), 0))
    row128 = _fix((1, 128))
    ct_specs = [_row(TM, 1024), nxt(1024)] * 3
    return pl.pallas_call(
        body, name="gdn_prep_bwd", grid=(n_tiles,),
        in_specs=[_row(TM, QKV), prev, nxt(QKV), _row(TM, 128), _fix((C_CONV, QKV)), row128, row128] + ct_specs + [_row(TM, 128)],
        out_specs=[_row(TM, QKV), _row(TM, 128), _fix((C_CONV, QKV)), row128, row128],
        out_shape=[_sds(s_len, QKV), _sds(s_len, 128), _sds(C_CONV, QKV), _sds(1, 128), _sds(1, 128)],
        scratch_shapes=[pltpu.VMEM((TM + 8, 128), F32)],
        compiler_params=_cp("arbitrary"),
    )(raw, raw, raw, ba, conv_w, alog_row, dtb_row, dq, dq, dk, dk, dv, dv, dbg)


def _tein(eq, a, b):
    return jnp.einsum(eq, a, b, precision=lax.Precision.HIGH, preferred_element_type=F32)


def _unit_lower_inverse(lower):
    ri = lax.broadcasted_iota(jnp.int32, (C_CHUNK, C_CHUNK), 0)
    ci = lax.broadcasted_iota(jnp.int32, (C_CHUNK, C_CHUNK), 1)
    eye = (ri == ci).astype(F32)[None]
    p_mat = -lower
    inv = eye + p_mat
    for _ in range(5):
        p_mat = _bein('hij,hjk->hik', p_mat, p_mat)
        inv = inv + _bein('hij,hjk->hik', inv, p_mat)
    inv = _tein('hij,hjk->hik', inv, 2.0 * eye - _tein('hij,hjk->hik', eye + lower, inv))
    return jnp.where((ri >= ci)[None], inv, 0.0)


@jax.custom_vjp
def _known_inverse(lower, inv):
    return inv


def _known_inverse_fwd(lower, inv):
    return inv, inv


def _known_inverse_bwd(inv, d_inv):
    d_lower = -_bein('hik,hjk->hij', _bein('hji,hjk->hik', inv, d_inv), inv)
    return d_lower, jnp.zeros_like(inv)


_known_inverse.defvjp(_known_inverse_fwd, _known_inverse_bwd)


def _gdn_local(q, k, v, bgs, inv_known=None):
    lane = lax.broadcasted_iota(jnp.int32, (1, 128), 1)
    ri = lax.broadcasted_iota(jnp.int32, (C_CHUNK, C_CHUNK), 0)
    ci = lax.broadcasted_iota(jnp.int32, (C_CHUNK, C_CHUNK), 1)
    row_id = lax.broadcasted_iota(jnp.int32, (128, C_CHUNK), 0)
    beta, gc, gcj = [], [], []
    for bg in bgs:
        gc_t = _hdot((ri >= ci).astype(F32), bg)
        gc_rows = gc_t.T
        for h in range(C_HEADS):
            beta.append(jnp.sum(jnp.where(lane == h, bg, 0.0), axis=-1, keepdims=True))
            gc.append(jnp.sum(jnp.where(lane == C_HEADS + h, gc_t, 0.0), axis=-1, keepdims=True))
            gcj.append(jnp.sum(jnp.where(row_id == C_HEADS + h, gc_rows, 0.0), axis=0, keepdims=True))
    beta, gc, gcj = jnp.stack(beta, axis=0), jnp.stack(gc, axis=0), jnp.stack(gcj, axis=0)
    tril, strict = (ri >= ci)[None], (ri > ci)[None]
    decay = jnp.exp(jnp.where(tril, gc - gcj, -1e30))
    kb = k * beta
    lower = jnp.where(strict, _bein('hid,hjd->hij', kb, k) * decay, 0.0)
    inv = _unit_lower_inverse(lower) if inv_known is None else _known_inverse(lower, inv_known)
    egc = jnp.exp(gc)
    u_c = _bein('hij,hjd->hid', inv, v * beta)
    w_c = _bein('hij,hjd->hid', inv, kb * egc)
    aqk = _bein('hid,hjd->hij', q, k) * decay
    rowi = lax.broadcasted_iota(jnp.int32, (1, C_CHUNK, 1), 1)
    g_last = jnp.sum(jnp.where(rowi == C_CHUNK - 1, gc, 0.0), axis=1, keepdims=True)
    kd = k * jnp.exp(g_last - gc)
    return (u_c, w_c, aqk, q * egc, kd, jnp.exp(g_last)), inv


def _gdn_state(local, state):
    u_c, w_c, aqk, qg, kd, dec = local
    v_new = u_c - _bein('hik,hkv->hiv', w_c, state)
    o = _bein('hik,hkv->hiv', qg, state) + _bein('hij,hjv->hiv', aqk, v_new)
    return o, state * dec + _bein('hik,hiv->hkv', kd, v_new)


C_SUB = 4


def _gdn_group(q, k, v, bgs, state, inv_known=None):
    local, inv = _gdn_local(q, k, v, bgs, inv_known)
    outs = []
    for s in range(len(bgs)):
        o, state = _gdn_state(tuple(t[s * C_HEADS:(s + 1) * C_HEADS] for t in local), state)
        outs.append(o)
    return outs, state, inv


def _heads(ref):
    return jnp.stack([ref[s * C_CHUNK:(s + 1) * C_CHUNK, h * C_DK:(h + 1) * C_DK] for s in range(C_SUB) for h in range(C_HEADS)], axis=0)


def _put_heads(ref, sub, val):
    rows = slice(sub * C_CHUNK, (sub + 1) * C_CHUNK)
    for h in range(C_HEADS):
        ref[rows, h * C_DK:(h + 1) * C_DK] = val[h]


def _gdn_specs(s_len, rev):
    rows = C_SUB * C_CHUNK
    n_g = s_len // rows
    at = (lambda i: n_g - 1 - i) if rev else (lambda i: i)
    col = lambda c: pl.BlockSpec((rows, 1024), lambda i: (at(i), c))
    row128 = pl.BlockSpec((rows, 128), lambda i: (at(i), 0))
    state = pl.BlockSpec((1, C_HEADS, C_DK, C_DK), lambda i: (at(i), 0, 0, 0))
    inv = pl.BlockSpec((1, C_SUB * C_HEADS, C_CHUNK, C_CHUNK), lambda i: (at(i), 0, 0, 0))
    return n_g, col, row128, state, inv


def _gdn_fwd(qkv, bg):
    s_len = qkv.shape[0]
    n_g, col, row128, state_spec, inv_spec = _gdn_specs(s_len, False)

    def body(q_ref, k_ref, v_ref, bg_ref, o_ref, ss_ref, inv_ref, st_ref):
        _zero_at_first([st_ref])
        s0 = st_ref[...]
        ss_ref[0] = s0
        bgs = [bg_ref[s * C_CHUNK:(s + 1) * C_CHUNK, :] for s in range(C_SUB)]
        outs, s2, inv = _gdn_group(_heads(q_ref), _heads(k_ref), _heads(v_ref), bgs, s0)
        st_ref[...] = s2
        inv_ref[0] = inv
        for s in range(C_SUB):
            _put_heads(o_ref, s, outs[s])

    return pl.pallas_call(
        body, name="gdn_fwd", grid=(n_g,),
        in_specs=[col(0), col(1), col(2), row128],
        out_specs=[col(0), state_spec, inv_spec],
        out_shape=[_sds(s_len, 1024), _sds(n_g, C_HEADS, C_DK, C_DK), _sds(n_g, C_SUB * C_HEADS, C_CHUNK, C_CHUNK)],
        scratch_shapes=[pltpu.VMEM((C_HEADS, C_DK, C_DK), F32)],
        compiler_params=_cp("arbitrary"),
    )(qkv, qkv, qkv, bg)


def _gdn_bwd(qkv, bg, states, invs, do):
    s_len = qkv.shape[0]
    n_g, col, row128, state_spec, inv_spec = _gdn_specs(s_len, True)

    def body(q_ref, k_ref, v_ref, bg_ref, ss_ref, inv_ref, do_ref, dq_ref, dk_ref, dv_ref, dbg_ref, ds_ref):
        _zero_at_first([ds_ref])
        inv_known = inv_ref[0]

        def group(q, k, v, bgs, st):
            outs, st2, _ = _gdn_group(q, k, v, bgs, st, inv_known)
            return outs, st2

        bgs = [bg_ref[s * C_CHUNK:(s + 1) * C_CHUNK, :] for s in range(C_SUB)]
        _, vjp = jax.vjp(group, _heads(q_ref), _heads(k_ref), _heads(v_ref), bgs, ss_ref[0])
        douts = [jnp.stack([do_ref[s * C_CHUNK:(s + 1) * C_CHUNK, h * C_DK:(h + 1) * C_DK] for h in range(C_HEADS)], axis=0)
                 for s in range(C_SUB)]
        dq, dk, dv, dbgs, ds = vjp((douts, ds_ref[...]))
        ds_ref[...] = ds
        for s in range(C_SUB):
            dbg_ref[s * C_CHUNK:(s + 1) * C_CHUNK, :] = dbgs[s]
            for ref, val in ((dq_ref, dq), (dk_ref, dk), (dv_ref, dv)):
                _put_heads(ref, s, val[s * C_HEADS:(s + 1) * C_HEADS])

    return pl.pallas_call(
        body, name="gdn_bwd", grid=(n_g,),
        in_specs=[col(0), col(1), col(2), row128, state_spec, inv_spec, col(0)],
        out_specs=[col(0), col(0), col(0), row128],
        out_shape=[_sds(s_len, 1024)] * 3 + [_sds(s_len, 128)],
        scratch_shapes=[pltpu.VMEM((C_HEADS, C_DK, C_DK), F32)],
        compiler_params=_cp("arbitrary"),
    )(qkv, qkv, qkv, bg, states, invs, do)


def _head_norm_gate(o, gate, norm_g):
    return (_rms(o) * norm_g) * jax.nn.silu(gate)


def _l1_out_fb(o, gate_c, x1, target, norm_g, w_out, post_g, gate):
    s_len = x1.shape[0]

    def body(o_ref, gc_ref, x1_ref, t_ref, ng_ref, w_ref, pg_ref, gt_ref,
             loss_ref, dres_ref, do_ref, dgc_ref, dw_ref, dng_ref, dpg_ref, dgt_ref):
        _zero_at_first([loss_ref, dw_ref, dng_ref, dpg_ref, dgt_ref])
        ng = ng_ref[...]
        ons, vjps = [], []
        for h in range(C_HEADS):
            sl = slice(h * C_DK, (h + 1) * C_DK)
            on, vjp_h = jax.vjp(_head_norm_gate, o_ref[:, sl], gc_ref[:, sl], ng)
            ons.append(on)
            vjps.append(vjp_h)
        on_all = jnp.concatenate(ons, axis=-1)
        y = _bdot(on_all, w_ref[...])
        x2, vjp2 = jax.vjp(_post_res, y, x1_ref[...], pg_ref[...], gt_ref[...])
        err = x2 - t_ref[...]
        _acc(loss_ref, jnp.full((1, 128), 0.5 * jnp.sum(jnp.mean(err * err, axis=-1)), F32))
        dx2 = err * (1.0 / D_MODEL)
        dy, _, dpg, dgt = vjp2(dx2)
        dres_ref[...] = dx2
        _acc(dpg_ref, dpg)
        _acc(dgt_ref, dgt)
        dw_ref[...] += _bdot_tn(on_all, dy)
        don = _bdot_nt(dy, w_ref[...])
        for h in range(C_HEADS):
            sl = slice(h * C_DK, (h + 1) * C_DK)
            do_h, dgc_h, dng = vjps[h](don[:, sl])
            do_ref[:, sl] = do_h
            dgc_ref[:, sl] = dgc_h
            _acc(dng_ref, dng)

    vec, r10 = _fix((1, D_MODEL)), _row(TM, D_MODEL)
    row128 = _fix((1, 128))
    return pl.pallas_call(
        body, name="l1_out_fb", grid=(s_len // TM,),
        in_specs=[r10, r10, r10, r10, row128, _fix((D_MODEL, D_MODEL)), vec, vec],
        out_specs=[row128, r10, r10, r10, _fix((D_MODEL, D_MODEL)), row128, vec, vec],
        out_shape=[_sds(1, 128), _sds(s_len, D_MODEL), _sds(s_len, D_MODEL), _sds(s_len, D_MODEL),
                   _sds(D_MODEL, D_MODEL), _sds(1, 128), _sds(1, D_MODEL), _sds(1, D_MODEL)],
        compiler_params=_cp("arbitrary"),
    )(o, gate_c, x1, target, norm_g, w_out, post_g, gate)


def _row_of(v, width, at):
    return jnp.zeros((1, width), F32).at[0, at:at + v.shape[-1]].set(v.reshape(-1))


def _local_step(x, target, mod, wd, comm=None):
    s_len = x.shape[0]
    shift0, scale0, gate0 = (mod[0:1, i * 1024:(i + 1) * 1024] for i in range(3))
    shift1, scale1, gate1 = (mod[1:2, i * 1024:(i + 1) * 1024] for i in range(3))
    pre_g0, pre_g1 = wd["pre_g"][0:1], wd["pre_g"][1:2]
    post_g0, post_g1 = wd["post_g"][0:1], wd["post_g"][1:2]
    w_in0 = wd["ab_w_in"].astype(BF)
    d_skip, glu_b = wd["s5_d"].reshape(1, 512), wd["s5_glu_b"].reshape(1, 512)
    norm_g = wd["gdn_norm_g"].reshape(1, 128)
    alog_row = _row_of(wd["gdn_a_log"], 128, C_HEADS)
    dtb_row = _row_of(wd["gdn_dt_bias"], 128, C_HEADS)
    conv_w = wd["gdn_conv"]

    a_re, a_im = wd["s5_a_re"], wd["s5_a_im"]
    log_dt = wd["s5_log_dt"].reshape(B_GROUPS, 1)
    bt_re = wd["s5_b_re"].transpose(0, 2, 1).reshape(B_WIDTH, B_STATE)
    bt_im = wd["s5_b_im"].transpose(0, 2, 1).reshape(B_WIDTH, B_STATE)
    abar_r, abar_i, bbar_r, bbar_i = _s5_params(a_re, a_im, log_dt, bt_re, bt_im)
    abr, abi = abar_r.reshape(1, -1), abar_i.reshape(1, -1)
    btr, bti = _blockdiag_b(bbar_r).astype(BF), _blockdiag_b(bbar_i).astype(BF)
    ctr, cti = _blockdiag_c(wd["s5_c_re"]).astype(BF), _blockdiag_c(wd["s5_c_im"]).astype(BF)

    table = _bucket_table()
    biases = _attn_bias(wd["rel_bias"], jnp.asarray(table))
    front = _l0_front(x, pre_g0, scale0, shift0, w_in0)
    qs, ks, vs = front[0:3], front[3:6], front[6:9]
    u, ga, gb, h0 = front[9:]
    riders = [None] * 4 if comm is None else comm.late_exchanges()
    os, ls, got = [], [], []
    for i in range(3):
        (o_d, l_d), g = _attn_fwd(qs[i], ks[i], vs[i], biases[i], exchange=riders[i])
        os.append(o_d)
        ls.append(l_d)
        got.append(g)
    seg_len = s_len // S5_SEG
    zero_state = (jnp.zeros((S5_SEG, S5_WIDTH), F32),) * 2
    ends, _ = _s5_seq_fwd(u, btr, bti, ctr, cti, abr, abi, zero_state, False)
    x_entry = _s5_entries("s5_entries_fwd", *ends, abr, abi, seg_len, False)
    (xr, xi, ypre3, _, _), g = _s5_seq_fwd(u, btr, bti, ctr, cti, abr, abi, x_entry, True, exchange=riders[3])
    got.append(g)
    ypre = ypre3.reshape(s_len, B_WIDTH)
    if comm is not None:
        wd = {**wd, **comm.late_weights(got)}
    w_out0 = wd["ab_w_out"].astype(BF)
    glu_w = wd["s5_glu_w"].astype(BF)
    w_in1 = jnp.concatenate([wd["gdn_w_in"], jnp.zeros((D_MODEL, C_IN_PAD - wd["gdn_w_in"].shape[1]), wd["gdn_w_in"].dtype)], axis=1).astype(BF)
    w_out1 = wd["gdn_w_out"].astype(BF)
    x1, y0 = _l0_out(os, ls, ga, gb, ypre, u, x, d_skip, glu_w, glu_b, w_out0, post_g0, gate0)

    raw, gate_c, ba, h1 = _l1_front(x1, pre_g1, scale1, shift1, w_in1)
    qkv, bg = _gdn_prep(raw, ba, conv_w, alog_row, dtb_row)
    o_gdn, states, invs = _gdn_fwd(qkv, bg)
    loss_row, dres1, do_gdn, dgate_c, dw_out1, dnorm_g, dpost_g1, dgate1 = _l1_out_fb(
        o_gdn, gate_c, x1, target, norm_g, w_out1, post_g1, gate1)

    dq1, dk1, dv1, dbg = _gdn_bwd(qkv, bg, states, invs, do_gdn)
    draw, dba, dconv_w, dalog_row, ddtb_row = _gdn_prep_bwd(raw, ba, conv_w, alog_row, dtb_row, dq1, dk1, dv1, dbg)
    dz1, dx1, dpre_g1, dscale1, dshift1 = _front_bwd(
        "l1_front_bwd", x1, pre_g1, scale1, shift1, w_in1, dres1, [[draw], [dgate_c], [dba]], [QKV, 1024, 128])
    dw_in1 = _matmul_tn("l1_dw_in", h1, dz1, 1408)

    n_w1 = wd["gdn_w_in"].shape[1]
    ex1 = None if comm is None else (comm.chip_partials("l1", {"gdn_w_in": dw_in1[:, :n_w1], "gdn_w_out": dw_out1}), True)
    l0b, got1 = _l0_out_bwd(os, ls, ga, gb, ypre, u, x, y0, d_skip, glu_w, glu_b, w_out0, post_g0, gate0, dx1, exchange=ex1)
    dos, dls = l0b[0:3], l0b[3:6]
    dga, dgb, dypre, du_skip, dd_skip, dglu_w, dglu_b, dw_out0, dpost_g0, dgate0 = l0b[6:]
    ex2 = None if comm is None else (comm.chip_partials("l0_out", {"ab_w_out": dw_out0, "s5_glu_w": dglu_w}), True)
    starts, _ = _s5_seq_bwd(dypre, None, None, None, btr, bti, ctr, cti, abr, abi, zero_state, None, False)
    g_entry = _s5_entries("s5_entries_bwd", *starts, abr, -abi, seg_len, True)
    (du3, dbtr, dbti, dctr, dcti, dabr, dabi, _, _), got2 = _s5_seq_bwd(
        dypre, xr, xi, u, btr, bti, ctr, cti, abr, abi, g_entry, x_entry, True, exchange=ex2)
    du_scan = du3.reshape(s_len, B_WIDTH)
    if comm is not None:
        comm.received.update(zip(("gdn_w_in", "gdn_w_out", "ab_w_out", "s5_glu_w"), list(got1) + list(got2)))
    dqs, dks, dvs, dbs = [], [], [], []
    for i in range(3):
        dq_d, dk_d, dv_d, db_d = _attn_bwd(qs[i], ks[i], vs[i], biases[i], os[i], ls[i], dos[i], dls[i])
        dqs.append(dq_d)
        dks.append(dk_d)
        dvs.append(dv_d)
        dbs.append(db_d)
    parts = [dqs, dks, dvs, [du_skip, du_scan], [dga], [dgb]]
    dz0, grad_x, dpre_g0, dscale0, dshift0 = _front_bwd(
        "l0_front_bwd", x, pre_g0, scale0, shift0, w_in0, dx1, parts, [512] * 6)
    dw_in0 = _matmul_tn("l0_dw_in", h0, dz0, 768)

    idx_rows = jnp.asarray(table.reshape(3, -1), F32)
    drel = _rel_bias_grad(dbs, idx_rows).T
    da_re, da_im, dlog_dt, dbt_re, dbt_im = _s5_params_bwd(
        a_re, a_im, log_dt, bt_re, bt_im, dabr.reshape(B_GROUPS, B_STATE), dabi.reshape(B_GROUPS, B_STATE),
        _blockdiag_b_t(dbtr), _blockdiag_b_t(dbti))
    unb = lambda d: d.reshape(B_GROUPS, B_GROUP, B_STATE).transpose(0, 2, 1)
    grads = {
        "pre_g": jnp.concatenate([dpre_g0, dpre_g1], 0), "post_g": jnp.concatenate([dpost_g0, dpost_g1], 0),
        "rel_bias": drel, "ab_w_in": dw_in0, "ab_w_out": dw_out0,
        "s5_a_re": da_re, "s5_a_im": da_im, "s5_log_dt": dlog_dt.reshape(B_GROUPS),
        "s5_b_re": unb(dbt_re), "s5_b_im": unb(dbt_im),
        "s5_c_re": _blockdiag_c_t(dctr), "s5_c_im": _blockdiag_c_t(dcti),
        "s5_d": dd_skip.reshape(512), "s5_glu_w": dglu_w, "s5_glu_b": dglu_b.reshape(512),
        "gdn_w_in": dw_in1[:, :wd["gdn_w_in"].shape[1]], "gdn_conv": dconv_w,
        "gdn_a_log": dalog_row[0, C_HEADS:2 * C_HEADS], "gdn_dt_bias": ddtb_row[0, C_HEADS:2 * C_HEADS],
        "gdn_norm_g": dnorm_g.reshape(128), "gdn_w_out": dw_out1,
    }
    dmod = jnp.concatenate([jnp.concatenate([dshift0, dscale0, dgate0], 1), jnp.concatenate([dshift1, dscale1, dgate1], 1)], 0)
    return loss_row[0, 0], grad_x, grads, dmod


def _place():
    return lax.axis_index("x"), lax.axis_index("y"), lax.axis_index("c")


def _flip(v, bit):
    return 1 - v if bit else v


def _hbm_call(name, body, arrs, out_shapes, n_sem):
    any_spec = pl.BlockSpec(memory_space=pl.ANY)
    return pl.pallas_call(
        body, name=name,
        in_specs=[any_spec] * len(arrs), out_specs=[any_spec] * len(out_shapes), out_shape=out_shapes,
        scratch_shapes=[pltpu.SemaphoreType.DMA((n_sem,)), pltpu.SemaphoreType.DMA((n_sem,))],
    )(*arrs)


def _own_slot(gathered, own, slot):
    idx = lax.broadcasted_iota(jnp.int32, (gathered.shape[0],) + (1,) * own.ndim, 0)
    return jnp.where(idx == slot, own[None], gathered)


def _all_gather8(name, arr):
    def body(x_ref, out_ref, send_sems, recv_sems):
        x, y, c = _place()
        me = 4 * x + 2 * y + c
        sends, recvs = [], []
        for m in range(1, 8):
            peer = (_flip(x, m & 4), _flip(y, m & 2), _flip(c, m & 1))
            sends.append(pltpu.make_async_remote_copy(x_ref, out_ref.at[me], send_sems.at[m - 1], recv_sems.at[m - 1],
                                                      device_id=peer, device_id_type=MESH))
            recvs.append(pltpu.make_async_remote_copy(x_ref, out_ref.at[4 * peer[0] + 2 * peer[1] + peer[2]], send_sems.at[m - 1],
                                                      recv_sems.at[m - 1], device_id=peer, device_id_type=MESH))
        for cp in sends:
            cp.start()
        for cp in recvs:
            cp.wait_recv()
        for cp in sends:
            cp.wait_send()

    return _hbm_call(name, body, [arr], [jax.ShapeDtypeStruct((8,) + arr.shape, arr.dtype)], 7)[0]


def _all_to_all8(name, arr):
    def body(x_ref, out_ref, send_sems, recv_sems):
        x, y, c = _place()
        me = 4 * x + 2 * y + c
        sends, recvs = [], []
        for m in range(1, 8):
            peer = (_flip(x, m & 4), _flip(y, m & 2), _flip(c, m & 1))
            peer_id = 4 * peer[0] + 2 * peer[1] + peer[2]
            sends.append(pltpu.make_async_remote_copy(x_ref.at[peer_id], out_ref.at[me], send_sems.at[m - 1], recv_sems.at[m - 1],
                                                      device_id=peer, device_id_type=MESH))
            recvs.append(pltpu.make_async_remote_copy(x_ref.at[peer_id], out_ref.at[peer_id], send_sems.at[m - 1], recv_sems.at[m - 1],
                                                      device_id=peer, device_id_type=MESH))
        for cp in sends:
            cp.start()
        for cp in recvs:
            cp.wait_recv()
        for cp in sends:
            cp.wait_send()

    return _hbm_call(name, body, [arr], [jax.ShapeDtypeStruct(arr.shape, arr.dtype)], 7)[0]


def _chip_copies(ins, outs, send_sems, recv_sems, scatter):
    x, y, c = _place()
    mine = 2 * x + y
    sends, recvs = [], []
    for a in range(len(ins)):
        for m in range(1, 4):
            px, py = _flip(x, m & 2), _flip(y, m & 1)
            k = 3 * a + m - 1
            src = ins[a].at[2 * px + py] if scatter else ins[a]
            sends.append(pltpu.make_async_remote_copy(src, outs[a].at[mine], send_sems.at[k], recv_sems.at[k],
                                                      device_id=(px, py, c), device_id_type=MESH))
            recvs.append(pltpu.make_async_remote_copy(src, outs[a].at[2 * px + py], send_sems.at[k], recv_sems.at[k],
                                                      device_id=(px, py, c), device_id_type=MESH))
    return sends, recvs


def _chip_shapes(arrs, scatter):
    return [jax.ShapeDtypeStruct(a.shape if scatter else (4,) + a.shape, a.dtype) for a in arrs]


def _chip_exchange(name, arrs, scatter):
    n = len(arrs)

    def body(*refs):
        sends, recvs = _chip_copies(refs[:n], refs[n:2 * n], refs[2 * n], refs[2 * n + 1], scatter)
        for cp in sends:
            cp.start()
        for cp in recvs:
            cp.wait_recv()
        for cp in sends:
            cp.wait_send()

    return _hbm_call(name, body, arrs, _chip_shapes(arrs, scatter), 3 * n)


def _call_with_exchange(body, name, grid, in_specs, out_specs, out_shape, scratch_shapes, args, exchange):
    if exchange is None:
        return pl.pallas_call(body, name=name, grid=grid, in_specs=in_specs, out_specs=out_specs, out_shape=out_shape,
                              scratch_shapes=scratch_shapes, compiler_params=_cp(*["arbitrary"] * len(grid)))(*args), []
    arrs, scatter = exchange
    n_in, n_out, n_ex, n_scr = len(in_specs), len(out_specs), len(arrs), len(scratch_shapes)

    def fused(*refs):
        ins, ex_in = refs[:n_in], refs[n_in:n_in + n_ex]
        outs, ex_out = refs[n_in + n_ex:n_in + n_ex + n_out], refs[n_in + n_ex + n_out:n_in + 2 * n_ex + n_out]
        rest = refs[n_in + 2 * n_ex + n_out:]
        sends, recvs = _chip_copies(ex_in, ex_out, rest[n_scr], rest[n_scr + 1], scatter)
        first, last = pl.program_id(0) == 0, pl.program_id(0) == grid[0] - 1
        for k in range(1, len(grid)):
            first, last = first & (pl.program_id(k) == 0), last & (pl.program_id(k) == grid[k] - 1)

        @pl.when(first)
        def _():
            for cp in sends:
                cp.start()

        body(*ins, *outs, *rest[:n_scr])

        @pl.when(last)
        def _():
            for cp in recvs:
                cp.wait_recv()
            for cp in sends:
                cp.wait_send()

    any_spec = pl.BlockSpec(memory_space=pl.ANY)
    res = pl.pallas_call(
        fused, name=name, grid=grid, in_specs=list(in_specs) + [any_spec] * n_ex, out_specs=list(out_specs) + [any_spec] * n_ex,
        out_shape=list(out_shape) + _chip_shapes(arrs, scatter),
        scratch_shapes=list(scratch_shapes) + [pltpu.SemaphoreType.DMA((3 * n_ex,))] * 2,
        compiler_params=_cp(*["arbitrary"] * len(grid)))(*args, *arrs)
    return res[:n_out], res[n_out:]


def _sibling_exchange(name, arrs):
    n = len(arrs)

    def body(*refs):
        ins, outs = refs[:n], refs[n:2 * n]
        send_sems, recv_sems = refs[2 * n:]
        x, y, c = _place()
        copies = [pltpu.make_async_remote_copy(ins[a], outs[a], send_sems.at[a], recv_sems.at[a],
                                               device_id=(x, y, 1 - c), device_id_type=MESH) for a in range(n)]
        for cp in copies:
            cp.start()
        for cp in copies:
            cp.wait_recv()
        for cp in copies:
            cp.wait_send()

    return _hbm_call(name, body, arrs, [jax.ShapeDtypeStruct(a.shape, a.dtype) for a in arrs], n)


def _row_tile(rows):
    for t in (256, 128, 64, 32, 16, 8):
        if rows % t == 0:
            return t
    return rows


def _pair_sum(name, a, b, out_dtype):
    rows, cols = a.shape
    tr = _row_tile(rows)

    def body(a_ref, b_ref, o_ref):
        o_ref[...] = (a_ref[...] + b_ref[...]).astype(out_dtype)

    return pl.pallas_call(body, name=name, grid=(rows // tr,), in_specs=[_row(tr, cols)] * 2, out_specs=_row(tr, cols),
                          out_shape=_sds(rows, cols, dtype=out_dtype), compiler_params=_cp("arbitrary"))(a, b)


def _chip_sum(name, recv, partial, mine):
    n, rows, cols = recv.shape
    tr = _row_tile(rows)

    def body(mine_ref, *refs):
        own = refs[n][0].astype(F32)
        acc = None
        for s in range(n):
            term = jnp.where(mine_ref[0] == s, own, refs[s][0].astype(F32))
            acc = term if acc is None else acc + term
        refs[-1][...] = acc

    def slot_spec(s):
        return pl.BlockSpec((1, tr, cols), lambda i, m: (jnp.where(m[0] == s, (s + 1) % n, s), i, 0))

    grid_spec = pltpu.PrefetchScalarGridSpec(
        num_scalar_prefetch=1, grid=(rows // tr,),
        in_specs=[slot_spec(s) for s in range(n)] + [pl.BlockSpec((1, tr, cols), lambda i, m: (m[0], i, 0))],
        out_specs=pl.BlockSpec((tr, cols), lambda i, m: (i, 0)))
    return pl.pallas_call(body, name=name, grid_spec=grid_spec, out_shape=_sds(rows, cols),
                          compiler_params=_cp("arbitrary"))(mine, *([recv] * n), partial)


def _slot_sum(name, arr):
    n, rows, cols = arr.shape
    tr = _row_tile(rows)

    def body(*refs):
        acc = refs[0][0]
        for r in refs[1:-1]:
            acc = acc + r[0]
        refs[-1][...] = acc

    specs = [pl.BlockSpec((1, tr, cols), functools.partial(lambda s, i: (s, i, 0), s)) for s in range(n)]
    return pl.pallas_call(body, name=name, grid=(rows // tr,), in_specs=specs, out_specs=_row(tr, cols),
                          out_shape=_sds(rows, cols), compiler_params=_cp("arbitrary"))(*([arr] * n))


def _adamw(name, w, g, m, v):
    rows, cols = w.shape
    tr = _row_tile(rows)

    def body(w_ref, g_ref, m_ref, v_ref, d_ref, nm_ref, nv_ref):
        g_ = g_ref[...]
        m_ = ADAM_B1 * m_ref[...] + (1.0 - ADAM_B1) * g_
        v_ = ADAM_B2 * v_ref[...] + (1.0 - ADAM_B2) * (g_ * g_)
        m_hat = m_ / (1.0 - ADAM_B1 ** ADAM_STEP)
        v_hat = v_ / (1.0 - ADAM_B2 ** ADAM_STEP)
        d_ref[...] = -ADAM_LR * (m_hat / (jnp.sqrt(v_hat) + ADAM_EPS) + ADAM_WD * w_ref[...])
        nm_ref[...] = m_
        nv_ref[...] = v_

    spec = _row(tr, cols)
    return pl.pallas_call(body, name=name, grid=(rows // tr,), in_specs=[spec] * 4, out_specs=[spec] * 3,
                          out_shape=[_sds(rows, cols)] * 3, compiler_params=_cp("arbitrary"))(w, g, m, v)


def _adamw_halves(name, w, g_mine, g_sibling, m, v, core):
    _, rows, cols = w.shape
    half = rows // 2
    tr = _row_tile(half)
    per_half = half // tr

    def body(core_ref, w_ref, gm_ref, gs_ref, m_ref, v_ref, g_ref, d_ref, nm_ref, nv_ref):
        g_ = jnp.where(pl.program_id(0) // per_half == core_ref[0], gm_ref[...], gs_ref[...])
        m_ = ADAM_B1 * m_ref[...] + (1.0 - ADAM_B1) * g_
        v_ = ADAM_B2 * v_ref[...] + (1.0 - ADAM_B2) * (g_ * g_)
        m_hat = m_ / (1.0 - ADAM_B1 ** ADAM_STEP)
        v_hat = v_ / (1.0 - ADAM_B2 ** ADAM_STEP)
        g_ref[...] = g_
        d_ref[...] = -ADAM_LR * (m_hat / (jnp.sqrt(v_hat) + ADAM_EPS) + ADAM_WD * w_ref[...])
        nm_ref[...] = m_
        nv_ref[...] = v_

    full = pl.BlockSpec((None, tr, cols), lambda i, c: (0, i, 0))
    in_half = pl.BlockSpec((tr, cols), lambda i, c: (i % per_half, 0))
    grid_spec = pltpu.PrefetchScalarGridSpec(num_scalar_prefetch=1, grid=(rows // tr,),
                                             in_specs=[full, in_half, in_half, full, full], out_specs=[full] * 4)
    return pl.pallas_call(body, name=name, grid_spec=grid_spec, out_shape=[_sds(1, rows, cols)] * 4,
                          compiler_params=_cp("arbitrary"))(core, w, g_mine, g_sibling, m, v)


def _mod_local(c_all, ada_w):
    def body(c_ref, w_ref, o_ref):
        c_act = jax.nn.silu(c_ref[...])
        for l in range(2):
            o_ref[l] = _hdot(c_act, w_ref[l])

    return pl.pallas_call(body, name="mod_local", out_shape=_sds(2, 8, ada_w.shape[2]),
                          compiler_params=pltpu.CompilerParams(vmem_limit_bytes=VMEM_LIMIT_BYTES))(c_all, ada_w)


def _ada_w_grad(c_all, dmod_cols):
    def body(c_ref, d_ref, o_ref):
        c_act = jax.nn.silu(c_ref[...])
        for l in range(2):
            o_ref[l] = lax.dot_general(c_act, d_ref[l], (((0,), (0,)), ((), ())), precision=HI, preferred_element_type=F32)

    return pl.pallas_call(body, name="ada_w_grad", out_shape=_sds(2, D_MODEL, dmod_cols.shape[2]),
                          compiler_params=pltpu.CompilerParams(vmem_limit_bytes=VMEM_LIMIT_BYTES))(c_all, dmod_cols)


_SMALL = ("ada_b", "pre_g", "post_g", "rel_bias", "s5_a_re", "s5_a_im", "s5_log_dt", "s5_b_re", "s5_b_im", "s5_c_re", "s5_c_im",
          "s5_d", "s5_glu_b", "gdn_a_log", "gdn_dt_bias", "gdn_norm_g")
_SHARDED = ("ab_w_in", "ab_w_out", "s5_glu_w", "gdn_w_in", "gdn_w_out")
_COL_SHARDED = ("ab_w_in", "gdn_w_in")
_WEIGHTS = ("ada_w", "ada_b", "pre_g", "post_g", "rel_bias", "ab_w_in", "ab_w_out", "s5_a_re", "s5_a_im", "s5_log_dt", "s5_b_re",
            "s5_b_im", "s5_c_re", "s5_c_im", "s5_d", "s5_glu_w", "s5_glu_b", "gdn_w_in", "gdn_conv", "gdn_a_log", "gdn_dt_bias",
            "gdn_norm_g", "gdn_w_out")


def _rows128(n):
    return -(-n // 128)


def _pack(arrs, total_rows):
    pieces = []
    for a in arrs:
        flat = a.reshape(-1)
        pieces.append(jnp.pad(flat, (0, _rows128(flat.shape[0]) * 128 - flat.shape[0])).reshape(-1, 128))
    used = sum(p.shape[0] for p in pieces)
    pieces.append(jnp.zeros((total_rows - used, 128), F32))
    return jnp.concatenate(pieces, axis=0)


def _unpack(buf, shapes):
    out, at = [], 0
    for shp in shapes:
        n = int(np.prod(shp))
        out.append(buf[at:at + _rows128(n)].reshape(-1)[:n].reshape(shp))
        at += _rows128(n)
    return out


def _full_from_halves(name, g):
    if name in _COL_SHARDED:
        return g.transpose(0, 2, 1, 3).reshape(2 * g.shape[2], 4 * g.shape[3])
    return g.transpose(1, 0, 2, 3).reshape(8 * g.shape[2], g.shape[3])


def _shard_major(name, g):
    if name in _COL_SHARDED:
        return g.reshape(g.shape[0], 4, g.shape[1] // 4).transpose(1, 0, 2)
    return g.reshape(4, g.shape[0] // 4, g.shape[1])


_LATE = ("ab_w_out", "s5_glu_w", "gdn_w_in", "gdn_w_out")


class _WeightExchanges:
    def __init__(self, shards, core, chip):
        self.core, self.chip = core, chip
        self.half = {}
        for name, shard in shards.items():
            h = shard.shape[0] // 2
            self.half[name] = lax.dynamic_slice_in_dim(shard.astype(BF), core * h, h, axis=0)
        self.partial, self.received = {}, {}

    def _full(self, label, names, from_chips):
        mine = [_own_slot(g, self.half[n], self.chip) for n, g in zip(names, from_chips)]
        theirs = _sibling_exchange("gather_w_sibling_" + label, mine)
        return {n: _full_from_halves(n, jnp.where(self.core == 0, jnp.stack([a, b], 0), jnp.stack([b, a], 0)))
                for n, a, b in zip(names, mine, theirs)}

    def first_weights(self):
        return self._full("first", ["ab_w_in"], _chip_exchange("gather_w_chips", [self.half["ab_w_in"]], False))

    def late_exchanges(self):
        rows = self.half["gdn_w_in"].shape[0] // 2
        pieces = [self.half["gdn_w_in"][:rows], self.half["gdn_w_in"][rows:]]
        return [([self.half["ab_w_out"], self.half["s5_glu_w"]], False), ([self.half["gdn_w_out"]], False),
                ([pieces[0]], False), ([pieces[1]], False)]

    def late_weights(self, got):
        from_chips = {"ab_w_out": got[0][0], "s5_glu_w": got[0][1], "gdn_w_out": got[1][0],
                      "gdn_w_in": jnp.concatenate([got[2][0], got[3][0]], axis=1)}
        return self._full("late", list(_LATE), [from_chips[n] for n in _LATE])

    def chip_partials(self, label, grads):
        mine, other = [], []
        for name, g in grads.items():
            sm = _shard_major(name, g)
            h = sm.shape[1] // 2
            mine.append(lax.dynamic_slice_in_dim(sm, self.core * h, h, axis=1))
            other.append(lax.dynamic_slice_in_dim(sm, (1 - self.core) * h, h, axis=1))
        out = []
        for name, a, b in zip(grads, mine, _sibling_exchange("reduce_sibling_" + label, other)):
            flat = lambda t: t.reshape(-1, t.shape[-1])
            self.partial[name] = _pair_sum("sum_sibling_" + name, flat(a), flat(b), BF).reshape(a.shape)
            out.append(self.partial[name])
        return out


def kernel(x, c, ada_w, ada_b, pre_g, post_g, rel_bias, ab_w_in, ab_w_out, s5_a_re, s5_a_im, s5_log_dt, s5_b_re, s5_b_im, s5_c_re, s5_c_im, s5_d, s5_glu_w, s5_glu_b, gdn_w_in, gdn_conv, gdn_a_log, gdn_dt_bias, gdn_norm_g, gdn_w_out, loss_target, m_ada_w, m_ada_b, m_pre_g, m_post_g, m_rel_bias, m_ab_w_in, m_ab_w_out, m_s5_a_re, m_s5_a_im, m_s5_log_dt, m_s5_b_re, m_s5_b_im, m_s5_c_re, m_s5_c_im, m_s5_d, m_s5_glu_w, m_s5_glu_b, m_gdn_w_in, m_gdn_conv, m_gdn_a_log, m_gdn_dt_bias, m_gdn_norm_g, m_gdn_w_out, v_ada_w, v_ada_b, v_pre_g, v_post_g, v_rel_bias, v_ab_w_in, v_ab_w_out, v_s5_a_re, v_s5_a_im, v_s5_log_dt, v_s5_b_re, v_s5_b_im, v_s5_c_re, v_s5_c_im, v_s5_d, v_s5_glu_w, v_s5_glu_b, v_gdn_w_in, v_gdn_conv, v_gdn_a_log, v_gdn_dt_bias, v_gdn_norm_g, v_gdn_w_out):
    w = dict(ada_w=ada_w, ada_b=ada_b, pre_g=pre_g, post_g=post_g, rel_bias=rel_bias, ab_w_in=ab_w_in, ab_w_out=ab_w_out,
             s5_a_re=s5_a_re, s5_a_im=s5_a_im, s5_log_dt=s5_log_dt, s5_b_re=s5_b_re, s5_b_im=s5_b_im, s5_c_re=s5_c_re, s5_c_im=s5_c_im,
             s5_d=s5_d, s5_glu_w=s5_glu_w, s5_glu_b=s5_glu_b, gdn_w_in=gdn_w_in, gdn_conv=gdn_conv, gdn_a_log=gdn_a_log,
             gdn_dt_bias=gdn_dt_bias, gdn_norm_g=gdn_norm_g, gdn_w_out=gdn_w_out)
    m = dict(ada_w=m_ada_w, ada_b=m_ada_b, pre_g=m_pre_g, post_g=m_post_g, rel_bias=m_rel_bias, ab_w_in=m_ab_w_in, ab_w_out=m_ab_w_out,
             s5_a_re=m_s5_a_re, s5_a_im=m_s5_a_im, s5_log_dt=m_s5_log_dt, s5_b_re=m_s5_b_re, s5_b_im=m_s5_b_im, s5_c_re=m_s5_c_re,
             s5_c_im=m_s5_c_im, s5_d=m_s5_d, s5_glu_w=m_s5_glu_w, s5_glu_b=m_s5_glu_b, gdn_w_in=m_gdn_w_in, gdn_conv=m_gdn_conv,
             gdn_a_log=m_gdn_a_log, gdn_dt_bias=m_gdn_dt_bias, gdn_norm_g=m_gdn_norm_g, gdn_w_out=m_gdn_w_out)
    v = dict(ada_w=v_ada_w, ada_b=v_ada_b, pre_g=v_pre_g, post_g=v_post_g, rel_bias=v_rel_bias, ab_w_in=v_ab_w_in, ab_w_out=v_ab_w_out,
             s5_a_re=v_s5_a_re, s5_a_im=v_s5_a_im, s5_log_dt=v_s5_log_dt, s5_b_re=v_s5_b_re, s5_b_im=v_s5_b_im, s5_c_re=v_s5_c_re,
             s5_c_im=v_s5_c_im, s5_d=v_s5_d, s5_glu_w=v_s5_glu_w, s5_glu_b=v_s5_glu_b, gdn_w_in=v_gdn_w_in, gdn_conv=v_gdn_conv,
             gdn_a_log=v_gdn_a_log, gdn_dt_bias=v_gdn_dt_bias, gdn_norm_g=v_gdn_norm_g, gdn_w_out=v_gdn_w_out)
    ix, iy, ic = _place()
    me = 4 * ix + 2 * iy + ic
    chip = 2 * ix + iy
    n_cols = ada_w.shape[2]

    mine_first = _pack([c, gdn_conv], 32)
    first = _own_slot(_all_gather8("gather_c_conv", mine_first), mine_first, me)
    c_all = first[:, 0:8].reshape(8, D_MODEL)
    conv_full = first[0::2, 8:32].reshape(4, C_CONV, n_cols).transpose(1, 0, 2).reshape(C_CONV, 4 * n_cols)
    mine_mod = _mod_local(c_all, ada_w)
    modl = _own_slot(_all_gather8("gather_mod", mine_mod), mine_mod, me)
    mod = lax.dynamic_index_in_dim(modl[0::2], me, axis=2, keepdims=False)
    mod = mod.transpose(1, 0, 2).reshape(2, 4 * n_cols) + ada_b

    comm = _WeightExchanges({name: w[name][0] for name in _SHARDED}, ic, chip)
    wd = {name: w[name] for name in _SMALL if name != "ada_b"}
    wd = {k: (a if k in ("pre_g", "post_g", "rel_bias") else a[0]) for k, a in wd.items()}
    wd["gdn_conv"] = conv_full
    wd.update(comm.first_weights())

    loss_local, grad_x, grads, dmod = _local_step(x[0], loss_target[0], mod, wd, comm)
    loss = lax.psum(loss_local, ("x", "y", "c"))

    small_shapes = [w[name].shape for name in _SMALL] + [(C_CONV, 4 * n_cols)]
    small_rows = -(-sum(_rows128(int(np.prod(s))) for s in small_shapes) // 64) * 64
    per_dev, dmod_rows = small_rows // 8, _rows128(2 * 3 * D_MODEL)
    partial = _pack([dmod] + [grads[name] for name in _SMALL[1:]] + [grads["gdn_conv"]], small_rows)
    outbound = jnp.concatenate([partial.reshape(8, per_dev, 128), jnp.broadcast_to(partial[None, :dmod_rows], (8, dmod_rows, 128))], axis=1)
    inbound = _own_slot(_all_to_all8("reduce_small_grads", outbound), lax.dynamic_index_in_dim(outbound, me, 0, keepdims=False), me)
    my_rows = _slot_sum("sum_small_grads", inbound[:, :per_dev])
    g_small = _own_slot(_all_gather8("gather_small_grads", my_rows), my_rows, me).reshape(small_rows, 128)
    g_list = _unpack(g_small, small_shapes)
    out_g, out_d, out_m, out_v = {}, {}, {}, {}

    def update(name, g2d):
        shp = w[name].shape
        two_d = lambda a: a.reshape(-1, shp[-1])
        d_, m_, v_ = _adamw("adamw_" + name, two_d(w[name]), g2d, two_d(m[name]), two_d(v[name]))
        out_g[name], out_d[name], out_m[name], out_v[name] = (a.reshape(shp) for a in (g2d, d_, m_, v_))

    for name, g in zip(_SMALL, g_list[:-1]):
        update(name, g.reshape(-1, g.shape[-1]))
    update("gdn_conv", lax.dynamic_slice_in_dim(g_list[-1], chip * n_cols, n_cols, axis=1))

    dmod_all = inbound[:, per_dev:].reshape(8, 2, 4, n_cols)
    dmod_cols = lax.dynamic_index_in_dim(dmod_all, chip, axis=2, keepdims=False).transpose(1, 0, 2)
    update("ada_w", _ada_w_grad(c_all, dmod_cols).reshape(-1, n_cols))

    comm.received["ab_w_in"] = _chip_exchange("reduce_chips", comm.chip_partials("l0_in", {"ab_w_in": grads["ab_w_in"]}), True)[0]
    chip_1 = jnp.reshape(chip, (1,)).astype(jnp.int32)
    core_1 = jnp.reshape(ic, (1,)).astype(jnp.int32)
    reduced = [_chip_sum("sum_chips_" + name, comm.received[name], comm.partial[name], chip_1) for name in _SHARDED]
    for name, g_mine, g_sib in zip(_SHARDED, reduced, _sibling_exchange("reduce_share", reduced)):
        out_g[name], out_d[name], out_m[name], out_v[name] = _adamw_halves(
            "adamw_" + name, w[name], g_mine, g_sib, m[name], v[name], core_1)

    return (loss, grad_x[None], *[out_g[n] for n in _WEIGHTS], *[out_d[n] for n in _WEIGHTS],
            *[out_m[n] for n in _WEIGHTS], *[out_v[n] for n in _WEIGHTS])
```

```python
import functools
import math

import numpy as np
import jax
import jax.numpy as jnp
from jax import lax
from jax.experimental import pallas as pl
from jax.experimental.pallas import tpu as pltpu

F32 = jnp.float32
BF = jnp.bfloat16
HI = lax.Precision.HIGHEST
MESH = pl.DeviceIdType.MESH

D_MODEL = 1024
EPS = 1e-6
A_HEADS, A_HD, A_WIDTH, A_BLOCK = 8, 64, 512, 128
DILATIONS = (1, 4, 16)
N_KEYS = 128
REL_BUCKETS, REL_MAX_DIST = 32, 2048
B_WIDTH, B_GROUP, B_GROUPS, B_STATE = 512, 16, 32, 64
S5_LANES = 512
S5_TILES = 4
C_HEADS, C_DK, C_CHUNK, C_CONV = 8, 128, 64, 4
QKV = 3072
C_IN_PAD = 4224
TM = 256
VMEM_LIMIT_BYTES = 56 * 1024 * 1024
ADAM_LR, ADAM_B1, ADAM_B2, ADAM_EPS, ADAM_WD, ADAM_STEP = 0.001, 0.9, 0.999, 1e-08, 0.01, 10
NEG = float(np.finfo(np.float32).min)


def _cp(*sem):
    return pltpu.CompilerParams(dimension_semantics=sem, vmem_limit_bytes=VMEM_LIMIT_BYTES)


def _bdot(a, b):
    return jnp.dot(a.astype(BF), b.astype(BF), preferred_element_type=F32)


def _bdot_nt(a, b):
    return lax.dot_general(a.astype(BF), b.astype(BF), (((1,), (1,)), ((), ())), preferred_element_type=F32)


def _bdot_tn(a, b):
    return lax.dot_general(a.astype(BF), b.astype(BF), (((0,), (0,)), ((), ())), preferred_element_type=F32)


def _hdot(a, b):
    return jnp.dot(a, b, precision=HI, preferred_element_type=F32)


def _bein(eq, a, b):
    return jnp.einsum(eq, a.astype(BF), b.astype(BF), preferred_element_type=F32)


def _row(tm, n):
    return pl.BlockSpec((tm, n), lambda i: (i, 0))


def _fix(shape):
    return pl.BlockSpec(shape, lambda i: (0,) * len(shape))


def _sds(*shape, dtype=F32):
    return jax.ShapeDtypeStruct(shape, dtype)


def _acc(ref, val):
    ref[...] += val


def _zero_at_first(refs, axis=0):
    @pl.when(pl.program_id(axis) == 0)
    def _():
        for r in refs:
            r[...] = jnp.zeros_like(r)


def _rms(x):
    return x * lax.rsqrt(jnp.mean(x * x, axis=-1, keepdims=True) + EPS)


def _pre_mod(x, g, scale, shift):
    return (_rms(x) * g) * (1.0 + scale) + shift


def _post_res(y, x, post_g, gate):
    return x + gate * (_rms(y) * post_g)


def _merge_gate(o1, o2, o3, l1, l2, l3, ga):
    m = jnp.maximum(jnp.maximum(l1, l2), l3)
    e1, e2, e3 = jnp.exp(l1 - m), jnp.exp(l2 - m), jnp.exp(l3 - m)
    inv = 1.0 / (e1 + e2 + e3)
    return ((e1 * inv) * o1 + (e2 * inv) * o2 + (e3 * inv) * o3) * jax.nn.silu(ga)


def _s5_gelu(ypre, u, d_skip):
    return jax.nn.gelu(ypre + d_skip * u)


def _s5_glu(yb, gl, gb):
    return yb * jax.nn.sigmoid(gl) * jax.nn.silu(gb)


def _l0_front(x, pre_g, scale, shift, w_in):
    s_len = x.shape[0]

    def body(x_ref, g_ref, sc_ref, sh_ref, w_ref, *out_refs):
        qkv_refs, (u_ref, ga_ref, gb_ref, h_ref) = out_refs[:9], out_refs[9:]
        hb = _pre_mod(x_ref[...], g_ref[...], sc_ref[...], sh_ref[...]).astype(BF)
        h_ref[...] = hb
        z = jnp.dot(hb, w_ref[...], preferred_element_type=F32)
        for a in range(3):
            piece = z[:, a * 512:(a + 1) * 512]
            for i, d in enumerate(DILATIONS):
                qkv_refs[3 * a + i][...] = _to_res(piece, d).astype(BF)
        u_ref[...] = z[:, 1536:2048]
        ga_ref[...] = z[:, 2048:2560]
        gb_ref[...] = z[:, 2560:3072]

    vec = _fix((1, D_MODEL))
    return pl.pallas_call(
        body, name="l0_front", grid=(s_len // TM,),
        in_specs=[_row(TM, D_MODEL), vec, vec, vec, _fix((D_MODEL, 3072))],
        out_specs=[_res_spec(d) for d in DILATIONS] * 3 + [_row(TM, 512)] * 3 + [_row(TM, D_MODEL)],
        out_shape=[_sds(*_res_shape(s_len, d), dtype=BF) for d in DILATIONS] * 3 + [_sds(s_len, 512)] * 3 + [_sds(s_len, D_MODEL, dtype=BF)],
        compiler_params=_cp("arbitrary"),
    )(x, pre_g, scale, shift, w_in)


def _front_bwd(name, x, pre_g, scale, shift, w_in, dres, parts, widths):
    s_len = x.shape[0]
    n_in = sum(len(p) for p in parts)
    n_cols = sum(widths)

    def body(*refs):
        x_ref, g_ref, sc_ref, sh_ref, w_ref, dres_ref = refs[:6]
        part_refs = refs[6:6 + n_in]
        dz_ref, dx_ref, dg_ref, dsc_ref, dsh_ref = refs[6 + n_in:]
        _zero_at_first([dg_ref, dsc_ref, dsh_ref])
        _, vjp = jax.vjp(_pre_mod, x_ref[...], g_ref[...], sc_ref[...], sh_ref[...])
        dh = jnp.zeros((TM, D_MODEL), F32)
        col, at = 0, 0
        for grp, width in zip(parts, widths):
            tile = lambda r: _from_res(r[...]) if len(r.shape) == 3 else r[...]
            dz = tile(part_refs[at])
            for r in part_refs[at + 1:at + len(grp)]:
                dz = dz + tile(r)
            at += len(grp)
            dzb = dz.astype(BF)
            dz_ref[:, col:col + width] = dzb
            dh = dh + lax.dot_general(dzb, w_ref[:, col:col + width], (((1,), (1,)), ((), ())), preferred_element_type=F32)
            col += width
        dx, dg, dsc, dsh = vjp(dh)
        dx_ref[...] = dx + dres_ref[...]
        _acc(dg_ref, dg)
        _acc(dsc_ref, dsc)
        _acc(dsh_ref, dsh)

    vec = _fix((1, D_MODEL))
    flat = [a for p in parts for a in p]
    return pl.pallas_call(
        body, name=name, grid=(s_len // TM,),
        in_specs=[_row(TM, D_MODEL), vec, vec, vec, _fix((D_MODEL, n_cols)), _row(TM, D_MODEL)]
        + [_res_spec(a.shape[0], a.shape[2]) if a.ndim == 3 else _row(TM, a.shape[1]) for a in flat],
        out_specs=[_row(TM, n_cols), _row(TM, D_MODEL), vec, vec, vec],
        out_shape=[_sds(s_len, n_cols, dtype=BF), _sds(s_len, D_MODEL), _sds(1, D_MODEL), _sds(1, D_MODEL), _sds(1, D_MODEL)],
        compiler_params=_cp("arbitrary"),
    )(x, pre_g, scale, shift, w_in, dres, *flat)


def _matmul_tn(name, a, b, tn):
    s_len, k_dim = a.shape
    n_dim = b.shape[1]
    ts = 512

    def body(a_ref, b_ref, o_ref):
        _zero_at_first([o_ref], axis=1)
        o_ref[...] += lax.dot_general(a_ref[...], b_ref[...], (((0,), (0,)), ((), ())), preferred_element_type=F32)

    return pl.pallas_call(
        body, name=name, grid=(n_dim // tn, s_len // ts),
        in_specs=[pl.BlockSpec((ts, k_dim), lambda j, i: (i, 0)), pl.BlockSpec((ts, tn), lambda j, i: (i, j))],
        out_specs=pl.BlockSpec((k_dim, tn), lambda j, i: (0, j)),
        out_shape=_sds(k_dim, n_dim),
        compiler_params=_cp("arbitrary", "arbitrary"),
    )(a, b)


def _t5_bucket_np(dist):
    dist = np.maximum(dist, 0)
    max_exact = REL_BUCKETS // 2
    large = max_exact + (np.log(np.maximum(dist, 1) / max_exact)
                         / math.log(REL_MAX_DIST / max_exact) * (REL_BUCKETS - max_exact)).astype(np.int32)
    large = np.minimum(large, REL_BUCKETS - 1)
    return np.where(dist < max_exact, dist, large).astype(np.int32)


def _to_res(z, dil):
    if dil == 1:
        return z[None]
    return jnp.swapaxes(z.reshape(z.shape[0] // dil, dil, z.shape[1]), 0, 1)


def _from_res(z):
    if z.shape[0] == 1:
        return z[0]
    return jnp.swapaxes(z, 0, 1).reshape(z.shape[0] * z.shape[1], z.shape[2])


def _res_shape(s_len, dil, width=A_WIDTH):
    return (dil, s_len // dil, width)


def _res_spec(dil, width=A_WIDTH):
    return pl.BlockSpec((dil, TM // dil, width), lambda i: (0, i, 0))


def _bucket_table():
    qi = np.arange(A_BLOCK)[:, None]
    kj = np.arange(2 * A_BLOCK)[None, :]
    return np.stack([_t5_bucket_np((qi + A_BLOCK - kj) * d) for d in DILATIONS], 0)


def _attn_mask(first):
    qi = lax.broadcasted_iota(jnp.int32, (A_BLOCK, 2 * A_BLOCK), 0)
    kj = lax.broadcasted_iota(jnp.int32, (A_BLOCK, 2 * A_BLOCK), 1)
    rel = qi + A_BLOCK - kj
    return (rel >= 0) & (rel <= N_KEYS) & (jnp.logical_not(first) | (kj >= A_BLOCK))


def _attn_specs(nb, rev):
    per = 2 if nb % 2 == 0 else 1
    steps = nb // per
    n_of = (lambda i: steps - 1 - i) if rev else (lambda i: i)
    cur = pl.BlockSpec((None, per * A_BLOCK, A_WIDTH), lambda r, i: (r, n_of(i), 0))
    prev = pl.BlockSpec((None, A_BLOCK, A_WIDTH), lambda r, i: (r, jnp.maximum(per * n_of(i) - 1, 0), 0))
    bias = pl.BlockSpec((A_HEADS, A_BLOCK, 2 * A_BLOCK), lambda r, i: (0, 0, 0))
    return per, steps, cur, prev, bias


def _attn_fwd(q, k, v, bias, exchange=None):
    dil, t_len, _ = q.shape
    per, steps, cur, prev, bias_spec = _attn_specs(t_len // A_BLOCK, False)
    scale = A_HD ** -0.5

    def body(q_ref, kp_ref, kc_ref, vp_ref, vc_ref, b_ref, o_ref, l_ref):
        lane = lax.broadcasted_iota(jnp.int32, (1, 128), 1)
        for sub in range(per):
            rows = slice(sub * A_BLOCK, (sub + 1) * A_BLOCK)
            before = slice((sub - 1) * A_BLOCK, sub * A_BLOCK)
            mask = _attn_mask((pl.program_id(1) == 0) if sub == 0 else False)
            for hp in range(A_HEADS // 2):
                sl = slice(hp * 128, (hp + 1) * 128)
                qp = q_ref[rows, sl]
                kw = jnp.concatenate([kp_ref[:, sl] if sub == 0 else kc_ref[before, sl], kc_ref[rows, sl]], axis=0).astype(BF)
                vw = jnp.concatenate([vp_ref[:, sl] if sub == 0 else vc_ref[before, sl], vc_ref[rows, sl]], axis=0).astype(BF)
                outs, lses = [], []
                for j in range(2):
                    hm = (lane < 64) if j == 0 else (lane >= 64)
                    s = _bdot_nt(jnp.where(hm, qp, 0.0), kw) * scale
                    s = jnp.where(mask, s + b_ref[2 * hp + j], NEG)
                    m = jnp.max(s, axis=-1, keepdims=True)
                    p = jnp.exp(s - m)
                    den = jnp.sum(p, axis=-1, keepdims=True)
                    outs.append(_bdot(p, vw) / den)
                    lses.append(m + jnp.log(den))
                hm0 = lane < 64
                o_ref[rows, sl] = jnp.where(hm0, outs[0], outs[1])
                l_ref[rows, sl] = jnp.where(hm0, lses[0], lses[1])

    return _call_with_exchange(body, f"attn_fwd_d{dil}", (dil, steps), [cur, prev, cur, prev, cur, bias_spec], [cur, cur],
                               [_sds(dil, t_len, A_WIDTH)] * 2, [], (q, k, k, v, v, bias), exchange)


def _attn_bwd(q, k, v, bias, o, l, do, dl, exchange=None):
    dil, t_len, _ = q.shape
    per, steps, cur, prev, bias_spec = _attn_specs(t_len // A_BLOCK, True)
    scale = A_HD ** -0.5

    def body(q_ref, kp_ref, kc_ref, vp_ref, vc_ref, b_ref, o_ref, l_ref, do_ref, dl_ref,
             dq_ref, dk_ref, dv_ref, db_ref, ck_ref, cv_ref):
        _zero_at_first([ck_ref, cv_ref], axis=1)

        @pl.when((pl.program_id(0) == 0) & (pl.program_id(1) == 0))
        def _():
            db_ref[...] = jnp.zeros_like(db_ref)

        lane = lax.broadcasted_iota(jnp.int32, (1, 128), 1)
        for hp in range(A_HEADS // 2):
            sl = slice(hp * 128, (hp + 1) * 128)
            to_prev_k, to_prev_v = ck_ref[:, sl], cv_ref[:, sl]
            for sub in range(per - 1, -1, -1):
                rows = slice(sub * A_BLOCK, (sub + 1) * A_BLOCK)
                before = slice((sub - 1) * A_BLOCK, sub * A_BLOCK)
                mask = _attn_mask((pl.program_id(1) == steps - 1) if sub == 0 else False)
                qp = q_ref[rows, sl]
                kw = jnp.concatenate([kp_ref[:, sl] if sub == 0 else kc_ref[before, sl], kc_ref[rows, sl]], axis=0).astype(BF)
                vw = jnp.concatenate([vp_ref[:, sl] if sub == 0 else vc_ref[before, sl], vc_ref[rows, sl]], axis=0).astype(BF)
                op, lp, dop, dlp = o_ref[rows, sl], l_ref[rows, sl], do_ref[rows, sl], dl_ref[rows, sl]
                dq_acc = jnp.zeros((A_BLOCK, 128), F32)
                dk_acc = jnp.zeros((2 * A_BLOCK, 128), F32)
                dv_acc = jnp.zeros((2 * A_BLOCK, 128), F32)
                for j in range(2):
                    hm = (lane < 64) if j == 0 else (lane >= 64)
                    qm = jnp.where(hm, qp, 0.0)
                    s = _bdot_nt(qm, kw) * scale
                    s = jnp.where(mask, s + b_ref[2 * hp + j], NEG)
                    lse = jnp.max(jnp.where(hm, lp, NEG), axis=-1, keepdims=True)
                    p = jnp.exp(s - lse)
                    do_h = jnp.where(hm, dop, 0.0)
                    dd = jnp.sum(do_h * op, axis=-1, keepdims=True)
                    dlse = jnp.sum(jnp.where(hm, dlp, 0.0), axis=-1, keepdims=True)
                    ds = p * (_bdot_nt(do_h, vw) - dd + dlse)
                    dv_acc = dv_acc + _bdot_tn(p, do_h)
                    dq_acc = dq_acc + jnp.where(hm, _bdot(ds, kw), 0.0) * scale
                    dk_acc = dk_acc + _bdot_tn(ds, qm) * scale
                    db_ref[2 * hp + j] += ds
                dq_ref[rows, sl] = dq_acc
                dk_ref[rows, sl] = dk_acc[A_BLOCK:] + to_prev_k
                dv_ref[rows, sl] = dv_acc[A_BLOCK:] + to_prev_v
                to_prev_k, to_prev_v = dk_acc[:A_BLOCK], dv_acc[:A_BLOCK]
            ck_ref[:, sl] = to_prev_k
            cv_ref[:, sl] = to_prev_v

    return _call_with_exchange(
        body, f"attn_bwd_d{dil}", (dil, steps), [cur, prev, cur, prev, cur, bias_spec, cur, cur, cur, cur],
        [cur, cur, cur, bias_spec], [_sds(dil, t_len, A_WIDTH)] * 3 + [_sds(A_HEADS, A_BLOCK, 2 * A_BLOCK)],
        [pltpu.VMEM((A_BLOCK, A_WIDTH), F32)] * 2, (q, k, k, v, v, bias, o, l, do, dl), exchange)


def _attn_bias(rel_bias, table):
    def body(rb_ref, t_ref, *o_refs):
        for c in range(3):
            t = t_ref[c]
            acc = [jnp.zeros((A_BLOCK, 2 * A_BLOCK), F32) for _ in range(A_HEADS)]
            for b in range(REL_BUCKETS):
                hit = t == b
                acc = [jnp.where(hit, rb_ref[b, h], acc[h]) for h in range(A_HEADS)]
            for h in range(A_HEADS):
                o_refs[c][h] = acc[h]

    return pl.pallas_call(body, name="attn_bias", out_shape=[_sds(A_HEADS, A_BLOCK, 2 * A_BLOCK)] * 3,
                          in_specs=[pl.BlockSpec(memory_space=pltpu.SMEM), pl.BlockSpec(memory_space=pltpu.VMEM)],
                          compiler_params=pltpu.CompilerParams(vmem_limit_bytes=VMEM_LIMIT_BYTES))(rel_bias, table)


def _rel_bias_grad(dbs, idx_rows):
    n = A_BLOCK * 2 * A_BLOCK

    def body(d0_ref, d1_ref, d2_ref, idx_ref, o_ref):
        bucket = lax.broadcasted_iota(jnp.int32, (REL_BUCKETS, n), 0).astype(F32)
        acc = jnp.zeros((A_HEADS, REL_BUCKETS), F32)
        for c, db_ref in enumerate((d0_ref, d1_ref, d2_ref)):
            onehot = (idx_ref[c:c + 1, :] == bucket).astype(F32)
            acc = acc + lax.dot_general(db_ref[...], onehot, (((1,), (1,)), ((), ())), precision=HI, preferred_element_type=F32)
        o_ref[...] = acc

    return pl.pallas_call(body, name="rel_bias_grad", out_shape=_sds(A_HEADS, REL_BUCKETS),
                          compiler_params=pltpu.CompilerParams(vmem_limit_bytes=VMEM_LIMIT_BYTES))(
                              *[d.reshape(A_HEADS, n) for d in dbs], idx_rows)


def _s5_param_fn(a_re, a_im, log_dt, bt_re, bt_im):
    dt = jnp.exp(log_dt)
    mag = jnp.exp(dt * a_re)
    abar_r, abar_i = mag * jnp.cos(dt * a_im), mag * jnp.sin(dt * a_im)
    den = a_re * a_re + a_im * a_im
    fr = ((abar_r - 1.0) * a_re + abar_i * a_im) / den
    fi = (abar_i * a_re - (abar_r - 1.0) * a_im) / den
    row = lax.broadcasted_iota(jnp.int32, (B_WIDTH, B_GROUPS), 0)
    grp = lax.broadcasted_iota(jnp.int32, (B_WIDTH, B_GROUPS), 1)
    expand = ((row // B_GROUP) == grp).astype(F32)
    fr_e, fi_e = _hdot(expand, fr), _hdot(expand, fi)
    return abar_r, abar_i, fr_e * bt_re - fi_e * bt_im, fr_e * bt_im + fi_e * bt_re


def _s5_params(a_re, a_im, log_dt, bt_re, bt_im):
    def body(ar, ai, ld, br, bi, o1, o2, o3, o4):
        o1[...], o2[...], o3[...], o4[...] = _s5_param_fn(ar[...], ai[...], ld[...], br[...], bi[...])

    return pl.pallas_call(body, name="s5_params",
                          out_shape=[_sds(B_GROUPS, B_STATE)] * 2 + [_sds(B_WIDTH, B_STATE)] * 2)(a_re, a_im, log_dt, bt_re, bt_im)


def _s5_params_bwd(a_re, a_im, log_dt, bt_re, bt_im, d1, d2, d3, d4):
    def body(ar, ai, ld, br, bi, c1, c2, c3, c4, o1, o2, o3, o4, o5):
        _, vjp = jax.vjp(_s5_param_fn, ar[...], ai[...], ld[...], br[...], bi[...])
        o1[...], o2[...], o3[...], o4[...], o5[...] = vjp((c1[...], c2[...], c3[...], c4[...]))

    return pl.pallas_call(body, name="s5_params_bwd",
                          out_shape=[_sds(B_GROUPS, B_STATE)] * 2 + [_sds(B_GROUPS, 1)] + [_sds(B_WIDTH, B_STATE)] * 2,
                          )(a_re, a_im, log_dt, bt_re, bt_im, d1, d2, d3, d4)


def _pick_row(x, r):
    rows = lax.broadcasted_iota(jnp.int32, x.shape, 0)
    return jnp.sum(jnp.where(rows == r, x, 0.0), axis=0, keepdims=True)


S5_SEG = 8
S5_STEPS = 32
S5_WIDTH = S5_TILES * S5_LANES


def _seg_rows(block):
    return jnp.swapaxes(block, 0, 1).reshape(block.shape[1] * S5_SEG, block.shape[2])


def _seg_block(rows):
    return jnp.swapaxes(rows.reshape(rows.shape[0] // S5_SEG, S5_SEG, rows.shape[1]), 0, 1)


def _seq_specs(n_i, rev):
    at = (lambda i: n_i - 1 - i) if rev else (lambda i: i)
    seg = pl.BlockSpec((S5_SEG, S5_STEPS, B_WIDTH), lambda i: (0, at(i), 0))
    x_spec = pl.BlockSpec((S5_SEG * S5_STEPS, S5_WIDTH), lambda i: (at(i), 0))
    return seg, x_spec, _fix((S5_TILES, 128, S5_LANES)), _fix((S5_TILES, S5_LANES, 128)), _fix((1, S5_WIDTH)), _fix((S5_SEG, S5_WIDTH))


def _tile_dots(dot, lhs, w_ref, lhs_width):
    return jnp.concatenate([dot(lhs[:, t * lhs_width:(t + 1) * lhs_width], w_ref[t]) for t in range(S5_TILES)], axis=1)


def _s5_entries(name, end_r, end_i, abr, abi, steps, reverse):
    def body(er_ref, ei_ref, ar_ref, ai_ref, or_ref, oi_ref):
        pr, pi_ = ar_ref[...], ai_ref[...]
        for _ in range(int(math.log2(steps))):
            pr, pi_ = pr * pr - pi_ * pi_, 2.0 * pr * pi_
        er, ei = er_ref[...], ei_ref[...]
        rows = lax.broadcasted_iota(jnp.int32, er.shape, 0)
        cr, ci = jnp.zeros_like(pr), jnp.zeros_like(pr)
        out_r, out_i = jnp.zeros_like(er), jnp.zeros_like(er)
        for g in (range(S5_SEG - 2, -1, -1) if reverse else range(1, S5_SEG)):
            src = g + 1 if reverse else g - 1
            cr, ci = _pick_row(er, src) + pr * cr - pi_ * ci, _pick_row(ei, src) + pr * ci + pi_ * cr
            out_r, out_i = jnp.where(rows == g, cr, out_r), jnp.where(rows == g, ci, out_i)
        or_ref[...] = out_r
        oi_ref[...] = out_i

    return pl.pallas_call(body, name=name, out_shape=[_sds(*end_r.shape)] * 2)(end_r, end_i, abr, abi)


def _s5_seq_fwd(u, btr, bti, ctr, cti, abr, abi, entry, store, exchange=None):
    s_len = u.shape[0]
    seg_len = s_len // S5_SEG
    n_i = seg_len // S5_STEPS
    rows = S5_SEG * S5_STEPS

    def body(u_ref, btr_ref, bti_ref, ctr_ref, cti_ref, ar_ref, ai_ref, er_ref, ei_ref, *rest):
        if store:
            xr_ref, xi_ref, y_ref, endr_ref, endi_ref, sr_ref, si_ref = rest
        else:
            endr_ref, endi_ref, sr_ref, si_ref = rest
        i = pl.program_id(0)

        @pl.when(i == 0)
        def _():
            sr_ref[...] = er_ref[...]
            si_ref[...] = ei_ref[...]

        ar = jnp.broadcast_to(ar_ref[...], (S5_SEG, S5_WIDTH))
        ai = jnp.broadcast_to(ai_ref[...], (S5_SEG, S5_WIDTH))
        ub = _seg_rows(u_ref[...])
        br, bi = _tile_dots(_bdot, ub, btr_ref, 128), _tile_dots(_bdot, ub, bti_ref, 128)
        sr, si = sr_ref[...], si_ref[...]
        for s in range(S5_STEPS):
            at = slice(S5_SEG * s, S5_SEG * (s + 1))
            sr, si = ar * sr - ai * si + br[at], ar * si + ai * sr + bi[at]
            if store:
                xr_ref[at, :] = sr
                xi_ref[at, :] = si
        sr_ref[...] = sr
        si_ref[...] = si
        if store:
            y_ref[...] = _seg_block(_tile_dots(_bdot, xr_ref[...], ctr_ref, S5_LANES) - _tile_dots(_bdot, xi_ref[...], cti_ref, S5_LANES))

        @pl.when(i == n_i - 1)
        def _():
            endr_ref[...] = sr
            endi_ref[...] = si

    seg, x_spec, b_spec, c_spec, a_spec, e_spec = _seq_specs(n_i, False)
    ends = [_sds(S5_SEG, S5_WIDTH)] * 2
    full = [_sds(s_len, S5_WIDTH)] * 2 + [_sds(S5_SEG, seg_len, B_WIDTH)] if store else []
    return _call_with_exchange(
        body, "s5_scan_fwd" if store else "s5_ends_fwd", (n_i,),
        [seg, b_spec, b_spec, c_spec, c_spec, a_spec, a_spec, e_spec, e_spec],
        ([x_spec, x_spec, seg] if store else []) + [e_spec, e_spec], full + ends,
        [pltpu.VMEM((S5_SEG, S5_WIDTH), F32)] * 2,
        (u.reshape(S5_SEG, seg_len, B_WIDTH), btr, bti, ctr, cti, abr, abi, *entry), exchange)


def _s5_seq_bwd(dy, xr, xi, u, btr, bti, ctr, cti, abr, abi, g_entry, x_entry, full, exchange=None):
    s_len = dy.shape[0]
    seg_len = s_len // S5_SEG
    n_i = seg_len // S5_STEPS
    rows = S5_SEG * S5_STEPS

    def body(*refs):
        if full:
            (dy_ref, btr_ref, bti_ref, ctr_ref, cti_ref, ar_ref, ai_ref, ger_ref, gei_ref,
             xr_ref, xi_ref, xrp_ref, xip_ref, xer_ref, xei_ref, u_ref,
             du_ref, dbtr_ref, dbti_ref, dctr_ref, dcti_ref, dar_ref, dai_ref, str_ref, sti_ref,
             sr_ref, si_ref, gr_s, gi_s) = refs
        else:
            (dy_ref, btr_ref, bti_ref, ctr_ref, cti_ref, ar_ref, ai_ref, ger_ref, gei_ref, str_ref, sti_ref, sr_ref, si_ref) = refs
        i = pl.program_id(0)

        @pl.when(i == 0)
        def _():
            sr_ref[...] = ger_ref[...]
            si_ref[...] = gei_ref[...]
            if full:
                for r in (dbtr_ref, dbti_ref, dctr_ref, dcti_ref, dar_ref, dai_ref):
                    r[...] = jnp.zeros_like(r)

        ar = jnp.broadcast_to(ar_ref[...], (S5_SEG, S5_WIDTH))
        ai = -jnp.broadcast_to(ai_ref[...], (S5_SEG, S5_WIDTH))
        dyb = _seg_rows(dy_ref[...])
        gr, gi = _tile_dots(_bdot_nt, dyb, ctr_ref, 128), -_tile_dots(_bdot_nt, dyb, cti_ref, 128)
        sr, si = sr_ref[...], si_ref[...]
        for s in range(S5_STEPS - 1, -1, -1):
            at = slice(S5_SEG * s, S5_SEG * (s + 1))
            sr, si = ar * sr - ai * si + gr[at], ar * si + ai * sr + gi[at]
            if full:
                gr_s[at, :] = sr
                gi_s[at, :] = si
        sr_ref[...] = sr
        si_ref[...] = si

        @pl.when(i == n_i - 1)
        def _():
            str_ref[...] = sr
            sti_ref[...] = si

        if full:
            g_r, g_i = gr_s[...], gi_s[...]
            du_ref[...] = _seg_block(_tile_dots(_bdot_nt, g_r, btr_ref, S5_LANES) + _tile_dots(_bdot_nt, g_i, bti_ref, S5_LANES))
            ub = _seg_rows(u_ref[...])
            xr_b, xi_b = xr_ref[...], xi_ref[...]
            for t in range(S5_TILES):
                lanes, cols = slice(t * S5_LANES, (t + 1) * S5_LANES), slice(t * 128, (t + 1) * 128)
                dbtr_ref[t] += _bdot_tn(ub[:, cols], g_r[:, lanes])
                dbti_ref[t] += _bdot_tn(ub[:, cols], g_i[:, lanes])
                dctr_ref[t] += _bdot_tn(xr_b[:, lanes], dyb[:, cols])
                dcti_ref[t] -= _bdot_tn(xi_b[:, lanes], dyb[:, cols])
            first = i == n_i - 1
            xpr = jnp.concatenate([jnp.where(first, xer_ref[...], xrp_ref[...]), xr_b[:rows - S5_SEG]], axis=0)
            xpi = jnp.concatenate([jnp.where(first, xei_ref[...], xip_ref[...]), xi_b[:rows - S5_SEG]], axis=0)
            dar_ref[...] += jnp.sum(g_r * xpr + g_i * xpi, axis=0, keepdims=True)
            dai_ref[...] += jnp.sum(g_i * xpr - g_r * xpi, axis=0, keepdims=True)

    seg, x_spec, b_spec, c_spec, a_spec, e_spec = _seq_specs(n_i, True)
    halo = pl.BlockSpec((S5_SEG, S5_WIDTH), lambda i: (jnp.maximum((n_i - 1 - i) * S5_STEPS - 1, 0), 0))
    starts = [_sds(S5_SEG, S5_WIDTH)] * 2
    in_specs = [seg, b_spec, b_spec, c_spec, c_spec, a_spec, a_spec, e_spec, e_spec]
    args = [dy.reshape(S5_SEG, seg_len, B_WIDTH), btr, bti, ctr, cti, abr, abi, *g_entry]
    state = [pltpu.VMEM((S5_SEG, S5_WIDTH), F32)] * 2
    if not full:
        return _call_with_exchange(body, "s5_starts_bwd", (n_i,), in_specs, [e_spec, e_spec], starts, state, args, None)
    return _call_with_exchange(
        body, "s5_scan_bwd", (n_i,),
        in_specs + [x_spec, x_spec, halo, halo, e_spec, e_spec, seg],
        [seg, b_spec, b_spec, c_spec, c_spec, a_spec, a_spec, e_spec, e_spec],
        [_sds(S5_SEG, seg_len, B_WIDTH)] + [_sds(S5_TILES, 128, S5_LANES)] * 2 + [_sds(S5_TILES, S5_LANES, 128)] * 2
        + [_sds(1, S5_WIDTH)] * 2 + starts,
        state + [pltpu.VMEM((rows, S5_WIDTH), F32)] * 2,
        args + [xr, xi, xr, xi, *x_entry, u.reshape(S5_SEG, seg_len, B_WIDTH)], exchange)


def _blockdiag_b(bbar_t):
    blocks = bbar_t.reshape(S5_TILES, 8, B_GROUP, B_STATE)
    return jnp.einsum('jgmp,gh->jgmhp', blocks, jnp.eye(8, dtype=F32)).reshape(S5_TILES, 128, S5_LANES)


def _blockdiag_b_t(d):
    return jnp.einsum('jgmgp->jgmp', d.reshape(S5_TILES, 8, B_GROUP, 8, B_STATE)).reshape(B_WIDTH, B_STATE)


def _blockdiag_c(c):
    blocks = c.reshape(S5_TILES, 8, B_GROUP, B_STATE)
    return jnp.einsum('jgmp,gh->jhpgm', blocks, jnp.eye(8, dtype=F32)).reshape(S5_TILES, S5_LANES, 128)


def _blockdiag_c_t(d):
    return jnp.einsum('jgpgm->jgmp', d.reshape(S5_TILES, 8, B_STATE, 8, B_GROUP)).reshape(B_GROUPS, B_GROUP, B_STATE)


def _l0_out(os, ls, ga, gb, ypre, u, x, d_skip, glu_w, glu_b, w_out, post_g, gate, exchange=None):
    s_len = x.shape[0]

    def body(o0, o1, o2, l0, l1, l2, ga_ref, gb_ref, yp_ref, u_ref, x_ref, d_ref, gw_ref, gbias_ref, w_ref, pg_ref, gt_ref, x1_ref, y_ref):
        oa = _merge_gate(*[_from_res(r[...]) for r in (o0, o1, o2, l0, l1, l2)], ga_ref[...])
        yb = _s5_gelu(yp_ref[...], u_ref[...], d_ref[...])
        ob = _s5_glu(yb, _bdot(yb, gw_ref[...]) + gbias_ref[...], gb_ref[...])
        y = _bdot(oa, w_ref[0:512, :]) + _bdot(ob, w_ref[512:1024, :])
        y_ref[...] = y
        x1_ref[...] = _post_res(y, x_ref[...], pg_ref[...], gt_ref[...])

    vec, half = _fix((1, D_MODEL)), _fix((1, 512))
    return _call_with_exchange(
        body, "l0_out", (s_len // TM,),
        [_res_spec(d) for d in DILATIONS] * 2 + [_row(TM, 512)] * 4
        + [_row(TM, D_MODEL), half, _fix((512, 512)), half, _fix((D_MODEL, D_MODEL)), vec, vec],
        [_row(TM, D_MODEL)] * 2, [_sds(s_len, D_MODEL)] * 2, [],
        (*os, *ls, ga, gb, ypre, u, x, d_skip, glu_w, glu_b, w_out, post_g, gate), exchange)


def _l0_out_bwd(os, ls, ga, gb, ypre, u, x, y, d_skip, glu_w, glu_b, w_out, post_g, gate, dx1, exchange=None):
    s_len = x.shape[0]

    def body(o0, o1, o2, l0, l1, l2, ga_ref, gb_ref, yp_ref, u_ref, x_ref, y_ref, d_ref, gw_ref, gbias_ref, w_ref, pg_ref, gt_ref, dx1_ref,
             do0, do1, do2, dl0, dl1, dl2, dga_ref, dgb_ref, dyp_ref, du_ref, dd_ref, dgw_ref, dgbias_ref, dw_ref, dpg_ref, dgt_ref):
        _zero_at_first([dd_ref, dgw_ref, dgbias_ref, dw_ref, dpg_ref, dgt_ref])
        _, vjp2 = jax.vjp(_post_res, y_ref[...], x_ref[...], pg_ref[...], gt_ref[...])
        dy, _, dpg, dgt = vjp2(dx1_ref[...])
        _acc(dpg_ref, dpg)
        _acc(dgt_ref, dgt)
        oa, vjp_a = jax.vjp(_merge_gate, *[_from_res(r[...]) for r in (o0, o1, o2, l0, l1, l2)], ga_ref[...])
        yb, vjp_g = jax.vjp(_s5_gelu, yp_ref[...], u_ref[...], d_ref[...])
        gl = _bdot(yb, gw_ref[...]) + gbias_ref[...]
        ob, vjp_b = jax.vjp(_s5_glu, yb, gl, gb_ref[...])
        dw_ref[0:512, :] += _bdot_tn(oa, dy)
        dw_ref[512:1024, :] += _bdot_tn(ob, dy)
        d1, d2, d3, e1, e2, e3, dga = vjp_a(_bdot_nt(dy, w_ref[0:512, :]))
        for ref, val, d in zip((do0, do1, do2, dl0, dl1, dl2), (d1, d2, d3, e1, e2, e3), DILATIONS * 2):
            ref[...] = _to_res(val, d)
        dga_ref[...] = dga
        dyb, dgl, dgb = vjp_b(_bdot_nt(dy, w_ref[512:1024, :]))
        dgb_ref[...] = dgb
        dgw_ref[...] += _bdot_tn(yb, dgl)
        _acc(dgbias_ref, jnp.sum(dgl, axis=0, keepdims=True))
        dyp, du, dd = vjp_g(dyb + _bdot_nt(dgl, gw_ref[...]))
        dyp_ref[...] = dyp
        du_ref[...] = du
        _acc(dd_ref, dd)

    vec, half = _fix((1, D_MODEL)), _fix((1, 512))
    r5, r10 = _row(TM, 512), _row(TM, D_MODEL)
    res6 = [_res_spec(d) for d in DILATIONS] * 2
    return _call_with_exchange(
        body, "l0_out_bwd", (s_len // TM,),
        res6 + [r5] * 4 + [r10, r10, half, _fix((512, 512)), half, _fix((D_MODEL, D_MODEL)), vec, vec, r10],
        res6 + [r5] * 4 + [half, _fix((512, 512)), half, _fix((D_MODEL, D_MODEL)), vec, vec],
        [_sds(*_res_shape(s_len, d)) for d in DILATIONS] * 2 + [_sds(s_len, 512)] * 4
        + [_sds(1, 512), _sds(512, 512), _sds(1, 512), _sds(D_MODEL, D_MODEL), _sds(1, D_MODEL), _sds(1, D_MODEL)],
        [], (*os, *ls, ga, gb, ypre, u, x, y, d_skip, glu_w, glu_b, w_out, post_g, gate, dx1), exchange)


def _l1_front(x, pre_g, scale, shift, w_in):
    s_len = x.shape[0]

    def body(x_ref, g_ref, sc_ref, sh_ref, w_ref, raw_ref, gate_ref, ba_ref, h_ref):
        hb = _pre_mod(x_ref[...], g_ref[...], sc_ref[...], sh_ref[...]).astype(BF)
        h_ref[...] = hb
        z = jnp.dot(hb, w_ref[...], preferred_element_type=F32)
        raw_ref[...] = z[:, 0:QKV]
        gate_ref[...] = z[:, QKV:QKV + 1024]
        ba_ref[...] = z[:, QKV + 1024:C_IN_PAD]

    vec = _fix((1, D_MODEL))
    return pl.pallas_call(
        body, name="l1_front", grid=(s_len // TM,),
        in_specs=[_row(TM, D_MODEL), vec, vec, vec, _fix((D_MODEL, C_IN_PAD))],
        out_specs=[_row(TM, QKV), _row(TM, 1024), _row(TM, 128), _row(TM, D_MODEL)],
        out_shape=[_sds(s_len, QKV), _sds(s_len, 1024), _sds(s_len, 128), _sds(s_len, D_MODEL, dtype=BF)],
        compiler_params=_cp("arbitrary"),
    )(x, pre_g, scale, shift, w_in)


def _bg_fn(ba, alog_row, dtb_row):
    lane = lax.broadcasted_iota(jnp.int32, (1, 128), 1)
    g = -jnp.exp(alog_row) * jax.nn.softplus(ba + dtb_row)
    return jnp.where(lane < C_HEADS, jax.nn.sigmoid(ba), jnp.where(lane < 2 * C_HEADS, g, 0.0))


def _act_q(c):
    q = jax.nn.silu(c)
    return q * lax.rsqrt(jnp.sum(q * q, axis=-1, keepdims=True) + EPS) * (C_DK ** -0.5)


def _act_k(c):
    k = jax.nn.silu(c)
    return k * lax.rsqrt(jnp.sum(k * k, axis=-1, keepdims=True) + EPS)


def _act_of(s):
    return _act_q if s < 8 else (_act_k if s < 16 else jax.nn.silu)


def _conv_taps(prev8, tile_ref, sl, next8=None):
    rows = tile_ref.shape[0]
    head = jnp.concatenate([prev8, tile_ref[0:8, sl]], axis=0)
    tail = None if next8 is None else jnp.concatenate([tile_ref[rows - 8:rows, sl], next8], axis=0)
    taps = []
    for j in range(C_CONV):
        shift = C_CONV - 1 - j
        pieces = [head[8:] if shift == 0 else pltpu.roll(head, shift, 0)[8:], tile_ref[pl.ds(8 - shift, rows - 8), sl]]
        if tail is not None:
            pieces.append(tail[8:] if shift == 0 else pltpu.roll(tail, shift, 0)[8:])
        taps.append(jnp.concatenate(pieces, axis=0))
    return taps


def _gdn_prep(raw, ba, conv_w, alog_row, dtb_row):
    s_len = raw.shape[0]

    def body(raw_ref, halo_ref, ba_ref, w_ref, al_ref, dt_ref, qkv_ref, bg_ref):
        bg_ref[...] = _bg_fn(ba_ref[...], al_ref[...], dt_ref[...])
        has_prev = (pl.program_id(0) > 0).astype(F32)
        for s in range(24):
            sl = slice(s * 128, (s + 1) * 128)
            taps = _conv_taps(halo_ref[:, sl] * has_prev, raw_ref, sl)
            conv = w_ref[3:4, sl] * taps[3]
            for j in range(3):
                conv = conv + w_ref[j:j + 1, sl] * taps[j]
            qkv_ref[:, sl] = _act_of(s)(conv)

    halo = pl.BlockSpec((8, QKV), lambda i: (jnp.maximum(i * (TM // 8) - 1, 0), 0))
    row128 = _fix((1, 128))
    return pl.pallas_call(
        body, name="gdn_prep", grid=(s_len // TM,),
        in_specs=[_row(TM, QKV), halo, _row(TM, 128), _fix((C_CONV, QKV)), row128, row128],
        out_specs=[_row(TM, QKV), _row(TM, 128)],
        out_shape=[_sds(s_len, QKV), _sds(s_len, 128)],
        compiler_params=_cp("arbitrary"),
    )(raw, raw, ba, conv_w, alog_row, dtb_row)


def _gdn_prep_bwd(raw, ba, conv_w, alog_row, dtb_row, dq, dk, dv, dbg):
    s_len = raw.shape[0]
    n_tiles = s_len // TM

    def body(raw_ref, prev_ref, next_ref, ba_ref, w_ref, al_ref, dt_ref, dq_ref, dqn_ref, dk_ref, dkn_ref, dv_ref, dvn_ref, dbg_ref,
             draw_ref, dba_ref, dw_ref, dal_ref, ddt_ref, dconv_ref):
        _zero_at_first([dw_ref, dal_ref, ddt_ref])
        i = pl.program_id(0)
        _, vjp_bg = jax.vjp(_bg_fn, ba_ref[...], al_ref[...], dt_ref[...])
        dba, dal, ddt = vjp_bg(dbg_ref[...])
        dba_ref[...] = dba
        _acc(dal_ref, dal)
        _acc(ddt_ref, ddt)
        has_prev = (i > 0).astype(F32)
        has_next = (i < n_tiles - 1).astype(F32)
        ct_refs = ((dq_ref, dqn_ref), (dk_ref, dkn_ref), (dv_ref, dvn_ref))
        for s in range(24):
            sl = slice(s * 128, (s + 1) * 128)
            hl = slice((s % 8) * 128, (s % 8 + 1) * 128)
            tile_ref, nxt_ref = ct_refs[s // 8]
            taps = _conv_taps(prev_ref[:, sl] * has_prev, raw_ref, sl, next_ref[:, sl] * has_next)
            conv = w_ref[3:4, sl] * taps[3]
            for j in range(3):
                conv = conv + w_ref[j:j + 1, sl] * taps[j]
            ct = jnp.concatenate([tile_ref[:, hl], nxt_ref[:, hl] * has_next], axis=0)
            _, vjp_act = jax.vjp(_act_of(s), conv)
            dconv, = vjp_act(ct)
            dconv_ref[...] = dconv
            draw = w_ref[3:4, sl] * dconv[:TM]
            for j in range(3):
                draw = draw + w_ref[j:j + 1, sl] * dconv_ref[pl.ds(3 - j, TM), :]
            draw_ref[:, sl] = draw
            for j in range(4):
                dw_ref[j:j + 1, sl] += jnp.sum(dconv[:TM] * taps[j][:TM], axis=0, keepdims=True)

    prev = pl.BlockSpec((8, QKV), lambda i: (jnp.maximum(i * (TM // 8) - 1, 0), 0))
    nxt = lambda n: pl.BlockSpec((8, n), lambda i: (jnp.minimum((i + 1) * (TM // 8), s_len // 8 - 1), 0))
    row128 = _fix((1, 128))
    ct_specs = [_row(TM, 1024), nxt(1024)] * 3
    return pl.pallas_call(
        body, name="gdn_prep_bwd", grid=(n_tiles,),
        in_specs=[_row(TM, QKV), prev, nxt(QKV), _row(TM, 128), _fix((C_CONV, QKV)), row128, row128] + ct_specs + [_row(TM, 128)],
        out_specs=[_row(TM, QKV), _row(TM, 128), _fix((C_CONV, QKV)), row128, row128],
        out_shape=[_sds(s_len, QKV), _sds(s_len, 128), _sds(C_CONV, QKV), _sds(1, 128), _sds(1, 128)],
        scratch_shapes=[pltpu.VMEM((TM + 8, 128), F32)],
        compiler_params=_cp("arbitrary"),
    )(raw, raw, raw, ba, conv_w, alog_row, dtb_row, dq, dq, dk, dk, dv, dv, dbg)


def _tein(eq, a, b):
    return jnp.einsum(eq, a, b, precision=lax.Precision.HIGH, preferred_element_type=F32)


def _unit_lower_inverse(lower):
    ri = lax.broadcasted_iota(jnp.int32, (C_CHUNK, C_CHUNK), 0)
    ci = lax.broadcasted_iota(jnp.int32, (C_CHUNK, C_CHUNK), 1)
    eye = (ri == ci).astype(F32)[None]
    p_mat = -lower
    inv = eye + p_mat
    for _ in range(5):
        p_mat = _bein('hij,hjk->hik', p_mat, p_mat)
        inv = inv + _bein('hij,hjk->hik', inv, p_mat)
    inv = _tein('hij,hjk->hik', inv, 2.0 * eye - _tein('hij,hjk->hik', eye + lower, inv))
    return jnp.where((ri >= ci)[None], inv, 0.0)


@jax.custom_vjp
def _known_inverse(lower, inv):
    return inv


def _known_inverse_fwd(lower, inv):
    return inv, inv


def _known_inverse_bwd(inv, d_inv):
    d_lower = -_bein('hik,hjk->hij', _bein('hji,hjk->hik', inv, d_inv), inv)
    return d_lower, jnp.zeros_like(inv)


_known_inverse.defvjp(_known_inverse_fwd, _known_inverse_bwd)


def _gdn_local(q, k, v, bgs, inv_known=None):
    lane = lax.broadcasted_iota(jnp.int32, (1, 128), 1)
    ri = lax.broadcasted_iota(jnp.int32, (C_CHUNK, C_CHUNK), 0)
    ci = lax.broadcasted_iota(jnp.int32, (C_CHUNK, C_CHUNK), 1)
    row_id = lax.broadcasted_iota(jnp.int32, (128, C_CHUNK), 0)
    beta, gc, gcj = [], [], []
    for bg in bgs:
        gc_t = _hdot((ri >= ci).astype(F32), bg)
        gc_rows = gc_t.T
        for h in range(C_HEADS):
            beta.append(jnp.sum(jnp.where(lane == h, bg, 0.0), axis=-1, keepdims=True))
            gc.append(jnp.sum(jnp.where(lane == C_HEADS + h, gc_t, 0.0), axis=-1, keepdims=True))
            gcj.append(jnp.sum(jnp.where(row_id == C_HEADS + h, gc_rows, 0.0), axis=0, keepdims=True))
    beta, gc, gcj = jnp.stack(beta, axis=0), jnp.stack(gc, axis=0), jnp.stack(gcj, axis=0)
    tril, strict = (ri >= ci)[None], (ri > ci)[None]
    decay = jnp.exp(jnp.where(tril, gc - gcj, -1e30))
    kb = k * beta
    lower = jnp.where(strict, _bein('hid,hjd->hij', kb, k) * decay, 0.0)
    inv = _unit_lower_inverse(lower) if inv_known is None else _known_inverse(lower, inv_known)
    egc = jnp.exp(gc)
    u_c = _bein('hij,hjd->hid', inv, v * beta)
    w_c = _bein('hij,hjd->hid', inv, kb * egc)
    aqk = _bein('hid,hjd->hij', q, k) * decay
    rowi = lax.broadcasted_iota(jnp.int32, (1, C_CHUNK, 1), 1)
    g_last = jnp.sum(jnp.where(rowi == C_CHUNK - 1, gc, 0.0), axis=1, keepdims=True)
    kd = k * jnp.exp(g_last - gc)
    return (u_c, w_c, aqk, q * egc, kd, jnp.exp(g_last)), inv


def _gdn_state(local, state):
    u_c, w_c, aqk, qg, kd, dec = local
    v_new = u_c - _bein('hik,hkv->hiv', w_c, state)
    o = _bein('hik,hkv->hiv', qg, state) + _bein('hij,hjv->hiv', aqk, v_new)
    return o, state * dec + _bein('hik,hiv->hkv', kd, v_new)


C_SUB = 4


def _gdn_group(q, k, v, bgs, state, inv_known=None):
    local, inv = _gdn_local(q, k, v, bgs, inv_known)
    outs = []
    for s in range(len(bgs)):
        o, state = _gdn_state(tuple(t[s * C_HEADS:(s + 1) * C_HEADS] for t in local), state)
        outs.append(o)
    return outs, state, inv


def _heads(ref):
    return jnp.stack([ref[s * C_CHUNK:(s + 1) * C_CHUNK, h * C_DK:(h + 1) * C_DK] for s in range(C_SUB) for h in range(C_HEADS)], axis=0)


def _put_heads(ref, sub, val):
    rows = slice(sub * C_CHUNK, (sub + 1) * C_CHUNK)
    for h in range(C_HEADS):
        ref[rows, h * C_DK:(h + 1) * C_DK] = val[h]


def _gdn_specs(s_len, rev):
    rows = C_SUB * C_CHUNK
    n_g = s_len // rows
    at = (lambda i: n_g - 1 - i) if rev else (lambda i: i)
    col = lambda c: pl.BlockSpec((rows, 1024), lambda i: (at(i), c))
    row128 = pl.BlockSpec((rows, 128), lambda i: (at(i), 0))
    state = pl.BlockSpec((1, C_HEADS, C_DK, C_DK), lambda i: (at(i), 0, 0, 0))
    inv = pl.BlockSpec((1, C_SUB * C_HEADS, C_CHUNK, C_CHUNK), lambda i: (at(i), 0, 0, 0))
    return n_g, col, row128, state, inv


def _gdn_fwd(qkv, bg):
    s_len = qkv.shape[0]
    n_g, col, row128, state_spec, inv_spec = _gdn_specs(s_len, False)

    def body(q_ref, k_ref, v_ref, bg_ref, o_ref, ss_ref, inv_ref, st_ref):
        _zero_at_first([st_ref])
        s0 = st_ref[...]
        ss_ref[0] = s0
        bgs = [bg_ref[s * C_CHUNK:(s + 1) * C_CHUNK, :] for s in range(C_SUB)]
        outs, s2, inv = _gdn_group(_heads(q_ref), _heads(k_ref), _heads(v_ref), bgs, s0)
        st_ref[...] = s2
        inv_ref[0] = inv
        for s in range(C_SUB):
            _put_heads(o_ref, s, outs[s])

    return pl.pallas_call(
        body, name="gdn_fwd", grid=(n_g,),
        in_specs=[col(0), col(1), col(2), row128],
        out_specs=[col(0), state_spec, inv_spec],
        out_shape=[_sds(s_len, 1024), _sds(n_g, C_HEADS, C_DK, C_DK), _sds(n_g, C_SUB * C_HEADS, C_CHUNK, C_CHUNK)],
        scratch_shapes=[pltpu.VMEM((C_HEADS, C_DK, C_DK), F32)],
        compiler_params=_cp("arbitrary"),
    )(qkv, qkv, qkv, bg)


def _gdn_bwd(qkv, bg, states, invs, do):
    s_len = qkv.shape[0]
    n_g, col, row128, state_spec, inv_spec = _gdn_specs(s_len, True)

    def body(q_ref, k_ref, v_ref, bg_ref, ss_ref, inv_ref, do_ref, dq_ref, dk_ref, dv_ref, dbg_ref, ds_ref):
        _zero_at_first([ds_ref])
        inv_known = inv_ref[0]

        def group(q, k, v, bgs, st):
            outs, st2, _ = _gdn_group(q, k, v, bgs, st, inv_known)
            return outs, st2

        bgs = [bg_ref[s * C_CHUNK:(s + 1) * C_CHUNK, :] for s in range(C_SUB)]
        _, vjp = jax.vjp(group, _heads(q_ref), _heads(k_ref), _heads(v_ref), bgs, ss_ref[0])
        douts = [jnp.stack([do_ref[s * C_CHUNK:(s + 1) * C_CHUNK, h * C_DK:(h + 1) * C_DK] for h in range(C_HEADS)], axis=0)
                 for s in range(C_SUB)]
        dq, dk, dv, dbgs, ds = vjp((douts, ds_ref[...]))
        ds_ref[...] = ds
        for s in range(C_SUB):
            dbg_ref[s * C_CHUNK:(s + 1) * C_CHUNK, :] = dbgs[s]
            for ref, val in ((dq_ref, dq), (dk_ref, dk), (dv_ref, dv)):
                _put_heads(ref, s, val[s * C_HEADS:(s + 1) * C_HEADS])

    return pl.pallas_call(
        body, name="gdn_bwd", grid=(n_g,),
        in_specs=[col(0), col(1), col(2), row128, state_spec, inv_spec, col(0)],
        out_specs=[col(0), col(0), col(0), row128],
        out_shape=[_sds(s_len, 1024)] * 3 + [_sds(s_len, 128)],
        scratch_shapes=[pltpu.VMEM((C_HEADS, C_DK, C_DK), F32)],
        compiler_params=_cp("arbitrary"),
    )(qkv, qkv, qkv, bg, states, invs, do)


def _head_norm_gate(o, gate, norm_g):
    return (_rms(o) * norm_g) * jax.nn.silu(gate)


def _l1_out_fb(o, gate_c, x1, target, norm_g, w_out, post_g, gate):
    s_len = x1.shape[0]

    def body(o_ref, gc_ref, x1_ref, t_ref, ng_ref, w_ref, pg_ref, gt_ref,
             loss_ref, dres_ref, do_ref, dgc_ref, dw_ref, dng_ref, dpg_ref, dgt_ref):
        _zero_at_first([loss_ref, dw_ref, dng_ref, dpg_ref, dgt_ref])
        ng = ng_ref[...]
        ons, vjps = [], []
        for h in range(C_HEADS):
            sl = slice(h * C_DK, (h + 1) * C_DK)
            on, vjp_h = jax.vjp(_head_norm_gate, o_ref[:, sl], gc_ref[:, sl], ng)
            ons.append(on)
            vjps.append(vjp_h)
        on_all = jnp.concatenate(ons, axis=-1)
        y = _bdot(on_all, w_ref[...])
        x2, vjp2 = jax.vjp(_post_res, y, x1_ref[...], pg_ref[...], gt_ref[...])
        err = x2 - t_ref[...]
        _acc(loss_ref, jnp.full((1, 128), 0.5 * jnp.sum(jnp.mean(err * err, axis=-1)), F32))
        dx2 = err * (1.0 / D_MODEL)
        dy, _, dpg, dgt = vjp2(dx2)
        dres_ref[...] = dx2
        _acc(dpg_ref, dpg)
        _acc(dgt_ref, dgt)
        dw_ref[...] += _bdot_tn(on_all, dy)
        don = _bdot_nt(dy, w_ref[...])
        for h in range(C_HEADS):
            sl = slice(h * C_DK, (h + 1) * C_DK)
            do_h, dgc_h, dng = vjps[h](don[:, sl])
            do_ref[:, sl] = do_h
            dgc_ref[:, sl] = dgc_h
            _acc(dng_ref, dng)

    vec, r10 = _fix((1, D_MODEL)), _row(TM, D_MODEL)
    row128 = _fix((1, 128))
    return pl.pallas_call(
        body, name="l1_out_fb", grid=(s_len // TM,),
        in_specs=[r10, r10, r10, r10, row128, _fix((D_MODEL, D_MODEL)), vec, vec],
        out_specs=[row128, r10, r10, r10, _fix((D_MODEL, D_MODEL)), row128, vec, vec],
        out_shape=[_sds(1, 128), _sds(s_len, D_MODEL), _sds(s_len, D_MODEL), _sds(s_len, D_MODEL),
                   _sds(D_MODEL, D_MODEL), _sds(1, 128), _sds(1, D_MODEL), _sds(1, D_MODEL)],
        compiler_params=_cp("arbitrary"),
    )(o, gate_c, x1, target, norm_g, w_out, post_g, gate)


def _row_of(v, width, at):
    return jnp.zeros((1, width), F32).at[0, at:at + v.shape[-1]].set(v.reshape(-1))


def _local_step(x, target, mod, wd, comm=None):
    s_len = x.shape[0]
    shift0, scale0, gate0 = (mod[0:1, i * 1024:(i + 1) * 1024] for i in range(3))
    shift1, scale1, gate1 = (mod[1:2, i * 1024:(i + 1) * 1024] for i in range(3))
    pre_g0, pre_g1 = wd["pre_g"][0:1], wd["pre_g"][1:2]
    post_g0, post_g1 = wd["post_g"][0:1], wd["post_g"][1:2]
    w_in0 = wd["ab_w_in"].astype(BF)
    d_skip, glu_b = wd["s5_d"].reshape(1, 512), wd["s5_glu_b"].reshape(1, 512)
    norm_g = wd["gdn_norm_g"].reshape(1, 128)
    alog_row = _row_of(wd["gdn_a_log"], 128, C_HEADS)
    dtb_row = _row_of(wd["gdn_dt_bias"], 128, C_HEADS)
    conv_w = wd["gdn_conv"]

    a_re, a_im = wd["s5_a_re"], wd["s5_a_im"]
    log_dt = wd["s5_log_dt"].reshape(B_GROUPS, 1)
    bt_re = wd["s5_b_re"].transpose(0, 2, 1).reshape(B_WIDTH, B_STATE)
    bt_im = wd["s5_b_im"].transpose(0, 2, 1).reshape(B_WIDTH, B_STATE)
    abar_r, abar_i, bbar_r, bbar_i = _s5_params(a_re, a_im, log_dt, bt_re, bt_im)
    abr, abi = abar_r.reshape(1, -1), abar_i.reshape(1, -1)
    btr, bti = _blockdiag_b(bbar_r).astype(BF), _blockdiag_b(bbar_i).astype(BF)
    ctr, cti = _blockdiag_c(wd["s5_c_re"]).astype(BF), _blockdiag_c(wd["s5_c_im"]).astype(BF)

    table = _bucket_table()
    biases = _attn_bias(wd["rel_bias"], jnp.asarray(table))
    front = _l0_front(x, pre_g0, scale0, shift0, w_in0)
    qs, ks, vs = front[0:3], front[3:6], front[6:9]
    u, ga, gb, h0 = front[9:]
    riders = [None] * 4 if comm is None else comm.late_exchanges()
    os, ls, got = [], [], []
    for i in range(3):
        (o_d, l_d), g = _attn_fwd(qs[i], ks[i], vs[i], biases[i], exchange=riders[i])
        os.append(o_d)
        ls.append(l_d)
        got.append(g)
    seg_len = s_len // S5_SEG
    zero_state = (jnp.zeros((S5_SEG, S5_WIDTH), F32),) * 2
    ends, _ = _s5_seq_fwd(u, btr, bti, ctr, cti, abr, abi, zero_state, False)
    x_entry = _s5_entries("s5_entries_fwd", *ends, abr, abi, seg_len, False)
    (xr, xi, ypre3, _, _), g = _s5_seq_fwd(u, btr, bti, ctr, cti, abr, abi, x_entry, True, exchange=riders[3])
    got.append(g)
    ypre = ypre3.reshape(s_len, B_WIDTH)
    rider = None
    if comm is not None:
        mine0, mine1 = comm.late_halves(got)
        wd = {**wd, **comm.full_weights(["ab_w_out", "s5_glu_w"], mine0, _sibling_exchange("gather_w_sibling_l0", mine0))}
        rider = (mine1, "sibling")
    w_out0 = wd["ab_w_out"].astype(BF)
    glu_w = wd["s5_glu_w"].astype(BF)
    (x1, y0), theirs1 = _l0_out(os, ls, ga, gb, ypre, u, x, d_skip, glu_w, glu_b, w_out0, post_g0, gate0, exchange=rider)
    if comm is not None:
        wd = {**wd, **comm.full_weights(["gdn_w_in", "gdn_w_out"], mine1, theirs1)}
    w_in1 = jnp.concatenate([wd["gdn_w_in"], jnp.zeros((D_MODEL, C_IN_PAD - wd["gdn_w_in"].shape[1]), wd["gdn_w_in"].dtype)], axis=1).astype(BF)
    w_out1 = wd["gdn_w_out"].astype(BF)

    raw, gate_c, ba, h1 = _l1_front(x1, pre_g1, scale1, shift1, w_in1)
    qkv, bg = _gdn_prep(raw, ba, conv_w, alog_row, dtb_row)
    o_gdn, states, invs = _gdn_fwd(qkv, bg)
    loss_row, dres1, do_gdn, dgate_c, dw_out1, dnorm_g, dpost_g1, dgate1 = _l1_out_fb(
        o_gdn, gate_c, x1, target, norm_g, w_out1, post_g1, gate1)

    dq1, dk1, dv1, dbg = _gdn_bwd(qkv, bg, states, invs, do_gdn)
    draw, dba, dconv_w, dalog_row, ddtb_row = _gdn_prep_bwd(raw, ba, conv_w, alog_row, dtb_row, dq1, dk1, dv1, dbg)
    dz1, dx1, dpre_g1, dscale1, dshift1 = _front_bwd(
        "l1_front_bwd", x1, pre_g1, scale1, shift1, w_in1, dres1, [[draw], [dgate_c], [dba]], [QKV, 1024, 128])
    dw_in1 = _matmul_tn("l1_dw_in", h1, dz1, 1408)

    n_w1 = wd["gdn_w_in"].shape[1]
    l1_names, l0_names = ["gdn_w_in", "gdn_w_out"], ["ab_w_out", "s5_glu_w"]
    rider = None if comm is None else (comm.split_halves({"gdn_w_in": dw_in1[:, :n_w1], "gdn_w_out": dw_out1}), "sibling")
    l0b, from_sibling1 = _l0_out_bwd(os, ls, ga, gb, ypre, u, x, y0, d_skip, glu_w, glu_b, w_out0, post_g0, gate0, dx1, exchange=rider)
    dos, dls = l0b[0:3], l0b[3:6]
    dga, dgb, dypre, du_skip, dd_skip, dglu_w, dglu_b, dw_out0, dpost_g0, dgate0 = l0b[6:]
    rider = None if comm is None else (comm.split_halves({"ab_w_out": dw_out0, "s5_glu_w": dglu_w}), "sibling")
    starts, _ = _s5_seq_bwd(dypre, None, None, None, btr, bti, ctr, cti, abr, abi, zero_state, None, False)
    g_entry = _s5_entries("s5_entries_bwd", *starts, abr, -abi, seg_len, True)
    (du3, dbtr, dbti, dctr, dcti, dabr, dabi, _, _), from_sibling0 = _s5_seq_bwd(
        dypre, xr, xi, u, btr, bti, ctr, cti, abr, abi, g_entry, x_entry, True, exchange=rider)
    du_scan = du3.reshape(s_len, B_WIDTH)
    riders = [None] * 3
    if comm is not None:
        riders = [(comm.chip_partials(l1_names, from_sibling1), "scatter"), (comm.chip_partials(l0_names, from_sibling0), "scatter"), None]
    dqs, dks, dvs, dbs = [], [], [], []
    for i in range(3):
        (dq_d, dk_d, dv_d, db_d), got_d = _attn_bwd(qs[i], ks[i], vs[i], biases[i], os[i], ls[i], dos[i], dls[i], exchange=riders[i])
        dqs.append(dq_d)
        dks.append(dk_d)
        dvs.append(dv_d)
        dbs.append(db_d)
        if comm is not None and riders[i] is not None:
            comm.received.update(zip((l1_names, l0_names)[i], got_d))
    parts = [dqs, dks, dvs, [du_skip, du_scan], [dga], [dgb]]
    dz0, grad_x, dpre_g0, dscale0, dshift0 = _front_bwd(
        "l0_front_bwd", x, pre_g0, scale0, shift0, w_in0, dx1, parts, [512] * 6)
    dw_in0 = _matmul_tn("l0_dw_in", h0, dz0, 768)

    idx_rows = jnp.asarray(table.reshape(3, -1), F32)
    drel = _rel_bias_grad(dbs, idx_rows).T
    da_re, da_im, dlog_dt, dbt_re, dbt_im = _s5_params_bwd(
        a_re, a_im, log_dt, bt_re, bt_im, dabr.reshape(B_GROUPS, B_STATE), dabi.reshape(B_GROUPS, B_STATE),
        _blockdiag_b_t(dbtr), _blockdiag_b_t(dbti))
    unb = lambda d: d.reshape(B_GROUPS, B_GROUP, B_STATE).transpose(0, 2, 1)
    grads = {
        "pre_g": jnp.concatenate([dpre_g0, dpre_g1], 0), "post_g": jnp.concatenate([dpost_g0, dpost_g1], 0),
        "rel_bias": drel, "ab_w_in": dw_in0, "ab_w_out": dw_out0,
        "s5_a_re": da_re, "s5_a_im": da_im, "s5_log_dt": dlog_dt.reshape(B_GROUPS),
        "s5_b_re": unb(dbt_re), "s5_b_im": unb(dbt_im),
        "s5_c_re": _blockdiag_c_t(dctr), "s5_c_im": _blockdiag_c_t(dcti),
        "s5_d": dd_skip.reshape(512), "s5_glu_w": dglu_w, "s5_glu_b": dglu_b.reshape(512),
        "gdn_w_in": dw_in1[:, :wd["gdn_w_in"].shape[1]], "gdn_conv": dconv_w,
        "gdn_a_log": dalog_row[0, C_HEADS:2 * C_HEADS], "gdn_dt_bias": ddtb_row[0, C_HEADS:2 * C_HEADS],
        "gdn_norm_g": dnorm_g.reshape(128), "gdn_w_out": dw_out1,
    }
    dmod = jnp.concatenate([jnp.concatenate([dshift0, dscale0, dgate0], 1), jnp.concatenate([dshift1, dscale1, dgate1], 1)], 0)
    return loss_row[0, 0], grad_x, grads, dmod


def _place():
    return lax.axis_index("x"), lax.axis_index("y"), lax.axis_index("c")


def _flip(v, bit):
    return 1 - v if bit else v


def _hbm_call(name, body, arrs, out_shapes, n_sem):
    any_spec = pl.BlockSpec(memory_space=pl.ANY)
    return pl.pallas_call(
        body, name=name,
        in_specs=[any_spec] * len(arrs), out_specs=[any_spec] * len(out_shapes), out_shape=out_shapes,
        scratch_shapes=[pltpu.SemaphoreType.DMA((n_sem,)), pltpu.SemaphoreType.DMA((n_sem,))],
    )(*arrs)


def _own_slot(gathered, own, slot):
    idx = lax.broadcasted_iota(jnp.int32, (gathered.shape[0],) + (1,) * own.ndim, 0)
    return jnp.where(idx == slot, own[None], gathered)


def _all_gather8(name, arr):
    def body(x_ref, out_ref, send_sems, recv_sems):
        x, y, c = _place()
        me = 4 * x + 2 * y + c
        sends, recvs = [], []
        for m in range(1, 8):
            peer = (_flip(x, m & 4), _flip(y, m & 2), _flip(c, m & 1))
            sends.append(pltpu.make_async_remote_copy(x_ref, out_ref.at[me], send_sems.at[m - 1], recv_sems.at[m - 1],
                                                      device_id=peer, device_id_type=MESH))
            recvs.append(pltpu.make_async_remote_copy(x_ref, out_ref.at[4 * peer[0] + 2 * peer[1] + peer[2]], send_sems.at[m - 1],
                                                      recv_sems.at[m - 1], device_id=peer, device_id_type=MESH))
        for cp in sends:
            cp.start()
        for cp in recvs:
            cp.wait_recv()
        for cp in sends:
            cp.wait_send()

    return _hbm_call(name, body, [arr], [jax.ShapeDtypeStruct((8,) + arr.shape, arr.dtype)], 7)[0]


def _all_to_all8(name, arr):
    def body(x_ref, out_ref, send_sems, recv_sems):
        x, y, c = _place()
        me = 4 * x + 2 * y + c
        sends, recvs = [], []
        for m in range(1, 8):
            peer = (_flip(x, m & 4), _flip(y, m & 2), _flip(c, m & 1))
            peer_id = 4 * peer[0] + 2 * peer[1] + peer[2]
            sends.append(pltpu.make_async_remote_copy(x_ref.at[peer_id], out_ref.at[me], send_sems.at[m - 1], recv_sems.at[m - 1],
                                                      device_id=peer, device_id_type=MESH))
            recvs.append(pltpu.make_async_remote_copy(x_ref.at[peer_id], out_ref.at[peer_id], send_sems.at[m - 1], recv_sems.at[m - 1],
                                                      device_id=peer, device_id_type=MESH))
        for cp in sends:
            cp.start()
        for cp in recvs:
            cp.wait_recv()
        for cp in sends:
            cp.wait_send()

    return _hbm_call(name, body, [arr], [jax.ShapeDtypeStruct(arr.shape, arr.dtype)], 7)[0]


def _chip_copies(ins, outs, send_sems, recv_sems, scatter):
    x, y, c = _place()
    mine = 2 * x + y
    sends, recvs = [], []
    for a in range(len(ins)):
        for m in range(1, 4):
            px, py = _flip(x, m & 2), _flip(y, m & 1)
            k = 3 * a + m - 1
            src = ins[a].at[2 * px + py] if scatter else ins[a]
            sends.append(pltpu.make_async_remote_copy(src, outs[a].at[mine], send_sems.at[k], recv_sems.at[k],
                                                      device_id=(px, py, c), device_id_type=MESH))
            recvs.append(pltpu.make_async_remote_copy(src, outs[a].at[2 * px + py], send_sems.at[k], recv_sems.at[k],
                                                      device_id=(px, py, c), device_id_type=MESH))
    return sends, recvs


def _chip_shapes(arrs, scatter):
    return [jax.ShapeDtypeStruct(a.shape if scatter else (4,) + a.shape, a.dtype) for a in arrs]


def _chip_exchange(name, arrs, scatter):
    n = len(arrs)

    def body(*refs):
        sends, recvs = _chip_copies(refs[:n], refs[n:2 * n], refs[2 * n], refs[2 * n + 1], scatter)
        for cp in sends:
            cp.start()
        for cp in recvs:
            cp.wait_recv()
        for cp in sends:
            cp.wait_send()

    return _hbm_call(name, body, arrs, _chip_shapes(arrs, scatter), 3 * n)


def _call_with_exchange(body, name, grid, in_specs, out_specs, out_shape, scratch_shapes, args, exchange):
    if exchange is None:
        return pl.pallas_call(body, name=name, grid=grid, in_specs=in_specs, out_specs=out_specs, out_shape=out_shape,
                              scratch_shapes=scratch_shapes, compiler_params=_cp(*["arbitrary"] * len(grid)))(*args), []
    arrs, kind = exchange
    n_in, n_out, n_ex, n_scr = len(in_specs), len(out_specs), len(arrs), len(scratch_shapes)
    n_sem = n_ex if kind == "sibling" else 3 * n_ex
    ex_shapes = [jax.ShapeDtypeStruct(a.shape, a.dtype) for a in arrs] if kind == "sibling" else _chip_shapes(arrs, kind == "scatter")

    def fused(*refs):
        ins, ex_in = refs[:n_in], refs[n_in:n_in + n_ex]
        outs, ex_out = refs[n_in + n_ex:n_in + n_ex + n_out], refs[n_in + n_ex + n_out:n_in + 2 * n_ex + n_out]
        rest = refs[n_in + 2 * n_ex + n_out:]
        if kind == "sibling":
            sends = recvs = _sibling_copies(ex_in, ex_out, rest[n_scr], rest[n_scr + 1])
        else:
            sends, recvs = _chip_copies(ex_in, ex_out, rest[n_scr], rest[n_scr + 1], kind == "scatter")
        first, last = pl.program_id(0) == 0, pl.program_id(0) == grid[0] - 1
        for k in range(1, len(grid)):
            first, last = first & (pl.program_id(k) == 0), last & (pl.program_id(k) == grid[k] - 1)

        @pl.when(first)
        def _():
            for cp in sends:
                cp.start()

        body(*ins, *outs, *rest[:n_scr])

        @pl.when(last)
        def _():
            for cp in recvs:
                cp.wait_recv()
            for cp in sends:
                cp.wait_send()

    any_spec = pl.BlockSpec(memory_space=pl.ANY)
    res = pl.pallas_call(
        fused, name=name, grid=grid, in_specs=list(in_specs) + [any_spec] * n_ex, out_specs=list(out_specs) + [any_spec] * n_ex,
        out_shape=list(out_shape) + ex_shapes,
        scratch_shapes=list(scratch_shapes) + [pltpu.SemaphoreType.DMA((n_sem,))] * 2,
        compiler_params=_cp(*["arbitrary"] * len(grid)))(*args, *arrs)
    return res[:n_out], res[n_out:]


def _sibling_copies(ins, outs, send_sems, recv_sems):
    x, y, c = _place()
    return [pltpu.make_async_remote_copy(ins[a], outs[a], send_sems.at[a], recv_sems.at[a],
                                         device_id=(x, y, 1 - c), device_id_type=MESH) for a in range(len(ins))]


def _sibling_exchange(name, arrs):
    n = len(arrs)

    def body(*refs):
        copies = _sibling_copies(refs[:n], refs[n:2 * n], refs[2 * n], refs[2 * n + 1])
        for cp in copies:
            cp.start()
        for cp in copies:
            cp.wait_recv()
        for cp in copies:
            cp.wait_send()

    return _hbm_call(name, body, arrs, [jax.ShapeDtypeStruct(a.shape, a.dtype) for a in arrs], n)


def _row_tile(rows):
    for t in (256, 128, 64, 32, 16, 8):
        if rows % t == 0:
            return t
    return rows


def _pair_sum(name, a, b, out_dtype):
    rows, cols = a.shape
    tr = _row_tile(rows)

    def body(a_ref, b_ref, o_ref):
        o_ref[...] = (a_ref[...] + b_ref[...]).astype(out_dtype)

    return pl.pallas_call(body, name=name, grid=(rows // tr,), in_specs=[_row(tr, cols)] * 2, out_specs=_row(tr, cols),
                          out_shape=_sds(rows, cols, dtype=out_dtype), compiler_params=_cp("arbitrary"))(a, b)


def _chip_sum(name, recv, partial, mine):
    n, rows, cols = recv.shape
    tr = _row_tile(rows)

    def body(mine_ref, *refs):
        own = refs[n][0].astype(F32)
        acc = None
        for s in range(n):
            term = jnp.where(mine_ref[0] == s, own, refs[s][0].astype(F32))
            acc = term if acc is None else acc + term
        refs[-1][...] = acc

    def slot_spec(s):
        return pl.BlockSpec((1, tr, cols), lambda i, m: (jnp.where(m[0] == s, (s + 1) % n, s), i, 0))

    grid_spec = pltpu.PrefetchScalarGridSpec(
        num_scalar_prefetch=1, grid=(rows // tr,),
        in_specs=[slot_spec(s) for s in range(n)] + [pl.BlockSpec((1, tr, cols), lambda i, m: (m[0], i, 0))],
        out_specs=pl.BlockSpec((tr, cols), lambda i, m: (i, 0)))
    return pl.pallas_call(body, name=name, grid_spec=grid_spec, out_shape=_sds(rows, cols),
                          compiler_params=_cp("arbitrary"))(mine, *([recv] * n), partial)


def _slot_sum(name, arr):
    n, rows, cols = arr.shape
    tr = _row_tile(rows)

    def body(*refs):
        acc = refs[0][0]
        for r in refs[1:-1]:
            acc = acc + r[0]
        refs[-1][...] = acc

    specs = [pl.BlockSpec((1, tr, cols), functools.partial(lambda s, i: (s, i, 0), s)) for s in range(n)]
    return pl.pallas_call(body, name=name, grid=(rows // tr,), in_specs=specs, out_specs=_row(tr, cols),
                          out_shape=_sds(rows, cols), compiler_params=_cp("arbitrary"))(*([arr] * n))


def _adamw(name, w, g, m, v):
    rows, cols = w.shape
    tr = _row_tile(rows)

    def body(w_ref, g_ref, m_ref, v_ref, d_ref, nm_ref, nv_ref):
        g_ = g_ref[...]
        m_ = ADAM_B1 * m_ref[...] + (1.0 - ADAM_B1) * g_
        v_ = ADAM_B2 * v_ref[...] + (1.0 - ADAM_B2) * (g_ * g_)
        m_hat = m_ / (1.0 - ADAM_B1 ** ADAM_STEP)
        v_hat = v_ / (1.0 - ADAM_B2 ** ADAM_STEP)
        d_ref[...] = -ADAM_LR * (m_hat / (jnp.sqrt(v_hat) + ADAM_EPS) + ADAM_WD * w_ref[...])
        nm_ref[...] = m_
        nv_ref[...] = v_

    spec = _row(tr, cols)
    return pl.pallas_call(body, name=name, grid=(rows // tr,), in_specs=[spec] * 4, out_specs=[spec] * 3,
                          out_shape=[_sds(rows, cols)] * 3, compiler_params=_cp("arbitrary"))(w, g, m, v)


def _adamw_halves(name, w, g_mine, g_sibling, m, v, core):
    _, rows, cols = w.shape
    half = rows // 2
    tr = _row_tile(half)
    per_half = half // tr

    def body(core_ref, w_ref, gm_ref, gs_ref, m_ref, v_ref, g_ref, d_ref, nm_ref, nv_ref):
        g_ = jnp.where(pl.program_id(0) // per_half == core_ref[0], gm_ref[...], gs_ref[...])
        m_ = ADAM_B1 * m_ref[...] + (1.0 - ADAM_B1) * g_
        v_ = ADAM_B2 * v_ref[...] + (1.0 - ADAM_B2) * (g_ * g_)
        m_hat = m_ / (1.0 - ADAM_B1 ** ADAM_STEP)
        v_hat = v_ / (1.0 - ADAM_B2 ** ADAM_STEP)
        g_ref[...] = g_
        d_ref[...] = -ADAM_LR * (m_hat / (jnp.sqrt(v_hat) + ADAM_EPS) + ADAM_WD * w_ref[...])
        nm_ref[...] = m_
        nv_ref[...] = v_

    full = pl.BlockSpec((None, tr, cols), lambda i, c: (0, i, 0))
    in_half = pl.BlockSpec((tr, cols), lambda i, c: (i % per_half, 0))
    grid_spec = pltpu.PrefetchScalarGridSpec(num_scalar_prefetch=1, grid=(rows // tr,),
                                             in_specs=[full, in_half, in_half, full, full], out_specs=[full] * 4)
    return pl.pallas_call(body, name=name, grid_spec=grid_spec, out_shape=[_sds(1, rows, cols)] * 4,
                          compiler_params=_cp("arbitrary"))(core, w, g_mine, g_sibling, m, v)


def _mod_local(c_all, ada_w):
    def body(c_ref, w_ref, o_ref):
        c_act = jax.nn.silu(c_ref[...])
        for l in range(2):
            o_ref[l] = _hdot(c_act, w_ref[l])

    return pl.pallas_call(body, name="mod_local", out_shape=_sds(2, 8, ada_w.shape[2]),
                          compiler_params=pltpu.CompilerParams(vmem_limit_bytes=VMEM_LIMIT_BYTES))(c_all, ada_w)


def _ada_w_grad(c_all, dmod_cols):
    def body(c_ref, d_ref, o_ref):
        c_act = jax.nn.silu(c_ref[...])
        for l in range(2):
            o_ref[l] = lax.dot_general(c_act, d_ref[l], (((0,), (0,)), ((), ())), precision=HI, preferred_element_type=F32)

    return pl.pallas_call(body, name="ada_w_grad", out_shape=_sds(2, D_MODEL, dmod_cols.shape[2]),
                          compiler_params=pltpu.CompilerParams(vmem_limit_bytes=VMEM_LIMIT_BYTES))(c_all, dmod_cols)


_SMALL = ("ada_b", "pre_g", "post_g", "rel_bias", "s5_a_re", "s5_a_im", "s5_log_dt", "s5_b_re", "s5_b_im", "s5_c_re", "s5_c_im",
          "s5_d", "s5_glu_b", "gdn_a_log", "gdn_dt_bias", "gdn_norm_g")
_SHARDED = ("ab_w_in", "ab_w_out", "s5_glu_w", "gdn_w_in", "gdn_w_out")
_COL_SHARDED = ("ab_w_in", "gdn_w_in")
_WEIGHTS = ("ada_w", "ada_b", "pre_g", "post_g", "rel_bias", "ab_w_in", "ab_w_out", "s5_a_re", "s5_a_im", "s5_log_dt", "s5_b_re",
            "s5_b_im", "s5_c_re", "s5_c_im", "s5_d", "s5_glu_w", "s5_glu_b", "gdn_w_in", "gdn_conv", "gdn_a_log", "gdn_dt_bias",
            "gdn_norm_g", "gdn_w_out")


def _rows128(n):
    return -(-n // 128)


def _pack(arrs, total_rows):
    pieces = []
    for a in arrs:
        flat = a.reshape(-1)
        pieces.append(jnp.pad(flat, (0, _rows128(flat.shape[0]) * 128 - flat.shape[0])).reshape(-1, 128))
    used = sum(p.shape[0] for p in pieces)
    pieces.append(jnp.zeros((total_rows - used, 128), F32))
    return jnp.concatenate(pieces, axis=0)


def _unpack(buf, shapes):
    out, at = [], 0
    for shp in shapes:
        n = int(np.prod(shp))
        out.append(buf[at:at + _rows128(n)].reshape(-1)[:n].reshape(shp))
        at += _rows128(n)
    return out


def _full_from_halves(name, g):
    if name in _COL_SHARDED:
        return g.transpose(0, 2, 1, 3).reshape(2 * g.shape[2], 4 * g.shape[3])
    return g.transpose(1, 0, 2, 3).reshape(8 * g.shape[2], g.shape[3])


def _shard_major(name, g):
    if name in _COL_SHARDED:
        return g.reshape(g.shape[0], 4, g.shape[1] // 4).transpose(1, 0, 2)
    return g.reshape(4, g.shape[0] // 4, g.shape[1])


_LATE = ("ab_w_out", "s5_glu_w", "gdn_w_in", "gdn_w_out")


class _WeightExchanges:
    def __init__(self, shards, core, chip):
        self.core, self.chip = core, chip
        self.half = {}
        for name, shard in shards.items():
            h = shard.shape[0] // 2
            self.half[name] = lax.dynamic_slice_in_dim(shard.astype(BF), core * h, h, axis=0)
        self.mine, self.partial, self.received = {}, {}, {}

    def my_halves(self, names, from_chips):
        return [_own_slot(g, self.half[n], self.chip) for n, g in zip(names, from_chips)]

    def full_weights(self, names, mine, theirs):
        return {n: _full_from_halves(n, jnp.where(self.core == 0, jnp.stack([a, b], 0), jnp.stack([b, a], 0)))
                for n, a, b in zip(names, mine, theirs)}

    def first_weights(self):
        mine = self.my_halves(["ab_w_in"], _chip_exchange("gather_w_chips", [self.half["ab_w_in"]], False))
        return self.full_weights(["ab_w_in"], mine, _sibling_exchange("gather_w_sibling_first", mine))

    def late_exchanges(self):
        rows = self.half["gdn_w_in"].shape[0] // 2
        pieces = [self.half["gdn_w_in"][:rows], self.half["gdn_w_in"][rows:]]
        return [([self.half["ab_w_out"], self.half["s5_glu_w"]], "gather"), ([self.half["gdn_w_out"]], "gather"),
                ([pieces[0]], "gather"), ([pieces[1]], "gather")]

    def late_halves(self, got):
        return (self.my_halves(["ab_w_out", "s5_glu_w"], got[0]),
                self.my_halves(["gdn_w_in", "gdn_w_out"], [jnp.concatenate([got[2][0], got[3][0]], axis=1), got[1][0]]))

    def split_halves(self, grads):
        other = []
        for name, g in grads.items():
            sm = _shard_major(name, g)
            h = sm.shape[1] // 2
            self.mine[name] = lax.dynamic_slice_in_dim(sm, self.core * h, h, axis=1)
            other.append(lax.dynamic_slice_in_dim(sm, (1 - self.core) * h, h, axis=1))
        return other

    def chip_partials(self, names, from_sibling):
        for name, b in zip(names, from_sibling):
            a = self.mine[name]
            flat = lambda t: t.reshape(-1, t.shape[-1])
            self.partial[name] = _pair_sum("sum_sibling_" + name, flat(a), flat(b), BF).reshape(a.shape)
        return [self.partial[n] for n in names]


def kernel(x, c, ada_w, ada_b, pre_g, post_g, rel_bias, ab_w_in, ab_w_out, s5_a_re, s5_a_im, s5_log_dt, s5_b_re, s5_b_im, s5_c_re, s5_c_im, s5_d, s5_glu_w, s5_glu_b, gdn_w_in, gdn_conv, gdn_a_log, gdn_dt_bias, gdn_norm_g, gdn_w_out, loss_target, m_ada_w, m_ada_b, m_pre_g, m_post_g, m_rel_bias, m_ab_w_in, m_ab_w_out, m_s5_a_re, m_s5_a_im, m_s5_log_dt, m_s5_b_re, m_s5_b_im, m_s5_c_re, m_s5_c_im, m_s5_d, m_s5_glu_w, m_s5_glu_b, m_gdn_w_in, m_gdn_conv, m_gdn_a_log, m_gdn_dt_bias, m_gdn_norm_g, m_gdn_w_out, v_ada_w, v_ada_b, v_pre_g, v_post_g, v_rel_bias, v_ab_w_in, v_ab_w_out, v_s5_a_re, v_s5_a_im, v_s5_log_dt, v_s5_b_re, v_s5_b_im, v_s5_c_re, v_s5_c_im, v_s5_d, v_s5_glu_w, v_s5_glu_b, v_gdn_w_in, v_gdn_conv, v_gdn_a_log, v_gdn_dt_bias, v_gdn_norm_g, v_gdn_w_out):
    w = dict(ada_w=ada_w, ada_b=ada_b, pre_g=pre_g, post_g=post_g, rel_bias=rel_bias, ab_w_in=ab_w_in, ab_w_out=ab_w_out,
             s5_a_re=s5_a_re, s5_a_im=s5_a_im, s5_log_dt=s5_log_dt, s5_b_re=s5_b_re, s5_b_im=s5_b_im, s5_c_re=s5_c_re, s5_c_im=s5_c_im,
             s5_d=s5_d, s5_glu_w=s5_glu_w, s5_glu_b=s5_glu_b, gdn_w_in=gdn_w_in, gdn_conv=gdn_conv, gdn_a_log=gdn_a_log,
             gdn_dt_bias=gdn_dt_bias, gdn_norm_g=gdn_norm_g, gdn_w_out=gdn_w_out)
    m = dict(ada_w=m_ada_w, ada_b=m_ada_b, pre_g=m_pre_g, post_g=m_post_g, rel_bias=m_rel_bias, ab_w_in=m_ab_w_in, ab_w_out=m_ab_w_out,
             s5_a_re=m_s5_a_re, s5_a_im=m_s5_a_im, s5_log_dt=m_s5_log_dt, s5_b_re=m_s5_b_re, s5_b_im=m_s5_b_im, s5_c_re=m_s5_c_re,
             s5_c_im=m_s5_c_im, s5_d=m_s5_d, s5_glu_w=m_s5_glu_w, s5_glu_b=m_s5_glu_b, gdn_w_in=m_gdn_w_in, gdn_conv=m_gdn_conv,
             gdn_a_log=m_gdn_a_log, gdn_dt_bias=m_gdn_dt_bias, gdn_norm_g=m_gdn_norm_g, gdn_w_out=m_gdn_w_out)
    v = dict(ada_w=v_ada_w, ada_b=v_ada_b, pre_g=v_pre_g, post_g=v_post_g, rel_bias=v_rel_bias, ab_w_in=v_ab_w_in, ab_w_out=v_ab_w_out,
             s5_a_re=v_s5_a_re, s5_a_im=v_s5_a_im, s5_log_dt=v_s5_log_dt, s5_b_re=v_s5_b_re, s5_b_im=v_s5_b_im, s5_c_re=v_s5_c_re,
             s5_c_im=v_s5_c_im, s5_d=v_s5_d, s5_glu_w=v_s5_glu_w, s5_glu_b=v_s5_glu_b, gdn_w_in=v_gdn_w_in, gdn_conv=v_gdn_conv,
             gdn_a_log=v_gdn_a_log, gdn_dt_bias=v_gdn_dt_bias, gdn_norm_g=v_gdn_norm_g, gdn_w_out=v_gdn_w_out)
    ix, iy, ic = _place()
    me = 4 * ix + 2 * iy + ic
    chip = 2 * ix + iy
    n_cols = ada_w.shape[2]

    mine_first = _pack([c, gdn_conv], 32)
    first = _own_slot(_all_gather8("gather_c_conv", mine_first), mine_first, me)
    c_all = first[:, 0:8].reshape(8, D_MODEL)
    conv_full = first[0::2, 8:32].reshape(4, C_CONV, n_cols).transpose(1, 0, 2).reshape(C_CONV, 4 * n_cols)
    mine_mod = _mod_local(c_all, ada_w)
    modl = _own_slot(_all_gather8("gather_mod", mine_mod), mine_mod, me)
    mod = lax.dynamic_index_in_dim(modl[0::2], me, axis=2, keepdims=False)
    mod = mod.transpose(1, 0, 2).reshape(2, 4 * n_cols) + ada_b

    comm = _WeightExchanges({name: w[name][0] for name in _SHARDED}, ic, chip)
    wd = {name: w[name] for name in _SMALL if name != "ada_b"}
    wd = {k: (a if k in ("pre_g", "post_g", "rel_bias") else a[0]) for k, a in wd.items()}
    wd["gdn_conv"] = conv_full
    wd.update(comm.first_weights())

    loss_local, grad_x, grads, dmod = _local_step(x[0], loss_target[0], mod, wd, comm)
    loss = lax.psum(loss_local, ("x", "y", "c"))

    small_shapes = [w[name].shape for name in _SMALL] + [(C_CONV, 4 * n_cols)]
    small_rows = -(-sum(_rows128(int(np.prod(s))) for s in small_shapes) // 64) * 64
    per_dev, dmod_rows = small_rows // 8, _rows128(2 * 3 * D_MODEL)
    partial = _pack([dmod] + [grads[name] for name in _SMALL[1:]] + [grads["gdn_conv"]], small_rows)
    outbound = jnp.concatenate([partial.reshape(8, per_dev, 128), jnp.broadcast_to(partial[None, :dmod_rows], (8, dmod_rows, 128))], axis=1)
    inbound = _own_slot(_all_to_all8("reduce_small_grads", outbound), lax.dynamic_index_in_dim(outbound, me, 0, keepdims=False), me)
    my_rows = _slot_sum("sum_small_grads", inbound[:, :per_dev])
    g_small = _own_slot(_all_gather8("gather_small_grads", my_rows), my_rows, me).reshape(small_rows, 128)
    g_list = _unpack(g_small, small_shapes)
    out_g, out_d, out_m, out_v = {}, {}, {}, {}

    def update(name, g2d):
        shp = w[name].shape
        two_d = lambda a: a.reshape(-1, shp[-1])
        d_, m_, v_ = _adamw("adamw_" + name, two_d(w[name]), g2d, two_d(m[name]), two_d(v[name]))
        out_g[name], out_d[name], out_m[name], out_v[name] = (a.reshape(shp) for a in (g2d, d_, m_, v_))

    for name, g in zip(_SMALL, g_list[:-1]):
        update(name, g.reshape(-1, g.shape[-1]))
    update("gdn_conv", lax.dynamic_slice_in_dim(g_list[-1], chip * n_cols, n_cols, axis=1))

    dmod_all = inbound[:, per_dev:].reshape(8, 2, 4, n_cols)
    dmod_cols = lax.dynamic_index_in_dim(dmod_all, chip, axis=2, keepdims=False).transpose(1, 0, 2)
    update("ada_w", _ada_w_grad(c_all, dmod_cols).reshape(-1, n_cols))

    from_sibling = _sibling_exchange("reduce_sibling", comm.split_halves({"ab_w_in": grads["ab_w_in"]}))
    comm.received["ab_w_in"] = _chip_exchange("reduce_chips", comm.chip_partials(["ab_w_in"], from_sibling), True)[0]
    chip_1 = jnp.reshape(chip, (1,)).astype(jnp.int32)
    core_1 = jnp.reshape(ic, (1,)).astype(jnp.int32)
    reduced = [_chip_sum("sum_chips_" + name, comm.received[name], comm.partial[name], chip_1) for name in _SHARDED]
    for name, g_mine, g_sib in zip(_SHARDED, reduced, _sibling_exchange("reduce_share", reduced)):
        out_g[name], out_d[name], out_m[name], out_v[name] = _adamw_halves(
            "adamw_" + name, w[name], g_mine, g_sib, m[name], v[name], core_1)

    return (loss, grad_x[None], *[out_g[n] for n in _WEIGHTS], *[out_d[n] for n in _WEIGHTS],
            *[out_m[n] for n in _WEIGHTS], *[out_v[n] for n in _WEIGHTS])
```

```python
import functools
import math

import numpy as np
import jax
import jax.numpy as jnp
from jax import lax
from jax.experimental import pallas as pl
from jax.experimental.pallas import tpu as pltpu

F32 = jnp.float32
BF = jnp.bfloat16
HI = lax.Precision.HIGHEST
MESH = pl.DeviceIdType.MESH

D_MODEL = 1024
EPS = 1e-6
A_HEADS, A_HD, A_WIDTH, A_BLOCK = 8, 64, 512, 128
DILATIONS = (1, 4, 16)
N_KEYS = 128
REL_BUCKETS, REL_MAX_DIST = 32, 2048
B_WIDTH, B_GROUP, B_GROUPS, B_STATE = 512, 16, 32, 64
S5_LANES = 512
S5_TILES = 4
C_HEADS, C_DK, C_CHUNK, C_CONV = 8, 128, 64, 4
QKV = 3072
C_IN_PAD = 4224
TM = 256
VMEM_LIMIT_BYTES = 56 * 1024 * 1024
ADAM_LR, ADAM_B1, ADAM_B2, ADAM_EPS, ADAM_WD, ADAM_STEP = 0.001, 0.9, 0.999, 1e-08, 0.01, 10
NEG = float(np.finfo(np.float32).min)


def _cp(*sem):
    return pltpu.CompilerParams(dimension_semantics=sem, vmem_limit_bytes=VMEM_LIMIT_BYTES)


def _bdot(a, b):
    return jnp.dot(a.astype(BF), b.astype(BF), preferred_element_type=F32)


def _bdot_nt(a, b):
    return lax.dot_general(a.astype(BF), b.astype(BF), (((1,), (1,)), ((), ())), preferred_element_type=F32)


def _bdot_tn(a, b):
    return lax.dot_general(a.astype(BF), b.astype(BF), (((0,), (0,)), ((), ())), preferred_element_type=F32)


def _hdot(a, b):
    return jnp.dot(a, b, precision=HI, preferred_element_type=F32)


def _bein(eq, a, b):
    return jnp.einsum(eq, a.astype(BF), b.astype(BF), preferred_element_type=F32)


def _row(tm, n):
    return pl.BlockSpec((tm, n), lambda i: (i, 0))


def _fix(shape):
    return pl.BlockSpec(shape, lambda i: (0,) * len(shape))


def _sds(*shape, dtype=F32):
    return jax.ShapeDtypeStruct(shape, dtype)


def _acc(ref, val):
    ref[...] += val


def _zero_at_first(refs, axis=0):
    @pl.when(pl.program_id(axis) == 0)
    def _():
        for r in refs:
            r[...] = jnp.zeros_like(r)


def _rms(x):
    return x * lax.rsqrt(jnp.mean(x * x, axis=-1, keepdims=True) + EPS)


def _pre_mod(x, g, scale, shift):
    return (_rms(x) * g) * (1.0 + scale) + shift


def _post_res(y, x, post_g, gate):
    return x + gate * (_rms(y) * post_g)


def _merge_gate(o1, o2, o3, l1, l2, l3, ga):
    m = jnp.maximum(jnp.maximum(l1, l2), l3)
    e1, e2, e3 = jnp.exp(l1 - m), jnp.exp(l2 - m), jnp.exp(l3 - m)
    inv = 1.0 / (e1 + e2 + e3)
    return ((e1 * inv) * o1 + (e2 * inv) * o2 + (e3 * inv) * o3) * jax.nn.silu(ga)


def _s5_gelu(ypre, u, d_skip):
    return jax.nn.gelu(ypre + d_skip * u)


def _s5_glu(yb, gl, gb):
    return yb * jax.nn.sigmoid(gl) * jax.nn.silu(gb)


def _l0_front(x, pre_g, scale, shift, w_in):
    s_len = x.shape[0]

    def body(x_ref, g_ref, sc_ref, sh_ref, w_ref, *out_refs):
        qkv_refs, (u_ref, ga_ref, gb_ref, h_ref) = out_refs[:9], out_refs[9:]
        hb = _pre_mod(x_ref[...], g_ref[...], sc_ref[...], sh_ref[...]).astype(BF)
        h_ref[...] = hb
        z = jnp.dot(hb, w_ref[...], preferred_element_type=F32)
        for a in range(3):
            piece = z[:, a * 512:(a + 1) * 512]
            for i, d in enumerate(DILATIONS):
                qkv_refs[3 * a + i][...] = _to_res(piece, d).astype(BF)
        u_ref[...] = z[:, 1536:2048]
        ga_ref[...] = z[:, 2048:2560]
        gb_ref[...] = z[:, 2560:3072]

    vec = _fix((1, D_MODEL))
    return pl.pallas_call(
        body, name="l0_front", grid=(s_len // TM,),
        in_specs=[_row(TM, D_MODEL), vec, vec, vec, _fix((D_MODEL, 3072))],
        out_specs=[_res_spec(d) for d in DILATIONS] * 3 + [_row(TM, 512)] * 3 + [_row(TM, D_MODEL)],
        out_shape=[_sds(*_res_shape(s_len, d), dtype=BF) for d in DILATIONS] * 3 + [_sds(s_len, 512)] * 3 + [_sds(s_len, D_MODEL, dtype=BF)],
        compiler_params=_cp("arbitrary"),
    )(x, pre_g, scale, shift, w_in)


def _front_bwd(name, x, pre_g, scale, shift, w_in, dres, parts, widths):
    s_len = x.shape[0]
    n_in = sum(len(p) for p in parts)
    n_cols = sum(widths)

    def body(*refs):
        x_ref, g_ref, sc_ref, sh_ref, w_ref, dres_ref = refs[:6]
        part_refs = refs[6:6 + n_in]
        dz_ref, dx_ref, dg_ref, dsc_ref, dsh_ref = refs[6 + n_in:]
        _zero_at_first([dg_ref, dsc_ref, dsh_ref])
        _, vjp = jax.vjp(_pre_mod, x_ref[...], g_ref[...], sc_ref[...], sh_ref[...])
        dh = jnp.zeros((TM, D_MODEL), F32)
        col, at = 0, 0
        for grp, width in zip(parts, widths):
            tile = lambda r: _from_res(r[...]) if len(r.shape) == 3 else r[...]
            dz = tile(part_refs[at])
            for r in part_refs[at + 1:at + len(grp)]:
                dz = dz + tile(r)
            at += len(grp)
            dzb = dz.astype(BF)
            dz_ref[:, col:col + width] = dzb
            dh = dh + lax.dot_general(dzb, w_ref[:, col:col + width], (((1,), (1,)), ((), ())), preferred_element_type=F32)
            col += width
        dx, dg, dsc, dsh = vjp(dh)
        dx_ref[...] = dx + dres_ref[...]
        _acc(dg_ref, dg)
        _acc(dsc_ref, dsc)
        _acc(dsh_ref, dsh)

    vec = _fix((1, D_MODEL))
    flat = [a for p in parts for a in p]
    return pl.pallas_call(
        body, name=name, grid=(s_len // TM,),
        in_specs=[_row(TM, D_MODEL), vec, vec, vec, _fix((D_MODEL, n_cols)), _row(TM, D_MODEL)]
        + [_res_spec(a.shape[0], a.shape[2]) if a.ndim == 3 else _row(TM, a.shape[1]) for a in flat],
        out_specs=[_row(TM, n_cols), _row(TM, D_MODEL), vec, vec, vec],
        out_shape=[_sds(s_len, n_cols, dtype=BF), _sds(s_len, D_MODEL), _sds(1, D_MODEL), _sds(1, D_MODEL), _sds(1, D_MODEL)],
        compiler_params=_cp("arbitrary"),
    )(x, pre_g, scale, shift, w_in, dres, *flat)


def _matmul_tn(name, a, b, tn):
    s_len, k_dim = a.shape
    n_dim = b.shape[1]
    ts = 512

    def body(a_ref, b_ref, o_ref):
        _zero_at_first([o_ref], axis=1)
        o_ref[...] += lax.dot_general(a_ref[...], b_ref[...], (((0,), (0,)), ((), ())), preferred_element_type=F32)

    return pl.pallas_call(
        body, name=name, grid=(n_dim // tn, s_len // ts),
        in_specs=[pl.BlockSpec((ts, k_dim), lambda j, i: (i, 0)), pl.BlockSpec((ts, tn), lambda j, i: (i, j))],
        out_specs=pl.BlockSpec((k_dim, tn), lambda j, i: (0, j)),
        out_shape=_sds(k_dim, n_dim),
        compiler_params=_cp("arbitrary", "arbitrary"),
    )(a, b)


def _t5_bucket_np(dist):
    dist = np.maximum(dist, 0)
    max_exact = REL_BUCKETS // 2
    large = max_exact + (np.log(np.maximum(dist, 1) / max_exact)
                         / math.log(REL_MAX_DIST / max_exact) * (REL_BUCKETS - max_exact)).astype(np.int32)
    large = np.minimum(large, REL_BUCKETS - 1)
    return np.where(dist < max_exact, dist, large).astype(np.int32)


def _to_res(z, dil):
    if dil == 1:
        return z[None]
    return jnp.swapaxes(z.reshape(z.shape[0] // dil, dil, z.shape[1]), 0, 1)


def _from_res(z):
    if z.shape[0] == 1:
        return z[0]
    return jnp.swapaxes(z, 0, 1).reshape(z.shape[0] * z.shape[1], z.shape[2])


def _res_shape(s_len, dil, width=A_WIDTH):
    return (dil, s_len // dil, width)


def _res_spec(dil, width=A_WIDTH):
    return pl.BlockSpec((dil, TM // dil, width), lambda i: (0, i, 0))


def _bucket_table():
    qi = np.arange(A_BLOCK)[:, None]
    kj = np.arange(2 * A_BLOCK)[None, :]
    return np.stack([_t5_bucket_np((qi + A_BLOCK - kj) * d) for d in DILATIONS], 0)


def _attn_mask(first):
    qi = lax.broadcasted_iota(jnp.int32, (A_BLOCK, 2 * A_BLOCK), 0)
    kj = lax.broadcasted_iota(jnp.int32, (A_BLOCK, 2 * A_BLOCK), 1)
    rel = qi + A_BLOCK - kj
    return (rel >= 0) & (rel <= N_KEYS) & (jnp.logical_not(first) | (kj >= A_BLOCK))


def _attn_specs(nb, rev):
    per = 2 if nb % 2 == 0 else 1
    steps = nb // per
    n_of = (lambda i: steps - 1 - i) if rev else (lambda i: i)
    cur = pl.BlockSpec((None, per * A_BLOCK, A_WIDTH), lambda r, i: (r, n_of(i), 0))
    prev = pl.BlockSpec((None, A_BLOCK, A_WIDTH), lambda r, i: (r, jnp.maximum(per * n_of(i) - 1, 0), 0))
    bias = pl.BlockSpec((A_HEADS, A_BLOCK, 2 * A_BLOCK), lambda r, i: (0, 0, 0))
    return per, steps, cur, prev, bias


def _attn_fwd(q, k, v, bias, exchange=None):
    dil, t_len, _ = q.shape
    per, steps, cur, prev, bias_spec = _attn_specs(t_len // A_BLOCK, False)
    scale = A_HD ** -0.5

    def body(q_ref, kp_ref, kc_ref, vp_ref, vc_ref, b_ref, o_ref, l_ref):
        lane = lax.broadcasted_iota(jnp.int32, (1, 128), 1)
        for sub in range(per):
            rows = slice(sub * A_BLOCK, (sub + 1) * A_BLOCK)
            before = slice((sub - 1) * A_BLOCK, sub * A_BLOCK)
            mask = _attn_mask((pl.program_id(1) == 0) if sub == 0 else False)
            for hp in range(A_HEADS // 2):
                sl = slice(hp * 128, (hp + 1) * 128)
                qp = q_ref[rows, sl]
                kw = jnp.concatenate([kp_ref[:, sl] if sub == 0 else kc_ref[before, sl], kc_ref[rows, sl]], axis=0).astype(BF)
                vw = jnp.concatenate([vp_ref[:, sl] if sub == 0 else vc_ref[before, sl], vc_ref[rows, sl]], axis=0).astype(BF)
                outs, lses = [], []
                for j in range(2):
                    hm = (lane < 64) if j == 0 else (lane >= 64)
                    s = _bdot_nt(jnp.where(hm, qp, 0.0), kw) * scale
                    s = jnp.where(mask, s + b_ref[2 * hp + j], NEG)
                    m = jnp.max(s, axis=-1, keepdims=True)
                    p = jnp.exp(s - m)
                    den = jnp.sum(p, axis=-1, keepdims=True)
                    outs.append(_bdot(p, vw) / den)
                    lses.append(m + jnp.log(den))
                hm0 = lane < 64
                o_ref[rows, sl] = jnp.where(hm0, outs[0], outs[1])
                l_ref[rows, sl] = jnp.where(hm0, lses[0], lses[1])

    return _call_with_exchange(body, f"attn_fwd_d{dil}", (dil, steps), [cur, prev, cur, prev, cur, bias_spec], [cur, cur],
                               [_sds(dil, t_len, A_WIDTH)] * 2, [], (q, k, k, v, v, bias), exchange)


def _attn_bwd(q, k, v, bias, o, l, do, dl, exchange=None):
    dil, t_len, _ = q.shape
    per, steps, cur, prev, bias_spec = _attn_specs(t_len // A_BLOCK, True)
    scale = A_HD ** -0.5

    def body(q_ref, kp_ref, kc_ref, vp_ref, vc_ref, b_ref, o_ref, l_ref, do_ref, dl_ref,
             dq_ref, dk_ref, dv_ref, db_ref, ck_ref, cv_ref):
        _zero_at_first([ck_ref, cv_ref], axis=1)

        @pl.when((pl.program_id(0) == 0) & (pl.program_id(1) == 0))
        def _():
            db_ref[...] = jnp.zeros_like(db_ref)

        lane = lax.broadcasted_iota(jnp.int32, (1, 128), 1)
        for hp in range(A_HEADS // 2):
            sl = slice(hp * 128, (hp + 1) * 128)
            to_prev_k, to_prev_v = ck_ref[:, sl], cv_ref[:, sl]
            for sub in range(per - 1, -1, -1):
                rows = slice(sub * A_BLOCK, (sub + 1) * A_BLOCK)
                before = slice((sub - 1) * A_BLOCK, sub * A_BLOCK)
                mask = _attn_mask((pl.program_id(1) == steps - 1) if sub == 0 else False)
                qp = q_ref[rows, sl]
                kw = jnp.concatenate([kp_ref[:, sl] if sub == 0 else kc_ref[before, sl], kc_ref[rows, sl]], axis=0).astype(BF)
                vw = jnp.concatenate([vp_ref[:, sl] if sub == 0 else vc_ref[before, sl], vc_ref[rows, sl]], axis=0).astype(BF)
                op, lp, dop, dlp = o_ref[rows, sl], l_ref[rows, sl], do_ref[rows, sl], dl_ref[rows, sl]
                dq_acc = jnp.zeros((A_BLOCK, 128), F32)
                dk_acc = jnp.zeros((2 * A_BLOCK, 128), F32)
                dv_acc = jnp.zeros((2 * A_BLOCK, 128), F32)
                for j in range(2):
                    hm = (lane < 64) if j == 0 else (lane >= 64)
                    qm = jnp.where(hm, qp, 0.0)
                    s = _bdot_nt(qm, kw) * scale
                    s = jnp.where(mask, s + b_ref[2 * hp + j], NEG)
                    lse = jnp.max(jnp.where(hm, lp, NEG), axis=-1, keepdims=True)
                    p = jnp.exp(s - lse)
                    do_h = jnp.where(hm, dop, 0.0)
                    dd = jnp.sum(do_h * op, axis=-1, keepdims=True)
                    dlse = jnp.sum(jnp.where(hm, dlp, 0.0), axis=-1, keepdims=True)
                    ds = p * (_bdot_nt(do_h, vw) - dd + dlse)
                    dv_acc = dv_acc + _bdot_tn(p, do_h)
                    dq_acc = dq_acc + jnp.where(hm, _bdot(ds, kw), 0.0) * scale
                    dk_acc = dk_acc + _bdot_tn(ds, qm) * scale
                    db_ref[2 * hp + j] += ds
                dq_ref[rows, sl] = dq_acc
                dk_ref[rows, sl] = dk_acc[A_BLOCK:] + to_prev_k
                dv_ref[rows, sl] = dv_acc[A_BLOCK:] + to_prev_v
                to_prev_k, to_prev_v = dk_acc[:A_BLOCK], dv_acc[:A_BLOCK]
            ck_ref[:, sl] = to_prev_k
            cv_ref[:, sl] = to_prev_v

    return _call_with_exchange(
        body, f"attn_bwd_d{dil}", (dil, steps), [cur, prev, cur, prev, cur, bias_spec, cur, cur, cur, cur],
        [cur, cur, cur, bias_spec], [_sds(dil, t_len, A_WIDTH)] * 3 + [_sds(A_HEADS, A_BLOCK, 2 * A_BLOCK)],
        [pltpu.VMEM((A_BLOCK, A_WIDTH), F32)] * 2, (q, k, k, v, v, bias, o, l, do, dl), exchange)


def _attn_bias(rel_bias, table):
    def body(rb_ref, t_ref, *o_refs):
        for c in range(3):
            t = t_ref[c]
            acc = [jnp.zeros((A_BLOCK, 2 * A_BLOCK), F32) for _ in range(A_HEADS)]
            for b in range(REL_BUCKETS):
                hit = t == b
                acc = [jnp.where(hit, rb_ref[b, h], acc[h]) for h in range(A_HEADS)]
            for h in range(A_HEADS):
                o_refs[c][h] = acc[h]

    return pl.pallas_call(body, name="attn_bias", out_shape=[_sds(A_HEADS, A_BLOCK, 2 * A_BLOCK)] * 3,
                          in_specs=[pl.BlockSpec(memory_space=pltpu.SMEM), pl.BlockSpec(memory_space=pltpu.VMEM)],
                          compiler_params=pltpu.CompilerParams(vmem_limit_bytes=VMEM_LIMIT_BYTES))(rel_bias, table)


def _rel_bias_grad(dbs, idx_rows):
    n = A_BLOCK * 2 * A_BLOCK

    def body(d0_ref, d1_ref, d2_ref, idx_ref, o_ref):
        bucket = lax.broadcasted_iota(jnp.int32, (REL_BUCKETS, n), 0).astype(F32)
        acc = jnp.zeros((A_HEADS, REL_BUCKETS), F32)
        for c, db_ref in enumerate((d0_ref, d1_ref, d2_ref)):
            onehot = (idx_ref[c:c + 1, :] == bucket).astype(F32)
            acc = acc + lax.dot_general(db_ref[...], onehot, (((1,), (1,)), ((), ())), precision=HI, preferred_element_type=F32)
        o_ref[...] = acc

    return pl.pallas_call(body, name="rel_bias_grad", out_shape=_sds(A_HEADS, REL_BUCKETS),
                          compiler_params=pltpu.CompilerParams(vmem_limit_bytes=VMEM_LIMIT_BYTES))(
                              *[d.reshape(A_HEADS, n) for d in dbs], idx_rows)


def _s5_param_fn(a_re, a_im, log_dt, bt_re, bt_im):
    dt = jnp.exp(log_dt)
    mag = jnp.exp(dt * a_re)
    abar_r, abar_i = mag * jnp.cos(dt * a_im), mag * jnp.sin(dt * a_im)
    den = a_re * a_re + a_im * a_im
    fr = ((abar_r - 1.0) * a_re + abar_i * a_im) / den
    fi = (abar_i * a_re - (abar_r - 1.0) * a_im) / den
    row = lax.broadcasted_iota(jnp.int32, (B_WIDTH, B_GROUPS), 0)
    grp = lax.broadcasted_iota(jnp.int32, (B_WIDTH, B_GROUPS), 1)
    expand = ((row // B_GROUP) == grp).astype(F32)
    fr_e, fi_e = _hdot(expand, fr), _hdot(expand, fi)
    return abar_r, abar_i, fr_e * bt_re - fi_e * bt_im, fr_e * bt_im + fi_e * bt_re


def _s5_params(a_re, a_im, log_dt, bt_re, bt_im):
    def body(ar, ai, ld, br, bi, o1, o2, o3, o4):
        o1[...], o2[...], o3[...], o4[...] = _s5_param_fn(ar[...], ai[...], ld[...], br[...], bi[...])

    return pl.pallas_call(body, name="s5_params",
                          out_shape=[_sds(B_GROUPS, B_STATE)] * 2 + [_sds(B_WIDTH, B_STATE)] * 2)(a_re, a_im, log_dt, bt_re, bt_im)


def _s5_params_bwd(a_re, a_im, log_dt, bt_re, bt_im, d1, d2, d3, d4):
    def body(ar, ai, ld, br, bi, c1, c2, c3, c4, o1, o2, o3, o4, o5):
        _, vjp = jax.vjp(_s5_param_fn, ar[...], ai[...], ld[...], br[...], bi[...])
        o1[...], o2[...], o3[...], o4[...], o5[...] = vjp((c1[...], c2[...], c3[...], c4[...]))

    return pl.pallas_call(body, name="s5_params_bwd",
                          out_shape=[_sds(B_GROUPS, B_STATE)] * 2 + [_sds(B_GROUPS, 1)] + [_sds(B_WIDTH, B_STATE)] * 2,
                          )(a_re, a_im, log_dt, bt_re, bt_im, d1, d2, d3, d4)


def _pick_row(x, r):
    rows = lax.broadcasted_iota(jnp.int32, x.shape, 0)
    return jnp.sum(jnp.where(rows == r, x, 0.0), axis=0, keepdims=True)


S5_SEG = 8
S5_STEPS = 32
S5_WIDTH = S5_TILES * S5_LANES


def _seg_rows(block):
    return jnp.swapaxes(block, 0, 1).reshape(block.shape[1] * S5_SEG, block.shape[2])


def _seg_block(rows):
    return jnp.swapaxes(rows.reshape(rows.shape[0] // S5_SEG, S5_SEG, rows.shape[1]), 0, 1)


def _seq_specs(n_i, rev):
    at = (lambda i: n_i - 1 - i) if rev else (lambda i: i)
    seg = pl.BlockSpec((S5_SEG, S5_STEPS, B_WIDTH), lambda i: (0, at(i), 0))
    x_spec = pl.BlockSpec((S5_SEG * S5_STEPS, S5_WIDTH), lambda i: (at(i), 0))
    return seg, x_spec, _fix((S5_TILES, 128, S5_LANES)), _fix((S5_TILES, S5_LANES, 128)), _fix((1, S5_WIDTH)), _fix((S5_SEG, S5_WIDTH))


def _tile_dots(dot, lhs, w_ref, lhs_width):
    return jnp.concatenate([dot(lhs[:, t * lhs_width:(t + 1) * lhs_width], w_ref[t]) for t in range(S5_TILES)], axis=1)


def _s5_entries(name, end_r, end_i, abr, abi, steps, reverse):
    def body(er_ref, ei_ref, ar_ref, ai_ref, or_ref, oi_ref):
        pr, pi_ = ar_ref[...], ai_ref[...]
        for _ in range(int(math.log2(steps))):
            pr, pi_ = pr * pr - pi_ * pi_, 2.0 * pr * pi_
        er, ei = er_ref[...], ei_ref[...]
        rows = lax.broadcasted_iota(jnp.int32, er.shape, 0)
        cr, ci = jnp.zeros_like(pr), jnp.zeros_like(pr)
        out_r, out_i = jnp.zeros_like(er), jnp.zeros_like(er)
        for g in (range(S5_SEG - 2, -1, -1) if reverse else range(1, S5_SEG)):
            src = g + 1 if reverse else g - 1
            cr, ci = _pick_row(er, src) + pr * cr - pi_ * ci, _pick_row(ei, src) + pr * ci + pi_ * cr
            out_r, out_i = jnp.where(rows == g, cr, out_r), jnp.where(rows == g, ci, out_i)
        or_ref[...] = out_r
        oi_ref[...] = out_i

    return pl.pallas_call(body, name=name, out_shape=[_sds(*end_r.shape)] * 2)(end_r, end_i, abr, abi)


def _s5_seq_fwd(u, btr, bti, ctr, cti, abr, abi, entry, store, exchange=None):
    s_len = u.shape[0]
    seg_len = s_len // S5_SEG
    n_i = seg_len // S5_STEPS
    rows = S5_SEG * S5_STEPS

    def body(u_ref, btr_ref, bti_ref, ctr_ref, cti_ref, ar_ref, ai_ref, er_ref, ei_ref, *rest):
        if store:
            xr_ref, xi_ref, y_ref, endr_ref, endi_ref, sr_ref, si_ref = rest
        else:
            endr_ref, endi_ref, sr_ref, si_ref = rest
        i = pl.program_id(0)

        @pl.when(i == 0)
        def _():
            sr_ref[...] = er_ref[...]
            si_ref[...] = ei_ref[...]

        ar = jnp.broadcast_to(ar_ref[...], (S5_SEG, S5_WIDTH))
        ai = jnp.broadcast_to(ai_ref[...], (S5_SEG, S5_WIDTH))
        ub = _seg_rows(u_ref[...])
        br, bi = _tile_dots(_bdot, ub, btr_ref, 128), _tile_dots(_bdot, ub, bti_ref, 128)
        sr, si = sr_ref[...], si_ref[...]
        for s in range(S5_STEPS):
            at = slice(S5_SEG * s, S5_SEG * (s + 1))
            sr, si = ar * sr - ai * si + br[at], ar * si + ai * sr + bi[at]
            if store:
                xr_ref[at, :] = sr
                xi_ref[at, :] = si
        sr_ref[...] = sr
        si_ref[...] = si
        if store:
            y_ref[...] = _seg_block(_tile_dots(_bdot, xr_ref[...], ctr_ref, S5_LANES) - _tile_dots(_bdot, xi_ref[...], cti_ref, S5_LANES))

        @pl.when(i == n_i - 1)
        def _():
            endr_ref[...] = sr
            endi_ref[...] = si

    seg, x_spec, b_spec, c_spec, a_spec, e_spec = _seq_specs(n_i, False)
    ends = [_sds(S5_SEG, S5_WIDTH)] * 2
    full = [_sds(s_len, S5_WIDTH)] * 2 + [_sds(S5_SEG, seg_len, B_WIDTH)] if store else []
    return _call_with_exchange(
        body, "s5_scan_fwd" if store else "s5_ends_fwd", (n_i,),
        [seg, b_spec, b_spec, c_spec, c_spec, a_spec, a_spec, e_spec, e_spec],
        ([x_spec, x_spec, seg] if store else []) + [e_spec, e_spec], full + ends,
        [pltpu.VMEM((S5_SEG, S5_WIDTH), F32)] * 2,
        (u.reshape(S5_SEG, seg_len, B_WIDTH), btr, bti, ctr, cti, abr, abi, *entry), exchange)


def _s5_seq_bwd(dy, xr, xi, u, btr, bti, ctr, cti, abr, abi, g_entry, x_entry, full, exchange=None):
    s_len = dy.shape[0]
    seg_len = s_len // S5_SEG
    n_i = seg_len // S5_STEPS
    rows = S5_SEG * S5_STEPS

    def body(*refs):
        if full:
            (dy_ref, btr_ref, bti_ref, ctr_ref, cti_ref, ar_ref, ai_ref, ger_ref, gei_ref,
             xr_ref, xi_ref, xrp_ref, xip_ref, xer_ref, xei_ref, u_ref,
             du_ref, dbtr_ref, dbti_ref, dctr_ref, dcti_ref, dar_ref, dai_ref, str_ref, sti_ref,
             sr_ref, si_ref, gr_s, gi_s) = refs
        else:
            (dy_ref, btr_ref, bti_ref, ctr_ref, cti_ref, ar_ref, ai_ref, ger_ref, gei_ref, str_ref, sti_ref, sr_ref, si_ref) = refs
        i = pl.program_id(0)

        @pl.when(i == 0)
        def _():
            sr_ref[...] = ger_ref[...]
            si_ref[...] = gei_ref[...]
            if full:
                for r in (dbtr_ref, dbti_ref, dctr_ref, dcti_ref, dar_ref, dai_ref):
                    r[...] = jnp.zeros_like(r)

        ar = jnp.broadcast_to(ar_ref[...], (S5_SEG, S5_WIDTH))
        ai = -jnp.broadcast_to(ai_ref[...], (S5_SEG, S5_WIDTH))
        dyb = _seg_rows(dy_ref[...])
        gr, gi = _tile_dots(_bdot_nt, dyb, ctr_ref, 128), -_tile_dots(_bdot_nt, dyb, cti_ref, 128)
        sr, si = sr_ref[...], si_ref[...]
        for s in range(S5_STEPS - 1, -1, -1):
            at = slice(S5_SEG * s, S5_SEG * (s + 1))
            sr, si = ar * sr - ai * si + gr[at], ar * si + ai * sr + gi[at]
            if full:
                gr_s[at, :] = sr
                gi_s[at, :] = si
        sr_ref[...] = sr
        si_ref[...] = si

        @pl.when(i == n_i - 1)
        def _():
            str_ref[...] = sr
            sti_ref[...] = si

        if full:
            g_r, g_i = gr_s[...], gi_s[...]
            du_ref[...] = _seg_block(_tile_dots(_bdot_nt, g_r, btr_ref, S5_LANES) + _tile_dots(_bdot_nt, g_i, bti_ref, S5_LANES))
            ub = _seg_rows(u_ref[...])
            xr_b, xi_b = xr_ref[...], xi_ref[...]
            for t in range(S5_TILES):
                lanes, cols = slice(t * S5_LANES, (t + 1) * S5_LANES), slice(t * 128, (t + 1) * 128)
                dbtr_ref[t] += _bdot_tn(ub[:, cols], g_r[:, lanes])
                dbti_ref[t] += _bdot_tn(ub[:, cols], g_i[:, lanes])
                dctr_ref[t] += _bdot_tn(xr_b[:, lanes], dyb[:, cols])
                dcti_ref[t] -= _bdot_tn(xi_b[:, lanes], dyb[:, cols])
            first = i == n_i - 1
            xpr = jnp.concatenate([jnp.where(first, xer_ref[...], xrp_ref[...]), xr_b[:rows - S5_SEG]], axis=0)
            xpi = jnp.concatenate([jnp.where(first, xei_ref[...], xip_ref[...]), xi_b[:rows - S5_SEG]], axis=0)
            dar_ref[...] += jnp.sum(g_r * xpr + g_i * xpi, axis=0, keepdims=True)
            dai_ref[...] += jnp.sum(g_i * xpr - g_r * xpi, axis=0, keepdims=True)

    seg, x_spec, b_spec, c_spec, a_spec, e_spec = _seq_specs(n_i, True)
    halo = pl.BlockSpec((S5_SEG, S5_WIDTH), lambda i: (jnp.maximum((n_i - 1 - i) * S5_STEPS - 1, 0), 0))
    starts = [_sds(S5_SEG, S5_WIDTH)] * 2
    in_specs = [seg, b_spec, b_spec, c_spec, c_spec, a_spec, a_spec, e_spec, e_spec]
    args = [dy.reshape(S5_SEG, seg_len, B_WIDTH), btr, bti, ctr, cti, abr, abi, *g_entry]
    state = [pltpu.VMEM((S5_SEG, S5_WIDTH), F32)] * 2
    if not full:
        return _call_with_exchange(body, "s5_starts_bwd", (n_i,), in_specs, [e_spec, e_spec], starts, state, args, None)
    return _call_with_exchange(
        body, "s5_scan_bwd", (n_i,),
        in_specs + [x_spec, x_spec, halo, halo, e_spec, e_spec, seg],
        [seg, b_spec, b_spec, c_spec, c_spec, a_spec, a_spec, e_spec, e_spec],
        [_sds(S5_SEG, seg_len, B_WIDTH)] + [_sds(S5_TILES, 128, S5_LANES)] * 2 + [_sds(S5_TILES, S5_LANES, 128)] * 2
        + [_sds(1, S5_WIDTH)] * 2 + starts,
        state + [pltpu.VMEM((rows, S5_WIDTH), F32)] * 2,
        args + [xr, xi, xr, xi, *x_entry, u.reshape(S5_SEG, seg_len, B_WIDTH)], exchange)


def _blockdiag_b(bbar_t):
    blocks = bbar_t.reshape(S5_TILES, 8, B_GROUP, B_STATE)
    return jnp.einsum('jgmp,gh->jgmhp', blocks, jnp.eye(8, dtype=F32)).reshape(S5_TILES, 128, S5_LANES)


def _blockdiag_b_t(d):
    return jnp.einsum('jgmgp->jgmp', d.reshape(S5_TILES, 8, B_GROUP, 8, B_STATE)).reshape(B_WIDTH, B_STATE)


def _blockdiag_c(c):
    blocks = c.reshape(S5_TILES, 8, B_GROUP, B_STATE)
    return jnp.einsum('jgmp,gh->jhpgm', blocks, jnp.eye(8, dtype=F32)).reshape(S5_TILES, S5_LANES, 128)


def _blockdiag_c_t(d):
    return jnp.einsum('jgpgm->jgmp', d.reshape(S5_TILES, 8, B_STATE, 8, B_GROUP)).reshape(B_GROUPS, B_GROUP, B_STATE)


def _l0_out(os, ls, ga, gb, ypre, u, x, d_skip, glu_w, glu_b, w_out, post_g, gate, exchange=None):
    s_len = x.shape[0]

    def body(o0, o1, o2, l0, l1, l2, ga_ref, gb_ref, yp_ref, u_ref, x_ref, d_ref, gw_ref, gbias_ref, w_ref, pg_ref, gt_ref, x1_ref, y_ref):
        oa = _merge_gate(*[_from_res(r[...]) for r in (o0, o1, o2, l0, l1, l2)], ga_ref[...])
        yb = _s5_gelu(yp_ref[...], u_ref[...], d_ref[...])
        ob = _s5_glu(yb, _bdot(yb, gw_ref[...]) + gbias_ref[...], gb_ref[...])
        y = _bdot(oa, w_ref[0:512, :]) + _bdot(ob, w_ref[512:1024, :])
        y_ref[...] = y
        x1_ref[...] = _post_res(y, x_ref[...], pg_ref[...], gt_ref[...])

    vec, half = _fix((1, D_MODEL)), _fix((1, 512))
    return _call_with_exchange(
        body, "l0_out", (s_len // TM,),
        [_res_spec(d) for d in DILATIONS] * 2 + [_row(TM, 512)] * 4
        + [_row(TM, D_MODEL), half, _fix((512, 512)), half, _fix((D_MODEL, D_MODEL)), vec, vec],
        [_row(TM, D_MODEL)] * 2, [_sds(s_len, D_MODEL)] * 2, [],
        (*os, *ls, ga, gb, ypre, u, x, d_skip, glu_w, glu_b, w_out, post_g, gate), exchange)


def _l0_out_bwd(os, ls, ga, gb, ypre, u, x, y, d_skip, glu_w, glu_b, w_out, post_g, gate, dx1, exchange=None):
    s_len = x.shape[0]

    def body(o0, o1, o2, l0, l1, l2, ga_ref, gb_ref, yp_ref, u_ref, x_ref, y_ref, d_ref, gw_ref, gbias_ref, w_ref, pg_ref, gt_ref, dx1_ref,
             do0, do1, do2, dl0, dl1, dl2, dga_ref, dgb_ref, dyp_ref, du_ref, dd_ref, dgw_ref, dgbias_ref, dw_ref, dpg_ref, dgt_ref):
        _zero_at_first([dd_ref, dgw_ref, dgbias_ref, dw_ref, dpg_ref, dgt_ref])
        _, vjp2 = jax.vjp(_post_res, y_ref[...], x_ref[...], pg_ref[...], gt_ref[...])
        dy, _, dpg, dgt = vjp2(dx1_ref[...])
        _acc(dpg_ref, dpg)
        _acc(dgt_ref, dgt)
        oa, vjp_a = jax.vjp(_merge_gate, *[_from_res(r[...]) for r in (o0, o1, o2, l0, l1, l2)], ga_ref[...])
        yb, vjp_g = jax.vjp(_s5_gelu, yp_ref[...], u_ref[...], d_ref[...])
        gl = _bdot(yb, gw_ref[...]) + gbias_ref[...]
        ob, vjp_b = jax.vjp(_s5_glu, yb, gl, gb_ref[...])
        dw_ref[0:512, :] += _bdot_tn(oa, dy)
        dw_ref[512:1024, :] += _bdot_tn(ob, dy)
        d1, d2, d3, e1, e2, e3, dga = vjp_a(_bdot_nt(dy, w_ref[0:512, :]))
        for ref, val, d in zip((do0, do1, do2, dl0, dl1, dl2), (d1, d2, d3, e1, e2, e3), DILATIONS * 2):
            ref[...] = _to_res(val, d)
        dga_ref[...] = dga
        dyb, dgl, dgb = vjp_b(_bdot_nt(dy, w_ref[512:1024, :]))
        dgb_ref[...] = dgb
        dgw_ref[...] += _bdot_tn(yb, dgl)
        _acc(dgbias_ref, jnp.sum(dgl, axis=0, keepdims=True))
        dyp, du, dd = vjp_g(dyb + _bdot_nt(dgl, gw_ref[...]))
        dyp_ref[...] = dyp
        du_ref[...] = du
        _acc(dd_ref, dd)

    vec, half = _fix((1, D_MODEL)), _fix((1, 512))
    r5, r10 = _row(TM, 512), _row(TM, D_MODEL)
    res6 = [_res_spec(d) for d in DILATIONS] * 2
    return _call_with_exchange(
        body, "l0_out_bwd", (s_len // TM,),
        res6 + [r5] * 4 + [r10, r10, half, _fix((512, 512)), half, _fix((D_MODEL, D_MODEL)), vec, vec, r10],
        res6 + [r5] * 4 + [half, _fix((512, 512)), half, _fix((D_MODEL, D_MODEL)), vec, vec],
        [_sds(*_res_shape(s_len, d)) for d in DILATIONS] * 2 + [_sds(s_len, 512)] * 4
        + [_sds(1, 512), _sds(512, 512), _sds(1, 512), _sds(D_MODEL, D_MODEL), _sds(1, D_MODEL), _sds(1, D_MODEL)],
        [], (*os, *ls, ga, gb, ypre, u, x, y, d_skip, glu_w, glu_b, w_out, post_g, gate, dx1), exchange)


def _l1_front(x, pre_g, scale, shift, w_in):
    s_len = x.shape[0]

    def body(x_ref, g_ref, sc_ref, sh_ref, w_ref, raw_ref, gate_ref, ba_ref, h_ref):
        hb = _pre_mod(x_ref[...], g_ref[...], sc_ref[...], sh_ref[...]).astype(BF)
        h_ref[...] = hb
        z = jnp.dot(hb, w_ref[...], preferred_element_type=F32)
        raw_ref[...] = z[:, 0:QKV]
        gate_ref[...] = z[:, QKV:QKV + 1024]
        ba_ref[...] = z[:, QKV + 1024:C_IN_PAD]

    vec = _fix((1, D_MODEL))
    return pl.pallas_call(
        body, name="l1_front", grid=(s_len // TM,),
        in_specs=[_row(TM, D_MODEL), vec, vec, vec, _fix((D_MODEL, C_IN_PAD))],
        out_specs=[_row(TM, QKV), _row(TM, 1024), _row(TM, 128), _row(TM, D_MODEL)],
        out_shape=[_sds(s_len, QKV), _sds(s_len, 1024), _sds(s_len, 128), _sds(s_len, D_MODEL, dtype=BF)],
        compiler_params=_cp("arbitrary"),
    )(x, pre_g, scale, shift, w_in)


def _bg_fn(ba, alog_row, dtb_row):
    lane = lax.broadcasted_iota(jnp.int32, (1, 128), 1)
    g = -jnp.exp(alog_row) * jax.nn.softplus(ba + dtb_row)
    return jnp.where(lane < C_HEADS, jax.nn.sigmoid(ba), jnp.where(lane < 2 * C_HEADS, g, 0.0))


def _act_q(c):
    q = jax.nn.silu(c)
    return q * lax.rsqrt(jnp.sum(q * q, axis=-1, keepdims=True) + EPS) * (C_DK ** -0.5)


def _act_k(c):
    k = jax.nn.silu(c)
    return k * lax.rsqrt(jnp.sum(k * k, axis=-1, keepdims=True) + EPS)


def _act_of(s):
    return _act_q if s < 8 else (_act_k if s < 16 else jax.nn.silu)


def _conv_taps(prev8, tile_ref, sl, next8=None):
    rows = tile_ref.shape[0]
    head = jnp.concatenate([prev8, tile_ref[0:8, sl]], axis=0)
    tail = None if next8 is None else jnp.concatenate([tile_ref[rows - 8:rows, sl], next8], axis=0)
    taps = []
    for j in range(C_CONV):
        shift = C_CONV - 1 - j
        pieces = [head[8:] if shift == 0 else pltpu.roll(head, shift, 0)[8:], tile_ref[pl.ds(8 - shift, rows - 8), sl]]
        if tail is not None:
            pieces.append(tail[8:] if shift == 0 else pltpu.roll(tail, shift, 0)[8:])
        taps.append(jnp.concatenate(pieces, axis=0))
    return taps


def _gdn_prep(raw, ba, conv_w, alog_row, dtb_row):
    s_len = raw.shape[0]

    def body(raw_ref, halo_ref, ba_ref, w_ref, al_ref, dt_ref, qkv_ref, bg_ref):
        bg_ref[...] = _bg_fn(ba_ref[...], al_ref[...], dt_ref[...])
        has_prev = (pl.program_id(0) > 0).astype(F32)
        for s in range(24):
            sl = slice(s * 128, (s + 1) * 128)
            taps = _conv_taps(halo_ref[:, sl] * has_prev, raw_ref, sl)
            conv = w_ref[3:4, sl] * taps[3]
            for j in range(3):
                conv = conv + w_ref[j:j + 1, sl] * taps[j]
            qkv_ref[:, sl] = _act_of(s)(conv)

    halo = pl.BlockSpec((8, QKV), lambda i: (jnp.maximum(i * (TM // 8) - 1, 0), 0))
    row128 = _fix((1, 128))
    return pl.pallas_call(
        body, name="gdn_prep", grid=(s_len // TM,),
        in_specs=[_row(TM, QKV), halo, _row(TM, 128), _fix((C_CONV, QKV)), row128, row128],
        out_specs=[_row(TM, QKV), _row(TM, 128)],
        out_shape=[_sds(s_len, QKV), _sds(s_len, 128)],
        compiler_params=_cp("arbitrary"),
    )(raw, raw, ba, conv_w, alog_row, dtb_row)


def _gdn_prep_bwd(raw, ba, conv_w, alog_row, dtb_row, dq, dk, dv, dbg):
    s_len = raw.shape[0]
    n_tiles = s_len // TM

    def body(raw_ref, prev_ref, next_ref, ba_ref, w_ref, al_ref, dt_ref, dq_ref, dqn_ref, dk_ref, dkn_ref, dv_ref, dvn_ref, dbg_ref,
             draw_ref, dba_ref, dw_ref, dal_ref, ddt_ref, dconv_ref):
        _zero_at_first([dw_ref, dal_ref, ddt_ref])
        i = pl.program_id(0)
        _, vjp_bg = jax.vjp(_bg_fn, ba_ref[...], al_ref[...], dt_ref[...])
        dba, dal, ddt = vjp_bg(dbg_ref[...])
        dba_ref[...] = dba
        _acc(dal_ref, dal)
        _acc(ddt_ref, ddt)
        has_prev = (i > 0).astype(F32)
        has_next = (i < n_tiles - 1).astype(F32)
        ct_refs = ((dq_ref, dqn_ref), (dk_ref, dkn_ref), (dv_ref, dvn_ref))
        for s in range(24):
            sl = slice(s * 128, (s + 1) * 128)
            hl = slice((s % 8) * 128, (s % 8 + 1) * 128)
            tile_ref, nxt_ref = ct_refs[s // 8]
            taps = _conv_taps(prev_ref[:, sl] * has_prev, raw_ref, sl, next_ref[:, sl] * has_next)
            conv = w_ref[3:4, sl] * taps[3]
            for j in range(3):
                conv = conv + w_ref[j:j + 1, sl] * taps[j]
            ct = jnp.concatenate([tile_ref[:, hl], nxt_ref[:, hl] * has_next], axis=0)
            _, vjp_act = jax.vjp(_act_of(s), conv)
            dconv, = vjp_act(ct)
            dconv_ref[...] = dconv
            draw = w_ref[3:4, sl] * dconv[:TM]
            for j in range(3):
                draw = draw + w_ref[j:j + 1, sl] * dconv_ref[pl.ds(3 - j, TM), :]
            draw_ref[:, sl] = draw
            for j in range(4):
                dw_ref[j:j + 1, sl] += jnp.sum(dconv[:TM] * taps[j][:TM], axis=0, keepdims=True)

    prev = pl.BlockSpec((8, QKV), lambda i: (jnp.maximum(i * (TM // 8) - 1, 0), 0))
    nxt = lambda n: pl.BlockSpec((8, n), lambda i: (jnp.minimum((i + 1) * (TM // 8), s_len // 8 - 1), 0))
    row128 = _fix((1, 128))
    ct_specs = [_row(TM, 1024), nxt(1024)] * 3
    return pl.pallas_call(
        body, name="gdn_prep_bwd", grid=(n_tiles,),
        in_specs=[_row(TM, QKV), prev, nxt(QKV), _row(TM, 128), _fix((C_CONV, QKV)), row128, row128] + ct_specs + [_row(TM, 128)],
        out_specs=[_row(TM, QKV), _row(TM, 128), _fix((C_CONV, QKV)), row128, row128],
        out_shape=[_sds(s_len, QKV), _sds(s_len, 128), _sds(C_CONV, QKV), _sds(1, 128), _sds(1, 128)],
        scratch_shapes=[pltpu.VMEM((TM + 8, 128), F32)],
        compiler_params=_cp("arbitrary"),
    )(raw, raw, raw, ba, conv_w, alog_row, dtb_row, dq, dq, dk, dk, dv, dv, dbg)


def _tein(eq, a, b):
    return jnp.einsum(eq, a, b, precision=lax.Precision.HIGH, preferred_element_type=F32)


def _unit_lower_inverse(lower):
    ri = lax.broadcasted_iota(jnp.int32, (C_CHUNK, C_CHUNK), 0)
    ci = lax.broadcasted_iota(jnp.int32, (C_CHUNK, C_CHUNK), 1)
    eye = (ri == ci).astype(F32)[None]
    p_mat = -lower
    inv = eye + p_mat
    for _ in range(5):
        p_mat = _bein('hij,hjk->hik', p_mat, p_mat)
        inv = inv + _bein('hij,hjk->hik', inv, p_mat)
    inv = _tein('hij,hjk->hik', inv, 2.0 * eye - _tein('hij,hjk->hik', eye + lower, inv))
    return jnp.where((ri >= ci)[None], inv, 0.0)


@jax.custom_vjp
def _known_inverse(lower, inv):
    return inv


def _known_inverse_fwd(lower, inv):
    return inv, inv


def _known_inverse_bwd(inv, d_inv):
    d_lower = -_bein('hik,hjk->hij', _bein('hji,hjk->hik', inv, d_inv), inv)
    return d_lower, jnp.zeros_like(inv)


_known_inverse.defvjp(_known_inverse_fwd, _known_inverse_bwd)


def _gdn_local(q, k, v, bgs, inv_known=None):
    lane = lax.broadcasted_iota(jnp.int32, (1, 128), 1)
    ri = lax.broadcasted_iota(jnp.int32, (C_CHUNK, C_CHUNK), 0)
    ci = lax.broadcasted_iota(jnp.int32, (C_CHUNK, C_CHUNK), 1)
    row_id = lax.broadcasted_iota(jnp.int32, (128, C_CHUNK), 0)
    beta, gc, gcj = [], [], []
    for bg in bgs:
        gc_t = _hdot((ri >= ci).astype(F32), bg)
        gc_rows = gc_t.T
        for h in range(C_HEADS):
            beta.append(jnp.sum(jnp.where(lane == h, bg, 0.0), axis=-1, keepdims=True))
            gc.append(jnp.sum(jnp.where(lane == C_HEADS + h, gc_t, 0.0), axis=-1, keepdims=True))
            gcj.append(jnp.sum(jnp.where(row_id == C_HEADS + h, gc_rows, 0.0), axis=0, keepdims=True))
    beta, gc, gcj = jnp.stack(beta, axis=0), jnp.stack(gc, axis=0), jnp.stack(gcj, axis=0)
    tril, strict = (ri >= ci)[None], (ri > ci)[None]
    decay = jnp.exp(jnp.where(tril, gc - gcj, -1e30))
    kb = k * beta
    lower = jnp.where(strict, _bein('hid,hjd->hij', kb, k) * decay, 0.0)
    inv = _unit_lower_inverse(lower) if inv_known is None else _known_inverse(lower, inv_known)
    egc = jnp.exp(gc)
    u_c = _bein('hij,hjd->hid', inv, v * beta)
    w_c = _bein('hij,hjd->hid', inv, kb * egc)
    aqk = _bein('hid,hjd->hij', q, k) * decay
    rowi = lax.broadcasted_iota(jnp.int32, (1, C_CHUNK, 1), 1)
    g_last = jnp.sum(jnp.where(rowi == C_CHUNK - 1, gc, 0.0), axis=1, keepdims=True)
    kd = k * jnp.exp(g_last - gc)
    return (u_c, w_c, aqk, q * egc, kd, jnp.exp(g_last)), inv


def _gdn_state(local, state):
    u_c, w_c, aqk, qg, kd, dec = local
    v_new = u_c - _bein('hik,hkv->hiv', w_c, state)
    o = _bein('hik,hkv->hiv', qg, state) + _bein('hij,hjv->hiv', aqk, v_new)
    return o, state * dec + _bein('hik,hiv->hkv', kd, v_new)


C_SUB = 4


def _gdn_group(q, k, v, bgs, state, inv_known=None):
    local, inv = _gdn_local(q, k, v, bgs, inv_known)
    outs = []
    for s in range(len(bgs)):
        o, state = _gdn_state(tuple(t[s * C_HEADS:(s + 1) * C_HEADS] for t in local), state)
        outs.append(o)
    return outs, state, inv


def _heads(ref):
    return jnp.stack([ref[s * C_CHUNK:(s + 1) * C_CHUNK, h * C_DK:(h + 1) * C_DK] for s in range(C_SUB) for h in range(C_HEADS)], axis=0)


def _put_heads(ref, sub, val):
    rows = slice(sub * C_CHUNK, (sub + 1) * C_CHUNK)
    for h in range(C_HEADS):
        ref[rows, h * C_DK:(h + 1) * C_DK] = val[h]


def _gdn_specs(s_len, rev):
    rows = C_SUB * C_CHUNK
    n_g = s_len // rows
    at = (lambda i: n_g - 1 - i) if rev else (lambda i: i)
    col = lambda c: pl.BlockSpec((rows, 1024), lambda i: (at(i), c))
    row128 = pl.BlockSpec((rows, 128), lambda i: (at(i), 0))
    state = pl.BlockSpec((1, C_HEADS, C_DK, C_DK), lambda i: (at(i), 0, 0, 0))
    inv = pl.BlockSpec((1, C_SUB * C_HEADS, C_CHUNK, C_CHUNK), lambda i: (at(i), 0, 0, 0))
    return n_g, col, row128, state, inv


def _gdn_fwd(qkv, bg):
    s_len = qkv.shape[0]
    n_g, col, row128, state_spec, inv_spec = _gdn_specs(s_len, False)

    def body(q_ref, k_ref, v_ref, bg_ref, o_ref, ss_ref, inv_ref, st_ref):
        _zero_at_first([st_ref])
        s0 = st_ref[...]
        ss_ref[0] = s0
        bgs = [bg_ref[s * C_CHUNK:(s + 1) * C_CHUNK, :] for s in range(C_SUB)]
        outs, s2, inv = _gdn_group(_heads(q_ref), _heads(k_ref), _heads(v_ref), bgs, s0)
        st_ref[...] = s2
        inv_ref[0] = inv
        for s in range(C_SUB):
            _put_heads(o_ref, s, outs[s])

    return pl.pallas_call(
        body, name="gdn_fwd", grid=(n_g,),
        in_specs=[col(0), col(1), col(2), row128],
        out_specs=[col(0), state_spec, inv_spec],
        out_shape=[_sds(s_len, 1024), _sds(n_g, C_HEADS, C_DK, C_DK), _sds(n_g, C_SUB * C_HEADS, C_CHUNK, C_CHUNK)],
        scratch_shapes=[pltpu.VMEM((C_HEADS, C_DK, C_DK), F32)],
        compiler_params=_cp("arbitrary"),
    )(qkv, qkv, qkv, bg)


def _gdn_bwd(qkv, bg, states, invs, do):
    s_len = qkv.shape[0]
    n_g, col, row128, state_spec, inv_spec = _gdn_specs(s_len, True)

    def body(q_ref, k_ref, v_ref, bg_ref, ss_ref, inv_ref, do_ref, dq_ref, dk_ref, dv_ref, dbg_ref, ds_ref):
        _zero_at_first([ds_ref])
        inv_known = inv_ref[0]

        def group(q, k, v, bgs, st):
            outs, st2, _ = _gdn_group(q, k, v, bgs, st, inv_known)
            return outs, st2

        bgs = [bg_ref[s * C_CHUNK:(s + 1) * C_CHUNK, :] for s in range(C_SUB)]
        _, vjp = jax.vjp(group, _heads(q_ref), _heads(k_ref), _heads(v_ref), bgs, ss_ref[0])
        douts = [jnp.stack([do_ref[s * C_CHUNK:(s + 1) * C_CHUNK, h * C_DK:(h + 1) * C_DK] for h in range(C_HEADS)], axis=0)
                 for s in range(C_SUB)]
        dq, dk, dv, dbgs, ds = vjp((douts, ds_ref[...]))
        ds_ref[...] = ds
        for s in range(C_SUB):
            dbg_ref[s * C_CHUNK:(s + 1) * C_CHUNK, :] = dbgs[s]
            for ref, val in ((dq_ref, dq), (dk_ref, dk), (dv_ref, dv)):
                _put_heads(ref, s, val[s * C_HEADS:(s + 1) * C_HEADS])

    return pl.pallas_call(
        body, name="gdn_bwd", grid=(n_g,),
        in_specs=[col(0), col(1), col(2), row128, state_spec, inv_spec, col(0)],
        out_specs=[col(0), col(0), col(0), row128],
        out_shape=[_sds(s_len, 1024)] * 3 + [_sds(s_len, 128)],
        scratch_shapes=[pltpu.VMEM((C_HEADS, C_DK, C_DK), F32)],
        compiler_params=_cp("arbitrary"),
    )(qkv, qkv, qkv, bg, states, invs, do)


def _head_norm_gate(o, gate, norm_g):
    return (_rms(o) * norm_g) * jax.nn.silu(gate)


def _l1_out_fb(o, gate_c, x1, target, norm_g, w_out, post_g, gate):
    s_len = x1.shape[0]

    def body(o_ref, gc_ref, x1_ref, t_ref, ng_ref, w_ref, pg_ref, gt_ref,
             loss_ref, dres_ref, do_ref, dgc_ref, dw_ref, dng_ref, dpg_ref, dgt_ref):
        _zero_at_first([loss_ref, dw_ref, dng_ref, dpg_ref, dgt_ref])
        ng = ng_ref[...]
        ons, vjps = [], []
        for h in range(C_HEADS):
            sl = slice(h * C_DK, (h + 1) * C_DK)
            on, vjp_h = jax.vjp(_head_norm_gate, o_ref[:, sl], gc_ref[:, sl], ng)
            ons.append(on)
            vjps.append(vjp_h)
        on_all = jnp.concatenate(ons, axis=-1)
        y = _bdot(on_all, w_ref[...])
        x2, vjp2 = jax.vjp(_post_res, y, x1_ref[...], pg_ref[...], gt_ref[...])
        err = x2 - t_ref[...]
        _acc(loss_ref, jnp.full((1, 128), 0.5 * jnp.sum(jnp.mean(err * err, axis=-1)), F32))
        dx2 = err * (1.0 / D_MODEL)
        dy, _, dpg, dgt = vjp2(dx2)
        dres_ref[...] = dx2
        _acc(dpg_ref, dpg)
        _acc(dgt_ref, dgt)
        dw_ref[...] += _bdot_tn(on_all, dy)
        don = _bdot_nt(dy, w_ref[...])
        for h in range(C_HEADS):
            sl = slice(h * C_DK, (h + 1) * C_DK)
            do_h, dgc_h, dng = vjps[h](don[:, sl])
            do_ref[:, sl] = do_h
            dgc_ref[:, sl] = dgc_h
            _acc(dng_ref, dng)

    vec, r10 = _fix((1, D_MODEL)), _row(TM, D_MODEL)
    row128 = _fix((1, 128))
    return pl.pallas_call(
        body, name="l1_out_fb", grid=(s_len // TM,),
        in_specs=[r10, r10, r10, r10, row128, _fix((D_MODEL, D_MODEL)), vec, vec],
        out_specs=[row128, r10, r10, r10, _fix((D_MODEL, D_MODEL)), row128, vec, vec],
        out_shape=[_sds(1, 128), _sds(s_len, D_MODEL), _sds(s_len, D_MODEL), _sds(s_len, D_MODEL),
                   _sds(D_MODEL, D_MODEL), _sds(1, 128), _sds(1, D_MODEL), _sds(1, D_MODEL)],
        compiler_params=_cp("arbitrary"),
    )(o, gate_c, x1, target, norm_g, w_out, post_g, gate)


def _row_of(v, width, at):
    return jnp.zeros((1, width), F32).at[0, at:at + v.shape[-1]].set(v.reshape(-1))


def _local_step(x, target, mod, wd, comm=None):
    s_len = x.shape[0]
    shift0, scale0, gate0 = (mod[0:1, i * 1024:(i + 1) * 1024] for i in range(3))
    shift1, scale1, gate1 = (mod[1:2, i * 1024:(i + 1) * 1024] for i in range(3))
    pre_g0, pre_g1 = wd["pre_g"][0:1], wd["pre_g"][1:2]
    post_g0, post_g1 = wd["post_g"][0:1], wd["post_g"][1:2]
    w_in0 = wd["ab_w_in"].astype(BF)
    d_skip, glu_b = wd["s5_d"].reshape(1, 512), wd["s5_glu_b"].reshape(1, 512)
    norm_g = wd["gdn_norm_g"].reshape(1, 128)
    alog_row = _row_of(wd["gdn_a_log"], 128, C_HEADS)
    dtb_row = _row_of(wd["gdn_dt_bias"], 128, C_HEADS)
    conv_w = wd["gdn_conv"]

    a_re, a_im = wd["s5_a_re"], wd["s5_a_im"]
    log_dt = wd["s5_log_dt"].reshape(B_GROUPS, 1)
    bt_re = wd["s5_b_re"].transpose(0, 2, 1).reshape(B_WIDTH, B_STATE)
    bt_im = wd["s5_b_im"].transpose(0, 2, 1).reshape(B_WIDTH, B_STATE)
    abar_r, abar_i, bbar_r, bbar_i = _s5_params(a_re, a_im, log_dt, bt_re, bt_im)
    abr, abi = abar_r.reshape(1, -1), abar_i.reshape(1, -1)
    btr, bti = _blockdiag_b(bbar_r).astype(BF), _blockdiag_b(bbar_i).astype(BF)
    ctr, cti = _blockdiag_c(wd["s5_c_re"]).astype(BF), _blockdiag_c(wd["s5_c_im"]).astype(BF)

    table = _bucket_table()
    biases = _attn_bias(wd["rel_bias"], jnp.asarray(table))
    front = _l0_front(x, pre_g0, scale0, shift0, w_in0)
    qs, ks, vs = front[0:3], front[3:6], front[6:9]
    u, ga, gb, h0 = front[9:]
    riders = [None] * 4 if comm is None else comm.late_exchanges()
    os, ls, got = [], [], []
    for i in range(3):
        (o_d, l_d), g = _attn_fwd(qs[i], ks[i], vs[i], biases[i], exchange=riders[i])
        os.append(o_d)
        ls.append(l_d)
        got.append(g)
    seg_len = s_len // S5_SEG
    zero_state = (jnp.zeros((S5_SEG, S5_WIDTH), F32),) * 2
    ends, _ = _s5_seq_fwd(u, btr, bti, ctr, cti, abr, abi, zero_state, False)
    x_entry = _s5_entries("s5_entries_fwd", *ends, abr, abi, seg_len, False)
    (xr, xi, ypre3, _, _), g = _s5_seq_fwd(u, btr, bti, ctr, cti, abr, abi, x_entry, True, exchange=riders[3])
    got.append(g)
    ypre = ypre3.reshape(s_len, B_WIDTH)
    rider = None
    if comm is not None:
        mine0, mine1 = comm.late_halves(got)
        wd = {**wd, **comm.full_weights(["ab_w_out", "s5_glu_w"], mine0, _sibling_exchange("gather_w_sibling_l0", mine0))}
        rider = (mine1, "sibling")
    w_out0 = wd["ab_w_out"].astype(BF)
    glu_w = wd["s5_glu_w"].astype(BF)
    (x1, y0), theirs1 = _l0_out(os, ls, ga, gb, ypre, u, x, d_skip, glu_w, glu_b, w_out0, post_g0, gate0, exchange=rider)
    if comm is not None:
        wd = {**wd, **comm.full_weights(["gdn_w_in", "gdn_w_out"], mine1, theirs1)}
    w_in1 = jnp.concatenate([wd["gdn_w_in"], jnp.zeros((D_MODEL, C_IN_PAD - wd["gdn_w_in"].shape[1]), wd["gdn_w_in"].dtype)], axis=1).astype(BF)
    w_out1 = wd["gdn_w_out"].astype(BF)

    raw, gate_c, ba, h1 = _l1_front(x1, pre_g1, scale1, shift1, w_in1)
    qkv, bg = _gdn_prep(raw, ba, conv_w, alog_row, dtb_row)
    o_gdn, states, invs = _gdn_fwd(qkv, bg)
    loss_row, dres1, do_gdn, dgate_c, dw_out1, dnorm_g, dpost_g1, dgate1 = _l1_out_fb(
        o_gdn, gate_c, x1, target, norm_g, w_out1, post_g1, gate1)

    dq1, dk1, dv1, dbg = _gdn_bwd(qkv, bg, states, invs, do_gdn)
    draw, dba, dconv_w, dalog_row, ddtb_row = _gdn_prep_bwd(raw, ba, conv_w, alog_row, dtb_row, dq1, dk1, dv1, dbg)
    dz1, dx1, dpre_g1, dscale1, dshift1 = _front_bwd(
        "l1_front_bwd", x1, pre_g1, scale1, shift1, w_in1, dres1, [[draw], [dgate_c], [dba]], [QKV, 1024, 128])
    dw_in1 = _matmul_tn("l1_dw_in", h1, dz1, 1408)

    n_w1 = wd["gdn_w_in"].shape[1]
    l1_names, l0_names = ["gdn_w_in", "gdn_w_out"], ["ab_w_out", "s5_glu_w"]
    rider = None if comm is None else (comm.split_halves({"gdn_w_in": dw_in1[:, :n_w1], "gdn_w_out": dw_out1}), "sibling")
    l0b, from_sibling1 = _l0_out_bwd(os, ls, ga, gb, ypre, u, x, y0, d_skip, glu_w, glu_b, w_out0, post_g0, gate0, dx1, exchange=rider)
    dos, dls = l0b[0:3], l0b[3:6]
    dga, dgb, dypre, du_skip, dd_skip, dglu_w, dglu_b, dw_out0, dpost_g0, dgate0 = l0b[6:]
    rider = None if comm is None else (comm.split_halves({"ab_w_out": dw_out0, "s5_glu_w": dglu_w}), "sibling")
    starts, _ = _s5_seq_bwd(dypre, None, None, None, btr, bti, ctr, cti, abr, abi, zero_state, None, False)
    g_entry = _s5_entries("s5_entries_bwd", *starts, abr, -abi, seg_len, True)
    (du3, dbtr, dbti, dctr, dcti, dabr, dabi, _, _), from_sibling0 = _s5_seq_bwd(
        dypre, xr, xi, u, btr, bti, ctr, cti, abr, abi, g_entry, x_entry, True, exchange=rider)
    du_scan = du3.reshape(s_len, B_WIDTH)
    riders = [None] * 3
    if comm is not None:
        riders = [(comm.chip_partials(l1_names, from_sibling1), "scatter"), (comm.chip_partials(l0_names, from_sibling0), "scatter"), None]
    dqs, dks, dvs, dbs = [], [], [], []
    for i in range(3):
        (dq_d, dk_d, dv_d, db_d), got_d = _attn_bwd(qs[i], ks[i], vs[i], biases[i], os[i], ls[i], dos[i], dls[i], exchange=riders[i])
        dqs.append(dq_d)
        dks.append(dk_d)
        dvs.append(dv_d)
        dbs.append(db_d)
        if comm is not None and riders[i] is not None:
            comm.received.update(zip((l1_names, l0_names)[i], got_d))
    parts = [dqs, dks, dvs, [du_skip, du_scan], [dga], [dgb]]
    dz0, grad_x, dpre_g0, dscale0, dshift0 = _front_bwd(
        "l0_front_bwd", x, pre_g0, scale0, shift0, w_in0, dx1, parts, [512] * 6)
    dw_in0 = _matmul_tn("l0_dw_in", h0, dz0, 768)

    idx_rows = jnp.asarray(table.reshape(3, -1), F32)
    drel = _rel_bias_grad(dbs, idx_rows).T
    da_re, da_im, dlog_dt, dbt_re, dbt_im = _s5_params_bwd(
        a_re, a_im, log_dt, bt_re, bt_im, dabr.reshape(B_GROUPS, B_STATE), dabi.reshape(B_GROUPS, B_STATE),
        _blockdiag_b_t(dbtr), _blockdiag_b_t(dbti))
    unb = lambda d: d.reshape(B_GROUPS, B_GROUP, B_STATE).transpose(0, 2, 1)
    grads = {
        "pre_g": jnp.concatenate([dpre_g0, dpre_g1], 0), "post_g": jnp.concatenate([dpost_g0, dpost_g1], 0),
        "rel_bias": drel, "ab_w_in": dw_in0, "ab_w_out": dw_out0,
        "s5_a_re": da_re, "s5_a_im": da_im, "s5_log_dt": dlog_dt.reshape(B_GROUPS),
        "s5_b_re": unb(dbt_re), "s5_b_im": unb(dbt_im),
        "s5_c_re": _blockdiag_c_t(dctr), "s5_c_im": _blockdiag_c_t(dcti),
        "s5_d": dd_skip.reshape(512), "s5_glu_w": dglu_w, "s5_glu_b": dglu_b.reshape(512),
        "gdn_w_in": dw_in1[:, :wd["gdn_w_in"].shape[1]], "gdn_conv": dconv_w,
        "gdn_a_log": dalog_row[0, C_HEADS:2 * C_HEADS], "gdn_dt_bias": ddtb_row[0, C_HEADS:2 * C_HEADS],
        "gdn_norm_g": dnorm_g.reshape(128), "gdn_w_out": dw_out1,
    }
    dmod = jnp.concatenate([jnp.concatenate([dshift0, dscale0, dgate0], 1), jnp.concatenate([dshift1, dscale1, dgate1], 1)], 0)
    return loss_row[0, 0], grad_x, grads, dmod


def _place():
    return lax.axis_index("x"), lax.axis_index("y"), lax.axis_index("c")


def _flip(v, bit):
    return 1 - v if bit else v


def _hbm_call(name, body, arrs, out_shapes, n_sem):
    any_spec = pl.BlockSpec(memory_space=pl.ANY)
    return pl.pallas_call(
        body, name=name,
        in_specs=[any_spec] * len(arrs), out_specs=[any_spec] * len(out_shapes), out_shape=out_shapes,
        scratch_shapes=[pltpu.SemaphoreType.DMA((n_sem,)), pltpu.SemaphoreType.DMA((n_sem,))],
    )(*arrs)


def _own_slot(gathered, own, slot):
    idx = lax.broadcasted_iota(jnp.int32, (gathered.shape[0],) + (1,) * own.ndim, 0)
    return jnp.where(idx == slot, own[None], gathered)


def _all_gather8(name, arr):
    def body(x_ref, out_ref, send_sems, recv_sems):
        x, y, c = _place()
        me = 4 * x + 2 * y + c
        sends, recvs = [], []
        for m in range(1, 8):
            peer = (_flip(x, m & 4), _flip(y, m & 2), _flip(c, m & 1))
            sends.append(pltpu.make_async_remote_copy(x_ref, out_ref.at[me], send_sems.at[m - 1], recv_sems.at[m - 1],
                                                      device_id=peer, device_id_type=MESH))
            recvs.append(pltpu.make_async_remote_copy(x_ref, out_ref.at[4 * peer[0] + 2 * peer[1] + peer[2]], send_sems.at[m - 1],
                                                      recv_sems.at[m - 1], device_id=peer, device_id_type=MESH))
        for cp in sends:
            cp.start()
        for cp in recvs:
            cp.wait_recv()
        for cp in sends:
            cp.wait_send()

    return _hbm_call(name, body, [arr], [jax.ShapeDtypeStruct((8,) + arr.shape, arr.dtype)], 7)[0]


def _all_to_all8(name, arr):
    def body(x_ref, out_ref, send_sems, recv_sems):
        x, y, c = _place()
        me = 4 * x + 2 * y + c
        sends, recvs = [], []
        for m in range(1, 8):
            peer = (_flip(x, m & 4), _flip(y, m & 2), _flip(c, m & 1))
            peer_id = 4 * peer[0] + 2 * peer[1] + peer[2]
            sends.append(pltpu.make_async_remote_copy(x_ref.at[peer_id], out_ref.at[me], send_sems.at[m - 1], recv_sems.at[m - 1],
                                                      device_id=peer, device_id_type=MESH))
            recvs.append(pltpu.make_async_remote_copy(x_ref.at[peer_id], out_ref.at[peer_id], send_sems.at[m - 1], recv_sems.at[m - 1],
                                                      device_id=peer, device_id_type=MESH))
        for cp in sends:
            cp.start()
        for cp in recvs:
            cp.wait_recv()
        for cp in sends:
            cp.wait_send()

    return _hbm_call(name, body, [arr], [jax.ShapeDtypeStruct(arr.shape, arr.dtype)], 7)[0]


def _chip_copies(ins, outs, send_sems, recv_sems, scatter):
    x, y, c = _place()
    mine = 2 * x + y
    sends, recvs = [], []
    for a in range(len(ins)):
        for m in range(1, 4):
            px, py = _flip(x, m & 2), _flip(y, m & 1)
            k = 3 * a + m - 1
            src = ins[a].at[2 * px + py] if scatter else ins[a]
            sends.append(pltpu.make_async_remote_copy(src, outs[a].at[mine], send_sems.at[k], recv_sems.at[k],
                                                      device_id=(px, py, c), device_id_type=MESH))
            recvs.append(pltpu.make_async_remote_copy(src, outs[a].at[2 * px + py], send_sems.at[k], recv_sems.at[k],
                                                      device_id=(px, py, c), device_id_type=MESH))
    return sends, recvs


def _chip_shapes(arrs, scatter):
    return [jax.ShapeDtypeStruct(a.shape if scatter else (4,) + a.shape, a.dtype) for a in arrs]


def _chip_exchange(name, arrs, scatter):
    n = len(arrs)

    def body(*refs):
        sends, recvs = _chip_copies(refs[:n], refs[n:2 * n], refs[2 * n], refs[2 * n + 1], scatter)
        for cp in sends:
            cp.start()
        for cp in recvs:
            cp.wait_recv()
        for cp in sends:
            cp.wait_send()

    return _hbm_call(name, body, arrs, _chip_shapes(arrs, scatter), 3 * n)


def _call_with_exchange(body, name, grid, in_specs, out_specs, out_shape, scratch_shapes, args, exchange):
    if exchange is None:
        return pl.pallas_call(body, name=name, grid=grid, in_specs=in_specs, out_specs=out_specs, out_shape=out_shape,
                              scratch_shapes=scratch_shapes, compiler_params=_cp(*["arbitrary"] * len(grid)))(*args), []
    arrs, kind = exchange
    n_in, n_out, n_ex, n_scr = len(in_specs), len(out_specs), len(arrs), len(scratch_shapes)
    n_sem = n_ex if kind == "sibling" else 3 * n_ex
    ex_shapes = [jax.ShapeDtypeStruct(a.shape, a.dtype) for a in arrs] if kind == "sibling" else _chip_shapes(arrs, kind == "scatter")

    def fused(*refs):
        ins, ex_in = refs[:n_in], refs[n_in:n_in + n_ex]
        outs, ex_out = refs[n_in + n_ex:n_in + n_ex + n_out], refs[n_in + n_ex + n_out:n_in + 2 * n_ex + n_out]
        rest = refs[n_in + 2 * n_ex + n_out:]
        if kind == "sibling":
            sends = recvs = _sibling_copies(ex_in, ex_out, rest[n_scr], rest[n_scr + 1])
        else:
            sends, recvs = _chip_copies(ex_in, ex_out, rest[n_scr], rest[n_scr + 1], kind == "scatter")
        first, last = pl.program_id(0) == 0, pl.program_id(0) == grid[0] - 1
        for k in range(1, len(grid)):
            first, last = first & (pl.program_id(k) == 0), last & (pl.program_id(k) == grid[k] - 1)

        @pl.when(first)
        def _():
            for cp in sends:
                cp.start()

        body(*ins, *outs, *rest[:n_scr])

        @pl.when(last)
        def _():
            for cp in recvs:
                cp.wait_recv()
            for cp in sends:
                cp.wait_send()

    any_spec = pl.BlockSpec(memory_space=pl.ANY)
    res = pl.pallas_call(
        fused, name=name, grid=grid, in_specs=list(in_specs) + [any_spec] * n_ex, out_specs=list(out_specs) + [any_spec] * n_ex,
        out_shape=list(out_shape) + ex_shapes,
        scratch_shapes=list(scratch_shapes) + [pltpu.SemaphoreType.DMA((n_sem,))] * 2,
        compiler_params=_cp(*["arbitrary"] * len(grid)))(*args, *arrs)
    return res[:n_out], res[n_out:]


def _sibling_copies(ins, outs, send_sems, recv_sems):
    x, y, c = _place()
    return [pltpu.make_async_remote_copy(ins[a], outs[a], send_sems.at[a], recv_sems.at[a],
                                         device_id=(x, y, 1 - c), device_id_type=MESH) for a in range(len(ins))]


def _sibling_exchange(name, arrs):
    n = len(arrs)

    def body(*refs):
        copies = _sibling_copies(refs[:n], refs[n:2 * n], refs[2 * n], refs[2 * n + 1])
        for cp in copies:
            cp.start()
        for cp in copies:
            cp.wait_recv()
        for cp in copies:
            cp.wait_send()

    return _hbm_call(name, body, arrs, [jax.ShapeDtypeStruct(a.shape, a.dtype) for a in arrs], n)


def _row_tile(rows):
    for t in (256, 128, 64, 32, 16, 8):
        if rows % t == 0:
            return t
    return rows


def _pair_sum(name, a, b, out_dtype):
    rows, cols = a.shape
    tr = _row_tile(rows)

    def body(a_ref, b_ref, o_ref):
        o_ref[...] = (a_ref[...] + b_ref[...]).astype(out_dtype)

    return pl.pallas_call(body, name=name, grid=(rows // tr,), in_specs=[_row(tr, cols)] * 2, out_specs=_row(tr, cols),
                          out_shape=_sds(rows, cols, dtype=out_dtype), compiler_params=_cp("arbitrary"))(a, b)


def _chip_sum(name, recv, partial, mine):
    n, rows, cols = recv.shape
    tr = _row_tile(rows)

    def body(mine_ref, *refs):
        own = refs[n][0].astype(F32)
        acc = None
        for s in range(n):
            term = jnp.where(mine_ref[0] == s, own, refs[s][0].astype(F32))
            acc = term if acc is None else acc + term
        refs[-1][...] = acc

    def slot_spec(s):
        return pl.BlockSpec((1, tr, cols), lambda i, m: (jnp.where(m[0] == s, (s + 1) % n, s), i, 0))

    grid_spec = pltpu.PrefetchScalarGridSpec(
        num_scalar_prefetch=1, grid=(rows // tr,),
        in_specs=[slot_spec(s) for s in range(n)] + [pl.BlockSpec((1, tr, cols), lambda i, m: (m[0], i, 0))],
        out_specs=pl.BlockSpec((tr, cols), lambda i, m: (i, 0)))
    return pl.pallas_call(body, name=name, grid_spec=grid_spec, out_shape=_sds(rows, cols),
                          compiler_params=_cp("arbitrary"))(mine, *([recv] * n), partial)


def _slot_sum(name, arr):
    n, rows, cols = arr.shape
    tr = _row_tile(rows)

    def body(*refs):
        acc = refs[0][0]
        for r in refs[1:-1]:
            acc = acc + r[0]
        refs[-1][...] = acc

    specs = [pl.BlockSpec((1, tr, cols), functools.partial(lambda s, i: (s, i, 0), s)) for s in range(n)]
    return pl.pallas_call(body, name=name, grid=(rows // tr,), in_specs=specs, out_specs=_row(tr, cols),
                          out_shape=_sds(rows, cols), compiler_params=_cp("arbitrary"))(*([arr] * n))


def _adamw(name, w, g, m, v):
    rows, cols = w.shape
    tr = _row_tile(rows)

    def body(w_ref, g_ref, m_ref, v_ref, d_ref, nm_ref, nv_ref):
        g_ = g_ref[...]
        m_ = ADAM_B1 * m_ref[...] + (1.0 - ADAM_B1) * g_
        v_ = ADAM_B2 * v_ref[...] + (1.0 - ADAM_B2) * (g_ * g_)
        m_hat = m_ / (1.0 - ADAM_B1 ** ADAM_STEP)
        v_hat = v_ / (1.0 - ADAM_B2 ** ADAM_STEP)
        d_ref[...] = -ADAM_LR * (m_hat / (jnp.sqrt(v_hat) + ADAM_EPS) + ADAM_WD * w_ref[...])
        nm_ref[...] = m_
        nv_ref[...] = v_

    spec = _row(tr, cols)
    return pl.pallas_call(body, name=name, grid=(rows // tr,), in_specs=[spec] * 4, out_specs=[spec] * 3,
                          out_shape=[_sds(rows, cols)] * 3, compiler_params=_cp("arbitrary"))(w, g, m, v)


def _adamw_many(name, ws, gs, ms, vs):
    n = len(ws)

    def body(*refs):
        for i in range(n):
            w_ref, g_ref, m_ref, v_ref = (refs[k * n + i] for k in range(4))
            d_ref, nm_ref, nv_ref = (refs[(4 + k) * n + i] for k in range(3))
            g_ = g_ref[...]
            m_ = ADAM_B1 * m_ref[...] + (1.0 - ADAM_B1) * g_
            v_ = ADAM_B2 * v_ref[...] + (1.0 - ADAM_B2) * (g_ * g_)
            m_hat = m_ / (1.0 - ADAM_B1 ** ADAM_STEP)
            v_hat = v_ / (1.0 - ADAM_B2 ** ADAM_STEP)
            d_ref[...] = -ADAM_LR * (m_hat / (jnp.sqrt(v_hat) + ADAM_EPS) + ADAM_WD * w_ref[...])
            nm_ref[...] = m_
            nv_ref[...] = v_

    shapes = [_sds(*a.shape) for a in ws]
    res = pl.pallas_call(body, name=name, out_shape=shapes * 3,
                         compiler_params=pltpu.CompilerParams(vmem_limit_bytes=VMEM_LIMIT_BYTES))(*ws, *gs, *ms, *vs)
    return [(res[i], res[n + i], res[2 * n + i]) for i in range(n)]


def _adamw_halves(name, w, g_mine, g_sibling, m, v, core):
    _, rows, cols = w.shape
    half = rows // 2
    tr = _row_tile(half)
    per_half = half // tr

    def body(core_ref, w_ref, gm_ref, gs_ref, m_ref, v_ref, g_ref, d_ref, nm_ref, nv_ref):
        g_ = jnp.where(pl.program_id(0) // per_half == core_ref[0], gm_ref[...], gs_ref[...])
        m_ = ADAM_B1 * m_ref[...] + (1.0 - ADAM_B1) * g_
        v_ = ADAM_B2 * v_ref[...] + (1.0 - ADAM_B2) * (g_ * g_)
        m_hat = m_ / (1.0 - ADAM_B1 ** ADAM_STEP)
        v_hat = v_ / (1.0 - ADAM_B2 ** ADAM_STEP)
        g_ref[...] = g_
        d_ref[...] = -ADAM_LR * (m_hat / (jnp.sqrt(v_hat) + ADAM_EPS) + ADAM_WD * w_ref[...])
        nm_ref[...] = m_
        nv_ref[...] = v_

    full = pl.BlockSpec((None, tr, cols), lambda i, c: (0, i, 0))
    in_half = pl.BlockSpec((tr, cols), lambda i, c: (i % per_half, 0))
    grid_spec = pltpu.PrefetchScalarGridSpec(num_scalar_prefetch=1, grid=(rows // tr,),
                                             in_specs=[full, in_half, in_half, full, full], out_specs=[full] * 4)
    return pl.pallas_call(body, name=name, grid_spec=grid_spec, out_shape=[_sds(1, rows, cols)] * 4,
                          compiler_params=_cp("arbitrary"))(core, w, g_mine, g_sibling, m, v)


def _mod_local(c_all, ada_w):
    def body(c_ref, w_ref, o_ref):
        c_act = jax.nn.silu(c_ref[...])
        for l in range(2):
            o_ref[l] = _hdot(c_act, w_ref[l])

    return pl.pallas_call(body, name="mod_local", out_shape=_sds(2, 8, ada_w.shape[2]),
                          compiler_params=pltpu.CompilerParams(vmem_limit_bytes=VMEM_LIMIT_BYTES))(c_all, ada_w)


def _ada_w_grad(c_all, dmod_cols):
    def body(c_ref, d_ref, o_ref):
        c_act = jax.nn.silu(c_ref[...])
        for l in range(2):
            o_ref[l] = lax.dot_general(c_act, d_ref[l], (((0,), (0,)), ((), ())), precision=HI, preferred_element_type=F32)

    return pl.pallas_call(body, name="ada_w_grad", out_shape=_sds(2, D_MODEL, dmod_cols.shape[2]),
                          compiler_params=pltpu.CompilerParams(vmem_limit_bytes=VMEM_LIMIT_BYTES))(c_all, dmod_cols)


_SMALL = ("ada_b", "pre_g", "post_g", "rel_bias", "s5_a_re", "s5_a_im", "s5_log_dt", "s5_b_re", "s5_b_im", "s5_c_re", "s5_c_im",
          "s5_d", "s5_glu_b", "gdn_a_log", "gdn_dt_bias", "gdn_norm_g")
_SHARDED = ("ab_w_in", "ab_w_out", "s5_glu_w", "gdn_w_in", "gdn_w_out")
_COL_SHARDED = ("ab_w_in", "gdn_w_in")
_WEIGHTS = ("ada_w", "ada_b", "pre_g", "post_g", "rel_bias", "ab_w_in", "ab_w_out", "s5_a_re", "s5_a_im", "s5_log_dt", "s5_b_re",
            "s5_b_im", "s5_c_re", "s5_c_im", "s5_d", "s5_glu_w", "s5_glu_b", "gdn_w_in", "gdn_conv", "gdn_a_log", "gdn_dt_bias",
            "gdn_norm_g", "gdn_w_out")


def _rows128(n):
    return -(-n // 128)


def _pack(arrs, total_rows):
    pieces = []
    for a in arrs:
        flat = a.reshape(-1)
        pieces.append(jnp.pad(flat, (0, _rows128(flat.shape[0]) * 128 - flat.shape[0])).reshape(-1, 128))
    used = sum(p.shape[0] for p in pieces)
    pieces.append(jnp.zeros((total_rows - used, 128), F32))
    return jnp.concatenate(pieces, axis=0)


def _unpack(buf, shapes):
    out, at = [], 0
    for shp in shapes:
        n = int(np.prod(shp))
        out.append(buf[at:at + _rows128(n)].reshape(-1)[:n].reshape(shp))
        at += _rows128(n)
    return out


def _full_from_halves(name, g):
    if name in _COL_SHARDED:
        return g.transpose(0, 2, 1, 3).reshape(2 * g.shape[2], 4 * g.shape[3])
    return g.transpose(1, 0, 2, 3).reshape(8 * g.shape[2], g.shape[3])


def _shard_major(name, g):
    if name in _COL_SHARDED:
        return g.reshape(g.shape[0], 4, g.shape[1] // 4).transpose(1, 0, 2)
    return g.reshape(4, g.shape[0] // 4, g.shape[1])


_LATE = ("ab_w_out", "s5_glu_w", "gdn_w_in", "gdn_w_out")


class _WeightExchanges:
    def __init__(self, shards, core, chip):
        self.core, self.chip = core, chip
        self.half = {}
        for name, shard in shards.items():
            h = shard.shape[0] // 2
            self.half[name] = lax.dynamic_slice_in_dim(shard.astype(BF), core * h, h, axis=0)
        self.mine, self.partial, self.received = {}, {}, {}

    def my_halves(self, names, from_chips):
        return [_own_slot(g, self.half[n], self.chip) for n, g in zip(names, from_chips)]

    def full_weights(self, names, mine, theirs):
        return {n: _full_from_halves(n, jnp.where(self.core == 0, jnp.stack([a, b], 0), jnp.stack([b, a], 0)))
                for n, a, b in zip(names, mine, theirs)}

    def first_weights(self):
        mine = self.my_halves(["ab_w_in"], _chip_exchange("gather_w_chips", [self.half["ab_w_in"]], False))
        return self.full_weights(["ab_w_in"], mine, _sibling_exchange("gather_w_sibling_first", mine))

    def late_exchanges(self):
        rows = self.half["gdn_w_in"].shape[0] // 2
        pieces = [self.half["gdn_w_in"][:rows], self.half["gdn_w_in"][rows:]]
        return [([self.half["ab_w_out"], self.half["s5_glu_w"]], "gather"), ([self.half["gdn_w_out"]], "gather"),
                ([pieces[0]], "gather"), ([pieces[1]], "gather")]

    def late_halves(self, got):
        return (self.my_halves(["ab_w_out", "s5_glu_w"], got[0]),
                self.my_halves(["gdn_w_in", "gdn_w_out"], [jnp.concatenate([got[2][0], got[3][0]], axis=1), got[1][0]]))

    def split_halves(self, grads):
        other = []
        for name, g in grads.items():
            sm = _shard_major(name, g)
            h = sm.shape[1] // 2
            self.mine[name] = lax.dynamic_slice_in_dim(sm, self.core * h, h, axis=1)
            other.append(lax.dynamic_slice_in_dim(sm, (1 - self.core) * h, h, axis=1))
        return other

    def chip_partials(self, names, from_sibling):
        for name, b in zip(names, from_sibling):
            a = self.mine[name]
            flat = lambda t: t.reshape(-1, t.shape[-1])
            self.partial[name] = _pair_sum("sum_sibling_" + name, flat(a), flat(b), BF).reshape(a.shape)
        return [self.partial[n] for n in names]


def kernel(x, c, ada_w, ada_b, pre_g, post_g, rel_bias, ab_w_in, ab_w_out, s5_a_re, s5_a_im, s5_log_dt, s5_b_re, s5_b_im, s5_c_re, s5_c_im, s5_d, s5_glu_w, s5_glu_b, gdn_w_in, gdn_conv, gdn_a_log, gdn_dt_bias, gdn_norm_g, gdn_w_out, loss_target, m_ada_w, m_ada_b, m_pre_g, m_post_g, m_rel_bias, m_ab_w_in, m_ab_w_out, m_s5_a_re, m_s5_a_im, m_s5_log_dt, m_s5_b_re, m_s5_b_im, m_s5_c_re, m_s5_c_im, m_s5_d, m_s5_glu_w, m_s5_glu_b, m_gdn_w_in, m_gdn_conv, m_gdn_a_log, m_gdn_dt_bias, m_gdn_norm_g, m_gdn_w_out, v_ada_w, v_ada_b, v_pre_g, v_post_g, v_rel_bias, v_ab_w_in, v_ab_w_out, v_s5_a_re, v_s5_a_im, v_s5_log_dt, v_s5_b_re, v_s5_b_im, v_s5_c_re, v_s5_c_im, v_s5_d, v_s5_glu_w, v_s5_glu_b, v_gdn_w_in, v_gdn_conv, v_gdn_a_log, v_gdn_dt_bias, v_gdn_norm_g, v_gdn_w_out):
    w = dict(ada_w=ada_w, ada_b=ada_b, pre_g=pre_g, post_g=post_g, rel_bias=rel_bias, ab_w_in=ab_w_in, ab_w_out=ab_w_out,
             s5_a_re=s5_a_re, s5_a_im=s5_a_im, s5_log_dt=s5_log_dt, s5_b_re=s5_b_re, s5_b_im=s5_b_im, s5_c_re=s5_c_re, s5_c_im=s5_c_im,
             s5_d=s5_d, s5_glu_w=s5_glu_w, s5_glu_b=s5_glu_b, gdn_w_in=gdn_w_in, gdn_conv=gdn_conv, gdn_a_log=gdn_a_log,
             gdn_dt_bias=gdn_dt_bias, gdn_norm_g=gdn_norm_g, gdn_w_out=gdn_w_out)
    m = dict(ada_w=m_ada_w, ada_b=m_ada_b, pre_g=m_pre_g, post_g=m_post_g, rel_bias=m_rel_bias, ab_w_in=m_ab_w_in, ab_w_out=m_ab_w_out,
             s5_a_re=m_s5_a_re, s5_a_im=m_s5_a_im, s5_log_dt=m_s5_log_dt, s5_b_re=m_s5_b_re, s5_b_im=m_s5_b_im, s5_c_re=m_s5_c_re,
             s5_c_im=m_s5_c_im, s5_d=m_s5_d, s5_glu_w=m_s5_glu_w, s5_glu_b=m_s5_glu_b, gdn_w_in=m_gdn_w_in, gdn_conv=m_gdn_conv,
             gdn_a_log=m_gdn_a_log, gdn_dt_bias=m_gdn_dt_bias, gdn_norm_g=m_gdn_norm_g, gdn_w_out=m_gdn_w_out)
    v = dict(ada_w=v_ada_w, ada_b=v_ada_b, pre_g=v_pre_g, post_g=v_post_g, rel_bias=v_rel_bias, ab_w_in=v_ab_w_in, ab_w_out=v_ab_w_out,
             s5_a_re=v_s5_a_re, s5_a_im=v_s5_a_im, s5_log_dt=v_s5_log_dt, s5_b_re=v_s5_b_re, s5_b_im=v_s5_b_im, s5_c_re=v_s5_c_re,
             s5_c_im=v_s5_c_im, s5_d=v_s5_d, s5_glu_w=v_s5_glu_w, s5_glu_b=v_s5_glu_b, gdn_w_in=v_gdn_w_in, gdn_conv=v_gdn_conv,
             gdn_a_log=v_gdn_a_log, gdn_dt_bias=v_gdn_dt_bias, gdn_norm_g=v_gdn_norm_g, gdn_w_out=v_gdn_w_out)
    ix, iy, ic = _place()
    me = 4 * ix + 2 * iy + ic
    chip = 2 * ix + iy
    n_cols = ada_w.shape[2]

    mine_first = _pack([c, gdn_conv], 32)
    first = _own_slot(_all_gather8("gather_c_conv", mine_first), mine_first, me)
    c_all = first[:, 0:8].reshape(8, D_MODEL)
    conv_full = first[0::2, 8:32].reshape(4, C_CONV, n_cols).transpose(1, 0, 2).reshape(C_CONV, 4 * n_cols)
    mine_mod = _mod_local(c_all, ada_w)
    modl = _own_slot(_all_gather8("gather_mod", mine_mod), mine_mod, me)
    mod = lax.dynamic_index_in_dim(modl[0::2], me, axis=2, keepdims=False)
    mod = mod.transpose(1, 0, 2).reshape(2, 4 * n_cols) + ada_b

    comm = _WeightExchanges({name: w[name][0] for name in _SHARDED}, ic, chip)
    wd = {name: w[name] for name in _SMALL if name != "ada_b"}
    wd = {k: (a if k in ("pre_g", "post_g", "rel_bias") else a[0]) for k, a in wd.items()}
    wd["gdn_conv"] = conv_full
    wd.update(comm.first_weights())

    loss_local, grad_x, grads, dmod = _local_step(x[0], loss_target[0], mod, wd, comm)
    loss = lax.psum(loss_local, ("x", "y", "c"))

    small_shapes = [w[name].shape for name in _SMALL] + [(C_CONV, 4 * n_cols)]
    small_rows = -(-sum(_rows128(int(np.prod(s))) for s in small_shapes) // 64) * 64
    per_dev, dmod_rows = small_rows // 8, _rows128(2 * 3 * D_MODEL)
    partial = _pack([dmod] + [grads[name] for name in _SMALL[1:]] + [grads["gdn_conv"]], small_rows)
    outbound = jnp.concatenate([partial.reshape(8, per_dev, 128), jnp.broadcast_to(partial[None, :dmod_rows], (8, dmod_rows, 128))], axis=1)
    inbound = _own_slot(_all_to_all8("reduce_small_grads", outbound), lax.dynamic_index_in_dim(outbound, me, 0, keepdims=False), me)
    my_rows = _slot_sum("sum_small_grads", inbound[:, :per_dev])
    g_small = _own_slot(_all_gather8("gather_small_grads", my_rows), my_rows, me).reshape(small_rows, 128)
    g_list = _unpack(g_small, small_shapes)
    out_g, out_d, out_m, out_v = {}, {}, {}, {}

    def update(name, g2d):
        shp = w[name].shape
        two_d = lambda a: a.reshape(-1, shp[-1])
        d_, m_, v_ = _adamw("adamw_" + name, two_d(w[name]), g2d, two_d(m[name]), two_d(v[name]))
        out_g[name], out_d[name], out_m[name], out_v[name] = (a.reshape(shp) for a in (g2d, d_, m_, v_))

    small = list(_SMALL) + ["gdn_conv"]
    small_g = [g.reshape(-1, g.shape[-1]) for g in g_list[:-1]] + [lax.dynamic_slice_in_dim(g_list[-1], chip * n_cols, n_cols, axis=1)]
    two_d = lambda a: a.reshape(-1, a.shape[-1])
    results = _adamw_many("adamw_small", [two_d(w[n]) for n in small], small_g, [two_d(m[n]) for n in small], [two_d(v[n]) for n in small])
    for name, g2d, (d_, m_, v_) in zip(small, small_g, results):
        out_g[name], out_d[name], out_m[name], out_v[name] = (a.reshape(w[name].shape) for a in (g2d, d_, m_, v_))

    dmod_all = inbound[:, per_dev:].reshape(8, 2, 4, n_cols)
    dmod_cols = lax.dynamic_index_in_dim(dmod_all, chip, axis=2, keepdims=False).transpose(1, 0, 2)
    update("ada_w", _ada_w_grad(c_all, dmod_cols).reshape(-1, n_cols))

    from_sibling = _sibling_exchange("reduce_sibling", comm.split_halves({"ab_w_in": grads["ab_w_in"]}))
    comm.received["ab_w_in"] = _chip_exchange("reduce_chips", comm.chip_partials(["ab_w_in"], from_sibling), True)[0]
    chip_1 = jnp.reshape(chip, (1,)).astype(jnp.int32)
    core_1 = jnp.reshape(ic, (1,)).astype(jnp.int32)
    reduced = [_chip_sum("sum_chips_" + name, comm.received[name], comm.partial[name], chip_1) for name in _SHARDED]
    for name, g_mine, g_sib in zip(_SHARDED, reduced, _sibling_exchange("reduce_share", reduced)):
        out_g[name], out_d[name], out_m[name], out_v[name] = _adamw_halves(
            "adamw_" + name, w[name], g_mine, g_sib, m[name], v[name], core_1)

    return (loss, grad_x[None], *[out_g[n] for n in _WEIGHTS], *[out_d[n] for n in _WEIGHTS],
            *[out_m[n] for n in _WEIGHTS], *[out_v[n] for n in _WEIGHTS])
```

```python
import functools
import math

import numpy as np
import jax
import jax.numpy as jnp
from jax import lax
from jax.experimental import pallas as pl
from jax.experimental.pallas import tpu as pltpu

F32 = jnp.float32
BF = jnp.bfloat16
HI = lax.Precision.HIGHEST
MESH = pl.DeviceIdType.MESH

D_MODEL = 1024
EPS = 1e-6
A_HEADS, A_HD, A_WIDTH, A_BLOCK = 8, 64, 512, 128
DILATIONS = (1, 4, 16)
N_KEYS = 128
REL_BUCKETS, REL_MAX_DIST = 32, 2048
B_WIDTH, B_GROUP, B_GROUPS, B_STATE = 512, 16, 32, 64
S5_LANES = 512
S5_TILES = 4
C_HEADS, C_DK, C_CHUNK, C_CONV = 8, 128, 64, 4
QKV = 3072
C_IN_PAD = 4224
TM = 256
VMEM_LIMIT_BYTES = 56 * 1024 * 1024
ADAM_LR, ADAM_B1, ADAM_B2, ADAM_EPS, ADAM_WD, ADAM_STEP = 0.001, 0.9, 0.999, 1e-08, 0.01, 10
NEG = float(np.finfo(np.float32).min)


def _cp(*sem):
    return pltpu.CompilerParams(dimension_semantics=sem, vmem_limit_bytes=VMEM_LIMIT_BYTES)


def _bdot(a, b):
    return jnp.dot(a.astype(BF), b.astype(BF), preferred_element_type=F32)


def _bdot_nt(a, b):
    return lax.dot_general(a.astype(BF), b.astype(BF), (((1,), (1,)), ((), ())), preferred_element_type=F32)


def _bdot_tn(a, b):
    return lax.dot_general(a.astype(BF), b.astype(BF), (((0,), (0,)), ((), ())), preferred_element_type=F32)


def _hdot(a, b):
    return jnp.dot(a, b, precision=HI, preferred_element_type=F32)


def _bein(eq, a, b):
    return jnp.einsum(eq, a.astype(BF), b.astype(BF), preferred_element_type=F32)


def _row(tm, n):
    return pl.BlockSpec((tm, n), lambda i: (i, 0))


def _fix(shape):
    return pl.BlockSpec(shape, lambda i: (0,) * len(shape))


def _sds(*shape, dtype=F32):
    return jax.ShapeDtypeStruct(shape, dtype)


def _acc(ref, val):
    ref[...] += val


def _zero_at_first(refs, axis=0):
    @pl.when(pl.program_id(axis) == 0)
    def _():
        for r in refs:
            r[...] = jnp.zeros_like(r)


def _rms(x):
    return x * lax.rsqrt(jnp.mean(x * x, axis=-1, keepdims=True) + EPS)


def _pre_mod(x, g, scale, shift):
    return (_rms(x) * g) * (1.0 + scale) + shift


def _post_res(y, x, post_g, gate):
    return x + gate * (_rms(y) * post_g)


def _merge_gate(o1, o2, o3, l1, l2, l3, ga):
    m = jnp.maximum(jnp.maximum(l1, l2), l3)
    e1, e2, e3 = jnp.exp(l1 - m), jnp.exp(l2 - m), jnp.exp(l3 - m)
    inv = 1.0 / (e1 + e2 + e3)
    return ((e1 * inv) * o1 + (e2 * inv) * o2 + (e3 * inv) * o3) * jax.nn.silu(ga)


def _s5_gelu(ypre, u, d_skip):
    return jax.nn.gelu(ypre + d_skip * u)


def _s5_glu(yb, gl, gb):
    return yb * jax.nn.sigmoid(gl) * jax.nn.silu(gb)


def _l0_front(x, pre_g, scale, shift, w_in):
    s_len = x.shape[0]

    def body(x_ref, g_ref, sc_ref, sh_ref, w_ref, *out_refs):
        qkv_refs, (u_ref, ga_ref, gb_ref, h_ref) = out_refs[:9], out_refs[9:]
        hb = _pre_mod(x_ref[...], g_ref[...], sc_ref[...], sh_ref[...]).astype(BF)
        h_ref[...] = hb
        z = jnp.dot(hb, w_ref[...], preferred_element_type=F32)
        for a in range(3):
            piece = z[:, a * 512:(a + 1) * 512]
            for i, d in enumerate(DILATIONS):
                qkv_refs[3 * a + i][...] = _to_res(piece, d).astype(BF)
        u_ref[...] = z[:, 1536:2048]
        ga_ref[...] = z[:, 2048:2560]
        gb_ref[...] = z[:, 2560:3072]

    vec = _fix((1, D_MODEL))
    return pl.pallas_call(
        body, name="l0_front", grid=(s_len // TM,),
        in_specs=[_row(TM, D_MODEL), vec, vec, vec, _fix((D_MODEL, 3072))],
        out_specs=[_res_spec(d) for d in DILATIONS] * 3 + [_row(TM, 512)] * 3 + [_row(TM, D_MODEL)],
        out_shape=[_sds(*_res_shape(s_len, d), dtype=BF) for d in DILATIONS] * 3 + [_sds(s_len, 512)] * 3 + [_sds(s_len, D_MODEL, dtype=BF)],
        compiler_params=_cp("arbitrary"),
    )(x, pre_g, scale, shift, w_in)


def _front_bwd(name, x, pre_g, scale, shift, w_in, dres, parts, widths):
    s_len = x.shape[0]
    n_in = sum(len(p) for p in parts)
    n_cols = sum(widths)

    def body(*refs):
        x_ref, g_ref, sc_ref, sh_ref, w_ref, dres_ref = refs[:6]
        part_refs = refs[6:6 + n_in]
        dz_ref, dx_ref, dg_ref, dsc_ref, dsh_ref = refs[6 + n_in:]
        _zero_at_first([dg_ref, dsc_ref, dsh_ref])
        _, vjp = jax.vjp(_pre_mod, x_ref[...], g_ref[...], sc_ref[...], sh_ref[...])
        dh = jnp.zeros((TM, D_MODEL), F32)
        col, at = 0, 0
        for grp, width in zip(parts, widths):
            tile = lambda r: _from_res(r[...]) if len(r.shape) == 3 else r[...]
            dz = tile(part_refs[at])
            for r in part_refs[at + 1:at + len(grp)]:
                dz = dz + tile(r)
            at += len(grp)
            dzb = dz.astype(BF)
            dz_ref[:, col:col + width] = dzb
            dh = dh + lax.dot_general(dzb, w_ref[:, col:col + width], (((1,), (1,)), ((), ())), preferred_element_type=F32)
            col += width
        dx, dg, dsc, dsh = vjp(dh)
        dx_ref[...] = dx + dres_ref[...]
        _acc(dg_ref, dg)
        _acc(dsc_ref, dsc)
        _acc(dsh_ref, dsh)

    vec = _fix((1, D_MODEL))
    flat = [a for p in parts for a in p]
    return pl.pallas_call(
        body, name=name, grid=(s_len // TM,),
        in_specs=[_row(TM, D_MODEL), vec, vec, vec, _fix((D_MODEL, n_cols)), _row(TM, D_MODEL)]
        + [_res_spec(a.shape[0], a.shape[2]) if a.ndim == 3 else _row(TM, a.shape[1]) for a in flat],
        out_specs=[_row(TM, n_cols), _row(TM, D_MODEL), vec, vec, vec],
        out_shape=[_sds(s_len, n_cols, dtype=BF), _sds(s_len, D_MODEL), _sds(1, D_MODEL), _sds(1, D_MODEL), _sds(1, D_MODEL)],
        compiler_params=_cp("arbitrary"),
    )(x, pre_g, scale, shift, w_in, dres, *flat)


def _matmul_tn(name, a, b, tn):
    s_len, k_dim = a.shape
    n_dim = b.shape[1]
    ts = 512

    def body(a_ref, b_ref, o_ref):
        _zero_at_first([o_ref], axis=1)
        o_ref[...] += lax.dot_general(a_ref[...], b_ref[...], (((0,), (0,)), ((), ())), preferred_element_type=F32)

    return pl.pallas_call(
        body, name=name, grid=(n_dim // tn, s_len // ts),
        in_specs=[pl.BlockSpec((ts, k_dim), lambda j, i: (i, 0)), pl.BlockSpec((ts, tn), lambda j, i: (i, j))],
        out_specs=pl.BlockSpec((k_dim, tn), lambda j, i: (0, j)),
        out_shape=_sds(k_dim, n_dim),
        compiler_params=_cp("arbitrary", "arbitrary"),
    )(a, b)


def _t5_bucket_np(dist):
    dist = np.maximum(dist, 0)
    max_exact = REL_BUCKETS // 2
    large = max_exact + (np.log(np.maximum(dist, 1) / max_exact)
                         / math.log(REL_MAX_DIST / max_exact) * (REL_BUCKETS - max_exact)).astype(np.int32)
    large = np.minimum(large, REL_BUCKETS - 1)
    return np.where(dist < max_exact, dist, large).astype(np.int32)


def _to_res(z, dil):
    if dil == 1:
        return z[None]
    return jnp.swapaxes(z.reshape(z.shape[0] // dil, dil, z.shape[1]), 0, 1)


def _from_res(z):
    if z.shape[0] == 1:
        return z[0]
    return jnp.swapaxes(z, 0, 1).reshape(z.shape[0] * z.shape[1], z.shape[2])


def _res_shape(s_len, dil, width=A_WIDTH):
    return (dil, s_len // dil, width)


def _res_spec(dil, width=A_WIDTH):
    return pl.BlockSpec((dil, TM // dil, width), lambda i: (0, i, 0))


def _bucket_table():
    qi = np.arange(A_BLOCK)[:, None]
    kj = np.arange(2 * A_BLOCK)[None, :]
    return np.stack([_t5_bucket_np((qi + A_BLOCK - kj) * d) for d in DILATIONS], 0)


def _attn_mask(first):
    qi = lax.broadcasted_iota(jnp.int32, (A_BLOCK, 2 * A_BLOCK), 0)
    kj = lax.broadcasted_iota(jnp.int32, (A_BLOCK, 2 * A_BLOCK), 1)
    rel = qi + A_BLOCK - kj
    return (rel >= 0) & (rel <= N_KEYS) & (jnp.logical_not(first) | (kj >= A_BLOCK))


def _attn_specs(nb, rev):
    per = 2 if nb % 2 == 0 else 1
    steps = nb // per
    n_of = (lambda i: steps - 1 - i) if rev else (lambda i: i)
    cur = pl.BlockSpec((None, per * A_BLOCK, A_WIDTH), lambda r, i: (r, n_of(i), 0))
    prev = pl.BlockSpec((None, A_BLOCK, A_WIDTH), lambda r, i: (r, jnp.maximum(per * n_of(i) - 1, 0), 0))
    bias = pl.BlockSpec((2, A_HEADS, A_BLOCK, 2 * A_BLOCK), lambda r, i: (0, 0, 0, 0))
    return per, steps, cur, prev, bias


def _attn_fwd(q, k, v, bias, exchange=None):
    dil, t_len, _ = q.shape
    per, steps, cur, prev, bias_spec = _attn_specs(t_len // A_BLOCK, False)
    scale = A_HD ** -0.5

    def body(q_ref, kp_ref, kc_ref, vp_ref, vc_ref, b_ref, o_ref, l_ref):
        lane = lax.broadcasted_iota(jnp.int32, (1, 128), 1)
        for sub in range(per):
            rows = slice(sub * A_BLOCK, (sub + 1) * A_BLOCK)
            before = slice((sub - 1) * A_BLOCK, sub * A_BLOCK)
            which = (pl.program_id(1) == 0).astype(jnp.int32) if sub == 0 else 0
            for hp in range(A_HEADS // 2):
                sl = slice(hp * 128, (hp + 1) * 128)
                qp = q_ref[rows, sl]
                kw = jnp.concatenate([kp_ref[:, sl] if sub == 0 else kc_ref[before, sl], kc_ref[rows, sl]], axis=0).astype(BF)
                vw = jnp.concatenate([vp_ref[:, sl] if sub == 0 else vc_ref[before, sl], vc_ref[rows, sl]], axis=0).astype(BF)
                outs, lses = [], []
                for j in range(2):
                    hm = (lane < 64) if j == 0 else (lane >= 64)
                    s = _bdot_nt(jnp.where(hm, qp, 0.0), kw) * scale + b_ref[which, 2 * hp + j]
                    m = jnp.max(s, axis=-1, keepdims=True)
                    p = jnp.exp(s - m)
                    den = jnp.sum(p, axis=-1, keepdims=True)
                    outs.append(_bdot(p, vw) / den)
                    lses.append(m + jnp.log(den))
                hm0 = lane < 64
                o_ref[rows, sl] = jnp.where(hm0, outs[0], outs[1])
                l_ref[rows, sl] = jnp.where(hm0, lses[0], lses[1])

    return _call_with_exchange(body, f"attn_fwd_d{dil}", (dil, steps), [cur, prev, cur, prev, cur, bias_spec], [cur, cur],
                               [_sds(dil, t_len, A_WIDTH)] * 2, [], (q, k, k, v, v, bias), exchange)


def _attn_bwd(q, k, v, bias, o, l, do, dl, exchange=None):
    dil, t_len, _ = q.shape
    per, steps, cur, prev, bias_spec = _attn_specs(t_len // A_BLOCK, True)
    scale = A_HD ** -0.5

    def body(q_ref, kp_ref, kc_ref, vp_ref, vc_ref, b_ref, o_ref, l_ref, do_ref, dl_ref,
             dq_ref, dk_ref, dv_ref, db_ref, ck_ref, cv_ref):
        _zero_at_first([ck_ref, cv_ref], axis=1)

        @pl.when((pl.program_id(0) == 0) & (pl.program_id(1) == 0))
        def _():
            db_ref[...] = jnp.zeros_like(db_ref)

        lane = lax.broadcasted_iota(jnp.int32, (1, 128), 1)
        for hp in range(A_HEADS // 2):
            sl = slice(hp * 128, (hp + 1) * 128)
            to_prev_k, to_prev_v = ck_ref[:, sl], cv_ref[:, sl]
            for sub in range(per - 1, -1, -1):
                rows = slice(sub * A_BLOCK, (sub + 1) * A_BLOCK)
                before = slice((sub - 1) * A_BLOCK, sub * A_BLOCK)
                which = (pl.program_id(1) == steps - 1).astype(jnp.int32) if sub == 0 else 0
                qp = q_ref[rows, sl]
                kw = jnp.concatenate([kp_ref[:, sl] if sub == 0 else kc_ref[before, sl], kc_ref[rows, sl]], axis=0).astype(BF)
                vw = jnp.concatenate([vp_ref[:, sl] if sub == 0 else vc_ref[before, sl], vc_ref[rows, sl]], axis=0).astype(BF)
                op, lp, dop, dlp = o_ref[rows, sl], l_ref[rows, sl], do_ref[rows, sl], dl_ref[rows, sl]
                dq_acc = jnp.zeros((A_BLOCK, 128), F32)
                dk_acc = jnp.zeros((2 * A_BLOCK, 128), F32)
                dv_acc = jnp.zeros((2 * A_BLOCK, 128), F32)
                for j in range(2):
                    hm = (lane < 64) if j == 0 else (lane >= 64)
                    qm = jnp.where(hm, qp, 0.0)
                    s = _bdot_nt(qm, kw) * scale + b_ref[which, 2 * hp + j]
                    lse = jnp.max(jnp.where(hm, lp, NEG), axis=-1, keepdims=True)
                    p = jnp.exp(s - lse)
                    do_h = jnp.where(hm, dop, 0.0)
                    dd = jnp.sum(do_h * op, axis=-1, keepdims=True)
                    dlse = jnp.sum(jnp.where(hm, dlp, 0.0), axis=-1, keepdims=True)
                    ds = p * (_bdot_nt(do_h, vw) - dd + dlse)
                    dv_acc = dv_acc + _bdot_tn(p, do_h)
                    dq_acc = dq_acc + jnp.where(hm, _bdot(ds, kw), 0.0) * scale
                    dk_acc = dk_acc + _bdot_tn(ds, qm) * scale
                    db_ref[2 * hp + j] += ds
                dq_ref[rows, sl] = dq_acc
                dk_ref[rows, sl] = dk_acc[A_BLOCK:] + to_prev_k
                dv_ref[rows, sl] = dv_acc[A_BLOCK:] + to_prev_v
                to_prev_k, to_prev_v = dk_acc[:A_BLOCK], dv_acc[:A_BLOCK]
            ck_ref[:, sl] = to_prev_k
            cv_ref[:, sl] = to_prev_v

    return _call_with_exchange(
        body, f"attn_bwd_d{dil}", (dil, steps), [cur, prev, cur, prev, cur, bias_spec, cur, cur, cur, cur],
        [cur, cur, cur, pl.BlockSpec((A_HEADS, A_BLOCK, 2 * A_BLOCK), lambda r, i: (0, 0, 0))],
        [_sds(dil, t_len, A_WIDTH)] * 3 + [_sds(A_HEADS, A_BLOCK, 2 * A_BLOCK)],
        [pltpu.VMEM((A_BLOCK, A_WIDTH), F32)] * 2, (q, k, k, v, v, bias, o, l, do, dl), exchange)


def _attn_bias(rel_bias, table):
    def body(rb_ref, t_ref, *o_refs):
        for c in range(3):
            t = t_ref[c]
            acc = [jnp.zeros((A_BLOCK, 2 * A_BLOCK), F32) for _ in range(A_HEADS)]
            for b in range(REL_BUCKETS):
                hit = t == b
                acc = [jnp.where(hit, rb_ref[b, h], acc[h]) for h in range(A_HEADS)]
            for h in range(A_HEADS):
                o_refs[c][0, h] = jnp.where(_attn_mask(False), acc[h], NEG)
                o_refs[c][1, h] = jnp.where(_attn_mask(True), acc[h], NEG)

    return pl.pallas_call(body, name="attn_bias", out_shape=[_sds(2, A_HEADS, A_BLOCK, 2 * A_BLOCK)] * 3,
                          in_specs=[pl.BlockSpec(memory_space=pltpu.SMEM), pl.BlockSpec(memory_space=pltpu.VMEM)],
                          compiler_params=pltpu.CompilerParams(vmem_limit_bytes=VMEM_LIMIT_BYTES))(rel_bias, table)


def _rel_bias_grad(dbs, idx_rows):
    n = A_BLOCK * 2 * A_BLOCK

    def body(d0_ref, d1_ref, d2_ref, idx_ref, o_ref):
        bucket = lax.broadcasted_iota(jnp.int32, (REL_BUCKETS, n), 0).astype(F32)
        acc = jnp.zeros((A_HEADS, REL_BUCKETS), F32)
        for c, db_ref in enumerate((d0_ref, d1_ref, d2_ref)):
            onehot = (idx_ref[c:c + 1, :] == bucket).astype(F32)
            acc = acc + lax.dot_general(db_ref[...], onehot, (((1,), (1,)), ((), ())), precision=HI, preferred_element_type=F32)
        o_ref[...] = acc

    return pl.pallas_call(body, name="rel_bias_grad", out_shape=_sds(A_HEADS, REL_BUCKETS),
                          compiler_params=pltpu.CompilerParams(vmem_limit_bytes=VMEM_LIMIT_BYTES))(
                              *[d.reshape(A_HEADS, n) for d in dbs], idx_rows)


def _s5_param_fn(a_re, a_im, log_dt, bt_re, bt_im):
    dt = jnp.exp(log_dt)
    mag = jnp.exp(dt * a_re)
    abar_r, abar_i = mag * jnp.cos(dt * a_im), mag * jnp.sin(dt * a_im)
    den = a_re * a_re + a_im * a_im
    fr = ((abar_r - 1.0) * a_re + abar_i * a_im) / den
    fi = (abar_i * a_re - (abar_r - 1.0) * a_im) / den
    row = lax.broadcasted_iota(jnp.int32, (B_WIDTH, B_GROUPS), 0)
    grp = lax.broadcasted_iota(jnp.int32, (B_WIDTH, B_GROUPS), 1)
    expand = ((row // B_GROUP) == grp).astype(F32)
    fr_e, fi_e = _hdot(expand, fr), _hdot(expand, fi)
    return abar_r, abar_i, fr_e * bt_re - fi_e * bt_im, fr_e * bt_im + fi_e * bt_re


def _s5_params(a_re, a_im, log_dt, bt_re, bt_im):
    def body(ar, ai, ld, br, bi, o1, o2, o3, o4):
        o1[...], o2[...], o3[...], o4[...] = _s5_param_fn(ar[...], ai[...], ld[...], br[...], bi[...])

    return pl.pallas_call(body, name="s5_params",
                          out_shape=[_sds(B_GROUPS, B_STATE)] * 2 + [_sds(B_WIDTH, B_STATE)] * 2)(a_re, a_im, log_dt, bt_re, bt_im)


def _s5_params_bwd(a_re, a_im, log_dt, bt_re, bt_im, d1, d2, d3, d4):
    def body(ar, ai, ld, br, bi, c1, c2, c3, c4, o1, o2, o3, o4, o5):
        _, vjp = jax.vjp(_s5_param_fn, ar[...], ai[...], ld[...], br[...], bi[...])
        o1[...], o2[...], o3[...], o4[...], o5[...] = vjp((c1[...], c2[...], c3[...], c4[...]))

    return pl.pallas_call(body, name="s5_params_bwd",
                          out_shape=[_sds(B_GROUPS, B_STATE)] * 2 + [_sds(B_GROUPS, 1)] + [_sds(B_WIDTH, B_STATE)] * 2,
                          )(a_re, a_im, log_dt, bt_re, bt_im, d1, d2, d3, d4)


def _pick_row(x, r):
    rows = lax.broadcasted_iota(jnp.int32, x.shape, 0)
    return jnp.sum(jnp.where(rows == r, x, 0.0), axis=0, keepdims=True)


S5_SEG = 8
S5_STEPS = 32
S5_WIDTH = S5_TILES * S5_LANES


def _seg_rows(block):
    return jnp.swapaxes(block, 0, 1).reshape(block.shape[1] * S5_SEG, block.shape[2])


def _seg_block(rows):
    return jnp.swapaxes(rows.reshape(rows.shape[0] // S5_SEG, S5_SEG, rows.shape[1]), 0, 1)


def _seq_specs(n_i, rev):
    at = (lambda i: n_i - 1 - i) if rev else (lambda i: i)
    seg = pl.BlockSpec((S5_SEG, S5_STEPS, B_WIDTH), lambda i: (0, at(i), 0))
    x_spec = pl.BlockSpec((S5_SEG * S5_STEPS, S5_WIDTH), lambda i: (at(i), 0))
    return seg, x_spec, _fix((S5_TILES, 128, S5_LANES)), _fix((S5_TILES, S5_LANES, 128)), _fix((1, S5_WIDTH)), _fix((S5_SEG, S5_WIDTH))


def _tile_dots(dot, lhs, w_ref, lhs_width):
    return jnp.concatenate([dot(lhs[:, t * lhs_width:(t + 1) * lhs_width], w_ref[t]) for t in range(S5_TILES)], axis=1)


def _s5_entries(name, end_r, end_i, abr, abi, steps, reverse):
    def body(er_ref, ei_ref, ar_ref, ai_ref, or_ref, oi_ref):
        pr, pi_ = ar_ref[...], ai_ref[...]
        for _ in range(int(math.log2(steps))):
            pr, pi_ = pr * pr - pi_ * pi_, 2.0 * pr * pi_
        er, ei = er_ref[...], ei_ref[...]
        rows = lax.broadcasted_iota(jnp.int32, er.shape, 0)
        cr, ci = jnp.zeros_like(pr), jnp.zeros_like(pr)
        out_r, out_i = jnp.zeros_like(er), jnp.zeros_like(er)
        for g in (range(S5_SEG - 2, -1, -1) if reverse else range(1, S5_SEG)):
            src = g + 1 if reverse else g - 1
            cr, ci = _pick_row(er, src) + pr * cr - pi_ * ci, _pick_row(ei, src) + pr * ci + pi_ * cr
            out_r, out_i = jnp.where(rows == g, cr, out_r), jnp.where(rows == g, ci, out_i)
        or_ref[...] = out_r
        oi_ref[...] = out_i

    return pl.pallas_call(body, name=name, out_shape=[_sds(*end_r.shape)] * 2)(end_r, end_i, abr, abi)


def _s5_seq_fwd(u, btr, bti, ctr, cti, abr, abi, entry, store, exchange=None):
    s_len = u.shape[0]
    seg_len = s_len // S5_SEG
    n_i = seg_len // S5_STEPS
    rows = S5_SEG * S5_STEPS

    def body(u_ref, btr_ref, bti_ref, ctr_ref, cti_ref, ar_ref, ai_ref, er_ref, ei_ref, *rest):
        if store:
            xr_ref, xi_ref, y_ref, endr_ref, endi_ref, sr_ref, si_ref = rest
        else:
            endr_ref, endi_ref, sr_ref, si_ref = rest
        i = pl.program_id(0)

        @pl.when(i == 0)
        def _():
            sr_ref[...] = er_ref[...]
            si_ref[...] = ei_ref[...]

        ar = jnp.broadcast_to(ar_ref[...], (S5_SEG, S5_WIDTH))
        ai = jnp.broadcast_to(ai_ref[...], (S5_SEG, S5_WIDTH))
        ub = _seg_rows(u_ref[...])
        br, bi = _tile_dots(_bdot, ub, btr_ref, 128), _tile_dots(_bdot, ub, bti_ref, 128)
        sr, si = sr_ref[...], si_ref[...]
        for s in range(S5_STEPS):
            at = slice(S5_SEG * s, S5_SEG * (s + 1))
            sr, si = ar * sr - ai * si + br[at], ar * si + ai * sr + bi[at]
            if store:
                xr_ref[at, :] = sr
                xi_ref[at, :] = si
        sr_ref[...] = sr
        si_ref[...] = si
        if store:
            y_ref[...] = _seg_block(_tile_dots(_bdot, xr_ref[...], ctr_ref, S5_LANES) - _tile_dots(_bdot, xi_ref[...], cti_ref, S5_LANES))

        @pl.when(i == n_i - 1)
        def _():
            endr_ref[...] = sr
            endi_ref[...] = si

    seg, x_spec, b_spec, c_spec, a_spec, e_spec = _seq_specs(n_i, False)
    ends = [_sds(S5_SEG, S5_WIDTH)] * 2
    full = [_sds(s_len, S5_WIDTH)] * 2 + [_sds(S5_SEG, seg_len, B_WIDTH)] if store else []
    return _call_with_exchange(
        body, "s5_scan_fwd" if store else "s5_ends_fwd", (n_i,),
        [seg, b_spec, b_spec, c_spec, c_spec, a_spec, a_spec, e_spec, e_spec],
        ([x_spec, x_spec, seg] if store else []) + [e_spec, e_spec], full + ends,
        [pltpu.VMEM((S5_SEG, S5_WIDTH), F32)] * 2,
        (u.reshape(S5_SEG, seg_len, B_WIDTH), btr, bti, ctr, cti, abr, abi, *entry), exchange)


def _s5_seq_bwd(dy, xr, xi, u, btr, bti, ctr, cti, abr, abi, g_entry, x_entry, full, exchange=None):
    s_len = dy.shape[0]
    seg_len = s_len // S5_SEG
    n_i = seg_len // S5_STEPS
    rows = S5_SEG * S5_STEPS

    def body(*refs):
        if full:
            (dy_ref, btr_ref, bti_ref, ctr_ref, cti_ref, ar_ref, ai_ref, ger_ref, gei_ref,
             xr_ref, xi_ref, xrp_ref, xip_ref, xer_ref, xei_ref, u_ref,
             du_ref, dbtr_ref, dbti_ref, dctr_ref, dcti_ref, dar_ref, dai_ref, str_ref, sti_ref,
             sr_ref, si_ref, gr_s, gi_s) = refs
        else:
            (dy_ref, btr_ref, bti_ref, ctr_ref, cti_ref, ar_ref, ai_ref, ger_ref, gei_ref, str_ref, sti_ref, sr_ref, si_ref) = refs
        i = pl.program_id(0)

        @pl.when(i == 0)
        def _():
            sr_ref[...] = ger_ref[...]
            si_ref[...] = gei_ref[...]
            if full:
                for r in (dbtr_ref, dbti_ref, dctr_ref, dcti_ref, dar_ref, dai_ref):
                    r[...] = jnp.zeros_like(r)

        ar = jnp.broadcast_to(ar_ref[...], (S5_SEG, S5_WIDTH))
        ai = -jnp.broadcast_to(ai_ref[...], (S5_SEG, S5_WIDTH))
        dyb = _seg_rows(dy_ref[...])
        gr, gi = _tile_dots(_bdot_nt, dyb, ctr_ref, 128), -_tile_dots(_bdot_nt, dyb, cti_ref, 128)
        sr, si = sr_ref[...], si_ref[...]
        for s in range(S5_STEPS - 1, -1, -1):
            at = slice(S5_SEG * s, S5_SEG * (s + 1))
            sr, si = ar * sr - ai * si + gr[at], ar * si + ai * sr + gi[at]
            if full:
                gr_s[at, :] = sr
                gi_s[at, :] = si
        sr_ref[...] = sr
        si_ref[...] = si

        @pl.when(i == n_i - 1)
        def _():
            str_ref[...] = sr
            sti_ref[...] = si

        if full:
            g_r, g_i = gr_s[...], gi_s[...]
            du_ref[...] = _seg_block(_tile_dots(_bdot_nt, g_r, btr_ref, S5_LANES) + _tile_dots(_bdot_nt, g_i, bti_ref, S5_LANES))
            ub = _seg_rows(u_ref[...])
            xr_b, xi_b = xr_ref[...], xi_ref[...]
            for t in range(S5_TILES):
                lanes, cols = slice(t * S5_LANES, (t + 1) * S5_LANES), slice(t * 128, (t + 1) * 128)
                dbtr_ref[t] += _bdot_tn(ub[:, cols], g_r[:, lanes])
                dbti_ref[t] += _bdot_tn(ub[:, cols], g_i[:, lanes])
                dctr_ref[t] += _bdot_tn(xr_b[:, lanes], dyb[:, cols])
                dcti_ref[t] -= _bdot_tn(xi_b[:, lanes], dyb[:, cols])
            first = i == n_i - 1
            xpr = jnp.concatenate([jnp.where(first, xer_ref[...], xrp_ref[...]), xr_b[:rows - S5_SEG]], axis=0)
            xpi = jnp.concatenate([jnp.where(first, xei_ref[...], xip_ref[...]), xi_b[:rows - S5_SEG]], axis=0)
            dar_ref[...] += jnp.sum(g_r * xpr + g_i * xpi, axis=0, keepdims=True)
            dai_ref[...] += jnp.sum(g_i * xpr - g_r * xpi, axis=0, keepdims=True)

    seg, x_spec, b_spec, c_spec, a_spec, e_spec = _seq_specs(n_i, True)
    halo = pl.BlockSpec((S5_SEG, S5_WIDTH), lambda i: (jnp.maximum((n_i - 1 - i) * S5_STEPS - 1, 0), 0))
    starts = [_sds(S5_SEG, S5_WIDTH)] * 2
    in_specs = [seg, b_spec, b_spec, c_spec, c_spec, a_spec, a_spec, e_spec, e_spec]
    args = [dy.reshape(S5_SEG, seg_len, B_WIDTH), btr, bti, ctr, cti, abr, abi, *g_entry]
    state = [pltpu.VMEM((S5_SEG, S5_WIDTH), F32)] * 2
    if not full:
        return _call_with_exchange(body, "s5_starts_bwd", (n_i,), in_specs, [e_spec, e_spec], starts, state, args, None)
    return _call_with_exchange(
        body, "s5_scan_bwd", (n_i,),
        in_specs + [x_spec, x_spec, halo, halo, e_spec, e_spec, seg],
        [seg, b_spec, b_spec, c_spec, c_spec, a_spec, a_spec, e_spec, e_spec],
        [_sds(S5_SEG, seg_len, B_WIDTH)] + [_sds(S5_TILES, 128, S5_LANES)] * 2 + [_sds(S5_TILES, S5_LANES, 128)] * 2
        + [_sds(1, S5_WIDTH)] * 2 + starts,
        state + [pltpu.VMEM((rows, S5_WIDTH), F32)] * 2,
        args + [xr, xi, xr, xi, *x_entry, u.reshape(S5_SEG, seg_len, B_WIDTH)], exchange)


def _blockdiag_b(bbar_t):
    blocks = bbar_t.reshape(S5_TILES, 8, B_GROUP, B_STATE)
    return jnp.einsum('jgmp,gh->jgmhp', blocks, jnp.eye(8, dtype=F32)).reshape(S5_TILES, 128, S5_LANES)


def _blockdiag_b_t(d):
    return jnp.einsum('jgmgp->jgmp', d.reshape(S5_TILES, 8, B_GROUP, 8, B_STATE)).reshape(B_WIDTH, B_STATE)


def _blockdiag_c(c):
    blocks = c.reshape(S5_TILES, 8, B_GROUP, B_STATE)
    return jnp.einsum('jgmp,gh->jhpgm', blocks, jnp.eye(8, dtype=F32)).reshape(S5_TILES, S5_LANES, 128)


def _blockdiag_c_t(d):
    return jnp.einsum('jgpgm->jgmp', d.reshape(S5_TILES, 8, B_STATE, 8, B_GROUP)).reshape(B_GROUPS, B_GROUP, B_STATE)


def _l0_out(os, ls, ga, gb, ypre, u, x, d_skip, glu_w, glu_b, w_out, post_g, gate, exchange=None):
    s_len = x.shape[0]

    def body(o0, o1, o2, l0, l1, l2, ga_ref, gb_ref, yp_ref, u_ref, x_ref, d_ref, gw_ref, gbias_ref, w_ref, pg_ref, gt_ref, x1_ref, y_ref):
        oa = _merge_gate(*[_from_res(r[...]) for r in (o0, o1, o2, l0, l1, l2)], ga_ref[...])
        yb = _s5_gelu(yp_ref[...], u_ref[...], d_ref[...])
        ob = _s5_glu(yb, _bdot(yb, gw_ref[...]) + gbias_ref[...], gb_ref[...])
        y = _bdot(oa, w_ref[0:512, :]) + _bdot(ob, w_ref[512:1024, :])
        y_ref[...] = y
        x1_ref[...] = _post_res(y, x_ref[...], pg_ref[...], gt_ref[...])

    vec, half = _fix((1, D_MODEL)), _fix((1, 512))
    return _call_with_exchange(
        body, "l0_out", (s_len // TM,),
        [_res_spec(d) for d in DILATIONS] * 2 + [_row(TM, 512)] * 4
        + [_row(TM, D_MODEL), half, _fix((512, 512)), half, _fix((D_MODEL, D_MODEL)), vec, vec],
        [_row(TM, D_MODEL)] * 2, [_sds(s_len, D_MODEL)] * 2, [],
        (*os, *ls, ga, gb, ypre, u, x, d_skip, glu_w, glu_b, w_out, post_g, gate), exchange)


def _l0_out_bwd(os, ls, ga, gb, ypre, u, x, y, d_skip, glu_w, glu_b, w_out, post_g, gate, dx1, exchange=None):
    s_len = x.shape[0]

    def body(o0, o1, o2, l0, l1, l2, ga_ref, gb_ref, yp_ref, u_ref, x_ref, y_ref, d_ref, gw_ref, gbias_ref, w_ref, pg_ref, gt_ref, dx1_ref,
             do0, do1, do2, dl0, dl1, dl2, dga_ref, dgb_ref, dyp_ref, du_ref, dd_ref, dgw_ref, dgbias_ref, dw_ref, dpg_ref, dgt_ref):
        _zero_at_first([dd_ref, dgw_ref, dgbias_ref, dw_ref, dpg_ref, dgt_ref])
        _, vjp2 = jax.vjp(_post_res, y_ref[...], x_ref[...], pg_ref[...], gt_ref[...])
        dy, _, dpg, dgt = vjp2(dx1_ref[...])
        _acc(dpg_ref, dpg)
        _acc(dgt_ref, dgt)
        oa, vjp_a = jax.vjp(_merge_gate, *[_from_res(r[...]) for r in (o0, o1, o2, l0, l1, l2)], ga_ref[...])
        yb, vjp_g = jax.vjp(_s5_gelu, yp_ref[...], u_ref[...], d_ref[...])
        gl = _bdot(yb, gw_ref[...]) + gbias_ref[...]
        ob, vjp_b = jax.vjp(_s5_glu, yb, gl, gb_ref[...])
        dw_ref[0:512, :] += _bdot_tn(oa, dy)
        dw_ref[512:1024, :] += _bdot_tn(ob, dy)
        d1, d2, d3, e1, e2, e3, dga = vjp_a(_bdot_nt(dy, w_ref[0:512, :]))
        for ref, val, d in zip((do0, do1, do2, dl0, dl1, dl2), (d1, d2, d3, e1, e2, e3), DILATIONS * 2):
            ref[...] = _to_res(val, d)
        dga_ref[...] = dga
        dyb, dgl, dgb = vjp_b(_bdot_nt(dy, w_ref[512:1024, :]))
        dgb_ref[...] = dgb
        dgw_ref[...] += _bdot_tn(yb, dgl)
        _acc(dgbias_ref, jnp.sum(dgl, axis=0, keepdims=True))
        dyp, du, dd = vjp_g(dyb + _bdot_nt(dgl, gw_ref[...]))
        dyp_ref[...] = dyp
        du_ref[...] = du
        _acc(dd_ref, dd)

    vec, half = _fix((1, D_MODEL)), _fix((1, 512))
    r5, r10 = _row(TM, 512), _row(TM, D_MODEL)
    res6 = [_res_spec(d) for d in DILATIONS] * 2
    return _call_with_exchange(
        body, "l0_out_bwd", (s_len // TM,),
        res6 + [r5] * 4 + [r10, r10, half, _fix((512, 512)), half, _fix((D_MODEL, D_MODEL)), vec, vec, r10],
        res6 + [r5] * 4 + [half, _fix((512, 512)), half, _fix((D_MODEL, D_MODEL)), vec, vec],
        [_sds(*_res_shape(s_len, d)) for d in DILATIONS] * 2 + [_sds(s_len, 512)] * 4
        + [_sds(1, 512), _sds(512, 512), _sds(1, 512), _sds(D_MODEL, D_MODEL), _sds(1, D_MODEL), _sds(1, D_MODEL)],
        [], (*os, *ls, ga, gb, ypre, u, x, y, d_skip, glu_w, glu_b, w_out, post_g, gate, dx1), exchange)


def _l1_front(x, pre_g, scale, shift, w_in):
    s_len = x.shape[0]

    def body(x_ref, g_ref, sc_ref, sh_ref, w_ref, raw_ref, gate_ref, ba_ref, h_ref):
        hb = _pre_mod(x_ref[...], g_ref[...], sc_ref[...], sh_ref[...]).astype(BF)
        h_ref[...] = hb
        z = jnp.dot(hb, w_ref[...], preferred_element_type=F32)
        raw_ref[...] = z[:, 0:QKV]
        gate_ref[...] = z[:, QKV:QKV + 1024]
        ba_ref[...] = z[:, QKV + 1024:C_IN_PAD]

    vec = _fix((1, D_MODEL))
    return pl.pallas_call(
        body, name="l1_front", grid=(s_len // TM,),
        in_specs=[_row(TM, D_MODEL), vec, vec, vec, _fix((D_MODEL, C_IN_PAD))],
        out_specs=[_row(TM, QKV), _row(TM, 1024), _row(TM, 128), _row(TM, D_MODEL)],
        out_shape=[_sds(s_len, QKV), _sds(s_len, 1024), _sds(s_len, 128), _sds(s_len, D_MODEL, dtype=BF)],
        compiler_params=_cp("arbitrary"),
    )(x, pre_g, scale, shift, w_in)


def _bg_fn(ba, alog_row, dtb_row):
    lane = lax.broadcasted_iota(jnp.int32, (1, 128), 1)
    g = -jnp.exp(alog_row) * jax.nn.softplus(ba + dtb_row)
    return jnp.where(lane < C_HEADS, jax.nn.sigmoid(ba), jnp.where(lane < 2 * C_HEADS, g, 0.0))


def _act_q(c):
    q = jax.nn.silu(c)
    return q * lax.rsqrt(jnp.sum(q * q, axis=-1, keepdims=True) + EPS) * (C_DK ** -0.5)


def _act_k(c):
    k = jax.nn.silu(c)
    return k * lax.rsqrt(jnp.sum(k * k, axis=-1, keepdims=True) + EPS)


def _act_of(s):
    return _act_q if s < 8 else (_act_k if s < 16 else jax.nn.silu)


def _conv_taps(prev8, tile_ref, sl, next8=None):
    rows = tile_ref.shape[0]
    head = jnp.concatenate([prev8, tile_ref[0:8, sl]], axis=0)
    tail = None if next8 is None else jnp.concatenate([tile_ref[rows - 8:rows, sl], next8], axis=0)
    taps = []
    for j in range(C_CONV):
        shift = C_CONV - 1 - j
        pieces = [head[8:] if shift == 0 else pltpu.roll(head, shift, 0)[8:], tile_ref[pl.ds(8 - shift, rows - 8), sl]]
        if tail is not None:
            pieces.append(tail[8:] if shift == 0 else pltpu.roll(tail, shift, 0)[8:])
        taps.append(jnp.concatenate(pieces, axis=0))
    return taps


def _gdn_prep(raw, ba, conv_w, alog_row, dtb_row):
    s_len = raw.shape[0]

    def body(raw_ref, halo_ref, ba_ref, w_ref, al_ref, dt_ref, qkv_ref, bg_ref):
        bg_ref[...] = _bg_fn(ba_ref[...], al_ref[...], dt_ref[...])
        has_prev = (pl.program_id(0) > 0).astype(F32)
        for s in range(24):
            sl = slice(s * 128, (s + 1) * 128)
            taps = _conv_taps(halo_ref[:, sl] * has_prev, raw_ref, sl)
            conv = w_ref[3:4, sl] * taps[3]
            for j in range(3):
                conv = conv + w_ref[j:j + 1, sl] * taps[j]
            qkv_ref[:, sl] = _act_of(s)(conv)

    halo = pl.BlockSpec((8, QKV), lambda i: (jnp.maximum(i * (TM // 8) - 1, 0), 0))
    row128 = _fix((1, 128))
    return pl.pallas_call(
        body, name="gdn_prep", grid=(s_len // TM,),
        in_specs=[_row(TM, QKV), halo, _row(TM, 128), _fix((C_CONV, QKV)), row128, row128],
        out_specs=[_row(TM, QKV), _row(TM, 128)],
        out_shape=[_sds(s_len, QKV), _sds(s_len, 128)],
        compiler_params=_cp("arbitrary"),
    )(raw, raw, ba, conv_w, alog_row, dtb_row)


def _gdn_prep_bwd(raw, ba, conv_w, alog_row, dtb_row, dq, dk, dv, dbg):
    s_len = raw.shape[0]
    n_tiles = s_len // TM

    def body(raw_ref, prev_ref, next_ref, ba_ref, w_ref, al_ref, dt_ref, dq_ref, dqn_ref, dk_ref, dkn_ref, dv_ref, dvn_ref, dbg_ref,
             draw_ref, dba_ref, dw_ref, dal_ref, ddt_ref, dconv_ref):
        _zero_at_first([dw_ref, dal_ref, ddt_ref])
        i = pl.program_id(0)
        _, vjp_bg = jax.vjp(_bg_fn, ba_ref[...], al_ref[...], dt_ref[...])
        dba, dal, ddt = vjp_bg(dbg_ref[...])
        dba_ref[...] = dba
        _acc(dal_ref, dal)
        _acc(ddt_ref, ddt)
        has_prev = (i > 0).astype(F32)
        has_next = (i < n_tiles - 1).astype(F32)
        ct_refs = ((dq_ref, dqn_ref), (dk_ref, dkn_ref), (dv_ref, dvn_ref))
        for s in range(24):
            sl = slice(s * 128, (s + 1) * 128)
            hl = slice((s % 8) * 128, (s % 8 + 1) * 128)
            tile_ref, nxt_ref = ct_refs[s // 8]
            taps = _conv_taps(prev_ref[:, sl] * has_prev, raw_ref, sl, next_ref[:, sl] * has_next)
            conv = w_ref[3:4, sl] * taps[3]
            for j in range(3):
                conv = conv + w_ref[j:j + 1, sl] * taps[j]
            ct = jnp.concatenate([tile_ref[:, hl], nxt_ref[:, hl] * has_next], axis=0)
            _, vjp_act = jax.vjp(_act_of(s), conv)
            dconv, = vjp_act(ct)
            dconv_ref[...] = dconv
            draw = w_ref[3:4, sl] * dconv[:TM]
            for j in range(3):
                draw = draw + w_ref[j:j + 1, sl] * dconv_ref[pl.ds(3 - j, TM), :]
            draw_ref[:, sl] = draw
            for j in range(4):
                dw_ref[j:j + 1, sl] += jnp.sum(dconv[:TM] * taps[j][:TM], axis=0, keepdims=True)

    prev = pl.BlockSpec((8, QKV), lambda i: (jnp.maximum(i * (TM // 8) - 1, 0), 0))
    nxt = lambda n: pl.BlockSpec((8, n), lambda i: (jnp.minimum((i + 1) * (TM // 8), s_len // 8 - 1), 0))
    row128 = _fix((1, 128))
    ct_specs = [_row(TM, 1024), nxt(1024)] * 3
    return pl.pallas_call(
        body, name="gdn_prep_bwd", grid=(n_tiles,),
        in_specs=[_row(TM, QKV), prev, nxt(QKV), _row(TM, 128), _fix((C_CONV, QKV)), row128, row128] + ct_specs + [_row(TM, 128)],
        out_specs=[_row(TM, QKV), _row(TM, 128), _fix((C_CONV, QKV)), row128, row128],
        out_shape=[_sds(s_len, QKV), _sds(s_len, 128), _sds(C_CONV, QKV), _sds(1, 128), _sds(1, 128)],
        scratch_shapes=[pltpu.VMEM((TM + 8, 128), F32)],
        compiler_params=_cp("arbitrary"),
    )(raw, raw, raw, ba, conv_w, alog_row, dtb_row, dq, dq, dk, dk, dv, dv, dbg)


def _tein(eq, a, b):
    return jnp.einsum(eq, a, b, precision=lax.Precision.HIGH, preferred_element_type=F32)


def _unit_lower_inverse(lower):
    ri = lax.broadcasted_iota(jnp.int32, (C_CHUNK, C_CHUNK), 0)
    ci = lax.broadcasted_iota(jnp.int32, (C_CHUNK, C_CHUNK), 1)
    eye = (ri == ci).astype(F32)[None]
    p_mat = -lower
    inv = eye + p_mat
    for _ in range(5):
        p_mat = _bein('hij,hjk->hik', p_mat, p_mat)
        inv = inv + _bein('hij,hjk->hik', inv, p_mat)
    inv = _tein('hij,hjk->hik', inv, 2.0 * eye - _tein('hij,hjk->hik', eye + lower, inv))
    return jnp.where((ri >= ci)[None], inv, 0.0)


@jax.custom_vjp
def _known_inverse(lower, inv):
    return inv


def _known_inverse_fwd(lower, inv):
    return inv, inv


def _known_inverse_bwd(inv, d_inv):
    d_lower = -_bein('hik,hjk->hij', _bein('hji,hjk->hik', inv, d_inv), inv)
    return d_lower, jnp.zeros_like(inv)


_known_inverse.defvjp(_known_inverse_fwd, _known_inverse_bwd)


def _gdn_local(q, k, v, bgs, inv_known=None):
    lane = lax.broadcasted_iota(jnp.int32, (1, 128), 1)
    ri = lax.broadcasted_iota(jnp.int32, (C_CHUNK, C_CHUNK), 0)
    ci = lax.broadcasted_iota(jnp.int32, (C_CHUNK, C_CHUNK), 1)
    row_id = lax.broadcasted_iota(jnp.int32, (128, C_CHUNK), 0)
    beta, gc, gcj = [], [], []
    for bg in bgs:
        gc_t = _hdot((ri >= ci).astype(F32), bg)
        gc_rows = gc_t.T
        for h in range(C_HEADS):
            beta.append(jnp.sum(jnp.where(lane == h, bg, 0.0), axis=-1, keepdims=True))
            gc.append(jnp.sum(jnp.where(lane == C_HEADS + h, gc_t, 0.0), axis=-1, keepdims=True))
            gcj.append(jnp.sum(jnp.where(row_id == C_HEADS + h, gc_rows, 0.0), axis=0, keepdims=True))
    beta, gc, gcj = jnp.stack(beta, axis=0), jnp.stack(gc, axis=0), jnp.stack(gcj, axis=0)
    tril, strict = (ri >= ci)[None], (ri > ci)[None]
    decay = jnp.exp(jnp.where(tril, gc - gcj, -1e30))
    kb = k * beta
    lower = jnp.where(strict, _bein('hid,hjd->hij', kb, k) * decay, 0.0)
    inv = _unit_lower_inverse(lower) if inv_known is None else _known_inverse(lower, inv_known)
    egc = jnp.exp(gc)
    u_c = _bein('hij,hjd->hid', inv, v * beta)
    w_c = _bein('hij,hjd->hid', inv, kb * egc)
    aqk = _bein('hid,hjd->hij', q, k) * decay
    rowi = lax.broadcasted_iota(jnp.int32, (1, C_CHUNK, 1), 1)
    g_last = jnp.sum(jnp.where(rowi == C_CHUNK - 1, gc, 0.0), axis=1, keepdims=True)
    kd = k * jnp.exp(g_last - gc)
    return (u_c, w_c, aqk, q * egc, kd, jnp.exp(g_last)), inv


def _gdn_state(local, state):
    u_c, w_c, aqk, qg, kd, dec = local
    v_new = u_c - _bein('hik,hkv->hiv', w_c, state)
    o = _bein('hik,hkv->hiv', qg, state) + _bein('hij,hjv->hiv', aqk, v_new)
    return o, state * dec + _bein('hik,hiv->hkv', kd, v_new)


C_SUB = 4


def _gdn_group(q, k, v, bgs, state, inv_known=None):
    local, inv = _gdn_local(q, k, v, bgs, inv_known)
    outs = []
    for s in range(len(bgs)):
        o, state = _gdn_state(tuple(t[s * C_HEADS:(s + 1) * C_HEADS] for t in local), state)
        outs.append(o)
    return outs, state, inv


def _heads(ref):
    return jnp.stack([ref[s * C_CHUNK:(s + 1) * C_CHUNK, h * C_DK:(h + 1) * C_DK] for s in range(C_SUB) for h in range(C_HEADS)], axis=0)


def _put_heads(ref, sub, val):
    rows = slice(sub * C_CHUNK, (sub + 1) * C_CHUNK)
    for h in range(C_HEADS):
        ref[rows, h * C_DK:(h + 1) * C_DK] = val[h]


def _gdn_specs(s_len, rev):
    rows = C_SUB * C_CHUNK
    n_g = s_len // rows
    at = (lambda i: n_g - 1 - i) if rev else (lambda i: i)
    col = lambda c: pl.BlockSpec((rows, 1024), lambda i: (at(i), c))
    row128 = pl.BlockSpec((rows, 128), lambda i: (at(i), 0))
    state = pl.BlockSpec((1, C_HEADS, C_DK, C_DK), lambda i: (at(i), 0, 0, 0))
    inv = pl.BlockSpec((1, C_SUB * C_HEADS, C_CHUNK, C_CHUNK), lambda i: (at(i), 0, 0, 0))
    return n_g, col, row128, state, inv


def _gdn_fwd(qkv, bg):
    s_len = qkv.shape[0]
    n_g, col, row128, state_spec, inv_spec = _gdn_specs(s_len, False)

    def body(q_ref, k_ref, v_ref, bg_ref, o_ref, ss_ref, inv_ref, st_ref):
        _zero_at_first([st_ref])
        s0 = st_ref[...]
        ss_ref[0] = s0
        bgs = [bg_ref[s * C_CHUNK:(s + 1) * C_CHUNK, :] for s in range(C_SUB)]
        outs, s2, inv = _gdn_group(_heads(q_ref), _heads(k_ref), _heads(v_ref), bgs, s0)
        st_ref[...] = s2
        inv_ref[0] = inv
        for s in range(C_SUB):
            _put_heads(o_ref, s, outs[s])

    return pl.pallas_call(
        body, name="gdn_fwd", grid=(n_g,),
        in_specs=[col(0), col(1), col(2), row128],
        out_specs=[col(0), state_spec, inv_spec],
        out_shape=[_sds(s_len, 1024), _sds(n_g, C_HEADS, C_DK, C_DK), _sds(n_g, C_SUB * C_HEADS, C_CHUNK, C_CHUNK)],
        scratch_shapes=[pltpu.VMEM((C_HEADS, C_DK, C_DK), F32)],
        compiler_params=_cp("arbitrary"),
    )(qkv, qkv, qkv, bg)


def _gdn_bwd(qkv, bg, states, invs, do):
    s_len = qkv.shape[0]
    n_g, col, row128, state_spec, inv_spec = _gdn_specs(s_len, True)

    def body(q_ref, k_ref, v_ref, bg_ref, ss_ref, inv_ref, do_ref, dq_ref, dk_ref, dv_ref, dbg_ref, ds_ref):
        _zero_at_first([ds_ref])
        inv_known = inv_ref[0]

        def group(q, k, v, bgs, st):
            outs, st2, _ = _gdn_group(q, k, v, bgs, st, inv_known)
            return outs, st2

        bgs = [bg_ref[s * C_CHUNK:(s + 1) * C_CHUNK, :] for s in range(C_SUB)]
        _, vjp = jax.vjp(group, _heads(q_ref), _heads(k_ref), _heads(v_ref), bgs, ss_ref[0])
        douts = [jnp.stack([do_ref[s * C_CHUNK:(s + 1) * C_CHUNK, h * C_DK:(h + 1) * C_DK] for h in range(C_HEADS)], axis=0)
                 for s in range(C_SUB)]
        dq, dk, dv, dbgs, ds = vjp((douts, ds_ref[...]))
        ds_ref[...] = ds
        for s in range(C_SUB):
            dbg_ref[s * C_CHUNK:(s + 1) * C_CHUNK, :] = dbgs[s]
            for ref, val in ((dq_ref, dq), (dk_ref, dk), (dv_ref, dv)):
                _put_heads(ref, s, val[s * C_HEADS:(s + 1) * C_HEADS])

    return pl.pallas_call(
        body, name="gdn_bwd", grid=(n_g,),
        in_specs=[col(0), col(1), col(2), row128, state_spec, inv_spec, col(0)],
        out_specs=[col(0), col(0), col(0), row128],
        out_shape=[_sds(s_len, 1024)] * 3 + [_sds(s_len, 128)],
        scratch_shapes=[pltpu.VMEM((C_HEADS, C_DK, C_DK), F32)],
        compiler_params=_cp("arbitrary"),
    )(qkv, qkv, qkv, bg, states, invs, do)


def _head_norm_gate(o, gate, norm_g):
    return (_rms(o) * norm_g) * jax.nn.silu(gate)


def _l1_out_fb(o, gate_c, x1, target, norm_g, w_out, post_g, gate):
    s_len = x1.shape[0]

    def body(o_ref, gc_ref, x1_ref, t_ref, ng_ref, w_ref, pg_ref, gt_ref,
             loss_ref, dres_ref, do_ref, dgc_ref, dw_ref, dng_ref, dpg_ref, dgt_ref):
        _zero_at_first([loss_ref, dw_ref, dng_ref, dpg_ref, dgt_ref])
        ng = ng_ref[...]
        ons, vjps = [], []
        for h in range(C_HEADS):
            sl = slice(h * C_DK, (h + 1) * C_DK)
            on, vjp_h = jax.vjp(_head_norm_gate, o_ref[:, sl], gc_ref[:, sl], ng)
            ons.append(on)
            vjps.append(vjp_h)
        on_all = jnp.concatenate(ons, axis=-1)
        y = _bdot(on_all, w_ref[...])
        x2, vjp2 = jax.vjp(_post_res, y, x1_ref[...], pg_ref[...], gt_ref[...])
        err = x2 - t_ref[...]
        _acc(loss_ref, jnp.full((1, 128), 0.5 * jnp.sum(jnp.mean(err * err, axis=-1)), F32))
        dx2 = err * (1.0 / D_MODEL)
        dy, _, dpg, dgt = vjp2(dx2)
        dres_ref[...] = dx2
        _acc(dpg_ref, dpg)
        _acc(dgt_ref, dgt)
        dw_ref[...] += _bdot_tn(on_all, dy)
        don = _bdot_nt(dy, w_ref[...])
        for h in range(C_HEADS):
            sl = slice(h * C_DK, (h + 1) * C_DK)
            do_h, dgc_h, dng = vjps[h](don[:, sl])
            do_ref[:, sl] = do_h
            dgc_ref[:, sl] = dgc_h
            _acc(dng_ref, dng)

    vec, r10 = _fix((1, D_MODEL)), _row(TM, D_MODEL)
    row128 = _fix((1, 128))
    return pl.pallas_call(
        body, name="l1_out_fb", grid=(s_len // TM,),
        in_specs=[r10, r10, r10, r10, row128, _fix((D_MODEL, D_MODEL)), vec, vec],
        out_specs=[row128, r10, r10, r10, _fix((D_MODEL, D_MODEL)), row128, vec, vec],
        out_shape=[_sds(1, 128), _sds(s_len, D_MODEL), _sds(s_len, D_MODEL), _sds(s_len, D_MODEL),
                   _sds(D_MODEL, D_MODEL), _sds(1, 128), _sds(1, D_MODEL), _sds(1, D_MODEL)],
        compiler_params=_cp("arbitrary"),
    )(o, gate_c, x1, target, norm_g, w_out, post_g, gate)


def _row_of(v, width, at):
    return jnp.zeros((1, width), F32).at[0, at:at + v.shape[-1]].set(v.reshape(-1))


def _local_step(x, target, mod, wd, comm=None):
    s_len = x.shape[0]
    shift0, scale0, gate0 = (mod[0:1, i * 1024:(i + 1) * 1024] for i in range(3))
    shift1, scale1, gate1 = (mod[1:2, i * 1024:(i + 1) * 1024] for i in range(3))
    pre_g0, pre_g1 = wd["pre_g"][0:1], wd["pre_g"][1:2]
    post_g0, post_g1 = wd["post_g"][0:1], wd["post_g"][1:2]
    w_in0 = wd["ab_w_in"].astype(BF)
    d_skip, glu_b = wd["s5_d"].reshape(1, 512), wd["s5_glu_b"].reshape(1, 512)
    norm_g = wd["gdn_norm_g"].reshape(1, 128)
    alog_row = _row_of(wd["gdn_a_log"], 128, C_HEADS)
    dtb_row = _row_of(wd["gdn_dt_bias"], 128, C_HEADS)
    conv_w = wd["gdn_conv"]

    a_re, a_im = wd["s5_a_re"], wd["s5_a_im"]
    log_dt = wd["s5_log_dt"].reshape(B_GROUPS, 1)
    bt_re = wd["s5_b_re"].transpose(0, 2, 1).reshape(B_WIDTH, B_STATE)
    bt_im = wd["s5_b_im"].transpose(0, 2, 1).reshape(B_WIDTH, B_STATE)
    abar_r, abar_i, bbar_r, bbar_i = _s5_params(a_re, a_im, log_dt, bt_re, bt_im)
    abr, abi = abar_r.reshape(1, -1), abar_i.reshape(1, -1)
    btr, bti = _blockdiag_b(bbar_r).astype(BF), _blockdiag_b(bbar_i).astype(BF)
    ctr, cti = _blockdiag_c(wd["s5_c_re"]).astype(BF), _blockdiag_c(wd["s5_c_im"]).astype(BF)

    table = _bucket_table()
    biases = _attn_bias(wd["rel_bias"], jnp.asarray(table))
    front = _l0_front(x, pre_g0, scale0, shift0, w_in0)
    qs, ks, vs = front[0:3], front[3:6], front[6:9]
    u, ga, gb, h0 = front[9:]
    riders = [None] * 4 if comm is None else comm.late_exchanges()
    os, ls, got = [], [], []
    for i in range(3):
        (o_d, l_d), g = _attn_fwd(qs[i], ks[i], vs[i], biases[i], exchange=riders[i])
        os.append(o_d)
        ls.append(l_d)
        got.append(g)
    seg_len = s_len // S5_SEG
    zero_state = (jnp.zeros((S5_SEG, S5_WIDTH), F32),) * 2
    ends, _ = _s5_seq_fwd(u, btr, bti, ctr, cti, abr, abi, zero_state, False)
    x_entry = _s5_entries("s5_entries_fwd", *ends, abr, abi, seg_len, False)
    (xr, xi, ypre3, _, _), g = _s5_seq_fwd(u, btr, bti, ctr, cti, abr, abi, x_entry, True, exchange=riders[3])
    got.append(g)
    ypre = ypre3.reshape(s_len, B_WIDTH)
    rider = None
    if comm is not None:
        mine0, mine1 = comm.late_halves(got)
        wd = {**wd, **comm.full_weights(["ab_w_out", "s5_glu_w"], mine0, _sibling_exchange("gather_w_sibling_l0", mine0))}
        rider = (mine1, "sibling")
    w_out0 = wd["ab_w_out"].astype(BF)
    glu_w = wd["s5_glu_w"].astype(BF)
    (x1, y0), theirs1 = _l0_out(os, ls, ga, gb, ypre, u, x, d_skip, glu_w, glu_b, w_out0, post_g0, gate0, exchange=rider)
    if comm is not None:
        wd = {**wd, **comm.full_weights(["gdn_w_in", "gdn_w_out"], mine1, theirs1)}
    w_in1 = jnp.concatenate([wd["gdn_w_in"], jnp.zeros((D_MODEL, C_IN_PAD - wd["gdn_w_in"].shape[1]), wd["gdn_w_in"].dtype)], axis=1).astype(BF)
    w_out1 = wd["gdn_w_out"].astype(BF)

    raw, gate_c, ba, h1 = _l1_front(x1, pre_g1, scale1, shift1, w_in1)
    qkv, bg = _gdn_prep(raw, ba, conv_w, alog_row, dtb_row)
    o_gdn, states, invs = _gdn_fwd(qkv, bg)
    loss_row, dres1, do_gdn, dgate_c, dw_out1, dnorm_g, dpost_g1, dgate1 = _l1_out_fb(
        o_gdn, gate_c, x1, target, norm_g, w_out1, post_g1, gate1)

    dq1, dk1, dv1, dbg = _gdn_bwd(qkv, bg, states, invs, do_gdn)
    draw, dba, dconv_w, dalog_row, ddtb_row = _gdn_prep_bwd(raw, ba, conv_w, alog_row, dtb_row, dq1, dk1, dv1, dbg)
    dz1, dx1, dpre_g1, dscale1, dshift1 = _front_bwd(
        "l1_front_bwd", x1, pre_g1, scale1, shift1, w_in1, dres1, [[draw], [dgate_c], [dba]], [QKV, 1024, 128])
    dw_in1 = _matmul_tn("l1_dw_in", h1, dz1, 1408)

    n_w1 = wd["gdn_w_in"].shape[1]
    l1_names, l0_names = ["gdn_w_in", "gdn_w_out"], ["ab_w_out", "s5_glu_w"]
    rider = None if comm is None else (comm.split_halves({"gdn_w_in": dw_in1[:, :n_w1], "gdn_w_out": dw_out1}), "sibling")
    l0b, from_sibling1 = _l0_out_bwd(os, ls, ga, gb, ypre, u, x, y0, d_skip, glu_w, glu_b, w_out0, post_g0, gate0, dx1, exchange=rider)
    dos, dls = l0b[0:3], l0b[3:6]
    dga, dgb, dypre, du_skip, dd_skip, dglu_w, dglu_b, dw_out0, dpost_g0, dgate0 = l0b[6:]
    rider = None if comm is None else (comm.split_halves({"ab_w_out": dw_out0, "s5_glu_w": dglu_w}), "sibling")
    starts, _ = _s5_seq_bwd(dypre, None, None, None, btr, bti, ctr, cti, abr, abi, zero_state, None, False)
    g_entry = _s5_entries("s5_entries_bwd", *starts, abr, -abi, seg_len, True)
    (du3, dbtr, dbti, dctr, dcti, dabr, dabi, _, _), from_sibling0 = _s5_seq_bwd(
        dypre, xr, xi, u, btr, bti, ctr, cti, abr, abi, g_entry, x_entry, True, exchange=rider)
    du_scan = du3.reshape(s_len, B_WIDTH)
    riders = [None] * 3
    if comm is not None:
        riders = [(comm.chip_partials(l1_names, from_sibling1), "scatter"), (comm.chip_partials(l0_names, from_sibling0), "scatter"), None]
    dqs, dks, dvs, dbs = [], [], [], []
    for i in range(3):
        (dq_d, dk_d, dv_d, db_d), got_d = _attn_bwd(qs[i], ks[i], vs[i], biases[i], os[i], ls[i], dos[i], dls[i], exchange=riders[i])
        dqs.append(dq_d)
        dks.append(dk_d)
        dvs.append(dv_d)
        dbs.append(db_d)
        if comm is not None and riders[i] is not None:
            comm.received.update(zip((l1_names, l0_names)[i], got_d))
    parts = [dqs, dks, dvs, [du_skip, du_scan], [dga], [dgb]]
    dz0, grad_x, dpre_g0, dscale0, dshift0 = _front_bwd(
        "l0_front_bwd", x, pre_g0, scale0, shift0, w_in0, dx1, parts, [512] * 6)
    dw_in0 = _matmul_tn("l0_dw_in", h0, dz0, 768)

    idx_rows = jnp.asarray(table.reshape(3, -1), F32)
    drel = _rel_bias_grad(dbs, idx_rows).T
    da_re, da_im, dlog_dt, dbt_re, dbt_im = _s5_params_bwd(
        a_re, a_im, log_dt, bt_re, bt_im, dabr.reshape(B_GROUPS, B_STATE), dabi.reshape(B_GROUPS, B_STATE),
        _blockdiag_b_t(dbtr), _blockdiag_b_t(dbti))
    unb = lambda d: d.reshape(B_GROUPS, B_GROUP, B_STATE).transpose(0, 2, 1)
    grads = {
        "pre_g": jnp.concatenate([dpre_g0, dpre_g1], 0), "post_g": jnp.concatenate([dpost_g0, dpost_g1], 0),
        "rel_bias": drel, "ab_w_in": dw_in0, "ab_w_out": dw_out0,
        "s5_a_re": da_re, "s5_a_im": da_im, "s5_log_dt": dlog_dt.reshape(B_GROUPS),
        "s5_b_re": unb(dbt_re), "s5_b_im": unb(dbt_im),
        "s5_c_re": _blockdiag_c_t(dctr), "s5_c_im": _blockdiag_c_t(dcti),
        "s5_d": dd_skip.reshape(512), "s5_glu_w": dglu_w, "s5_glu_b": dglu_b.reshape(512),
        "gdn_w_in": dw_in1[:, :wd["gdn_w_in"].shape[1]], "gdn_conv": dconv_w,
        "gdn_a_log": dalog_row[0, C_HEADS:2 * C_HEADS], "gdn_dt_bias": ddtb_row[0, C_HEADS:2 * C_HEADS],
        "gdn_norm_g": dnorm_g.reshape(128), "gdn_w_out": dw_out1,
    }
    dmod = jnp.concatenate([jnp.concatenate([dshift0, dscale0, dgate0], 1), jnp.concatenate([dshift1, dscale1, dgate1], 1)], 0)
    return loss_row[0, 0], grad_x, grads, dmod


def _place():
    return lax.axis_index("x"), lax.axis_index("y"), lax.axis_index("c")


def _flip(v, bit):
    return 1 - v if bit else v


def _hbm_call(name, body, arrs, out_shapes, n_sem):
    any_spec = pl.BlockSpec(memory_space=pl.ANY)
    return pl.pallas_call(
        body, name=name,
        in_specs=[any_spec] * len(arrs), out_specs=[any_spec] * len(out_shapes), out_shape=out_shapes,
        scratch_shapes=[pltpu.SemaphoreType.DMA((n_sem,)), pltpu.SemaphoreType.DMA((n_sem,))],
    )(*arrs)


def _own_slot(gathered, own, slot):
    idx = lax.broadcasted_iota(jnp.int32, (gathered.shape[0],) + (1,) * own.ndim, 0)
    return jnp.where(idx == slot, own[None], gathered)


def _all_gather8(name, arr):
    def body(x_ref, out_ref, send_sems, recv_sems):
        x, y, c = _place()
        me = 4 * x + 2 * y + c
        sends, recvs = [], []
        for m in range(1, 8):
            peer = (_flip(x, m & 4), _flip(y, m & 2), _flip(c, m & 1))
            sends.append(pltpu.make_async_remote_copy(x_ref, out_ref.at[me], send_sems.at[m - 1], recv_sems.at[m - 1],
                                                      device_id=peer, device_id_type=MESH))
            recvs.append(pltpu.make_async_remote_copy(x_ref, out_ref.at[4 * peer[0] + 2 * peer[1] + peer[2]], send_sems.at[m - 1],
                                                      recv_sems.at[m - 1], device_id=peer, device_id_type=MESH))
        for cp in sends:
            cp.start()
        for cp in recvs:
            cp.wait_recv()
        for cp in sends:
            cp.wait_send()

    return _hbm_call(name, body, [arr], [jax.ShapeDtypeStruct((8,) + arr.shape, arr.dtype)], 7)[0]


def _all_to_all8(name, arr):
    def body(x_ref, out_ref, send_sems, recv_sems):
        x, y, c = _place()
        me = 4 * x + 2 * y + c
        sends, recvs = [], []
        for m in range(1, 8):
            peer = (_flip(x, m & 4), _flip(y, m & 2), _flip(c, m & 1))
            peer_id = 4 * peer[0] + 2 * peer[1] + peer[2]
            sends.append(pltpu.make_async_remote_copy(x_ref.at[peer_id], out_ref.at[me], send_sems.at[m - 1], recv_sems.at[m - 1],
                                                      device_id=peer, device_id_type=MESH))
            recvs.append(pltpu.make_async_remote_copy(x_ref.at[peer_id], out_ref.at[peer_id], send_sems.at[m - 1], recv_sems.at[m - 1],
                                                      device_id=peer, device_id_type=MESH))
        for cp in sends:
            cp.start()
        for cp in recvs:
            cp.wait_recv()
        for cp in sends:
            cp.wait_send()

    return _hbm_call(name, body, [arr], [jax.ShapeDtypeStruct(arr.shape, arr.dtype)], 7)[0]


def _chip_copies(ins, outs, send_sems, recv_sems, scatter):
    x, y, c = _place()
    mine = 2 * x + y
    sends, recvs = [], []
    for a in range(len(ins)):
        for m in range(1, 4):
            px, py = _flip(x, m & 2), _flip(y, m & 1)
            k = 3 * a + m - 1
            src = ins[a].at[2 * px + py] if scatter else ins[a]
            sends.append(pltpu.make_async_remote_copy(src, outs[a].at[mine], send_sems.at[k], recv_sems.at[k],
                                                      device_id=(px, py, c), device_id_type=MESH))
            recvs.append(pltpu.make_async_remote_copy(src, outs[a].at[2 * px + py], send_sems.at[k], recv_sems.at[k],
                                                      device_id=(px, py, c), device_id_type=MESH))
    return sends, recvs


def _chip_shapes(arrs, scatter):
    return [jax.ShapeDtypeStruct(a.shape if scatter else (4,) + a.shape, a.dtype) for a in arrs]


def _chip_exchange(name, arrs, scatter):
    n = len(arrs)

    def body(*refs):
        sends, recvs = _chip_copies(refs[:n], refs[n:2 * n], refs[2 * n], refs[2 * n + 1], scatter)
        for cp in sends:
            cp.start()
        for cp in recvs:
            cp.wait_recv()
        for cp in sends:
            cp.wait_send()

    return _hbm_call(name, body, arrs, _chip_shapes(arrs, scatter), 3 * n)


def _call_with_exchange(body, name, grid, in_specs, out_specs, out_shape, scratch_shapes, args, exchange):
    if exchange is None:
        return pl.pallas_call(body, name=name, grid=grid, in_specs=in_specs, out_specs=out_specs, out_shape=out_shape,
                              scratch_shapes=scratch_shapes, compiler_params=_cp(*["arbitrary"] * len(grid)))(*args), []
    arrs, kind = exchange
    n_in, n_out, n_ex, n_scr = len(in_specs), len(out_specs), len(arrs), len(scratch_shapes)
    n_sem = n_ex if kind == "sibling" else 3 * n_ex
    ex_shapes = [jax.ShapeDtypeStruct(a.shape, a.dtype) for a in arrs] if kind == "sibling" else _chip_shapes(arrs, kind == "scatter")

    def fused(*refs):
        ins, ex_in = refs[:n_in], refs[n_in:n_in + n_ex]
        outs, ex_out = refs[n_in + n_ex:n_in + n_ex + n_out], refs[n_in + n_ex + n_out:n_in + 2 * n_ex + n_out]
        rest = refs[n_in + 2 * n_ex + n_out:]
        if kind == "sibling":
            sends = recvs = _sibling_copies(ex_in, ex_out, rest[n_scr], rest[n_scr + 1])
        else:
            sends, recvs = _chip_copies(ex_in, ex_out, rest[n_scr], rest[n_scr + 1], kind == "scatter")
        first, last = pl.program_id(0) == 0, pl.program_id(0) == grid[0] - 1
        for k in range(1, len(grid)):
            first, last = first & (pl.program_id(k) == 0), last & (pl.program_id(k) == grid[k] - 1)

        @pl.when(first)
        def _():
            for cp in sends:
                cp.start()

        body(*ins, *outs, *rest[:n_scr])

        @pl.when(last)
        def _():
            for cp in recvs:
                cp.wait_recv()
            for cp in sends:
                cp.wait_send()

    any_spec = pl.BlockSpec(memory_space=pl.ANY)
    res = pl.pallas_call(
        fused, name=name, grid=grid, in_specs=list(in_specs) + [any_spec] * n_ex, out_specs=list(out_specs) + [any_spec] * n_ex,
        out_shape=list(out_shape) + ex_shapes,
        scratch_shapes=list(scratch_shapes) + [pltpu.SemaphoreType.DMA((n_sem,))] * 2,
        compiler_params=_cp(*["arbitrary"] * len(grid)))(*args, *arrs)
    return res[:n_out], res[n_out:]


def _sibling_copies(ins, outs, send_sems, recv_sems):
    x, y, c = _place()
    return [pltpu.make_async_remote_copy(ins[a], outs[a], send_sems.at[a], recv_sems.at[a],
                                         device_id=(x, y, 1 - c), device_id_type=MESH) for a in range(len(ins))]


def _sibling_exchange(name, arrs):
    n = len(arrs)

    def body(*refs):
        copies = _sibling_copies(refs[:n], refs[n:2 * n], refs[2 * n], refs[2 * n + 1])
        for cp in copies:
            cp.start()
        for cp in copies:
            cp.wait_recv()
        for cp in copies:
            cp.wait_send()

    return _hbm_call(name, body, arrs, [jax.ShapeDtypeStruct(a.shape, a.dtype) for a in arrs], n)


def _row_tile(rows):
    for t in (256, 128, 64, 32, 16, 8):
        if rows % t == 0:
            return t
    return rows


def _pair_sum(name, a, b, out_dtype):
    rows, cols = a.shape
    tr = _row_tile(rows)

    def body(a_ref, b_ref, o_ref):
        o_ref[...] = (a_ref[...] + b_ref[...]).astype(out_dtype)

    return pl.pallas_call(body, name=name, grid=(rows // tr,), in_specs=[_row(tr, cols)] * 2, out_specs=_row(tr, cols),
                          out_shape=_sds(rows, cols, dtype=out_dtype), compiler_params=_cp("arbitrary"))(a, b)


def _chip_sum(name, recv, partial, mine):
    n, rows, cols = recv.shape
    tr = _row_tile(rows)

    def body(mine_ref, *refs):
        own = refs[n][0].astype(F32)
        acc = None
        for s in range(n):
            term = jnp.where(mine_ref[0] == s, own, refs[s][0].astype(F32))
            acc = term if acc is None else acc + term
        refs[-1][...] = acc

    def slot_spec(s):
        return pl.BlockSpec((1, tr, cols), lambda i, m: (jnp.where(m[0] == s, (s + 1) % n, s), i, 0))

    grid_spec = pltpu.PrefetchScalarGridSpec(
        num_scalar_prefetch=1, grid=(rows // tr,),
        in_specs=[slot_spec(s) for s in range(n)] + [pl.BlockSpec((1, tr, cols), lambda i, m: (m[0], i, 0))],
        out_specs=pl.BlockSpec((tr, cols), lambda i, m: (i, 0)))
    return pl.pallas_call(body, name=name, grid_spec=grid_spec, out_shape=_sds(rows, cols),
                          compiler_params=_cp("arbitrary"))(mine, *([recv] * n), partial)


def _slot_sum(name, arr):
    n, rows, cols = arr.shape
    tr = _row_tile(rows)

    def body(*refs):
        acc = refs[0][0]
        for r in refs[1:-1]:
            acc = acc + r[0]
        refs[-1][...] = acc

    specs = [pl.BlockSpec((1, tr, cols), functools.partial(lambda s, i: (s, i, 0), s)) for s in range(n)]
    return pl.pallas_call(body, name=name, grid=(rows // tr,), in_specs=specs, out_specs=_row(tr, cols),
                          out_shape=_sds(rows, cols), compiler_params=_cp("arbitrary"))(*([arr] * n))


def _adamw(name, w, g, m, v):
    rows, cols = w.shape
    tr = _row_tile(rows)

    def body(w_ref, g_ref, m_ref, v_ref, d_ref, nm_ref, nv_ref):
        g_ = g_ref[...]
        m_ = ADAM_B1 * m_ref[...] + (1.0 - ADAM_B1) * g_
        v_ = ADAM_B2 * v_ref[...] + (1.0 - ADAM_B2) * (g_ * g_)
        m_hat = m_ / (1.0 - ADAM_B1 ** ADAM_STEP)
        v_hat = v_ / (1.0 - ADAM_B2 ** ADAM_STEP)
        d_ref[...] = -ADAM_LR * (m_hat / (jnp.sqrt(v_hat) + ADAM_EPS) + ADAM_WD * w_ref[...])
        nm_ref[...] = m_
        nv_ref[...] = v_

    spec = _row(tr, cols)
    return pl.pallas_call(body, name=name, grid=(rows // tr,), in_specs=[spec] * 4, out_specs=[spec] * 3,
                          out_shape=[_sds(rows, cols)] * 3, compiler_params=_cp("arbitrary"))(w, g, m, v)


def _adamw_many(name, ws, gs, ms, vs):
    n = len(ws)

    def body(*refs):
        for i in range(n):
            w_ref, g_ref, m_ref, v_ref = (refs[k * n + i] for k in range(4))
            d_ref, nm_ref, nv_ref = (refs[(4 + k) * n + i] for k in range(3))
            g_ = g_ref[...]
            m_ = ADAM_B1 * m_ref[...] + (1.0 - ADAM_B1) * g_
            v_ = ADAM_B2 * v_ref[...] + (1.0 - ADAM_B2) * (g_ * g_)
            m_hat = m_ / (1.0 - ADAM_B1 ** ADAM_STEP)
            v_hat = v_ / (1.0 - ADAM_B2 ** ADAM_STEP)
            d_ref[...] = -ADAM_LR * (m_hat / (jnp.sqrt(v_hat) + ADAM_EPS) + ADAM_WD * w_ref[...])
            nm_ref[...] = m_
            nv_ref[...] = v_

    shapes = [_sds(*a.shape) for a in ws]
    res = pl.pallas_call(body, name=name, out_shape=shapes * 3,
                         compiler_params=pltpu.CompilerParams(vmem_limit_bytes=VMEM_LIMIT_BYTES))(*ws, *gs, *ms, *vs)
    return [(res[i], res[n + i], res[2 * n + i]) for i in range(n)]


def _adamw_halves(name, w, g_mine, g_sibling, m, v, core):
    _, rows, cols = w.shape
    half = rows // 2
    tr = _row_tile(half)
    per_half = half // tr

    def body(core_ref, w_ref, gm_ref, gs_ref, m_ref, v_ref, g_ref, d_ref, nm_ref, nv_ref):
        g_ = jnp.where(pl.program_id(0) // per_half == core_ref[0], gm_ref[...], gs_ref[...])
        m_ = ADAM_B1 * m_ref[...] + (1.0 - ADAM_B1) * g_
        v_ = ADAM_B2 * v_ref[...] + (1.0 - ADAM_B2) * (g_ * g_)
        m_hat = m_ / (1.0 - ADAM_B1 ** ADAM_STEP)
        v_hat = v_ / (1.0 - ADAM_B2 ** ADAM_STEP)
        g_ref[...] = g_
        d_ref[...] = -ADAM_LR * (m_hat / (jnp.sqrt(v_hat) + ADAM_EPS) + ADAM_WD * w_ref[...])
        nm_ref[...] = m_
        nv_ref[...] = v_

    full = pl.BlockSpec((None, tr, cols), lambda i, c: (0, i, 0))
    in_half = pl.BlockSpec((tr, cols), lambda i, c: (i % per_half, 0))
    grid_spec = pltpu.PrefetchScalarGridSpec(num_scalar_prefetch=1, grid=(rows // tr,),
                                             in_specs=[full, in_half, in_half, full, full], out_specs=[full] * 4)
    return pl.pallas_call(body, name=name, grid_spec=grid_spec, out_shape=[_sds(1, rows, cols)] * 4,
                          compiler_params=_cp("arbitrary"))(core, w, g_mine, g_sibling, m, v)


def _mod_local(c_all, ada_w):
    def body(c_ref, w_ref, o_ref):
        c_act = jax.nn.silu(c_ref[...])
        for l in range(2):
            o_ref[l] = _hdot(c_act, w_ref[l])

    return pl.pallas_call(body, name="mod_local", out_shape=_sds(2, 8, ada_w.shape[2]),
                          compiler_params=pltpu.CompilerParams(vmem_limit_bytes=VMEM_LIMIT_BYTES))(c_all, ada_w)


def _ada_w_grad(c_all, dmod_cols):
    def body(c_ref, d_ref, o_ref):
        c_act = jax.nn.silu(c_ref[...])
        for l in range(2):
            o_ref[l] = lax.dot_general(c_act, d_ref[l], (((0,), (0,)), ((), ())), precision=HI, preferred_element_type=F32)

    return pl.pallas_call(body, name="ada_w_grad", out_shape=_sds(2, D_MODEL, dmod_cols.shape[2]),
                          compiler_params=pltpu.CompilerParams(vmem_limit_bytes=VMEM_LIMIT_BYTES))(c_all, dmod_cols)


_SMALL = ("ada_b", "pre_g", "post_g", "rel_bias", "s5_a_re", "s5_a_im", "s5_log_dt", "s5_b_re", "s5_b_im", "s5_c_re", "s5_c_im",
          "s5_d", "s5_glu_b", "gdn_a_log", "gdn_dt_bias", "gdn_norm_g")
_SHARDED = ("ab_w_in", "ab_w_out", "s5_glu_w", "gdn_w_in", "gdn_w_out")
_COL_SHARDED = ("ab_w_in", "gdn_w_in")
_WEIGHTS = ("ada_w", "ada_b", "pre_g", "post_g", "rel_bias", "ab_w_in", "ab_w_out", "s5_a_re", "s5_a_im", "s5_log_dt", "s5_b_re",
            "s5_b_im", "s5_c_re", "s5_c_im", "s5_d", "s5_glu_w", "s5_glu_b", "gdn_w_in", "gdn_conv", "gdn_a_log", "gdn_dt_bias",
            "gdn_norm_g", "gdn_w_out")


def _rows128(n):
    return -(-n // 128)


def _pack(arrs, total_rows):
    pieces = []
    for a in arrs:
        flat = a.reshape(-1)
        pieces.append(jnp.pad(flat, (0, _rows128(flat.shape[0]) * 128 - flat.shape[0])).reshape(-1, 128))
    used = sum(p.shape[0] for p in pieces)
    pieces.append(jnp.zeros((total_rows - used, 128), F32))
    return jnp.concatenate(pieces, axis=0)


def _unpack(buf, shapes):
    out, at = [], 0
    for shp in shapes:
        n = int(np.prod(shp))
        out.append(buf[at:at + _rows128(n)].reshape(-1)[:n].reshape(shp))
        at += _rows128(n)
    return out


def _full_from_halves(name, g):
    if name in _COL_SHARDED:
        return g.transpose(0, 2, 1, 3).reshape(2 * g.shape[2], 4 * g.shape[3])
    return g.transpose(1, 0, 2, 3).reshape(8 * g.shape[2], g.shape[3])


def _shard_major(name, g):
    if name in _COL_SHARDED:
        return g.reshape(g.shape[0], 4, g.shape[1] // 4).transpose(1, 0, 2)
    return g.reshape(4, g.shape[0] // 4, g.shape[1])


_LATE = ("ab_w_out", "s5_glu_w", "gdn_w_in", "gdn_w_out")


class _WeightExchanges:
    def __init__(self, shards, core, chip):
        self.core, self.chip = core, chip
        self.half = {}
        for name, shard in shards.items():
            h = shard.shape[0] // 2
            self.half[name] = lax.dynamic_slice_in_dim(shard.astype(BF), core * h, h, axis=0)
        self.mine, self.partial, self.received = {}, {}, {}

    def my_halves(self, names, from_chips):
        return [_own_slot(g, self.half[n], self.chip) for n, g in zip(names, from_chips)]

    def full_weights(self, names, mine, theirs):
        return {n: _full_from_halves(n, jnp.where(self.core == 0, jnp.stack([a, b], 0), jnp.stack([b, a], 0)))
                for n, a, b in zip(names, mine, theirs)}

    def first_weights(self):
        mine = self.my_halves(["ab_w_in"], _chip_exchange("gather_w_chips", [self.half["ab_w_in"]], False))
        return self.full_weights(["ab_w_in"], mine, _sibling_exchange("gather_w_sibling_first", mine))

    def late_exchanges(self):
        rows = self.half["gdn_w_in"].shape[0] // 2
        pieces = [self.half["gdn_w_in"][:rows], self.half["gdn_w_in"][rows:]]
        return [([self.half["ab_w_out"], self.half["s5_glu_w"]], "gather"), ([self.half["gdn_w_out"]], "gather"),
                ([pieces[0]], "gather"), ([pieces[1]], "gather")]

    def late_halves(self, got):
        return (self.my_halves(["ab_w_out", "s5_glu_w"], got[0]),
                self.my_halves(["gdn_w_in", "gdn_w_out"], [jnp.concatenate([got[2][0], got[3][0]], axis=1), got[1][0]]))

    def split_halves(self, grads):
        other = []
        for name, g in grads.items():
            sm = _shard_major(name, g)
            h = sm.shape[1] // 2
            self.mine[name] = lax.dynamic_slice_in_dim(sm, self.core * h, h, axis=1)
            other.append(lax.dynamic_slice_in_dim(sm, (1 - self.core) * h, h, axis=1))
        return other

    def chip_partials(self, names, from_sibling):
        for name, b in zip(names, from_sibling):
            a = self.mine[name]
            flat = lambda t: t.reshape(-1, t.shape[-1])
            self.partial[name] = _pair_sum("sum_sibling_" + name, flat(a), flat(b), BF).reshape(a.shape)
        return [self.partial[n] for n in names]


def kernel(x, c, ada_w, ada_b, pre_g, post_g, rel_bias, ab_w_in, ab_w_out, s5_a_re, s5_a_im, s5_log_dt, s5_b_re, s5_b_im, s5_c_re, s5_c_im, s5_d, s5_glu_w, s5_glu_b, gdn_w_in, gdn_conv, gdn_a_log, gdn_dt_bias, gdn_norm_g, gdn_w_out, loss_target, m_ada_w, m_ada_b, m_pre_g, m_post_g, m_rel_bias, m_ab_w_in, m_ab_w_out, m_s5_a_re, m_s5_a_im, m_s5_log_dt, m_s5_b_re, m_s5_b_im, m_s5_c_re, m_s5_c_im, m_s5_d, m_s5_glu_w, m_s5_glu_b, m_gdn_w_in, m_gdn_conv, m_gdn_a_log, m_gdn_dt_bias, m_gdn_norm_g, m_gdn_w_out, v_ada_w, v_ada_b, v_pre_g, v_post_g, v_rel_bias, v_ab_w_in, v_ab_w_out, v_s5_a_re, v_s5_a_im, v_s5_log_dt, v_s5_b_re, v_s5_b_im, v_s5_c_re, v_s5_c_im, v_s5_d, v_s5_glu_w, v_s5_glu_b, v_gdn_w_in, v_gdn_conv, v_gdn_a_log, v_gdn_dt_bias, v_gdn_norm_g, v_gdn_w_out):
    w = dict(ada_w=ada_w, ada_b=ada_b, pre_g=pre_g, post_g=post_g, rel_bias=rel_bias, ab_w_in=ab_w_in, ab_w_out=ab_w_out,
             s5_a_re=s5_a_re, s5_a_im=s5_a_im, s5_log_dt=s5_log_dt, s5_b_re=s5_b_re, s5_b_im=s5_b_im, s5_c_re=s5_c_re, s5_c_im=s5_c_im,
             s5_d=s5_d, s5_glu_w=s5_glu_w, s5_glu_b=s5_glu_b, gdn_w_in=gdn_w_in, gdn_conv=gdn_conv, gdn_a_log=gdn_a_log,
             gdn_dt_bias=gdn_dt_bias, gdn_norm_g=gdn_norm_g, gdn_w_out=gdn_w_out)
    m = dict(ada_w=m_ada_w, ada_b=m_ada_b, pre_g=m_pre_g, post_g=m_post_g, rel_bias=m_rel_bias, ab_w_in=m_ab_w_in, ab_w_out=m_ab_w_out,
             s5_a_re=m_s5_a_re, s5_a_im=m_s5_a_im, s5_log_dt=m_s5_log_dt, s5_b_re=m_s5_b_re, s5_b_im=m_s5_b_im, s5_c_re=m_s5_c_re,
             s5_c_im=m_s5_c_im, s5_d=m_s5_d, s5_glu_w=m_s5_glu_w, s5_glu_b=m_s5_glu_b, gdn_w_in=m_gdn_w_in, gdn_conv=m_gdn_conv,
             gdn_a_log=m_gdn_a_log, gdn_dt_bias=m_gdn_dt_bias, gdn_norm_g=m_gdn_norm_g, gdn_w_out=m_gdn_w_out)
    v = dict(ada_w=v_ada_w, ada_b=v_ada_b, pre_g=v_pre_g, post_g=v_post_g, rel_bias=v_rel_bias, ab_w_in=v_ab_w_in, ab_w_out=v_ab_w_out,
             s5_a_re=v_s5_a_re, s5_a_im=v_s5_a_im, s5_log_dt=v_s5_log_dt, s5_b_re=v_s5_b_re, s5_b_im=v_s5_b_im, s5_c_re=v_s5_c_re,
             s5_c_im=v_s5_c_im, s5_d=v_s5_d, s5_glu_w=v_s5_glu_w, s5_glu_b=v_s5_glu_b, gdn_w_in=v_gdn_w_in, gdn_conv=v_gdn_conv,
             gdn_a_log=v_gdn_a_log, gdn_dt_bias=v_gdn_dt_bias, gdn_norm_g=v_gdn_norm_g, gdn_w_out=v_gdn_w_out)
    ix, iy, ic = _place()
    me = 4 * ix + 2 * iy + ic
    chip = 2 * ix + iy
    n_cols = ada_w.shape[2]

    mine_first = _pack([c, gdn_conv], 32)
    first = _own_slot(_all_gather8("gather_c_conv", mine_first), mine_first, me)
    c_all = first[:, 0:8].reshape(8, D_MODEL)
    conv_full = first[0::2, 8:32].reshape(4, C_CONV, n_cols).transpose(1, 0, 2).reshape(C_CONV, 4 * n_cols)
    mine_mod = _mod_local(c_all, ada_w)
    modl = _own_slot(_all_gather8("gather_mod", mine_mod), mine_mod, me)
    mod = lax.dynamic_index_in_dim(modl[0::2], me, axis=2, keepdims=False)
    mod = mod.transpose(1, 0, 2).reshape(2, 4 * n_cols) + ada_b

    comm = _WeightExchanges({name: w[name][0] for name in _SHARDED}, ic, chip)
    wd = {name: w[name] for name in _SMALL if name != "ada_b"}
    wd = {k: (a if k in ("pre_g", "post_g", "rel_bias") else a[0]) for k, a in wd.items()}
    wd["gdn_conv"] = conv_full
    wd.update(comm.first_weights())

    loss_local, grad_x, grads, dmod = _local_step(x[0], loss_target[0], mod, wd, comm)
    loss = lax.psum(loss_local, ("x", "y", "c"))

    small_shapes = [w[name].shape for name in _SMALL] + [(C_CONV, 4 * n_cols)]
    small_rows = -(-sum(_rows128(int(np.prod(s))) for s in small_shapes) // 64) * 64
    per_dev, dmod_rows = small_rows // 8, _rows128(2 * 3 * D_MODEL)
    partial = _pack([dmod] + [grads[name] for name in _SMALL[1:]] + [grads["gdn_conv"]], small_rows)
    outbound = jnp.concatenate([partial.reshape(8, per_dev, 128), jnp.broadcast_to(partial[None, :dmod_rows], (8, dmod_rows, 128))], axis=1)
    inbound = _own_slot(_all_to_all8("reduce_small_grads", outbound), lax.dynamic_index_in_dim(outbound, me, 0, keepdims=False), me)
    my_rows = _slot_sum("sum_small_grads", inbound[:, :per_dev])
    g_small = _own_slot(_all_gather8("gather_small_grads", my_rows), my_rows, me).reshape(small_rows, 128)
    g_list = _unpack(g_small, small_shapes)
    out_g, out_d, out_m, out_v = {}, {}, {}, {}

    def update(name, g2d):
        shp = w[name].shape
        two_d = lambda a: a.reshape(-1, shp[-1])
        d_, m_, v_ = _adamw("adamw_" + name, two_d(w[name]), g2d, two_d(m[name]), two_d(v[name]))
        out_g[name], out_d[name], out_m[name], out_v[name] = (a.reshape(shp) for a in (g2d, d_, m_, v_))

    small = list(_SMALL) + ["gdn_conv"]
    small_g = [g.reshape(-1, g.shape[-1]) for g in g_list[:-1]] + [lax.dynamic_slice_in_dim(g_list[-1], chip * n_cols, n_cols, axis=1)]
    two_d = lambda a: a.reshape(-1, a.shape[-1])
    results = _adamw_many("adamw_small", [two_d(w[n]) for n in small], small_g, [two_d(m[n]) for n in small], [two_d(v[n]) for n in small])
    for name, g2d, (d_, m_, v_) in zip(small, small_g, results):
        out_g[name], out_d[name], out_m[name], out_v[name] = (a.reshape(w[name].shape) for a in (g2d, d_, m_, v_))

    dmod_all = inbound[:, per_dev:].reshape(8, 2, 4, n_cols)
    dmod_cols = lax.dynamic_index_in_dim(dmod_all, chip, axis=2, keepdims=False).transpose(1, 0, 2)
    update("ada_w", _ada_w_grad(c_all, dmod_cols).reshape(-1, n_cols))

    from_sibling = _sibling_exchange("reduce_sibling", comm.split_halves({"ab_w_in": grads["ab_w_in"]}))
    comm.received["ab_w_in"] = _chip_exchange("reduce_chips", comm.chip_partials(["ab_w_in"], from_sibling), True)[0]
    chip_1 = jnp.reshape(chip, (1,)).astype(jnp.int32)
    core_1 = jnp.reshape(ic, (1,)).astype(jnp.int32)
    reduced = [_chip_sum("sum_chips_" + name, comm.received[name], comm.partial[name], chip_1) for name in _SHARDED]
    for name, g_mine, g_sib in zip(_SHARDED, reduced, _sibling_exchange("reduce_share", reduced)):
        out_g[name], out_d[name], out_m[name], out_v[name] = _adamw_halves(
            "adamw_" + name, w[name], g_mine, g_sib, m[name], v[name], core_1)

    return (loss, grad_x[None], *[out_g[n] for n in _WEIGHTS], *[out_d[n] for n in _WEIGHTS],
            *[out_m[n] for n in _WEIGHTS], *[out_v[n] for n in _WEIGHTS])
```

```python
import functools
import math

import numpy as np
import jax
import jax.numpy as jnp
from jax import lax
from jax.experimental import pallas as pl
from jax.experimental.pallas import tpu as pltpu

F32 = jnp.float32
BF = jnp.bfloat16
HI = lax.Precision.HIGHEST
MESH = pl.DeviceIdType.MESH

D_MODEL = 1024
EPS = 1e-6
A_HEADS, A_HD, A_WIDTH, A_BLOCK = 8, 64, 512, 128
DILATIONS = (1, 4, 16)
N_KEYS = 128
REL_BUCKETS, REL_MAX_DIST = 32, 2048
B_WIDTH, B_GROUP, B_GROUPS, B_STATE = 512, 16, 32, 64
S5_LANES = 512
S5_TILES = 4
C_HEADS, C_DK, C_CHUNK, C_CONV = 8, 128, 64, 4
QKV = 3072
C_IN_PAD = 4224
TM = 256
VMEM_LIMIT_BYTES = 56 * 1024 * 1024
ADAM_LR, ADAM_B1, ADAM_B2, ADAM_EPS, ADAM_WD, ADAM_STEP = 0.001, 0.9, 0.999, 1e-08, 0.01, 10
NEG = float(np.finfo(np.float32).min)


def _cp(*sem):
    return pltpu.CompilerParams(dimension_semantics=sem, vmem_limit_bytes=VMEM_LIMIT_BYTES)


def _bdot(a, b):
    return jnp.dot(a.astype(BF), b.astype(BF), preferred_element_type=F32)


def _bdot_nt(a, b):
    return lax.dot_general(a.astype(BF), b.astype(BF), (((1,), (1,)), ((), ())), preferred_element_type=F32)


def _bdot_tn(a, b):
    return lax.dot_general(a.astype(BF), b.astype(BF), (((0,), (0,)), ((), ())), preferred_element_type=F32)


def _hdot(a, b):
    return jnp.dot(a, b, precision=HI, preferred_element_type=F32)


def _bein(eq, a, b):
    return jnp.einsum(eq, a.astype(BF), b.astype(BF), preferred_element_type=F32)


def _row(tm, n):
    return pl.BlockSpec((tm, n), lambda i: (i, 0))


def _fix(shape):
    return pl.BlockSpec(shape, lambda i: (0,) * len(shape))


def _sds(*shape, dtype=F32):
    return jax.ShapeDtypeStruct(shape, dtype)


def _acc(ref, val):
    ref[...] += val


def _zero_at_first(refs, axis=0):
    @pl.when(pl.program_id(axis) == 0)
    def _():
        for r in refs:
            r[...] = jnp.zeros_like(r)


def _rms(x):
    return x * lax.rsqrt(jnp.mean(x * x, axis=-1, keepdims=True) + EPS)


def _pre_mod(x, g, scale, shift):
    return (_rms(x) * g) * (1.0 + scale) + shift


def _post_res(y, x, post_g, gate):
    return x + gate * (_rms(y) * post_g)


def _merge_gate(o1, o2, o3, l1, l2, l3, ga):
    m = jnp.maximum(jnp.maximum(l1, l2), l3)
    e1, e2, e3 = jnp.exp(l1 - m), jnp.exp(l2 - m), jnp.exp(l3 - m)
    inv = 1.0 / (e1 + e2 + e3)
    return ((e1 * inv) * o1 + (e2 * inv) * o2 + (e3 * inv) * o3) * jax.nn.silu(ga)


def _s5_gelu(ypre, u, d_skip):
    return jax.nn.gelu(ypre + d_skip * u)


def _s5_glu(yb, gl, gb):
    return yb * jax.nn.sigmoid(gl) * jax.nn.silu(gb)


def _l0_front(x, pre_g, scale, shift, w_in):
    s_len = x.shape[0]

    def body(x_ref, g_ref, sc_ref, sh_ref, w_ref, *out_refs):
        qkv_refs, (u_ref, ga_ref, gb_ref, h_ref) = out_refs[:9], out_refs[9:]
        hb = _pre_mod(x_ref[...], g_ref[...], sc_ref[...], sh_ref[...]).astype(BF)
        h_ref[...] = hb
        z = jnp.dot(hb, w_ref[...], preferred_element_type=F32)
        for a in range(3):
            piece = z[:, a * 512:(a + 1) * 512]
            for i, d in enumerate(DILATIONS):
                qkv_refs[3 * a + i][...] = _to_res(piece, d).astype(BF)
        u_ref[...] = z[:, 1536:2048]
        ga_ref[...] = z[:, 2048:2560]
        gb_ref[...] = z[:, 2560:3072]

    vec = _fix((1, D_MODEL))
    return pl.pallas_call(
        body, name="l0_front", grid=(s_len // TM,),
        in_specs=[_row(TM, D_MODEL), vec, vec, vec, _fix((D_MODEL, 3072))],
        out_specs=[_res_spec(d) for d in DILATIONS] * 3 + [_row(TM, 512)] * 3 + [_row(TM, D_MODEL)],
        out_shape=[_sds(*_res_shape(s_len, d), dtype=BF) for d in DILATIONS] * 3 + [_sds(s_len, 512)] * 3 + [_sds(s_len, D_MODEL, dtype=BF)],
        compiler_params=_cp("arbitrary"),
    )(x, pre_g, scale, shift, w_in)


def _front_bwd(name, x, pre_g, scale, shift, w_in, dres, parts, widths):
    s_len = x.shape[0]
    n_in = sum(len(p) for p in parts)
    n_cols = sum(widths)

    def body(*refs):
        x_ref, g_ref, sc_ref, sh_ref, w_ref, dres_ref = refs[:6]
        part_refs = refs[6:6 + n_in]
        dz_ref, dx_ref, dg_ref, dsc_ref, dsh_ref = refs[6 + n_in:]
        _zero_at_first([dg_ref, dsc_ref, dsh_ref])
        _, vjp = jax.vjp(_pre_mod, x_ref[...], g_ref[...], sc_ref[...], sh_ref[...])
        dh = jnp.zeros((TM, D_MODEL), F32)
        col, at = 0, 0
        for grp, width in zip(parts, widths):
            tile = lambda r: _from_res(r[...]) if len(r.shape) == 3 else r[...]
            dz = tile(part_refs[at])
            for r in part_refs[at + 1:at + len(grp)]:
                dz = dz + tile(r)
            at += len(grp)
            dzb = dz.astype(BF)
            dz_ref[:, col:col + width] = dzb
            dh = dh + lax.dot_general(dzb, w_ref[:, col:col + width], (((1,), (1,)), ((), ())), preferred_element_type=F32)
            col += width
        dx, dg, dsc, dsh = vjp(dh)
        dx_ref[...] = dx + dres_ref[...]
        _acc(dg_ref, dg)
        _acc(dsc_ref, dsc)
        _acc(dsh_ref, dsh)

    vec = _fix((1, D_MODEL))
    flat = [a for p in parts for a in p]
    return pl.pallas_call(
        body, name=name, grid=(s_len // TM,),
        in_specs=[_row(TM, D_MODEL), vec, vec, vec, _fix((D_MODEL, n_cols)), _row(TM, D_MODEL)]
        + [_res_spec(a.shape[0], a.shape[2]) if a.ndim == 3 else _row(TM, a.shape[1]) for a in flat],
        out_specs=[_row(TM, n_cols), _row(TM, D_MODEL), vec, vec, vec],
        out_shape=[_sds(s_len, n_cols, dtype=BF), _sds(s_len, D_MODEL), _sds(1, D_MODEL), _sds(1, D_MODEL), _sds(1, D_MODEL)],
        compiler_params=_cp("arbitrary"),
    )(x, pre_g, scale, shift, w_in, dres, *flat)


def _matmul_tn(name, a, b, tn):
    s_len, k_dim = a.shape
    n_dim = b.shape[1]
    ts = 512

    def body(a_ref, b_ref, o_ref):
        _zero_at_first([o_ref], axis=1)
        o_ref[...] += lax.dot_general(a_ref[...], b_ref[...], (((0,), (0,)), ((), ())), preferred_element_type=F32)

    return pl.pallas_call(
        body, name=name, grid=(n_dim // tn, s_len // ts),
        in_specs=[pl.BlockSpec((ts, k_dim), lambda j, i: (i, 0)), pl.BlockSpec((ts, tn), lambda j, i: (i, j))],
        out_specs=pl.BlockSpec((k_dim, tn), lambda j, i: (0, j)),
        out_shape=_sds(k_dim, n_dim),
        compiler_params=_cp("arbitrary", "arbitrary"),
    )(a, b)


def _t5_bucket_np(dist):
    dist = np.maximum(dist, 0)
    max_exact = REL_BUCKETS // 2
    large = max_exact + (np.log(np.maximum(dist, 1) / max_exact)
                         / math.log(REL_MAX_DIST / max_exact) * (REL_BUCKETS - max_exact)).astype(np.int32)
    large = np.minimum(large, REL_BUCKETS - 1)
    return np.where(dist < max_exact, dist, large).astype(np.int32)


def _to_res(z, dil):
    if dil == 1:
        return z[None]
    return jnp.swapaxes(z.reshape(z.shape[0] // dil, dil, z.shape[1]), 0, 1)


def _from_res(z):
    if z.shape[0] == 1:
        return z[0]
    return jnp.swapaxes(z, 0, 1).reshape(z.shape[0] * z.shape[1], z.shape[2])


def _res_shape(s_len, dil, width=A_WIDTH):
    return (dil, s_len // dil, width)


def _res_spec(dil, width=A_WIDTH):
    return pl.BlockSpec((dil, TM // dil, width), lambda i: (0, i, 0))


def _bucket_table():
    qi = np.arange(A_BLOCK)[:, None]
    kj = np.arange(2 * A_BLOCK)[None, :]
    return np.stack([_t5_bucket_np((qi + A_BLOCK - kj) * d) for d in DILATIONS], 0)


def _attn_mask(first):
    qi = lax.broadcasted_iota(jnp.int32, (A_BLOCK, 2 * A_BLOCK), 0)
    kj = lax.broadcasted_iota(jnp.int32, (A_BLOCK, 2 * A_BLOCK), 1)
    rel = qi + A_BLOCK - kj
    return (rel >= 0) & (rel <= N_KEYS) & (jnp.logical_not(first) | (kj >= A_BLOCK))


def _attn_specs(nb, rev):
    per = 2 if nb % 2 == 0 else 1
    steps = nb // per
    n_of = (lambda i: steps - 1 - i) if rev else (lambda i: i)
    cur = pl.BlockSpec((None, per * A_BLOCK, A_WIDTH), lambda r, i: (r, n_of(i), 0))
    prev = pl.BlockSpec((None, A_BLOCK, A_WIDTH), lambda r, i: (r, jnp.maximum(per * n_of(i) - 1, 0), 0))
    bias = pl.BlockSpec((A_HEADS, A_BLOCK, 2 * A_BLOCK), lambda r, i: (0, 0, 0))
    return per, steps, cur, prev, bias


def _attn_fwd(q, k, v, bias, exchange=None):
    dil, t_len, _ = q.shape
    per, steps, cur, prev, bias_spec = _attn_specs(t_len // A_BLOCK, False)
    scale = A_HD ** -0.5

    def body(q_ref, kp_ref, kc_ref, vp_ref, vc_ref, b_ref, o_ref, l_ref):
        lane = lax.broadcasted_iota(jnp.int32, (1, 128), 1)
        for sub in range(per):
            rows = slice(sub * A_BLOCK, (sub + 1) * A_BLOCK)
            before = slice((sub - 1) * A_BLOCK, sub * A_BLOCK)
            mask = _attn_mask((pl.program_id(1) == 0) if sub == 0 else False)
            for hp in range(A_HEADS // 2):
                sl = slice(hp * 128, (hp + 1) * 128)
                qp = q_ref[rows, sl]
                kw = jnp.concatenate([kp_ref[:, sl] if sub == 0 else kc_ref[before, sl], kc_ref[rows, sl]], axis=0).astype(BF)
                vw = jnp.concatenate([vp_ref[:, sl] if sub == 0 else vc_ref[before, sl], vc_ref[rows, sl]], axis=0).astype(BF)
                outs, lses = [], []
                for j in range(2):
                    hm = (lane < 64) if j == 0 else (lane >= 64)
                    s = _bdot_nt(jnp.where(hm, qp, 0.0), kw) * scale
                    s = jnp.where(mask, s + b_ref[2 * hp + j], NEG)
                    m = jnp.max(s, axis=-1, keepdims=True)
                    p = jnp.exp(s - m)
                    den = jnp.sum(p, axis=-1, keepdims=True)
                    outs.append(_bdot(p, vw) / den)
                    lses.append(m + jnp.log(den))
                hm0 = lane < 64
                o_ref[rows, sl] = jnp.where(hm0, outs[0], outs[1])
                l_ref[rows, sl] = jnp.where(hm0, lses[0], lses[1])

    return _call_with_exchange(body, f"attn_fwd_d{dil}", (dil, steps), [cur, prev, cur, prev, cur, bias_spec], [cur, cur],
                               [_sds(dil, t_len, A_WIDTH)] * 2, [], (q, k, k, v, v, bias), exchange)


def _attn_bwd(q, k, v, bias, o, l, do, dl, exchange=None):
    dil, t_len, _ = q.shape
    per, steps, cur, prev, bias_spec = _attn_specs(t_len // A_BLOCK, True)
    scale = A_HD ** -0.5

    def body(q_ref, kp_ref, kc_ref, vp_ref, vc_ref, b_ref, o_ref, l_ref, do_ref, dl_ref,
             dq_ref, dk_ref, dv_ref, db_ref, ck_ref, cv_ref):
        _zero_at_first([ck_ref, cv_ref], axis=1)

        @pl.when((pl.program_id(0) == 0) & (pl.program_id(1) == 0))
        def _():
            db_ref[...] = jnp.zeros_like(db_ref)

        lane = lax.broadcasted_iota(jnp.int32, (1, 128), 1)
        for hp in range(A_HEADS // 2):
            sl = slice(hp * 128, (hp + 1) * 128)
            to_prev_k, to_prev_v = ck_ref[:, sl], cv_ref[:, sl]
            for sub in range(per - 1, -1, -1):
                rows = slice(sub * A_BLOCK, (sub + 1) * A_BLOCK)
                before = slice((sub - 1) * A_BLOCK, sub * A_BLOCK)
                mask = _attn_mask((pl.program_id(1) == steps - 1) if sub == 0 else False)
                qp = q_ref[rows, sl]
                kw = jnp.concatenate([kp_ref[:, sl] if sub == 0 else kc_ref[before, sl], kc_ref[rows, sl]], axis=0).astype(BF)
                vw = jnp.concatenate([vp_ref[:, sl] if sub == 0 else vc_ref[before, sl], vc_ref[rows, sl]], axis=0).astype(BF)
                op, lp, dop, dlp = o_ref[rows, sl], l_ref[rows, sl], do_ref[rows, sl], dl_ref[rows, sl]
                dq_acc = jnp.zeros((A_BLOCK, 128), F32)
                dk_acc = jnp.zeros((2 * A_BLOCK, 128), F32)
                dv_acc = jnp.zeros((2 * A_BLOCK, 128), F32)
                for j in range(2):
                    hm = (lane < 64) if j == 0 else (lane >= 64)
                    qm = jnp.where(hm, qp, 0.0)
                    s = _bdot_nt(qm, kw) * scale
                    s = jnp.where(mask, s + b_ref[2 * hp + j], NEG)
                    lse = jnp.max(jnp.where(hm, lp, NEG), axis=-1, keepdims=True)
                    p = jnp.exp(s - lse)
                    do_h = jnp.where(hm, dop, 0.0)
                    dd = jnp.sum(do_h * op, axis=-1, keepdims=True)
                    dlse = jnp.sum(jnp.where(hm, dlp, 0.0), axis=-1, keepdims=True)
                    ds = p * (_bdot_nt(do_h, vw) - dd + dlse)
                    dv_acc = dv_acc + _bdot_tn(p, do_h)
                    dq_acc = dq_acc + jnp.where(hm, _bdot(ds, kw), 0.0) * scale
                    dk_acc = dk_acc + _bdot_tn(ds, qm) * scale
                    db_ref[2 * hp + j] += ds
                dq_ref[rows, sl] = dq_acc
                dk_ref[rows, sl] = dk_acc[A_BLOCK:] + to_prev_k
                dv_ref[rows, sl] = dv_acc[A_BLOCK:] + to_prev_v
                to_prev_k, to_prev_v = dk_acc[:A_BLOCK], dv_acc[:A_BLOCK]
            ck_ref[:, sl] = to_prev_k
            cv_ref[:, sl] = to_prev_v

    return _call_with_exchange(
        body, f"attn_bwd_d{dil}", (dil, steps), [cur, prev, cur, prev, cur, bias_spec, cur, cur, cur, cur],
        [cur, cur, cur, bias_spec], [_sds(dil, t_len, A_WIDTH)] * 3 + [_sds(A_HEADS, A_BLOCK, 2 * A_BLOCK)],
        [pltpu.VMEM((A_BLOCK, A_WIDTH), F32)] * 2, (q, k, k, v, v, bias, o, l, do, dl), exchange)


def _attn_bias(rel_bias, table):
    def body(rb_ref, t_ref, *o_refs):
        for c in range(3):
            t = t_ref[c]
            acc = [jnp.zeros((A_BLOCK, 2 * A_BLOCK), F32) for _ in range(A_HEADS)]
            for b in range(REL_BUCKETS):
                hit = t == b
                acc = [jnp.where(hit, rb_ref[b, h], acc[h]) for h in range(A_HEADS)]
            for h in range(A_HEADS):
                o_refs[c][h] = acc[h]

    return pl.pallas_call(body, name="attn_bias", out_shape=[_sds(A_HEADS, A_BLOCK, 2 * A_BLOCK)] * 3,
                          in_specs=[pl.BlockSpec(memory_space=pltpu.SMEM), pl.BlockSpec(memory_space=pltpu.VMEM)],
                          compiler_params=pltpu.CompilerParams(vmem_limit_bytes=VMEM_LIMIT_BYTES))(rel_bias, table)


def _rel_bias_grad(dbs, idx_rows):
    n = A_BLOCK * 2 * A_BLOCK

    def body(d0_ref, d1_ref, d2_ref, idx_ref, o_ref):
        bucket = lax.broadcasted_iota(jnp.int32, (REL_BUCKETS, n), 0).astype(F32)
        acc = jnp.zeros((A_HEADS, REL_BUCKETS), F32)
        for c, db_ref in enumerate((d0_ref, d1_ref, d2_ref)):
            onehot = (idx_ref[c:c + 1, :] == bucket).astype(F32)
            acc = acc + lax.dot_general(db_ref[...], onehot, (((1,), (1,)), ((), ())), precision=HI, preferred_element_type=F32)
        o_ref[...] = acc

    return pl.pallas_call(body, name="rel_bias_grad", out_shape=_sds(A_HEADS, REL_BUCKETS),
                          compiler_params=pltpu.CompilerParams(vmem_limit_bytes=VMEM_LIMIT_BYTES))(
                              *[d.reshape(A_HEADS, n) for d in dbs], idx_rows)


def _s5_param_fn(a_re, a_im, log_dt, bt_re, bt_im):
    dt = jnp.exp(log_dt)
    mag = jnp.exp(dt * a_re)
    abar_r, abar_i = mag * jnp.cos(dt * a_im), mag * jnp.sin(dt * a_im)
    den = a_re * a_re + a_im * a_im
    fr = ((abar_r - 1.0) * a_re + abar_i * a_im) / den
    fi = (abar_i * a_re - (abar_r - 1.0) * a_im) / den
    row = lax.broadcasted_iota(jnp.int32, (B_WIDTH, B_GROUPS), 0)
    grp = lax.broadcasted_iota(jnp.int32, (B_WIDTH, B_GROUPS), 1)
    expand = ((row // B_GROUP) == grp).astype(F32)
    fr_e, fi_e = _hdot(expand, fr), _hdot(expand, fi)
    return abar_r, abar_i, fr_e * bt_re - fi_e * bt_im, fr_e * bt_im + fi_e * bt_re


def _s5_params(a_re, a_im, log_dt, bt_re, bt_im):
    def body(ar, ai, ld, br, bi, o1, o2, o3, o4):
        o1[...], o2[...], o3[...], o4[...] = _s5_param_fn(ar[...], ai[...], ld[...], br[...], bi[...])

    return pl.pallas_call(body, name="s5_params",
                          out_shape=[_sds(B_GROUPS, B_STATE)] * 2 + [_sds(B_WIDTH, B_STATE)] * 2)(a_re, a_im, log_dt, bt_re, bt_im)


def _s5_params_bwd(a_re, a_im, log_dt, bt_re, bt_im, d1, d2, d3, d4):
    def body(ar, ai, ld, br, bi, c1, c2, c3, c4, o1, o2, o3, o4, o5):
        _, vjp = jax.vjp(_s5_param_fn, ar[...], ai[...], ld[...], br[...], bi[...])
        o1[...], o2[...], o3[...], o4[...], o5[...] = vjp((c1[...], c2[...], c3[...], c4[...]))

    return pl.pallas_call(body, name="s5_params_bwd",
                          out_shape=[_sds(B_GROUPS, B_STATE)] * 2 + [_sds(B_GROUPS, 1)] + [_sds(B_WIDTH, B_STATE)] * 2,
                          )(a_re, a_im, log_dt, bt_re, bt_im, d1, d2, d3, d4)


def _pick_row(x, r):
    rows = lax.broadcasted_iota(jnp.int32, x.shape, 0)
    return jnp.sum(jnp.where(rows == r, x, 0.0), axis=0, keepdims=True)


S5_SEG = 8
S5_STEPS = 32
S5_WIDTH = S5_TILES * S5_LANES


def _seg_rows(block):
    return jnp.swapaxes(block, 0, 1).reshape(block.shape[1] * S5_SEG, block.shape[2])


def _seg_block(rows):
    return jnp.swapaxes(rows.reshape(rows.shape[0] // S5_SEG, S5_SEG, rows.shape[1]), 0, 1)


def _seq_specs(n_i, rev):
    at = (lambda i: n_i - 1 - i) if rev else (lambda i: i)
    seg = pl.BlockSpec((S5_SEG, S5_STEPS, B_WIDTH), lambda i: (0, at(i), 0))
    x_spec = pl.BlockSpec((S5_SEG * S5_STEPS, S5_WIDTH), lambda i: (at(i), 0))
    return seg, x_spec, _fix((S5_TILES, 128, S5_LANES)), _fix((S5_TILES, S5_LANES, 128)), _fix((1, S5_WIDTH)), _fix((S5_SEG, S5_WIDTH))


def _tile_dots(dot, lhs, w_ref, lhs_width):
    return jnp.concatenate([dot(lhs[:, t * lhs_width:(t + 1) * lhs_width], w_ref[t]) for t in range(S5_TILES)], axis=1)


def _s5_entries(name, end_r, end_i, abr, abi, steps, reverse):
    def body(er_ref, ei_ref, ar_ref, ai_ref, or_ref, oi_ref):
        pr, pi_ = ar_ref[...], ai_ref[...]
        for _ in range(int(math.log2(steps))):
            pr, pi_ = pr * pr - pi_ * pi_, 2.0 * pr * pi_
        er, ei = er_ref[...], ei_ref[...]
        rows = lax.broadcasted_iota(jnp.int32, er.shape, 0)
        cr, ci = jnp.zeros_like(pr), jnp.zeros_like(pr)
        out_r, out_i = jnp.zeros_like(er), jnp.zeros_like(er)
        for g in (range(S5_SEG - 2, -1, -1) if reverse else range(1, S5_SEG)):
            src = g + 1 if reverse else g - 1
            cr, ci = _pick_row(er, src) + pr * cr - pi_ * ci, _pick_row(ei, src) + pr * ci + pi_ * cr
            out_r, out_i = jnp.where(rows == g, cr, out_r), jnp.where(rows == g, ci, out_i)
        or_ref[...] = out_r
        oi_ref[...] = out_i

    return pl.pallas_call(body, name=name, out_shape=[_sds(*end_r.shape)] * 2)(end_r, end_i, abr, abi)


def _s5_seq_fwd(u, btr, bti, ctr, cti, abr, abi, entry, store, exchange=None):
    s_len = u.shape[0]
    seg_len = s_len // S5_SEG
    n_i = seg_len // S5_STEPS
    rows = S5_SEG * S5_STEPS

    def body(u_ref, btr_ref, bti_ref, ctr_ref, cti_ref, ar_ref, ai_ref, er_ref, ei_ref, *rest):
        if store:
            xr_ref, xi_ref, y_ref, endr_ref, endi_ref, sr_ref, si_ref = rest
        else:
            endr_ref, endi_ref, sr_ref, si_ref = rest
        i = pl.program_id(0)

        @pl.when(i == 0)
        def _():
            sr_ref[...] = er_ref[...]
            si_ref[...] = ei_ref[...]

        ar = jnp.broadcast_to(ar_ref[...], (S5_SEG, S5_WIDTH))
        ai = jnp.broadcast_to(ai_ref[...], (S5_SEG, S5_WIDTH))
        ub = _seg_rows(u_ref[...])
        br, bi = _tile_dots(_bdot, ub, btr_ref, 128), _tile_dots(_bdot, ub, bti_ref, 128)
        sr, si = sr_ref[...], si_ref[...]
        for s in range(S5_STEPS):
            at = slice(S5_SEG * s, S5_SEG * (s + 1))
            sr, si = ar * sr - ai * si + br[at], ar * si + ai * sr + bi[at]
            if store:
                xr_ref[at, :] = sr
                xi_ref[at, :] = si
        sr_ref[...] = sr
        si_ref[...] = si
        if store:
            y_ref[...] = _seg_block(_tile_dots(_bdot, xr_ref[...], ctr_ref, S5_LANES) - _tile_dots(_bdot, xi_ref[...], cti_ref, S5_LANES))

        @pl.when(i == n_i - 1)
        def _():
            endr_ref[...] = sr
            endi_ref[...] = si

    seg, x_spec, b_spec, c_spec, a_spec, e_spec = _seq_specs(n_i, False)
    ends = [_sds(S5_SEG, S5_WIDTH)] * 2
    full = [_sds(s_len, S5_WIDTH)] * 2 + [_sds(S5_SEG, seg_len, B_WIDTH)] if store else []
    return _call_with_exchange(
        body, "s5_scan_fwd" if store else "s5_ends_fwd", (n_i,),
        [seg, b_spec, b_spec, c_spec, c_spec, a_spec, a_spec, e_spec, e_spec],
        ([x_spec, x_spec, seg] if store else []) + [e_spec, e_spec], full + ends,
        [pltpu.VMEM((S5_SEG, S5_WIDTH), F32)] * 2,
        (u.reshape(S5_SEG, seg_len, B_WIDTH), btr, bti, ctr, cti, abr, abi, *entry), exchange)


def _s5_seq_bwd(dy, xr, xi, u, btr, bti, ctr, cti, abr, abi, g_entry, x_entry, full, exchange=None):
    s_len = dy.shape[0]
    seg_len = s_len // S5_SEG
    n_i = seg_len // S5_STEPS
    rows = S5_SEG * S5_STEPS

    def body(*refs):
        if full:
            (dy_ref, btr_ref, bti_ref, ctr_ref, cti_ref, ar_ref, ai_ref, ger_ref, gei_ref,
             xr_ref, xi_ref, xrp_ref, xip_ref, xer_ref, xei_ref, u_ref,
             du_ref, dbtr_ref, dbti_ref, dctr_ref, dcti_ref, dar_ref, dai_ref, str_ref, sti_ref,
             sr_ref, si_ref, gr_s, gi_s) = refs
        else:
            (dy_ref, btr_ref, bti_ref, ctr_ref, cti_ref, ar_ref, ai_ref, ger_ref, gei_ref, str_ref, sti_ref, sr_ref, si_ref) = refs
        i = pl.program_id(0)

        @pl.when(i == 0)
        def _():
            sr_ref[...] = ger_ref[...]
            si_ref[...] = gei_ref[...]
            if full:
                for r in (dbtr_ref, dbti_ref, dctr_ref, dcti_ref, dar_ref, dai_ref):
                    r[...] = jnp.zeros_like(r)

        ar = jnp.broadcast_to(ar_ref[...], (S5_SEG, S5_WIDTH))
        ai = -jnp.broadcast_to(ai_ref[...], (S5_SEG, S5_WIDTH))
        dyb = _seg_rows(dy_ref[...])
        gr, gi = _tile_dots(_bdot_nt, dyb, ctr_ref, 128), -_tile_dots(_bdot_nt, dyb, cti_ref, 128)
        sr, si = sr_ref[...], si_ref[...]
        for s in range(S5_STEPS - 1, -1, -1):
            at = slice(S5_SEG * s, S5_SEG * (s + 1))
            sr, si = ar * sr - ai * si + gr[at], ar * si + ai * sr + gi[at]
            if full:
                gr_s[at, :] = sr
                gi_s[at, :] = si
        sr_ref[...] = sr
        si_ref[...] = si

        @pl.when(i == n_i - 1)
        def _():
            str_ref[...] = sr
            sti_ref[...] = si

        if full:
            g_r, g_i = gr_s[...], gi_s[...]
            du_ref[...] = _seg_block(_tile_dots(_bdot_nt, g_r, btr_ref, S5_LANES) + _tile_dots(_bdot_nt, g_i, bti_ref, S5_LANES))
            ub = _seg_rows(u_ref[...])
            xr_b, xi_b = xr_ref[...], xi_ref[...]
            for t in range(S5_TILES):
                lanes, cols = slice(t * S5_LANES, (t + 1) * S5_LANES), slice(t * 128, (t + 1) * 128)
                dbtr_ref[t] += _bdot_tn(ub[:, cols], g_r[:, lanes])
                dbti_ref[t] += _bdot_tn(ub[:, cols], g_i[:, lanes])
                dctr_ref[t] += _bdot_tn(xr_b[:, lanes], dyb[:, cols])
                dcti_ref[t] -= _bdot_tn(xi_b[:, lanes], dyb[:, cols])
            first = i == n_i - 1
            xpr = jnp.concatenate([jnp.where(first, xer_ref[...], xrp_ref[...]), xr_b[:rows - S5_SEG]], axis=0)
            xpi = jnp.concatenate([jnp.where(first, xei_ref[...], xip_ref[...]), xi_b[:rows - S5_SEG]], axis=0)
            dar_ref[...] += jnp.sum(g_r * xpr + g_i * xpi, axis=0, keepdims=True)
            dai_ref[...] += jnp.sum(g_i * xpr - g_r * xpi, axis=0, keepdims=True)

    seg, x_spec, b_spec, c_spec, a_spec, e_spec = _seq_specs(n_i, True)
    halo = pl.BlockSpec((S5_SEG, S5_WIDTH), lambda i: (jnp.maximum((n_i - 1 - i) * S5_STEPS - 1, 0), 0))
    starts = [_sds(S5_SEG, S5_WIDTH)] * 2
    in_specs = [seg, b_spec, b_spec, c_spec, c_spec, a_spec, a_spec, e_spec, e_spec]
    args = [dy.reshape(S5_SEG, seg_len, B_WIDTH), btr, bti, ctr, cti, abr, abi, *g_entry]
    state = [pltpu.VMEM((S5_SEG, S5_WIDTH), F32)] * 2
    if not full:
        return _call_with_exchange(body, "s5_starts_bwd", (n_i,), in_specs, [e_spec, e_spec], starts, state, args, None)
    return _call_with_exchange(
        body, "s5_scan_bwd", (n_i,),
        in_specs + [x_spec, x_spec, halo, halo, e_spec, e_spec, seg],
        [seg, b_spec, b_spec, c_spec, c_spec, a_spec, a_spec, e_spec, e_spec],
        [_sds(S5_SEG, seg_len, B_WIDTH)] + [_sds(S5_TILES, 128, S5_LANES)] * 2 + [_sds(S5_TILES, S5_LANES, 128)] * 2
        + [_sds(1, S5_WIDTH)] * 2 + starts,
        state + [pltpu.VMEM((rows, S5_WIDTH), F32)] * 2,
        args + [xr, xi, xr, xi, *x_entry, u.reshape(S5_SEG, seg_len, B_WIDTH)], exchange)


def _blockdiag_b(bbar_t):
    blocks = bbar_t.reshape(S5_TILES, 8, B_GROUP, B_STATE)
    return jnp.einsum('jgmp,gh->jgmhp', blocks, jnp.eye(8, dtype=F32)).reshape(S5_TILES, 128, S5_LANES)


def _blockdiag_b_t(d):
    return jnp.einsum('jgmgp->jgmp', d.reshape(S5_TILES, 8, B_GROUP, 8, B_STATE)).reshape(B_WIDTH, B_STATE)


def _blockdiag_c(c):
    blocks = c.reshape(S5_TILES, 8, B_GROUP, B_STATE)
    return jnp.einsum('jgmp,gh->jhpgm', blocks, jnp.eye(8, dtype=F32)).reshape(S5_TILES, S5_LANES, 128)


def _blockdiag_c_t(d):
    return jnp.einsum('jgpgm->jgmp', d.reshape(S5_TILES, 8, B_STATE, 8, B_GROUP)).reshape(B_GROUPS, B_GROUP, B_STATE)


def _l0_out(os, ls, ga, gb, ypre, u, x, d_skip, glu_w, glu_b, w_out, post_g, gate, exchange=None):
    s_len = x.shape[0]

    def body(o0, o1, o2, l0, l1, l2, ga_ref, gb_ref, yp_ref, u_ref, x_ref, d_ref, gw_ref, gbias_ref, w_ref, pg_ref, gt_ref, x1_ref, y_ref):
        oa = _merge_gate(*[_from_res(r[...]) for r in (o0, o1, o2, l0, l1, l2)], ga_ref[...])
        yb = _s5_gelu(yp_ref[...], u_ref[...], d_ref[...])
        ob = _s5_glu(yb, _bdot(yb, gw_ref[...]) + gbias_ref[...], gb_ref[...])
        y = _bdot(oa, w_ref[0:512, :]) + _bdot(ob, w_ref[512:1024, :])
        y_ref[...] = y
        x1_ref[...] = _post_res(y, x_ref[...], pg_ref[...], gt_ref[...])

    vec, half = _fix((1, D_MODEL)), _fix((1, 512))
    return _call_with_exchange(
        body, "l0_out", (s_len // TM,),
        [_res_spec(d) for d in DILATIONS] * 2 + [_row(TM, 512)] * 4
        + [_row(TM, D_MODEL), half, _fix((512, 512)), half, _fix((D_MODEL, D_MODEL)), vec, vec],
        [_row(TM, D_MODEL)] * 2, [_sds(s_len, D_MODEL)] * 2, [],
        (*os, *ls, ga, gb, ypre, u, x, d_skip, glu_w, glu_b, w_out, post_g, gate), exchange)


def _l0_out_bwd(os, ls, ga, gb, ypre, u, x, y, d_skip, glu_w, glu_b, w_out, post_g, gate, dx1, exchange=None):
    s_len = x.shape[0]

    def body(o0, o1, o2, l0, l1, l2, ga_ref, gb_ref, yp_ref, u_ref, x_ref, y_ref, d_ref, gw_ref, gbias_ref, w_ref, pg_ref, gt_ref, dx1_ref,
             do0, do1, do2, dl0, dl1, dl2, dga_ref, dgb_ref, dyp_ref, du_ref, dd_ref, dgw_ref, dgbias_ref, dw_ref, dpg_ref, dgt_ref):
        _zero_at_first([dd_ref, dgw_ref, dgbias_ref, dw_ref, dpg_ref, dgt_ref])
        _, vjp2 = jax.vjp(_post_res, y_ref[...], x_ref[...], pg_ref[...], gt_ref[...])
        dy, _, dpg, dgt = vjp2(dx1_ref[...])
        _acc(dpg_ref, dpg)
        _acc(dgt_ref, dgt)
        oa, vjp_a = jax.vjp(_merge_gate, *[_from_res(r[...]) for r in (o0, o1, o2, l0, l1, l2)], ga_ref[...])
        yb, vjp_g = jax.vjp(_s5_gelu, yp_ref[...], u_ref[...], d_ref[...])
        gl = _bdot(yb, gw_ref[...]) + gbias_ref[...]
        ob, vjp_b = jax.vjp(_s5_glu, yb, gl, gb_ref[...])
        dw_ref[0:512, :] += _bdot_tn(oa, dy)
        dw_ref[512:1024, :] += _bdot_tn(ob, dy)
        d1, d2, d3, e1, e2, e3, dga = vjp_a(_bdot_nt(dy, w_ref[0:512, :]))
        for ref, val, d in zip((do0, do1, do2, dl0, dl1, dl2), (d1, d2, d3, e1, e2, e3), DILATIONS * 2):
            ref[...] = _to_res(val, d)
        dga_ref[...] = dga
        dyb, dgl, dgb = vjp_b(_bdot_nt(dy, w_ref[512:1024, :]))
        dgb_ref[...] = dgb
        dgw_ref[...] += _bdot_tn(yb, dgl)
        _acc(dgbias_ref, jnp.sum(dgl, axis=0, keepdims=True))
        dyp, du, dd = vjp_g(dyb + _bdot_nt(dgl, gw_ref[...]))
        dyp_ref[...] = dyp
        du_ref[...] = du
        _acc(dd_ref, dd)

    vec, half = _fix((1, D_MODEL)), _fix((1, 512))
    r5, r10 = _row(TM, 512), _row(TM, D_MODEL)
    res6 = [_res_spec(d) for d in DILATIONS] * 2
    return _call_with_exchange(
        body, "l0_out_bwd", (s_len // TM,),
        res6 + [r5] * 4 + [r10, r10, half, _fix((512, 512)), half, _fix((D_MODEL, D_MODEL)), vec, vec, r10],
        res6 + [r5] * 4 + [half, _fix((512, 512)), half, _fix((D_MODEL, D_MODEL)), vec, vec],
        [_sds(*_res_shape(s_len, d)) for d in DILATIONS] * 2 + [_sds(s_len, 512)] * 4
        + [_sds(1, 512), _sds(512, 512), _sds(1, 512), _sds(D_MODEL, D_MODEL), _sds(1, D_MODEL), _sds(1, D_MODEL)],
        [], (*os, *ls, ga, gb, ypre, u, x, y, d_skip, glu_w, glu_b, w_out, post_g, gate, dx1), exchange)


def _l1_front(x, pre_g, scale, shift, w_in):
    s_len = x.shape[0]

    def body(x_ref, g_ref, sc_ref, sh_ref, w_ref, raw_ref, gate_ref, ba_ref, h_ref):
        hb = _pre_mod(x_ref[...], g_ref[...], sc_ref[...], sh_ref[...]).astype(BF)
        h_ref[...] = hb
        z = jnp.dot(hb, w_ref[...], preferred_element_type=F32)
        raw_ref[...] = z[:, 0:QKV]
        gate_ref[...] = z[:, QKV:QKV + 1024]
        ba_ref[...] = z[:, QKV + 1024:C_IN_PAD]

    vec = _fix((1, D_MODEL))
    return pl.pallas_call(
        body, name="l1_front", grid=(s_len // TM,),
        in_specs=[_row(TM, D_MODEL), vec, vec, vec, _fix((D_MODEL, C_IN_PAD))],
        out_specs=[_row(TM, QKV), _row(TM, 1024), _row(TM, 128), _row(TM, D_MODEL)],
        out_shape=[_sds(s_len, QKV), _sds(s_len, 1024), _sds(s_len, 128), _sds(s_len, D_MODEL, dtype=BF)],
        compiler_params=_cp("arbitrary"),
    )(x, pre_g, scale, shift, w_in)


def _bg_fn(ba, alog_row, dtb_row):
    lane = lax.broadcasted_iota(jnp.int32, (1, 128), 1)
    g = -jnp.exp(alog_row) * jax.nn.softplus(ba + dtb_row)
    return jnp.where(lane < C_HEADS, jax.nn.sigmoid(ba), jnp.where(lane < 2 * C_HEADS, g, 0.0))


def _act_q(c):
    q = jax.nn.silu(c)
    return q * lax.rsqrt(jnp.sum(q * q, axis=-1, keepdims=True) + EPS) * (C_DK ** -0.5)


def _act_k(c):
    k = jax.nn.silu(c)
    return k * lax.rsqrt(jnp.sum(k * k, axis=-1, keepdims=True) + EPS)


def _act_of(s):
    return _act_q if s < 8 else (_act_k if s < 16 else jax.nn.silu)


def _conv_taps(prev8, tile_ref, sl, next8=None):
    rows = tile_ref.shape[0]
    head = jnp.concatenate([prev8, tile_ref[0:8, sl]], axis=0)
    tail = None if next8 is None else jnp.concatenate([tile_ref[rows - 8:rows, sl], next8], axis=0)
    taps = []
    for j in range(C_CONV):
        shift = C_CONV - 1 - j
        pieces = [head[8:] if shift == 0 else pltpu.roll(head, shift, 0)[8:], tile_ref[pl.ds(8 - shift, rows - 8), sl]]
        if tail is not None:
            pieces.append(tail[8:] if shift == 0 else pltpu.roll(tail, shift, 0)[8:])
        taps.append(jnp.concatenate(pieces, axis=0))
    return taps


def _gdn_prep(raw, ba, conv_w, alog_row, dtb_row):
    s_len = raw.shape[0]

    def body(raw_ref, halo_ref, ba_ref, w_ref, al_ref, dt_ref, qkv_ref, bg_ref):
        bg_ref[...] = _bg_fn(ba_ref[...], al_ref[...], dt_ref[...])
        has_prev = (pl.program_id(0) > 0).astype(F32)
        for s in range(24):
            sl = slice(s * 128, (s + 1) * 128)
            taps = _conv_taps(halo_ref[:, sl] * has_prev, raw_ref, sl)
            conv = w_ref[3:4, sl] * taps[3]
            for j in range(3):
                conv = conv + w_ref[j:j + 1, sl] * taps[j]
            qkv_ref[:, sl] = _act_of(s)(conv)

    halo = pl.BlockSpec((8, QKV), lambda i: (jnp.maximum(i * (TM // 8) - 1, 0), 0))
    row128 = _fix((1, 128))
    return pl.pallas_call(
        body, name="gdn_prep", grid=(s_len // TM,),
        in_specs=[_row(TM, QKV), halo, _row(TM, 128), _fix((C_CONV, QKV)), row128, row128],
        out_specs=[_row(TM, QKV), _row(TM, 128)],
        out_shape=[_sds(s_len, QKV), _sds(s_len, 128)],
        compiler_params=_cp("arbitrary"),
    )(raw, raw, ba, conv_w, alog_row, dtb_row)


def _gdn_prep_bwd(raw, ba, conv_w, alog_row, dtb_row, dq, dk, dv, dbg):
    s_len = raw.shape[0]
    n_tiles = s_len // TM

    def body(raw_ref, prev_ref, next_ref, ba_ref, w_ref, al_ref, dt_ref, dq_ref, dqn_ref, dk_ref, dkn_ref, dv_ref, dvn_ref, dbg_ref,
             draw_ref, dba_ref, dw_ref, dal_ref, ddt_ref, dconv_ref):
        _zero_at_first([dw_ref, dal_ref, ddt_ref])
        i = pl.program_id(0)
        _, vjp_bg = jax.vjp(_bg_fn, ba_ref[...], al_ref[...], dt_ref[...])
        dba, dal, ddt = vjp_bg(dbg_ref[...])
        dba_ref[...] = dba
        _acc(dal_ref, dal)
        _acc(ddt_ref, ddt)
        has_prev = (i > 0).astype(F32)
        has_next = (i < n_tiles - 1).astype(F32)
        ct_refs = ((dq_ref, dqn_ref), (dk_ref, dkn_ref), (dv_ref, dvn_ref))
        for s in range(24):
            sl = slice(s * 128, (s + 1) * 128)
            hl = slice((s % 8) * 128, (s % 8 + 1) * 128)
            tile_ref, nxt_ref = ct_refs[s // 8]
            taps = _conv_taps(prev_ref[:, sl] * has_prev, raw_ref, sl, next_ref[:, sl] * has_next)
            conv = w_ref[3:4, sl] * taps[3]
            for j in range(3):
                conv = conv + w_ref[j:j + 1, sl] * taps[j]
            ct = jnp.concatenate([tile_ref[:, hl], nxt_ref[:, hl] * has_next], axis=0)
            _, vjp_act = jax.vjp(_act_of(s), conv)
            dconv, = vjp_act(ct)
            dconv_ref[...] = dconv
            draw = w_ref[3:4, sl] * dconv[:TM]
            for j in range(3):
                draw = draw + w_ref[j:j + 1, sl] * dconv_ref[pl.ds(3 - j, TM), :]
            draw_ref[:, sl] = draw
            for j in range(4):
                dw_ref[j:j + 1, sl] += jnp.sum(dconv[:TM] * taps[j][:TM], axis=0, keepdims=True)

    prev = pl.BlockSpec((8, QKV), lambda i: (jnp.maximum(i * (TM // 8) - 1, 0), 0))
    nxt = lambda n: pl.BlockSpec((8, n), lambda i: (jnp.minimum((i + 1) * (TM // 8), s_len // 8 - 1), 0))
    row128 = _fix((1, 128))
    ct_specs = [_row(TM, 1024), nxt(1024)] * 3
    return pl.pallas_call(
        body, name="gdn_prep_bwd", grid=(n_tiles,),
        in_specs=[_row(TM, QKV), prev, nxt(QKV), _row(TM, 128), _fix((C_CONV, QKV)), row128, row128] + ct_specs + [_row(TM, 128)],
        out_specs=[_row(TM, QKV), _row(TM, 128), _fix((C_CONV, QKV)), row128, row128],
        out_shape=[_sds(s_len, QKV), _sds(s_len, 128), _sds(C_CONV, QKV), _sds(1, 128), _sds(1, 128)],
        scratch_shapes=[pltpu.VMEM((TM + 8, 128), F32)],
        compiler_params=_cp("arbitrary"),
    )(raw, raw, raw, ba, conv_w, alog_row, dtb_row, dq, dq, dk, dk, dv, dv, dbg)


def _tein(eq, a, b):
    return jnp.einsum(eq, a, b, precision=lax.Precision.HIGH, preferred_element_type=F32)


def _unit_lower_inverse(lower):
    ri = lax.broadcasted_iota(jnp.int32, (C_CHUNK, C_CHUNK), 0)
    ci = lax.broadcasted_iota(jnp.int32, (C_CHUNK, C_CHUNK), 1)
    eye = (ri == ci).astype(F32)[None]
    mm = functools.partial(_bein, 'hij,hjk->hik')
    same_block = lambda size: (ri // size == ci // size)[None]
    n_mat = jnp.where(same_block(4), -lower, 0.0)
    inv = mm(eye + n_mat, eye + mm(n_mat, n_mat))
    for size in (4, 8, 16, 32):
        below = jnp.where(same_block(2 * size) & jnp.logical_not(same_block(size)), lower, 0.0)
        inv = inv - mm(inv, mm(below, inv))
    inv = _tein('hij,hjk->hik', inv, 2.0 * eye - _tein('hij,hjk->hik', eye + lower, inv))
    return jnp.where((ri >= ci)[None], inv, 0.0)


@jax.custom_vjp
def _known_inverse(lower, inv):
    return inv


def _known_inverse_fwd(lower, inv):
    return inv, inv


def _known_inverse_bwd(inv, d_inv):
    d_lower = -_bein('hik,hjk->hij', _bein('hji,hjk->hik', inv, d_inv), inv)
    return d_lower, jnp.zeros_like(inv)


_known_inverse.defvjp(_known_inverse_fwd, _known_inverse_bwd)


def _gdn_local(q, k, v, bgs, inv_known=None):
    lane = lax.broadcasted_iota(jnp.int32, (1, 128), 1)
    ri = lax.broadcasted_iota(jnp.int32, (C_CHUNK, C_CHUNK), 0)
    ci = lax.broadcasted_iota(jnp.int32, (C_CHUNK, C_CHUNK), 1)
    row_id = lax.broadcasted_iota(jnp.int32, (128, C_CHUNK), 0)
    beta, gc, gcj = [], [], []
    for bg in bgs:
        gc_t = _hdot((ri >= ci).astype(F32), bg)
        gc_rows = gc_t.T
        for h in range(C_HEADS):
            beta.append(jnp.sum(jnp.where(lane == h, bg, 0.0), axis=-1, keepdims=True))
            gc.append(jnp.sum(jnp.where(lane == C_HEADS + h, gc_t, 0.0), axis=-1, keepdims=True))
            gcj.append(jnp.sum(jnp.where(row_id == C_HEADS + h, gc_rows, 0.0), axis=0, keepdims=True))
    beta, gc, gcj = jnp.stack(beta, axis=0), jnp.stack(gc, axis=0), jnp.stack(gcj, axis=0)
    tril, strict = (ri >= ci)[None], (ri > ci)[None]
    decay = jnp.exp(jnp.where(tril, gc - gcj, -1e30))
    kb = k * beta
    lower = jnp.where(strict, _bein('hid,hjd->hij', kb, k) * decay, 0.0)
    inv = _unit_lower_inverse(lower) if inv_known is None else _known_inverse(lower, inv_known)
    egc = jnp.exp(gc)
    u_c = _bein('hij,hjd->hid', inv, v * beta)
    w_c = _bein('hij,hjd->hid', inv, kb * egc)
    aqk = _bein('hid,hjd->hij', q, k) * decay
    rowi = lax.broadcasted_iota(jnp.int32, (1, C_CHUNK, 1), 1)
    g_last = jnp.sum(jnp.where(rowi == C_CHUNK - 1, gc, 0.0), axis=1, keepdims=True)
    kd = k * jnp.exp(g_last - gc)
    return (u_c, w_c, aqk, q * egc, kd, jnp.exp(g_last)), inv


def _gdn_state(local, state):
    u_c, w_c, aqk, qg, kd, dec = local
    v_new = u_c - _bein('hik,hkv->hiv', w_c, state)
    o = _bein('hik,hkv->hiv', qg, state) + _bein('hij,hjv->hiv', aqk, v_new)
    return o, state * dec + _bein('hik,hiv->hkv', kd, v_new)


C_SUB = 4


def _gdn_group(q, k, v, bgs, state, inv_known=None):
    local, inv = _gdn_local(q, k, v, bgs, inv_known)
    outs = []
    for s in range(len(bgs)):
        o, state = _gdn_state(tuple(t[s * C_HEADS:(s + 1) * C_HEADS] for t in local), state)
        outs.append(o)
    return outs, state, inv


def _heads(ref):
    return jnp.stack([ref[s * C_CHUNK:(s + 1) * C_CHUNK, h * C_DK:(h + 1) * C_DK] for s in range(C_SUB) for h in range(C_HEADS)], axis=0)


def _put_heads(ref, sub, val):
    rows = slice(sub * C_CHUNK, (sub + 1) * C_CHUNK)
    for h in range(C_HEADS):
        ref[rows, h * C_DK:(h + 1) * C_DK] = val[h]


def _gdn_specs(s_len, rev):
    rows = C_SUB * C_CHUNK
    n_g = s_len // rows
    at = (lambda i: n_g - 1 - i) if rev else (lambda i: i)
    col = lambda c: pl.BlockSpec((rows, 1024), lambda i: (at(i), c))
    row128 = pl.BlockSpec((rows, 128), lambda i: (at(i), 0))
    state = pl.BlockSpec((1, C_HEADS, C_DK, C_DK), lambda i: (at(i), 0, 0, 0))
    inv = pl.BlockSpec((1, C_SUB * C_HEADS, C_CHUNK, C_CHUNK), lambda i: (at(i), 0, 0, 0))
    return n_g, col, row128, state, inv


def _gdn_fwd(qkv, bg):
    s_len = qkv.shape[0]
    n_g, col, row128, state_spec, inv_spec = _gdn_specs(s_len, False)

    def body(q_ref, k_ref, v_ref, bg_ref, o_ref, ss_ref, inv_ref, st_ref):
        _zero_at_first([st_ref])
        s0 = st_ref[...]
        ss_ref[0] = s0
        bgs = [bg_ref[s * C_CHUNK:(s + 1) * C_CHUNK, :] for s in range(C_SUB)]
        outs, s2, inv = _gdn_group(_heads(q_ref), _heads(k_ref), _heads(v_ref), bgs, s0)
        st_ref[...] = s2
        inv_ref[0] = inv
        for s in range(C_SUB):
            _put_heads(o_ref, s, outs[s])

    return pl.pallas_call(
        body, name="gdn_fwd", grid=(n_g,),
        in_specs=[col(0), col(1), col(2), row128],
        out_specs=[col(0), state_spec, inv_spec],
        out_shape=[_sds(s_len, 1024), _sds(n_g, C_HEADS, C_DK, C_DK), _sds(n_g, C_SUB * C_HEADS, C_CHUNK, C_CHUNK)],
        scratch_shapes=[pltpu.VMEM((C_HEADS, C_DK, C_DK), F32)],
        compiler_params=_cp("arbitrary"),
    )(qkv, qkv, qkv, bg)


def _gdn_bwd(qkv, bg, states, invs, do):
    s_len = qkv.shape[0]
    n_g, col, row128, state_spec, inv_spec = _gdn_specs(s_len, True)

    def body(q_ref, k_ref, v_ref, bg_ref, ss_ref, inv_ref, do_ref, dq_ref, dk_ref, dv_ref, dbg_ref, ds_ref):
        _zero_at_first([ds_ref])
        inv_known = inv_ref[0]

        def group(q, k, v, bgs, st):
            outs, st2, _ = _gdn_group(q, k, v, bgs, st, inv_known)
            return outs, st2

        bgs = [bg_ref[s * C_CHUNK:(s + 1) * C_CHUNK, :] for s in range(C_SUB)]
        _, vjp = jax.vjp(group, _heads(q_ref), _heads(k_ref), _heads(v_ref), bgs, ss_ref[0])
        douts = [jnp.stack([do_ref[s * C_CHUNK:(s + 1) * C_CHUNK, h * C_DK:(h + 1) * C_DK] for h in range(C_HEADS)], axis=0)
                 for s in range(C_SUB)]
        dq, dk, dv, dbgs, ds = vjp((douts, ds_ref[...]))
        ds_ref[...] = ds
        for s in range(C_SUB):
            dbg_ref[s * C_CHUNK:(s + 1) * C_CHUNK, :] = dbgs[s]
            for ref, val in ((dq_ref, dq), (dk_ref, dk), (dv_ref, dv)):
                _put_heads(ref, s, val[s * C_HEADS:(s + 1) * C_HEADS])

    return pl.pallas_call(
        body, name="gdn_bwd", grid=(n_g,),
        in_specs=[col(0), col(1), col(2), row128, state_spec, inv_spec, col(0)],
        out_specs=[col(0), col(0), col(0), row128],
        out_shape=[_sds(s_len, 1024)] * 3 + [_sds(s_len, 128)],
        scratch_shapes=[pltpu.VMEM((C_HEADS, C_DK, C_DK), F32)],
        compiler_params=_cp("arbitrary"),
    )(qkv, qkv, qkv, bg, states, invs, do)


def _head_norm_gate(o, gate, norm_g):
    return (_rms(o) * norm_g) * jax.nn.silu(gate)


def _l1_out_fb(o, gate_c, x1, target, norm_g, w_out, post_g, gate):
    s_len = x1.shape[0]

    def body(o_ref, gc_ref, x1_ref, t_ref, ng_ref, w_ref, pg_ref, gt_ref,
             loss_ref, dres_ref, do_ref, dgc_ref, dw_ref, dng_ref, dpg_ref, dgt_ref):
        _zero_at_first([loss_ref, dw_ref, dng_ref, dpg_ref, dgt_ref])
        ng = ng_ref[...]
        ons, vjps = [], []
        for h in range(C_HEADS):
            sl = slice(h * C_DK, (h + 1) * C_DK)
            on, vjp_h = jax.vjp(_head_norm_gate, o_ref[:, sl], gc_ref[:, sl], ng)
            ons.append(on)
            vjps.append(vjp_h)
        on_all = jnp.concatenate(ons, axis=-1)
        y = _bdot(on_all, w_ref[...])
        x2, vjp2 = jax.vjp(_post_res, y, x1_ref[...], pg_ref[...], gt_ref[...])
        err = x2 - t_ref[...]
        _acc(loss_ref, jnp.full((1, 128), 0.5 * jnp.sum(jnp.mean(err * err, axis=-1)), F32))
        dx2 = err * (1.0 / D_MODEL)
        dy, _, dpg, dgt = vjp2(dx2)
        dres_ref[...] = dx2
        _acc(dpg_ref, dpg)
        _acc(dgt_ref, dgt)
        dw_ref[...] += _bdot_tn(on_all, dy)
        don = _bdot_nt(dy, w_ref[...])
        for h in range(C_HEADS):
            sl = slice(h * C_DK, (h + 1) * C_DK)
            do_h, dgc_h, dng = vjps[h](don[:, sl])
            do_ref[:, sl] = do_h
            dgc_ref[:, sl] = dgc_h
            _acc(dng_ref, dng)

    vec, r10 = _fix((1, D_MODEL)), _row(TM, D_MODEL)
    row128 = _fix((1, 128))
    return pl.pallas_call(
        body, name="l1_out_fb", grid=(s_len // TM,),
        in_specs=[r10, r10, r10, r10, row128, _fix((D_MODEL, D_MODEL)), vec, vec],
        out_specs=[row128, r10, r10, r10, _fix((D_MODEL, D_MODEL)), row128, vec, vec],
        out_shape=[_sds(1, 128), _sds(s_len, D_MODEL), _sds(s_len, D_MODEL), _sds(s_len, D_MODEL),
                   _sds(D_MODEL, D_MODEL), _sds(1, 128), _sds(1, D_MODEL), _sds(1, D_MODEL)],
        compiler_params=_cp("arbitrary"),
    )(o, gate_c, x1, target, norm_g, w_out, post_g, gate)


def _row_of(v, width, at):
    return jnp.zeros((1, width), F32).at[0, at:at + v.shape[-1]].set(v.reshape(-1))


def _local_step(x, target, mod, wd, comm=None):
    s_len = x.shape[0]
    shift0, scale0, gate0 = (mod[0:1, i * 1024:(i + 1) * 1024] for i in range(3))
    shift1, scale1, gate1 = (mod[1:2, i * 1024:(i + 1) * 1024] for i in range(3))
    pre_g0, pre_g1 = wd["pre_g"][0:1], wd["pre_g"][1:2]
    post_g0, post_g1 = wd["post_g"][0:1], wd["post_g"][1:2]
    w_in0 = wd["ab_w_in"].astype(BF)
    d_skip, glu_b = wd["s5_d"].reshape(1, 512), wd["s5_glu_b"].reshape(1, 512)
    norm_g = wd["gdn_norm_g"].reshape(1, 128)
    alog_row = _row_of(wd["gdn_a_log"], 128, C_HEADS)
    dtb_row = _row_of(wd["gdn_dt_bias"], 128, C_HEADS)
    conv_w = wd["gdn_conv"]

    a_re, a_im = wd["s5_a_re"], wd["s5_a_im"]
    log_dt = wd["s5_log_dt"].reshape(B_GROUPS, 1)
    bt_re = wd["s5_b_re"].transpose(0, 2, 1).reshape(B_WIDTH, B_STATE)
    bt_im = wd["s5_b_im"].transpose(0, 2, 1).reshape(B_WIDTH, B_STATE)
    abar_r, abar_i, bbar_r, bbar_i = _s5_params(a_re, a_im, log_dt, bt_re, bt_im)
    abr, abi = abar_r.reshape(1, -1), abar_i.reshape(1, -1)
    btr, bti = _blockdiag_b(bbar_r).astype(BF), _blockdiag_b(bbar_i).astype(BF)
    ctr, cti = _blockdiag_c(wd["s5_c_re"]).astype(BF), _blockdiag_c(wd["s5_c_im"]).astype(BF)

    table = _bucket_table()
    biases = _attn_bias(wd["rel_bias"], jnp.asarray(table))
    front = _l0_front(x, pre_g0, scale0, shift0, w_in0)
    qs, ks, vs = front[0:3], front[3:6], front[6:9]
    u, ga, gb, h0 = front[9:]
    riders = [None] * 4 if comm is None else comm.late_exchanges()
    os, ls, got = [], [], []
    for i in range(3):
        (o_d, l_d), g = _attn_fwd(qs[i], ks[i], vs[i], biases[i], exchange=riders[i])
        os.append(o_d)
        ls.append(l_d)
        got.append(g)
    seg_len = s_len // S5_SEG
    zero_state = (jnp.zeros((S5_SEG, S5_WIDTH), F32),) * 2
    ends, _ = _s5_seq_fwd(u, btr, bti, ctr, cti, abr, abi, zero_state, False)
    x_entry = _s5_entries("s5_entries_fwd", *ends, abr, abi, seg_len, False)
    (xr, xi, ypre3, _, _), g = _s5_seq_fwd(u, btr, bti, ctr, cti, abr, abi, x_entry, True, exchange=riders[3])
    got.append(g)
    ypre = ypre3.reshape(s_len, B_WIDTH)
    rider = None
    if comm is not None:
        mine0, mine1 = comm.late_halves(got)
        wd = {**wd, **comm.full_weights(["ab_w_out", "s5_glu_w"], mine0, _sibling_exchange("gather_w_sibling_l0", mine0))}
        rider = (mine1, "sibling")
    w_out0 = wd["ab_w_out"].astype(BF)
    glu_w = wd["s5_glu_w"].astype(BF)
    (x1, y0), theirs1 = _l0_out(os, ls, ga, gb, ypre, u, x, d_skip, glu_w, glu_b, w_out0, post_g0, gate0, exchange=rider)
    if comm is not None:
        wd = {**wd, **comm.full_weights(["gdn_w_in", "gdn_w_out"], mine1, theirs1)}
    w_in1 = jnp.concatenate([wd["gdn_w_in"], jnp.zeros((D_MODEL, C_IN_PAD - wd["gdn_w_in"].shape[1]), wd["gdn_w_in"].dtype)], axis=1).astype(BF)
    w_out1 = wd["gdn_w_out"].astype(BF)

    raw, gate_c, ba, h1 = _l1_front(x1, pre_g1, scale1, shift1, w_in1)
    qkv, bg = _gdn_prep(raw, ba, conv_w, alog_row, dtb_row)
    o_gdn, states, invs = _gdn_fwd(qkv, bg)
    loss_row, dres1, do_gdn, dgate_c, dw_out1, dnorm_g, dpost_g1, dgate1 = _l1_out_fb(
        o_gdn, gate_c, x1, target, norm_g, w_out1, post_g1, gate1)

    dq1, dk1, dv1, dbg = _gdn_bwd(qkv, bg, states, invs, do_gdn)
    draw, dba, dconv_w, dalog_row, ddtb_row = _gdn_prep_bwd(raw, ba, conv_w, alog_row, dtb_row, dq1, dk1, dv1, dbg)
    dz1, dx1, dpre_g1, dscale1, dshift1 = _front_bwd(
        "l1_front_bwd", x1, pre_g1, scale1, shift1, w_in1, dres1, [[draw], [dgate_c], [dba]], [QKV, 1024, 128])
    dw_in1 = _matmul_tn("l1_dw_in", h1, dz1, 1408)

    n_w1 = wd["gdn_w_in"].shape[1]
    l1_names, l0_names = ["gdn_w_in", "gdn_w_out"], ["ab_w_out", "s5_glu_w"]
    rider = None if comm is None else (comm.split_halves({"gdn_w_in": dw_in1[:, :n_w1], "gdn_w_out": dw_out1}), "sibling")
    l0b, from_sibling1 = _l0_out_bwd(os, ls, ga, gb, ypre, u, x, y0, d_skip, glu_w, glu_b, w_out0, post_g0, gate0, dx1, exchange=rider)
    dos, dls = l0b[0:3], l0b[3:6]
    dga, dgb, dypre, du_skip, dd_skip, dglu_w, dglu_b, dw_out0, dpost_g0, dgate0 = l0b[6:]
    rider = None if comm is None else (comm.split_halves({"ab_w_out": dw_out0, "s5_glu_w": dglu_w}), "sibling")
    starts, _ = _s5_seq_bwd(dypre, None, None, None, btr, bti, ctr, cti, abr, abi, zero_state, None, False)
    g_entry = _s5_entries("s5_entries_bwd", *starts, abr, -abi, seg_len, True)
    (du3, dbtr, dbti, dctr, dcti, dabr, dabi, _, _), from_sibling0 = _s5_seq_bwd(
        dypre, xr, xi, u, btr, bti, ctr, cti, abr, abi, g_entry, x_entry, True, exchange=rider)
    du_scan = du3.reshape(s_len, B_WIDTH)
    riders = [None] * 3
    if comm is not None:
        riders = [(comm.chip_partials(l1_names, from_sibling1), "scatter"), (comm.chip_partials(l0_names, from_sibling0), "scatter"), None]
    dqs, dks, dvs, dbs = [], [], [], []
    for i in range(3):
        (dq_d, dk_d, dv_d, db_d), got_d = _attn_bwd(qs[i], ks[i], vs[i], biases[i], os[i], ls[i], dos[i], dls[i], exchange=riders[i])
        dqs.append(dq_d)
        dks.append(dk_d)
        dvs.append(dv_d)
        dbs.append(db_d)
        if comm is not None and riders[i] is not None:
            comm.received.update(zip((l1_names, l0_names)[i], got_d))
    parts = [dqs, dks, dvs, [du_skip, du_scan], [dga], [dgb]]
    dz0, grad_x, dpre_g0, dscale0, dshift0 = _front_bwd(
        "l0_front_bwd", x, pre_g0, scale0, shift0, w_in0, dx1, parts, [512] * 6)
    dw_in0 = _matmul_tn("l0_dw_in", h0, dz0, 768)

    idx_rows = jnp.asarray(table.reshape(3, -1), F32)
    drel = _rel_bias_grad(dbs, idx_rows).T
    da_re, da_im, dlog_dt, dbt_re, dbt_im = _s5_params_bwd(
        a_re, a_im, log_dt, bt_re, bt_im, dabr.reshape(B_GROUPS, B_STATE), dabi.reshape(B_GROUPS, B_STATE),
        _blockdiag_b_t(dbtr), _blockdiag_b_t(dbti))
    unb = lambda d: d.reshape(B_GROUPS, B_GROUP, B_STATE).transpose(0, 2, 1)
    grads = {
        "pre_g": jnp.concatenate([dpre_g0, dpre_g1], 0), "post_g": jnp.concatenate([dpost_g0, dpost_g1], 0),
        "rel_bias": drel, "ab_w_in": dw_in0, "ab_w_out": dw_out0,
        "s5_a_re": da_re, "s5_a_im": da_im, "s5_log_dt": dlog_dt.reshape(B_GROUPS),
        "s5_b_re": unb(dbt_re), "s5_b_im": unb(dbt_im),
        "s5_c_re": _blockdiag_c_t(dctr), "s5_c_im": _blockdiag_c_t(dcti),
        "s5_d": dd_skip.reshape(512), "s5_glu_w": dglu_w, "s5_glu_b": dglu_b.reshape(512),
        "gdn_w_in": dw_in1[:, :wd["gdn_w_in"].shape[1]], "gdn_conv": dconv_w,
        "gdn_a_log": dalog_row[0, C_HEADS:2 * C_HEADS], "gdn_dt_bias": ddtb_row[0, C_HEADS:2 * C_HEADS],
        "gdn_norm_g": dnorm_g.reshape(128), "gdn_w_out": dw_out1,
    }
    dmod = jnp.concatenate([jnp.concatenate([dshift0, dscale0, dgate0], 1), jnp.concatenate([dshift1, dscale1, dgate1], 1)], 0)
    return loss_row[0, 0], grad_x, grads, dmod


def _place():
    return lax.axis_index("x"), lax.axis_index("y"), lax.axis_index("c")


def _flip(v, bit):
    return 1 - v if bit else v


def _hbm_call(name, body, arrs, out_shapes, n_sem):
    any_spec = pl.BlockSpec(memory_space=pl.ANY)
    return pl.pallas_call(
        body, name=name,
        in_specs=[any_spec] * len(arrs), out_specs=[any_spec] * len(out_shapes), out_shape=out_shapes,
        scratch_shapes=[pltpu.SemaphoreType.DMA((n_sem,)), pltpu.SemaphoreType.DMA((n_sem,))],
    )(*arrs)


def _own_slot(gathered, own, slot):
    idx = lax.broadcasted_iota(jnp.int32, (gathered.shape[0],) + (1,) * own.ndim, 0)
    return jnp.where(idx == slot, own[None], gathered)


def _all_gather8(name, arr):
    def body(x_ref, out_ref, send_sems, recv_sems):
        x, y, c = _place()
        me = 4 * x + 2 * y + c
        sends, recvs = [], []
        for m in range(1, 8):
            peer = (_flip(x, m & 4), _flip(y, m & 2), _flip(c, m & 1))
            sends.append(pltpu.make_async_remote_copy(x_ref, out_ref.at[me], send_sems.at[m - 1], recv_sems.at[m - 1],
                                                      device_id=peer, device_id_type=MESH))
            recvs.append(pltpu.make_async_remote_copy(x_ref, out_ref.at[4 * peer[0] + 2 * peer[1] + peer[2]], send_sems.at[m - 1],
                                                      recv_sems.at[m - 1], device_id=peer, device_id_type=MESH))
        for cp in sends:
            cp.start()
        for cp in recvs:
            cp.wait_recv()
        for cp in sends:
            cp.wait_send()

    return _hbm_call(name, body, [arr], [jax.ShapeDtypeStruct((8,) + arr.shape, arr.dtype)], 7)[0]


def _all_to_all8(name, arr):
    def body(x_ref, out_ref, send_sems, recv_sems):
        x, y, c = _place()
        me = 4 * x + 2 * y + c
        sends, recvs = [], []
        for m in range(1, 8):
            peer = (_flip(x, m & 4), _flip(y, m & 2), _flip(c, m & 1))
            peer_id = 4 * peer[0] + 2 * peer[1] + peer[2]
            sends.append(pltpu.make_async_remote_copy(x_ref.at[peer_id], out_ref.at[me], send_sems.at[m - 1], recv_sems.at[m - 1],
                                                      device_id=peer, device_id_type=MESH))
            recvs.append(pltpu.make_async_remote_copy(x_ref.at[peer_id], out_ref.at[peer_id], send_sems.at[m - 1], recv_sems.at[m - 1],
                                                      device_id=peer, device_id_type=MESH))
        for cp in sends:
            cp.start()
        for cp in recvs:
            cp.wait_recv()
        for cp in sends:
            cp.wait_send()

    return _hbm_call(name, body, [arr], [jax.ShapeDtypeStruct(arr.shape, arr.dtype)], 7)[0]


def _chip_copies(ins, outs, send_sems, recv_sems, scatter):
    x, y, c = _place()
    mine = 2 * x + y
    sends, recvs = [], []
    for a in range(len(ins)):
        for m in range(1, 4):
            px, py = _flip(x, m & 2), _flip(y, m & 1)
            k = 3 * a + m - 1
            src = ins[a].at[2 * px + py] if scatter else ins[a]
            sends.append(pltpu.make_async_remote_copy(src, outs[a].at[mine], send_sems.at[k], recv_sems.at[k],
                                                      device_id=(px, py, c), device_id_type=MESH))
            recvs.append(pltpu.make_async_remote_copy(src, outs[a].at[2 * px + py], send_sems.at[k], recv_sems.at[k],
                                                      device_id=(px, py, c), device_id_type=MESH))
    return sends, recvs


def _chip_shapes(arrs, scatter):
    return [jax.ShapeDtypeStruct(a.shape if scatter else (4,) + a.shape, a.dtype) for a in arrs]


def _chip_exchange(name, arrs, scatter):
    n = len(arrs)

    def body(*refs):
        sends, recvs = _chip_copies(refs[:n], refs[n:2 * n], refs[2 * n], refs[2 * n + 1], scatter)
        for cp in sends:
            cp.start()
        for cp in recvs:
            cp.wait_recv()
        for cp in sends:
            cp.wait_send()

    return _hbm_call(name, body, arrs, _chip_shapes(arrs, scatter), 3 * n)


def _call_with_exchange(body, name, grid, in_specs, out_specs, out_shape, scratch_shapes, args, exchange):
    if exchange is None:
        return pl.pallas_call(body, name=name, grid=grid, in_specs=in_specs, out_specs=out_specs, out_shape=out_shape,
                              scratch_shapes=scratch_shapes, compiler_params=_cp(*["arbitrary"] * len(grid)))(*args), []
    arrs, kind = exchange
    n_in, n_out, n_ex, n_scr = len(in_specs), len(out_specs), len(arrs), len(scratch_shapes)
    n_sem = n_ex if kind == "sibling" else 3 * n_ex
    ex_shapes = [jax.ShapeDtypeStruct(a.shape, a.dtype) for a in arrs] if kind == "sibling" else _chip_shapes(arrs, kind == "scatter")

    def fused(*refs):
        ins, ex_in = refs[:n_in], refs[n_in:n_in + n_ex]
        outs, ex_out = refs[n_in + n_ex:n_in + n_ex + n_out], refs[n_in + n_ex + n_out:n_in + 2 * n_ex + n_out]
        rest = refs[n_in + 2 * n_ex + n_out:]
        if kind == "sibling":
            sends = recvs = _sibling_copies(ex_in, ex_out, rest[n_scr], rest[n_scr + 1])
        else:
            sends, recvs = _chip_copies(ex_in, ex_out, rest[n_scr], rest[n_scr + 1], kind == "scatter")
        first, last = pl.program_id(0) == 0, pl.program_id(0) == grid[0] - 1
        for k in range(1, len(grid)):
            first, last = first & (pl.program_id(k) == 0), last & (pl.program_id(k) == grid[k] - 1)

        @pl.when(first)
        def _():
            for cp in sends:
                cp.start()

        body(*ins, *outs, *rest[:n_scr])

        @pl.when(last)
        def _():
            for cp in recvs:
                cp.wait_recv()
            for cp in sends:
                cp.wait_send()

    any_spec = pl.BlockSpec(memory_space=pl.ANY)
    res = pl.pallas_call(
        fused, name=name, grid=grid, in_specs=list(in_specs) + [any_spec] * n_ex, out_specs=list(out_specs) + [any_spec] * n_ex,
        out_shape=list(out_shape) + ex_shapes,
        scratch_shapes=list(scratch_shapes) + [pltpu.SemaphoreType.DMA((n_sem,))] * 2,
        compiler_params=_cp(*["arbitrary"] * len(grid)))(*args, *arrs)
    return res[:n_out], res[n_out:]


def _sibling_copies(ins, outs, send_sems, recv_sems):
    x, y, c = _place()
    return [pltpu.make_async_remote_copy(ins[a], outs[a], send_sems.at[a], recv_sems.at[a],
                                         device_id=(x, y, 1 - c), device_id_type=MESH) for a in range(len(ins))]


def _sibling_exchange(name, arrs):
    n = len(arrs)

    def body(*refs):
        copies = _sibling_copies(refs[:n], refs[n:2 * n], refs[2 * n], refs[2 * n + 1])
        for cp in copies:
            cp.start()
        for cp in copies:
            cp.wait_recv()
        for cp in copies:
            cp.wait_send()

    return _hbm_call(name, body, arrs, [jax.ShapeDtypeStruct(a.shape, a.dtype) for a in arrs], n)


def _row_tile(rows):
    for t in (256, 128, 64, 32, 16, 8):
        if rows % t == 0:
            return t
    return rows


def _pair_sum(name, a, b, out_dtype):
    rows, cols = a.shape
    tr = _row_tile(rows)

    def body(a_ref, b_ref, o_ref):
        o_ref[...] = (a_ref[...] + b_ref[...]).astype(out_dtype)

    return pl.pallas_call(body, name=name, grid=(rows // tr,), in_specs=[_row(tr, cols)] * 2, out_specs=_row(tr, cols),
                          out_shape=_sds(rows, cols, dtype=out_dtype), compiler_params=_cp("arbitrary"))(a, b)


def _chip_sum(name, recv, partial, mine):
    n, rows, cols = recv.shape
    tr = _row_tile(rows)

    def body(mine_ref, *refs):
        own = refs[n][0].astype(F32)
        acc = None
        for s in range(n):
            term = jnp.where(mine_ref[0] == s, own, refs[s][0].astype(F32))
            acc = term if acc is None else acc + term
        refs[-1][...] = acc

    def slot_spec(s):
        return pl.BlockSpec((1, tr, cols), lambda i, m: (jnp.where(m[0] == s, (s + 1) % n, s), i, 0))

    grid_spec = pltpu.PrefetchScalarGridSpec(
        num_scalar_prefetch=1, grid=(rows // tr,),
        in_specs=[slot_spec(s) for s in range(n)] + [pl.BlockSpec((1, tr, cols), lambda i, m: (m[0], i, 0))],
        out_specs=pl.BlockSpec((tr, cols), lambda i, m: (i, 0)))
    return pl.pallas_call(body, name=name, grid_spec=grid_spec, out_shape=_sds(rows, cols),
                          compiler_params=_cp("arbitrary"))(mine, *([recv] * n), partial)


def _slot_sum(name, arr):
    n, rows, cols = arr.shape
    tr = _row_tile(rows)

    def body(*refs):
        acc = refs[0][0]
        for r in refs[1:-1]:
            acc = acc + r[0]
        refs[-1][...] = acc

    specs = [pl.BlockSpec((1, tr, cols), functools.partial(lambda s, i: (s, i, 0), s)) for s in range(n)]
    return pl.pallas_call(body, name=name, grid=(rows // tr,), in_specs=specs, out_specs=_row(tr, cols),
                          out_shape=_sds(rows, cols), compiler_params=_cp("arbitrary"))(*([arr] * n))


def _adamw(name, w, g, m, v):
    rows, cols = w.shape
    tr = _row_tile(rows)

    def body(w_ref, g_ref, m_ref, v_ref, d_ref, nm_ref, nv_ref):
        g_ = g_ref[...]
        m_ = ADAM_B1 * m_ref[...] + (1.0 - ADAM_B1) * g_
        v_ = ADAM_B2 * v_ref[...] + (1.0 - ADAM_B2) * (g_ * g_)
        m_hat = m_ / (1.0 - ADAM_B1 ** ADAM_STEP)
        v_hat = v_ / (1.0 - ADAM_B2 ** ADAM_STEP)
        d_ref[...] = -ADAM_LR * (m_hat / (jnp.sqrt(v_hat) + ADAM_EPS) + ADAM_WD * w_ref[...])
        nm_ref[...] = m_
        nv_ref[...] = v_

    spec = _row(tr, cols)
    return pl.pallas_call(body, name=name, grid=(rows // tr,), in_specs=[spec] * 4, out_specs=[spec] * 3,
                          out_shape=[_sds(rows, cols)] * 3, compiler_params=_cp("arbitrary"))(w, g, m, v)


def _adamw_many(name, ws, gs, ms, vs):
    n = len(ws)

    def body(*refs):
        for i in range(n):
            w_ref, g_ref, m_ref, v_ref = (refs[k * n + i] for k in range(4))
            d_ref, nm_ref, nv_ref = (refs[(4 + k) * n + i] for k in range(3))
            g_ = g_ref[...]
            m_ = ADAM_B1 * m_ref[...] + (1.0 - ADAM_B1) * g_
            v_ = ADAM_B2 * v_ref[...] + (1.0 - ADAM_B2) * (g_ * g_)
            m_hat = m_ / (1.0 - ADAM_B1 ** ADAM_STEP)
            v_hat = v_ / (1.0 - ADAM_B2 ** ADAM_STEP)
            d_ref[...] = -ADAM_LR * (m_hat / (jnp.sqrt(v_hat) + ADAM_EPS) + ADAM_WD * w_ref[...])
            nm_ref[...] = m_
            nv_ref[...] = v_

    shapes = [_sds(*a.shape) for a in ws]
    res = pl.pallas_call(body, name=name, out_shape=shapes * 3,
                         compiler_params=pltpu.CompilerParams(vmem_limit_bytes=VMEM_LIMIT_BYTES))(*ws, *gs, *ms, *vs)
    return [(res[i], res[n + i], res[2 * n + i]) for i in range(n)]


def _adamw_halves(name, w, g_mine, g_sibling, m, v, core):
    _, rows, cols = w.shape
    half = rows // 2
    tr = _row_tile(half)
    per_half = half // tr

    def body(core_ref, w_ref, gm_ref, gs_ref, m_ref, v_ref, g_ref, d_ref, nm_ref, nv_ref):
        g_ = jnp.where(pl.program_id(0) // per_half == core_ref[0], gm_ref[...], gs_ref[...])
        m_ = ADAM_B1 * m_ref[...] + (1.0 - ADAM_B1) * g_
        v_ = ADAM_B2 * v_ref[...] + (1.0 - ADAM_B2) * (g_ * g_)
        m_hat = m_ / (1.0 - ADAM_B1 ** ADAM_STEP)
        v_hat = v_ / (1.0 - ADAM_B2 ** ADAM_STEP)
        g_ref[...] = g_
        d_ref[...] = -ADAM_LR * (m_hat / (jnp.sqrt(v_hat) + ADAM_EPS) + ADAM_WD * w_ref[...])
        nm_ref[...] = m_
        nv_ref[...] = v_

    full = pl.BlockSpec((None, tr, cols), lambda i, c: (0, i, 0))
    in_half = pl.BlockSpec((tr, cols), lambda i, c: (i % per_half, 0))
    grid_spec = pltpu.PrefetchScalarGridSpec(num_scalar_prefetch=1, grid=(rows // tr,),
                                             in_specs=[full, in_half, in_half, full, full], out_specs=[full] * 4)
    return pl.pallas_call(body, name=name, grid_spec=grid_spec, out_shape=[_sds(1, rows, cols)] * 4,
                          compiler_params=_cp("arbitrary"))(core, w, g_mine, g_sibling, m, v)


def _mod_local(c_all, ada_w):
    def body(c_ref, w_ref, o_ref):
        c_act = jax.nn.silu(c_ref[...])
        for l in range(2):
            o_ref[l] = _hdot(c_act, w_ref[l])

    return pl.pallas_call(body, name="mod_local", out_shape=_sds(2, 8, ada_w.shape[2]),
                          compiler_params=pltpu.CompilerParams(vmem_limit_bytes=VMEM_LIMIT_BYTES))(c_all, ada_w)


def _ada_w_grad(c_all, dmod_cols):
    def body(c_ref, d_ref, o_ref):
        c_act = jax.nn.silu(c_ref[...])
        for l in range(2):
            o_ref[l] = lax.dot_general(c_act, d_ref[l], (((0,), (0,)), ((), ())), precision=HI, preferred_element_type=F32)

    return pl.pallas_call(body, name="ada_w_grad", out_shape=_sds(2, D_MODEL, dmod_cols.shape[2]),
                          compiler_params=pltpu.CompilerParams(vmem_limit_bytes=VMEM_LIMIT_BYTES))(c_all, dmod_cols)


_SMALL = ("ada_b", "pre_g", "post_g", "rel_bias", "s5_a_re", "s5_a_im", "s5_log_dt", "s5_b_re", "s5_b_im", "s5_c_re", "s5_c_im",
          "s5_d", "s5_glu_b", "gdn_a_log", "gdn_dt_bias", "gdn_norm_g")
_SHARDED = ("ab_w_in", "ab_w_out", "s5_glu_w", "gdn_w_in", "gdn_w_out")
_COL_SHARDED = ("ab_w_in", "gdn_w_in")
_WEIGHTS = ("ada_w", "ada_b", "pre_g", "post_g", "rel_bias", "ab_w_in", "ab_w_out", "s5_a_re", "s5_a_im", "s5_log_dt", "s5_b_re",
            "s5_b_im", "s5_c_re", "s5_c_im", "s5_d", "s5_glu_w", "s5_glu_b", "gdn_w_in", "gdn_conv", "gdn_a_log", "gdn_dt_bias",
            "gdn_norm_g", "gdn_w_out")


def _rows128(n):
    return -(-n // 128)


def _pack(arrs, total_rows):
    pieces = []
    for a in arrs:
        flat = a.reshape(-1)
        pieces.append(jnp.pad(flat, (0, _rows128(flat.shape[0]) * 128 - flat.shape[0])).reshape(-1, 128))
    used = sum(p.shape[0] for p in pieces)
    pieces.append(jnp.zeros((total_rows - used, 128), F32))
    return jnp.concatenate(pieces, axis=0)


def _unpack(buf, shapes):
    out, at = [], 0
    for shp in shapes:
        n = int(np.prod(shp))
        out.append(buf[at:at + _rows128(n)].reshape(-1)[:n].reshape(shp))
        at += _rows128(n)
    return out


def _full_from_halves(name, g):
    if name in _COL_SHARDED:
        return g.transpose(0, 2, 1, 3).reshape(2 * g.shape[2], 4 * g.shape[3])
    return g.transpose(1, 0, 2, 3).reshape(8 * g.shape[2], g.shape[3])


def _shard_major(name, g):
    if name in _COL_SHARDED:
        return g.reshape(g.shape[0], 4, g.shape[1] // 4).transpose(1, 0, 2)
    return g.reshape(4, g.shape[0] // 4, g.shape[1])


_LATE = ("ab_w_out", "s5_glu_w", "gdn_w_in", "gdn_w_out")


class _WeightExchanges:
    def __init__(self, shards, core, chip):
        self.core, self.chip = core, chip
        self.half = {}
        for name, shard in shards.items():
            h = shard.shape[0] // 2
            self.half[name] = lax.dynamic_slice_in_dim(shard.astype(BF), core * h, h, axis=0)
        self.mine, self.partial, self.received = {}, {}, {}

    def my_halves(self, names, from_chips):
        return [_own_slot(g, self.half[n], self.chip) for n, g in zip(names, from_chips)]

    def full_weights(self, names, mine, theirs):
        return {n: _full_from_halves(n, jnp.where(self.core == 0, jnp.stack([a, b], 0), jnp.stack([b, a], 0)))
                for n, a, b in zip(names, mine, theirs)}

    def first_weights(self):
        mine = self.my_halves(["ab_w_in"], _chip_exchange("gather_w_chips", [self.half["ab_w_in"]], False))
        return self.full_weights(["ab_w_in"], mine, _sibling_exchange("gather_w_sibling_first", mine))

    def late_exchanges(self):
        rows = self.half["gdn_w_in"].shape[0] // 2
        pieces = [self.half["gdn_w_in"][:rows], self.half["gdn_w_in"][rows:]]
        return [([self.half["ab_w_out"], self.half["s5_glu_w"]], "gather"), ([self.half["gdn_w_out"]], "gather"),
                ([pieces[0]], "gather"), ([pieces[1]], "gather")]

    def late_halves(self, got):
        return (self.my_halves(["ab_w_out", "s5_glu_w"], got[0]),
                self.my_halves(["gdn_w_in", "gdn_w_out"], [jnp.concatenate([got[2][0], got[3][0]], axis=1), got[1][0]]))

    def split_halves(self, grads):
        other = []
        for name, g in grads.items():
            sm = _shard_major(name, g)
            h = sm.shape[1] // 2
            self.mine[name] = lax.dynamic_slice_in_dim(sm, self.core * h, h, axis=1)
            other.append(lax.dynamic_slice_in_dim(sm, (1 - self.core) * h, h, axis=1))
        return other

    def chip_partials(self, names, from_sibling):
        for name, b in zip(names, from_sibling):
            a = self.mine[name]
            flat = lambda t: t.reshape(-1, t.shape[-1])
            self.partial[name] = _pair_sum("sum_sibling_" + name, flat(a), flat(b), BF).reshape(a.shape)
        return [self.partial[n] for n in names]


def kernel(x, c, ada_w, ada_b, pre_g, post_g, rel_bias, ab_w_in, ab_w_out, s5_a_re, s5_a_im, s5_log_dt, s5_b_re, s5_b_im, s5_c_re, s5_c_im, s5_d, s5_glu_w, s5_glu_b, gdn_w_in, gdn_conv, gdn_a_log, gdn_dt_bias, gdn_norm_g, gdn_w_out, loss_target, m_ada_w, m_ada_b, m_pre_g, m_post_g, m_rel_bias, m_ab_w_in, m_ab_w_out, m_s5_a_re, m_s5_a_im, m_s5_log_dt, m_s5_b_re, m_s5_b_im, m_s5_c_re, m_s5_c_im, m_s5_d, m_s5_glu_w, m_s5_glu_b, m_gdn_w_in, m_gdn_conv, m_gdn_a_log, m_gdn_dt_bias, m_gdn_norm_g, m_gdn_w_out, v_ada_w, v_ada_b, v_pre_g, v_post_g, v_rel_bias, v_ab_w_in, v_ab_w_out, v_s5_a_re, v_s5_a_im, v_s5_log_dt, v_s5_b_re, v_s5_b_im, v_s5_c_re, v_s5_c_im, v_s5_d, v_s5_glu_w, v_s5_glu_b, v_gdn_w_in, v_gdn_conv, v_gdn_a_log, v_gdn_dt_bias, v_gdn_norm_g, v_gdn_w_out):
    w = dict(ada_w=ada_w, ada_b=ada_b, pre_g=pre_g, post_g=post_g, rel_bias=rel_bias, ab_w_in=ab_w_in, ab_w_out=ab_w_out,
             s5_a_re=s5_a_re, s5_a_im=s5_a_im, s5_log_dt=s5_log_dt, s5_b_re=s5_b_re, s5_b_im=s5_b_im, s5_c_re=s5_c_re, s5_c_im=s5_c_im,
             s5_d=s5_d, s5_glu_w=s5_glu_w, s5_glu_b=s5_glu_b, gdn_w_in=gdn_w_in, gdn_conv=gdn_conv, gdn_a_log=gdn_a_log,
             gdn_dt_bias=gdn_dt_bias, gdn_norm_g=gdn_norm_g, gdn_w_out=gdn_w_out)
    m = dict(ada_w=m_ada_w, ada_b=m_ada_b, pre_g=m_pre_g, post_g=m_post_g, rel_bias=m_rel_bias, ab_w_in=m_ab_w_in, ab_w_out=m_ab_w_out,
             s5_a_re=m_s5_a_re, s5_a_im=m_s5_a_im, s5_log_dt=m_s5_log_dt, s5_b_re=m_s5_b_re, s5_b_im=m_s5_b_im, s5_c_re=m_s5_c_re,
             s5_c_im=m_s5_c_im, s5_d=m_s5_d, s5_glu_w=m_s5_glu_w, s5_glu_b=m_s5_glu_b, gdn_w_in=m_gdn_w_in, gdn_conv=m_gdn_conv,
             gdn_a_log=m_gdn_a_log, gdn_dt_bias=m_gdn_dt_bias, gdn_norm_g=m_gdn_norm_g, gdn_w_out=m_gdn_w_out)
    v = dict(ada_w=v_ada_w, ada_b=v_ada_b, pre_g=v_pre_g, post_g=v_post_g, rel_bias=v_rel_bias, ab_w_in=v_ab_w_in, ab_w_out=v_ab_w_out,
             s5_a_re=v_s5_a_re, s5_a_im=v_s5_a_im, s5_log_dt=v_s5_log_dt, s5_b_re=v_s5_b_re, s5_b_im=v_s5_b_im, s5_c_re=v_s5_c_re,
             s5_c_im=v_s5_c_im, s5_d=v_s5_d, s5_glu_w=v_s5_glu_w, s5_glu_b=v_s5_glu_b, gdn_w_in=v_gdn_w_in, gdn_conv=v_gdn_conv,
             gdn_a_log=v_gdn_a_log, gdn_dt_bias=v_gdn_dt_bias, gdn_norm_g=v_gdn_norm_g, gdn_w_out=v_gdn_w_out)
    ix, iy, ic = _place()
    me = 4 * ix + 2 * iy + ic
    chip = 2 * ix + iy
    n_cols = ada_w.shape[2]

    mine_first = _pack([c, gdn_conv], 32)
    first = _own_slot(_all_gather8("gather_c_conv", mine_first), mine_first, me)
    c_all = first[:, 0:8].reshape(8, D_MODEL)
    conv_full = first[0::2, 8:32].reshape(4, C_CONV, n_cols).transpose(1, 0, 2).reshape(C_CONV, 4 * n_cols)
    mine_mod = _mod_local(c_all, ada_w)
    modl = _own_slot(_all_gather8("gather_mod", mine_mod), mine_mod, me)
    mod = lax.dynamic_index_in_dim(modl[0::2], me, axis=2, keepdims=False)
    mod = mod.transpose(1, 0, 2).reshape(2, 4 * n_cols) + ada_b

    comm = _WeightExchanges({name: w[name][0] for name in _SHARDED}, ic, chip)
    wd = {name: w[name] for name in _SMALL if name != "ada_b"}
    wd = {k: (a if k in ("pre_g", "post_g", "rel_bias") else a[0]) for k, a in wd.items()}
    wd["gdn_conv"] = conv_full
    wd.update(comm.first_weights())

    loss_local, grad_x, grads, dmod = _local_step(x[0], loss_target[0], mod, wd, comm)
    loss = lax.psum(loss_local, ("x", "y", "c"))

    small_shapes = [w[name].shape for name in _SMALL] + [(C_CONV, 4 * n_cols)]
    small_rows = -(-sum(_rows128(int(np.prod(s))) for s in small_shapes) // 64) * 64
    per_dev, dmod_rows = small_rows // 8, _rows128(2 * 3 * D_MODEL)
    partial = _pack([dmod] + [grads[name] for name in _SMALL[1:]] + [grads["gdn_conv"]], small_rows)
    outbound = jnp.concatenate([partial.reshape(8, per_dev, 128), jnp.broadcast_to(partial[None, :dmod_rows], (8, dmod_rows, 128))], axis=1)
    inbound = _own_slot(_all_to_all8("reduce_small_grads", outbound), lax.dynamic_index_in_dim(outbound, me, 0, keepdims=False), me)
    my_rows = _slot_sum("sum_small_grads", inbound[:, :per_dev])
    g_small = _own_slot(_all_gather8("gather_small_grads", my_rows), my_rows, me).reshape(small_rows, 128)
    g_list = _unpack(g_small, small_shapes)
    out_g, out_d, out_m, out_v = {}, {}, {}, {}

    def update(name, g2d):
        shp = w[name].shape
        two_d = lambda a: a.reshape(-1, shp[-1])
        d_, m_, v_ = _adamw("adamw_" + name, two_d(w[name]), g2d, two_d(m[name]), two_d(v[name]))
        out_g[name], out_d[name], out_m[name], out_v[name] = (a.reshape(shp) for a in (g2d, d_, m_, v_))

    small = list(_SMALL) + ["gdn_conv"]
    small_g = [g.reshape(-1, g.shape[-1]) for g in g_list[:-1]] + [lax.dynamic_slice_in_dim(g_list[-1], chip * n_cols, n_cols, axis=1)]
    two_d = lambda a: a.reshape(-1, a.shape[-1])
    results = _adamw_many("adamw_small", [two_d(w[n]) for n in small], small_g, [two_d(m[n]) for n in small], [two_d(v[n]) for n in small])
    for name, g2d, (d_, m_, v_) in zip(small, small_g, results):
        out_g[name], out_d[name], out_m[name], out_v[name] = (a.reshape(w[name].shape) for a in (g2d, d_, m_, v_))

    dmod_all = inbound[:, per_dev:].reshape(8, 2, 4, n_cols)
    dmod_cols = lax.dynamic_index_in_dim(dmod_all, chip, axis=2, keepdims=False).transpose(1, 0, 2)
    update("ada_w", _ada_w_grad(c_all, dmod_cols).reshape(-1, n_cols))

    from_sibling = _sibling_exchange("reduce_sibling", comm.split_halves({"ab_w_in": grads["ab_w_in"]}))
    comm.received["ab_w_in"] = _chip_exchange("reduce_chips", comm.chip_partials(["ab_w_in"], from_sibling), True)[0]
    chip_1 = jnp.reshape(chip, (1,)).astype(jnp.int32)
    core_1 = jnp.reshape(ic, (1,)).astype(jnp.int32)
    reduced = [_chip_sum("sum_chips_" + name, comm.received[name], comm.partial[name], chip_1) for name in _SHARDED]
    for name, g_mine, g_sib in zip(_SHARDED, reduced, _sibling_exchange("reduce_share", reduced)):
        out_g[name], out_d[name], out_m[name], out_v[name] = _adamw_halves(
            "adamw_" + name, w[name], g_mine, g_sib, m[name], v[name], core_1)

    return (loss, grad_x[None], *[out_g[n] for n in _WEIGHTS], *[out_d[n] for n in _WEIGHTS],
            *[out_m[n] for n in _WEIGHTS], *[out_v[n] for n in _WEIGHTS])
```

```python
import functools
import math

import numpy as np
import jax
import jax.numpy as jnp
from jax import lax
from jax.experimental import pallas as pl
from jax.experimental.pallas import tpu as pltpu

F32 = jnp.float32
BF = jnp.bfloat16
HI = lax.Precision.HIGHEST
MESH = pl.DeviceIdType.MESH

D_MODEL = 1024
EPS = 1e-6
A_HEADS, A_HD, A_WIDTH, A_BLOCK = 8, 64, 512, 128
DILATIONS = (1, 4, 16)
N_KEYS = 128
REL_BUCKETS, REL_MAX_DIST = 32, 2048
B_WIDTH, B_GROUP, B_GROUPS, B_STATE = 512, 16, 32, 64
S5_LANES = 512
S5_TILES = 4
C_HEADS, C_DK, C_CHUNK, C_CONV = 8, 128, 64, 4
QKV = 3072
C_IN = QKV + 1024 + 2 * C_HEADS
C_IN_PAD = 4224
TM = 256
VMEM_LIMIT_BYTES = 56 * 1024 * 1024
ADAM_LR, ADAM_B1, ADAM_B2, ADAM_EPS, ADAM_WD, ADAM_STEP = 0.001, 0.9, 0.999, 1e-08, 0.01, 10
NEG = float(np.finfo(np.float32).min)


def _cp(*sem):
    return pltpu.CompilerParams(dimension_semantics=sem, vmem_limit_bytes=VMEM_LIMIT_BYTES)


def _bdot(a, b):
    return jnp.dot(a.astype(BF), b.astype(BF), preferred_element_type=F32)


def _bdot_nt(a, b):
    return lax.dot_general(a.astype(BF), b.astype(BF), (((1,), (1,)), ((), ())), preferred_element_type=F32)


def _bdot_tn(a, b):
    return lax.dot_general(a.astype(BF), b.astype(BF), (((0,), (0,)), ((), ())), preferred_element_type=F32)


def _hdot(a, b):
    return jnp.dot(a, b, precision=HI, preferred_element_type=F32)


def _bein(eq, a, b):
    return jnp.einsum(eq, a.astype(BF), b.astype(BF), preferred_element_type=F32)


def _row(tm, n):
    return pl.BlockSpec((tm, n), lambda i: (i, 0))


def _fix(shape):
    return pl.BlockSpec(shape, lambda i: (0,) * len(shape))


def _sds(*shape, dtype=F32):
    return jax.ShapeDtypeStruct(shape, dtype)


def _acc(ref, val):
    ref[...] += val


def _zero_at_first(refs, axis=0):
    @pl.when(pl.program_id(axis) == 0)
    def _():
        for r in refs:
            r[...] = jnp.zeros_like(r)


def _rms(x):
    return x * lax.rsqrt(jnp.mean(x * x, axis=-1, keepdims=True) + EPS)


def _pre_mod(x, g, scale, shift):
    return (_rms(x) * g) * (1.0 + scale) + shift


def _post_res(y, x, post_g, gate):
    return x + gate * (_rms(y) * post_g)


def _merge_gate(o1, o2, o3, l1, l2, l3, ga):
    m = jnp.maximum(jnp.maximum(l1, l2), l3)
    e1, e2, e3 = jnp.exp(l1 - m), jnp.exp(l2 - m), jnp.exp(l3 - m)
    inv = 1.0 / (e1 + e2 + e3)
    return ((e1 * inv) * o1 + (e2 * inv) * o2 + (e3 * inv) * o3) * jax.nn.silu(ga)


def _s5_gelu(ypre, u, d_skip):
    return jax.nn.gelu(ypre + d_skip * u)


def _s5_glu(yb, gl, gb):
    return yb * jax.nn.sigmoid(gl) * jax.nn.silu(gb)


def _l0_front(x, pre_g, scale, shift, w_in):
    s_len = x.shape[0]

    def body(x_ref, g_ref, sc_ref, sh_ref, w_ref, *out_refs):
        qkv_refs, (u_ref, ga_ref, gb_ref, h_ref) = out_refs[:9], out_refs[9:]
        hb = _pre_mod(x_ref[...], g_ref[...], sc_ref[...], sh_ref[...]).astype(BF)
        h_ref[...] = hb
        z = jnp.dot(hb, w_ref[...], preferred_element_type=F32)
        for a in range(3):
            piece = z[:, a * 512:(a + 1) * 512]
            for i, d in enumerate(DILATIONS):
                qkv_refs[3 * a + i][...] = _to_res(piece, d).astype(BF)
        u_ref[...] = z[:, 1536:2048]
        ga_ref[...] = z[:, 2048:2560]
        gb_ref[...] = z[:, 2560:3072]

    vec = _fix((1, D_MODEL))
    return pl.pallas_call(
        body, name="l0_front", grid=(s_len // TM,),
        in_specs=[_row(TM, D_MODEL), vec, vec, vec, _fix((D_MODEL, 3072))],
        out_specs=[_res_spec(d) for d in DILATIONS] * 3 + [_row(TM, 512)] * 3 + [_row(TM, D_MODEL)],
        out_shape=[_sds(*_res_shape(s_len, d), dtype=BF) for d in DILATIONS] * 3 + [_sds(s_len, 512)] * 3 + [_sds(s_len, D_MODEL, dtype=BF)],
        compiler_params=_cp("arbitrary"),
    )(x, pre_g, scale, shift, w_in)


def _front_bwd(name, x, pre_g, scale, shift, w_in, dres, parts, widths):
    s_len = x.shape[0]
    n_in = sum(len(p) for p in parts)
    n_cols = sum(widths)

    def body(*refs):
        x_ref, g_ref, sc_ref, sh_ref, w_ref, dres_ref = refs[:6]
        part_refs = refs[6:6 + n_in]
        dz_ref, dx_ref, dg_ref, dsc_ref, dsh_ref = refs[6 + n_in:]
        _zero_at_first([dg_ref, dsc_ref, dsh_ref])
        _, vjp = jax.vjp(_pre_mod, x_ref[...], g_ref[...], sc_ref[...], sh_ref[...])
        dh = jnp.zeros((TM, D_MODEL), F32)
        col, at = 0, 0
        for grp, width in zip(parts, widths):
            tile = lambda r: _from_res(r[...]) if len(r.shape) == 3 else r[...]
            dz = tile(part_refs[at])
            for r in part_refs[at + 1:at + len(grp)]:
                dz = dz + tile(r)
            at += len(grp)
            dzb = dz.astype(BF)
            dz_ref[:, col:col + width] = dzb
            dh = dh + lax.dot_general(dzb, w_ref[:, col:col + width], (((1,), (1,)), ((), ())), preferred_element_type=F32)
            col += width
        dx, dg, dsc, dsh = vjp(dh)
        dx_ref[...] = dx + dres_ref[...]
        _acc(dg_ref, dg)
        _acc(dsc_ref, dsc)
        _acc(dsh_ref, dsh)

    vec = _fix((1, D_MODEL))
    flat = [a for p in parts for a in p]
    return pl.pallas_call(
        body, name=name, grid=(s_len // TM,),
        in_specs=[_row(TM, D_MODEL), vec, vec, vec, _fix((D_MODEL, n_cols)), _row(TM, D_MODEL)]
        + [_res_spec(a.shape[0], a.shape[2]) if a.ndim == 3 else _row(TM, a.shape[1]) for a in flat],
        out_specs=[_row(TM, n_cols), _row(TM, D_MODEL), vec, vec, vec],
        out_shape=[_sds(s_len, n_cols, dtype=BF), _sds(s_len, D_MODEL), _sds(1, D_MODEL), _sds(1, D_MODEL), _sds(1, D_MODEL)],
        compiler_params=_cp("arbitrary"),
    )(x, pre_g, scale, shift, w_in, dres, *flat)


def _matmul_tn(name, a, b, tn):
    s_len, k_dim = a.shape
    n_dim = b.shape[1]
    ts = 512

    def body(a_ref, b_ref, o_ref):
        _zero_at_first([o_ref], axis=1)
        o_ref[...] += lax.dot_general(a_ref[...], b_ref[...], (((0,), (0,)), ((), ())), preferred_element_type=F32)

    return pl.pallas_call(
        body, name=name, grid=(n_dim // tn, s_len // ts),
        in_specs=[pl.BlockSpec((ts, k_dim), lambda j, i: (i, 0)), pl.BlockSpec((ts, tn), lambda j, i: (i, j))],
        out_specs=pl.BlockSpec((k_dim, tn), lambda j, i: (0, j)),
        out_shape=_sds(k_dim, n_dim),
        compiler_params=_cp("arbitrary", "arbitrary"),
    )(a, b)


def _t5_bucket_np(dist):
    dist = np.maximum(dist, 0)
    max_exact = REL_BUCKETS // 2
    large = max_exact + (np.log(np.maximum(dist, 1) / max_exact)
                         / math.log(REL_MAX_DIST / max_exact) * (REL_BUCKETS - max_exact)).astype(np.int32)
    large = np.minimum(large, REL_BUCKETS - 1)
    return np.where(dist < max_exact, dist, large).astype(np.int32)


def _to_res(z, dil):
    if dil == 1:
        return z[None]
    return jnp.swapaxes(z.reshape(z.shape[0] // dil, dil, z.shape[1]), 0, 1)


def _from_res(z):
    if z.shape[0] == 1:
        return z[0]
    return jnp.swapaxes(z, 0, 1).reshape(z.shape[0] * z.shape[1], z.shape[2])


def _res_shape(s_len, dil, width=A_WIDTH):
    return (dil, s_len // dil, width)


def _res_spec(dil, width=A_WIDTH):
    return pl.BlockSpec((dil, TM // dil, width), lambda i: (0, i, 0))


def _bucket_table():
    qi = np.arange(A_BLOCK)[:, None]
    kj = np.arange(2 * A_BLOCK)[None, :]
    return np.stack([_t5_bucket_np((qi + A_BLOCK - kj) * d) for d in DILATIONS], 0)


def _attn_mask(first):
    qi = lax.broadcasted_iota(jnp.int32, (A_BLOCK, 2 * A_BLOCK), 0)
    kj = lax.broadcasted_iota(jnp.int32, (A_BLOCK, 2 * A_BLOCK), 1)
    rel = qi + A_BLOCK - kj
    return (rel >= 0) & (rel <= N_KEYS) & (jnp.logical_not(first) | (kj >= A_BLOCK))


def _attn_specs(nb, rev):
    per = 2 if nb % 2 == 0 else 1
    steps = nb // per
    n_of = (lambda i: steps - 1 - i) if rev else (lambda i: i)
    cur = pl.BlockSpec((None, per * A_BLOCK, A_WIDTH), lambda r, i: (r, n_of(i), 0))
    prev = pl.BlockSpec((None, A_BLOCK, A_WIDTH), lambda r, i: (r, jnp.maximum(per * n_of(i) - 1, 0), 0))
    bias = pl.BlockSpec((A_HEADS, A_BLOCK, 2 * A_BLOCK), lambda r, i: (0, 0, 0))
    return per, steps, cur, prev, bias


def _attn_fwd(q, k, v, bias, exchange=None):
    dil, t_len, _ = q.shape
    per, steps, cur, prev, bias_spec = _attn_specs(t_len // A_BLOCK, False)
    scale = A_HD ** -0.5

    def body(q_ref, kp_ref, kc_ref, vp_ref, vc_ref, b_ref, o_ref, l_ref):
        lane = lax.broadcasted_iota(jnp.int32, (1, 128), 1)
        for sub in range(per):
            rows = slice(sub * A_BLOCK, (sub + 1) * A_BLOCK)
            before = slice((sub - 1) * A_BLOCK, sub * A_BLOCK)
            mask = _attn_mask((pl.program_id(1) == 0) if sub == 0 else False)
            for hp in range(A_HEADS // 2):
                sl = slice(hp * 128, (hp + 1) * 128)
                qp = q_ref[rows, sl]
                kw = jnp.concatenate([kp_ref[:, sl] if sub == 0 else kc_ref[before, sl], kc_ref[rows, sl]], axis=0).astype(BF)
                vw = jnp.concatenate([vp_ref[:, sl] if sub == 0 else vc_ref[before, sl], vc_ref[rows, sl]], axis=0).astype(BF)
                outs, lses = [], []
                for j in range(2):
                    hm = (lane < 64) if j == 0 else (lane >= 64)
                    s = _bdot_nt(jnp.where(hm, qp, 0.0), kw) * scale
                    s = jnp.where(mask, s + b_ref[2 * hp + j], NEG)
                    m = jnp.max(s, axis=-1, keepdims=True)
                    p = jnp.exp(s - m)
                    den = jnp.sum(p, axis=-1, keepdims=True)
                    outs.append(_bdot(p, vw) / den)
                    lses.append(m + jnp.log(den))
                hm0 = lane < 64
                o_ref[rows, sl] = jnp.where(hm0, outs[0], outs[1])
                l_ref[rows, sl] = jnp.where(hm0, lses[0], lses[1])

    return _call_with_exchange(body, f"attn_fwd_d{dil}", (dil, steps), [cur, prev, cur, prev, cur, bias_spec], [cur, cur],
                               [_sds(dil, t_len, A_WIDTH)] * 2, [], (q, k, k, v, v, bias), exchange)


def _attn_bwd(q, k, v, bias, o, l, do, dl, exchange=None):
    dil, t_len, _ = q.shape
    per, steps, cur, prev, bias_spec = _attn_specs(t_len // A_BLOCK, True)
    scale = A_HD ** -0.5

    def body(q_ref, kp_ref, kc_ref, vp_ref, vc_ref, b_ref, o_ref, l_ref, do_ref, dl_ref,
             dq_ref, dk_ref, dv_ref, db_ref, ck_ref, cv_ref):
        _zero_at_first([ck_ref, cv_ref], axis=1)

        @pl.when((pl.program_id(0) == 0) & (pl.program_id(1) == 0))
        def _():
            db_ref[...] = jnp.zeros_like(db_ref)

        lane = lax.broadcasted_iota(jnp.int32, (1, 128), 1)
        for hp in range(A_HEADS // 2):
            sl = slice(hp * 128, (hp + 1) * 128)
            to_prev_k, to_prev_v = ck_ref[:, sl], cv_ref[:, sl]
            for sub in range(per - 1, -1, -1):
                rows = slice(sub * A_BLOCK, (sub + 1) * A_BLOCK)
                before = slice((sub - 1) * A_BLOCK, sub * A_BLOCK)
                mask = _attn_mask((pl.program_id(1) == steps - 1) if sub == 0 else False)
                qp = q_ref[rows, sl]
                kw = jnp.concatenate([kp_ref[:, sl] if sub == 0 else kc_ref[before, sl], kc_ref[rows, sl]], axis=0).astype(BF)
                vw = jnp.concatenate([vp_ref[:, sl] if sub == 0 else vc_ref[before, sl], vc_ref[rows, sl]], axis=0).astype(BF)
                op, lp, dop, dlp = o_ref[rows, sl], l_ref[rows, sl], do_ref[rows, sl], dl_ref[rows, sl]
                dq_acc = jnp.zeros((A_BLOCK, 128), F32)
                dk_acc = jnp.zeros((2 * A_BLOCK, 128), F32)
                dv_acc = jnp.zeros((2 * A_BLOCK, 128), F32)
                for j in range(2):
                    hm = (lane < 64) if j == 0 else (lane >= 64)
                    qm = jnp.where(hm, qp, 0.0)
                    s = _bdot_nt(qm, kw) * scale
                    s = jnp.where(mask, s + b_ref[2 * hp + j], NEG)
                    lse = jnp.max(jnp.where(hm, lp, NEG), axis=-1, keepdims=True)
                    p = jnp.exp(s - lse)
                    do_h = jnp.where(hm, dop, 0.0)
                    dd = jnp.sum(do_h * op, axis=-1, keepdims=True)
                    dlse = jnp.sum(jnp.where(hm, dlp, 0.0), axis=-1, keepdims=True)
                    ds = p * (_bdot_nt(do_h, vw) - dd + dlse)
                    dv_acc = dv_acc + _bdot_tn(p, do_h)
                    dq_acc = dq_acc + jnp.where(hm, _bdot(ds, kw), 0.0) * scale
                    dk_acc = dk_acc + _bdot_tn(ds, qm) * scale
                    db_ref[2 * hp + j] += ds
                dq_ref[rows, sl] = dq_acc
                dk_ref[rows, sl] = dk_acc[A_BLOCK:] + to_prev_k
                dv_ref[rows, sl] = dv_acc[A_BLOCK:] + to_prev_v
                to_prev_k, to_prev_v = dk_acc[:A_BLOCK], dv_acc[:A_BLOCK]
            ck_ref[:, sl] = to_prev_k
            cv_ref[:, sl] = to_prev_v

    return _call_with_exchange(
        body, f"attn_bwd_d{dil}", (dil, steps), [cur, prev, cur, prev, cur, bias_spec, cur, cur, cur, cur],
        [cur, cur, cur, bias_spec], [_sds(dil, t_len, A_WIDTH)] * 3 + [_sds(A_HEADS, A_BLOCK, 2 * A_BLOCK)],
        [pltpu.VMEM((A_BLOCK, A_WIDTH), F32)] * 2, (q, k, k, v, v, bias, o, l, do, dl), exchange)


def _attn_bias(rel_bias, table):
    def body(rb_ref, t_ref, *o_refs):
        for c in range(3):
            t = t_ref[c]
            acc = [jnp.zeros((A_BLOCK, 2 * A_BLOCK), F32) for _ in range(A_HEADS)]
            for b in range(REL_BUCKETS):
                hit = t == b
                acc = [jnp.where(hit, rb_ref[b, h], acc[h]) for h in range(A_HEADS)]
            for h in range(A_HEADS):
                o_refs[c][h] = acc[h]

    return pl.pallas_call(body, name="attn_bias", out_shape=[_sds(A_HEADS, A_BLOCK, 2 * A_BLOCK)] * 3,
                          in_specs=[pl.BlockSpec(memory_space=pltpu.SMEM), pl.BlockSpec(memory_space=pltpu.VMEM)],
                          compiler_params=pltpu.CompilerParams(vmem_limit_bytes=VMEM_LIMIT_BYTES))(rel_bias, table)


def _rel_bias_grad(dbs, idx_rows):
    n = A_BLOCK * 2 * A_BLOCK

    def body(d0_ref, d1_ref, d2_ref, idx_ref, o_ref):
        bucket = lax.broadcasted_iota(jnp.int32, (REL_BUCKETS, n), 0).astype(F32)
        acc = jnp.zeros((A_HEADS, REL_BUCKETS), F32)
        for c, db_ref in enumerate((d0_ref, d1_ref, d2_ref)):
            onehot = (idx_ref[c:c + 1, :] == bucket).astype(F32)
            acc = acc + lax.dot_general(db_ref[...], onehot, (((1,), (1,)), ((), ())), precision=HI, preferred_element_type=F32)
        o_ref[...] = acc

    return pl.pallas_call(body, name="rel_bias_grad", out_shape=_sds(A_HEADS, REL_BUCKETS),
                          compiler_params=pltpu.CompilerParams(vmem_limit_bytes=VMEM_LIMIT_BYTES))(
                              *[d.reshape(A_HEADS, n) for d in dbs], idx_rows)


def _s5_param_fn(a_re, a_im, log_dt, bt_re, bt_im):
    dt = jnp.exp(log_dt)
    mag = jnp.exp(dt * a_re)
    abar_r, abar_i = mag * jnp.cos(dt * a_im), mag * jnp.sin(dt * a_im)
    den = a_re * a_re + a_im * a_im
    fr = ((abar_r - 1.0) * a_re + abar_i * a_im) / den
    fi = (abar_i * a_re - (abar_r - 1.0) * a_im) / den
    row = lax.broadcasted_iota(jnp.int32, (B_WIDTH, B_GROUPS), 0)
    grp = lax.broadcasted_iota(jnp.int32, (B_WIDTH, B_GROUPS), 1)
    expand = ((row // B_GROUP) == grp).astype(F32)
    fr_e, fi_e = _hdot(expand, fr), _hdot(expand, fi)
    return abar_r, abar_i, fr_e * bt_re - fi_e * bt_im, fr_e * bt_im + fi_e * bt_re


def _s5_params(a_re, a_im, log_dt, bt_re, bt_im):
    def body(ar, ai, ld, br, bi, o1, o2, o3, o4):
        o1[...], o2[...], o3[...], o4[...] = _s5_param_fn(ar[...], ai[...], ld[...], br[...], bi[...])

    return pl.pallas_call(body, name="s5_params",
                          out_shape=[_sds(B_GROUPS, B_STATE)] * 2 + [_sds(B_WIDTH, B_STATE)] * 2)(a_re, a_im, log_dt, bt_re, bt_im)


def _s5_params_bwd(a_re, a_im, log_dt, bt_re, bt_im, d1, d2, d3, d4):
    def body(ar, ai, ld, br, bi, c1, c2, c3, c4, o1, o2, o3, o4, o5):
        _, vjp = jax.vjp(_s5_param_fn, ar[...], ai[...], ld[...], br[...], bi[...])
        o1[...], o2[...], o3[...], o4[...], o5[...] = vjp((c1[...], c2[...], c3[...], c4[...]))

    return pl.pallas_call(body, name="s5_params_bwd",
                          out_shape=[_sds(B_GROUPS, B_STATE)] * 2 + [_sds(B_GROUPS, 1)] + [_sds(B_WIDTH, B_STATE)] * 2,
                          )(a_re, a_im, log_dt, bt_re, bt_im, d1, d2, d3, d4)


def _pick_row(x, r):
    rows = lax.broadcasted_iota(jnp.int32, x.shape, 0)
    return jnp.sum(jnp.where(rows == r, x, 0.0), axis=0, keepdims=True)


S5_SEG = 8
S5_STEPS = 32
S5_WIDTH = S5_TILES * S5_LANES


def _seg_rows(block):
    return jnp.swapaxes(block, 0, 1).reshape(block.shape[1] * S5_SEG, block.shape[2])


def _seg_block(rows):
    return jnp.swapaxes(rows.reshape(rows.shape[0] // S5_SEG, S5_SEG, rows.shape[1]), 0, 1)


def _seq_specs(n_i, rev):
    at = (lambda i: n_i - 1 - i) if rev else (lambda i: i)
    seg = pl.BlockSpec((S5_SEG, S5_STEPS, B_WIDTH), lambda i: (0, at(i), 0))
    x_spec = pl.BlockSpec((S5_SEG * S5_STEPS, S5_WIDTH), lambda i: (at(i), 0))
    return seg, x_spec, _fix((S5_TILES, 128, S5_LANES)), _fix((S5_TILES, S5_LANES, 128)), _fix((1, S5_WIDTH)), _fix((S5_SEG, S5_WIDTH))


def _tile_dots(dot, lhs, w_ref, lhs_width):
    return jnp.concatenate([dot(lhs[:, t * lhs_width:(t + 1) * lhs_width], w_ref[t]) for t in range(S5_TILES)], axis=1)


def _s5_entries(name, end_r, end_i, abr, abi, steps, reverse):
    def body(er_ref, ei_ref, ar_ref, ai_ref, or_ref, oi_ref):
        pr, pi_ = ar_ref[...], ai_ref[...]
        for _ in range(int(math.log2(steps))):
            pr, pi_ = pr * pr - pi_ * pi_, 2.0 * pr * pi_
        er, ei = er_ref[...], ei_ref[...]
        rows = lax.broadcasted_iota(jnp.int32, er.shape, 0)
        cr, ci = jnp.zeros_like(pr), jnp.zeros_like(pr)
        out_r, out_i = jnp.zeros_like(er), jnp.zeros_like(er)
        for g in (range(S5_SEG - 2, -1, -1) if reverse else range(1, S5_SEG)):
            src = g + 1 if reverse else g - 1
            cr, ci = _pick_row(er, src) + pr * cr - pi_ * ci, _pick_row(ei, src) + pr * ci + pi_ * cr
            out_r, out_i = jnp.where(rows == g, cr, out_r), jnp.where(rows == g, ci, out_i)
        or_ref[...] = out_r
        oi_ref[...] = out_i

    return pl.pallas_call(body, name=name, out_shape=[_sds(*end_r.shape)] * 2)(end_r, end_i, abr, abi)


def _s5_seq_fwd(u, btr, bti, ctr, cti, abr, abi, entry, store, exchange=None):
    s_len = u.shape[0]
    seg_len = s_len // S5_SEG
    n_i = seg_len // S5_STEPS
    rows = S5_SEG * S5_STEPS

    def body(u_ref, btr_ref, bti_ref, ctr_ref, cti_ref, ar_ref, ai_ref, er_ref, ei_ref, *rest):
        if store:
            xr_ref, xi_ref, y_ref, endr_ref, endi_ref, sr_ref, si_ref = rest
        else:
            endr_ref, endi_ref, sr_ref, si_ref = rest
        i = pl.program_id(0)

        @pl.when(i == 0)
        def _():
            sr_ref[...] = er_ref[...]
            si_ref[...] = ei_ref[...]

        ar = jnp.broadcast_to(ar_ref[...], (S5_SEG, S5_WIDTH))
        ai = jnp.broadcast_to(ai_ref[...], (S5_SEG, S5_WIDTH))
        ub = _seg_rows(u_ref[...])
        br, bi = _tile_dots(_bdot, ub, btr_ref, 128), _tile_dots(_bdot, ub, bti_ref, 128)
        sr, si = sr_ref[...], si_ref[...]
        for s in range(S5_STEPS):
            at = slice(S5_SEG * s, S5_SEG * (s + 1))
            sr, si = ar * sr - ai * si + br[at], ar * si + ai * sr + bi[at]
            if store:
                xr_ref[at, :] = sr
                xi_ref[at, :] = si
        sr_ref[...] = sr
        si_ref[...] = si
        if store:
            y_ref[...] = _seg_block(_tile_dots(_bdot, xr_ref[...], ctr_ref, S5_LANES) - _tile_dots(_bdot, xi_ref[...], cti_ref, S5_LANES))

        @pl.when(i == n_i - 1)
        def _():
            endr_ref[...] = sr
            endi_ref[...] = si

    seg, x_spec, b_spec, c_spec, a_spec, e_spec = _seq_specs(n_i, False)
    ends = [_sds(S5_SEG, S5_WIDTH)] * 2
    full = [_sds(s_len, S5_WIDTH)] * 2 + [_sds(S5_SEG, seg_len, B_WIDTH)] if store else []
    return _call_with_exchange(
        body, "s5_scan_fwd" if store else "s5_ends_fwd", (n_i,),
        [seg, b_spec, b_spec, c_spec, c_spec, a_spec, a_spec, e_spec, e_spec],
        ([x_spec, x_spec, seg] if store else []) + [e_spec, e_spec], full + ends,
        [pltpu.VMEM((S5_SEG, S5_WIDTH), F32)] * 2,
        (u.reshape(S5_SEG, seg_len, B_WIDTH), btr, bti, ctr, cti, abr, abi, *entry), exchange)


def _s5_seq_bwd(dy, xr, xi, u, btr, bti, ctr, cti, abr, abi, g_entry, x_entry, full, exchange=None):
    s_len = dy.shape[0]
    seg_len = s_len // S5_SEG
    n_i = seg_len // S5_STEPS
    rows = S5_SEG * S5_STEPS

    def body(*refs):
        if full:
            (dy_ref, btr_ref, bti_ref, ctr_ref, cti_ref, ar_ref, ai_ref, ger_ref, gei_ref,
             xr_ref, xi_ref, xrp_ref, xip_ref, xer_ref, xei_ref, u_ref,
             du_ref, dbtr_ref, dbti_ref, dctr_ref, dcti_ref, dar_ref, dai_ref, str_ref, sti_ref,
             sr_ref, si_ref, gr_s, gi_s) = refs
        else:
            (dy_ref, btr_ref, bti_ref, ctr_ref, cti_ref, ar_ref, ai_ref, ger_ref, gei_ref, str_ref, sti_ref, sr_ref, si_ref) = refs
        i = pl.program_id(0)

        @pl.when(i == 0)
        def _():
            sr_ref[...] = ger_ref[...]
            si_ref[...] = gei_ref[...]
            if full:
                for r in (dbtr_ref, dbti_ref, dctr_ref, dcti_ref, dar_ref, dai_ref):
                    r[...] = jnp.zeros_like(r)

        ar = jnp.broadcast_to(ar_ref[...], (S5_SEG, S5_WIDTH))
        ai = -jnp.broadcast_to(ai_ref[...], (S5_SEG, S5_WIDTH))
        dyb = _seg_rows(dy_ref[...])
        gr, gi = _tile_dots(_bdot_nt, dyb, ctr_ref, 128), -_tile_dots(_bdot_nt, dyb, cti_ref, 128)
        sr, si = sr_ref[...], si_ref[...]
        for s in range(S5_STEPS - 1, -1, -1):
            at = slice(S5_SEG * s, S5_SEG * (s + 1))
            sr, si = ar * sr - ai * si + gr[at], ar * si + ai * sr + gi[at]
            if full:
                gr_s[at, :] = sr
                gi_s[at, :] = si
        sr_ref[...] = sr
        si_ref[...] = si

        @pl.when(i == n_i - 1)
        def _():
            str_ref[...] = sr
            sti_ref[...] = si

        if full:
            g_r, g_i = gr_s[...], gi_s[...]
            du_ref[...] = _seg_block(_tile_dots(_bdot_nt, g_r, btr_ref, S5_LANES) + _tile_dots(_bdot_nt, g_i, bti_ref, S5_LANES))
            ub = _seg_rows(u_ref[...])
            xr_b, xi_b = xr_ref[...], xi_ref[...]
            for t in range(S5_TILES):
                lanes, cols = slice(t * S5_LANES, (t + 1) * S5_LANES), slice(t * 128, (t + 1) * 128)
                dbtr_ref[t] += _bdot_tn(ub[:, cols], g_r[:, lanes])
                dbti_ref[t] += _bdot_tn(ub[:, cols], g_i[:, lanes])
                dctr_ref[t] += _bdot_tn(xr_b[:, lanes], dyb[:, cols])
                dcti_ref[t] -= _bdot_tn(xi_b[:, lanes], dyb[:, cols])
            first = i == n_i - 1
            xpr = jnp.concatenate([jnp.where(first, xer_ref[...], xrp_ref[...]), xr_b[:rows - S5_SEG]], axis=0)
            xpi = jnp.concatenate([jnp.where(first, xei_ref[...], xip_ref[...]), xi_b[:rows - S5_SEG]], axis=0)
            dar_ref[...] += jnp.sum(g_r * xpr + g_i * xpi, axis=0, keepdims=True)
            dai_ref[...] += jnp.sum(g_i * xpr - g_r * xpi, axis=0, keepdims=True)

    seg, x_spec, b_spec, c_spec, a_spec, e_spec = _seq_specs(n_i, True)
    halo = pl.BlockSpec((S5_SEG, S5_WIDTH), lambda i: (jnp.maximum((n_i - 1 - i) * S5_STEPS - 1, 0), 0))
    starts = [_sds(S5_SEG, S5_WIDTH)] * 2
    in_specs = [seg, b_spec, b_spec, c_spec, c_spec, a_spec, a_spec, e_spec, e_spec]
    args = [dy.reshape(S5_SEG, seg_len, B_WIDTH), btr, bti, ctr, cti, abr, abi, *g_entry]
    state = [pltpu.VMEM((S5_SEG, S5_WIDTH), F32)] * 2
    if not full:
        return _call_with_exchange(body, "s5_starts_bwd", (n_i,), in_specs, [e_spec, e_spec], starts, state, args, None)
    return _call_with_exchange(
        body, "s5_scan_bwd", (n_i,),
        in_specs + [x_spec, x_spec, halo, halo, e_spec, e_spec, seg],
        [seg, b_spec, b_spec, c_spec, c_spec, a_spec, a_spec, e_spec, e_spec],
        [_sds(S5_SEG, seg_len, B_WIDTH)] + [_sds(S5_TILES, 128, S5_LANES)] * 2 + [_sds(S5_TILES, S5_LANES, 128)] * 2
        + [_sds(1, S5_WIDTH)] * 2 + starts,
        state + [pltpu.VMEM((rows, S5_WIDTH), F32)] * 2,
        args + [xr, xi, xr, xi, *x_entry, u.reshape(S5_SEG, seg_len, B_WIDTH)], exchange)


def _blockdiag_b(bbar_t):
    blocks = bbar_t.reshape(S5_TILES, 8, B_GROUP, B_STATE)
    return jnp.einsum('jgmp,gh->jgmhp', blocks, jnp.eye(8, dtype=F32)).reshape(S5_TILES, 128, S5_LANES)


def _blockdiag_b_t(d):
    return jnp.einsum('jgmgp->jgmp', d.reshape(S5_TILES, 8, B_GROUP, 8, B_STATE)).reshape(B_WIDTH, B_STATE)


def _blockdiag_c(c):
    blocks = c.reshape(S5_TILES, 8, B_GROUP, B_STATE)
    return jnp.einsum('jgmp,gh->jhpgm', blocks, jnp.eye(8, dtype=F32)).reshape(S5_TILES, S5_LANES, 128)


def _blockdiag_c_t(d):
    return jnp.einsum('jgpgm->jgmp', d.reshape(S5_TILES, 8, B_STATE, 8, B_GROUP)).reshape(B_GROUPS, B_GROUP, B_STATE)


def _l0_out(os, ls, ga, gb, ypre, u, x, d_skip, glu_w, glu_b, w_out, post_g, gate, exchange=None):
    s_len = x.shape[0]

    def body(o0, o1, o2, l0, l1, l2, ga_ref, gb_ref, yp_ref, u_ref, x_ref, d_ref, gw_ref, gbias_ref, w_ref, pg_ref, gt_ref, x1_ref, y_ref):
        oa = _merge_gate(*[_from_res(r[...]) for r in (o0, o1, o2, l0, l1, l2)], ga_ref[...])
        yb = _s5_gelu(yp_ref[...], u_ref[...], d_ref[...])
        ob = _s5_glu(yb, _bdot(yb, gw_ref[...]) + gbias_ref[...], gb_ref[...])
        y = _bdot(oa, w_ref[0:512, :]) + _bdot(ob, w_ref[512:1024, :])
        y_ref[...] = y
        x1_ref[...] = _post_res(y, x_ref[...], pg_ref[...], gt_ref[...])

    vec, half = _fix((1, D_MODEL)), _fix((1, 512))
    return _call_with_exchange(
        body, "l0_out", (s_len // TM,),
        [_res_spec(d) for d in DILATIONS] * 2 + [_row(TM, 512)] * 4
        + [_row(TM, D_MODEL), half, _fix((512, 512)), half, _fix((D_MODEL, D_MODEL)), vec, vec],
        [_row(TM, D_MODEL)] * 2, [_sds(s_len, D_MODEL)] * 2, [],
        (*os, *ls, ga, gb, ypre, u, x, d_skip, glu_w, glu_b, w_out, post_g, gate), exchange)


def _l0_out_bwd(os, ls, ga, gb, ypre, u, x, y, d_skip, glu_w, glu_b, w_out, post_g, gate, dx1, exchange=None):
    s_len = x.shape[0]

    def body(o0, o1, o2, l0, l1, l2, ga_ref, gb_ref, yp_ref, u_ref, x_ref, y_ref, d_ref, gw_ref, gbias_ref, w_ref, pg_ref, gt_ref, dx1_ref,
             do0, do1, do2, dl0, dl1, dl2, dga_ref, dgb_ref, dyp_ref, du_ref, dd_ref, dgw_ref, dgbias_ref, dw_ref, dpg_ref, dgt_ref):
        _zero_at_first([dd_ref, dgw_ref, dgbias_ref, dw_ref, dpg_ref, dgt_ref])
        _, vjp2 = jax.vjp(_post_res, y_ref[...], x_ref[...], pg_ref[...], gt_ref[...])
        dy, _, dpg, dgt = vjp2(dx1_ref[...])
        _acc(dpg_ref, dpg)
        _acc(dgt_ref, dgt)
        oa, vjp_a = jax.vjp(_merge_gate, *[_from_res(r[...]) for r in (o0, o1, o2, l0, l1, l2)], ga_ref[...])
        yb, vjp_g = jax.vjp(_s5_gelu, yp_ref[...], u_ref[...], d_ref[...])
        gl = _bdot(yb, gw_ref[...]) + gbias_ref[...]
        ob, vjp_b = jax.vjp(_s5_glu, yb, gl, gb_ref[...])
        dw_ref[0:512, :] += _bdot_tn(oa, dy)
        dw_ref[512:1024, :] += _bdot_tn(ob, dy)
        d1, d2, d3, e1, e2, e3, dga = vjp_a(_bdot_nt(dy, w_ref[0:512, :]))
        for ref, val, d in zip((do0, do1, do2, dl0, dl1, dl2), (d1, d2, d3, e1, e2, e3), DILATIONS * 2):
            ref[...] = _to_res(val, d)
        dga_ref[...] = dga
        dyb, dgl, dgb = vjp_b(_bdot_nt(dy, w_ref[512:1024, :]))
        dgb_ref[...] = dgb
        dgw_ref[...] += _bdot_tn(yb, dgl)
        _acc(dgbias_ref, jnp.sum(dgl, axis=0, keepdims=True))
        dyp, du, dd = vjp_g(dyb + _bdot_nt(dgl, gw_ref[...]))
        dyp_ref[...] = dyp
        du_ref[...] = du
        _acc(dd_ref, dd)

    vec, half = _fix((1, D_MODEL)), _fix((1, 512))
    r5, r10 = _row(TM, 512), _row(TM, D_MODEL)
    res6 = [_res_spec(d) for d in DILATIONS] * 2
    return _call_with_exchange(
        body, "l0_out_bwd", (s_len // TM,),
        res6 + [r5] * 4 + [r10, r10, half, _fix((512, 512)), half, _fix((D_MODEL, D_MODEL)), vec, vec, r10],
        res6 + [r5] * 4 + [half, _fix((512, 512)), half, _fix((D_MODEL, D_MODEL)), vec, vec],
        [_sds(*_res_shape(s_len, d)) for d in DILATIONS] * 2 + [_sds(s_len, 512)] * 4
        + [_sds(1, 512), _sds(512, 512), _sds(1, 512), _sds(D_MODEL, D_MODEL), _sds(1, D_MODEL), _sds(1, D_MODEL)],
        [], (*os, *ls, ga, gb, ypre, u, x, y, d_skip, glu_w, glu_b, w_out, post_g, gate, dx1), exchange)


def _l1_front(x, pre_g, scale, shift, w_in):
    s_len = x.shape[0]

    def body(x_ref, g_ref, sc_ref, sh_ref, w_ref, raw_ref, gate_ref, ba_ref, h_ref):
        hb = _pre_mod(x_ref[...], g_ref[...], sc_ref[...], sh_ref[...]).astype(BF)
        h_ref[...] = hb
        z = jnp.dot(hb, w_ref[...], preferred_element_type=F32)
        raw_ref[...] = z[:, 0:QKV]
        gate_ref[...] = z[:, QKV:QKV + 1024]
        ba_ref[...] = z[:, QKV + 1024:C_IN_PAD]

    vec = _fix((1, D_MODEL))
    return pl.pallas_call(
        body, name="l1_front", grid=(s_len // TM,),
        in_specs=[_row(TM, D_MODEL), vec, vec, vec, _fix((D_MODEL, C_IN_PAD))],
        out_specs=[_row(TM, QKV), _row(TM, 1024), _row(TM, 128), _row(TM, D_MODEL)],
        out_shape=[_sds(s_len, QKV), _sds(s_len, 1024), _sds(s_len, 128), _sds(s_len, D_MODEL, dtype=BF)],
        compiler_params=_cp("arbitrary"),
    )(x, pre_g, scale, shift, w_in)


def _bg_fn(ba, alog_row, dtb_row):
    lane = lax.broadcasted_iota(jnp.int32, (1, 128), 1)
    g = -jnp.exp(alog_row) * jax.nn.softplus(ba + dtb_row)
    return jnp.where(lane < C_HEADS, jax.nn.sigmoid(ba), jnp.where(lane < 2 * C_HEADS, g, 0.0))


def _act_q(c):
    q = jax.nn.silu(c)
    return q * lax.rsqrt(jnp.sum(q * q, axis=-1, keepdims=True) + EPS) * (C_DK ** -0.5)


def _act_k(c):
    k = jax.nn.silu(c)
    return k * lax.rsqrt(jnp.sum(k * k, axis=-1, keepdims=True) + EPS)


def _act_of(s):
    return _act_q if s < 8 else (_act_k if s < 16 else jax.nn.silu)


def _conv_taps(prev8, tile_ref, sl, next8=None):
    rows = tile_ref.shape[0]
    head = jnp.concatenate([prev8, tile_ref[0:8, sl]], axis=0)
    tail = None if next8 is None else jnp.concatenate([tile_ref[rows - 8:rows, sl], next8], axis=0)
    taps = []
    for j in range(C_CONV):
        shift = C_CONV - 1 - j
        pieces = [head[8:] if shift == 0 else pltpu.roll(head, shift, 0)[8:], tile_ref[pl.ds(8 - shift, rows - 8), sl]]
        if tail is not None:
            pieces.append(tail[8:] if shift == 0 else pltpu.roll(tail, shift, 0)[8:])
        taps.append(jnp.concatenate(pieces, axis=0))
    return taps


def _gdn_prep(raw, ba, conv_w, alog_row, dtb_row):
    s_len = raw.shape[0]

    def body(raw_ref, halo_ref, ba_ref, w_ref, al_ref, dt_ref, qkv_ref, bg_ref):
        bg_ref[...] = _bg_fn(ba_ref[...], al_ref[...], dt_ref[...])
        has_prev = (pl.program_id(0) > 0).astype(F32)
        for s in range(24):
            sl = slice(s * 128, (s + 1) * 128)
            taps = _conv_taps(halo_ref[:, sl] * has_prev, raw_ref, sl)
            conv = w_ref[3:4, sl] * taps[3]
            for j in range(3):
                conv = conv + w_ref[j:j + 1, sl] * taps[j]
            qkv_ref[:, sl] = _act_of(s)(conv)

    halo = pl.BlockSpec((8, QKV), lambda i: (jnp.maximum(i * (TM // 8) - 1, 0), 0))
    row128 = _fix((1, 128))
    return pl.pallas_call(
        body, name="gdn_prep", grid=(s_len // TM,),
        in_specs=[_row(TM, QKV), halo, _row(TM, 128), _fix((C_CONV, QKV)), row128, row128],
        out_specs=[_row(TM, QKV), _row(TM, 128)],
        out_shape=[_sds(s_len, QKV), _sds(s_len, 128)],
        compiler_params=_cp("arbitrary"),
    )(raw, raw, ba, conv_w, alog_row, dtb_row)


def _gdn_prep_bwd(raw, ba, conv_w, alog_row, dtb_row, dq, dk, dv, dbg):
    s_len = raw.shape[0]
    n_tiles = s_len // TM

    def body(raw_ref, prev_ref, next_ref, ba_ref, w_ref, al_ref, dt_ref, dq_ref, dqn_ref, dk_ref, dkn_ref, dv_ref, dvn_ref, dbg_ref,
             draw_ref, dba_ref, dw_ref, dal_ref, ddt_ref, dconv_ref):
        _zero_at_first([dw_ref, dal_ref, ddt_ref])
        i = pl.program_id(0)
        _, vjp_bg = jax.vjp(_bg_fn, ba_ref[...], al_ref[...], dt_ref[...])
        dba, dal, ddt = vjp_bg(dbg_ref[...])
        dba_ref[...] = dba
        _acc(dal_ref, dal)
        _acc(ddt_ref, ddt)
        has_prev = (i > 0).astype(F32)
        has_next = (i < n_tiles - 1).astype(F32)
        ct_refs = ((dq_ref, dqn_ref), (dk_ref, dkn_ref), (dv_ref, dvn_ref))
        for s in range(24):
            sl = slice(s * 128, (s + 1) * 128)
            hl = slice((s % 8) * 128, (s % 8 + 1) * 128)
            tile_ref, nxt_ref = ct_refs[s // 8]
            taps = _conv_taps(prev_ref[:, sl] * has_prev, raw_ref, sl, next_ref[:, sl] * has_next)
            conv = w_ref[3:4, sl] * taps[3]
            for j in range(3):
                conv = conv + w_ref[j:j + 1, sl] * taps[j]
            ct = jnp.concatenate([tile_ref[:, hl], nxt_ref[:, hl] * has_next], axis=0)
            _, vjp_act = jax.vjp(_act_of(s), conv)
            dconv, = vjp_act(ct)
            dconv_ref[...] = dconv
            draw = w_ref[3:4, sl] * dconv[:TM]
            for j in range(3):
                draw = draw + w_ref[j:j + 1, sl] * dconv_ref[pl.ds(3 - j, TM), :]
            draw_ref[:, sl] = draw
            for j in range(4):
                dw_ref[j:j + 1, sl] += jnp.sum(dconv[:TM] * taps[j][:TM], axis=0, keepdims=True)

    prev = pl.BlockSpec((8, QKV), lambda i: (jnp.maximum(i * (TM // 8) - 1, 0), 0))
    nxt = lambda n: pl.BlockSpec((8, n), lambda i: (jnp.minimum((i + 1) * (TM // 8), s_len // 8 - 1), 0))
    row128 = _fix((1, 128))
    ct_specs = [_row(TM, 1024), nxt(1024)] * 3
    return pl.pallas_call(
        body, name="gdn_prep_bwd", grid=(n_tiles,),
        in_specs=[_row(TM, QKV), prev, nxt(QKV), _row(TM, 128), _fix((C_CONV, QKV)), row128, row128] + ct_specs + [_row(TM, 128)],
        out_specs=[_row(TM, QKV), _row(TM, 128), _fix((C_CONV, QKV)), row128, row128],
        out_shape=[_sds(s_len, QKV), _sds(s_len, 128), _sds(C_CONV, QKV), _sds(1, 128), _sds(1, 128)],
        scratch_shapes=[pltpu.VMEM((TM + 8, 128), F32)],
        compiler_params=_cp("arbitrary"),
    )(raw, raw, raw, ba, conv_w, alog_row, dtb_row, dq, dq, dk, dk, dv, dv, dbg)


def _tein(eq, a, b):
    return jnp.einsum(eq, a, b, precision=lax.Precision.HIGH, preferred_element_type=F32)


def _unit_lower_inverse(lower):
    ri = lax.broadcasted_iota(jnp.int32, (C_CHUNK, C_CHUNK), 0)
    ci = lax.broadcasted_iota(jnp.int32, (C_CHUNK, C_CHUNK), 1)
    eye = (ri == ci).astype(F32)[None]
    mm = functools.partial(_bein, 'hij,hjk->hik')
    same_block = lambda size: (ri // size == ci // size)[None]
    n_mat = jnp.where(same_block(4), -lower, 0.0)
    inv = mm(eye + n_mat, eye + mm(n_mat, n_mat))
    for size in (4, 8, 16, 32):
        below = jnp.where(same_block(2 * size) & jnp.logical_not(same_block(size)), lower, 0.0)
        inv = inv - mm(inv, mm(below, inv))
    inv = _tein('hij,hjk->hik', inv, 2.0 * eye - _tein('hij,hjk->hik', eye + lower, inv))
    return jnp.where((ri >= ci)[None], inv, 0.0)


@jax.custom_vjp
def _known_inverse(lower, inv):
    return inv


def _known_inverse_fwd(lower, inv):
    return inv, inv


def _known_inverse_bwd(inv, d_inv):
    d_lower = -_bein('hik,hjk->hij', _bein('hji,hjk->hik', inv, d_inv), inv)
    return d_lower, jnp.zeros_like(inv)


_known_inverse.defvjp(_known_inverse_fwd, _known_inverse_bwd)


def _gdn_local(q, k, v, bgs, inv_known=None):
    lane = lax.broadcasted_iota(jnp.int32, (1, 128), 1)
    ri = lax.broadcasted_iota(jnp.int32, (C_CHUNK, C_CHUNK), 0)
    ci = lax.broadcasted_iota(jnp.int32, (C_CHUNK, C_CHUNK), 1)
    row_id = lax.broadcasted_iota(jnp.int32, (128, C_CHUNK), 0)
    beta, gc, gcj = [], [], []
    for bg in bgs:
        gc_t = _hdot((ri >= ci).astype(F32), bg)
        gc_rows = gc_t.T
        for h in range(C_HEADS):
            beta.append(jnp.sum(jnp.where(lane == h, bg, 0.0), axis=-1, keepdims=True))
            gc.append(jnp.sum(jnp.where(lane == C_HEADS + h, gc_t, 0.0), axis=-1, keepdims=True))
            gcj.append(jnp.sum(jnp.where(row_id == C_HEADS + h, gc_rows, 0.0), axis=0, keepdims=True))
    beta, gc, gcj = jnp.stack(beta, axis=0), jnp.stack(gc, axis=0), jnp.stack(gcj, axis=0)
    tril, strict = (ri >= ci)[None], (ri > ci)[None]
    decay = jnp.exp(jnp.where(tril, gc - gcj, -1e30))
    kb = k * beta
    lower = jnp.where(strict, _bein('hid,hjd->hij', kb, k) * decay, 0.0)
    inv = _unit_lower_inverse(lower) if inv_known is None else _known_inverse(lower, inv_known)
    egc = jnp.exp(gc)
    u_c = _bein('hij,hjd->hid', inv, v * beta)
    w_c = _bein('hij,hjd->hid', inv, kb * egc)
    aqk = _bein('hid,hjd->hij', q, k) * decay
    rowi = lax.broadcasted_iota(jnp.int32, (1, C_CHUNK, 1), 1)
    g_last = jnp.sum(jnp.where(rowi == C_CHUNK - 1, gc, 0.0), axis=1, keepdims=True)
    kd = k * jnp.exp(g_last - gc)
    return (u_c, w_c, aqk, q * egc, kd, jnp.exp(g_last)), inv


def _gdn_state(local, state):
    u_c, w_c, aqk, qg, kd, dec = local
    v_new = u_c - _bein('hik,hkv->hiv', w_c, state)
    o = _bein('hik,hkv->hiv', qg, state) + _bein('hij,hjv->hiv', aqk, v_new)
    return o, state * dec + _bein('hik,hiv->hkv', kd, v_new)


C_SUB = 4


def _gdn_group(q, k, v, bgs, state, inv_known=None):
    local, inv = _gdn_local(q, k, v, bgs, inv_known)
    outs = []
    for s in range(len(bgs)):
        o, state = _gdn_state(tuple(t[s * C_HEADS:(s + 1) * C_HEADS] for t in local), state)
        outs.append(o)
    return outs, state, inv


def _heads(ref):
    return jnp.stack([ref[s * C_CHUNK:(s + 1) * C_CHUNK, h * C_DK:(h + 1) * C_DK] for s in range(C_SUB) for h in range(C_HEADS)], axis=0)


def _put_heads(ref, sub, val):
    rows = slice(sub * C_CHUNK, (sub + 1) * C_CHUNK)
    for h in range(C_HEADS):
        ref[rows, h * C_DK:(h + 1) * C_DK] = val[h]


def _gdn_specs(s_len, rev):
    rows = C_SUB * C_CHUNK
    n_g = s_len // rows
    at = (lambda i: n_g - 1 - i) if rev else (lambda i: i)
    col = lambda c: pl.BlockSpec((rows, 1024), lambda i: (at(i), c))
    row128 = pl.BlockSpec((rows, 128), lambda i: (at(i), 0))
    state = pl.BlockSpec((1, C_HEADS, C_DK, C_DK), lambda i: (at(i), 0, 0, 0))
    inv = pl.BlockSpec((1, C_SUB * C_HEADS, C_CHUNK, C_CHUNK), lambda i: (at(i), 0, 0, 0))
    return n_g, col, row128, state, inv


def _gdn_fwd(qkv, bg):
    s_len = qkv.shape[0]
    n_g, col, row128, state_spec, inv_spec = _gdn_specs(s_len, False)

    def body(q_ref, k_ref, v_ref, bg_ref, o_ref, ss_ref, inv_ref, st_ref):
        _zero_at_first([st_ref])
        s0 = st_ref[...]
        ss_ref[0] = s0
        bgs = [bg_ref[s * C_CHUNK:(s + 1) * C_CHUNK, :] for s in range(C_SUB)]
        outs, s2, inv = _gdn_group(_heads(q_ref), _heads(k_ref), _heads(v_ref), bgs, s0)
        st_ref[...] = s2
        inv_ref[0] = inv
        for s in range(C_SUB):
            _put_heads(o_ref, s, outs[s])

    return pl.pallas_call(
        body, name="gdn_fwd", grid=(n_g,),
        in_specs=[col(0), col(1), col(2), row128],
        out_specs=[col(0), state_spec, inv_spec],
        out_shape=[_sds(s_len, 1024), _sds(n_g, C_HEADS, C_DK, C_DK), _sds(n_g, C_SUB * C_HEADS, C_CHUNK, C_CHUNK)],
        scratch_shapes=[pltpu.VMEM((C_HEADS, C_DK, C_DK), F32)],
        compiler_params=_cp("arbitrary"),
    )(qkv, qkv, qkv, bg)


def _gdn_bwd(qkv, bg, states, invs, do):
    s_len = qkv.shape[0]
    n_g, col, row128, state_spec, inv_spec = _gdn_specs(s_len, True)

    def body(q_ref, k_ref, v_ref, bg_ref, ss_ref, inv_ref, do_ref, dq_ref, dk_ref, dv_ref, dbg_ref, ds_ref):
        _zero_at_first([ds_ref])
        inv_known = inv_ref[0]

        def group(q, k, v, bgs, st):
            outs, st2, _ = _gdn_group(q, k, v, bgs, st, inv_known)
            return outs, st2

        bgs = [bg_ref[s * C_CHUNK:(s + 1) * C_CHUNK, :] for s in range(C_SUB)]
        _, vjp = jax.vjp(group, _heads(q_ref), _heads(k_ref), _heads(v_ref), bgs, ss_ref[0])
        douts = [jnp.stack([do_ref[s * C_CHUNK:(s + 1) * C_CHUNK, h * C_DK:(h + 1) * C_DK] for h in range(C_HEADS)], axis=0)
                 for s in range(C_SUB)]
        dq, dk, dv, dbgs, ds = vjp((douts, ds_ref[...]))
        ds_ref[...] = ds
        for s in range(C_SUB):
            dbg_ref[s * C_CHUNK:(s + 1) * C_CHUNK, :] = dbgs[s]
            for ref, val in ((dq_ref, dq), (dk_ref, dk), (dv_ref, dv)):
                _put_heads(ref, s, val[s * C_HEADS:(s + 1) * C_HEADS])

    return pl.pallas_call(
        body, name="gdn_bwd", grid=(n_g,),
        in_specs=[col(0), col(1), col(2), row128, state_spec, inv_spec, col(0)],
        out_specs=[col(0), col(0), col(0), row128],
        out_shape=[_sds(s_len, 1024)] * 3 + [_sds(s_len, 128)],
        scratch_shapes=[pltpu.VMEM((C_HEADS, C_DK, C_DK), F32)],
        compiler_params=_cp("arbitrary"),
    )(qkv, qkv, qkv, bg, states, invs, do)


def _head_norm_gate(o, gate, norm_g):
    return (_rms(o) * norm_g) * jax.nn.silu(gate)


def _l1_out_fb(o, gate_c, x1, target, norm_g, w_out, post_g, gate):
    s_len = x1.shape[0]

    def body(o_ref, gc_ref, x1_ref, t_ref, ng_ref, w_ref, pg_ref, gt_ref,
             loss_ref, dres_ref, do_ref, dgc_ref, dw_ref, dng_ref, dpg_ref, dgt_ref):
        _zero_at_first([loss_ref, dw_ref, dng_ref, dpg_ref, dgt_ref])
        ng = ng_ref[...]
        ons, vjps = [], []
        for h in range(C_HEADS):
            sl = slice(h * C_DK, (h + 1) * C_DK)
            on, vjp_h = jax.vjp(_head_norm_gate, o_ref[:, sl], gc_ref[:, sl], ng)
            ons.append(on)
            vjps.append(vjp_h)
        on_all = jnp.concatenate(ons, axis=-1)
        y = _bdot(on_all, w_ref[...])
        x2, vjp2 = jax.vjp(_post_res, y, x1_ref[...], pg_ref[...], gt_ref[...])
        err = x2 - t_ref[...]
        _acc(loss_ref, jnp.full((1, 128), 0.5 * jnp.sum(jnp.mean(err * err, axis=-1)), F32))
        dx2 = err * (1.0 / D_MODEL)
        dy, _, dpg, dgt = vjp2(dx2)
        dres_ref[...] = dx2
        _acc(dpg_ref, dpg)
        _acc(dgt_ref, dgt)
        dw_ref[...] += _bdot_tn(on_all, dy)
        don = _bdot_nt(dy, w_ref[...])
        for h in range(C_HEADS):
            sl = slice(h * C_DK, (h + 1) * C_DK)
            do_h, dgc_h, dng = vjps[h](don[:, sl])
            do_ref[:, sl] = do_h
            dgc_ref[:, sl] = dgc_h
            _acc(dng_ref, dng)

    vec, r10 = _fix((1, D_MODEL)), _row(TM, D_MODEL)
    row128 = _fix((1, 128))
    return pl.pallas_call(
        body, name="l1_out_fb", grid=(s_len // TM,),
        in_specs=[r10, r10, r10, r10, row128, _fix((D_MODEL, D_MODEL)), vec, vec],
        out_specs=[row128, r10, r10, r10, _fix((D_MODEL, D_MODEL)), row128, vec, vec],
        out_shape=[_sds(1, 128), _sds(s_len, D_MODEL), _sds(s_len, D_MODEL), _sds(s_len, D_MODEL),
                   _sds(D_MODEL, D_MODEL), _sds(1, 128), _sds(1, D_MODEL), _sds(1, D_MODEL)],
        compiler_params=_cp("arbitrary"),
    )(o, gate_c, x1, target, norm_g, w_out, post_g, gate)


def _row_of(v, width, at):
    return jnp.zeros((1, width), F32).at[0, at:at + v.shape[-1]].set(v.reshape(-1))


def _local_step(x, target, mod, wd, comm=None):
    s_len = x.shape[0]
    shift0, scale0, gate0 = (mod[0:1, i * 1024:(i + 1) * 1024] for i in range(3))
    shift1, scale1, gate1 = (mod[1:2, i * 1024:(i + 1) * 1024] for i in range(3))
    pre_g0, pre_g1 = wd["pre_g"][0:1], wd["pre_g"][1:2]
    post_g0, post_g1 = wd["post_g"][0:1], wd["post_g"][1:2]
    w_in0 = wd["ab_w_in"].astype(BF)
    d_skip, glu_b = wd["s5_d"].reshape(1, 512), wd["s5_glu_b"].reshape(1, 512)
    norm_g = wd["gdn_norm_g"].reshape(1, 128)
    alog_row = _row_of(wd["gdn_a_log"], 128, C_HEADS)
    dtb_row = _row_of(wd["gdn_dt_bias"], 128, C_HEADS)
    conv_w = wd["gdn_conv"]

    a_re, a_im = wd["s5_a_re"], wd["s5_a_im"]
    log_dt = wd["s5_log_dt"].reshape(B_GROUPS, 1)
    bt_re = wd["s5_b_re"].transpose(0, 2, 1).reshape(B_WIDTH, B_STATE)
    bt_im = wd["s5_b_im"].transpose(0, 2, 1).reshape(B_WIDTH, B_STATE)
    abar_r, abar_i, bbar_r, bbar_i = _s5_params(a_re, a_im, log_dt, bt_re, bt_im)
    abr, abi = abar_r.reshape(1, -1), abar_i.reshape(1, -1)
    btr, bti = _blockdiag_b(bbar_r).astype(BF), _blockdiag_b(bbar_i).astype(BF)
    ctr, cti = _blockdiag_c(wd["s5_c_re"]).astype(BF), _blockdiag_c(wd["s5_c_im"]).astype(BF)

    table = _bucket_table()
    biases = _attn_bias(wd["rel_bias"], jnp.asarray(table))
    front = _l0_front(x, pre_g0, scale0, shift0, w_in0)
    qs, ks, vs = front[0:3], front[3:6], front[6:9]
    u, ga, gb, h0 = front[9:]
    riders = [None] * 4 if comm is None else comm.late_exchanges()
    os, ls, got = [], [], []
    for i in range(3):
        (o_d, l_d), g = _attn_fwd(qs[i], ks[i], vs[i], biases[i], exchange=riders[i])
        os.append(o_d)
        ls.append(l_d)
        got.append(g)
    seg_len = s_len // S5_SEG
    zero_state = (jnp.zeros((S5_SEG, S5_WIDTH), F32),) * 2
    ends, _ = _s5_seq_fwd(u, btr, bti, ctr, cti, abr, abi, zero_state, False)
    x_entry = _s5_entries("s5_entries_fwd", *ends, abr, abi, seg_len, False)
    (xr, xi, ypre3, _, _), g = _s5_seq_fwd(u, btr, bti, ctr, cti, abr, abi, x_entry, True, exchange=riders[3])
    got.append(g)
    ypre = ypre3.reshape(s_len, B_WIDTH)
    rider = None
    if comm is not None:
        mine0, mine1 = comm.late_halves(got)
        wd = {**wd, **comm.full_weights(["ab_w_out", "s5_glu_w"], mine0, _sibling_exchange("gather_w_sibling_l0", mine0))}
        rider = (mine1, "sibling")
    w_out0 = wd["ab_w_out"].astype(BF)
    glu_w = wd["s5_glu_w"].astype(BF)
    (x1, y0), theirs1 = _l0_out(os, ls, ga, gb, ypre, u, x, d_skip, glu_w, glu_b, w_out0, post_g0, gate0, exchange=rider)
    if comm is not None:
        wd = {**wd, **comm.full_weights(["gdn_w_in", "gdn_w_out"], mine1, theirs1)}
    w_in1 = wd["gdn_w_in"]
    if w_in1.shape[1] == C_IN:
        w_in1 = jnp.concatenate([w_in1, jnp.zeros((D_MODEL, C_IN_PAD - C_IN), w_in1.dtype)], axis=1)
    w_in1 = w_in1.astype(BF)
    w_out1 = wd["gdn_w_out"].astype(BF)

    raw, gate_c, ba, h1 = _l1_front(x1, pre_g1, scale1, shift1, w_in1)
    qkv, bg = _gdn_prep(raw, ba, conv_w, alog_row, dtb_row)
    o_gdn, states, invs = _gdn_fwd(qkv, bg)
    loss_row, dres1, do_gdn, dgate_c, dw_out1, dnorm_g, dpost_g1, dgate1 = _l1_out_fb(
        o_gdn, gate_c, x1, target, norm_g, w_out1, post_g1, gate1)

    dq1, dk1, dv1, dbg = _gdn_bwd(qkv, bg, states, invs, do_gdn)
    draw, dba, dconv_w, dalog_row, ddtb_row = _gdn_prep_bwd(raw, ba, conv_w, alog_row, dtb_row, dq1, dk1, dv1, dbg)
    dz1, dx1, dpre_g1, dscale1, dshift1 = _front_bwd(
        "l1_front_bwd", x1, pre_g1, scale1, shift1, w_in1, dres1, [[draw], [dgate_c], [dba]], [QKV, 1024, 128])
    dw_in1 = _matmul_tn("l1_dw_in", h1, dz1, 1408)

    l1_names, l0_names = ["gdn_w_in", "gdn_w_out"], ["ab_w_out", "s5_glu_w"]
    rider = None if comm is None else (comm.split_halves({"gdn_w_in": dw_in1, "gdn_w_out": dw_out1}), "sibling")
    l0b, from_sibling1 = _l0_out_bwd(os, ls, ga, gb, ypre, u, x, y0, d_skip, glu_w, glu_b, w_out0, post_g0, gate0, dx1, exchange=rider)
    dos, dls = l0b[0:3], l0b[3:6]
    dga, dgb, dypre, du_skip, dd_skip, dglu_w, dglu_b, dw_out0, dpost_g0, dgate0 = l0b[6:]
    rider = None if comm is None else (comm.split_halves({"ab_w_out": dw_out0, "s5_glu_w": dglu_w}), "sibling")
    starts, _ = _s5_seq_bwd(dypre, None, None, None, btr, bti, ctr, cti, abr, abi, zero_state, None, False)
    g_entry = _s5_entries("s5_entries_bwd", *starts, abr, -abi, seg_len, True)
    (du3, dbtr, dbti, dctr, dcti, dabr, dabi, _, _), from_sibling0 = _s5_seq_bwd(
        dypre, xr, xi, u, btr, bti, ctr, cti, abr, abi, g_entry, x_entry, True, exchange=rider)
    du_scan = du3.reshape(s_len, B_WIDTH)
    riders = [None] * 3
    if comm is not None:
        riders = [(comm.chip_partials(l1_names, from_sibling1), "scatter"), (comm.chip_partials(l0_names, from_sibling0), "scatter"), None]
    dqs, dks, dvs, dbs = [], [], [], []
    for i in range(3):
        (dq_d, dk_d, dv_d, db_d), got_d = _attn_bwd(qs[i], ks[i], vs[i], biases[i], os[i], ls[i], dos[i], dls[i], exchange=riders[i])
        dqs.append(dq_d)
        dks.append(dk_d)
        dvs.append(dv_d)
        dbs.append(db_d)
        if comm is not None and riders[i] is not None:
            comm.received.update(zip((l1_names, l0_names)[i], got_d))
    parts = [dqs, dks, dvs, [du_skip, du_scan], [dga], [dgb]]
    dz0, grad_x, dpre_g0, dscale0, dshift0 = _front_bwd(
        "l0_front_bwd", x, pre_g0, scale0, shift0, w_in0, dx1, parts, [512] * 6)
    dw_in0 = _matmul_tn("l0_dw_in", h0, dz0, 768)

    idx_rows = jnp.asarray(table.reshape(3, -1), F32)
    drel = _rel_bias_grad(dbs, idx_rows).T
    da_re, da_im, dlog_dt, dbt_re, dbt_im = _s5_params_bwd(
        a_re, a_im, log_dt, bt_re, bt_im, dabr.reshape(B_GROUPS, B_STATE), dabi.reshape(B_GROUPS, B_STATE),
        _blockdiag_b_t(dbtr), _blockdiag_b_t(dbti))
    unb = lambda d: d.reshape(B_GROUPS, B_GROUP, B_STATE).transpose(0, 2, 1)
    grads = {
        "pre_g": jnp.concatenate([dpre_g0, dpre_g1], 0), "post_g": jnp.concatenate([dpost_g0, dpost_g1], 0),
        "rel_bias": drel, "ab_w_in": dw_in0, "ab_w_out": dw_out0,
        "s5_a_re": da_re, "s5_a_im": da_im, "s5_log_dt": dlog_dt.reshape(B_GROUPS),
        "s5_b_re": unb(dbt_re), "s5_b_im": unb(dbt_im),
        "s5_c_re": _blockdiag_c_t(dctr), "s5_c_im": _blockdiag_c_t(dcti),
        "s5_d": dd_skip.reshape(512), "s5_glu_w": dglu_w, "s5_glu_b": dglu_b.reshape(512),
        "gdn_w_in": dw_in1[:, :C_IN], "gdn_conv": dconv_w,
        "gdn_a_log": dalog_row[0, C_HEADS:2 * C_HEADS], "gdn_dt_bias": ddtb_row[0, C_HEADS:2 * C_HEADS],
        "gdn_norm_g": dnorm_g.reshape(128), "gdn_w_out": dw_out1,
    }
    dmod = jnp.concatenate([jnp.concatenate([dshift0, dscale0, dgate0], 1), jnp.concatenate([dshift1, dscale1, dgate1], 1)], 0)
    return loss_row[0, 0], grad_x, grads, dmod


def _place():
    return lax.axis_index("x"), lax.axis_index("y"), lax.axis_index("c")


def _flip(v, bit):
    return 1 - v if bit else v


def _hbm_call(name, body, arrs, out_shapes, n_sem):
    any_spec = pl.BlockSpec(memory_space=pl.ANY)
    return pl.pallas_call(
        body, name=name,
        in_specs=[any_spec] * len(arrs), out_specs=[any_spec] * len(out_shapes), out_shape=out_shapes,
        scratch_shapes=[pltpu.SemaphoreType.DMA((n_sem,)), pltpu.SemaphoreType.DMA((n_sem,))],
    )(*arrs)


def _own_slot(gathered, own, slot):
    idx = lax.broadcasted_iota(jnp.int32, (gathered.shape[0],) + (1,) * own.ndim, 0)
    return jnp.where(idx == slot, own[None], gathered)


def _all_gather8(name, arr):
    def body(x_ref, out_ref, send_sems, recv_sems):
        x, y, c = _place()
        me = 4 * x + 2 * y + c
        sends, recvs = [], []
        for m in range(1, 8):
            peer = (_flip(x, m & 4), _flip(y, m & 2), _flip(c, m & 1))
            sends.append(pltpu.make_async_remote_copy(x_ref, out_ref.at[me], send_sems.at[m - 1], recv_sems.at[m - 1],
                                                      device_id=peer, device_id_type=MESH))
            recvs.append(pltpu.make_async_remote_copy(x_ref, out_ref.at[4 * peer[0] + 2 * peer[1] + peer[2]], send_sems.at[m - 1],
                                                      recv_sems.at[m - 1], device_id=peer, device_id_type=MESH))
        for cp in sends:
            cp.start()
        for cp in recvs:
            cp.wait_recv()
        for cp in sends:
            cp.wait_send()

    return _hbm_call(name, body, [arr], [jax.ShapeDtypeStruct((8,) + arr.shape, arr.dtype)], 7)[0]


def _all_to_all8(name, arr):
    def body(x_ref, out_ref, send_sems, recv_sems):
        x, y, c = _place()
        me = 4 * x + 2 * y + c
        sends, recvs = [], []
        for m in range(1, 8):
            peer = (_flip(x, m & 4), _flip(y, m & 2), _flip(c, m & 1))
            peer_id = 4 * peer[0] + 2 * peer[1] + peer[2]
            sends.append(pltpu.make_async_remote_copy(x_ref.at[peer_id], out_ref.at[me], send_sems.at[m - 1], recv_sems.at[m - 1],
                                                      device_id=peer, device_id_type=MESH))
            recvs.append(pltpu.make_async_remote_copy(x_ref.at[peer_id], out_ref.at[peer_id], send_sems.at[m - 1], recv_sems.at[m - 1],
                                                      device_id=peer, device_id_type=MESH))
        for cp in sends:
            cp.start()
        for cp in recvs:
            cp.wait_recv()
        for cp in sends:
            cp.wait_send()

    return _hbm_call(name, body, [arr], [jax.ShapeDtypeStruct(arr.shape, arr.dtype)], 7)[0]


def _chip_copies(ins, outs, send_sems, recv_sems, scatter):
    x, y, c = _place()
    mine = 2 * x + y
    sends, recvs = [], []
    for a in range(len(ins)):
        for m in range(1, 4):
            px, py = _flip(x, m & 2), _flip(y, m & 1)
            k = 3 * a + m - 1
            src = ins[a].at[2 * px + py] if scatter else ins[a]
            sends.append(pltpu.make_async_remote_copy(src, outs[a].at[mine], send_sems.at[k], recv_sems.at[k],
                                                      device_id=(px, py, c), device_id_type=MESH))
            recvs.append(pltpu.make_async_remote_copy(src, outs[a].at[2 * px + py], send_sems.at[k], recv_sems.at[k],
                                                      device_id=(px, py, c), device_id_type=MESH))
    return sends, recvs


def _chip_shapes(arrs, scatter):
    return [jax.ShapeDtypeStruct(a.shape if scatter else (4,) + a.shape, a.dtype) for a in arrs]


def _chip_exchange(name, arrs, scatter):
    n = len(arrs)

    def body(*refs):
        sends, recvs = _chip_copies(refs[:n], refs[n:2 * n], refs[2 * n], refs[2 * n + 1], scatter)
        for cp in sends:
            cp.start()
        for cp in recvs:
            cp.wait_recv()
        for cp in sends:
            cp.wait_send()

    return _hbm_call(name, body, arrs, _chip_shapes(arrs, scatter), 3 * n)


def _call_with_exchange(body, name, grid, in_specs, out_specs, out_shape, scratch_shapes, args, exchange):
    if exchange is None:
        return pl.pallas_call(body, name=name, grid=grid, in_specs=in_specs, out_specs=out_specs, out_shape=out_shape,
                              scratch_shapes=scratch_shapes, compiler_params=_cp(*["arbitrary"] * len(grid)))(*args), []
    arrs, kind = exchange
    n_in, n_out, n_ex, n_scr = len(in_specs), len(out_specs), len(arrs), len(scratch_shapes)
    n_sem = n_ex if kind == "sibling" else 3 * n_ex
    ex_shapes = [jax.ShapeDtypeStruct(a.shape, a.dtype) for a in arrs] if kind == "sibling" else _chip_shapes(arrs, kind == "scatter")

    def fused(*refs):
        ins, ex_in = refs[:n_in], refs[n_in:n_in + n_ex]
        outs, ex_out = refs[n_in + n_ex:n_in + n_ex + n_out], refs[n_in + n_ex + n_out:n_in + 2 * n_ex + n_out]
        rest = refs[n_in + 2 * n_ex + n_out:]
        if kind == "sibling":
            sends = recvs = _sibling_copies(ex_in, ex_out, rest[n_scr], rest[n_scr + 1])
        else:
            sends, recvs = _chip_copies(ex_in, ex_out, rest[n_scr], rest[n_scr + 1], kind == "scatter")
        first, last = pl.program_id(0) == 0, pl.program_id(0) == grid[0] - 1
        for k in range(1, len(grid)):
            first, last = first & (pl.program_id(k) == 0), last & (pl.program_id(k) == grid[k] - 1)

        @pl.when(first)
        def _():
            for cp in sends:
                cp.start()

        body(*ins, *outs, *rest[:n_scr])

        @pl.when(last)
        def _():
            for cp in recvs:
                cp.wait_recv()
            for cp in sends:
                cp.wait_send()

    any_spec = pl.BlockSpec(memory_space=pl.ANY)
    res = pl.pallas_call(
        fused, name=name, grid=grid, in_specs=list(in_specs) + [any_spec] * n_ex, out_specs=list(out_specs) + [any_spec] * n_ex,
        out_shape=list(out_shape) + ex_shapes,
        scratch_shapes=list(scratch_shapes) + [pltpu.SemaphoreType.DMA((n_sem,))] * 2,
        compiler_params=_cp(*["arbitrary"] * len(grid)))(*args, *arrs)
    return res[:n_out], res[n_out:]


def _sibling_copies(ins, outs, send_sems, recv_sems):
    x, y, c = _place()
    return [pltpu.make_async_remote_copy(ins[a], outs[a], send_sems.at[a], recv_sems.at[a],
                                         device_id=(x, y, 1 - c), device_id_type=MESH) for a in range(len(ins))]


def _sibling_exchange(name, arrs):
    n = len(arrs)

    def body(*refs):
        copies = _sibling_copies(refs[:n], refs[n:2 * n], refs[2 * n], refs[2 * n + 1])
        for cp in copies:
            cp.start()
        for cp in copies:
            cp.wait_recv()
        for cp in copies:
            cp.wait_send()

    return _hbm_call(name, body, arrs, [jax.ShapeDtypeStruct(a.shape, a.dtype) for a in arrs], n)


def _row_tile(rows):
    for t in (256, 128, 64, 32, 16, 8):
        if rows % t == 0:
            return t
    return rows


def _pair_sum(name, a, b, out_dtype):
    rows, cols = a.shape
    tr = _row_tile(rows)

    def body(a_ref, b_ref, o_ref):
        o_ref[...] = (a_ref[...] + b_ref[...]).astype(out_dtype)

    return pl.pallas_call(body, name=name, grid=(rows // tr,), in_specs=[_row(tr, cols)] * 2, out_specs=_row(tr, cols),
                          out_shape=_sds(rows, cols, dtype=out_dtype), compiler_params=_cp("arbitrary"))(a, b)


def _chip_sum(name, recv, partial, mine):
    n, rows, cols = recv.shape
    tr = _row_tile(rows)

    def body(mine_ref, *refs):
        own = refs[n][0].astype(F32)
        acc = None
        for s in range(n):
            term = jnp.where(mine_ref[0] == s, own, refs[s][0].astype(F32))
            acc = term if acc is None else acc + term
        refs[-1][...] = acc

    def slot_spec(s):
        return pl.BlockSpec((1, tr, cols), lambda i, m: (jnp.where(m[0] == s, (s + 1) % n, s), i, 0))

    grid_spec = pltpu.PrefetchScalarGridSpec(
        num_scalar_prefetch=1, grid=(rows // tr,),
        in_specs=[slot_spec(s) for s in range(n)] + [pl.BlockSpec((1, tr, cols), lambda i, m: (m[0], i, 0))],
        out_specs=pl.BlockSpec((tr, cols), lambda i, m: (i, 0)))
    return pl.pallas_call(body, name=name, grid_spec=grid_spec, out_shape=_sds(rows, cols),
                          compiler_params=_cp("arbitrary"))(mine, *([recv] * n), partial)


def _slot_sum(name, arr):
    n, rows, cols = arr.shape
    tr = _row_tile(rows)

    def body(*refs):
        acc = refs[0][0]
        for r in refs[1:-1]:
            acc = acc + r[0]
        refs[-1][...] = acc

    specs = [pl.BlockSpec((1, tr, cols), functools.partial(lambda s, i: (s, i, 0), s)) for s in range(n)]
    return pl.pallas_call(body, name=name, grid=(rows // tr,), in_specs=specs, out_specs=_row(tr, cols),
                          out_shape=_sds(rows, cols), compiler_params=_cp("arbitrary"))(*([arr] * n))


def _adamw(name, w, g, m, v):
    rows, cols = w.shape
    tr = _row_tile(rows)

    def body(w_ref, g_ref, m_ref, v_ref, d_ref, nm_ref, nv_ref):
        g_ = g_ref[...]
        m_ = ADAM_B1 * m_ref[...] + (1.0 - ADAM_B1) * g_
        v_ = ADAM_B2 * v_ref[...] + (1.0 - ADAM_B2) * (g_ * g_)
        m_hat = m_ / (1.0 - ADAM_B1 ** ADAM_STEP)
        v_hat = v_ / (1.0 - ADAM_B2 ** ADAM_STEP)
        d_ref[...] = -ADAM_LR * (m_hat / (jnp.sqrt(v_hat) + ADAM_EPS) + ADAM_WD * w_ref[...])
        nm_ref[...] = m_
        nv_ref[...] = v_

    spec = _row(tr, cols)
    return pl.pallas_call(body, name=name, grid=(rows // tr,), in_specs=[spec] * 4, out_specs=[spec] * 3,
                          out_shape=[_sds(rows, cols)] * 3, compiler_params=_cp("arbitrary"))(w, g, m, v)


def _adamw_many(name, ws, gs, ms, vs):
    n = len(ws)

    def body(*refs):
        for i in range(n):
            w_ref, g_ref, m_ref, v_ref = (refs[k * n + i] for k in range(4))
            d_ref, nm_ref, nv_ref = (refs[(4 + k) * n + i] for k in range(3))
            g_ = g_ref[...]
            m_ = ADAM_B1 * m_ref[...] + (1.0 - ADAM_B1) * g_
            v_ = ADAM_B2 * v_ref[...] + (1.0 - ADAM_B2) * (g_ * g_)
            m_hat = m_ / (1.0 - ADAM_B1 ** ADAM_STEP)
            v_hat = v_ / (1.0 - ADAM_B2 ** ADAM_STEP)
            d_ref[...] = -ADAM_LR * (m_hat / (jnp.sqrt(v_hat) + ADAM_EPS) + ADAM_WD * w_ref[...])
            nm_ref[...] = m_
            nv_ref[...] = v_

    shapes = [_sds(*a.shape) for a in ws]
    res = pl.pallas_call(body, name=name, out_shape=shapes * 3,
                         compiler_params=pltpu.CompilerParams(vmem_limit_bytes=VMEM_LIMIT_BYTES))(*ws, *gs, *ms, *vs)
    return [(res[i], res[n + i], res[2 * n + i]) for i in range(n)]


def _adamw_halves(name, w, g_mine, g_sibling, m, v, core):
    _, rows, cols = w.shape
    half = rows // 2
    tr = _row_tile(half)
    per_half = half // tr

    def body(core_ref, w_ref, gm_ref, gs_ref, m_ref, v_ref, g_ref, d_ref, nm_ref, nv_ref):
        g_ = jnp.where(pl.program_id(0) // per_half == core_ref[0], gm_ref[...], gs_ref[...])
        m_ = ADAM_B1 * m_ref[...] + (1.0 - ADAM_B1) * g_
        v_ = ADAM_B2 * v_ref[...] + (1.0 - ADAM_B2) * (g_ * g_)
        m_hat = m_ / (1.0 - ADAM_B1 ** ADAM_STEP)
        v_hat = v_ / (1.0 - ADAM_B2 ** ADAM_STEP)
        g_ref[...] = g_
        d_ref[...] = -ADAM_LR * (m_hat / (jnp.sqrt(v_hat) + ADAM_EPS) + ADAM_WD * w_ref[...])
        nm_ref[...] = m_
        nv_ref[...] = v_

    full = pl.BlockSpec((None, tr, cols), lambda i, c: (0, i, 0))
    in_half = pl.BlockSpec((tr, cols), lambda i, c: (i % per_half, 0))
    grid_spec = pltpu.PrefetchScalarGridSpec(num_scalar_prefetch=1, grid=(rows // tr,),
                                             in_specs=[full, in_half, in_half, full, full], out_specs=[full] * 4)
    return pl.pallas_call(body, name=name, grid_spec=grid_spec, out_shape=[_sds(1, rows, cols)] * 4,
                          compiler_params=_cp("arbitrary"))(core, w, g_mine, g_sibling, m, v)


def _mod_local(c_all, ada_w):
    def body(c_ref, w_ref, o_ref):
        c_act = jax.nn.silu(c_ref[...])
        for l in range(2):
            o_ref[l] = _hdot(c_act, w_ref[l])

    return pl.pallas_call(body, name="mod_local", out_shape=_sds(2, 8, ada_w.shape[2]),
                          compiler_params=pltpu.CompilerParams(vmem_limit_bytes=VMEM_LIMIT_BYTES))(c_all, ada_w)


def _ada_w_grad(c_all, dmod_cols):
    def body(c_ref, d_ref, o_ref):
        c_act = jax.nn.silu(c_ref[...])
        for l in range(2):
            o_ref[l] = lax.dot_general(c_act, d_ref[l], (((0,), (0,)), ((), ())), precision=HI, preferred_element_type=F32)

    return pl.pallas_call(body, name="ada_w_grad", out_shape=_sds(2, D_MODEL, dmod_cols.shape[2]),
                          compiler_params=pltpu.CompilerParams(vmem_limit_bytes=VMEM_LIMIT_BYTES))(c_all, dmod_cols)


_SMALL = ("ada_b", "pre_g", "post_g", "rel_bias", "s5_a_re", "s5_a_im", "s5_log_dt", "s5_b_re", "s5_b_im", "s5_c_re", "s5_c_im",
          "s5_d", "s5_glu_b", "gdn_a_log", "gdn_dt_bias", "gdn_norm_g")
_SHARDED = ("ab_w_in", "ab_w_out", "s5_glu_w", "gdn_w_in", "gdn_w_out")
_COL_SHARDED = ("ab_w_in", "gdn_w_in")
_WEIGHTS = ("ada_w", "ada_b", "pre_g", "post_g", "rel_bias", "ab_w_in", "ab_w_out", "s5_a_re", "s5_a_im", "s5_log_dt", "s5_b_re",
            "s5_b_im", "s5_c_re", "s5_c_im", "s5_d", "s5_glu_w", "s5_glu_b", "gdn_w_in", "gdn_conv", "gdn_a_log", "gdn_dt_bias",
            "gdn_norm_g", "gdn_w_out")


def _rows128(n):
    return -(-n // 128)


def _pack(arrs, total_rows):
    pieces = []
    for a in arrs:
        flat = a.reshape(-1)
        pieces.append(jnp.pad(flat, (0, _rows128(flat.shape[0]) * 128 - flat.shape[0])).reshape(-1, 128))
    used = sum(p.shape[0] for p in pieces)
    pieces.append(jnp.zeros((total_rows - used, 128), F32))
    return jnp.concatenate(pieces, axis=0)


def _unpack(buf, shapes):
    out, at = [], 0
    for shp in shapes:
        n = int(np.prod(shp))
        out.append(buf[at:at + _rows128(n)].reshape(-1)[:n].reshape(shp))
        at += _rows128(n)
    return out


def _full_from_halves(g):
    return g.transpose(1, 0, 2, 3).reshape(8 * g.shape[2], g.shape[3])


def _join_col_shards(name, mine, theirs, core, width):
    n, h, cols = mine.shape
    tr = _row_tile(h)
    per_half = h // tr

    def body(core_ref, a_ref, b_ref, o_ref):
        def put(src_ref):
            pad = [jnp.zeros((tr, width - n * cols), o_ref.dtype)] if width > n * cols else []
            o_ref[...] = jnp.concatenate([src_ref[s] for s in range(n)] + pad, axis=1)

        is_mine = pl.program_id(0) // per_half == core_ref[0]
        pl.when(is_mine)(functools.partial(put, a_ref))
        pl.when(jnp.logical_not(is_mine))(functools.partial(put, b_ref))

    half = pl.BlockSpec((n, tr, cols), lambda i, c: (0, i % per_half, 0))
    grid_spec = pltpu.PrefetchScalarGridSpec(num_scalar_prefetch=1, grid=(2 * per_half,), in_specs=[half, half],
                                             out_specs=pl.BlockSpec((tr, width), lambda i, c: (i, 0)))
    return pl.pallas_call(body, name=name, grid_spec=grid_spec, out_shape=_sds(2 * h, width, dtype=mine.dtype),
                          compiler_params=_cp("arbitrary"))(core, mine, theirs)


def _split_col_shards(name, g, cols, half):
    h = g.shape[0] // 2
    tr = _row_tile(h)
    per_half = h // tr

    def body(half_ref, g_ref, o_ref):
        for s in range(4):
            o_ref[s] = g_ref[:, s * cols:(s + 1) * cols]

    grid_spec = pltpu.PrefetchScalarGridSpec(
        num_scalar_prefetch=1, grid=(per_half,),
        in_specs=[pl.BlockSpec((tr, g.shape[1]), lambda i, c: (c[0] * per_half + i, 0))],
        out_specs=pl.BlockSpec((4, tr, cols), lambda i, c: (0, i, 0)))
    return pl.pallas_call(body, name=name, grid_spec=grid_spec, out_shape=_sds(4, h, cols),
                          compiler_params=_cp("arbitrary"))(half, g)


_LATE = ("ab_w_out", "s5_glu_w", "gdn_w_in", "gdn_w_out")


class _WeightExchanges:
    def __init__(self, shards, core, chip):
        self.core, self.chip = core, chip
        self.core_1 = jnp.reshape(core, (1,)).astype(jnp.int32)
        self.half = {}
        for name, shard in shards.items():
            h = shard.shape[0] // 2
            self.half[name] = lax.dynamic_slice_in_dim(shard.astype(BF), core * h, h, axis=0)
        self.mine, self.partial, self.received = {}, {}, {}

    def my_halves(self, names, from_chips):
        return [_own_slot(g, self.half[n], self.chip) for n, g in zip(names, from_chips)]

    def full_weights(self, names, mine, theirs):
        full = {}
        for n, a, b in zip(names, mine, theirs):
            if n in _COL_SHARDED:
                full[n] = _join_col_shards("join_" + n, a, b, self.core_1, C_IN_PAD if n == "gdn_w_in" else 4 * a.shape[2])
            else:
                full[n] = _full_from_halves(jnp.where(self.core == 0, jnp.stack([a, b], 0), jnp.stack([b, a], 0)))
        return full

    def first_weights(self):
        mine = self.my_halves(["ab_w_in"], _chip_exchange("gather_w_chips", [self.half["ab_w_in"]], False))
        return self.full_weights(["ab_w_in"], mine, _sibling_exchange("gather_w_sibling_first", mine))

    def late_exchanges(self):
        rows = self.half["gdn_w_in"].shape[0] // 2
        pieces = [self.half["gdn_w_in"][:rows], self.half["gdn_w_in"][rows:]]
        return [([self.half["ab_w_out"], self.half["s5_glu_w"]], "gather"), ([self.half["gdn_w_out"]], "gather"),
                ([pieces[0]], "gather"), ([pieces[1]], "gather")]

    def late_halves(self, got):
        return (self.my_halves(["ab_w_out", "s5_glu_w"], got[0]),
                self.my_halves(["gdn_w_in", "gdn_w_out"], [jnp.concatenate([got[2][0], got[3][0]], axis=1), got[1][0]]))

    def split_halves(self, grads):
        other = []
        for name, g in grads.items():
            if name in _COL_SHARDED:
                cols = self.half[name].shape[1]
                self.mine[name] = _split_col_shards("split_mine_" + name, g, cols, self.core_1)
                other.append(_split_col_shards("split_other_" + name, g, cols, 1 - self.core_1))
                continue
            sm = g.reshape(4, g.shape[0] // 4, g.shape[1])
            h = sm.shape[1] // 2
            self.mine[name] = lax.dynamic_slice_in_dim(sm, self.core * h, h, axis=1)
            other.append(lax.dynamic_slice_in_dim(sm, (1 - self.core) * h, h, axis=1))
        return other

    def chip_partials(self, names, from_sibling):
        for name, b in zip(names, from_sibling):
            a = self.mine[name]
            flat = lambda t: t.reshape(-1, t.shape[-1])
            self.partial[name] = _pair_sum("sum_sibling_" + name, flat(a), flat(b), BF).reshape(a.shape)
        return [self.partial[n] for n in names]


def kernel(x, c, ada_w, ada_b, pre_g, post_g, rel_bias, ab_w_in, ab_w_out, s5_a_re, s5_a_im, s5_log_dt, s5_b_re, s5_b_im, s5_c_re, s5_c_im, s5_d, s5_glu_w, s5_glu_b, gdn_w_in, gdn_conv, gdn_a_log, gdn_dt_bias, gdn_norm_g, gdn_w_out, loss_target, m_ada_w, m_ada_b, m_pre_g, m_post_g, m_rel_bias, m_ab_w_in, m_ab_w_out, m_s5_a_re, m_s5_a_im, m_s5_log_dt, m_s5_b_re, m_s5_b_im, m_s5_c_re, m_s5_c_im, m_s5_d, m_s5_glu_w, m_s5_glu_b, m_gdn_w_in, m_gdn_conv, m_gdn_a_log, m_gdn_dt_bias, m_gdn_norm_g, m_gdn_w_out, v_ada_w, v_ada_b, v_pre_g, v_post_g, v_rel_bias, v_ab_w_in, v_ab_w_out, v_s5_a_re, v_s5_a_im, v_s5_log_dt, v_s5_b_re, v_s5_b_im, v_s5_c_re, v_s5_c_im, v_s5_d, v_s5_glu_w, v_s5_glu_b, v_gdn_w_in, v_gdn_conv, v_gdn_a_log, v_gdn_dt_bias, v_gdn_norm_g, v_gdn_w_out):
    w = dict(ada_w=ada_w, ada_b=ada_b, pre_g=pre_g, post_g=post_g, rel_bias=rel_bias, ab_w_in=ab_w_in, ab_w_out=ab_w_out,
             s5_a_re=s5_a_re, s5_a_im=s5_a_im, s5_log_dt=s5_log_dt, s5_b_re=s5_b_re, s5_b_im=s5_b_im, s5_c_re=s5_c_re, s5_c_im=s5_c_im,
             s5_d=s5_d, s5_glu_w=s5_glu_w, s5_glu_b=s5_glu_b, gdn_w_in=gdn_w_in, gdn_conv=gdn_conv, gdn_a_log=gdn_a_log,
             gdn_dt_bias=gdn_dt_bias, gdn_norm_g=gdn_norm_g, gdn_w_out=gdn_w_out)
    m = dict(ada_w=m_ada_w, ada_b=m_ada_b, pre_g=m_pre_g, post_g=m_post_g, rel_bias=m_rel_bias, ab_w_in=m_ab_w_in, ab_w_out=m_ab_w_out,
             s5_a_re=m_s5_a_re, s5_a_im=m_s5_a_im, s5_log_dt=m_s5_log_dt, s5_b_re=m_s5_b_re, s5_b_im=m_s5_b_im, s5_c_re=m_s5_c_re,
             s5_c_im=m_s5_c_im, s5_d=m_s5_d, s5_glu_w=m_s5_glu_w, s5_glu_b=m_s5_glu_b, gdn_w_in=m_gdn_w_in, gdn_conv=m_gdn_conv,
             gdn_a_log=m_gdn_a_log, gdn_dt_bias=m_gdn_dt_bias, gdn_norm_g=m_gdn_norm_g, gdn_w_out=m_gdn_w_out)
    v = dict(ada_w=v_ada_w, ada_b=v_ada_b, pre_g=v_pre_g, post_g=v_post_g, rel_bias=v_rel_bias, ab_w_in=v_ab_w_in, ab_w_out=v_ab_w_out,
             s5_a_re=v_s5_a_re, s5_a_im=v_s5_a_im, s5_log_dt=v_s5_log_dt, s5_b_re=v_s5_b_re, s5_b_im=v_s5_b_im, s5_c_re=v_s5_c_re,
             s5_c_im=v_s5_c_im, s5_d=v_s5_d, s5_glu_w=v_s5_glu_w, s5_glu_b=v_s5_glu_b, gdn_w_in=v_gdn_w_in, gdn_conv=v_gdn_conv,
             gdn_a_log=v_gdn_a_log, gdn_dt_bias=v_gdn_dt_bias, gdn_norm_g=v_gdn_norm_g, gdn_w_out=v_gdn_w_out)
    ix, iy, ic = _place()
    me = 4 * ix + 2 * iy + ic
    chip = 2 * ix + iy
    n_cols = ada_w.shape[2]

    mine_first = _pack([c, gdn_conv], 32)
    first = _own_slot(_all_gather8("gather_c_conv", mine_first), mine_first, me)
    c_all = first[:, 0:8].reshape(8, D_MODEL)
    conv_full = first[0::2, 8:32].reshape(4, C_CONV, n_cols).transpose(1, 0, 2).reshape(C_CONV, 4 * n_cols)
    mine_mod = _mod_local(c_all, ada_w)
    modl = _own_slot(_all_gather8("gather_mod", mine_mod), mine_mod, me)
    mod = lax.dynamic_index_in_dim(modl[0::2], me, axis=2, keepdims=False)
    mod = mod.transpose(1, 0, 2).reshape(2, 4 * n_cols) + ada_b

    comm = _WeightExchanges({name: w[name][0] for name in _SHARDED}, ic, chip)
    wd = {name: w[name] for name in _SMALL if name != "ada_b"}
    wd = {k: (a if k in ("pre_g", "post_g", "rel_bias") else a[0]) for k, a in wd.items()}
    wd["gdn_conv"] = conv_full
    wd.update(comm.first_weights())

    loss_local, grad_x, grads, dmod = _local_step(x[0], loss_target[0], mod, wd, comm)
    loss = lax.psum(loss_local, ("x", "y", "c"))

    small_shapes = [w[name].shape for name in _SMALL] + [(C_CONV, 4 * n_cols)]
    small_rows = -(-sum(_rows128(int(np.prod(s))) for s in small_shapes) // 64) * 64
    per_dev, dmod_rows = small_rows // 8, _rows128(2 * 3 * D_MODEL)
    partial = _pack([dmod] + [grads[name] for name in _SMALL[1:]] + [grads["gdn_conv"]], small_rows)
    outbound = jnp.concatenate([partial.reshape(8, per_dev, 128), jnp.broadcast_to(partial[None, :dmod_rows], (8, dmod_rows, 128))], axis=1)
    inbound = _own_slot(_all_to_all8("reduce_small_grads", outbound), lax.dynamic_index_in_dim(outbound, me, 0, keepdims=False), me)
    my_rows = _slot_sum("sum_small_grads", inbound[:, :per_dev])
    g_small = _own_slot(_all_gather8("gather_small_grads", my_rows), my_rows, me).reshape(small_rows, 128)
    g_list = _unpack(g_small, small_shapes)
    out_g, out_d, out_m, out_v = {}, {}, {}, {}

    def update(name, g2d):
        shp = w[name].shape
        two_d = lambda a: a.reshape(-1, shp[-1])
        d_, m_, v_ = _adamw("adamw_" + name, two_d(w[name]), g2d, two_d(m[name]), two_d(v[name]))
        out_g[name], out_d[name], out_m[name], out_v[name] = (a.reshape(shp) for a in (g2d, d_, m_, v_))

    small = list(_SMALL) + ["gdn_conv"]
    small_g = [g.reshape(-1, g.shape[-1]) for g in g_list[:-1]] + [lax.dynamic_slice_in_dim(g_list[-1], chip * n_cols, n_cols, axis=1)]
    two_d = lambda a: a.reshape(-1, a.shape[-1])
    results = _adamw_many("adamw_small", [two_d(w[n]) for n in small], small_g, [two_d(m[n]) for n in small], [two_d(v[n]) for n in small])
    for name, g2d, (d_, m_, v_) in zip(small, small_g, results):
        out_g[name], out_d[name], out_m[name], out_v[name] = (a.reshape(w[name].shape) for a in (g2d, d_, m_, v_))

    dmod_all = inbound[:, per_dev:].reshape(8, 2, 4, n_cols)
    dmod_cols = lax.dynamic_index_in_dim(dmod_all, chip, axis=2, keepdims=False).transpose(1, 0, 2)
    update("ada_w", _ada_w_grad(c_all, dmod_cols).reshape(-1, n_cols))

    from_sibling = _sibling_exchange("reduce_sibling", comm.split_halves({"ab_w_in": grads["ab_w_in"]}))
    comm.received["ab_w_in"] = _chip_exchange("reduce_chips", comm.chip_partials(["ab_w_in"], from_sibling), True)[0]
    chip_1 = jnp.reshape(chip, (1,)).astype(jnp.int32)
    core_1 = jnp.reshape(ic, (1,)).astype(jnp.int32)
    reduced = [_chip_sum("sum_chips_" + name, comm.received[name], comm.partial[name], chip_1) for name in _SHARDED]
    for name, g_mine, g_sib in zip(_SHARDED, reduced, _sibling_exchange("reduce_share", reduced)):
        out_g[name], out_d[name], out_m[name], out_v[name] = _adamw_halves(
            "adamw_" + name, w[name], g_mine, g_sib, m[name], v[name], core_1)

    return (loss, grad_x[None], *[out_g[n] for n in _WEIGHTS], *[out_d[n] for n in _WEIGHTS],
            *[out_m[n] for n in _WEIGHTS], *[out_v[n] for n in _WEIGHTS])
```

```python
import functools
import math

import numpy as np
import jax
import jax.numpy as jnp
from jax import lax
from jax.experimental import pallas as pl
from jax.experimental.pallas import tpu as pltpu

F32 = jnp.float32
BF = jnp.bfloat16
HI = lax.Precision.HIGHEST
MESH = pl.DeviceIdType.MESH

D_MODEL = 1024
EPS = 1e-6
A_HEADS, A_HD, A_WIDTH, A_BLOCK = 8, 64, 512, 128
DILATIONS = (1, 4, 16)
N_KEYS = 128
REL_BUCKETS, REL_MAX_DIST = 32, 2048
B_WIDTH, B_GROUP, B_GROUPS, B_STATE = 512, 16, 32, 64
S5_LANES = 512
S5_TILES = 4
C_HEADS, C_DK, C_CHUNK, C_CONV = 8, 128, 64, 4
QKV = 3072
C_IN = QKV + 1024 + 2 * C_HEADS
C_IN_PAD = 4224
TM = 256
VMEM_LIMIT_BYTES = 56 * 1024 * 1024
ADAM_LR, ADAM_B1, ADAM_B2, ADAM_EPS, ADAM_WD, ADAM_STEP = 0.001, 0.9, 0.999, 1e-08, 0.01, 10
NEG = float(np.finfo(np.float32).min)


def _cp(*sem):
    return pltpu.CompilerParams(dimension_semantics=sem, vmem_limit_bytes=VMEM_LIMIT_BYTES)


def _bdot(a, b):
    return jnp.dot(a.astype(BF), b.astype(BF), preferred_element_type=F32)


def _bdot_nt(a, b):
    return lax.dot_general(a.astype(BF), b.astype(BF), (((1,), (1,)), ((), ())), preferred_element_type=F32)


def _bdot_tn(a, b):
    return lax.dot_general(a.astype(BF), b.astype(BF), (((0,), (0,)), ((), ())), preferred_element_type=F32)


def _hdot(a, b):
    return jnp.dot(a, b, precision=HI, preferred_element_type=F32)


def _bein(eq, a, b):
    return jnp.einsum(eq, a.astype(BF), b.astype(BF), preferred_element_type=F32)


def _row(tm, n):
    return pl.BlockSpec((tm, n), lambda i: (i, 0))


def _fix(shape):
    return pl.BlockSpec(shape, lambda i: (0,) * len(shape))


def _sds(*shape, dtype=F32):
    return jax.ShapeDtypeStruct(shape, dtype)


def _acc(ref, val):
    ref[...] += val


def _zero_at_first(refs, axis=0):
    @pl.when(pl.program_id(axis) == 0)
    def _():
        for r in refs:
            r[...] = jnp.zeros_like(r)


def _rms(x):
    return x * lax.rsqrt(jnp.mean(x * x, axis=-1, keepdims=True) + EPS)


def _pre_mod(x, g, scale, shift):
    return (_rms(x) * g) * (1.0 + scale) + shift


def _post_res(y, x, post_g, gate):
    return x + gate * (_rms(y) * post_g)


def _merge_gate(o1, o2, o3, l1, l2, l3, ga):
    m = jnp.maximum(jnp.maximum(l1, l2), l3)
    e1, e2, e3 = jnp.exp(l1 - m), jnp.exp(l2 - m), jnp.exp(l3 - m)
    inv = 1.0 / (e1 + e2 + e3)
    return ((e1 * inv) * o1 + (e2 * inv) * o2 + (e3 * inv) * o3) * jax.nn.silu(ga)


def _s5_gelu(ypre, u, d_skip):
    return jax.nn.gelu(ypre + d_skip * u)


def _s5_glu(yb, gl, gb):
    return yb * jax.nn.sigmoid(gl) * jax.nn.silu(gb)


def _l0_front(x, pre_g, scale, shift, w_in):
    s_len = x.shape[0]

    def body(x_ref, g_ref, sc_ref, sh_ref, w_ref, *out_refs):
        qkv_refs, (u_ref, ga_ref, gb_ref, h_ref) = out_refs[:9], out_refs[9:]
        hb = _pre_mod(x_ref[...], g_ref[...], sc_ref[...], sh_ref[...]).astype(BF)
        h_ref[...] = hb
        z = jnp.dot(hb, w_ref[...], preferred_element_type=F32)
        for a in range(3):
            piece = z[:, a * 512:(a + 1) * 512]
            for i, d in enumerate(DILATIONS):
                qkv_refs[3 * a + i][...] = _to_res(piece, d).astype(BF)
        u_ref[...] = z[:, 1536:2048]
        ga_ref[...] = z[:, 2048:2560]
        gb_ref[...] = z[:, 2560:3072]

    vec = _fix((1, D_MODEL))
    return pl.pallas_call(
        body, name="l0_front", grid=(s_len // TM,),
        in_specs=[_row(TM, D_MODEL), vec, vec, vec, _fix((D_MODEL, 3072))],
        out_specs=[_res_spec(d) for d in DILATIONS] * 3 + [_row(TM, 512)] * 3 + [_row(TM, D_MODEL)],
        out_shape=[_sds(*_res_shape(s_len, d), dtype=BF) for d in DILATIONS] * 3 + [_sds(s_len, 512)] * 3 + [_sds(s_len, D_MODEL, dtype=BF)],
        compiler_params=_cp("arbitrary"),
    )(x, pre_g, scale, shift, w_in)


def _front_bwd(name, x, pre_g, scale, shift, w_in, dres, parts, widths):
    s_len = x.shape[0]
    n_in = sum(len(p) for p in parts)
    n_cols = sum(widths)

    def body(*refs):
        x_ref, g_ref, sc_ref, sh_ref, w_ref, dres_ref = refs[:6]
        part_refs = refs[6:6 + n_in]
        dz_ref, dx_ref, dg_ref, dsc_ref, dsh_ref = refs[6 + n_in:]
        _zero_at_first([dg_ref, dsc_ref, dsh_ref])
        _, vjp = jax.vjp(_pre_mod, x_ref[...], g_ref[...], sc_ref[...], sh_ref[...])
        dh = jnp.zeros((TM, D_MODEL), F32)
        col, at = 0, 0
        for grp, width in zip(parts, widths):
            tile = lambda r: _from_res(r[...]) if len(r.shape) == 3 else r[...]
            dz = tile(part_refs[at])
            for r in part_refs[at + 1:at + len(grp)]:
                dz = dz + tile(r)
            at += len(grp)
            dzb = dz.astype(BF)
            dz_ref[:, col:col + width] = dzb
            dh = dh + lax.dot_general(dzb, w_ref[:, col:col + width], (((1,), (1,)), ((), ())), preferred_element_type=F32)
            col += width
        dx, dg, dsc, dsh = vjp(dh)
        dx_ref[...] = dx + dres_ref[...]
        _acc(dg_ref, dg)
        _acc(dsc_ref, dsc)
        _acc(dsh_ref, dsh)

    vec = _fix((1, D_MODEL))
    flat = [a for p in parts for a in p]
    return pl.pallas_call(
        body, name=name, grid=(s_len // TM,),
        in_specs=[_row(TM, D_MODEL), vec, vec, vec, _fix((D_MODEL, n_cols)), _row(TM, D_MODEL)]
        + [_res_spec(a.shape[0], a.shape[2]) if a.ndim == 3 else _row(TM, a.shape[1]) for a in flat],
        out_specs=[_row(TM, n_cols), _row(TM, D_MODEL), vec, vec, vec],
        out_shape=[_sds(s_len, n_cols, dtype=BF), _sds(s_len, D_MODEL), _sds(1, D_MODEL), _sds(1, D_MODEL), _sds(1, D_MODEL)],
        compiler_params=_cp("arbitrary"),
    )(x, pre_g, scale, shift, w_in, dres, *flat)


def _matmul_tn(name, a, b, tn):
    s_len, k_dim = a.shape
    n_dim = b.shape[1]
    ts = 512

    def body(a_ref, b_ref, o_ref):
        _zero_at_first([o_ref], axis=1)
        o_ref[...] += lax.dot_general(a_ref[...], b_ref[...], (((0,), (0,)), ((), ())), preferred_element_type=F32)

    return pl.pallas_call(
        body, name=name, grid=(n_dim // tn, s_len // ts),
        in_specs=[pl.BlockSpec((ts, k_dim), lambda j, i: (i, 0)), pl.BlockSpec((ts, tn), lambda j, i: (i, j))],
        out_specs=pl.BlockSpec((k_dim, tn), lambda j, i: (0, j)),
        out_shape=_sds(k_dim, n_dim),
        compiler_params=_cp("arbitrary", "arbitrary"),
    )(a, b)


def _t5_bucket_np(dist):
    dist = np.maximum(dist, 0)
    max_exact = REL_BUCKETS // 2
    large = max_exact + (np.log(np.maximum(dist, 1) / max_exact)
                         / math.log(REL_MAX_DIST / max_exact) * (REL_BUCKETS - max_exact)).astype(np.int32)
    large = np.minimum(large, REL_BUCKETS - 1)
    return np.where(dist < max_exact, dist, large).astype(np.int32)


def _to_res(z, dil):
    if dil == 1:
        return z[None]
    return jnp.swapaxes(z.reshape(z.shape[0] // dil, dil, z.shape[1]), 0, 1)


def _from_res(z):
    if z.shape[0] == 1:
        return z[0]
    return jnp.swapaxes(z, 0, 1).reshape(z.shape[0] * z.shape[1], z.shape[2])


def _res_shape(s_len, dil, width=A_WIDTH):
    return (dil, s_len // dil, width)


def _res_spec(dil, width=A_WIDTH):
    return pl.BlockSpec((dil, TM // dil, width), lambda i: (0, i, 0))


def _bucket_table():
    qi = np.arange(A_BLOCK)[:, None]
    kj = np.arange(2 * A_BLOCK)[None, :]
    return np.stack([_t5_bucket_np((qi + A_BLOCK - kj) * d) for d in DILATIONS], 0)


def _attn_mask(first):
    qi = lax.broadcasted_iota(jnp.int32, (A_BLOCK, 2 * A_BLOCK), 0)
    kj = lax.broadcasted_iota(jnp.int32, (A_BLOCK, 2 * A_BLOCK), 1)
    rel = qi + A_BLOCK - kj
    return (rel >= 0) & (rel <= N_KEYS) & (jnp.logical_not(first) | (kj >= A_BLOCK))


def _attn_specs(nb, rev):
    per = 2 if nb % 2 == 0 else 1
    steps = nb // per
    n_of = (lambda i: steps - 1 - i) if rev else (lambda i: i)
    cur = pl.BlockSpec((None, per * A_BLOCK, A_WIDTH), lambda r, i: (r, n_of(i), 0))
    prev = pl.BlockSpec((None, A_BLOCK, A_WIDTH), lambda r, i: (r, jnp.maximum(per * n_of(i) - 1, 0), 0))
    bias = pl.BlockSpec((A_HEADS, A_BLOCK, 2 * A_BLOCK), lambda r, i: (0, 0, 0))
    return per, steps, cur, prev, bias


def _attn_fwd(q, k, v, bias, exchange=None):
    dil, t_len, _ = q.shape
    per, steps, cur, prev, bias_spec = _attn_specs(t_len // A_BLOCK, False)
    scale = A_HD ** -0.5

    def body(q_ref, kp_ref, kc_ref, vp_ref, vc_ref, b_ref, o_ref, l_ref):
        lane = lax.broadcasted_iota(jnp.int32, (1, 128), 1)
        for sub in range(per):
            rows = slice(sub * A_BLOCK, (sub + 1) * A_BLOCK)
            before = slice((sub - 1) * A_BLOCK, sub * A_BLOCK)
            mask = _attn_mask((pl.program_id(1) == 0) if sub == 0 else False)
            for hp in range(A_HEADS // 2):
                sl = slice(hp * 128, (hp + 1) * 128)
                qp = q_ref[rows, sl]
                kw = jnp.concatenate([kp_ref[:, sl] if sub == 0 else kc_ref[before, sl], kc_ref[rows, sl]], axis=0).astype(BF)
                vw = jnp.concatenate([vp_ref[:, sl] if sub == 0 else vc_ref[before, sl], vc_ref[rows, sl]], axis=0).astype(BF)
                outs, lses = [], []
                for j in range(2):
                    hm = (lane < 64) if j == 0 else (lane >= 64)
                    s = _bdot_nt(jnp.where(hm, qp, 0.0), kw) * scale
                    s = jnp.where(mask, s + b_ref[2 * hp + j], NEG)
                    m = jnp.max(s, axis=-1, keepdims=True)
                    p = jnp.exp(s - m)
                    den = jnp.sum(p, axis=-1, keepdims=True)
                    outs.append(_bdot(p, vw) / den)
                    lses.append(m + jnp.log(den))
                hm0 = lane < 64
                o_ref[rows, sl] = jnp.where(hm0, outs[0], outs[1])
                l_ref[rows, sl] = jnp.where(hm0, lses[0], lses[1])

    return _call_with_exchange(body, f"attn_fwd_d{dil}", (dil, steps), [cur, prev, cur, prev, cur, bias_spec], [cur, cur],
                               [_sds(dil, t_len, A_WIDTH)] * 2, [], (q, k, k, v, v, bias), exchange)


def _attn_bwd(q, k, v, bias, o, l, do, dl, exchange=None):
    dil, t_len, _ = q.shape
    per, steps, cur, prev, bias_spec = _attn_specs(t_len // A_BLOCK, True)
    scale = A_HD ** -0.5

    def body(q_ref, kp_ref, kc_ref, vp_ref, vc_ref, b_ref, o_ref, l_ref, do_ref, dl_ref,
             dq_ref, dk_ref, dv_ref, db_ref, ck_ref, cv_ref):
        _zero_at_first([ck_ref, cv_ref], axis=1)

        @pl.when((pl.program_id(0) == 0) & (pl.program_id(1) == 0))
        def _():
            db_ref[...] = jnp.zeros_like(db_ref)

        lane = lax.broadcasted_iota(jnp.int32, (1, 128), 1)
        for hp in range(A_HEADS // 2):
            sl = slice(hp * 128, (hp + 1) * 128)
            to_prev_k, to_prev_v = ck_ref[:, sl], cv_ref[:, sl]
            for sub in range(per - 1, -1, -1):
                rows = slice(sub * A_BLOCK, (sub + 1) * A_BLOCK)
                before = slice((sub - 1) * A_BLOCK, sub * A_BLOCK)
                mask = _attn_mask((pl.program_id(1) == steps - 1) if sub == 0 else False)
                qp = q_ref[rows, sl]
                kw = jnp.concatenate([kp_ref[:, sl] if sub == 0 else kc_ref[before, sl], kc_ref[rows, sl]], axis=0).astype(BF)
                vw = jnp.concatenate([vp_ref[:, sl] if sub == 0 else vc_ref[before, sl], vc_ref[rows, sl]], axis=0).astype(BF)
                op, lp, dop, dlp = o_ref[rows, sl], l_ref[rows, sl], do_ref[rows, sl], dl_ref[rows, sl]
                dq_acc = jnp.zeros((A_BLOCK, 128), F32)
                dk_acc = jnp.zeros((2 * A_BLOCK, 128), F32)
                dv_acc = jnp.zeros((2 * A_BLOCK, 128), F32)
                for j in range(2):
                    hm = (lane < 64) if j == 0 else (lane >= 64)
                    qm = jnp.where(hm, qp, 0.0)
                    s = _bdot_nt(qm, kw) * scale
                    s = jnp.where(mask, s + b_ref[2 * hp + j], NEG)
                    lse = jnp.max(jnp.where(hm, lp, NEG), axis=-1, keepdims=True)
                    p = jnp.exp(s - lse)
                    do_h = jnp.where(hm, dop, 0.0)
                    dd = jnp.sum(do_h * op, axis=-1, keepdims=True)
                    dlse = jnp.sum(jnp.where(hm, dlp, 0.0), axis=-1, keepdims=True)
                    ds = p * (_bdot_nt(do_h, vw) - dd + dlse)
                    dv_acc = dv_acc + _bdot_tn(p, do_h)
                    dq_acc = dq_acc + jnp.where(hm, _bdot(ds, kw), 0.0) * scale
                    dk_acc = dk_acc + _bdot_tn(ds, qm) * scale
                    db_ref[2 * hp + j] += ds
                dq_ref[rows, sl] = dq_acc
                dk_ref[rows, sl] = dk_acc[A_BLOCK:] + to_prev_k
                dv_ref[rows, sl] = dv_acc[A_BLOCK:] + to_prev_v
                to_prev_k, to_prev_v = dk_acc[:A_BLOCK], dv_acc[:A_BLOCK]
            ck_ref[:, sl] = to_prev_k
            cv_ref[:, sl] = to_prev_v

    return _call_with_exchange(
        body, f"attn_bwd_d{dil}", (dil, steps), [cur, prev, cur, prev, cur, bias_spec, cur, cur, cur, cur],
        [cur, cur, cur, bias_spec], [_sds(dil, t_len, A_WIDTH)] * 3 + [_sds(A_HEADS, A_BLOCK, 2 * A_BLOCK)],
        [pltpu.VMEM((A_BLOCK, A_WIDTH), F32)] * 2, (q, k, k, v, v, bias, o, l, do, dl), exchange)


def _attn_bias(rel_bias, table):
    def body(rb_ref, t_ref, *o_refs):
        for c in range(3):
            t = t_ref[c]
            acc = [jnp.zeros((A_BLOCK, 2 * A_BLOCK), F32) for _ in range(A_HEADS)]
            for b in range(REL_BUCKETS):
                hit = t == b
                acc = [jnp.where(hit, rb_ref[b, h], acc[h]) for h in range(A_HEADS)]
            for h in range(A_HEADS):
                o_refs[c][h] = acc[h]

    return pl.pallas_call(body, name="attn_bias", out_shape=[_sds(A_HEADS, A_BLOCK, 2 * A_BLOCK)] * 3,
                          in_specs=[pl.BlockSpec(memory_space=pltpu.SMEM), pl.BlockSpec(memory_space=pltpu.VMEM)],
                          compiler_params=pltpu.CompilerParams(vmem_limit_bytes=VMEM_LIMIT_BYTES))(rel_bias, table)


def _rel_bias_grad(dbs, idx_rows):
    n = A_BLOCK * 2 * A_BLOCK

    def body(d0_ref, d1_ref, d2_ref, idx_ref, o_ref):
        bucket = lax.broadcasted_iota(jnp.int32, (REL_BUCKETS, n), 0).astype(F32)
        acc = jnp.zeros((A_HEADS, REL_BUCKETS), F32)
        for c, db_ref in enumerate((d0_ref, d1_ref, d2_ref)):
            onehot = (idx_ref[c:c + 1, :] == bucket).astype(F32)
            acc = acc + lax.dot_general(db_ref[...], onehot, (((1,), (1,)), ((), ())), precision=HI, preferred_element_type=F32)
        o_ref[...] = acc

    return pl.pallas_call(body, name="rel_bias_grad", out_shape=_sds(A_HEADS, REL_BUCKETS),
                          compiler_params=pltpu.CompilerParams(vmem_limit_bytes=VMEM_LIMIT_BYTES))(
                              *[d.reshape(A_HEADS, n) for d in dbs], idx_rows)


def _s5_param_fn(a_re, a_im, log_dt, bt_re, bt_im):
    dt = jnp.exp(log_dt)
    mag = jnp.exp(dt * a_re)
    abar_r, abar_i = mag * jnp.cos(dt * a_im), mag * jnp.sin(dt * a_im)
    den = a_re * a_re + a_im * a_im
    fr = ((abar_r - 1.0) * a_re + abar_i * a_im) / den
    fi = (abar_i * a_re - (abar_r - 1.0) * a_im) / den
    row = lax.broadcasted_iota(jnp.int32, (B_WIDTH, B_GROUPS), 0)
    grp = lax.broadcasted_iota(jnp.int32, (B_WIDTH, B_GROUPS), 1)
    expand = ((row // B_GROUP) == grp).astype(F32)
    fr_e, fi_e = _hdot(expand, fr), _hdot(expand, fi)
    return abar_r, abar_i, fr_e * bt_re - fi_e * bt_im, fr_e * bt_im + fi_e * bt_re


def _s5_params(a_re, a_im, log_dt, bt_re, bt_im):
    def body(ar, ai, ld, br, bi, o1, o2, o3, o4):
        o1[...], o2[...], o3[...], o4[...] = _s5_param_fn(ar[...], ai[...], ld[...], br[...], bi[...])

    return pl.pallas_call(body, name="s5_params",
                          out_shape=[_sds(B_GROUPS, B_STATE)] * 2 + [_sds(B_WIDTH, B_STATE)] * 2)(a_re, a_im, log_dt, bt_re, bt_im)


def _s5_params_bwd(a_re, a_im, log_dt, bt_re, bt_im, d1, d2, d3, d4):
    def body(ar, ai, ld, br, bi, c1, c2, c3, c4, o1, o2, o3, o4, o5):
        _, vjp = jax.vjp(_s5_param_fn, ar[...], ai[...], ld[...], br[...], bi[...])
        o1[...], o2[...], o3[...], o4[...], o5[...] = vjp((c1[...], c2[...], c3[...], c4[...]))

    return pl.pallas_call(body, name="s5_params_bwd",
                          out_shape=[_sds(B_GROUPS, B_STATE)] * 2 + [_sds(B_GROUPS, 1)] + [_sds(B_WIDTH, B_STATE)] * 2,
                          )(a_re, a_im, log_dt, bt_re, bt_im, d1, d2, d3, d4)


def _pick_row(x, r):
    rows = lax.broadcasted_iota(jnp.int32, x.shape, 0)
    return jnp.sum(jnp.where(rows == r, x, 0.0), axis=0, keepdims=True)


S5_SEG = 8
S5_STEPS = 32
S5_WIDTH = S5_TILES * S5_LANES


def _seg_rows(block):
    return jnp.swapaxes(block, 0, 1).reshape(block.shape[1] * S5_SEG, block.shape[2])


def _seg_block(rows):
    return jnp.swapaxes(rows.reshape(rows.shape[0] // S5_SEG, S5_SEG, rows.shape[1]), 0, 1)


def _seq_specs(n_i, rev):
    at = (lambda i: n_i - 1 - i) if rev else (lambda i: i)
    seg = pl.BlockSpec((S5_SEG, S5_STEPS, B_WIDTH), lambda i: (0, at(i), 0))
    x_spec = pl.BlockSpec((S5_SEG * S5_STEPS, S5_WIDTH), lambda i: (at(i), 0))
    return seg, x_spec, _fix((S5_TILES, 128, S5_LANES)), _fix((S5_TILES, S5_LANES, 128)), _fix((1, S5_WIDTH)), _fix((S5_SEG, S5_WIDTH))


def _tile_dots(dot, lhs, w_ref, lhs_width):
    return jnp.concatenate([dot(lhs[:, t * lhs_width:(t + 1) * lhs_width], w_ref[t]) for t in range(S5_TILES)], axis=1)


def _s5_entries(name, end_r, end_i, abr, abi, steps, reverse):
    def body(er_ref, ei_ref, ar_ref, ai_ref, or_ref, oi_ref):
        pr, pi_ = ar_ref[...], ai_ref[...]
        for _ in range(int(math.log2(steps))):
            pr, pi_ = pr * pr - pi_ * pi_, 2.0 * pr * pi_
        er, ei = er_ref[...], ei_ref[...]
        rows = lax.broadcasted_iota(jnp.int32, er.shape, 0)
        cr, ci = jnp.zeros_like(pr), jnp.zeros_like(pr)
        out_r, out_i = jnp.zeros_like(er), jnp.zeros_like(er)
        for g in (range(S5_SEG - 2, -1, -1) if reverse else range(1, S5_SEG)):
            src = g + 1 if reverse else g - 1
            cr, ci = _pick_row(er, src) + pr * cr - pi_ * ci, _pick_row(ei, src) + pr * ci + pi_ * cr
            out_r, out_i = jnp.where(rows == g, cr, out_r), jnp.where(rows == g, ci, out_i)
        or_ref[...] = out_r
        oi_ref[...] = out_i

    return pl.pallas_call(body, name=name, out_shape=[_sds(*end_r.shape)] * 2)(end_r, end_i, abr, abi)


def _s5_seq_fwd(u, btr, bti, ctr, cti, abr, abi, entry, store, exchange=None):
    s_len = u.shape[0]
    seg_len = s_len // S5_SEG
    n_i = seg_len // S5_STEPS
    rows = S5_SEG * S5_STEPS

    def body(u_ref, btr_ref, bti_ref, ctr_ref, cti_ref, ar_ref, ai_ref, er_ref, ei_ref, *rest):
        if store:
            xr_ref, xi_ref, y_ref, endr_ref, endi_ref, sr_ref, si_ref = rest
        else:
            endr_ref, endi_ref, sr_ref, si_ref = rest
        i = pl.program_id(0)

        @pl.when(i == 0)
        def _():
            sr_ref[...] = er_ref[...]
            si_ref[...] = ei_ref[...]

        ar = jnp.broadcast_to(ar_ref[...], (S5_SEG, S5_WIDTH))
        ai = jnp.broadcast_to(ai_ref[...], (S5_SEG, S5_WIDTH))
        ub = _seg_rows(u_ref[...])
        br, bi = _tile_dots(_bdot, ub, btr_ref, 128), _tile_dots(_bdot, ub, bti_ref, 128)
        sr, si = sr_ref[...], si_ref[...]
        for s in range(S5_STEPS):
            at = slice(S5_SEG * s, S5_SEG * (s + 1))
            sr, si = ar * sr - ai * si + br[at], ar * si + ai * sr + bi[at]
            if store:
                xr_ref[at, :] = sr
                xi_ref[at, :] = si
        sr_ref[...] = sr
        si_ref[...] = si
        if store:
            y_ref[...] = _seg_block(_tile_dots(_bdot, xr_ref[...], ctr_ref, S5_LANES) - _tile_dots(_bdot, xi_ref[...], cti_ref, S5_LANES))

        @pl.when(i == n_i - 1)
        def _():
            endr_ref[...] = sr
            endi_ref[...] = si

    seg, x_spec, b_spec, c_spec, a_spec, e_spec = _seq_specs(n_i, False)
    ends = [_sds(S5_SEG, S5_WIDTH)] * 2
    full = [_sds(s_len, S5_WIDTH)] * 2 + [_sds(S5_SEG, seg_len, B_WIDTH)] if store else []
    return _call_with_exchange(
        body, "s5_scan_fwd" if store else "s5_ends_fwd", (n_i,),
        [seg, b_spec, b_spec, c_spec, c_spec, a_spec, a_spec, e_spec, e_spec],
        ([x_spec, x_spec, seg] if store else []) + [e_spec, e_spec], full + ends,
        [pltpu.VMEM((S5_SEG, S5_WIDTH), F32)] * 2,
        (u.reshape(S5_SEG, seg_len, B_WIDTH), btr, bti, ctr, cti, abr, abi, *entry), exchange)


def _s5_seq_bwd(dy, xr, xi, u, btr, bti, ctr, cti, abr, abi, g_entry, x_entry, full, exchange=None):
    s_len = dy.shape[0]
    seg_len = s_len // S5_SEG
    n_i = seg_len // S5_STEPS
    rows = S5_SEG * S5_STEPS

    def body(*refs):
        if full:
            (dy_ref, btr_ref, bti_ref, ctr_ref, cti_ref, ar_ref, ai_ref, ger_ref, gei_ref,
             xr_ref, xi_ref, xrp_ref, xip_ref, xer_ref, xei_ref, u_ref,
             du_ref, dbtr_ref, dbti_ref, dctr_ref, dcti_ref, dar_ref, dai_ref, str_ref, sti_ref,
             sr_ref, si_ref, gr_s, gi_s) = refs
        else:
            (dy_ref, btr_ref, bti_ref, ctr_ref, cti_ref, ar_ref, ai_ref, ger_ref, gei_ref, str_ref, sti_ref, sr_ref, si_ref) = refs
        i = pl.program_id(0)

        @pl.when(i == 0)
        def _():
            sr_ref[...] = ger_ref[...]
            si_ref[...] = gei_ref[...]
            if full:
                for r in (dbtr_ref, dbti_ref, dctr_ref, dcti_ref, dar_ref, dai_ref):
                    r[...] = jnp.zeros_like(r)

        ar = jnp.broadcast_to(ar_ref[...], (S5_SEG, S5_WIDTH))
        ai = -jnp.broadcast_to(ai_ref[...], (S5_SEG, S5_WIDTH))
        dyb = _seg_rows(dy_ref[...])
        gr, gi = _tile_dots(_bdot_nt, dyb, ctr_ref, 128), -_tile_dots(_bdot_nt, dyb, cti_ref, 128)
        sr, si = sr_ref[...], si_ref[...]
        for s in range(S5_STEPS - 1, -1, -1):
            at = slice(S5_SEG * s, S5_SEG * (s + 1))
            sr, si = ar * sr - ai * si + gr[at], ar * si + ai * sr + gi[at]
            if full:
                gr_s[at, :] = sr
                gi_s[at, :] = si
        sr_ref[...] = sr
        si_ref[...] = si

        @pl.when(i == n_i - 1)
        def _():
            str_ref[...] = sr
            sti_ref[...] = si

        if full:
            g_r, g_i = gr_s[...], gi_s[...]
            du_ref[...] = _seg_block(_tile_dots(_bdot_nt, g_r, btr_ref, S5_LANES) + _tile_dots(_bdot_nt, g_i, bti_ref, S5_LANES))
            ub = _seg_rows(u_ref[...])
            xr_b, xi_b = xr_ref[...], xi_ref[...]
            for t in range(S5_TILES):
                lanes, cols = slice(t * S5_LANES, (t + 1) * S5_LANES), slice(t * 128, (t + 1) * 128)
                dbtr_ref[t] += _bdot_tn(ub[:, cols], g_r[:, lanes])
                dbti_ref[t] += _bdot_tn(ub[:, cols], g_i[:, lanes])
                dctr_ref[t] += _bdot_tn(xr_b[:, lanes], dyb[:, cols])
                dcti_ref[t] -= _bdot_tn(xi_b[:, lanes], dyb[:, cols])
            first = i == n_i - 1
            xpr = jnp.concatenate([jnp.where(first, xer_ref[...], xrp_ref[...]), xr_b[:rows - S5_SEG]], axis=0)
            xpi = jnp.concatenate([jnp.where(first, xei_ref[...], xip_ref[...]), xi_b[:rows - S5_SEG]], axis=0)
            dar_ref[...] += jnp.sum(g_r * xpr + g_i * xpi, axis=0, keepdims=True)
            dai_ref[...] += jnp.sum(g_i * xpr - g_r * xpi, axis=0, keepdims=True)

    seg, x_spec, b_spec, c_spec, a_spec, e_spec = _seq_specs(n_i, True)
    halo = pl.BlockSpec((S5_SEG, S5_WIDTH), lambda i: (jnp.maximum((n_i - 1 - i) * S5_STEPS - 1, 0), 0))
    starts = [_sds(S5_SEG, S5_WIDTH)] * 2
    in_specs = [seg, b_spec, b_spec, c_spec, c_spec, a_spec, a_spec, e_spec, e_spec]
    args = [dy.reshape(S5_SEG, seg_len, B_WIDTH), btr, bti, ctr, cti, abr, abi, *g_entry]
    state = [pltpu.VMEM((S5_SEG, S5_WIDTH), F32)] * 2
    if not full:
        return _call_with_exchange(body, "s5_starts_bwd", (n_i,), in_specs, [e_spec, e_spec], starts, state, args, None)
    return _call_with_exchange(
        body, "s5_scan_bwd", (n_i,),
        in_specs + [x_spec, x_spec, halo, halo, e_spec, e_spec, seg],
        [seg, b_spec, b_spec, c_spec, c_spec, a_spec, a_spec, e_spec, e_spec],
        [_sds(S5_SEG, seg_len, B_WIDTH)] + [_sds(S5_TILES, 128, S5_LANES)] * 2 + [_sds(S5_TILES, S5_LANES, 128)] * 2
        + [_sds(1, S5_WIDTH)] * 2 + starts,
        state + [pltpu.VMEM((rows, S5_WIDTH), F32)] * 2,
        args + [xr, xi, xr, xi, *x_entry, u.reshape(S5_SEG, seg_len, B_WIDTH)], exchange)


def _blockdiag_b(bbar_t):
    blocks = bbar_t.reshape(S5_TILES, 8, B_GROUP, B_STATE)
    return jnp.einsum('jgmp,gh->jgmhp', blocks, jnp.eye(8, dtype=F32)).reshape(S5_TILES, 128, S5_LANES)


def _blockdiag_b_t(d):
    return jnp.einsum('jgmgp->jgmp', d.reshape(S5_TILES, 8, B_GROUP, 8, B_STATE)).reshape(B_WIDTH, B_STATE)


def _blockdiag_c(c):
    blocks = c.reshape(S5_TILES, 8, B_GROUP, B_STATE)
    return jnp.einsum('jgmp,gh->jhpgm', blocks, jnp.eye(8, dtype=F32)).reshape(S5_TILES, S5_LANES, 128)


def _blockdiag_c_t(d):
    return jnp.einsum('jgpgm->jgmp', d.reshape(S5_TILES, 8, B_STATE, 8, B_GROUP)).reshape(B_GROUPS, B_GROUP, B_STATE)


def _l0_out(os, ls, ga, gb, ypre, u, x, d_skip, glu_w, glu_b, w_out, post_g, gate, exchange=None):
    s_len = x.shape[0]

    def body(o0, o1, o2, l0, l1, l2, ga_ref, gb_ref, yp_ref, u_ref, x_ref, d_ref, gw_ref, gbias_ref, w_ref, pg_ref, gt_ref, x1_ref, y_ref):
        oa = _merge_gate(*[_from_res(r[...]) for r in (o0, o1, o2, l0, l1, l2)], ga_ref[...])
        yb = _s5_gelu(yp_ref[...], u_ref[...], d_ref[...])
        ob = _s5_glu(yb, _bdot(yb, gw_ref[...]) + gbias_ref[...], gb_ref[...])
        y = _bdot(oa, w_ref[0:512, :]) + _bdot(ob, w_ref[512:1024, :])
        y_ref[...] = y
        x1_ref[...] = _post_res(y, x_ref[...], pg_ref[...], gt_ref[...])

    vec, half = _fix((1, D_MODEL)), _fix((1, 512))
    return _call_with_exchange(
        body, "l0_out", (s_len // TM,),
        [_res_spec(d) for d in DILATIONS] * 2 + [_row(TM, 512)] * 4
        + [_row(TM, D_MODEL), half, _fix((512, 512)), half, _fix((D_MODEL, D_MODEL)), vec, vec],
        [_row(TM, D_MODEL)] * 2, [_sds(s_len, D_MODEL)] * 2, [],
        (*os, *ls, ga, gb, ypre, u, x, d_skip, glu_w, glu_b, w_out, post_g, gate), exchange)


def _l0_out_bwd(os, ls, ga, gb, ypre, u, x, y, d_skip, glu_w, glu_b, w_out, post_g, gate, dx1, exchange=None):
    s_len = x.shape[0]

    def body(o0, o1, o2, l0, l1, l2, ga_ref, gb_ref, yp_ref, u_ref, x_ref, y_ref, d_ref, gw_ref, gbias_ref, w_ref, pg_ref, gt_ref, dx1_ref,
             do0, do1, do2, dl0, dl1, dl2, dga_ref, dgb_ref, dyp_ref, du_ref, dd_ref, dgw_ref, dgbias_ref, dw_ref, dpg_ref, dgt_ref):
        _zero_at_first([dd_ref, dgw_ref, dgbias_ref, dw_ref, dpg_ref, dgt_ref])
        _, vjp2 = jax.vjp(_post_res, y_ref[...], x_ref[...], pg_ref[...], gt_ref[...])
        dy, _, dpg, dgt = vjp2(dx1_ref[...])
        _acc(dpg_ref, dpg)
        _acc(dgt_ref, dgt)
        oa, vjp_a = jax.vjp(_merge_gate, *[_from_res(r[...]) for r in (o0, o1, o2, l0, l1, l2)], ga_ref[...])
        yb, vjp_g = jax.vjp(_s5_gelu, yp_ref[...], u_ref[...], d_ref[...])
        gl = _bdot(yb, gw_ref[...]) + gbias_ref[...]
        ob, vjp_b = jax.vjp(_s5_glu, yb, gl, gb_ref[...])
        dw_ref[0:512, :] += _bdot_tn(oa, dy)
        dw_ref[512:1024, :] += _bdot_tn(ob, dy)
        d1, d2, d3, e1, e2, e3, dga = vjp_a(_bdot_nt(dy, w_ref[0:512, :]))
        for ref, val, d in zip((do0, do1, do2, dl0, dl1, dl2), (d1, d2, d3, e1, e2, e3), DILATIONS * 2):
            ref[...] = _to_res(val, d)
        dga_ref[...] = dga
        dyb, dgl, dgb = vjp_b(_bdot_nt(dy, w_ref[512:1024, :]))
        dgb_ref[...] = dgb
        dgw_ref[...] += _bdot_tn(yb, dgl)
        _acc(dgbias_ref, jnp.sum(dgl, axis=0, keepdims=True))
        dyp, du, dd = vjp_g(dyb + _bdot_nt(dgl, gw_ref[...]))
        dyp_ref[...] = dyp
        du_ref[...] = du
        _acc(dd_ref, dd)

    vec, half = _fix((1, D_MODEL)), _fix((1, 512))
    r5, r10 = _row(TM, 512), _row(TM, D_MODEL)
    res6 = [_res_spec(d) for d in DILATIONS] * 2
    return _call_with_exchange(
        body, "l0_out_bwd", (s_len // TM,),
        res6 + [r5] * 4 + [r10, r10, half, _fix((512, 512)), half, _fix((D_MODEL, D_MODEL)), vec, vec, r10],
        res6 + [r5] * 4 + [half, _fix((512, 512)), half, _fix((D_MODEL, D_MODEL)), vec, vec],
        [_sds(*_res_shape(s_len, d)) for d in DILATIONS] * 2 + [_sds(s_len, 512)] * 4
        + [_sds(1, 512), _sds(512, 512), _sds(1, 512), _sds(D_MODEL, D_MODEL), _sds(1, D_MODEL), _sds(1, D_MODEL)],
        [], (*os, *ls, ga, gb, ypre, u, x, y, d_skip, glu_w, glu_b, w_out, post_g, gate, dx1), exchange)


def _l1_front(x, pre_g, scale, shift, w_in):
    s_len = x.shape[0]

    def body(x_ref, g_ref, sc_ref, sh_ref, w_ref, raw_ref, gate_ref, ba_ref, h_ref):
        hb = _pre_mod(x_ref[...], g_ref[...], sc_ref[...], sh_ref[...]).astype(BF)
        h_ref[...] = hb
        z = jnp.dot(hb, w_ref[...], preferred_element_type=F32)
        raw_ref[...] = z[:, 0:QKV]
        gate_ref[...] = z[:, QKV:QKV + 1024]
        ba_ref[...] = z[:, QKV + 1024:C_IN_PAD]

    vec = _fix((1, D_MODEL))
    return pl.pallas_call(
        body, name="l1_front", grid=(s_len // TM,),
        in_specs=[_row(TM, D_MODEL), vec, vec, vec, _fix((D_MODEL, C_IN_PAD))],
        out_specs=[_row(TM, QKV), _row(TM, 1024), _row(TM, 128), _row(TM, D_MODEL)],
        out_shape=[_sds(s_len, QKV), _sds(s_len, 1024), _sds(s_len, 128), _sds(s_len, D_MODEL, dtype=BF)],
        compiler_params=_cp("arbitrary"),
    )(x, pre_g, scale, shift, w_in)


def _bg_fn(ba, alog_row, dtb_row):
    lane = lax.broadcasted_iota(jnp.int32, (1, 128), 1)
    g = -jnp.exp(alog_row) * jax.nn.softplus(ba + dtb_row)
    return jnp.where(lane < C_HEADS, jax.nn.sigmoid(ba), jnp.where(lane < 2 * C_HEADS, g, 0.0))


def _act_q(c):
    q = jax.nn.silu(c)
    return q * lax.rsqrt(jnp.sum(q * q, axis=-1, keepdims=True) + EPS) * (C_DK ** -0.5)


def _act_k(c):
    k = jax.nn.silu(c)
    return k * lax.rsqrt(jnp.sum(k * k, axis=-1, keepdims=True) + EPS)


def _act_of(s):
    return _act_q if s < 8 else (_act_k if s < 16 else jax.nn.silu)


def _conv_taps(prev8, tile_ref, sl, next8=None):
    rows = tile_ref.shape[0]
    head = jnp.concatenate([prev8, tile_ref[0:8, sl]], axis=0)
    tail = None if next8 is None else jnp.concatenate([tile_ref[rows - 8:rows, sl], next8], axis=0)
    taps = []
    for j in range(C_CONV):
        shift = C_CONV - 1 - j
        pieces = [head[8:] if shift == 0 else pltpu.roll(head, shift, 0)[8:], tile_ref[pl.ds(8 - shift, rows - 8), sl]]
        if tail is not None:
            pieces.append(tail[8:] if shift == 0 else pltpu.roll(tail, shift, 0)[8:])
        taps.append(jnp.concatenate(pieces, axis=0))
    return taps


def _gdn_prep(raw, ba, conv_w, alog_row, dtb_row):
    s_len = raw.shape[0]

    def body(raw_ref, halo_ref, ba_ref, w_ref, al_ref, dt_ref, qkv_ref, bg_ref):
        bg_ref[...] = _bg_fn(ba_ref[...], al_ref[...], dt_ref[...])
        has_prev = (pl.program_id(0) > 0).astype(F32)
        for s in range(24):
            sl = slice(s * 128, (s + 1) * 128)
            taps = _conv_taps(halo_ref[:, sl] * has_prev, raw_ref, sl)
            conv = w_ref[3:4, sl] * taps[3]
            for j in range(3):
                conv = conv + w_ref[j:j + 1, sl] * taps[j]
            qkv_ref[:, sl] = _act_of(s)(conv)

    halo = pl.BlockSpec((8, QKV), lambda i: (jnp.maximum(i * (TM // 8) - 1, 0), 0))
    row128 = _fix((1, 128))
    return pl.pallas_call(
        body, name="gdn_prep", grid=(s_len // TM,),
        in_specs=[_row(TM, QKV), halo, _row(TM, 128), _fix((C_CONV, QKV)), row128, row128],
        out_specs=[_row(TM, QKV), _row(TM, 128)],
        out_shape=[_sds(s_len, QKV), _sds(s_len, 128)],
        compiler_params=_cp("arbitrary"),
    )(raw, raw, ba, conv_w, alog_row, dtb_row)


def _gdn_prep_bwd(raw, ba, conv_w, alog_row, dtb_row, dq, dk, dv, dbg):
    s_len = raw.shape[0]
    n_tiles = s_len // TM

    def body(raw_ref, prev_ref, next_ref, ba_ref, w_ref, al_ref, dt_ref, dq_ref, dqn_ref, dk_ref, dkn_ref, dv_ref, dvn_ref, dbg_ref,
             draw_ref, dba_ref, dw_ref, dal_ref, ddt_ref, dconv_ref):
        _zero_at_first([dw_ref, dal_ref, ddt_ref])
        i = pl.program_id(0)
        _, vjp_bg = jax.vjp(_bg_fn, ba_ref[...], al_ref[...], dt_ref[...])
        dba, dal, ddt = vjp_bg(dbg_ref[...])
        dba_ref[...] = dba
        _acc(dal_ref, dal)
        _acc(ddt_ref, ddt)
        has_prev = (i > 0).astype(F32)
        has_next = (i < n_tiles - 1).astype(F32)
        ct_refs = ((dq_ref, dqn_ref), (dk_ref, dkn_ref), (dv_ref, dvn_ref))
        for s in range(24):
            sl = slice(s * 128, (s + 1) * 128)
            hl = slice((s % 8) * 128, (s % 8 + 1) * 128)
            tile_ref, nxt_ref = ct_refs[s // 8]
            taps = _conv_taps(prev_ref[:, sl] * has_prev, raw_ref, sl, next_ref[:, sl] * has_next)
            conv = w_ref[3:4, sl] * taps[3]
            for j in range(3):
                conv = conv + w_ref[j:j + 1, sl] * taps[j]
            ct = jnp.concatenate([tile_ref[:, hl], nxt_ref[:, hl] * has_next], axis=0)
            _, vjp_act = jax.vjp(_act_of(s), conv)
            dconv, = vjp_act(ct)
            dconv_ref[...] = dconv
            draw = w_ref[3:4, sl] * dconv[:TM]
            for j in range(3):
                draw = draw + w_ref[j:j + 1, sl] * dconv_ref[pl.ds(3 - j, TM), :]
            draw_ref[:, sl] = draw
            for j in range(4):
                dw_ref[j:j + 1, sl] += jnp.sum(dconv[:TM] * taps[j][:TM], axis=0, keepdims=True)

    prev = pl.BlockSpec((8, QKV), lambda i: (jnp.maximum(i * (TM // 8) - 1, 0), 0))
    nxt = lambda n: pl.BlockSpec((8, n), lambda i: (jnp.minimum((i + 1) * (TM // 8), s_len // 8 - 1), 0))
    row128 = _fix((1, 128))
    ct_specs = [_row(TM, 1024), nxt(1024)] * 3
    return pl.pallas_call(
        body, name="gdn_prep_bwd", grid=(n_tiles,),
        in_specs=[_row(TM, QKV), prev, nxt(QKV), _row(TM, 128), _fix((C_CONV, QKV)), row128, row128] + ct_specs + [_row(TM, 128)],
        out_specs=[_row(TM, QKV), _row(TM, 128), _fix((C_CONV, QKV)), row128, row128],
        out_shape=[_sds(s_len, QKV), _sds(s_len, 128), _sds(C_CONV, QKV), _sds(1, 128), _sds(1, 128)],
        scratch_shapes=[pltpu.VMEM((TM + 8, 128), F32)],
        compiler_params=_cp("arbitrary"),
    )(raw, raw, raw, ba, conv_w, alog_row, dtb_row, dq, dq, dk, dk, dv, dv, dbg)


def _tein(eq, a, b):
    return jnp.einsum(eq, a, b, precision=lax.Precision.HIGH, preferred_element_type=F32)


def _unit_lower_inverse(lower):
    ri = lax.broadcasted_iota(jnp.int32, (C_CHUNK, C_CHUNK), 0)
    ci = lax.broadcasted_iota(jnp.int32, (C_CHUNK, C_CHUNK), 1)
    eye = (ri == ci).astype(F32)[None]
    mm = functools.partial(_bein, 'hij,hjk->hik')
    same_block = lambda size: (ri // size == ci // size)[None]
    n_mat = jnp.where(same_block(4), -lower, 0.0)
    inv = mm(eye + n_mat, eye + mm(n_mat, n_mat))
    for size in (4, 8, 16, 32):
        below = jnp.where(same_block(2 * size) & jnp.logical_not(same_block(size)), lower, 0.0)
        inv = inv - mm(inv, mm(below, inv))
    inv = _tein('hij,hjk->hik', inv, 2.0 * eye - _tein('hij,hjk->hik', eye + lower, inv))
    return jnp.where((ri >= ci)[None], inv, 0.0)


@jax.custom_vjp
def _known_inverse(lower, inv):
    return inv


def _known_inverse_fwd(lower, inv):
    return inv, inv


def _known_inverse_bwd(inv, d_inv):
    d_lower = -_bein('hik,hjk->hij', _bein('hji,hjk->hik', inv, d_inv), inv)
    return d_lower, jnp.zeros_like(inv)


_known_inverse.defvjp(_known_inverse_fwd, _known_inverse_bwd)


def _gdn_local(q, k, v, bgs, inv_known=None):
    lane = lax.broadcasted_iota(jnp.int32, (1, 128), 1)
    ri = lax.broadcasted_iota(jnp.int32, (C_CHUNK, C_CHUNK), 0)
    ci = lax.broadcasted_iota(jnp.int32, (C_CHUNK, C_CHUNK), 1)
    row_id = lax.broadcasted_iota(jnp.int32, (128, C_CHUNK), 0)
    beta, gc, gcj = [], [], []
    for bg in bgs:
        gc_t = _hdot((ri >= ci).astype(F32), bg)
        gc_rows = gc_t.T
        for h in range(C_HEADS):
            beta.append(jnp.sum(jnp.where(lane == h, bg, 0.0), axis=-1, keepdims=True))
            gc.append(jnp.sum(jnp.where(lane == C_HEADS + h, gc_t, 0.0), axis=-1, keepdims=True))
            gcj.append(jnp.sum(jnp.where(row_id == C_HEADS + h, gc_rows, 0.0), axis=0, keepdims=True))
    beta, gc, gcj = jnp.stack(beta, axis=0), jnp.stack(gc, axis=0), jnp.stack(gcj, axis=0)
    tril, strict = (ri >= ci)[None], (ri > ci)[None]
    decay = jnp.exp(jnp.where(tril, gc - gcj, -1e30))
    kb = k * beta
    lower = jnp.where(strict, _bein('hid,hjd->hij', kb, k) * decay, 0.0)
    inv = _unit_lower_inverse(lower) if inv_known is None else _known_inverse(lower, inv_known)
    egc = jnp.exp(gc)
    u_c = _bein('hij,hjd->hid', inv, v * beta)
    w_c = _bein('hij,hjd->hid', inv, kb * egc)
    aqk = _bein('hid,hjd->hij', q, k) * decay
    rowi = lax.broadcasted_iota(jnp.int32, (1, C_CHUNK, 1), 1)
    g_last = jnp.sum(jnp.where(rowi == C_CHUNK - 1, gc, 0.0), axis=1, keepdims=True)
    kd = k * jnp.exp(g_last - gc)
    return (u_c, w_c, aqk, q * egc, kd, jnp.exp(g_last)), inv


def _gdn_state(local, state):
    u_c, w_c, aqk, qg, kd, dec = local
    v_new = u_c - _bein('hik,hkv->hiv', w_c, state)
    o = _bein('hik,hkv->hiv', qg, state) + _bein('hij,hjv->hiv', aqk, v_new)
    return o, state * dec + _bein('hik,hiv->hkv', kd, v_new)


C_SUB = 4


def _gdn_group(q, k, v, bgs, state, inv_known=None):
    local, inv = _gdn_local(q, k, v, bgs, inv_known)
    outs = []
    for s in range(len(bgs)):
        o, state = _gdn_state(tuple(t[s * C_HEADS:(s + 1) * C_HEADS] for t in local), state)
        outs.append(o)
    return outs, state, inv


def _heads(ref):
    return jnp.stack([ref[s * C_CHUNK:(s + 1) * C_CHUNK, h * C_DK:(h + 1) * C_DK] for s in range(C_SUB) for h in range(C_HEADS)], axis=0)


def _put_heads(ref, sub, val):
    rows = slice(sub * C_CHUNK, (sub + 1) * C_CHUNK)
    for h in range(C_HEADS):
        ref[rows, h * C_DK:(h + 1) * C_DK] = val[h]


def _gdn_specs(s_len, rev):
    rows = C_SUB * C_CHUNK
    n_g = s_len // rows
    at = (lambda i: n_g - 1 - i) if rev else (lambda i: i)
    col = lambda c: pl.BlockSpec((rows, 1024), lambda i: (at(i), c))
    row128 = pl.BlockSpec((rows, 128), lambda i: (at(i), 0))
    state = pl.BlockSpec((1, C_HEADS, C_DK, C_DK), lambda i: (at(i), 0, 0, 0))
    inv = pl.BlockSpec((1, C_SUB * C_HEADS, C_CHUNK, C_CHUNK), lambda i: (at(i), 0, 0, 0))
    return n_g, col, row128, state, inv


def _gdn_fwd(qkv, bg):
    s_len = qkv.shape[0]
    n_g, col, row128, state_spec, inv_spec = _gdn_specs(s_len, False)

    def body(q_ref, k_ref, v_ref, bg_ref, o_ref, ss_ref, inv_ref, st_ref):
        _zero_at_first([st_ref])
        s0 = st_ref[...]
        ss_ref[0] = s0
        bgs = [bg_ref[s * C_CHUNK:(s + 1) * C_CHUNK, :] for s in range(C_SUB)]
        outs, s2, inv = _gdn_group(_heads(q_ref), _heads(k_ref), _heads(v_ref), bgs, s0)
        st_ref[...] = s2
        inv_ref[0] = inv
        for s in range(C_SUB):
            _put_heads(o_ref, s, outs[s])

    return pl.pallas_call(
        body, name="gdn_fwd", grid=(n_g,),
        in_specs=[col(0), col(1), col(2), row128],
        out_specs=[col(0), state_spec, inv_spec],
        out_shape=[_sds(s_len, 1024), _sds(n_g, C_HEADS, C_DK, C_DK), _sds(n_g, C_SUB * C_HEADS, C_CHUNK, C_CHUNK)],
        scratch_shapes=[pltpu.VMEM((C_HEADS, C_DK, C_DK), F32)],
        compiler_params=_cp("arbitrary"),
    )(qkv, qkv, qkv, bg)


def _gdn_bwd(qkv, bg, states, invs, do):
    s_len = qkv.shape[0]
    n_g, col, row128, state_spec, inv_spec = _gdn_specs(s_len, True)

    def body(q_ref, k_ref, v_ref, bg_ref, ss_ref, inv_ref, do_ref, dq_ref, dk_ref, dv_ref, dbg_ref, ds_ref):
        _zero_at_first([ds_ref])
        inv_known = inv_ref[0]

        def group(q, k, v, bgs, st):
            outs, st2, _ = _gdn_group(q, k, v, bgs, st, inv_known)
            return outs, st2

        bgs = [bg_ref[s * C_CHUNK:(s + 1) * C_CHUNK, :] for s in range(C_SUB)]
        _, vjp = jax.vjp(group, _heads(q_ref), _heads(k_ref), _heads(v_ref), bgs, ss_ref[0])
        douts = [jnp.stack([do_ref[s * C_CHUNK:(s + 1) * C_CHUNK, h * C_DK:(h + 1) * C_DK] for h in range(C_HEADS)], axis=0)
                 for s in range(C_SUB)]
        dq, dk, dv, dbgs, ds = vjp((douts, ds_ref[...]))
        ds_ref[...] = ds
        for s in range(C_SUB):
            dbg_ref[s * C_CHUNK:(s + 1) * C_CHUNK, :] = dbgs[s]
            for ref, val in ((dq_ref, dq), (dk_ref, dk), (dv_ref, dv)):
                _put_heads(ref, s, val[s * C_HEADS:(s + 1) * C_HEADS])

    return pl.pallas_call(
        body, name="gdn_bwd", grid=(n_g,),
        in_specs=[col(0), col(1), col(2), row128, state_spec, inv_spec, col(0)],
        out_specs=[col(0), col(0), col(0), row128],
        out_shape=[_sds(s_len, 1024)] * 3 + [_sds(s_len, 128)],
        scratch_shapes=[pltpu.VMEM((C_HEADS, C_DK, C_DK), F32)],
        compiler_params=_cp("arbitrary"),
    )(qkv, qkv, qkv, bg, states, invs, do)


def _head_norm_gate(o, gate, norm_g):
    return (_rms(o) * norm_g) * jax.nn.silu(gate)


def _l1_out_fb(o, gate_c, x1, target, norm_g, w_out, post_g, gate):
    s_len = x1.shape[0]

    def body(o_ref, gc_ref, x1_ref, t_ref, ng_ref, w_ref, pg_ref, gt_ref,
             loss_ref, dres_ref, do_ref, dgc_ref, dw_ref, dng_ref, dpg_ref, dgt_ref):
        _zero_at_first([loss_ref, dw_ref, dng_ref, dpg_ref, dgt_ref])
        ng = ng_ref[...]
        ons, vjps = [], []
        for h in range(C_HEADS):
            sl = slice(h * C_DK, (h + 1) * C_DK)
            on, vjp_h = jax.vjp(_head_norm_gate, o_ref[:, sl], gc_ref[:, sl], ng)
            ons.append(on)
            vjps.append(vjp_h)
        on_all = jnp.concatenate(ons, axis=-1)
        y = _bdot(on_all, w_ref[...])
        x2, vjp2 = jax.vjp(_post_res, y, x1_ref[...], pg_ref[...], gt_ref[...])
        err = x2 - t_ref[...]
        _acc(loss_ref, jnp.full((1, 128), 0.5 * jnp.sum(jnp.mean(err * err, axis=-1)), F32))
        dx2 = err * (1.0 / D_MODEL)
        dy, _, dpg, dgt = vjp2(dx2)
        dres_ref[...] = dx2
        _acc(dpg_ref, dpg)
        _acc(dgt_ref, dgt)
        dw_ref[...] += _bdot_tn(on_all, dy)
        don = _bdot_nt(dy, w_ref[...])
        for h in range(C_HEADS):
            sl = slice(h * C_DK, (h + 1) * C_DK)
            do_h, dgc_h, dng = vjps[h](don[:, sl])
            do_ref[:, sl] = do_h
            dgc_ref[:, sl] = dgc_h
            _acc(dng_ref, dng)

    vec, r10 = _fix((1, D_MODEL)), _row(TM, D_MODEL)
    row128 = _fix((1, 128))
    return pl.pallas_call(
        body, name="l1_out_fb", grid=(s_len // TM,),
        in_specs=[r10, r10, r10, r10, row128, _fix((D_MODEL, D_MODEL)), vec, vec],
        out_specs=[row128, r10, r10, r10, _fix((D_MODEL, D_MODEL)), row128, vec, vec],
        out_shape=[_sds(1, 128), _sds(s_len, D_MODEL), _sds(s_len, D_MODEL), _sds(s_len, D_MODEL),
                   _sds(D_MODEL, D_MODEL), _sds(1, 128), _sds(1, D_MODEL), _sds(1, D_MODEL)],
        compiler_params=_cp("arbitrary"),
    )(o, gate_c, x1, target, norm_g, w_out, post_g, gate)


def _row_of(v, width, at):
    return jnp.zeros((1, width), F32).at[0, at:at + v.shape[-1]].set(v.reshape(-1))


def _local_step(x, target, mod, wd, comm=None):
    s_len = x.shape[0]
    shift0, scale0, gate0 = (mod[0:1, i * 1024:(i + 1) * 1024] for i in range(3))
    shift1, scale1, gate1 = (mod[1:2, i * 1024:(i + 1) * 1024] for i in range(3))
    pre_g0, pre_g1 = wd["pre_g"][0:1], wd["pre_g"][1:2]
    post_g0, post_g1 = wd["post_g"][0:1], wd["post_g"][1:2]
    w_in0 = wd["ab_w_in"].astype(BF)
    d_skip, glu_b = wd["s5_d"].reshape(1, 512), wd["s5_glu_b"].reshape(1, 512)
    norm_g = wd["gdn_norm_g"].reshape(1, 128)
    alog_row = _row_of(wd["gdn_a_log"], 128, C_HEADS)
    dtb_row = _row_of(wd["gdn_dt_bias"], 128, C_HEADS)
    conv_w = wd["gdn_conv"]

    a_re, a_im = wd["s5_a_re"], wd["s5_a_im"]
    log_dt = wd["s5_log_dt"].reshape(B_GROUPS, 1)
    bt_re = wd["s5_b_re"].transpose(0, 2, 1).reshape(B_WIDTH, B_STATE)
    bt_im = wd["s5_b_im"].transpose(0, 2, 1).reshape(B_WIDTH, B_STATE)
    abar_r, abar_i, bbar_r, bbar_i = _s5_params(a_re, a_im, log_dt, bt_re, bt_im)
    abr, abi = abar_r.reshape(1, -1), abar_i.reshape(1, -1)
    btr, bti = _blockdiag_b(bbar_r).astype(BF), _blockdiag_b(bbar_i).astype(BF)
    ctr, cti = _blockdiag_c(wd["s5_c_re"]).astype(BF), _blockdiag_c(wd["s5_c_im"]).astype(BF)

    table = _bucket_table()
    biases = _attn_bias(wd["rel_bias"], jnp.asarray(table))
    front = _l0_front(x, pre_g0, scale0, shift0, w_in0)
    qs, ks, vs = front[0:3], front[3:6], front[6:9]
    u, ga, gb, h0 = front[9:]
    riders = [None] * 4 if comm is None else comm.late_exchanges()
    os, ls, got = [], [], []
    for i in range(3):
        (o_d, l_d), g = _attn_fwd(qs[i], ks[i], vs[i], biases[i], exchange=riders[i])
        os.append(o_d)
        ls.append(l_d)
        got.append(g)
    seg_len = s_len // S5_SEG
    zero_state = (jnp.zeros((S5_SEG, S5_WIDTH), F32),) * 2
    ends, _ = _s5_seq_fwd(u, btr, bti, ctr, cti, abr, abi, zero_state, False)
    x_entry = _s5_entries("s5_entries_fwd", *ends, abr, abi, seg_len, False)
    (xr, xi, ypre3, _, _), g = _s5_seq_fwd(u, btr, bti, ctr, cti, abr, abi, x_entry, True, exchange=riders[3])
    got.append(g)
    ypre = ypre3.reshape(s_len, B_WIDTH)
    rider = None
    if comm is not None:
        mine0, mine1 = comm.late_halves(got)
        wd = {**wd, **comm.full_weights(["ab_w_out", "s5_glu_w"], mine0, _sibling_exchange("gather_w_sibling_l0", mine0))}
        rider = (mine1, "sibling")
    w_out0 = wd["ab_w_out"].astype(BF)
    glu_w = wd["s5_glu_w"].astype(BF)
    (x1, y0), theirs1 = _l0_out(os, ls, ga, gb, ypre, u, x, d_skip, glu_w, glu_b, w_out0, post_g0, gate0, exchange=rider)
    if comm is not None:
        wd = {**wd, **comm.full_weights(["gdn_w_in", "gdn_w_out"], mine1, theirs1)}
    w_in1 = wd["gdn_w_in"]
    if w_in1.shape[1] == C_IN:
        w_in1 = jnp.concatenate([w_in1, jnp.zeros((D_MODEL, C_IN_PAD - C_IN), w_in1.dtype)], axis=1)
    w_in1 = w_in1.astype(BF)
    w_out1 = wd["gdn_w_out"].astype(BF)

    raw, gate_c, ba, h1 = _l1_front(x1, pre_g1, scale1, shift1, w_in1)
    qkv, bg = _gdn_prep(raw, ba, conv_w, alog_row, dtb_row)
    o_gdn, states, invs = _gdn_fwd(qkv, bg)
    loss_row, dres1, do_gdn, dgate_c, dw_out1, dnorm_g, dpost_g1, dgate1 = _l1_out_fb(
        o_gdn, gate_c, x1, target, norm_g, w_out1, post_g1, gate1)

    dq1, dk1, dv1, dbg = _gdn_bwd(qkv, bg, states, invs, do_gdn)
    draw, dba, dconv_w, dalog_row, ddtb_row = _gdn_prep_bwd(raw, ba, conv_w, alog_row, dtb_row, dq1, dk1, dv1, dbg)
    dz1, dx1, dpre_g1, dscale1, dshift1 = _front_bwd(
        "l1_front_bwd", x1, pre_g1, scale1, shift1, w_in1, dres1, [[draw], [dgate_c], [dba]], [QKV, 1024, 128])
    dw_in1 = _matmul_tn("l1_dw_in", h1, dz1, 1408)

    l1_names, l0_names = ["gdn_w_in", "gdn_w_out"], ["ab_w_out", "s5_glu_w"]
    rider = None if comm is None else (comm.split_halves({"gdn_w_in": dw_in1, "gdn_w_out": dw_out1}), "sibling")
    l0b, from_sibling1 = _l0_out_bwd(os, ls, ga, gb, ypre, u, x, y0, d_skip, glu_w, glu_b, w_out0, post_g0, gate0, dx1, exchange=rider)
    dos, dls = l0b[0:3], l0b[3:6]
    dga, dgb, dypre, du_skip, dd_skip, dglu_w, dglu_b, dw_out0, dpost_g0, dgate0 = l0b[6:]
    rider = None if comm is None else (comm.split_halves({"ab_w_out": dw_out0, "s5_glu_w": dglu_w}), "sibling")
    starts, _ = _s5_seq_bwd(dypre, None, None, None, btr, bti, ctr, cti, abr, abi, zero_state, None, False)
    g_entry = _s5_entries("s5_entries_bwd", *starts, abr, -abi, seg_len, True)
    (du3, dbtr, dbti, dctr, dcti, dabr, dabi, _, _), from_sibling0 = _s5_seq_bwd(
        dypre, xr, xi, u, btr, bti, ctr, cti, abr, abi, g_entry, x_entry, True, exchange=rider)
    du_scan = du3.reshape(s_len, B_WIDTH)
    riders = [None] * 3
    if comm is not None:
        riders = [(comm.chip_partials(l1_names, from_sibling1), "scatter"), (comm.chip_partials(l0_names, from_sibling0), "scatter"), None]
    dqs, dks, dvs, dbs = [], [], [], []
    for i in range(3):
        (dq_d, dk_d, dv_d, db_d), got_d = _attn_bwd(qs[i], ks[i], vs[i], biases[i], os[i], ls[i], dos[i], dls[i], exchange=riders[i])
        dqs.append(dq_d)
        dks.append(dk_d)
        dvs.append(dv_d)
        dbs.append(db_d)
        if comm is not None and riders[i] is not None:
            comm.received.update(zip((l1_names, l0_names)[i], got_d))
    parts = [dqs, dks, dvs, [du_skip, du_scan], [dga], [dgb]]
    dz0, grad_x, dpre_g0, dscale0, dshift0 = _front_bwd(
        "l0_front_bwd", x, pre_g0, scale0, shift0, w_in0, dx1, parts, [512] * 6)
    dw_in0 = _matmul_tn("l0_dw_in", h0, dz0, 768)

    idx_rows = jnp.asarray(table.reshape(3, -1), F32)
    drel = _rel_bias_grad(dbs, idx_rows).T
    da_re, da_im, dlog_dt, dbt_re, dbt_im = _s5_params_bwd(
        a_re, a_im, log_dt, bt_re, bt_im, dabr.reshape(B_GROUPS, B_STATE), dabi.reshape(B_GROUPS, B_STATE),
        _blockdiag_b_t(dbtr), _blockdiag_b_t(dbti))
    unb = lambda d: d.reshape(B_GROUPS, B_GROUP, B_STATE).transpose(0, 2, 1)
    grads = {
        "pre_g": jnp.concatenate([dpre_g0, dpre_g1], 0), "post_g": jnp.concatenate([dpost_g0, dpost_g1], 0),
        "rel_bias": drel, "ab_w_in": dw_in0, "ab_w_out": dw_out0,
        "s5_a_re": da_re, "s5_a_im": da_im, "s5_log_dt": dlog_dt.reshape(B_GROUPS),
        "s5_b_re": unb(dbt_re), "s5_b_im": unb(dbt_im),
        "s5_c_re": _blockdiag_c_t(dctr), "s5_c_im": _blockdiag_c_t(dcti),
        "s5_d": dd_skip.reshape(512), "s5_glu_w": dglu_w, "s5_glu_b": dglu_b.reshape(512),
        "gdn_w_in": dw_in1[:, :C_IN], "gdn_conv": dconv_w,
        "gdn_a_log": dalog_row[0, C_HEADS:2 * C_HEADS], "gdn_dt_bias": ddtb_row[0, C_HEADS:2 * C_HEADS],
        "gdn_norm_g": dnorm_g.reshape(128), "gdn_w_out": dw_out1,
    }
    dmod = jnp.concatenate([jnp.concatenate([dshift0, dscale0, dgate0], 1), jnp.concatenate([dshift1, dscale1, dgate1], 1)], 0)
    return loss_row[0, 0], grad_x, grads, dmod


def _place():
    return lax.axis_index("x"), lax.axis_index("y"), lax.axis_index("c")


def _flip(v, bit):
    return 1 - v if bit else v


def _hbm_call(name, body, arrs, out_shapes, n_sem):
    any_spec = pl.BlockSpec(memory_space=pl.ANY)
    return pl.pallas_call(
        body, name=name,
        in_specs=[any_spec] * len(arrs), out_specs=[any_spec] * len(out_shapes), out_shape=out_shapes,
        scratch_shapes=[pltpu.SemaphoreType.DMA((n_sem,)), pltpu.SemaphoreType.DMA((n_sem,))],
    )(*arrs)


def _own_slot(gathered, own, slot):
    idx = lax.broadcasted_iota(jnp.int32, (gathered.shape[0],) + (1,) * own.ndim, 0)
    return jnp.where(idx == slot, own[None], gathered)


def _all_gather8(name, arr):
    def body(x_ref, out_ref, send_sems, recv_sems):
        x, y, c = _place()
        me = 4 * x + 2 * y + c
        sends, recvs = [], []
        for m in range(1, 8):
            peer = (_flip(x, m & 4), _flip(y, m & 2), _flip(c, m & 1))
            sends.append(pltpu.make_async_remote_copy(x_ref, out_ref.at[me], send_sems.at[m - 1], recv_sems.at[m - 1],
                                                      device_id=peer, device_id_type=MESH))
            recvs.append(pltpu.make_async_remote_copy(x_ref, out_ref.at[4 * peer[0] + 2 * peer[1] + peer[2]], send_sems.at[m - 1],
                                                      recv_sems.at[m - 1], device_id=peer, device_id_type=MESH))
        for cp in sends:
            cp.start()
        for cp in recvs:
            cp.wait_recv()
        for cp in sends:
            cp.wait_send()

    return _hbm_call(name, body, [arr], [jax.ShapeDtypeStruct((8,) + arr.shape, arr.dtype)], 7)[0]


def _all_to_all8(name, arr):
    def body(x_ref, out_ref, send_sems, recv_sems):
        x, y, c = _place()
        me = 4 * x + 2 * y + c
        sends, recvs = [], []
        for m in range(1, 8):
            peer = (_flip(x, m & 4), _flip(y, m & 2), _flip(c, m & 1))
            peer_id = 4 * peer[0] + 2 * peer[1] + peer[2]
            sends.append(pltpu.make_async_remote_copy(x_ref.at[peer_id], out_ref.at[me], send_sems.at[m - 1], recv_sems.at[m - 1],
                                                      device_id=peer, device_id_type=MESH))
            recvs.append(pltpu.make_async_remote_copy(x_ref.at[peer_id], out_ref.at[peer_id], send_sems.at[m - 1], recv_sems.at[m - 1],
                                                      device_id=peer, device_id_type=MESH))
        for cp in sends:
            cp.start()
        for cp in recvs:
            cp.wait_recv()
        for cp in sends:
            cp.wait_send()

    return _hbm_call(name, body, [arr], [jax.ShapeDtypeStruct(arr.shape, arr.dtype)], 7)[0]


def _chip_copies(ins, outs, send_sems, recv_sems, scatter):
    x, y, c = _place()
    mine = 2 * x + y
    sends, recvs = [], []
    for a in range(len(ins)):
        for m in range(1, 4):
            px, py = _flip(x, m & 2), _flip(y, m & 1)
            k = 3 * a + m - 1
            src = ins[a].at[2 * px + py] if scatter else ins[a]
            sends.append(pltpu.make_async_remote_copy(src, outs[a].at[mine], send_sems.at[k], recv_sems.at[k],
                                                      device_id=(px, py, c), device_id_type=MESH))
            recvs.append(pltpu.make_async_remote_copy(src, outs[a].at[2 * px + py], send_sems.at[k], recv_sems.at[k],
                                                      device_id=(px, py, c), device_id_type=MESH))
    return sends, recvs


def _chip_shapes(arrs, scatter):
    return [jax.ShapeDtypeStruct(a.shape if scatter else (4,) + a.shape, a.dtype) for a in arrs]


def _chip_exchange(name, arrs, scatter):
    n = len(arrs)

    def body(*refs):
        sends, recvs = _chip_copies(refs[:n], refs[n:2 * n], refs[2 * n], refs[2 * n + 1], scatter)
        for cp in sends:
            cp.start()
        for cp in recvs:
            cp.wait_recv()
        for cp in sends:
            cp.wait_send()

    return _hbm_call(name, body, arrs, _chip_shapes(arrs, scatter), 3 * n)


def _call_with_exchange(body, name, grid, in_specs, out_specs, out_shape, scratch_shapes, args, exchange):
    if exchange is None:
        return pl.pallas_call(body, name=name, grid=grid, in_specs=in_specs, out_specs=out_specs, out_shape=out_shape,
                              scratch_shapes=scratch_shapes, compiler_params=_cp(*["arbitrary"] * len(grid)))(*args), []
    arrs, kind = exchange
    n_in, n_out, n_ex, n_scr = len(in_specs), len(out_specs), len(arrs), len(scratch_shapes)
    n_sem = n_ex if kind == "sibling" else 3 * n_ex
    ex_shapes = [jax.ShapeDtypeStruct(a.shape, a.dtype) for a in arrs] if kind == "sibling" else _chip_shapes(arrs, kind == "scatter")

    def fused(*refs):
        ins, ex_in = refs[:n_in], refs[n_in:n_in + n_ex]
        outs, ex_out = refs[n_in + n_ex:n_in + n_ex + n_out], refs[n_in + n_ex + n_out:n_in + 2 * n_ex + n_out]
        rest = refs[n_in + 2 * n_ex + n_out:]
        if kind == "sibling":
            sends = recvs = _sibling_copies(ex_in, ex_out, rest[n_scr], rest[n_scr + 1])
        else:
            sends, recvs = _chip_copies(ex_in, ex_out, rest[n_scr], rest[n_scr + 1], kind == "scatter")
        first, last = pl.program_id(0) == 0, pl.program_id(0) == grid[0] - 1
        for k in range(1, len(grid)):
            first, last = first & (pl.program_id(k) == 0), last & (pl.program_id(k) == grid[k] - 1)

        @pl.when(first)
        def _():
            for cp in sends:
                cp.start()

        body(*ins, *outs, *rest[:n_scr])

        @pl.when(last)
        def _():
            for cp in recvs:
                cp.wait_recv()
            for cp in sends:
                cp.wait_send()

    any_spec = pl.BlockSpec(memory_space=pl.ANY)
    res = pl.pallas_call(
        fused, name=name, grid=grid, in_specs=list(in_specs) + [any_spec] * n_ex, out_specs=list(out_specs) + [any_spec] * n_ex,
        out_shape=list(out_shape) + ex_shapes,
        scratch_shapes=list(scratch_shapes) + [pltpu.SemaphoreType.DMA((n_sem,))] * 2,
        compiler_params=_cp(*["arbitrary"] * len(grid)))(*args, *arrs)
    return res[:n_out], res[n_out:]


def _sibling_copies(ins, outs, send_sems, recv_sems):
    x, y, c = _place()
    return [pltpu.make_async_remote_copy(ins[a], outs[a], send_sems.at[a], recv_sems.at[a],
                                         device_id=(x, y, 1 - c), device_id_type=MESH) for a in range(len(ins))]


def _sibling_exchange(name, arrs):
    n = len(arrs)

    def body(*refs):
        copies = _sibling_copies(refs[:n], refs[n:2 * n], refs[2 * n], refs[2 * n + 1])
        for cp in copies:
            cp.start()
        for cp in copies:
            cp.wait_recv()
        for cp in copies:
            cp.wait_send()

    return _hbm_call(name, body, arrs, [jax.ShapeDtypeStruct(a.shape, a.dtype) for a in arrs], n)


def _row_tile(rows):
    for t in (256, 128, 64, 32, 16, 8):
        if rows % t == 0:
            return t
    return rows


def _pair_sum(name, a, b, out_dtype):
    rows, cols = a.shape
    tr = _row_tile(rows)

    def body(a_ref, b_ref, o_ref):
        o_ref[...] = (a_ref[...] + b_ref[...]).astype(out_dtype)

    return pl.pallas_call(body, name=name, grid=(rows // tr,), in_specs=[_row(tr, cols)] * 2, out_specs=_row(tr, cols),
                          out_shape=_sds(rows, cols, dtype=out_dtype), compiler_params=_cp("arbitrary"))(a, b)


def _chip_sum(name, recv, partial, mine):
    n, rows, cols = recv.shape
    tr = _row_tile(rows)

    def body(mine_ref, *refs):
        own = refs[n][0].astype(F32)
        acc = None
        for s in range(n):
            term = jnp.where(mine_ref[0] == s, own, refs[s][0].astype(F32))
            acc = term if acc is None else acc + term
        refs[-1][...] = acc

    def slot_spec(s):
        return pl.BlockSpec((1, tr, cols), lambda i, m: (jnp.where(m[0] == s, (s + 1) % n, s), i, 0))

    grid_spec = pltpu.PrefetchScalarGridSpec(
        num_scalar_prefetch=1, grid=(rows // tr,),
        in_specs=[slot_spec(s) for s in range(n)] + [pl.BlockSpec((1, tr, cols), lambda i, m: (m[0], i, 0))],
        out_specs=pl.BlockSpec((tr, cols), lambda i, m: (i, 0)))
    return pl.pallas_call(body, name=name, grid_spec=grid_spec, out_shape=_sds(rows, cols),
                          compiler_params=_cp("arbitrary"))(mine, *([recv] * n), partial)


def _slot_sum(name, arr):
    n, rows, cols = arr.shape
    tr = _row_tile(rows)

    def body(*refs):
        acc = refs[0][0]
        for r in refs[1:-1]:
            acc = acc + r[0]
        refs[-1][...] = acc

    specs = [pl.BlockSpec((1, tr, cols), functools.partial(lambda s, i: (s, i, 0), s)) for s in range(n)]
    return pl.pallas_call(body, name=name, grid=(rows // tr,), in_specs=specs, out_specs=_row(tr, cols),
                          out_shape=_sds(rows, cols), compiler_params=_cp("arbitrary"))(*([arr] * n))


def _adamw(name, w, g, m, v):
    rows, cols = w.shape
    tr = _row_tile(rows)

    def body(w_ref, g_ref, m_ref, v_ref, d_ref, nm_ref, nv_ref):
        g_ = g_ref[...]
        m_ = ADAM_B1 * m_ref[...] + (1.0 - ADAM_B1) * g_
        v_ = ADAM_B2 * v_ref[...] + (1.0 - ADAM_B2) * (g_ * g_)
        m_hat = m_ / (1.0 - ADAM_B1 ** ADAM_STEP)
        v_hat = v_ / (1.0 - ADAM_B2 ** ADAM_STEP)
        d_ref[...] = -ADAM_LR * (m_hat / (jnp.sqrt(v_hat) + ADAM_EPS) + ADAM_WD * w_ref[...])
        nm_ref[...] = m_
        nv_ref[...] = v_

    spec = _row(tr, cols)
    return pl.pallas_call(body, name=name, grid=(rows // tr,), in_specs=[spec] * 4, out_specs=[spec] * 3,
                          out_shape=[_sds(rows, cols)] * 3, compiler_params=_cp("arbitrary"))(w, g, m, v)


def _adamw_many(name, ws, gs, ms, vs):
    n = len(ws)

    def body(*refs):
        for i in range(n):
            w_ref, g_ref, m_ref, v_ref = (refs[k * n + i] for k in range(4))
            d_ref, nm_ref, nv_ref = (refs[(4 + k) * n + i] for k in range(3))
            g_ = g_ref[...]
            m_ = ADAM_B1 * m_ref[...] + (1.0 - ADAM_B1) * g_
            v_ = ADAM_B2 * v_ref[...] + (1.0 - ADAM_B2) * (g_ * g_)
            m_hat = m_ / (1.0 - ADAM_B1 ** ADAM_STEP)
            v_hat = v_ / (1.0 - ADAM_B2 ** ADAM_STEP)
            d_ref[...] = -ADAM_LR * (m_hat / (jnp.sqrt(v_hat) + ADAM_EPS) + ADAM_WD * w_ref[...])
            nm_ref[...] = m_
            nv_ref[...] = v_

    shapes = [_sds(*a.shape) for a in ws]
    res = pl.pallas_call(body, name=name, out_shape=shapes * 3,
                         compiler_params=pltpu.CompilerParams(vmem_limit_bytes=VMEM_LIMIT_BYTES))(*ws, *gs, *ms, *vs)
    return [(res[i], res[n + i], res[2 * n + i]) for i in range(n)]


def _adamw_halves(name, w, g_mine, g_sibling, m, v, core):
    _, rows, cols = w.shape
    half = rows // 2
    tr = _row_tile(half)
    per_half = half // tr

    def body(core_ref, w_ref, gm_ref, gs_ref, m_ref, v_ref, g_ref, d_ref, nm_ref, nv_ref):
        g_ = jnp.where(pl.program_id(0) // per_half == core_ref[0], gm_ref[...], gs_ref[...])
        m_ = ADAM_B1 * m_ref[...] + (1.0 - ADAM_B1) * g_
        v_ = ADAM_B2 * v_ref[...] + (1.0 - ADAM_B2) * (g_ * g_)
        m_hat = m_ / (1.0 - ADAM_B1 ** ADAM_STEP)
        v_hat = v_ / (1.0 - ADAM_B2 ** ADAM_STEP)
        g_ref[...] = g_
        d_ref[...] = -ADAM_LR * (m_hat / (jnp.sqrt(v_hat) + ADAM_EPS) + ADAM_WD * w_ref[...])
        nm_ref[...] = m_
        nv_ref[...] = v_

    full = pl.BlockSpec((None, tr, cols), lambda i, c: (0, i, 0))
    in_half = pl.BlockSpec((tr, cols), lambda i, c: (i % per_half, 0))
    grid_spec = pltpu.PrefetchScalarGridSpec(num_scalar_prefetch=1, grid=(rows // tr,),
                                             in_specs=[full, in_half, in_half, full, full], out_specs=[full] * 4)
    return pl.pallas_call(body, name=name, grid_spec=grid_spec, out_shape=[_sds(1, rows, cols)] * 4,
                          compiler_params=_cp("arbitrary"))(core, w, g_mine, g_sibling, m, v)


def _mod_local(c_all, ada_w):
    def body(c_ref, w_ref, o_ref):
        c_act = jax.nn.silu(c_ref[...])
        for l in range(2):
            o_ref[l] = _hdot(c_act, w_ref[l])

    return pl.pallas_call(body, name="mod_local", out_shape=_sds(2, 8, ada_w.shape[2]),
                          compiler_params=pltpu.CompilerParams(vmem_limit_bytes=VMEM_LIMIT_BYTES))(c_all, ada_w)


def _ada_w_grad(c_all, dmod_cols):
    def body(c_ref, d_ref, o_ref):
        c_act = jax.nn.silu(c_ref[...])
        for l in range(2):
            o_ref[l] = lax.dot_general(c_act, d_ref[l], (((0,), (0,)), ((), ())), precision=HI, preferred_element_type=F32)

    return pl.pallas_call(body, name="ada_w_grad", out_shape=_sds(2, D_MODEL, dmod_cols.shape[2]),
                          compiler_params=pltpu.CompilerParams(vmem_limit_bytes=VMEM_LIMIT_BYTES))(c_all, dmod_cols)


_SMALL = ("ada_b", "pre_g", "post_g", "rel_bias", "s5_a_re", "s5_a_im", "s5_log_dt", "s5_b_re", "s5_b_im", "s5_c_re", "s5_c_im",
          "s5_d", "s5_glu_b", "gdn_a_log", "gdn_dt_bias", "gdn_norm_g")
_SHARDED = ("ab_w_in", "ab_w_out", "s5_glu_w", "gdn_w_in", "gdn_w_out")
_COL_SHARDED = ("ab_w_in", "gdn_w_in")
_WEIGHTS = ("ada_w", "ada_b", "pre_g", "post_g", "rel_bias", "ab_w_in", "ab_w_out", "s5_a_re", "s5_a_im", "s5_log_dt", "s5_b_re",
            "s5_b_im", "s5_c_re", "s5_c_im", "s5_d", "s5_glu_w", "s5_glu_b", "gdn_w_in", "gdn_conv", "gdn_a_log", "gdn_dt_bias",
            "gdn_norm_g", "gdn_w_out")


def _rows128(n):
    return -(-n // 128)


def _pack(arrs, total_rows):
    pieces = []
    for a in arrs:
        flat = a.reshape(-1)
        pieces.append(jnp.pad(flat, (0, _rows128(flat.shape[0]) * 128 - flat.shape[0])).reshape(-1, 128))
    used = sum(p.shape[0] for p in pieces)
    pieces.append(jnp.zeros((total_rows - used, 128), F32))
    return jnp.concatenate(pieces, axis=0)


def _unpack(buf, shapes):
    out, at = [], 0
    for shp in shapes:
        n = int(np.prod(shp))
        out.append(buf[at:at + _rows128(n)].reshape(-1)[:n].reshape(shp))
        at += _rows128(n)
    return out


def _full_from_halves(g):
    return g.transpose(1, 0, 2, 3).reshape(8 * g.shape[2], g.shape[3])


def _join_col_shards(name, mine, theirs, core, width):
    n, h, cols = mine.shape
    tr = _row_tile(h)
    per_half = h // tr

    def body(core_ref, a_ref, b_ref, o_ref):
        def put(src_ref):
            pad = [jnp.zeros((tr, width - n * cols), o_ref.dtype)] if width > n * cols else []
            o_ref[...] = jnp.concatenate([src_ref[s] for s in range(n)] + pad, axis=1)

        is_mine = pl.program_id(0) // per_half == core_ref[0]
        pl.when(is_mine)(functools.partial(put, a_ref))
        pl.when(jnp.logical_not(is_mine))(functools.partial(put, b_ref))

    half = pl.BlockSpec((n, tr, cols), lambda i, c: (0, i % per_half, 0))
    grid_spec = pltpu.PrefetchScalarGridSpec(num_scalar_prefetch=1, grid=(2 * per_half,), in_specs=[half, half],
                                             out_specs=pl.BlockSpec((tr, width), lambda i, c: (i, 0)))
    return pl.pallas_call(body, name=name, grid_spec=grid_spec, out_shape=_sds(2 * h, width, dtype=mine.dtype),
                          compiler_params=_cp("arbitrary"))(core, mine, theirs)


def _split_col_shards(name, g, cols, half, add=None, out_dtype=F32):
    h = g.shape[0] // 2
    tr = _row_tile(h)
    per_half = h // tr

    def body(half_ref, g_ref, *refs):
        for s in range(4):
            part = g_ref[:, s * cols:(s + 1) * cols]
            refs[-1][s] = (part if add is None else part + refs[0][s]).astype(out_dtype)

    shards = pl.BlockSpec((4, tr, cols), lambda i, c: (0, i, 0))
    grid_spec = pltpu.PrefetchScalarGridSpec(
        num_scalar_prefetch=1, grid=(per_half,),
        in_specs=[pl.BlockSpec((tr, g.shape[1]), lambda i, c: (c[0] * per_half + i, 0))] + ([] if add is None else [shards]),
        out_specs=shards)
    return pl.pallas_call(body, name=name, grid_spec=grid_spec, out_shape=_sds(4, h, cols, dtype=out_dtype),
                          compiler_params=_cp("arbitrary"))(half, g, *([] if add is None else [add]))


_LATE = ("ab_w_out", "s5_glu_w", "gdn_w_in", "gdn_w_out")


class _WeightExchanges:
    def __init__(self, shards, core, chip):
        self.core, self.chip = core, chip
        self.core_1 = jnp.reshape(core, (1,)).astype(jnp.int32)
        self.half = {}
        for name, shard in shards.items():
            h = shard.shape[0] // 2
            self.half[name] = lax.dynamic_slice_in_dim(shard.astype(BF), core * h, h, axis=0)
        self.mine, self.partial, self.received = {}, {}, {}

    def my_halves(self, names, from_chips):
        return [_own_slot(g, self.half[n], self.chip) for n, g in zip(names, from_chips)]

    def full_weights(self, names, mine, theirs):
        full = {}
        for n, a, b in zip(names, mine, theirs):
            if n in _COL_SHARDED:
                full[n] = _join_col_shards("join_" + n, a, b, self.core_1, C_IN_PAD if n == "gdn_w_in" else 4 * a.shape[2])
            else:
                full[n] = _full_from_halves(jnp.where(self.core == 0, jnp.stack([a, b], 0), jnp.stack([b, a], 0)))
        return full

    def first_weights(self):
        mine = self.my_halves(["ab_w_in"], _chip_exchange("gather_w_chips", [self.half["ab_w_in"]], False))
        return self.full_weights(["ab_w_in"], mine, _sibling_exchange("gather_w_sibling_first", mine))

    def late_exchanges(self):
        rows = self.half["gdn_w_in"].shape[0] // 2
        pieces = [self.half["gdn_w_in"][:rows], self.half["gdn_w_in"][rows:]]
        return [([self.half["ab_w_out"], self.half["s5_glu_w"]], "gather"), ([self.half["gdn_w_out"]], "gather"),
                ([pieces[0]], "gather"), ([pieces[1]], "gather")]

    def late_halves(self, got):
        return (self.my_halves(["ab_w_out", "s5_glu_w"], got[0]),
                self.my_halves(["gdn_w_in", "gdn_w_out"], [jnp.concatenate([got[2][0], got[3][0]], axis=1), got[1][0]]))

    def split_halves(self, grads):
        other = []
        for name, g in grads.items():
            if name in _COL_SHARDED:
                self.mine[name] = g
                other.append(_split_col_shards("split_other_" + name, g, self.half[name].shape[1], 1 - self.core_1))
                continue
            sm = g.reshape(4, g.shape[0] // 4, g.shape[1])
            h = sm.shape[1] // 2
            self.mine[name] = lax.dynamic_slice_in_dim(sm, self.core * h, h, axis=1)
            other.append(lax.dynamic_slice_in_dim(sm, (1 - self.core) * h, h, axis=1))
        return other

    def chip_partials(self, names, from_sibling):
        for name, b in zip(names, from_sibling):
            a = self.mine[name]
            if name in _COL_SHARDED:
                self.partial[name] = _split_col_shards("sum_sibling_" + name, a, b.shape[2], self.core_1, add=b, out_dtype=BF)
                continue
            flat = lambda t: t.reshape(-1, t.shape[-1])
            self.partial[name] = _pair_sum("sum_sibling_" + name, flat(a), flat(b), BF).reshape(a.shape)
        return [self.partial[n] for n in names]


def kernel(x, c, ada_w, ada_b, pre_g, post_g, rel_bias, ab_w_in, ab_w_out, s5_a_re, s5_a_im, s5_log_dt, s5_b_re, s5_b_im, s5_c_re, s5_c_im, s5_d, s5_glu_w, s5_glu_b, gdn_w_in, gdn_conv, gdn_a_log, gdn_dt_bias, gdn_norm_g, gdn_w_out, loss_target, m_ada_w, m_ada_b, m_pre_g, m_post_g, m_rel_bias, m_ab_w_in, m_ab_w_out, m_s5_a_re, m_s5_a_im, m_s5_log_dt, m_s5_b_re, m_s5_b_im, m_s5_c_re, m_s5_c_im, m_s5_d, m_s5_glu_w, m_s5_glu_b, m_gdn_w_in, m_gdn_conv, m_gdn_a_log, m_gdn_dt_bias, m_gdn_norm_g, m_gdn_w_out, v_ada_w, v_ada_b, v_pre_g, v_post_g, v_rel_bias, v_ab_w_in, v_ab_w_out, v_s5_a_re, v_s5_a_im, v_s5_log_dt, v_s5_b_re, v_s5_b_im, v_s5_c_re, v_s5_c_im, v_s5_d, v_s5_glu_w, v_s5_glu_b, v_gdn_w_in, v_gdn_conv, v_gdn_a_log, v_gdn_dt_bias, v_gdn_norm_g, v_gdn_w_out):
    w = dict(ada_w=ada_w, ada_b=ada_b, pre_g=pre_g, post_g=post_g, rel_bias=rel_bias, ab_w_in=ab_w_in, ab_w_out=ab_w_out,
             s5_a_re=s5_a_re, s5_a_im=s5_a_im, s5_log_dt=s5_log_dt, s5_b_re=s5_b_re, s5_b_im=s5_b_im, s5_c_re=s5_c_re, s5_c_im=s5_c_im,
             s5_d=s5_d, s5_glu_w=s5_glu_w, s5_glu_b=s5_glu_b, gdn_w_in=gdn_w_in, gdn_conv=gdn_conv, gdn_a_log=gdn_a_log,
             gdn_dt_bias=gdn_dt_bias, gdn_norm_g=gdn_norm_g, gdn_w_out=gdn_w_out)
    m = dict(ada_w=m_ada_w, ada_b=m_ada_b, pre_g=m_pre_g, post_g=m_post_g, rel_bias=m_rel_bias, ab_w_in=m_ab_w_in, ab_w_out=m_ab_w_out,
             s5_a_re=m_s5_a_re, s5_a_im=m_s5_a_im, s5_log_dt=m_s5_log_dt, s5_b_re=m_s5_b_re, s5_b_im=m_s5_b_im, s5_c_re=m_s5_c_re,
             s5_c_im=m_s5_c_im, s5_d=m_s5_d, s5_glu_w=m_s5_glu_w, s5_glu_b=m_s5_glu_b, gdn_w_in=m_gdn_w_in, gdn_conv=m_gdn_conv,
             gdn_a_log=m_gdn_a_log, gdn_dt_bias=m_gdn_dt_bias, gdn_norm_g=m_gdn_norm_g, gdn_w_out=m_gdn_w_out)
    v = dict(ada_w=v_ada_w, ada_b=v_ada_b, pre_g=v_pre_g, post_g=v_post_g, rel_bias=v_rel_bias, ab_w_in=v_ab_w_in, ab_w_out=v_ab_w_out,
             s5_a_re=v_s5_a_re, s5_a_im=v_s5_a_im, s5_log_dt=v_s5_log_dt, s5_b_re=v_s5_b_re, s5_b_im=v_s5_b_im, s5_c_re=v_s5_c_re,
             s5_c_im=v_s5_c_im, s5_d=v_s5_d, s5_glu_w=v_s5_glu_w, s5_glu_b=v_s5_glu_b, gdn_w_in=v_gdn_w_in, gdn_conv=v_gdn_conv,
             gdn_a_log=v_gdn_a_log, gdn_dt_bias=v_gdn_dt_bias, gdn_norm_g=v_gdn_norm_g, gdn_w_out=v_gdn_w_out)
    ix, iy, ic = _place()
    me = 4 * ix + 2 * iy + ic
    chip = 2 * ix + iy
    n_cols = ada_w.shape[2]

    mine_first = _pack([c, gdn_conv], 32)
    first = _own_slot(_all_gather8("gather_c_conv", mine_first), mine_first, me)
    c_all = first[:, 0:8].reshape(8, D_MODEL)
    conv_full = first[0::2, 8:32].reshape(4, C_CONV, n_cols).transpose(1, 0, 2).reshape(C_CONV, 4 * n_cols)
    mine_mod = _mod_local(c_all, ada_w)
    modl = _own_slot(_all_gather8("gather_mod", mine_mod), mine_mod, me)
    mod = lax.dynamic_index_in_dim(modl[0::2], me, axis=2, keepdims=False)
    mod = mod.transpose(1, 0, 2).reshape(2, 4 * n_cols) + ada_b

    comm = _WeightExchanges({name: w[name][0] for name in _SHARDED}, ic, chip)
    wd = {name: w[name] for name in _SMALL if name != "ada_b"}
    wd = {k: (a if k in ("pre_g", "post_g", "rel_bias") else a[0]) for k, a in wd.items()}
    wd["gdn_conv"] = conv_full
    wd.update(comm.first_weights())

    loss_local, grad_x, grads, dmod = _local_step(x[0], loss_target[0], mod, wd, comm)

    small_shapes = [w[name].shape for name in _SMALL] + [(C_CONV, 4 * n_cols)]
    small_rows = -(-sum(_rows128(int(np.prod(s))) for s in small_shapes) // 64) * 64
    per_dev, dmod_rows = small_rows // 8, _rows128(2 * 3 * D_MODEL)
    partial = _pack([dmod] + [grads[name] for name in _SMALL[1:]] + [grads["gdn_conv"]], small_rows)
    to_all = jnp.concatenate([partial[:dmod_rows], jnp.full((8, 128), loss_local, F32)], axis=0)
    outbound = jnp.concatenate([partial.reshape(8, per_dev, 128), jnp.broadcast_to(to_all[None], (8,) + to_all.shape)], axis=1)
    inbound = _own_slot(_all_to_all8("reduce_small_grads", outbound), lax.dynamic_index_in_dim(outbound, me, 0, keepdims=False), me)
    loss = functools.reduce(lambda a, b: a + b, [inbound[d, per_dev + dmod_rows, 0] for d in range(8)])
    my_rows = _slot_sum("sum_small_grads", inbound[:, :per_dev])
    g_small = _own_slot(_all_gather8("gather_small_grads", my_rows), my_rows, me).reshape(small_rows, 128)
    g_list = _unpack(g_small, small_shapes)
    out_g, out_d, out_m, out_v = {}, {}, {}, {}

    def update(name, g2d):
        shp = w[name].shape
        two_d = lambda a: a.reshape(-1, shp[-1])
        d_, m_, v_ = _adamw("adamw_" + name, two_d(w[name]), g2d, two_d(m[name]), two_d(v[name]))
        out_g[name], out_d[name], out_m[name], out_v[name] = (a.reshape(shp) for a in (g2d, d_, m_, v_))

    small = list(_SMALL) + ["gdn_conv"]
    small_g = [g.reshape(-1, g.shape[-1]) for g in g_list[:-1]] + [lax.dynamic_slice_in_dim(g_list[-1], chip * n_cols, n_cols, axis=1)]
    two_d = lambda a: a.reshape(-1, a.shape[-1])
    results = _adamw_many("adamw_small", [two_d(w[n]) for n in small], small_g, [two_d(m[n]) for n in small], [two_d(v[n]) for n in small])
    for name, g2d, (d_, m_, v_) in zip(small, small_g, results):
        out_g[name], out_d[name], out_m[name], out_v[name] = (a.reshape(w[name].shape) for a in (g2d, d_, m_, v_))

    dmod_all = inbound[:, per_dev:per_dev + dmod_rows].reshape(8, 2, 4, n_cols)
    dmod_cols = lax.dynamic_index_in_dim(dmod_all, chip, axis=2, keepdims=False).transpose(1, 0, 2)
    update("ada_w", _ada_w_grad(c_all, dmod_cols).reshape(-1, n_cols))

    from_sibling = _sibling_exchange("reduce_sibling", comm.split_halves({"ab_w_in": grads["ab_w_in"]}))
    comm.received["ab_w_in"] = _chip_exchange("reduce_chips", comm.chip_partials(["ab_w_in"], from_sibling), True)[0]
    chip_1 = jnp.reshape(chip, (1,)).astype(jnp.int32)
    core_1 = jnp.reshape(ic, (1,)).astype(jnp.int32)
    reduced = [_chip_sum("sum_chips_" + name, comm.received[name], comm.partial[name], chip_1) for name in _SHARDED]
    for name, g_mine, g_sib in zip(_SHARDED, reduced, _sibling_exchange("reduce_share", reduced)):
        out_g[name], out_d[name], out_m[name], out_v[name] = _adamw_halves(
            "adamw_" + name, w[name], g_mine, g_sib, m[name], v[name], core_1)

    return (loss, grad_x[None], *[out_g[n] for n in _WEIGHTS], *[out_d[n] for n in _WEIGHTS],
            *[out_m[n] for n in _WEIGHTS], *[out_v[n] for n in _WEIGHTS])
```

```python
import functools
import math

import numpy as np
import jax
import jax.numpy as jnp
from jax import lax
from jax.experimental import pallas as pl
from jax.experimental.pallas import tpu as pltpu

F32 = jnp.float32
BF = jnp.bfloat16
HI = lax.Precision.HIGHEST
MESH = pl.DeviceIdType.MESH

D_MODEL = 1024
EPS = 1e-6
A_HEADS, A_HD, A_WIDTH, A_BLOCK = 8, 64, 512, 128
DILATIONS = (1, 4, 16)
N_KEYS = 128
REL_BUCKETS, REL_MAX_DIST = 32, 2048
B_WIDTH, B_GROUP, B_GROUPS, B_STATE = 512, 16, 32, 64
S5_LANES = 512
S5_TILES = 4
C_HEADS, C_DK, C_CHUNK, C_CONV = 8, 128, 64, 4
QKV = 3072
C_IN = QKV + 1024 + 2 * C_HEADS
C_IN_PAD = 4224
TM = 256
VMEM_LIMIT_BYTES = 56 * 1024 * 1024
ADAM_LR, ADAM_B1, ADAM_B2, ADAM_EPS, ADAM_WD, ADAM_STEP = 0.001, 0.9, 0.999, 1e-08, 0.01, 10
NEG = float(np.finfo(np.float32).min)


def _cp(*sem):
    return pltpu.CompilerParams(dimension_semantics=sem, vmem_limit_bytes=VMEM_LIMIT_BYTES)


def _bdot(a, b):
    return jnp.dot(a.astype(BF), b.astype(BF), preferred_element_type=F32)


def _bdot_nt(a, b):
    return lax.dot_general(a.astype(BF), b.astype(BF), (((1,), (1,)), ((), ())), preferred_element_type=F32)


def _bdot_tn(a, b):
    return lax.dot_general(a.astype(BF), b.astype(BF), (((0,), (0,)), ((), ())), preferred_element_type=F32)


def _hdot(a, b):
    return jnp.dot(a, b, precision=HI, preferred_element_type=F32)


def _bein(eq, a, b):
    return jnp.einsum(eq, a.astype(BF), b.astype(BF), preferred_element_type=F32)


def _row(tm, n):
    return pl.BlockSpec((tm, n), lambda i: (i, 0))


def _fix(shape):
    return pl.BlockSpec(shape, lambda i: (0,) * len(shape))


def _sds(*shape, dtype=F32):
    return jax.ShapeDtypeStruct(shape, dtype)


def _acc(ref, val):
    ref[...] += val


def _zero_at_first(refs, axis=0):
    @pl.when(pl.program_id(axis) == 0)
    def _():
        for r in refs:
            r[...] = jnp.zeros_like(r)


def _rms(x):
    return x * lax.rsqrt(jnp.mean(x * x, axis=-1, keepdims=True) + EPS)


def _pre_mod(x, g, scale, shift):
    return (_rms(x) * g) * (1.0 + scale) + shift


def _post_res(y, x, post_g, gate):
    return x + gate * (_rms(y) * post_g)


def _merge_gate(o1, o2, o3, l1, l2, l3, ga):
    m = jnp.maximum(jnp.maximum(l1, l2), l3)
    e1, e2, e3 = jnp.exp(l1 - m), jnp.exp(l2 - m), jnp.exp(l3 - m)
    inv = 1.0 / (e1 + e2 + e3)
    return ((e1 * inv) * o1 + (e2 * inv) * o2 + (e3 * inv) * o3) * jax.nn.silu(ga)


def _s5_gelu(ypre, u, d_skip):
    return jax.nn.gelu(ypre + d_skip * u)


def _s5_glu(yb, gl, gb):
    return yb * jax.nn.sigmoid(gl) * jax.nn.silu(gb)


def _l0_front(x, pre_g, scale, shift, w_in):
    s_len = x.shape[0]

    def body(x_ref, g_ref, sc_ref, sh_ref, w_ref, *out_refs):
        qkv_refs, (u_ref, ga_ref, gb_ref, h_ref) = out_refs[:9], out_refs[9:]
        hb = _pre_mod(x_ref[...], g_ref[...], sc_ref[...], sh_ref[...]).astype(BF)
        h_ref[...] = hb
        z = jnp.dot(hb, w_ref[...], preferred_element_type=F32)
        for a in range(3):
            piece = z[:, a * 512:(a + 1) * 512]
            for i, d in enumerate(DILATIONS):
                qkv_refs[3 * a + i][...] = _to_res(piece, d).astype(BF)
        u_ref[...] = z[:, 1536:2048]
        ga_ref[...] = z[:, 2048:2560]
        gb_ref[...] = z[:, 2560:3072]

    vec = _fix((1, D_MODEL))
    return pl.pallas_call(
        body, name="l0_front", grid=(s_len // TM,),
        in_specs=[_row(TM, D_MODEL), vec, vec, vec, _fix((D_MODEL, 3072))],
        out_specs=[_res_spec(d) for d in DILATIONS] * 3 + [_row(TM, 512)] * 3 + [_row(TM, D_MODEL)],
        out_shape=[_sds(*_res_shape(s_len, d), dtype=BF) for d in DILATIONS] * 3 + [_sds(s_len, 512)] * 3 + [_sds(s_len, D_MODEL, dtype=BF)],
        compiler_params=_cp("arbitrary"),
    )(x, pre_g, scale, shift, w_in)


def _front_bwd(name, x, pre_g, scale, shift, w_in, dres, parts, widths):
    s_len = x.shape[0]
    n_in = sum(len(p) for p in parts)
    n_cols = sum(widths)

    def body(*refs):
        x_ref, g_ref, sc_ref, sh_ref, w_ref, dres_ref = refs[:6]
        part_refs = refs[6:6 + n_in]
        dz_ref, dx_ref, dg_ref, dsc_ref, dsh_ref = refs[6 + n_in:]
        _zero_at_first([dg_ref, dsc_ref, dsh_ref])
        _, vjp = jax.vjp(_pre_mod, x_ref[...], g_ref[...], sc_ref[...], sh_ref[...])
        dh = jnp.zeros((TM, D_MODEL), F32)
        col, at = 0, 0
        for grp, width in zip(parts, widths):
            tile = lambda r: _from_res(r[...]) if len(r.shape) == 3 else r[...]
            dz = tile(part_refs[at])
            for r in part_refs[at + 1:at + len(grp)]:
                dz = dz + tile(r)
            at += len(grp)
            dzb = dz.astype(BF)
            dz_ref[:, col:col + width] = dzb
            dh = dh + lax.dot_general(dzb, w_ref[:, col:col + width], (((1,), (1,)), ((), ())), preferred_element_type=F32)
            col += width
        dx, dg, dsc, dsh = vjp(dh)
        dx_ref[...] = dx + dres_ref[...]
        _acc(dg_ref, dg)
        _acc(dsc_ref, dsc)
        _acc(dsh_ref, dsh)

    vec = _fix((1, D_MODEL))
    flat = [a for p in parts for a in p]
    return pl.pallas_call(
        body, name=name, grid=(s_len // TM,),
        in_specs=[_row(TM, D_MODEL), vec, vec, vec, _fix((D_MODEL, n_cols)), _row(TM, D_MODEL)]
        + [_res_spec(a.shape[0], a.shape[2]) if a.ndim == 3 else _row(TM, a.shape[1]) for a in flat],
        out_specs=[_row(TM, n_cols), _row(TM, D_MODEL), vec, vec, vec],
        out_shape=[_sds(s_len, n_cols, dtype=BF), _sds(s_len, D_MODEL), _sds(1, D_MODEL), _sds(1, D_MODEL), _sds(1, D_MODEL)],
        compiler_params=_cp("arbitrary"),
    )(x, pre_g, scale, shift, w_in, dres, *flat)


def _matmul_tn(name, a, b, tn):
    s_len, k_dim = a.shape
    n_dim = b.shape[1]
    ts = 512

    def body(a_ref, b_ref, o_ref):
        _zero_at_first([o_ref], axis=1)
        o_ref[...] += lax.dot_general(a_ref[...], b_ref[...], (((0,), (0,)), ((), ())), preferred_element_type=F32)

    return pl.pallas_call(
        body, name=name, grid=(n_dim // tn, s_len // ts),
        in_specs=[pl.BlockSpec((ts, k_dim), lambda j, i: (i, 0)), pl.BlockSpec((ts, tn), lambda j, i: (i, j))],
        out_specs=pl.BlockSpec((k_dim, tn), lambda j, i: (0, j)),
        out_shape=_sds(k_dim, n_dim),
        compiler_params=_cp("arbitrary", "arbitrary"),
    )(a, b)


def _t5_bucket_np(dist):
    dist = np.maximum(dist, 0)
    max_exact = REL_BUCKETS // 2
    large = max_exact + (np.log(np.maximum(dist, 1) / max_exact)
                         / math.log(REL_MAX_DIST / max_exact) * (REL_BUCKETS - max_exact)).astype(np.int32)
    large = np.minimum(large, REL_BUCKETS - 1)
    return np.where(dist < max_exact, dist, large).astype(np.int32)


def _to_res(z, dil):
    if dil == 1:
        return z[None]
    return jnp.swapaxes(z.reshape(z.shape[0] // dil, dil, z.shape[1]), 0, 1)


def _from_res(z):
    if z.shape[0] == 1:
        return z[0]
    return jnp.swapaxes(z, 0, 1).reshape(z.shape[0] * z.shape[1], z.shape[2])


def _res_shape(s_len, dil, width=A_WIDTH):
    return (dil, s_len // dil, width)


def _res_spec(dil, width=A_WIDTH):
    return pl.BlockSpec((dil, TM // dil, width), lambda i: (0, i, 0))


def _bucket_table():
    qi = np.arange(A_BLOCK)[:, None]
    kj = np.arange(2 * A_BLOCK)[None, :]
    return np.stack([_t5_bucket_np((qi + A_BLOCK - kj) * d) for d in DILATIONS], 0)


def _attn_mask(first):
    qi = lax.broadcasted_iota(jnp.int32, (A_BLOCK, 2 * A_BLOCK), 0)
    kj = lax.broadcasted_iota(jnp.int32, (A_BLOCK, 2 * A_BLOCK), 1)
    rel = qi + A_BLOCK - kj
    return (rel >= 0) & (rel <= N_KEYS) & (jnp.logical_not(first) | (kj >= A_BLOCK))


def _attn_specs(nb, rev):
    per = 2 if nb % 2 == 0 else 1
    steps = nb // per
    n_of = (lambda i: steps - 1 - i) if rev else (lambda i: i)
    cur = pl.BlockSpec((None, per * A_BLOCK, A_WIDTH), lambda r, i: (r, n_of(i), 0))
    prev = pl.BlockSpec((None, A_BLOCK, A_WIDTH), lambda r, i: (r, jnp.maximum(per * n_of(i) - 1, 0), 0))
    bias = pl.BlockSpec((A_HEADS, A_BLOCK, 2 * A_BLOCK), lambda r, i: (0, 0, 0))
    return per, steps, cur, prev, bias


def _attn_fwd(q, k, v, bias, exchange=None):
    dil, t_len, _ = q.shape
    per, steps, cur, prev, bias_spec = _attn_specs(t_len // A_BLOCK, False)
    scale = A_HD ** -0.5

    def body(q_ref, kp_ref, kc_ref, vp_ref, vc_ref, b_ref, o_ref, l_ref):
        lane = lax.broadcasted_iota(jnp.int32, (1, 128), 1)
        for sub in range(per):
            rows = slice(sub * A_BLOCK, (sub + 1) * A_BLOCK)
            before = slice((sub - 1) * A_BLOCK, sub * A_BLOCK)
            mask = _attn_mask((pl.program_id(1) == 0) if sub == 0 else False)
            for hp in range(A_HEADS // 2):
                sl = slice(hp * 128, (hp + 1) * 128)
                qp = q_ref[rows, sl]
                kw = jnp.concatenate([kp_ref[:, sl] if sub == 0 else kc_ref[before, sl], kc_ref[rows, sl]], axis=0).astype(BF)
                vw = jnp.concatenate([vp_ref[:, sl] if sub == 0 else vc_ref[before, sl], vc_ref[rows, sl]], axis=0).astype(BF)
                outs, lses = [], []
                for j in range(2):
                    hm = (lane < 64) if j == 0 else (lane >= 64)
                    s = _bdot_nt(jnp.where(hm, qp, 0.0), kw) * scale
                    s = jnp.where(mask, s + b_ref[2 * hp + j], NEG)
                    m = jnp.max(s, axis=-1, keepdims=True)
                    p = jnp.exp(s - m)
                    den = jnp.sum(p, axis=-1, keepdims=True)
                    outs.append(_bdot(p, vw) / den)
                    lses.append(m + jnp.log(den))
                hm0 = lane < 64
                o_ref[rows, sl] = jnp.where(hm0, outs[0], outs[1])
                l_ref[rows, sl] = jnp.where(hm0, lses[0], lses[1])

    return _call_with_exchange(body, f"attn_fwd_d{dil}", (dil, steps), [cur, prev, cur, prev, cur, bias_spec], [cur, cur],
                               [_sds(dil, t_len, A_WIDTH)] * 2, [], (q, k, k, v, v, bias), exchange)


def _attn_bwd(q, k, v, bias, o, l, do, dl, exchange=None):
    dil, t_len, _ = q.shape
    per, steps, cur, prev, bias_spec = _attn_specs(t_len // A_BLOCK, True)
    scale = A_HD ** -0.5

    def body(q_ref, kp_ref, kc_ref, vp_ref, vc_ref, b_ref, o_ref, l_ref, do_ref, dl_ref,
             dq_ref, dk_ref, dv_ref, db_ref, ck_ref, cv_ref):
        _zero_at_first([ck_ref, cv_ref], axis=1)

        @pl.when((pl.program_id(0) == 0) & (pl.program_id(1) == 0))
        def _():
            db_ref[...] = jnp.zeros_like(db_ref)

        lane = lax.broadcasted_iota(jnp.int32, (1, 128), 1)
        for hp in range(A_HEADS // 2):
            sl = slice(hp * 128, (hp + 1) * 128)
            to_prev_k, to_prev_v = ck_ref[:, sl], cv_ref[:, sl]
            for sub in range(per - 1, -1, -1):
                rows = slice(sub * A_BLOCK, (sub + 1) * A_BLOCK)
                before = slice((sub - 1) * A_BLOCK, sub * A_BLOCK)
                mask = _attn_mask((pl.program_id(1) == steps - 1) if sub == 0 else False)
                qp = q_ref[rows, sl]
                kw = jnp.concatenate([kp_ref[:, sl] if sub == 0 else kc_ref[before, sl], kc_ref[rows, sl]], axis=0).astype(BF)
                vw = jnp.concatenate([vp_ref[:, sl] if sub == 0 else vc_ref[before, sl], vc_ref[rows, sl]], axis=0).astype(BF)
                op, lp, dop, dlp = o_ref[rows, sl], l_ref[rows, sl], do_ref[rows, sl], dl_ref[rows, sl]
                dq_acc = jnp.zeros((A_BLOCK, 128), F32)
                dk_acc = jnp.zeros((2 * A_BLOCK, 128), F32)
                dv_acc = jnp.zeros((2 * A_BLOCK, 128), F32)
                for j in range(2):
                    hm = (lane < 64) if j == 0 else (lane >= 64)
                    qm = jnp.where(hm, qp, 0.0)
                    s = _bdot_nt(qm, kw) * scale
                    s = jnp.where(mask, s + b_ref[2 * hp + j], NEG)
                    lse = jnp.max(jnp.where(hm, lp, NEG), axis=-1, keepdims=True)
                    p = jnp.exp(s - lse)
                    do_h = jnp.where(hm, dop, 0.0)
                    dd = jnp.sum(do_h * op, axis=-1, keepdims=True)
                    dlse = jnp.sum(jnp.where(hm, dlp, 0.0), axis=-1, keepdims=True)
                    ds = p * (_bdot_nt(do_h, vw) - dd + dlse)
                    dv_acc = dv_acc + _bdot_tn(p, do_h)
                    dq_acc = dq_acc + jnp.where(hm, _bdot(ds, kw), 0.0) * scale
                    dk_acc = dk_acc + _bdot_tn(ds, qm) * scale
                    db_ref[2 * hp + j] += ds
                dq_ref[rows, sl] = dq_acc
                dk_ref[rows, sl] = dk_acc[A_BLOCK:] + to_prev_k
                dv_ref[rows, sl] = dv_acc[A_BLOCK:] + to_prev_v
                to_prev_k, to_prev_v = dk_acc[:A_BLOCK], dv_acc[:A_BLOCK]
            ck_ref[:, sl] = to_prev_k
            cv_ref[:, sl] = to_prev_v

    return _call_with_exchange(
        body, f"attn_bwd_d{dil}", (dil, steps), [cur, prev, cur, prev, cur, bias_spec, cur, cur, cur, cur],
        [cur, cur, cur, bias_spec], [_sds(dil, t_len, A_WIDTH)] * 3 + [_sds(A_HEADS, A_BLOCK, 2 * A_BLOCK)],
        [pltpu.VMEM((A_BLOCK, A_WIDTH), F32)] * 2, (q, k, k, v, v, bias, o, l, do, dl), exchange)


def _attn_bias(rel_bias, table):
    def body(rb_ref, t_ref, *o_refs):
        for c in range(3):
            t = t_ref[c]
            acc = [jnp.zeros((A_BLOCK, 2 * A_BLOCK), F32) for _ in range(A_HEADS)]
            for b in range(REL_BUCKETS):
                hit = t == b
                acc = [jnp.where(hit, rb_ref[b, h], acc[h]) for h in range(A_HEADS)]
            for h in range(A_HEADS):
                o_refs[c][h] = acc[h]

    return pl.pallas_call(body, name="attn_bias", out_shape=[_sds(A_HEADS, A_BLOCK, 2 * A_BLOCK)] * 3,
                          in_specs=[pl.BlockSpec(memory_space=pltpu.SMEM), pl.BlockSpec(memory_space=pltpu.VMEM)],
                          compiler_params=pltpu.CompilerParams(vmem_limit_bytes=VMEM_LIMIT_BYTES))(rel_bias, table)


def _rel_bias_grad(dbs, idx_rows):
    n = A_BLOCK * 2 * A_BLOCK

    def body(d0_ref, d1_ref, d2_ref, idx_ref, o_ref):
        bucket = lax.broadcasted_iota(jnp.int32, (REL_BUCKETS, n), 0).astype(F32)
        acc = jnp.zeros((A_HEADS, REL_BUCKETS), F32)
        for c, db_ref in enumerate((d0_ref, d1_ref, d2_ref)):
            onehot = (idx_ref[c:c + 1, :] == bucket).astype(F32)
            acc = acc + lax.dot_general(db_ref[...], onehot, (((1,), (1,)), ((), ())), precision=HI, preferred_element_type=F32)
        o_ref[...] = acc

    return pl.pallas_call(body, name="rel_bias_grad", out_shape=_sds(A_HEADS, REL_BUCKETS),
                          compiler_params=pltpu.CompilerParams(vmem_limit_bytes=VMEM_LIMIT_BYTES))(
                              *[d.reshape(A_HEADS, n) for d in dbs], idx_rows)


def _s5_param_fn(a_re, a_im, log_dt, bt_re, bt_im):
    dt = jnp.exp(log_dt)
    mag = jnp.exp(dt * a_re)
    abar_r, abar_i = mag * jnp.cos(dt * a_im), mag * jnp.sin(dt * a_im)
    den = a_re * a_re + a_im * a_im
    fr = ((abar_r - 1.0) * a_re + abar_i * a_im) / den
    fi = (abar_i * a_re - (abar_r - 1.0) * a_im) / den
    row = lax.broadcasted_iota(jnp.int32, (B_WIDTH, B_GROUPS), 0)
    grp = lax.broadcasted_iota(jnp.int32, (B_WIDTH, B_GROUPS), 1)
    expand = ((row // B_GROUP) == grp).astype(F32)
    fr_e, fi_e = _hdot(expand, fr), _hdot(expand, fi)
    return abar_r, abar_i, fr_e * bt_re - fi_e * bt_im, fr_e * bt_im + fi_e * bt_re


def _s5_params(a_re, a_im, log_dt, bt_re, bt_im):
    def body(ar, ai, ld, br, bi, o1, o2, o3, o4):
        o1[...], o2[...], o3[...], o4[...] = _s5_param_fn(ar[...], ai[...], ld[...], br[...], bi[...])

    return pl.pallas_call(body, name="s5_params",
                          out_shape=[_sds(B_GROUPS, B_STATE)] * 2 + [_sds(B_WIDTH, B_STATE)] * 2)(a_re, a_im, log_dt, bt_re, bt_im)


def _s5_params_bwd(a_re, a_im, log_dt, bt_re, bt_im, d1, d2, d3, d4):
    def body(ar, ai, ld, br, bi, c1, c2, c3, c4, o1, o2, o3, o4, o5):
        _, vjp = jax.vjp(_s5_param_fn, ar[...], ai[...], ld[...], br[...], bi[...])
        o1[...], o2[...], o3[...], o4[...], o5[...] = vjp((c1[...], c2[...], c3[...], c4[...]))

    return pl.pallas_call(body, name="s5_params_bwd",
                          out_shape=[_sds(B_GROUPS, B_STATE)] * 2 + [_sds(B_GROUPS, 1)] + [_sds(B_WIDTH, B_STATE)] * 2,
                          )(a_re, a_im, log_dt, bt_re, bt_im, d1, d2, d3, d4)


def _pick_row(x, r):
    rows = lax.broadcasted_iota(jnp.int32, x.shape, 0)
    return jnp.sum(jnp.where(rows == r, x, 0.0), axis=0, keepdims=True)


S5_SEG = 8
S5_STEPS = 32
S5_WIDTH = S5_TILES * S5_LANES


def _seg_rows(block):
    return jnp.swapaxes(block, 0, 1).reshape(block.shape[1] * S5_SEG, block.shape[2])


def _seg_block(rows):
    return jnp.swapaxes(rows.reshape(rows.shape[0] // S5_SEG, S5_SEG, rows.shape[1]), 0, 1)


def _seq_specs(n_i, rev):
    at = (lambda i: n_i - 1 - i) if rev else (lambda i: i)
    seg = pl.BlockSpec((S5_SEG, S5_STEPS, B_WIDTH), lambda i: (0, at(i), 0))
    x_spec = pl.BlockSpec((S5_SEG * S5_STEPS, S5_WIDTH), lambda i: (at(i), 0))
    return seg, x_spec, _fix((S5_TILES, 128, S5_LANES)), _fix((S5_TILES, S5_LANES, 128)), _fix((1, S5_WIDTH)), _fix((S5_SEG, S5_WIDTH))


def _tile_dots(dot, lhs, w_ref, lhs_width):
    return jnp.concatenate([dot(lhs[:, t * lhs_width:(t + 1) * lhs_width], w_ref[t]) for t in range(S5_TILES)], axis=1)


def _s5_entries(name, end_r, end_i, abr, abi, steps, reverse):
    def body(er_ref, ei_ref, ar_ref, ai_ref, or_ref, oi_ref):
        pr, pi_ = ar_ref[...], ai_ref[...]
        for _ in range(int(math.log2(steps))):
            pr, pi_ = pr * pr - pi_ * pi_, 2.0 * pr * pi_
        er, ei = er_ref[...], ei_ref[...]
        rows = lax.broadcasted_iota(jnp.int32, er.shape, 0)
        cr, ci = jnp.zeros_like(pr), jnp.zeros_like(pr)
        out_r, out_i = jnp.zeros_like(er), jnp.zeros_like(er)
        for g in (range(S5_SEG - 2, -1, -1) if reverse else range(1, S5_SEG)):
            src = g + 1 if reverse else g - 1
            cr, ci = _pick_row(er, src) + pr * cr - pi_ * ci, _pick_row(ei, src) + pr * ci + pi_ * cr
            out_r, out_i = jnp.where(rows == g, cr, out_r), jnp.where(rows == g, ci, out_i)
        or_ref[...] = out_r
        oi_ref[...] = out_i

    return pl.pallas_call(body, name=name, out_shape=[_sds(*end_r.shape)] * 2)(end_r, end_i, abr, abi)


def _s5_seq_fwd(u, btr, bti, ctr, cti, abr, abi, entry, store, exchange=None):
    s_len = u.shape[0]
    seg_len = s_len // S5_SEG
    n_i = seg_len // S5_STEPS
    rows = S5_SEG * S5_STEPS

    def body(u_ref, btr_ref, bti_ref, ctr_ref, cti_ref, ar_ref, ai_ref, er_ref, ei_ref, *rest):
        if store:
            xr_ref, xi_ref, y_ref, endr_ref, endi_ref, sr_ref, si_ref = rest
        else:
            endr_ref, endi_ref, sr_ref, si_ref = rest
        i = pl.program_id(0)

        @pl.when(i == 0)
        def _():
            sr_ref[...] = er_ref[...]
            si_ref[...] = ei_ref[...]

        ar = jnp.broadcast_to(ar_ref[...], (S5_SEG, S5_WIDTH))
        ai = jnp.broadcast_to(ai_ref[...], (S5_SEG, S5_WIDTH))
        ub = _seg_rows(u_ref[...])
        br, bi = _tile_dots(_bdot, ub, btr_ref, 128), _tile_dots(_bdot, ub, bti_ref, 128)
        sr, si = sr_ref[...], si_ref[...]
        for s in range(S5_STEPS):
            at = slice(S5_SEG * s, S5_SEG * (s + 1))
            sr, si = ar * sr - ai * si + br[at], ar * si + ai * sr + bi[at]
            if store:
                xr_ref[at, :] = sr
                xi_ref[at, :] = si
        sr_ref[...] = sr
        si_ref[...] = si
        if store:
            y_ref[...] = _seg_block(_tile_dots(_bdot, xr_ref[...], ctr_ref, S5_LANES) - _tile_dots(_bdot, xi_ref[...], cti_ref, S5_LANES))

        @pl.when(i == n_i - 1)
        def _():
            endr_ref[...] = sr
            endi_ref[...] = si

    seg, x_spec, b_spec, c_spec, a_spec, e_spec = _seq_specs(n_i, False)
    ends = [_sds(S5_SEG, S5_WIDTH)] * 2
    full = [_sds(s_len, S5_WIDTH)] * 2 + [_sds(S5_SEG, seg_len, B_WIDTH)] if store else []
    return _call_with_exchange(
        body, "s5_scan_fwd" if store else "s5_ends_fwd", (n_i,),
        [seg, b_spec, b_spec, c_spec, c_spec, a_spec, a_spec, e_spec, e_spec],
        ([x_spec, x_spec, seg] if store else []) + [e_spec, e_spec], full + ends,
        [pltpu.VMEM((S5_SEG, S5_WIDTH), F32)] * 2,
        (u.reshape(S5_SEG, seg_len, B_WIDTH), btr, bti, ctr, cti, abr, abi, *entry), exchange)


def _s5_seq_bwd(dy, xr, xi, u, btr, bti, ctr, cti, abr, abi, g_entry, x_entry, full, exchange=None):
    s_len = dy.shape[0]
    seg_len = s_len // S5_SEG
    n_i = seg_len // S5_STEPS
    rows = S5_SEG * S5_STEPS

    def body(*refs):
        if full:
            (dy_ref, btr_ref, bti_ref, ctr_ref, cti_ref, ar_ref, ai_ref, ger_ref, gei_ref,
             xr_ref, xi_ref, xrp_ref, xip_ref, xer_ref, xei_ref, u_ref,
             du_ref, dbtr_ref, dbti_ref, dctr_ref, dcti_ref, dar_ref, dai_ref, str_ref, sti_ref,
             sr_ref, si_ref, gr_s, gi_s) = refs
        else:
            (dy_ref, btr_ref, bti_ref, ctr_ref, cti_ref, ar_ref, ai_ref, ger_ref, gei_ref, str_ref, sti_ref, sr_ref, si_ref) = refs
        i = pl.program_id(0)

        @pl.when(i == 0)
        def _():
            sr_ref[...] = ger_ref[...]
            si_ref[...] = gei_ref[...]
            if full:
                for r in (dbtr_ref, dbti_ref, dctr_ref, dcti_ref, dar_ref, dai_ref):
                    r[...] = jnp.zeros_like(r)

        ar = jnp.broadcast_to(ar_ref[...], (S5_SEG, S5_WIDTH))
        ai = -jnp.broadcast_to(ai_ref[...], (S5_SEG, S5_WIDTH))
        dyb = _seg_rows(dy_ref[...])
        gr, gi = _tile_dots(_bdot_nt, dyb, ctr_ref, 128), -_tile_dots(_bdot_nt, dyb, cti_ref, 128)
        sr, si = sr_ref[...], si_ref[...]
        for s in range(S5_STEPS - 1, -1, -1):
            at = slice(S5_SEG * s, S5_SEG * (s + 1))
            sr, si = ar * sr - ai * si + gr[at], ar * si + ai * sr + gi[at]
            if full:
                gr_s[at, :] = sr
                gi_s[at, :] = si
        sr_ref[...] = sr
        si_ref[...] = si

        @pl.when(i == n_i - 1)
        def _():
            str_ref[...] = sr
            sti_ref[...] = si

        if full:
            g_r, g_i = gr_s[...], gi_s[...]
            du_ref[...] = _seg_block(_tile_dots(_bdot_nt, g_r, btr_ref, S5_LANES) + _tile_dots(_bdot_nt, g_i, bti_ref, S5_LANES))
            ub = _seg_rows(u_ref[...])
            xr_b, xi_b = xr_ref[...], xi_ref[...]
            for t in range(S5_TILES):
                lanes, cols = slice(t * S5_LANES, (t + 1) * S5_LANES), slice(t * 128, (t + 1) * 128)
                dbtr_ref[t] += _bdot_tn(ub[:, cols], g_r[:, lanes])
                dbti_ref[t] += _bdot_tn(ub[:, cols], g_i[:, lanes])
                dctr_ref[t] += _bdot_tn(xr_b[:, lanes], dyb[:, cols])
                dcti_ref[t] -= _bdot_tn(xi_b[:, lanes], dyb[:, cols])
            first = i == n_i - 1
            xpr = jnp.concatenate([jnp.where(first, xer_ref[...], xrp_ref[...]), xr_b[:rows - S5_SEG]], axis=0)
            xpi = jnp.concatenate([jnp.where(first, xei_ref[...], xip_ref[...]), xi_b[:rows - S5_SEG]], axis=0)
            dar_ref[...] += jnp.sum(g_r * xpr + g_i * xpi, axis=0, keepdims=True)
            dai_ref[...] += jnp.sum(g_i * xpr - g_r * xpi, axis=0, keepdims=True)

    seg, x_spec, b_spec, c_spec, a_spec, e_spec = _seq_specs(n_i, True)
    halo = pl.BlockSpec((S5_SEG, S5_WIDTH), lambda i: (jnp.maximum((n_i - 1 - i) * S5_STEPS - 1, 0), 0))
    starts = [_sds(S5_SEG, S5_WIDTH)] * 2
    in_specs = [seg, b_spec, b_spec, c_spec, c_spec, a_spec, a_spec, e_spec, e_spec]
    args = [dy.reshape(S5_SEG, seg_len, B_WIDTH), btr, bti, ctr, cti, abr, abi, *g_entry]
    state = [pltpu.VMEM((S5_SEG, S5_WIDTH), F32)] * 2
    if not full:
        return _call_with_exchange(body, "s5_starts_bwd", (n_i,), in_specs, [e_spec, e_spec], starts, state, args, None)
    return _call_with_exchange(
        body, "s5_scan_bwd", (n_i,),
        in_specs + [x_spec, x_spec, halo, halo, e_spec, e_spec, seg],
        [seg, b_spec, b_spec, c_spec, c_spec, a_spec, a_spec, e_spec, e_spec],
        [_sds(S5_SEG, seg_len, B_WIDTH)] + [_sds(S5_TILES, 128, S5_LANES)] * 2 + [_sds(S5_TILES, S5_LANES, 128)] * 2
        + [_sds(1, S5_WIDTH)] * 2 + starts,
        state + [pltpu.VMEM((rows, S5_WIDTH), F32)] * 2,
        args + [xr, xi, xr, xi, *x_entry, u.reshape(S5_SEG, seg_len, B_WIDTH)], exchange)


def _blockdiag_b(bbar_t):
    blocks = bbar_t.reshape(S5_TILES, 8, B_GROUP, B_STATE)
    return jnp.einsum('jgmp,gh->jgmhp', blocks, jnp.eye(8, dtype=F32)).reshape(S5_TILES, 128, S5_LANES)


def _blockdiag_b_t(d):
    return jnp.einsum('jgmgp->jgmp', d.reshape(S5_TILES, 8, B_GROUP, 8, B_STATE)).reshape(B_WIDTH, B_STATE)


def _blockdiag_c(c):
    blocks = c.reshape(S5_TILES, 8, B_GROUP, B_STATE)
    return jnp.einsum('jgmp,gh->jhpgm', blocks, jnp.eye(8, dtype=F32)).reshape(S5_TILES, S5_LANES, 128)


def _blockdiag_c_t(d):
    return jnp.einsum('jgpgm->jgmp', d.reshape(S5_TILES, 8, B_STATE, 8, B_GROUP)).reshape(B_GROUPS, B_GROUP, B_STATE)


def _l0_out(os, ls, ga, gb, ypre, u, x, d_skip, glu_w, glu_b, w_out, post_g, gate, exchange=None):
    s_len = x.shape[0]

    def body(o0, o1, o2, l0, l1, l2, ga_ref, gb_ref, yp_ref, u_ref, x_ref, d_ref, gw_ref, gbias_ref, w_ref, pg_ref, gt_ref, x1_ref, y_ref):
        oa = _merge_gate(*[_from_res(r[...]) for r in (o0, o1, o2, l0, l1, l2)], ga_ref[...])
        yb = _s5_gelu(yp_ref[...], u_ref[...], d_ref[...])
        ob = _s5_glu(yb, _bdot(yb, gw_ref[...]) + gbias_ref[...], gb_ref[...])
        y = _bdot(oa, w_ref[0:512, :]) + _bdot(ob, w_ref[512:1024, :])
        y_ref[...] = y
        x1_ref[...] = _post_res(y, x_ref[...], pg_ref[...], gt_ref[...])

    vec, half = _fix((1, D_MODEL)), _fix((1, 512))
    return _call_with_exchange(
        body, "l0_out", (s_len // TM,),
        [_res_spec(d) for d in DILATIONS] * 2 + [_row(TM, 512)] * 4
        + [_row(TM, D_MODEL), half, _fix((512, 512)), half, _fix((D_MODEL, D_MODEL)), vec, vec],
        [_row(TM, D_MODEL)] * 2, [_sds(s_len, D_MODEL)] * 2, [],
        (*os, *ls, ga, gb, ypre, u, x, d_skip, glu_w, glu_b, w_out, post_g, gate), exchange)


def _l0_out_bwd(os, ls, ga, gb, ypre, u, x, y, d_skip, glu_w, glu_b, w_out, post_g, gate, dx1, exchange=None):
    s_len = x.shape[0]

    def body(o0, o1, o2, l0, l1, l2, ga_ref, gb_ref, yp_ref, u_ref, x_ref, y_ref, d_ref, gw_ref, gbias_ref, w_ref, pg_ref, gt_ref, dx1_ref,
             do0, do1, do2, dl0, dl1, dl2, dga_ref, dgb_ref, dyp_ref, du_ref, dd_ref, dgw_ref, dgbias_ref, dw_ref, dpg_ref, dgt_ref):
        _zero_at_first([dd_ref, dgw_ref, dgbias_ref, dw_ref, dpg_ref, dgt_ref])
        _, vjp2 = jax.vjp(_post_res, y_ref[...], x_ref[...], pg_ref[...], gt_ref[...])
        dy, _, dpg, dgt = vjp2(dx1_ref[...])
        _acc(dpg_ref, dpg)
        _acc(dgt_ref, dgt)
        oa, vjp_a = jax.vjp(_merge_gate, *[_from_res(r[...]) for r in (o0, o1, o2, l0, l1, l2)], ga_ref[...])
        yb, vjp_g = jax.vjp(_s5_gelu, yp_ref[...], u_ref[...], d_ref[...])
        gl = _bdot(yb, gw_ref[...]) + gbias_ref[...]
        ob, vjp_b = jax.vjp(_s5_glu, yb, gl, gb_ref[...])
        dw_ref[0:512, :] += _bdot_tn(oa, dy)
        dw_ref[512:1024, :] += _bdot_tn(ob, dy)
        d1, d2, d3, e1, e2, e3, dga = vjp_a(_bdot_nt(dy, w_ref[0:512, :]))
        for ref, val, d in zip((do0, do1, do2, dl0, dl1, dl2), (d1, d2, d3, e1, e2, e3), DILATIONS * 2):
            ref[...] = _to_res(val, d)
        dga_ref[...] = dga
        dyb, dgl, dgb = vjp_b(_bdot_nt(dy, w_ref[512:1024, :]))
        dgb_ref[...] = dgb
        dgw_ref[...] += _bdot_tn(yb, dgl)
        _acc(dgbias_ref, jnp.sum(dgl, axis=0, keepdims=True))
        dyp, du, dd = vjp_g(dyb + _bdot_nt(dgl, gw_ref[...]))
        dyp_ref[...] = dyp
        du_ref[...] = du
        _acc(dd_ref, dd)

    vec, half = _fix((1, D_MODEL)), _fix((1, 512))
    r5, r10 = _row(TM, 512), _row(TM, D_MODEL)
    res6 = [_res_spec(d) for d in DILATIONS] * 2
    return _call_with_exchange(
        body, "l0_out_bwd", (s_len // TM,),
        res6 + [r5] * 4 + [r10, r10, half, _fix((512, 512)), half, _fix((D_MODEL, D_MODEL)), vec, vec, r10],
        res6 + [r5] * 4 + [half, _fix((512, 512)), half, _fix((D_MODEL, D_MODEL)), vec, vec],
        [_sds(*_res_shape(s_len, d)) for d in DILATIONS] * 2 + [_sds(s_len, 512)] * 4
        + [_sds(1, 512), _sds(512, 512), _sds(1, 512), _sds(D_MODEL, D_MODEL), _sds(1, D_MODEL), _sds(1, D_MODEL)],
        [], (*os, *ls, ga, gb, ypre, u, x, y, d_skip, glu_w, glu_b, w_out, post_g, gate, dx1), exchange)


def _l1_front(x, pre_g, scale, shift, w_in):
    s_len = x.shape[0]

    def body(x_ref, g_ref, sc_ref, sh_ref, w_ref, raw_ref, gate_ref, ba_ref, h_ref):
        hb = _pre_mod(x_ref[...], g_ref[...], sc_ref[...], sh_ref[...]).astype(BF)
        h_ref[...] = hb
        z = jnp.dot(hb, w_ref[...], preferred_element_type=F32)
        raw_ref[...] = z[:, 0:QKV]
        gate_ref[...] = z[:, QKV:QKV + 1024]
        ba_ref[...] = z[:, QKV + 1024:C_IN_PAD]

    vec = _fix((1, D_MODEL))
    return pl.pallas_call(
        body, name="l1_front", grid=(s_len // TM,),
        in_specs=[_row(TM, D_MODEL), vec, vec, vec, _fix((D_MODEL, C_IN_PAD))],
        out_specs=[_row(TM, QKV), _row(TM, 1024), _row(TM, 128), _row(TM, D_MODEL)],
        out_shape=[_sds(s_len, QKV), _sds(s_len, 1024), _sds(s_len, 128), _sds(s_len, D_MODEL, dtype=BF)],
        compiler_params=_cp("arbitrary"),
    )(x, pre_g, scale, shift, w_in)


def _bg_fn(ba, alog_row, dtb_row):
    lane = lax.broadcasted_iota(jnp.int32, (1, 128), 1)
    g = -jnp.exp(alog_row) * jax.nn.softplus(ba + dtb_row)
    return jnp.where(lane < C_HEADS, jax.nn.sigmoid(ba), jnp.where(lane < 2 * C_HEADS, g, 0.0))


def _act_q(c):
    q = jax.nn.silu(c)
    return q * lax.rsqrt(jnp.sum(q * q, axis=-1, keepdims=True) + EPS) * (C_DK ** -0.5)


def _act_k(c):
    k = jax.nn.silu(c)
    return k * lax.rsqrt(jnp.sum(k * k, axis=-1, keepdims=True) + EPS)


def _act_of(s):
    return _act_q if s < 8 else (_act_k if s < 16 else jax.nn.silu)


def _conv_taps(prev8, tile_ref, sl, next8=None):
    rows = tile_ref.shape[0]
    head = jnp.concatenate([prev8, tile_ref[0:8, sl]], axis=0)
    tail = None if next8 is None else jnp.concatenate([tile_ref[rows - 8:rows, sl], next8], axis=0)
    taps = []
    for j in range(C_CONV):
        shift = C_CONV - 1 - j
        pieces = [head[8:] if shift == 0 else pltpu.roll(head, shift, 0)[8:], tile_ref[pl.ds(8 - shift, rows - 8), sl]]
        if tail is not None:
            pieces.append(tail[8:] if shift == 0 else pltpu.roll(tail, shift, 0)[8:])
        taps.append(jnp.concatenate(pieces, axis=0))
    return taps


def _gdn_prep(raw, ba, conv_w, alog_row, dtb_row):
    s_len = raw.shape[0]

    def body(raw_ref, halo_ref, ba_ref, w_ref, al_ref, dt_ref, qkv_ref, bg_ref):
        bg_ref[...] = _bg_fn(ba_ref[...], al_ref[...], dt_ref[...])
        has_prev = (pl.program_id(0) > 0).astype(F32)
        for s in range(24):
            sl = slice(s * 128, (s + 1) * 128)
            taps = _conv_taps(halo_ref[:, sl] * has_prev, raw_ref, sl)
            conv = w_ref[3:4, sl] * taps[3]
            for j in range(3):
                conv = conv + w_ref[j:j + 1, sl] * taps[j]
            qkv_ref[:, sl] = _act_of(s)(conv)

    halo = pl.BlockSpec((8, QKV), lambda i: (jnp.maximum(i * (TM // 8) - 1, 0), 0))
    row128 = _fix((1, 128))
    return pl.pallas_call(
        body, name="gdn_prep", grid=(s_len // TM,),
        in_specs=[_row(TM, QKV), halo, _row(TM, 128), _fix((C_CONV, QKV)), row128, row128],
        out_specs=[_row(TM, QKV), _row(TM, 128)],
        out_shape=[_sds(s_len, QKV), _sds(s_len, 128)],
        compiler_params=_cp("arbitrary"),
    )(raw, raw, ba, conv_w, alog_row, dtb_row)


def _gdn_prep_bwd(raw, ba, conv_w, alog_row, dtb_row, dq, dk, dv, dbg):
    s_len = raw.shape[0]
    n_tiles = s_len // TM

    def body(raw_ref, prev_ref, next_ref, ba_ref, w_ref, al_ref, dt_ref, dq_ref, dqn_ref, dk_ref, dkn_ref, dv_ref, dvn_ref, dbg_ref,
             draw_ref, dba_ref, dw_ref, dal_ref, ddt_ref, dconv_ref):
        _zero_at_first([dw_ref, dal_ref, ddt_ref])
        i = pl.program_id(0)
        _, vjp_bg = jax.vjp(_bg_fn, ba_ref[...], al_ref[...], dt_ref[...])
        dba, dal, ddt = vjp_bg(dbg_ref[...])
        dba_ref[...] = dba
        _acc(dal_ref, dal)
        _acc(ddt_ref, ddt)
        has_prev = (i > 0).astype(F32)
        has_next = (i < n_tiles - 1).astype(F32)
        ct_refs = ((dq_ref, dqn_ref), (dk_ref, dkn_ref), (dv_ref, dvn_ref))
        for s in range(24):
            sl = slice(s * 128, (s + 1) * 128)
            hl = slice((s % 8) * 128, (s % 8 + 1) * 128)
            tile_ref, nxt_ref = ct_refs[s // 8]
            taps = _conv_taps(prev_ref[:, sl] * has_prev, raw_ref, sl, next_ref[:, sl] * has_next)
            conv = w_ref[3:4, sl] * taps[3]
            for j in range(3):
                conv = conv + w_ref[j:j + 1, sl] * taps[j]
            ct = jnp.concatenate([tile_ref[:, hl], nxt_ref[:, hl] * has_next], axis=0)
            _, vjp_act = jax.vjp(_act_of(s), conv)
            dconv, = vjp_act(ct)
            dconv_ref[...] = dconv
            draw = w_ref[3:4, sl] * dconv[:TM]
            for j in range(3):
                draw = draw + w_ref[j:j + 1, sl] * dconv_ref[pl.ds(3 - j, TM), :]
            draw_ref[:, sl] = draw
            for j in range(4):
                dw_ref[j:j + 1, sl] += jnp.sum(dconv[:TM] * taps[j][:TM], axis=0, keepdims=True)

    prev = pl.BlockSpec((8, QKV), lambda i: (jnp.maximum(i * (TM // 8) - 1, 0), 0))
    nxt = lambda n: pl.BlockSpec((8, n), lambda i: (jnp.minimum((i + 1) * (TM // 8), s_len // 8 - 1), 0))
    row128 = _fix((1, 128))
    ct_specs = [_row(TM, 1024), nxt(1024)] * 3
    return pl.pallas_call(
        body, name="gdn_prep_bwd", grid=(n_tiles,),
        in_specs=[_row(TM, QKV), prev, nxt(QKV), _row(TM, 128), _fix((C_CONV, QKV)), row128, row128] + ct_specs + [_row(TM, 128)],
        out_specs=[_row(TM, QKV), _row(TM, 128), _fix((C_CONV, QKV)), row128, row128],
        out_shape=[_sds(s_len, QKV), _sds(s_len, 128), _sds(C_CONV, QKV), _sds(1, 128), _sds(1, 128)],
        scratch_shapes=[pltpu.VMEM((TM + 8, 128), F32)],
        compiler_params=_cp("arbitrary"),
    )(raw, raw, raw, ba, conv_w, alog_row, dtb_row, dq, dq, dk, dk, dv, dv, dbg)


def _tein(eq, a, b):
    return jnp.einsum(eq, a, b, precision=lax.Precision.HIGH, preferred_element_type=F32)


def _unit_lower_inverse(lower):
    ri = lax.broadcasted_iota(jnp.int32, (C_CHUNK, C_CHUNK), 0)
    ci = lax.broadcasted_iota(jnp.int32, (C_CHUNK, C_CHUNK), 1)
    eye = (ri == ci).astype(F32)[None]
    mm = functools.partial(_bein, 'hij,hjk->hik')
    same_block = lambda size: (ri // size == ci // size)[None]
    n_mat = jnp.where(same_block(4), -lower, 0.0)
    inv = mm(eye + n_mat, eye + mm(n_mat, n_mat))
    for size in (4, 8, 16, 32):
        below = jnp.where(same_block(2 * size) & jnp.logical_not(same_block(size)), lower, 0.0)
        inv = inv - mm(inv, mm(below, inv))
    inv = _tein('hij,hjk->hik', inv, 2.0 * eye - _tein('hij,hjk->hik', eye + lower, inv))
    return jnp.where((ri >= ci)[None], inv, 0.0)


@jax.custom_vjp
def _known_inverse(lower, inv):
    return inv


def _known_inverse_fwd(lower, inv):
    return inv, inv


def _known_inverse_bwd(inv, d_inv):
    d_lower = -_bein('hik,hjk->hij', _bein('hji,hjk->hik', inv, d_inv), inv)
    return d_lower, jnp.zeros_like(inv)


_known_inverse.defvjp(_known_inverse_fwd, _known_inverse_bwd)


def _gdn_local(q, k, v, bgs, inv_known=None):
    lane = lax.broadcasted_iota(jnp.int32, (1, 128), 1)
    ri = lax.broadcasted_iota(jnp.int32, (C_CHUNK, C_CHUNK), 0)
    ci = lax.broadcasted_iota(jnp.int32, (C_CHUNK, C_CHUNK), 1)
    row_id = lax.broadcasted_iota(jnp.int32, (128, C_CHUNK), 0)
    beta, gc, gcj = [], [], []
    for bg in bgs:
        gc_t = _hdot((ri >= ci).astype(F32), bg)
        gc_rows = gc_t.T
        for h in range(C_HEADS):
            beta.append(jnp.sum(jnp.where(lane == h, bg, 0.0), axis=-1, keepdims=True))
            gc.append(jnp.sum(jnp.where(lane == C_HEADS + h, gc_t, 0.0), axis=-1, keepdims=True))
            gcj.append(jnp.sum(jnp.where(row_id == C_HEADS + h, gc_rows, 0.0), axis=0, keepdims=True))
    beta, gc, gcj = jnp.stack(beta, axis=0), jnp.stack(gc, axis=0), jnp.stack(gcj, axis=0)
    tril, strict = (ri >= ci)[None], (ri > ci)[None]
    decay = jnp.exp(jnp.where(tril, gc - gcj, -1e30))
    kb = k * beta
    lower = jnp.where(strict, _bein('hid,hjd->hij', kb, k) * decay, 0.0)
    inv = _unit_lower_inverse(lower) if inv_known is None else _known_inverse(lower, inv_known)
    egc = jnp.exp(gc)
    u_c = _bein('hij,hjd->hid', inv, v * beta)
    w_c = _bein('hij,hjd->hid', inv, kb * egc)
    aqk = _bein('hid,hjd->hij', q, k) * decay
    rowi = lax.broadcasted_iota(jnp.int32, (1, C_CHUNK, 1), 1)
    g_last = jnp.sum(jnp.where(rowi == C_CHUNK - 1, gc, 0.0), axis=1, keepdims=True)
    kd = k * jnp.exp(g_last - gc)
    return (u_c, w_c, aqk, q * egc, kd, jnp.exp(g_last)), inv


def _gdn_state(local, state):
    u_c, w_c, aqk, qg, kd, dec = local
    v_new = u_c - _bein('hik,hkv->hiv', w_c, state)
    o = _bein('hik,hkv->hiv', qg, state) + _bein('hij,hjv->hiv', aqk, v_new)
    return o, state * dec + _bein('hik,hiv->hkv', kd, v_new)


C_SUB = 4


def _gdn_group(q, k, v, bgs, state, inv_known=None):
    local, inv = _gdn_local(q, k, v, bgs, inv_known)
    outs = []
    for s in range(len(bgs)):
        o, state = _gdn_state(tuple(t[s * C_HEADS:(s + 1) * C_HEADS] for t in local), state)
        outs.append(o)
    return outs, state, inv


def _heads(ref):
    return jnp.stack([ref[s * C_CHUNK:(s + 1) * C_CHUNK, h * C_DK:(h + 1) * C_DK] for s in range(C_SUB) for h in range(C_HEADS)], axis=0)


def _put_heads(ref, sub, val):
    rows = slice(sub * C_CHUNK, (sub + 1) * C_CHUNK)
    for h in range(C_HEADS):
        ref[rows, h * C_DK:(h + 1) * C_DK] = val[h]


def _gdn_specs(s_len, rev):
    rows = C_SUB * C_CHUNK
    n_g = s_len // rows
    at = (lambda i: n_g - 1 - i) if rev else (lambda i: i)
    col = lambda c: pl.BlockSpec((rows, 1024), lambda i: (at(i), c))
    row128 = pl.BlockSpec((rows, 128), lambda i: (at(i), 0))
    state = pl.BlockSpec((1, C_HEADS, C_DK, C_DK), lambda i: (at(i), 0, 0, 0))
    inv = pl.BlockSpec((1, C_SUB * C_HEADS, C_CHUNK, C_CHUNK), lambda i: (at(i), 0, 0, 0))
    return n_g, col, row128, state, inv


def _gdn_fwd(qkv, bg):
    s_len = qkv.shape[0]
    n_g, col, row128, state_spec, inv_spec = _gdn_specs(s_len, False)

    def body(q_ref, k_ref, v_ref, bg_ref, o_ref, ss_ref, inv_ref, st_ref):
        _zero_at_first([st_ref])
        s0 = st_ref[...]
        ss_ref[0] = s0
        bgs = [bg_ref[s * C_CHUNK:(s + 1) * C_CHUNK, :] for s in range(C_SUB)]
        outs, s2, inv = _gdn_group(_heads(q_ref), _heads(k_ref), _heads(v_ref), bgs, s0)
        st_ref[...] = s2
        inv_ref[0] = inv
        for s in range(C_SUB):
            _put_heads(o_ref, s, outs[s])

    return pl.pallas_call(
        body, name="gdn_fwd", grid=(n_g,),
        in_specs=[col(0), col(1), col(2), row128],
        out_specs=[col(0), state_spec, inv_spec],
        out_shape=[_sds(s_len, 1024), _sds(n_g, C_HEADS, C_DK, C_DK), _sds(n_g, C_SUB * C_HEADS, C_CHUNK, C_CHUNK)],
        scratch_shapes=[pltpu.VMEM((C_HEADS, C_DK, C_DK), F32)],
        compiler_params=_cp("arbitrary"),
    )(qkv, qkv, qkv, bg)


def _gdn_bwd(qkv, bg, states, invs, do):
    s_len = qkv.shape[0]
    n_g, col, row128, state_spec, inv_spec = _gdn_specs(s_len, True)

    def body(q_ref, k_ref, v_ref, bg_ref, ss_ref, inv_ref, do_ref, dq_ref, dk_ref, dv_ref, dbg_ref, ds_ref):
        _zero_at_first([ds_ref])
        inv_known = inv_ref[0]

        def group(q, k, v, bgs, st):
            outs, st2, _ = _gdn_group(q, k, v, bgs, st, inv_known)
            return outs, st2

        bgs = [bg_ref[s * C_CHUNK:(s + 1) * C_CHUNK, :] for s in range(C_SUB)]
        _, vjp = jax.vjp(group, _heads(q_ref), _heads(k_ref), _heads(v_ref), bgs, ss_ref[0])
        douts = [jnp.stack([do_ref[s * C_CHUNK:(s + 1) * C_CHUNK, h * C_DK:(h + 1) * C_DK] for h in range(C_HEADS)], axis=0)
                 for s in range(C_SUB)]
        dq, dk, dv, dbgs, ds = vjp((douts, ds_ref[...]))
        ds_ref[...] = ds
        for s in range(C_SUB):
            dbg_ref[s * C_CHUNK:(s + 1) * C_CHUNK, :] = dbgs[s]
            for ref, val in ((dq_ref, dq), (dk_ref, dk), (dv_ref, dv)):
                _put_heads(ref, s, val[s * C_HEADS:(s + 1) * C_HEADS])

    return pl.pallas_call(
        body, name="gdn_bwd", grid=(n_g,),
        in_specs=[col(0), col(1), col(2), row128, state_spec, inv_spec, col(0)],
        out_specs=[col(0), col(0), col(0), row128],
        out_shape=[_sds(s_len, 1024)] * 3 + [_sds(s_len, 128)],
        scratch_shapes=[pltpu.VMEM((C_HEADS, C_DK, C_DK), F32)],
        compiler_params=_cp("arbitrary"),
    )(qkv, qkv, qkv, bg, states, invs, do)


def _head_norm_gate(o, gate, norm_g):
    return (_rms(o) * norm_g) * jax.nn.silu(gate)


def _l1_out_fb(o, gate_c, x1, target, norm_g, w_out, post_g, gate):
    s_len = x1.shape[0]

    def body(o_ref, gc_ref, x1_ref, t_ref, ng_ref, w_ref, pg_ref, gt_ref,
             loss_ref, dres_ref, do_ref, dgc_ref, dw_ref, dng_ref, dpg_ref, dgt_ref):
        _zero_at_first([loss_ref, dw_ref, dng_ref, dpg_ref, dgt_ref])
        ng = ng_ref[...]
        ons, vjps = [], []
        for h in range(C_HEADS):
            sl = slice(h * C_DK, (h + 1) * C_DK)
            on, vjp_h = jax.vjp(_head_norm_gate, o_ref[:, sl], gc_ref[:, sl], ng)
            ons.append(on)
            vjps.append(vjp_h)
        on_all = jnp.concatenate(ons, axis=-1)
        y = _bdot(on_all, w_ref[...])
        x2, vjp2 = jax.vjp(_post_res, y, x1_ref[...], pg_ref[...], gt_ref[...])
        err = x2 - t_ref[...]
        _acc(loss_ref, jnp.full((1, 128), 0.5 * jnp.sum(jnp.mean(err * err, axis=-1)), F32))
        dx2 = err * (1.0 / D_MODEL)
        dy, _, dpg, dgt = vjp2(dx2)
        dres_ref[...] = dx2
        _acc(dpg_ref, dpg)
        _acc(dgt_ref, dgt)
        dw_ref[...] += _bdot_tn(on_all, dy)
        don = _bdot_nt(dy, w_ref[...])
        for h in range(C_HEADS):
            sl = slice(h * C_DK, (h + 1) * C_DK)
            do_h, dgc_h, dng = vjps[h](don[:, sl])
            do_ref[:, sl] = do_h
            dgc_ref[:, sl] = dgc_h
            _acc(dng_ref, dng)

    vec, r10 = _fix((1, D_MODEL)), _row(TM, D_MODEL)
    row128 = _fix((1, 128))
    return pl.pallas_call(
        body, name="l1_out_fb", grid=(s_len // TM,),
        in_specs=[r10, r10, r10, r10, row128, _fix((D_MODEL, D_MODEL)), vec, vec],
        out_specs=[row128, r10, r10, r10, _fix((D_MODEL, D_MODEL)), row128, vec, vec],
        out_shape=[_sds(1, 128), _sds(s_len, D_MODEL), _sds(s_len, D_MODEL), _sds(s_len, D_MODEL),
                   _sds(D_MODEL, D_MODEL), _sds(1, 128), _sds(1, D_MODEL), _sds(1, D_MODEL)],
        compiler_params=_cp("arbitrary"),
    )(o, gate_c, x1, target, norm_g, w_out, post_g, gate)


def _row_of(v, width, at):
    return jnp.zeros((1, width), F32).at[0, at:at + v.shape[-1]].set(v.reshape(-1))


def _local_step(x, target, mod, wd, comm=None):
    s_len = x.shape[0]
    shift0, scale0, gate0 = (mod[0:1, i * 1024:(i + 1) * 1024] for i in range(3))
    shift1, scale1, gate1 = (mod[1:2, i * 1024:(i + 1) * 1024] for i in range(3))
    pre_g0, pre_g1 = wd["pre_g"][0:1], wd["pre_g"][1:2]
    post_g0, post_g1 = wd["post_g"][0:1], wd["post_g"][1:2]
    w_in0 = wd["ab_w_in"].astype(BF)
    d_skip, glu_b = wd["s5_d"].reshape(1, 512), wd["s5_glu_b"].reshape(1, 512)
    norm_g = wd["gdn_norm_g"].reshape(1, 128)
    alog_row = _row_of(wd["gdn_a_log"], 128, C_HEADS)
    dtb_row = _row_of(wd["gdn_dt_bias"], 128, C_HEADS)
    conv_w = wd["gdn_conv"]

    a_re, a_im = wd["s5_a_re"], wd["s5_a_im"]
    log_dt = wd["s5_log_dt"].reshape(B_GROUPS, 1)
    bt_re = wd["s5_b_re"].transpose(0, 2, 1).reshape(B_WIDTH, B_STATE)
    bt_im = wd["s5_b_im"].transpose(0, 2, 1).reshape(B_WIDTH, B_STATE)
    abar_r, abar_i, bbar_r, bbar_i = _s5_params(a_re, a_im, log_dt, bt_re, bt_im)
    abr, abi = abar_r.reshape(1, -1), abar_i.reshape(1, -1)
    btr, bti = _blockdiag_b(bbar_r).astype(BF), _blockdiag_b(bbar_i).astype(BF)
    ctr, cti = _blockdiag_c(wd["s5_c_re"]).astype(BF), _blockdiag_c(wd["s5_c_im"]).astype(BF)

    table = _bucket_table()
    biases = _attn_bias(wd["rel_bias"], jnp.asarray(table))
    front = _l0_front(x, pre_g0, scale0, shift0, w_in0)
    qs, ks, vs = front[0:3], front[3:6], front[6:9]
    u, ga, gb, h0 = front[9:]
    riders = [None] * 4 if comm is None else comm.late_exchanges()
    os, ls, got = [], [], []
    for i in range(3):
        (o_d, l_d), g = _attn_fwd(qs[i], ks[i], vs[i], biases[i], exchange=riders[i])
        os.append(o_d)
        ls.append(l_d)
        got.append(g)
    seg_len = s_len // S5_SEG
    zero_state = (jnp.zeros((S5_SEG, S5_WIDTH), F32),) * 2
    ends, _ = _s5_seq_fwd(u, btr, bti, ctr, cti, abr, abi, zero_state, False)
    x_entry = _s5_entries("s5_entries_fwd", *ends, abr, abi, seg_len, False)
    (xr, xi, ypre3, _, _), g = _s5_seq_fwd(u, btr, bti, ctr, cti, abr, abi, x_entry, True, exchange=riders[3])
    got.append(g)
    ypre = ypre3.reshape(s_len, B_WIDTH)
    rider = None
    if comm is not None:
        mine0, mine1 = comm.late_halves(got)
        wd = {**wd, **comm.full_weights(["ab_w_out", "s5_glu_w"], mine0, _sibling_exchange("gather_w_sibling_l0", mine0))}
        rider = (mine1, "sibling")
    w_out0 = wd["ab_w_out"].astype(BF)
    glu_w = wd["s5_glu_w"].astype(BF)
    (x1, y0), theirs1 = _l0_out(os, ls, ga, gb, ypre, u, x, d_skip, glu_w, glu_b, w_out0, post_g0, gate0, exchange=rider)
    if comm is not None:
        wd = {**wd, **comm.full_weights(["gdn_w_in", "gdn_w_out"], mine1, theirs1)}
    w_in1 = wd["gdn_w_in"]
    if w_in1.shape[1] == C_IN:
        w_in1 = jnp.concatenate([w_in1, jnp.zeros((D_MODEL, C_IN_PAD - C_IN), w_in1.dtype)], axis=1)
    w_in1 = w_in1.astype(BF)
    w_out1 = wd["gdn_w_out"].astype(BF)

    raw, gate_c, ba, h1 = _l1_front(x1, pre_g1, scale1, shift1, w_in1)
    qkv, bg = _gdn_prep(raw, ba, conv_w, alog_row, dtb_row)
    o_gdn, states, invs = _gdn_fwd(qkv, bg)
    loss_row, dres1, do_gdn, dgate_c, dw_out1, dnorm_g, dpost_g1, dgate1 = _l1_out_fb(
        o_gdn, gate_c, x1, target, norm_g, w_out1, post_g1, gate1)

    dq1, dk1, dv1, dbg = _gdn_bwd(qkv, bg, states, invs, do_gdn)
    draw, dba, dconv_w, dalog_row, ddtb_row = _gdn_prep_bwd(raw, ba, conv_w, alog_row, dtb_row, dq1, dk1, dv1, dbg)
    dz1, dx1, dpre_g1, dscale1, dshift1 = _front_bwd(
        "l1_front_bwd", x1, pre_g1, scale1, shift1, w_in1, dres1, [[draw], [dgate_c], [dba]], [QKV, 1024, 128])
    dw_in1 = _matmul_tn("l1_dw_in", h1, dz1, 1408)

    l1_names, l0_names = ["gdn_w_in", "gdn_w_out"], ["ab_w_out", "s5_glu_w"]
    rider = None if comm is None else (comm.split_halves({"gdn_w_in": dw_in1, "gdn_w_out": dw_out1}), "sibling")
    l0b, from_sibling1 = _l0_out_bwd(os, ls, ga, gb, ypre, u, x, y0, d_skip, glu_w, glu_b, w_out0, post_g0, gate0, dx1, exchange=rider)
    dos, dls = l0b[0:3], l0b[3:6]
    dga, dgb, dypre, du_skip, dd_skip, dglu_w, dglu_b, dw_out0, dpost_g0, dgate0 = l0b[6:]
    rider = None if comm is None else (comm.split_halves({"ab_w_out": dw_out0, "s5_glu_w": dglu_w}), "sibling")
    starts, _ = _s5_seq_bwd(dypre, None, None, None, btr, bti, ctr, cti, abr, abi, zero_state, None, False)
    g_entry = _s5_entries("s5_entries_bwd", *starts, abr, -abi, seg_len, True)
    (du3, dbtr, dbti, dctr, dcti, dabr, dabi, _, _), from_sibling0 = _s5_seq_bwd(
        dypre, xr, xi, u, btr, bti, ctr, cti, abr, abi, g_entry, x_entry, True, exchange=rider)
    du_scan = du3.reshape(s_len, B_WIDTH)
    riders = [None] * 3
    if comm is not None:
        riders = [(comm.chip_partials(l1_names, from_sibling1), "scatter"), (comm.chip_partials(l0_names, from_sibling0), "scatter"), None]
    dqs, dks, dvs, dbs = [], [], [], []
    for i in range(3):
        (dq_d, dk_d, dv_d, db_d), got_d = _attn_bwd(qs[i], ks[i], vs[i], biases[i], os[i], ls[i], dos[i], dls[i], exchange=riders[i])
        dqs.append(dq_d)
        dks.append(dk_d)
        dvs.append(dv_d)
        dbs.append(db_d)
        if comm is not None and riders[i] is not None:
            comm.received.update(zip((l1_names, l0_names)[i], got_d))
    parts = [dqs, dks, dvs, [du_skip, du_scan], [dga], [dgb]]
    dz0, grad_x, dpre_g0, dscale0, dshift0 = _front_bwd(
        "l0_front_bwd", x, pre_g0, scale0, shift0, w_in0, dx1, parts, [512] * 6)
    dw_in0 = _matmul_tn("l0_dw_in", h0, dz0, 768)

    idx_rows = jnp.asarray(table.reshape(3, -1), F32)
    drel = _rel_bias_grad(dbs, idx_rows).T
    da_re, da_im, dlog_dt, dbt_re, dbt_im = _s5_params_bwd(
        a_re, a_im, log_dt, bt_re, bt_im, dabr.reshape(B_GROUPS, B_STATE), dabi.reshape(B_GROUPS, B_STATE),
        _blockdiag_b_t(dbtr), _blockdiag_b_t(dbti))
    unb = lambda d: d.reshape(B_GROUPS, B_GROUP, B_STATE).transpose(0, 2, 1)
    grads = {
        "pre_g": jnp.concatenate([dpre_g0, dpre_g1], 0), "post_g": jnp.concatenate([dpost_g0, dpost_g1], 0),
        "rel_bias": drel, "ab_w_in": dw_in0, "ab_w_out": dw_out0,
        "s5_a_re": da_re, "s5_a_im": da_im, "s5_log_dt": dlog_dt.reshape(B_GROUPS),
        "s5_b_re": unb(dbt_re), "s5_b_im": unb(dbt_im),
        "s5_c_re": _blockdiag_c_t(dctr), "s5_c_im": _blockdiag_c_t(dcti),
        "s5_d": dd_skip.reshape(512), "s5_glu_w": dglu_w, "s5_glu_b": dglu_b.reshape(512),
        "gdn_w_in": dw_in1[:, :C_IN], "gdn_conv": dconv_w,
        "gdn_a_log": dalog_row[0, C_HEADS:2 * C_HEADS], "gdn_dt_bias": ddtb_row[0, C_HEADS:2 * C_HEADS],
        "gdn_norm_g": dnorm_g.reshape(128), "gdn_w_out": dw_out1,
    }
    dmod = jnp.concatenate([jnp.concatenate([dshift0, dscale0, dgate0], 1), jnp.concatenate([dshift1, dscale1, dgate1], 1)], 0)
    return loss_row[0, 0], grad_x, grads, dmod


def _place():
    return lax.axis_index("x"), lax.axis_index("y"), lax.axis_index("c")


def _flip(v, bit):
    return 1 - v if bit else v


def _hbm_call(name, body, arrs, out_shapes, n_sem):
    any_spec = pl.BlockSpec(memory_space=pl.ANY)
    return pl.pallas_call(
        body, name=name,
        in_specs=[any_spec] * len(arrs), out_specs=[any_spec] * len(out_shapes), out_shape=out_shapes,
        scratch_shapes=[pltpu.SemaphoreType.DMA((n_sem,)), pltpu.SemaphoreType.DMA((n_sem,))],
    )(*arrs)


def _own_slot(gathered, own, slot):
    idx = lax.broadcasted_iota(jnp.int32, (gathered.shape[0],) + (1,) * own.ndim, 0)
    return jnp.where(idx == slot, own[None], gathered)


def _all_gather8(name, arr):
    def body(x_ref, out_ref, send_sems, recv_sems):
        x, y, c = _place()
        me = 4 * x + 2 * y + c
        sends, recvs = [], []
        for m in range(1, 8):
            peer = (_flip(x, m & 4), _flip(y, m & 2), _flip(c, m & 1))
            sends.append(pltpu.make_async_remote_copy(x_ref, out_ref.at[me], send_sems.at[m - 1], recv_sems.at[m - 1],
                                                      device_id=peer, device_id_type=MESH))
            recvs.append(pltpu.make_async_remote_copy(x_ref, out_ref.at[4 * peer[0] + 2 * peer[1] + peer[2]], send_sems.at[m - 1],
                                                      recv_sems.at[m - 1], device_id=peer, device_id_type=MESH))
        for cp in sends:
            cp.start()
        for cp in recvs:
            cp.wait_recv()
        for cp in sends:
            cp.wait_send()

    return _hbm_call(name, body, [arr], [jax.ShapeDtypeStruct((8,) + arr.shape, arr.dtype)], 7)[0]


def _all_to_all8(name, arr):
    def body(x_ref, out_ref, send_sems, recv_sems):
        x, y, c = _place()
        me = 4 * x + 2 * y + c
        sends, recvs = [], []
        for m in range(1, 8):
            peer = (_flip(x, m & 4), _flip(y, m & 2), _flip(c, m & 1))
            peer_id = 4 * peer[0] + 2 * peer[1] + peer[2]
            sends.append(pltpu.make_async_remote_copy(x_ref.at[peer_id], out_ref.at[me], send_sems.at[m - 1], recv_sems.at[m - 1],
                                                      device_id=peer, device_id_type=MESH))
            recvs.append(pltpu.make_async_remote_copy(x_ref.at[peer_id], out_ref.at[peer_id], send_sems.at[m - 1], recv_sems.at[m - 1],
                                                      device_id=peer, device_id_type=MESH))
        for cp in sends:
            cp.start()
        for cp in recvs:
            cp.wait_recv()
        for cp in sends:
            cp.wait_send()

    return _hbm_call(name, body, [arr], [jax.ShapeDtypeStruct(arr.shape, arr.dtype)], 7)[0]


def _chip_copies(ins, outs, send_sems, recv_sems, scatter):
    x, y, c = _place()
    mine = 2 * x + y
    sends, recvs = [], []
    for a in range(len(ins)):
        for m in range(1, 4):
            px, py = _flip(x, m & 2), _flip(y, m & 1)
            k = 3 * a + m - 1
            src = ins[a].at[2 * px + py] if scatter else ins[a]
            sends.append(pltpu.make_async_remote_copy(src, outs[a].at[mine], send_sems.at[k], recv_sems.at[k],
                                                      device_id=(px, py, c), device_id_type=MESH))
            recvs.append(pltpu.make_async_remote_copy(src, outs[a].at[2 * px + py], send_sems.at[k], recv_sems.at[k],
                                                      device_id=(px, py, c), device_id_type=MESH))
    return sends, recvs


def _chip_shapes(arrs, scatter):
    return [jax.ShapeDtypeStruct(a.shape if scatter else (4,) + a.shape, a.dtype) for a in arrs]


def _chip_exchange(name, arrs, scatter):
    n = len(arrs)

    def body(*refs):
        sends, recvs = _chip_copies(refs[:n], refs[n:2 * n], refs[2 * n], refs[2 * n + 1], scatter)
        for cp in sends:
            cp.start()
        for cp in recvs:
            cp.wait_recv()
        for cp in sends:
            cp.wait_send()

    return _hbm_call(name, body, arrs, _chip_shapes(arrs, scatter), 3 * n)


def _call_with_exchange(body, name, grid, in_specs, out_specs, out_shape, scratch_shapes, args, exchange):
    if exchange is None:
        return pl.pallas_call(body, name=name, grid=grid, in_specs=in_specs, out_specs=out_specs, out_shape=out_shape,
                              scratch_shapes=scratch_shapes, compiler_params=_cp(*["arbitrary"] * len(grid)))(*args), []
    arrs, kind = exchange
    n_in, n_out, n_ex, n_scr = len(in_specs), len(out_specs), len(arrs), len(scratch_shapes)
    n_sem = n_ex if kind == "sibling" else 3 * n_ex
    ex_shapes = [jax.ShapeDtypeStruct(a.shape, a.dtype) for a in arrs] if kind == "sibling" else _chip_shapes(arrs, kind == "scatter")

    def fused(*refs):
        ins, ex_in = refs[:n_in], refs[n_in:n_in + n_ex]
        outs, ex_out = refs[n_in + n_ex:n_in + n_ex + n_out], refs[n_in + n_ex + n_out:n_in + 2 * n_ex + n_out]
        rest = refs[n_in + 2 * n_ex + n_out:]
        if kind == "sibling":
            sends = recvs = _sibling_copies(ex_in, ex_out, rest[n_scr], rest[n_scr + 1])
        else:
            sends, recvs = _chip_copies(ex_in, ex_out, rest[n_scr], rest[n_scr + 1], kind == "scatter")
        first, last = pl.program_id(0) == 0, pl.program_id(0) == grid[0] - 1
        for k in range(1, len(grid)):
            first, last = first & (pl.program_id(k) == 0), last & (pl.program_id(k) == grid[k] - 1)

        @pl.when(first)
        def _():
            for cp in sends:
                cp.start()

        body(*ins, *outs, *rest[:n_scr])

        @pl.when(last)
        def _():
            for cp in recvs:
                cp.wait_recv()
            for cp in sends:
                cp.wait_send()

    any_spec = pl.BlockSpec(memory_space=pl.ANY)
    res = pl.pallas_call(
        fused, name=name, grid=grid, in_specs=list(in_specs) + [any_spec] * n_ex, out_specs=list(out_specs) + [any_spec] * n_ex,
        out_shape=list(out_shape) + ex_shapes,
        scratch_shapes=list(scratch_shapes) + [pltpu.SemaphoreType.DMA((n_sem,))] * 2,
        compiler_params=_cp(*["arbitrary"] * len(grid)))(*args, *arrs)
    return res[:n_out], res[n_out:]


def _sibling_copies(ins, outs, send_sems, recv_sems):
    x, y, c = _place()
    return [pltpu.make_async_remote_copy(ins[a], outs[a], send_sems.at[a], recv_sems.at[a],
                                         device_id=(x, y, 1 - c), device_id_type=MESH) for a in range(len(ins))]


def _sibling_exchange(name, arrs):
    n = len(arrs)

    def body(*refs):
        copies = _sibling_copies(refs[:n], refs[n:2 * n], refs[2 * n], refs[2 * n + 1])
        for cp in copies:
            cp.start()
        for cp in copies:
            cp.wait_recv()
        for cp in copies:
            cp.wait_send()

    return _hbm_call(name, body, arrs, [jax.ShapeDtypeStruct(a.shape, a.dtype) for a in arrs], n)


def _row_tile(rows):
    for t in (256, 128, 64, 32, 16, 8):
        if rows % t == 0:
            return t
    return rows


def _pair_sum(name, a, b, out_dtype):
    rows, cols = a.shape
    tr = _row_tile(rows)

    def body(a_ref, b_ref, o_ref):
        o_ref[...] = (a_ref[...] + b_ref[...]).astype(out_dtype)

    return pl.pallas_call(body, name=name, grid=(rows // tr,), in_specs=[_row(tr, cols)] * 2, out_specs=_row(tr, cols),
                          out_shape=_sds(rows, cols, dtype=out_dtype), compiler_params=_cp("arbitrary"))(a, b)


def _chip_sum(name, recv, partial, mine):
    n, rows, cols = recv.shape
    tr = _row_tile(rows)

    def body(mine_ref, *refs):
        own = refs[n][0].astype(F32)
        acc = None
        for s in range(n):
            term = jnp.where(mine_ref[0] == s, own, refs[s][0].astype(F32))
            acc = term if acc is None else acc + term
        refs[-1][...] = acc

    def slot_spec(s):
        return pl.BlockSpec((1, tr, cols), lambda i, m: (jnp.where(m[0] == s, (s + 1) % n, s), i, 0))

    grid_spec = pltpu.PrefetchScalarGridSpec(
        num_scalar_prefetch=1, grid=(rows // tr,),
        in_specs=[slot_spec(s) for s in range(n)] + [pl.BlockSpec((1, tr, cols), lambda i, m: (m[0], i, 0))],
        out_specs=pl.BlockSpec((tr, cols), lambda i, m: (i, 0)))
    return pl.pallas_call(body, name=name, grid_spec=grid_spec, out_shape=_sds(rows, cols),
                          compiler_params=_cp("arbitrary"))(mine, *([recv] * n), partial)


def _slot_sum(name, arr):
    n, rows, cols = arr.shape
    tr = _row_tile(rows)

    def body(*refs):
        acc = refs[0][0]
        for r in refs[1:-1]:
            acc = acc + r[0]
        refs[-1][...] = acc

    specs = [pl.BlockSpec((1, tr, cols), functools.partial(lambda s, i: (s, i, 0), s)) for s in range(n)]
    return pl.pallas_call(body, name=name, grid=(rows // tr,), in_specs=specs, out_specs=_row(tr, cols),
                          out_shape=_sds(rows, cols), compiler_params=_cp("arbitrary"))(*([arr] * n))


def _adamw(name, w, g, m, v, exchange=None):
    rows, cols = w.shape
    tr = _row_tile(rows)

    def body(w_ref, g_ref, m_ref, v_ref, d_ref, nm_ref, nv_ref):
        g_ = g_ref[...]
        m_ = ADAM_B1 * m_ref[...] + (1.0 - ADAM_B1) * g_
        v_ = ADAM_B2 * v_ref[...] + (1.0 - ADAM_B2) * (g_ * g_)
        m_hat = m_ / (1.0 - ADAM_B1 ** ADAM_STEP)
        v_hat = v_ / (1.0 - ADAM_B2 ** ADAM_STEP)
        d_ref[...] = -ADAM_LR * (m_hat / (jnp.sqrt(v_hat) + ADAM_EPS) + ADAM_WD * w_ref[...])
        nm_ref[...] = m_
        nv_ref[...] = v_

    spec = _row(tr, cols)
    return _call_with_exchange(body, name, (rows // tr,), [spec] * 4, [spec] * 3, [_sds(rows, cols)] * 3, [], (w, g, m, v), exchange)


def _adamw_many(name, ws, gs, ms, vs):
    n = len(ws)

    def body(*refs):
        for i in range(n):
            w_ref, g_ref, m_ref, v_ref = (refs[k * n + i] for k in range(4))
            d_ref, nm_ref, nv_ref = (refs[(4 + k) * n + i] for k in range(3))
            g_ = g_ref[...]
            m_ = ADAM_B1 * m_ref[...] + (1.0 - ADAM_B1) * g_
            v_ = ADAM_B2 * v_ref[...] + (1.0 - ADAM_B2) * (g_ * g_)
            m_hat = m_ / (1.0 - ADAM_B1 ** ADAM_STEP)
            v_hat = v_ / (1.0 - ADAM_B2 ** ADAM_STEP)
            d_ref[...] = -ADAM_LR * (m_hat / (jnp.sqrt(v_hat) + ADAM_EPS) + ADAM_WD * w_ref[...])
            nm_ref[...] = m_
            nv_ref[...] = v_

    shapes = [_sds(*a.shape) for a in ws]
    res = pl.pallas_call(body, name=name, out_shape=shapes * 3,
                         compiler_params=pltpu.CompilerParams(vmem_limit_bytes=VMEM_LIMIT_BYTES))(*ws, *gs, *ms, *vs)
    return [(res[i], res[n + i], res[2 * n + i]) for i in range(n)]


def _adamw_halves(name, w, g_mine, g_sibling, m, v, core):
    _, rows, cols = w.shape
    half = rows // 2
    tr = _row_tile(half)
    per_half = half // tr

    def body(core_ref, w_ref, gm_ref, gs_ref, m_ref, v_ref, g_ref, d_ref, nm_ref, nv_ref):
        g_ = jnp.where(pl.program_id(0) // per_half == core_ref[0], gm_ref[...], gs_ref[...])
        m_ = ADAM_B1 * m_ref[...] + (1.0 - ADAM_B1) * g_
        v_ = ADAM_B2 * v_ref[...] + (1.0 - ADAM_B2) * (g_ * g_)
        m_hat = m_ / (1.0 - ADAM_B1 ** ADAM_STEP)
        v_hat = v_ / (1.0 - ADAM_B2 ** ADAM_STEP)
        g_ref[...] = g_
        d_ref[...] = -ADAM_LR * (m_hat / (jnp.sqrt(v_hat) + ADAM_EPS) + ADAM_WD * w_ref[...])
        nm_ref[...] = m_
        nv_ref[...] = v_

    full = pl.BlockSpec((None, tr, cols), lambda i, c: (0, i, 0))
    in_half = pl.BlockSpec((tr, cols), lambda i, c: (i % per_half, 0))
    grid_spec = pltpu.PrefetchScalarGridSpec(num_scalar_prefetch=1, grid=(rows // tr,),
                                             in_specs=[full, in_half, in_half, full, full], out_specs=[full] * 4)
    return pl.pallas_call(body, name=name, grid_spec=grid_spec, out_shape=[_sds(1, rows, cols)] * 4,
                          compiler_params=_cp("arbitrary"))(core, w, g_mine, g_sibling, m, v)


def _mod_local(c_all, ada_w):
    def body(c_ref, w_ref, o_ref):
        c_act = jax.nn.silu(c_ref[...])
        for l in range(2):
            o_ref[l] = _hdot(c_act, w_ref[l])

    return pl.pallas_call(body, name="mod_local", out_shape=_sds(2, 8, ada_w.shape[2]),
                          compiler_params=pltpu.CompilerParams(vmem_limit_bytes=VMEM_LIMIT_BYTES))(c_all, ada_w)


def _ada_w_grad(c_all, dmod_cols):
    def body(c_ref, d_ref, o_ref):
        c_act = jax.nn.silu(c_ref[...])
        for l in range(2):
            o_ref[l] = lax.dot_general(c_act, d_ref[l], (((0,), (0,)), ((), ())), precision=HI, preferred_element_type=F32)

    return pl.pallas_call(body, name="ada_w_grad", out_shape=_sds(2, D_MODEL, dmod_cols.shape[2]),
                          compiler_params=pltpu.CompilerParams(vmem_limit_bytes=VMEM_LIMIT_BYTES))(c_all, dmod_cols)


_SMALL = ("ada_b", "pre_g", "post_g", "rel_bias", "s5_a_re", "s5_a_im", "s5_log_dt", "s5_b_re", "s5_b_im", "s5_c_re", "s5_c_im",
          "s5_d", "s5_glu_b", "gdn_a_log", "gdn_dt_bias", "gdn_norm_g")
_SHARDED = ("ab_w_in", "ab_w_out", "s5_glu_w", "gdn_w_in", "gdn_w_out")
_COL_SHARDED = ("ab_w_in", "gdn_w_in")
_WEIGHTS = ("ada_w", "ada_b", "pre_g", "post_g", "rel_bias", "ab_w_in", "ab_w_out", "s5_a_re", "s5_a_im", "s5_log_dt", "s5_b_re",
            "s5_b_im", "s5_c_re", "s5_c_im", "s5_d", "s5_glu_w", "s5_glu_b", "gdn_w_in", "gdn_conv", "gdn_a_log", "gdn_dt_bias",
            "gdn_norm_g", "gdn_w_out")


def _rows128(n):
    return -(-n // 128)


def _pack(arrs, total_rows):
    pieces = []
    for a in arrs:
        flat = a.reshape(-1)
        pieces.append(jnp.pad(flat, (0, _rows128(flat.shape[0]) * 128 - flat.shape[0])).reshape(-1, 128))
    used = sum(p.shape[0] for p in pieces)
    pieces.append(jnp.zeros((total_rows - used, 128), F32))
    return jnp.concatenate(pieces, axis=0)


def _unpack(buf, shapes):
    out, at = [], 0
    for shp in shapes:
        n = int(np.prod(shp))
        out.append(buf[at:at + _rows128(n)].reshape(-1)[:n].reshape(shp))
        at += _rows128(n)
    return out


def _full_from_halves(g):
    return g.transpose(1, 0, 2, 3).reshape(8 * g.shape[2], g.shape[3])


def _join_col_shards(name, mine, theirs, core, width):
    n, h, cols = mine.shape
    tr = _row_tile(h)
    per_half = h // tr

    def body(core_ref, a_ref, b_ref, o_ref):
        def put(src_ref):
            pad = [jnp.zeros((tr, width - n * cols), o_ref.dtype)] if width > n * cols else []
            o_ref[...] = jnp.concatenate([src_ref[s] for s in range(n)] + pad, axis=1)

        is_mine = pl.program_id(0) // per_half == core_ref[0]
        pl.when(is_mine)(functools.partial(put, a_ref))
        pl.when(jnp.logical_not(is_mine))(functools.partial(put, b_ref))

    half = pl.BlockSpec((n, tr, cols), lambda i, c: (0, i % per_half, 0))
    grid_spec = pltpu.PrefetchScalarGridSpec(num_scalar_prefetch=1, grid=(2 * per_half,), in_specs=[half, half],
                                             out_specs=pl.BlockSpec((tr, width), lambda i, c: (i, 0)))
    return pl.pallas_call(body, name=name, grid_spec=grid_spec, out_shape=_sds(2 * h, width, dtype=mine.dtype),
                          compiler_params=_cp("arbitrary"))(core, mine, theirs)


def _split_col_shards(name, g, cols, half, add=None, out_dtype=F32):
    h = g.shape[0] // 2
    tr = _row_tile(h)
    per_half = h // tr

    def body(half_ref, g_ref, *refs):
        for s in range(4):
            part = g_ref[:, s * cols:(s + 1) * cols]
            refs[-1][s] = (part if add is None else part + refs[0][s]).astype(out_dtype)

    shards = pl.BlockSpec((4, tr, cols), lambda i, c: (0, i, 0))
    grid_spec = pltpu.PrefetchScalarGridSpec(
        num_scalar_prefetch=1, grid=(per_half,),
        in_specs=[pl.BlockSpec((tr, g.shape[1]), lambda i, c: (c[0] * per_half + i, 0))] + ([] if add is None else [shards]),
        out_specs=shards)
    return pl.pallas_call(body, name=name, grid_spec=grid_spec, out_shape=_sds(4, h, cols, dtype=out_dtype),
                          compiler_params=_cp("arbitrary"))(half, g, *([] if add is None else [add]))


_LATE = ("ab_w_out", "s5_glu_w", "gdn_w_in", "gdn_w_out")


class _WeightExchanges:
    def __init__(self, shards, core, chip):
        self.core, self.chip = core, chip
        self.core_1 = jnp.reshape(core, (1,)).astype(jnp.int32)
        self.half = {}
        for name, shard in shards.items():
            h = shard.shape[0] // 2
            self.half[name] = lax.dynamic_slice_in_dim(shard.astype(BF), core * h, h, axis=0)
        self.mine, self.partial, self.received = {}, {}, {}

    def my_halves(self, names, from_chips):
        return [_own_slot(g, self.half[n], self.chip) for n, g in zip(names, from_chips)]

    def full_weights(self, names, mine, theirs):
        full = {}
        for n, a, b in zip(names, mine, theirs):
            if n in _COL_SHARDED:
                full[n] = _join_col_shards("join_" + n, a, b, self.core_1, C_IN_PAD if n == "gdn_w_in" else 4 * a.shape[2])
            else:
                full[n] = _full_from_halves(jnp.where(self.core == 0, jnp.stack([a, b], 0), jnp.stack([b, a], 0)))
        return full

    def first_weights(self):
        mine = self.my_halves(["ab_w_in"], _chip_exchange("gather_w_chips", [self.half["ab_w_in"]], False))
        return self.full_weights(["ab_w_in"], mine, _sibling_exchange("gather_w_sibling_first", mine))

    def late_exchanges(self):
        rows = self.half["gdn_w_in"].shape[0] // 2
        pieces = [self.half["gdn_w_in"][:rows], self.half["gdn_w_in"][rows:]]
        return [([self.half["ab_w_out"], self.half["s5_glu_w"]], "gather"), ([self.half["gdn_w_out"]], "gather"),
                ([pieces[0]], "gather"), ([pieces[1]], "gather")]

    def late_halves(self, got):
        return (self.my_halves(["ab_w_out", "s5_glu_w"], got[0]),
                self.my_halves(["gdn_w_in", "gdn_w_out"], [jnp.concatenate([got[2][0], got[3][0]], axis=1), got[1][0]]))

    def split_halves(self, grads):
        other = []
        for name, g in grads.items():
            if name in _COL_SHARDED:
                self.mine[name] = g
                other.append(_split_col_shards("split_other_" + name, g, self.half[name].shape[1], 1 - self.core_1))
                continue
            sm = g.reshape(4, g.shape[0] // 4, g.shape[1])
            h = sm.shape[1] // 2
            self.mine[name] = lax.dynamic_slice_in_dim(sm, self.core * h, h, axis=1)
            other.append(lax.dynamic_slice_in_dim(sm, (1 - self.core) * h, h, axis=1))
        return other

    def chip_partials(self, names, from_sibling):
        for name, b in zip(names, from_sibling):
            a = self.mine[name]
            if name in _COL_SHARDED:
                self.partial[name] = _split_col_shards("sum_sibling_" + name, a, b.shape[2], self.core_1, add=b, out_dtype=BF)
                continue
            flat = lambda t: t.reshape(-1, t.shape[-1])
            self.partial[name] = _pair_sum("sum_sibling_" + name, flat(a), flat(b), BF).reshape(a.shape)
        return [self.partial[n] for n in names]


def kernel(x, c, ada_w, ada_b, pre_g, post_g, rel_bias, ab_w_in, ab_w_out, s5_a_re, s5_a_im, s5_log_dt, s5_b_re, s5_b_im, s5_c_re, s5_c_im, s5_d, s5_glu_w, s5_glu_b, gdn_w_in, gdn_conv, gdn_a_log, gdn_dt_bias, gdn_norm_g, gdn_w_out, loss_target, m_ada_w, m_ada_b, m_pre_g, m_post_g, m_rel_bias, m_ab_w_in, m_ab_w_out, m_s5_a_re, m_s5_a_im, m_s5_log_dt, m_s5_b_re, m_s5_b_im, m_s5_c_re, m_s5_c_im, m_s5_d, m_s5_glu_w, m_s5_glu_b, m_gdn_w_in, m_gdn_conv, m_gdn_a_log, m_gdn_dt_bias, m_gdn_norm_g, m_gdn_w_out, v_ada_w, v_ada_b, v_pre_g, v_post_g, v_rel_bias, v_ab_w_in, v_ab_w_out, v_s5_a_re, v_s5_a_im, v_s5_log_dt, v_s5_b_re, v_s5_b_im, v_s5_c_re, v_s5_c_im, v_s5_d, v_s5_glu_w, v_s5_glu_b, v_gdn_w_in, v_gdn_conv, v_gdn_a_log, v_gdn_dt_bias, v_gdn_norm_g, v_gdn_w_out):
    w = dict(ada_w=ada_w, ada_b=ada_b, pre_g=pre_g, post_g=post_g, rel_bias=rel_bias, ab_w_in=ab_w_in, ab_w_out=ab_w_out,
             s5_a_re=s5_a_re, s5_a_im=s5_a_im, s5_log_dt=s5_log_dt, s5_b_re=s5_b_re, s5_b_im=s5_b_im, s5_c_re=s5_c_re, s5_c_im=s5_c_im,
             s5_d=s5_d, s5_glu_w=s5_glu_w, s5_glu_b=s5_glu_b, gdn_w_in=gdn_w_in, gdn_conv=gdn_conv, gdn_a_log=gdn_a_log,
             gdn_dt_bias=gdn_dt_bias, gdn_norm_g=gdn_norm_g, gdn_w_out=gdn_w_out)
    m = dict(ada_w=m_ada_w, ada_b=m_ada_b, pre_g=m_pre_g, post_g=m_post_g, rel_bias=m_rel_bias, ab_w_in=m_ab_w_in, ab_w_out=m_ab_w_out,
             s5_a_re=m_s5_a_re, s5_a_im=m_s5_a_im, s5_log_dt=m_s5_log_dt, s5_b_re=m_s5_b_re, s5_b_im=m_s5_b_im, s5_c_re=m_s5_c_re,
             s5_c_im=m_s5_c_im, s5_d=m_s5_d, s5_glu_w=m_s5_glu_w, s5_glu_b=m_s5_glu_b, gdn_w_in=m_gdn_w_in, gdn_conv=m_gdn_conv,
             gdn_a_log=m_gdn_a_log, gdn_dt_bias=m_gdn_dt_bias, gdn_norm_g=m_gdn_norm_g, gdn_w_out=m_gdn_w_out)
    v = dict(ada_w=v_ada_w, ada_b=v_ada_b, pre_g=v_pre_g, post_g=v_post_g, rel_bias=v_rel_bias, ab_w_in=v_ab_w_in, ab_w_out=v_ab_w_out,
             s5_a_re=v_s5_a_re, s5_a_im=v_s5_a_im, s5_log_dt=v_s5_log_dt, s5_b_re=v_s5_b_re, s5_b_im=v_s5_b_im, s5_c_re=v_s5_c_re,
             s5_c_im=v_s5_c_im, s5_d=v_s5_d, s5_glu_w=v_s5_glu_w, s5_glu_b=v_s5_glu_b, gdn_w_in=v_gdn_w_in, gdn_conv=v_gdn_conv,
             gdn_a_log=v_gdn_a_log, gdn_dt_bias=v_gdn_dt_bias, gdn_norm_g=v_gdn_norm_g, gdn_w_out=v_gdn_w_out)
    ix, iy, ic = _place()
    me = 4 * ix + 2 * iy + ic
    chip = 2 * ix + iy
    n_cols = ada_w.shape[2]

    mine_first = _pack([c, gdn_conv], 32)
    first = _own_slot(_all_gather8("gather_c_conv", mine_first), mine_first, me)
    c_all = first[:, 0:8].reshape(8, D_MODEL)
    conv_full = first[0::2, 8:32].reshape(4, C_CONV, n_cols).transpose(1, 0, 2).reshape(C_CONV, 4 * n_cols)
    mine_mod = _mod_local(c_all, ada_w)
    modl = _own_slot(_all_gather8("gather_mod", mine_mod), mine_mod, me)
    mod = lax.dynamic_index_in_dim(modl[0::2], me, axis=2, keepdims=False)
    mod = mod.transpose(1, 0, 2).reshape(2, 4 * n_cols) + ada_b

    comm = _WeightExchanges({name: w[name][0] for name in _SHARDED}, ic, chip)
    wd = {name: w[name] for name in _SMALL if name != "ada_b"}
    wd = {k: (a if k in ("pre_g", "post_g", "rel_bias") else a[0]) for k, a in wd.items()}
    wd["gdn_conv"] = conv_full
    wd.update(comm.first_weights())

    loss_local, grad_x, grads, dmod = _local_step(x[0], loss_target[0], mod, wd, comm)

    small_shapes = [w[name].shape for name in _SMALL] + [(C_CONV, 4 * n_cols)]
    small_rows = -(-sum(_rows128(int(np.prod(s))) for s in small_shapes) // 64) * 64
    per_dev, dmod_rows = small_rows // 8, _rows128(2 * 3 * D_MODEL)
    partial = _pack([dmod] + [grads[name] for name in _SMALL[1:]] + [grads["gdn_conv"]], small_rows)
    to_all = jnp.concatenate([partial[:dmod_rows], jnp.full((8, 128), loss_local, F32)], axis=0)
    outbound = jnp.concatenate([partial.reshape(8, per_dev, 128), jnp.broadcast_to(to_all[None], (8,) + to_all.shape)], axis=1)
    inbound = _own_slot(_all_to_all8("reduce_small_grads", outbound), lax.dynamic_index_in_dim(outbound, me, 0, keepdims=False), me)
    loss = functools.reduce(lambda a, b: a + b, [inbound[d, per_dev + dmod_rows, 0] for d in range(8)])
    my_rows = _slot_sum("sum_small_grads", inbound[:, :per_dev])
    g_small = _own_slot(_all_gather8("gather_small_grads", my_rows), my_rows, me).reshape(small_rows, 128)
    g_list = _unpack(g_small, small_shapes)
    out_g, out_d, out_m, out_v = {}, {}, {}, {}

    small =list(_SMALL) + ["gdn_conv"]
    small_g = [g.reshape(-1, g.shape[-1]) for g in g_list[:-1]] + [lax.dynamic_slice_in_dim(g_list[-1], chip * n_cols, n_cols, axis=1)]
    two_d = lambda a: a.reshape(-1, a.shape[-1])
    results = _adamw_many("adamw_small", [two_d(w[n]) for n in small], small_g, [two_d(m[n]) for n in small], [two_d(v[n]) for n in small])
    for name, g2d, (d_, m_, v_) in zip(small, small_g, results):
        out_g[name], out_d[name], out_m[name], out_v[name] = (a.reshape(w[name].shape) for a in (g2d, d_, m_, v_))

    from_sibling = _sibling_exchange("reduce_sibling", comm.split_halves({"ab_w_in": grads["ab_w_in"]}))
    rider = (comm.chip_partials(["ab_w_in"], from_sibling), "scatter")
    dmod_all = inbound[:, per_dev:per_dev + dmod_rows].reshape(8, 2, 4, n_cols)
    dmod_cols = lax.dynamic_index_in_dim(dmod_all, chip, axis=2, keepdims=False).transpose(1, 0, 2)
    g_ada = _ada_w_grad(c_all, dmod_cols).reshape(-1, n_cols)
    (d_, m_, v_), got = _adamw("adamw_ada_w", two_d(w["ada_w"]), g_ada, two_d(m["ada_w"]), two_d(v["ada_w"]), exchange=rider)
    out_g["ada_w"], out_d["ada_w"], out_m["ada_w"], out_v["ada_w"] = (a.reshape(w["ada_w"].shape) for a in (g_ada, d_, m_, v_))
    comm.received["ab_w_in"] = got[0]
    chip_1 = jnp.reshape(chip, (1,)).astype(jnp.int32)
    core_1 = jnp.reshape(ic, (1,)).astype(jnp.int32)
    reduced = [_chip_sum("sum_chips_" + name, comm.received[name], comm.partial[name], chip_1) for name in _SHARDED]
    for name, g_mine, g_sib in zip(_SHARDED, reduced, _sibling_exchange("reduce_share", reduced)):
        out_g[name], out_d[name], out_m[name], out_v[name] = _adamw_halves(
            "adamw_" + name, w[name], g_mine, g_sib, m[name], v[name], core_1)

    return (loss, grad_x[None], *[out_g[n] for n in _WEIGHTS], *[out_d[n] for n in _WEIGHTS],
            *[out_m[n] for n in _WEIGHTS], *[out_v[n] for n in _WEIGHTS])
```

```python
import functools
import math

import numpy as np
import jax
import jax.numpy as jnp
from jax import lax
from jax.experimental import pallas as pl
from jax.experimental.pallas import tpu as pltpu

F32 = jnp.float32
BF = jnp.bfloat16
HI = lax.Precision.HIGHEST
MESH = pl.DeviceIdType.MESH

D_MODEL = 1024
EPS = 1e-6
A_HEADS, A_HD, A_WIDTH, A_BLOCK = 8, 64, 512, 128
DILATIONS = (1, 4, 16)
N_KEYS = 128
REL_BUCKETS, REL_MAX_DIST = 32, 2048
B_WIDTH, B_GROUP, B_GROUPS, B_STATE = 512, 16, 32, 64
S5_LANES = 512
S5_TILES = 4
C_HEADS, C_DK, C_CHUNK, C_CONV = 8, 128, 64, 4
QKV = 3072
C_IN = QKV + 1024 + 2 * C_HEADS
C_IN_PAD = 4224
TM = 256
VMEM_LIMIT_BYTES = 56 * 1024 * 1024
ADAM_LR, ADAM_B1, ADAM_B2, ADAM_EPS, ADAM_WD, ADAM_STEP = 0.001, 0.9, 0.999, 1e-08, 0.01, 10
NEG = float(np.finfo(np.float32).min)


def _cp(*sem):
    return pltpu.CompilerParams(dimension_semantics=sem, vmem_limit_bytes=VMEM_LIMIT_BYTES)


def _bdot(a, b):
    return jnp.dot(a.astype(BF), b.astype(BF), preferred_element_type=F32)


def _bdot_nt(a, b):
    return lax.dot_general(a.astype(BF), b.astype(BF), (((1,), (1,)), ((), ())), preferred_element_type=F32)


def _bdot_tn(a, b):
    return lax.dot_general(a.astype(BF), b.astype(BF), (((0,), (0,)), ((), ())), preferred_element_type=F32)


def _hdot(a, b):
    return jnp.dot(a, b, precision=HI, preferred_element_type=F32)


def _bein(eq, a, b):
    return jnp.einsum(eq, a.astype(BF), b.astype(BF), preferred_element_type=F32)


def _row(tm, n):
    return pl.BlockSpec((tm, n), lambda i: (i, 0))


def _fix(shape):
    return pl.BlockSpec(shape, lambda i: (0,) * len(shape))


def _sds(*shape, dtype=F32):
    return jax.ShapeDtypeStruct(shape, dtype)


def _acc(ref, val):
    ref[...] += val


def _zero_at_first(refs, axis=0):
    @pl.when(pl.program_id(axis) == 0)
    def _():
        for r in refs:
            r[...] = jnp.zeros_like(r)


def _rms(x):
    return x * lax.rsqrt(jnp.mean(x * x, axis=-1, keepdims=True) + EPS)


def _pre_mod(x, g, scale, shift):
    return (_rms(x) * g) * (1.0 + scale) + shift


def _post_res(y, x, post_g, gate):
    return x + gate * (_rms(y) * post_g)


def _merge_gate(o1, o2, o3, l1, l2, l3, ga):
    m = jnp.maximum(jnp.maximum(l1, l2), l3)
    e1, e2, e3 = jnp.exp(l1 - m), jnp.exp(l2 - m), jnp.exp(l3 - m)
    inv = 1.0 / (e1 + e2 + e3)
    return ((e1 * inv) * o1 + (e2 * inv) * o2 + (e3 * inv) * o3) * jax.nn.silu(ga)


def _s5_gelu(ypre, u, d_skip):
    return jax.nn.gelu(ypre + d_skip * u)


def _s5_glu(yb, gl, gb):
    return yb * jax.nn.sigmoid(gl) * jax.nn.silu(gb)


def _l0_front(x, pre_g, scale, shift, w_in):
    s_len = x.shape[0]

    def body(x_ref, g_ref, sc_ref, sh_ref, w_ref, *out_refs):
        qkv_refs, (u_ref, ga_ref, gb_ref, h_ref) = out_refs[:9], out_refs[9:]
        hb = _pre_mod(x_ref[...], g_ref[...], sc_ref[...], sh_ref[...]).astype(BF)
        h_ref[...] = hb
        z = jnp.dot(hb, w_ref[...], preferred_element_type=F32)
        for a in range(3):
            piece = z[:, a * 512:(a + 1) * 512]
            for i, d in enumerate(DILATIONS):
                qkv_refs[3 * a + i][...] = _to_res(piece, d).astype(BF)
        u_ref[...] = z[:, 1536:2048]
        ga_ref[...] = z[:, 2048:2560]
        gb_ref[...] = z[:, 2560:3072]

    vec = _fix((1, D_MODEL))
    return pl.pallas_call(
        body, name="l0_front", grid=(s_len // TM,),
        in_specs=[_row(TM, D_MODEL), vec, vec, vec, _fix((D_MODEL, 3072))],
        out_specs=[_res_spec(d) for d in DILATIONS] * 3 + [_row(TM, 512)] * 3 + [_row(TM, D_MODEL)],
        out_shape=[_sds(*_res_shape(s_len, d), dtype=BF) for d in DILATIONS] * 3 + [_sds(s_len, 512)] * 3 + [_sds(s_len, D_MODEL, dtype=BF)],
        compiler_params=_cp("arbitrary"),
    )(x, pre_g, scale, shift, w_in)


def _front_bwd(name, x, pre_g, scale, shift, w_in, dres, parts, widths):
    s_len = x.shape[0]
    n_in = sum(len(p) for p in parts)
    n_cols = sum(widths)

    def body(*refs):
        x_ref, g_ref, sc_ref, sh_ref, w_ref, dres_ref = refs[:6]
        part_refs = refs[6:6 + n_in]
        dz_ref, dx_ref, dg_ref, dsc_ref, dsh_ref = refs[6 + n_in:]
        _zero_at_first([dg_ref, dsc_ref, dsh_ref])
        _, vjp = jax.vjp(_pre_mod, x_ref[...], g_ref[...], sc_ref[...], sh_ref[...])
        dh = jnp.zeros((TM, D_MODEL), F32)
        col, at = 0, 0
        for grp, width in zip(parts, widths):
            tile = lambda r: _from_res(r[...]) if len(r.shape) == 3 else r[...]
            dz = tile(part_refs[at])
            for r in part_refs[at + 1:at + len(grp)]:
                dz = dz + tile(r)
            at += len(grp)
            dzb = dz.astype(BF)
            dz_ref[:, col:col + width] = dzb
            dh = dh + lax.dot_general(dzb, w_ref[:, col:col + width], (((1,), (1,)), ((), ())), preferred_element_type=F32)
            col += width
        dx, dg, dsc, dsh = vjp(dh)
        dx_ref[...] = dx + dres_ref[...]
        _acc(dg_ref, dg)
        _acc(dsc_ref, dsc)
        _acc(dsh_ref, dsh)

    vec = _fix((1, D_MODEL))
    flat = [a for p in parts for a in p]
    return pl.pallas_call(
        body, name=name, grid=(s_len // TM,),
        in_specs=[_row(TM, D_MODEL), vec, vec, vec, _fix((D_MODEL, n_cols)), _row(TM, D_MODEL)]
        + [_res_spec(a.shape[0], a.shape[2]) if a.ndim == 3 else _row(TM, a.shape[1]) for a in flat],
        out_specs=[_row(TM, n_cols), _row(TM, D_MODEL), vec, vec, vec],
        out_shape=[_sds(s_len, n_cols, dtype=BF), _sds(s_len, D_MODEL), _sds(1, D_MODEL), _sds(1, D_MODEL), _sds(1, D_MODEL)],
        compiler_params=_cp("arbitrary"),
    )(x, pre_g, scale, shift, w_in, dres, *flat)


def _matmul_tn(name, a, b, tn):
    s_len, k_dim = a.shape
    n_dim = b.shape[1]
    ts = 512

    def body(a_ref, b_ref, o_ref):
        _zero_at_first([o_ref], axis=1)
        o_ref[...] += lax.dot_general(a_ref[...], b_ref[...], (((0,), (0,)), ((), ())), preferred_element_type=F32)

    return pl.pallas_call(
        body, name=name, grid=(n_dim // tn, s_len // ts),
        in_specs=[pl.BlockSpec((ts, k_dim), lambda j, i: (i, 0)), pl.BlockSpec((ts, tn), lambda j, i: (i, j))],
        out_specs=pl.BlockSpec((k_dim, tn), lambda j, i: (0, j)),
        out_shape=_sds(k_dim, n_dim),
        compiler_params=_cp("arbitrary", "arbitrary"),
    )(a, b)


def _matmul_tn_half(name, a, b, tn, half, exchange=None):
    s_len, k_dim = a.shape
    n_dim = b.shape[1]
    ts, kh = 512, k_dim // 2

    def body(half_ref, a_ref, b_ref, o_ref):
        _zero_at_first([o_ref], axis=1)
        o_ref[...] += lax.dot_general(a_ref[...], b_ref[...], (((0,), (0,)), ((), ())), preferred_element_type=F32)

    outs, got = _call_with_exchange(
        body, name, (n_dim // tn, s_len // ts),
        [pl.BlockSpec((ts, kh), lambda j, i, c: (i, c[0])), pl.BlockSpec((ts, tn), lambda j, i, c: (i, j))],
        [pl.BlockSpec((None, kh, tn), lambda j, i, c: (j, 0, 0))], [_sds(n_dim // tn, kh, tn)], [], (a, b), exchange, prefetch=half)
    return outs[0], got


def _t5_bucket_np(dist):
    dist = np.maximum(dist, 0)
    max_exact = REL_BUCKETS // 2
    large = max_exact + (np.log(np.maximum(dist, 1) / max_exact)
                         / math.log(REL_MAX_DIST / max_exact) * (REL_BUCKETS - max_exact)).astype(np.int32)
    large = np.minimum(large, REL_BUCKETS - 1)
    return np.where(dist < max_exact, dist, large).astype(np.int32)


def _to_res(z, dil):
    if dil == 1:
        return z[None]
    return jnp.swapaxes(z.reshape(z.shape[0] // dil, dil, z.shape[1]), 0, 1)


def _from_res(z):
    if z.shape[0] == 1:
        return z[0]
    return jnp.swapaxes(z, 0, 1).reshape(z.shape[0] * z.shape[1], z.shape[2])


def _res_shape(s_len, dil, width=A_WIDTH):
    return (dil, s_len // dil, width)


def _res_spec(dil, width=A_WIDTH):
    return pl.BlockSpec((dil, TM // dil, width), lambda i: (0, i, 0))


def _bucket_table():
    qi = np.arange(A_BLOCK)[:, None]
    kj = np.arange(2 * A_BLOCK)[None, :]
    return np.stack([_t5_bucket_np((qi + A_BLOCK - kj) * d) for d in DILATIONS], 0)


def _attn_mask(first):
    qi = lax.broadcasted_iota(jnp.int32, (A_BLOCK, 2 * A_BLOCK), 0)
    kj = lax.broadcasted_iota(jnp.int32, (A_BLOCK, 2 * A_BLOCK), 1)
    rel = qi + A_BLOCK - kj
    return (rel >= 0) & (rel <= N_KEYS) & (jnp.logical_not(first) | (kj >= A_BLOCK))


def _attn_specs(nb, rev):
    per = 2 if nb % 2 == 0 else 1
    steps = nb // per
    n_of = (lambda i: steps - 1 - i) if rev else (lambda i: i)
    cur = pl.BlockSpec((None, per * A_BLOCK, A_WIDTH), lambda r, i: (r, n_of(i), 0))
    prev = pl.BlockSpec((None, A_BLOCK, A_WIDTH), lambda r, i: (r, jnp.maximum(per * n_of(i) - 1, 0), 0))
    bias = pl.BlockSpec((A_HEADS, A_BLOCK, 2 * A_BLOCK), lambda r, i: (0, 0, 0))
    return per, steps, cur, prev, bias


def _attn_fwd(q, k, v, bias, exchange=None):
    dil, t_len, _ = q.shape
    per, steps, cur, prev, bias_spec = _attn_specs(t_len // A_BLOCK, False)
    scale = A_HD ** -0.5

    def body(q_ref, kp_ref, kc_ref, vp_ref, vc_ref, b_ref, o_ref, l_ref):
        lane = lax.broadcasted_iota(jnp.int32, (1, 128), 1)
        for sub in range(per):
            rows = slice(sub * A_BLOCK, (sub + 1) * A_BLOCK)
            before = slice((sub - 1) * A_BLOCK, sub * A_BLOCK)
            mask = _attn_mask((pl.program_id(1) == 0) if sub == 0 else False)
            for hp in range(A_HEADS // 2):
                sl = slice(hp * 128, (hp + 1) * 128)
                qp = q_ref[rows, sl]
                kw = jnp.concatenate([kp_ref[:, sl] if sub == 0 else kc_ref[before, sl], kc_ref[rows, sl]], axis=0).astype(BF)
                vw = jnp.concatenate([vp_ref[:, sl] if sub == 0 else vc_ref[before, sl], vc_ref[rows, sl]], axis=0).astype(BF)
                outs, lses = [], []
                for j in range(2):
                    hm = (lane < 64) if j == 0 else (lane >= 64)
                    s = _bdot_nt(jnp.where(hm, qp, 0.0), kw) * scale
                    s = jnp.where(mask, s + b_ref[2 * hp + j], NEG)
                    m = jnp.max(s, axis=-1, keepdims=True)
                    p = jnp.exp(s - m)
                    den = jnp.sum(p, axis=-1, keepdims=True)
                    outs.append(_bdot(p, vw) / den)
                    lses.append(m + jnp.log(den))
                hm0 = lane < 64
                o_ref[rows, sl] = jnp.where(hm0, outs[0], outs[1])
                l_ref[rows, sl] = jnp.where(hm0, lses[0], lses[1])

    return _call_with_exchange(body, f"attn_fwd_d{dil}", (dil, steps), [cur, prev, cur, prev, cur, bias_spec], [cur, cur],
                               [_sds(dil, t_len, A_WIDTH)] * 2, [], (q, k, k, v, v, bias), exchange)


def _attn_bwd(q, k, v, bias, o, l, do, dl, exchange=None):
    dil, t_len, _ = q.shape
    per, steps, cur, prev, bias_spec = _attn_specs(t_len // A_BLOCK, True)
    scale = A_HD ** -0.5

    def body(q_ref, kp_ref, kc_ref, vp_ref, vc_ref, b_ref, o_ref, l_ref, do_ref, dl_ref,
             dq_ref, dk_ref, dv_ref, db_ref, ck_ref, cv_ref):
        _zero_at_first([ck_ref, cv_ref], axis=1)

        @pl.when((pl.program_id(0) == 0) & (pl.program_id(1) == 0))
        def _():
            db_ref[...] = jnp.zeros_like(db_ref)

        lane = lax.broadcasted_iota(jnp.int32, (1, 128), 1)
        for hp in range(A_HEADS // 2):
            sl = slice(hp * 128, (hp + 1) * 128)
            to_prev_k, to_prev_v = ck_ref[:, sl], cv_ref[:, sl]
            for sub in range(per - 1, -1, -1):
                rows = slice(sub * A_BLOCK, (sub + 1) * A_BLOCK)
                before = slice((sub - 1) * A_BLOCK, sub * A_BLOCK)
                mask = _attn_mask((pl.program_id(1) == steps - 1) if sub == 0 else False)
                qp = q_ref[rows, sl]
                kw = jnp.concatenate([kp_ref[:, sl] if sub == 0 else kc_ref[before, sl], kc_ref[rows, sl]], axis=0).astype(BF)
                vw = jnp.concatenate([vp_ref[:, sl] if sub == 0 else vc_ref[before, sl], vc_ref[rows, sl]], axis=0).astype(BF)
                op, lp, dop, dlp = o_ref[rows, sl], l_ref[rows, sl], do_ref[rows, sl], dl_ref[rows, sl]
                dq_acc = jnp.zeros((A_BLOCK, 128), F32)
                dk_acc = jnp.zeros((2 * A_BLOCK, 128), F32)
                dv_acc = jnp.zeros((2 * A_BLOCK, 128), F32)
                for j in range(2):
                    hm = (lane < 64) if j == 0 else (lane >= 64)
                    qm = jnp.where(hm, qp, 0.0)
                    s = _bdot_nt(qm, kw) * scale
                    s = jnp.where(mask, s + b_ref[2 * hp + j], NEG)
                    lse = jnp.max(jnp.where(hm, lp, NEG), axis=-1, keepdims=True)
                    p = jnp.exp(s - lse)
                    do_h = jnp.where(hm, dop, 0.0)
                    dd = jnp.sum(do_h * op, axis=-1, keepdims=True)
                    dlse = jnp.sum(jnp.where(hm, dlp, 0.0), axis=-1, keepdims=True)
                    ds = p * (_bdot_nt(do_h, vw) - dd + dlse)
                    dv_acc = dv_acc + _bdot_tn(p, do_h)
                    dq_acc = dq_acc + jnp.where(hm, _bdot(ds, kw), 0.0) * scale
                    dk_acc = dk_acc + _bdot_tn(ds, qm) * scale
                    db_ref[2 * hp + j] += ds
                dq_ref[rows, sl] = dq_acc
                dk_ref[rows, sl] = dk_acc[A_BLOCK:] + to_prev_k
                dv_ref[rows, sl] = dv_acc[A_BLOCK:] + to_prev_v
                to_prev_k, to_prev_v = dk_acc[:A_BLOCK], dv_acc[:A_BLOCK]
            ck_ref[:, sl] = to_prev_k
            cv_ref[:, sl] = to_prev_v

    return _call_with_exchange(
        body, f"attn_bwd_d{dil}", (dil, steps), [cur, prev, cur, prev, cur, bias_spec, cur, cur, cur, cur],
        [cur, cur, cur, bias_spec], [_sds(dil, t_len, A_WIDTH)] * 3 + [_sds(A_HEADS, A_BLOCK, 2 * A_BLOCK)],
        [pltpu.VMEM((A_BLOCK, A_WIDTH), F32)] * 2, (q, k, k, v, v, bias, o, l, do, dl), exchange)


def _attn_bias(rel_bias, table):
    def body(rb_ref, t_ref, *o_refs):
        for c in range(3):
            t = t_ref[c]
            acc = [jnp.zeros((A_BLOCK, 2 * A_BLOCK), F32) for _ in range(A_HEADS)]
            for b in range(REL_BUCKETS):
                hit = t == b
                acc = [jnp.where(hit, rb_ref[b, h], acc[h]) for h in range(A_HEADS)]
            for h in range(A_HEADS):
                o_refs[c][h] = acc[h]

    return pl.pallas_call(body, name="attn_bias", out_shape=[_sds(A_HEADS, A_BLOCK, 2 * A_BLOCK)] * 3,
                          in_specs=[pl.BlockSpec(memory_space=pltpu.SMEM), pl.BlockSpec(memory_space=pltpu.VMEM)],
                          compiler_params=pltpu.CompilerParams(vmem_limit_bytes=VMEM_LIMIT_BYTES))(rel_bias, table)


def _rel_bias_grad(dbs, idx_rows):
    n = A_BLOCK * 2 * A_BLOCK

    def body(d0_ref, d1_ref, d2_ref, idx_ref, o_ref):
        bucket = lax.broadcasted_iota(jnp.int32, (REL_BUCKETS, n), 0).astype(F32)
        acc = jnp.zeros((A_HEADS, REL_BUCKETS), F32)
        for c, db_ref in enumerate((d0_ref, d1_ref, d2_ref)):
            onehot = (idx_ref[c:c + 1, :] == bucket).astype(F32)
            acc = acc + lax.dot_general(db_ref[...], onehot, (((1,), (1,)), ((), ())), precision=HI, preferred_element_type=F32)
        o_ref[...] = acc

    return pl.pallas_call(body, name="rel_bias_grad", out_shape=_sds(A_HEADS, REL_BUCKETS),
                          compiler_params=pltpu.CompilerParams(vmem_limit_bytes=VMEM_LIMIT_BYTES))(
                              *[d.reshape(A_HEADS, n) for d in dbs], idx_rows)


def _s5_param_fn(a_re, a_im, log_dt, bt_re, bt_im):
    dt = jnp.exp(log_dt)
    mag = jnp.exp(dt * a_re)
    abar_r, abar_i = mag * jnp.cos(dt * a_im), mag * jnp.sin(dt * a_im)
    den = a_re * a_re + a_im * a_im
    fr = ((abar_r - 1.0) * a_re + abar_i * a_im) / den
    fi = (abar_i * a_re - (abar_r - 1.0) * a_im) / den
    row = lax.broadcasted_iota(jnp.int32, (B_WIDTH, B_GROUPS), 0)
    grp = lax.broadcasted_iota(jnp.int32, (B_WIDTH, B_GROUPS), 1)
    expand = ((row // B_GROUP) == grp).astype(F32)
    fr_e, fi_e = _hdot(expand, fr), _hdot(expand, fi)
    return abar_r, abar_i, fr_e * bt_re - fi_e * bt_im, fr_e * bt_im + fi_e * bt_re


def _s5_params(a_re, a_im, log_dt, bt_re, bt_im):
    def body(ar, ai, ld, br, bi, o1, o2, o3, o4):
        o1[...], o2[...], o3[...], o4[...] = _s5_param_fn(ar[...], ai[...], ld[...], br[...], bi[...])

    return pl.pallas_call(body, name="s5_params",
                          out_shape=[_sds(B_GROUPS, B_STATE)] * 2 + [_sds(B_WIDTH, B_STATE)] * 2)(a_re, a_im, log_dt, bt_re, bt_im)


def _s5_params_bwd(a_re, a_im, log_dt, bt_re, bt_im, d1, d2, d3, d4):
    def body(ar, ai, ld, br, bi, c1, c2, c3, c4, o1, o2, o3, o4, o5):
        _, vjp = jax.vjp(_s5_param_fn, ar[...], ai[...], ld[...], br[...], bi[...])
        o1[...], o2[...], o3[...], o4[...], o5[...] = vjp((c1[...], c2[...], c3[...], c4[...]))

    return pl.pallas_call(body, name="s5_params_bwd",
                          out_shape=[_sds(B_GROUPS, B_STATE)] * 2 + [_sds(B_GROUPS, 1)] + [_sds(B_WIDTH, B_STATE)] * 2,
                          )(a_re, a_im, log_dt, bt_re, bt_im, d1, d2, d3, d4)


def _pick_row(x, r):
    rows = lax.broadcasted_iota(jnp.int32, x.shape, 0)
    return jnp.sum(jnp.where(rows == r, x, 0.0), axis=0, keepdims=True)


S5_SEG = 8
S5_STEPS = 32
S5_WIDTH = S5_TILES * S5_LANES


def _seg_rows(block):
    return jnp.swapaxes(block, 0, 1).reshape(block.shape[1] * S5_SEG, block.shape[2])


def _seg_block(rows):
    return jnp.swapaxes(rows.reshape(rows.shape[0] // S5_SEG, S5_SEG, rows.shape[1]), 0, 1)


def _seq_specs(n_i, rev):
    at = (lambda i: n_i - 1 - i) if rev else (lambda i: i)
    seg = pl.BlockSpec((S5_SEG, S5_STEPS, B_WIDTH), lambda i: (0, at(i), 0))
    x_spec = pl.BlockSpec((S5_SEG * S5_STEPS, S5_WIDTH), lambda i: (at(i), 0))
    return seg, x_spec, _fix((S5_TILES, 128, S5_LANES)), _fix((S5_TILES, S5_LANES, 128)), _fix((1, S5_WIDTH)), _fix((S5_SEG, S5_WIDTH))


def _tile_dots(dot, lhs, w_ref, lhs_width):
    return jnp.concatenate([dot(lhs[:, t * lhs_width:(t + 1) * lhs_width], w_ref[t]) for t in range(S5_TILES)], axis=1)


def _s5_entries(name, end_r, end_i, abr, abi, steps, reverse):
    def body(er_ref, ei_ref, ar_ref, ai_ref, or_ref, oi_ref):
        pr, pi_ = ar_ref[...], ai_ref[...]
        for _ in range(int(math.log2(steps))):
            pr, pi_ = pr * pr - pi_ * pi_, 2.0 * pr * pi_
        er, ei = er_ref[...], ei_ref[...]
        rows = lax.broadcasted_iota(jnp.int32, er.shape, 0)
        cr, ci = jnp.zeros_like(pr), jnp.zeros_like(pr)
        out_r, out_i = jnp.zeros_like(er), jnp.zeros_like(er)
        for g in (range(S5_SEG - 2, -1, -1) if reverse else range(1, S5_SEG)):
            src = g + 1 if reverse else g - 1
            cr, ci = _pick_row(er, src) + pr * cr - pi_ * ci, _pick_row(ei, src) + pr * ci + pi_ * cr
            out_r, out_i = jnp.where(rows == g, cr, out_r), jnp.where(rows == g, ci, out_i)
        or_ref[...] = out_r
        oi_ref[...] = out_i

    return pl.pallas_call(body, name=name, out_shape=[_sds(*end_r.shape)] * 2)(end_r, end_i, abr, abi)


def _s5_seq_fwd(u, btr, bti, ctr, cti, abr, abi, entry, store, exchange=None):
    s_len = u.shape[0]
    seg_len = s_len // S5_SEG
    n_i = seg_len // S5_STEPS
    rows = S5_SEG * S5_STEPS

    def body(u_ref, btr_ref, bti_ref, ctr_ref, cti_ref, ar_ref, ai_ref, er_ref, ei_ref, *rest):
        if store:
            xr_ref, xi_ref, y_ref, endr_ref, endi_ref, sr_ref, si_ref = rest
        else:
            endr_ref, endi_ref, sr_ref, si_ref = rest
        i = pl.program_id(0)

        @pl.when(i == 0)
        def _():
            sr_ref[...] = er_ref[...]
            si_ref[...] = ei_ref[...]

        ar = jnp.broadcast_to(ar_ref[...], (S5_SEG, S5_WIDTH))
        ai = jnp.broadcast_to(ai_ref[...], (S5_SEG, S5_WIDTH))
        ub = _seg_rows(u_ref[...])
        br, bi = _tile_dots(_bdot, ub, btr_ref, 128), _tile_dots(_bdot, ub, bti_ref, 128)
        sr, si = sr_ref[...], si_ref[...]
        for s in range(S5_STEPS):
            at = slice(S5_SEG * s, S5_SEG * (s + 1))
            sr, si = ar * sr - ai * si + br[at], ar * si + ai * sr + bi[at]
            if store:
                xr_ref[at, :] = sr
                xi_ref[at, :] = si
        sr_ref[...] = sr
        si_ref[...] = si
        if store:
            y_ref[...] = _seg_block(_tile_dots(_bdot, xr_ref[...], ctr_ref, S5_LANES) - _tile_dots(_bdot, xi_ref[...], cti_ref, S5_LANES))

        @pl.when(i == n_i - 1)
        def _():
            endr_ref[...] = sr
            endi_ref[...] = si

    seg, x_spec, b_spec, c_spec, a_spec, e_spec = _seq_specs(n_i, False)
    ends = [_sds(S5_SEG, S5_WIDTH)] * 2
    full = [_sds(s_len, S5_WIDTH)] * 2 + [_sds(S5_SEG, seg_len, B_WIDTH)] if store else []
    return _call_with_exchange(
        body, "s5_scan_fwd" if store else "s5_ends_fwd", (n_i,),
        [seg, b_spec, b_spec, c_spec, c_spec, a_spec, a_spec, e_spec, e_spec],
        ([x_spec, x_spec, seg] if store else []) + [e_spec, e_spec], full + ends,
        [pltpu.VMEM((S5_SEG, S5_WIDTH), F32)] * 2,
        (u.reshape(S5_SEG, seg_len, B_WIDTH), btr, bti, ctr, cti, abr, abi, *entry), exchange)


def _s5_seq_bwd(dy, xr, xi, u, btr, bti, ctr, cti, abr, abi, g_entry, x_entry, full, exchange=None):
    s_len = dy.shape[0]
    seg_len = s_len // S5_SEG
    n_i = seg_len // S5_STEPS
    rows = S5_SEG * S5_STEPS

    def body(*refs):
        if full:
            (dy_ref, btr_ref, bti_ref, ctr_ref, cti_ref, ar_ref, ai_ref, ger_ref, gei_ref,
             xr_ref, xi_ref, xrp_ref, xip_ref, xer_ref, xei_ref, u_ref,
             du_ref, dbtr_ref, dbti_ref, dctr_ref, dcti_ref, dar_ref, dai_ref, str_ref, sti_ref,
             sr_ref, si_ref, gr_s, gi_s) = refs
        else:
            (dy_ref, btr_ref, bti_ref, ctr_ref, cti_ref, ar_ref, ai_ref, ger_ref, gei_ref, str_ref, sti_ref, sr_ref, si_ref) = refs
        i = pl.program_id(0)

        @pl.when(i == 0)
        def _():
            sr_ref[...] = ger_ref[...]
            si_ref[...] = gei_ref[...]
            if full:
                for r in (dbtr_ref, dbti_ref, dctr_ref, dcti_ref, dar_ref, dai_ref):
                    r[...] = jnp.zeros_like(r)

        ar = jnp.broadcast_to(ar_ref[...], (S5_SEG, S5_WIDTH))
        ai = -jnp.broadcast_to(ai_ref[...], (S5_SEG, S5_WIDTH))
        dyb = _seg_rows(dy_ref[...])
        gr, gi = _tile_dots(_bdot_nt, dyb, ctr_ref, 128), -_tile_dots(_bdot_nt, dyb, cti_ref, 128)
        sr, si = sr_ref[...], si_ref[...]
        for s in range(S5_STEPS - 1, -1, -1):
            at = slice(S5_SEG * s, S5_SEG * (s + 1))
            sr, si = ar * sr - ai * si + gr[at], ar * si + ai * sr + gi[at]
            if full:
                gr_s[at, :] = sr
                gi_s[at, :] = si
        sr_ref[...] = sr
        si_ref[...] = si

        @pl.when(i == n_i - 1)
        def _():
            str_ref[...] = sr
            sti_ref[...] = si

        if full:
            g_r, g_i = gr_s[...], gi_s[...]
            du_ref[...] = _seg_block(_tile_dots(_bdot_nt, g_r, btr_ref, S5_LANES) + _tile_dots(_bdot_nt, g_i, bti_ref, S5_LANES))
            ub = _seg_rows(u_ref[...])
            xr_b, xi_b = xr_ref[...], xi_ref[...]
            for t in range(S5_TILES):
                lanes, cols = slice(t * S5_LANES, (t + 1) * S5_LANES), slice(t * 128, (t + 1) * 128)
                dbtr_ref[t] += _bdot_tn(ub[:, cols], g_r[:, lanes])
                dbti_ref[t] += _bdot_tn(ub[:, cols], g_i[:, lanes])
                dctr_ref[t] += _bdot_tn(xr_b[:, lanes], dyb[:, cols])
                dcti_ref[t] -= _bdot_tn(xi_b[:, lanes], dyb[:, cols])
            first = i == n_i - 1
            xpr = jnp.concatenate([jnp.where(first, xer_ref[...], xrp_ref[...]), xr_b[:rows - S5_SEG]], axis=0)
            xpi = jnp.concatenate([jnp.where(first, xei_ref[...], xip_ref[...]), xi_b[:rows - S5_SEG]], axis=0)
            dar_ref[...] += jnp.sum(g_r * xpr + g_i * xpi, axis=0, keepdims=True)
            dai_ref[...] += jnp.sum(g_i * xpr - g_r * xpi, axis=0, keepdims=True)

    seg, x_spec, b_spec, c_spec, a_spec, e_spec = _seq_specs(n_i, True)
    halo = pl.BlockSpec((S5_SEG, S5_WIDTH), lambda i: (jnp.maximum((n_i - 1 - i) * S5_STEPS - 1, 0), 0))
    starts = [_sds(S5_SEG, S5_WIDTH)] * 2
    in_specs = [seg, b_spec, b_spec, c_spec, c_spec, a_spec, a_spec, e_spec, e_spec]
    args = [dy.reshape(S5_SEG, seg_len, B_WIDTH), btr, bti, ctr, cti, abr, abi, *g_entry]
    state = [pltpu.VMEM((S5_SEG, S5_WIDTH), F32)] * 2
    if not full:
        return _call_with_exchange(body, "s5_starts_bwd", (n_i,), in_specs, [e_spec, e_spec], starts, state, args, None)
    return _call_with_exchange(
        body, "s5_scan_bwd", (n_i,),
        in_specs + [x_spec, x_spec, halo, halo, e_spec, e_spec, seg],
        [seg, b_spec, b_spec, c_spec, c_spec, a_spec, a_spec, e_spec, e_spec],
        [_sds(S5_SEG, seg_len, B_WIDTH)] + [_sds(S5_TILES, 128, S5_LANES)] * 2 + [_sds(S5_TILES, S5_LANES, 128)] * 2
        + [_sds(1, S5_WIDTH)] * 2 + starts,
        state + [pltpu.VMEM((rows, S5_WIDTH), F32)] * 2,
        args + [xr, xi, xr, xi, *x_entry, u.reshape(S5_SEG, seg_len, B_WIDTH)], exchange)


def _blockdiag_b(bbar_t):
    blocks = bbar_t.reshape(S5_TILES, 8, B_GROUP, B_STATE)
    return jnp.einsum('jgmp,gh->jgmhp', blocks, jnp.eye(8, dtype=F32)).reshape(S5_TILES, 128, S5_LANES)


def _blockdiag_b_t(d):
    return jnp.einsum('jgmgp->jgmp', d.reshape(S5_TILES, 8, B_GROUP, 8, B_STATE)).reshape(B_WIDTH, B_STATE)


def _blockdiag_c(c):
    blocks = c.reshape(S5_TILES, 8, B_GROUP, B_STATE)
    return jnp.einsum('jgmp,gh->jhpgm', blocks, jnp.eye(8, dtype=F32)).reshape(S5_TILES, S5_LANES, 128)


def _blockdiag_c_t(d):
    return jnp.einsum('jgpgm->jgmp', d.reshape(S5_TILES, 8, B_STATE, 8, B_GROUP)).reshape(B_GROUPS, B_GROUP, B_STATE)


def _l0_out(os, ls, ga, gb, ypre, u, x, d_skip, glu_w, glu_b, w_out, post_g, gate, exchange=None):
    s_len = x.shape[0]

    def body(o0, o1, o2, l0, l1, l2, ga_ref, gb_ref, yp_ref, u_ref, x_ref, d_ref, gw_ref, gbias_ref, w_ref, pg_ref, gt_ref, x1_ref, y_ref):
        oa = _merge_gate(*[_from_res(r[...]) for r in (o0, o1, o2, l0, l1, l2)], ga_ref[...])
        yb = _s5_gelu(yp_ref[...], u_ref[...], d_ref[...])
        ob = _s5_glu(yb, _bdot(yb, gw_ref[...]) + gbias_ref[...], gb_ref[...])
        y = _bdot(oa, w_ref[0:512, :]) + _bdot(ob, w_ref[512:1024, :])
        y_ref[...] = y
        x1_ref[...] = _post_res(y, x_ref[...], pg_ref[...], gt_ref[...])

    vec, half = _fix((1, D_MODEL)), _fix((1, 512))
    return _call_with_exchange(
        body, "l0_out", (s_len // TM,),
        [_res_spec(d) for d in DILATIONS] * 2 + [_row(TM, 512)] * 4
        + [_row(TM, D_MODEL), half, _fix((512, 512)), half, _fix((D_MODEL, D_MODEL)), vec, vec],
        [_row(TM, D_MODEL)] * 2, [_sds(s_len, D_MODEL)] * 2, [],
        (*os, *ls, ga, gb, ypre, u, x, d_skip, glu_w, glu_b, w_out, post_g, gate), exchange)


def _l0_out_bwd(os, ls, ga, gb, ypre, u, x, y, d_skip, glu_w, glu_b, w_out, post_g, gate, dx1, exchange=None):
    s_len = x.shape[0]

    def body(o0, o1, o2, l0, l1, l2, ga_ref, gb_ref, yp_ref, u_ref, x_ref, y_ref, d_ref, gw_ref, gbias_ref, w_ref, pg_ref, gt_ref, dx1_ref,
             do0, do1, do2, dl0, dl1, dl2, dga_ref, dgb_ref, dyp_ref, du_ref, dd_ref, dgw_ref, dgbias_ref, dw_ref, dpg_ref, dgt_ref):
        _zero_at_first([dd_ref, dgw_ref, dgbias_ref, dw_ref, dpg_ref, dgt_ref])
        _, vjp2 = jax.vjp(_post_res, y_ref[...], x_ref[...], pg_ref[...], gt_ref[...])
        dy, _, dpg, dgt = vjp2(dx1_ref[...])
        _acc(dpg_ref, dpg)
        _acc(dgt_ref, dgt)
        oa, vjp_a = jax.vjp(_merge_gate, *[_from_res(r[...]) for r in (o0, o1, o2, l0, l1, l2)], ga_ref[...])
        yb, vjp_g = jax.vjp(_s5_gelu, yp_ref[...], u_ref[...], d_ref[...])
        gl = _bdot(yb, gw_ref[...]) + gbias_ref[...]
        ob, vjp_b = jax.vjp(_s5_glu, yb, gl, gb_ref[...])
        dw_ref[0:512, :] += _bdot_tn(oa, dy)
        dw_ref[512:1024, :] += _bdot_tn(ob, dy)
        d1, d2, d3, e1, e2, e3, dga = vjp_a(_bdot_nt(dy, w_ref[0:512, :]))
        for ref, val, d in zip((do0, do1, do2, dl0, dl1, dl2), (d1, d2, d3, e1, e2, e3), DILATIONS * 2):
            ref[...] = _to_res(val, d)
        dga_ref[...] = dga
        dyb, dgl, dgb = vjp_b(_bdot_nt(dy, w_ref[512:1024, :]))
        dgb_ref[...] = dgb
        dgw_ref[...] += _bdot_tn(yb, dgl)
        _acc(dgbias_ref, jnp.sum(dgl, axis=0, keepdims=True))
        dyp, du, dd = vjp_g(dyb + _bdot_nt(dgl, gw_ref[...]))
        dyp_ref[...] = dyp
        du_ref[...] = du
        _acc(dd_ref, dd)

    vec, half = _fix((1, D_MODEL)), _fix((1, 512))
    r5, r10 = _row(TM, 512), _row(TM, D_MODEL)
    res6 = [_res_spec(d) for d in DILATIONS] * 2
    return _call_with_exchange(
        body, "l0_out_bwd", (s_len // TM,),
        res6 + [r5] * 4 + [r10, r10, half, _fix((512, 512)), half, _fix((D_MODEL, D_MODEL)), vec, vec, r10],
        res6 + [r5] * 4 + [half, _fix((512, 512)), half, _fix((D_MODEL, D_MODEL)), vec, vec],
        [_sds(*_res_shape(s_len, d)) for d in DILATIONS] * 2 + [_sds(s_len, 512)] * 4
        + [_sds(1, 512), _sds(512, 512), _sds(1, 512), _sds(D_MODEL, D_MODEL), _sds(1, D_MODEL), _sds(1, D_MODEL)],
        [], (*os, *ls, ga, gb, ypre, u, x, y, d_skip, glu_w, glu_b, w_out, post_g, gate, dx1), exchange)


def _l1_front(x, pre_g, scale, shift, w_in):
    s_len = x.shape[0]

    def body(x_ref, g_ref, sc_ref, sh_ref, w_ref, raw_ref, gate_ref, ba_ref, h_ref):
        hb = _pre_mod(x_ref[...], g_ref[...], sc_ref[...], sh_ref[...]).astype(BF)
        h_ref[...] = hb
        z = jnp.dot(hb, w_ref[...], preferred_element_type=F32)
        raw_ref[...] = z[:, 0:QKV]
        gate_ref[...] = z[:, QKV:QKV + 1024]
        ba_ref[...] = z[:, QKV + 1024:C_IN_PAD]

    vec = _fix((1, D_MODEL))
    return pl.pallas_call(
        body, name="l1_front", grid=(s_len // TM,),
        in_specs=[_row(TM, D_MODEL), vec, vec, vec, _fix((D_MODEL, C_IN_PAD))],
        out_specs=[_row(TM, QKV), _row(TM, 1024), _row(TM, 128), _row(TM, D_MODEL)],
        out_shape=[_sds(s_len, QKV), _sds(s_len, 1024), _sds(s_len, 128), _sds(s_len, D_MODEL, dtype=BF)],
        compiler_params=_cp("arbitrary"),
    )(x, pre_g, scale, shift, w_in)


def _bg_fn(ba, alog_row, dtb_row):
    lane = lax.broadcasted_iota(jnp.int32, (1, 128), 1)
    g = -jnp.exp(alog_row) * jax.nn.softplus(ba + dtb_row)
    return jnp.where(lane < C_HEADS, jax.nn.sigmoid(ba), jnp.where(lane < 2 * C_HEADS, g, 0.0))


def _act_q(c):
    q = jax.nn.silu(c)
    return q * lax.rsqrt(jnp.sum(q * q, axis=-1, keepdims=True) + EPS) * (C_DK ** -0.5)


def _act_k(c):
    k = jax.nn.silu(c)
    return k * lax.rsqrt(jnp.sum(k * k, axis=-1, keepdims=True) + EPS)


def _act_of(s):
    return _act_q if s < 8 else (_act_k if s < 16 else jax.nn.silu)


def _conv_taps(prev8, tile_ref, sl, next8=None):
    rows = tile_ref.shape[0]
    head = jnp.concatenate([prev8, tile_ref[0:8, sl]], axis=0)
    tail = None if next8 is None else jnp.concatenate([tile_ref[rows - 8:rows, sl], next8], axis=0)
    taps = []
    for j in range(C_CONV):
        shift = C_CONV - 1 - j
        pieces = [head[8:] if shift == 0 else pltpu.roll(head, shift, 0)[8:], tile_ref[pl.ds(8 - shift, rows - 8), sl]]
        if tail is not None:
            pieces.append(tail[8:] if shift == 0 else pltpu.roll(tail, shift, 0)[8:])
        taps.append(jnp.concatenate(pieces, axis=0))
    return taps


def _gdn_prep(raw, ba, conv_w, alog_row, dtb_row):
    s_len = raw.shape[0]

    def body(raw_ref, halo_ref, ba_ref, w_ref, al_ref, dt_ref, qkv_ref, bg_ref):
        bg_ref[...] = _bg_fn(ba_ref[...], al_ref[...], dt_ref[...])
        has_prev = (pl.program_id(0) > 0).astype(F32)
        for s in range(24):
            sl = slice(s * 128, (s + 1) * 128)
            taps = _conv_taps(halo_ref[:, sl] * has_prev, raw_ref, sl)
            conv = w_ref[3:4, sl] * taps[3]
            for j in range(3):
                conv = conv + w_ref[j:j + 1, sl] * taps[j]
            qkv_ref[:, sl] = _act_of(s)(conv)

    halo = pl.BlockSpec((8, QKV), lambda i: (jnp.maximum(i * (TM // 8) - 1, 0), 0))
    row128 = _fix((1, 128))
    return pl.pallas_call(
        body, name="gdn_prep", grid=(s_len // TM,),
        in_specs=[_row(TM, QKV), halo, _row(TM, 128), _fix((C_CONV, QKV)), row128, row128],
        out_specs=[_row(TM, QKV), _row(TM, 128)],
        out_shape=[_sds(s_len, QKV), _sds(s_len, 128)],
        compiler_params=_cp("arbitrary"),
    )(raw, raw, ba, conv_w, alog_row, dtb_row)


def _gdn_prep_bwd(raw, ba, conv_w, alog_row, dtb_row, dq, dk, dv, dbg):
    s_len = raw.shape[0]
    n_tiles = s_len // TM

    def body(raw_ref, prev_ref, next_ref, ba_ref, w_ref, al_ref, dt_ref, dq_ref, dqn_ref, dk_ref, dkn_ref, dv_ref, dvn_ref, dbg_ref,
             draw_ref, dba_ref, dw_ref, dal_ref, ddt_ref, dconv_ref):
        _zero_at_first([dw_ref, dal_ref, ddt_ref])
        i = pl.program_id(0)
        _, vjp_bg = jax.vjp(_bg_fn, ba_ref[...], al_ref[...], dt_ref[...])
        dba, dal, ddt = vjp_bg(dbg_ref[...])
        dba_ref[...] = dba
        _acc(dal_ref, dal)
        _acc(ddt_ref, ddt)
        has_prev = (i > 0).astype(F32)
        has_next = (i < n_tiles - 1).astype(F32)
        ct_refs = ((dq_ref, dqn_ref), (dk_ref, dkn_ref), (dv_ref, dvn_ref))
        for s in range(24):
            sl = slice(s * 128, (s + 1) * 128)
            hl = slice((s % 8) * 128, (s % 8 + 1) * 128)
            tile_ref, nxt_ref = ct_refs[s // 8]
            taps = _conv_taps(prev_ref[:, sl] * has_prev, raw_ref, sl, next_ref[:, sl] * has_next)
            conv = w_ref[3:4, sl] * taps[3]
            for j in range(3):
                conv = conv + w_ref[j:j + 1, sl] * taps[j]
            ct = jnp.concatenate([tile_ref[:, hl], nxt_ref[:, hl] * has_next], axis=0)
            _, vjp_act = jax.vjp(_act_of(s), conv)
            dconv, = vjp_act(ct)
            dconv_ref[...] = dconv
            draw = w_ref[3:4, sl] * dconv[:TM]
            for j in range(3):
                draw = draw + w_ref[j:j + 1, sl] * dconv_ref[pl.ds(3 - j, TM), :]
            draw_ref[:, sl] = draw
            for j in range(4):
                dw_ref[j:j + 1, sl] += jnp.sum(dconv[:TM] * taps[j][:TM], axis=0, keepdims=True)

    prev = pl.BlockSpec((8, QKV), lambda i: (jnp.maximum(i * (TM // 8) - 1, 0), 0))
    nxt = lambda n: pl.BlockSpec((8, n), lambda i: (jnp.minimum((i + 1) * (TM // 8), s_len // 8 - 1), 0))
    row128 = _fix((1, 128))
    ct_specs = [_row(TM, 1024), nxt(1024)] * 3
    return pl.pallas_call(
        body, name="gdn_prep_bwd", grid=(n_tiles,),
        in_specs=[_row(TM, QKV), prev, nxt(QKV), _row(TM, 128), _fix((C_CONV, QKV)), row128, row128] + ct_specs + [_row(TM, 128)],
        out_specs=[_row(TM, QKV), _row(TM, 128), _fix((C_CONV, QKV)), row128, row128],
        out_shape=[_sds(s_len, QKV), _sds(s_len, 128), _sds(C_CONV, QKV), _sds(1, 128), _sds(1, 128)],
        scratch_shapes=[pltpu.VMEM((TM + 8, 128), F32)],
        compiler_params=_cp("arbitrary"),
    )(raw, raw, raw, ba, conv_w, alog_row, dtb_row, dq, dq, dk, dk, dv, dv, dbg)


def _tein(eq, a, b):
    return jnp.einsum(eq, a, b, precision=lax.Precision.HIGH, preferred_element_type=F32)


def _unit_lower_inverse(lower):
    ri = lax.broadcasted_iota(jnp.int32, (C_CHUNK, C_CHUNK), 0)
    ci = lax.broadcasted_iota(jnp.int32, (C_CHUNK, C_CHUNK), 1)
    eye = (ri == ci).astype(F32)[None]
    mm = functools.partial(_bein, 'hij,hjk->hik')
    same_block = lambda size: (ri // size == ci // size)[None]
    n_mat = jnp.where(same_block(4), -lower, 0.0)
    inv = mm(eye + n_mat, eye + mm(n_mat, n_mat))
    for size in (4, 8, 16, 32):
        below = jnp.where(same_block(2 * size) & jnp.logical_not(same_block(size)), lower, 0.0)
        inv = inv - mm(inv, mm(below, inv))
    inv = _tein('hij,hjk->hik', inv, 2.0 * eye - _tein('hij,hjk->hik', eye + lower, inv))
    return jnp.where((ri >= ci)[None], inv, 0.0)


@jax.custom_vjp
def _known_inverse(lower, inv):
    return inv


def _known_inverse_fwd(lower, inv):
    return inv, inv


def _known_inverse_bwd(inv, d_inv):
    d_lower = -_bein('hik,hjk->hij', _bein('hji,hjk->hik', inv, d_inv), inv)
    return d_lower, jnp.zeros_like(inv)


_known_inverse.defvjp(_known_inverse_fwd, _known_inverse_bwd)


def _gdn_local(q, k, v, bgs, inv_known=None):
    lane = lax.broadcasted_iota(jnp.int32, (1, 128), 1)
    ri = lax.broadcasted_iota(jnp.int32, (C_CHUNK, C_CHUNK), 0)
    ci = lax.broadcasted_iota(jnp.int32, (C_CHUNK, C_CHUNK), 1)
    row_id = lax.broadcasted_iota(jnp.int32, (128, C_CHUNK), 0)
    beta, gc, gcj = [], [], []
    for bg in bgs:
        gc_t = _hdot((ri >= ci).astype(F32), bg)
        gc_rows = gc_t.T
        for h in range(C_HEADS):
            beta.append(jnp.sum(jnp.where(lane == h, bg, 0.0), axis=-1, keepdims=True))
            gc.append(jnp.sum(jnp.where(lane == C_HEADS + h, gc_t, 0.0), axis=-1, keepdims=True))
            gcj.append(jnp.sum(jnp.where(row_id == C_HEADS + h, gc_rows, 0.0), axis=0, keepdims=True))
    beta, gc, gcj = jnp.stack(beta, axis=0), jnp.stack(gc, axis=0), jnp.stack(gcj, axis=0)
    tril, strict = (ri >= ci)[None], (ri > ci)[None]
    decay = jnp.exp(jnp.where(tril, gc - gcj, -1e30))
    kb = k * beta
    lower = jnp.where(strict, _bein('hid,hjd->hij', kb, k) * decay, 0.0)
    inv = _unit_lower_inverse(lower) if inv_known is None else _known_inverse(lower, inv_known)
    egc = jnp.exp(gc)
    u_c = _bein('hij,hjd->hid', inv, v * beta)
    w_c = _bein('hij,hjd->hid', inv, kb * egc)
    aqk = _bein('hid,hjd->hij', q, k) * decay
    rowi = lax.broadcasted_iota(jnp.int32, (1, C_CHUNK, 1), 1)
    g_last = jnp.sum(jnp.where(rowi == C_CHUNK - 1, gc, 0.0), axis=1, keepdims=True)
    kd = k * jnp.exp(g_last - gc)
    return (u_c, w_c, aqk, q * egc, kd, jnp.exp(g_last)), inv


def _gdn_state(local, state):
    u_c, w_c, aqk, qg, kd, dec = local
    v_new = u_c - _bein('hik,hkv->hiv', w_c, state)
    o = _bein('hik,hkv->hiv', qg, state) + _bein('hij,hjv->hiv', aqk, v_new)
    return o, state * dec + _bein('hik,hiv->hkv', kd, v_new)


C_SUB = 4


def _gdn_group(q, k, v, bgs, state, inv_known=None):
    local, inv = _gdn_local(q, k, v, bgs, inv_known)
    outs = []
    for s in range(len(bgs)):
        o, state = _gdn_state(tuple(t[s * C_HEADS:(s + 1) * C_HEADS] for t in local), state)
        outs.append(o)
    return outs, state, inv


def _heads(ref):
    return jnp.stack([ref[s * C_CHUNK:(s + 1) * C_CHUNK, h * C_DK:(h + 1) * C_DK] for s in range(C_SUB) for h in range(C_HEADS)], axis=0)


def _put_heads(ref, sub, val):
    rows = slice(sub * C_CHUNK, (sub + 1) * C_CHUNK)
    for h in range(C_HEADS):
        ref[rows, h * C_DK:(h + 1) * C_DK] = val[h]


def _gdn_specs(s_len, rev):
    rows = C_SUB * C_CHUNK
    n_g = s_len // rows
    at = (lambda i: n_g - 1 - i) if rev else (lambda i: i)
    col = lambda c: pl.BlockSpec((rows, 1024), lambda i: (at(i), c))
    row128 = pl.BlockSpec((rows, 128), lambda i: (at(i), 0))
    state = pl.BlockSpec((1, C_HEADS, C_DK, C_DK), lambda i: (at(i), 0, 0, 0))
    inv = pl.BlockSpec((1, C_SUB * C_HEADS, C_CHUNK, C_CHUNK), lambda i: (at(i), 0, 0, 0))
    return n_g, col, row128, state, inv


def _gdn_fwd(qkv, bg):
    s_len = qkv.shape[0]
    n_g, col, row128, state_spec, inv_spec = _gdn_specs(s_len, False)

    def body(q_ref, k_ref, v_ref, bg_ref, o_ref, ss_ref, inv_ref, st_ref):
        _zero_at_first([st_ref])
        s0 = st_ref[...]
        ss_ref[0] = s0
        bgs = [bg_ref[s * C_CHUNK:(s + 1) * C_CHUNK, :] for s in range(C_SUB)]
        outs, s2, inv = _gdn_group(_heads(q_ref), _heads(k_ref), _heads(v_ref), bgs, s0)
        st_ref[...] = s2
        inv_ref[0] = inv
        for s in range(C_SUB):
            _put_heads(o_ref, s, outs[s])

    return pl.pallas_call(
        body, name="gdn_fwd", grid=(n_g,),
        in_specs=[col(0), col(1), col(2), row128],
        out_specs=[col(0), state_spec, inv_spec],
        out_shape=[_sds(s_len, 1024), _sds(n_g, C_HEADS, C_DK, C_DK), _sds(n_g, C_SUB * C_HEADS, C_CHUNK, C_CHUNK)],
        scratch_shapes=[pltpu.VMEM((C_HEADS, C_DK, C_DK), F32)],
        compiler_params=_cp("arbitrary"),
    )(qkv, qkv, qkv, bg)


def _gdn_bwd(qkv, bg, states, invs, do):
    s_len = qkv.shape[0]
    n_g, col, row128, state_spec, inv_spec = _gdn_specs(s_len, True)

    def body(q_ref, k_ref, v_ref, bg_ref, ss_ref, inv_ref, do_ref, dq_ref, dk_ref, dv_ref, dbg_ref, ds_ref):
        _zero_at_first([ds_ref])
        inv_known = inv_ref[0]

        def group(q, k, v, bgs, st):
            outs, st2, _ = _gdn_group(q, k, v, bgs, st, inv_known)
            return outs, st2

        bgs = [bg_ref[s * C_CHUNK:(s + 1) * C_CHUNK, :] for s in range(C_SUB)]
        _, vjp = jax.vjp(group, _heads(q_ref), _heads(k_ref), _heads(v_ref), bgs, ss_ref[0])
        douts = [jnp.stack([do_ref[s * C_CHUNK:(s + 1) * C_CHUNK, h * C_DK:(h + 1) * C_DK] for h in range(C_HEADS)], axis=0)
                 for s in range(C_SUB)]
        dq, dk, dv, dbgs, ds = vjp((douts, ds_ref[...]))
        ds_ref[...] = ds
        for s in range(C_SUB):
            dbg_ref[s * C_CHUNK:(s + 1) * C_CHUNK, :] = dbgs[s]
            for ref, val in ((dq_ref, dq), (dk_ref, dk), (dv_ref, dv)):
                _put_heads(ref, s, val[s * C_HEADS:(s + 1) * C_HEADS])

    return pl.pallas_call(
        body, name="gdn_bwd", grid=(n_g,),
        in_specs=[col(0), col(1), col(2), row128, state_spec, inv_spec, col(0)],
        out_specs=[col(0), col(0), col(0), row128],
        out_shape=[_sds(s_len, 1024)] * 3 + [_sds(s_len, 128)],
        scratch_shapes=[pltpu.VMEM((C_HEADS, C_DK, C_DK), F32)],
        compiler_params=_cp("arbitrary"),
    )(qkv, qkv, qkv, bg, states, invs, do)


def _head_norm_gate(o, gate, norm_g):
    return (_rms(o) * norm_g) * jax.nn.silu(gate)


def _l1_out_fb(o, gate_c, x1, target, norm_g, w_out, post_g, gate):
    s_len = x1.shape[0]

    def body(o_ref, gc_ref, x1_ref, t_ref, ng_ref, w_ref, pg_ref, gt_ref,
             loss_ref, dres_ref, do_ref, dgc_ref, dw_ref, dng_ref, dpg_ref, dgt_ref):
        _zero_at_first([loss_ref, dw_ref, dng_ref, dpg_ref, dgt_ref])
        ng = ng_ref[...]
        ons, vjps = [], []
        for h in range(C_HEADS):
            sl = slice(h * C_DK, (h + 1) * C_DK)
            on, vjp_h = jax.vjp(_head_norm_gate, o_ref[:, sl], gc_ref[:, sl], ng)
            ons.append(on)
            vjps.append(vjp_h)
        on_all = jnp.concatenate(ons, axis=-1)
        y = _bdot(on_all, w_ref[...])
        x2, vjp2 = jax.vjp(_post_res, y, x1_ref[...], pg_ref[...], gt_ref[...])
        err = x2 - t_ref[...]
        _acc(loss_ref, jnp.full((1, 128), 0.5 * jnp.sum(jnp.mean(err * err, axis=-1)), F32))
        dx2 = err * (1.0 / D_MODEL)
        dy, _, dpg, dgt = vjp2(dx2)
        dres_ref[...] = dx2
        _acc(dpg_ref, dpg)
        _acc(dgt_ref, dgt)
        dw_ref[...] += _bdot_tn(on_all, dy)
        don = _bdot_nt(dy, w_ref[...])
        for h in range(C_HEADS):
            sl = slice(h * C_DK, (h + 1) * C_DK)
            do_h, dgc_h, dng = vjps[h](don[:, sl])
            do_ref[:, sl] = do_h
            dgc_ref[:, sl] = dgc_h
            _acc(dng_ref, dng)

    vec, r10 = _fix((1, D_MODEL)), _row(TM, D_MODEL)
    row128 = _fix((1, 128))
    return pl.pallas_call(
        body, name="l1_out_fb", grid=(s_len // TM,),
        in_specs=[r10, r10, r10, r10, row128, _fix((D_MODEL, D_MODEL)), vec, vec],
        out_specs=[row128, r10, r10, r10, _fix((D_MODEL, D_MODEL)), row128, vec, vec],
        out_shape=[_sds(1, 128), _sds(s_len, D_MODEL), _sds(s_len, D_MODEL), _sds(s_len, D_MODEL),
                   _sds(D_MODEL, D_MODEL), _sds(1, 128), _sds(1, D_MODEL), _sds(1, D_MODEL)],
        compiler_params=_cp("arbitrary"),
    )(o, gate_c, x1, target, norm_g, w_out, post_g, gate)


def _row_of(v, width, at):
    return jnp.zeros((1, width), F32).at[0, at:at + v.shape[-1]].set(v.reshape(-1))


def _local_step(x, target, mod, wd, comm=None):
    s_len = x.shape[0]
    shift0, scale0, gate0 = (mod[0:1, i * 1024:(i + 1) * 1024] for i in range(3))
    shift1, scale1, gate1 = (mod[1:2, i * 1024:(i + 1) * 1024] for i in range(3))
    pre_g0, pre_g1 = wd["pre_g"][0:1], wd["pre_g"][1:2]
    post_g0, post_g1 = wd["post_g"][0:1], wd["post_g"][1:2]
    w_in0 = wd["ab_w_in"].astype(BF)
    d_skip, glu_b = wd["s5_d"].reshape(1, 512), wd["s5_glu_b"].reshape(1, 512)
    norm_g = wd["gdn_norm_g"].reshape(1, 128)
    alog_row = _row_of(wd["gdn_a_log"], 128, C_HEADS)
    dtb_row = _row_of(wd["gdn_dt_bias"], 128, C_HEADS)
    conv_w = wd["gdn_conv"]

    a_re, a_im = wd["s5_a_re"], wd["s5_a_im"]
    log_dt = wd["s5_log_dt"].reshape(B_GROUPS, 1)
    bt_re = wd["s5_b_re"].transpose(0, 2, 1).reshape(B_WIDTH, B_STATE)
    bt_im = wd["s5_b_im"].transpose(0, 2, 1).reshape(B_WIDTH, B_STATE)
    abar_r, abar_i, bbar_r, bbar_i = _s5_params(a_re, a_im, log_dt, bt_re, bt_im)
    abr, abi = abar_r.reshape(1, -1), abar_i.reshape(1, -1)
    btr, bti = _blockdiag_b(bbar_r).astype(BF), _blockdiag_b(bbar_i).astype(BF)
    ctr, cti = _blockdiag_c(wd["s5_c_re"]).astype(BF), _blockdiag_c(wd["s5_c_im"]).astype(BF)

    table = _bucket_table()
    biases = _attn_bias(wd["rel_bias"], jnp.asarray(table))
    front = _l0_front(x, pre_g0, scale0, shift0, w_in0)
    qs, ks, vs = front[0:3], front[3:6], front[6:9]
    u, ga, gb, h0 = front[9:]
    riders = [None] * 4 if comm is None else comm.late_exchanges()
    os, ls, got = [], [], []
    for i in range(3):
        (o_d, l_d), g = _attn_fwd(qs[i], ks[i], vs[i], biases[i], exchange=riders[i])
        os.append(o_d)
        ls.append(l_d)
        got.append(g)
    seg_len = s_len // S5_SEG
    zero_state = (jnp.zeros((S5_SEG, S5_WIDTH), F32),) * 2
    ends, _ = _s5_seq_fwd(u, btr, bti, ctr, cti, abr, abi, zero_state, False)
    x_entry = _s5_entries("s5_entries_fwd", *ends, abr, abi, seg_len, False)
    (xr, xi, ypre3, _, _), g = _s5_seq_fwd(u, btr, bti, ctr, cti, abr, abi, x_entry, True, exchange=riders[3])
    got.append(g)
    ypre = ypre3.reshape(s_len, B_WIDTH)
    rider = None
    if comm is not None:
        mine0, mine1 = comm.late_halves(got)
        wd = {**wd, **comm.full_weights(["ab_w_out", "s5_glu_w"], mine0, _sibling_exchange("gather_w_sibling_l0", mine0))}
        rider = (mine1, "sibling")
    w_out0 = wd["ab_w_out"].astype(BF)
    glu_w = wd["s5_glu_w"].astype(BF)
    (x1, y0), theirs1 = _l0_out(os, ls, ga, gb, ypre, u, x, d_skip, glu_w, glu_b, w_out0, post_g0, gate0, exchange=rider)
    if comm is not None:
        wd = {**wd, **comm.full_weights(["gdn_w_in", "gdn_w_out"], mine1, theirs1)}
    w_in1 = wd["gdn_w_in"]
    if w_in1.shape[1] == C_IN:
        w_in1 = jnp.concatenate([w_in1, jnp.zeros((D_MODEL, C_IN_PAD - C_IN), w_in1.dtype)], axis=1)
    w_in1 = w_in1.astype(BF)
    w_out1 = wd["gdn_w_out"].astype(BF)

    raw, gate_c, ba, h1 = _l1_front(x1, pre_g1, scale1, shift1, w_in1)
    qkv, bg = _gdn_prep(raw, ba, conv_w, alog_row, dtb_row)
    o_gdn, states, invs = _gdn_fwd(qkv, bg)
    loss_row, dres1, do_gdn, dgate_c, dw_out1, dnorm_g, dpost_g1, dgate1 = _l1_out_fb(
        o_gdn, gate_c, x1, target, norm_g, w_out1, post_g1, gate1)

    dq1, dk1, dv1, dbg = _gdn_bwd(qkv, bg, states, invs, do_gdn)
    draw, dba, dconv_w, dalog_row, ddtb_row = _gdn_prep_bwd(raw, ba, conv_w, alog_row, dtb_row, dq1, dk1, dv1, dbg)
    dz1, dx1, dpre_g1, dscale1, dshift1 = _front_bwd(
        "l1_front_bwd", x1, pre_g1, scale1, shift1, w_in1, dres1, [[draw], [dgate_c], [dba]], [QKV, 1024, 128])
    dw_in1 = _matmul_tn("l1_dw_in", h1, dz1, 1408)

    l1_names, l0_names = ["gdn_w_in", "gdn_w_out"], ["ab_w_out", "s5_glu_w"]
    rider = None if comm is None else (comm.split_halves({"gdn_w_in": dw_in1, "gdn_w_out": dw_out1}), "sibling")
    l0b, from_sibling1 = _l0_out_bwd(os, ls, ga, gb, ypre, u, x, y0, d_skip, glu_w, glu_b, w_out0, post_g0, gate0, dx1, exchange=rider)
    dos, dls = l0b[0:3], l0b[3:6]
    dga, dgb, dypre, du_skip, dd_skip, dglu_w, dglu_b, dw_out0, dpost_g0, dgate0 = l0b[6:]
    rider = None if comm is None else (comm.split_halves({"ab_w_out": dw_out0, "s5_glu_w": dglu_w}), "sibling")
    starts, _ = _s5_seq_bwd(dypre, None, None, None, btr, bti, ctr, cti, abr, abi, zero_state, None, False)
    g_entry = _s5_entries("s5_entries_bwd", *starts, abr, -abi, seg_len, True)
    (du3, dbtr, dbti, dctr, dcti, dabr, dabi, _, _), from_sibling0 = _s5_seq_bwd(
        dypre, xr, xi, u, btr, bti, ctr, cti, abr, abi, g_entry, x_entry, True, exchange=rider)
    du_scan = du3.reshape(s_len, B_WIDTH)
    riders = [None] * 3
    if comm is not None:
        riders = [(comm.chip_partials(l1_names, from_sibling1), "scatter"), (comm.chip_partials(l0_names, from_sibling0), "scatter"), None]
    dqs, dks, dvs, dbs = [], [], [], []
    for i in range(3):
        (dq_d, dk_d, dv_d, db_d), got_d = _attn_bwd(qs[i], ks[i], vs[i], biases[i], os[i], ls[i], dos[i], dls[i], exchange=riders[i])
        dqs.append(dq_d)
        dks.append(dk_d)
        dvs.append(dv_d)
        dbs.append(db_d)
        if comm is not None and riders[i] is not None:
            comm.received.update(zip((l1_names, l0_names)[i], got_d))
    parts = [dqs, dks, dvs, [du_skip, du_scan], [dga], [dgb]]
    dz0, grad_x, dpre_g0, dscale0, dshift0 = _front_bwd(
        "l0_front_bwd", x, pre_g0, scale0, shift0, w_in0, dx1, parts, [512] * 6)
    if comm is None:
        dw_in0 = _matmul_tn("l0_dw_in", h0, dz0, 768)
    else:
        dw_in0 = None
        theirs, _ = _matmul_tn_half("l0_dw_in_other", h0, dz0, 768, 1 - comm.core_1)
        comm.mine["ab_w_in"], comm.from_sibling["ab_w_in"] = _matmul_tn_half(
            "l0_dw_in_mine", h0, dz0, 768, comm.core_1, exchange=([theirs], "sibling"))

    idx_rows = jnp.asarray(table.reshape(3, -1), F32)
    drel = _rel_bias_grad(dbs, idx_rows).T
    da_re, da_im, dlog_dt, dbt_re, dbt_im = _s5_params_bwd(
        a_re, a_im, log_dt, bt_re, bt_im, dabr.reshape(B_GROUPS, B_STATE), dabi.reshape(B_GROUPS, B_STATE),
        _blockdiag_b_t(dbtr), _blockdiag_b_t(dbti))
    unb = lambda d: d.reshape(B_GROUPS, B_GROUP, B_STATE).transpose(0, 2, 1)
    grads = {
        "pre_g": jnp.concatenate([dpre_g0, dpre_g1], 0), "post_g": jnp.concatenate([dpost_g0, dpost_g1], 0),
        "rel_bias": drel, "ab_w_in": dw_in0, "ab_w_out": dw_out0,
        "s5_a_re": da_re, "s5_a_im": da_im, "s5_log_dt": dlog_dt.reshape(B_GROUPS),
        "s5_b_re": unb(dbt_re), "s5_b_im": unb(dbt_im),
        "s5_c_re": _blockdiag_c_t(dctr), "s5_c_im": _blockdiag_c_t(dcti),
        "s5_d": dd_skip.reshape(512), "s5_glu_w": dglu_w, "s5_glu_b": dglu_b.reshape(512),
        "gdn_w_in": dw_in1[:, :C_IN], "gdn_conv": dconv_w,
        "gdn_a_log": dalog_row[0, C_HEADS:2 * C_HEADS], "gdn_dt_bias": ddtb_row[0, C_HEADS:2 * C_HEADS],
        "gdn_norm_g": dnorm_g.reshape(128), "gdn_w_out": dw_out1,
    }
    dmod = jnp.concatenate([jnp.concatenate([dshift0, dscale0, dgate0], 1), jnp.concatenate([dshift1, dscale1, dgate1], 1)], 0)
    return loss_row[0, 0], grad_x, grads, dmod


def _place():
    return lax.axis_index("x"), lax.axis_index("y"), lax.axis_index("c")


def _flip(v, bit):
    return 1 - v if bit else v


def _hbm_call(name, body, arrs, out_shapes, n_sem):
    any_spec = pl.BlockSpec(memory_space=pl.ANY)
    return pl.pallas_call(
        body, name=name,
        in_specs=[any_spec] * len(arrs), out_specs=[any_spec] * len(out_shapes), out_shape=out_shapes,
        scratch_shapes=[pltpu.SemaphoreType.DMA((n_sem,)), pltpu.SemaphoreType.DMA((n_sem,))],
    )(*arrs)


def _own_slot(gathered, own, slot):
    idx = lax.broadcasted_iota(jnp.int32, (gathered.shape[0],) + (1,) * own.ndim, 0)
    return jnp.where(idx == slot, own[None], gathered)


def _all_gather8(name, arr):
    def body(x_ref, out_ref, send_sems, recv_sems):
        x, y, c = _place()
        me = 4 * x + 2 * y + c
        sends, recvs = [], []
        for m in range(1, 8):
            peer = (_flip(x, m & 4), _flip(y, m & 2), _flip(c, m & 1))
            sends.append(pltpu.make_async_remote_copy(x_ref, out_ref.at[me], send_sems.at[m - 1], recv_sems.at[m - 1],
                                                      device_id=peer, device_id_type=MESH))
            recvs.append(pltpu.make_async_remote_copy(x_ref, out_ref.at[4 * peer[0] + 2 * peer[1] + peer[2]], send_sems.at[m - 1],
                                                      recv_sems.at[m - 1], device_id=peer, device_id_type=MESH))
        for cp in sends:
            cp.start()
        for cp in recvs:
            cp.wait_recv()
        for cp in sends:
            cp.wait_send()

    return _hbm_call(name, body, [arr], [jax.ShapeDtypeStruct((8,) + arr.shape, arr.dtype)], 7)[0]


def _all_to_all8(name, arr):
    def body(x_ref, out_ref, send_sems, recv_sems):
        x, y, c = _place()
        me = 4 * x + 2 * y + c
        sends, recvs = [], []
        for m in range(1, 8):
            peer = (_flip(x, m & 4), _flip(y, m & 2), _flip(c, m & 1))
            peer_id = 4 * peer[0] + 2 * peer[1] + peer[2]
            sends.append(pltpu.make_async_remote_copy(x_ref.at[peer_id], out_ref.at[me], send_sems.at[m - 1], recv_sems.at[m - 1],
                                                      device_id=peer, device_id_type=MESH))
            recvs.append(pltpu.make_async_remote_copy(x_ref.at[peer_id], out_ref.at[peer_id], send_sems.at[m - 1], recv_sems.at[m - 1],
                                                      device_id=peer, device_id_type=MESH))
        for cp in sends:
            cp.start()
        for cp in recvs:
            cp.wait_recv()
        for cp in sends:
            cp.wait_send()

    return _hbm_call(name, body, [arr], [jax.ShapeDtypeStruct(arr.shape, arr.dtype)], 7)[0]


def _chip_copies(ins, outs, send_sems, recv_sems, scatter):
    x, y, c = _place()
    mine = 2 * x + y
    sends, recvs = [], []
    for a in range(len(ins)):
        for m in range(1, 4):
            px, py = _flip(x, m & 2), _flip(y, m & 1)
            k = 3 * a + m - 1
            src = ins[a].at[2 * px + py] if scatter else ins[a]
            sends.append(pltpu.make_async_remote_copy(src, outs[a].at[mine], send_sems.at[k], recv_sems.at[k],
                                                      device_id=(px, py, c), device_id_type=MESH))
            recvs.append(pltpu.make_async_remote_copy(src, outs[a].at[2 * px + py], send_sems.at[k], recv_sems.at[k],
                                                      device_id=(px, py, c), device_id_type=MESH))
    return sends, recvs


def _chip_shapes(arrs, scatter):
    return [jax.ShapeDtypeStruct(a.shape if scatter else (4,) + a.shape, a.dtype) for a in arrs]


def _chip_exchange(name, arrs, scatter):
    n = len(arrs)

    def body(*refs):
        sends, recvs = _chip_copies(refs[:n], refs[n:2 * n], refs[2 * n], refs[2 * n + 1], scatter)
        for cp in sends:
            cp.start()
        for cp in recvs:
            cp.wait_recv()
        for cp in sends:
            cp.wait_send()

    return _hbm_call(name, body, arrs, _chip_shapes(arrs, scatter), 3 * n)


def _call_with_exchange(body, name, grid, in_specs, out_specs, out_shape, scratch_shapes, args, exchange, prefetch=None):
    n_pre = 0 if prefetch is None else 1

    def call(fn, in_specs, out_specs, out_shape, scratch_shapes, args):
        params = _cp(*["arbitrary"] * len(grid))
        if prefetch is None:
            return pl.pallas_call(fn, name=name, grid=grid, in_specs=in_specs, out_specs=out_specs, out_shape=out_shape,
                                  scratch_shapes=scratch_shapes, compiler_params=params)(*args)
        grid_spec = pltpu.PrefetchScalarGridSpec(num_scalar_prefetch=1, grid=grid, in_specs=in_specs, out_specs=out_specs,
                                                 scratch_shapes=scratch_shapes)
        return pl.pallas_call(fn, name=name, grid_spec=grid_spec, out_shape=out_shape, compiler_params=params)(prefetch, *args)

    if exchange is None:
        return call(body, in_specs, out_specs, out_shape, scratch_shapes, args), []
    arrs, kind = exchange
    n_in, n_out, n_ex, n_scr = len(in_specs), len(out_specs), len(arrs), len(scratch_shapes)
    n_sem = n_ex if kind == "sibling" else 3 * n_ex
    ex_shapes = [jax.ShapeDtypeStruct(a.shape, a.dtype) for a in arrs] if kind == "sibling" else _chip_shapes(arrs, kind == "scatter")

    def fused(*refs):
        pre, refs = refs[:n_pre], refs[n_pre:]
        ins, ex_in = refs[:n_in], refs[n_in:n_in + n_ex]
        outs, ex_out = refs[n_in + n_ex:n_in + n_ex + n_out], refs[n_in + n_ex + n_out:n_in + 2 * n_ex + n_out]
        rest = refs[n_in + 2 * n_ex + n_out:]
        if kind == "sibling":
            sends = recvs = _sibling_copies(ex_in, ex_out, rest[n_scr], rest[n_scr + 1])
        else:
            sends, recvs = _chip_copies(ex_in, ex_out, rest[n_scr], rest[n_scr + 1], kind == "scatter")
        first, last = pl.program_id(0) == 0, pl.program_id(0) == grid[0] - 1
        for k in range(1, len(grid)):
            first, last = first & (pl.program_id(k) == 0), last & (pl.program_id(k) == grid[k] - 1)

        @pl.when(first)
        def _():
            for cp in sends:
                cp.start()

        body(*pre, *ins, *outs, *rest[:n_scr])

        @pl.when(last)
        def _():
            for cp in recvs:
                cp.wait_recv()
            for cp in sends:
                cp.wait_send()

    any_spec = pl.BlockSpec(memory_space=pl.ANY)
    res = call(fused, list(in_specs) + [any_spec] * n_ex, list(out_specs) + [any_spec] * n_ex, list(out_shape) + ex_shapes,
               list(scratch_shapes) + [pltpu.SemaphoreType.DMA((n_sem,))] * 2, (*args, *arrs))
    return res[:n_out], res[n_out:]


def _sibling_copies(ins, outs, send_sems, recv_sems):
    x, y, c = _place()
    return [pltpu.make_async_remote_copy(ins[a], outs[a], send_sems.at[a], recv_sems.at[a],
                                         device_id=(x, y, 1 - c), device_id_type=MESH) for a in range(len(ins))]


def _sibling_exchange(name, arrs):
    n = len(arrs)

    def body(*refs):
        copies = _sibling_copies(refs[:n], refs[n:2 * n], refs[2 * n], refs[2 * n + 1])
        for cp in copies:
            cp.start()
        for cp in copies:
            cp.wait_recv()
        for cp in copies:
            cp.wait_send()

    return _hbm_call(name, body, arrs, [jax.ShapeDtypeStruct(a.shape, a.dtype) for a in arrs], n)


def _row_tile(rows):
    for t in (256, 128, 64, 32, 16, 8):
        if rows % t == 0:
            return t
    return rows


def _pair_sum(name, a, b, out_dtype):
    rows, cols = a.shape
    tr = _row_tile(rows)

    def body(a_ref, b_ref, o_ref):
        o_ref[...] = (a_ref[...] + b_ref[...]).astype(out_dtype)

    return pl.pallas_call(body, name=name, grid=(rows // tr,), in_specs=[_row(tr, cols)] * 2, out_specs=_row(tr, cols),
                          out_shape=_sds(rows, cols, dtype=out_dtype), compiler_params=_cp("arbitrary"))(a, b)


def _chip_sum(name, recv, partial, mine):
    n, rows, cols = recv.shape
    tr = _row_tile(rows)

    def body(mine_ref, *refs):
        own = refs[n][0].astype(F32)
        acc = None
        for s in range(n):
            term = jnp.where(mine_ref[0] == s, own, refs[s][0].astype(F32))
            acc = term if acc is None else acc + term
        refs[-1][...] = acc

    def slot_spec(s):
        return pl.BlockSpec((1, tr, cols), lambda i, m: (jnp.where(m[0] == s, (s + 1) % n, s), i, 0))

    grid_spec = pltpu.PrefetchScalarGridSpec(
        num_scalar_prefetch=1, grid=(rows // tr,),
        in_specs=[slot_spec(s) for s in range(n)] + [pl.BlockSpec((1, tr, cols), lambda i, m: (m[0], i, 0))],
        out_specs=pl.BlockSpec((tr, cols), lambda i, m: (i, 0)))
    return pl.pallas_call(body, name=name, grid_spec=grid_spec, out_shape=_sds(rows, cols),
                          compiler_params=_cp("arbitrary"))(mine, *([recv] * n), partial)


def _slot_sum(name, arr):
    n, rows, cols = arr.shape
    tr = _row_tile(rows)

    def body(*refs):
        acc = refs[0][0]
        for r in refs[1:-1]:
            acc = acc + r[0]
        refs[-1][...] = acc

    specs = [pl.BlockSpec((1, tr, cols), functools.partial(lambda s, i: (s, i, 0), s)) for s in range(n)]
    return pl.pallas_call(body, name=name, grid=(rows // tr,), in_specs=specs, out_specs=_row(tr, cols),
                          out_shape=_sds(rows, cols), compiler_params=_cp("arbitrary"))(*([arr] * n))


def _adamw(name, w, g, m, v, exchange=None):
    rows, cols = w.shape
    tr = _row_tile(rows)

    def body(w_ref, g_ref, m_ref, v_ref, d_ref, nm_ref, nv_ref):
        g_ = g_ref[...]
        m_ = ADAM_B1 * m_ref[...] + (1.0 - ADAM_B1) * g_
        v_ = ADAM_B2 * v_ref[...] + (1.0 - ADAM_B2) * (g_ * g_)
        m_hat = m_ / (1.0 - ADAM_B1 ** ADAM_STEP)
        v_hat = v_ / (1.0 - ADAM_B2 ** ADAM_STEP)
        d_ref[...] = -ADAM_LR * (m_hat / (jnp.sqrt(v_hat) + ADAM_EPS) + ADAM_WD * w_ref[...])
        nm_ref[...] = m_
        nv_ref[...] = v_

    spec = _row(tr, cols)
    return _call_with_exchange(body, name, (rows // tr,), [spec] * 4, [spec] * 3, [_sds(rows, cols)] * 3, [], (w, g, m, v), exchange)


def _adamw_many(name, ws, gs, ms, vs):
    n = len(ws)

    def body(*refs):
        for i in range(n):
            w_ref, g_ref, m_ref, v_ref = (refs[k * n + i] for k in range(4))
            d_ref, nm_ref, nv_ref = (refs[(4 + k) * n + i] for k in range(3))
            g_ = g_ref[...]
            m_ = ADAM_B1 * m_ref[...] + (1.0 - ADAM_B1) * g_
            v_ = ADAM_B2 * v_ref[...] + (1.0 - ADAM_B2) * (g_ * g_)
            m_hat = m_ / (1.0 - ADAM_B1 ** ADAM_STEP)
            v_hat = v_ / (1.0 - ADAM_B2 ** ADAM_STEP)
            d_ref[...] = -ADAM_LR * (m_hat / (jnp.sqrt(v_hat) + ADAM_EPS) + ADAM_WD * w_ref[...])
            nm_ref[...] = m_
            nv_ref[...] = v_

    shapes = [_sds(*a.shape) for a in ws]
    res = pl.pallas_call(body, name=name, out_shape=shapes * 3,
                         compiler_params=pltpu.CompilerParams(vmem_limit_bytes=VMEM_LIMIT_BYTES))(*ws, *gs, *ms, *vs)
    return [(res[i], res[n + i], res[2 * n + i]) for i in range(n)]


def _adamw_halves(name, w, g_mine, g_sibling, m, v, core):
    _, rows, cols = w.shape
    half = rows // 2
    tr = _row_tile(half)
    per_half = half // tr

    def body(core_ref, w_ref, gm_ref, gs_ref, m_ref, v_ref, g_ref, d_ref, nm_ref, nv_ref):
        g_ = jnp.where(pl.program_id(0) // per_half == core_ref[0], gm_ref[...], gs_ref[...])
        m_ = ADAM_B1 * m_ref[...] + (1.0 - ADAM_B1) * g_
        v_ = ADAM_B2 * v_ref[...] + (1.0 - ADAM_B2) * (g_ * g_)
        m_hat = m_ / (1.0 - ADAM_B1 ** ADAM_STEP)
        v_hat = v_ / (1.0 - ADAM_B2 ** ADAM_STEP)
        g_ref[...] = g_
        d_ref[...] = -ADAM_LR * (m_hat / (jnp.sqrt(v_hat) + ADAM_EPS) + ADAM_WD * w_ref[...])
        nm_ref[...] = m_
        nv_ref[...] = v_

    full = pl.BlockSpec((None, tr, cols), lambda i, c: (0, i, 0))
    in_half = pl.BlockSpec((tr, cols), lambda i, c: (i % per_half, 0))
    grid_spec = pltpu.PrefetchScalarGridSpec(num_scalar_prefetch=1, grid=(rows // tr,),
                                             in_specs=[full, in_half, in_half, full, full], out_specs=[full] * 4)
    return pl.pallas_call(body, name=name, grid_spec=grid_spec, out_shape=[_sds(1, rows, cols)] * 4,
                          compiler_params=_cp("arbitrary"))(core, w, g_mine, g_sibling, m, v)


def _mod_local(c_all, ada_w):
    def body(c_ref, w_ref, o_ref):
        c_act = jax.nn.silu(c_ref[...])
        for l in range(2):
            o_ref[l] = _hdot(c_act, w_ref[l])

    return pl.pallas_call(body, name="mod_local", out_shape=_sds(2, 8, ada_w.shape[2]),
                          compiler_params=pltpu.CompilerParams(vmem_limit_bytes=VMEM_LIMIT_BYTES))(c_all, ada_w)


def _ada_w_grad(c_all, dmod_cols):
    def body(c_ref, d_ref, o_ref):
        c_act = jax.nn.silu(c_ref[...])
        for l in range(2):
            o_ref[l] = lax.dot_general(c_act, d_ref[l], (((0,), (0,)), ((), ())), precision=HI, preferred_element_type=F32)

    return pl.pallas_call(body, name="ada_w_grad", out_shape=_sds(2, D_MODEL, dmod_cols.shape[2]),
                          compiler_params=pltpu.CompilerParams(vmem_limit_bytes=VMEM_LIMIT_BYTES))(c_all, dmod_cols)


_SMALL = ("ada_b", "pre_g", "post_g", "rel_bias", "s5_a_re", "s5_a_im", "s5_log_dt", "s5_b_re", "s5_b_im", "s5_c_re", "s5_c_im",
          "s5_d", "s5_glu_b", "gdn_a_log", "gdn_dt_bias", "gdn_norm_g")
_SHARDED = ("ab_w_in", "ab_w_out", "s5_glu_w", "gdn_w_in", "gdn_w_out")
_COL_SHARDED = ("ab_w_in", "gdn_w_in")
_WEIGHTS = ("ada_w", "ada_b", "pre_g", "post_g", "rel_bias", "ab_w_in", "ab_w_out", "s5_a_re", "s5_a_im", "s5_log_dt", "s5_b_re",
            "s5_b_im", "s5_c_re", "s5_c_im", "s5_d", "s5_glu_w", "s5_glu_b", "gdn_w_in", "gdn_conv", "gdn_a_log", "gdn_dt_bias",
            "gdn_norm_g", "gdn_w_out")


def _rows128(n):
    return -(-n // 128)


def _pack(arrs, total_rows):
    pieces = []
    for a in arrs:
        flat = a.reshape(-1)
        pieces.append(jnp.pad(flat, (0, _rows128(flat.shape[0]) * 128 - flat.shape[0])).reshape(-1, 128))
    used = sum(p.shape[0] for p in pieces)
    pieces.append(jnp.zeros((total_rows - used, 128), F32))
    return jnp.concatenate(pieces, axis=0)


def _unpack(buf, shapes):
    out, at = [], 0
    for shp in shapes:
        n = int(np.prod(shp))
        out.append(buf[at:at + _rows128(n)].reshape(-1)[:n].reshape(shp))
        at += _rows128(n)
    return out


def _full_from_halves(g):
    return g.transpose(1, 0, 2, 3).reshape(8 * g.shape[2], g.shape[3])


def _join_col_shards(name, mine, theirs, core, width):
    n, h, cols = mine.shape
    tr = _row_tile(h)
    per_half = h // tr

    def body(core_ref, a_ref, b_ref, o_ref):
        def put(src_ref):
            pad = [jnp.zeros((tr, width - n * cols), o_ref.dtype)] if width > n * cols else []
            o_ref[...] = jnp.concatenate([src_ref[s] for s in range(n)] + pad, axis=1)

        is_mine = pl.program_id(0) // per_half == core_ref[0]
        pl.when(is_mine)(functools.partial(put, a_ref))
        pl.when(jnp.logical_not(is_mine))(functools.partial(put, b_ref))

    half = pl.BlockSpec((n, tr, cols), lambda i, c: (0, i % per_half, 0))
    grid_spec = pltpu.PrefetchScalarGridSpec(num_scalar_prefetch=1, grid=(2 * per_half,), in_specs=[half, half],
                                             out_specs=pl.BlockSpec((tr, width), lambda i, c: (i, 0)))
    return pl.pallas_call(body, name=name, grid_spec=grid_spec, out_shape=_sds(2 * h, width, dtype=mine.dtype),
                          compiler_params=_cp("arbitrary"))(core, mine, theirs)


def _split_col_shards(name, g, cols, half, add=None, out_dtype=F32):
    h = g.shape[0] // 2
    tr = _row_tile(h)
    per_half = h // tr

    def body(half_ref, g_ref, *refs):
        for s in range(4):
            part = g_ref[:, s * cols:(s + 1) * cols]
            refs[-1][s] = (part if add is None else part + refs[0][s]).astype(out_dtype)

    shards = pl.BlockSpec((4, tr, cols), lambda i, c: (0, i, 0))
    grid_spec = pltpu.PrefetchScalarGridSpec(
        num_scalar_prefetch=1, grid=(per_half,),
        in_specs=[pl.BlockSpec((tr, g.shape[1]), lambda i, c: (c[0] * per_half + i, 0))] + ([] if add is None else [shards]),
        out_specs=shards)
    return pl.pallas_call(body, name=name, grid_spec=grid_spec, out_shape=_sds(4, h, cols, dtype=out_dtype),
                          compiler_params=_cp("arbitrary"))(half, g, *([] if add is None else [add]))


_LATE = ("ab_w_out", "s5_glu_w", "gdn_w_in", "gdn_w_out")


class _WeightExchanges:
    def __init__(self, shards, core, chip):
        self.core, self.chip = core, chip
        self.core_1 = jnp.reshape(core, (1,)).astype(jnp.int32)
        self.half = {}
        for name, shard in shards.items():
            h = shard.shape[0] // 2
            self.half[name] = lax.dynamic_slice_in_dim(shard.astype(BF), core * h, h, axis=0)
        self.mine, self.from_sibling, self.partial, self.received = {}, {}, {}, {}

    def my_halves(self, names, from_chips):
        return [_own_slot(g, self.half[n], self.chip) for n, g in zip(names, from_chips)]

    def full_weights(self, names, mine, theirs):
        full = {}
        for n, a, b in zip(names, mine, theirs):
            if n in _COL_SHARDED:
                full[n] = _join_col_shards("join_" + n, a, b, self.core_1, C_IN_PAD if n == "gdn_w_in" else 4 * a.shape[2])
            else:
                full[n] = _full_from_halves(jnp.where(self.core == 0, jnp.stack([a, b], 0), jnp.stack([b, a], 0)))
        return full

    def first_weights(self):
        mine = self.my_halves(["ab_w_in"], _chip_exchange("gather_w_chips", [self.half["ab_w_in"]], False))
        return self.full_weights(["ab_w_in"], mine, _sibling_exchange("gather_w_sibling_first", mine))

    def late_exchanges(self):
        rows = self.half["gdn_w_in"].shape[0] // 2
        pieces = [self.half["gdn_w_in"][:rows], self.half["gdn_w_in"][rows:]]
        return [([self.half["ab_w_out"], self.half["s5_glu_w"]], "gather"), ([self.half["gdn_w_out"]], "gather"),
                ([pieces[0]], "gather"), ([pieces[1]], "gather")]

    def late_halves(self, got):
        return (self.my_halves(["ab_w_out", "s5_glu_w"], got[0]),
                self.my_halves(["gdn_w_in", "gdn_w_out"], [jnp.concatenate([got[2][0], got[3][0]], axis=1), got[1][0]]))

    def split_halves(self, grads):
        other = []
        for name, g in grads.items():
            if name in _COL_SHARDED:
                self.mine[name] = g
                other.append(_split_col_shards("split_other_" + name, g, self.half[name].shape[1], 1 - self.core_1))
                continue
            sm = g.reshape(4, g.shape[0] // 4, g.shape[1])
            h = sm.shape[1] // 2
            self.mine[name] = lax.dynamic_slice_in_dim(sm, self.core * h, h, axis=1)
            other.append(lax.dynamic_slice_in_dim(sm, (1 - self.core) * h, h, axis=1))
        return other

    def chip_partials(self, names, from_sibling):
        for name, b in zip(names, from_sibling):
            a = self.mine[name]
            if a.ndim == 2:
                self.partial[name] = _split_col_shards("sum_sibling_" + name, a, b.shape[2], self.core_1, add=b, out_dtype=BF)
                continue
            flat = lambda t: t.reshape(-1, t.shape[-1])
            self.partial[name] = _pair_sum("sum_sibling_" + name, flat(a), flat(b), BF).reshape(a.shape)
        return [self.partial[n] for n in names]


def kernel(x, c, ada_w, ada_b, pre_g, post_g, rel_bias, ab_w_in, ab_w_out, s5_a_re, s5_a_im, s5_log_dt, s5_b_re, s5_b_im, s5_c_re, s5_c_im, s5_d, s5_glu_w, s5_glu_b, gdn_w_in, gdn_conv, gdn_a_log, gdn_dt_bias, gdn_norm_g, gdn_w_out, loss_target, m_ada_w, m_ada_b, m_pre_g, m_post_g, m_rel_bias, m_ab_w_in, m_ab_w_out, m_s5_a_re, m_s5_a_im, m_s5_log_dt, m_s5_b_re, m_s5_b_im, m_s5_c_re, m_s5_c_im, m_s5_d, m_s5_glu_w, m_s5_glu_b, m_gdn_w_in, m_gdn_conv, m_gdn_a_log, m_gdn_dt_bias, m_gdn_norm_g, m_gdn_w_out, v_ada_w, v_ada_b, v_pre_g, v_post_g, v_rel_bias, v_ab_w_in, v_ab_w_out, v_s5_a_re, v_s5_a_im, v_s5_log_dt, v_s5_b_re, v_s5_b_im, v_s5_c_re, v_s5_c_im, v_s5_d, v_s5_glu_w, v_s5_glu_b, v_gdn_w_in, v_gdn_conv, v_gdn_a_log, v_gdn_dt_bias, v_gdn_norm_g, v_gdn_w_out):
    w = dict(ada_w=ada_w, ada_b=ada_b, pre_g=pre_g, post_g=post_g, rel_bias=rel_bias, ab_w_in=ab_w_in, ab_w_out=ab_w_out,
             s5_a_re=s5_a_re, s5_a_im=s5_a_im, s5_log_dt=s5_log_dt, s5_b_re=s5_b_re, s5_b_im=s5_b_im, s5_c_re=s5_c_re, s5_c_im=s5_c_im,
             s5_d=s5_d, s5_glu_w=s5_glu_w, s5_glu_b=s5_glu_b, gdn_w_in=gdn_w_in, gdn_conv=gdn_conv, gdn_a_log=gdn_a_log,
             gdn_dt_bias=gdn_dt_bias, gdn_norm_g=gdn_norm_g, gdn_w_out=gdn_w_out)
    m = dict(ada_w=m_ada_w, ada_b=m_ada_b, pre_g=m_pre_g, post_g=m_post_g, rel_bias=m_rel_bias, ab_w_in=m_ab_w_in, ab_w_out=m_ab_w_out,
             s5_a_re=m_s5_a_re, s5_a_im=m_s5_a_im, s5_log_dt=m_s5_log_dt, s5_b_re=m_s5_b_re, s5_b_im=m_s5_b_im, s5_c_re=m_s5_c_re,
             s5_c_im=m_s5_c_im, s5_d=m_s5_d, s5_glu_w=m_s5_glu_w, s5_glu_b=m_s5_glu_b, gdn_w_in=m_gdn_w_in, gdn_conv=m_gdn_conv,
             gdn_a_log=m_gdn_a_log, gdn_dt_bias=m_gdn_dt_bias, gdn_norm_g=m_gdn_norm_g, gdn_w_out=m_gdn_w_out)
    v = dict(ada_w=v_ada_w, ada_b=v_ada_b, pre_g=v_pre_g, post_g=v_post_g, rel_bias=v_rel_bias, ab_w_in=v_ab_w_in, ab_w_out=v_ab_w_out,
             s5_a_re=v_s5_a_re, s5_a_im=v_s5_a_im, s5_log_dt=v_s5_log_dt, s5_b_re=v_s5_b_re, s5_b_im=v_s5_b_im, s5_c_re=v_s5_c_re,
             s5_c_im=v_s5_c_im, s5_d=v_s5_d, s5_glu_w=v_s5_glu_w, s5_glu_b=v_s5_glu_b, gdn_w_in=v_gdn_w_in, gdn_conv=v_gdn_conv,
             gdn_a_log=v_gdn_a_log, gdn_dt_bias=v_gdn_dt_bias, gdn_norm_g=v_gdn_norm_g, gdn_w_out=v_gdn_w_out)
    ix, iy, ic = _place()
    me = 4 * ix + 2 * iy + ic
    chip = 2 * ix + iy
    n_cols = ada_w.shape[2]

    mine_first = _pack([c, gdn_conv], 32)
    first = _own_slot(_all_gather8("gather_c_conv", mine_first), mine_first, me)
    c_all = first[:, 0:8].reshape(8, D_MODEL)
    conv_full = first[0::2, 8:32].reshape(4, C_CONV, n_cols).transpose(1, 0, 2).reshape(C_CONV, 4 * n_cols)
    mine_mod = _mod_local(c_all, ada_w)
    modl = _own_slot(_all_gather8("gather_mod", mine_mod), mine_mod, me)
    mod = lax.dynamic_index_in_dim(modl[0::2], me, axis=2, keepdims=False)
    mod = mod.transpose(1, 0, 2).reshape(2, 4 * n_cols) + ada_b

    comm = _WeightExchanges({name: w[name][0] for name in _SHARDED}, ic, chip)
    wd = {name: w[name] for name in _SMALL if name != "ada_b"}
    wd = {k: (a if k in ("pre_g", "post_g", "rel_bias") else a[0]) for k, a in wd.items()}
    wd["gdn_conv"] = conv_full
    wd.update(comm.first_weights())

    loss_local, grad_x, grads, dmod = _local_step(x[0], loss_target[0], mod, wd, comm)

    small_shapes = [w[name].shape for name in _SMALL] + [(C_CONV, 4 * n_cols)]
    small_rows = -(-sum(_rows128(int(np.prod(s))) for s in small_shapes) // 64) * 64
    per_dev, dmod_rows = small_rows // 8, _rows128(2 * 3 * D_MODEL)
    partial = _pack([dmod] + [grads[name] for name in _SMALL[1:]] + [grads["gdn_conv"]], small_rows)
    to_all = jnp.concatenate([partial[:dmod_rows], jnp.full((8, 128), loss_local, F32)], axis=0)
    outbound = jnp.concatenate([partial.reshape(8, per_dev, 128), jnp.broadcast_to(to_all[None], (8,) + to_all.shape)], axis=1)
    inbound = _own_slot(_all_to_all8("reduce_small_grads", outbound), lax.dynamic_index_in_dim(outbound, me, 0, keepdims=False), me)
    loss = functools.reduce(lambda a, b: a + b, [inbound[d, per_dev + dmod_rows, 0] for d in range(8)])
    my_rows = _slot_sum("sum_small_grads", inbound[:, :per_dev])
    g_small = _own_slot(_all_gather8("gather_small_grads", my_rows), my_rows, me).reshape(small_rows, 128)
    g_list = _unpack(g_small, small_shapes)
    out_g, out_d, out_m, out_v = {}, {}, {}, {}

    small =list(_SMALL) + ["gdn_conv"]
    small_g = [g.reshape(-1, g.shape[-1]) for g in g_list[:-1]] + [lax.dynamic_slice_in_dim(g_list[-1], chip * n_cols, n_cols, axis=1)]
    two_d = lambda a: a.reshape(-1, a.shape[-1])
    results = _adamw_many("adamw_small", [two_d(w[n]) for n in small], small_g, [two_d(m[n]) for n in small], [two_d(v[n]) for n in small])
    for name, g2d, (d_, m_, v_) in zip(small, small_g, results):
        out_g[name], out_d[name], out_m[name], out_v[name] = (a.reshape(w[name].shape) for a in (g2d, d_, m_, v_))

    rider = (comm.chip_partials(["ab_w_in"], comm.from_sibling["ab_w_in"]), "scatter")
    dmod_all = inbound[:, per_dev:per_dev + dmod_rows].reshape(8, 2, 4, n_cols)
    dmod_cols = lax.dynamic_index_in_dim(dmod_all, chip, axis=2, keepdims=False).transpose(1, 0, 2)
    g_ada = _ada_w_grad(c_all, dmod_cols).reshape(-1, n_cols)
    (d_, m_, v_), got = _adamw("adamw_ada_w", two_d(w["ada_w"]), g_ada, two_d(m["ada_w"]), two_d(v["ada_w"]), exchange=rider)
    out_g["ada_w"], out_d["ada_w"], out_m["ada_w"], out_v["ada_w"] = (a.reshape(w["ada_w"].shape) for a in (g_ada, d_, m_, v_))
    comm.received["ab_w_in"] = got[0]
    chip_1 = jnp.reshape(chip, (1,)).astype(jnp.int32)
    core_1 = jnp.reshape(ic, (1,)).astype(jnp.int32)
    reduced = [_chip_sum("sum_chips_" + name, comm.received[name], comm.partial[name], chip_1) for name in _SHARDED]
    for name, g_mine, g_sib in zip(_SHARDED, reduced, _sibling_exchange("reduce_share", reduced)):
        out_g[name], out_d[name], out_m[name], out_v[name] = _adamw_halves(
            "adamw_" + name, w[name], g_mine, g_sib, m[name], v[name], core_1)

    return (loss, grad_x[None], *[out_g[n] for n in _WEIGHTS], *[out_d[n] for n in _WEIGHTS],
            *[out_m[n] for n in _WEIGHTS], *[out_v[n] for n in _WEIGHTS])
```

```python
import functools
import math

import numpy as np
import jax
import jax.numpy as jnp
from jax import lax
from jax.experimental import pallas as pl
from jax.experimental.pallas import tpu as pltpu

F32 = jnp.float32
BF = jnp.bfloat16
HI = lax.Precision.HIGHEST
MESH = pl.DeviceIdType.MESH

D_MODEL = 1024
EPS = 1e-6
A_HEADS, A_HD, A_WIDTH, A_BLOCK = 8, 64, 512, 128
DILATIONS = (1, 4, 16)
N_KEYS = 128
REL_BUCKETS, REL_MAX_DIST = 32, 2048
B_WIDTH, B_GROUP, B_GROUPS, B_STATE = 512, 16, 32, 64
S5_LANES = 512
S5_TILES = 4
C_HEADS, C_DK, C_CHUNK, C_CONV = 8, 128, 64, 4
QKV = 3072
C_IN = QKV + 1024 + 2 * C_HEADS
C_IN_PAD = 4224
TM = 256
VMEM_LIMIT_BYTES = 56 * 1024 * 1024
ADAM_LR, ADAM_B1, ADAM_B2, ADAM_EPS, ADAM_WD, ADAM_STEP = 0.001, 0.9, 0.999, 1e-08, 0.01, 10
NEG = float(np.finfo(np.float32).min)


def _cp(*sem):
    return pltpu.CompilerParams(dimension_semantics=sem, vmem_limit_bytes=VMEM_LIMIT_BYTES)


def _bdot(a, b):
    return jnp.dot(a.astype(BF), b.astype(BF), preferred_element_type=F32)


def _bdot_nt(a, b):
    return lax.dot_general(a.astype(BF), b.astype(BF), (((1,), (1,)), ((), ())), preferred_element_type=F32)


def _bdot_tn(a, b):
    return lax.dot_general(a.astype(BF), b.astype(BF), (((0,), (0,)), ((), ())), preferred_element_type=F32)


def _hdot(a, b):
    return jnp.dot(a, b, precision=HI, preferred_element_type=F32)


def _bein(eq, a, b):
    return jnp.einsum(eq, a.astype(BF), b.astype(BF), preferred_element_type=F32)


def _row(tm, n):
    return pl.BlockSpec((tm, n), lambda i: (i, 0))


def _fix(shape):
    return pl.BlockSpec(shape, lambda i: (0,) * len(shape))


def _sds(*shape, dtype=F32):
    return jax.ShapeDtypeStruct(shape, dtype)


def _acc(ref, val):
    ref[...] += val


def _zero_at_first(refs, axis=0):
    @pl.when(pl.program_id(axis) == 0)
    def _():
        for r in refs:
            r[...] = jnp.zeros_like(r)


def _rms(x):
    return x * lax.rsqrt(jnp.mean(x * x, axis=-1, keepdims=True) + EPS)


def _pre_mod(x, g, scale, shift):
    return (_rms(x) * g) * (1.0 + scale) + shift


def _post_res(y, x, post_g, gate):
    return x + gate * (_rms(y) * post_g)


def _merge_gate(o1, o2, o3, l1, l2, l3, ga):
    m = jnp.maximum(jnp.maximum(l1, l2), l3)
    e1, e2, e3 = jnp.exp(l1 - m), jnp.exp(l2 - m), jnp.exp(l3 - m)
    inv = 1.0 / (e1 + e2 + e3)
    return ((e1 * inv) * o1 + (e2 * inv) * o2 + (e3 * inv) * o3) * jax.nn.silu(ga)


def _s5_gelu(ypre, u, d_skip):
    return jax.nn.gelu(ypre + d_skip * u)


def _s5_glu(yb, gl, gb):
    return yb * jax.nn.sigmoid(gl) * jax.nn.silu(gb)


def _l0_front(x, pre_g, scale, shift, w_in):
    s_len = x.shape[0]

    def body(x_ref, g_ref, sc_ref, sh_ref, w_ref, *out_refs):
        qkv_refs, (u_ref, ga_ref, gb_ref, h_ref) = out_refs[:9], out_refs[9:]
        hb = _pre_mod(x_ref[...], g_ref[...], sc_ref[...], sh_ref[...]).astype(BF)
        h_ref[...] = hb
        z = jnp.dot(hb, w_ref[...], preferred_element_type=F32)
        for a in range(3):
            piece = z[:, a * 512:(a + 1) * 512]
            for i, d in enumerate(DILATIONS):
                qkv_refs[3 * a + i][...] = _to_res(piece, d).astype(BF)
        u_ref[...] = z[:, 1536:2048]
        ga_ref[...] = z[:, 2048:2560]
        gb_ref[...] = z[:, 2560:3072]

    vec = _fix((1, D_MODEL))
    return pl.pallas_call(
        body, name="l0_front", grid=(s_len // TM,),
        in_specs=[_row(TM, D_MODEL), vec, vec, vec, _fix((D_MODEL, 3072))],
        out_specs=[_res_spec(d) for d in DILATIONS] * 3 + [_row(TM, 512)] * 3 + [_row(TM, D_MODEL)],
        out_shape=[_sds(*_res_shape(s_len, d), dtype=BF) for d in DILATIONS] * 3 + [_sds(s_len, 512)] * 3 + [_sds(s_len, D_MODEL, dtype=BF)],
        compiler_params=_cp("arbitrary"),
    )(x, pre_g, scale, shift, w_in)


def _front_bwd(name, x, pre_g, scale, shift, w_in, dres, parts, widths):
    s_len = x.shape[0]
    n_in = sum(len(p) for p in parts)
    n_cols = sum(widths)

    def body(*refs):
        x_ref, g_ref, sc_ref, sh_ref, w_ref, dres_ref = refs[:6]
        part_refs = refs[6:6 + n_in]
        dz_ref, dx_ref, dg_ref, dsc_ref, dsh_ref = refs[6 + n_in:]
        _zero_at_first([dg_ref, dsc_ref, dsh_ref])
        _, vjp = jax.vjp(_pre_mod, x_ref[...], g_ref[...], sc_ref[...], sh_ref[...])
        dh = jnp.zeros((TM, D_MODEL), F32)
        col, at = 0, 0
        for grp, width in zip(parts, widths):
            tile = lambda r: _from_res(r[...]) if len(r.shape) == 3 else r[...]
            dz = tile(part_refs[at])
            for r in part_refs[at + 1:at + len(grp)]:
                dz = dz + tile(r)
            at += len(grp)
            dzb = dz.astype(BF)
            dz_ref[:, col:col + width] = dzb
            dh = dh + lax.dot_general(dzb, w_ref[:, col:col + width], (((1,), (1,)), ((), ())), preferred_element_type=F32)
            col += width
        dx, dg, dsc, dsh = vjp(dh)
        dx_ref[...] = dx + dres_ref[...]
        _acc(dg_ref, dg)
        _acc(dsc_ref, dsc)
        _acc(dsh_ref, dsh)

    vec = _fix((1, D_MODEL))
    flat = [a for p in parts for a in p]
    return pl.pallas_call(
        body, name=name, grid=(s_len // TM,),
        in_specs=[_row(TM, D_MODEL), vec, vec, vec, _fix((D_MODEL, n_cols)), _row(TM, D_MODEL)]
        + [_res_spec(a.shape[0], a.shape[2]) if a.ndim == 3 else _row(TM, a.shape[1]) for a in flat],
        out_specs=[_row(TM, n_cols), _row(TM, D_MODEL), vec, vec, vec],
        out_shape=[_sds(s_len, n_cols, dtype=BF), _sds(s_len, D_MODEL), _sds(1, D_MODEL), _sds(1, D_MODEL), _sds(1, D_MODEL)],
        compiler_params=_cp("arbitrary"),
    )(x, pre_g, scale, shift, w_in, dres, *flat)


def _matmul_tn(name, a, b, tn):
    s_len, k_dim = a.shape
    n_dim = b.shape[1]
    ts = 1024

    def body(a_ref, b_ref, o_ref):
        _zero_at_first([o_ref], axis=1)
        o_ref[...] += lax.dot_general(a_ref[...], b_ref[...], (((0,), (0,)), ((), ())), preferred_element_type=F32)

    return pl.pallas_call(
        body, name=name, grid=(n_dim // tn, s_len // ts),
        in_specs=[pl.BlockSpec((ts, k_dim), lambda j, i: (i, 0)), pl.BlockSpec((ts, tn), lambda j, i: (i, j))],
        out_specs=pl.BlockSpec((k_dim, tn), lambda j, i: (0, j)),
        out_shape=_sds(k_dim, n_dim),
        compiler_params=_cp("arbitrary", "arbitrary"),
    )(a, b)


def _t5_bucket_np(dist):
    dist = np.maximum(dist, 0)
    max_exact = REL_BUCKETS // 2
    large = max_exact + (np.log(np.maximum(dist, 1) / max_exact)
                         / math.log(REL_MAX_DIST / max_exact) * (REL_BUCKETS - max_exact)).astype(np.int32)
    large = np.minimum(large, REL_BUCKETS - 1)
    return np.where(dist < max_exact, dist, large).astype(np.int32)


def _to_res(z, dil):
    if dil == 1:
        return z[None]
    return jnp.swapaxes(z.reshape(z.shape[0] // dil, dil, z.shape[1]), 0, 1)


def _from_res(z):
    if z.shape[0] == 1:
        return z[0]
    return jnp.swapaxes(z, 0, 1).reshape(z.shape[0] * z.shape[1], z.shape[2])


def _res_shape(s_len, dil, width=A_WIDTH):
    return (dil, s_len // dil, width)


def _res_spec(dil, width=A_WIDTH):
    return pl.BlockSpec((dil, TM // dil, width), lambda i: (0, i, 0))


def _bucket_table():
    qi = np.arange(A_BLOCK)[:, None]
    kj = np.arange(2 * A_BLOCK)[None, :]
    return np.stack([_t5_bucket_np((qi + A_BLOCK - kj) * d) for d in DILATIONS], 0)


def _attn_mask(first):
    qi = lax.broadcasted_iota(jnp.int32, (A_BLOCK, 2 * A_BLOCK), 0)
    kj = lax.broadcasted_iota(jnp.int32, (A_BLOCK, 2 * A_BLOCK), 1)
    rel = qi + A_BLOCK - kj
    return (rel >= 0) & (rel <= N_KEYS) & (jnp.logical_not(first) | (kj >= A_BLOCK))


def _attn_specs(nb, rev):
    per = 2 if nb % 2 == 0 else 1
    steps = nb // per
    n_of = (lambda i: steps - 1 - i) if rev else (lambda i: i)
    cur = pl.BlockSpec((None, per * A_BLOCK, A_WIDTH), lambda r, i: (r, n_of(i), 0))
    prev = pl.BlockSpec((None, A_BLOCK, A_WIDTH), lambda r, i: (r, jnp.maximum(per * n_of(i) - 1, 0), 0))
    bias = pl.BlockSpec((A_HEADS, A_BLOCK, 2 * A_BLOCK), lambda r, i: (0, 0, 0))
    return per, steps, cur, prev, bias


def _attn_fwd(q, k, v, bias, exchange=None):
    dil, t_len, _ = q.shape
    per, steps, cur, prev, bias_spec = _attn_specs(t_len // A_BLOCK, False)
    scale = A_HD ** -0.5

    def body(q_ref, kp_ref, kc_ref, vp_ref, vc_ref, b_ref, o_ref, l_ref):
        lane = lax.broadcasted_iota(jnp.int32, (1, 128), 1)
        for sub in range(per):
            rows = slice(sub * A_BLOCK, (sub + 1) * A_BLOCK)
            before = slice((sub - 1) * A_BLOCK, sub * A_BLOCK)
            mask = _attn_mask((pl.program_id(1) == 0) if sub == 0 else False)
            for hp in range(A_HEADS // 2):
                sl = slice(hp * 128, (hp + 1) * 128)
                qp = q_ref[rows, sl]
                kw = jnp.concatenate([kp_ref[:, sl] if sub == 0 else kc_ref[before, sl], kc_ref[rows, sl]], axis=0).astype(BF)
                vw = jnp.concatenate([vp_ref[:, sl] if sub == 0 else vc_ref[before, sl], vc_ref[rows, sl]], axis=0).astype(BF)
                outs, lses = [], []
                for j in range(2):
                    hm = (lane < 64) if j == 0 else (lane >= 64)
                    s = _bdot_nt(jnp.where(hm, qp, 0.0), kw) * scale
                    s = jnp.where(mask, s + b_ref[2 * hp + j], NEG)
                    m = jnp.max(s, axis=-1, keepdims=True)
                    p = jnp.exp(s - m)
                    den = jnp.sum(p, axis=-1, keepdims=True)
                    outs.append(_bdot(p, vw) / den)
                    lses.append(m + jnp.log(den))
                hm0 = lane < 64
                o_ref[rows, sl] = jnp.where(hm0, outs[0], outs[1])
                l_ref[rows, sl] = jnp.where(hm0, lses[0], lses[1])

    return _call_with_exchange(body, f"attn_fwd_d{dil}", (dil, steps), [cur, prev, cur, prev, cur, bias_spec], [cur, cur],
                               [_sds(dil, t_len, A_WIDTH)] * 2, [], (q, k, k, v, v, bias), exchange)


def _attn_bwd(q, k, v, bias, o, l, do, dl, exchange=None):
    dil, t_len, _ = q.shape
    per, steps, cur, prev, bias_spec = _attn_specs(t_len // A_BLOCK, True)
    scale = A_HD ** -0.5

    def body(q_ref, kp_ref, kc_ref, vp_ref, vc_ref, b_ref, o_ref, l_ref, do_ref, dl_ref,
             dq_ref, dk_ref, dv_ref, db_ref, ck_ref, cv_ref):
        _zero_at_first([ck_ref, cv_ref], axis=1)

        @pl.when((pl.program_id(0) == 0) & (pl.program_id(1) == 0))
        def _():
            db_ref[...] = jnp.zeros_like(db_ref)

        lane = lax.broadcasted_iota(jnp.int32, (1, 128), 1)
        for hp in range(A_HEADS // 2):
            sl = slice(hp * 128, (hp + 1) * 128)
            to_prev_k, to_prev_v = ck_ref[:, sl], cv_ref[:, sl]
            for sub in range(per - 1, -1, -1):
                rows = slice(sub * A_BLOCK, (sub + 1) * A_BLOCK)
                before = slice((sub - 1) * A_BLOCK, sub * A_BLOCK)
                mask = _attn_mask((pl.program_id(1) == steps - 1) if sub == 0 else False)
                qp = q_ref[rows, sl]
                kw = jnp.concatenate([kp_ref[:, sl] if sub == 0 else kc_ref[before, sl], kc_ref[rows, sl]], axis=0).astype(BF)
                vw = jnp.concatenate([vp_ref[:, sl] if sub == 0 else vc_ref[before, sl], vc_ref[rows, sl]], axis=0).astype(BF)
                op, lp, dop, dlp = o_ref[rows, sl], l_ref[rows, sl], do_ref[rows, sl], dl_ref[rows, sl]
                dq_acc = jnp.zeros((A_BLOCK, 128), F32)
                dk_acc = jnp.zeros((2 * A_BLOCK, 128), F32)
                dv_acc = jnp.zeros((2 * A_BLOCK, 128), F32)
                for j in range(2):
                    hm = (lane < 64) if j == 0 else (lane >= 64)
                    qm = jnp.where(hm, qp, 0.0)
                    s = _bdot_nt(qm, kw) * scale
                    s = jnp.where(mask, s + b_ref[2 * hp + j], NEG)
                    lse = jnp.max(jnp.where(hm, lp, NEG), axis=-1, keepdims=True)
                    p = jnp.exp(s - lse)
                    do_h = jnp.where(hm, dop, 0.0)
                    dd = jnp.sum(do_h * op, axis=-1, keepdims=True)
                    dlse = jnp.sum(jnp.where(hm, dlp, 0.0), axis=-1, keepdims=True)
                    ds = p * (_bdot_nt(do_h, vw) - dd + dlse)
                    dv_acc = dv_acc + _bdot_tn(p, do_h)
                    dq_acc = dq_acc + jnp.where(hm, _bdot(ds, kw), 0.0) * scale
                    dk_acc = dk_acc + _bdot_tn(ds, qm) * scale
                    db_ref[2 * hp + j] += ds
                dq_ref[rows, sl] = dq_acc
                dk_ref[rows, sl] = dk_acc[A_BLOCK:] + to_prev_k
                dv_ref[rows, sl] = dv_acc[A_BLOCK:] + to_prev_v
                to_prev_k, to_prev_v = dk_acc[:A_BLOCK], dv_acc[:A_BLOCK]
            ck_ref[:, sl] = to_prev_k
            cv_ref[:, sl] = to_prev_v

    return _call_with_exchange(
        body, f"attn_bwd_d{dil}", (dil, steps), [cur, prev, cur, prev, cur, bias_spec, cur, cur, cur, cur],
        [cur, cur, cur, bias_spec], [_sds(dil, t_len, A_WIDTH)] * 3 + [_sds(A_HEADS, A_BLOCK, 2 * A_BLOCK)],
        [pltpu.VMEM((A_BLOCK, A_WIDTH), F32)] * 2, (q, k, k, v, v, bias, o, l, do, dl), exchange)


def _attn_bias(rel_bias, table):
    def body(rb_ref, t_ref, *o_refs):
        for c in range(3):
            t = t_ref[c]
            acc = [jnp.zeros((A_BLOCK, 2 * A_BLOCK), F32) for _ in range(A_HEADS)]
            for b in range(REL_BUCKETS):
                hit = t == b
                acc = [jnp.where(hit, rb_ref[b, h], acc[h]) for h in range(A_HEADS)]
            for h in range(A_HEADS):
                o_refs[c][h] = acc[h]

    return pl.pallas_call(body, name="attn_bias", out_shape=[_sds(A_HEADS, A_BLOCK, 2 * A_BLOCK)] * 3,
                          in_specs=[pl.BlockSpec(memory_space=pltpu.SMEM), pl.BlockSpec(memory_space=pltpu.VMEM)],
                          compiler_params=pltpu.CompilerParams(vmem_limit_bytes=VMEM_LIMIT_BYTES))(rel_bias, table)


def _rel_bias_grad(dbs, idx_rows):
    n = A_BLOCK * 2 * A_BLOCK

    def body(d0_ref, d1_ref, d2_ref, idx_ref, o_ref):
        bucket = lax.broadcasted_iota(jnp.int32, (REL_BUCKETS, n), 0).astype(F32)
        acc = jnp.zeros((A_HEADS, REL_BUCKETS), F32)
        for c, db_ref in enumerate((d0_ref, d1_ref, d2_ref)):
            onehot = (idx_ref[c:c + 1, :] == bucket).astype(F32)
            acc = acc + lax.dot_general(db_ref[...], onehot, (((1,), (1,)), ((), ())), precision=HI, preferred_element_type=F32)
        o_ref[...] = acc

    return pl.pallas_call(body, name="rel_bias_grad", out_shape=_sds(A_HEADS, REL_BUCKETS),
                          compiler_params=pltpu.CompilerParams(vmem_limit_bytes=VMEM_LIMIT_BYTES))(
                              *[d.reshape(A_HEADS, n) for d in dbs], idx_rows)


def _s5_param_fn(a_re, a_im, log_dt, bt_re, bt_im):
    dt = jnp.exp(log_dt)
    mag = jnp.exp(dt * a_re)
    abar_r, abar_i = mag * jnp.cos(dt * a_im), mag * jnp.sin(dt * a_im)
    den = a_re * a_re + a_im * a_im
    fr = ((abar_r - 1.0) * a_re + abar_i * a_im) / den
    fi = (abar_i * a_re - (abar_r - 1.0) * a_im) / den
    row = lax.broadcasted_iota(jnp.int32, (B_WIDTH, B_GROUPS), 0)
    grp = lax.broadcasted_iota(jnp.int32, (B_WIDTH, B_GROUPS), 1)
    expand = ((row // B_GROUP) == grp).astype(F32)
    fr_e, fi_e = _hdot(expand, fr), _hdot(expand, fi)
    return abar_r, abar_i, fr_e * bt_re - fi_e * bt_im, fr_e * bt_im + fi_e * bt_re


def _s5_params(a_re, a_im, log_dt, bt_re, bt_im):
    def body(ar, ai, ld, br, bi, o1, o2, o3, o4):
        o1[...], o2[...], o3[...], o4[...] = _s5_param_fn(ar[...], ai[...], ld[...], br[...], bi[...])

    return pl.pallas_call(body, name="s5_params",
                          out_shape=[_sds(B_GROUPS, B_STATE)] * 2 + [_sds(B_WIDTH, B_STATE)] * 2)(a_re, a_im, log_dt, bt_re, bt_im)


def _s5_params_bwd(a_re, a_im, log_dt, bt_re, bt_im, d1, d2, d3, d4):
    def body(ar, ai, ld, br, bi, c1, c2, c3, c4, o1, o2, o3, o4, o5):
        _, vjp = jax.vjp(_s5_param_fn, ar[...], ai[...], ld[...], br[...], bi[...])
        o1[...], o2[...], o3[...], o4[...], o5[...] = vjp((c1[...], c2[...], c3[...], c4[...]))

    return pl.pallas_call(body, name="s5_params_bwd",
                          out_shape=[_sds(B_GROUPS, B_STATE)] * 2 + [_sds(B_GROUPS, 1)] + [_sds(B_WIDTH, B_STATE)] * 2,
                          )(a_re, a_im, log_dt, bt_re, bt_im, d1, d2, d3, d4)


def _pick_row(x, r):
    rows = lax.broadcasted_iota(jnp.int32, x.shape, 0)
    return jnp.sum(jnp.where(rows == r, x, 0.0), axis=0, keepdims=True)


S5_SEG = 8
S5_STEPS = 32
S5_WIDTH = S5_TILES * S5_LANES


def _seg_rows(block):
    return jnp.swapaxes(block, 0, 1).reshape(block.shape[1] * S5_SEG, block.shape[2])


def _seg_block(rows):
    return jnp.swapaxes(rows.reshape(rows.shape[0] // S5_SEG, S5_SEG, rows.shape[1]), 0, 1)


def _seq_specs(n_i, rev):
    at = (lambda i: n_i - 1 - i) if rev else (lambda i: i)
    seg = pl.BlockSpec((S5_SEG, S5_STEPS, B_WIDTH), lambda i: (0, at(i), 0))
    x_spec = pl.BlockSpec((S5_SEG * S5_STEPS, S5_WIDTH), lambda i: (at(i), 0))
    return seg, x_spec, _fix((S5_TILES, 128, S5_LANES)), _fix((S5_TILES, S5_LANES, 128)), _fix((1, S5_WIDTH)), _fix((S5_SEG, S5_WIDTH))


def _tile_dots(dot, lhs, w_ref, lhs_width):
    return jnp.concatenate([dot(lhs[:, t * lhs_width:(t + 1) * lhs_width], w_ref[t]) for t in range(S5_TILES)], axis=1)


def _s5_entries(name, end_r, end_i, abr, abi, steps, reverse):
    def body(er_ref, ei_ref, ar_ref, ai_ref, or_ref, oi_ref):
        pr, pi_ = ar_ref[...], ai_ref[...]
        for _ in range(int(math.log2(steps))):
            pr, pi_ = pr * pr - pi_ * pi_, 2.0 * pr * pi_
        er, ei = er_ref[...], ei_ref[...]
        rows = lax.broadcasted_iota(jnp.int32, er.shape, 0)
        cr, ci = jnp.zeros_like(pr), jnp.zeros_like(pr)
        out_r, out_i = jnp.zeros_like(er), jnp.zeros_like(er)
        for g in (range(S5_SEG - 2, -1, -1) if reverse else range(1, S5_SEG)):
            src = g + 1 if reverse else g - 1
            cr, ci = _pick_row(er, src) + pr * cr - pi_ * ci, _pick_row(ei, src) + pr * ci + pi_ * cr
            out_r, out_i = jnp.where(rows == g, cr, out_r), jnp.where(rows == g, ci, out_i)
        or_ref[...] = out_r
        oi_ref[...] = out_i

    return pl.pallas_call(body, name=name, out_shape=[_sds(*end_r.shape)] * 2)(end_r, end_i, abr, abi)


def _s5_seq_fwd(u, btr, bti, ctr, cti, abr, abi, entry, store, exchange=None):
    s_len = u.shape[0]
    seg_len = s_len // S5_SEG
    n_i = seg_len // S5_STEPS
    rows = S5_SEG * S5_STEPS

    def body(u_ref, btr_ref, bti_ref, ctr_ref, cti_ref, ar_ref, ai_ref, er_ref, ei_ref, *rest):
        if store:
            xr_ref, xi_ref, y_ref, endr_ref, endi_ref, sr_ref, si_ref = rest
        else:
            endr_ref, endi_ref, sr_ref, si_ref = rest
        i = pl.program_id(0)

        @pl.when(i == 0)
        def _():
            sr_ref[...] = er_ref[...]
            si_ref[...] = ei_ref[...]

        ar = jnp.broadcast_to(ar_ref[...], (S5_SEG, S5_WIDTH))
        ai = jnp.broadcast_to(ai_ref[...], (S5_SEG, S5_WIDTH))
        ub = _seg_rows(u_ref[...])
        br, bi = _tile_dots(_bdot, ub, btr_ref, 128), _tile_dots(_bdot, ub, bti_ref, 128)
        sr, si = sr_ref[...], si_ref[...]
        for s in range(S5_STEPS):
            at = slice(S5_SEG * s, S5_SEG * (s + 1))
            sr, si = ar * sr - ai * si + br[at], ar * si + ai * sr + bi[at]
            if store:
                xr_ref[at, :] = sr
                xi_ref[at, :] = si
        sr_ref[...] = sr
        si_ref[...] = si
        if store:
            y_ref[...] = _seg_block(_tile_dots(_bdot, xr_ref[...], ctr_ref, S5_LANES) - _tile_dots(_bdot, xi_ref[...], cti_ref, S5_LANES))

        @pl.when(i == n_i - 1)
        def _():
            endr_ref[...] = sr
            endi_ref[...] = si

    seg, x_spec, b_spec, c_spec, a_spec, e_spec = _seq_specs(n_i, False)
    ends = [_sds(S5_SEG, S5_WIDTH)] * 2
    full = [_sds(s_len, S5_WIDTH)] * 2 + [_sds(S5_SEG, seg_len, B_WIDTH)] if store else []
    return _call_with_exchange(
        body, "s5_scan_fwd" if store else "s5_ends_fwd", (n_i,),
        [seg, b_spec, b_spec, c_spec, c_spec, a_spec, a_spec, e_spec, e_spec],
        ([x_spec, x_spec, seg] if store else []) + [e_spec, e_spec], full + ends,
        [pltpu.VMEM((S5_SEG, S5_WIDTH), F32)] * 2,
        (u.reshape(S5_SEG, seg_len, B_WIDTH), btr, bti, ctr, cti, abr, abi, *entry), exchange)


def _s5_seq_bwd(dy, xr, xi, u, btr, bti, ctr, cti, abr, abi, g_entry, x_entry, full, exchange=None):
    s_len = dy.shape[0]
    seg_len = s_len // S5_SEG
    n_i = seg_len // S5_STEPS
    rows = S5_SEG * S5_STEPS

    def body(*refs):
        if full:
            (dy_ref, btr_ref, bti_ref, ctr_ref, cti_ref, ar_ref, ai_ref, ger_ref, gei_ref,
             xr_ref, xi_ref, xrp_ref, xip_ref, xer_ref, xei_ref, u_ref,
             du_ref, dbtr_ref, dbti_ref, dctr_ref, dcti_ref, dar_ref, dai_ref, str_ref, sti_ref,
             sr_ref, si_ref, gr_s, gi_s) = refs
        else:
            (dy_ref, btr_ref, bti_ref, ctr_ref, cti_ref, ar_ref, ai_ref, ger_ref, gei_ref, str_ref, sti_ref, sr_ref, si_ref) = refs
        i = pl.program_id(0)

        @pl.when(i == 0)
        def _():
            sr_ref[...] = ger_ref[...]
            si_ref[...] = gei_ref[...]
            if full:
                for r in (dbtr_ref, dbti_ref, dctr_ref, dcti_ref, dar_ref, dai_ref):
                    r[...] = jnp.zeros_like(r)

        ar = jnp.broadcast_to(ar_ref[...], (S5_SEG, S5_WIDTH))
        ai = -jnp.broadcast_to(ai_ref[...], (S5_SEG, S5_WIDTH))
        dyb = _seg_rows(dy_ref[...])
        gr, gi = _tile_dots(_bdot_nt, dyb, ctr_ref, 128), -_tile_dots(_bdot_nt, dyb, cti_ref, 128)
        sr, si = sr_ref[...], si_ref[...]
        for s in range(S5_STEPS - 1, -1, -1):
            at = slice(S5_SEG * s, S5_SEG * (s + 1))
            sr, si = ar * sr - ai * si + gr[at], ar * si + ai * sr + gi[at]
            if full:
                gr_s[at, :] = sr
                gi_s[at, :] = si
        sr_ref[...] = sr
        si_ref[...] = si

        @pl.when(i == n_i - 1)
        def _():
            str_ref[...] = sr
            sti_ref[...] = si

        if full:
            g_r, g_i = gr_s[...], gi_s[...]
            du_ref[...] = _seg_block(_tile_dots(_bdot_nt, g_r, btr_ref, S5_LANES) + _tile_dots(_bdot_nt, g_i, bti_ref, S5_LANES))
            ub = _seg_rows(u_ref[...])
            xr_b, xi_b = xr_ref[...], xi_ref[...]
            for t in range(S5_TILES):
                lanes, cols = slice(t * S5_LANES, (t + 1) * S5_LANES), slice(t * 128, (t + 1) * 128)
                dbtr_ref[t] += _bdot_tn(ub[:, cols], g_r[:, lanes])
                dbti_ref[t] += _bdot_tn(ub[:, cols], g_i[:, lanes])
                dctr_ref[t] += _bdot_tn(xr_b[:, lanes], dyb[:, cols])
                dcti_ref[t] -= _bdot_tn(xi_b[:, lanes], dyb[:, cols])
            first = i == n_i - 1
            xpr = jnp.concatenate([jnp.where(first, xer_ref[...], xrp_ref[...]), xr_b[:rows - S5_SEG]], axis=0)
            xpi = jnp.concatenate([jnp.where(first, xei_ref[...], xip_ref[...]), xi_b[:rows - S5_SEG]], axis=0)
            dar_ref[...] += jnp.sum(g_r * xpr + g_i * xpi, axis=0, keepdims=True)
            dai_ref[...] += jnp.sum(g_i * xpr - g_r * xpi, axis=0, keepdims=True)

    seg, x_spec, b_spec, c_spec, a_spec, e_spec = _seq_specs(n_i, True)
    halo = pl.BlockSpec((S5_SEG, S5_WIDTH), lambda i: (jnp.maximum((n_i - 1 - i) * S5_STEPS - 1, 0), 0))
    starts = [_sds(S5_SEG, S5_WIDTH)] * 2
    in_specs = [seg, b_spec, b_spec, c_spec, c_spec, a_spec, a_spec, e_spec, e_spec]
    args = [dy.reshape(S5_SEG, seg_len, B_WIDTH), btr, bti, ctr, cti, abr, abi, *g_entry]
    state = [pltpu.VMEM((S5_SEG, S5_WIDTH), F32)] * 2
    if not full:
        return _call_with_exchange(body, "s5_starts_bwd", (n_i,), in_specs, [e_spec, e_spec], starts, state, args, None)
    return _call_with_exchange(
        body, "s5_scan_bwd", (n_i,),
        in_specs + [x_spec, x_spec, halo, halo, e_spec, e_spec, seg],
        [seg, b_spec, b_spec, c_spec, c_spec, a_spec, a_spec, e_spec, e_spec],
        [_sds(S5_SEG, seg_len, B_WIDTH)] + [_sds(S5_TILES, 128, S5_LANES)] * 2 + [_sds(S5_TILES, S5_LANES, 128)] * 2
        + [_sds(1, S5_WIDTH)] * 2 + starts,
        state + [pltpu.VMEM((rows, S5_WIDTH), F32)] * 2,
        args + [xr, xi, xr, xi, *x_entry, u.reshape(S5_SEG, seg_len, B_WIDTH)], exchange)


def _blockdiag_b(bbar_t):
    blocks = bbar_t.reshape(S5_TILES, 8, B_GROUP, B_STATE)
    return jnp.einsum('jgmp,gh->jgmhp', blocks, jnp.eye(8, dtype=F32)).reshape(S5_TILES, 128, S5_LANES)


def _blockdiag_b_t(d):
    return jnp.einsum('jgmgp->jgmp', d.reshape(S5_TILES, 8, B_GROUP, 8, B_STATE)).reshape(B_WIDTH, B_STATE)


def _blockdiag_c(c):
    blocks = c.reshape(S5_TILES, 8, B_GROUP, B_STATE)
    return jnp.einsum('jgmp,gh->jhpgm', blocks, jnp.eye(8, dtype=F32)).reshape(S5_TILES, S5_LANES, 128)


def _blockdiag_c_t(d):
    return jnp.einsum('jgpgm->jgmp', d.reshape(S5_TILES, 8, B_STATE, 8, B_GROUP)).reshape(B_GROUPS, B_GROUP, B_STATE)


def _l0_out(os, ls, ga, gb, ypre, u, x, d_skip, glu_w, glu_b, w_out, post_g, gate, exchange=None):
    s_len = x.shape[0]

    def body(o0, o1, o2, l0, l1, l2, ga_ref, gb_ref, yp_ref, u_ref, x_ref, d_ref, gw_ref, gbias_ref, w_ref, pg_ref, gt_ref, x1_ref, y_ref):
        oa = _merge_gate(*[_from_res(r[...]) for r in (o0, o1, o2, l0, l1, l2)], ga_ref[...])
        yb = _s5_gelu(yp_ref[...], u_ref[...], d_ref[...])
        ob = _s5_glu(yb, _bdot(yb, gw_ref[...]) + gbias_ref[...], gb_ref[...])
        y = _bdot(oa, w_ref[0:512, :]) + _bdot(ob, w_ref[512:1024, :])
        y_ref[...] = y
        x1_ref[...] = _post_res(y, x_ref[...], pg_ref[...], gt_ref[...])

    vec, half = _fix((1, D_MODEL)), _fix((1, 512))
    return _call_with_exchange(
        body, "l0_out", (s_len // TM,),
        [_res_spec(d) for d in DILATIONS] * 2 + [_row(TM, 512)] * 4
        + [_row(TM, D_MODEL), half, _fix((512, 512)), half, _fix((D_MODEL, D_MODEL)), vec, vec],
        [_row(TM, D_MODEL)] * 2, [_sds(s_len, D_MODEL)] * 2, [],
        (*os, *ls, ga, gb, ypre, u, x, d_skip, glu_w, glu_b, w_out, post_g, gate), exchange)


def _l0_out_bwd(os, ls, ga, gb, ypre, u, x, y, d_skip, glu_w, glu_b, w_out, post_g, gate, dx1, exchange=None):
    s_len = x.shape[0]

    def body(o0, o1, o2, l0, l1, l2, ga_ref, gb_ref, yp_ref, u_ref, x_ref, y_ref, d_ref, gw_ref, gbias_ref, w_ref, pg_ref, gt_ref, dx1_ref,
             do0, do1, do2, dl0, dl1, dl2, dga_ref, dgb_ref, dyp_ref, du_ref, dd_ref, dgw_ref, dgbias_ref, dw_ref, dpg_ref, dgt_ref):
        _zero_at_first([dd_ref, dgw_ref, dgbias_ref, dw_ref, dpg_ref, dgt_ref])
        _, vjp2 = jax.vjp(_post_res, y_ref[...], x_ref[...], pg_ref[...], gt_ref[...])
        dy, _, dpg, dgt = vjp2(dx1_ref[...])
        _acc(dpg_ref, dpg)
        _acc(dgt_ref, dgt)
        oa, vjp_a = jax.vjp(_merge_gate, *[_from_res(r[...]) for r in (o0, o1, o2, l0, l1, l2)], ga_ref[...])
        yb, vjp_g = jax.vjp(_s5_gelu, yp_ref[...], u_ref[...], d_ref[...])
        gl = _bdot(yb, gw_ref[...]) + gbias_ref[...]
        ob, vjp_b = jax.vjp(_s5_glu, yb, gl, gb_ref[...])
        dw_ref[0:512, :] += _bdot_tn(oa, dy)
        dw_ref[512:1024, :] += _bdot_tn(ob, dy)
        d1, d2, d3, e1, e2, e3, dga = vjp_a(_bdot_nt(dy, w_ref[0:512, :]))
        for ref, val, d in zip((do0, do1, do2, dl0, dl1, dl2), (d1, d2, d3, e1, e2, e3), DILATIONS * 2):
            ref[...] = _to_res(val, d)
        dga_ref[...] = dga
        dyb, dgl, dgb = vjp_b(_bdot_nt(dy, w_ref[512:1024, :]))
        dgb_ref[...] = dgb
        dgw_ref[...] += _bdot_tn(yb, dgl)
        _acc(dgbias_ref, jnp.sum(dgl, axis=0, keepdims=True))
        dyp, du, dd = vjp_g(dyb + _bdot_nt(dgl, gw_ref[...]))
        dyp_ref[...] = dyp
        du_ref[...] = du
        _acc(dd_ref, dd)

    vec, half = _fix((1, D_MODEL)), _fix((1, 512))
    r5, r10 = _row(TM, 512), _row(TM, D_MODEL)
    res6 = [_res_spec(d) for d in DILATIONS] * 2
    return _call_with_exchange(
        body, "l0_out_bwd", (s_len // TM,),
        res6 + [r5] * 4 + [r10, r10, half, _fix((512, 512)), half, _fix((D_MODEL, D_MODEL)), vec, vec, r10],
        res6 + [r5] * 4 + [half, _fix((512, 512)), half, _fix((D_MODEL, D_MODEL)), vec, vec],
        [_sds(*_res_shape(s_len, d)) for d in DILATIONS] * 2 + [_sds(s_len, 512)] * 4
        + [_sds(1, 512), _sds(512, 512), _sds(1, 512), _sds(D_MODEL, D_MODEL), _sds(1, D_MODEL), _sds(1, D_MODEL)],
        [], (*os, *ls, ga, gb, ypre, u, x, y, d_skip, glu_w, glu_b, w_out, post_g, gate, dx1), exchange)


def _l1_front(x, pre_g, scale, shift, w_in):
    s_len = x.shape[0]

    def body(x_ref, g_ref, sc_ref, sh_ref, w_ref, raw_ref, gate_ref, ba_ref, h_ref):
        hb = _pre_mod(x_ref[...], g_ref[...], sc_ref[...], sh_ref[...]).astype(BF)
        h_ref[...] = hb
        z = jnp.dot(hb, w_ref[...], preferred_element_type=F32)
        raw_ref[...] = z[:, 0:QKV]
        gate_ref[...] = z[:, QKV:QKV + 1024]
        ba_ref[...] = z[:, QKV + 1024:C_IN_PAD]

    vec = _fix((1, D_MODEL))
    return pl.pallas_call(
        body, name="l1_front", grid=(s_len // TM,),
        in_specs=[_row(TM, D_MODEL), vec, vec, vec, _fix((D_MODEL, C_IN_PAD))],
        out_specs=[_row(TM, QKV), _row(TM, 1024), _row(TM, 128), _row(TM, D_MODEL)],
        out_shape=[_sds(s_len, QKV), _sds(s_len, 1024), _sds(s_len, 128), _sds(s_len, D_MODEL, dtype=BF)],
        compiler_params=_cp("arbitrary"),
    )(x, pre_g, scale, shift, w_in)


def _bg_fn(ba, alog_row, dtb_row):
    lane = lax.broadcasted_iota(jnp.int32, (1, 128), 1)
    g = -jnp.exp(alog_row) * jax.nn.softplus(ba + dtb_row)
    return jnp.where(lane < C_HEADS, jax.nn.sigmoid(ba), jnp.where(lane < 2 * C_HEADS, g, 0.0))


def _act_q(c):
    q = jax.nn.silu(c)
    return q * lax.rsqrt(jnp.sum(q * q, axis=-1, keepdims=True) + EPS) * (C_DK ** -0.5)


def _act_k(c):
    k = jax.nn.silu(c)
    return k * lax.rsqrt(jnp.sum(k * k, axis=-1, keepdims=True) + EPS)


def _act_of(s):
    return _act_q if s < 8 else (_act_k if s < 16 else jax.nn.silu)


def _conv_taps(prev8, tile_ref, sl, next8=None):
    rows = tile_ref.shape[0]
    head = jnp.concatenate([prev8, tile_ref[0:8, sl]], axis=0)
    tail = None if next8 is None else jnp.concatenate([tile_ref[rows - 8:rows, sl], next8], axis=0)
    taps = []
    for j in range(C_CONV):
        shift = C_CONV - 1 - j
        pieces = [head[8:] if shift == 0 else pltpu.roll(head, shift, 0)[8:], tile_ref[pl.ds(8 - shift, rows - 8), sl]]
        if tail is not None:
            pieces.append(tail[8:] if shift == 0 else pltpu.roll(tail, shift, 0)[8:])
        taps.append(jnp.concatenate(pieces, axis=0))
    return taps


def _gdn_prep(raw, ba, conv_w, alog_row, dtb_row):
    s_len = raw.shape[0]

    def body(raw_ref, halo_ref, ba_ref, w_ref, al_ref, dt_ref, qkv_ref, bg_ref):
        bg_ref[...] = _bg_fn(ba_ref[...], al_ref[...], dt_ref[...])
        has_prev = (pl.program_id(0) > 0).astype(F32)
        for s in range(24):
            sl = slice(s * 128, (s + 1) * 128)
            taps = _conv_taps(halo_ref[:, sl] * has_prev, raw_ref, sl)
            conv = w_ref[3:4, sl] * taps[3]
            for j in range(3):
                conv = conv + w_ref[j:j + 1, sl] * taps[j]
            qkv_ref[:, sl] = _act_of(s)(conv)

    halo = pl.BlockSpec((8, QKV), lambda i: (jnp.maximum(i * (TM // 8) - 1, 0), 0))
    row128 = _fix((1, 128))
    return pl.pallas_call(
        body, name="gdn_prep", grid=(s_len // TM,),
        in_specs=[_row(TM, QKV), halo, _row(TM, 128), _fix((C_CONV, QKV)), row128, row128],
        out_specs=[_row(TM, QKV), _row(TM, 128)],
        out_shape=[_sds(s_len, QKV), _sds(s_len, 128)],
        compiler_params=_cp("arbitrary"),
    )(raw, raw, ba, conv_w, alog_row, dtb_row)


def _gdn_prep_bwd(raw, ba, conv_w, alog_row, dtb_row, dq, dk, dv, dbg):
    s_len = raw.shape[0]
    n_tiles = s_len // TM

    def body(raw_ref, prev_ref, next_ref, ba_ref, w_ref, al_ref, dt_ref, dq_ref, dqn_ref, dk_ref, dkn_ref, dv_ref, dvn_ref, dbg_ref,
             draw_ref, dba_ref, dw_ref, dal_ref, ddt_ref, dconv_ref):
        _zero_at_first([dw_ref, dal_ref, ddt_ref])
        i = pl.program_id(0)
        _, vjp_bg = jax.vjp(_bg_fn, ba_ref[...], al_ref[...], dt_ref[...])
        dba, dal, ddt = vjp_bg(dbg_ref[...])
        dba_ref[...] = dba
        _acc(dal_ref, dal)
        _acc(ddt_ref, ddt)
        has_prev = (i > 0).astype(F32)
        has_next = (i < n_tiles - 1).astype(F32)
        ct_refs = ((dq_ref, dqn_ref), (dk_ref, dkn_ref), (dv_ref, dvn_ref))
        for s in range(24):
            sl = slice(s * 128, (s + 1) * 128)
            hl = slice((s % 8) * 128, (s % 8 + 1) * 128)
            tile_ref, nxt_ref = ct_refs[s // 8]
            taps = _conv_taps(prev_ref[:, sl] * has_prev, raw_ref, sl, next_ref[:, sl] * has_next)
            conv = w_ref[3:4, sl] * taps[3]
            for j in range(3):
                conv = conv + w_ref[j:j + 1, sl] * taps[j]
            ct = jnp.concatenate([tile_ref[:, hl], nxt_ref[:, hl] * has_next], axis=0)
            _, vjp_act = jax.vjp(_act_of(s), conv)
            dconv, = vjp_act(ct)
            dconv_ref[...] = dconv
            draw = w_ref[3:4, sl] * dconv[:TM]
            for j in range(3):
                draw = draw + w_ref[j:j + 1, sl] * dconv_ref[pl.ds(3 - j, TM), :]
            draw_ref[:, sl] = draw
            for j in range(4):
                dw_ref[j:j + 1, sl] += jnp.sum(dconv[:TM] * taps[j][:TM], axis=0, keepdims=True)

    prev = pl.BlockSpec((8, QKV), lambda i: (jnp.maximum(i * (TM // 8) - 1, 0), 0))
    nxt = lambda n: pl.BlockSpec((8, n), lambda i: (jnp.minimum((i + 1) * (TM // 8), s_len // 8 - 1), 0))
    row128 = _fix((1, 128))
    ct_specs = [_row(TM, 1024), nxt(1024)] * 3
    return pl.pallas_call(
        body, name="gdn_prep_bwd", grid=(n_tiles,),
        in_specs=[_row(TM, QKV), prev, nxt(QKV), _row(TM, 128), _fix((C_CONV, QKV)), row128, row128] + ct_specs + [_row(TM, 128)],
        out_specs=[_row(TM, QKV), _row(TM, 128), _fix((C_CONV, QKV)), row128, row128],
        out_shape=[_sds(s_len, QKV), _sds(s_len, 128), _sds(C_CONV, QKV), _sds(1, 128), _sds(1, 128)],
        scratch_shapes=[pltpu.VMEM((TM + 8, 128), F32)],
        compiler_params=_cp("arbitrary"),
    )(raw, raw, raw, ba, conv_w, alog_row, dtb_row, dq, dq, dk, dk, dv, dv, dbg)


def _tein(eq, a, b):
    return jnp.einsum(eq, a, b, precision=lax.Precision.HIGH, preferred_element_type=F32)


def _unit_lower_inverse(lower):
    ri = lax.broadcasted_iota(jnp.int32, (C_CHUNK, C_CHUNK), 0)
    ci = lax.broadcasted_iota(jnp.int32, (C_CHUNK, C_CHUNK), 1)
    eye = (ri == ci).astype(F32)[None]
    mm = functools.partial(_bein, 'hij,hjk->hik')
    same_block = lambda size: (ri // size == ci // size)[None]
    n_mat = jnp.where(same_block(4), -lower, 0.0)
    inv = mm(eye + n_mat, eye + mm(n_mat, n_mat))
    for size in (4, 8, 16, 32):
        below = jnp.where(same_block(2 * size) & jnp.logical_not(same_block(size)), lower, 0.0)
        inv = inv - mm(inv, mm(below, inv))
    inv = _tein('hij,hjk->hik', inv, 2.0 * eye - _tein('hij,hjk->hik', eye + lower, inv))
    return jnp.where((ri >= ci)[None], inv, 0.0)


@jax.custom_vjp
def _known_inverse(lower, inv):
    return inv


def _known_inverse_fwd(lower, inv):
    return inv, inv


def _known_inverse_bwd(inv, d_inv):
    d_lower = -_bein('hik,hjk->hij', _bein('hji,hjk->hik', inv, d_inv), inv)
    return d_lower, jnp.zeros_like(inv)


_known_inverse.defvjp(_known_inverse_fwd, _known_inverse_bwd)


def _gdn_local(q, k, v, bgs, inv_known=None):
    lane = lax.broadcasted_iota(jnp.int32, (1, 128), 1)
    ri = lax.broadcasted_iota(jnp.int32, (C_CHUNK, C_CHUNK), 0)
    ci = lax.broadcasted_iota(jnp.int32, (C_CHUNK, C_CHUNK), 1)
    row_id = lax.broadcasted_iota(jnp.int32, (128, C_CHUNK), 0)
    beta, gc, gcj = [], [], []
    for bg in bgs:
        gc_t = _hdot((ri >= ci).astype(F32), bg)
        gc_rows = gc_t.T
        for h in range(C_HEADS):
            beta.append(jnp.sum(jnp.where(lane == h, bg, 0.0), axis=-1, keepdims=True))
            gc.append(jnp.sum(jnp.where(lane == C_HEADS + h, gc_t, 0.0), axis=-1, keepdims=True))
            gcj.append(jnp.sum(jnp.where(row_id == C_HEADS + h, gc_rows, 0.0), axis=0, keepdims=True))
    beta, gc, gcj = jnp.stack(beta, axis=0), jnp.stack(gc, axis=0), jnp.stack(gcj, axis=0)
    tril, strict = (ri >= ci)[None], (ri > ci)[None]
    decay = jnp.exp(jnp.where(tril, gc - gcj, -1e30))
    kb = k * beta
    lower = jnp.where(strict, _bein('hid,hjd->hij', kb, k) * decay, 0.0)
    inv = _unit_lower_inverse(lower) if inv_known is None else _known_inverse(lower, inv_known)
    egc = jnp.exp(gc)
    u_c = _bein('hij,hjd->hid', inv, v * beta)
    w_c = _bein('hij,hjd->hid', inv, kb * egc)
    aqk = _bein('hid,hjd->hij', q, k) * decay
    rowi = lax.broadcasted_iota(jnp.int32, (1, C_CHUNK, 1), 1)
    g_last = jnp.sum(jnp.where(rowi == C_CHUNK - 1, gc, 0.0), axis=1, keepdims=True)
    kd = k * jnp.exp(g_last - gc)
    return (u_c, w_c, aqk, q * egc, kd, jnp.exp(g_last)), inv


def _gdn_state(local, state):
    u_c, w_c, aqk, qg, kd, dec = local
    v_new = u_c - _bein('hik,hkv->hiv', w_c, state)
    o = _bein('hik,hkv->hiv', qg, state) + _bein('hij,hjv->hiv', aqk, v_new)
    return o, state * dec + _bein('hik,hiv->hkv', kd, v_new)


C_SUB = 4


def _gdn_group(q, k, v, bgs, state, inv_known=None):
    local, inv = _gdn_local(q, k, v, bgs, inv_known)
    outs = []
    for s in range(len(bgs)):
        o, state = _gdn_state(tuple(t[s * C_HEADS:(s + 1) * C_HEADS] for t in local), state)
        outs.append(o)
    return outs, state, inv


def _heads(ref):
    return jnp.stack([ref[s * C_CHUNK:(s + 1) * C_CHUNK, h * C_DK:(h + 1) * C_DK] for s in range(C_SUB) for h in range(C_HEADS)], axis=0)


def _put_heads(ref, sub, val):
    rows = slice(sub * C_CHUNK, (sub + 1) * C_CHUNK)
    for h in range(C_HEADS):
        ref[rows, h * C_DK:(h + 1) * C_DK] = val[h]


def _gdn_specs(s_len, rev):
    rows = C_SUB * C_CHUNK
    n_g = s_len // rows
    at = (lambda i: n_g - 1 - i) if rev else (lambda i: i)
    col = lambda c: pl.BlockSpec((rows, 1024), lambda i: (at(i), c))
    row128 = pl.BlockSpec((rows, 128), lambda i: (at(i), 0))
    state = pl.BlockSpec((1, C_HEADS, C_DK, C_DK), lambda i: (at(i), 0, 0, 0))
    inv = pl.BlockSpec((1, C_SUB * C_HEADS, C_CHUNK, C_CHUNK), lambda i: (at(i), 0, 0, 0))
    return n_g, col, row128, state, inv


def _gdn_fwd(qkv, bg):
    s_len = qkv.shape[0]
    n_g, col, row128, state_spec, inv_spec = _gdn_specs(s_len, False)

    def body(q_ref, k_ref, v_ref, bg_ref, o_ref, ss_ref, inv_ref, st_ref):
        _zero_at_first([st_ref])
        s0 = st_ref[...]
        ss_ref[0] = s0
        bgs = [bg_ref[s * C_CHUNK:(s + 1) * C_CHUNK, :] for s in range(C_SUB)]
        outs, s2, inv = _gdn_group(_heads(q_ref), _heads(k_ref), _heads(v_ref), bgs, s0)
        st_ref[...] = s2
        inv_ref[0] = inv
        for s in range(C_SUB):
            _put_heads(o_ref, s, outs[s])

    return pl.pallas_call(
        body, name="gdn_fwd", grid=(n_g,),
        in_specs=[col(0), col(1), col(2), row128],
        out_specs=[col(0), state_spec, inv_spec],
        out_shape=[_sds(s_len, 1024), _sds(n_g, C_HEADS, C_DK, C_DK), _sds(n_g, C_SUB * C_HEADS, C_CHUNK, C_CHUNK)],
        scratch_shapes=[pltpu.VMEM((C_HEADS, C_DK, C_DK), F32)],
        compiler_params=_cp("arbitrary"),
    )(qkv, qkv, qkv, bg)


def _gdn_bwd(qkv, bg, states, invs, do):
    s_len = qkv.shape[0]
    n_g, col, row128, state_spec, inv_spec = _gdn_specs(s_len, True)

    def body(q_ref, k_ref, v_ref, bg_ref, ss_ref, inv_ref, do_ref, dq_ref, dk_ref, dv_ref, dbg_ref, ds_ref):
        _zero_at_first([ds_ref])
        inv_known = inv_ref[0]

        def group(q, k, v, bgs, st):
            outs, st2, _ = _gdn_group(q, k, v, bgs, st, inv_known)
            return outs, st2

        bgs = [bg_ref[s * C_CHUNK:(s + 1) * C_CHUNK, :] for s in range(C_SUB)]
        _, vjp = jax.vjp(group, _heads(q_ref), _heads(k_ref), _heads(v_ref), bgs, ss_ref[0])
        douts = [jnp.stack([do_ref[s * C_CHUNK:(s + 1) * C_CHUNK, h * C_DK:(h + 1) * C_DK] for h in range(C_HEADS)], axis=0)
                 for s in range(C_SUB)]
        dq, dk, dv, dbgs, ds = vjp((douts, ds_ref[...]))
        ds_ref[...] = ds
        for s in range(C_SUB):
            dbg_ref[s * C_CHUNK:(s + 1) * C_CHUNK, :] = dbgs[s]
            for ref, val in ((dq_ref, dq), (dk_ref, dk), (dv_ref, dv)):
                _put_heads(ref, s, val[s * C_HEADS:(s + 1) * C_HEADS])

    return pl.pallas_call(
        body, name="gdn_bwd", grid=(n_g,),
        in_specs=[col(0), col(1), col(2), row128, state_spec, inv_spec, col(0)],
        out_specs=[col(0), col(0), col(0), row128],
        out_shape=[_sds(s_len, 1024)] * 3 + [_sds(s_len, 128)],
        scratch_shapes=[pltpu.VMEM((C_HEADS, C_DK, C_DK), F32)],
        compiler_params=_cp("arbitrary"),
    )(qkv, qkv, qkv, bg, states, invs, do)


def _head_norm_gate(o, gate, norm_g):
    return (_rms(o) * norm_g) * jax.nn.silu(gate)


def _l1_out_fb(o, gate_c, x1, target, norm_g, w_out, post_g, gate):
    s_len = x1.shape[0]

    def body(o_ref, gc_ref, x1_ref, t_ref, ng_ref, w_ref, pg_ref, gt_ref,
             loss_ref, dres_ref, do_ref, dgc_ref, dw_ref, dng_ref, dpg_ref, dgt_ref):
        _zero_at_first([loss_ref, dw_ref, dng_ref, dpg_ref, dgt_ref])
        ng = ng_ref[...]
        ons, vjps = [], []
        for h in range(C_HEADS):
            sl = slice(h * C_DK, (h + 1) * C_DK)
            on, vjp_h = jax.vjp(_head_norm_gate, o_ref[:, sl], gc_ref[:, sl], ng)
            ons.append(on)
            vjps.append(vjp_h)
        on_all = jnp.concatenate(ons, axis=-1)
        y = _bdot(on_all, w_ref[...])
        x2, vjp2 = jax.vjp(_post_res, y, x1_ref[...], pg_ref[...], gt_ref[...])
        err = x2 - t_ref[...]
        _acc(loss_ref, jnp.full((1, 128), 0.5 * jnp.sum(jnp.mean(err * err, axis=-1)), F32))
        dx2 = err * (1.0 / D_MODEL)
        dy, _, dpg, dgt = vjp2(dx2)
        dres_ref[...] = dx2
        _acc(dpg_ref, dpg)
        _acc(dgt_ref, dgt)
        dw_ref[...] += _bdot_tn(on_all, dy)
        don = _bdot_nt(dy, w_ref[...])
        for h in range(C_HEADS):
            sl = slice(h * C_DK, (h + 1) * C_DK)
            do_h, dgc_h, dng = vjps[h](don[:, sl])
            do_ref[:, sl] = do_h
            dgc_ref[:, sl] = dgc_h
            _acc(dng_ref, dng)

    vec, r10 = _fix((1, D_MODEL)), _row(TM, D_MODEL)
    row128 = _fix((1, 128))
    return pl.pallas_call(
        body, name="l1_out_fb", grid=(s_len // TM,),
        in_specs=[r10, r10, r10, r10, row128, _fix((D_MODEL, D_MODEL)), vec, vec],
        out_specs=[row128, r10, r10, r10, _fix((D_MODEL, D_MODEL)), row128, vec, vec],
        out_shape=[_sds(1, 128), _sds(s_len, D_MODEL), _sds(s_len, D_MODEL), _sds(s_len, D_MODEL),
                   _sds(D_MODEL, D_MODEL), _sds(1, 128), _sds(1, D_MODEL), _sds(1, D_MODEL)],
        compiler_params=_cp("arbitrary"),
    )(o, gate_c, x1, target, norm_g, w_out, post_g, gate)


def _row_of(v, width, at):
    return jnp.zeros((1, width), F32).at[0, at:at + v.shape[-1]].set(v.reshape(-1))


def _local_step(x, target, mod, wd, comm=None):
    s_len = x.shape[0]
    shift0, scale0, gate0 = (mod[0:1, i * 1024:(i + 1) * 1024] for i in range(3))
    shift1, scale1, gate1 = (mod[1:2, i * 1024:(i + 1) * 1024] for i in range(3))
    pre_g0, pre_g1 = wd["pre_g"][0:1], wd["pre_g"][1:2]
    post_g0, post_g1 = wd["post_g"][0:1], wd["post_g"][1:2]
    w_in0 = wd["ab_w_in"].astype(BF)
    d_skip, glu_b = wd["s5_d"].reshape(1, 512), wd["s5_glu_b"].reshape(1, 512)
    norm_g = wd["gdn_norm_g"].reshape(1, 128)
    alog_row = _row_of(wd["gdn_a_log"], 128, C_HEADS)
    dtb_row = _row_of(wd["gdn_dt_bias"], 128, C_HEADS)
    conv_w = wd["gdn_conv"]

    a_re, a_im = wd["s5_a_re"], wd["s5_a_im"]
    log_dt = wd["s5_log_dt"].reshape(B_GROUPS, 1)
    bt_re = wd["s5_b_re"].transpose(0, 2, 1).reshape(B_WIDTH, B_STATE)
    bt_im = wd["s5_b_im"].transpose(0, 2, 1).reshape(B_WIDTH, B_STATE)
    abar_r, abar_i, bbar_r, bbar_i = _s5_params(a_re, a_im, log_dt, bt_re, bt_im)
    abr, abi = abar_r.reshape(1, -1), abar_i.reshape(1, -1)
    btr, bti = _blockdiag_b(bbar_r).astype(BF), _blockdiag_b(bbar_i).astype(BF)
    ctr, cti = _blockdiag_c(wd["s5_c_re"]).astype(BF), _blockdiag_c(wd["s5_c_im"]).astype(BF)

    table = _bucket_table()
    biases = _attn_bias(wd["rel_bias"], jnp.asarray(table))
    front = _l0_front(x, pre_g0, scale0, shift0, w_in0)
    qs, ks, vs = front[0:3], front[3:6], front[6:9]
    u, ga, gb, h0 = front[9:]
    riders = [None] * 4 if comm is None else comm.late_exchanges()
    os, ls, got = [], [], []
    for i in range(3):
        (o_d, l_d), g = _attn_fwd(qs[i], ks[i], vs[i], biases[i], exchange=riders[i])
        os.append(o_d)
        ls.append(l_d)
        got.append(g)
    seg_len = s_len // S5_SEG
    zero_state = (jnp.zeros((S5_SEG, S5_WIDTH), F32),) * 2
    ends, _ = _s5_seq_fwd(u, btr, bti, ctr, cti, abr, abi, zero_state, False)
    x_entry = _s5_entries("s5_entries_fwd", *ends, abr, abi, seg_len, False)
    (xr, xi, ypre3, _, _), g = _s5_seq_fwd(u, btr, bti, ctr, cti, abr, abi, x_entry, True, exchange=riders[3])
    got.append(g)
    ypre = ypre3.reshape(s_len, B_WIDTH)
    rider = None
    if comm is not None:
        mine0, mine1 = comm.late_halves(got)
        wd = {**wd, **comm.full_weights(["ab_w_out", "s5_glu_w"], mine0, _sibling_exchange("gather_w_sibling_l0", mine0))}
        rider = (mine1, "sibling")
    w_out0 = wd["ab_w_out"].astype(BF)
    glu_w = wd["s5_glu_w"].astype(BF)
    (x1, y0), theirs1 = _l0_out(os, ls, ga, gb, ypre, u, x, d_skip, glu_w, glu_b, w_out0, post_g0, gate0, exchange=rider)
    if comm is not None:
        wd = {**wd, **comm.full_weights(["gdn_w_in", "gdn_w_out"], mine1, theirs1)}
    w_in1 = wd["gdn_w_in"]
    if w_in1.shape[1] == C_IN:
        w_in1 = jnp.concatenate([w_in1, jnp.zeros((D_MODEL, C_IN_PAD - C_IN), w_in1.dtype)], axis=1)
    w_in1 = w_in1.astype(BF)
    w_out1 = wd["gdn_w_out"].astype(BF)

    raw, gate_c, ba, h1 = _l1_front(x1, pre_g1, scale1, shift1, w_in1)
    qkv, bg = _gdn_prep(raw, ba, conv_w, alog_row, dtb_row)
    o_gdn, states, invs = _gdn_fwd(qkv, bg)
    loss_row, dres1, do_gdn, dgate_c, dw_out1, dnorm_g, dpost_g1, dgate1 = _l1_out_fb(
        o_gdn, gate_c, x1, target, norm_g, w_out1, post_g1, gate1)

    dq1, dk1, dv1, dbg = _gdn_bwd(qkv, bg, states, invs, do_gdn)
    draw, dba, dconv_w, dalog_row, ddtb_row = _gdn_prep_bwd(raw, ba, conv_w, alog_row, dtb_row, dq1, dk1, dv1, dbg)
    dz1, dx1, dpre_g1, dscale1, dshift1 = _front_bwd(
        "l1_front_bwd", x1, pre_g1, scale1, shift1, w_in1, dres1, [[draw], [dgate_c], [dba]], [QKV, 1024, 128])
    dw_in1 = _matmul_tn("l1_dw_in", h1, dz1, 1408)

    l1_names, l0_names = ["gdn_w_in", "gdn_w_out"], ["ab_w_out", "s5_glu_w"]
    rider = None if comm is None else (comm.split_halves({"gdn_w_in": dw_in1, "gdn_w_out": dw_out1}), "sibling")
    l0b, from_sibling1 = _l0_out_bwd(os, ls, ga, gb, ypre, u, x, y0, d_skip, glu_w, glu_b, w_out0, post_g0, gate0, dx1, exchange=rider)
    dos, dls = l0b[0:3], l0b[3:6]
    dga, dgb, dypre, du_skip, dd_skip, dglu_w, dglu_b, dw_out0, dpost_g0, dgate0 = l0b[6:]
    rider = None if comm is None else (comm.split_halves({"ab_w_out": dw_out0, "s5_glu_w": dglu_w}), "sibling")
    starts, _ = _s5_seq_bwd(dypre, None, None, None, btr, bti, ctr, cti, abr, abi, zero_state, None, False)
    g_entry = _s5_entries("s5_entries_bwd", *starts, abr, -abi, seg_len, True)
    (du3, dbtr, dbti, dctr, dcti, dabr, dabi, _, _), from_sibling0 = _s5_seq_bwd(
        dypre, xr, xi, u, btr, bti, ctr, cti, abr, abi, g_entry, x_entry, True, exchange=rider)
    du_scan = du3.reshape(s_len, B_WIDTH)
    riders = [None] * 3
    if comm is not None:
        riders = [(comm.chip_partials(l1_names, from_sibling1), "scatter"), (comm.chip_partials(l0_names, from_sibling0), "scatter"), None]
    dqs, dks, dvs, dbs = [], [], [], []
    for i in range(3):
        (dq_d, dk_d, dv_d, db_d), got_d = _attn_bwd(qs[i], ks[i], vs[i], biases[i], os[i], ls[i], dos[i], dls[i], exchange=riders[i])
        dqs.append(dq_d)
        dks.append(dk_d)
        dvs.append(dv_d)
        dbs.append(db_d)
        if comm is not None and riders[i] is not None:
            comm.received.update(zip((l1_names, l0_names)[i], got_d))
    parts = [dqs, dks, dvs, [du_skip, du_scan], [dga], [dgb]]
    dz0, grad_x, dpre_g0, dscale0, dshift0 = _front_bwd(
        "l0_front_bwd", x, pre_g0, scale0, shift0, w_in0, dx1, parts, [512] * 6)
    dw_in0 = _matmul_tn("l0_dw_in", h0, dz0, 768)

    idx_rows = jnp.asarray(table.reshape(3, -1), F32)
    drel = _rel_bias_grad(dbs, idx_rows).T
    da_re, da_im, dlog_dt, dbt_re, dbt_im = _s5_params_bwd(
        a_re, a_im, log_dt, bt_re, bt_im, dabr.reshape(B_GROUPS, B_STATE), dabi.reshape(B_GROUPS, B_STATE),
        _blockdiag_b_t(dbtr), _blockdiag_b_t(dbti))
    unb = lambda d: d.reshape(B_GROUPS, B_GROUP, B_STATE).transpose(0, 2, 1)
    grads = {
        "pre_g": jnp.concatenate([dpre_g0, dpre_g1], 0), "post_g": jnp.concatenate([dpost_g0, dpost_g1], 0),
        "rel_bias": drel, "ab_w_in": dw_in0, "ab_w_out": dw_out0,
        "s5_a_re": da_re, "s5_a_im": da_im, "s5_log_dt": dlog_dt.reshape(B_GROUPS),
        "s5_b_re": unb(dbt_re), "s5_b_im": unb(dbt_im),
        "s5_c_re": _blockdiag_c_t(dctr), "s5_c_im": _blockdiag_c_t(dcti),
        "s5_d": dd_skip.reshape(512), "s5_glu_w": dglu_w, "s5_glu_b": dglu_b.reshape(512),
        "gdn_w_in": dw_in1[:, :C_IN], "gdn_conv": dconv_w,
        "gdn_a_log": dalog_row[0, C_HEADS:2 * C_HEADS], "gdn_dt_bias": ddtb_row[0, C_HEADS:2 * C_HEADS],
        "gdn_norm_g": dnorm_g.reshape(128), "gdn_w_out": dw_out1,
    }
    dmod = jnp.concatenate([jnp.concatenate([dshift0, dscale0, dgate0], 1), jnp.concatenate([dshift1, dscale1, dgate1], 1)], 0)
    return loss_row[0, 0], grad_x, grads, dmod


def _place():
    return lax.axis_index("x"), lax.axis_index("y"), lax.axis_index("c")


def _flip(v, bit):
    return 1 - v if bit else v


def _hbm_call(name, body, arrs, out_shapes, n_sem):
    any_spec = pl.BlockSpec(memory_space=pl.ANY)
    return pl.pallas_call(
        body, name=name,
        in_specs=[any_spec] * len(arrs), out_specs=[any_spec] * len(out_shapes), out_shape=out_shapes,
        scratch_shapes=[pltpu.SemaphoreType.DMA((n_sem,)), pltpu.SemaphoreType.DMA((n_sem,))],
    )(*arrs)


def _own_slot(gathered, own, slot):
    idx = lax.broadcasted_iota(jnp.int32, (gathered.shape[0],) + (1,) * own.ndim, 0)
    return jnp.where(idx == slot, own[None], gathered)


def _all_gather8(name, arr):
    def body(x_ref, out_ref, send_sems, recv_sems):
        x, y, c = _place()
        me = 4 * x + 2 * y + c
        sends, recvs = [], []
        for m in range(1, 8):
            peer = (_flip(x, m & 4), _flip(y, m & 2), _flip(c, m & 1))
            sends.append(pltpu.make_async_remote_copy(x_ref, out_ref.at[me], send_sems.at[m - 1], recv_sems.at[m - 1],
                                                      device_id=peer, device_id_type=MESH))
            recvs.append(pltpu.make_async_remote_copy(x_ref, out_ref.at[4 * peer[0] + 2 * peer[1] + peer[2]], send_sems.at[m - 1],
                                                      recv_sems.at[m - 1], device_id=peer, device_id_type=MESH))
        for cp in sends:
            cp.start()
        for cp in recvs:
            cp.wait_recv()
        for cp in sends:
            cp.wait_send()

    return _hbm_call(name, body, [arr], [jax.ShapeDtypeStruct((8,) + arr.shape, arr.dtype)], 7)[0]


def _all_to_all8(name, arr):
    def body(x_ref, out_ref, send_sems, recv_sems):
        x, y, c = _place()
        me = 4 * x + 2 * y + c
        sends, recvs = [], []
        for m in range(1, 8):
            peer = (_flip(x, m & 4), _flip(y, m & 2), _flip(c, m & 1))
            peer_id = 4 * peer[0] + 2 * peer[1] + peer[2]
            sends.append(pltpu.make_async_remote_copy(x_ref.at[peer_id], out_ref.at[me], send_sems.at[m - 1], recv_sems.at[m - 1],
                                                      device_id=peer, device_id_type=MESH))
            recvs.append(pltpu.make_async_remote_copy(x_ref.at[peer_id], out_ref.at[peer_id], send_sems.at[m - 1], recv_sems.at[m - 1],
                                                      device_id=peer, device_id_type=MESH))
        for cp in sends:
            cp.start()
        for cp in recvs:
            cp.wait_recv()
        for cp in sends:
            cp.wait_send()

    return _hbm_call(name, body, [arr], [jax.ShapeDtypeStruct(arr.shape, arr.dtype)], 7)[0]


def _chip_copies(ins, outs, send_sems, recv_sems, scatter):
    x, y, c = _place()
    mine = 2 * x + y
    sends, recvs = [], []
    for a in range(len(ins)):
        for m in range(1, 4):
            px, py = _flip(x, m & 2), _flip(y, m & 1)
            k = 3 * a + m - 1
            src = ins[a].at[2 * px + py] if scatter else ins[a]
            sends.append(pltpu.make_async_remote_copy(src, outs[a].at[mine], send_sems.at[k], recv_sems.at[k],
                                                      device_id=(px, py, c), device_id_type=MESH))
            recvs.append(pltpu.make_async_remote_copy(src, outs[a].at[2 * px + py], send_sems.at[k], recv_sems.at[k],
                                                      device_id=(px, py, c), device_id_type=MESH))
    return sends, recvs


def _chip_shapes(arrs, scatter):
    return [jax.ShapeDtypeStruct(a.shape if scatter else (4,) + a.shape, a.dtype) for a in arrs]


def _chip_exchange(name, arrs, scatter):
    n = len(arrs)

    def body(*refs):
        sends, recvs = _chip_copies(refs[:n], refs[n:2 * n], refs[2 * n], refs[2 * n + 1], scatter)
        for cp in sends:
            cp.start()
        for cp in recvs:
            cp.wait_recv()
        for cp in sends:
            cp.wait_send()

    return _hbm_call(name, body, arrs, _chip_shapes(arrs, scatter), 3 * n)


def _call_with_exchange(body, name, grid, in_specs, out_specs, out_shape, scratch_shapes, args, exchange):
    if exchange is None:
        return pl.pallas_call(body, name=name, grid=grid, in_specs=in_specs, out_specs=out_specs, out_shape=out_shape,
                              scratch_shapes=scratch_shapes, compiler_params=_cp(*["arbitrary"] * len(grid)))(*args), []
    arrs, kind = exchange
    n_in, n_out, n_ex, n_scr = len(in_specs), len(out_specs), len(arrs), len(scratch_shapes)
    n_sem = n_ex if kind == "sibling" else 3 * n_ex
    ex_shapes = [jax.ShapeDtypeStruct(a.shape, a.dtype) for a in arrs] if kind == "sibling" else _chip_shapes(arrs, kind == "scatter")

    def fused(*refs):
        ins, ex_in = refs[:n_in], refs[n_in:n_in + n_ex]
        outs, ex_out = refs[n_in + n_ex:n_in + n_ex + n_out], refs[n_in + n_ex + n_out:n_in + 2 * n_ex + n_out]
        rest = refs[n_in + 2 * n_ex + n_out:]
        if kind == "sibling":
            sends = recvs = _sibling_copies(ex_in, ex_out, rest[n_scr], rest[n_scr + 1])
        else:
            sends, recvs = _chip_copies(ex_in, ex_out, rest[n_scr], rest[n_scr + 1], kind == "scatter")
        first, last = pl.program_id(0) == 0, pl.program_id(0) == grid[0] - 1
        for k in range(1, len(grid)):
            first, last = first & (pl.program_id(k) == 0), last & (pl.program_id(k) == grid[k] - 1)

        @pl.when(first)
        def _():
            for cp in sends:
                cp.start()

        body(*ins, *outs, *rest[:n_scr])

        @pl.when(last)
        def _():
            for cp in recvs:
                cp.wait_recv()
            for cp in sends:
                cp.wait_send()

    any_spec = pl.BlockSpec(memory_space=pl.ANY)
    res = pl.pallas_call(
        fused, name=name, grid=grid, in_specs=list(in_specs) + [any_spec] * n_ex, out_specs=list(out_specs) + [any_spec] * n_ex,
        out_shape=list(out_shape) + ex_shapes,
        scratch_shapes=list(scratch_shapes) + [pltpu.SemaphoreType.DMA((n_sem,))] * 2,
        compiler_params=_cp(*["arbitrary"] * len(grid)))(*args, *arrs)
    return res[:n_out], res[n_out:]


def _sibling_copies(ins, outs, send_sems, recv_sems):
    x, y, c = _place()
    return [pltpu.make_async_remote_copy(ins[a], outs[a], send_sems.at[a], recv_sems.at[a],
                                         device_id=(x, y, 1 - c), device_id_type=MESH) for a in range(len(ins))]


def _sibling_exchange(name, arrs):
    n = len(arrs)

    def body(*refs):
        copies = _sibling_copies(refs[:n], refs[n:2 * n], refs[2 * n], refs[2 * n + 1])
        for cp in copies:
            cp.start()
        for cp in copies:
            cp.wait_recv()
        for cp in copies:
            cp.wait_send()

    return _hbm_call(name, body, arrs, [jax.ShapeDtypeStruct(a.shape, a.dtype) for a in arrs], n)


def _row_tile(rows):
    for t in (256, 128, 64, 32, 16, 8):
        if rows % t == 0:
            return t
    return rows


def _pair_sum(name, a, b, out_dtype):
    rows, cols = a.shape
    tr = _row_tile(rows)

    def body(a_ref, b_ref, o_ref):
        o_ref[...] = (a_ref[...] + b_ref[...]).astype(out_dtype)

    return pl.pallas_call(body, name=name, grid=(rows // tr,), in_specs=[_row(tr, cols)] * 2, out_specs=_row(tr, cols),
                          out_shape=_sds(rows, cols, dtype=out_dtype), compiler_params=_cp("arbitrary"))(a, b)


def _chip_sum(name, recv, partial, mine):
    n, rows, cols = recv.shape
    tr = _row_tile(rows)

    def body(mine_ref, *refs):
        own = refs[n][0].astype(F32)
        acc = None
        for s in range(n):
            term = jnp.where(mine_ref[0] == s, own, refs[s][0].astype(F32))
            acc = term if acc is None else acc + term
        refs[-1][...] = acc

    def slot_spec(s):
        return pl.BlockSpec((1, tr, cols), lambda i, m: (jnp.where(m[0] == s, (s + 1) % n, s), i, 0))

    grid_spec = pltpu.PrefetchScalarGridSpec(
        num_scalar_prefetch=1, grid=(rows // tr,),
        in_specs=[slot_spec(s) for s in range(n)] + [pl.BlockSpec((1, tr, cols), lambda i, m: (m[0], i, 0))],
        out_specs=pl.BlockSpec((tr, cols), lambda i, m: (i, 0)))
    return pl.pallas_call(body, name=name, grid_spec=grid_spec, out_shape=_sds(rows, cols),
                          compiler_params=_cp("arbitrary"))(mine, *([recv] * n), partial)


def _slot_sum(name, arr):
    n, rows, cols = arr.shape
    tr = _row_tile(rows)

    def body(*refs):
        acc = refs[0][0]
        for r in refs[1:-1]:
            acc = acc + r[0]
        refs[-1][...] = acc

    specs = [pl.BlockSpec((1, tr, cols), functools.partial(lambda s, i: (s, i, 0), s)) for s in range(n)]
    return pl.pallas_call(body, name=name, grid=(rows // tr,), in_specs=specs, out_specs=_row(tr, cols),
                          out_shape=_sds(rows, cols), compiler_params=_cp("arbitrary"))(*([arr] * n))


def _adamw(name, w, g, m, v, exchange=None):
    rows, cols = w.shape
    tr = _row_tile(rows)

    def body(w_ref, g_ref, m_ref, v_ref, d_ref, nm_ref, nv_ref):
        g_ = g_ref[...]
        m_ = ADAM_B1 * m_ref[...] + (1.0 - ADAM_B1) * g_
        v_ = ADAM_B2 * v_ref[...] + (1.0 - ADAM_B2) * (g_ * g_)
        m_hat = m_ / (1.0 - ADAM_B1 ** ADAM_STEP)
        v_hat = v_ / (1.0 - ADAM_B2 ** ADAM_STEP)
        d_ref[...] = -ADAM_LR * (m_hat / (jnp.sqrt(v_hat) + ADAM_EPS) + ADAM_WD * w_ref[...])
        nm_ref[...] = m_
        nv_ref[...] = v_

    spec = _row(tr, cols)
    return _call_with_exchange(body, name, (rows // tr,), [spec] * 4, [spec] * 3, [_sds(rows, cols)] * 3, [], (w, g, m, v), exchange)


def _adamw_many(name, ws, gs, ms, vs):
    n = len(ws)

    def body(*refs):
        for i in range(n):
            w_ref, g_ref, m_ref, v_ref = (refs[k * n + i] for k in range(4))
            d_ref, nm_ref, nv_ref = (refs[(4 + k) * n + i] for k in range(3))
            g_ = g_ref[...]
            m_ = ADAM_B1 * m_ref[...] + (1.0 - ADAM_B1) * g_
            v_ = ADAM_B2 * v_ref[...] + (1.0 - ADAM_B2) * (g_ * g_)
            m_hat = m_ / (1.0 - ADAM_B1 ** ADAM_STEP)
            v_hat = v_ / (1.0 - ADAM_B2 ** ADAM_STEP)
            d_ref[...] = -ADAM_LR * (m_hat / (jnp.sqrt(v_hat) + ADAM_EPS) + ADAM_WD * w_ref[...])
            nm_ref[...] = m_
            nv_ref[...] = v_

    shapes = [_sds(*a.shape) for a in ws]
    res = pl.pallas_call(body, name=name, out_shape=shapes * 3,
                         compiler_params=pltpu.CompilerParams(vmem_limit_bytes=VMEM_LIMIT_BYTES))(*ws, *gs, *ms, *vs)
    return [(res[i], res[n + i], res[2 * n + i]) for i in range(n)]


def _adamw_halves(name, w, g_mine, g_sibling, m, v, core):
    _, rows, cols = w.shape
    half = rows // 2
    tr = _row_tile(half)
    per_half = half // tr

    def body(core_ref, w_ref, gm_ref, gs_ref, m_ref, v_ref, g_ref, d_ref, nm_ref, nv_ref):
        g_ = jnp.where(pl.program_id(0) // per_half == core_ref[0], gm_ref[...], gs_ref[...])
        m_ = ADAM_B1 * m_ref[...] + (1.0 - ADAM_B1) * g_
        v_ = ADAM_B2 * v_ref[...] + (1.0 - ADAM_B2) * (g_ * g_)
        m_hat = m_ / (1.0 - ADAM_B1 ** ADAM_STEP)
        v_hat = v_ / (1.0 - ADAM_B2 ** ADAM_STEP)
        g_ref[...] = g_
        d_ref[...] = -ADAM_LR * (m_hat / (jnp.sqrt(v_hat) + ADAM_EPS) + ADAM_WD * w_ref[...])
        nm_ref[...] = m_
        nv_ref[...] = v_

    full = pl.BlockSpec((None, tr, cols), lambda i, c: (0, i, 0))
    in_half = pl.BlockSpec((tr, cols), lambda i, c: (i % per_half, 0))
    grid_spec = pltpu.PrefetchScalarGridSpec(num_scalar_prefetch=1, grid=(rows // tr,),
                                             in_specs=[full, in_half, in_half, full, full], out_specs=[full] * 4)
    return pl.pallas_call(body, name=name, grid_spec=grid_spec, out_shape=[_sds(1, rows, cols)] * 4,
                          compiler_params=_cp("arbitrary"))(core, w, g_mine, g_sibling, m, v)


def _mod_local(c_all, ada_w):
    def body(c_ref, w_ref, o_ref):
        c_act = jax.nn.silu(c_ref[...])
        for l in range(2):
            o_ref[l] = _hdot(c_act, w_ref[l])

    return pl.pallas_call(body, name="mod_local", out_shape=_sds(2, 8, ada_w.shape[2]),
                          compiler_params=pltpu.CompilerParams(vmem_limit_bytes=VMEM_LIMIT_BYTES))(c_all, ada_w)


def _ada_w_grad(c_all, dmod_cols):
    def body(c_ref, d_ref, o_ref):
        c_act = jax.nn.silu(c_ref[...])
        for l in range(2):
            o_ref[l] = lax.dot_general(c_act, d_ref[l], (((0,), (0,)), ((), ())), precision=HI, preferred_element_type=F32)

    return pl.pallas_call(body, name="ada_w_grad", out_shape=_sds(2, D_MODEL, dmod_cols.shape[2]),
                          compiler_params=pltpu.CompilerParams(vmem_limit_bytes=VMEM_LIMIT_BYTES))(c_all, dmod_cols)


_SMALL = ("ada_b", "pre_g", "post_g", "rel_bias", "s5_a_re", "s5_a_im", "s5_log_dt", "s5_b_re", "s5_b_im", "s5_c_re", "s5_c_im",
          "s5_d", "s5_glu_b", "gdn_a_log", "gdn_dt_bias", "gdn_norm_g")
_SHARDED = ("ab_w_in", "ab_w_out", "s5_glu_w", "gdn_w_in", "gdn_w_out")
_COL_SHARDED = ("ab_w_in", "gdn_w_in")
_WEIGHTS = ("ada_w", "ada_b", "pre_g", "post_g", "rel_bias", "ab_w_in", "ab_w_out", "s5_a_re", "s5_a_im", "s5_log_dt", "s5_b_re",
            "s5_b_im", "s5_c_re", "s5_c_im", "s5_d", "s5_glu_w", "s5_glu_b", "gdn_w_in", "gdn_conv", "gdn_a_log", "gdn_dt_bias",
            "gdn_norm_g", "gdn_w_out")


def _rows128(n):
    return -(-n // 128)


def _pack(arrs, total_rows):
    pieces = []
    for a in arrs:
        flat = a.reshape(-1)
        pieces.append(jnp.pad(flat, (0, _rows128(flat.shape[0]) * 128 - flat.shape[0])).reshape(-1, 128))
    used = sum(p.shape[0] for p in pieces)
    pieces.append(jnp.zeros((total_rows - used, 128), F32))
    return jnp.concatenate(pieces, axis=0)


def _unpack(buf, shapes):
    out, at = [], 0
    for shp in shapes:
        n = int(np.prod(shp))
        out.append(buf[at:at + _rows128(n)].reshape(-1)[:n].reshape(shp))
        at += _rows128(n)
    return out


def _full_from_halves(g):
    return g.transpose(1, 0, 2, 3).reshape(8 * g.shape[2], g.shape[3])


def _join_col_shards(name, mine, theirs, core, width):
    n, h, cols = mine.shape
    tr = _row_tile(h)
    per_half = h // tr

    def body(core_ref, a_ref, b_ref, o_ref):
        def put(src_ref):
            pad = [jnp.zeros((tr, width - n * cols), o_ref.dtype)] if width > n * cols else []
            o_ref[...] = jnp.concatenate([src_ref[s] for s in range(n)] + pad, axis=1)

        is_mine = pl.program_id(0) // per_half == core_ref[0]
        pl.when(is_mine)(functools.partial(put, a_ref))
        pl.when(jnp.logical_not(is_mine))(functools.partial(put, b_ref))

    half = pl.BlockSpec((n, tr, cols), lambda i, c: (0, i % per_half, 0))
    grid_spec = pltpu.PrefetchScalarGridSpec(num_scalar_prefetch=1, grid=(2 * per_half,), in_specs=[half, half],
                                             out_specs=pl.BlockSpec((tr, width), lambda i, c: (i, 0)))
    return pl.pallas_call(body, name=name, grid_spec=grid_spec, out_shape=_sds(2 * h, width, dtype=mine.dtype),
                          compiler_params=_cp("arbitrary"))(core, mine, theirs)


def _split_col_shards(name, g, cols, half, add=None, out_dtype=F32):
    h = g.shape[0] // 2
    tr = _row_tile(h)
    per_half = h // tr

    def body(half_ref, g_ref, *refs):
        for s in range(4):
            part = g_ref[:, s * cols:(s + 1) * cols]
            refs[-1][s] = (part if add is None else part + refs[0][s]).astype(out_dtype)

    shards = pl.BlockSpec((4, tr, cols), lambda i, c: (0, i, 0))
    grid_spec = pltpu.PrefetchScalarGridSpec(
        num_scalar_prefetch=1, grid=(per_half,),
        in_specs=[pl.BlockSpec((tr, g.shape[1]), lambda i, c: (c[0] * per_half + i, 0))] + ([] if add is None else [shards]),
        out_specs=shards)
    return pl.pallas_call(body, name=name, grid_spec=grid_spec, out_shape=_sds(4, h, cols, dtype=out_dtype),
                          compiler_params=_cp("arbitrary"))(half, g, *([] if add is None else [add]))


_LATE = ("ab_w_out", "s5_glu_w", "gdn_w_in", "gdn_w_out")


class _WeightExchanges:
    def __init__(self, shards, core, chip):
        self.core, self.chip = core, chip
        self.core_1 = jnp.reshape(core, (1,)).astype(jnp.int32)
        self.half = {}
        for name, shard in shards.items():
            h = shard.shape[0] // 2
            self.half[name] = lax.dynamic_slice_in_dim(shard.astype(BF), core * h, h, axis=0)
        self.mine, self.partial, self.received = {}, {}, {}

    def my_halves(self, names, from_chips):
        return [_own_slot(g, self.half[n], self.chip) for n, g in zip(names, from_chips)]

    def full_weights(self, names, mine, theirs):
        full = {}
        for n, a, b in zip(names, mine, theirs):
            if n in _COL_SHARDED:
                full[n] = _join_col_shards("join_" + n, a, b, self.core_1, C_IN_PAD if n == "gdn_w_in" else 4 * a.shape[2])
            else:
                full[n] = _full_from_halves(jnp.where(self.core == 0, jnp.stack([a, b], 0), jnp.stack([b, a], 0)))
        return full

    def first_weights(self):
        mine = self.my_halves(["ab_w_in"], _chip_exchange("gather_w_chips", [self.half["ab_w_in"]], False))
        return self.full_weights(["ab_w_in"], mine, _sibling_exchange("gather_w_sibling_first", mine))

    def late_exchanges(self):
        rows = self.half["gdn_w_in"].shape[0] // 2
        pieces = [self.half["gdn_w_in"][:rows], self.half["gdn_w_in"][rows:]]
        return [([self.half["ab_w_out"], self.half["s5_glu_w"]], "gather"), ([self.half["gdn_w_out"]], "gather"),
                ([pieces[0]], "gather"), ([pieces[1]], "gather")]

    def late_halves(self, got):
        return (self.my_halves(["ab_w_out", "s5_glu_w"], got[0]),
                self.my_halves(["gdn_w_in", "gdn_w_out"], [jnp.concatenate([got[2][0], got[3][0]], axis=1), got[1][0]]))

    def split_halves(self, grads):
        other = []
        for name, g in grads.items():
            if name in _COL_SHARDED:
                self.mine[name] = g
                other.append(_split_col_shards("split_other_" + name, g, self.half[name].shape[1], 1 - self.core_1))
                continue
            sm = g.reshape(4, g.shape[0] // 4, g.shape[1])
            h = sm.shape[1] // 2
            self.mine[name] = lax.dynamic_slice_in_dim(sm, self.core * h, h, axis=1)
            other.append(lax.dynamic_slice_in_dim(sm, (1 - self.core) * h, h, axis=1))
        return other

    def chip_partials(self, names, from_sibling):
        for name, b in zip(names, from_sibling):
            a = self.mine[name]
            if name in _COL_SHARDED:
                self.partial[name] = _split_col_shards("sum_sibling_" + name, a, b.shape[2], self.core_1, add=b, out_dtype=BF)
                continue
            flat = lambda t: t.reshape(-1, t.shape[-1])
            self.partial[name] = _pair_sum("sum_sibling_" + name, flat(a), flat(b), BF).reshape(a.shape)
        return [self.partial[n] for n in names]


def kernel(x, c, ada_w, ada_b, pre_g, post_g, rel_bias, ab_w_in, ab_w_out, s5_a_re, s5_a_im, s5_log_dt, s5_b_re, s5_b_im, s5_c_re, s5_c_im, s5_d, s5_glu_w, s5_glu_b, gdn_w_in, gdn_conv, gdn_a_log, gdn_dt_bias, gdn_norm_g, gdn_w_out, loss_target, m_ada_w, m_ada_b, m_pre_g, m_post_g, m_rel_bias, m_ab_w_in, m_ab_w_out, m_s5_a_re, m_s5_a_im, m_s5_log_dt, m_s5_b_re, m_s5_b_im, m_s5_c_re, m_s5_c_im, m_s5_d, m_s5_glu_w, m_s5_glu_b, m_gdn_w_in, m_gdn_conv, m_gdn_a_log, m_gdn_dt_bias, m_gdn_norm_g, m_gdn_w_out, v_ada_w, v_ada_b, v_pre_g, v_post_g, v_rel_bias, v_ab_w_in, v_ab_w_out, v_s5_a_re, v_s5_a_im, v_s5_log_dt, v_s5_b_re, v_s5_b_im, v_s5_c_re, v_s5_c_im, v_s5_d, v_s5_glu_w, v_s5_glu_b, v_gdn_w_in, v_gdn_conv, v_gdn_a_log, v_gdn_dt_bias, v_gdn_norm_g, v_gdn_w_out):
    w = dict(ada_w=ada_w, ada_b=ada_b, pre_g=pre_g, post_g=post_g, rel_bias=rel_bias, ab_w_in=ab_w_in, ab_w_out=ab_w_out,
             s5_a_re=s5_a_re, s5_a_im=s5_a_im, s5_log_dt=s5_log_dt, s5_b_re=s5_b_re, s5_b_im=s5_b_im, s5_c_re=s5_c_re, s5_c_im=s5_c_im,
             s5_d=s5_d, s5_glu_w=s5_glu_w, s5_glu_b=s5_glu_b, gdn_w_in=gdn_w_in, gdn_conv=gdn_conv, gdn_a_log=gdn_a_log,
             gdn_dt_bias=gdn_dt_bias, gdn_norm_g=gdn_norm_g, gdn_w_out=gdn_w_out)
    m = dict(ada_w=m_ada_w, ada_b=m_ada_b, pre_g=m_pre_g, post_g=m_post_g, rel_bias=m_rel_bias, ab_w_in=m_ab_w_in, ab_w_out=m_ab_w_out,
             s5_a_re=m_s5_a_re, s5_a_im=m_s5_a_im, s5_log_dt=m_s5_log_dt, s5_b_re=m_s5_b_re, s5_b_im=m_s5_b_im, s5_c_re=m_s5_c_re,
             s5_c_im=m_s5_c_im, s5_d=m_s5_d, s5_glu_w=m_s5_glu_w, s5_glu_b=m_s5_glu_b, gdn_w_in=m_gdn_w_in, gdn_conv=m_gdn_conv,
             gdn_a_log=m_gdn_a_log, gdn_dt_bias=m_gdn_dt_bias, gdn_norm_g=m_gdn_norm_g, gdn_w_out=m_gdn_w_out)
    v = dict(ada_w=v_ada_w, ada_b=v_ada_b, pre_g=v_pre_g, post_g=v_post_g, rel_bias=v_rel_bias, ab_w_in=v_ab_w_in, ab_w_out=v_ab_w_out,
             s5_a_re=v_s5_a_re, s5_a_im=v_s5_a_im, s5_log_dt=v_s5_log_dt, s5_b_re=v_s5_b_re, s5_b_im=v_s5_b_im, s5_c_re=v_s5_c_re,
             s5_c_im=v_s5_c_im, s5_d=v_s5_d, s5_glu_w=v_s5_glu_w, s5_glu_b=v_s5_glu_b, gdn_w_in=v_gdn_w_in, gdn_conv=v_gdn_conv,
             gdn_a_log=v_gdn_a_log, gdn_dt_bias=v_gdn_dt_bias, gdn_norm_g=v_gdn_norm_g, gdn_w_out=v_gdn_w_out)
    ix, iy, ic = _place()
    me = 4 * ix + 2 * iy + ic
    chip = 2 * ix + iy
    n_cols = ada_w.shape[2]

    mine_first = _pack([c, gdn_conv], 32)
    first = _own_slot(_all_gather8("gather_c_conv", mine_first), mine_first, me)
    c_all = first[:, 0:8].reshape(8, D_MODEL)
    conv_full = first[0::2, 8:32].reshape(4, C_CONV, n_cols).transpose(1, 0, 2).reshape(C_CONV, 4 * n_cols)
    mine_mod = _mod_local(c_all, ada_w)
    modl = _own_slot(_all_gather8("gather_mod", mine_mod), mine_mod, me)
    mod = lax.dynamic_index_in_dim(modl[0::2], me, axis=2, keepdims=False)
    mod = mod.transpose(1, 0, 2).reshape(2, 4 * n_cols) + ada_b

    comm = _WeightExchanges({name: w[name][0] for name in _SHARDED}, ic, chip)
    wd = {name: w[name] for name in _SMALL if name != "ada_b"}
    wd = {k: (a if k in ("pre_g", "post_g", "rel_bias") else a[0]) for k, a in wd.items()}
    wd["gdn_conv"] = conv_full
    wd.update(comm.first_weights())

    loss_local, grad_x, grads, dmod = _local_step(x[0], loss_target[0], mod, wd, comm)

    small_shapes = [w[name].shape for name in _SMALL] + [(C_CONV, 4 * n_cols)]
    small_rows = -(-sum(_rows128(int(np.prod(s))) for s in small_shapes) // 64) * 64
    per_dev, dmod_rows = small_rows // 8, _rows128(2 * 3 * D_MODEL)
    partial = _pack([dmod] + [grads[name] for name in _SMALL[1:]] + [grads["gdn_conv"]], small_rows)
    to_all = jnp.concatenate([partial[:dmod_rows], jnp.full((8, 128), loss_local, F32)], axis=0)
    outbound = jnp.concatenate([partial.reshape(8, per_dev, 128), jnp.broadcast_to(to_all[None], (8,) + to_all.shape)], axis=1)
    inbound = _own_slot(_all_to_all8("reduce_small_grads", outbound), lax.dynamic_index_in_dim(outbound, me, 0, keepdims=False), me)
    loss = functools.reduce(lambda a, b: a + b, [inbound[d, per_dev + dmod_rows, 0] for d in range(8)])
    my_rows = _slot_sum("sum_small_grads", inbound[:, :per_dev])
    g_small = _own_slot(_all_gather8("gather_small_grads", my_rows), my_rows, me).reshape(small_rows, 128)
    g_list = _unpack(g_small, small_shapes)
    out_g, out_d, out_m, out_v = {}, {}, {}, {}

    small =list(_SMALL) + ["gdn_conv"]
    small_g = [g.reshape(-1, g.shape[-1]) for g in g_list[:-1]] + [lax.dynamic_slice_in_dim(g_list[-1], chip * n_cols, n_cols, axis=1)]
    two_d = lambda a: a.reshape(-1, a.shape[-1])
    results = _adamw_many("adamw_small", [two_d(w[n]) for n in small], small_g, [two_d(m[n]) for n in small], [two_d(v[n]) for n in small])
    for name, g2d, (d_, m_, v_) in zip(small, small_g, results):
        out_g[name], out_d[name], out_m[name], out_v[name] = (a.reshape(w[name].shape) for a in (g2d, d_, m_, v_))

    from_sibling = _sibling_exchange("reduce_sibling", comm.split_halves({"ab_w_in": grads["ab_w_in"]}))
    rider = (comm.chip_partials(["ab_w_in"], from_sibling), "scatter")
    dmod_all = inbound[:, per_dev:per_dev + dmod_rows].reshape(8, 2, 4, n_cols)
    dmod_cols = lax.dynamic_index_in_dim(dmod_all, chip, axis=2, keepdims=False).transpose(1, 0, 2)
    g_ada = _ada_w_grad(c_all, dmod_cols).reshape(-1, n_cols)
    (d_, m_, v_), got = _adamw("adamw_ada_w", two_d(w["ada_w"]), g_ada, two_d(m["ada_w"]), two_d(v["ada_w"]), exchange=rider)
    out_g["ada_w"], out_d["ada_w"], out_m["ada_w"], out_v["ada_w"] = (a.reshape(w["ada_w"].shape) for a in (g_ada, d_, m_, v_))
    comm.received["ab_w_in"] = got[0]
    chip_1 = jnp.reshape(chip, (1,)).astype(jnp.int32)
    core_1 = jnp.reshape(ic, (1,)).astype(jnp.int32)
    reduced = [_chip_sum("sum_chips_" + name, comm.received[name], comm.partial[name], chip_1) for name in _SHARDED]
    for name, g_mine, g_sib in zip(_SHARDED, reduced, _sibling_exchange("reduce_share", reduced)):
        out_g[name], out_d[name], out_m[name], out_v[name] = _adamw_halves(
            "adamw_" + name, w[name], g_mine, g_sib, m[name], v[name], core_1)

    return (loss, grad_x[None], *[out_g[n] for n in _WEIGHTS], *[out_d[n] for n in _WEIGHTS],
            *[out_m[n] for n in _WEIGHTS], *[out_v[n] for n in _WEIGHTS])
```

```python
import functools
import math

import numpy as np
import jax
import jax.numpy as jnp
from jax import lax
from jax.experimental import pallas as pl
from jax.experimental.pallas import tpu as pltpu

F32 = jnp.float32
BF = jnp.bfloat16
HI = lax.Precision.HIGHEST
MESH = pl.DeviceIdType.MESH

D_MODEL = 1024
EPS = 1e-6
A_HEADS, A_HD, A_WIDTH, A_BLOCK = 8, 64, 512, 128
DILATIONS = (1, 4, 16)
N_KEYS = 128
REL_BUCKETS, REL_MAX_DIST = 32, 2048
B_WIDTH, B_GROUP, B_GROUPS, B_STATE = 512, 16, 32, 64
S5_LANES = 512
S5_TILES = 4
C_HEADS, C_DK, C_CHUNK, C_CONV = 8, 128, 64, 4
QKV = 3072
C_IN = QKV + 1024 + 2 * C_HEADS
C_IN_PAD = 4224
TM = 256
VMEM_LIMIT_BYTES = 56 * 1024 * 1024
ADAM_LR, ADAM_B1, ADAM_B2, ADAM_EPS, ADAM_WD, ADAM_STEP = 0.001, 0.9, 0.999, 1e-08, 0.01, 10
NEG = float(np.finfo(np.float32).min)


def _cp(*sem):
    return pltpu.CompilerParams(dimension_semantics=sem, vmem_limit_bytes=VMEM_LIMIT_BYTES)


def _bdot(a, b):
    return jnp.dot(a.astype(BF), b.astype(BF), preferred_element_type=F32)


def _bdot_nt(a, b):
    return lax.dot_general(a.astype(BF), b.astype(BF), (((1,), (1,)), ((), ())), preferred_element_type=F32)


def _bdot_tn(a, b):
    return lax.dot_general(a.astype(BF), b.astype(BF), (((0,), (0,)), ((), ())), preferred_element_type=F32)


def _hdot(a, b):
    return jnp.dot(a, b, precision=HI, preferred_element_type=F32)


def _bein(eq, a, b):
    return jnp.einsum(eq, a.astype(BF), b.astype(BF), preferred_element_type=F32)


def _row(tm, n):
    return pl.BlockSpec((tm, n), lambda i: (i, 0))


def _fix(shape):
    return pl.BlockSpec(shape, lambda i: (0,) * len(shape))


def _sds(*shape, dtype=F32):
    return jax.ShapeDtypeStruct(shape, dtype)


def _acc(ref, val):
    ref[...] += val


def _zero_at_first(refs, axis=0):
    @pl.when(pl.program_id(axis) == 0)
    def _():
        for r in refs:
            r[...] = jnp.zeros_like(r)


def _rms(x):
    return x * lax.rsqrt(jnp.mean(x * x, axis=-1, keepdims=True) + EPS)


def _pre_mod(x, g, scale, shift):
    return (_rms(x) * g) * (1.0 + scale) + shift


def _post_res(y, x, post_g, gate):
    return x + gate * (_rms(y) * post_g)


def _merge_gate(o1, o2, o3, l1, l2, l3, ga):
    m = jnp.maximum(jnp.maximum(l1, l2), l3)
    e1, e2, e3 = jnp.exp(l1 - m), jnp.exp(l2 - m), jnp.exp(l3 - m)
    inv = 1.0 / (e1 + e2 + e3)
    return ((e1 * inv) * o1 + (e2 * inv) * o2 + (e3 * inv) * o3) * jax.nn.silu(ga)


def _s5_gelu(ypre, u, d_skip):
    return jax.nn.gelu(ypre + d_skip * u)


def _s5_glu(yb, gl, gb):
    return yb * jax.nn.sigmoid(gl) * jax.nn.silu(gb)


def _l0_front(x, pre_g, scale, shift, w_in):
    s_len = x.shape[0]

    def body(x_ref, g_ref, sc_ref, sh_ref, w_ref, *out_refs):
        qkv_refs, (u_ref, ga_ref, gb_ref, h_ref) = out_refs[:9], out_refs[9:]
        hb = _pre_mod(x_ref[...], g_ref[...], sc_ref[...], sh_ref[...]).astype(BF)
        h_ref[...] = hb
        z = jnp.dot(hb, w_ref[...], preferred_element_type=F32)
        for a in range(3):
            piece = z[:, a * 512:(a + 1) * 512]
            for i, d in enumerate(DILATIONS):
                qkv_refs[3 * a + i][...] = _to_res(piece, d).astype(BF)
        u_ref[...] = z[:, 1536:2048]
        ga_ref[...] = z[:, 2048:2560]
        gb_ref[...] = z[:, 2560:3072]

    vec = _fix((1, D_MODEL))
    return pl.pallas_call(
        body, name="l0_front", grid=(s_len // TM,),
        in_specs=[_row(TM, D_MODEL), vec, vec, vec, _fix((D_MODEL, 3072))],
        out_specs=[_res_spec(d) for d in DILATIONS] * 3 + [_row(TM, 512)] * 3 + [_row(TM, D_MODEL)],
        out_shape=[_sds(*_res_shape(s_len, d), dtype=BF) for d in DILATIONS] * 3 + [_sds(s_len, 512)] * 3 + [_sds(s_len, D_MODEL, dtype=BF)],
        compiler_params=_cp("arbitrary"),
    )(x, pre_g, scale, shift, w_in)


def _front_bwd(name, x, pre_g, scale, shift, w_in, dres, parts, widths):
    s_len = x.shape[0]
    n_in = sum(len(p) for p in parts)
    n_cols = sum(widths)

    def body(*refs):
        x_ref, g_ref, sc_ref, sh_ref, w_ref, dres_ref = refs[:6]
        part_refs = refs[6:6 + n_in]
        dz_ref, dx_ref, dg_ref, dsc_ref, dsh_ref = refs[6 + n_in:]
        _zero_at_first([dg_ref, dsc_ref, dsh_ref])
        _, vjp = jax.vjp(_pre_mod, x_ref[...], g_ref[...], sc_ref[...], sh_ref[...])
        dh = jnp.zeros((TM, D_MODEL), F32)
        col, at = 0, 0
        for grp, width in zip(parts, widths):
            tile = lambda r: _from_res(r[...]) if len(r.shape) == 3 else r[...]
            dz = tile(part_refs[at])
            for r in part_refs[at + 1:at + len(grp)]:
                dz = dz + tile(r)
            at += len(grp)
            dzb = dz.astype(BF)
            dz_ref[:, col:col + width] = dzb
            dh = dh + lax.dot_general(dzb, w_ref[:, col:col + width], (((1,), (1,)), ((), ())), preferred_element_type=F32)
            col += width
        dx, dg, dsc, dsh = vjp(dh)
        dx_ref[...] = dx + dres_ref[...]
        _acc(dg_ref, dg)
        _acc(dsc_ref, dsc)
        _acc(dsh_ref, dsh)

    vec = _fix((1, D_MODEL))
    flat = [a for p in parts for a in p]
    return pl.pallas_call(
        body, name=name, grid=(s_len // TM,),
        in_specs=[_row(TM, D_MODEL), vec, vec, vec, _fix((D_MODEL, n_cols)), _row(TM, D_MODEL)]
        + [_res_spec(a.shape[0], a.shape[2]) if a.ndim == 3 else _row(TM, a.shape[1]) for a in flat],
        out_specs=[_row(TM, n_cols), _row(TM, D_MODEL), vec, vec, vec],
        out_shape=[_sds(s_len, n_cols, dtype=BF), _sds(s_len, D_MODEL), _sds(1, D_MODEL), _sds(1, D_MODEL), _sds(1, D_MODEL)],
        compiler_params=_cp("arbitrary"),
    )(x, pre_g, scale, shift, w_in, dres, *flat)


def _matmul_tn(name, a, b, tn):
    s_len, k_dim = a.shape
    n_dim = b.shape[1]
    ts = 2048

    def body(a_ref, b_ref, o_ref):
        _zero_at_first([o_ref], axis=1)
        o_ref[...] += lax.dot_general(a_ref[...], b_ref[...], (((0,), (0,)), ((), ())), preferred_element_type=F32)

    return pl.pallas_call(
        body, name=name, grid=(n_dim // tn, s_len // ts),
        in_specs=[pl.BlockSpec((ts, k_dim), lambda j, i: (i, 0)), pl.BlockSpec((ts, tn), lambda j, i: (i, j))],
        out_specs=pl.BlockSpec((k_dim, tn), lambda j, i: (0, j)),
        out_shape=_sds(k_dim, n_dim),
        compiler_params=_cp("arbitrary", "arbitrary"),
    )(a, b)


def _t5_bucket_np(dist):
    dist = np.maximum(dist, 0)
    max_exact = REL_BUCKETS // 2
    large = max_exact + (np.log(np.maximum(dist, 1) / max_exact)
                         / math.log(REL_MAX_DIST / max_exact) * (REL_BUCKETS - max_exact)).astype(np.int32)
    large = np.minimum(large, REL_BUCKETS - 1)
    return np.where(dist < max_exact, dist, large).astype(np.int32)


def _to_res(z, dil):
    if dil == 1:
        return z[None]
    return jnp.swapaxes(z.reshape(z.shape[0] // dil, dil, z.shape[1]), 0, 1)


def _from_res(z):
    if z.shape[0] == 1:
        return z[0]
    return jnp.swapaxes(z, 0, 1).reshape(z.shape[0] * z.shape[1], z.shape[2])


def _res_shape(s_len, dil, width=A_WIDTH):
    return (dil, s_len // dil, width)


def _res_spec(dil, width=A_WIDTH):
    return pl.BlockSpec((dil, TM // dil, width), lambda i: (0, i, 0))


def _bucket_table():
    qi = np.arange(A_BLOCK)[:, None]
    kj = np.arange(2 * A_BLOCK)[None, :]
    return np.stack([_t5_bucket_np((qi + A_BLOCK - kj) * d) for d in DILATIONS], 0)


def _attn_mask(first):
    qi = lax.broadcasted_iota(jnp.int32, (A_BLOCK, 2 * A_BLOCK), 0)
    kj = lax.broadcasted_iota(jnp.int32, (A_BLOCK, 2 * A_BLOCK), 1)
    rel = qi + A_BLOCK - kj
    return (rel >= 0) & (rel <= N_KEYS) & (jnp.logical_not(first) | (kj >= A_BLOCK))


def _attn_specs(nb, rev):
    per = 2 if nb % 2 == 0 else 1
    steps = nb // per
    n_of = (lambda i: steps - 1 - i) if rev else (lambda i: i)
    cur = pl.BlockSpec((None, per * A_BLOCK, A_WIDTH), lambda r, i: (r, n_of(i), 0))
    prev = pl.BlockSpec((None, A_BLOCK, A_WIDTH), lambda r, i: (r, jnp.maximum(per * n_of(i) - 1, 0), 0))
    bias = pl.BlockSpec((A_HEADS, A_BLOCK, 2 * A_BLOCK), lambda r, i: (0, 0, 0))
    return per, steps, cur, prev, bias


def _attn_fwd(q, k, v, bias, exchange=None):
    dil, t_len, _ = q.shape
    per, steps, cur, prev, bias_spec = _attn_specs(t_len // A_BLOCK, False)
    scale = A_HD ** -0.5

    def body(q_ref, kp_ref, kc_ref, vp_ref, vc_ref, b_ref, o_ref, l_ref):
        lane = lax.broadcasted_iota(jnp.int32, (1, 128), 1)
        for sub in range(per):
            rows = slice(sub * A_BLOCK, (sub + 1) * A_BLOCK)
            before = slice((sub - 1) * A_BLOCK, sub * A_BLOCK)
            mask = _attn_mask((pl.program_id(1) == 0) if sub == 0 else False)
            for hp in range(A_HEADS // 2):
                sl = slice(hp * 128, (hp + 1) * 128)
                qp = q_ref[rows, sl]
                kw = jnp.concatenate([kp_ref[:, sl] if sub == 0 else kc_ref[before, sl], kc_ref[rows, sl]], axis=0).astype(BF)
                vw = jnp.concatenate([vp_ref[:, sl] if sub == 0 else vc_ref[before, sl], vc_ref[rows, sl]], axis=0).astype(BF)
                outs, lses = [], []
                for j in range(2):
                    hm = (lane < 64) if j == 0 else (lane >= 64)
                    s = _bdot_nt(jnp.where(hm, qp, 0.0), kw) * scale
                    s = jnp.where(mask, s + b_ref[2 * hp + j], NEG)
                    m = jnp.max(s, axis=-1, keepdims=True)
                    p = jnp.exp(s - m)
                    den = jnp.sum(p, axis=-1, keepdims=True)
                    outs.append(_bdot(p, vw) / den)
                    lses.append(m + jnp.log(den))
                hm0 = lane < 64
                o_ref[rows, sl] = jnp.where(hm0, outs[0], outs[1])
                l_ref[rows, sl] = jnp.where(hm0, lses[0], lses[1])

    return _call_with_exchange(body, f"attn_fwd_d{dil}", (dil, steps), [cur, prev, cur, prev, cur, bias_spec], [cur, cur],
                               [_sds(dil, t_len, A_WIDTH)] * 2, [], (q, k, k, v, v, bias), exchange)


def _attn_bwd(q, k, v, bias, o, l, do, dl, exchange=None):
    dil, t_len, _ = q.shape
    per, steps, cur, prev, bias_spec = _attn_specs(t_len // A_BLOCK, True)
    scale = A_HD ** -0.5

    def body(q_ref, kp_ref, kc_ref, vp_ref, vc_ref, b_ref, o_ref, l_ref, do_ref, dl_ref,
             dq_ref, dk_ref, dv_ref, db_ref, ck_ref, cv_ref):
        _zero_at_first([ck_ref, cv_ref], axis=1)

        @pl.when((pl.program_id(0) == 0) & (pl.program_id(1) == 0))
        def _():
            db_ref[...] = jnp.zeros_like(db_ref)

        lane = lax.broadcasted_iota(jnp.int32, (1, 128), 1)
        for hp in range(A_HEADS // 2):
            sl = slice(hp * 128, (hp + 1) * 128)
            to_prev_k, to_prev_v = ck_ref[:, sl], cv_ref[:, sl]
            for sub in range(per - 1, -1, -1):
                rows = slice(sub * A_BLOCK, (sub + 1) * A_BLOCK)
                before = slice((sub - 1) * A_BLOCK, sub * A_BLOCK)
                mask = _attn_mask((pl.program_id(1) == steps - 1) if sub == 0 else False)
                qp = q_ref[rows, sl]
                kw = jnp.concatenate([kp_ref[:, sl] if sub == 0 else kc_ref[before, sl], kc_ref[rows, sl]], axis=0).astype(BF)
                vw = jnp.concatenate([vp_ref[:, sl] if sub == 0 else vc_ref[before, sl], vc_ref[rows, sl]], axis=0).astype(BF)
                op, lp, dop, dlp = o_ref[rows, sl], l_ref[rows, sl], do_ref[rows, sl], dl_ref[rows, sl]
                dq_acc = jnp.zeros((A_BLOCK, 128), F32)
                dk_acc = jnp.zeros((2 * A_BLOCK, 128), F32)
                dv_acc = jnp.zeros((2 * A_BLOCK, 128), F32)
                for j in range(2):
                    hm = (lane < 64) if j == 0 else (lane >= 64)
                    qm = jnp.where(hm, qp, 0.0)
                    s = _bdot_nt(qm, kw) * scale
                    s = jnp.where(mask, s + b_ref[2 * hp + j], NEG)
                    lse = jnp.max(jnp.where(hm, lp, NEG), axis=-1, keepdims=True)
                    p = jnp.exp(s - lse)
                    do_h = jnp.where(hm, dop, 0.0)
                    dd = jnp.sum(do_h * op, axis=-1, keepdims=True)
                    dlse = jnp.sum(jnp.where(hm, dlp, 0.0), axis=-1, keepdims=True)
                    ds = p * (_bdot_nt(do_h, vw) - dd + dlse)
                    dv_acc = dv_acc + _bdot_tn(p, do_h)
                    dq_acc = dq_acc + jnp.where(hm, _bdot(ds, kw), 0.0) * scale
                    dk_acc = dk_acc + _bdot_tn(ds, qm) * scale
                    db_ref[2 * hp + j] += ds
                dq_ref[rows, sl] = dq_acc
                dk_ref[rows, sl] = dk_acc[A_BLOCK:] + to_prev_k
                dv_ref[rows, sl] = dv_acc[A_BLOCK:] + to_prev_v
                to_prev_k, to_prev_v = dk_acc[:A_BLOCK], dv_acc[:A_BLOCK]
            ck_ref[:, sl] = to_prev_k
            cv_ref[:, sl] = to_prev_v

    return _call_with_exchange(
        body, f"attn_bwd_d{dil}", (dil, steps), [cur, prev, cur, prev, cur, bias_spec, cur, cur, cur, cur],
        [cur, cur, cur, bias_spec], [_sds(dil, t_len, A_WIDTH)] * 3 + [_sds(A_HEADS, A_BLOCK, 2 * A_BLOCK)],
        [pltpu.VMEM((A_BLOCK, A_WIDTH), F32)] * 2, (q, k, k, v, v, bias, o, l, do, dl), exchange)


def _attn_bias(rel_bias, table):
    def body(rb_ref, t_ref, *o_refs):
        for c in range(3):
            t = t_ref[c]
            acc = [jnp.zeros((A_BLOCK, 2 * A_BLOCK), F32) for _ in range(A_HEADS)]
            for b in range(REL_BUCKETS):
                hit = t == b
                acc = [jnp.where(hit, rb_ref[b, h], acc[h]) for h in range(A_HEADS)]
            for h in range(A_HEADS):
                o_refs[c][h] = acc[h]

    return pl.pallas_call(body, name="attn_bias", out_shape=[_sds(A_HEADS, A_BLOCK, 2 * A_BLOCK)] * 3,
                          in_specs=[pl.BlockSpec(memory_space=pltpu.SMEM), pl.BlockSpec(memory_space=pltpu.VMEM)],
                          compiler_params=pltpu.CompilerParams(vmem_limit_bytes=VMEM_LIMIT_BYTES))(rel_bias, table)


def _rel_bias_grad(dbs, idx_rows):
    n = A_BLOCK * 2 * A_BLOCK

    def body(d0_ref, d1_ref, d2_ref, idx_ref, o_ref):
        bucket = lax.broadcasted_iota(jnp.int32, (REL_BUCKETS, n), 0).astype(F32)
        acc = jnp.zeros((A_HEADS, REL_BUCKETS), F32)
        for c, db_ref in enumerate((d0_ref, d1_ref, d2_ref)):
            onehot = (idx_ref[c:c + 1, :] == bucket).astype(F32)
            acc = acc + lax.dot_general(db_ref[...], onehot, (((1,), (1,)), ((), ())), precision=HI, preferred_element_type=F32)
        o_ref[...] = acc

    return pl.pallas_call(body, name="rel_bias_grad", out_shape=_sds(A_HEADS, REL_BUCKETS),
                          compiler_params=pltpu.CompilerParams(vmem_limit_bytes=VMEM_LIMIT_BYTES))(
                              *[d.reshape(A_HEADS, n) for d in dbs], idx_rows)


def _s5_param_fn(a_re, a_im, log_dt, bt_re, bt_im):
    dt = jnp.exp(log_dt)
    mag = jnp.exp(dt * a_re)
    abar_r, abar_i = mag * jnp.cos(dt * a_im), mag * jnp.sin(dt * a_im)
    den = a_re * a_re + a_im * a_im
    fr = ((abar_r - 1.0) * a_re + abar_i * a_im) / den
    fi = (abar_i * a_re - (abar_r - 1.0) * a_im) / den
    row = lax.broadcasted_iota(jnp.int32, (B_WIDTH, B_GROUPS), 0)
    grp = lax.broadcasted_iota(jnp.int32, (B_WIDTH, B_GROUPS), 1)
    expand = ((row // B_GROUP) == grp).astype(F32)
    fr_e, fi_e = _hdot(expand, fr), _hdot(expand, fi)
    return abar_r, abar_i, fr_e * bt_re - fi_e * bt_im, fr_e * bt_im + fi_e * bt_re


def _s5_params(a_re, a_im, log_dt, bt_re, bt_im):
    def body(ar, ai, ld, br, bi, o1, o2, o3, o4):
        o1[...], o2[...], o3[...], o4[...] = _s5_param_fn(ar[...], ai[...], ld[...], br[...], bi[...])

    return pl.pallas_call(body, name="s5_params",
                          out_shape=[_sds(B_GROUPS, B_STATE)] * 2 + [_sds(B_WIDTH, B_STATE)] * 2)(a_re, a_im, log_dt, bt_re, bt_im)


def _s5_params_bwd(a_re, a_im, log_dt, bt_re, bt_im, d1, d2, d3, d4):
    def body(ar, ai, ld, br, bi, c1, c2, c3, c4, o1, o2, o3, o4, o5):
        _, vjp = jax.vjp(_s5_param_fn, ar[...], ai[...], ld[...], br[...], bi[...])
        o1[...], o2[...], o3[...], o4[...], o5[...] = vjp((c1[...], c2[...], c3[...], c4[...]))

    return pl.pallas_call(body, name="s5_params_bwd",
                          out_shape=[_sds(B_GROUPS, B_STATE)] * 2 + [_sds(B_GROUPS, 1)] + [_sds(B_WIDTH, B_STATE)] * 2,
                          )(a_re, a_im, log_dt, bt_re, bt_im, d1, d2, d3, d4)


def _pick_row(x, r):
    rows = lax.broadcasted_iota(jnp.int32, x.shape, 0)
    return jnp.sum(jnp.where(rows == r, x, 0.0), axis=0, keepdims=True)


S5_SEG = 8
S5_STEPS = 32
S5_WIDTH = S5_TILES * S5_LANES


def _seg_rows(block):
    return jnp.swapaxes(block, 0, 1).reshape(block.shape[1] * S5_SEG, block.shape[2])


def _seg_block(rows):
    return jnp.swapaxes(rows.reshape(rows.shape[0] // S5_SEG, S5_SEG, rows.shape[1]), 0, 1)


def _seq_specs(n_i, rev):
    at = (lambda i: n_i - 1 - i) if rev else (lambda i: i)
    seg = pl.BlockSpec((S5_SEG, S5_STEPS, B_WIDTH), lambda i: (0, at(i), 0))
    x_spec = pl.BlockSpec((S5_SEG * S5_STEPS, S5_WIDTH), lambda i: (at(i), 0))
    return seg, x_spec, _fix((S5_TILES, 128, S5_LANES)), _fix((S5_TILES, S5_LANES, 128)), _fix((1, S5_WIDTH)), _fix((S5_SEG, S5_WIDTH))


def _tile_dots(dot, lhs, w_ref, lhs_width):
    return jnp.concatenate([dot(lhs[:, t * lhs_width:(t + 1) * lhs_width], w_ref[t]) for t in range(S5_TILES)], axis=1)


def _s5_entries(name, end_r, end_i, abr, abi, steps, reverse):
    def body(er_ref, ei_ref, ar_ref, ai_ref, or_ref, oi_ref):
        pr, pi_ = ar_ref[...], ai_ref[...]
        for _ in range(int(math.log2(steps))):
            pr, pi_ = pr * pr - pi_ * pi_, 2.0 * pr * pi_
        er, ei = er_ref[...], ei_ref[...]
        rows = lax.broadcasted_iota(jnp.int32, er.shape, 0)
        cr, ci = jnp.zeros_like(pr), jnp.zeros_like(pr)
        out_r, out_i = jnp.zeros_like(er), jnp.zeros_like(er)
        for g in (range(S5_SEG - 2, -1, -1) if reverse else range(1, S5_SEG)):
            src = g + 1 if reverse else g - 1
            cr, ci = _pick_row(er, src) + pr * cr - pi_ * ci, _pick_row(ei, src) + pr * ci + pi_ * cr
            out_r, out_i = jnp.where(rows == g, cr, out_r), jnp.where(rows == g, ci, out_i)
        or_ref[...] = out_r
        oi_ref[...] = out_i

    return pl.pallas_call(body, name=name, out_shape=[_sds(*end_r.shape)] * 2)(end_r, end_i, abr, abi)


def _s5_seq_fwd(u, btr, bti, ctr, cti, abr, abi, entry, store, exchange=None):
    s_len = u.shape[0]
    seg_len = s_len // S5_SEG
    n_i = seg_len // S5_STEPS
    rows = S5_SEG * S5_STEPS

    def body(u_ref, btr_ref, bti_ref, ctr_ref, cti_ref, ar_ref, ai_ref, er_ref, ei_ref, *rest):
        if store:
            xr_ref, xi_ref, y_ref, endr_ref, endi_ref, sr_ref, si_ref = rest
        else:
            endr_ref, endi_ref, sr_ref, si_ref = rest
        i = pl.program_id(0)

        @pl.when(i == 0)
        def _():
            sr_ref[...] = er_ref[...]
            si_ref[...] = ei_ref[...]

        ar = jnp.broadcast_to(ar_ref[...], (S5_SEG, S5_WIDTH))
        ai = jnp.broadcast_to(ai_ref[...], (S5_SEG, S5_WIDTH))
        ub = _seg_rows(u_ref[...])
        br, bi = _tile_dots(_bdot, ub, btr_ref, 128), _tile_dots(_bdot, ub, bti_ref, 128)
        sr, si = sr_ref[...], si_ref[...]
        for s in range(S5_STEPS):
            at = slice(S5_SEG * s, S5_SEG * (s + 1))
            sr, si = ar * sr - ai * si + br[at], ar * si + ai * sr + bi[at]
            if store:
                xr_ref[at, :] = sr
                xi_ref[at, :] = si
        sr_ref[...] = sr
        si_ref[...] = si
        if store:
            y_ref[...] = _seg_block(_tile_dots(_bdot, xr_ref[...], ctr_ref, S5_LANES) - _tile_dots(_bdot, xi_ref[...], cti_ref, S5_LANES))

        @pl.when(i == n_i - 1)
        def _():
            endr_ref[...] = sr
            endi_ref[...] = si

    seg, x_spec, b_spec, c_spec, a_spec, e_spec = _seq_specs(n_i, False)
    ends = [_sds(S5_SEG, S5_WIDTH)] * 2
    full = [_sds(s_len, S5_WIDTH)] * 2 + [_sds(S5_SEG, seg_len, B_WIDTH)] if store else []
    return _call_with_exchange(
        body, "s5_scan_fwd" if store else "s5_ends_fwd", (n_i,),
        [seg, b_spec, b_spec, c_spec, c_spec, a_spec, a_spec, e_spec, e_spec],
        ([x_spec, x_spec, seg] if store else []) + [e_spec, e_spec], full + ends,
        [pltpu.VMEM((S5_SEG, S5_WIDTH), F32)] * 2,
        (u.reshape(S5_SEG, seg_len, B_WIDTH), btr, bti, ctr, cti, abr, abi, *entry), exchange)


def _s5_seq_bwd(dy, xr, xi, u, btr, bti, ctr, cti, abr, abi, g_entry, x_entry, full, exchange=None):
    s_len = dy.shape[0]
    seg_len = s_len // S5_SEG
    n_i = seg_len // S5_STEPS
    rows = S5_SEG * S5_STEPS

    def body(*refs):
        if full:
            (dy_ref, btr_ref, bti_ref, ctr_ref, cti_ref, ar_ref, ai_ref, ger_ref, gei_ref,
             xr_ref, xi_ref, xrp_ref, xip_ref, xer_ref, xei_ref, u_ref,
             du_ref, dbtr_ref, dbti_ref, dctr_ref, dcti_ref, dar_ref, dai_ref, str_ref, sti_ref,
             sr_ref, si_ref, gr_s, gi_s) = refs
        else:
            (dy_ref, btr_ref, bti_ref, ctr_ref, cti_ref, ar_ref, ai_ref, ger_ref, gei_ref, str_ref, sti_ref, sr_ref, si_ref) = refs
        i = pl.program_id(0)

        @pl.when(i == 0)
        def _():
            sr_ref[...] = ger_ref[...]
            si_ref[...] = gei_ref[...]
            if full:
                for r in (dbtr_ref, dbti_ref, dctr_ref, dcti_ref, dar_ref, dai_ref):
                    r[...] = jnp.zeros_like(r)

        ar = jnp.broadcast_to(ar_ref[...], (S5_SEG, S5_WIDTH))
        ai = -jnp.broadcast_to(ai_ref[...], (S5_SEG, S5_WIDTH))
        dyb = _seg_rows(dy_ref[...])
        gr, gi = _tile_dots(_bdot_nt, dyb, ctr_ref, 128), -_tile_dots(_bdot_nt, dyb, cti_ref, 128)
        sr, si = sr_ref[...], si_ref[...]
        for s in range(S5_STEPS - 1, -1, -1):
            at = slice(S5_SEG * s, S5_SEG * (s + 1))
            sr, si = ar * sr - ai * si + gr[at], ar * si + ai * sr + gi[at]
            if full:
                gr_s[at, :] = sr
                gi_s[at, :] = si
        sr_ref[...] = sr
        si_ref[...] = si

        @pl.when(i == n_i - 1)
        def _():
            str_ref[...] = sr
            sti_ref[...] = si

        if full:
            g_r, g_i = gr_s[...], gi_s[...]
            du_ref[...] = _seg_block(_tile_dots(_bdot_nt, g_r, btr_ref, S5_LANES) + _tile_dots(_bdot_nt, g_i, bti_ref, S5_LANES))
            ub = _seg_rows(u_ref[...])
            xr_b, xi_b = xr_ref[...], xi_ref[...]
            for t in range(S5_TILES):
                lanes, cols = slice(t * S5_LANES, (t + 1) * S5_LANES), slice(t * 128, (t + 1) * 128)
                dbtr_ref[t] += _bdot_tn(ub[:, cols], g_r[:, lanes])
                dbti_ref[t] += _bdot_tn(ub[:, cols], g_i[:, lanes])
                dctr_ref[t] += _bdot_tn(xr_b[:, lanes], dyb[:, cols])
                dcti_ref[t] -= _bdot_tn(xi_b[:, lanes], dyb[:, cols])
            first = i == n_i - 1
            xpr = jnp.concatenate([jnp.where(first, xer_ref[...], xrp_ref[...]), xr_b[:rows - S5_SEG]], axis=0)
            xpi = jnp.concatenate([jnp.where(first, xei_ref[...], xip_ref[...]), xi_b[:rows - S5_SEG]], axis=0)
            dar_ref[...] += jnp.sum(g_r * xpr + g_i * xpi, axis=0, keepdims=True)
            dai_ref[...] += jnp.sum(g_i * xpr - g_r * xpi, axis=0, keepdims=True)

    seg, x_spec, b_spec, c_spec, a_spec, e_spec = _seq_specs(n_i, True)
    halo = pl.BlockSpec((S5_SEG, S5_WIDTH), lambda i: (jnp.maximum((n_i - 1 - i) * S5_STEPS - 1, 0), 0))
    starts = [_sds(S5_SEG, S5_WIDTH)] * 2
    in_specs = [seg, b_spec, b_spec, c_spec, c_spec, a_spec, a_spec, e_spec, e_spec]
    args = [dy.reshape(S5_SEG, seg_len, B_WIDTH), btr, bti, ctr, cti, abr, abi, *g_entry]
    state = [pltpu.VMEM((S5_SEG, S5_WIDTH), F32)] * 2
    if not full:
        return _call_with_exchange(body, "s5_starts_bwd", (n_i,), in_specs, [e_spec, e_spec], starts, state, args, None)
    return _call_with_exchange(
        body, "s5_scan_bwd", (n_i,),
        in_specs + [x_spec, x_spec, halo, halo, e_spec, e_spec, seg],
        [seg, b_spec, b_spec, c_spec, c_spec, a_spec, a_spec, e_spec, e_spec],
        [_sds(S5_SEG, seg_len, B_WIDTH)] + [_sds(S5_TILES, 128, S5_LANES)] * 2 + [_sds(S5_TILES, S5_LANES, 128)] * 2
        + [_sds(1, S5_WIDTH)] * 2 + starts,
        state + [pltpu.VMEM((rows, S5_WIDTH), F32)] * 2,
        args + [xr, xi, xr, xi, *x_entry, u.reshape(S5_SEG, seg_len, B_WIDTH)], exchange)


def _blockdiag_b(bbar_t):
    blocks = bbar_t.reshape(S5_TILES, 8, B_GROUP, B_STATE)
    return jnp.einsum('jgmp,gh->jgmhp', blocks, jnp.eye(8, dtype=F32)).reshape(S5_TILES, 128, S5_LANES)


def _blockdiag_b_t(d):
    return jnp.einsum('jgmgp->jgmp', d.reshape(S5_TILES, 8, B_GROUP, 8, B_STATE)).reshape(B_WIDTH, B_STATE)


def _blockdiag_c(c):
    blocks = c.reshape(S5_TILES, 8, B_GROUP, B_STATE)
    return jnp.einsum('jgmp,gh->jhpgm', blocks, jnp.eye(8, dtype=F32)).reshape(S5_TILES, S5_LANES, 128)


def _blockdiag_c_t(d):
    return jnp.einsum('jgpgm->jgmp', d.reshape(S5_TILES, 8, B_STATE, 8, B_GROUP)).reshape(B_GROUPS, B_GROUP, B_STATE)


def _l0_out(os, ls, ga, gb, ypre, u, x, d_skip, glu_w, glu_b, w_out, post_g, gate, exchange=None):
    s_len = x.shape[0]

    def body(o0, o1, o2, l0, l1, l2, ga_ref, gb_ref, yp_ref, u_ref, x_ref, d_ref, gw_ref, gbias_ref, w_ref, pg_ref, gt_ref, x1_ref, y_ref):
        oa = _merge_gate(*[_from_res(r[...]) for r in (o0, o1, o2, l0, l1, l2)], ga_ref[...])
        yb = _s5_gelu(yp_ref[...], u_ref[...], d_ref[...])
        ob = _s5_glu(yb, _bdot(yb, gw_ref[...]) + gbias_ref[...], gb_ref[...])
        y = _bdot(oa, w_ref[0:512, :]) + _bdot(ob, w_ref[512:1024, :])
        y_ref[...] = y
        x1_ref[...] = _post_res(y, x_ref[...], pg_ref[...], gt_ref[...])

    vec, half = _fix((1, D_MODEL)), _fix((1, 512))
    return _call_with_exchange(
        body, "l0_out", (s_len // TM,),
        [_res_spec(d) for d in DILATIONS] * 2 + [_row(TM, 512)] * 4
        + [_row(TM, D_MODEL), half, _fix((512, 512)), half, _fix((D_MODEL, D_MODEL)), vec, vec],
        [_row(TM, D_MODEL)] * 2, [_sds(s_len, D_MODEL)] * 2, [],
        (*os, *ls, ga, gb, ypre, u, x, d_skip, glu_w, glu_b, w_out, post_g, gate), exchange)


def _l0_out_bwd(os, ls, ga, gb, ypre, u, x, y, d_skip, glu_w, glu_b, w_out, post_g, gate, dx1, exchange=None):
    s_len = x.shape[0]

    def body(o0, o1, o2, l0, l1, l2, ga_ref, gb_ref, yp_ref, u_ref, x_ref, y_ref, d_ref, gw_ref, gbias_ref, w_ref, pg_ref, gt_ref, dx1_ref,
             do0, do1, do2, dl0, dl1, dl2, dga_ref, dgb_ref, dyp_ref, du_ref, dd_ref, dgw_ref, dgbias_ref, dw_ref, dpg_ref, dgt_ref):
        _zero_at_first([dd_ref, dgw_ref, dgbias_ref, dw_ref, dpg_ref, dgt_ref])
        _, vjp2 = jax.vjp(_post_res, y_ref[...], x_ref[...], pg_ref[...], gt_ref[...])
        dy, _, dpg, dgt = vjp2(dx1_ref[...])
        _acc(dpg_ref, dpg)
        _acc(dgt_ref, dgt)
        oa, vjp_a = jax.vjp(_merge_gate, *[_from_res(r[...]) for r in (o0, o1, o2, l0, l1, l2)], ga_ref[...])
        yb, vjp_g = jax.vjp(_s5_gelu, yp_ref[...], u_ref[...], d_ref[...])
        gl = _bdot(yb, gw_ref[...]) + gbias_ref[...]
        ob, vjp_b = jax.vjp(_s5_glu, yb, gl, gb_ref[...])
        dw_ref[0:512, :] += _bdot_tn(oa, dy)
        dw_ref[512:1024, :] += _bdot_tn(ob, dy)
        d1, d2, d3, e1, e2, e3, dga = vjp_a(_bdot_nt(dy, w_ref[0:512, :]))
        for ref, val, d in zip((do0, do1, do2, dl0, dl1, dl2), (d1, d2, d3, e1, e2, e3), DILATIONS * 2):
            ref[...] = _to_res(val, d)
        dga_ref[...] = dga
        dyb, dgl, dgb = vjp_b(_bdot_nt(dy, w_ref[512:1024, :]))
        dgb_ref[...] = dgb
        dgw_ref[...] += _bdot_tn(yb, dgl)
        _acc(dgbias_ref, jnp.sum(dgl, axis=0, keepdims=True))
        dyp, du, dd = vjp_g(dyb + _bdot_nt(dgl, gw_ref[...]))
        dyp_ref[...] = dyp
        du_ref[...] = du
        _acc(dd_ref, dd)

    vec, half = _fix((1, D_MODEL)), _fix((1, 512))
    r5, r10 = _row(TM, 512), _row(TM, D_MODEL)
    res6 = [_res_spec(d) for d in DILATIONS] * 2
    return _call_with_exchange(
        body, "l0_out_bwd", (s_len // TM,),
        res6 + [r5] * 4 + [r10, r10, half, _fix((512, 512)), half, _fix((D_MODEL, D_MODEL)), vec, vec, r10],
        res6 + [r5] * 4 + [half, _fix((512, 512)), half, _fix((D_MODEL, D_MODEL)), vec, vec],
        [_sds(*_res_shape(s_len, d)) for d in DILATIONS] * 2 + [_sds(s_len, 512)] * 4
        + [_sds(1, 512), _sds(512, 512), _sds(1, 512), _sds(D_MODEL, D_MODEL), _sds(1, D_MODEL), _sds(1, D_MODEL)],
        [], (*os, *ls, ga, gb, ypre, u, x, y, d_skip, glu_w, glu_b, w_out, post_g, gate, dx1), exchange)


def _l1_front(x, pre_g, scale, shift, w_in):
    s_len = x.shape[0]

    def body(x_ref, g_ref, sc_ref, sh_ref, w_ref, raw_ref, gate_ref, ba_ref, h_ref):
        hb = _pre_mod(x_ref[...], g_ref[...], sc_ref[...], sh_ref[...]).astype(BF)
        h_ref[...] = hb
        z = jnp.dot(hb, w_ref[...], preferred_element_type=F32)
        raw_ref[...] = z[:, 0:QKV]
        gate_ref[...] = z[:, QKV:QKV + 1024]
        ba_ref[...] = z[:, QKV + 1024:C_IN_PAD]

    vec = _fix((1, D_MODEL))
    return pl.pallas_call(
        body, name="l1_front", grid=(s_len // TM,),
        in_specs=[_row(TM, D_MODEL), vec, vec, vec, _fix((D_MODEL, C_IN_PAD))],
        out_specs=[_row(TM, QKV), _row(TM, 1024), _row(TM, 128), _row(TM, D_MODEL)],
        out_shape=[_sds(s_len, QKV), _sds(s_len, 1024), _sds(s_len, 128), _sds(s_len, D_MODEL, dtype=BF)],
        compiler_params=_cp("arbitrary"),
    )(x, pre_g, scale, shift, w_in)


def _bg_fn(ba, alog_row, dtb_row):
    lane = lax.broadcasted_iota(jnp.int32, (1, 128), 1)
    g = -jnp.exp(alog_row) * jax.nn.softplus(ba + dtb_row)
    return jnp.where(lane < C_HEADS, jax.nn.sigmoid(ba), jnp.where(lane < 2 * C_HEADS, g, 0.0))


def _act_q(c):
    q = jax.nn.silu(c)
    return q * lax.rsqrt(jnp.sum(q * q, axis=-1, keepdims=True) + EPS) * (C_DK ** -0.5)


def _act_k(c):
    k = jax.nn.silu(c)
    return k * lax.rsqrt(jnp.sum(k * k, axis=-1, keepdims=True) + EPS)


def _act_of(s):
    return _act_q if s < 8 else (_act_k if s < 16 else jax.nn.silu)


def _conv_taps(prev8, tile_ref, sl, next8=None):
    rows = tile_ref.shape[0]
    head = jnp.concatenate([prev8, tile_ref[0:8, sl]], axis=0)
    tail = None if next8 is None else jnp.concatenate([tile_ref[rows - 8:rows, sl], next8], axis=0)
    taps = []
    for j in range(C_CONV):
        shift = C_CONV - 1 - j
        pieces = [head[8:] if shift == 0 else pltpu.roll(head, shift, 0)[8:], tile_ref[pl.ds(8 - shift, rows - 8), sl]]
        if tail is not None:
            pieces.append(tail[8:] if shift == 0 else pltpu.roll(tail, shift, 0)[8:])
        taps.append(jnp.concatenate(pieces, axis=0))
    return taps


def _gdn_prep(raw, ba, conv_w, alog_row, dtb_row):
    s_len = raw.shape[0]

    def body(raw_ref, halo_ref, ba_ref, w_ref, al_ref, dt_ref, qkv_ref, bg_ref):
        bg_ref[...] = _bg_fn(ba_ref[...], al_ref[...], dt_ref[...])
        has_prev = (pl.program_id(0) > 0).astype(F32)
        for s in range(24):
            sl = slice(s * 128, (s + 1) * 128)
            taps = _conv_taps(halo_ref[:, sl] * has_prev, raw_ref, sl)
            conv = w_ref[3:4, sl] * taps[3]
            for j in range(3):
                conv = conv + w_ref[j:j + 1, sl] * taps[j]
            qkv_ref[:, sl] = _act_of(s)(conv)

    halo = pl.BlockSpec((8, QKV), lambda i: (jnp.maximum(i * (TM // 8) - 1, 0), 0))
    row128 = _fix((1, 128))
    return pl.pallas_call(
        body, name="gdn_prep", grid=(s_len // TM,),
        in_specs=[_row(TM, QKV), halo, _row(TM, 128), _fix((C_CONV, QKV)), row128, row128],
        out_specs=[_row(TM, QKV), _row(TM, 128)],
        out_shape=[_sds(s_len, QKV), _sds(s_len, 128)],
        compiler_params=_cp("arbitrary"),
    )(raw, raw, ba, conv_w, alog_row, dtb_row)


def _gdn_prep_bwd(raw, ba, conv_w, alog_row, dtb_row, dq, dk, dv, dbg):
    s_len = raw.shape[0]
    n_tiles = s_len // TM

    def body(raw_ref, prev_ref, next_ref, ba_ref, w_ref, al_ref, dt_ref, dq_ref, dqn_ref, dk_ref, dkn_ref, dv_ref, dvn_ref, dbg_ref,
             draw_ref, dba_ref, dw_ref, dal_ref, ddt_ref, dconv_ref):
        _zero_at_first([dw_ref, dal_ref, ddt_ref])
        i = pl.program_id(0)
        _, vjp_bg = jax.vjp(_bg_fn, ba_ref[...], al_ref[...], dt_ref[...])
        dba, dal, ddt = vjp_bg(dbg_ref[...])
        dba_ref[...] = dba
        _acc(dal_ref, dal)
        _acc(ddt_ref, ddt)
        has_prev = (i > 0).astype(F32)
        has_next = (i < n_tiles - 1).astype(F32)
        ct_refs = ((dq_ref, dqn_ref), (dk_ref, dkn_ref), (dv_ref, dvn_ref))
        for s in range(24):
            sl = slice(s * 128, (s + 1) * 128)
            hl = slice((s % 8) * 128, (s % 8 + 1) * 128)
            tile_ref, nxt_ref = ct_refs[s // 8]
            taps = _conv_taps(prev_ref[:, sl] * has_prev, raw_ref, sl, next_ref[:, sl] * has_next)
            conv = w_ref[3:4, sl] * taps[3]
            for j in range(3):
                conv = conv + w_ref[j:j + 1, sl] * taps[j]
            ct = jnp.concatenate([tile_ref[:, hl], nxt_ref[:, hl] * has_next], axis=0)
            _, vjp_act = jax.vjp(_act_of(s), conv)
            dconv, = vjp_act(ct)
            dconv_ref[...] = dconv
            draw = w_ref[3:4, sl] * dconv[:TM]
            for j in range(3):
                draw = draw + w_ref[j:j + 1, sl] * dconv_ref[pl.ds(3 - j, TM), :]
            draw_ref[:, sl] = draw
            for j in range(4):
                dw_ref[j:j + 1, sl] += jnp.sum(dconv[:TM] * taps[j][:TM], axis=0, keepdims=True)

    prev = pl.BlockSpec((8, QKV), lambda i: (jnp.maximum(i * (TM // 8) - 1, 0), 0))
    nxt = lambda n: pl.BlockSpec((8, n), lambda i: (jnp.minimum((i + 1) * (TM // 8), s_len // 8 - 1), 0))
    row128 = _fix((1, 128))
    ct_specs = [_row(TM, 1024), nxt(1024)] * 3
    return pl.pallas_call(
        body, name="gdn_prep_bwd", grid=(n_tiles,),
        in_specs=[_row(TM, QKV), prev, nxt(QKV), _row(TM, 128), _fix((C_CONV, QKV)), row128, row128] + ct_specs + [_row(TM, 128)],
        out_specs=[_row(TM, QKV), _row(TM, 128), _fix((C_CONV, QKV)), row128, row128],
        out_shape=[_sds(s_len, QKV), _sds(s_len, 128), _sds(C_CONV, QKV), _sds(1, 128), _sds(1, 128)],
        scratch_shapes=[pltpu.VMEM((TM + 8, 128), F32)],
        compiler_params=_cp("arbitrary"),
    )(raw, raw, raw, ba, conv_w, alog_row, dtb_row, dq, dq, dk, dk, dv, dv, dbg)


def _tein(eq, a, b):
    return jnp.einsum(eq, a, b, precision=lax.Precision.HIGH, preferred_element_type=F32)


def _unit_lower_inverse(lower):
    ri = lax.broadcasted_iota(jnp.int32, (C_CHUNK, C_CHUNK), 0)
    ci = lax.broadcasted_iota(jnp.int32, (C_CHUNK, C_CHUNK), 1)
    eye = (ri == ci).astype(F32)[None]
    mm = functools.partial(_bein, 'hij,hjk->hik')
    same_block = lambda size: (ri // size == ci // size)[None]
    n_mat = jnp.where(same_block(4), -lower, 0.0)
    inv = mm(eye + n_mat, eye + mm(n_mat, n_mat))
    for size in (4, 8, 16, 32):
        below = jnp.where(same_block(2 * size) & jnp.logical_not(same_block(size)), lower, 0.0)
        inv = inv - mm(inv, mm(below, inv))
    inv = _tein('hij,hjk->hik', inv, 2.0 * eye - _tein('hij,hjk->hik', eye + lower, inv))
    return jnp.where((ri >= ci)[None], inv, 0.0)


@jax.custom_vjp
def _known_inverse(lower, inv):
    return inv


def _known_inverse_fwd(lower, inv):
    return inv, inv


def _known_inverse_bwd(inv, d_inv):
    d_lower = -_bein('hik,hjk->hij', _bein('hji,hjk->hik', inv, d_inv), inv)
    return d_lower, jnp.zeros_like(inv)


_known_inverse.defvjp(_known_inverse_fwd, _known_inverse_bwd)


def _gdn_local(q, k, v, bgs, inv_known=None):
    lane = lax.broadcasted_iota(jnp.int32, (1, 128), 1)
    ri = lax.broadcasted_iota(jnp.int32, (C_CHUNK, C_CHUNK), 0)
    ci = lax.broadcasted_iota(jnp.int32, (C_CHUNK, C_CHUNK), 1)
    row_id = lax.broadcasted_iota(jnp.int32, (128, C_CHUNK), 0)
    beta, gc, gcj = [], [], []
    for bg in bgs:
        gc_t = _hdot((ri >= ci).astype(F32), bg)
        gc_rows = gc_t.T
        for h in range(C_HEADS):
            beta.append(jnp.sum(jnp.where(lane == h, bg, 0.0), axis=-1, keepdims=True))
            gc.append(jnp.sum(jnp.where(lane == C_HEADS + h, gc_t, 0.0), axis=-1, keepdims=True))
            gcj.append(jnp.sum(jnp.where(row_id == C_HEADS + h, gc_rows, 0.0), axis=0, keepdims=True))
    beta, gc, gcj = jnp.stack(beta, axis=0), jnp.stack(gc, axis=0), jnp.stack(gcj, axis=0)
    tril, strict = (ri >= ci)[None], (ri > ci)[None]
    decay = jnp.exp(jnp.where(tril, gc - gcj, -1e30))
    kb = k * beta
    lower = jnp.where(strict, _bein('hid,hjd->hij', kb, k) * decay, 0.0)
    inv = _unit_lower_inverse(lower) if inv_known is None else _known_inverse(lower, inv_known)
    egc = jnp.exp(gc)
    u_c = _bein('hij,hjd->hid', inv, v * beta)
    w_c = _bein('hij,hjd->hid', inv, kb * egc)
    aqk = _bein('hid,hjd->hij', q, k) * decay
    rowi = lax.broadcasted_iota(jnp.int32, (1, C_CHUNK, 1), 1)
    g_last = jnp.sum(jnp.where(rowi == C_CHUNK - 1, gc, 0.0), axis=1, keepdims=True)
    kd = k * jnp.exp(g_last - gc)
    return (u_c, w_c, aqk, q * egc, kd, jnp.exp(g_last)), inv


def _gdn_state(local, state):
    u_c, w_c, aqk, qg, kd, dec = local
    v_new = u_c - _bein('hik,hkv->hiv', w_c, state)
    o = _bein('hik,hkv->hiv', qg, state) + _bein('hij,hjv->hiv', aqk, v_new)
    return o, state * dec + _bein('hik,hiv->hkv', kd, v_new)


C_SUB = 4


def _gdn_group(q, k, v, bgs, state, inv_known=None):
    local, inv = _gdn_local(q, k, v, bgs, inv_known)
    outs = []
    for s in range(len(bgs)):
        o, state = _gdn_state(tuple(t[s * C_HEADS:(s + 1) * C_HEADS] for t in local), state)
        outs.append(o)
    return outs, state, inv


def _heads(ref):
    return jnp.stack([ref[s * C_CHUNK:(s + 1) * C_CHUNK, h * C_DK:(h + 1) * C_DK] for s in range(C_SUB) for h in range(C_HEADS)], axis=0)


def _put_heads(ref, sub, val):
    rows = slice(sub * C_CHUNK, (sub + 1) * C_CHUNK)
    for h in range(C_HEADS):
        ref[rows, h * C_DK:(h + 1) * C_DK] = val[h]


def _gdn_specs(s_len, rev):
    rows = C_SUB * C_CHUNK
    n_g = s_len // rows
    at = (lambda i: n_g - 1 - i) if rev else (lambda i: i)
    col = lambda c: pl.BlockSpec((rows, 1024), lambda i: (at(i), c))
    row128 = pl.BlockSpec((rows, 128), lambda i: (at(i), 0))
    state = pl.BlockSpec((1, C_HEADS, C_DK, C_DK), lambda i: (at(i), 0, 0, 0))
    inv = pl.BlockSpec((1, C_SUB * C_HEADS, C_CHUNK, C_CHUNK), lambda i: (at(i), 0, 0, 0))
    return n_g, col, row128, state, inv


def _gdn_fwd(qkv, bg):
    s_len = qkv.shape[0]
    n_g, col, row128, state_spec, inv_spec = _gdn_specs(s_len, False)

    def body(q_ref, k_ref, v_ref, bg_ref, o_ref, ss_ref, inv_ref, st_ref):
        _zero_at_first([st_ref])
        s0 = st_ref[...]
        ss_ref[0] = s0
        bgs = [bg_ref[s * C_CHUNK:(s + 1) * C_CHUNK, :] for s in range(C_SUB)]
        outs, s2, inv = _gdn_group(_heads(q_ref), _heads(k_ref), _heads(v_ref), bgs, s0)
        st_ref[...] = s2
        inv_ref[0] = inv
        for s in range(C_SUB):
            _put_heads(o_ref, s, outs[s])

    return pl.pallas_call(
        body, name="gdn_fwd", grid=(n_g,),
        in_specs=[col(0), col(1), col(2), row128],
        out_specs=[col(0), state_spec, inv_spec],
        out_shape=[_sds(s_len, 1024), _sds(n_g, C_HEADS, C_DK, C_DK), _sds(n_g, C_SUB * C_HEADS, C_CHUNK, C_CHUNK)],
        scratch_shapes=[pltpu.VMEM((C_HEADS, C_DK, C_DK), F32)],
        compiler_params=_cp("arbitrary"),
    )(qkv, qkv, qkv, bg)


def _gdn_bwd(qkv, bg, states, invs, do):
    s_len = qkv.shape[0]
    n_g, col, row128, state_spec, inv_spec = _gdn_specs(s_len, True)

    def body(q_ref, k_ref, v_ref, bg_ref, ss_ref, inv_ref, do_ref, dq_ref, dk_ref, dv_ref, dbg_ref, ds_ref):
        _zero_at_first([ds_ref])
        inv_known = inv_ref[0]

        def group(q, k, v, bgs, st):
            outs, st2, _ = _gdn_group(q, k, v, bgs, st, inv_known)
            return outs, st2

        bgs = [bg_ref[s * C_CHUNK:(s + 1) * C_CHUNK, :] for s in range(C_SUB)]
        _, vjp = jax.vjp(group, _heads(q_ref), _heads(k_ref), _heads(v_ref), bgs, ss_ref[0])
        douts = [jnp.stack([do_ref[s * C_CHUNK:(s + 1) * C_CHUNK, h * C_DK:(h + 1) * C_DK] for h in range(C_HEADS)], axis=0)
                 for s in range(C_SUB)]
        dq, dk, dv, dbgs, ds = vjp((douts, ds_ref[...]))
        ds_ref[...] = ds
        for s in range(C_SUB):
            dbg_ref[s * C_CHUNK:(s + 1) * C_CHUNK, :] = dbgs[s]
            for ref, val in ((dq_ref, dq), (dk_ref, dk), (dv_ref, dv)):
                _put_heads(ref, s, val[s * C_HEADS:(s + 1) * C_HEADS])

    return pl.pallas_call(
        body, name="gdn_bwd", grid=(n_g,),
        in_specs=[col(0), col(1), col(2), row128, state_spec, inv_spec, col(0)],
        out_specs=[col(0), col(0), col(0), row128],
        out_shape=[_sds(s_len, 1024)] * 3 + [_sds(s_len, 128)],
        scratch_shapes=[pltpu.VMEM((C_HEADS, C_DK, C_DK), F32)],
        compiler_params=_cp("arbitrary"),
    )(qkv, qkv, qkv, bg, states, invs, do)


def _head_norm_gate(o, gate, norm_g):
    return (_rms(o) * norm_g) * jax.nn.silu(gate)


def _l1_out_fb(o, gate_c, x1, target, norm_g, w_out, post_g, gate):
    s_len = x1.shape[0]

    def body(o_ref, gc_ref, x1_ref, t_ref, ng_ref, w_ref, pg_ref, gt_ref,
             loss_ref, dres_ref, do_ref, dgc_ref, dw_ref, dng_ref, dpg_ref, dgt_ref):
        _zero_at_first([loss_ref, dw_ref, dng_ref, dpg_ref, dgt_ref])
        ng = ng_ref[...]
        ons, vjps = [], []
        for h in range(C_HEADS):
            sl = slice(h * C_DK, (h + 1) * C_DK)
            on, vjp_h = jax.vjp(_head_norm_gate, o_ref[:, sl], gc_ref[:, sl], ng)
            ons.append(on)
            vjps.append(vjp_h)
        on_all = jnp.concatenate(ons, axis=-1)
        y = _bdot(on_all, w_ref[...])
        x2, vjp2 = jax.vjp(_post_res, y, x1_ref[...], pg_ref[...], gt_ref[...])
        err = x2 - t_ref[...]
        _acc(loss_ref, jnp.full((1, 128), 0.5 * jnp.sum(jnp.mean(err * err, axis=-1)), F32))
        dx2 = err * (1.0 / D_MODEL)
        dy, _, dpg, dgt = vjp2(dx2)
        dres_ref[...] = dx2
        _acc(dpg_ref, dpg)
        _acc(dgt_ref, dgt)
        dw_ref[...] += _bdot_tn(on_all, dy)
        don = _bdot_nt(dy, w_ref[...])
        for h in range(C_HEADS):
            sl = slice(h * C_DK, (h + 1) * C_DK)
            do_h, dgc_h, dng = vjps[h](don[:, sl])
            do_ref[:, sl] = do_h
            dgc_ref[:, sl] = dgc_h
            _acc(dng_ref, dng)

    vec, r10 = _fix((1, D_MODEL)), _row(TM, D_MODEL)
    row128 = _fix((1, 128))
    return pl.pallas_call(
        body, name="l1_out_fb", grid=(s_len // TM,),
        in_specs=[r10, r10, r10, r10, row128, _fix((D_MODEL, D_MODEL)), vec, vec],
        out_specs=[row128, r10, r10, r10, _fix((D_MODEL, D_MODEL)), row128, vec, vec],
        out_shape=[_sds(1, 128), _sds(s_len, D_MODEL), _sds(s_len, D_MODEL), _sds(s_len, D_MODEL),
                   _sds(D_MODEL, D_MODEL), _sds(1, 128), _sds(1, D_MODEL), _sds(1, D_MODEL)],
        compiler_params=_cp("arbitrary"),
    )(o, gate_c, x1, target, norm_g, w_out, post_g, gate)


def _row_of(v, width, at):
    return jnp.zeros((1, width), F32).at[0, at:at + v.shape[-1]].set(v.reshape(-1))


def _local_step(x, target, mod, wd, comm=None):
    s_len = x.shape[0]
    shift0, scale0, gate0 = (mod[0:1, i * 1024:(i + 1) * 1024] for i in range(3))
    shift1, scale1, gate1 = (mod[1:2, i * 1024:(i + 1) * 1024] for i in range(3))
    pre_g0, pre_g1 = wd["pre_g"][0:1], wd["pre_g"][1:2]
    post_g0, post_g1 = wd["post_g"][0:1], wd["post_g"][1:2]
    w_in0 = wd["ab_w_in"].astype(BF)
    d_skip, glu_b = wd["s5_d"].reshape(1, 512), wd["s5_glu_b"].reshape(1, 512)
    norm_g = wd["gdn_norm_g"].reshape(1, 128)
    alog_row = _row_of(wd["gdn_a_log"], 128, C_HEADS)
    dtb_row = _row_of(wd["gdn_dt_bias"], 128, C_HEADS)
    conv_w = wd["gdn_conv"]

    a_re, a_im = wd["s5_a_re"], wd["s5_a_im"]
    log_dt = wd["s5_log_dt"].reshape(B_GROUPS, 1)
    bt_re = wd["s5_b_re"].transpose(0, 2, 1).reshape(B_WIDTH, B_STATE)
    bt_im = wd["s5_b_im"].transpose(0, 2, 1).reshape(B_WIDTH, B_STATE)
    abar_r, abar_i, bbar_r, bbar_i = _s5_params(a_re, a_im, log_dt, bt_re, bt_im)
    abr, abi = abar_r.reshape(1, -1), abar_i.reshape(1, -1)
    btr, bti = _blockdiag_b(bbar_r).astype(BF), _blockdiag_b(bbar_i).astype(BF)
    ctr, cti = _blockdiag_c(wd["s5_c_re"]).astype(BF), _blockdiag_c(wd["s5_c_im"]).astype(BF)

    table = _bucket_table()
    biases = _attn_bias(wd["rel_bias"], jnp.asarray(table))
    front = _l0_front(x, pre_g0, scale0, shift0, w_in0)
    qs, ks, vs = front[0:3], front[3:6], front[6:9]
    u, ga, gb, h0 = front[9:]
    riders = [None] * 4 if comm is None else comm.late_exchanges()
    os, ls, got = [], [], []
    for i in range(3):
        (o_d, l_d), g = _attn_fwd(qs[i], ks[i], vs[i], biases[i], exchange=riders[i])
        os.append(o_d)
        ls.append(l_d)
        got.append(g)
    seg_len = s_len // S5_SEG
    zero_state = (jnp.zeros((S5_SEG, S5_WIDTH), F32),) * 2
    ends, _ = _s5_seq_fwd(u, btr, bti, ctr, cti, abr, abi, zero_state, False)
    x_entry = _s5_entries("s5_entries_fwd", *ends, abr, abi, seg_len, False)
    (xr, xi, ypre3, _, _), g = _s5_seq_fwd(u, btr, bti, ctr, cti, abr, abi, x_entry, True, exchange=riders[3])
    got.append(g)
    ypre = ypre3.reshape(s_len, B_WIDTH)
    rider = None
    if comm is not None:
        mine0, mine1 = comm.late_halves(got)
        wd = {**wd, **comm.full_weights(["ab_w_out", "s5_glu_w"], mine0, _sibling_exchange("gather_w_sibling_l0", mine0))}
        rider = (mine1, "sibling")
    w_out0 = wd["ab_w_out"].astype(BF)
    glu_w = wd["s5_glu_w"].astype(BF)
    (x1, y0), theirs1 = _l0_out(os, ls, ga, gb, ypre, u, x, d_skip, glu_w, glu_b, w_out0, post_g0, gate0, exchange=rider)
    if comm is not None:
        wd = {**wd, **comm.full_weights(["gdn_w_in", "gdn_w_out"], mine1, theirs1)}
    w_in1 = wd["gdn_w_in"]
    if w_in1.shape[1] == C_IN:
        w_in1 = jnp.concatenate([w_in1, jnp.zeros((D_MODEL, C_IN_PAD - C_IN), w_in1.dtype)], axis=1)
    w_in1 = w_in1.astype(BF)
    w_out1 = wd["gdn_w_out"].astype(BF)

    raw, gate_c, ba, h1 = _l1_front(x1, pre_g1, scale1, shift1, w_in1)
    qkv, bg = _gdn_prep(raw, ba, conv_w, alog_row, dtb_row)
    o_gdn, states, invs = _gdn_fwd(qkv, bg)
    loss_row, dres1, do_gdn, dgate_c, dw_out1, dnorm_g, dpost_g1, dgate1 = _l1_out_fb(
        o_gdn, gate_c, x1, target, norm_g, w_out1, post_g1, gate1)

    dq1, dk1, dv1, dbg = _gdn_bwd(qkv, bg, states, invs, do_gdn)
    draw, dba, dconv_w, dalog_row, ddtb_row = _gdn_prep_bwd(raw, ba, conv_w, alog_row, dtb_row, dq1, dk1, dv1, dbg)
    dz1, dx1, dpre_g1, dscale1, dshift1 = _front_bwd(
        "l1_front_bwd", x1, pre_g1, scale1, shift1, w_in1, dres1, [[draw], [dgate_c], [dba]], [QKV, 1024, 128])
    dw_in1 = _matmul_tn("l1_dw_in", h1, dz1, 1408)

    l1_names, l0_names = ["gdn_w_in", "gdn_w_out"], ["ab_w_out", "s5_glu_w"]
    rider = None if comm is None else (comm.split_halves({"gdn_w_in": dw_in1, "gdn_w_out": dw_out1}), "sibling")
    l0b, from_sibling1 = _l0_out_bwd(os, ls, ga, gb, ypre, u, x, y0, d_skip, glu_w, glu_b, w_out0, post_g0, gate0, dx1, exchange=rider)
    dos, dls = l0b[0:3], l0b[3:6]
    dga, dgb, dypre, du_skip, dd_skip, dglu_w, dglu_b, dw_out0, dpost_g0, dgate0 = l0b[6:]
    rider = None if comm is None else (comm.split_halves({"ab_w_out": dw_out0, "s5_glu_w": dglu_w}), "sibling")
    starts, _ = _s5_seq_bwd(dypre, None, None, None, btr, bti, ctr, cti, abr, abi, zero_state, None, False)
    g_entry = _s5_entries("s5_entries_bwd", *starts, abr, -abi, seg_len, True)
    (du3, dbtr, dbti, dctr, dcti, dabr, dabi, _, _), from_sibling0 = _s5_seq_bwd(
        dypre, xr, xi, u, btr, bti, ctr, cti, abr, abi, g_entry, x_entry, True, exchange=rider)
    du_scan = du3.reshape(s_len, B_WIDTH)
    riders = [None] * 3
    if comm is not None:
        riders = [(comm.chip_partials(l1_names, from_sibling1), "scatter"), (comm.chip_partials(l0_names, from_sibling0), "scatter"), None]
    dqs, dks, dvs, dbs = [], [], [], []
    for i in range(3):
        (dq_d, dk_d, dv_d, db_d), got_d = _attn_bwd(qs[i], ks[i], vs[i], biases[i], os[i], ls[i], dos[i], dls[i], exchange=riders[i])
        dqs.append(dq_d)
        dks.append(dk_d)
        dvs.append(dv_d)
        dbs.append(db_d)
        if comm is not None and riders[i] is not None:
            comm.received.update(zip((l1_names, l0_names)[i], got_d))
    parts = [dqs, dks, dvs, [du_skip, du_scan], [dga], [dgb]]
    dz0, grad_x, dpre_g0, dscale0, dshift0 = _front_bwd(
        "l0_front_bwd", x, pre_g0, scale0, shift0, w_in0, dx1, parts, [512] * 6)
    dw_in0 = _matmul_tn("l0_dw_in", h0, dz0, 1536)

    idx_rows = jnp.asarray(table.reshape(3, -1), F32)
    drel = _rel_bias_grad(dbs, idx_rows).T
    da_re, da_im, dlog_dt, dbt_re, dbt_im = _s5_params_bwd(
        a_re, a_im, log_dt, bt_re, bt_im, dabr.reshape(B_GROUPS, B_STATE), dabi.reshape(B_GROUPS, B_STATE),
        _blockdiag_b_t(dbtr), _blockdiag_b_t(dbti))
    unb = lambda d: d.reshape(B_GROUPS, B_GROUP, B_STATE).transpose(0, 2, 1)
    grads = {
        "pre_g": jnp.concatenate([dpre_g0, dpre_g1], 0), "post_g": jnp.concatenate([dpost_g0, dpost_g1], 0),
        "rel_bias": drel, "ab_w_in": dw_in0, "ab_w_out": dw_out0,
        "s5_a_re": da_re, "s5_a_im": da_im, "s5_log_dt": dlog_dt.reshape(B_GROUPS),
        "s5_b_re": unb(dbt_re), "s5_b_im": unb(dbt_im),
        "s5_c_re": _blockdiag_c_t(dctr), "s5_c_im": _blockdiag_c_t(dcti),
        "s5_d": dd_skip.reshape(512), "s5_glu_w": dglu_w, "s5_glu_b": dglu_b.reshape(512),
        "gdn_w_in": dw_in1[:, :C_IN], "gdn_conv": dconv_w,
        "gdn_a_log": dalog_row[0, C_HEADS:2 * C_HEADS], "gdn_dt_bias": ddtb_row[0, C_HEADS:2 * C_HEADS],
        "gdn_norm_g": dnorm_g.reshape(128), "gdn_w_out": dw_out1,
    }
    dmod = jnp.concatenate([jnp.concatenate([dshift0, dscale0, dgate0], 1), jnp.concatenate([dshift1, dscale1, dgate1], 1)], 0)
    return loss_row[0, 0], grad_x, grads, dmod


def _place():
    return lax.axis_index("x"), lax.axis_index("y"), lax.axis_index("c")


def _flip(v, bit):
    return 1 - v if bit else v


def _hbm_call(name, body, arrs, out_shapes, n_sem):
    any_spec = pl.BlockSpec(memory_space=pl.ANY)
    return pl.pallas_call(
        body, name=name,
        in_specs=[any_spec] * len(arrs), out_specs=[any_spec] * len(out_shapes), out_shape=out_shapes,
        scratch_shapes=[pltpu.SemaphoreType.DMA((n_sem,)), pltpu.SemaphoreType.DMA((n_sem,))],
    )(*arrs)


def _own_slot(gathered, own, slot):
    idx = lax.broadcasted_iota(jnp.int32, (gathered.shape[0],) + (1,) * own.ndim, 0)
    return jnp.where(idx == slot, own[None], gathered)


def _all_gather8(name, arr):
    def body(x_ref, out_ref, send_sems, recv_sems):
        x, y, c = _place()
        me = 4 * x + 2 * y + c
        sends, recvs = [], []
        for m in range(1, 8):
            peer = (_flip(x, m & 4), _flip(y, m & 2), _flip(c, m & 1))
            sends.append(pltpu.make_async_remote_copy(x_ref, out_ref.at[me], send_sems.at[m - 1], recv_sems.at[m - 1],
                                                      device_id=peer, device_id_type=MESH))
            recvs.append(pltpu.make_async_remote_copy(x_ref, out_ref.at[4 * peer[0] + 2 * peer[1] + peer[2]], send_sems.at[m - 1],
                                                      recv_sems.at[m - 1], device_id=peer, device_id_type=MESH))
        for cp in sends:
            cp.start()
        for cp in recvs:
            cp.wait_recv()
        for cp in sends:
            cp.wait_send()

    return _hbm_call(name, body, [arr], [jax.ShapeDtypeStruct((8,) + arr.shape, arr.dtype)], 7)[0]


def _all_to_all8(name, arr):
    def body(x_ref, out_ref, send_sems, recv_sems):
        x, y, c = _place()
        me = 4 * x + 2 * y + c
        sends, recvs = [], []
        for m in range(1, 8):
            peer = (_flip(x, m & 4), _flip(y, m & 2), _flip(c, m & 1))
            peer_id = 4 * peer[0] + 2 * peer[1] + peer[2]
            sends.append(pltpu.make_async_remote_copy(x_ref.at[peer_id], out_ref.at[me], send_sems.at[m - 1], recv_sems.at[m - 1],
                                                      device_id=peer, device_id_type=MESH))
            recvs.append(pltpu.make_async_remote_copy(x_ref.at[peer_id], out_ref.at[peer_id], send_sems.at[m - 1], recv_sems.at[m - 1],
                                                      device_id=peer, device_id_type=MESH))
        for cp in sends:
            cp.start()
        for cp in recvs:
            cp.wait_recv()
        for cp in sends:
            cp.wait_send()

    return _hbm_call(name, body, [arr], [jax.ShapeDtypeStruct(arr.shape, arr.dtype)], 7)[0]


def _chip_copies(ins, outs, send_sems, recv_sems, scatter):
    x, y, c = _place()
    mine = 2 * x + y
    sends, recvs = [], []
    for a in range(len(ins)):
        for m in range(1, 4):
            px, py = _flip(x, m & 2), _flip(y, m & 1)
            k = 3 * a + m - 1
            src = ins[a].at[2 * px + py] if scatter else ins[a]
            sends.append(pltpu.make_async_remote_copy(src, outs[a].at[mine], send_sems.at[k], recv_sems.at[k],
                                                      device_id=(px, py, c), device_id_type=MESH))
            recvs.append(pltpu.make_async_remote_copy(src, outs[a].at[2 * px + py], send_sems.at[k], recv_sems.at[k],
                                                      device_id=(px, py, c), device_id_type=MESH))
    return sends, recvs


def _chip_shapes(arrs, scatter):
    return [jax.ShapeDtypeStruct(a.shape if scatter else (4,) + a.shape, a.dtype) for a in arrs]


def _chip_exchange(name, arrs, scatter):
    n = len(arrs)

    def body(*refs):
        sends, recvs = _chip_copies(refs[:n], refs[n:2 * n], refs[2 * n], refs[2 * n + 1], scatter)
        for cp in sends:
            cp.start()
        for cp in recvs:
            cp.wait_recv()
        for cp in sends:
            cp.wait_send()

    return _hbm_call(name, body, arrs, _chip_shapes(arrs, scatter), 3 * n)


def _call_with_exchange(body, name, grid, in_specs, out_specs, out_shape, scratch_shapes, args, exchange):
    if exchange is None:
        return pl.pallas_call(body, name=name, grid=grid, in_specs=in_specs, out_specs=out_specs, out_shape=out_shape,
                              scratch_shapes=scratch_shapes, compiler_params=_cp(*["arbitrary"] * len(grid)))(*args), []
    arrs, kind = exchange
    n_in, n_out, n_ex, n_scr = len(in_specs), len(out_specs), len(arrs), len(scratch_shapes)
    n_sem = n_ex if kind == "sibling" else 3 * n_ex
    ex_shapes = [jax.ShapeDtypeStruct(a.shape, a.dtype) for a in arrs] if kind == "sibling" else _chip_shapes(arrs, kind == "scatter")

    def fused(*refs):
        ins, ex_in = refs[:n_in], refs[n_in:n_in + n_ex]
        outs, ex_out = refs[n_in + n_ex:n_in + n_ex + n_out], refs[n_in + n_ex + n_out:n_in + 2 * n_ex + n_out]
        rest = refs[n_in + 2 * n_ex + n_out:]
        if kind == "sibling":
            sends = recvs = _sibling_copies(ex_in, ex_out, rest[n_scr], rest[n_scr + 1])
        else:
            sends, recvs = _chip_copies(ex_in, ex_out, rest[n_scr], rest[n_scr + 1], kind == "scatter")
        first, last = pl.program_id(0) == 0, pl.program_id(0) == grid[0] - 1
        for k in range(1, len(grid)):
            first, last = first & (pl.program_id(k) == 0), last & (pl.program_id(k) == grid[k] - 1)

        @pl.when(first)
        def _():
            for cp in sends:
                cp.start()

        body(*ins, *outs, *rest[:n_scr])

        @pl.when(last)
        def _():
            for cp in recvs:
                cp.wait_recv()
            for cp in sends:
                cp.wait_send()

    any_spec = pl.BlockSpec(memory_space=pl.ANY)
    res = pl.pallas_call(
        fused, name=name, grid=grid, in_specs=list(in_specs) + [any_spec] * n_ex, out_specs=list(out_specs) + [any_spec] * n_ex,
        out_shape=list(out_shape) + ex_shapes,
        scratch_shapes=list(scratch_shapes) + [pltpu.SemaphoreType.DMA((n_sem,))] * 2,
        compiler_params=_cp(*["arbitrary"] * len(grid)))(*args, *arrs)
    return res[:n_out], res[n_out:]


def _sibling_copies(ins, outs, send_sems, recv_sems):
    x, y, c = _place()
    return [pltpu.make_async_remote_copy(ins[a], outs[a], send_sems.at[a], recv_sems.at[a],
                                         device_id=(x, y, 1 - c), device_id_type=MESH) for a in range(len(ins))]


def _sibling_exchange(name, arrs):
    n = len(arrs)

    def body(*refs):
        copies = _sibling_copies(refs[:n], refs[n:2 * n], refs[2 * n], refs[2 * n + 1])
        for cp in copies:
            cp.start()
        for cp in copies:
            cp.wait_recv()
        for cp in copies:
            cp.wait_send()

    return _hbm_call(name, body, arrs, [jax.ShapeDtypeStruct(a.shape, a.dtype) for a in arrs], n)


def _row_tile(rows):
    for t in (256, 128, 64, 32, 16, 8):
        if rows % t == 0:
            return t
    return rows


def _pair_sum(name, a, b, out_dtype):
    rows, cols = a.shape
    tr = _row_tile(rows)

    def body(a_ref, b_ref, o_ref):
        o_ref[...] = (a_ref[...] + b_ref[...]).astype(out_dtype)

    return pl.pallas_call(body, name=name, grid=(rows // tr,), in_specs=[_row(tr, cols)] * 2, out_specs=_row(tr, cols),
                          out_shape=_sds(rows, cols, dtype=out_dtype), compiler_params=_cp("arbitrary"))(a, b)


def _chip_sum(name, recv, partial, mine):
    n, rows, cols = recv.shape
    tr = _row_tile(rows)

    def body(mine_ref, *refs):
        own = refs[n][0].astype(F32)
        acc = None
        for s in range(n):
            term = jnp.where(mine_ref[0] == s, own, refs[s][0].astype(F32))
            acc = term if acc is None else acc + term
        refs[-1][...] = acc

    def slot_spec(s):
        return pl.BlockSpec((1, tr, cols), lambda i, m: (jnp.where(m[0] == s, (s + 1) % n, s), i, 0))

    grid_spec = pltpu.PrefetchScalarGridSpec(
        num_scalar_prefetch=1, grid=(rows // tr,),
        in_specs=[slot_spec(s) for s in range(n)] + [pl.BlockSpec((1, tr, cols), lambda i, m: (m[0], i, 0))],
        out_specs=pl.BlockSpec((tr, cols), lambda i, m: (i, 0)))
    return pl.pallas_call(body, name=name, grid_spec=grid_spec, out_shape=_sds(rows, cols),
                          compiler_params=_cp("arbitrary"))(mine, *([recv] * n), partial)


def _slot_sum(name, arr):
    n, rows, cols = arr.shape
    tr = _row_tile(rows)

    def body(*refs):
        acc = refs[0][0]
        for r in refs[1:-1]:
            acc = acc + r[0]
        refs[-1][...] = acc

    specs = [pl.BlockSpec((1, tr, cols), functools.partial(lambda s, i: (s, i, 0), s)) for s in range(n)]
    return pl.pallas_call(body, name=name, grid=(rows // tr,), in_specs=specs, out_specs=_row(tr, cols),
                          out_shape=_sds(rows, cols), compiler_params=_cp("arbitrary"))(*([arr] * n))


def _adamw(name, w, g, m, v, exchange=None):
    rows, cols = w.shape
    tr = _row_tile(rows)

    def body(w_ref, g_ref, m_ref, v_ref, d_ref, nm_ref, nv_ref):
        g_ = g_ref[...]
        m_ = ADAM_B1 * m_ref[...] + (1.0 - ADAM_B1) * g_
        v_ = ADAM_B2 * v_ref[...] + (1.0 - ADAM_B2) * (g_ * g_)
        m_hat = m_ / (1.0 - ADAM_B1 ** ADAM_STEP)
        v_hat = v_ / (1.0 - ADAM_B2 ** ADAM_STEP)
        d_ref[...] = -ADAM_LR * (m_hat / (jnp.sqrt(v_hat) + ADAM_EPS) + ADAM_WD * w_ref[...])
        nm_ref[...] = m_
        nv_ref[...] = v_

    spec = _row(tr, cols)
    return _call_with_exchange(body, name, (rows // tr,), [spec] * 4, [spec] * 3, [_sds(rows, cols)] * 3, [], (w, g, m, v), exchange)


def _adamw_many(name, ws, gs, ms, vs):
    n = len(ws)

    def body(*refs):
        for i in range(n):
            w_ref, g_ref, m_ref, v_ref = (refs[k * n + i] for k in range(4))
            d_ref, nm_ref, nv_ref = (refs[(4 + k) * n + i] for k in range(3))
            g_ = g_ref[...]
            m_ = ADAM_B1 * m_ref[...] + (1.0 - ADAM_B1) * g_
            v_ = ADAM_B2 * v_ref[...] + (1.0 - ADAM_B2) * (g_ * g_)
            m_hat = m_ / (1.0 - ADAM_B1 ** ADAM_STEP)
            v_hat = v_ / (1.0 - ADAM_B2 ** ADAM_STEP)
            d_ref[...] = -ADAM_LR * (m_hat / (jnp.sqrt(v_hat) + ADAM_EPS) + ADAM_WD * w_ref[...])
            nm_ref[...] = m_
            nv_ref[...] = v_

    shapes = [_sds(*a.shape) for a in ws]
    res = pl.pallas_call(body, name=name, out_shape=shapes * 3,
                         compiler_params=pltpu.CompilerParams(vmem_limit_bytes=VMEM_LIMIT_BYTES))(*ws, *gs, *ms, *vs)
    return [(res[i], res[n + i], res[2 * n + i]) for i in range(n)]


def _adamw_halves(name, w, g_mine, g_sibling, m, v, core):
    _, rows, cols = w.shape
    half = rows // 2
    tr = _row_tile(half)
    per_half = half // tr

    def body(core_ref, w_ref, gm_ref, gs_ref, m_ref, v_ref, g_ref, d_ref, nm_ref, nv_ref):
        g_ = jnp.where(pl.program_id(0) // per_half == core_ref[0], gm_ref[...], gs_ref[...])
        m_ = ADAM_B1 * m_ref[...] + (1.0 - ADAM_B1) * g_
        v_ = ADAM_B2 * v_ref[...] + (1.0 - ADAM_B2) * (g_ * g_)
        m_hat = m_ / (1.0 - ADAM_B1 ** ADAM_STEP)
        v_hat = v_ / (1.0 - ADAM_B2 ** ADAM_STEP)
        g_ref[...] = g_
        d_ref[...] = -ADAM_LR * (m_hat / (jnp.sqrt(v_hat) + ADAM_EPS) + ADAM_WD * w_ref[...])
        nm_ref[...] = m_
        nv_ref[...] = v_

    full = pl.BlockSpec((None, tr, cols), lambda i, c: (0, i, 0))
    in_half = pl.BlockSpec((tr, cols), lambda i, c: (i % per_half, 0))
    grid_spec = pltpu.PrefetchScalarGridSpec(num_scalar_prefetch=1, grid=(rows // tr,),
                                             in_specs=[full, in_half, in_half, full, full], out_specs=[full] * 4)
    return pl.pallas_call(body, name=name, grid_spec=grid_spec, out_shape=[_sds(1, rows, cols)] * 4,
                          compiler_params=_cp("arbitrary"))(core, w, g_mine, g_sibling, m, v)


def _mod_local(c_all, ada_w):
    def body(c_ref, w_ref, o_ref):
        c_act = jax.nn.silu(c_ref[...])
        for l in range(2):
            o_ref[l] = _hdot(c_act, w_ref[l])

    return pl.pallas_call(body, name="mod_local", out_shape=_sds(2, 8, ada_w.shape[2]),
                          compiler_params=pltpu.CompilerParams(vmem_limit_bytes=VMEM_LIMIT_BYTES))(c_all, ada_w)


def _ada_w_grad(c_all, dmod_cols):
    def body(c_ref, d_ref, o_ref):
        c_act = jax.nn.silu(c_ref[...])
        for l in range(2):
            o_ref[l] = lax.dot_general(c_act, d_ref[l], (((0,), (0,)), ((), ())), precision=HI, preferred_element_type=F32)

    return pl.pallas_call(body, name="ada_w_grad", out_shape=_sds(2, D_MODEL, dmod_cols.shape[2]),
                          compiler_params=pltpu.CompilerParams(vmem_limit_bytes=VMEM_LIMIT_BYTES))(c_all, dmod_cols)


_SMALL = ("ada_b", "pre_g", "post_g", "rel_bias", "s5_a_re", "s5_a_im", "s5_log_dt", "s5_b_re", "s5_b_im", "s5_c_re", "s5_c_im",
          "s5_d", "s5_glu_b", "gdn_a_log", "gdn_dt_bias", "gdn_norm_g")
_SHARDED = ("ab_w_in", "ab_w_out", "s5_glu_w", "gdn_w_in", "gdn_w_out")
_COL_SHARDED = ("ab_w_in", "gdn_w_in")
_WEIGHTS = ("ada_w", "ada_b", "pre_g", "post_g", "rel_bias", "ab_w_in", "ab_w_out", "s5_a_re", "s5_a_im", "s5_log_dt", "s5_b_re",
            "s5_b_im", "s5_c_re", "s5_c_im", "s5_d", "s5_glu_w", "s5_glu_b", "gdn_w_in", "gdn_conv", "gdn_a_log", "gdn_dt_bias",
            "gdn_norm_g", "gdn_w_out")


def _rows128(n):
    return -(-n // 128)


def _pack(arrs, total_rows):
    pieces = []
    for a in arrs:
        flat = a.reshape(-1)
        pieces.append(jnp.pad(flat, (0, _rows128(flat.shape[0]) * 128 - flat.shape[0])).reshape(-1, 128))
    used = sum(p.shape[0] for p in pieces)
    pieces.append(jnp.zeros((total_rows - used, 128), F32))
    return jnp.concatenate(pieces, axis=0)


def _unpack(buf, shapes):
    out, at = [], 0
    for shp in shapes:
        n = int(np.prod(shp))
        out.append(buf[at:at + _rows128(n)].reshape(-1)[:n].reshape(shp))
        at += _rows128(n)
    return out


def _full_from_halves(g):
    return g.transpose(1, 0, 2, 3).reshape(8 * g.shape[2], g.shape[3])


def _join_col_shards(name, mine, theirs, core, width):
    n, h, cols = mine.shape
    tr = _row_tile(h)
    per_half = h // tr

    def body(core_ref, a_ref, b_ref, o_ref):
        def put(src_ref):
            pad = [jnp.zeros((tr, width - n * cols), o_ref.dtype)] if width > n * cols else []
            o_ref[...] = jnp.concatenate([src_ref[s] for s in range(n)] + pad, axis=1)

        is_mine = pl.program_id(0) // per_half == core_ref[0]
        pl.when(is_mine)(functools.partial(put, a_ref))
        pl.when(jnp.logical_not(is_mine))(functools.partial(put, b_ref))

    half = pl.BlockSpec((n, tr, cols), lambda i, c: (0, i % per_half, 0))
    grid_spec = pltpu.PrefetchScalarGridSpec(num_scalar_prefetch=1, grid=(2 * per_half,), in_specs=[half, half],
                                             out_specs=pl.BlockSpec((tr, width), lambda i, c: (i, 0)))
    return pl.pallas_call(body, name=name, grid_spec=grid_spec, out_shape=_sds(2 * h, width, dtype=mine.dtype),
                          compiler_params=_cp("arbitrary"))(core, mine, theirs)


def _split_col_shards(name, g, cols, half, add=None, out_dtype=F32):
    h = g.shape[0] // 2
    tr = _row_tile(h)
    per_half = h // tr

    def body(half_ref, g_ref, *refs):
        for s in range(4):
            part = g_ref[:, s * cols:(s + 1) * cols]
            refs[-1][s] = (part if add is None else part + refs[0][s]).astype(out_dtype)

    shards = pl.BlockSpec((4, tr, cols), lambda i, c: (0, i, 0))
    grid_spec = pltpu.PrefetchScalarGridSpec(
        num_scalar_prefetch=1, grid=(per_half,),
        in_specs=[pl.BlockSpec((tr, g.shape[1]), lambda i, c: (c[0] * per_half + i, 0))] + ([] if add is None else [shards]),
        out_specs=shards)
    return pl.pallas_call(body, name=name, grid_spec=grid_spec, out_shape=_sds(4, h, cols, dtype=out_dtype),
                          compiler_params=_cp("arbitrary"))(half, g, *([] if add is None else [add]))


_LATE = ("ab_w_out", "s5_glu_w", "gdn_w_in", "gdn_w_out")


class _WeightExchanges:
    def __init__(self, shards, core, chip):
        self.core, self.chip = core, chip
        self.core_1 = jnp.reshape(core, (1,)).astype(jnp.int32)
        self.half = {}
        for name, shard in shards.items():
            h = shard.shape[0] // 2
            self.half[name] = lax.dynamic_slice_in_dim(shard.astype(BF), core * h, h, axis=0)
        self.mine, self.partial, self.received = {}, {}, {}

    def my_halves(self, names, from_chips):
        return [_own_slot(g, self.half[n], self.chip) for n, g in zip(names, from_chips)]

    def full_weights(self, names, mine, theirs):
        full = {}
        for n, a, b in zip(names, mine, theirs):
            if n in _COL_SHARDED:
                full[n] = _join_col_shards("join_" + n, a, b, self.core_1, C_IN_PAD if n == "gdn_w_in" else 4 * a.shape[2])
            else:
                full[n] = _full_from_halves(jnp.where(self.core == 0, jnp.stack([a, b], 0), jnp.stack([b, a], 0)))
        return full

    def first_weights(self):
        mine = self.my_halves(["ab_w_in"], _chip_exchange("gather_w_chips", [self.half["ab_w_in"]], False))
        return self.full_weights(["ab_w_in"], mine, _sibling_exchange("gather_w_sibling_first", mine))

    def late_exchanges(self):
        rows = self.half["gdn_w_in"].shape[0] // 2
        pieces = [self.half["gdn_w_in"][:rows], self.half["gdn_w_in"][rows:]]
        return [([self.half["ab_w_out"], self.half["s5_glu_w"]], "gather"), ([self.half["gdn_w_out"]], "gather"),
                ([pieces[0]], "gather"), ([pieces[1]], "gather")]

    def late_halves(self, got):
        return (self.my_halves(["ab_w_out", "s5_glu_w"], got[0]),
                self.my_halves(["gdn_w_in", "gdn_w_out"], [jnp.concatenate([got[2][0], got[3][0]], axis=1), got[1][0]]))

    def split_halves(self, grads):
        other = []
        for name, g in grads.items():
            if name in _COL_SHARDED:
                self.mine[name] = g
                other.append(_split_col_shards("split_other_" + name, g, self.half[name].shape[1], 1 - self.core_1))
                continue
            sm = g.reshape(4, g.shape[0] // 4, g.shape[1])
            h = sm.shape[1] // 2
            self.mine[name] = lax.dynamic_slice_in_dim(sm, self.core * h, h, axis=1)
            other.append(lax.dynamic_slice_in_dim(sm, (1 - self.core) * h, h, axis=1))
        return other

    def chip_partials(self, names, from_sibling):
        for name, b in zip(names, from_sibling):
            a = self.mine[name]
            if name in _COL_SHARDED:
                self.partial[name] = _split_col_shards("sum_sibling_" + name, a, b.shape[2], self.core_1, add=b, out_dtype=BF)
                continue
            flat = lambda t: t.reshape(-1, t.shape[-1])
            self.partial[name] = _pair_sum("sum_sibling_" + name, flat(a), flat(b), BF).reshape(a.shape)
        return [self.partial[n] for n in names]


def kernel(x, c, ada_w, ada_b, pre_g, post_g, rel_bias, ab_w_in, ab_w_out, s5_a_re, s5_a_im, s5_log_dt, s5_b_re, s5_b_im, s5_c_re, s5_c_im, s5_d, s5_glu_w, s5_glu_b, gdn_w_in, gdn_conv, gdn_a_log, gdn_dt_bias, gdn_norm_g, gdn_w_out, loss_target, m_ada_w, m_ada_b, m_pre_g, m_post_g, m_rel_bias, m_ab_w_in, m_ab_w_out, m_s5_a_re, m_s5_a_im, m_s5_log_dt, m_s5_b_re, m_s5_b_im, m_s5_c_re, m_s5_c_im, m_s5_d, m_s5_glu_w, m_s5_glu_b, m_gdn_w_in, m_gdn_conv, m_gdn_a_log, m_gdn_dt_bias, m_gdn_norm_g, m_gdn_w_out, v_ada_w, v_ada_b, v_pre_g, v_post_g, v_rel_bias, v_ab_w_in, v_ab_w_out, v_s5_a_re, v_s5_a_im, v_s5_log_dt, v_s5_b_re, v_s5_b_im, v_s5_c_re, v_s5_c_im, v_s5_d, v_s5_glu_w, v_s5_glu_b, v_gdn_w_in, v_gdn_conv, v_gdn_a_log, v_gdn_dt_bias, v_gdn_norm_g, v_gdn_w_out):
    w = dict(ada_w=ada_w, ada_b=ada_b, pre_g=pre_g, post_g=post_g, rel_bias=rel_bias, ab_w_in=ab_w_in, ab_w_out=ab_w_out,
             s5_a_re=s5_a_re, s5_a_im=s5_a_im, s5_log_dt=s5_log_dt, s5_b_re=s5_b_re, s5_b_im=s5_b_im, s5_c_re=s5_c_re, s5_c_im=s5_c_im,
             s5_d=s5_d, s5_glu_w=s5_glu_w, s5_glu_b=s5_glu_b, gdn_w_in=gdn_w_in, gdn_conv=gdn_conv, gdn_a_log=gdn_a_log,
             gdn_dt_bias=gdn_dt_bias, gdn_norm_g=gdn_norm_g, gdn_w_out=gdn_w_out)
    m = dict(ada_w=m_ada_w, ada_b=m_ada_b, pre_g=m_pre_g, post_g=m_post_g, rel_bias=m_rel_bias, ab_w_in=m_ab_w_in, ab_w_out=m_ab_w_out,
             s5_a_re=m_s5_a_re, s5_a_im=m_s5_a_im, s5_log_dt=m_s5_log_dt, s5_b_re=m_s5_b_re, s5_b_im=m_s5_b_im, s5_c_re=m_s5_c_re,
             s5_c_im=m_s5_c_im, s5_d=m_s5_d, s5_glu_w=m_s5_glu_w, s5_glu_b=m_s5_glu_b, gdn_w_in=m_gdn_w_in, gdn_conv=m_gdn_conv,
             gdn_a_log=m_gdn_a_log, gdn_dt_bias=m_gdn_dt_bias, gdn_norm_g=m_gdn_norm_g, gdn_w_out=m_gdn_w_out)
    v = dict(ada_w=v_ada_w, ada_b=v_ada_b, pre_g=v_pre_g, post_g=v_post_g, rel_bias=v_rel_bias, ab_w_in=v_ab_w_in, ab_w_out=v_ab_w_out,
             s5_a_re=v_s5_a_re, s5_a_im=v_s5_a_im, s5_log_dt=v_s5_log_dt, s5_b_re=v_s5_b_re, s5_b_im=v_s5_b_im, s5_c_re=v_s5_c_re,
             s5_c_im=v_s5_c_im, s5_d=v_s5_d, s5_glu_w=v_s5_glu_w, s5_glu_b=v_s5_glu_b, gdn_w_in=v_gdn_w_in, gdn_conv=v_gdn_conv,
             gdn_a_log=v_gdn_a_log, gdn_dt_bias=v_gdn_dt_bias, gdn_norm_g=v_gdn_norm_g, gdn_w_out=v_gdn_w_out)
    ix, iy, ic = _place()
    me = 4 * ix + 2 * iy + ic
    chip = 2 * ix + iy
    n_cols = ada_w.shape[2]

    mine_first = _pack([c, gdn_conv], 32)
    first = _own_slot(_all_gather8("gather_c_conv", mine_first), mine_first, me)
    c_all = first[:, 0:8].reshape(8, D_MODEL)
    conv_full = first[0::2, 8:32].reshape(4, C_CONV, n_cols).transpose(1, 0, 2).reshape(C_CONV, 4 * n_cols)
    mine_mod = _mod_local(c_all, ada_w)
    modl = _own_slot(_all_gather8("gather_mod", mine_mod), mine_mod, me)
    mod = lax.dynamic_index_in_dim(modl[0::2], me, axis=2, keepdims=False)
    mod = mod.transpose(1, 0, 2).reshape(2, 4 * n_cols) + ada_b

    comm = _WeightExchanges({name: w[name][0] for name in _SHARDED}, ic, chip)
    wd = {name: w[name] for name in _SMALL if name != "ada_b"}
    wd = {k: (a if k in ("pre_g", "post_g", "rel_bias") else a[0]) for k, a in wd.items()}
    wd["gdn_conv"] = conv_full
    wd.update(comm.first_weights())

    loss_local, grad_x, grads, dmod = _local_step(x[0], loss_target[0], mod, wd, comm)

    small_shapes = [w[name].shape for name in _SMALL] + [(C_CONV, 4 * n_cols)]
    small_rows = -(-sum(_rows128(int(np.prod(s))) for s in small_shapes) // 64) * 64
    per_dev, dmod_rows = small_rows // 8, _rows128(2 * 3 * D_MODEL)
    partial = _pack([dmod] + [grads[name] for name in _SMALL[1:]] + [grads["gdn_conv"]], small_rows)
    to_all = jnp.concatenate([partial[:dmod_rows], jnp.full((8, 128), loss_local, F32)], axis=0)
    outbound = jnp.concatenate([partial.reshape(8, per_dev, 128), jnp.broadcast_to(to_all[None], (8,) + to_all.shape)], axis=1)
    inbound = _own_slot(_all_to_all8("reduce_small_grads", outbound), lax.dynamic_index_in_dim(outbound, me, 0, keepdims=False), me)
    loss = functools.reduce(lambda a, b: a + b, [inbound[d, per_dev + dmod_rows, 0] for d in range(8)])
    my_rows = _slot_sum("sum_small_grads", inbound[:, :per_dev])
    g_small = _own_slot(_all_gather8("gather_small_grads", my_rows), my_rows, me).reshape(small_rows, 128)
    g_list = _unpack(g_small, small_shapes)
    out_g, out_d, out_m, out_v = {}, {}, {}, {}

    small =list(_SMALL) + ["gdn_conv"]
    small_g = [g.reshape(-1, g.shape[-1]) for g in g_list[:-1]] + [lax.dynamic_slice_in_dim(g_list[-1], chip * n_cols, n_cols, axis=1)]
    two_d = lambda a: a.reshape(-1, a.shape[-1])
    results = _adamw_many("adamw_small", [two_d(w[n]) for n in small], small_g, [two_d(m[n]) for n in small], [two_d(v[n]) for n in small])
    for name, g2d, (d_, m_, v_) in zip(small, small_g, results):
        out_g[name], out_d[name], out_m[name], out_v[name] = (a.reshape(w[name].shape) for a in (g2d, d_, m_, v_))

    from_sibling = _sibling_exchange("reduce_sibling", comm.split_halves({"ab_w_in": grads["ab_w_in"]}))
    rider = (comm.chip_partials(["ab_w_in"], from_sibling), "scatter")
    dmod_all = inbound[:, per_dev:per_dev + dmod_rows].reshape(8, 2, 4, n_cols)
    dmod_cols = lax.dynamic_index_in_dim(dmod_all, chip, axis=2, keepdims=False).transpose(1, 0, 2)
    g_ada = _ada_w_grad(c_all, dmod_cols).reshape(-1, n_cols)
    (d_, m_, v_), got = _adamw("adamw_ada_w", two_d(w["ada_w"]), g_ada, two_d(m["ada_w"]), two_d(v["ada_w"]), exchange=rider)
    out_g["ada_w"], out_d["ada_w"], out_m["ada_w"], out_v["ada_w"] = (a.reshape(w["ada_w"].shape) for a in (g_ada, d_, m_, v_))
    comm.received["ab_w_in"] = got[0]
    chip_1 = jnp.reshape(chip, (1,)).astype(jnp.int32)
    core_1 = jnp.reshape(ic, (1,)).astype(jnp.int32)
    reduced = [_chip_sum("sum_chips_" + name, comm.received[name], comm.partial[name], chip_1) for name in _SHARDED]
    for name, g_mine, g_sib in zip(_SHARDED, reduced, _sibling_exchange("reduce_share", reduced)):
        out_g[name], out_d[name], out_m[name], out_v[name] = _adamw_halves(
            "adamw_" + name, w[name], g_mine, g_sib, m[name], v[name], core_1)

    return (loss, grad_x[None], *[out_g[n] for n in _WEIGHTS], *[out_d[n] for n in _WEIGHTS],
            *[out_m[n] for n in _WEIGHTS], *[out_v[n] for n in _WEIGHTS])
```

```python
import functools
import math

import numpy as np
import jax
import jax.numpy as jnp
from jax import lax
from jax.experimental import pallas as pl
from jax.experimental.pallas import tpu as pltpu

F32 = jnp.float32
BF = jnp.bfloat16
HI = lax.Precision.HIGHEST
MESH = pl.DeviceIdType.MESH

D_MODEL = 1024
EPS = 1e-6
A_HEADS, A_HD, A_WIDTH, A_BLOCK = 8, 64, 512, 128
DILATIONS = (1, 4, 16)
N_KEYS = 128
REL_BUCKETS, REL_MAX_DIST = 32, 2048
B_WIDTH, B_GROUP, B_GROUPS, B_STATE = 512, 16, 32, 64
S5_LANES = 512
S5_TILES = 4
C_HEADS, C_DK, C_CHUNK, C_CONV = 8, 128, 64, 4
QKV = 3072
C_IN = QKV + 1024 + 2 * C_HEADS
C_IN_PAD = 4224
TM = 256
VMEM_LIMIT_BYTES = 56 * 1024 * 1024
ADAM_LR, ADAM_B1, ADAM_B2, ADAM_EPS, ADAM_WD, ADAM_STEP = 0.001, 0.9, 0.999, 1e-08, 0.01, 10
NEG = float(np.finfo(np.float32).min)


def _cp(*sem):
    return pltpu.CompilerParams(dimension_semantics=sem, vmem_limit_bytes=VMEM_LIMIT_BYTES)


def _bdot(a, b):
    return jnp.dot(a.astype(BF), b.astype(BF), preferred_element_type=F32)


def _bdot_nt(a, b):
    return lax.dot_general(a.astype(BF), b.astype(BF), (((1,), (1,)), ((), ())), preferred_element_type=F32)


def _bdot_tn(a, b):
    return lax.dot_general(a.astype(BF), b.astype(BF), (((0,), (0,)), ((), ())), preferred_element_type=F32)


def _hdot(a, b):
    return jnp.dot(a, b, precision=HI, preferred_element_type=F32)


def _bein(eq, a, b):
    return jnp.einsum(eq, a.astype(BF), b.astype(BF), preferred_element_type=F32)


def _row(tm, n):
    return pl.BlockSpec((tm, n), lambda i: (i, 0))


def _fix(shape):
    return pl.BlockSpec(shape, lambda i: (0,) * len(shape))


def _sds(*shape, dtype=F32):
    return jax.ShapeDtypeStruct(shape, dtype)


def _acc(ref, val):
    ref[...] += val


def _zero_at_first(refs, axis=0):
    @pl.when(pl.program_id(axis) == 0)
    def _():
        for r in refs:
            r[...] = jnp.zeros_like(r)


def _rms(x):
    return x * lax.rsqrt(jnp.mean(x * x, axis=-1, keepdims=True) + EPS)


def _pre_mod(x, g, scale, shift):
    return (_rms(x) * g) * (1.0 + scale) + shift


def _post_res(y, x, post_g, gate):
    return x + gate * (_rms(y) * post_g)


def _merge_gate(o1, o2, o3, l1, l2, l3, ga):
    m = jnp.maximum(jnp.maximum(l1, l2), l3)
    e1, e2, e3 = jnp.exp(l1 - m), jnp.exp(l2 - m), jnp.exp(l3 - m)
    inv = 1.0 / (e1 + e2 + e3)
    return ((e1 * inv) * o1 + (e2 * inv) * o2 + (e3 * inv) * o3) * jax.nn.silu(ga)


def _s5_gelu(ypre, u, d_skip):
    return jax.nn.gelu(ypre + d_skip * u)


def _s5_glu(yb, gl, gb):
    return yb * jax.nn.sigmoid(gl) * jax.nn.silu(gb)


def _l0_front(x, pre_g, scale, shift, w_in):
    s_len = x.shape[0]

    def body(x_ref, g_ref, sc_ref, sh_ref, w_ref, *out_refs):
        qkv_refs, (u_ref, ga_ref, gb_ref, h_ref) = out_refs[:9], out_refs[9:]
        hb = _pre_mod(x_ref[...], g_ref[...], sc_ref[...], sh_ref[...]).astype(BF)
        h_ref[...] = hb
        z = jnp.dot(hb, w_ref[...], preferred_element_type=F32)
        for a in range(3):
            piece = z[:, a * 512:(a + 1) * 512]
            for i, d in enumerate(DILATIONS):
                qkv_refs[3 * a + i][...] = _to_res(piece, d).astype(BF)
        u_ref[...] = z[:, 1536:2048]
        ga_ref[...] = z[:, 2048:2560]
        gb_ref[...] = z[:, 2560:3072]

    vec = _fix((1, D_MODEL))
    return pl.pallas_call(
        body, name="l0_front", grid=(s_len // TM,),
        in_specs=[_row(TM, D_MODEL), vec, vec, vec, _fix((D_MODEL, 3072))],
        out_specs=[_res_spec(d) for d in DILATIONS] * 3 + [_row(TM, 512)] * 3 + [_row(TM, D_MODEL)],
        out_shape=[_sds(*_res_shape(s_len, d), dtype=BF) for d in DILATIONS] * 3 + [_sds(s_len, 512)] * 3 + [_sds(s_len, D_MODEL, dtype=BF)],
        compiler_params=_cp("arbitrary"),
    )(x, pre_g, scale, shift, w_in)


def _front_bwd(name, x, pre_g, scale, shift, w_in, dres, parts, widths):
    s_len = x.shape[0]
    n_in = sum(len(p) for p in parts)
    n_cols = sum(widths)

    def body(*refs):
        x_ref, g_ref, sc_ref, sh_ref, w_ref, dres_ref = refs[:6]
        part_refs = refs[6:6 + n_in]
        dz_ref, dx_ref, dg_ref, dsc_ref, dsh_ref = refs[6 + n_in:]
        _zero_at_first([dg_ref, dsc_ref, dsh_ref])
        _, vjp = jax.vjp(_pre_mod, x_ref[...], g_ref[...], sc_ref[...], sh_ref[...])
        dh = jnp.zeros((TM, D_MODEL), F32)
        col, at = 0, 0
        for grp, width in zip(parts, widths):
            tile = lambda r: _from_res(r[...]) if len(r.shape) == 3 else r[...]
            dz = tile(part_refs[at])
            for r in part_refs[at + 1:at + len(grp)]:
                dz = dz + tile(r)
            at += len(grp)
            dzb = dz.astype(BF)
            dz_ref[:, col:col + width] = dzb
            dh = dh + lax.dot_general(dzb, w_ref[:, col:col + width], (((1,), (1,)), ((), ())), preferred_element_type=F32)
            col += width
        dx, dg, dsc, dsh = vjp(dh)
        dx_ref[...] = dx + dres_ref[...]
        _acc(dg_ref, dg)
        _acc(dsc_ref, dsc)
        _acc(dsh_ref, dsh)

    vec = _fix((1, D_MODEL))
    flat = [a for p in parts for a in p]
    return pl.pallas_call(
        body, name=name, grid=(s_len // TM,),
        in_specs=[_row(TM, D_MODEL), vec, vec, vec, _fix((D_MODEL, n_cols)), _row(TM, D_MODEL)]
        + [_res_spec(a.shape[0], a.shape[2]) if a.ndim == 3 else _row(TM, a.shape[1]) for a in flat],
        out_specs=[_row(TM, n_cols), _row(TM, D_MODEL), vec, vec, vec],
        out_shape=[_sds(s_len, n_cols, dtype=BF), _sds(s_len, D_MODEL), _sds(1, D_MODEL), _sds(1, D_MODEL), _sds(1, D_MODEL)],
        compiler_params=_cp("arbitrary"),
    )(x, pre_g, scale, shift, w_in, dres, *flat)


def _matmul_tn(name, a, b, tn):
    s_len, k_dim = a.shape
    n_dim = b.shape[1]
    ts = 2048

    def body(a_ref, b_ref, o_ref):
        _zero_at_first([o_ref], axis=1)
        o_ref[...] += lax.dot_general(a_ref[...], b_ref[...], (((0,), (0,)), ((), ())), preferred_element_type=F32)

    return pl.pallas_call(
        body, name=name, grid=(n_dim // tn, s_len // ts),
        in_specs=[pl.BlockSpec((ts, k_dim), lambda j, i: (i, 0)), pl.BlockSpec((ts, tn), lambda j, i: (i, j))],
        out_specs=pl.BlockSpec((k_dim, tn), lambda j, i: (0, j)),
        out_shape=_sds(k_dim, n_dim),
        compiler_params=_cp("arbitrary", "arbitrary"),
    )(a, b)


def _matmul_nn(name, at, b, tn):
    k_dim, s_len = at.shape
    n_dim = b.shape[1]
    ts = 2048

    def body(a_ref, b_ref, o_ref):
        _zero_at_first([o_ref], axis=1)
        o_ref[...] += jnp.dot(a_ref[...], b_ref[...], preferred_element_type=F32)

    return pl.pallas_call(
        body, name=name, grid=(n_dim // tn, s_len // ts),
        in_specs=[pl.BlockSpec((k_dim, ts), lambda j, i: (0, i)), pl.BlockSpec((ts, tn), lambda j, i: (i, j))],
        out_specs=pl.BlockSpec((k_dim, tn), lambda j, i: (0, j)),
        out_shape=_sds(k_dim, n_dim),
        compiler_params=_cp("arbitrary", "arbitrary"),
    )(at, b)


def _t5_bucket_np(dist):
    dist = np.maximum(dist, 0)
    max_exact = REL_BUCKETS // 2
    large = max_exact + (np.log(np.maximum(dist, 1) / max_exact)
                         / math.log(REL_MAX_DIST / max_exact) * (REL_BUCKETS - max_exact)).astype(np.int32)
    large = np.minimum(large, REL_BUCKETS - 1)
    return np.where(dist < max_exact, dist, large).astype(np.int32)


def _to_res(z, dil):
    if dil == 1:
        return z[None]
    return jnp.swapaxes(z.reshape(z.shape[0] // dil, dil, z.shape[1]), 0, 1)


def _from_res(z):
    if z.shape[0] == 1:
        return z[0]
    return jnp.swapaxes(z, 0, 1).reshape(z.shape[0] * z.shape[1], z.shape[2])


def _res_shape(s_len, dil, width=A_WIDTH):
    return (dil, s_len // dil, width)


def _res_spec(dil, width=A_WIDTH):
    return pl.BlockSpec((dil, TM // dil, width), lambda i: (0, i, 0))


def _bucket_table():
    qi = np.arange(A_BLOCK)[:, None]
    kj = np.arange(2 * A_BLOCK)[None, :]
    return np.stack([_t5_bucket_np((qi + A_BLOCK - kj) * d) for d in DILATIONS], 0)


def _attn_mask(first):
    qi = lax.broadcasted_iota(jnp.int32, (A_BLOCK, 2 * A_BLOCK), 0)
    kj = lax.broadcasted_iota(jnp.int32, (A_BLOCK, 2 * A_BLOCK), 1)
    rel = qi + A_BLOCK - kj
    return (rel >= 0) & (rel <= N_KEYS) & (jnp.logical_not(first) | (kj >= A_BLOCK))


def _attn_specs(nb, rev):
    per = 2 if nb % 2 == 0 else 1
    steps = nb // per
    n_of = (lambda i: steps - 1 - i) if rev else (lambda i: i)
    cur = pl.BlockSpec((None, per * A_BLOCK, A_WIDTH), lambda r, i: (r, n_of(i), 0))
    prev = pl.BlockSpec((None, A_BLOCK, A_WIDTH), lambda r, i: (r, jnp.maximum(per * n_of(i) - 1, 0), 0))
    bias = pl.BlockSpec((A_HEADS, A_BLOCK, 2 * A_BLOCK), lambda r, i: (0, 0, 0))
    return per, steps, cur, prev, bias


def _attn_fwd(q, k, v, bias, exchange=None):
    dil, t_len, _ = q.shape
    per, steps, cur, prev, bias_spec = _attn_specs(t_len // A_BLOCK, False)
    scale = A_HD ** -0.5

    def body(q_ref, kp_ref, kc_ref, vp_ref, vc_ref, b_ref, o_ref, l_ref):
        lane = lax.broadcasted_iota(jnp.int32, (1, 128), 1)
        for sub in range(per):
            rows = slice(sub * A_BLOCK, (sub + 1) * A_BLOCK)
            before = slice((sub - 1) * A_BLOCK, sub * A_BLOCK)
            mask = _attn_mask((pl.program_id(1) == 0) if sub == 0 else False)
            for hp in range(A_HEADS // 2):
                sl = slice(hp * 128, (hp + 1) * 128)
                qp = q_ref[rows, sl]
                kw = jnp.concatenate([kp_ref[:, sl] if sub == 0 else kc_ref[before, sl], kc_ref[rows, sl]], axis=0).astype(BF)
                vw = jnp.concatenate([vp_ref[:, sl] if sub == 0 else vc_ref[before, sl], vc_ref[rows, sl]], axis=0).astype(BF)
                outs, lses = [], []
                for j in range(2):
                    hm = (lane < 64) if j == 0 else (lane >= 64)
                    s = _bdot_nt(jnp.where(hm, qp, 0.0), kw) * scale
                    s = jnp.where(mask, s + b_ref[2 * hp + j], NEG)
                    m = jnp.max(s, axis=-1, keepdims=True)
                    p = jnp.exp(s - m)
                    den = jnp.sum(p, axis=-1, keepdims=True)
                    outs.append(_bdot(p, vw) / den)
                    lses.append(m + jnp.log(den))
                hm0 = lane < 64
                o_ref[rows, sl] = jnp.where(hm0, outs[0], outs[1])
                l_ref[rows, sl] = jnp.where(hm0, lses[0], lses[1])

    return _call_with_exchange(body, f"attn_fwd_d{dil}", (dil, steps), [cur, prev, cur, prev, cur, bias_spec], [cur, cur],
                               [_sds(dil, t_len, A_WIDTH)] * 2, [], (q, k, k, v, v, bias), exchange)


def _attn_bwd(q, k, v, bias, o, l, do, dl, exchange=None):
    dil, t_len, _ = q.shape
    per, steps, cur, prev, bias_spec = _attn_specs(t_len // A_BLOCK, True)
    scale = A_HD ** -0.5

    def body(q_ref, kp_ref, kc_ref, vp_ref, vc_ref, b_ref, o_ref, l_ref, do_ref, dl_ref,
             dq_ref, dk_ref, dv_ref, db_ref, ck_ref, cv_ref):
        _zero_at_first([ck_ref, cv_ref], axis=1)

        @pl.when((pl.program_id(0) == 0) & (pl.program_id(1) == 0))
        def _():
            db_ref[...] = jnp.zeros_like(db_ref)

        lane = lax.broadcasted_iota(jnp.int32, (1, 128), 1)
        for hp in range(A_HEADS // 2):
            sl = slice(hp * 128, (hp + 1) * 128)
            to_prev_k, to_prev_v = ck_ref[:, sl], cv_ref[:, sl]
            for sub in range(per - 1, -1, -1):
                rows = slice(sub * A_BLOCK, (sub + 1) * A_BLOCK)
                before = slice((sub - 1) * A_BLOCK, sub * A_BLOCK)
                mask = _attn_mask((pl.program_id(1) == steps - 1) if sub == 0 else False)
                qp = q_ref[rows, sl]
                kw = jnp.concatenate([kp_ref[:, sl] if sub == 0 else kc_ref[before, sl], kc_ref[rows, sl]], axis=0).astype(BF)
                vw = jnp.concatenate([vp_ref[:, sl] if sub == 0 else vc_ref[before, sl], vc_ref[rows, sl]], axis=0).astype(BF)
                op, lp, dop, dlp = o_ref[rows, sl], l_ref[rows, sl], do_ref[rows, sl], dl_ref[rows, sl]
                dq_acc = jnp.zeros((A_BLOCK, 128), F32)
                dk_acc = jnp.zeros((2 * A_BLOCK, 128), F32)
                dv_acc = jnp.zeros((2 * A_BLOCK, 128), F32)
                for j in range(2):
                    hm = (lane < 64) if j == 0 else (lane >= 64)
                    qm = jnp.where(hm, qp, 0.0)
                    s = _bdot_nt(qm, kw) * scale
                    s = jnp.where(mask, s + b_ref[2 * hp + j], NEG)
                    lse = jnp.max(jnp.where(hm, lp, NEG), axis=-1, keepdims=True)
                    p = jnp.exp(s - lse)
                    do_h = jnp.where(hm, dop, 0.0)
                    dd = jnp.sum(do_h * op, axis=-1, keepdims=True)
                    dlse = jnp.sum(jnp.where(hm, dlp, 0.0), axis=-1, keepdims=True)
                    ds = p * (_bdot_nt(do_h, vw) - dd + dlse)
                    dv_acc = dv_acc + _bdot_tn(p, do_h)
                    dq_acc = dq_acc + jnp.where(hm, _bdot(ds, kw), 0.0) * scale
                    dk_acc = dk_acc + _bdot_tn(ds, qm) * scale
                    db_ref[2 * hp + j] += ds
                dq_ref[rows, sl] = dq_acc
                dk_ref[rows, sl] = dk_acc[A_BLOCK:] + to_prev_k
                dv_ref[rows, sl] = dv_acc[A_BLOCK:] + to_prev_v
                to_prev_k, to_prev_v = dk_acc[:A_BLOCK], dv_acc[:A_BLOCK]
            ck_ref[:, sl] = to_prev_k
            cv_ref[:, sl] = to_prev_v

    return _call_with_exchange(
        body, f"attn_bwd_d{dil}", (dil, steps), [cur, prev, cur, prev, cur, bias_spec, cur, cur, cur, cur],
        [cur, cur, cur, bias_spec], [_sds(dil, t_len, A_WIDTH)] * 3 + [_sds(A_HEADS, A_BLOCK, 2 * A_BLOCK)],
        [pltpu.VMEM((A_BLOCK, A_WIDTH), F32)] * 2, (q, k, k, v, v, bias, o, l, do, dl), exchange)


def _attn_bias(rel_bias, table):
    def body(rb_ref, t_ref, *o_refs):
        for c in range(3):
            t = t_ref[c]
            acc = [jnp.zeros((A_BLOCK, 2 * A_BLOCK), F32) for _ in range(A_HEADS)]
            for b in range(REL_BUCKETS):
                hit = t == b
                acc = [jnp.where(hit, rb_ref[b, h], acc[h]) for h in range(A_HEADS)]
            for h in range(A_HEADS):
                o_refs[c][h] = acc[h]

    return pl.pallas_call(body, name="attn_bias", out_shape=[_sds(A_HEADS, A_BLOCK, 2 * A_BLOCK)] * 3,
                          in_specs=[pl.BlockSpec(memory_space=pltpu.SMEM), pl.BlockSpec(memory_space=pltpu.VMEM)],
                          compiler_params=pltpu.CompilerParams(vmem_limit_bytes=VMEM_LIMIT_BYTES))(rel_bias, table)


def _rel_bias_grad(dbs, idx_rows):
    n = A_BLOCK * 2 * A_BLOCK

    def body(d0_ref, d1_ref, d2_ref, idx_ref, o_ref):
        bucket = lax.broadcasted_iota(jnp.int32, (REL_BUCKETS, n), 0).astype(F32)
        acc = jnp.zeros((A_HEADS, REL_BUCKETS), F32)
        for c, db_ref in enumerate((d0_ref, d1_ref, d2_ref)):
            onehot = (idx_ref[c:c + 1, :] == bucket).astype(F32)
            acc = acc + lax.dot_general(db_ref[...], onehot, (((1,), (1,)), ((), ())), precision=HI, preferred_element_type=F32)
        o_ref[...] = acc

    return pl.pallas_call(body, name="rel_bias_grad", out_shape=_sds(A_HEADS, REL_BUCKETS),
                          compiler_params=pltpu.CompilerParams(vmem_limit_bytes=VMEM_LIMIT_BYTES))(
                              *[d.reshape(A_HEADS, n) for d in dbs], idx_rows)


def _s5_param_fn(a_re, a_im, log_dt, bt_re, bt_im):
    dt = jnp.exp(log_dt)
    mag = jnp.exp(dt * a_re)
    abar_r, abar_i = mag * jnp.cos(dt * a_im), mag * jnp.sin(dt * a_im)
    den = a_re * a_re + a_im * a_im
    fr = ((abar_r - 1.0) * a_re + abar_i * a_im) / den
    fi = (abar_i * a_re - (abar_r - 1.0) * a_im) / den
    row = lax.broadcasted_iota(jnp.int32, (B_WIDTH, B_GROUPS), 0)
    grp = lax.broadcasted_iota(jnp.int32, (B_WIDTH, B_GROUPS), 1)
    expand = ((row // B_GROUP) == grp).astype(F32)
    fr_e, fi_e = _hdot(expand, fr), _hdot(expand, fi)
    return abar_r, abar_i, fr_e * bt_re - fi_e * bt_im, fr_e * bt_im + fi_e * bt_re


def _s5_params(a_re, a_im, log_dt, bt_re, bt_im):
    def body(ar, ai, ld, br, bi, o1, o2, o3, o4):
        o1[...], o2[...], o3[...], o4[...] = _s5_param_fn(ar[...], ai[...], ld[...], br[...], bi[...])

    return pl.pallas_call(body, name="s5_params",
                          out_shape=[_sds(B_GROUPS, B_STATE)] * 2 + [_sds(B_WIDTH, B_STATE)] * 2)(a_re, a_im, log_dt, bt_re, bt_im)


def _s5_params_bwd(a_re, a_im, log_dt, bt_re, bt_im, d1, d2, d3, d4):
    def body(ar, ai, ld, br, bi, c1, c2, c3, c4, o1, o2, o3, o4, o5):
        _, vjp = jax.vjp(_s5_param_fn, ar[...], ai[...], ld[...], br[...], bi[...])
        o1[...], o2[...], o3[...], o4[...], o5[...] = vjp((c1[...], c2[...], c3[...], c4[...]))

    return pl.pallas_call(body, name="s5_params_bwd",
                          out_shape=[_sds(B_GROUPS, B_STATE)] * 2 + [_sds(B_GROUPS, 1)] + [_sds(B_WIDTH, B_STATE)] * 2,
                          )(a_re, a_im, log_dt, bt_re, bt_im, d1, d2, d3, d4)


def _pick_row(x, r):
    rows = lax.broadcasted_iota(jnp.int32, x.shape, 0)
    return jnp.sum(jnp.where(rows == r, x, 0.0), axis=0, keepdims=True)


S5_SEG = 8
S5_STEPS = 32
S5_WIDTH = S5_TILES * S5_LANES


def _seg_rows(block):
    return jnp.swapaxes(block, 0, 1).reshape(block.shape[1] * S5_SEG, block.shape[2])


def _seg_block(rows):
    return jnp.swapaxes(rows.reshape(rows.shape[0] // S5_SEG, S5_SEG, rows.shape[1]), 0, 1)


def _seq_specs(n_i, rev):
    at = (lambda i: n_i - 1 - i) if rev else (lambda i: i)
    seg = pl.BlockSpec((S5_SEG, S5_STEPS, B_WIDTH), lambda i: (0, at(i), 0))
    x_spec = pl.BlockSpec((S5_SEG * S5_STEPS, S5_WIDTH), lambda i: (at(i), 0))
    return seg, x_spec, _fix((S5_TILES, 128, S5_LANES)), _fix((S5_TILES, S5_LANES, 128)), _fix((1, S5_WIDTH)), _fix((S5_SEG, S5_WIDTH))


def _tile_dots(dot, lhs, w_ref, lhs_width):
    return jnp.concatenate([dot(lhs[:, t * lhs_width:(t + 1) * lhs_width], w_ref[t]) for t in range(S5_TILES)], axis=1)


def _s5_entries(name, end_r, end_i, abr, abi, steps, reverse):
    def body(er_ref, ei_ref, ar_ref, ai_ref, or_ref, oi_ref):
        pr, pi_ = ar_ref[...], ai_ref[...]
        for _ in range(int(math.log2(steps))):
            pr, pi_ = pr * pr - pi_ * pi_, 2.0 * pr * pi_
        er, ei = er_ref[...], ei_ref[...]
        rows = lax.broadcasted_iota(jnp.int32, er.shape, 0)
        cr, ci = jnp.zeros_like(pr), jnp.zeros_like(pr)
        out_r, out_i = jnp.zeros_like(er), jnp.zeros_like(er)
        for g in (range(S5_SEG - 2, -1, -1) if reverse else range(1, S5_SEG)):
            src = g + 1 if reverse else g - 1
            cr, ci = _pick_row(er, src) + pr * cr - pi_ * ci, _pick_row(ei, src) + pr * ci + pi_ * cr
            out_r, out_i = jnp.where(rows == g, cr, out_r), jnp.where(rows == g, ci, out_i)
        or_ref[...] = out_r
        oi_ref[...] = out_i

    return pl.pallas_call(body, name=name, out_shape=[_sds(*end_r.shape)] * 2)(end_r, end_i, abr, abi)


def _s5_seq_fwd(u, btr, bti, ctr, cti, abr, abi, entry, store, exchange=None):
    s_len = u.shape[0]
    seg_len = s_len // S5_SEG
    n_i = seg_len // S5_STEPS
    rows = S5_SEG * S5_STEPS

    def body(u_ref, btr_ref, bti_ref, ctr_ref, cti_ref, ar_ref, ai_ref, er_ref, ei_ref, *rest):
        if store:
            xr_ref, xi_ref, y_ref, endr_ref, endi_ref, sr_ref, si_ref = rest
        else:
            endr_ref, endi_ref, sr_ref, si_ref = rest
        i = pl.program_id(0)

        @pl.when(i == 0)
        def _():
            sr_ref[...] = er_ref[...]
            si_ref[...] = ei_ref[...]

        ar = jnp.broadcast_to(ar_ref[...], (S5_SEG, S5_WIDTH))
        ai = jnp.broadcast_to(ai_ref[...], (S5_SEG, S5_WIDTH))
        ub = _seg_rows(u_ref[...])
        br, bi = _tile_dots(_bdot, ub, btr_ref, 128), _tile_dots(_bdot, ub, bti_ref, 128)
        sr, si = sr_ref[...], si_ref[...]
        for s in range(S5_STEPS):
            at = slice(S5_SEG * s, S5_SEG * (s + 1))
            sr, si = ar * sr - ai * si + br[at], ar * si + ai * sr + bi[at]
            if store:
                xr_ref[at, :] = sr
                xi_ref[at, :] = si
        sr_ref[...] = sr
        si_ref[...] = si
        if store:
            y_ref[...] = _seg_block(_tile_dots(_bdot, xr_ref[...], ctr_ref, S5_LANES) - _tile_dots(_bdot, xi_ref[...], cti_ref, S5_LANES))

        @pl.when(i == n_i - 1)
        def _():
            endr_ref[...] = sr
            endi_ref[...] = si

    seg, x_spec, b_spec, c_spec, a_spec, e_spec = _seq_specs(n_i, False)
    ends = [_sds(S5_SEG, S5_WIDTH)] * 2
    full = [_sds(s_len, S5_WIDTH)] * 2 + [_sds(S5_SEG, seg_len, B_WIDTH)] if store else []
    return _call_with_exchange(
        body, "s5_scan_fwd" if store else "s5_ends_fwd", (n_i,),
        [seg, b_spec, b_spec, c_spec, c_spec, a_spec, a_spec, e_spec, e_spec],
        ([x_spec, x_spec, seg] if store else []) + [e_spec, e_spec], full + ends,
        [pltpu.VMEM((S5_SEG, S5_WIDTH), F32)] * 2,
        (u.reshape(S5_SEG, seg_len, B_WIDTH), btr, bti, ctr, cti, abr, abi, *entry), exchange)


def _s5_seq_bwd(dy, xr, xi, u, btr, bti, ctr, cti, abr, abi, g_entry, x_entry, full, exchange=None):
    s_len = dy.shape[0]
    seg_len = s_len // S5_SEG
    n_i = seg_len // S5_STEPS
    rows = S5_SEG * S5_STEPS

    def body(*refs):
        if full:
            (dy_ref, btr_ref, bti_ref, ctr_ref, cti_ref, ar_ref, ai_ref, ger_ref, gei_ref,
             xr_ref, xi_ref, xrp_ref, xip_ref, xer_ref, xei_ref, u_ref,
             du_ref, dbtr_ref, dbti_ref, dctr_ref, dcti_ref, dar_ref, dai_ref, str_ref, sti_ref,
             sr_ref, si_ref, gr_s, gi_s) = refs
        else:
            (dy_ref, btr_ref, bti_ref, ctr_ref, cti_ref, ar_ref, ai_ref, ger_ref, gei_ref, str_ref, sti_ref, sr_ref, si_ref) = refs
        i = pl.program_id(0)

        @pl.when(i == 0)
        def _():
            sr_ref[...] = ger_ref[...]
            si_ref[...] = gei_ref[...]
            if full:
                for r in (dbtr_ref, dbti_ref, dctr_ref, dcti_ref, dar_ref, dai_ref):
                    r[...] = jnp.zeros_like(r)

        ar = jnp.broadcast_to(ar_ref[...], (S5_SEG, S5_WIDTH))
        ai = -jnp.broadcast_to(ai_ref[...], (S5_SEG, S5_WIDTH))
        dyb = _seg_rows(dy_ref[...])
        gr, gi = _tile_dots(_bdot_nt, dyb, ctr_ref, 128), -_tile_dots(_bdot_nt, dyb, cti_ref, 128)
        sr, si = sr_ref[...], si_ref[...]
        for s in range(S5_STEPS - 1, -1, -1):
            at = slice(S5_SEG * s, S5_SEG * (s + 1))
            sr, si = ar * sr - ai * si + gr[at], ar * si + ai * sr + gi[at]
            if full:
                gr_s[at, :] = sr
                gi_s[at, :] = si
        sr_ref[...] = sr
        si_ref[...] = si

        @pl.when(i == n_i - 1)
        def _():
            str_ref[...] = sr
            sti_ref[...] = si

        if full:
            g_r, g_i = gr_s[...], gi_s[...]
            du_ref[...] = _seg_block(_tile_dots(_bdot_nt, g_r, btr_ref, S5_LANES) + _tile_dots(_bdot_nt, g_i, bti_ref, S5_LANES))
            ub = _seg_rows(u_ref[...])
            xr_b, xi_b = xr_ref[...], xi_ref[...]
            for t in range(S5_TILES):
                lanes, cols = slice(t * S5_LANES, (t + 1) * S5_LANES), slice(t * 128, (t + 1) * 128)
                dbtr_ref[t] += _bdot_tn(ub[:, cols], g_r[:, lanes])
                dbti_ref[t] += _bdot_tn(ub[:, cols], g_i[:, lanes])
                dctr_ref[t] += _bdot_tn(xr_b[:, lanes], dyb[:, cols])
                dcti_ref[t] -= _bdot_tn(xi_b[:, lanes], dyb[:, cols])
            first = i == n_i - 1
            xpr = jnp.concatenate([jnp.where(first, xer_ref[...], xrp_ref[...]), xr_b[:rows - S5_SEG]], axis=0)
            xpi = jnp.concatenate([jnp.where(first, xei_ref[...], xip_ref[...]), xi_b[:rows - S5_SEG]], axis=0)
            dar_ref[...] += jnp.sum(g_r * xpr + g_i * xpi, axis=0, keepdims=True)
            dai_ref[...] += jnp.sum(g_i * xpr - g_r * xpi, axis=0, keepdims=True)

    seg, x_spec, b_spec, c_spec, a_spec, e_spec = _seq_specs(n_i, True)
    halo = pl.BlockSpec((S5_SEG, S5_WIDTH), lambda i: (jnp.maximum((n_i - 1 - i) * S5_STEPS - 1, 0), 0))
    starts = [_sds(S5_SEG, S5_WIDTH)] * 2
    in_specs = [seg, b_spec, b_spec, c_spec, c_spec, a_spec, a_spec, e_spec, e_spec]
    args = [dy.reshape(S5_SEG, seg_len, B_WIDTH), btr, bti, ctr, cti, abr, abi, *g_entry]
    state = [pltpu.VMEM((S5_SEG, S5_WIDTH), F32)] * 2
    if not full:
        return _call_with_exchange(body, "s5_starts_bwd", (n_i,), in_specs, [e_spec, e_spec], starts, state, args, None)
    return _call_with_exchange(
        body, "s5_scan_bwd", (n_i,),
        in_specs + [x_spec, x_spec, halo, halo, e_spec, e_spec, seg],
        [seg, b_spec, b_spec, c_spec, c_spec, a_spec, a_spec, e_spec, e_spec],
        [_sds(S5_SEG, seg_len, B_WIDTH)] + [_sds(S5_TILES, 128, S5_LANES)] * 2 + [_sds(S5_TILES, S5_LANES, 128)] * 2
        + [_sds(1, S5_WIDTH)] * 2 + starts,
        state + [pltpu.VMEM((rows, S5_WIDTH), F32)] * 2,
        args + [xr, xi, xr, xi, *x_entry, u.reshape(S5_SEG, seg_len, B_WIDTH)], exchange)


def _blockdiag_b(bbar_t):
    blocks = bbar_t.reshape(S5_TILES, 8, B_GROUP, B_STATE)
    return jnp.einsum('jgmp,gh->jgmhp', blocks, jnp.eye(8, dtype=F32)).reshape(S5_TILES, 128, S5_LANES)


def _blockdiag_b_t(d):
    return jnp.einsum('jgmgp->jgmp', d.reshape(S5_TILES, 8, B_GROUP, 8, B_STATE)).reshape(B_WIDTH, B_STATE)


def _blockdiag_c(c):
    blocks = c.reshape(S5_TILES, 8, B_GROUP, B_STATE)
    return jnp.einsum('jgmp,gh->jhpgm', blocks, jnp.eye(8, dtype=F32)).reshape(S5_TILES, S5_LANES, 128)


def _blockdiag_c_t(d):
    return jnp.einsum('jgpgm->jgmp', d.reshape(S5_TILES, 8, B_STATE, 8, B_GROUP)).reshape(B_GROUPS, B_GROUP, B_STATE)


def _l0_out(os, ls, ga, gb, ypre, u, x, d_skip, glu_w, glu_b, w_out, post_g, gate, exchange=None):
    s_len = x.shape[0]

    def body(o0, o1, o2, l0, l1, l2, ga_ref, gb_ref, yp_ref, u_ref, x_ref, d_ref, gw_ref, gbias_ref, w_ref, pg_ref, gt_ref, x1_ref, y_ref):
        oa = _merge_gate(*[_from_res(r[...]) for r in (o0, o1, o2, l0, l1, l2)], ga_ref[...])
        yb = _s5_gelu(yp_ref[...], u_ref[...], d_ref[...])
        ob = _s5_glu(yb, _bdot(yb, gw_ref[...]) + gbias_ref[...], gb_ref[...])
        y = _bdot(oa, w_ref[0:512, :]) + _bdot(ob, w_ref[512:1024, :])
        y_ref[...] = y
        x1_ref[...] = _post_res(y, x_ref[...], pg_ref[...], gt_ref[...])

    vec, half = _fix((1, D_MODEL)), _fix((1, 512))
    return _call_with_exchange(
        body, "l0_out", (s_len // TM,),
        [_res_spec(d) for d in DILATIONS] * 2 + [_row(TM, 512)] * 4
        + [_row(TM, D_MODEL), half, _fix((512, 512)), half, _fix((D_MODEL, D_MODEL)), vec, vec],
        [_row(TM, D_MODEL)] * 2, [_sds(s_len, D_MODEL)] * 2, [],
        (*os, *ls, ga, gb, ypre, u, x, d_skip, glu_w, glu_b, w_out, post_g, gate), exchange)


def _l0_out_bwd(os, ls, ga, gb, ypre, u, x, y, d_skip, glu_w, glu_b, w_out, post_g, gate, dx1, exchange=None):
    s_len = x.shape[0]

    def body(o0, o1, o2, l0, l1, l2, ga_ref, gb_ref, yp_ref, u_ref, x_ref, y_ref, d_ref, gw_ref, gbias_ref, w_ref, pg_ref, gt_ref, dx1_ref,
             do0, do1, do2, dl0, dl1, dl2, dga_ref, dgb_ref, dyp_ref, du_ref, dd_ref, dgw_ref, dgbias_ref, dw_ref, dpg_ref, dgt_ref):
        _zero_at_first([dd_ref, dgw_ref, dgbias_ref, dw_ref, dpg_ref, dgt_ref])
        _, vjp2 = jax.vjp(_post_res, y_ref[...], x_ref[...], pg_ref[...], gt_ref[...])
        dy, _, dpg, dgt = vjp2(dx1_ref[...])
        _acc(dpg_ref, dpg)
        _acc(dgt_ref, dgt)
        oa, vjp_a = jax.vjp(_merge_gate, *[_from_res(r[...]) for r in (o0, o1, o2, l0, l1, l2)], ga_ref[...])
        yb, vjp_g = jax.vjp(_s5_gelu, yp_ref[...], u_ref[...], d_ref[...])
        gl = _bdot(yb, gw_ref[...]) + gbias_ref[...]
        ob, vjp_b = jax.vjp(_s5_glu, yb, gl, gb_ref[...])
        dw_ref[0:512, :] += _bdot_tn(oa, dy)
        dw_ref[512:1024, :] += _bdot_tn(ob, dy)
        d1, d2, d3, e1, e2, e3, dga = vjp_a(_bdot_nt(dy, w_ref[0:512, :]))
        for ref, val, d in zip((do0, do1, do2, dl0, dl1, dl2), (d1, d2, d3, e1, e2, e3), DILATIONS * 2):
            ref[...] = _to_res(val, d)
        dga_ref[...] = dga
        dyb, dgl, dgb = vjp_b(_bdot_nt(dy, w_ref[512:1024, :]))
        dgb_ref[...] = dgb
        dgw_ref[...] += _bdot_tn(yb, dgl)
        _acc(dgbias_ref, jnp.sum(dgl, axis=0, keepdims=True))
        dyp, du, dd = vjp_g(dyb + _bdot_nt(dgl, gw_ref[...]))
        dyp_ref[...] = dyp
        du_ref[...] = du
        _acc(dd_ref, dd)

    vec, half = _fix((1, D_MODEL)), _fix((1, 512))
    r5, r10 = _row(TM, 512), _row(TM, D_MODEL)
    res6 = [_res_spec(d) for d in DILATIONS] * 2
    return _call_with_exchange(
        body, "l0_out_bwd", (s_len // TM,),
        res6 + [r5] * 4 + [r10, r10, half, _fix((512, 512)), half, _fix((D_MODEL, D_MODEL)), vec, vec, r10],
        res6 + [r5] * 4 + [half, _fix((512, 512)), half, _fix((D_MODEL, D_MODEL)), vec, vec],
        [_sds(*_res_shape(s_len, d)) for d in DILATIONS] * 2 + [_sds(s_len, 512)] * 4
        + [_sds(1, 512), _sds(512, 512), _sds(1, 512), _sds(D_MODEL, D_MODEL), _sds(1, D_MODEL), _sds(1, D_MODEL)],
        [], (*os, *ls, ga, gb, ypre, u, x, y, d_skip, glu_w, glu_b, w_out, post_g, gate, dx1), exchange)


def _l1_front(x, pre_g, scale, shift, w_in):
    s_len = x.shape[0]

    def body(x_ref, g_ref, sc_ref, sh_ref, w_ref, raw_ref, gate_ref, ba_ref, ht_ref):
        hb = _pre_mod(x_ref[...], g_ref[...], sc_ref[...], sh_ref[...]).astype(BF)
        ht_ref[...] = hb.astype(F32).T.astype(BF)
        z = jnp.dot(hb, w_ref[...], preferred_element_type=F32)
        raw_ref[...] = z[:, 0:QKV]
        gate_ref[...] = z[:, QKV:QKV + 1024]
        ba_ref[...] = z[:, QKV + 1024:C_IN_PAD]

    vec = _fix((1, D_MODEL))
    return pl.pallas_call(
        body, name="l1_front", grid=(s_len // TM,),
        in_specs=[_row(TM, D_MODEL), vec, vec, vec, _fix((D_MODEL, C_IN_PAD))],
        out_specs=[_row(TM, QKV), _row(TM, 1024), _row(TM, 128), pl.BlockSpec((D_MODEL, TM), lambda i: (0, i))],
        out_shape=[_sds(s_len, QKV), _sds(s_len, 1024), _sds(s_len, 128), _sds(D_MODEL, s_len, dtype=BF)],
        compiler_params=_cp("arbitrary"),
    )(x, pre_g, scale, shift, w_in)


def _bg_fn(ba, alog_row, dtb_row):
    lane = lax.broadcasted_iota(jnp.int32, (1, 128), 1)
    g = -jnp.exp(alog_row) * jax.nn.softplus(ba + dtb_row)
    return jnp.where(lane < C_HEADS, jax.nn.sigmoid(ba), jnp.where(lane < 2 * C_HEADS, g, 0.0))


def _act_q(c):
    q = jax.nn.silu(c)
    return q * lax.rsqrt(jnp.sum(q * q, axis=-1, keepdims=True) + EPS) * (C_DK ** -0.5)


def _act_k(c):
    k = jax.nn.silu(c)
    return k * lax.rsqrt(jnp.sum(k * k, axis=-1, keepdims=True) + EPS)


def _act_of(s):
    return _act_q if s < 8 else (_act_k if s < 16 else jax.nn.silu)


def _conv_taps(prev8, tile_ref, sl, next8=None):
    rows = tile_ref.shape[0]
    head = jnp.concatenate([prev8, tile_ref[0:8, sl]], axis=0)
    tail = None if next8 is None else jnp.concatenate([tile_ref[rows - 8:rows, sl], next8], axis=0)
    taps = []
    for j in range(C_CONV):
        shift = C_CONV - 1 - j
        pieces = [head[8:] if shift == 0 else pltpu.roll(head, shift, 0)[8:], tile_ref[pl.ds(8 - shift, rows - 8), sl]]
        if tail is not None:
            pieces.append(tail[8:] if shift == 0 else pltpu.roll(tail, shift, 0)[8:])
        taps.append(jnp.concatenate(pieces, axis=0))
    return taps


def _gdn_prep(raw, ba, conv_w, alog_row, dtb_row):
    s_len = raw.shape[0]

    def body(raw_ref, halo_ref, ba_ref, w_ref, al_ref, dt_ref, qkv_ref, bg_ref):
        bg_ref[...] = _bg_fn(ba_ref[...], al_ref[...], dt_ref[...])
        has_prev = (pl.program_id(0) > 0).astype(F32)
        for s in range(24):
            sl = slice(s * 128, (s + 1) * 128)
            taps = _conv_taps(halo_ref[:, sl] * has_prev, raw_ref, sl)
            conv = w_ref[3:4, sl] * taps[3]
            for j in range(3):
                conv = conv + w_ref[j:j + 1, sl] * taps[j]
            qkv_ref[:, sl] = _act_of(s)(conv)

    halo = pl.BlockSpec((8, QKV), lambda i: (jnp.maximum(i * (TM // 8) - 1, 0), 0))
    row128 = _fix((1, 128))
    return pl.pallas_call(
        body, name="gdn_prep", grid=(s_len // TM,),
        in_specs=[_row(TM, QKV), halo, _row(TM, 128), _fix((C_CONV, QKV)), row128, row128],
        out_specs=[_row(TM, QKV), _row(TM, 128)],
        out_shape=[_sds(s_len, QKV), _sds(s_len, 128)],
        compiler_params=_cp("arbitrary"),
    )(raw, raw, ba, conv_w, alog_row, dtb_row)


def _gdn_prep_bwd(raw, ba, conv_w, alog_row, dtb_row, dq, dk, dv, dbg):
    s_len = raw.shape[0]
    n_tiles = s_len // TM

    def body(raw_ref, prev_ref, next_ref, ba_ref, w_ref, al_ref, dt_ref, dq_ref, dqn_ref, dk_ref, dkn_ref, dv_ref, dvn_ref, dbg_ref,
             draw_ref, dba_ref, dw_ref, dal_ref, ddt_ref, dconv_ref):
        _zero_at_first([dw_ref, dal_ref, ddt_ref])
        i = pl.program_id(0)
        _, vjp_bg = jax.vjp(_bg_fn, ba_ref[...], al_ref[...], dt_ref[...])
        dba, dal, ddt = vjp_bg(dbg_ref[...])
        dba_ref[...] = dba
        _acc(dal_ref, dal)
        _acc(ddt_ref, ddt)
        has_prev = (i > 0).astype(F32)
        has_next = (i < n_tiles - 1).astype(F32)
        ct_refs = ((dq_ref, dqn_ref), (dk_ref, dkn_ref), (dv_ref, dvn_ref))
        for s in range(24):
            sl = slice(s * 128, (s + 1) * 128)
            hl = slice((s % 8) * 128, (s % 8 + 1) * 128)
            tile_ref, nxt_ref = ct_refs[s // 8]
            taps = _conv_taps(prev_ref[:, sl] * has_prev, raw_ref, sl, next_ref[:, sl] * has_next)
            conv = w_ref[3:4, sl] * taps[3]
            for j in range(3):
                conv = conv + w_ref[j:j + 1, sl] * taps[j]
            ct = jnp.concatenate([tile_ref[:, hl], nxt_ref[:, hl] * has_next], axis=0)
            _, vjp_act = jax.vjp(_act_of(s), conv)
            dconv, = vjp_act(ct)
            dconv_ref[...] = dconv
            draw = w_ref[3:4, sl] * dconv[:TM]
            for j in range(3):
                draw = draw + w_ref[j:j + 1, sl] * dconv_ref[pl.ds(3 - j, TM), :]
            draw_ref[:, sl] = draw
            for j in range(4):
                dw_ref[j:j + 1, sl] += jnp.sum(dconv[:TM] * taps[j][:TM], axis=0, keepdims=True)

    prev = pl.BlockSpec((8, QKV), lambda i: (jnp.maximum(i * (TM // 8) - 1, 0), 0))
    nxt = lambda n: pl.BlockSpec((8, n), lambda i: (jnp.minimum((i + 1) * (TM // 8), s_len // 8 - 1), 0))
    row128 = _fix((1, 128))
    ct_specs = [_row(TM, 1024), nxt(1024)] * 3
    return pl.pallas_call(
        body, name="gdn_prep_bwd", grid=(n_tiles,),
        in_specs=[_row(TM, QKV), prev, nxt(QKV), _row(TM, 128), _fix((C_CONV, QKV)), row128, row128] + ct_specs + [_row(TM, 128)],
        out_specs=[_row(TM, QKV), _row(TM, 128), _fix((C_CONV, QKV)), row128, row128],
        out_shape=[_sds(s_len, QKV), _sds(s_len, 128), _sds(C_CONV, QKV), _sds(1, 128), _sds(1, 128)],
        scratch_shapes=[pltpu.VMEM((TM + 8, 128), F32)],
        compiler_params=_cp("arbitrary"),
    )(raw, raw, raw, ba, conv_w, alog_row, dtb_row, dq, dq, dk, dk, dv, dv, dbg)


def _tein(eq, a, b):
    return jnp.einsum(eq, a, b, precision=lax.Precision.HIGH, preferred_element_type=F32)


def _unit_lower_inverse(lower):
    ri = lax.broadcasted_iota(jnp.int32, (C_CHUNK, C_CHUNK), 0)
    ci = lax.broadcasted_iota(jnp.int32, (C_CHUNK, C_CHUNK), 1)
    eye = (ri == ci).astype(F32)[None]
    mm = functools.partial(_bein, 'hij,hjk->hik')
    same_block = lambda size: (ri // size == ci // size)[None]
    n_mat = jnp.where(same_block(4), -lower, 0.0)
    inv = mm(eye + n_mat, eye + mm(n_mat, n_mat))
    for size in (4, 8, 16, 32):
        below = jnp.where(same_block(2 * size) & jnp.logical_not(same_block(size)), lower, 0.0)
        inv = inv - mm(inv, mm(below, inv))
    inv = _tein('hij,hjk->hik', inv, 2.0 * eye - _tein('hij,hjk->hik', eye + lower, inv))
    return jnp.where((ri >= ci)[None], inv, 0.0)


@jax.custom_vjp
def _known_inverse(lower, inv):
    return inv


def _known_inverse_fwd(lower, inv):
    return inv, inv


def _known_inverse_bwd(inv, d_inv):
    d_lower = -_bein('hik,hjk->hij', _bein('hji,hjk->hik', inv, d_inv), inv)
    return d_lower, jnp.zeros_like(inv)


_known_inverse.defvjp(_known_inverse_fwd, _known_inverse_bwd)


def _gdn_local(q, k, v, bgs, inv_known=None):
    lane = lax.broadcasted_iota(jnp.int32, (1, 128), 1)
    ri = lax.broadcasted_iota(jnp.int32, (C_CHUNK, C_CHUNK), 0)
    ci = lax.broadcasted_iota(jnp.int32, (C_CHUNK, C_CHUNK), 1)
    row_id = lax.broadcasted_iota(jnp.int32, (128, C_CHUNK), 0)
    beta, gc, gcj = [], [], []
    for bg in bgs:
        gc_t = _hdot((ri >= ci).astype(F32), bg)
        gc_rows = gc_t.T
        for h in range(C_HEADS):
            beta.append(jnp.sum(jnp.where(lane == h, bg, 0.0), axis=-1, keepdims=True))
            gc.append(jnp.sum(jnp.where(lane == C_HEADS + h, gc_t, 0.0), axis=-1, keepdims=True))
            gcj.append(jnp.sum(jnp.where(row_id == C_HEADS + h, gc_rows, 0.0), axis=0, keepdims=True))
    beta, gc, gcj = jnp.stack(beta, axis=0), jnp.stack(gc, axis=0), jnp.stack(gcj, axis=0)
    tril, strict = (ri >= ci)[None], (ri > ci)[None]
    decay = jnp.exp(jnp.where(tril, gc - gcj, -1e30))
    kb = k * beta
    lower = jnp.where(strict, _bein('hid,hjd->hij', kb, k) * decay, 0.0)
    inv = _unit_lower_inverse(lower) if inv_known is None else _known_inverse(lower, inv_known)
    egc = jnp.exp(gc)
    u_c = _bein('hij,hjd->hid', inv, v * beta)
    w_c = _bein('hij,hjd->hid', inv, kb * egc)
    aqk = _bein('hid,hjd->hij', q, k) * decay
    rowi = lax.broadcasted_iota(jnp.int32, (1, C_CHUNK, 1), 1)
    g_last = jnp.sum(jnp.where(rowi == C_CHUNK - 1, gc, 0.0), axis=1, keepdims=True)
    kd = k * jnp.exp(g_last - gc)
    return (u_c, w_c, aqk, q * egc, kd, jnp.exp(g_last)), inv


def _gdn_state(local, state):
    u_c, w_c, aqk, qg, kd, dec = local
    v_new = u_c - _bein('hik,hkv->hiv', w_c, state)
    o = _bein('hik,hkv->hiv', qg, state) + _bein('hij,hjv->hiv', aqk, v_new)
    return o, state * dec + _bein('hik,hiv->hkv', kd, v_new)


C_SUB = 4


def _gdn_group(q, k, v, bgs, state, inv_known=None):
    local, inv = _gdn_local(q, k, v, bgs, inv_known)
    outs = []
    for s in range(len(bgs)):
        o, state = _gdn_state(tuple(t[s * C_HEADS:(s + 1) * C_HEADS] for t in local), state)
        outs.append(o)
    return outs, state, inv


def _heads(ref):
    return jnp.stack([ref[s * C_CHUNK:(s + 1) * C_CHUNK, h * C_DK:(h + 1) * C_DK] for s in range(C_SUB) for h in range(C_HEADS)], axis=0)


def _put_heads(ref, sub, val):
    rows = slice(sub * C_CHUNK, (sub + 1) * C_CHUNK)
    for h in range(C_HEADS):
        ref[rows, h * C_DK:(h + 1) * C_DK] = val[h]


def _gdn_specs(s_len, rev):
    rows = C_SUB * C_CHUNK
    n_g = s_len // rows
    at = (lambda i: n_g - 1 - i) if rev else (lambda i: i)
    col = lambda c: pl.BlockSpec((rows, 1024), lambda i: (at(i), c))
    row128 = pl.BlockSpec((rows, 128), lambda i: (at(i), 0))
    state = pl.BlockSpec((1, C_HEADS, C_DK, C_DK), lambda i: (at(i), 0, 0, 0))
    inv = pl.BlockSpec((1, C_SUB * C_HEADS, C_CHUNK, C_CHUNK), lambda i: (at(i), 0, 0, 0))
    return n_g, col, row128, state, inv


def _gdn_fwd(qkv, bg):
    s_len = qkv.shape[0]
    n_g, col, row128, state_spec, inv_spec = _gdn_specs(s_len, False)

    def body(q_ref, k_ref, v_ref, bg_ref, o_ref, ss_ref, inv_ref, st_ref):
        _zero_at_first([st_ref])
        s0 = st_ref[...]
        ss_ref[0] = s0
        bgs = [bg_ref[s * C_CHUNK:(s + 1) * C_CHUNK, :] for s in range(C_SUB)]
        outs, s2, inv = _gdn_group(_heads(q_ref), _heads(k_ref), _heads(v_ref), bgs, s0)
        st_ref[...] = s2
        inv_ref[0] = inv
        for s in range(C_SUB):
            _put_heads(o_ref, s, outs[s])

    return pl.pallas_call(
        body, name="gdn_fwd", grid=(n_g,),
        in_specs=[col(0), col(1), col(2), row128],
        out_specs=[col(0), state_spec, inv_spec],
        out_shape=[_sds(s_len, 1024), _sds(n_g, C_HEADS, C_DK, C_DK), _sds(n_g, C_SUB * C_HEADS, C_CHUNK, C_CHUNK)],
        scratch_shapes=[pltpu.VMEM((C_HEADS, C_DK, C_DK), F32)],
        compiler_params=_cp("arbitrary"),
    )(qkv, qkv, qkv, bg)


def _gdn_bwd(qkv, bg, states, invs, do):
    s_len = qkv.shape[0]
    n_g, col, row128, state_spec, inv_spec = _gdn_specs(s_len, True)

    def body(q_ref, k_ref, v_ref, bg_ref, ss_ref, inv_ref, do_ref, dq_ref, dk_ref, dv_ref, dbg_ref, ds_ref):
        _zero_at_first([ds_ref])
        inv_known = inv_ref[0]

        def group(q, k, v, bgs, st):
            outs, st2, _ = _gdn_group(q, k, v, bgs, st, inv_known)
            return outs, st2

        bgs = [bg_ref[s * C_CHUNK:(s + 1) * C_CHUNK, :] for s in range(C_SUB)]
        _, vjp = jax.vjp(group, _heads(q_ref), _heads(k_ref), _heads(v_ref), bgs, ss_ref[0])
        douts = [jnp.stack([do_ref[s * C_CHUNK:(s + 1) * C_CHUNK, h * C_DK:(h + 1) * C_DK] for h in range(C_HEADS)], axis=0)
                 for s in range(C_SUB)]
        dq, dk, dv, dbgs, ds = vjp((douts, ds_ref[...]))
        ds_ref[...] = ds
        for s in range(C_SUB):
            dbg_ref[s * C_CHUNK:(s + 1) * C_CHUNK, :] = dbgs[s]
            for ref, val in ((dq_ref, dq), (dk_ref, dk), (dv_ref, dv)):
                _put_heads(ref, s, val[s * C_HEADS:(s + 1) * C_HEADS])

    return pl.pallas_call(
        body, name="gdn_bwd", grid=(n_g,),
        in_specs=[col(0), col(1), col(2), row128, state_spec, inv_spec, col(0)],
        out_specs=[col(0), col(0), col(0), row128],
        out_shape=[_sds(s_len, 1024)] * 3 + [_sds(s_len, 128)],
        scratch_shapes=[pltpu.VMEM((C_HEADS, C_DK, C_DK), F32)],
        compiler_params=_cp("arbitrary"),
    )(qkv, qkv, qkv, bg, states, invs, do)


def _head_norm_gate(o, gate, norm_g):
    return (_rms(o) * norm_g) * jax.nn.silu(gate)


def _l1_out_fb(o, gate_c, x1, target, norm_g, w_out, post_g, gate):
    s_len = x1.shape[0]

    def body(o_ref, gc_ref, x1_ref, t_ref, ng_ref, w_ref, pg_ref, gt_ref,
             loss_ref, dres_ref, do_ref, dgc_ref, dw_ref, dng_ref, dpg_ref, dgt_ref):
        _zero_at_first([loss_ref, dw_ref, dng_ref, dpg_ref, dgt_ref])
        ng = ng_ref[...]
        ons, vjps = [], []
        for h in range(C_HEADS):
            sl = slice(h * C_DK, (h + 1) * C_DK)
            on, vjp_h = jax.vjp(_head_norm_gate, o_ref[:, sl], gc_ref[:, sl], ng)
            ons.append(on)
            vjps.append(vjp_h)
        on_all = jnp.concatenate(ons, axis=-1)
        y = _bdot(on_all, w_ref[...])
        x2, vjp2 = jax.vjp(_post_res, y, x1_ref[...], pg_ref[...], gt_ref[...])
        err = x2 - t_ref[...]
        _acc(loss_ref, jnp.full((1, 128), 0.5 * jnp.sum(jnp.mean(err * err, axis=-1)), F32))
        dx2 = err * (1.0 / D_MODEL)
        dy, _, dpg, dgt = vjp2(dx2)
        dres_ref[...] = dx2
        _acc(dpg_ref, dpg)
        _acc(dgt_ref, dgt)
        dw_ref[...] += _bdot_tn(on_all, dy)
        don = _bdot_nt(dy, w_ref[...])
        for h in range(C_HEADS):
            sl = slice(h * C_DK, (h + 1) * C_DK)
            do_h, dgc_h, dng = vjps[h](don[:, sl])
            do_ref[:, sl] = do_h
            dgc_ref[:, sl] = dgc_h
            _acc(dng_ref, dng)

    vec, r10 = _fix((1, D_MODEL)), _row(TM, D_MODEL)
    row128 = _fix((1, 128))
    return pl.pallas_call(
        body, name="l1_out_fb", grid=(s_len // TM,),
        in_specs=[r10, r10, r10, r10, row128, _fix((D_MODEL, D_MODEL)), vec, vec],
        out_specs=[row128, r10, r10, r10, _fix((D_MODEL, D_MODEL)), row128, vec, vec],
        out_shape=[_sds(1, 128), _sds(s_len, D_MODEL), _sds(s_len, D_MODEL), _sds(s_len, D_MODEL),
                   _sds(D_MODEL, D_MODEL), _sds(1, 128), _sds(1, D_MODEL), _sds(1, D_MODEL)],
        compiler_params=_cp("arbitrary"),
    )(o, gate_c, x1, target, norm_g, w_out, post_g, gate)


def _row_of(v, width, at):
    return jnp.zeros((1, width), F32).at[0, at:at + v.shape[-1]].set(v.reshape(-1))


def _local_step(x, target, mod, wd, comm=None):
    s_len = x.shape[0]
    shift0, scale0, gate0 = (mod[0:1, i * 1024:(i + 1) * 1024] for i in range(3))
    shift1, scale1, gate1 = (mod[1:2, i * 1024:(i + 1) * 1024] for i in range(3))
    pre_g0, pre_g1 = wd["pre_g"][0:1], wd["pre_g"][1:2]
    post_g0, post_g1 = wd["post_g"][0:1], wd["post_g"][1:2]
    w_in0 = wd["ab_w_in"].astype(BF)
    d_skip, glu_b = wd["s5_d"].reshape(1, 512), wd["s5_glu_b"].reshape(1, 512)
    norm_g = wd["gdn_norm_g"].reshape(1, 128)
    alog_row = _row_of(wd["gdn_a_log"], 128, C_HEADS)
    dtb_row = _row_of(wd["gdn_dt_bias"], 128, C_HEADS)
    conv_w = wd["gdn_conv"]

    a_re, a_im = wd["s5_a_re"], wd["s5_a_im"]
    log_dt = wd["s5_log_dt"].reshape(B_GROUPS, 1)
    bt_re = wd["s5_b_re"].transpose(0, 2, 1).reshape(B_WIDTH, B_STATE)
    bt_im = wd["s5_b_im"].transpose(0, 2, 1).reshape(B_WIDTH, B_STATE)
    abar_r, abar_i, bbar_r, bbar_i = _s5_params(a_re, a_im, log_dt, bt_re, bt_im)
    abr, abi = abar_r.reshape(1, -1), abar_i.reshape(1, -1)
    btr, bti = _blockdiag_b(bbar_r).astype(BF), _blockdiag_b(bbar_i).astype(BF)
    ctr, cti = _blockdiag_c(wd["s5_c_re"]).astype(BF), _blockdiag_c(wd["s5_c_im"]).astype(BF)

    table = _bucket_table()
    biases = _attn_bias(wd["rel_bias"], jnp.asarray(table))
    front = _l0_front(x, pre_g0, scale0, shift0, w_in0)
    qs, ks, vs = front[0:3], front[3:6], front[6:9]
    u, ga, gb, h0 = front[9:]
    riders = [None] * 4 if comm is None else comm.late_exchanges()
    os, ls, got = [], [], []
    for i in range(3):
        (o_d, l_d), g = _attn_fwd(qs[i], ks[i], vs[i], biases[i], exchange=riders[i])
        os.append(o_d)
        ls.append(l_d)
        got.append(g)
    seg_len = s_len // S5_SEG
    zero_state = (jnp.zeros((S5_SEG, S5_WIDTH), F32),) * 2
    ends, _ = _s5_seq_fwd(u, btr, bti, ctr, cti, abr, abi, zero_state, False)
    x_entry = _s5_entries("s5_entries_fwd", *ends, abr, abi, seg_len, False)
    (xr, xi, ypre3, _, _), g = _s5_seq_fwd(u, btr, bti, ctr, cti, abr, abi, x_entry, True, exchange=riders[3])
    got.append(g)
    ypre = ypre3.reshape(s_len, B_WIDTH)
    rider = None
    if comm is not None:
        mine0, mine1 = comm.late_halves(got)
        wd = {**wd, **comm.full_weights(["ab_w_out", "s5_glu_w"], mine0, _sibling_exchange("gather_w_sibling_l0", mine0))}
        rider = (mine1, "sibling")
    w_out0 = wd["ab_w_out"].astype(BF)
    glu_w = wd["s5_glu_w"].astype(BF)
    (x1, y0), theirs1 = _l0_out(os, ls, ga, gb, ypre, u, x, d_skip, glu_w, glu_b, w_out0, post_g0, gate0, exchange=rider)
    if comm is not None:
        wd = {**wd, **comm.full_weights(["gdn_w_in", "gdn_w_out"], mine1, theirs1)}
    w_in1 = wd["gdn_w_in"]
    if w_in1.shape[1] == C_IN:
        w_in1 = jnp.concatenate([w_in1, jnp.zeros((D_MODEL, C_IN_PAD - C_IN), w_in1.dtype)], axis=1)
    w_in1 = w_in1.astype(BF)
    w_out1 = wd["gdn_w_out"].astype(BF)

    raw, gate_c, ba, h1_t = _l1_front(x1, pre_g1, scale1, shift1, w_in1)
    qkv, bg = _gdn_prep(raw, ba, conv_w, alog_row, dtb_row)
    o_gdn, states, invs = _gdn_fwd(qkv, bg)
    loss_row, dres1, do_gdn, dgate_c, dw_out1, dnorm_g, dpost_g1, dgate1 = _l1_out_fb(
        o_gdn, gate_c, x1, target, norm_g, w_out1, post_g1, gate1)

    dq1, dk1, dv1, dbg = _gdn_bwd(qkv, bg, states, invs, do_gdn)
    draw, dba, dconv_w, dalog_row, ddtb_row = _gdn_prep_bwd(raw, ba, conv_w, alog_row, dtb_row, dq1, dk1, dv1, dbg)
    dz1, dx1, dpre_g1, dscale1, dshift1 = _front_bwd(
        "l1_front_bwd", x1, pre_g1, scale1, shift1, w_in1, dres1, [[draw], [dgate_c], [dba]], [QKV, 1024, 128])
    dw_in1 = _matmul_nn("l1_dw_in", h1_t, dz1, 1408)

    l1_names, l0_names = ["gdn_w_in", "gdn_w_out"], ["ab_w_out", "s5_glu_w"]
    rider = None if comm is None else (comm.split_halves({"gdn_w_in": dw_in1, "gdn_w_out": dw_out1}), "sibling")
    l0b, from_sibling1 = _l0_out_bwd(os, ls, ga, gb, ypre, u, x, y0, d_skip, glu_w, glu_b, w_out0, post_g0, gate0, dx1, exchange=rider)
    dos, dls = l0b[0:3], l0b[3:6]
    dga, dgb, dypre, du_skip, dd_skip, dglu_w, dglu_b, dw_out0, dpost_g0, dgate0 = l0b[6:]
    rider = None if comm is None else (comm.split_halves({"ab_w_out": dw_out0, "s5_glu_w": dglu_w}), "sibling")
    starts, _ = _s5_seq_bwd(dypre, None, None, None, btr, bti, ctr, cti, abr, abi, zero_state, None, False)
    g_entry = _s5_entries("s5_entries_bwd", *starts, abr, -abi, seg_len, True)
    (du3, dbtr, dbti, dctr, dcti, dabr, dabi, _, _), from_sibling0 = _s5_seq_bwd(
        dypre, xr, xi, u, btr, bti, ctr, cti, abr, abi, g_entry, x_entry, True, exchange=rider)
    du_scan = du3.reshape(s_len, B_WIDTH)
    riders = [None] * 3
    if comm is not None:
        riders = [(comm.chip_partials(l1_names, from_sibling1), "scatter"), (comm.chip_partials(l0_names, from_sibling0), "scatter"), None]
    dqs, dks, dvs, dbs = [], [], [], []
    for i in range(3):
        (dq_d, dk_d, dv_d, db_d), got_d = _attn_bwd(qs[i], ks[i], vs[i], biases[i], os[i], ls[i], dos[i], dls[i], exchange=riders[i])
        dqs.append(dq_d)
        dks.append(dk_d)
        dvs.append(dv_d)
        dbs.append(db_d)
        if comm is not None and riders[i] is not None:
            comm.received.update(zip((l1_names, l0_names)[i], got_d))
    parts = [dqs, dks, dvs, [du_skip, du_scan], [dga], [dgb]]
    dz0, grad_x, dpre_g0, dscale0, dshift0 = _front_bwd(
        "l0_front_bwd", x, pre_g0, scale0, shift0, w_in0, dx1, parts, [512] * 6)
    dw_in0 = _matmul_tn("l0_dw_in", h0, dz0, 1536)

    idx_rows = jnp.asarray(table.reshape(3, -1), F32)
    drel = _rel_bias_grad(dbs, idx_rows).T
    da_re, da_im, dlog_dt, dbt_re, dbt_im = _s5_params_bwd(
        a_re, a_im, log_dt, bt_re, bt_im, dabr.reshape(B_GROUPS, B_STATE), dabi.reshape(B_GROUPS, B_STATE),
        _blockdiag_b_t(dbtr), _blockdiag_b_t(dbti))
    unb = lambda d: d.reshape(B_GROUPS, B_GROUP, B_STATE).transpose(0, 2, 1)
    grads = {
        "pre_g": jnp.concatenate([dpre_g0, dpre_g1], 0), "post_g": jnp.concatenate([dpost_g0, dpost_g1], 0),
        "rel_bias": drel, "ab_w_in": dw_in0, "ab_w_out": dw_out0,
        "s5_a_re": da_re, "s5_a_im": da_im, "s5_log_dt": dlog_dt.reshape(B_GROUPS),
        "s5_b_re": unb(dbt_re), "s5_b_im": unb(dbt_im),
        "s5_c_re": _blockdiag_c_t(dctr), "s5_c_im": _blockdiag_c_t(dcti),
        "s5_d": dd_skip.reshape(512), "s5_glu_w": dglu_w, "s5_glu_b": dglu_b.reshape(512),
        "gdn_w_in": dw_in1[:, :C_IN], "gdn_conv": dconv_w,
        "gdn_a_log": dalog_row[0, C_HEADS:2 * C_HEADS], "gdn_dt_bias": ddtb_row[0, C_HEADS:2 * C_HEADS],
        "gdn_norm_g": dnorm_g.reshape(128), "gdn_w_out": dw_out1,
    }
    dmod = jnp.concatenate([jnp.concatenate([dshift0, dscale0, dgate0], 1), jnp.concatenate([dshift1, dscale1, dgate1], 1)], 0)
    return loss_row[0, 0], grad_x, grads, dmod


def _place():
    return lax.axis_index("x"), lax.axis_index("y"), lax.axis_index("c")


def _flip(v, bit):
    return 1 - v if bit else v


def _hbm_call(name, body, arrs, out_shapes, n_sem):
    any_spec = pl.BlockSpec(memory_space=pl.ANY)
    return pl.pallas_call(
        body, name=name,
        in_specs=[any_spec] * len(arrs), out_specs=[any_spec] * len(out_shapes), out_shape=out_shapes,
        scratch_shapes=[pltpu.SemaphoreType.DMA((n_sem,)), pltpu.SemaphoreType.DMA((n_sem,))],
    )(*arrs)


def _own_slot(gathered, own, slot):
    idx = lax.broadcasted_iota(jnp.int32, (gathered.shape[0],) + (1,) * own.ndim, 0)
    return jnp.where(idx == slot, own[None], gathered)


def _all_gather8(name, arr):
    def body(x_ref, out_ref, send_sems, recv_sems):
        x, y, c = _place()
        me = 4 * x + 2 * y + c
        sends, recvs = [], []
        for m in range(1, 8):
            peer = (_flip(x, m & 4), _flip(y, m & 2), _flip(c, m & 1))
            sends.append(pltpu.make_async_remote_copy(x_ref, out_ref.at[me], send_sems.at[m - 1], recv_sems.at[m - 1],
                                                      device_id=peer, device_id_type=MESH))
            recvs.append(pltpu.make_async_remote_copy(x_ref, out_ref.at[4 * peer[0] + 2 * peer[1] + peer[2]], send_sems.at[m - 1],
                                                      recv_sems.at[m - 1], device_id=peer, device_id_type=MESH))
        for cp in sends:
            cp.start()
        for cp in recvs:
            cp.wait_recv()
        for cp in sends:
            cp.wait_send()

    return _hbm_call(name, body, [arr], [jax.ShapeDtypeStruct((8,) + arr.shape, arr.dtype)], 7)[0]


def _all_to_all8(name, arr):
    def body(x_ref, out_ref, send_sems, recv_sems):
        x, y, c = _place()
        me = 4 * x + 2 * y + c
        sends, recvs = [], []
        for m in range(1, 8):
            peer = (_flip(x, m & 4), _flip(y, m & 2), _flip(c, m & 1))
            peer_id = 4 * peer[0] + 2 * peer[1] + peer[2]
            sends.append(pltpu.make_async_remote_copy(x_ref.at[peer_id], out_ref.at[me], send_sems.at[m - 1], recv_sems.at[m - 1],
                                                      device_id=peer, device_id_type=MESH))
            recvs.append(pltpu.make_async_remote_copy(x_ref.at[peer_id], out_ref.at[peer_id], send_sems.at[m - 1], recv_sems.at[m - 1],
                                                      device_id=peer, device_id_type=MESH))
        for cp in sends:
            cp.start()
        for cp in recvs:
            cp.wait_recv()
        for cp in sends:
            cp.wait_send()

    return _hbm_call(name, body, [arr], [jax.ShapeDtypeStruct(arr.shape, arr.dtype)], 7)[0]


def _chip_copies(ins, outs, send_sems, recv_sems, scatter):
    x, y, c = _place()
    mine = 2 * x + y
    sends, recvs = [], []
    for a in range(len(ins)):
        for m in range(1, 4):
            px, py = _flip(x, m & 2), _flip(y, m & 1)
            k = 3 * a + m - 1
            src = ins[a].at[2 * px + py] if scatter else ins[a]
            sends.append(pltpu.make_async_remote_copy(src, outs[a].at[mine], send_sems.at[k], recv_sems.at[k],
                                                      device_id=(px, py, c), device_id_type=MESH))
            recvs.append(pltpu.make_async_remote_copy(src, outs[a].at[2 * px + py], send_sems.at[k], recv_sems.at[k],
                                                      device_id=(px, py, c), device_id_type=MESH))
    return sends, recvs


def _chip_shapes(arrs, scatter):
    return [jax.ShapeDtypeStruct(a.shape if scatter else (4,) + a.shape, a.dtype) for a in arrs]


def _chip_exchange(name, arrs, scatter):
    n = len(arrs)

    def body(*refs):
        sends, recvs = _chip_copies(refs[:n], refs[n:2 * n], refs[2 * n], refs[2 * n + 1], scatter)
        for cp in sends:
            cp.start()
        for cp in recvs:
            cp.wait_recv()
        for cp in sends:
            cp.wait_send()

    return _hbm_call(name, body, arrs, _chip_shapes(arrs, scatter), 3 * n)


def _call_with_exchange(body, name, grid, in_specs, out_specs, out_shape, scratch_shapes, args, exchange):
    if exchange is None:
        return pl.pallas_call(body, name=name, grid=grid, in_specs=in_specs, out_specs=out_specs, out_shape=out_shape,
                              scratch_shapes=scratch_shapes, compiler_params=_cp(*["arbitrary"] * len(grid)))(*args), []
    arrs, kind = exchange
    n_in, n_out, n_ex, n_scr = len(in_specs), len(out_specs), len(arrs), len(scratch_shapes)
    n_sem = n_ex if kind == "sibling" else 3 * n_ex
    ex_shapes = [jax.ShapeDtypeStruct(a.shape, a.dtype) for a in arrs] if kind == "sibling" else _chip_shapes(arrs, kind == "scatter")

    def fused(*refs):
        ins, ex_in = refs[:n_in], refs[n_in:n_in + n_ex]
        outs, ex_out = refs[n_in + n_ex:n_in + n_ex + n_out], refs[n_in + n_ex + n_out:n_in + 2 * n_ex + n_out]
        rest = refs[n_in + 2 * n_ex + n_out:]
        if kind == "sibling":
            sends = recvs = _sibling_copies(ex_in, ex_out, rest[n_scr], rest[n_scr + 1])
        else:
            sends, recvs = _chip_copies(ex_in, ex_out, rest[n_scr], rest[n_scr + 1], kind == "scatter")
        first, last = pl.program_id(0) == 0, pl.program_id(0) == grid[0] - 1
        for k in range(1, len(grid)):
            first, last = first & (pl.program_id(k) == 0), last & (pl.program_id(k) == grid[k] - 1)

        @pl.when(first)
        def _():
            for cp in sends:
                cp.start()

        body(*ins, *outs, *rest[:n_scr])

        @pl.when(last)
        def _():
            for cp in recvs:
                cp.wait_recv()
            for cp in sends:
                cp.wait_send()

    any_spec = pl.BlockSpec(memory_space=pl.ANY)
    res = pl.pallas_call(
        fused, name=name, grid=grid, in_specs=list(in_specs) + [any_spec] * n_ex, out_specs=list(out_specs) + [any_spec] * n_ex,
        out_shape=list(out_shape) + ex_shapes,
        scratch_shapes=list(scratch_shapes) + [pltpu.SemaphoreType.DMA((n_sem,))] * 2,
        compiler_params=_cp(*["arbitrary"] * len(grid)))(*args, *arrs)
    return res[:n_out], res[n_out:]


def _sibling_copies(ins, outs, send_sems, recv_sems):
    x, y, c = _place()
    return [pltpu.make_async_remote_copy(ins[a], outs[a], send_sems.at[a], recv_sems.at[a],
                                         device_id=(x, y, 1 - c), device_id_type=MESH) for a in range(len(ins))]


def _sibling_exchange(name, arrs):
    n = len(arrs)

    def body(*refs):
        copies = _sibling_copies(refs[:n], refs[n:2 * n], refs[2 * n], refs[2 * n + 1])
        for cp in copies:
            cp.start()
        for cp in copies:
            cp.wait_recv()
        for cp in copies:
            cp.wait_send()

    return _hbm_call(name, body, arrs, [jax.ShapeDtypeStruct(a.shape, a.dtype) for a in arrs], n)


def _row_tile(rows):
    for t in (256, 128, 64, 32, 16, 8):
        if rows % t == 0:
            return t
    return rows


def _pair_sum(name, a, b, out_dtype):
    rows, cols = a.shape
    tr = _row_tile(rows)

    def body(a_ref, b_ref, o_ref):
        o_ref[...] = (a_ref[...] + b_ref[...]).astype(out_dtype)

    return pl.pallas_call(body, name=name, grid=(rows // tr,), in_specs=[_row(tr, cols)] * 2, out_specs=_row(tr, cols),
                          out_shape=_sds(rows, cols, dtype=out_dtype), compiler_params=_cp("arbitrary"))(a, b)


def _chip_sum(name, recv, partial, mine):
    n, rows, cols = recv.shape
    tr = _row_tile(rows)

    def body(mine_ref, *refs):
        own = refs[n][0].astype(F32)
        acc = None
        for s in range(n):
            term = jnp.where(mine_ref[0] == s, own, refs[s][0].astype(F32))
            acc = term if acc is None else acc + term
        refs[-1][...] = acc

    def slot_spec(s):
        return pl.BlockSpec((1, tr, cols), lambda i, m: (jnp.where(m[0] == s, (s + 1) % n, s), i, 0))

    grid_spec = pltpu.PrefetchScalarGridSpec(
        num_scalar_prefetch=1, grid=(rows // tr,),
        in_specs=[slot_spec(s) for s in range(n)] + [pl.BlockSpec((1, tr, cols), lambda i, m: (m[0], i, 0))],
        out_specs=pl.BlockSpec((tr, cols), lambda i, m: (i, 0)))
    return pl.pallas_call(body, name=name, grid_spec=grid_spec, out_shape=_sds(rows, cols),
                          compiler_params=_cp("arbitrary"))(mine, *([recv] * n), partial)


def _slot_sum(name, arr):
    n, rows, cols = arr.shape
    tr = _row_tile(rows)

    def body(*refs):
        acc = refs[0][0]
        for r in refs[1:-1]:
            acc = acc + r[0]
        refs[-1][...] = acc

    specs = [pl.BlockSpec((1, tr, cols), functools.partial(lambda s, i: (s, i, 0), s)) for s in range(n)]
    return pl.pallas_call(body, name=name, grid=(rows // tr,), in_specs=specs, out_specs=_row(tr, cols),
                          out_shape=_sds(rows, cols), compiler_params=_cp("arbitrary"))(*([arr] * n))


def _adamw(name, w, g, m, v, exchange=None):
    rows, cols = w.shape
    tr = _row_tile(rows)

    def body(w_ref, g_ref, m_ref, v_ref, d_ref, nm_ref, nv_ref):
        g_ = g_ref[...]
        m_ = ADAM_B1 * m_ref[...] + (1.0 - ADAM_B1) * g_
        v_ = ADAM_B2 * v_ref[...] + (1.0 - ADAM_B2) * (g_ * g_)
        m_hat = m_ / (1.0 - ADAM_B1 ** ADAM_STEP)
        v_hat = v_ / (1.0 - ADAM_B2 ** ADAM_STEP)
        d_ref[...] = -ADAM_LR * (m_hat / (jnp.sqrt(v_hat) + ADAM_EPS) + ADAM_WD * w_ref[...])
        nm_ref[...] = m_
        nv_ref[...] = v_

    spec = _row(tr, cols)
    return _call_with_exchange(body, name, (rows // tr,), [spec] * 4, [spec] * 3, [_sds(rows, cols)] * 3, [], (w, g, m, v), exchange)


def _adamw_many(name, ws, gs, ms, vs):
    n = len(ws)

    def body(*refs):
        for i in range(n):
            w_ref, g_ref, m_ref, v_ref = (refs[k * n + i] for k in range(4))
            d_ref, nm_ref, nv_ref = (refs[(4 + k) * n + i] for k in range(3))
            g_ = g_ref[...]
            m_ = ADAM_B1 * m_ref[...] + (1.0 - ADAM_B1) * g_
            v_ = ADAM_B2 * v_ref[...] + (1.0 - ADAM_B2) * (g_ * g_)
            m_hat = m_ / (1.0 - ADAM_B1 ** ADAM_STEP)
            v_hat = v_ / (1.0 - ADAM_B2 ** ADAM_STEP)
            d_ref[...] = -ADAM_LR * (m_hat / (jnp.sqrt(v_hat) + ADAM_EPS) + ADAM_WD * w_ref[...])
            nm_ref[...] = m_
            nv_ref[...] = v_

    shapes = [_sds(*a.shape) for a in ws]
    res = pl.pallas_call(body, name=name, out_shape=shapes * 3,
                         compiler_params=pltpu.CompilerParams(vmem_limit_bytes=VMEM_LIMIT_BYTES))(*ws, *gs, *ms, *vs)
    return [(res[i], res[n + i], res[2 * n + i]) for i in range(n)]


def _adamw_halves(name, w, g_mine, g_sibling, m, v, core):
    _, rows, cols = w.shape
    half = rows // 2
    tr = _row_tile(half)
    per_half = half // tr

    def body(core_ref, w_ref, gm_ref, gs_ref, m_ref, v_ref, g_ref, d_ref, nm_ref, nv_ref):
        g_ = jnp.where(pl.program_id(0) // per_half == core_ref[0], gm_ref[...], gs_ref[...])
        m_ = ADAM_B1 * m_ref[...] + (1.0 - ADAM_B1) * g_
        v_ = ADAM_B2 * v_ref[...] + (1.0 - ADAM_B2) * (g_ * g_)
        m_hat = m_ / (1.0 - ADAM_B1 ** ADAM_STEP)
        v_hat = v_ / (1.0 - ADAM_B2 ** ADAM_STEP)
        g_ref[...] = g_
        d_ref[...] = -ADAM_LR * (m_hat / (jnp.sqrt(v_hat) + ADAM_EPS) + ADAM_WD * w_ref[...])
        nm_ref[...] = m_
        nv_ref[...] = v_

    full = pl.BlockSpec((None, tr, cols), lambda i, c: (0, i, 0))
    in_half = pl.BlockSpec((tr, cols), lambda i, c: (i % per_half, 0))
    grid_spec = pltpu.PrefetchScalarGridSpec(num_scalar_prefetch=1, grid=(rows // tr,),
                                             in_specs=[full, in_half, in_half, full, full], out_specs=[full] * 4)
    return pl.pallas_call(body, name=name, grid_spec=grid_spec, out_shape=[_sds(1, rows, cols)] * 4,
                          compiler_params=_cp("arbitrary"))(core, w, g_mine, g_sibling, m, v)


def _mod_local(c_all, ada_w):
    def body(c_ref, w_ref, o_ref):
        c_act = jax.nn.silu(c_ref[...])
        for l in range(2):
            o_ref[l] = _hdot(c_act, w_ref[l])

    return pl.pallas_call(body, name="mod_local", out_shape=_sds(2, 8, ada_w.shape[2]),
                          compiler_params=pltpu.CompilerParams(vmem_limit_bytes=VMEM_LIMIT_BYTES))(c_all, ada_w)


def _ada_w_grad(c_all, dmod_cols):
    def body(c_ref, d_ref, o_ref):
        c_act = jax.nn.silu(c_ref[...])
        for l in range(2):
            o_ref[l] = lax.dot_general(c_act, d_ref[l], (((0,), (0,)), ((), ())), precision=HI, preferred_element_type=F32)

    return pl.pallas_call(body, name="ada_w_grad", out_shape=_sds(2, D_MODEL, dmod_cols.shape[2]),
                          compiler_params=pltpu.CompilerParams(vmem_limit_bytes=VMEM_LIMIT_BYTES))(c_all, dmod_cols)


_SMALL = ("ada_b", "pre_g", "post_g", "rel_bias", "s5_a_re", "s5_a_im", "s5_log_dt", "s5_b_re", "s5_b_im", "s5_c_re", "s5_c_im",
          "s5_d", "s5_glu_b", "gdn_a_log", "gdn_dt_bias", "gdn_norm_g")
_SHARDED = ("ab_w_in", "ab_w_out", "s5_glu_w", "gdn_w_in", "gdn_w_out")
_COL_SHARDED = ("ab_w_in", "gdn_w_in")
_WEIGHTS = ("ada_w", "ada_b", "pre_g", "post_g", "rel_bias", "ab_w_in", "ab_w_out", "s5_a_re", "s5_a_im", "s5_log_dt", "s5_b_re",
            "s5_b_im", "s5_c_re", "s5_c_im", "s5_d", "s5_glu_w", "s5_glu_b", "gdn_w_in", "gdn_conv", "gdn_a_log", "gdn_dt_bias",
            "gdn_norm_g", "gdn_w_out")


def _rows128(n):
    return -(-n // 128)


def _pack(arrs, total_rows):
    pieces = []
    for a in arrs:
        flat = a.reshape(-1)
        pieces.append(jnp.pad(flat, (0, _rows128(flat.shape[0]) * 128 - flat.shape[0])).reshape(-1, 128))
    used = sum(p.shape[0] for p in pieces)
    pieces.append(jnp.zeros((total_rows - used, 128), F32))
    return jnp.concatenate(pieces, axis=0)


def _unpack(buf, shapes):
    out, at = [], 0
    for shp in shapes:
        n = int(np.prod(shp))
        out.append(buf[at:at + _rows128(n)].reshape(-1)[:n].reshape(shp))
        at += _rows128(n)
    return out


def _full_from_halves(g):
    return g.transpose(1, 0, 2, 3).reshape(8 * g.shape[2], g.shape[3])


def _join_col_shards(name, mine, theirs, core, width):
    n, h, cols = mine.shape
    tr = _row_tile(h)
    per_half = h // tr

    def body(core_ref, a_ref, b_ref, o_ref):
        def put(src_ref):
            pad = [jnp.zeros((tr, width - n * cols), o_ref.dtype)] if width > n * cols else []
            o_ref[...] = jnp.concatenate([src_ref[s] for s in range(n)] + pad, axis=1)

        is_mine = pl.program_id(0) // per_half == core_ref[0]
        pl.when(is_mine)(functools.partial(put, a_ref))
        pl.when(jnp.logical_not(is_mine))(functools.partial(put, b_ref))

    half = pl.BlockSpec((n, tr, cols), lambda i, c: (0, i % per_half, 0))
    grid_spec = pltpu.PrefetchScalarGridSpec(num_scalar_prefetch=1, grid=(2 * per_half,), in_specs=[half, half],
                                             out_specs=pl.BlockSpec((tr, width), lambda i, c: (i, 0)))
    return pl.pallas_call(body, name=name, grid_spec=grid_spec, out_shape=_sds(2 * h, width, dtype=mine.dtype),
                          compiler_params=_cp("arbitrary"))(core, mine, theirs)


def _split_col_shards(name, g, cols, half, add=None, out_dtype=F32):
    h = g.shape[0] // 2
    tr = _row_tile(h)
    per_half = h // tr

    def body(half_ref, g_ref, *refs):
        for s in range(4):
            part = g_ref[:, s * cols:(s + 1) * cols]
            refs[-1][s] = (part if add is None else part + refs[0][s]).astype(out_dtype)

    shards = pl.BlockSpec((4, tr, cols), lambda i, c: (0, i, 0))
    grid_spec = pltpu.PrefetchScalarGridSpec(
        num_scalar_prefetch=1, grid=(per_half,),
        in_specs=[pl.BlockSpec((tr, g.shape[1]), lambda i, c: (c[0] * per_half + i, 0))] + ([] if add is None else [shards]),
        out_specs=shards)
    return pl.pallas_call(body, name=name, grid_spec=grid_spec, out_shape=_sds(4, h, cols, dtype=out_dtype),
                          compiler_params=_cp("arbitrary"))(half, g, *([] if add is None else [add]))


_LATE = ("ab_w_out", "s5_glu_w", "gdn_w_in", "gdn_w_out")


class _WeightExchanges:
    def __init__(self, shards, core, chip):
        self.core, self.chip = core, chip
        self.core_1 = jnp.reshape(core, (1,)).astype(jnp.int32)
        self.half = {}
        for name, shard in shards.items():
            h = shard.shape[0] // 2
            self.half[name] = lax.dynamic_slice_in_dim(shard.astype(BF), core * h, h, axis=0)
        self.mine, self.partial, self.received = {}, {}, {}

    def my_halves(self, names, from_chips):
        return [_own_slot(g, self.half[n], self.chip) for n, g in zip(names, from_chips)]

    def full_weights(self, names, mine, theirs):
        full = {}
        for n, a, b in zip(names, mine, theirs):
            if n in _COL_SHARDED:
                full[n] = _join_col_shards("join_" + n, a, b, self.core_1, C_IN_PAD if n == "gdn_w_in" else 4 * a.shape[2])
            else:
                full[n] = _full_from_halves(jnp.where(self.core == 0, jnp.stack([a, b], 0), jnp.stack([b, a], 0)))
        return full

    def first_weights(self):
        mine = self.my_halves(["ab_w_in"], _chip_exchange("gather_w_chips", [self.half["ab_w_in"]], False))
        return self.full_weights(["ab_w_in"], mine, _sibling_exchange("gather_w_sibling_first", mine))

    def late_exchanges(self):
        rows = self.half["gdn_w_in"].shape[0] // 2
        pieces = [self.half["gdn_w_in"][:rows], self.half["gdn_w_in"][rows:]]
        return [([self.half["ab_w_out"], self.half["s5_glu_w"]], "gather"), ([self.half["gdn_w_out"]], "gather"),
                ([pieces[0]], "gather"), ([pieces[1]], "gather")]

    def late_halves(self, got):
        return (self.my_halves(["ab_w_out", "s5_glu_w"], got[0]),
                self.my_halves(["gdn_w_in", "gdn_w_out"], [jnp.concatenate([got[2][0], got[3][0]], axis=1), got[1][0]]))

    def split_halves(self, grads):
        other = []
        for name, g in grads.items():
            if name in _COL_SHARDED:
                self.mine[name] = g
                other.append(_split_col_shards("split_other_" + name, g, self.half[name].shape[1], 1 - self.core_1))
                continue
            sm = g.reshape(4, g.shape[0] // 4, g.shape[1])
            h = sm.shape[1] // 2
            self.mine[name] = lax.dynamic_slice_in_dim(sm, self.core * h, h, axis=1)
            other.append(lax.dynamic_slice_in_dim(sm, (1 - self.core) * h, h, axis=1))
        return other

    def chip_partials(self, names, from_sibling):
        for name, b in zip(names, from_sibling):
            a = self.mine[name]
            if name in _COL_SHARDED:
                self.partial[name] = _split_col_shards("sum_sibling_" + name, a, b.shape[2], self.core_1, add=b, out_dtype=BF)
                continue
            flat = lambda t: t.reshape(-1, t.shape[-1])
            self.partial[name] = _pair_sum("sum_sibling_" + name, flat(a), flat(b), BF).reshape(a.shape)
        return [self.partial[n] for n in names]


def kernel(x, c, ada_w, ada_b, pre_g, post_g, rel_bias, ab_w_in, ab_w_out, s5_a_re, s5_a_im, s5_log_dt, s5_b_re, s5_b_im, s5_c_re, s5_c_im, s5_d, s5_glu_w, s5_glu_b, gdn_w_in, gdn_conv, gdn_a_log, gdn_dt_bias, gdn_norm_g, gdn_w_out, loss_target, m_ada_w, m_ada_b, m_pre_g, m_post_g, m_rel_bias, m_ab_w_in, m_ab_w_out, m_s5_a_re, m_s5_a_im, m_s5_log_dt, m_s5_b_re, m_s5_b_im, m_s5_c_re, m_s5_c_im, m_s5_d, m_s5_glu_w, m_s5_glu_b, m_gdn_w_in, m_gdn_conv, m_gdn_a_log, m_gdn_dt_bias, m_gdn_norm_g, m_gdn_w_out, v_ada_w, v_ada_b, v_pre_g, v_post_g, v_rel_bias, v_ab_w_in, v_ab_w_out, v_s5_a_re, v_s5_a_im, v_s5_log_dt, v_s5_b_re, v_s5_b_im, v_s5_c_re, v_s5_c_im, v_s5_d, v_s5_glu_w, v_s5_glu_b, v_gdn_w_in, v_gdn_conv, v_gdn_a_log, v_gdn_dt_bias, v_gdn_norm_g, v_gdn_w_out):
    w = dict(ada_w=ada_w, ada_b=ada_b, pre_g=pre_g, post_g=post_g, rel_bias=rel_bias, ab_w_in=ab_w_in, ab_w_out=ab_w_out,
             s5_a_re=s5_a_re, s5_a_im=s5_a_im, s5_log_dt=s5_log_dt, s5_b_re=s5_b_re, s5_b_im=s5_b_im, s5_c_re=s5_c_re, s5_c_im=s5_c_im,
             s5_d=s5_d, s5_glu_w=s5_glu_w, s5_glu_b=s5_glu_b, gdn_w_in=gdn_w_in, gdn_conv=gdn_conv, gdn_a_log=gdn_a_log,
             gdn_dt_bias=gdn_dt_bias, gdn_norm_g=gdn_norm_g, gdn_w_out=gdn_w_out)
    m = dict(ada_w=m_ada_w, ada_b=m_ada_b, pre_g=m_pre_g, post_g=m_post_g, rel_bias=m_rel_bias, ab_w_in=m_ab_w_in, ab_w_out=m_ab_w_out,
             s5_a_re=m_s5_a_re, s5_a_im=m_s5_a_im, s5_log_dt=m_s5_log_dt, s5_b_re=m_s5_b_re, s5_b_im=m_s5_b_im, s5_c_re=m_s5_c_re,
             s5_c_im=m_s5_c_im, s5_d=m_s5_d, s5_glu_w=m_s5_glu_w, s5_glu_b=m_s5_glu_b, gdn_w_in=m_gdn_w_in, gdn_conv=m_gdn_conv,
             gdn_a_log=m_gdn_a_log, gdn_dt_bias=m_gdn_dt_bias, gdn_norm_g=m_gdn_norm_g, gdn_w_out=m_gdn_w_out)
    v = dict(ada_w=v_ada_w, ada_b=v_ada_b, pre_g=v_pre_g, post_g=v_post_g, rel_bias=v_rel_bias, ab_w_in=v_ab_w_in, ab_w_out=v_ab_w_out,
             s5_a_re=v_s5_a_re, s5_a_im=v_s5_a_im, s5_log_dt=v_s5_log_dt, s5_b_re=v_s5_b_re, s5_b_im=v_s5_b_im, s5_c_re=v_s5_c_re,
             s5_c_im=v_s5_c_im, s5_d=v_s5_d, s5_glu_w=v_s5_glu_w, s5_glu_b=v_s5_glu_b, gdn_w_in=v_gdn_w_in, gdn_conv=v_gdn_conv,
             gdn_a_log=v_gdn_a_log, gdn_dt_bias=v_gdn_dt_bias, gdn_norm_g=v_gdn_norm_g, gdn_w_out=v_gdn_w_out)
    ix, iy, ic = _place()
    me = 4 * ix + 2 * iy + ic
    chip = 2 * ix + iy
    n_cols = ada_w.shape[2]

    mine_first = _pack([c, gdn_conv], 32)
    first = _own_slot(_all_gather8("gather_c_conv", mine_first), mine_first, me)
    c_all = first[:, 0:8].reshape(8, D_MODEL)
    conv_full = first[0::2, 8:32].reshape(4, C_CONV, n_cols).transpose(1, 0, 2).reshape(C_CONV, 4 * n_cols)
    mine_mod = _mod_local(c_all, ada_w)
    modl = _own_slot(_all_gather8("gather_mod", mine_mod), mine_mod, me)
    mod = lax.dynamic_index_in_dim(modl[0::2], me, axis=2, keepdims=False)
    mod = mod.transpose(1, 0, 2).reshape(2, 4 * n_cols) + ada_b

    comm = _WeightExchanges({name: w[name][0] for name in _SHARDED}, ic, chip)
    wd = {name: w[name] for name in _SMALL if name != "ada_b"}
    wd = {k: (a if k in ("pre_g", "post_g", "rel_bias") else a[0]) for k, a in wd.items()}
    wd["gdn_conv"] = conv_full
    wd.update(comm.first_weights())

    loss_local, grad_x, grads, dmod = _local_step(x[0], loss_target[0], mod, wd, comm)

    small_shapes = [w[name].shape for name in _SMALL] + [(C_CONV, 4 * n_cols)]
    small_rows = -(-sum(_rows128(int(np.prod(s))) for s in small_shapes) // 64) * 64
    per_dev, dmod_rows = small_rows // 8, _rows128(2 * 3 * D_MODEL)
    partial = _pack([dmod] + [grads[name] for name in _SMALL[1:]] + [grads["gdn_conv"]], small_rows)
    to_all = jnp.concatenate([partial[:dmod_rows], jnp.full((8, 128), loss_local, F32)], axis=0)
    outbound = jnp.concatenate([partial.reshape(8, per_dev, 128), jnp.broadcast_to(to_all[None], (8,) + to_all.shape)], axis=1)
    inbound = _own_slot(_all_to_all8("reduce_small_grads", outbound), lax.dynamic_index_in_dim(outbound, me, 0, keepdims=False), me)
    loss = functools.reduce(lambda a, b: a + b, [inbound[d, per_dev + dmod_rows, 0] for d in range(8)])
    my_rows = _slot_sum("sum_small_grads", inbound[:, :per_dev])
    g_small = _own_slot(_all_gather8("gather_small_grads", my_rows), my_rows, me).reshape(small_rows, 128)
    g_list = _unpack(g_small, small_shapes)
    out_g, out_d, out_m, out_v = {}, {}, {}, {}

    small =list(_SMALL) + ["gdn_conv"]
    small_g = [g.reshape(-1, g.shape[-1]) for g in g_list[:-1]] + [lax.dynamic_slice_in_dim(g_list[-1], chip * n_cols, n_cols, axis=1)]
    two_d = lambda a: a.reshape(-1, a.shape[-1])
    results = _adamw_many("adamw_small", [two_d(w[n]) for n in small], small_g, [two_d(m[n]) for n in small], [two_d(v[n]) for n in small])
    for name, g2d, (d_, m_, v_) in zip(small, small_g, results):
        out_g[name], out_d[name], out_m[name], out_v[name] = (a.reshape(w[name].shape) for a in (g2d, d_, m_, v_))

    from_sibling = _sibling_exchange("reduce_sibling", comm.split_halves({"ab_w_in": grads["ab_w_in"]}))
    rider = (comm.chip_partials(["ab_w_in"], from_sibling), "scatter")
    dmod_all = inbound[:, per_dev:per_dev + dmod_rows].reshape(8, 2, 4, n_cols)
    dmod_cols = lax.dynamic_index_in_dim(dmod_all, chip, axis=2, keepdims=False).transpose(1, 0, 2)
    g_ada = _ada_w_grad(c_all, dmod_cols).reshape(-1, n_cols)
    (d_, m_, v_), got = _adamw("adamw_ada_w", two_d(w["ada_w"]), g_ada, two_d(m["ada_w"]), two_d(v["ada_w"]), exchange=rider)
    out_g["ada_w"], out_d["ada_w"], out_m["ada_w"], out_v["ada_w"] = (a.reshape(w["ada_w"].shape) for a in (g_ada, d_, m_, v_))
    comm.received["ab_w_in"] = got[0]
    chip_1 = jnp.reshape(chip, (1,)).astype(jnp.int32)
    core_1 = jnp.reshape(ic, (1,)).astype(jnp.int32)
    reduced = [_chip_sum("sum_chips_" + name, comm.received[name], comm.partial[name], chip_1) for name in _SHARDED]
    for name, g_mine, g_sib in zip(_SHARDED, reduced, _sibling_exchange("reduce_share", reduced)):
        out_g[name], out_d[name], out_m[name], out_v[name] = _adamw_halves(
            "adamw_" + name, w[name], g_mine, g_sib, m[name], v[name], core_1)

    return (loss, grad_x[None], *[out_g[n] for n in _WEIGHTS], *[out_d[n] for n in _WEIGHTS],
            *[out_m[n] for n in _WEIGHTS], *[out_v[n] for n in _WEIGHTS])
```
